```python
import math
import jax, jax.numpy as jnp
from jax import lax
import numpy as np

D_MODEL = 2048
BATCH = 8
SEQ = 2048
DEPTH = 1

CHUNK = 64
Q_BLOCK = 128
HEAD_DIM = 128
N_HEADS_TOTAL = D_MODEL // HEAD_DIM
N_MEM_HEADS = 4
N_FOX_HEADS = (N_HEADS_TOTAL - N_MEM_HEADS) // 2
N_GDN_HEADS = N_HEADS_TOTAL - N_MEM_HEADS - N_FOX_HEADS
FOX_W = N_FOX_HEADS * HEAD_DIM
GDN_W = N_GDN_HEADS * HEAD_DIM
MEM_W = N_MEM_HEADS * HEAD_DIM
MIX_W = FOX_W + GDN_W + MEM_W
MEM_LEN = 256
CONV_WIDTH = 4
FF_DIM = -(-8 * D_MODEL // (3 * 256)) * 256
NORM_EPS = 1e-6

_SIZES = (FOX_W, FOX_W, FOX_W, N_FOX_HEADS,
          3 * GDN_W, GDN_W, N_GDN_HEADS, N_GDN_HEADS,
          MEM_W)
IN_COLS = sum(_SIZES)
SPLITS = tuple(int(s) for s in np.cumsum(_SIZES)[:-1])

kernel_name = "hymba_fox_gdn_memory_layer"


def rms_norm(x, gain):
    xf = x.astype(jnp.float32)
    y = xf * lax.rsqrt(jnp.mean(xf * xf, axis=-1, keepdims=True) + NORM_EPS)
    return (y * gain.astype(jnp.float32)).astype(x.dtype)


def l2_norm(x):
    return x * lax.rsqrt(jnp.sum(x * x, axis=-1, keepdims=True) + NORM_EPS)


def causal_depthwise_conv(x, w):
    k_w, c = w.shape
    return lax.conv_general_dilated(
        x, w[:, None, :].astype(x.dtype), window_strides=(1,), padding=[(k_w - 1, 0)],
        dimension_numbers=("NWC", "WIO", "NWC"), feature_group_count=c)


def forgetting_attention(q, k, v, log_f):
    b, t, h, d = q.shape
    c = jnp.cumsum(log_f, axis=1).transpose(0, 2, 1)
    scale = d ** -0.5
    outs = []
    for i in range(t // Q_BLOCK):
        q0, q1 = i * Q_BLOCK, (i + 1) * Q_BLOCK
        s = jnp.einsum("bqhd,bkhd->bhqk", q[:, q0:q1], k[:, :q1]).astype(jnp.float32) * scale
        s = s + c[:, :, q0:q1, None] - c[:, :, None, :q1]
        mask = jnp.arange(q1)[None, :] <= (q0 + jnp.arange(Q_BLOCK))[:, None]
        p = jax.nn.softmax(jnp.where(mask, s, -jnp.inf), axis=-1)
        outs.append(jnp.einsum("bhqk,bkhd->bqhd", p.astype(v.dtype), v[:, :q1]))
    return jnp.concatenate(outs, axis=1)


def chunked_gated_delta_rule(q, k, v, g, beta):
    b, t, h, dk = q.shape
    dv = v.shape[-1]
    n = t // CHUNK

    def to_chunks(a):
        return a.reshape(b, n, CHUNK, h, -1).transpose(1, 0, 3, 2, 4)

    q, k, v = to_chunks(q), to_chunks(k), to_chunks(v)
    g = g.reshape(b, n, CHUNK, h).transpose(1, 0, 3, 2)
    beta = beta.reshape(b, n, CHUNK, h).transpose(1, 0, 3, 2)
    gc = jnp.cumsum(g, axis=-1)
    tril = jnp.tril(jnp.ones((CHUNK, CHUNK), bool))
    strict = jnp.tril(jnp.ones((CHUNK, CHUNK), bool), -1)
    decay = jnp.exp(jnp.where(tril, gc[..., :, None] - gc[..., None, :], -jnp.inf))
    kb = k * beta[..., None]
    vb = v * beta[..., None]
    lower = jnp.where(strict, jnp.einsum("nbhcd,nbhed->nbhce", kb, k) * decay, 0.0)
    a_mat = lower + jnp.eye(CHUNK, dtype=lower.dtype)
    rhs = jnp.concatenate([vb, kb * jnp.exp(gc)[..., None]], axis=-1)
    sol = lax.linalg.triangular_solve(a_mat, rhs, left_side=True, lower=True, unit_diagonal=True)
    u, w = sol[..., :dv], sol[..., dv:]
    attn_intra = jnp.where(tril, jnp.einsum("nbhcd,nbhed->nbhce", q, k) * decay, 0.0)
    qg = q * jnp.exp(gc)[..., None]
    g_last = gc[..., -1]
    kdec = k * jnp.exp(g_last[..., None] - gc)[..., None]

    def step(state, inp):
        u_i, w_i, qg_i, at_i, kd_i, gl_i = inp
        v_new = u_i - jnp.einsum("bhcd,bhde->bhce", w_i, state)
        o = jnp.einsum("bhcd,bhde->bhce", qg_i, state) + jnp.einsum("bhcs,bhse->bhce", at_i, v_new)
        state = state * jnp.exp(gl_i)[..., None, None] + jnp.einsum("bhcd,bhce->bhde", kd_i, v_new)
        return state, o

    s0 = jnp.zeros((b, h, dk, dv), jnp.float32)
    _, o = lax.scan(step, s0, (u, w, qg, attn_intra, kdec, g_last))
    return o.transpose(1, 0, 3, 2, 4).reshape(b, t, h, dv)


def gated_deltanet(qkv, z, a, bt, conv_w, a_log, dt_bias, out_gain):
    b, t, _ = qkv.shape
    qkv = jax.nn.silu(causal_depthwise_conv(qkv, conv_w))
    q, k, v = jnp.split(qkv, 3, axis=-1)
    q = l2_norm(q.reshape(b, t, N_GDN_HEADS, HEAD_DIM).astype(jnp.float32)) * (HEAD_DIM ** -0.5)
    k = l2_norm(k.reshape(b, t, N_GDN_HEADS, HEAD_DIM).astype(jnp.float32))
    v = v.reshape(b, t, N_GDN_HEADS, HEAD_DIM).astype(jnp.float32)
    beta = jax.nn.sigmoid(bt.astype(jnp.float32))
    g = -jnp.exp(a_log.astype(jnp.float32)) * jax.nn.softplus(a.astype(jnp.float32) + dt_bias.astype(jnp.float32))
    o = chunked_gated_delta_rule(q, k, v, g, beta)
    o = rms_norm(o, out_gain) * jax.nn.silu(z.reshape(b, t, N_GDN_HEADS, HEAD_DIM).astype(jnp.float32))
    return o.reshape(b, t, GDN_W).astype(qkv.dtype)


def memory_attention(q, k, v):
    s = jnp.einsum("bthd,bmhd->bhtm", q, k).astype(jnp.float32) * (q.shape[-1] ** -0.5)
    p = jax.nn.softmax(s, axis=-1)
    return jnp.einsum("bhtm,bmhd->bthd", p.astype(v.dtype), v)


def _fwd_setup_inputs(seed: int = 0) -> dict:
    key = jax.random.key(seed)
    ks = jax.random.split(key, 24)
    nrm = lambda k, shape, s: jax.random.normal(k, shape, jnp.float32) * s
    gain = lambda k, shape: 1.0 + 0.02 * jax.random.normal(k, shape, jnp.float32)
    dt = jnp.exp(jax.random.uniform(ks[9], (DEPTH, N_GDN_HEADS), jnp.float32,
                                    minval=math.log(1e-3), maxval=math.log(1e-1)))
    return {
        "x": nrm(ks[0], (BATCH, SEQ, D_MODEL), 1.0),
        "mem": nrm(ks[1], (BATCH, MEM_LEN, D_MODEL), 1.0),
        "norm_mix": gain(ks[2], (DEPTH, D_MODEL)),
        "w_in": nrm(ks[3], (DEPTH, D_MODEL, IN_COLS), D_MODEL ** -0.5),
        "fox_f_bias": 2.0 + 0.1 * jax.random.normal(ks[4], (DEPTH, N_FOX_HEADS), jnp.float32),
        "fox_q_norm": gain(ks[5], (DEPTH, HEAD_DIM)),
        "fox_k_norm": gain(ks[6], (DEPTH, HEAD_DIM)),
        "gdn_conv": nrm(ks[7], (DEPTH, CONV_WIDTH, 3 * GDN_W), CONV_WIDTH ** -0.5),
        "gdn_a_log": jnp.log(jax.random.uniform(ks[8], (DEPTH, N_GDN_HEADS), jnp.float32, minval=1.0, maxval=16.0)),
        "gdn_dt_bias": dt + jnp.log(-jnp.expm1(-dt)),
        "gdn_out_norm": gain(ks[10], (DEPTH, HEAD_DIM)),
        "mem_norm": gain(ks[11], (DEPTH, D_MODEL)),
        "w_mem_kv": nrm(ks[12], (DEPTH, D_MODEL, 2 * MEM_W), D_MODEL ** -0.5),
        "mem_q_norm": gain(ks[13], (DEPTH, HEAD_DIM)),
        "mem_k_norm": gain(ks[14], (DEPTH, HEAD_DIM)),
        "w_out": nrm(ks[15], (DEPTH, MIX_W, D_MODEL), MIX_W ** -0.5),
        "norm_ffn": gain(ks[16], (DEPTH, D_MODEL)),
        "w_gate_up": nrm(ks[17], (DEPTH, D_MODEL, 2 * FF_DIM), D_MODEL ** -0.5),
        "w_down": nrm(ks[18], (DEPTH, FF_DIM, D_MODEL), FF_DIM ** -0.5),
    }


def _fwd_reference(x, mem, norm_mix, w_in, fox_f_bias, fox_q_norm, fox_k_norm, gdn_conv, gdn_a_log,
              gdn_dt_bias, gdn_out_norm, mem_norm, w_mem_kv, mem_q_norm, mem_k_norm, w_out,
              norm_ffn, w_gate_up, w_down):
    b, t, _ = x.shape
    m = mem.shape[1]
    h = x
    for l in range(DEPTH):
        u = rms_norm(h, norm_mix[l])
        p = u @ w_in[l]
        fq, fk, fv, ff, gqkv, gz, ga, gb, mq = jnp.split(p, SPLITS, axis=-1)

        fq = rms_norm(fq.reshape(b, t, N_FOX_HEADS, HEAD_DIM), fox_q_norm[l])
        fk = rms_norm(fk.reshape(b, t, N_FOX_HEADS, HEAD_DIM), fox_k_norm[l])
        fv = fv.reshape(b, t, N_FOX_HEADS, HEAD_DIM)
        log_f = jax.nn.log_sigmoid(ff.astype(jnp.float32) + fox_f_bias[l].astype(jnp.float32))
        o_fox = forgetting_attention(fq, fk, fv, log_f).reshape(b, t, FOX_W)

        o_gdn = gated_deltanet(gqkv, gz, ga, gb, gdn_conv[l], gdn_a_log[l], gdn_dt_bias[l], gdn_out_norm[l])

        mkv = rms_norm(mem, mem_norm[l]) @ w_mem_kv[l]
        mk, mv = jnp.split(mkv, 2, axis=-1)
        mq = rms_norm(mq.reshape(b, t, N_MEM_HEADS, HEAD_DIM), mem_q_norm[l])
        mk = rms_norm(mk.reshape(b, m, N_MEM_HEADS, HEAD_DIM), mem_k_norm[l])
        mv = mv.reshape(b, m, N_MEM_HEADS, HEAD_DIM)
        o_mem = memory_attention(mq, mk, mv).reshape(b, t, MEM_W)

        mix = jnp.concatenate([o_fox, o_gdn.astype(o_fox.dtype), o_mem], axis=-1)
        h = h + mix @ w_out[l]

        gate, up = jnp.split(rms_norm(h, norm_ffn[l]) @ w_gate_up[l], 2, axis=-1)
        h = h + (jax.nn.silu(gate) * up) @ w_down[l]
    return h


import jax as _jax
import jax.numpy as _jnp

TWIN_FORMAT = 'train_step'
FWD_PARAMS = ['x', 'mem', 'norm_mix', 'w_in', 'fox_f_bias', 'fox_q_norm', 'fox_k_norm', 'gdn_conv', 'gdn_a_log', 'gdn_dt_bias', 'gdn_out_norm', 'mem_norm', 'w_mem_kv', 'mem_q_norm', 'mem_k_norm', 'w_out', 'norm_ffn', 'w_gate_up', 'w_down']
TWIN_WEIGHTS = ['norm_mix', 'w_in', 'fox_f_bias', 'fox_q_norm', 'fox_k_norm', 'gdn_conv', 'gdn_a_log', 'gdn_dt_bias', 'gdn_out_norm', 'mem_norm', 'w_mem_kv', 'mem_q_norm', 'mem_k_norm', 'w_out', 'norm_ffn', 'w_gate_up', 'w_down']
TWIN_DIFF_INPUT = 'x'
TWIN_INPUTS = ['x', 'mem', 'norm_mix', 'w_in', 'fox_f_bias', 'fox_q_norm', 'fox_k_norm', 'gdn_conv', 'gdn_a_log', 'gdn_dt_bias', 'gdn_out_norm', 'mem_norm', 'w_mem_kv', 'mem_q_norm', 'mem_k_norm', 'w_out', 'norm_ffn', 'w_gate_up', 'w_down', 'loss_target', 'm_norm_mix', 'm_w_in', 'm_fox_f_bias', 'm_fox_q_norm', 'm_fox_k_norm', 'm_gdn_conv', 'm_gdn_a_log', 'm_gdn_dt_bias', 'm_gdn_out_norm', 'm_mem_norm', 'm_w_mem_kv', 'm_mem_q_norm', 'm_mem_k_norm', 'm_w_out', 'm_norm_ffn', 'm_w_gate_up', 'm_w_down', 'v_norm_mix', 'v_w_in', 'v_fox_f_bias', 'v_fox_q_norm', 'v_fox_k_norm', 'v_gdn_conv', 'v_gdn_a_log', 'v_gdn_dt_bias', 'v_gdn_out_norm', 'v_mem_norm', 'v_w_mem_kv', 'v_mem_q_norm', 'v_mem_k_norm', 'v_w_out', 'v_norm_ffn', 'v_w_gate_up', 'v_w_down']
TWIN_OUTPUTS = ['loss', 'grad_x', 'grad_norm_mix', 'grad_w_in', 'grad_fox_f_bias', 'grad_fox_q_norm', 'grad_fox_k_norm', 'grad_gdn_conv', 'grad_gdn_a_log', 'grad_gdn_dt_bias', 'grad_gdn_out_norm', 'grad_mem_norm', 'grad_w_mem_kv', 'grad_mem_q_norm', 'grad_mem_k_norm', 'grad_w_out', 'grad_norm_ffn', 'grad_w_gate_up', 'grad_w_down', 'delta_norm_mix', 'delta_w_in', 'delta_fox_f_bias', 'delta_fox_q_norm', 'delta_fox_k_norm', 'delta_gdn_conv', 'delta_gdn_a_log', 'delta_gdn_dt_bias', 'delta_gdn_out_norm', 'delta_mem_norm', 'delta_w_mem_kv', 'delta_mem_q_norm', 'delta_mem_k_norm', 'delta_w_out', 'delta_norm_ffn', 'delta_w_gate_up', 'delta_w_down', 'new_m_norm_mix', 'new_m_w_in', 'new_m_fox_f_bias', 'new_m_fox_q_norm', 'new_m_fox_k_norm', 'new_m_gdn_conv', 'new_m_gdn_a_log', 'new_m_gdn_dt_bias', 'new_m_gdn_out_norm', 'new_m_mem_norm', 'new_m_w_mem_kv', 'new_m_mem_q_norm', 'new_m_mem_k_norm', 'new_m_w_out', 'new_m_norm_ffn', 'new_m_w_gate_up', 'new_m_w_down', 'new_v_norm_mix', 'new_v_w_in', 'new_v_fox_f_bias', 'new_v_fox_q_norm', 'new_v_fox_k_norm', 'new_v_gdn_conv', 'new_v_gdn_a_log', 'new_v_gdn_dt_bias', 'new_v_gdn_out_norm', 'new_v_mem_norm', 'new_v_w_mem_kv', 'new_v_mem_q_norm', 'new_v_mem_k_norm', 'new_v_w_out', 'new_v_norm_ffn', 'new_v_w_gate_up', 'new_v_w_down']
TWIN_LEAF_KINDS = {'loss': 'loss', 'grad_x': 'grad_x', 'grad_norm_mix': 'grad_w', 'grad_w_in': 'grad_w', 'grad_fox_f_bias': 'grad_w', 'grad_fox_q_norm': 'grad_w', 'grad_fox_k_norm': 'grad_w', 'grad_gdn_conv': 'grad_w', 'grad_gdn_a_log': 'grad_w', 'grad_gdn_dt_bias': 'grad_w', 'grad_gdn_out_norm': 'grad_w', 'grad_mem_norm': 'grad_w', 'grad_w_mem_kv': 'grad_w', 'grad_mem_q_norm': 'grad_w', 'grad_mem_k_norm': 'grad_w', 'grad_w_out': 'grad_w', 'grad_norm_ffn': 'grad_w', 'grad_w_gate_up': 'grad_w', 'grad_w_down': 'grad_w', 'delta_norm_mix': 'delta_w', 'delta_w_in': 'delta_w', 'delta_fox_f_bias': 'delta_w', 'delta_fox_q_norm': 'delta_w', 'delta_fox_k_norm': 'delta_w', 'delta_gdn_conv': 'delta_w', 'delta_gdn_a_log': 'delta_w', 'delta_gdn_dt_bias': 'delta_w', 'delta_gdn_out_norm': 'delta_w', 'delta_mem_norm': 'delta_w', 'delta_w_mem_kv': 'delta_w', 'delta_mem_q_norm': 'delta_w', 'delta_mem_k_norm': 'delta_w', 'delta_w_out': 'delta_w', 'delta_norm_ffn': 'delta_w', 'delta_w_gate_up': 'delta_w', 'delta_w_down': 'delta_w', 'new_m_norm_mix': 'new_m', 'new_m_w_in': 'new_m', 'new_m_fox_f_bias': 'new_m', 'new_m_fox_q_norm': 'new_m', 'new_m_fox_k_norm': 'new_m', 'new_m_gdn_conv': 'new_m', 'new_m_gdn_a_log': 'new_m', 'new_m_gdn_dt_bias': 'new_m', 'new_m_gdn_out_norm': 'new_m', 'new_m_mem_norm': 'new_m', 'new_m_w_mem_kv': 'new_m', 'new_m_mem_q_norm': 'new_m', 'new_m_mem_k_norm': 'new_m', 'new_m_w_out': 'new_m', 'new_m_norm_ffn': 'new_m', 'new_m_w_gate_up': 'new_m', 'new_m_w_down': 'new_m', 'new_v_norm_mix': 'new_v', 'new_v_w_in': 'new_v', 'new_v_fox_f_bias': 'new_v', 'new_v_fox_q_norm': 'new_v', 'new_v_fox_k_norm': 'new_v', 'new_v_gdn_conv': 'new_v', 'new_v_gdn_a_log': 'new_v', 'new_v_gdn_dt_bias': 'new_v', 'new_v_gdn_out_norm': 'new_v', 'new_v_mem_norm': 'new_v', 'new_v_w_mem_kv': 'new_v', 'new_v_mem_q_norm': 'new_v', 'new_v_mem_k_norm': 'new_v', 'new_v_w_out': 'new_v', 'new_v_norm_ffn': 'new_v', 'new_v_w_gate_up': 'new_v', 'new_v_w_down': 'new_v'}


def _forward(args):
    return _fwd_reference(*[args[k] for k in FWD_PARAMS])


def _output_shape():
    out = _jax.eval_shape(lambda: _forward(_fwd_setup_inputs(0)))
    return out.shape, out.dtype

N_MICROBATCH = 1
ADAM_LR = 0.001
ADAM_B1 = 0.9
ADAM_B2 = 0.999
ADAM_EPS = 1e-08
ADAM_WD = 0.01
ADAM_STEP = 10
PER_EXAMPLE_BATCH_AXIS = {'x': 0, 'mem': 0, 'loss_target': 0}
SHARED_INPUTS = []
_WEIGHT_DTYPES = {'norm_mix': _jnp.float32, 'w_in': _jnp.float32, 'fox_f_bias': _jnp.float32, 'fox_q_norm': _jnp.float32, 'fox_k_norm': _jnp.float32, 'gdn_conv': _jnp.float32, 'gdn_a_log': _jnp.float32, 'gdn_dt_bias': _jnp.float32, 'gdn_out_norm': _jnp.float32, 'mem_norm': _jnp.float32, 'w_mem_kv': _jnp.float32, 'mem_q_norm': _jnp.float32, 'mem_k_norm': _jnp.float32, 'w_out': _jnp.float32, 'norm_ffn': _jnp.float32, 'w_gate_up': _jnp.float32, 'w_down': _jnp.float32}
MOMENT_SCALE = {'norm_mix': 1.819100e+00, 'w_in': 9.177868e-02, 'fox_f_bias': 5.132450e+01, 'fox_q_norm': 3.018229e+00, 'fox_k_norm': 3.016995e+00, 'gdn_conv': 1.303558e-01, 'gdn_a_log': 8.846419e+00, 'gdn_dt_bias': 8.155373e+00, 'gdn_out_norm': 1.737866e+01, 'mem_norm': 2.862751e-02, 'w_mem_kv': 3.107238e-02, 'mem_q_norm': 2.894289e-01, 'mem_k_norm': 2.898963e-01, 'w_out': 1.172084e-01, 'norm_ffn': 6.211741e+00, 'w_gate_up': 5.763438e-02, 'w_down': 7.269417e-02}


def _to_microbatches(a, axis):
    t = _jnp.moveaxis(a, axis, 0)
    t = t.reshape((N_MICROBATCH, t.shape[0] // N_MICROBATCH) + t.shape[1:])
    return _jnp.moveaxis(t, 1, axis + 1)


def setup_inputs(seed: int = 0) -> dict:
    inp = _fwd_setup_inputs(seed)
    key = _jax.random.fold_in(_jax.random.key(seed), 7919)
    shape, _ = _output_shape()
    out = dict(inp)
    out["loss_target"] = _jax.random.normal(_jax.random.fold_in(key, 0), shape, _jnp.float32)
    for i, name in enumerate(TWIN_WEIGHTS):
        w = inp[name].astype(_jnp.float32)
        if MOMENT_SCALE is None:
            s = _jnp.sqrt(_jnp.mean(_jnp.square(w)) + 1e-30)
        else:
            s = MOMENT_SCALE[name]
        km, kv = _jax.random.split(_jax.random.fold_in(key, i + 1))
        out[name] = w
        out["m_" + name] = s * _jax.random.normal(km, w.shape, _jnp.float32)
        out["v_" + name] = (s * s) * _jax.random.uniform(kv, w.shape, _jnp.float32, 0.5, 1.5)
    if N_MICROBATCH > 1:
        for name, axis in PER_EXAMPLE_BATCH_AXIS.items():
            out[name] = _to_microbatches(out[name], axis)
    return {'x': out['x'], 'mem': out['mem'], 'norm_mix': out['norm_mix'], 'w_in': out['w_in'], 'fox_f_bias': out['fox_f_bias'], 'fox_q_norm': out['fox_q_norm'], 'fox_k_norm': out['fox_k_norm'], 'gdn_conv': out['gdn_conv'], 'gdn_a_log': out['gdn_a_log'], 'gdn_dt_bias': out['gdn_dt_bias'], 'gdn_out_norm': out['gdn_out_norm'], 'mem_norm': out['mem_norm'], 'w_mem_kv': out['w_mem_kv'], 'mem_q_norm': out['mem_q_norm'], 'mem_k_norm': out['mem_k_norm'], 'w_out': out['w_out'], 'norm_ffn': out['norm_ffn'], 'w_gate_up': out['w_gate_up'], 'w_down': out['w_down'], 'loss_target': out['loss_target'], 'm_norm_mix': out['m_norm_mix'], 'm_w_in': out['m_w_in'], 'm_fox_f_bias': out['m_fox_f_bias'], 'm_fox_q_norm': out['m_fox_q_norm'], 'm_fox_k_norm': out['m_fox_k_norm'], 'm_gdn_conv': out['m_gdn_conv'], 'm_gdn_a_log': out['m_gdn_a_log'], 'm_gdn_dt_bias': out['m_gdn_dt_bias'], 'm_gdn_out_norm': out['m_gdn_out_norm'], 'm_mem_norm': out['m_mem_norm'], 'm_w_mem_kv': out['m_w_mem_kv'], 'm_mem_q_norm': out['m_mem_q_norm'], 'm_mem_k_norm': out['m_mem_k_norm'], 'm_w_out': out['m_w_out'], 'm_norm_ffn': out['m_norm_ffn'], 'm_w_gate_up': out['m_w_gate_up'], 'm_w_down': out['m_w_down'], 'v_norm_mix': out['v_norm_mix'], 'v_w_in': out['v_w_in'], 'v_fox_f_bias': out['v_fox_f_bias'], 'v_fox_q_norm': out['v_fox_q_norm'], 'v_fox_k_norm': out['v_fox_k_norm'], 'v_gdn_conv': out['v_gdn_conv'], 'v_gdn_a_log': out['v_gdn_a_log'], 'v_gdn_dt_bias': out['v_gdn_dt_bias'], 'v_gdn_out_norm': out['v_gdn_out_norm'], 'v_mem_norm': out['v_mem_norm'], 'v_w_mem_kv': out['v_w_mem_kv'], 'v_mem_q_norm': out['v_mem_q_norm'], 'v_mem_k_norm': out['v_mem_k_norm'], 'v_w_out': out['v_w_out'], 'v_norm_ffn': out['v_norm_ffn'], 'v_w_gate_up': out['v_w_gate_up'], 'v_w_down': out['v_w_down']}


def _loss(weights, diff, rest, loss_target):
    with _jax.named_scope("forward"):
        args = {**rest, TWIN_DIFF_INPUT: diff, **{k: w.astype(_WEIGHT_DTYPES[k]) for k, w in weights.items()}}
        y = _forward(args)
    with _jax.named_scope("loss_head"):
        err = _jnp.square(y.astype(_jnp.float32) - loss_target)
        return 0.5 * _jnp.sum(_jnp.mean(err, axis=-1)) if err.ndim else 0.5 * err


def _adamw(w, g, m, v):
    m = ADAM_B1 * m + (1.0 - ADAM_B1) * g
    v = ADAM_B2 * v + (1.0 - ADAM_B2) * _jnp.square(g)
    m_hat = m / (1.0 - ADAM_B1 ** ADAM_STEP)
    v_hat = v / (1.0 - ADAM_B2 ** ADAM_STEP)
    delta = -ADAM_LR * (m_hat / (_jnp.sqrt(v_hat) + ADAM_EPS) + ADAM_WD * w)
    return delta, m, v


def reference(x, mem, norm_mix, w_in, fox_f_bias, fox_q_norm, fox_k_norm, gdn_conv, gdn_a_log, gdn_dt_bias, gdn_out_norm, mem_norm, w_mem_kv, mem_q_norm, mem_k_norm, w_out, norm_ffn, w_gate_up, w_down, loss_target, m_norm_mix, m_w_in, m_fox_f_bias, m_fox_q_norm, m_fox_k_norm, m_gdn_conv, m_gdn_a_log, m_gdn_dt_bias, m_gdn_out_norm, m_mem_norm, m_w_mem_kv, m_mem_q_norm, m_mem_k_norm, m_w_out, m_norm_ffn, m_w_gate_up, m_w_down, v_norm_mix, v_w_in, v_fox_f_bias, v_fox_q_norm, v_fox_k_norm, v_gdn_conv, v_gdn_a_log, v_gdn_dt_bias, v_gdn_out_norm, v_mem_norm, v_w_mem_kv, v_mem_q_norm, v_mem_k_norm, v_w_out, v_norm_ffn, v_w_gate_up, v_w_down):
    given = dict(x=x, mem=mem, norm_mix=norm_mix, w_in=w_in, fox_f_bias=fox_f_bias, fox_q_norm=fox_q_norm, fox_k_norm=fox_k_norm, gdn_conv=gdn_conv, gdn_a_log=gdn_a_log, gdn_dt_bias=gdn_dt_bias, gdn_out_norm=gdn_out_norm, mem_norm=mem_norm, w_mem_kv=w_mem_kv, mem_q_norm=mem_q_norm, mem_k_norm=mem_k_norm, w_out=w_out, norm_ffn=norm_ffn, w_gate_up=w_gate_up, w_down=w_down, loss_target=loss_target, m_norm_mix=m_norm_mix, m_w_in=m_w_in, m_fox_f_bias=m_fox_f_bias, m_fox_q_norm=m_fox_q_norm, m_fox_k_norm=m_fox_k_norm, m_gdn_conv=m_gdn_conv, m_gdn_a_log=m_gdn_a_log, m_gdn_dt_bias=m_gdn_dt_bias, m_gdn_out_norm=m_gdn_out_norm, m_mem_norm=m_mem_norm, m_w_mem_kv=m_w_mem_kv, m_mem_q_norm=m_mem_q_norm, m_mem_k_norm=m_mem_k_norm, m_w_out=m_w_out, m_norm_ffn=m_norm_ffn, m_w_gate_up=m_w_gate_up, m_w_down=m_w_down, v_norm_mix=v_norm_mix, v_w_in=v_w_in, v_fox_f_bias=v_fox_f_bias, v_fox_q_norm=v_fox_q_norm, v_fox_k_norm=v_fox_k_norm, v_gdn_conv=v_gdn_conv, v_gdn_a_log=v_gdn_a_log, v_gdn_dt_bias=v_gdn_dt_bias, v_gdn_out_norm=v_gdn_out_norm, v_mem_norm=v_mem_norm, v_w_mem_kv=v_w_mem_kv, v_mem_q_norm=v_mem_q_norm, v_mem_k_norm=v_mem_k_norm, v_w_out=v_w_out, v_norm_ffn=v_norm_ffn, v_w_gate_up=v_w_gate_up, v_w_down=v_w_down)
    weights = {n: given[n] for n in TWIN_WEIGHTS}
    shared = {n: given[n] for n in SHARED_INPUTS}
    per_example = {n: given[n] for n in ['x', 'mem']}
    grad_fn = _jax.value_and_grad(_loss, argnums=(0, 1))

    def one_microbatch(ex, loss_target):
        ex = dict(ex)
        diff = ex.pop(TWIN_DIFF_INPUT)
        return grad_fn(weights, diff, {**shared, **ex}, loss_target)

    if N_MICROBATCH == 1:
        loss, (grad_w, grad_x) = one_microbatch(per_example, given["loss_target"])
    else:
        def body(carry, xs):
            loss_sum, grad_sum = carry
            l_k, (gw_k, gx_k) = one_microbatch(xs[0], xs[1])
            with _jax.named_scope("update"):
                return (loss_sum + l_k, _jax.tree.map(_jnp.add, grad_sum, gw_k)), gx_k

        init = (_jnp.zeros((), _jnp.float32), _jax.tree.map(_jnp.zeros_like, weights))
        (loss, grad_w), grad_x = _jax.lax.scan(body, init, (per_example, given["loss_target"]))
    with _jax.named_scope("update"):
        delta_w, new_m, new_v = {}, {}, {}
        for n in TWIN_WEIGHTS:
            delta_w[n], new_m[n], new_v[n] = _adamw(weights[n], grad_w[n], given["m_" + n], given["v_" + n])
    return (loss, grad_x, *[grad_w[n] for n in TWIN_WEIGHTS], *[delta_w[n] for n in TWIN_WEIGHTS],
            *[new_m[n] for n in TWIN_WEIGHTS], *[new_v[n] for n in TWIN_WEIGHTS])
```

```python
import functools
import math

import jax
import jax.numpy as jnp
from jax import lax
from jax.experimental import pallas as pl
from jax.experimental.pallas import tpu as pltpu

F32, BF16 = jnp.float32, jnp.bfloat16
HEAD_DIM = 128
CHUNK = 64
N_MEM_HEADS = 4
CONV_WIDTH = 4
NORM_EPS = 1e-6
ADAM_LR, ADAM_B1, ADAM_B2, ADAM_EPS, ADAM_WD, ADAM_STEP = 0.001, 0.9, 0.999, 1e-08, 0.01, 10
VMEM_LIMIT = 48 * 1024 * 1024
NEG = -1e30
MESH = pl.DeviceIdType.MESH


def _cparams(sem=None, **kw):
    if sem is not None:
        kw["dimension_semantics"] = sem
    return pltpu.CompilerParams(vmem_limit_bytes=VMEM_LIMIT, **kw)


def _tile(n, target, mult=128):
    best = None
    d = mult
    while d <= min(n, target):
        if n % d == 0:
            best = d
        d += mult
    return best if best is not None else n


def _dot(a, b, dims, hi):
    if hi:
        return lax.dot_general(a, b, (dims, ((), ())), precision=lax.Precision.HIGHEST,
                               preferred_element_type=F32)
    return lax.dot_general(a.astype(BF16), b.astype(BF16), (dims, ((), ())), preferred_element_type=F32)


def _make_dots(hi):
    @jax.custom_vjp
    def nn(a, b):
        return _dot(a, b, ((1,), (0,)), hi)

    @jax.custom_vjp
    def nt(a, b):
        return _dot(a, b, ((1,), (1,)), hi)

    @jax.custom_vjp
    def tn(a, b):
        return _dot(a, b, ((0,), (0,)), hi)

    nn.defvjp(lambda a, b: (nn(a, b), (a, b)), lambda r, g: (nt(g, r[1]), tn(r[0], g)))
    nt.defvjp(lambda a, b: (nt(a, b), (a, b)), lambda r, g: (nn(g, r[1]), tn(g, r[0])))
    tn.defvjp(lambda a, b: (tn(a, b), (a, b)), lambda r, g: (nt(r[1], g), nn(r[0], g)))
    return nn, nt, tn


_nn, _nt, _tn = _make_dots(False)
_nn_hi, _nt_hi, _tn_hi = _make_dots(True)


def _sigmoid(x):
    return 1.0 / (1.0 + jnp.exp(-x))


@jax.custom_vjp
def _softplus(x):
    return jnp.maximum(x, 0.0) + jnp.log(1.0 + jnp.exp(-jnp.abs(x)))


_softplus.defvjp(lambda x: (_softplus(x), x), lambda x, g: (g * _sigmoid(x),))


def _silu(x):
    return x * _sigmoid(x)


def _rms_fn(x, gain, z=None):
    y = x * lax.rsqrt(jnp.mean(x * x, axis=-1, keepdims=True) + NORM_EPS) * gain
    if z is not None:
        y = y * _silu(z)
    return y


def _mm(a, b, *, ta=False, tb=False, out_dtype=F32, res=None, name):
    m = a.shape[1] if ta else a.shape[0]
    k = a.shape[0] if ta else a.shape[1]
    n = b.shape[0] if tb else b.shape[1]
    assert k == (b.shape[1] if tb else b.shape[0])
    tm, tn, tk = _tile(m, 1024), _tile(n, 1024), _tile(k, 512)
    nk = k // tk
    dims = ((0 if ta else 1,), (1 if tb else 0,))

    def body(*refs):
        if res is None:
            a_ref, b_ref, o_ref, acc = refs
        else:
            a_ref, b_ref, r_ref, o_ref, acc = refs
        kk = pl.program_id(2)

        @pl.when(kk == 0)
        def _():
            acc[...] = jnp.zeros_like(acc)

        acc[...] += lax.dot_general(a_ref[...].astype(BF16), b_ref[...].astype(BF16), (dims, ((), ())),
                                    preferred_element_type=F32)

        @pl.when(kk == nk - 1)
        def _():
            r = acc[...]
            if res is not None:
                r = r + r_ref[...]
            o_ref[...] = r.astype(out_dtype)

    a_spec = (pl.BlockSpec((tk, tm), lambda i, j, kk: (kk, i)) if ta
              else pl.BlockSpec((tm, tk), lambda i, j, kk: (i, kk)))
    b_spec = (pl.BlockSpec((tn, tk), lambda i, j, kk: (j, kk)) if tb
              else pl.BlockSpec((tk, tn), lambda i, j, kk: (kk, j)))
    o_spec = pl.BlockSpec((tm, tn), lambda i, j, kk: (i, j))
    ins, specs = [a, b], [a_spec, b_spec]
    if res is not None:
        ins.append(res)
        specs.append(o_spec)
    return pl.pallas_call(
        body, name=name, grid=(m // tm, n // tn, nk), in_specs=specs, out_specs=o_spec,
        out_shape=jax.ShapeDtypeStruct((m, n), out_dtype),
        scratch_shapes=[pltpu.VMEM((tm, tn), F32)],
        compiler_params=_cparams(("parallel", "parallel", "arbitrary")),
    )(*ins)


def _norm_fwd(x, xoff, gain, ncol, w, out_dtype, *, z=None, zoff=0, name):
    t = x.shape[0]
    tr = _tile(t, 256, 8)

    def body(*refs):
        if z is None:
            x_ref, g_ref, o_ref = refs
            y = _rms_fn(x_ref[...], g_ref[...])
        else:
            x_ref, g_ref, z_ref, o_ref = refs
            y = _rms_fn(x_ref[...], g_ref[...], z_ref[...])
        o_ref[...] = y.astype(out_dtype)

    ins = [x, gain]
    specs = [pl.BlockSpec((tr, w), lambda j, r: (r, xoff + j)), pl.BlockSpec((1, w), lambda j, r: (0, 0))]
    if z is not None:
        ins.append(z)
        specs.append(pl.BlockSpec((tr, w), lambda j, r: (r, zoff + j)))
    return pl.pallas_call(
        body, name=name, grid=(ncol, t // tr), in_specs=specs,
        out_specs=pl.BlockSpec((tr, w), lambda j, r: (r, j)),
        out_shape=jax.ShapeDtypeStruct((t, ncol * w), out_dtype),
        compiler_params=_cparams(("parallel", "parallel")),
    )(*ins)


def _norm_bwd(x, xoff, gain, dy, dyoff, ncol, w, *, z=None, zoff=0, res=None, name):
    t = x.shape[0]
    tr = _tile(t, 256, 8)

    def body(*refs):
        it = iter(refs)
        x_ref, g_ref = next(it), next(it)
        z_ref = next(it) if z is not None else None
        dy_ref = next(it)
        r_ref = next(it) if res is not None else None
        dx_ref = next(it)
        dz_ref = next(it) if z is not None else None
        dg_ref = next(it)

        @pl.when((pl.program_id(0) == 0) & (pl.program_id(1) == 0))
        def _():
            dg_ref[...] = jnp.zeros_like(dg_ref)

        args = (x_ref[...], g_ref[...]) + ((z_ref[...],) if z is not None else ())
        _, vjp = jax.vjp(_rms_fn, *args)
        grads = vjp(dy_ref[...].astype(F32))
        dx = grads[0]
        if res is not None:
            dx = dx + r_ref[...]
        dx_ref[...] = dx
        if z is not None:
            dz_ref[...] = grads[2]
        dg_ref[...] += grads[1]

    ins = [x, gain]
    specs = [pl.BlockSpec((tr, w), lambda j, r: (r, xoff + j)), pl.BlockSpec((1, w), lambda j, r: (0, 0))]
    if z is not None:
        ins.append(z)
        specs.append(pl.BlockSpec((tr, w), lambda j, r: (r, zoff + j)))
    ins.append(dy)
    specs.append(pl.BlockSpec((tr, w), lambda j, r: (r, dyoff + j)))
    blk = pl.BlockSpec((tr, w), lambda j, r: (r, j))
    if res is not None:
        ins.append(res)
        specs.append(blk)
    full = jax.ShapeDtypeStruct((t, ncol * w), F32)
    out_shape, out_specs = [full], [blk]
    if z is not None:
        out_shape.append(full)
        out_specs.append(blk)
    out_shape.append(jax.ShapeDtypeStruct((1, w), F32))
    out_specs.append(pl.BlockSpec((1, w), lambda j, r: (0, 0)))
    return pl.pallas_call(
        body, name=name, grid=(ncol, t // tr), in_specs=specs, out_specs=out_specs, out_shape=out_shape,
        compiler_params=_cparams(("arbitrary", "arbitrary")),
    )(*ins)


def _small_fn(x, pa, pb, nf, ng):
    lane = lax.broadcasted_iota(jnp.int32, x.shape, 1)
    zz = x + pb
    logf = -_softplus(-zz)
    g = -jnp.exp(pa) * _softplus(zz)
    beta = _sigmoid(x)
    return jnp.where(lane < nf, logf, jnp.where(lane < nf + ng, g, beta))


def _tri(n, upper):
    r = lax.broadcasted_iota(jnp.int32, (n, n), 0)
    c = lax.broadcasted_iota(jnp.int32, (n, n), 1)
    return jnp.where((c >= r) if upper else (c <= r), 1.0, 0.0).astype(F32)


def _small_fwd(p, off, pa, pb, nf, ng):
    t = p.shape[0]
    blk = HEAD_DIM
    nb = t // blk

    def body(x_ref, pa_ref, pb_ref, v_ref, c_ref):
        v_ref[...] = _small_fn(x_ref[...], pa_ref[...], pb_ref[...], nf, ng)
        tri = _tri(blk, False)

        carry = jnp.zeros((1, HEAD_DIM), F32)
        for i in range(nb):
            rows = slice(i * blk, (i + 1) * blk)
            c = _nn_hi(tri, v_ref[rows, :]) + carry
            c_ref[rows, :] = c
            carry = c[blk - 1:blk, :]

    row = pl.BlockSpec((1, HEAD_DIM), lambda i: (0, 0))
    out = pl.BlockSpec((t, HEAD_DIM), lambda i: (0, 0))
    return pl.pallas_call(
        body, name="small_fwd", grid=(1,),
        in_specs=[pl.BlockSpec((t, HEAD_DIM), lambda i: (0, off)), row, row], out_specs=[out, out],
        out_shape=[jax.ShapeDtypeStruct((t, HEAD_DIM), F32)] * 2,
        compiler_params=_cparams(("arbitrary",)),
    )(p, pa, pb)


def _small_bwd(p, off, pa, pb, dvals, dcsum, nf, ng):
    t = p.shape[0]
    blk = HEAD_DIM
    nb = t // blk

    def body(x_ref, pa_ref, pb_ref, dv_ref, dc_ref, dx_ref, dpa_ref, dpb_ref, tot_ref):
        tri = _tri(blk, True)

        carry = jnp.zeros((1, HEAD_DIM), F32)
        for i in reversed(range(nb)):
            rows = slice(i * blk, (i + 1) * blk)
            c = _nn_hi(tri, dc_ref[rows, :]) + carry
            tot_ref[rows, :] = c + dv_ref[rows, :]
            carry = c[0:1, :]
        f = functools.partial(_small_fn, nf=nf, ng=ng)
        _, vjp = jax.vjp(f, x_ref[...], pa_ref[...], pb_ref[...])
        dx, dpa, dpb = vjp(tot_ref[...])
        dx_ref[...] = dx
        dpa_ref[...] = dpa
        dpb_ref[...] = dpb

    row = pl.BlockSpec((1, HEAD_DIM), lambda i: (0, 0))
    full = pl.BlockSpec((t, HEAD_DIM), lambda i: (0, 0))
    return pl.pallas_call(
        body, name="small_bwd", grid=(1,),
        in_specs=[pl.BlockSpec((t, HEAD_DIM), lambda i: (0, off)), row, row, full, full],
        out_specs=[full, row, row],
        out_shape=[jax.ShapeDtypeStruct((t, HEAD_DIM), F32), jax.ShapeDtypeStruct((1, HEAD_DIM), F32),
                   jax.ShapeDtypeStruct((1, HEAD_DIM), F32)],
        scratch_shapes=[pltpu.VMEM((t, HEAD_DIM), F32)],
        compiler_params=_cparams(("arbitrary",)),
    )(p, pa, pb, dvals, dcsum)


def _fox_fwd(q, k, v, cc, cr, nf, tq, tk):
    t = q.shape[0]
    scale = HEAD_DIM ** -0.5
    ratio = tq // tk

    def body(q_ref, k_ref, v_ref, cc_ref, cr_ref, o_ref, lse_ref):
        i = pl.program_id(1)
        qv = q_ref[...]
        ccol = cc_ref[0]
        rows = i * tq + lax.broadcasted_iota(jnp.int32, (tq, tk), 0)
        cols0 = lax.broadcasted_iota(jnp.int32, (tq, tk), 1)

        def step(j, carry):
            m, l, acc = carry
            ks = pl.ds(pl.multiple_of(j * tk, tk), tk)
            s = lax.dot_general(qv, k_ref[ks, :], (((1,), (1,)), ((), ())), preferred_element_type=F32) * scale
            s = s + ccol - cr_ref[0, j]
            s = jnp.where(cols0 + j * tk <= rows, s, NEG)
            m_new = jnp.maximum(m, jnp.max(s, axis=1, keepdims=True))
            pr = jnp.exp(s - m_new)
            alpha = jnp.exp(m - m_new)
            l = alpha * l + jnp.sum(pr, axis=1, keepdims=True)
            acc = alpha * acc + jnp.dot(pr.astype(BF16), v_ref[ks, :], preferred_element_type=F32)
            return m_new, l, acc

        init = (jnp.full((tq, 1), NEG, F32), jnp.zeros((tq, 1), F32), jnp.zeros((tq, HEAD_DIM), F32))
        m, l, acc = lax.fori_loop(0, (i + 1) * ratio, step, init)
        o_ref[...] = acc / l
        lse_ref[0] = m + jnp.log(l)

    head_all = pl.BlockSpec((t, HEAD_DIM), lambda h, i: (0, h))
    return pl.pallas_call(
        body, name="fox_fwd", grid=(nf, t // tq),
        in_specs=[pl.BlockSpec((tq, HEAD_DIM), lambda h, i: (i, h)), head_all, head_all,
                  pl.BlockSpec((1, tq, 1), lambda h, i: (h, i, 0)),
                  pl.BlockSpec((1, t // tk, 1, tk), lambda h, i: (h, 0, 0, 0))],
        out_specs=[pl.BlockSpec((tq, HEAD_DIM), lambda h, i: (i, h)),
                   pl.BlockSpec((1, tq, 1), lambda h, i: (h, i, 0))],
        out_shape=[jax.ShapeDtypeStruct((t, nf * HEAD_DIM), F32), jax.ShapeDtypeStruct((nf, t, 1), F32)],
        compiler_params=_cparams(("parallel", "parallel")),
    )(q, k, v, cc, cr)


def _fox_bwd(q, k, v, cc, cr, o, lse, dmix, nf, tq, tk):
    t = q.shape[0]
    scale = HEAD_DIM ** -0.5
    ratio = tq // tk

    def body(q_ref, k_ref, v_ref, cc_ref, cr_ref, o_ref, lse_ref, do_ref,
             dq_ref, dk_ref, dv_ref, dcc_ref, dcr_ref):
        i = pl.program_id(1)

        @pl.when(i == 0)
        def _():
            dk_ref[...] = jnp.zeros_like(dk_ref)
            dv_ref[...] = jnp.zeros_like(dv_ref)
            dcr_ref[...] = jnp.zeros_like(dcr_ref)

        qv = q_ref[...]
        ccol = cc_ref[0]
        lse_v = lse_ref[0]
        do = do_ref[...]
        do_b = do.astype(BF16)
        delta = jnp.sum(do * o_ref[...], axis=1, keepdims=True)
        rows = i * tq + lax.broadcasted_iota(jnp.int32, (tq, tk), 0)
        cols0 = lax.broadcasted_iota(jnp.int32, (tq, tk), 1)

        def step(j, carry):
            dq, dcc = carry
            ks = pl.ds(pl.multiple_of(j * tk, tk), tk)
            kj, vj = k_ref[ks, :], v_ref[ks, :]
            s = lax.dot_general(qv, kj, (((1,), (1,)), ((), ())), preferred_element_type=F32) * scale
            s = s + ccol - cr_ref[0, j]
            pr = jnp.where(cols0 + j * tk <= rows, jnp.exp(s - lse_v), 0.0)
            dp = lax.dot_general(do_b, vj, (((1,), (1,)), ((), ())), preferred_element_type=F32)
            ds = pr * (dp - delta)
            ds_b = ds.astype(BF16)
            dq = dq + jnp.dot(ds_b, kj, preferred_element_type=F32) * scale
            dk_ref[ks, :] += lax.dot_general(ds_b, qv, (((0,), (0,)), ((), ())),
                                             preferred_element_type=F32) * scale
            dv_ref[ks, :] += lax.dot_general(pr.astype(BF16), do_b, (((0,), (0,)), ((), ())),
                                             preferred_element_type=F32)
            dcr_ref[0, j] -= jnp.sum(ds, axis=0, keepdims=True)
            return dq, dcc + jnp.sum(ds, axis=1, keepdims=True)

        init = (jnp.zeros((tq, HEAD_DIM), F32), jnp.zeros((tq, 1), F32))
        dq, dcc = lax.fori_loop(0, (i + 1) * ratio, step, init)
        dq_ref[...] = dq
        dcc_ref[0] = dcc

    head_all = pl.BlockSpec((t, HEAD_DIM), lambda h, i: (0, h))
    qblk = pl.BlockSpec((tq, HEAD_DIM), lambda h, i: (i, h))
    colv = pl.BlockSpec((1, tq, 1), lambda h, i: (h, i, 0))
    rowv = pl.BlockSpec((1, t // tk, 1, tk), lambda h, i: (h, 0, 0, 0))
    wide = jax.ShapeDtypeStruct((t, nf * HEAD_DIM), F32)
    return pl.pallas_call(
        body, name="fox_bwd", grid=(nf, t // tq),
        in_specs=[qblk, head_all, head_all, colv, rowv, qblk, colv, qblk],
        out_specs=[qblk, head_all, head_all, colv, rowv],
        out_shape=[wide, wide, wide, jax.ShapeDtypeStruct((nf, t, 1), F32),
                   jax.ShapeDtypeStruct((nf, t // tk, 1, tk), F32)],
        compiler_params=_cparams(("parallel", "arbitrary")),
    )(q, k, v, cc, cr, o, lse, dmix)


def _mem_fn(mq, mk, mv, gq, gk):
    qn = _rms_fn(mq, gq)
    kn = _rms_fn(mk, gk)
    s = _nt(qn, kn) * (HEAD_DIM ** -0.5)
    e = jnp.exp(s - lax.stop_gradient(jnp.max(s, axis=1, keepdims=True)))
    pr = e / jnp.sum(e, axis=1, keepdims=True)
    return _nn(pr, mv)


def _mem_specs(t, m, tq, qoff):
    qblk = pl.BlockSpec((tq, HEAD_DIM), lambda h, i: (i, qoff + h))
    kblk = pl.BlockSpec((m, HEAD_DIM), lambda h, i: (0, h))
    vblk = pl.BlockSpec((m, HEAD_DIM), lambda h, i: (0, N_MEM_HEADS + h))
    row = pl.BlockSpec((1, HEAD_DIM), lambda h, i: (0, 0))
    return qblk, kblk, vblk, row


def _mem_fwd(p, qoff, mkv, gq, gk, tq):
    t, m = p.shape[0], mkv.shape[0]
    qblk, kblk, vblk, row = _mem_specs(t, m, tq, qoff)

    def body(q_ref, k_ref, v_ref, gq_ref, gk_ref, o_ref):
        o_ref[...] = _mem_fn(q_ref[...], k_ref[...], v_ref[...], gq_ref[...], gk_ref[...])

    return pl.pallas_call(
        body, name="mem_fwd", grid=(N_MEM_HEADS, t // tq), in_specs=[qblk, kblk, vblk, row, row],
        out_specs=pl.BlockSpec((tq, HEAD_DIM), lambda h, i: (i, h)),
        out_shape=jax.ShapeDtypeStruct((t, N_MEM_HEADS * HEAD_DIM), F32),
        compiler_params=_cparams(("parallel", "parallel")),
    )(p, mkv, mkv, gq, gk)


def _mem_bwd(p, qoff, mkv, gq, gk, dmix, dooff, tq):
    t, m = p.shape[0], mkv.shape[0]
    qblk, kblk, vblk, row = _mem_specs(t, m, tq, qoff)

    def body(q_ref, k_ref, v_ref, gq_ref, gk_ref, do_ref, dq_ref, dkv_k_ref, dkv_v_ref, dgq_ref, dgk_ref):
        h, i = pl.program_id(0), pl.program_id(1)

        @pl.when((h == 0) & (i == 0))
        def _():
            dgq_ref[...] = jnp.zeros_like(dgq_ref)
            dgk_ref[...] = jnp.zeros_like(dgk_ref)

        @pl.when(i == 0)
        def _():
            dkv_k_ref[...] = jnp.zeros_like(dkv_k_ref)
            dkv_v_ref[...] = jnp.zeros_like(dkv_v_ref)

        _, vjp = jax.vjp(_mem_fn, q_ref[...], k_ref[...], v_ref[...], gq_ref[...], gk_ref[...])
        dq, dk, dv, dgq, dgk = vjp(do_ref[...])
        dq_ref[...] = dq
        dkv_k_ref[...] += dk
        dkv_v_ref[...] += dv
        dgq_ref[...] += dgq
        dgk_ref[...] += dgk

    oblk = pl.BlockSpec((tq, HEAD_DIM), lambda h, i: (i, h))
    kout = pl.BlockSpec((m, HEAD_DIM), lambda h, i: (0, h))
    half = jax.ShapeDtypeStruct((m, N_MEM_HEADS * HEAD_DIM), F32)
    rshape = jax.ShapeDtypeStruct((1, HEAD_DIM), F32)
    return pl.pallas_call(
        body, name="mem_bwd", grid=(N_MEM_HEADS, t // tq),
        in_specs=[qblk, kblk, vblk, row, row, pl.BlockSpec((tq, HEAD_DIM), lambda h, i: (i, dooff + h))],
        out_specs=[oblk, kout, kout, row, row],
        out_shape=[jax.ShapeDtypeStruct((t, N_MEM_HEADS * HEAD_DIM), F32), half, half, rshape, rshape],
        compiler_params=_cparams(("arbitrary", "arbitrary")),
    )(p, mkv, mkv, gq, gk, dmix)


def _shift_down(x, s):
    if s == 0:
        return x
    r = lax.broadcasted_iota(jnp.int32, x.shape, 0)
    return jnp.where(r >= s, pltpu.roll(x, s, 0), 0.0)


def _shift_up(x, s):
    if s == 0:
        return x
    n = x.shape[0]
    r = lax.broadcasted_iota(jnp.int32, x.shape, 0)
    return jnp.where(r < n - s, pltpu.roll(x, n - s, 0), 0.0)


def _conv_fn(x0, x1, x2, x3, w0, w1, w2, w3, kind):
    y = _silu(x0 * w0 + x1 * w1 + x2 * w2 + x3 * w3)
    if kind == 2:
        return y
    y = y * lax.rsqrt(jnp.sum(y * y, axis=-1, keepdims=True) + NORM_EPS)
    return y * (HEAD_DIM ** -0.5) if kind == 0 else y


def _conv_fwd(p, off, conv_w, ng):
    t = p.shape[0]

    def body(x_ref, w_ref, o_ref):
        kind = pl.program_id(0) // ng
        x = x_ref[...]
        xs = [_shift_down(x, CONV_WIDTH - 1 - j) for j in range(CONV_WIDTH)]
        ws = [w_ref[j:j + 1, :] for j in range(CONV_WIDTH)]
        for kd in range(3):
            @pl.when(kind == kd)
            def _(kd=kd):
                o_ref[...] = _conv_fn(*xs, *ws, kd)

    return pl.pallas_call(
        body, name="gdn_conv_fwd", grid=(3 * ng,),
        in_specs=[pl.BlockSpec((t, HEAD_DIM), lambda c: (0, off + c)),
                  pl.BlockSpec((CONV_WIDTH, HEAD_DIM), lambda c: (0, c))],
        out_specs=pl.BlockSpec((t, HEAD_DIM), lambda c: (0, c)),
        out_shape=jax.ShapeDtypeStruct((t, 3 * ng * HEAD_DIM), F32),
        compiler_params=_cparams(("parallel",)),
    )(p, conv_w)


def _conv_bwd(p, off, conv_w, dy, ng):
    t = p.shape[0]

    def body(x_ref, w_ref, dy_ref, dx_ref, dw_ref):
        kind = pl.program_id(0) // ng
        x = x_ref[...]
        xs = [_shift_down(x, CONV_WIDTH - 1 - j) for j in range(CONV_WIDTH)]
        ws = [w_ref[j:j + 1, :] for j in range(CONV_WIDTH)]
        for kd in range(3):
            @pl.when(kind == kd)
            def _(kd=kd):
                _, vjp = jax.vjp(functools.partial(_conv_fn, kind=kd), *xs, *ws)
                g = vjp(dy_ref[...])
                dx = _shift_up(g[0], CONV_WIDTH - 1)
                for j in range(1, CONV_WIDTH):
                    dx = dx + _shift_up(g[j], CONV_WIDTH - 1 - j)
                dx_ref[...] = dx
                for j in range(CONV_WIDTH):
                    dw_ref[j:j + 1, :] = g[CONV_WIDTH + j]

    blk = pl.BlockSpec((t, HEAD_DIM), lambda c: (0, c))
    wblk = pl.BlockSpec((CONV_WIDTH, HEAD_DIM), lambda c: (0, c))
    return pl.pallas_call(
        body, name="gdn_conv_bwd", grid=(3 * ng,),
        in_specs=[pl.BlockSpec((t, HEAD_DIM), lambda c: (0, off + c)), wblk, blk],
        out_specs=[blk, wblk],
        out_shape=[jax.ShapeDtypeStruct((t, 3 * ng * HEAD_DIM), F32),
                   jax.ShapeDtypeStruct((CONV_WIDTH, 3 * ng * HEAD_DIM), F32)],
        compiler_params=_cparams(("parallel",)),
    )(p, conv_w, dy)


def _chunk_fn(q, k, v, gcol, grow, bcol, state):
    c = q.shape[0]
    r = lax.broadcasted_iota(jnp.int32, (c, c), 0)
    e = lax.broadcasted_iota(jnp.int32, (c, c), 1)
    tril, strict = e <= r, e < r
    gc_col = jnp.sum(jnp.where(tril, grow, 0.0), axis=1, keepdims=True)
    gc_row = jnp.sum(jnp.where(r <= e, gcol, 0.0), axis=0, keepdims=True)
    g_last = jnp.sum(gcol, axis=0, keepdims=True)
    decay = jnp.exp(jnp.where(tril, gc_col - gc_row, NEG))
    kb, vb = k * bcol, v * bcol
    lower = jnp.where(strict, _nt(kb, k) * decay, 0.0)
    inv = jnp.where(r == e, 1.0, 0.0) - lower
    pw = lower
    for _ in range(int(math.log2(c)) - 1):
        pw = _nn_hi(pw, pw)
        inv = inv + _nn_hi(inv, pw)
    u = _nn_hi(inv, vb)
    w = _nn_hi(inv, kb * jnp.exp(gc_col))
    attn = jnp.where(tril, _nt(q, k) * decay, 0.0)
    v_new = u - _nn(w, state)
    o = _nn(q * jnp.exp(gc_col), state) + _nn(attn, v_new)
    new_state = state * jnp.exp(g_last) + _tn(k * jnp.exp(g_last - gc_col), v_new)
    return o, new_state


def _gdn_fwd(qkv, gcol, grow, bcol, ng):
    t = qkv.shape[0]
    nch = t // CHUNK

    def body(q_ref, k_ref, v_ref, gc_ref, gr_ref, bc_ref, o_ref, st_ref, state):
        @pl.when(pl.program_id(1) == 0)
        def _():
            state[...] = jnp.zeros_like(state)

        st_ref[0, 0] = state[...]
        o, new = _chunk_fn(q_ref[...], k_ref[...], v_ref[...], gc_ref[0], gr_ref[0, 0], bc_ref[0], state[...])
        o_ref[...] = o
        state[...] = new

    col = pl.BlockSpec((1, CHUNK, 1), lambda h, i: (h, i, 0))
    return pl.pallas_call(
        body, name="gdn_fwd", grid=(ng, nch),
        in_specs=[pl.BlockSpec((CHUNK, HEAD_DIM), lambda h, i: (i, h)),
                  pl.BlockSpec((CHUNK, HEAD_DIM), lambda h, i: (i, ng + h)),
                  pl.BlockSpec((CHUNK, HEAD_DIM), lambda h, i: (i, 2 * ng + h)),
                  col, pl.BlockSpec((1, 1, 1, CHUNK), lambda h, i: (h, i, 0, 0)), col],
        out_specs=[pl.BlockSpec((CHUNK, HEAD_DIM), lambda h, i: (i, h)),
                   pl.BlockSpec((1, 1, HEAD_DIM, HEAD_DIM), lambda h, i: (h, i, 0, 0))],
        out_shape=[jax.ShapeDtypeStruct((t, ng * HEAD_DIM), F32),
                   jax.ShapeDtypeStruct((ng, nch, HEAD_DIM, HEAD_DIM), F32)],
        scratch_shapes=[pltpu.VMEM((HEAD_DIM, HEAD_DIM), F32)],
        compiler_params=_cparams(("parallel", "arbitrary")),
    )(qkv, qkv, qkv, gcol, grow, bcol)


def _gdn_bwd(qkv, gcol, grow, bcol, states, do, ng):
    t = qkv.shape[0]
    nch = t // CHUNK

    def body(q_ref, k_ref, v_ref, gc_ref, gr_ref, bc_ref, st_ref, do_ref,
             dqkv_q, dqkv_k, dqkv_v, dgc_ref, dgr_ref, dbc_ref, dstate):
        @pl.when(pl.program_id(1) == 0)
        def _():
            dstate[...] = jnp.zeros_like(dstate)

        _, vjp = jax.vjp(_chunk_fn, q_ref[...], k_ref[...], v_ref[...], gc_ref[0], gr_ref[0, 0], bc_ref[0],
                         st_ref[0, 0])
        dq, dk, dv, dgc, dgr, dbc, dst = vjp((do_ref[...], dstate[...]))
        dqkv_q[...] = dq
        dqkv_k[...] = dk
        dqkv_v[...] = dv
        dgc_ref[0] = dgc
        dgr_ref[0, 0] = dgr
        dbc_ref[0] = dbc
        dstate[...] = dst

    rev = lambda i: nch - 1 - i
    blk = lambda o: pl.BlockSpec((CHUNK, HEAD_DIM), lambda h, i: (rev(i), o * ng + h))
    col = pl.BlockSpec((1, CHUNK, 1), lambda h, i: (h, rev(i), 0))
    rowv = pl.BlockSpec((1, 1, 1, CHUNK), lambda h, i: (h, rev(i), 0, 0))
    hb = pl.BlockSpec((CHUNK, HEAD_DIM), lambda h, i: (rev(i), h))
    wide = jax.ShapeDtypeStruct((t, ng * HEAD_DIM), F32)
    cshape = jax.ShapeDtypeStruct((ng, t, 1), F32)
    return pl.pallas_call(
        body, name="gdn_bwd", grid=(ng, nch),
        in_specs=[blk(0), blk(1), blk(2), col, rowv, col,
                  pl.BlockSpec((1, 1, HEAD_DIM, HEAD_DIM), lambda h, i: (h, rev(i), 0, 0)), hb],
        out_specs=[hb, hb, hb, col, rowv, col],
        out_shape=[wide, wide, wide, cshape, jax.ShapeDtypeStruct((ng, nch, 1, CHUNK), F32), cshape],
        scratch_shapes=[pltpu.VMEM((HEAD_DIM, HEAD_DIM), F32)],
        compiler_params=_cparams(("parallel", "arbitrary")),
    )(qkv, qkv, qkv, gcol, grow, bcol, states, do)


def _swiglu_fn(gate, up):
    return _silu(gate) * up


def _swiglu_fwd(gu, ff):
    t = gu.shape[0]
    tr, tc = _tile(t, 512, 8), _tile(ff, 1024)
    nc = ff // tc

    def body(g_ref, u_ref, o_ref):
        o_ref[...] = _swiglu_fn(g_ref[...], u_ref[...]).astype(BF16)

    return pl.pallas_call(
        body, name="swiglu_fwd", grid=(t // tr, nc),
        in_specs=[pl.BlockSpec((tr, tc), lambda r, c: (r, c)), pl.BlockSpec((tr, tc), lambda r, c: (r, nc + c))],
        out_specs=pl.BlockSpec((tr, tc), lambda r, c: (r, c)),
        out_shape=jax.ShapeDtypeStruct((t, ff), BF16),
        compiler_params=_cparams(("parallel", "parallel")),
    )(gu, gu)


def _swiglu_bwd(gu, dact, ff):
    t = gu.shape[0]
    tr, tc = _tile(t, 512, 8), _tile(ff, 1024)
    nc = ff // tc

    def body(g_ref, u_ref, d_ref, o_ref):
        _, vjp = jax.vjp(_swiglu_fn, g_ref[...], u_ref[...])
        dg, du = vjp(d_ref[...])

        @pl.when(pl.program_id(1) < nc)
        def _():
            o_ref[...] = dg.astype(BF16)

        @pl.when(pl.program_id(1) >= nc)
        def _():
            o_ref[...] = du.astype(BF16)

    lo = pl.BlockSpec((tr, tc), lambda r, c: (r, c % nc))
    hi = pl.BlockSpec((tr, tc), lambda r, c: (r, nc + c % nc))
    return pl.pallas_call(
        body, name="swiglu_bwd", grid=(t // tr, 2 * nc), in_specs=[lo, hi, lo],
        out_specs=pl.BlockSpec((tr, tc), lambda r, c: (r, c)),
        out_shape=jax.ShapeDtypeStruct((t, 2 * ff), BF16),
        compiler_params=_cparams(("parallel", "parallel")),
    )(gu, gu, dact)


def _loss_head(h2, target):
    t, d = h2.shape
    tr = _tile(t, 256, 8)

    def body(h_ref, t_ref, l_ref, d_ref):
        @pl.when(pl.program_id(0) == 0)
        def _():
            l_ref[...] = jnp.zeros_like(l_ref)

        err = h_ref[...] - t_ref[...]
        d_ref[...] = err * (1.0 / d)
        part = 0.5 * jnp.sum(jnp.mean(err * err, axis=-1, keepdims=True), axis=0, keepdims=True)
        lane = lax.broadcasted_iota(jnp.int32, (8, HEAD_DIM), 1)
        row = lax.broadcasted_iota(jnp.int32, (8, HEAD_DIM), 0)
        l_ref[...] += jnp.where((lane == 0) & (row == 0), part, 0.0)

    blk = pl.BlockSpec((tr, d), lambda r: (r, 0))
    return pl.pallas_call(
        body, name="loss_head", grid=(t // tr,), in_specs=[blk, blk],
        out_specs=[pl.BlockSpec((8, HEAD_DIM), lambda r: (0, 0)), blk],
        out_shape=[jax.ShapeDtypeStruct((8, HEAD_DIM), F32), jax.ShapeDtypeStruct((t, d), F32)],
        compiler_params=_cparams(("arbitrary",)),
    )(h2, target)


def _adamw(w, g, m, v, *, g2=None, name):
    r, c = w.shape
    tr = _tile(r, max(8, (1 << 19) // c // 8 * 8), 8)

    def body(*refs):
        if g2 is None:
            w_ref, g_ref, m_ref, v_ref, go_ref, d_ref, mo_ref, vo_ref = refs
            gr = g_ref[...]
        else:
            w_ref, g_ref, g2_ref, m_ref, v_ref, go_ref, d_ref, mo_ref, vo_ref = refs
            gr = g_ref[...] + g2_ref[...]
        mn = ADAM_B1 * m_ref[...] + (1.0 - ADAM_B1) * gr
        vn = ADAM_B2 * v_ref[...] + (1.0 - ADAM_B2) * (gr * gr)
        m_hat = mn / (1.0 - ADAM_B1 ** ADAM_STEP)
        v_hat = vn / (1.0 - ADAM_B2 ** ADAM_STEP)
        go_ref[...] = gr
        d_ref[...] = -ADAM_LR * (m_hat / (jnp.sqrt(v_hat) + ADAM_EPS) + ADAM_WD * w_ref[...])
        mo_ref[...] = mn
        vo_ref[...] = vn

    blk = pl.BlockSpec((tr, c), lambda i: (i, 0))
    n_in = 4 if g2 is None else 5
    ins = [w, g] + ([g2] if g2 is not None else []) + [m, v]
    return pl.pallas_call(
        body, name=name, grid=(r // tr,), in_specs=[blk] * n_in, out_specs=[blk] * 4,
        out_shape=[jax.ShapeDtypeStruct((r, c), F32)] * 4,
        compiler_params=_cparams(("parallel",)),
    )(*ins)


class _Layout:
    def __init__(self, d):
        nh = d // HEAD_DIM
        self.nm = N_MEM_HEADS
        self.nf = (nh - self.nm) // 2
        self.ng = nh - self.nm - self.nf
        nf, ng, nm = self.nf, self.ng, self.nm
        self.o_fq, self.o_fk, self.o_fv = 0, nf, 2 * nf
        self.o_gq = 3 * nf
        self.o_gz = 3 * nf + 3 * ng
        self.o_mq = 3 * nf + 4 * ng
        self.o_sm = self.o_mq + nm
        self.blocks = -(-(self.o_sm + 1) // 8) * 8
        self.cols = self.blocks * HEAD_DIM
        hd = HEAD_DIM
        sizes = [nf * hd, nf * hd, nf * hd, nf, 3 * ng * hd, ng * hd, ng, ng, nm * hd]
        starts = [sum(sizes[:i]) for i in range(len(sizes))]
        self.ref = list(zip(starts, sizes))
        self.in_cols = sum(sizes)

    def regroup(self, w):
        part = lambda i: w[:, self.ref[i][0]:self.ref[i][0] + self.ref[i][1]]
        pieces = [part(0), part(1), part(2), part(4), part(5), part(8), part(3), part(6), part(7)]
        pad = self.cols - self.in_cols
        return jnp.concatenate(pieces + [jnp.zeros((w.shape[0], pad), w.dtype)], axis=1)

    def ungroup(self, g):
        hd, nf, ng, nm = HEAD_DIM, self.nf, self.ng, self.nm
        sm = self.o_sm * hd
        return jnp.concatenate([
            g[:, :3 * nf * hd], g[:, sm:sm + nf], g[:, self.o_gq * hd:self.o_gz * hd],
            g[:, self.o_gz * hd:self.o_mq * hd], g[:, sm + nf:sm + nf + ng], g[:, sm + nf + ng:sm + nf + 2 * ng],
            g[:, self.o_mq * hd:self.o_sm * hd]], axis=1)


def _lane_row(pieces):
    row = jnp.zeros((1, HEAD_DIM), F32)
    for off, a in pieces:
        row = lax.dynamic_update_slice(row, a.astype(F32), (0, off))
    return row


def _local_step(x, mem, target, win, wmkv, wout, wgu, wd, sp):
    t, d = x.shape
    lay = _Layout(d)
    nf, ng, nm, hd = lay.nf, lay.ng, lay.nm, HEAD_DIM
    ff = wd.shape[0]
    nch = t // CHUNK
    tq = _tile(t, 256)
    tk = tq

    u = _norm_fwd(x, 0, sp["norm_mix"], 1, d, BF16, name="norm_mix_fwd")
    p = _mm(u, win, name="mm_in")
    pa = _lane_row([(nf, sp["gdn_a_log"])])
    pb = _lane_row([(0, sp["fox_f_bias"]), (nf, sp["gdn_dt_bias"])])
    vals, csum = _small_fwd(p, lay.o_sm, pa, pb, nf, ng)

    c_t = csum[:, :nf].T
    cc, cr = c_t.reshape(nf, t, 1), c_t.reshape(nf, t // tk, 1, tk)
    fq = _norm_fwd(p, lay.o_fq, sp["fox_q_norm"], nf, hd, BF16, name="fox_qnorm_fwd")
    fk = _norm_fwd(p, lay.o_fk, sp["fox_k_norm"], nf, hd, BF16, name="fox_knorm_fwd")
    fv = p[:, lay.o_fv * hd:(lay.o_fv + nf) * hd].astype(BF16)
    o_fox, lse = _fox_fwd(fq, fk, fv, cc, cr, nf, tq, tk)

    qkv = _conv_fwd(p, lay.o_gq, sp["gdn_conv"], ng)
    g_t, b_t = vals[:, nf:nf + ng].T, vals[:, nf + ng:nf + 2 * ng].T
    gcol, grow, bcol = g_t.reshape(ng, t, 1), g_t.reshape(ng, nch, 1, CHUNK), b_t.reshape(ng, t, 1)
    o_g, states = _gdn_fwd(qkv, gcol, grow, bcol, ng)
    o_gdn = _norm_fwd(o_g, 0, sp["gdn_out_norm"], ng, hd, BF16, z=p, zoff=lay.o_gz, name="gdn_out_fwd")

    mem_n = _norm_fwd(mem, 0, sp["mem_norm"], 1, d, BF16, name="mem_norm_fwd")
    mkv = _mm(mem_n, wmkv, name="mm_memkv")
    o_mem = _mem_fwd(p, lay.o_mq, mkv, sp["mem_q_norm"], sp["mem_k_norm"], tq)

    mix = jnp.concatenate([o_fox.astype(BF16), o_gdn, o_mem.astype(BF16)], axis=1)
    h1 = _mm(mix, wout, res=x, name="mm_out")
    n2 = _norm_fwd(h1, 0, sp["norm_ffn"], 1, d, BF16, name="norm_ffn_fwd")
    gu = _mm(n2, wgu, name="mm_gate_up")
    act = _swiglu_fwd(gu, ff)
    h2 = _mm(act, wd, res=h1, name="mm_down")
    loss_blk, dh2 = _loss_head(h2, target)

    g = {}
    g["w_down"] = _mm(act, dh2, ta=True, out_dtype=BF16, name="mm_dw_down")
    dact = _mm(dh2, wd, tb=True, name="mm_dact")
    dgu = _swiglu_bwd(gu, dact, ff)
    g["w_gate_up"] = _mm(n2, dgu, ta=True, out_dtype=BF16, name="mm_dw_gate_up")
    dn2 = _mm(dgu, wgu, tb=True, name="mm_dn2")
    dh1, g["norm_ffn"] = _norm_bwd(h1, 0, sp["norm_ffn"], dn2, 0, 1, d, res=dh2, name="norm_ffn_bwd")
    g["w_out"] = _mm(mix, dh1, ta=True, out_dtype=BF16, name="mm_dw_out")
    dmix = _mm(dh1, wout, tb=True, name="mm_dmix")

    dmq, dmk, dmv, g["mem_q_norm"], g["mem_k_norm"] = _mem_bwd(
        p, lay.o_mq, mkv, sp["mem_q_norm"], sp["mem_k_norm"], dmix, nf + ng, tq)
    dmkv = jnp.concatenate([dmk, dmv], axis=1)
    g["w_mem_kv"] = _mm(mem_n, dmkv, ta=True, out_dtype=BF16, name="mm_dw_memkv")
    dmem_n = _mm(dmkv, wmkv, tb=True, name="mm_dmem")
    _, g["mem_norm"] = _norm_bwd(mem, 0, sp["mem_norm"], dmem_n, 0, 1, d, name="mem_norm_bwd")

    do_g, dgz, g["gdn_out_norm"] = _norm_bwd(o_g, 0, sp["gdn_out_norm"], dmix, nf, ng, hd, z=p, zoff=lay.o_gz,
                                             name="gdn_out_bwd")
    dq, dk, dv, dgc, dgr, dbc = _gdn_bwd(qkv, gcol, grow, bcol, states, do_g, ng)
    dgqkv, g["gdn_conv"] = _conv_bwd(p, lay.o_gq, sp["gdn_conv"], jnp.concatenate([dq, dk, dv], axis=1), ng)
    dg_t = dgc.reshape(ng, t) + dgr.reshape(ng, t)
    db_t = dbc.reshape(ng, t)

    dfq_n, dfk_n, dfv, dcc, dcr = _fox_bwd(fq, fk, fv, cc, cr, o_fox, lse, dmix, nf, tq, tk)
    dfq, g["fox_q_norm"] = _norm_bwd(p, lay.o_fq, sp["fox_q_norm"], dfq_n, 0, nf, hd, name="fox_qnorm_bwd")
    dfk, g["fox_k_norm"] = _norm_bwd(p, lay.o_fk, sp["fox_k_norm"], dfk_n, 0, nf, hd, name="fox_knorm_bwd")
    dc_t = dcc.reshape(nf, t) + dcr.reshape(nf, t)

    lanes_left = hd - nf - 2 * ng
    dvals = jnp.concatenate([jnp.zeros((t, nf), F32), dg_t.T, db_t.T, jnp.zeros((t, lanes_left), F32)], axis=1)
    dcsum = jnp.concatenate([dc_t.T, jnp.zeros((t, hd - nf), F32)], axis=1)
    dsm, dpa, dpb = _small_bwd(p, lay.o_sm, pa, pb, dvals, dcsum, nf, ng)
    g["fox_f_bias"] = dpb[:, :nf]
    g["gdn_dt_bias"] = dpb[:, nf:nf + ng]
    g["gdn_a_log"] = dpa[:, nf:nf + ng]

    pad = jnp.zeros((t, lay.cols - (lay.o_sm + 1) * hd), F32)
    dp = jnp.concatenate([dfq, dfk, dfv, dgqkv, dgz, dmq, dsm, pad], axis=1)
    g["w_in"] = _mm(u, dp, ta=True, out_dtype=BF16, name="mm_dw_in")
    du = _mm(dp, win, tb=True, name="mm_du")
    dx, g["norm_mix"] = _norm_bwd(x, 0, sp["norm_mix"], du, 0, 1, d, res=dh1, name="norm_mix_bwd")
    return loss_blk, dx, g


ANY = pl.BlockSpec(memory_space=pl.ANY)


def _me():
    x, y, c = lax.axis_index("x"), lax.axis_index("y"), lax.axis_index("c")
    chips = [(1 - x, y), (x, 1 - y), (1 - x, 1 - y)]
    return x, y, c, chips


def _slab(ref, axis, rows, cols, k, h):
    half = rows // 2
    if axis == 0:
        return ref.at[pl.ds(k * rows + h * half, half), :]
    return ref.at[pl.ds(h * half, half), pl.ds(k * cols, cols)]


def _remote(src, dst, send_sem, recv_sem, dev):
    return pltpu.make_async_remote_copy(src_ref=src, dst_ref=dst, send_sem=send_sem, recv_sem=recv_sem,
                                        device_id=dev, device_id_type=MESH)


def _gather_weights(shards, axes):
    n = len(shards)
    shapes = [s.shape for s in shards]

    def body(*refs):
        src, dst = refs[:n], refs[n:2 * n]
        send_sems, recv_sems, local_sems = refs[2 * n:]
        x, y, c, chips = _me()
        k = 2 * x + y
        sibling = (x, y, 1 - c)
        local = []
        for w in range(n):
            r, cl = shapes[w]
            for h in range(2):
                cp = pltpu.make_async_copy(src[w].at[pl.ds(h * (r // 2), r // 2), :],
                                           _slab(dst[w], axes[w], r, cl, k, h), local_sems.at[2 * w + h])
                cp.start()
                local.append(cp)
        sends = []
        for w in range(n):
            r, cl = shapes[w]
            for j, (px, py) in enumerate(chips):
                place = _slab(dst[w], axes[w], r, cl, k, c)
                cp = _remote(src[w].at[pl.ds(c * (r // 2), r // 2), :], place,
                             send_sems.at[6 * w + j], recv_sems.at[6 * w + j], (px, py, c))
                cp.start()
                sends.append(cp)
        for w in range(n):
            r, cl = shapes[w]
            for j, (px, py) in enumerate(chips):
                got = _slab(dst[w], axes[w], r, cl, 2 * px + py, c)
                _remote(got, got, send_sems.at[6 * w + j], recv_sems.at[6 * w + j], (px, py, c)).wait_recv()
                cp = _remote(got, got, send_sems.at[6 * w + 3 + j], recv_sems.at[6 * w + 3 + j], sibling)
                cp.start()
                sends.append(cp)
        for w in range(n):
            r, cl = shapes[w]
            for j, (px, py) in enumerate(chips):
                got = _slab(dst[w], axes[w], r, cl, 2 * px + py, 1 - c)
                _remote(got, got, send_sems.at[6 * w + 3 + j], recv_sems.at[6 * w + 3 + j], sibling).wait_recv()
        for cp in sends:
            cp.wait_send()
        for cp in local:
            cp.wait()

    out_shape = [jax.ShapeDtypeStruct((4 * r, cl) if a == 0 else (r, 4 * cl), s.dtype)
                 for (r, cl), a, s in zip(shapes, axes, shards)]
    return pl.pallas_call(
        body, name="gather_weights", in_specs=[ANY] * n, out_specs=[ANY] * n, out_shape=out_shape,
        scratch_shapes=[pltpu.SemaphoreType.DMA((6 * n,)), pltpu.SemaphoreType.DMA((6 * n,)),
                        pltpu.SemaphoreType.DMA((2 * n,))],
    )(*shards)


def _pair_exchange(fulls, axes, shapes):
    n = len(fulls)

    def body(*refs):
        src, dst = refs[:n], refs[n:2 * n]
        send_sems, recv_sems = refs[2 * n:]
        x, y, c, _ = _me()
        sibling = (x, y, 1 - c)
        cps = []
        for w in range(n):
            r, cl = shapes[w]
            for j in range(4):
                cp = _remote(_slab(src[w], axes[w], r, cl, j, 1 - c), dst[w].at[j],
                             send_sems.at[4 * w + j], recv_sems.at[4 * w + j], sibling)
                cp.start()
                cps.append(cp)
        for cp in cps:
            cp.wait()

    out_shape = [jax.ShapeDtypeStruct((4, r // 2, cl), f.dtype) for (r, cl), f in zip(shapes, fulls)]
    return pl.pallas_call(
        body, name="reduce_pair_exchange", in_specs=[ANY] * n, out_specs=[ANY] * n, out_shape=out_shape,
        scratch_shapes=[pltpu.SemaphoreType.DMA((4 * n,)), pltpu.SemaphoreType.DMA((4 * n,))],
    )(*fulls)


def _chip_exchange(parts):
    n = len(parts)

    def body(*refs):
        src, dst = refs[:n], refs[n:2 * n]
        send_sems, recv_sems, local_sems = refs[2 * n:]
        x, y, c, chips = _me()
        k = 2 * x + y
        cps = []
        for w in range(n):
            own = pltpu.make_async_copy(src[w].at[k], dst[w].at[k], local_sems.at[w])
            own.start()
            cps.append(own)
        for w in range(n):
            for j, (px, py) in enumerate(chips):
                cp = _remote(src[w].at[2 * px + py], dst[w].at[k], send_sems.at[3 * w + j],
                             recv_sems.at[3 * w + j], (px, py, c))
                cp.start()
                cps.append(cp)
        for w in range(n):
            for j, (px, py) in enumerate(chips):
                got = dst[w].at[2 * px + py]
                _remote(got, got, send_sems.at[3 * w + j], recv_sems.at[3 * w + j], (px, py, c)).wait_recv()
        for w in range(n):
            cps[w].wait()
        for cp in cps[n:]:
            cp.wait_send()

    return pl.pallas_call(
        body, name="reduce_chip_exchange", in_specs=[ANY] * n, out_specs=[ANY] * n,
        out_shape=[jax.ShapeDtypeStruct(p.shape, p.dtype) for p in parts],
        scratch_shapes=[pltpu.SemaphoreType.DMA((3 * n,)), pltpu.SemaphoreType.DMA((3 * n,)),
                        pltpu.SemaphoreType.DMA((n,))],
    )(*parts)


def _half_swap(halves):
    n = len(halves)

    def body(*refs):
        src, dst = refs[:n], refs[n:2 * n]
        send_sems, recv_sems, local_sems = refs[2 * n:]
        x, y, c, _ = _me()
        sibling = (x, y, 1 - c)
        cps = []
        for w in range(n):
            half = halves[w].shape[0]
            mine = dst[w].at[pl.ds(c * half, half), :]
            own = pltpu.make_async_copy(src[w], mine, local_sems.at[w])
            own.start()
            cp = _remote(src[w], mine, send_sems.at[w], recv_sems.at[w], sibling)
            cp.start()
            cps.append((own, cp))
        for w in range(n):
            half = halves[w].shape[0]
            other = dst[w].at[pl.ds((1 - c) * half, half), :]
            _remote(other, other, send_sems.at[w], recv_sems.at[w], sibling).wait_recv()
        for own, cp in cps:
            own.wait()
            cp.wait_send()

    return pl.pallas_call(
        body, name="reduce_half_swap", in_specs=[ANY] * n, out_specs=[ANY] * n,
        out_shape=[jax.ShapeDtypeStruct((2 * h.shape[0], h.shape[1]), h.dtype) for h in halves],
        scratch_shapes=[pltpu.SemaphoreType.DMA((n,)), pltpu.SemaphoreType.DMA((n,)), pltpu.SemaphoreType.DMA((n,))],
    )(*halves)


def _add_parts(a, b, name):
    _, r, c = a.shape
    tr, tc = _tile(r, 256, 8), _tile(c, 2048)

    def body(a_ref, b_ref, o_ref):
        o_ref[...] = (a_ref[...].astype(F32) + b_ref[...].astype(F32)).astype(BF16)

    blk = pl.BlockSpec((1, tr, tc), lambda j, i, l: (j, i, l))
    return pl.pallas_call(
        body, name=name, grid=(4, r // tr, c // tc), in_specs=[blk, blk], out_specs=blk,
        out_shape=jax.ShapeDtypeStruct(a.shape, BF16),
        compiler_params=_cparams(("parallel", "parallel", "parallel")),
    )(a, b)


def _sum_slots(a, name):
    _, r, c = a.shape
    tr, tc = _tile(r, 256, 8), _tile(c, 2048)

    def body(a_ref, o_ref):
        v = a_ref[...].astype(F32)
        o_ref[...] = ((v[0] + v[1]) + v[2]) + v[3]

    return pl.pallas_call(
        body, name=name, grid=(r // tr, c // tc),
        in_specs=[pl.BlockSpec((4, tr, tc), lambda i, l: (0, i, l))],
        out_specs=pl.BlockSpec((tr, tc), lambda i, l: (i, l)),
        out_shape=jax.ShapeDtypeStruct((r, c), F32),
        compiler_params=_cparams(("parallel", "parallel")),
    )(a)


def _half_of(full, axis, rows, cols, c):
    half = rows // 2
    if axis == 0:
        v = full.reshape(4, 2, half, cols)
        return lax.dynamic_index_in_dim(v, c, 1, keepdims=False)
    v = full.reshape(2, half, 4, cols)
    return jnp.transpose(lax.dynamic_index_in_dim(v, c, 0, keepdims=False), (1, 0, 2))


def _reduce_grads(fulls, axes, shapes):
    c = lax.axis_index("c")
    from_sibling = _pair_exchange(fulls, axes, shapes)
    parts = [_add_parts(_half_of(f, a, r, cl, c), s, name=f"reduce_add_{w}")
             for w, (f, a, (r, cl), s) in enumerate(zip(fulls, axes, shapes, from_sibling))]
    slots = _chip_exchange(parts)
    halves = [_sum_slots(s, name=f"reduce_sum_{w}") for w, s in enumerate(slots)]
    return _half_swap(halves)


def _allreduce_small(pack):
    rows = pack.shape[0]

    def body(p_ref, o_ref, slots, send_sems, recv_sems):
        x, y, c, _ = _me()
        me = 4 * x + 2 * y + c
        slots[me] = p_ref[...]
        cps = []
        for r in range(1, 8):
            peer = (x ^ (r >> 2), y ^ ((r >> 1) & 1), c ^ (r & 1))
            cp = _remote(p_ref, slots.at[me], send_sems.at[r - 1], recv_sems.at[r - 1], peer)
            cp.start()
            cps.append(cp)
        for r in range(1, 8):
            frm = me ^ r
            _remote(slots.at[frm], slots.at[frm], send_sems.at[r - 1], recv_sems.at[r - 1], (x, y, c)).wait_recv()
        for cp in cps:
            cp.wait_send()
        acc = slots[0]
        for s in range(1, 8):
            acc = acc + slots[s]
        o_ref[...] = acc

    vm = pl.BlockSpec(memory_space=pltpu.VMEM)
    return pl.pallas_call(
        body, name="allreduce_small", in_specs=[vm], out_specs=vm,
        out_shape=jax.ShapeDtypeStruct(pack.shape, F32),
        scratch_shapes=[pltpu.VMEM((8, rows, HEAD_DIM), F32), pltpu.SemaphoreType.DMA((7,)),
                        pltpu.SemaphoreType.DMA((7,))],
    )(pack)


_ROWS = ["norm_mix", "norm_ffn", "mem_norm", "fox_q_norm", "fox_k_norm", "gdn_out_norm", "mem_q_norm",
         "mem_k_norm", "fox_f_bias", "gdn_a_log", "gdn_dt_bias"]


def _pack_rows(vals):
    out = []
    for name in _ROWS:
        v = vals[name].reshape(-1)
        n = -(-v.shape[0] // HEAD_DIM) * HEAD_DIM
        out.append(jnp.pad(v, (0, n - v.shape[0])).reshape(-1, HEAD_DIM))
    return jnp.concatenate(out, axis=0)


def _unpack_rows(pack, like):
    out, r = {}, 0
    for name in _ROWS:
        n = like[name].shape[-1]
        nr = -(-n // HEAD_DIM)
        out[name] = pack[r:r + nr].reshape(1, -1)[:, :n]
        r += nr
    return out, r


def kernel(x, mem, norm_mix, w_in, fox_f_bias, fox_q_norm, fox_k_norm, gdn_conv, gdn_a_log, gdn_dt_bias, gdn_out_norm, mem_norm, w_mem_kv, mem_q_norm, mem_k_norm, w_out, norm_ffn, w_gate_up, w_down, loss_target, m_norm_mix, m_w_in, m_fox_f_bias, m_fox_q_norm, m_fox_k_norm, m_gdn_conv, m_gdn_a_log, m_gdn_dt_bias, m_gdn_out_norm, m_mem_norm, m_w_mem_kv, m_mem_q_norm, m_mem_k_norm, m_w_out, m_norm_ffn, m_w_gate_up, m_w_down, v_norm_mix, v_w_in, v_fox_f_bias, v_fox_q_norm, v_fox_k_norm, v_gdn_conv, v_gdn_a_log, v_gdn_dt_bias, v_gdn_out_norm, v_mem_norm, v_w_mem_kv, v_mem_q_norm, v_mem_k_norm, v_w_out, v_norm_ffn, v_w_gate_up, v_w_down):
    a = dict(locals())
    d = x.shape[-1]
    lay = _Layout(d)
    chip = 2 * lax.axis_index("x") + lax.axis_index("y")
    small = {n: a[n] for n in _ROWS}
    big = ["w_in", "w_mem_kv", "w_out", "w_gate_up", "w_down"]
    axes = [0, 0, 0, 1, 0]

    conv_cols = gdn_conv.shape[-1]
    conv_n = CONV_WIDTH * conv_cols
    conv_rows = -(-conv_n // HEAD_DIM)
    conv_blk = jnp.pad(gdn_conv.reshape(-1), (0, 32 * HEAD_DIM - conv_n)).reshape(32, HEAD_DIM)
    shards = [lay.regroup(w_in[0]).astype(BF16), w_mem_kv[0].astype(BF16), w_out[0].astype(BF16),
              w_gate_up[0].astype(BF16), w_down[0].astype(BF16)]
    shapes = [s.shape for s in shards]
    win, wmkv, wout, wgu, wd, conv_all = _gather_weights(shards + [conv_blk], axes + [0])
    conv_full = conv_all.reshape(4, 32 * HEAD_DIM)[:, :conv_n].reshape(4, CONV_WIDTH, conv_cols)
    conv_full = jnp.transpose(conv_full, (1, 0, 2)).reshape(CONV_WIDTH, 4 * conv_cols)

    sp = dict(small)
    sp["gdn_conv"] = conv_full
    loss_blk, dx, g = _local_step(x[0], mem[0], loss_target[0], win, wmkv, wout, wgu, wd, sp)

    gbig = list(_reduce_grads([g[n] for n in big], axes, shapes))
    gbig[0] = lay.ungroup(gbig[0])
    gsmall = {n: g[n] for n in _ROWS}
    pack = jnp.concatenate([_pack_rows(gsmall), g["gdn_conv"].reshape(-1, HEAD_DIM), loss_blk], axis=0)
    pack = jnp.pad(pack, ((0, -pack.shape[0] % 8), (0, 0)))
    tot = _allreduce_small(pack)
    gs, r0 = _unpack_rows(tot, small)
    conv_g = tot[r0:r0 + CONV_WIDTH * 4 * conv_cols // HEAD_DIM].reshape(CONV_WIDTH, 4 * conv_cols)
    gs_conv = lax.dynamic_slice_in_dim(conv_g, chip * conv_cols, conv_cols, axis=1)
    loss = tot[r0 + CONV_WIDTH * 4 * conv_cols // HEAD_DIM, 0]

    out = {"loss": loss, "grad_x": dx[None]}
    for n, gsh in zip(big, gbig):
        res = _adamw(a[n][0], gsh, a["m_" + n][0], a["v_" + n][0], name="adamw_" + n)
        for pre, r in zip(["grad_", "delta_", "new_m_", "new_v_"], res):
            out[pre + n] = r[None]
    conv_pad = lambda v: jnp.pad(v.reshape(-1), (0, conv_rows * HEAD_DIM - conv_n)).reshape(conv_rows, HEAD_DIM)
    packs = []
    for src, cv in [(small, gdn_conv), (gs, gs_conv), ({n: a["m_" + n] for n in _ROWS}, m_gdn_conv),
                    ({n: a["v_" + n] for n in _ROWS}, v_gdn_conv)]:
        packs.append(jnp.concatenate([_pack_rows(src), conv_pad(cv)], axis=0))
    res = _adamw(*packs, name="adamw_small")
    for pre, r in zip(["grad_", "delta_", "new_m_", "new_v_"], res):
        vals, r1 = _unpack_rows(r, small)
        for n in _ROWS:
            out[pre + n] = vals[n]
        out[pre + "gdn_conv"] = r[r1:r1 + conv_rows].reshape(-1)[:conv_n].reshape(gdn_conv.shape)
    names = ["norm_mix", "w_in", "fox_f_bias", "fox_q_norm", "fox_k_norm", "gdn_conv", "gdn_a_log", "gdn_dt_bias",
             "gdn_out_norm", "mem_norm", "w_mem_kv", "mem_q_norm", "mem_k_norm", "w_out", "norm_ffn", "w_gate_up",
             "w_down"]
    return (out["loss"], out["grad_x"], *[out[p + n] for p in ["grad_", "delta_", "new_m_", "new_v_"] for n in names])
```

```python
import functools
import math

import jax
import jax.numpy as jnp
from jax import lax
from jax.experimental import pallas as pl
from jax.experimental.pallas import tpu as pltpu

F32, BF16 = jnp.float32, jnp.bfloat16
HEAD_DIM = 128
CHUNK = 64
N_MEM_HEADS = 4
CONV_WIDTH = 4
NORM_EPS = 1e-6
ADAM_LR, ADAM_B1, ADAM_B2, ADAM_EPS, ADAM_WD, ADAM_STEP = 0.001, 0.9, 0.999, 1e-08, 0.01, 10
VMEM_LIMIT = 48 * 1024 * 1024
NEG = -1e30
MESH = pl.DeviceIdType.MESH


def _cparams(sem=None, **kw):
    if sem is not None:
        kw["dimension_semantics"] = sem
    return pltpu.CompilerParams(vmem_limit_bytes=VMEM_LIMIT, **kw)


def _tile(n, target, mult=128):
    best = None
    d = mult
    while d <= min(n, target):
        if n % d == 0:
            best = d
        d += mult
    return best if best is not None else n


def _dot(a, b, dims, hi):
    if hi:
        return lax.dot_general(a, b, (dims, ((), ())), precision=lax.Precision.HIGHEST,
                               preferred_element_type=F32)
    return lax.dot_general(a.astype(BF16), b.astype(BF16), (dims, ((), ())), preferred_element_type=F32)


def _make_dots(hi):
    @jax.custom_vjp
    def nn(a, b):
        return _dot(a, b, ((1,), (0,)), hi)

    @jax.custom_vjp
    def nt(a, b):
        return _dot(a, b, ((1,), (1,)), hi)

    @jax.custom_vjp
    def tn(a, b):
        return _dot(a, b, ((0,), (0,)), hi)

    nn.defvjp(lambda a, b: (nn(a, b), (a, b)), lambda r, g: (nt(g, r[1]), tn(r[0], g)))
    nt.defvjp(lambda a, b: (nt(a, b), (a, b)), lambda r, g: (nn(g, r[1]), tn(g, r[0])))
    tn.defvjp(lambda a, b: (tn(a, b), (a, b)), lambda r, g: (nt(r[1], g), nn(r[0], g)))
    return nn, nt, tn


_nn, _nt, _tn = _make_dots(False)
_nn_hi, _nt_hi, _tn_hi = _make_dots(True)


def _sigmoid(x):
    return 1.0 / (1.0 + jnp.exp(-x))


@jax.custom_vjp
def _softplus(x):
    return jnp.maximum(x, 0.0) + jnp.log(1.0 + jnp.exp(-jnp.abs(x)))


_softplus.defvjp(lambda x: (_softplus(x), x), lambda x, g: (g * _sigmoid(x),))


def _silu(x):
    return x * _sigmoid(x)


def _rms_fn(x, gain, z=None):
    y = x * lax.rsqrt(jnp.mean(x * x, axis=-1, keepdims=True) + NORM_EPS) * gain
    if z is not None:
        y = y * _silu(z)
    return y


def _mm(a, b, *, ta=False, tb=False, out_dtype=F32, res=None, name):
    m = a.shape[1] if ta else a.shape[0]
    k = a.shape[0] if ta else a.shape[1]
    n = b.shape[0] if tb else b.shape[1]
    assert k == (b.shape[1] if tb else b.shape[0])
    tm, tn, tk = _tile(m, 1024), _tile(n, 1024), _tile(k, 512)
    nk = k // tk
    dims = ((0 if ta else 1,), (1 if tb else 0,))

    def body(*refs):
        if res is None:
            a_ref, b_ref, o_ref, acc = refs
        else:
            a_ref, b_ref, r_ref, o_ref, acc = refs
        kk = pl.program_id(2)

        @pl.when(kk == 0)
        def _():
            acc[...] = jnp.zeros_like(acc)

        acc[...] += lax.dot_general(a_ref[...].astype(BF16), b_ref[...].astype(BF16), (dims, ((), ())),
                                    preferred_element_type=F32)

        @pl.when(kk == nk - 1)
        def _():
            r = acc[...]
            if res is not None:
                r = r + r_ref[...]
            o_ref[...] = r.astype(out_dtype)

    a_spec = (pl.BlockSpec((tk, tm), lambda i, j, kk: (kk, i)) if ta
              else pl.BlockSpec((tm, tk), lambda i, j, kk: (i, kk)))
    b_spec = (pl.BlockSpec((tn, tk), lambda i, j, kk: (j, kk)) if tb
              else pl.BlockSpec((tk, tn), lambda i, j, kk: (kk, j)))
    o_spec = pl.BlockSpec((tm, tn), lambda i, j, kk: (i, j))
    ins, specs = [a, b], [a_spec, b_spec]
    if res is not None:
        ins.append(res)
        specs.append(o_spec)
    return pl.pallas_call(
        body, name=name, grid=(m // tm, n // tn, nk), in_specs=specs, out_specs=o_spec,
        out_shape=jax.ShapeDtypeStruct((m, n), out_dtype),
        scratch_shapes=[pltpu.VMEM((tm, tn), F32)],
        compiler_params=_cparams(("parallel", "parallel", "arbitrary")),
    )(*ins)


def _norm_fwd(x, xoff, gain, ncol, w, out_dtype, *, z=None, zoff=0, name):
    t = x.shape[0]
    tr = _tile(t, 256, 8)

    def body(*refs):
        if z is None:
            x_ref, g_ref, o_ref = refs
            y = _rms_fn(x_ref[...], g_ref[...])
        else:
            x_ref, g_ref, z_ref, o_ref = refs
            y = _rms_fn(x_ref[...], g_ref[...], z_ref[...])
        o_ref[...] = y.astype(out_dtype)

    ins = [x, gain]
    specs = [pl.BlockSpec((tr, w), lambda j, r: (r, xoff + j)), pl.BlockSpec((1, w), lambda j, r: (0, 0))]
    if z is not None:
        ins.append(z)
        specs.append(pl.BlockSpec((tr, w), lambda j, r: (r, zoff + j)))
    return pl.pallas_call(
        body, name=name, grid=(ncol, t // tr), in_specs=specs,
        out_specs=pl.BlockSpec((tr, w), lambda j, r: (r, j)),
        out_shape=jax.ShapeDtypeStruct((t, ncol * w), out_dtype),
        compiler_params=_cparams(("parallel", "parallel")),
    )(*ins)


def _norm_bwd(x, xoff, gain, dy, dyoff, ncol, w, *, z=None, zoff=0, res=None, name):
    t = x.shape[0]
    tr = _tile(t, 256, 8)

    def body(*refs):
        it = iter(refs)
        x_ref, g_ref = next(it), next(it)
        z_ref = next(it) if z is not None else None
        dy_ref = next(it)
        r_ref = next(it) if res is not None else None
        dx_ref = next(it)
        dz_ref = next(it) if z is not None else None
        dg_ref = next(it)

        @pl.when((pl.program_id(0) == 0) & (pl.program_id(1) == 0))
        def _():
            dg_ref[...] = jnp.zeros_like(dg_ref)

        args = (x_ref[...], g_ref[...]) + ((z_ref[...],) if z is not None else ())
        _, vjp = jax.vjp(_rms_fn, *args)
        grads = vjp(dy_ref[...].astype(F32))
        dx = grads[0]
        if res is not None:
            dx = dx + r_ref[...]
        dx_ref[...] = dx
        if z is not None:
            dz_ref[...] = grads[2]
        dg_ref[...] += grads[1]

    ins = [x, gain]
    specs = [pl.BlockSpec((tr, w), lambda j, r: (r, xoff + j)), pl.BlockSpec((1, w), lambda j, r: (0, 0))]
    if z is not None:
        ins.append(z)
        specs.append(pl.BlockSpec((tr, w), lambda j, r: (r, zoff + j)))
    ins.append(dy)
    specs.append(pl.BlockSpec((tr, w), lambda j, r: (r, dyoff + j)))
    blk = pl.BlockSpec((tr, w), lambda j, r: (r, j))
    if res is not None:
        ins.append(res)
        specs.append(blk)
    full = jax.ShapeDtypeStruct((t, ncol * w), F32)
    out_shape, out_specs = [full], [blk]
    if z is not None:
        out_shape.append(full)
        out_specs.append(blk)
    out_shape.append(jax.ShapeDtypeStruct((1, w), F32))
    out_specs.append(pl.BlockSpec((1, w), lambda j, r: (0, 0)))
    return pl.pallas_call(
        body, name=name, grid=(ncol, t // tr), in_specs=specs, out_specs=out_specs, out_shape=out_shape,
        compiler_params=_cparams(("arbitrary", "arbitrary")),
    )(*ins)


def _small_fn(x, pa, pb, nf, ng):
    lane = lax.broadcasted_iota(jnp.int32, x.shape, 1)
    zz = x + pb
    logf = -_softplus(-zz)
    g = -jnp.exp(pa) * _softplus(zz)
    beta = _sigmoid(x)
    return jnp.where(lane < nf, logf, jnp.where(lane < nf + ng, g, beta))


def _tri(n, upper):
    r = lax.broadcasted_iota(jnp.int32, (n, n), 0)
    c = lax.broadcasted_iota(jnp.int32, (n, n), 1)
    return jnp.where((c >= r) if upper else (c <= r), 1.0, 0.0).astype(F32)


def _small_fwd(p, off, pa, pb, nf, ng):
    t = p.shape[0]
    blk = HEAD_DIM
    nb = t // blk

    def body(x_ref, pa_ref, pb_ref, v_ref, c_ref):
        v_ref[...] = _small_fn(x_ref[...], pa_ref[...], pb_ref[...], nf, ng)
        tri = _tri(blk, False)

        carry = jnp.zeros((1, HEAD_DIM), F32)
        for i in range(nb):
            rows = slice(i * blk, (i + 1) * blk)
            c = _nn_hi(tri, v_ref[rows, :]) + carry
            c_ref[rows, :] = c
            carry = c[blk - 1:blk, :]

    row = pl.BlockSpec((1, HEAD_DIM), lambda i: (0, 0))
    out = pl.BlockSpec((t, HEAD_DIM), lambda i: (0, 0))
    return pl.pallas_call(
        body, name="small_fwd", grid=(1,),
        in_specs=[pl.BlockSpec((t, HEAD_DIM), lambda i: (0, off)), row, row], out_specs=[out, out],
        out_shape=[jax.ShapeDtypeStruct((t, HEAD_DIM), F32)] * 2,
        compiler_params=_cparams(("arbitrary",)),
    )(p, pa, pb)


def _small_bwd(p, off, pa, pb, dvals, dcsum, nf, ng):
    t = p.shape[0]
    blk = HEAD_DIM
    nb = t // blk

    def body(x_ref, pa_ref, pb_ref, dv_ref, dc_ref, dx_ref, dpa_ref, dpb_ref, tot_ref):
        tri = _tri(blk, True)

        carry = jnp.zeros((1, HEAD_DIM), F32)
        for i in reversed(range(nb)):
            rows = slice(i * blk, (i + 1) * blk)
            c = _nn_hi(tri, dc_ref[rows, :]) + carry
            tot_ref[rows, :] = c + dv_ref[rows, :]
            carry = c[0:1, :]
        f = functools.partial(_small_fn, nf=nf, ng=ng)
        _, vjp = jax.vjp(f, x_ref[...], pa_ref[...], pb_ref[...])
        dx, dpa, dpb = vjp(tot_ref[...])
        dx_ref[...] = dx
        dpa_ref[...] = dpa
        dpb_ref[...] = dpb

    row = pl.BlockSpec((1, HEAD_DIM), lambda i: (0, 0))
    full = pl.BlockSpec((t, HEAD_DIM), lambda i: (0, 0))
    return pl.pallas_call(
        body, name="small_bwd", grid=(1,),
        in_specs=[pl.BlockSpec((t, HEAD_DIM), lambda i: (0, off)), row, row, full, full],
        out_specs=[full, row, row],
        out_shape=[jax.ShapeDtypeStruct((t, HEAD_DIM), F32), jax.ShapeDtypeStruct((1, HEAD_DIM), F32),
                   jax.ShapeDtypeStruct((1, HEAD_DIM), F32)],
        scratch_shapes=[pltpu.VMEM((t, HEAD_DIM), F32)],
        compiler_params=_cparams(("arbitrary",)),
    )(p, pa, pb, dvals, dcsum)


def _fox_fwd(q, k, v, cc, cr, nf, tq, tk):
    t = q.shape[0]
    scale = HEAD_DIM ** -0.5
    ratio = tq // tk

    def body(q_ref, k_ref, v_ref, cc_ref, cr_ref, o_ref, lse_ref):
        i = pl.program_id(1)
        qv = q_ref[...]
        ccol = cc_ref[0]
        rows = i * tq + lax.broadcasted_iota(jnp.int32, (tq, tk), 0)
        cols0 = lax.broadcasted_iota(jnp.int32, (tq, tk), 1)

        def step(j, carry):
            m, l, acc = carry
            ks = pl.ds(pl.multiple_of(j * tk, tk), tk)
            s = lax.dot_general(qv, k_ref[ks, :], (((1,), (1,)), ((), ())), preferred_element_type=F32) * scale
            s = s + ccol - cr_ref[0, j]
            s = jnp.where(cols0 + j * tk <= rows, s, NEG)
            m_new = jnp.maximum(m, jnp.max(s, axis=1, keepdims=True))
            pr = jnp.exp(s - m_new)
            alpha = jnp.exp(m - m_new)
            l = alpha * l + jnp.sum(pr, axis=1, keepdims=True)
            acc = alpha * acc + jnp.dot(pr.astype(BF16), v_ref[ks, :], preferred_element_type=F32)
            return m_new, l, acc

        init = (jnp.full((tq, 1), NEG, F32), jnp.zeros((tq, 1), F32), jnp.zeros((tq, HEAD_DIM), F32))
        m, l, acc = lax.fori_loop(0, (i + 1) * ratio, step, init)
        o_ref[...] = acc / l
        lse_ref[0] = m + jnp.log(l)

    head_all = pl.BlockSpec((t, HEAD_DIM), lambda h, i: (0, h))
    return pl.pallas_call(
        body, name="fox_fwd", grid=(nf, t // tq),
        in_specs=[pl.BlockSpec((tq, HEAD_DIM), lambda h, i: (i, h)), head_all, head_all,
                  pl.BlockSpec((1, tq, 1), lambda h, i: (h, i, 0)),
                  pl.BlockSpec((1, t // tk, 1, tk), lambda h, i: (h, 0, 0, 0))],
        out_specs=[pl.BlockSpec((tq, HEAD_DIM), lambda h, i: (i, h)),
                   pl.BlockSpec((1, tq, 1), lambda h, i: (h, i, 0))],
        out_shape=[jax.ShapeDtypeStruct((t, nf * HEAD_DIM), F32), jax.ShapeDtypeStruct((nf, t, 1), F32)],
        compiler_params=_cparams(("parallel", "parallel")),
    )(q, k, v, cc, cr)


def _fox_bwd(q, k, v, cc, cr, o, lse, dmix, nf, tq, tk):
    t = q.shape[0]
    scale = HEAD_DIM ** -0.5
    ratio = tq // tk

    def body(q_ref, k_ref, v_ref, cc_ref, cr_ref, o_ref, lse_ref, do_ref,
             dq_ref, dk_ref, dv_ref, dcc_ref, dcr_ref):
        i = pl.program_id(1)

        @pl.when(i == 0)
        def _():
            dk_ref[...] = jnp.zeros_like(dk_ref)
            dv_ref[...] = jnp.zeros_like(dv_ref)
            dcr_ref[...] = jnp.zeros_like(dcr_ref)

        qv = q_ref[...]
        ccol = cc_ref[0]
        lse_v = lse_ref[0]
        do = do_ref[...]
        do_b = do.astype(BF16)
        delta = jnp.sum(do * o_ref[...], axis=1, keepdims=True)
        rows = i * tq + lax.broadcasted_iota(jnp.int32, (tq, tk), 0)
        cols0 = lax.broadcasted_iota(jnp.int32, (tq, tk), 1)

        def step(j, carry):
            dq, dcc = carry
            ks = pl.ds(pl.multiple_of(j * tk, tk), tk)
            kj, vj = k_ref[ks, :], v_ref[ks, :]
            s = lax.dot_general(qv, kj, (((1,), (1,)), ((), ())), preferred_element_type=F32) * scale
            s = s + ccol - cr_ref[0, j]
            pr = jnp.where(cols0 + j * tk <= rows, jnp.exp(s - lse_v), 0.0)
            dp = lax.dot_general(do_b, vj, (((1,), (1,)), ((), ())), preferred_element_type=F32)
            ds = pr * (dp - delta)
            ds_b = ds.astype(BF16)
            dq = dq + jnp.dot(ds_b, kj, preferred_element_type=F32) * scale
            dk_ref[ks, :] += lax.dot_general(ds_b, qv, (((0,), (0,)), ((), ())),
                                             preferred_element_type=F32) * scale
            dv_ref[ks, :] += lax.dot_general(pr.astype(BF16), do_b, (((0,), (0,)), ((), ())),
                                             preferred_element_type=F32)
            dcr_ref[0, j] -= jnp.sum(ds, axis=0, keepdims=True)
            return dq, dcc + jnp.sum(ds, axis=1, keepdims=True)

        init = (jnp.zeros((tq, HEAD_DIM), F32), jnp.zeros((tq, 1), F32))
        dq, dcc = lax.fori_loop(0, (i + 1) * ratio, step, init)
        dq_ref[...] = dq
        dcc_ref[0] = dcc

    head_all = pl.BlockSpec((t, HEAD_DIM), lambda h, i: (0, h))
    qblk = pl.BlockSpec((tq, HEAD_DIM), lambda h, i: (i, h))
    colv = pl.BlockSpec((1, tq, 1), lambda h, i: (h, i, 0))
    rowv = pl.BlockSpec((1, t // tk, 1, tk), lambda h, i: (h, 0, 0, 0))
    wide = jax.ShapeDtypeStruct((t, nf * HEAD_DIM), F32)
    return pl.pallas_call(
        body, name="fox_bwd", grid=(nf, t // tq),
        in_specs=[qblk, head_all, head_all, colv, rowv, qblk, colv, qblk],
        out_specs=[qblk, head_all, head_all, colv, rowv],
        out_shape=[wide, wide, wide, jax.ShapeDtypeStruct((nf, t, 1), F32),
                   jax.ShapeDtypeStruct((nf, t // tk, 1, tk), F32)],
        compiler_params=_cparams(("parallel", "arbitrary")),
    )(q, k, v, cc, cr, o, lse, dmix)


def _mem_fn(mq, mk, mv, gq, gk):
    qn = _rms_fn(mq, gq)
    kn = _rms_fn(mk, gk)
    s = _nt(qn, kn) * (HEAD_DIM ** -0.5)
    e = jnp.exp(s - lax.stop_gradient(jnp.max(s, axis=1, keepdims=True)))
    pr = e / jnp.sum(e, axis=1, keepdims=True)
    return _nn(pr, mv)


def _mem_specs(t, m, tq, qoff):
    qblk = pl.BlockSpec((tq, HEAD_DIM), lambda h, i: (i, qoff + h))
    kblk = pl.BlockSpec((m, HEAD_DIM), lambda h, i: (0, h))
    vblk = pl.BlockSpec((m, HEAD_DIM), lambda h, i: (0, N_MEM_HEADS + h))
    row = pl.BlockSpec((1, HEAD_DIM), lambda h, i: (0, 0))
    return qblk, kblk, vblk, row


def _mem_fwd(p, qoff, mkv, gq, gk, tq):
    t, m = p.shape[0], mkv.shape[0]
    qblk, kblk, vblk, row = _mem_specs(t, m, tq, qoff)

    def body(q_ref, k_ref, v_ref, gq_ref, gk_ref, o_ref):
        o_ref[...] = _mem_fn(q_ref[...], k_ref[...], v_ref[...], gq_ref[...], gk_ref[...])

    return pl.pallas_call(
        body, name="mem_fwd", grid=(N_MEM_HEADS, t // tq), in_specs=[qblk, kblk, vblk, row, row],
        out_specs=pl.BlockSpec((tq, HEAD_DIM), lambda h, i: (i, h)),
        out_shape=jax.ShapeDtypeStruct((t, N_MEM_HEADS * HEAD_DIM), F32),
        compiler_params=_cparams(("parallel", "parallel")),
    )(p, mkv, mkv, gq, gk)


def _mem_bwd(p, qoff, mkv, gq, gk, dmix, dooff, tq):
    t, m = p.shape[0], mkv.shape[0]
    qblk, kblk, vblk, row = _mem_specs(t, m, tq, qoff)

    def body(q_ref, k_ref, v_ref, gq_ref, gk_ref, do_ref, dq_ref, dkv_k_ref, dkv_v_ref, dgq_ref, dgk_ref):
        h, i = pl.program_id(0), pl.program_id(1)

        @pl.when((h == 0) & (i == 0))
        def _():
            dgq_ref[...] = jnp.zeros_like(dgq_ref)
            dgk_ref[...] = jnp.zeros_like(dgk_ref)

        @pl.when(i == 0)
        def _():
            dkv_k_ref[...] = jnp.zeros_like(dkv_k_ref)
            dkv_v_ref[...] = jnp.zeros_like(dkv_v_ref)

        _, vjp = jax.vjp(_mem_fn, q_ref[...], k_ref[...], v_ref[...], gq_ref[...], gk_ref[...])
        dq, dk, dv, dgq, dgk = vjp(do_ref[...])
        dq_ref[...] = dq
        dkv_k_ref[...] += dk
        dkv_v_ref[...] += dv
        dgq_ref[...] += dgq
        dgk_ref[...] += dgk

    oblk = pl.BlockSpec((tq, HEAD_DIM), lambda h, i: (i, h))
    kout = pl.BlockSpec((m, HEAD_DIM), lambda h, i: (0, h))
    half = jax.ShapeDtypeStruct((m, N_MEM_HEADS * HEAD_DIM), F32)
    rshape = jax.ShapeDtypeStruct((1, HEAD_DIM), F32)
    return pl.pallas_call(
        body, name="mem_bwd", grid=(N_MEM_HEADS, t // tq),
        in_specs=[qblk, kblk, vblk, row, row, pl.BlockSpec((tq, HEAD_DIM), lambda h, i: (i, dooff + h))],
        out_specs=[oblk, kout, kout, row, row],
        out_shape=[jax.ShapeDtypeStruct((t, N_MEM_HEADS * HEAD_DIM), F32), half, half, rshape, rshape],
        compiler_params=_cparams(("arbitrary", "arbitrary")),
    )(p, mkv, mkv, gq, gk, dmix)


def _shift_down(x, s):
    if s == 0:
        return x
    r = lax.broadcasted_iota(jnp.int32, x.shape, 0)
    return jnp.where(r >= s, pltpu.roll(x, s, 0), 0.0)


def _shift_up(x, s):
    if s == 0:
        return x
    n = x.shape[0]
    r = lax.broadcasted_iota(jnp.int32, x.shape, 0)
    return jnp.where(r < n - s, pltpu.roll(x, n - s, 0), 0.0)


def _conv_fn(x0, x1, x2, x3, w0, w1, w2, w3, kind):
    y = _silu(x0 * w0 + x1 * w1 + x2 * w2 + x3 * w3)
    if kind == 2:
        return y
    y = y * lax.rsqrt(jnp.sum(y * y, axis=-1, keepdims=True) + NORM_EPS)
    return y * (HEAD_DIM ** -0.5) if kind == 0 else y


def _conv_fwd(p, off, conv_w, ng):
    t = p.shape[0]

    def body(x_ref, w_ref, o_ref):
        kind = pl.program_id(0) // ng
        x = x_ref[...]
        xs = [_shift_down(x, CONV_WIDTH - 1 - j) for j in range(CONV_WIDTH)]
        ws = [w_ref[j:j + 1, :] for j in range(CONV_WIDTH)]
        for kd in range(3):
            @pl.when(kind == kd)
            def _(kd=kd):
                o_ref[...] = _conv_fn(*xs, *ws, kd)

    return pl.pallas_call(
        body, name="gdn_conv_fwd", grid=(3 * ng,),
        in_specs=[pl.BlockSpec((t, HEAD_DIM), lambda c: (0, off + c)),
                  pl.BlockSpec((CONV_WIDTH, HEAD_DIM), lambda c: (0, c))],
        out_specs=pl.BlockSpec((t, HEAD_DIM), lambda c: (0, c)),
        out_shape=jax.ShapeDtypeStruct((t, 3 * ng * HEAD_DIM), F32),
        compiler_params=_cparams(("parallel",)),
    )(p, conv_w)


def _conv_bwd(p, off, conv_w, dy, ng):
    t = p.shape[0]

    def body(x_ref, w_ref, dy_ref, dx_ref, dw_ref):
        kind = pl.program_id(0) // ng
        x = x_ref[...]
        xs = [_shift_down(x, CONV_WIDTH - 1 - j) for j in range(CONV_WIDTH)]
        ws = [w_ref[j:j + 1, :] for j in range(CONV_WIDTH)]
        for kd in range(3):
            @pl.when(kind == kd)
            def _(kd=kd):
                _, vjp = jax.vjp(functools.partial(_conv_fn, kind=kd), *xs, *ws)
                g = vjp(dy_ref[...])
                dx = _shift_up(g[0], CONV_WIDTH - 1)
                for j in range(1, CONV_WIDTH):
                    dx = dx + _shift_up(g[j], CONV_WIDTH - 1 - j)
                dx_ref[...] = dx
                for j in range(CONV_WIDTH):
                    dw_ref[j:j + 1, :] = g[CONV_WIDTH + j]

    blk = pl.BlockSpec((t, HEAD_DIM), lambda c: (0, c))
    wblk = pl.BlockSpec((CONV_WIDTH, HEAD_DIM), lambda c: (0, c))
    return pl.pallas_call(
        body, name="gdn_conv_bwd", grid=(3 * ng,),
        in_specs=[pl.BlockSpec((t, HEAD_DIM), lambda c: (0, off + c)), wblk, blk],
        out_specs=[blk, wblk],
        out_shape=[jax.ShapeDtypeStruct((t, 3 * ng * HEAD_DIM), F32),
                   jax.ShapeDtypeStruct((CONV_WIDTH, 3 * ng * HEAD_DIM), F32)],
        compiler_params=_cparams(("parallel",)),
    )(p, conv_w, dy)


def _chunk_fn(q, k, v, gcol, grow, bcol, state):
    c = q.shape[0]
    r = lax.broadcasted_iota(jnp.int32, (c, c), 0)
    e = lax.broadcasted_iota(jnp.int32, (c, c), 1)
    tril, strict = e <= r, e < r
    gc_col = jnp.sum(jnp.where(tril, grow, 0.0), axis=1, keepdims=True)
    gc_row = jnp.sum(jnp.where(r <= e, gcol, 0.0), axis=0, keepdims=True)
    g_last = jnp.sum(gcol, axis=0, keepdims=True)
    decay = jnp.exp(jnp.where(tril, gc_col - gc_row, NEG))
    kb, vb = k * bcol, v * bcol
    lower = jnp.where(strict, _nt(kb, k) * decay, 0.0)
    inv = jnp.where(r == e, 1.0, 0.0) - lower
    pw = lower
    for _ in range(int(math.log2(c)) - 1):
        pw = _nn_hi(pw, pw)
        inv = inv + _nn_hi(inv, pw)
    u = _nn_hi(inv, vb)
    w = _nn_hi(inv, kb * jnp.exp(gc_col))
    attn = jnp.where(tril, _nt(q, k) * decay, 0.0)
    v_new = u - _nn(w, state)
    o = _nn(q * jnp.exp(gc_col), state) + _nn(attn, v_new)
    new_state = state * jnp.exp(g_last) + _tn(k * jnp.exp(g_last - gc_col), v_new)
    return o, new_state


def _gdn_fwd(qkv, gcol, grow, bcol, ng):
    t = qkv.shape[0]
    nch = t // CHUNK

    def body(q_ref, k_ref, v_ref, gc_ref, gr_ref, bc_ref, o_ref, st_ref, state):
        @pl.when(pl.program_id(0) == 0)
        def _():
            state[...] = jnp.zeros_like(state)

        for h in range(ng):
            ln = slice(h * HEAD_DIM, (h + 1) * HEAD_DIM)
            st_ref[h, 0] = state[h]
            o, new = _chunk_fn(q_ref[:, ln], k_ref[:, ln], v_ref[:, ln], gc_ref[h], gr_ref[h, 0], bc_ref[h],
                               state[h])
            o_ref[:, ln] = o
            state[h] = new

    w = ng * HEAD_DIM
    col = pl.BlockSpec((ng, CHUNK, 1), lambda i: (0, i, 0))
    return pl.pallas_call(
        body, name="gdn_fwd", grid=(nch,),
        in_specs=[pl.BlockSpec((CHUNK, w), lambda i: (i, 0)), pl.BlockSpec((CHUNK, w), lambda i: (i, 1)),
                  pl.BlockSpec((CHUNK, w), lambda i: (i, 2)),
                  col, pl.BlockSpec((ng, 1, 1, CHUNK), lambda i: (0, i, 0, 0)), col],
        out_specs=[pl.BlockSpec((CHUNK, w), lambda i: (i, 0)),
                   pl.BlockSpec((ng, 1, HEAD_DIM, HEAD_DIM), lambda i: (0, i, 0, 0))],
        out_shape=[jax.ShapeDtypeStruct((t, w), F32),
                   jax.ShapeDtypeStruct((ng, nch, HEAD_DIM, HEAD_DIM), F32)],
        scratch_shapes=[pltpu.VMEM((ng, HEAD_DIM, HEAD_DIM), F32)],
        compiler_params=_cparams(("arbitrary",)),
    )(qkv, qkv, qkv, gcol, grow, bcol)


def _gdn_bwd(qkv, gcol, grow, bcol, states, do, ng):
    t = qkv.shape[0]
    nch = t // CHUNK

    def body(q_ref, k_ref, v_ref, gc_ref, gr_ref, bc_ref, st_ref, do_ref,
             dqkv_q, dqkv_k, dqkv_v, dgc_ref, dgr_ref, dbc_ref, dstate):
        @pl.when(pl.program_id(0) == 0)
        def _():
            dstate[...] = jnp.zeros_like(dstate)

        for h in range(ng):
            ln = slice(h * HEAD_DIM, (h + 1) * HEAD_DIM)
            _, vjp = jax.vjp(_chunk_fn, q_ref[:, ln], k_ref[:, ln], v_ref[:, ln], gc_ref[h], gr_ref[h, 0],
                             bc_ref[h], st_ref[h, 0])
            dq, dk, dv, dgc, dgr, dbc, dst = vjp((do_ref[:, ln], dstate[h]))
            dqkv_q[:, ln] = dq
            dqkv_k[:, ln] = dk
            dqkv_v[:, ln] = dv
            dgc_ref[h] = dgc
            dgr_ref[h, 0] = dgr
            dbc_ref[h] = dbc
            dstate[h] = dst

    w = ng * HEAD_DIM
    rev = lambda i: nch - 1 - i
    blk = lambda o: pl.BlockSpec((CHUNK, w), lambda i: (rev(i), o))
    col = pl.BlockSpec((ng, CHUNK, 1), lambda i: (0, rev(i), 0))
    rowv = pl.BlockSpec((ng, 1, 1, CHUNK), lambda i: (0, rev(i), 0, 0))
    wide = jax.ShapeDtypeStruct((t, w), F32)
    cshape = jax.ShapeDtypeStruct((ng, t, 1), F32)
    return pl.pallas_call(
        body, name="gdn_bwd", grid=(nch,),
        in_specs=[blk(0), blk(1), blk(2), col, rowv, col,
                  pl.BlockSpec((ng, 1, HEAD_DIM, HEAD_DIM), lambda i: (0, rev(i), 0, 0)), blk(0)],
        out_specs=[blk(0), blk(0), blk(0), col, rowv, col],
        out_shape=[wide, wide, wide, cshape, jax.ShapeDtypeStruct((ng, nch, 1, CHUNK), F32), cshape],
        scratch_shapes=[pltpu.VMEM((ng, HEAD_DIM, HEAD_DIM), F32)],
        compiler_params=_cparams(("arbitrary",)),
    )(qkv, qkv, qkv, gcol, grow, bcol, states, do)


def _swiglu_fn(gate, up):
    return _silu(gate) * up


def _swiglu_fwd(gu, ff):
    t = gu.shape[0]
    tr, tc = _tile(t, 512, 8), _tile(ff, 1024)
    nc = ff // tc

    def body(g_ref, u_ref, o_ref):
        o_ref[...] = _swiglu_fn(g_ref[...], u_ref[...]).astype(BF16)

    return pl.pallas_call(
        body, name="swiglu_fwd", grid=(t // tr, nc),
        in_specs=[pl.BlockSpec((tr, tc), lambda r, c: (r, c)), pl.BlockSpec((tr, tc), lambda r, c: (r, nc + c))],
        out_specs=pl.BlockSpec((tr, tc), lambda r, c: (r, c)),
        out_shape=jax.ShapeDtypeStruct((t, ff), BF16),
        compiler_params=_cparams(("parallel", "parallel")),
    )(gu, gu)


def _swiglu_bwd(gu, dact, ff):
    t = gu.shape[0]
    tr, tc = _tile(t, 512, 8), _tile(ff, 1024)
    nc = ff // tc

    def body(g_ref, u_ref, d_ref, o_ref):
        _, vjp = jax.vjp(_swiglu_fn, g_ref[...], u_ref[...])
        dg, du = vjp(d_ref[...])

        @pl.when(pl.program_id(1) < nc)
        def _():
            o_ref[...] = dg.astype(BF16)

        @pl.when(pl.program_id(1) >= nc)
        def _():
            o_ref[...] = du.astype(BF16)

    lo = pl.BlockSpec((tr, tc), lambda r, c: (r, c % nc))
    hi = pl.BlockSpec((tr, tc), lambda r, c: (r, nc + c % nc))
    return pl.pallas_call(
        body, name="swiglu_bwd", grid=(t // tr, 2 * nc), in_specs=[lo, hi, lo],
        out_specs=pl.BlockSpec((tr, tc), lambda r, c: (r, c)),
        out_shape=jax.ShapeDtypeStruct((t, 2 * ff), BF16),
        compiler_params=_cparams(("parallel", "parallel")),
    )(gu, gu, dact)


def _loss_head(h2, target):
    t, d = h2.shape
    tr = _tile(t, 256, 8)

    def body(h_ref, t_ref, l_ref, d_ref):
        @pl.when(pl.program_id(0) == 0)
        def _():
            l_ref[...] = jnp.zeros_like(l_ref)

        err = h_ref[...] - t_ref[...]
        d_ref[...] = err * (1.0 / d)
        part = 0.5 * jnp.sum(jnp.mean(err * err, axis=-1, keepdims=True), axis=0, keepdims=True)
        lane = lax.broadcasted_iota(jnp.int32, (8, HEAD_DIM), 1)
        row = lax.broadcasted_iota(jnp.int32, (8, HEAD_DIM), 0)
        l_ref[...] += jnp.where((lane == 0) & (row == 0), part, 0.0)

    blk = pl.BlockSpec((tr, d), lambda r: (r, 0))
    return pl.pallas_call(
        body, name="loss_head", grid=(t // tr,), in_specs=[blk, blk],
        out_specs=[pl.BlockSpec((8, HEAD_DIM), lambda r: (0, 0)), blk],
        out_shape=[jax.ShapeDtypeStruct((8, HEAD_DIM), F32), jax.ShapeDtypeStruct((t, d), F32)],
        compiler_params=_cparams(("arbitrary",)),
    )(h2, target)


def _adamw(w, g, m, v, *, g2=None, name):
    r, c = w.shape
    tr = _tile(r, max(8, (1 << 19) // c // 8 * 8), 8)

    def body(*refs):
        if g2 is None:
            w_ref, g_ref, m_ref, v_ref, go_ref, d_ref, mo_ref, vo_ref = refs
            gr = g_ref[...]
        else:
            w_ref, g_ref, g2_ref, m_ref, v_ref, go_ref, d_ref, mo_ref, vo_ref = refs
            gr = g_ref[...] + g2_ref[...]
        mn = ADAM_B1 * m_ref[...] + (1.0 - ADAM_B1) * gr
        vn = ADAM_B2 * v_ref[...] + (1.0 - ADAM_B2) * (gr * gr)
        m_hat = mn / (1.0 - ADAM_B1 ** ADAM_STEP)
        v_hat = vn / (1.0 - ADAM_B2 ** ADAM_STEP)
        go_ref[...] = gr
        d_ref[...] = -ADAM_LR * (m_hat / (jnp.sqrt(v_hat) + ADAM_EPS) + ADAM_WD * w_ref[...])
        mo_ref[...] = mn
        vo_ref[...] = vn

    blk = pl.BlockSpec((tr, c), lambda i: (i, 0))
    n_in = 4 if g2 is None else 5
    ins = [w, g] + ([g2] if g2 is not None else []) + [m, v]
    return pl.pallas_call(
        body, name=name, grid=(r // tr,), in_specs=[blk] * n_in, out_specs=[blk] * 4,
        out_shape=[jax.ShapeDtypeStruct((r, c), F32)] * 4,
        compiler_params=_cparams(("parallel",)),
    )(*ins)


class _Layout:
    def __init__(self, d):
        nh = d // HEAD_DIM
        self.nm = N_MEM_HEADS
        self.nf = (nh - self.nm) // 2
        self.ng = nh - self.nm - self.nf
        nf, ng, nm = self.nf, self.ng, self.nm
        self.o_fq, self.o_fk, self.o_fv = 0, nf, 2 * nf
        self.o_gq = 3 * nf
        self.o_gz = 3 * nf + 3 * ng
        self.o_mq = 3 * nf + 4 * ng
        self.o_sm = self.o_mq + nm
        self.blocks = -(-(self.o_sm + 1) // 8) * 8
        self.cols = self.blocks * HEAD_DIM
        hd = HEAD_DIM
        sizes = [nf * hd, nf * hd, nf * hd, nf, 3 * ng * hd, ng * hd, ng, ng, nm * hd]
        starts = [sum(sizes[:i]) for i in range(len(sizes))]
        self.ref = list(zip(starts, sizes))
        self.in_cols = sum(sizes)

    def regroup(self, w):
        part = lambda i: w[:, self.ref[i][0]:self.ref[i][0] + self.ref[i][1]]
        pieces = [part(0), part(1), part(2), part(4), part(5), part(8), part(3), part(6), part(7)]
        pad = self.cols - self.in_cols
        return jnp.concatenate(pieces + [jnp.zeros((w.shape[0], pad), w.dtype)], axis=1)

    def ungroup(self, g):
        hd, nf, ng, nm = HEAD_DIM, self.nf, self.ng, self.nm
        sm = self.o_sm * hd
        return jnp.concatenate([
            g[:, :3 * nf * hd], g[:, sm:sm + nf], g[:, self.o_gq * hd:self.o_gz * hd],
            g[:, self.o_gz * hd:self.o_mq * hd], g[:, sm + nf:sm + nf + ng], g[:, sm + nf + ng:sm + nf + 2 * ng],
            g[:, self.o_mq * hd:self.o_sm * hd]], axis=1)


def _lane_row(pieces):
    row = jnp.zeros((1, HEAD_DIM), F32)
    for off, a in pieces:
        row = lax.dynamic_update_slice(row, a.astype(F32), (0, off))
    return row


def _local_step(x, mem, target, win, wmkv, wout, wgu, wd, sp):
    t, d = x.shape
    lay = _Layout(d)
    nf, ng, nm, hd = lay.nf, lay.ng, lay.nm, HEAD_DIM
    ff = wd.shape[0]
    nch = t // CHUNK
    tq = _tile(t, 256)
    tk = tq

    u = _norm_fwd(x, 0, sp["norm_mix"], 1, d, BF16, name="norm_mix_fwd")
    p = _mm(u, win, name="mm_in")
    pa = _lane_row([(nf, sp["gdn_a_log"])])
    pb = _lane_row([(0, sp["fox_f_bias"]), (nf, sp["gdn_dt_bias"])])
    vals, csum = _small_fwd(p, lay.o_sm, pa, pb, nf, ng)

    c_t = csum[:, :nf].T
    cc, cr = c_t.reshape(nf, t, 1), c_t.reshape(nf, t // tk, 1, tk)
    fq = _norm_fwd(p, lay.o_fq, sp["fox_q_norm"], nf, hd, BF16, name="fox_qnorm_fwd")
    fk = _norm_fwd(p, lay.o_fk, sp["fox_k_norm"], nf, hd, BF16, name="fox_knorm_fwd")
    fv = p[:, lay.o_fv * hd:(lay.o_fv + nf) * hd].astype(BF16)
    o_fox, lse = _fox_fwd(fq, fk, fv, cc, cr, nf, tq, tk)

    qkv = _conv_fwd(p, lay.o_gq, sp["gdn_conv"], ng)
    g_t, b_t = vals[:, nf:nf + ng].T, vals[:, nf + ng:nf + 2 * ng].T
    gcol, grow, bcol = g_t.reshape(ng, t, 1), g_t.reshape(ng, nch, 1, CHUNK), b_t.reshape(ng, t, 1)
    o_g, states = _gdn_fwd(qkv, gcol, grow, bcol, ng)
    o_gdn = _norm_fwd(o_g, 0, sp["gdn_out_norm"], ng, hd, BF16, z=p, zoff=lay.o_gz, name="gdn_out_fwd")

    mem_n = _norm_fwd(mem, 0, sp["mem_norm"], 1, d, BF16, name="mem_norm_fwd")
    mkv = _mm(mem_n, wmkv, name="mm_memkv")
    o_mem = _mem_fwd(p, lay.o_mq, mkv, sp["mem_q_norm"], sp["mem_k_norm"], tq)

    mix = jnp.concatenate([o_fox.astype(BF16), o_gdn, o_mem.astype(BF16)], axis=1)
    h1 = _mm(mix, wout, res=x, name="mm_out")
    n2 = _norm_fwd(h1, 0, sp["norm_ffn"], 1, d, BF16, name="norm_ffn_fwd")
    gu = _mm(n2, wgu, name="mm_gate_up")
    act = _swiglu_fwd(gu, ff)
    h2 = _mm(act, wd, res=h1, name="mm_down")
    loss_blk, dh2 = _loss_head(h2, target)

    g = {}
    g["w_down"] = _mm(act, dh2, ta=True, out_dtype=BF16, name="mm_dw_down")
    dact = _mm(dh2, wd, tb=True, name="mm_dact")
    dgu = _swiglu_bwd(gu, dact, ff)
    g["w_gate_up"] = _mm(n2, dgu, ta=True, out_dtype=BF16, name="mm_dw_gate_up")
    dn2 = _mm(dgu, wgu, tb=True, name="mm_dn2")
    dh1, g["norm_ffn"] = _norm_bwd(h1, 0, sp["norm_ffn"], dn2, 0, 1, d, res=dh2, name="norm_ffn_bwd")
    g["w_out"] = _mm(mix, dh1, ta=True, out_dtype=BF16, name="mm_dw_out")
    dmix = _mm(dh1, wout, tb=True, name="mm_dmix")

    dmq, dmk, dmv, g["mem_q_norm"], g["mem_k_norm"] = _mem_bwd(
        p, lay.o_mq, mkv, sp["mem_q_norm"], sp["mem_k_norm"], dmix, nf + ng, tq)
    dmkv = jnp.concatenate([dmk, dmv], axis=1)
    g["w_mem_kv"] = _mm(mem_n, dmkv, ta=True, out_dtype=BF16, name="mm_dw_memkv")
    dmem_n = _mm(dmkv, wmkv, tb=True, name="mm_dmem")
    _, g["mem_norm"] = _norm_bwd(mem, 0, sp["mem_norm"], dmem_n, 0, 1, d, name="mem_norm_bwd")

    do_g, dgz, g["gdn_out_norm"] = _norm_bwd(o_g, 0, sp["gdn_out_norm"], dmix, nf, ng, hd, z=p, zoff=lay.o_gz,
                                             name="gdn_out_bwd")
    dq, dk, dv, dgc, dgr, dbc = _gdn_bwd(qkv, gcol, grow, bcol, states, do_g, ng)
    dgqkv, g["gdn_conv"] = _conv_bwd(p, lay.o_gq, sp["gdn_conv"], jnp.concatenate([dq, dk, dv], axis=1), ng)
    dg_t = dgc.reshape(ng, t) + dgr.reshape(ng, t)
    db_t = dbc.reshape(ng, t)

    dfq_n, dfk_n, dfv, dcc, dcr = _fox_bwd(fq, fk, fv, cc, cr, o_fox, lse, dmix, nf, tq, tk)
    dfq, g["fox_q_norm"] = _norm_bwd(p, lay.o_fq, sp["fox_q_norm"], dfq_n, 0, nf, hd, name="fox_qnorm_bwd")
    dfk, g["fox_k_norm"] = _norm_bwd(p, lay.o_fk, sp["fox_k_norm"], dfk_n, 0, nf, hd, name="fox_knorm_bwd")
    dc_t = dcc.reshape(nf, t) + dcr.reshape(nf, t)

    lanes_left = hd - nf - 2 * ng
    dvals = jnp.concatenate([jnp.zeros((t, nf), F32), dg_t.T, db_t.T, jnp.zeros((t, lanes_left), F32)], axis=1)
    dcsum = jnp.concatenate([dc_t.T, jnp.zeros((t, hd - nf), F32)], axis=1)
    dsm, dpa, dpb = _small_bwd(p, lay.o_sm, pa, pb, dvals, dcsum, nf, ng)
    g["fox_f_bias"] = dpb[:, :nf]
    g["gdn_dt_bias"] = dpb[:, nf:nf + ng]
    g["gdn_a_log"] = dpa[:, nf:nf + ng]

    pad = jnp.zeros((t, lay.cols - (lay.o_sm + 1) * hd), F32)
    dp = jnp.concatenate([dfq, dfk, dfv, dgqkv, dgz, dmq, dsm, pad], axis=1)
    g["w_in"] = _mm(u, dp, ta=True, out_dtype=BF16, name="mm_dw_in")
    du = _mm(dp, win, tb=True, name="mm_du")
    dx, g["norm_mix"] = _norm_bwd(x, 0, sp["norm_mix"], du, 0, 1, d, res=dh1, name="norm_mix_bwd")
    return loss_blk, dx, g


ANY = pl.BlockSpec(memory_space=pl.ANY)


def _me():
    x, y, c = lax.axis_index("x"), lax.axis_index("y"), lax.axis_index("c")
    chips = [(1 - x, y), (x, 1 - y), (1 - x, 1 - y)]
    return x, y, c, chips


def _slab(ref, axis, rows, cols, k, h):
    half = rows // 2
    if axis == 0:
        return ref.at[pl.ds(k * rows + h * half, half), :]
    return ref.at[pl.ds(h * half, half), pl.ds(k * cols, cols)]


def _remote(src, dst, send_sem, recv_sem, dev):
    return pltpu.make_async_remote_copy(src_ref=src, dst_ref=dst, send_sem=send_sem, recv_sem=recv_sem,
                                        device_id=dev, device_id_type=MESH)


def _gather_weights(shards, axes):
    n = len(shards)
    shapes = [s.shape for s in shards]
    chip = 2 * lax.axis_index("x") + lax.axis_index("y")
    bufs = []
    for s, a in zip(shards, axes):
        r, cl = s.shape
        full = lax.empty((4 * r, cl) if a == 0 else (r, 4 * cl), s.dtype)
        bufs.append(lax.dynamic_update_slice(full, s, (chip * r, 0) if a == 0 else (0, chip * cl)))

    def body(*refs):
        dst = refs[n:2 * n]
        send_sems, recv_sems = refs[2 * n:]
        x, y, c, chips = _me()
        k = 2 * x + y
        sibling = (x, y, 1 - c)
        sends = []
        for w in range(n):
            r, cl = shapes[w]
            for j, (px, py) in enumerate(chips):
                place = _slab(dst[w], axes[w], r, cl, k, c)
                cp = _remote(place, place, send_sems.at[6 * w + j], recv_sems.at[6 * w + j], (px, py, c))
                cp.start()
                sends.append(cp)
        for w in range(n):
            r, cl = shapes[w]
            for j, (px, py) in enumerate(chips):
                got = _slab(dst[w], axes[w], r, cl, 2 * px + py, c)
                _remote(got, got, send_sems.at[6 * w + j], recv_sems.at[6 * w + j], (px, py, c)).wait_recv()
                cp = _remote(got, got, send_sems.at[6 * w + 3 + j], recv_sems.at[6 * w + 3 + j], sibling)
                cp.start()
                sends.append(cp)
        for w in range(n):
            r, cl = shapes[w]
            for j, (px, py) in enumerate(chips):
                got = _slab(dst[w], axes[w], r, cl, 2 * px + py, 1 - c)
                _remote(got, got, send_sems.at[6 * w + 3 + j], recv_sems.at[6 * w + 3 + j], sibling).wait_recv()
        for cp in sends:
            cp.wait_send()

    return pl.pallas_call(
        body, name="gather_weights", in_specs=[ANY] * n, out_specs=[ANY] * n,
        out_shape=[jax.ShapeDtypeStruct(b.shape, b.dtype) for b in bufs],
        input_output_aliases={w: w for w in range(n)},
        scratch_shapes=[pltpu.SemaphoreType.DMA((6 * n,)), pltpu.SemaphoreType.DMA((6 * n,))],
    )(*bufs)


def _pair_exchange(fulls, axes, shapes):
    n = len(fulls)

    def body(*refs):
        src, dst = refs[:n], refs[n:2 * n]
        send_sems, recv_sems = refs[2 * n:]
        x, y, c, _ = _me()
        sibling = (x, y, 1 - c)
        cps = []
        for w in range(n):
            r, cl = shapes[w]
            for j in range(4):
                cp = _remote(_slab(src[w], axes[w], r, cl, j, 1 - c), dst[w].at[j],
                             send_sems.at[4 * w + j], recv_sems.at[4 * w + j], sibling)
                cp.start()
                cps.append(cp)
        for cp in cps:
            cp.wait()

    out_shape = [jax.ShapeDtypeStruct((4, r // 2, cl), f.dtype) for (r, cl), f in zip(shapes, fulls)]
    return pl.pallas_call(
        body, name="reduce_pair_exchange", in_specs=[ANY] * n, out_specs=[ANY] * n, out_shape=out_shape,
        scratch_shapes=[pltpu.SemaphoreType.DMA((4 * n,)), pltpu.SemaphoreType.DMA((4 * n,))],
    )(*fulls)


def _chip_exchange(parts):
    n = len(parts)

    def body(*refs):
        src, dst = refs[:n], refs[n:2 * n]
        send_sems, recv_sems = refs[2 * n:]
        x, y, c, chips = _me()
        k = 2 * x + y
        cps = []
        for w in range(n):
            for j, (px, py) in enumerate(chips):
                cp = _remote(src[w].at[2 * px + py], dst[w].at[k], send_sems.at[3 * w + j],
                             recv_sems.at[3 * w + j], (px, py, c))
                cp.start()
                cps.append(cp)
        for w in range(n):
            for j, (px, py) in enumerate(chips):
                got = dst[w].at[2 * px + py]
                _remote(got, got, send_sems.at[3 * w + j], recv_sems.at[3 * w + j], (px, py, c)).wait_recv()
        for cp in cps:
            cp.wait_send()

    slots = pl.pallas_call(
        body, name="reduce_chip_exchange", in_specs=[ANY] * n, out_specs=[ANY] * n,
        out_shape=[jax.ShapeDtypeStruct(p.shape, p.dtype) for p in parts],
        scratch_shapes=[pltpu.SemaphoreType.DMA((3 * n,)), pltpu.SemaphoreType.DMA((3 * n,))],
    )(*parts)
    chip = 2 * lax.axis_index("x") + lax.axis_index("y")
    return [lax.dynamic_update_slice(s, lax.dynamic_index_in_dim(p, chip, 0, keepdims=True), (chip, 0, 0))
            for s, p in zip(slots, parts)]


def _half_swap(halves):
    n = len(halves)
    core = lax.axis_index("c")
    bufs = [lax.dynamic_update_slice(lax.empty((2,) + h.shape, h.dtype), h[None], (core, 0, 0)) for h in halves]

    def body(*refs):
        dst = refs[n:2 * n]
        send_sems, recv_sems = refs[2 * n:]
        x, y, c, _ = _me()
        sibling = (x, y, 1 - c)
        cps = []
        for w in range(n):
            cp = _remote(dst[w].at[c], dst[w].at[c], send_sems.at[w], recv_sems.at[w], sibling)
            cp.start()
            cps.append(cp)
        for w in range(n):
            other = dst[w].at[1 - c]
            _remote(other, other, send_sems.at[w], recv_sems.at[w], sibling).wait_recv()
        for cp in cps:
            cp.wait_send()

    outs = pl.pallas_call(
        body, name="reduce_half_swap", in_specs=[ANY] * n, out_specs=[ANY] * n,
        out_shape=[jax.ShapeDtypeStruct(b.shape, b.dtype) for b in bufs],
        input_output_aliases={w: w for w in range(n)},
        scratch_shapes=[pltpu.SemaphoreType.DMA((n,)), pltpu.SemaphoreType.DMA((n,))],
    )(*bufs)
    return [o.reshape(2 * o.shape[1], o.shape[2]) for o in outs]


def _add_parts(a, b, name):
    _, r, c = a.shape
    tr, tc = _tile(r, 256, 8), _tile(c, 2048)

    def body(a_ref, b_ref, o_ref):
        o_ref[...] = (a_ref[...].astype(F32) + b_ref[...].astype(F32)).astype(BF16)

    blk = pl.BlockSpec((1, tr, tc), lambda j, i, l: (j, i, l))
    return pl.pallas_call(
        body, name=name, grid=(4, r // tr, c // tc), in_specs=[blk, blk], out_specs=blk,
        out_shape=jax.ShapeDtypeStruct(a.shape, BF16),
        compiler_params=_cparams(("parallel", "parallel", "parallel")),
    )(a, b)


def _sum_slots(a, name):
    _, r, c = a.shape
    tr, tc = _tile(r, 256, 8), _tile(c, 2048)

    def body(a_ref, o_ref):
        v = a_ref[...].astype(F32)
        o_ref[...] = ((v[0] + v[1]) + v[2]) + v[3]

    return pl.pallas_call(
        body, name=name, grid=(r // tr, c // tc),
        in_specs=[pl.BlockSpec((4, tr, tc), lambda i, l: (0, i, l))],
        out_specs=pl.BlockSpec((tr, tc), lambda i, l: (i, l)),
        out_shape=jax.ShapeDtypeStruct((r, c), F32),
        compiler_params=_cparams(("parallel", "parallel")),
    )(a)


def _half_of(full, axis, rows, cols, c):
    half = rows // 2
    if axis == 0:
        v = full.reshape(4, 2, half, cols)
        return lax.dynamic_index_in_dim(v, c, 1, keepdims=False)
    v = full.reshape(2, half, 4, cols)
    return jnp.transpose(lax.dynamic_index_in_dim(v, c, 0, keepdims=False), (1, 0, 2))


def _reduce_grads(fulls, axes, shapes):
    c = lax.axis_index("c")
    from_sibling = _pair_exchange(fulls, axes, shapes)
    parts = [_add_parts(_half_of(f, a, r, cl, c), s, name=f"reduce_add_{w}")
             for w, (f, a, (r, cl), s) in enumerate(zip(fulls, axes, shapes, from_sibling))]
    slots = _chip_exchange(parts)
    halves = [_sum_slots(s, name=f"reduce_sum_{w}") for w, s in enumerate(slots)]
    return _half_swap(halves)


def _allreduce_small(pack):
    rows = pack.shape[0]

    def body(p_ref, o_ref, slots, send_sems, recv_sems):
        x, y, c, _ = _me()
        me = 4 * x + 2 * y + c
        slots[me] = p_ref[...]
        cps = []
        for r in range(1, 8):
            peer = (x ^ (r >> 2), y ^ ((r >> 1) & 1), c ^ (r & 1))
            cp = _remote(p_ref, slots.at[me], send_sems.at[r - 1], recv_sems.at[r - 1], peer)
            cp.start()
            cps.append(cp)
        for r in range(1, 8):
            frm = me ^ r
            _remote(slots.at[frm], slots.at[frm], send_sems.at[r - 1], recv_sems.at[r - 1], (x, y, c)).wait_recv()
        for cp in cps:
            cp.wait_send()
        acc = slots[0]
        for s in range(1, 8):
            acc = acc + slots[s]
        o_ref[...] = acc

    vm = pl.BlockSpec(memory_space=pltpu.VMEM)
    return pl.pallas_call(
        body, name="allreduce_small", in_specs=[vm], out_specs=vm,
        out_shape=jax.ShapeDtypeStruct(pack.shape, F32),
        scratch_shapes=[pltpu.VMEM((8, rows, HEAD_DIM), F32), pltpu.SemaphoreType.DMA((7,)),
                        pltpu.SemaphoreType.DMA((7,))],
    )(pack)


_ROWS = ["norm_mix", "norm_ffn", "mem_norm", "fox_q_norm", "fox_k_norm", "gdn_out_norm", "mem_q_norm",
         "mem_k_norm", "fox_f_bias", "gdn_a_log", "gdn_dt_bias"]


def _pack_rows(vals):
    out = []
    for name in _ROWS:
        v = vals[name].reshape(-1)
        n = -(-v.shape[0] // HEAD_DIM) * HEAD_DIM
        out.append(jnp.pad(v, (0, n - v.shape[0])).reshape(-1, HEAD_DIM))
    return jnp.concatenate(out, axis=0)


def _unpack_rows(pack, like):
    out, r = {}, 0
    for name in _ROWS:
        n = like[name].shape[-1]
        nr = -(-n // HEAD_DIM)
        out[name] = pack[r:r + nr].reshape(1, -1)[:, :n]
        r += nr
    return out, r


def kernel(x, mem, norm_mix, w_in, fox_f_bias, fox_q_norm, fox_k_norm, gdn_conv, gdn_a_log, gdn_dt_bias, gdn_out_norm, mem_norm, w_mem_kv, mem_q_norm, mem_k_norm, w_out, norm_ffn, w_gate_up, w_down, loss_target, m_norm_mix, m_w_in, m_fox_f_bias, m_fox_q_norm, m_fox_k_norm, m_gdn_conv, m_gdn_a_log, m_gdn_dt_bias, m_gdn_out_norm, m_mem_norm, m_w_mem_kv, m_mem_q_norm, m_mem_k_norm, m_w_out, m_norm_ffn, m_w_gate_up, m_w_down, v_norm_mix, v_w_in, v_fox_f_bias, v_fox_q_norm, v_fox_k_norm, v_gdn_conv, v_gdn_a_log, v_gdn_dt_bias, v_gdn_out_norm, v_mem_norm, v_w_mem_kv, v_mem_q_norm, v_mem_k_norm, v_w_out, v_norm_ffn, v_w_gate_up, v_w_down):
    a = dict(locals())
    d = x.shape[-1]
    lay = _Layout(d)
    chip = 2 * lax.axis_index("x") + lax.axis_index("y")
    small = {n: a[n] for n in _ROWS}
    big = ["w_in", "w_mem_kv", "w_out", "w_gate_up", "w_down"]
    axes = [0, 0, 0, 1, 0]

    conv_cols = gdn_conv.shape[-1]
    conv_n = CONV_WIDTH * conv_cols
    conv_rows = -(-conv_n // HEAD_DIM)
    conv_blk = jnp.pad(gdn_conv.reshape(-1), (0, 32 * HEAD_DIM - conv_n)).reshape(32, HEAD_DIM)
    shards = [lay.regroup(w_in[0]).astype(BF16), w_mem_kv[0].astype(BF16), w_out[0].astype(BF16),
              w_gate_up[0].astype(BF16), w_down[0].astype(BF16)]
    shapes = [s.shape for s in shards]
    win, wmkv, wout, wgu, wd, conv_all = _gather_weights(shards + [conv_blk], axes + [0])
    conv_full = conv_all.reshape(4, 32 * HEAD_DIM)[:, :conv_n].reshape(4, CONV_WIDTH, conv_cols)
    conv_full = jnp.transpose(conv_full, (1, 0, 2)).reshape(CONV_WIDTH, 4 * conv_cols)

    sp = dict(small)
    sp["gdn_conv"] = conv_full
    loss_blk, dx, g = _local_step(x[0], mem[0], loss_target[0], win, wmkv, wout, wgu, wd, sp)

    gbig = list(_reduce_grads([g[n] for n in big], axes, shapes))
    gbig[0] = lay.ungroup(gbig[0])
    gsmall = {n: g[n] for n in _ROWS}
    pack = jnp.concatenate([_pack_rows(gsmall), g["gdn_conv"].reshape(-1, HEAD_DIM), loss_blk], axis=0)
    pack = jnp.pad(pack, ((0, -pack.shape[0] % 8), (0, 0)))
    tot = _allreduce_small(pack)
    gs, r0 = _unpack_rows(tot, small)
    conv_g = tot[r0:r0 + CONV_WIDTH * 4 * conv_cols // HEAD_DIM].reshape(CONV_WIDTH, 4 * conv_cols)
    gs_conv = lax.dynamic_slice_in_dim(conv_g, chip * conv_cols, conv_cols, axis=1)
    loss = tot[r0 + CONV_WIDTH * 4 * conv_cols // HEAD_DIM, 0]

    out = {"loss": loss, "grad_x": dx[None]}
    for n, gsh in zip(big, gbig):
        res = _adamw(a[n][0], gsh, a["m_" + n][0], a["v_" + n][0], name="adamw_" + n)
        for pre, r in zip(["grad_", "delta_", "new_m_", "new_v_"], res):
            out[pre + n] = r[None]
    conv_pad = lambda v: jnp.pad(v.reshape(-1), (0, conv_rows * HEAD_DIM - conv_n)).reshape(conv_rows, HEAD_DIM)
    packs = []
    for src, cv in [(small, gdn_conv), (gs, gs_conv), ({n: a["m_" + n] for n in _ROWS}, m_gdn_conv),
                    ({n: a["v_" + n] for n in _ROWS}, v_gdn_conv)]:
        packs.append(jnp.concatenate([_pack_rows(src), conv_pad(cv)], axis=0))
    res = _adamw(*packs, name="adamw_small")
    for pre, r in zip(["grad_", "delta_", "new_m_", "new_v_"], res):
        vals, r1 = _unpack_rows(r, small)
        for n in _ROWS:
            out[pre + n] = vals[n]
        out[pre + "gdn_conv"] = r[r1:r1 + conv_rows].reshape(-1)[:conv_n].reshape(gdn_conv.shape)
    names = ["norm_mix", "w_in", "fox_f_bias", "fox_q_norm", "fox_k_norm", "gdn_conv", "gdn_a_log", "gdn_dt_bias",
             "gdn_out_norm", "mem_norm", "w_mem_kv", "mem_q_norm", "mem_k_norm", "w_out", "norm_ffn", "w_gate_up",
             "w_down"]
    return (out["loss"], out["grad_x"], *[out[p + n] for p in ["grad_", "delta_", "new_m_", "new_v_"] for n in names])
```

```python
import functools
import math

import jax
import jax.numpy as jnp
from jax import lax
from jax.experimental import pallas as pl
from jax.experimental.pallas import tpu as pltpu

F32, BF16 = jnp.float32, jnp.bfloat16
HEAD_DIM = 128
CHUNK = 64
N_MEM_HEADS = 4
CONV_WIDTH = 4
NORM_EPS = 1e-6
ADAM_LR, ADAM_B1, ADAM_B2, ADAM_EPS, ADAM_WD, ADAM_STEP = 0.001, 0.9, 0.999, 1e-08, 0.01, 10
VMEM_LIMIT = 48 * 1024 * 1024
NEG = -1e30
MESH = pl.DeviceIdType.MESH


def _cparams(sem=None, **kw):
    if sem is not None:
        kw["dimension_semantics"] = sem
    return pltpu.CompilerParams(vmem_limit_bytes=VMEM_LIMIT, **kw)


def _tile(n, target, mult=128):
    best = None
    d = mult
    while d <= min(n, target):
        if n % d == 0:
            best = d
        d += mult
    return best if best is not None else n


def _dot(a, b, dims, hi):
    if a.ndim == 3:
        dn = (((dims[0][0] + 1,), (dims[1][0] + 1,)), ((0,), (0,)))
    else:
        dn = (dims, ((), ()))
    if hi is not None:
        return lax.dot_general(a, b, dn, precision=hi, preferred_element_type=F32)
    return lax.dot_general(a.astype(BF16), b.astype(BF16), dn, preferred_element_type=F32)


def _make_dots(hi):
    @jax.custom_vjp
    def nn(a, b):
        return _dot(a, b, ((1,), (0,)), hi)

    @jax.custom_vjp
    def nt(a, b):
        return _dot(a, b, ((1,), (1,)), hi)

    @jax.custom_vjp
    def tn(a, b):
        return _dot(a, b, ((0,), (0,)), hi)

    nn.defvjp(lambda a, b: (nn(a, b), (a, b)), lambda r, g: (nt(g, r[1]), tn(r[0], g)))
    nt.defvjp(lambda a, b: (nt(a, b), (a, b)), lambda r, g: (nn(g, r[1]), tn(g, r[0])))
    tn.defvjp(lambda a, b: (tn(a, b), (a, b)), lambda r, g: (nt(r[1], g), nn(r[0], g)))
    return nn, nt, tn


_nn, _nt, _tn = _make_dots(None)
_nn_hi, _nt_hi, _tn_hi = _make_dots(lax.Precision.HIGHEST)
_nn_x3, _nt_x3, _tn_x3 = _make_dots(lax.Precision.HIGH)


def _sigmoid(x):
    return 1.0 / (1.0 + jnp.exp(-x))


@jax.custom_vjp
def _softplus(x):
    return jnp.maximum(x, 0.0) + jnp.log(1.0 + jnp.exp(-jnp.abs(x)))


_softplus.defvjp(lambda x: (_softplus(x), x), lambda x, g: (g * _sigmoid(x),))


def _silu(x):
    return x * _sigmoid(x)


def _rms_fn(x, gain, z=None):
    y = x * lax.rsqrt(jnp.mean(x * x, axis=-1, keepdims=True) + NORM_EPS) * gain
    if z is not None:
        y = y * _silu(z)
    return y


def _mm(a, b, *, ta=False, tb=False, out_dtype=F32, res=None, stack=None, name):
    a2, b2 = a.shape[-2:], b.shape[-2:]
    ns = b.shape[0] if stack else 1
    m = a2[1] if ta else a2[0]
    k = a2[0] if ta else a2[1]
    n = b2[0] if tb else b2[1]
    assert k == (b2[1] if tb else b2[0])
    tm, tn, tk = _tile(m, 1024), _tile(n, 1408 if n % 1408 == 0 else 1024), _tile(k, 512)
    nk = k // tk
    dims = ((0 if ta else 1,), (1 if tb else 0,))
    if stack == "sum":
        order = lambda g0, g1, g2, g3: (g2, g0, g1, g3)
        grid = (m // tm, n // tn, ns, nk)
    else:
        order = lambda g0, g1, g2, g3: (g0, g1, g2, g3)
        grid = (ns, m // tm, n // tn, nk)

    def body(*refs):
        if res is None:
            a_ref, b_ref, o_ref, acc = refs
        else:
            a_ref, b_ref, r_ref, o_ref, acc = refs
        s, _, _, kk = order(*[pl.program_id(d) for d in range(4)])
        first = kk == 0
        last = kk == nk - 1
        if stack == "sum":
            first, last = first & (s == 0), last & (s == ns - 1)

        @pl.when(first)
        def _():
            acc[...] = jnp.zeros_like(acc)

        acc[...] += lax.dot_general(a_ref[...].astype(BF16), b_ref[...].astype(BF16), (dims, ((), ())),
                                    preferred_element_type=F32)

        @pl.when(last)
        def _():
            r = acc[...]
            if res is not None:
                r = r + r_ref[...]
            o_ref[...] = r.astype(out_dtype)

    def spec(shape, idx, stacked):
        if stacked:
            return pl.BlockSpec((None,) + shape, lambda *g: (order(*g)[0],) + idx(*order(*g)))
        return pl.BlockSpec(shape, lambda *g: idx(*order(*g)))

    a_spec = (spec((tk, tm), lambda s, i, j, kk: (kk, i), stack == "sum") if ta
              else spec((tm, tk), lambda s, i, j, kk: (i, kk), stack == "sum"))
    b_spec = (spec((tn, tk), lambda s, i, j, kk: (j, kk), bool(stack)) if tb
              else spec((tk, tn), lambda s, i, j, kk: (kk, j), bool(stack)))
    o_spec = spec((tm, tn), lambda s, i, j, kk: (i, j), stack == "out")
    ins, specs = [a, b], [a_spec, b_spec]
    if res is not None:
        ins.append(res)
        specs.append(o_spec)
    sem = (("parallel", "parallel", "arbitrary", "arbitrary") if stack == "sum"
           else ("parallel", "parallel", "parallel", "arbitrary"))
    return pl.pallas_call(
        body, name=name, grid=grid, in_specs=specs, out_specs=o_spec,
        out_shape=jax.ShapeDtypeStruct(((ns,) if stack == "out" else ()) + (m, n), out_dtype),
        scratch_shapes=[pltpu.VMEM((tm, tn), F32)],
        compiler_params=_cparams(sem),
    )(*ins)


def _norm_fwd(x, xoff, gain, ncol, w, out_dtype, *, z=None, zoff=0, name):
    t = x.shape[0]
    tr = _tile(t, 256, 8)

    def body(*refs):
        if z is None:
            x_ref, g_ref, o_ref = refs
            y = _rms_fn(x_ref[...], g_ref[...])
        else:
            x_ref, g_ref, z_ref, o_ref = refs
            y = _rms_fn(x_ref[...], g_ref[...], z_ref[...])
        o_ref[...] = y.astype(out_dtype)

    ins = [x, gain]
    specs = [pl.BlockSpec((tr, w), lambda j, r: (r, xoff + j)), pl.BlockSpec((1, w), lambda j, r: (0, 0))]
    if z is not None:
        ins.append(z)
        specs.append(pl.BlockSpec((tr, w), lambda j, r: (r, zoff + j)))
    return pl.pallas_call(
        body, name=name, grid=(ncol, t // tr), in_specs=specs,
        out_specs=pl.BlockSpec((tr, w), lambda j, r: (r, j)),
        out_shape=jax.ShapeDtypeStruct((t, ncol * w), out_dtype),
        compiler_params=_cparams(("parallel", "parallel")),
    )(*ins)


def _norm_bwd(x, xoff, gain, dy, dyoff, ncol, w, *, z=None, zoff=0, res=None, name):
    t = x.shape[0]
    tr = _tile(t, 256, 8)

    def body(*refs):
        it = iter(refs)
        x_ref, g_ref = next(it), next(it)
        z_ref = next(it) if z is not None else None
        dy_ref = next(it)
        r_ref = next(it) if res is not None else None
        dx_ref = next(it)
        dz_ref = next(it) if z is not None else None
        dg_ref = next(it)

        @pl.when((pl.program_id(0) == 0) & (pl.program_id(1) == 0))
        def _():
            dg_ref[...] = jnp.zeros_like(dg_ref)

        args = (x_ref[...], g_ref[...]) + ((z_ref[...],) if z is not None else ())
        _, vjp = jax.vjp(_rms_fn, *args)
        grads = vjp(dy_ref[...].astype(F32))
        dx = grads[0]
        if res is not None:
            dx = dx + r_ref[...]
        dx_ref[...] = dx
        if z is not None:
            dz_ref[...] = grads[2]
        dg_ref[...] += grads[1]

    ins = [x, gain]
    specs = [pl.BlockSpec((tr, w), lambda j, r: (r, xoff + j)), pl.BlockSpec((1, w), lambda j, r: (0, 0))]
    if z is not None:
        ins.append(z)
        specs.append(pl.BlockSpec((tr, w), lambda j, r: (r, zoff + j)))
    ins.append(dy)
    specs.append(pl.BlockSpec((tr, w), lambda j, r: (r, dyoff + j)))
    blk = pl.BlockSpec((tr, w), lambda j, r: (r, j))
    if res is not None:
        ins.append(res)
        specs.append(blk)
    full = jax.ShapeDtypeStruct((t, ncol * w), F32)
    out_shape, out_specs = [full], [blk]
    if z is not None:
        out_shape.append(full)
        out_specs.append(blk)
    out_shape.append(jax.ShapeDtypeStruct((1, w), F32))
    out_specs.append(pl.BlockSpec((1, w), lambda j, r: (0, 0)))
    return pl.pallas_call(
        body, name=name, grid=(ncol, t // tr), in_specs=specs, out_specs=out_specs, out_shape=out_shape,
        compiler_params=_cparams(("arbitrary", "arbitrary")),
    )(*ins)


def _small_fn(x, pa, pb, nf, ng):
    lane = lax.broadcasted_iota(jnp.int32, x.shape, 1)
    zz = x + pb
    logf = -_softplus(-zz)
    g = -jnp.exp(pa) * _softplus(zz)
    beta = _sigmoid(x)
    return jnp.where(lane < nf, logf, jnp.where(lane < nf + ng, g, beta))


def _tri(n, upper):
    r = lax.broadcasted_iota(jnp.int32, (n, n), 0)
    c = lax.broadcasted_iota(jnp.int32, (n, n), 1)
    return jnp.where((c >= r) if upper else (c <= r), 1.0, 0.0).astype(F32)


def _small_fwd(p, off, pa, pb, nf, ng):
    t = p.shape[0]
    blk = HEAD_DIM
    nb = t // blk

    def body(x_ref, pa_ref, pb_ref, v_ref, c_ref):
        v_ref[...] = _small_fn(x_ref[...], pa_ref[...], pb_ref[...], nf, ng)
        tri = _tri(blk, False)

        carry = jnp.zeros((1, HEAD_DIM), F32)
        for i in range(nb):
            rows = slice(i * blk, (i + 1) * blk)
            c = _nn_hi(tri, v_ref[rows, :]) + carry
            c_ref[rows, :] = c
            carry = c[blk - 1:blk, :]

    row = pl.BlockSpec((1, HEAD_DIM), lambda i: (0, 0))
    out = pl.BlockSpec((t, HEAD_DIM), lambda i: (0, 0))
    return pl.pallas_call(
        body, name="small_fwd", grid=(1,),
        in_specs=[pl.BlockSpec((t, HEAD_DIM), lambda i: (0, off)), row, row], out_specs=[out, out],
        out_shape=[jax.ShapeDtypeStruct((t, HEAD_DIM), F32)] * 2,
        compiler_params=_cparams(("arbitrary",)),
    )(p, pa, pb)


def _small_bwd(p, off, pa, pb, dvals, dcsum, nf, ng):
    t = p.shape[0]
    blk = HEAD_DIM
    nb = t // blk

    def body(x_ref, pa_ref, pb_ref, dv_ref, dc_ref, dx_ref, dpa_ref, dpb_ref, tot_ref):
        tri = _tri(blk, True)

        carry = jnp.zeros((1, HEAD_DIM), F32)
        for i in reversed(range(nb)):
            rows = slice(i * blk, (i + 1) * blk)
            c = _nn_hi(tri, dc_ref[rows, :]) + carry
            tot_ref[rows, :] = c + dv_ref[rows, :]
            carry = c[0:1, :]
        f = functools.partial(_small_fn, nf=nf, ng=ng)
        _, vjp = jax.vjp(f, x_ref[...], pa_ref[...], pb_ref[...])
        dx, dpa, dpb = vjp(tot_ref[...])
        dx_ref[...] = dx
        dpa_ref[...] = dpa
        dpb_ref[...] = dpb

    row = pl.BlockSpec((1, HEAD_DIM), lambda i: (0, 0))
    full = pl.BlockSpec((t, HEAD_DIM), lambda i: (0, 0))
    return pl.pallas_call(
        body, name="small_bwd", grid=(1,),
        in_specs=[pl.BlockSpec((t, HEAD_DIM), lambda i: (0, off)), row, row, full, full],
        out_specs=[full, row, row],
        out_shape=[jax.ShapeDtypeStruct((t, HEAD_DIM), F32), jax.ShapeDtypeStruct((1, HEAD_DIM), F32),
                   jax.ShapeDtypeStruct((1, HEAD_DIM), F32)],
        scratch_shapes=[pltpu.VMEM((t, HEAD_DIM), F32)],
        compiler_params=_cparams(("arbitrary",)),
    )(p, pa, pb, dvals, dcsum)


def _fox_fwd(q, k, v, cc, cr, nf, tq, tk):
    t = q.shape[0]
    scale = HEAD_DIM ** -0.5
    ratio = tq // tk

    def body(q_ref, k_ref, v_ref, cc_ref, cr_ref, o_ref, lse_ref):
        i = pl.program_id(1)
        qv = q_ref[...]
        ccol = cc_ref[0]
        rows = i * tq + lax.broadcasted_iota(jnp.int32, (tq, tk), 0)
        cols0 = lax.broadcasted_iota(jnp.int32, (tq, tk), 1)

        def step(j, carry):
            m, l, acc = carry
            ks = pl.ds(pl.multiple_of(j * tk, tk), tk)
            s = lax.dot_general(qv, k_ref[ks, :], (((1,), (1,)), ((), ())), preferred_element_type=F32) * scale
            s = s + ccol - cr_ref[0, j]
            s = jnp.where(cols0 + j * tk <= rows, s, NEG)
            m_new = jnp.maximum(m, jnp.max(s, axis=1, keepdims=True))
            pr = jnp.exp(s - m_new)
            alpha = jnp.exp(m - m_new)
            l = alpha * l + jnp.sum(pr, axis=1, keepdims=True)
            acc = alpha * acc + jnp.dot(pr.astype(BF16), v_ref[ks, :], preferred_element_type=F32)
            return m_new, l, acc

        init = (jnp.full((tq, 1), NEG, F32), jnp.zeros((tq, 1), F32), jnp.zeros((tq, HEAD_DIM), F32))
        m, l, acc = lax.fori_loop(0, (i + 1) * ratio, step, init)
        o_ref[...] = acc / l
        lse_ref[0] = m + jnp.log(l)

    head_all = pl.BlockSpec((t, HEAD_DIM), lambda h, i: (0, h))
    return pl.pallas_call(
        body, name="fox_fwd", grid=(nf, t // tq),
        in_specs=[pl.BlockSpec((tq, HEAD_DIM), lambda h, i: (i, h)), head_all, head_all,
                  pl.BlockSpec((1, tq, 1), lambda h, i: (h, i, 0)),
                  pl.BlockSpec((1, t // tk, 1, tk), lambda h, i: (h, 0, 0, 0))],
        out_specs=[pl.BlockSpec((tq, HEAD_DIM), lambda h, i: (i, h)),
                   pl.BlockSpec((1, tq, 1), lambda h, i: (h, i, 0))],
        out_shape=[jax.ShapeDtypeStruct((t, nf * HEAD_DIM), F32), jax.ShapeDtypeStruct((nf, t, 1), F32)],
        compiler_params=_cparams(("parallel", "parallel")),
    )(q, k, v, cc, cr)


def _fox_bwd(q, k, v, cc, cr, o, lse, dmix, nf, tq, tk):
    t = q.shape[0]
    scale = HEAD_DIM ** -0.5
    ratio = tq // tk

    def body(q_ref, k_ref, v_ref, cc_ref, cr_ref, o_ref, lse_ref, do_ref,
             dq_ref, dk_ref, dv_ref, dcc_ref, dcr_ref):
        i = pl.program_id(1)

        @pl.when(i == 0)
        def _():
            dk_ref[...] = jnp.zeros_like(dk_ref)
            dv_ref[...] = jnp.zeros_like(dv_ref)
            dcr_ref[...] = jnp.zeros_like(dcr_ref)

        qv = q_ref[...]
        ccol = cc_ref[0]
        lse_v = lse_ref[0]
        do = do_ref[...]
        do_b = do.astype(BF16)
        delta = jnp.sum(do * o_ref[...], axis=1, keepdims=True)
        rows = i * tq + lax.broadcasted_iota(jnp.int32, (tq, tk), 0)
        cols0 = lax.broadcasted_iota(jnp.int32, (tq, tk), 1)

        def step(j, carry):
            dq, dcc = carry
            ks = pl.ds(pl.multiple_of(j * tk, tk), tk)
            kj, vj = k_ref[ks, :], v_ref[ks, :]
            s = lax.dot_general(qv, kj, (((1,), (1,)), ((), ())), preferred_element_type=F32) * scale
            s = s + ccol - cr_ref[0, j]
            pr = jnp.where(cols0 + j * tk <= rows, jnp.exp(s - lse_v), 0.0)
            dp = lax.dot_general(do_b, vj, (((1,), (1,)), ((), ())), preferred_element_type=F32)
            ds = pr * (dp - delta)
            ds_b = ds.astype(BF16)
            dq = dq + jnp.dot(ds_b, kj, preferred_element_type=F32) * scale
            dk_ref[ks, :] += lax.dot_general(ds_b, qv, (((0,), (0,)), ((), ())),
                                             preferred_element_type=F32) * scale
            dv_ref[ks, :] += lax.dot_general(pr.astype(BF16), do_b, (((0,), (0,)), ((), ())),
                                             preferred_element_type=F32)
            dcr_ref[0, j] -= jnp.sum(ds, axis=0, keepdims=True)
            return dq, dcc + jnp.sum(ds, axis=1, keepdims=True)

        init = (jnp.zeros((tq, HEAD_DIM), F32), jnp.zeros((tq, 1), F32))
        dq, dcc = lax.fori_loop(0, (i + 1) * ratio, step, init)
        dq_ref[...] = dq
        dcc_ref[0] = dcc

    head_all = pl.BlockSpec((t, HEAD_DIM), lambda h, i: (0, h))
    qblk = pl.BlockSpec((tq, HEAD_DIM), lambda h, i: (i, h))
    colv = pl.BlockSpec((1, tq, 1), lambda h, i: (h, i, 0))
    rowv = pl.BlockSpec((1, t // tk, 1, tk), lambda h, i: (h, 0, 0, 0))
    wide = jax.ShapeDtypeStruct((t, nf * HEAD_DIM), F32)
    return pl.pallas_call(
        body, name="fox_bwd", grid=(nf, t // tq),
        in_specs=[qblk, head_all, head_all, colv, rowv, qblk, colv, qblk],
        out_specs=[qblk, head_all, head_all, colv, rowv],
        out_shape=[wide, wide, wide, jax.ShapeDtypeStruct((nf, t, 1), F32),
                   jax.ShapeDtypeStruct((nf, t // tk, 1, tk), F32)],
        compiler_params=_cparams(("parallel", "arbitrary")),
    )(q, k, v, cc, cr, o, lse, dmix)


def _mem_fn(mq, mk, mv, gq, gk):
    qn = _rms_fn(mq, gq)
    kn = _rms_fn(mk, gk)
    s = _nt(qn, kn) * (HEAD_DIM ** -0.5)
    e = jnp.exp(s - lax.stop_gradient(jnp.max(s, axis=1, keepdims=True)))
    pr = e / jnp.sum(e, axis=1, keepdims=True)
    return _nn(pr, mv)


def _mem_specs(t, m, tq, qoff):
    qblk = pl.BlockSpec((tq, HEAD_DIM), lambda h, i: (i, qoff + h))
    kblk = pl.BlockSpec((m, HEAD_DIM), lambda h, i: (0, h))
    vblk = pl.BlockSpec((m, HEAD_DIM), lambda h, i: (0, N_MEM_HEADS + h))
    row = pl.BlockSpec((1, HEAD_DIM), lambda h, i: (0, 0))
    return qblk, kblk, vblk, row


def _mem_fwd(p, qoff, mkv, gq, gk, tq):
    t, m = p.shape[0], mkv.shape[0]
    qblk, kblk, vblk, row = _mem_specs(t, m, tq, qoff)

    def body(q_ref, k_ref, v_ref, gq_ref, gk_ref, o_ref):
        o_ref[...] = _mem_fn(q_ref[...], k_ref[...], v_ref[...], gq_ref[...], gk_ref[...])

    return pl.pallas_call(
        body, name="mem_fwd", grid=(N_MEM_HEADS, t // tq), in_specs=[qblk, kblk, vblk, row, row],
        out_specs=pl.BlockSpec((tq, HEAD_DIM), lambda h, i: (i, h)),
        out_shape=jax.ShapeDtypeStruct((t, N_MEM_HEADS * HEAD_DIM), F32),
        compiler_params=_cparams(("parallel", "parallel")),
    )(p, mkv, mkv, gq, gk)


def _mem_bwd(p, qoff, mkv, gq, gk, dmix, dooff, tq):
    t, m = p.shape[0], mkv.shape[0]
    qblk, kblk, vblk, row = _mem_specs(t, m, tq, qoff)

    def body(q_ref, k_ref, v_ref, gq_ref, gk_ref, do_ref, dq_ref, dkv_k_ref, dkv_v_ref, dgq_ref, dgk_ref):
        h, i = pl.program_id(0), pl.program_id(1)

        @pl.when((h == 0) & (i == 0))
        def _():
            dgq_ref[...] = jnp.zeros_like(dgq_ref)
            dgk_ref[...] = jnp.zeros_like(dgk_ref)

        @pl.when(i == 0)
        def _():
            dkv_k_ref[...] = jnp.zeros_like(dkv_k_ref)
            dkv_v_ref[...] = jnp.zeros_like(dkv_v_ref)

        _, vjp = jax.vjp(_mem_fn, q_ref[...], k_ref[...], v_ref[...], gq_ref[...], gk_ref[...])
        dq, dk, dv, dgq, dgk = vjp(do_ref[...])
        dq_ref[...] = dq
        dkv_k_ref[...] += dk
        dkv_v_ref[...] += dv
        dgq_ref[...] += dgq
        dgk_ref[...] += dgk

    oblk = pl.BlockSpec((tq, HEAD_DIM), lambda h, i: (i, h))
    kout = pl.BlockSpec((m, HEAD_DIM), lambda h, i: (0, h))
    half = jax.ShapeDtypeStruct((m, N_MEM_HEADS * HEAD_DIM), F32)
    rshape = jax.ShapeDtypeStruct((1, HEAD_DIM), F32)
    return pl.pallas_call(
        body, name="mem_bwd", grid=(N_MEM_HEADS, t // tq),
        in_specs=[qblk, kblk, vblk, row, row, pl.BlockSpec((tq, HEAD_DIM), lambda h, i: (i, dooff + h))],
        out_specs=[oblk, kout, kout, row, row],
        out_shape=[jax.ShapeDtypeStruct((t, N_MEM_HEADS * HEAD_DIM), F32), half, half, rshape, rshape],
        compiler_params=_cparams(("arbitrary", "arbitrary")),
    )(p, mkv, mkv, gq, gk, dmix)


def _shift_down(x, s):
    if s == 0:
        return x
    r = lax.broadcasted_iota(jnp.int32, x.shape, 0)
    return jnp.where(r >= s, pltpu.roll(x, s, 0), 0.0)


def _shift_up(x, s):
    if s == 0:
        return x
    n = x.shape[0]
    r = lax.broadcasted_iota(jnp.int32, x.shape, 0)
    return jnp.where(r < n - s, pltpu.roll(x, n - s, 0), 0.0)


def _conv_fn(x0, x1, x2, x3, w0, w1, w2, w3, kind):
    y = _silu(x0 * w0 + x1 * w1 + x2 * w2 + x3 * w3)
    if kind == 2:
        return y
    y = y * lax.rsqrt(jnp.sum(y * y, axis=-1, keepdims=True) + NORM_EPS)
    return y * (HEAD_DIM ** -0.5) if kind == 0 else y


def _conv_fwd(p, off, conv_w, ng):
    t = p.shape[0]

    def body(x_ref, w_ref, o_ref):
        kind = pl.program_id(0) // ng
        x = x_ref[...]
        xs = [_shift_down(x, CONV_WIDTH - 1 - j) for j in range(CONV_WIDTH)]
        ws = [w_ref[j:j + 1, :] for j in range(CONV_WIDTH)]
        for kd in range(3):
            @pl.when(kind == kd)
            def _(kd=kd):
                o_ref[...] = _conv_fn(*xs, *ws, kd)

    return pl.pallas_call(
        body, name="gdn_conv_fwd", grid=(3 * ng,),
        in_specs=[pl.BlockSpec((t, HEAD_DIM), lambda c: (0, off + c)),
                  pl.BlockSpec((CONV_WIDTH, HEAD_DIM), lambda c: (0, c))],
        out_specs=pl.BlockSpec((t, HEAD_DIM), lambda c: (0, c)),
        out_shape=jax.ShapeDtypeStruct((t, 3 * ng * HEAD_DIM), F32),
        compiler_params=_cparams(("parallel",)),
    )(p, conv_w)


def _conv_bwd(p, off, conv_w, dy, ng):
    t = p.shape[0]

    def body(x_ref, w_ref, dy_ref, dx_ref, dw_ref):
        kind = pl.program_id(0) // ng
        x = x_ref[...]
        xs = [_shift_down(x, CONV_WIDTH - 1 - j) for j in range(CONV_WIDTH)]
        ws = [w_ref[j:j + 1, :] for j in range(CONV_WIDTH)]
        for kd in range(3):
            @pl.when(kind == kd)
            def _(kd=kd):
                _, vjp = jax.vjp(functools.partial(_conv_fn, kind=kd), *xs, *ws)
                g = vjp(dy_ref[...])
                dx = _shift_up(g[0], CONV_WIDTH - 1)
                for j in range(1, CONV_WIDTH):
                    dx = dx + _shift_up(g[j], CONV_WIDTH - 1 - j)
                dx_ref[...] = dx
                for j in range(CONV_WIDTH):
                    dw_ref[j:j + 1, :] = g[CONV_WIDTH + j]

    blk = pl.BlockSpec((t, HEAD_DIM), lambda c: (0, c))
    wblk = pl.BlockSpec((CONV_WIDTH, HEAD_DIM), lambda c: (0, c))
    return pl.pallas_call(
        body, name="gdn_conv_bwd", grid=(3 * ng,),
        in_specs=[pl.BlockSpec((t, HEAD_DIM), lambda c: (0, off + c)), wblk, blk],
        out_specs=[blk, wblk],
        out_shape=[jax.ShapeDtypeStruct((t, 3 * ng * HEAD_DIM), F32),
                   jax.ShapeDtypeStruct((CONV_WIDTH, 3 * ng * HEAD_DIM), F32)],
        compiler_params=_cparams(("parallel",)),
    )(p, conv_w, dy)


def _wy_fn(q, k, v, gcol, grow, bcol):
    b, c, dk = q.shape
    r = lax.broadcasted_iota(jnp.int32, (1, c, c), 1)
    e = lax.broadcasted_iota(jnp.int32, (1, c, c), 2)
    tril, strict = e <= r, e < r
    gc_col = jnp.sum(jnp.where(tril, grow, 0.0), axis=2, keepdims=True)
    gc_row = jnp.sum(jnp.where(r <= e, gcol, 0.0), axis=1, keepdims=True)
    g_last = jnp.sum(gcol, axis=1, keepdims=True)
    decay = jnp.exp(jnp.where(tril, gc_col - gc_row, NEG))
    kb, vb = k * bcol, v * bcol
    lower = jnp.where(strict, _nt(kb, k) * decay, 0.0)
    inv = jnp.where(r == e, 1.0, 0.0) - lower
    pw = lower
    for _ in range(int(math.log2(c)) - 1):
        pw = _nn_x3(pw, pw)
        inv = inv + _nn_x3(inv, pw)
    u = _nn_x3(inv, vb)
    w = _nn_x3(inv, kb * jnp.exp(gc_col))
    attn = jnp.where(tril, _nt(q, k) * decay, 0.0)
    qg = q * jnp.exp(gc_col)
    kdec = k * jnp.exp(g_last - gc_col)
    egl = jnp.broadcast_to(jnp.exp(g_last), (b, 1, dk))
    return u, w, qg, kdec, attn, egl


def _scan_fn(u, w, qg, kdec, attn, egl, state):
    v_new = u - _nn(w, state)
    o = _nn(qg, state) + _nn(attn, v_new)
    return o, state * egl + _tn(kdec, v_new)


GDN_CHUNKS_PER_STEP = 4


def _gdn_fwd(qkv, gcol, grow, bcol, ng):
    t = qkv.shape[0]
    nch = t // CHUNK

    cb = GDN_CHUNKS_PER_STEP
    wy = _gdn_wy(qkv, gcol, grow, bcol, ng, cb)

    def body(u_ref, w_ref, qg_ref, kd_ref, at_ref, eg_ref, o_ref, st_ref, state):
        @pl.when(pl.program_id(0) == 0)
        def _():
            state[...] = jnp.zeros_like(state)

        st_ref[:, 0] = state[...]
        heads = lambda ref: jnp.stack([ref[:, h * HEAD_DIM:(h + 1) * HEAD_DIM] for h in range(ng)])
        o, new = _scan_fn(heads(u_ref), heads(w_ref), heads(qg_ref), heads(kd_ref), at_ref[:, 0], eg_ref[:, 0],
                          state[...])
        for h in range(ng):
            o_ref[:, h * HEAD_DIM:(h + 1) * HEAD_DIM] = o[h]
        state[...] = new

    w = ng * HEAD_DIM
    blk = pl.BlockSpec((CHUNK, w), lambda i: (i, 0))
    o, states = pl.pallas_call(
        body, name="gdn_scan_fwd", grid=(nch,),
        in_specs=[blk, blk, blk, blk, pl.BlockSpec((ng, 1, CHUNK, CHUNK), lambda i: (0, i, 0, 0)),
                  pl.BlockSpec((ng, 1, 1, HEAD_DIM), lambda i: (0, i, 0, 0))],
        out_specs=[blk, pl.BlockSpec((ng, 1, HEAD_DIM, HEAD_DIM), lambda i: (0, i, 0, 0))],
        out_shape=[jax.ShapeDtypeStruct((t, w), F32),
                   jax.ShapeDtypeStruct((ng, nch, HEAD_DIM, HEAD_DIM), F32)],
        scratch_shapes=[pltpu.VMEM((ng, HEAD_DIM, HEAD_DIM), F32)],
        compiler_params=_cparams(("arbitrary",)),
    )(*wy)
    return o, (wy, states)


def _wy_batch(q_ref, k_ref, v_ref, gc_ref, gr_ref, bc_ref, ng, cb):
    idx = [(c, h) for c in range(cb) for h in range(ng)]
    rows = lambda c: slice(c * CHUNK, (c + 1) * CHUNK)
    lanes = lambda h: slice(h * HEAD_DIM, (h + 1) * HEAD_DIM)
    wide = lambda ref: jnp.stack([ref[rows(c), lanes(h)] for c, h in idx])
    col = lambda ref: jnp.stack([ref[h, rows(c), :] for c, h in idx])
    return idx, (wide(q_ref), wide(k_ref), wide(v_ref), col(gc_ref), jnp.stack([gr_ref[h, c] for c, h in idx]),
                 col(bc_ref))


def _gdn_wy(qkv, gcol, grow, bcol, ng, cb):
    t = qkv.shape[0]
    nch = t // CHUNK

    def body(q_ref, k_ref, v_ref, gc_ref, gr_ref, bc_ref, u_ref, w_ref, qg_ref, kd_ref, at_ref, eg_ref):
        idx, args = _wy_batch(q_ref, k_ref, v_ref, gc_ref, gr_ref, bc_ref, ng, cb)
        u, w, qg, kd, at, eg = _wy_fn(*args)
        for b, (c, h) in enumerate(idx):
            rows, lanes = slice(c * CHUNK, (c + 1) * CHUNK), slice(h * HEAD_DIM, (h + 1) * HEAD_DIM)
            u_ref[rows, lanes] = u[b]
            w_ref[rows, lanes] = w[b]
            qg_ref[rows, lanes] = qg[b]
            kd_ref[rows, lanes] = kd[b]
            at_ref[h, c] = at[b]
            eg_ref[h, c] = eg[b]

    wd = ng * HEAD_DIM
    blk = lambda o: pl.BlockSpec((cb * CHUNK, wd), lambda i: (i, o))
    col = pl.BlockSpec((ng, cb * CHUNK, 1), lambda i: (0, i, 0))
    wide = jax.ShapeDtypeStruct((t, wd), F32)
    return pl.pallas_call(
        body, name="gdn_wy_fwd", grid=(nch // cb,),
        in_specs=[blk(0), blk(1), blk(2), col, pl.BlockSpec((ng, cb, 1, CHUNK), lambda i: (0, i, 0, 0)), col],
        out_specs=[blk(0), blk(0), blk(0), blk(0), pl.BlockSpec((ng, cb, CHUNK, CHUNK), lambda i: (0, i, 0, 0)),
                   pl.BlockSpec((ng, cb, 1, HEAD_DIM), lambda i: (0, i, 0, 0))],
        out_shape=[wide, wide, wide, wide, jax.ShapeDtypeStruct((ng, nch, CHUNK, CHUNK), F32),
                   jax.ShapeDtypeStruct((ng, nch, 1, HEAD_DIM), F32)],
        compiler_params=_cparams(("parallel",)),
    )(qkv, qkv, qkv, gcol, grow, bcol)


def _gdn_bwd(qkv, gcol, grow, bcol, saved, do, ng):
    t = qkv.shape[0]
    nch = t // CHUNK
    cb = GDN_CHUNKS_PER_STEP // 2
    wy, states = saved
    wd = ng * HEAD_DIM

    def scan_body(u_ref, w_ref, qg_ref, kd_ref, at_ref, eg_ref, st_ref, do_ref,
                  du_ref, dw_ref, dqg_ref, dkd_ref, dat_ref, deg_ref, dstate):
        @pl.when(pl.program_id(0) == 0)
        def _():
            dstate[...] = jnp.zeros_like(dstate)

        heads = lambda ref: jnp.stack([ref[:, h * HEAD_DIM:(h + 1) * HEAD_DIM] for h in range(ng)])
        _, vjp = jax.vjp(_scan_fn, heads(u_ref), heads(w_ref), heads(qg_ref), heads(kd_ref), at_ref[:, 0],
                         eg_ref[:, 0], st_ref[:, 0])
        du, dw, dqg, dkd, dat, deg, dst = vjp((heads(do_ref), dstate[...]))
        for h in range(ng):
            lanes = slice(h * HEAD_DIM, (h + 1) * HEAD_DIM)
            du_ref[:, lanes] = du[h]
            dw_ref[:, lanes] = dw[h]
            dqg_ref[:, lanes] = dqg[h]
            dkd_ref[:, lanes] = dkd[h]
        dat_ref[:, 0] = dat
        deg_ref[:, 0] = deg
        dstate[...] = dst

    rev = lambda i: nch - 1 - i
    blk = pl.BlockSpec((CHUNK, wd), lambda i: (rev(i), 0))
    atb = pl.BlockSpec((ng, 1, CHUNK, CHUNK), lambda i: (0, rev(i), 0, 0))
    egb = pl.BlockSpec((ng, 1, 1, HEAD_DIM), lambda i: (0, rev(i), 0, 0))
    wide = jax.ShapeDtypeStruct((t, wd), F32)
    at_shape = jax.ShapeDtypeStruct((ng, nch, CHUNK, CHUNK), F32)
    eg_shape = jax.ShapeDtypeStruct((ng, nch, 1, HEAD_DIM), F32)
    dwy = pl.pallas_call(
        scan_body, name="gdn_scan_bwd", grid=(nch,),
        in_specs=[blk, blk, blk, blk, atb, egb,
                  pl.BlockSpec((ng, 1, HEAD_DIM, HEAD_DIM), lambda i: (0, rev(i), 0, 0)), blk],
        out_specs=[blk, blk, blk, blk, atb, egb],
        out_shape=[wide, wide, wide, wide, at_shape, eg_shape],
        scratch_shapes=[pltpu.VMEM((ng, HEAD_DIM, HEAD_DIM), F32)],
        compiler_params=_cparams(("arbitrary",)),
    )(*wy, states, do)

    def wy_body(q_ref, k_ref, v_ref, gc_ref, gr_ref, bc_ref, du_ref, dw_ref, dqg_ref, dkd_ref, dat_ref, deg_ref,
                dq_ref, dk_ref, dv_ref, dgc_ref, dgr_ref, dbc_ref):
        idx, args = _wy_batch(q_ref, k_ref, v_ref, gc_ref, gr_ref, bc_ref, ng, cb)
        rows = lambda c: slice(c * CHUNK, (c + 1) * CHUNK)
        lanes = lambda h: slice(h * HEAD_DIM, (h + 1) * HEAD_DIM)
        wide_ct = lambda ref: jnp.stack([ref[rows(c), lanes(h)] for c, h in idx])
        cts = (wide_ct(du_ref), wide_ct(dw_ref), wide_ct(dqg_ref), wide_ct(dkd_ref),
               jnp.stack([dat_ref[h, c] for c, h in idx]), jnp.stack([deg_ref[h, c] for c, h in idx]))
        _, vjp = jax.vjp(_wy_fn, *args)
        dq, dk, dv, dgc, dgr, dbc = vjp(cts)
        for b, (c, h) in enumerate(idx):
            dq_ref[rows(c), lanes(h)] = dq[b]
            dk_ref[rows(c), lanes(h)] = dk[b]
            dv_ref[rows(c), lanes(h)] = dv[b]
            dgc_ref[h, rows(c), :] = dgc[b]
            dgr_ref[h, c] = dgr[b]
            dbc_ref[h, rows(c), :] = dbc[b]

    cblk = lambda o: pl.BlockSpec((cb * CHUNK, wd), lambda i: (i, o))
    col = pl.BlockSpec((ng, cb * CHUNK, 1), lambda i: (0, i, 0))
    rowv = pl.BlockSpec((ng, cb, 1, CHUNK), lambda i: (0, i, 0, 0))
    cshape = jax.ShapeDtypeStruct((ng, t, 1), F32)
    return pl.pallas_call(
        wy_body, name="gdn_wy_bwd", grid=(nch // cb,),
        in_specs=[cblk(0), cblk(1), cblk(2), col, rowv, col, cblk(0), cblk(0), cblk(0), cblk(0),
                  pl.BlockSpec((ng, cb, CHUNK, CHUNK), lambda i: (0, i, 0, 0)),
                  pl.BlockSpec((ng, cb, 1, HEAD_DIM), lambda i: (0, i, 0, 0))],
        out_specs=[cblk(0), cblk(0), cblk(0), col, rowv, col],
        out_shape=[wide, wide, wide, cshape, jax.ShapeDtypeStruct((ng, nch, 1, CHUNK), F32), cshape],
        compiler_params=_cparams(("parallel",)),
    )(qkv, qkv, qkv, gcol, grow, bcol, *dwy)


def _swiglu_fn(gate, up):
    return _silu(gate) * up


def _swiglu_specs(gu):
    _, t, w = gu.shape
    tr, tc = _tile(t, 512, 8), _tile(w, 1408)
    nc = w // tc
    pair = pl.BlockSpec((2, tr, tc), lambda j, r, c: (j, r, c))
    flat = pl.BlockSpec((tr, tc), lambda j, r, c: (r, j * nc + c))
    return (2, t // tr, nc), pair, flat


def _swiglu_fwd(gu):
    grid, pair, flat = _swiglu_specs(gu)

    def body(gu_ref, o_ref):
        o_ref[...] = _swiglu_fn(gu_ref[0], gu_ref[1]).astype(BF16)

    return pl.pallas_call(
        body, name="swiglu_fwd", grid=grid, in_specs=[pair], out_specs=flat,
        out_shape=jax.ShapeDtypeStruct((gu.shape[1], 2 * gu.shape[2]), BF16),
        compiler_params=_cparams(("parallel", "parallel", "parallel")),
    )(gu)


def _swiglu_bwd(gu, dact):
    grid, pair, flat = _swiglu_specs(gu)

    def body(gu_ref, d_ref, o_ref):
        _, vjp = jax.vjp(_swiglu_fn, gu_ref[0], gu_ref[1])
        dg, du = vjp(d_ref[...])
        o_ref[0] = dg.astype(BF16)
        o_ref[1] = du.astype(BF16)

    return pl.pallas_call(
        body, name="swiglu_bwd", grid=grid, in_specs=[pair, flat], out_specs=pair,
        out_shape=jax.ShapeDtypeStruct(gu.shape, BF16),
        compiler_params=_cparams(("parallel", "parallel", "parallel")),
    )(gu, dact)


def _loss_head(h2, target):
    t, d = h2.shape
    tr = _tile(t, 256, 8)

    def body(h_ref, t_ref, l_ref, d_ref):
        @pl.when(pl.program_id(0) == 0)
        def _():
            l_ref[...] = jnp.zeros_like(l_ref)

        err = h_ref[...] - t_ref[...]
        d_ref[...] = err * (1.0 / d)
        part = 0.5 * jnp.sum(jnp.mean(err * err, axis=-1, keepdims=True), axis=0, keepdims=True)
        lane = lax.broadcasted_iota(jnp.int32, (8, HEAD_DIM), 1)
        row = lax.broadcasted_iota(jnp.int32, (8, HEAD_DIM), 0)
        l_ref[...] += jnp.where((lane == 0) & (row == 0), part, 0.0)

    blk = pl.BlockSpec((tr, d), lambda r: (r, 0))
    return pl.pallas_call(
        body, name="loss_head", grid=(t // tr,), in_specs=[blk, blk],
        out_specs=[pl.BlockSpec((8, HEAD_DIM), lambda r: (0, 0)), blk],
        out_shape=[jax.ShapeDtypeStruct((8, HEAD_DIM), F32), jax.ShapeDtypeStruct((t, d), F32)],
        compiler_params=_cparams(("arbitrary",)),
    )(h2, target)


def _adamw(w, g, m, v, *, g2=None, name):
    r, c = w.shape
    tr = _tile(r, max(8, (1 << 19) // c // 8 * 8), 8)

    def body(*refs):
        if g2 is None:
            w_ref, g_ref, m_ref, v_ref, go_ref, d_ref, mo_ref, vo_ref = refs
            gr = g_ref[...]
        else:
            w_ref, g_ref, g2_ref, m_ref, v_ref, go_ref, d_ref, mo_ref, vo_ref = refs
            gr = g_ref[...] + g2_ref[...]
        mn = ADAM_B1 * m_ref[...] + (1.0 - ADAM_B1) * gr
        vn = ADAM_B2 * v_ref[...] + (1.0 - ADAM_B2) * (gr * gr)
        m_hat = mn / (1.0 - ADAM_B1 ** ADAM_STEP)
        v_hat = vn / (1.0 - ADAM_B2 ** ADAM_STEP)
        go_ref[...] = gr
        d_ref[...] = -ADAM_LR * (m_hat / (jnp.sqrt(v_hat) + ADAM_EPS) + ADAM_WD * w_ref[...])
        mo_ref[...] = mn
        vo_ref[...] = vn

    blk = pl.BlockSpec((tr, c), lambda i: (i, 0))
    n_in = 4 if g2 is None else 5
    ins = [w, g] + ([g2] if g2 is not None else []) + [m, v]
    return pl.pallas_call(
        body, name=name, grid=(r // tr,), in_specs=[blk] * n_in, out_specs=[blk] * 4,
        out_shape=[jax.ShapeDtypeStruct((r, c), F32)] * 4,
        compiler_params=_cparams(("parallel",)),
    )(*ins)


class _Layout:
    def __init__(self, d):
        nh = d // HEAD_DIM
        self.nm = N_MEM_HEADS
        self.nf = (nh - self.nm) // 2
        self.ng = nh - self.nm - self.nf
        nf, ng, nm = self.nf, self.ng, self.nm
        self.o_fq, self.o_fk, self.o_fv = 0, nf, 2 * nf
        self.o_gq = 3 * nf
        self.o_gz = 3 * nf + 3 * ng
        self.o_mq = 3 * nf + 4 * ng
        self.o_sm = self.o_mq + nm
        self.blocks = -(-(self.o_sm + 1) // 8) * 8
        self.cols = self.blocks * HEAD_DIM
        hd = HEAD_DIM
        sizes = [nf * hd, nf * hd, nf * hd, nf, 3 * ng * hd, ng * hd, ng, ng, nm * hd]
        starts = [sum(sizes[:i]) for i in range(len(sizes))]
        self.ref = list(zip(starts, sizes))
        self.in_cols = sum(sizes)

    def regroup(self, w):
        part = lambda i: w[:, self.ref[i][0]:self.ref[i][0] + self.ref[i][1]]
        pieces = [part(0), part(1), part(2), part(4), part(5), part(8), part(3), part(6), part(7)]
        pad = self.cols - self.in_cols
        return jnp.concatenate(pieces + [jnp.zeros((w.shape[0], pad), w.dtype)], axis=1)

    def ungroup(self, g):
        hd, nf, ng, nm = HEAD_DIM, self.nf, self.ng, self.nm
        sm = self.o_sm * hd
        return jnp.concatenate([
            g[:, :3 * nf * hd], g[:, sm:sm + nf], g[:, self.o_gq * hd:self.o_gz * hd],
            g[:, self.o_gz * hd:self.o_mq * hd], g[:, sm + nf:sm + nf + ng], g[:, sm + nf + ng:sm + nf + 2 * ng],
            g[:, self.o_mq * hd:self.o_sm * hd]], axis=1)


def _lane_row(pieces):
    row = jnp.zeros((1, HEAD_DIM), F32)
    for off, a in pieces:
        row = lax.dynamic_update_slice(row, a.astype(F32), (0, off))
    return row


def _local_step(x, mem, target, win, wmkv, wout, wgu, wd, sp):
    t, d = x.shape
    lay = _Layout(d)
    nf, ng, nm, hd = lay.nf, lay.ng, lay.nm, HEAD_DIM
    nch = t // CHUNK
    tq = _tile(t, 256)
    tk = tq

    u = _norm_fwd(x, 0, sp["norm_mix"], 1, d, BF16, name="norm_mix_fwd")
    p = _mm(u, win, name="mm_in")
    pa = _lane_row([(nf, sp["gdn_a_log"])])
    pb = _lane_row([(0, sp["fox_f_bias"]), (nf, sp["gdn_dt_bias"])])
    vals, csum = _small_fwd(p, lay.o_sm, pa, pb, nf, ng)

    c_t = csum[:, :nf].T
    cc, cr = c_t.reshape(nf, t, 1), c_t.reshape(nf, t // tk, 1, tk)
    fq = _norm_fwd(p, lay.o_fq, sp["fox_q_norm"], nf, hd, BF16, name="fox_qnorm_fwd")
    fk = _norm_fwd(p, lay.o_fk, sp["fox_k_norm"], nf, hd, BF16, name="fox_knorm_fwd")
    fv = p[:, lay.o_fv * hd:(lay.o_fv + nf) * hd].astype(BF16)
    o_fox, lse = _fox_fwd(fq, fk, fv, cc, cr, nf, tq, tk)

    qkv = _conv_fwd(p, lay.o_gq, sp["gdn_conv"], ng)
    g_t, b_t = vals[:, nf:nf + ng].T, vals[:, nf + ng:nf + 2 * ng].T
    gcol, grow, bcol = g_t.reshape(ng, t, 1), g_t.reshape(ng, nch, 1, CHUNK), b_t.reshape(ng, t, 1)
    o_g, states = _gdn_fwd(qkv, gcol, grow, bcol, ng)
    o_gdn = _norm_fwd(o_g, 0, sp["gdn_out_norm"], ng, hd, BF16, z=p, zoff=lay.o_gz, name="gdn_out_fwd")

    mem_n = _norm_fwd(mem, 0, sp["mem_norm"], 1, d, BF16, name="mem_norm_fwd")
    mkv = _mm(mem_n, wmkv, name="mm_memkv")
    o_mem = _mem_fwd(p, lay.o_mq, mkv, sp["mem_q_norm"], sp["mem_k_norm"], tq)

    mix = jnp.concatenate([o_fox.astype(BF16), o_gdn, o_mem.astype(BF16)], axis=1)
    h1 = _mm(mix, wout, res=x, name="mm_out")
    n2 = _norm_fwd(h1, 0, sp["norm_ffn"], 1, d, BF16, name="norm_ffn_fwd")
    wgu4 = wgu.reshape(4, d, -1)
    gu = _mm(n2, wgu4, stack="out", name="mm_gate_up")
    act = _swiglu_fwd(gu)
    h2 = _mm(act, wd, res=h1, name="mm_down")
    loss_blk, dh2 = _loss_head(h2, target)

    g = {}
    g["w_down"] = _mm(act, dh2, ta=True, out_dtype=BF16, name="mm_dw_down")
    dact = _mm(dh2, wd, tb=True, name="mm_dact")
    dgu = _swiglu_bwd(gu, dact)
    g["w_gate_up"] = _mm(n2, dgu, ta=True, stack="out", out_dtype=BF16, name="mm_dw_gate_up").reshape(wgu.shape)
    dn2 = _mm(dgu, wgu4, tb=True, stack="sum", name="mm_dn2")
    dh1, g["norm_ffn"] = _norm_bwd(h1, 0, sp["norm_ffn"], dn2, 0, 1, d, res=dh2, name="norm_ffn_bwd")
    g["w_out"] = _mm(mix, dh1, ta=True, out_dtype=BF16, name="mm_dw_out")
    dmix = _mm(dh1, wout, tb=True, name="mm_dmix")

    dmq, dmk, dmv, g["mem_q_norm"], g["mem_k_norm"] = _mem_bwd(
        p, lay.o_mq, mkv, sp["mem_q_norm"], sp["mem_k_norm"], dmix, nf + ng, tq)
    dmkv = jnp.concatenate([dmk, dmv], axis=1)
    g["w_mem_kv"] = _mm(mem_n, dmkv, ta=True, out_dtype=BF16, name="mm_dw_memkv")
    dmem_n = _mm(dmkv, wmkv, tb=True, name="mm_dmem")
    _, g["mem_norm"] = _norm_bwd(mem, 0, sp["mem_norm"], dmem_n, 0, 1, d, name="mem_norm_bwd")

    do_g, dgz, g["gdn_out_norm"] = _norm_bwd(o_g, 0, sp["gdn_out_norm"], dmix, nf, ng, hd, z=p, zoff=lay.o_gz,
                                             name="gdn_out_bwd")
    dq, dk, dv, dgc, dgr, dbc = _gdn_bwd(qkv, gcol, grow, bcol, states, do_g, ng)
    dgqkv, g["gdn_conv"] = _conv_bwd(p, lay.o_gq, sp["gdn_conv"], jnp.concatenate([dq, dk, dv], axis=1), ng)
    dg_t = dgc.reshape(ng, t) + dgr.reshape(ng, t)
    db_t = dbc.reshape(ng, t)

    dfq_n, dfk_n, dfv, dcc, dcr = _fox_bwd(fq, fk, fv, cc, cr, o_fox, lse, dmix, nf, tq, tk)
    dfq, g["fox_q_norm"] = _norm_bwd(p, lay.o_fq, sp["fox_q_norm"], dfq_n, 0, nf, hd, name="fox_qnorm_bwd")
    dfk, g["fox_k_norm"] = _norm_bwd(p, lay.o_fk, sp["fox_k_norm"], dfk_n, 0, nf, hd, name="fox_knorm_bwd")
    dc_t = dcc.reshape(nf, t) + dcr.reshape(nf, t)

    lanes_left = hd - nf - 2 * ng
    dvals = jnp.concatenate([jnp.zeros((t, nf), F32), dg_t.T, db_t.T, jnp.zeros((t, lanes_left), F32)], axis=1)
    dcsum = jnp.concatenate([dc_t.T, jnp.zeros((t, hd - nf), F32)], axis=1)
    dsm, dpa, dpb = _small_bwd(p, lay.o_sm, pa, pb, dvals, dcsum, nf, ng)
    g["fox_f_bias"] = dpb[:, :nf]
    g["gdn_dt_bias"] = dpb[:, nf:nf + ng]
    g["gdn_a_log"] = dpa[:, nf:nf + ng]

    pad = jnp.zeros((t, lay.cols - (lay.o_sm + 1) * hd), F32)
    dp = jnp.concatenate([dfq, dfk, dfv, dgqkv, dgz, dmq, dsm, pad], axis=1)
    g["w_in"] = _mm(u, dp, ta=True, out_dtype=BF16, name="mm_dw_in")
    du = _mm(dp, win, tb=True, name="mm_du")
    dx, g["norm_mix"] = _norm_bwd(x, 0, sp["norm_mix"], du, 0, 1, d, res=dh1, name="norm_mix_bwd")
    return loss_blk, dx, g


ANY = pl.BlockSpec(memory_space=pl.ANY)


def _me():
    x, y, c = lax.axis_index("x"), lax.axis_index("y"), lax.axis_index("c")
    chips = [(1 - x, y), (x, 1 - y), (1 - x, 1 - y)]
    return x, y, c, chips


def _slot(axis, k):
    return k if axis == 0 else 2 * (k % 2) + k // 2


def _slab(ref, axis, rows, cols, k, h):
    half = rows // 2
    return ref.at[pl.ds(_slot(axis, k) * rows + h * half, half), :]


def _remote(src, dst, send_sem, recv_sem, dev):
    return pltpu.make_async_remote_copy(src_ref=src, dst_ref=dst, send_sem=send_sem, recv_sem=recv_sem,
                                        device_id=dev, device_id_type=MESH)


def _gather_weights(shards, axes):
    n = len(shards)
    shapes = [s.shape for s in shards]
    chip = 2 * lax.axis_index("x") + lax.axis_index("y")
    bufs = []
    for s, a in zip(shards, axes):
        r, cl = s.shape
        bufs.append(lax.dynamic_update_slice(lax.empty((4 * r, cl), s.dtype), s, (_slot(a, chip) * r, 0)))

    def body(*refs):
        dst = refs[n:2 * n]
        send_sems, recv_sems = refs[2 * n:]
        x, y, c, chips = _me()
        k = 2 * x + y
        sibling = (x, y, 1 - c)
        sends = []
        for w in range(n):
            r, cl = shapes[w]
            for j, (px, py) in enumerate(chips):
                place = _slab(dst[w], axes[w], r, cl, k, c)
                cp = _remote(place, place, send_sems.at[6 * w + j], recv_sems.at[6 * w + j], (px, py, c))
                cp.start()
                sends.append(cp)
        for w in range(n):
            r, cl = shapes[w]
            for j, (px, py) in enumerate(chips):
                got = _slab(dst[w], axes[w], r, cl, 2 * px + py, c)
                _remote(got, got, send_sems.at[6 * w + j], recv_sems.at[6 * w + j], (px, py, c)).wait_recv()
                cp = _remote(got, got, send_sems.at[6 * w + 3 + j], recv_sems.at[6 * w + 3 + j], sibling)
                cp.start()
                sends.append(cp)
        for w in range(n):
            r, cl = shapes[w]
            for j, (px, py) in enumerate(chips):
                got = _slab(dst[w], axes[w], r, cl, 2 * px + py, 1 - c)
                _remote(got, got, send_sems.at[6 * w + 3 + j], recv_sems.at[6 * w + 3 + j], sibling).wait_recv()
        for cp in sends:
            cp.wait_send()

    return pl.pallas_call(
        body, name="gather_weights", in_specs=[ANY] * n, out_specs=[ANY] * n,
        out_shape=[jax.ShapeDtypeStruct(b.shape, b.dtype) for b in bufs],
        input_output_aliases={w: w for w in range(n)},
        scratch_shapes=[pltpu.SemaphoreType.DMA((6 * n,)), pltpu.SemaphoreType.DMA((6 * n,))],
    )(*bufs)


def _pair_exchange(fulls, axes, shapes):
    n = len(fulls)

    def body(*refs):
        src, dst = refs[:n], refs[n:2 * n]
        send_sems, recv_sems = refs[2 * n:]
        x, y, c, _ = _me()
        sibling = (x, y, 1 - c)
        cps = []
        for w in range(n):
            r, cl = shapes[w]
            for j in range(4):
                cp = _remote(_slab(src[w], axes[w], r, cl, j, 1 - c), dst[w].at[j],
                             send_sems.at[4 * w + j], recv_sems.at[4 * w + j], sibling)
                cp.start()
                cps.append(cp)
        for cp in cps:
            cp.wait()

    out_shape = [jax.ShapeDtypeStruct((4, r // 2, cl), f.dtype) for (r, cl), f in zip(shapes, fulls)]
    return pl.pallas_call(
        body, name="reduce_pair_exchange", in_specs=[ANY] * n, out_specs=[ANY] * n, out_shape=out_shape,
        scratch_shapes=[pltpu.SemaphoreType.DMA((4 * n,)), pltpu.SemaphoreType.DMA((4 * n,))],
    )(*fulls)


def _chip_exchange(parts):
    n = len(parts)

    def body(*refs):
        src, dst = refs[:n], refs[n:2 * n]
        send_sems, recv_sems = refs[2 * n:]
        x, y, c, chips = _me()
        k = 2 * x + y
        cps = []
        for w in range(n):
            for j, (px, py) in enumerate(chips):
                cp = _remote(src[w].at[2 * px + py], dst[w].at[k], send_sems.at[3 * w + j],
                             recv_sems.at[3 * w + j], (px, py, c))
                cp.start()
                cps.append(cp)
        for w in range(n):
            for j, (px, py) in enumerate(chips):
                got = dst[w].at[2 * px + py]
                _remote(got, got, send_sems.at[3 * w + j], recv_sems.at[3 * w + j], (px, py, c)).wait_recv()
        for cp in cps:
            cp.wait_send()

    slots = pl.pallas_call(
        body, name="reduce_chip_exchange", in_specs=[ANY] * n, out_specs=[ANY] * n,
        out_shape=[jax.ShapeDtypeStruct(p.shape, p.dtype) for p in parts],
        scratch_shapes=[pltpu.SemaphoreType.DMA((3 * n,)), pltpu.SemaphoreType.DMA((3 * n,))],
    )(*parts)
    chip = 2 * lax.axis_index("x") + lax.axis_index("y")
    return [lax.dynamic_update_slice(s, lax.dynamic_index_in_dim(p, chip, 0, keepdims=True), (chip, 0, 0))
            for s, p in zip(slots, parts)]


def _half_swap(halves):
    n = len(halves)
    core = lax.axis_index("c")
    bufs = [lax.dynamic_update_slice(lax.empty((2,) + h.shape, h.dtype), h[None], (core, 0, 0)) for h in halves]

    def body(*refs):
        dst = refs[n:2 * n]
        send_sems, recv_sems = refs[2 * n:]
        x, y, c, _ = _me()
        sibling = (x, y, 1 - c)
        cps = []
        for w in range(n):
            cp = _remote(dst[w].at[c], dst[w].at[c], send_sems.at[w], recv_sems.at[w], sibling)
            cp.start()
            cps.append(cp)
        for w in range(n):
            other = dst[w].at[1 - c]
            _remote(other, other, send_sems.at[w], recv_sems.at[w], sibling).wait_recv()
        for cp in cps:
            cp.wait_send()

    outs = pl.pallas_call(
        body, name="reduce_half_swap", in_specs=[ANY] * n, out_specs=[ANY] * n,
        out_shape=[jax.ShapeDtypeStruct(b.shape, b.dtype) for b in bufs],
        input_output_aliases={w: w for w in range(n)},
        scratch_shapes=[pltpu.SemaphoreType.DMA((n,)), pltpu.SemaphoreType.DMA((n,))],
    )(*bufs)
    return [o.reshape(2 * o.shape[1], o.shape[2]) for o in outs]


def _add_parts(a, b, name):
    _, r, c = a.shape
    tr, tc = _tile(r, 256, 8), _tile(c, 2048)

    def body(a_ref, b_ref, o_ref):
        o_ref[...] = (a_ref[...].astype(F32) + b_ref[...].astype(F32)).astype(BF16)

    blk = pl.BlockSpec((1, tr, tc), lambda j, i, l: (j, i, l))
    return pl.pallas_call(
        body, name=name, grid=(4, r // tr, c // tc), in_specs=[blk, blk], out_specs=blk,
        out_shape=jax.ShapeDtypeStruct(a.shape, BF16),
        compiler_params=_cparams(("parallel", "parallel", "parallel")),
    )(a, b)


def _sum_slots(a, name):
    _, r, c = a.shape
    tr, tc = _tile(r, 256, 8), _tile(c, 2048)

    def body(a_ref, o_ref):
        v = a_ref[...].astype(F32)
        o_ref[...] = ((v[0] + v[1]) + v[2]) + v[3]

    return pl.pallas_call(
        body, name=name, grid=(r // tr, c // tc),
        in_specs=[pl.BlockSpec((4, tr, tc), lambda i, l: (0, i, l))],
        out_specs=pl.BlockSpec((tr, tc), lambda i, l: (i, l)),
        out_shape=jax.ShapeDtypeStruct((r, c), F32),
        compiler_params=_cparams(("parallel", "parallel")),
    )(a)


def _half_of(full, axis, rows, cols, c):
    half = rows // 2
    v = lax.dynamic_index_in_dim(full.reshape(4, 2, half, cols), c, 1, keepdims=False)
    if axis == 0:
        return v
    return jnp.stack([v[_slot(axis, k)] for k in range(4)])


def _reduce_grads(fulls, axes, shapes):
    c = lax.axis_index("c")
    from_sibling = _pair_exchange(fulls, axes, shapes)
    parts = [_add_parts(_half_of(f, a, r, cl, c), s, name=f"reduce_add_{w}")
             for w, (f, a, (r, cl), s) in enumerate(zip(fulls, axes, shapes, from_sibling))]
    slots = _chip_exchange(parts)
    halves = [_sum_slots(s, name=f"reduce_sum_{w}") for w, s in enumerate(slots)]
    return _half_swap(halves)


def _allreduce_small(pack):
    rows = pack.shape[0]

    def body(p_ref, o_ref, slots, send_sems, recv_sems):
        x, y, c, _ = _me()
        me = 4 * x + 2 * y + c
        slots[me] = p_ref[...]
        cps = []
        for r in range(1, 8):
            peer = (x ^ (r >> 2), y ^ ((r >> 1) & 1), c ^ (r & 1))
            cp = _remote(p_ref, slots.at[me], send_sems.at[r - 1], recv_sems.at[r - 1], peer)
            cp.start()
            cps.append(cp)
        for r in range(1, 8):
            frm = me ^ r
            _remote(slots.at[frm], slots.at[frm], send_sems.at[r - 1], recv_sems.at[r - 1], (x, y, c)).wait_recv()
        for cp in cps:
            cp.wait_send()
        acc = slots[0]
        for s in range(1, 8):
            acc = acc + slots[s]
        o_ref[...] = acc

    vm = pl.BlockSpec(memory_space=pltpu.VMEM)
    return pl.pallas_call(
        body, name="allreduce_small", in_specs=[vm], out_specs=vm,
        out_shape=jax.ShapeDtypeStruct(pack.shape, F32),
        scratch_shapes=[pltpu.VMEM((8, rows, HEAD_DIM), F32), pltpu.SemaphoreType.DMA((7,)),
                        pltpu.SemaphoreType.DMA((7,))],
    )(pack)


_ROWS = ["norm_mix", "norm_ffn", "mem_norm", "fox_q_norm", "fox_k_norm", "gdn_out_norm", "mem_q_norm",
         "mem_k_norm", "fox_f_bias", "gdn_a_log", "gdn_dt_bias"]


def _pack_rows(vals):
    out = []
    for name in _ROWS:
        v = vals[name].reshape(-1)
        n = -(-v.shape[0] // HEAD_DIM) * HEAD_DIM
        out.append(jnp.pad(v, (0, n - v.shape[0])).reshape(-1, HEAD_DIM))
    return jnp.concatenate(out, axis=0)


def _unpack_rows(pack, like):
    out, r = {}, 0
    for name in _ROWS:
        n = like[name].shape[-1]
        nr = -(-n // HEAD_DIM)
        out[name] = pack[r:r + nr].reshape(1, -1)[:, :n]
        r += nr
    return out, r


def kernel(x, mem, norm_mix, w_in, fox_f_bias, fox_q_norm, fox_k_norm, gdn_conv, gdn_a_log, gdn_dt_bias, gdn_out_norm, mem_norm, w_mem_kv, mem_q_norm, mem_k_norm, w_out, norm_ffn, w_gate_up, w_down, loss_target, m_norm_mix, m_w_in, m_fox_f_bias, m_fox_q_norm, m_fox_k_norm, m_gdn_conv, m_gdn_a_log, m_gdn_dt_bias, m_gdn_out_norm, m_mem_norm, m_w_mem_kv, m_mem_q_norm, m_mem_k_norm, m_w_out, m_norm_ffn, m_w_gate_up, m_w_down, v_norm_mix, v_w_in, v_fox_f_bias, v_fox_q_norm, v_fox_k_norm, v_gdn_conv, v_gdn_a_log, v_gdn_dt_bias, v_gdn_out_norm, v_mem_norm, v_w_mem_kv, v_mem_q_norm, v_mem_k_norm, v_w_out, v_norm_ffn, v_w_gate_up, v_w_down):
    a = dict(locals())
    d = x.shape[-1]
    lay = _Layout(d)
    chip = 2 * lax.axis_index("x") + lax.axis_index("y")
    small = {n: a[n] for n in _ROWS}
    big = ["w_in", "w_mem_kv", "w_out", "w_gate_up", "w_down"]
    axes = [0, 0, 0, 1, 0]

    conv_cols = gdn_conv.shape[-1]
    conv_n = CONV_WIDTH * conv_cols
    conv_rows = -(-conv_n // HEAD_DIM)
    conv_blk = jnp.pad(gdn_conv.reshape(-1), (0, 32 * HEAD_DIM - conv_n)).reshape(32, HEAD_DIM)
    shards = [lay.regroup(w_in[0]).astype(BF16), w_mem_kv[0].astype(BF16), w_out[0].astype(BF16),
              w_gate_up[0].astype(BF16), w_down[0].astype(BF16)]
    shapes = [s.shape for s in shards]
    win, wmkv, wout, wgu, wd, conv_all = _gather_weights(shards + [conv_blk], axes + [0])
    conv_full = conv_all.reshape(4, 32 * HEAD_DIM)[:, :conv_n].reshape(4, CONV_WIDTH, conv_cols)
    conv_full = jnp.transpose(conv_full, (1, 0, 2)).reshape(CONV_WIDTH, 4 * conv_cols)

    sp = dict(small)
    sp["gdn_conv"] = conv_full
    loss_blk, dx, g = _local_step(x[0], mem[0], loss_target[0], win, wmkv, wout, wgu, wd, sp)

    gbig = list(_reduce_grads([g[n] for n in big], axes, shapes))
    gbig[0] = lay.ungroup(gbig[0])
    gsmall = {n: g[n] for n in _ROWS}
    pack = jnp.concatenate([_pack_rows(gsmall), g["gdn_conv"].reshape(-1, HEAD_DIM), loss_blk], axis=0)
    pack = jnp.pad(pack, ((0, -pack.shape[0] % 8), (0, 0)))
    tot = _allreduce_small(pack)
    gs, r0 = _unpack_rows(tot, small)
    conv_g = tot[r0:r0 + CONV_WIDTH * 4 * conv_cols // HEAD_DIM].reshape(CONV_WIDTH, 4 * conv_cols)
    gs_conv = lax.dynamic_slice_in_dim(conv_g, chip * conv_cols, conv_cols, axis=1)
    loss = tot[r0 + CONV_WIDTH * 4 * conv_cols // HEAD_DIM, 0]

    out = {"loss": loss, "grad_x": dx[None]}
    for n, gsh in zip(big, gbig):
        res = _adamw(a[n][0], gsh, a["m_" + n][0], a["v_" + n][0], name="adamw_" + n)
        for pre, r in zip(["grad_", "delta_", "new_m_", "new_v_"], res):
            out[pre + n] = r[None]
    conv_pad = lambda v: jnp.pad(v.reshape(-1), (0, conv_rows * HEAD_DIM - conv_n)).reshape(conv_rows, HEAD_DIM)
    packs = []
    for src, cv in [(small, gdn_conv), (gs, gs_conv), ({n: a["m_" + n] for n in _ROWS}, m_gdn_conv),
                    ({n: a["v_" + n] for n in _ROWS}, v_gdn_conv)]:
        packs.append(jnp.concatenate([_pack_rows(src), conv_pad(cv)], axis=0))
    res = _adamw(*packs, name="adamw_small")
    for pre, r in zip(["grad_", "delta_", "new_m_", "new_v_"], res):
        vals, r1 = _unpack_rows(r, small)
        for n in _ROWS:
            out[pre + n] = vals[n]
        out[pre + "gdn_conv"] = r[r1:r1 + conv_rows].reshape(-1)[:conv_n].reshape(gdn_conv.shape)
    names = ["norm_mix", "w_in", "fox_f_bias", "fox_q_norm", "fox_k_norm", "gdn_conv", "gdn_a_log", "gdn_dt_bias",
             "gdn_out_norm", "mem_norm", "w_mem_kv", "mem_q_norm", "mem_k_norm", "w_out", "norm_ffn", "w_gate_up",
             "w_down"]
    return (out["loss"], out["grad_x"], *[out[p + n] for p in ["grad_", "delta_", "new_m_", "new_v_"] for n in names])
```

```python
import functools
import math

import jax
import jax.numpy as jnp
from jax import lax
from jax.experimental import pallas as pl
from jax.experimental.pallas import tpu as pltpu

F32, BF16 = jnp.float32, jnp.bfloat16
HEAD_DIM = 128
CHUNK = 64
N_MEM_HEADS = 4
CONV_WIDTH = 4
NORM_EPS = 1e-6
ADAM_LR, ADAM_B1, ADAM_B2, ADAM_EPS, ADAM_WD, ADAM_STEP = 0.001, 0.9, 0.999, 1e-08, 0.01, 10
VMEM_LIMIT = 48 * 1024 * 1024
NEG = -1e30
MESH = pl.DeviceIdType.MESH


def _cparams(sem=None, **kw):
    if sem is not None:
        kw["dimension_semantics"] = sem
    return pltpu.CompilerParams(vmem_limit_bytes=VMEM_LIMIT, **kw)


def _tile(n, target, mult=128):
    best = None
    d = mult
    while d <= min(n, target):
        if n % d == 0:
            best = d
        d += mult
    return best if best is not None else n


def _dot(a, b, dims, hi):
    if a.ndim == 3:
        dn = (((dims[0][0] + 1,), (dims[1][0] + 1,)), ((0,), (0,)))
    else:
        dn = (dims, ((), ()))
    if hi is not None:
        return lax.dot_general(a, b, dn, precision=hi, preferred_element_type=F32)
    return lax.dot_general(a.astype(BF16), b.astype(BF16), dn, preferred_element_type=F32)


def _make_dots(hi):
    @jax.custom_vjp
    def nn(a, b):
        return _dot(a, b, ((1,), (0,)), hi)

    @jax.custom_vjp
    def nt(a, b):
        return _dot(a, b, ((1,), (1,)), hi)

    @jax.custom_vjp
    def tn(a, b):
        return _dot(a, b, ((0,), (0,)), hi)

    nn.defvjp(lambda a, b: (nn(a, b), (a, b)), lambda r, g: (nt(g, r[1]), tn(r[0], g)))
    nt.defvjp(lambda a, b: (nt(a, b), (a, b)), lambda r, g: (nn(g, r[1]), tn(g, r[0])))
    tn.defvjp(lambda a, b: (tn(a, b), (a, b)), lambda r, g: (nt(r[1], g), nn(r[0], g)))
    return nn, nt, tn


_nn, _nt, _tn = _make_dots(None)
_nn_hi, _nt_hi, _tn_hi = _make_dots(lax.Precision.HIGHEST)
_nn_x3, _nt_x3, _tn_x3 = _make_dots(lax.Precision.HIGH)


def _sigmoid(x):
    return 1.0 / (1.0 + jnp.exp(-x))


@jax.custom_vjp
def _softplus(x):
    return jnp.maximum(x, 0.0) + jnp.log(1.0 + jnp.exp(-jnp.abs(x)))


_softplus.defvjp(lambda x: (_softplus(x), x), lambda x, g: (g * _sigmoid(x),))


def _silu(x):
    return x * _sigmoid(x)


def _rms_fn(x, gain, z=None):
    y = x * lax.rsqrt(jnp.mean(x * x, axis=-1, keepdims=True) + NORM_EPS) * gain
    if z is not None:
        y = y * _silu(z)
    return y


def _mm(a, b, *, ta=False, tb=False, out_dtype=F32, res=None, stack=None, name):
    a2, b2 = a.shape[-2:], b.shape[-2:]
    ns = b.shape[0] if stack else 1
    m = a2[1] if ta else a2[0]
    k = a2[0] if ta else a2[1]
    n = b2[0] if tb else b2[1]
    assert k == (b2[1] if tb else b2[0])
    tm, tn, tk = _tile(m, 1024), _tile(n, 1408 if n % 1408 == 0 else 1024), _tile(k, 512)
    nk = k // tk
    dims = ((0 if ta else 1,), (1 if tb else 0,))
    if stack == "sum":
        order = lambda g0, g1, g2, g3: (g2, g0, g1, g3)
        grid = (m // tm, n // tn, ns, nk)
    else:
        order = lambda g0, g1, g2, g3: (g0, g1, g2, g3)
        grid = (ns, m // tm, n // tn, nk)

    def body(*refs):
        if res is None:
            a_ref, b_ref, o_ref, acc = refs
        else:
            a_ref, b_ref, r_ref, o_ref, acc = refs
        s, _, _, kk = order(*[pl.program_id(d) for d in range(4)])
        first = kk == 0
        last = kk == nk - 1
        if stack == "sum":
            first, last = first & (s == 0), last & (s == ns - 1)

        @pl.when(first)
        def _():
            acc[...] = jnp.zeros_like(acc)

        acc[...] += lax.dot_general(a_ref[...].astype(BF16), b_ref[...].astype(BF16), (dims, ((), ())),
                                    preferred_element_type=F32)

        @pl.when(last)
        def _():
            r = acc[...]
            if res is not None:
                r = r + r_ref[...]
            o_ref[...] = r.astype(out_dtype)

    def spec(shape, idx, stacked):
        if stacked:
            return pl.BlockSpec((None,) + shape, lambda *g: (order(*g)[0],) + idx(*order(*g)))
        return pl.BlockSpec(shape, lambda *g: idx(*order(*g)))

    a_spec = (spec((tk, tm), lambda s, i, j, kk: (kk, i), stack == "sum") if ta
              else spec((tm, tk), lambda s, i, j, kk: (i, kk), stack == "sum"))
    b_spec = (spec((tn, tk), lambda s, i, j, kk: (j, kk), bool(stack)) if tb
              else spec((tk, tn), lambda s, i, j, kk: (kk, j), bool(stack)))
    o_spec = spec((tm, tn), lambda s, i, j, kk: (i, j), stack == "out")
    ins, specs = [a, b], [a_spec, b_spec]
    if res is not None:
        ins.append(res)
        specs.append(o_spec)
    sem = (("parallel", "parallel", "arbitrary", "arbitrary") if stack == "sum"
           else ("parallel", "parallel", "parallel", "arbitrary"))
    return pl.pallas_call(
        body, name=name, grid=grid, in_specs=specs, out_specs=o_spec,
        out_shape=jax.ShapeDtypeStruct(((ns,) if stack == "out" else ()) + (m, n), out_dtype),
        scratch_shapes=[pltpu.VMEM((tm, tn), F32)],
        compiler_params=_cparams(sem),
    )(*ins)


def _norm_fwd(x, xoff, gain, ncol, w, out_dtype, *, z=None, zoff=0, name):
    t = x.shape[0]
    tr = _tile(t, 256, 8)

    def body(*refs):
        if z is None:
            x_ref, g_ref, o_ref = refs
            y = _rms_fn(x_ref[...], g_ref[...])
        else:
            x_ref, g_ref, z_ref, o_ref = refs
            y = _rms_fn(x_ref[...], g_ref[...], z_ref[...])
        o_ref[...] = y.astype(out_dtype)

    ins = [x, gain]
    specs = [pl.BlockSpec((tr, w), lambda j, r: (r, xoff + j)), pl.BlockSpec((1, w), lambda j, r: (0, 0))]
    if z is not None:
        ins.append(z)
        specs.append(pl.BlockSpec((tr, w), lambda j, r: (r, zoff + j)))
    return pl.pallas_call(
        body, name=name, grid=(ncol, t // tr), in_specs=specs,
        out_specs=pl.BlockSpec((tr, w), lambda j, r: (r, j)),
        out_shape=jax.ShapeDtypeStruct((t, ncol * w), out_dtype),
        compiler_params=_cparams(("parallel", "parallel")),
    )(*ins)


def _norm_bwd(x, xoff, gain, dy, dyoff, ncol, w, *, z=None, zoff=0, res=None, name):
    t = x.shape[0]
    tr = _tile(t, 256, 8)

    def body(*refs):
        it = iter(refs)
        x_ref, g_ref = next(it), next(it)
        z_ref = next(it) if z is not None else None
        dy_ref = next(it)
        r_ref = next(it) if res is not None else None
        dx_ref = next(it)
        dz_ref = next(it) if z is not None else None
        dg_ref = next(it)

        @pl.when((pl.program_id(0) == 0) & (pl.program_id(1) == 0))
        def _():
            dg_ref[...] = jnp.zeros_like(dg_ref)

        args = (x_ref[...], g_ref[...]) + ((z_ref[...],) if z is not None else ())
        _, vjp = jax.vjp(_rms_fn, *args)
        grads = vjp(dy_ref[...].astype(F32))
        dx = grads[0]
        if res is not None:
            dx = dx + r_ref[...]
        dx_ref[...] = dx
        if z is not None:
            dz_ref[...] = grads[2]
        dg_ref[...] += grads[1]

    ins = [x, gain]
    specs = [pl.BlockSpec((tr, w), lambda j, r: (r, xoff + j)), pl.BlockSpec((1, w), lambda j, r: (0, 0))]
    if z is not None:
        ins.append(z)
        specs.append(pl.BlockSpec((tr, w), lambda j, r: (r, zoff + j)))
    ins.append(dy)
    specs.append(pl.BlockSpec((tr, w), lambda j, r: (r, dyoff + j)))
    blk = pl.BlockSpec((tr, w), lambda j, r: (r, j))
    if res is not None:
        ins.append(res)
        specs.append(blk)
    full = jax.ShapeDtypeStruct((t, ncol * w), F32)
    out_shape, out_specs = [full], [blk]
    if z is not None:
        out_shape.append(full)
        out_specs.append(blk)
    out_shape.append(jax.ShapeDtypeStruct((1, w), F32))
    out_specs.append(pl.BlockSpec((1, w), lambda j, r: (0, 0)))
    return pl.pallas_call(
        body, name=name, grid=(ncol, t // tr), in_specs=specs, out_specs=out_specs, out_shape=out_shape,
        compiler_params=_cparams(("arbitrary", "arbitrary")),
    )(*ins)


def _small_fn(x, pa, pb, nf, ng):
    lane = lax.broadcasted_iota(jnp.int32, x.shape, 1)
    zz = x + pb
    logf = -_softplus(-zz)
    g = -jnp.exp(pa) * _softplus(zz)
    beta = _sigmoid(x)
    return jnp.where(lane < nf, logf, jnp.where(lane < nf + ng, g, beta))


def _tri(n, upper):
    r = lax.broadcasted_iota(jnp.int32, (n, n), 0)
    c = lax.broadcasted_iota(jnp.int32, (n, n), 1)
    return jnp.where((c >= r) if upper else (c <= r), 1.0, 0.0).astype(F32)


def _small_fwd(p, off, pa, pb, nf, ng):
    t = p.shape[0]
    blk = HEAD_DIM
    nb = t // blk

    def body(x_ref, pa_ref, pb_ref, v_ref, c_ref):
        v_ref[...] = _small_fn(x_ref[...], pa_ref[...], pb_ref[...], nf, ng)
        tri = _tri(blk, False)

        carry = jnp.zeros((1, HEAD_DIM), F32)
        for i in range(nb):
            rows = slice(i * blk, (i + 1) * blk)
            c = _nn_hi(tri, v_ref[rows, :]) + carry
            c_ref[rows, :] = c
            carry = c[blk - 1:blk, :]

    row = pl.BlockSpec((1, HEAD_DIM), lambda i: (0, 0))
    out = pl.BlockSpec((t, HEAD_DIM), lambda i: (0, 0))
    return pl.pallas_call(
        body, name="small_fwd", grid=(1,),
        in_specs=[pl.BlockSpec((t, HEAD_DIM), lambda i: (0, off)), row, row], out_specs=[out, out],
        out_shape=[jax.ShapeDtypeStruct((t, HEAD_DIM), F32)] * 2,
        compiler_params=_cparams(("arbitrary",)),
    )(p, pa, pb)


def _small_bwd(p, off, pa, pb, dvals, dcsum, nf, ng):
    t = p.shape[0]
    blk = HEAD_DIM
    nb = t // blk

    def body(x_ref, pa_ref, pb_ref, dv_ref, dc_ref, dx_ref, dpa_ref, dpb_ref, tot_ref):
        tri = _tri(blk, True)

        carry = jnp.zeros((1, HEAD_DIM), F32)
        for i in reversed(range(nb)):
            rows = slice(i * blk, (i + 1) * blk)
            c = _nn_hi(tri, dc_ref[rows, :]) + carry
            tot_ref[rows, :] = c + dv_ref[rows, :]
            carry = c[0:1, :]
        f = functools.partial(_small_fn, nf=nf, ng=ng)
        _, vjp = jax.vjp(f, x_ref[...], pa_ref[...], pb_ref[...])
        dx, dpa, dpb = vjp(tot_ref[...])
        dx_ref[...] = dx
        dpa_ref[...] = dpa
        dpb_ref[...] = dpb

    row = pl.BlockSpec((1, HEAD_DIM), lambda i: (0, 0))
    full = pl.BlockSpec((t, HEAD_DIM), lambda i: (0, 0))
    return pl.pallas_call(
        body, name="small_bwd", grid=(1,),
        in_specs=[pl.BlockSpec((t, HEAD_DIM), lambda i: (0, off)), row, row, full, full],
        out_specs=[full, row, row],
        out_shape=[jax.ShapeDtypeStruct((t, HEAD_DIM), F32), jax.ShapeDtypeStruct((1, HEAD_DIM), F32),
                   jax.ShapeDtypeStruct((1, HEAD_DIM), F32)],
        scratch_shapes=[pltpu.VMEM((t, HEAD_DIM), F32)],
        compiler_params=_cparams(("arbitrary",)),
    )(p, pa, pb, dvals, dcsum)


def _fox_fwd(q, k, v, cc, cr, nf, tq, tk):
    t = q.shape[0]
    scale = HEAD_DIM ** -0.5
    ratio = tq // tk

    def body(q_ref, k_ref, v_ref, cc_ref, cr_ref, o_ref, lse_ref):
        i = pl.program_id(1)
        qv = q_ref[...]
        ccol = cc_ref[0]
        rows = i * tq + lax.broadcasted_iota(jnp.int32, (tq, tk), 0)
        cols0 = lax.broadcasted_iota(jnp.int32, (tq, tk), 1)

        def step(j, carry):
            m, l, acc = carry
            ks = pl.ds(pl.multiple_of(j * tk, tk), tk)
            s = lax.dot_general(qv, k_ref[ks, :], (((1,), (1,)), ((), ())), preferred_element_type=F32) * scale
            s = s + ccol - cr_ref[0, j]
            s = jnp.where(cols0 + j * tk <= rows, s, NEG)
            m_new = jnp.maximum(m, jnp.max(s, axis=1, keepdims=True))
            pr = jnp.exp(s - m_new)
            alpha = jnp.exp(m - m_new)
            l = alpha * l + jnp.sum(pr, axis=1, keepdims=True)
            acc = alpha * acc + jnp.dot(pr.astype(BF16), v_ref[ks, :], preferred_element_type=F32)
            return m_new, l, acc

        init = (jnp.full((tq, 1), NEG, F32), jnp.zeros((tq, 1), F32), jnp.zeros((tq, HEAD_DIM), F32))
        m, l, acc = lax.fori_loop(0, (i + 1) * ratio, step, init)
        o_ref[...] = acc / l
        lse_ref[0] = m + jnp.log(l)

    head_all = pl.BlockSpec((t, HEAD_DIM), lambda h, i: (0, h))
    return pl.pallas_call(
        body, name="fox_fwd", grid=(nf, t // tq),
        in_specs=[pl.BlockSpec((tq, HEAD_DIM), lambda h, i: (i, h)), head_all, head_all,
                  pl.BlockSpec((1, tq, 1), lambda h, i: (h, i, 0)),
                  pl.BlockSpec((1, t // tk, 1, tk), lambda h, i: (h, 0, 0, 0))],
        out_specs=[pl.BlockSpec((tq, HEAD_DIM), lambda h, i: (i, h)),
                   pl.BlockSpec((1, tq, 1), lambda h, i: (h, i, 0))],
        out_shape=[jax.ShapeDtypeStruct((t, nf * HEAD_DIM), F32), jax.ShapeDtypeStruct((nf, t, 1), F32)],
        compiler_params=_cparams(("parallel", "parallel")),
    )(q, k, v, cc, cr)


def _fox_bwd(q, k, v, cc, cr, o, lse, dmix, nf, tq, tk):
    t = q.shape[0]
    scale = HEAD_DIM ** -0.5
    ratio = tq // tk

    def body(q_ref, k_ref, v_ref, cc_ref, cr_ref, o_ref, lse_ref, do_ref,
             dq_ref, dk_ref, dv_ref, dcc_ref, dcr_ref):
        i = pl.program_id(1)

        @pl.when(i == 0)
        def _():
            dk_ref[...] = jnp.zeros_like(dk_ref)
            dv_ref[...] = jnp.zeros_like(dv_ref)
            dcr_ref[...] = jnp.zeros_like(dcr_ref)

        qv = q_ref[...]
        ccol = cc_ref[0]
        lse_v = lse_ref[0]
        do = do_ref[...]
        do_b = do.astype(BF16)
        delta = jnp.sum(do * o_ref[...], axis=1, keepdims=True)
        rows = i * tq + lax.broadcasted_iota(jnp.int32, (tq, tk), 0)
        cols0 = lax.broadcasted_iota(jnp.int32, (tq, tk), 1)

        def step(j, carry):
            dq, dcc = carry
            ks = pl.ds(pl.multiple_of(j * tk, tk), tk)
            kj, vj = k_ref[ks, :], v_ref[ks, :]
            s = lax.dot_general(qv, kj, (((1,), (1,)), ((), ())), preferred_element_type=F32) * scale
            s = s + ccol - cr_ref[0, j]
            pr = jnp.where(cols0 + j * tk <= rows, jnp.exp(s - lse_v), 0.0)
            dp = lax.dot_general(do_b, vj, (((1,), (1,)), ((), ())), preferred_element_type=F32)
            ds = pr * (dp - delta)
            ds_b = ds.astype(BF16)
            dq = dq + jnp.dot(ds_b, kj, preferred_element_type=F32) * scale
            dk_ref[ks, :] += lax.dot_general(ds_b, qv, (((0,), (0,)), ((), ())),
                                             preferred_element_type=F32) * scale
            dv_ref[ks, :] += lax.dot_general(pr.astype(BF16), do_b, (((0,), (0,)), ((), ())),
                                             preferred_element_type=F32)
            dcr_ref[0, j] -= jnp.sum(ds, axis=0, keepdims=True)
            return dq, dcc + jnp.sum(ds, axis=1, keepdims=True)

        init = (jnp.zeros((tq, HEAD_DIM), F32), jnp.zeros((tq, 1), F32))
        dq, dcc = lax.fori_loop(0, (i + 1) * ratio, step, init)
        dq_ref[...] = dq
        dcc_ref[0] = dcc

    head_all = pl.BlockSpec((t, HEAD_DIM), lambda h, i: (0, h))
    qblk = pl.BlockSpec((tq, HEAD_DIM), lambda h, i: (i, h))
    colv = pl.BlockSpec((1, tq, 1), lambda h, i: (h, i, 0))
    rowv = pl.BlockSpec((1, t // tk, 1, tk), lambda h, i: (h, 0, 0, 0))
    wide = jax.ShapeDtypeStruct((t, nf * HEAD_DIM), F32)
    return pl.pallas_call(
        body, name="fox_bwd", grid=(nf, t // tq),
        in_specs=[qblk, head_all, head_all, colv, rowv, qblk, colv, qblk],
        out_specs=[qblk, head_all, head_all, colv, rowv],
        out_shape=[wide, wide, wide, jax.ShapeDtypeStruct((nf, t, 1), F32),
                   jax.ShapeDtypeStruct((nf, t // tk, 1, tk), F32)],
        compiler_params=_cparams(("parallel", "arbitrary")),
    )(q, k, v, cc, cr, o, lse, dmix)


def _mem_fn(mq, mk, mv, gq, gk):
    qn = _rms_fn(mq, gq)
    kn = _rms_fn(mk, gk)
    s = _nt(qn, kn) * (HEAD_DIM ** -0.5)
    e = jnp.exp(s - lax.stop_gradient(jnp.max(s, axis=1, keepdims=True)))
    pr = e / jnp.sum(e, axis=1, keepdims=True)
    return _nn(pr, mv)


def _mem_specs(t, m, tq, qoff):
    qblk = pl.BlockSpec((tq, HEAD_DIM), lambda h, i: (i, qoff + h))
    kblk = pl.BlockSpec((m, HEAD_DIM), lambda h, i: (0, h))
    vblk = pl.BlockSpec((m, HEAD_DIM), lambda h, i: (0, N_MEM_HEADS + h))
    row = pl.BlockSpec((1, HEAD_DIM), lambda h, i: (0, 0))
    return qblk, kblk, vblk, row


def _mem_fwd(p, qoff, mkv, gq, gk, tq):
    t, m = p.shape[0], mkv.shape[0]
    qblk, kblk, vblk, row = _mem_specs(t, m, tq, qoff)

    def body(q_ref, k_ref, v_ref, gq_ref, gk_ref, o_ref):
        o_ref[...] = _mem_fn(q_ref[...], k_ref[...], v_ref[...], gq_ref[...], gk_ref[...])

    return pl.pallas_call(
        body, name="mem_fwd", grid=(N_MEM_HEADS, t // tq), in_specs=[qblk, kblk, vblk, row, row],
        out_specs=pl.BlockSpec((tq, HEAD_DIM), lambda h, i: (i, h)),
        out_shape=jax.ShapeDtypeStruct((t, N_MEM_HEADS * HEAD_DIM), F32),
        compiler_params=_cparams(("parallel", "parallel")),
    )(p, mkv, mkv, gq, gk)


def _mem_bwd(p, qoff, mkv, gq, gk, dmix, dooff, tq):
    t, m = p.shape[0], mkv.shape[0]
    qblk, kblk, vblk, row = _mem_specs(t, m, tq, qoff)

    def body(q_ref, k_ref, v_ref, gq_ref, gk_ref, do_ref, dq_ref, dkv_k_ref, dkv_v_ref, dgq_ref, dgk_ref):
        h, i = pl.program_id(0), pl.program_id(1)

        @pl.when((h == 0) & (i == 0))
        def _():
            dgq_ref[...] = jnp.zeros_like(dgq_ref)
            dgk_ref[...] = jnp.zeros_like(dgk_ref)

        @pl.when(i == 0)
        def _():
            dkv_k_ref[...] = jnp.zeros_like(dkv_k_ref)
            dkv_v_ref[...] = jnp.zeros_like(dkv_v_ref)

        _, vjp = jax.vjp(_mem_fn, q_ref[...], k_ref[...], v_ref[...], gq_ref[...], gk_ref[...])
        dq, dk, dv, dgq, dgk = vjp(do_ref[...])
        dq_ref[...] = dq
        dkv_k_ref[...] += dk
        dkv_v_ref[...] += dv
        dgq_ref[...] += dgq
        dgk_ref[...] += dgk

    oblk = pl.BlockSpec((tq, HEAD_DIM), lambda h, i: (i, h))
    kout = pl.BlockSpec((m, HEAD_DIM), lambda h, i: (0, h))
    half = jax.ShapeDtypeStruct((m, N_MEM_HEADS * HEAD_DIM), F32)
    rshape = jax.ShapeDtypeStruct((1, HEAD_DIM), F32)
    return pl.pallas_call(
        body, name="mem_bwd", grid=(N_MEM_HEADS, t // tq),
        in_specs=[qblk, kblk, vblk, row, row, pl.BlockSpec((tq, HEAD_DIM), lambda h, i: (i, dooff + h))],
        out_specs=[oblk, kout, kout, row, row],
        out_shape=[jax.ShapeDtypeStruct((t, N_MEM_HEADS * HEAD_DIM), F32), half, half, rshape, rshape],
        compiler_params=_cparams(("arbitrary", "arbitrary")),
    )(p, mkv, mkv, gq, gk, dmix)


def _shift_down(x, s):
    if s == 0:
        return x
    r = lax.broadcasted_iota(jnp.int32, x.shape, 0)
    return jnp.where(r >= s, pltpu.roll(x, s, 0), 0.0)


def _shift_up(x, s):
    if s == 0:
        return x
    n = x.shape[0]
    r = lax.broadcasted_iota(jnp.int32, x.shape, 0)
    return jnp.where(r < n - s, pltpu.roll(x, n - s, 0), 0.0)


def _conv_fn(x0, x1, x2, x3, w0, w1, w2, w3, kind):
    y = _silu(x0 * w0 + x1 * w1 + x2 * w2 + x3 * w3)
    if kind == 2:
        return y
    y = y * lax.rsqrt(jnp.sum(y * y, axis=-1, keepdims=True) + NORM_EPS)
    return y * (HEAD_DIM ** -0.5) if kind == 0 else y


def _conv_fwd(p, off, conv_w, ng):
    t = p.shape[0]

    def body(x_ref, w_ref, o_ref):
        kind = pl.program_id(0) // ng
        x = x_ref[...]
        xs = [_shift_down(x, CONV_WIDTH - 1 - j) for j in range(CONV_WIDTH)]
        ws = [w_ref[j:j + 1, :] for j in range(CONV_WIDTH)]
        for kd in range(3):
            @pl.when(kind == kd)
            def _(kd=kd):
                o_ref[...] = _conv_fn(*xs, *ws, kd)

    return pl.pallas_call(
        body, name="gdn_conv_fwd", grid=(3 * ng,),
        in_specs=[pl.BlockSpec((t, HEAD_DIM), lambda c: (0, off + c)),
                  pl.BlockSpec((CONV_WIDTH, HEAD_DIM), lambda c: (0, c))],
        out_specs=pl.BlockSpec((t, HEAD_DIM), lambda c: (0, c)),
        out_shape=jax.ShapeDtypeStruct((t, 3 * ng * HEAD_DIM), F32),
        compiler_params=_cparams(("parallel",)),
    )(p, conv_w)


def _conv_bwd(p, off, conv_w, dy, ng):
    t = p.shape[0]

    def body(x_ref, w_ref, dy_ref, dx_ref, dw_ref):
        kind = pl.program_id(0) // ng
        x = x_ref[...]
        xs = [_shift_down(x, CONV_WIDTH - 1 - j) for j in range(CONV_WIDTH)]
        ws = [w_ref[j:j + 1, :] for j in range(CONV_WIDTH)]
        for kd in range(3):
            @pl.when(kind == kd)
            def _(kd=kd):
                _, vjp = jax.vjp(functools.partial(_conv_fn, kind=kd), *xs, *ws)
                g = vjp(dy_ref[...])
                dx = _shift_up(g[0], CONV_WIDTH - 1)
                for j in range(1, CONV_WIDTH):
                    dx = dx + _shift_up(g[j], CONV_WIDTH - 1 - j)
                dx_ref[...] = dx
                for j in range(CONV_WIDTH):
                    dw_ref[j:j + 1, :] = g[CONV_WIDTH + j]

    blk = pl.BlockSpec((t, HEAD_DIM), lambda c: (0, c))
    wblk = pl.BlockSpec((CONV_WIDTH, HEAD_DIM), lambda c: (0, c))
    return pl.pallas_call(
        body, name="gdn_conv_bwd", grid=(3 * ng,),
        in_specs=[pl.BlockSpec((t, HEAD_DIM), lambda c: (0, off + c)), wblk, blk],
        out_specs=[blk, wblk],
        out_shape=[jax.ShapeDtypeStruct((t, 3 * ng * HEAD_DIM), F32),
                   jax.ShapeDtypeStruct((CONV_WIDTH, 3 * ng * HEAD_DIM), F32)],
        compiler_params=_cparams(("parallel",)),
    )(p, conv_w, dy)


def _wy_fn(q, k, v, gcol, grow, bcol):
    b, c, dk = q.shape
    r = lax.broadcasted_iota(jnp.int32, (1, c, c), 1)
    e = lax.broadcasted_iota(jnp.int32, (1, c, c), 2)
    tril, strict = e <= r, e < r
    gc_col = jnp.sum(jnp.where(tril, grow, 0.0), axis=2, keepdims=True)
    gc_row = jnp.sum(jnp.where(r <= e, gcol, 0.0), axis=1, keepdims=True)
    g_last = jnp.sum(gcol, axis=1, keepdims=True)
    decay = jnp.exp(jnp.where(tril, gc_col - gc_row, NEG))
    kb, vb = k * bcol, v * bcol
    lower = jnp.where(strict, _nt(kb, k) * decay, 0.0)
    inv = jnp.where(r == e, 1.0, 0.0) - lower
    pw = lower
    for _ in range(int(math.log2(c)) - 1):
        pw = _nn_x3(pw, pw)
        inv = inv + _nn_x3(inv, pw)
    u = _nn_x3(inv, vb)
    w = _nn_x3(inv, kb * jnp.exp(gc_col))
    attn = jnp.where(tril, _nt(q, k) * decay, 0.0)
    qg = q * jnp.exp(gc_col)
    kdec = k * jnp.exp(g_last - gc_col)
    egl = jnp.broadcast_to(jnp.exp(g_last), (b, 1, dk))
    return u, w, qg, kdec, attn, egl


def _scan_fn(u, w, qg, kdec, attn, egl, state):
    v_new = u - _nn(w, state)
    o = _nn(qg, state) + _nn(attn, v_new)
    return o, state * egl + _tn(kdec, v_new)


GDN_CHUNKS_PER_STEP = 4


def _gdn_fwd(qkv, gcol, grow, bcol, ng):
    t = qkv.shape[0]
    nch = t // CHUNK

    cb = GDN_CHUNKS_PER_STEP
    wy = _gdn_wy(qkv, gcol, grow, bcol, ng, cb)

    def body(u_ref, w_ref, qg_ref, kd_ref, at_ref, eg_ref, o_ref, st_ref, state):
        @pl.when(pl.program_id(0) == 0)
        def _():
            state[...] = jnp.zeros_like(state)

        st_ref[:, 0] = state[...]
        heads = lambda ref: jnp.stack([ref[:, h * HEAD_DIM:(h + 1) * HEAD_DIM] for h in range(ng)])
        o, new = _scan_fn(heads(u_ref), heads(w_ref), heads(qg_ref), heads(kd_ref), at_ref[:, 0], eg_ref[:, 0],
                          state[...])
        for h in range(ng):
            o_ref[:, h * HEAD_DIM:(h + 1) * HEAD_DIM] = o[h]
        state[...] = new

    w = ng * HEAD_DIM
    blk = pl.BlockSpec((CHUNK, w), lambda i: (i, 0))
    o, states = pl.pallas_call(
        body, name="gdn_scan_fwd", grid=(nch,),
        in_specs=[blk, blk, blk, blk, pl.BlockSpec((ng, 1, CHUNK, CHUNK), lambda i: (0, i, 0, 0)),
                  pl.BlockSpec((ng, 1, 1, HEAD_DIM), lambda i: (0, i, 0, 0))],
        out_specs=[blk, pl.BlockSpec((ng, 1, HEAD_DIM, HEAD_DIM), lambda i: (0, i, 0, 0))],
        out_shape=[jax.ShapeDtypeStruct((t, w), F32),
                   jax.ShapeDtypeStruct((ng, nch, HEAD_DIM, HEAD_DIM), F32)],
        scratch_shapes=[pltpu.VMEM((ng, HEAD_DIM, HEAD_DIM), F32)],
        compiler_params=_cparams(("arbitrary",)),
    )(*wy)
    return o, (wy, states)


def _wy_batch(q_ref, k_ref, v_ref, gc_ref, gr_ref, bc_ref, ng, cb):
    idx = [(c, h) for c in range(cb) for h in range(ng)]
    rows = lambda c: slice(c * CHUNK, (c + 1) * CHUNK)
    lanes = lambda h: slice(h * HEAD_DIM, (h + 1) * HEAD_DIM)
    wide = lambda ref: jnp.stack([ref[rows(c), lanes(h)] for c, h in idx])
    col = lambda ref: jnp.stack([ref[h, rows(c), :] for c, h in idx])
    return idx, (wide(q_ref), wide(k_ref), wide(v_ref), col(gc_ref), jnp.stack([gr_ref[h, c] for c, h in idx]),
                 col(bc_ref))


def _gdn_wy(qkv, gcol, grow, bcol, ng, cb):
    t = qkv.shape[0]
    nch = t // CHUNK

    def body(q_ref, k_ref, v_ref, gc_ref, gr_ref, bc_ref, u_ref, w_ref, qg_ref, kd_ref, at_ref, eg_ref):
        idx, args = _wy_batch(q_ref, k_ref, v_ref, gc_ref, gr_ref, bc_ref, ng, cb)
        u, w, qg, kd, at, eg = _wy_fn(*args)
        for b, (c, h) in enumerate(idx):
            rows, lanes = slice(c * CHUNK, (c + 1) * CHUNK), slice(h * HEAD_DIM, (h + 1) * HEAD_DIM)
            u_ref[rows, lanes] = u[b]
            w_ref[rows, lanes] = w[b]
            qg_ref[rows, lanes] = qg[b]
            kd_ref[rows, lanes] = kd[b]
            at_ref[h, c] = at[b]
            eg_ref[h, c] = eg[b]

    wd = ng * HEAD_DIM
    blk = lambda o: pl.BlockSpec((cb * CHUNK, wd), lambda i: (i, o))
    col = pl.BlockSpec((ng, cb * CHUNK, 1), lambda i: (0, i, 0))
    wide = jax.ShapeDtypeStruct((t, wd), F32)
    return pl.pallas_call(
        body, name="gdn_wy_fwd", grid=(nch // cb,),
        in_specs=[blk(0), blk(1), blk(2), col, pl.BlockSpec((ng, cb, 1, CHUNK), lambda i: (0, i, 0, 0)), col],
        out_specs=[blk(0), blk(0), blk(0), blk(0), pl.BlockSpec((ng, cb, CHUNK, CHUNK), lambda i: (0, i, 0, 0)),
                   pl.BlockSpec((ng, cb, 1, HEAD_DIM), lambda i: (0, i, 0, 0))],
        out_shape=[wide, wide, wide, wide, jax.ShapeDtypeStruct((ng, nch, CHUNK, CHUNK), F32),
                   jax.ShapeDtypeStruct((ng, nch, 1, HEAD_DIM), F32)],
        compiler_params=_cparams(("parallel",)),
    )(qkv, qkv, qkv, gcol, grow, bcol)


def _gdn_bwd(qkv, gcol, grow, bcol, saved, do, ng):
    t = qkv.shape[0]
    nch = t // CHUNK
    cb = GDN_CHUNKS_PER_STEP // 2
    wy, states = saved
    wd = ng * HEAD_DIM

    def scan_body(u_ref, w_ref, qg_ref, kd_ref, at_ref, eg_ref, st_ref, do_ref,
                  du_ref, dw_ref, dqg_ref, dkd_ref, dat_ref, deg_ref, dstate):
        @pl.when(pl.program_id(0) == 0)
        def _():
            dstate[...] = jnp.zeros_like(dstate)

        heads = lambda ref: jnp.stack([ref[:, h * HEAD_DIM:(h + 1) * HEAD_DIM] for h in range(ng)])
        _, vjp = jax.vjp(_scan_fn, heads(u_ref), heads(w_ref), heads(qg_ref), heads(kd_ref), at_ref[:, 0],
                         eg_ref[:, 0], st_ref[:, 0])
        du, dw, dqg, dkd, dat, deg, dst = vjp((heads(do_ref), dstate[...]))
        for h in range(ng):
            lanes = slice(h * HEAD_DIM, (h + 1) * HEAD_DIM)
            du_ref[:, lanes] = du[h]
            dw_ref[:, lanes] = dw[h]
            dqg_ref[:, lanes] = dqg[h]
            dkd_ref[:, lanes] = dkd[h]
        dat_ref[:, 0] = dat
        deg_ref[:, 0] = deg
        dstate[...] = dst

    rev = lambda i: nch - 1 - i
    blk = pl.BlockSpec((CHUNK, wd), lambda i: (rev(i), 0))
    atb = pl.BlockSpec((ng, 1, CHUNK, CHUNK), lambda i: (0, rev(i), 0, 0))
    egb = pl.BlockSpec((ng, 1, 1, HEAD_DIM), lambda i: (0, rev(i), 0, 0))
    wide = jax.ShapeDtypeStruct((t, wd), F32)
    at_shape = jax.ShapeDtypeStruct((ng, nch, CHUNK, CHUNK), F32)
    eg_shape = jax.ShapeDtypeStruct((ng, nch, 1, HEAD_DIM), F32)
    dwy = pl.pallas_call(
        scan_body, name="gdn_scan_bwd", grid=(nch,),
        in_specs=[blk, blk, blk, blk, atb, egb,
                  pl.BlockSpec((ng, 1, HEAD_DIM, HEAD_DIM), lambda i: (0, rev(i), 0, 0)), blk],
        out_specs=[blk, blk, blk, blk, atb, egb],
        out_shape=[wide, wide, wide, wide, at_shape, eg_shape],
        scratch_shapes=[pltpu.VMEM((ng, HEAD_DIM, HEAD_DIM), F32)],
        compiler_params=_cparams(("arbitrary",)),
    )(*wy, states, do)

    def wy_body(q_ref, k_ref, v_ref, gc_ref, gr_ref, bc_ref, du_ref, dw_ref, dqg_ref, dkd_ref, dat_ref, deg_ref,
                dq_ref, dk_ref, dv_ref, dgc_ref, dgr_ref, dbc_ref):
        idx, args = _wy_batch(q_ref, k_ref, v_ref, gc_ref, gr_ref, bc_ref, ng, cb)
        rows = lambda c: slice(c * CHUNK, (c + 1) * CHUNK)
        lanes = lambda h: slice(h * HEAD_DIM, (h + 1) * HEAD_DIM)
        wide_ct = lambda ref: jnp.stack([ref[rows(c), lanes(h)] for c, h in idx])
        cts = (wide_ct(du_ref), wide_ct(dw_ref), wide_ct(dqg_ref), wide_ct(dkd_ref),
               jnp.stack([dat_ref[h, c] for c, h in idx]), jnp.stack([deg_ref[h, c] for c, h in idx]))
        _, vjp = jax.vjp(_wy_fn, *args)
        dq, dk, dv, dgc, dgr, dbc = vjp(cts)
        for b, (c, h) in enumerate(idx):
            dq_ref[rows(c), lanes(h)] = dq[b]
            dk_ref[rows(c), lanes(h)] = dk[b]
            dv_ref[rows(c), lanes(h)] = dv[b]
            dgc_ref[h, rows(c), :] = dgc[b]
            dgr_ref[h, c] = dgr[b]
            dbc_ref[h, rows(c), :] = dbc[b]

    cblk = lambda o: pl.BlockSpec((cb * CHUNK, wd), lambda i: (i, o))
    col = pl.BlockSpec((ng, cb * CHUNK, 1), lambda i: (0, i, 0))
    rowv = pl.BlockSpec((ng, cb, 1, CHUNK), lambda i: (0, i, 0, 0))
    cshape = jax.ShapeDtypeStruct((ng, t, 1), F32)
    return pl.pallas_call(
        wy_body, name="gdn_wy_bwd", grid=(nch // cb,),
        in_specs=[cblk(0), cblk(1), cblk(2), col, rowv, col, cblk(0), cblk(0), cblk(0), cblk(0),
                  pl.BlockSpec((ng, cb, CHUNK, CHUNK), lambda i: (0, i, 0, 0)),
                  pl.BlockSpec((ng, cb, 1, HEAD_DIM), lambda i: (0, i, 0, 0))],
        out_specs=[cblk(0), cblk(0), cblk(0), col, rowv, col],
        out_shape=[wide, wide, wide, cshape, jax.ShapeDtypeStruct((ng, nch, 1, CHUNK), F32), cshape],
        compiler_params=_cparams(("parallel",)),
    )(qkv, qkv, qkv, gcol, grow, bcol, *dwy)


def _swiglu_fn(gate, up):
    return _silu(gate) * up


def _swiglu_specs(gu):
    _, t, w = gu.shape
    tr, tc = _tile(t, 512, 8), _tile(w, 1408)
    nc = w // tc
    pair = pl.BlockSpec((2, tr, tc), lambda j, r, c: (j, r, c))
    flat = pl.BlockSpec((tr, tc), lambda j, r, c: (r, j * nc + c))
    return (2, t // tr, nc), pair, flat


def _swiglu_fwd(gu):
    grid, pair, flat = _swiglu_specs(gu)

    def body(gu_ref, o_ref):
        o_ref[...] = _swiglu_fn(gu_ref[0], gu_ref[1]).astype(BF16)

    return pl.pallas_call(
        body, name="swiglu_fwd", grid=grid, in_specs=[pair], out_specs=flat,
        out_shape=jax.ShapeDtypeStruct((gu.shape[1], 2 * gu.shape[2]), BF16),
        compiler_params=_cparams(("parallel", "parallel", "parallel")),
    )(gu)


def _swiglu_bwd(gu, dact):
    grid, pair, flat = _swiglu_specs(gu)

    def body(gu_ref, d_ref, o_ref):
        _, vjp = jax.vjp(_swiglu_fn, gu_ref[0], gu_ref[1])
        dg, du = vjp(d_ref[...])
        o_ref[0] = dg.astype(BF16)
        o_ref[1] = du.astype(BF16)

    return pl.pallas_call(
        body, name="swiglu_bwd", grid=grid, in_specs=[pair, flat], out_specs=pair,
        out_shape=jax.ShapeDtypeStruct(gu.shape, BF16),
        compiler_params=_cparams(("parallel", "parallel", "parallel")),
    )(gu, dact)


def _loss_head(h2, target):
    t, d = h2.shape
    tr = _tile(t, 256, 8)

    def body(h_ref, t_ref, l_ref, d_ref):
        @pl.when(pl.program_id(0) == 0)
        def _():
            l_ref[...] = jnp.zeros_like(l_ref)

        err = h_ref[...] - t_ref[...]
        d_ref[...] = err * (1.0 / d)
        part = 0.5 * jnp.sum(jnp.mean(err * err, axis=-1, keepdims=True), axis=0, keepdims=True)
        lane = lax.broadcasted_iota(jnp.int32, (8, HEAD_DIM), 1)
        row = lax.broadcasted_iota(jnp.int32, (8, HEAD_DIM), 0)
        l_ref[...] += jnp.where((lane == 0) & (row == 0), part, 0.0)

    blk = pl.BlockSpec((tr, d), lambda r: (r, 0))
    return pl.pallas_call(
        body, name="loss_head", grid=(t // tr,), in_specs=[blk, blk],
        out_specs=[pl.BlockSpec((8, HEAD_DIM), lambda r: (0, 0)), blk],
        out_shape=[jax.ShapeDtypeStruct((8, HEAD_DIM), F32), jax.ShapeDtypeStruct((t, d), F32)],
        compiler_params=_cparams(("arbitrary",)),
    )(h2, target)


def _adamw(w, g, m, v, *, g2=None, name):
    r, c = w.shape
    tr = _tile(r, max(8, (1 << 19) // c // 8 * 8), 8)

    def body(*refs):
        if g2 is None:
            w_ref, g_ref, m_ref, v_ref, go_ref, d_ref, mo_ref, vo_ref = refs
            gr = g_ref[...]
        else:
            w_ref, g_ref, g2_ref, m_ref, v_ref, go_ref, d_ref, mo_ref, vo_ref = refs
            gr = g_ref[...] + g2_ref[...]
        mn = ADAM_B1 * m_ref[...] + (1.0 - ADAM_B1) * gr
        vn = ADAM_B2 * v_ref[...] + (1.0 - ADAM_B2) * (gr * gr)
        m_hat = mn / (1.0 - ADAM_B1 ** ADAM_STEP)
        v_hat = vn / (1.0 - ADAM_B2 ** ADAM_STEP)
        go_ref[...] = gr
        d_ref[...] = -ADAM_LR * (m_hat / (jnp.sqrt(v_hat) + ADAM_EPS) + ADAM_WD * w_ref[...])
        mo_ref[...] = mn
        vo_ref[...] = vn

    blk = pl.BlockSpec((tr, c), lambda i: (i, 0))
    n_in = 4 if g2 is None else 5
    ins = [w, g] + ([g2] if g2 is not None else []) + [m, v]
    return pl.pallas_call(
        body, name=name, grid=(r // tr,), in_specs=[blk] * n_in, out_specs=[blk] * 4,
        out_shape=[jax.ShapeDtypeStruct((r, c), F32)] * 4,
        compiler_params=_cparams(("parallel",)),
    )(*ins)


class _Layout:
    def __init__(self, d):
        nh = d // HEAD_DIM
        self.nm = N_MEM_HEADS
        self.nf = (nh - self.nm) // 2
        self.ng = nh - self.nm - self.nf
        nf, ng, nm = self.nf, self.ng, self.nm
        self.o_fq, self.o_fk, self.o_fv = 0, nf, 2 * nf
        self.o_gq = 3 * nf
        self.o_gz = 3 * nf + 3 * ng
        self.o_mq = 3 * nf + 4 * ng
        self.o_sm = self.o_mq + nm
        self.blocks = -(-(self.o_sm + 1) // 8) * 8
        self.cols = self.blocks * HEAD_DIM
        hd = HEAD_DIM
        sizes = [nf * hd, nf * hd, nf * hd, nf, 3 * ng * hd, ng * hd, ng, ng, nm * hd]
        starts = [sum(sizes[:i]) for i in range(len(sizes))]
        self.ref = list(zip(starts, sizes))
        self.in_cols = sum(sizes)

    def regroup(self, w):
        part = lambda i: w[:, self.ref[i][0]:self.ref[i][0] + self.ref[i][1]]
        pieces = [part(0), part(1), part(2), part(4), part(5), part(8), part(3), part(6), part(7)]
        pad = self.cols - self.in_cols
        return jnp.concatenate(pieces + [jnp.zeros((w.shape[0], pad), w.dtype)], axis=1)

    def ungroup(self, g):
        hd, nf, ng, nm = HEAD_DIM, self.nf, self.ng, self.nm
        sm = self.o_sm * hd
        return jnp.concatenate([
            g[:, :3 * nf * hd], g[:, sm:sm + nf], g[:, self.o_gq * hd:self.o_gz * hd],
            g[:, self.o_gz * hd:self.o_mq * hd], g[:, sm + nf:sm + nf + ng], g[:, sm + nf + ng:sm + nf + 2 * ng],
            g[:, self.o_mq * hd:self.o_sm * hd]], axis=1)


def _lane_row(pieces):
    row = jnp.zeros((1, HEAD_DIM), F32)
    for off, a in pieces:
        row = lax.dynamic_update_slice(row, a.astype(F32), (0, off))
    return row


def _local_step(x, mem, target, win, wmkv, late_weights, reduce_start, sp):
    t, d = x.shape
    lay = _Layout(d)
    nf, ng, nm, hd = lay.nf, lay.ng, lay.nm, HEAD_DIM
    nch = t // CHUNK
    tq = _tile(t, 256)
    tk = tq

    u = _norm_fwd(x, 0, sp["norm_mix"], 1, d, BF16, name="norm_mix_fwd")
    p = _mm(u, win, name="mm_in")
    pa = _lane_row([(nf, sp["gdn_a_log"])])
    pb = _lane_row([(0, sp["fox_f_bias"]), (nf, sp["gdn_dt_bias"])])
    vals, csum = _small_fwd(p, lay.o_sm, pa, pb, nf, ng)

    c_t = csum[:, :nf].T
    cc, cr = c_t.reshape(nf, t, 1), c_t.reshape(nf, t // tk, 1, tk)
    fq = _norm_fwd(p, lay.o_fq, sp["fox_q_norm"], nf, hd, BF16, name="fox_qnorm_fwd")
    fk = _norm_fwd(p, lay.o_fk, sp["fox_k_norm"], nf, hd, BF16, name="fox_knorm_fwd")
    fv = p[:, lay.o_fv * hd:(lay.o_fv + nf) * hd].astype(BF16)
    o_fox, lse = _fox_fwd(fq, fk, fv, cc, cr, nf, tq, tk)

    qkv = _conv_fwd(p, lay.o_gq, sp["gdn_conv"], ng)
    g_t, b_t = vals[:, nf:nf + ng].T, vals[:, nf + ng:nf + 2 * ng].T
    gcol, grow, bcol = g_t.reshape(ng, t, 1), g_t.reshape(ng, nch, 1, CHUNK), b_t.reshape(ng, t, 1)
    o_g, states = _gdn_fwd(qkv, gcol, grow, bcol, ng)
    o_gdn = _norm_fwd(o_g, 0, sp["gdn_out_norm"], ng, hd, BF16, z=p, zoff=lay.o_gz, name="gdn_out_fwd")

    mem_n = _norm_fwd(mem, 0, sp["mem_norm"], 1, d, BF16, name="mem_norm_fwd")
    mkv = _mm(mem_n, wmkv, name="mm_memkv")
    o_mem = _mem_fwd(p, lay.o_mq, mkv, sp["mem_q_norm"], sp["mem_k_norm"], tq)

    mix = jnp.concatenate([o_fox.astype(BF16), o_gdn, o_mem.astype(BF16)], axis=1)
    wout, wgu, wd = late_weights(mix)
    h1 = _mm(mix, wout, res=x, name="mm_out")
    n2 = _norm_fwd(h1, 0, sp["norm_ffn"], 1, d, BF16, name="norm_ffn_fwd")
    wgu4 = wgu.reshape(4, d, -1)
    gu = _mm(n2, wgu4, stack="out", name="mm_gate_up")
    act = _swiglu_fwd(gu)
    h2 = _mm(act, wd, res=h1, name="mm_down")
    loss_blk, dh2 = _loss_head(h2, target)

    g = {}
    dw_down = _mm(act, dh2, ta=True, out_dtype=BF16, name="mm_dw_down")
    dact = _mm(dh2, wd, tb=True, name="mm_dact")
    dgu = _swiglu_bwd(gu, dact)
    dw_gate_up = _mm(n2, dgu, ta=True, stack="out", out_dtype=BF16, name="mm_dw_gate_up").reshape(wgu.shape)
    token = reduce_start("ffn", {"w_down": dw_down, "w_gate_up": dw_gate_up})
    dn2 = _mm(dgu, wgu4, tb=True, stack="sum", name="mm_dn2")
    dh1, g["norm_ffn"] = _norm_bwd(h1, 0, sp["norm_ffn"] + token[0, 0], dn2, 0, 1, d, res=dh2,
                                   name="norm_ffn_bwd")
    dw_out = _mm(mix, dh1, ta=True, out_dtype=BF16, name="mm_dw_out")
    dmix = _mm(dh1, wout, tb=True, name="mm_dmix")

    dmq, dmk, dmv, g["mem_q_norm"], g["mem_k_norm"] = _mem_bwd(
        p, lay.o_mq, mkv, sp["mem_q_norm"], sp["mem_k_norm"], dmix, nf + ng, tq)
    dmkv = jnp.concatenate([dmk, dmv], axis=1)
    dw_mem_kv = _mm(mem_n, dmkv, ta=True, out_dtype=BF16, name="mm_dw_memkv")
    token = reduce_start("mix", {"w_out": dw_out, "w_mem_kv": dw_mem_kv})
    dmem_n = _mm(dmkv, wmkv, tb=True, name="mm_dmem")
    _, g["mem_norm"] = _norm_bwd(mem, 0, sp["mem_norm"], dmem_n, 0, 1, d, name="mem_norm_bwd")

    do_g, dgz, g["gdn_out_norm"] = _norm_bwd(o_g, 0, sp["gdn_out_norm"] + token[0, 0], dmix, nf, ng, hd, z=p,
                                             zoff=lay.o_gz, name="gdn_out_bwd")
    dq, dk, dv, dgc, dgr, dbc = _gdn_bwd(qkv, gcol, grow, bcol, states, do_g, ng)
    dgqkv, g["gdn_conv"] = _conv_bwd(p, lay.o_gq, sp["gdn_conv"], jnp.concatenate([dq, dk, dv], axis=1), ng)
    dg_t = dgc.reshape(ng, t) + dgr.reshape(ng, t)
    db_t = dbc.reshape(ng, t)

    dfq_n, dfk_n, dfv, dcc, dcr = _fox_bwd(fq, fk, fv, cc, cr, o_fox, lse, dmix, nf, tq, tk)
    dfq, g["fox_q_norm"] = _norm_bwd(p, lay.o_fq, sp["fox_q_norm"], dfq_n, 0, nf, hd, name="fox_qnorm_bwd")
    dfk, g["fox_k_norm"] = _norm_bwd(p, lay.o_fk, sp["fox_k_norm"], dfk_n, 0, nf, hd, name="fox_knorm_bwd")
    dc_t = dcc.reshape(nf, t) + dcr.reshape(nf, t)

    lanes_left = hd - nf - 2 * ng
    dvals = jnp.concatenate([jnp.zeros((t, nf), F32), dg_t.T, db_t.T, jnp.zeros((t, lanes_left), F32)], axis=1)
    dcsum = jnp.concatenate([dc_t.T, jnp.zeros((t, hd - nf), F32)], axis=1)
    dsm, dpa, dpb = _small_bwd(p, lay.o_sm, pa, pb, dvals, dcsum, nf, ng)
    g["fox_f_bias"] = dpb[:, :nf]
    g["gdn_dt_bias"] = dpb[:, nf:nf + ng]
    g["gdn_a_log"] = dpa[:, nf:nf + ng]

    pad = jnp.zeros((t, lay.cols - (lay.o_sm + 1) * hd), F32)
    dp = jnp.concatenate([dfq, dfk, dfv, dgqkv, dgz, dmq, dsm, pad], axis=1)
    token = reduce_start("in", {"w_in": _mm(u, dp, ta=True, out_dtype=BF16, name="mm_dw_in")})
    du = _mm(dp, win, tb=True, name="mm_du")
    dx, g["norm_mix"] = _norm_bwd(x, 0, sp["norm_mix"] + token[0, 0], du, 0, 1, d, res=dh1, name="norm_mix_bwd")
    return loss_blk, dx, g


ANY = pl.BlockSpec(memory_space=pl.ANY)


def _me():
    x, y, c = lax.axis_index("x"), lax.axis_index("y"), lax.axis_index("c")
    chips = [(1 - x, y), (x, 1 - y), (1 - x, 1 - y)]
    return x, y, c, chips


def _slot(axis, k):
    return k if axis == 0 else 2 * (k % 2) + k // 2


def _slab(ref, axis, rows, cols, k, h):
    half = rows // 2
    return ref.at[pl.ds(_slot(axis, k) * rows + h * half, half), :]


def _remote(src, dst, send_sem, recv_sem, dev):
    return pltpu.make_async_remote_copy(src_ref=src, dst_ref=dst, send_sem=send_sem, recv_sem=recv_sem,
                                        device_id=dev, device_id_type=MESH)


HBM = pl.BlockSpec(memory_space=pltpu.HBM)
SEM = pl.BlockSpec(memory_space=pltpu.SEMAPHORE)
SPLIT = pltpu.CompilerParams(has_side_effects=pltpu.SideEffectType.DATAFLOW_SIDE_EFFECTING)
TOKEN = jax.ShapeDtypeStruct((8, HEAD_DIM), F32)


def _in_hbm(v):
    return pltpu.with_memory_space_constraint(v, pltpu.HBM)


def _gather_start(shards, axes, groups):
    n = len(shards)
    shapes = [s.shape for s in shards]
    chip = 2 * lax.axis_index("x") + lax.axis_index("y")
    bufs = []
    for s, a in zip(shards, axes):
        r, cl = s.shape
        bufs.append(lax.dynamic_update_slice(lax.empty((4 * r, cl), s.dtype), s, (_slot(a, chip) * r, 0)))

    def body(*refs):
        dst = refs[n:2 * n]
        sems = refs[2 * n:2 * n + 2 * len(groups)]
        token = refs[-1]
        x, y, c, chips = _me()
        k = 2 * x + y
        for gi, ws in enumerate(groups):
            for i, w in enumerate(ws):
                r, cl = shapes[w]
                place = _slab(dst[w], axes[w], r, cl, k, c)
                for j, (px, py) in enumerate(chips):
                    _remote(place, place, sems[2 * gi].at[3 * i + j], sems[2 * gi + 1].at[3 * i + j],
                            (px, py, c)).start()
        token[...] = jnp.zeros_like(token)

    sem_shapes = [pltpu.SemaphoreType.DMA((3 * len(ws),)) for ws in groups for _ in range(2)]
    outs = pl.pallas_call(
        body, name="gather_ici_start", in_specs=[HBM] * n,
        out_specs=[HBM] * n + [SEM] * len(sem_shapes) + [pl.BlockSpec(memory_space=pltpu.VMEM)],
        out_shape=[pltpu.HBM(b.shape, b.dtype) for b in bufs] + sem_shapes + [TOKEN],
        input_output_aliases={w: w for w in range(n)}, compiler_params=SPLIT,
    )(*[_in_hbm(b) for b in bufs])
    sems = outs[n:-1]
    return outs[:n], [(sems[2 * g], sems[2 * g + 1]) for g in range(len(groups))], outs[-1]


def _gather_wait(bufs, axes, shapes, sems, after, name):
    n = len(bufs)

    def body(*refs):
        send_sems, recv_sems = refs[n], refs[n + 1]
        dst = refs[n + 3:]
        x, y, c, chips = _me()
        k = 2 * x + y
        for i in range(n):
            r, cl = shapes[i]
            for j, (px, py) in enumerate(chips):
                got = _slab(dst[i], axes[i], r, cl, 2 * px + py, c)
                _remote(got, got, send_sems.at[3 * i + j], recv_sems.at[3 * i + j], (px, py, c)).wait_recv()
        for i in range(n):
            r, cl = shapes[i]
            mine = _slab(dst[i], axes[i], r, cl, k, c)
            for j, (px, py) in enumerate(chips):
                _remote(mine, mine, send_sems.at[3 * i + j], recv_sems.at[3 * i + j], (px, py, c)).wait_send()

    return pl.pallas_call(
        body, name=name, in_specs=[HBM] * n + [SEM, SEM, ANY], out_specs=[HBM] * n,
        out_shape=[pltpu.HBM(b.shape, b.dtype) for b in bufs],
        input_output_aliases={i: i for i in range(n)}, compiler_params=SPLIT,
    )(*bufs, sems[0], sems[1], after)


def _gather_forward(bufs, axes, shapes, name):
    n = len(bufs)

    def body(*refs):
        dst = refs[n:2 * n]
        send_sems, recv_sems = refs[2 * n:]
        x, y, c, chips = _me()
        sibling = (x, y, 1 - c)
        sends = []
        for i in range(n):
            r, cl = shapes[i]
            for j, (px, py) in enumerate(chips):
                got = _slab(dst[i], axes[i], r, cl, 2 * px + py, c)
                cp = _remote(got, got, send_sems.at[3 * i + j], recv_sems.at[3 * i + j], sibling)
                cp.start()
                sends.append(cp)
        for i in range(n):
            r, cl = shapes[i]
            for j, (px, py) in enumerate(chips):
                got = _slab(dst[i], axes[i], r, cl, 2 * px + py, 1 - c)
                _remote(got, got, send_sems.at[3 * i + j], recv_sems.at[3 * i + j], sibling).wait_recv()
        for cp in sends:
            cp.wait_send()

    return pl.pallas_call(
        body, name=name, in_specs=[ANY] * n, out_specs=[ANY] * n,
        out_shape=[jax.ShapeDtypeStruct(b.shape, b.dtype) for b in bufs],
        input_output_aliases={i: i for i in range(n)},
        scratch_shapes=[pltpu.SemaphoreType.DMA((3 * n,)), pltpu.SemaphoreType.DMA((3 * n,))],
    )(*bufs)


def _pair_exchange(fulls, axes, shapes, tag):
    n = len(fulls)

    def body(*refs):
        src, dst = refs[:n], refs[n:2 * n]
        send_sems, recv_sems = refs[2 * n:]
        x, y, c, _ = _me()
        sibling = (x, y, 1 - c)
        cps = []
        for w in range(n):
            r, cl = shapes[w]
            for j in range(4):
                cp = _remote(_slab(src[w], axes[w], r, cl, j, 1 - c), dst[w].at[j],
                             send_sems.at[4 * w + j], recv_sems.at[4 * w + j], sibling)
                cp.start()
                cps.append(cp)
        for cp in cps:
            cp.wait()

    out_shape = [jax.ShapeDtypeStruct((4, r // 2, cl), f.dtype) for (r, cl), f in zip(shapes, fulls)]
    return pl.pallas_call(
        body, name="reduce_pair_exchange_" + tag, in_specs=[ANY] * n, out_specs=[ANY] * n, out_shape=out_shape,
        scratch_shapes=[pltpu.SemaphoreType.DMA((4 * n,)), pltpu.SemaphoreType.DMA((4 * n,))],
    )(*fulls)


def _chip_start(parts, tag):
    n = len(parts)

    def body(*refs):
        src, land = refs[2 * n:3 * n], refs[3 * n:4 * n]
        send_sems, recv_sems, token = refs[4 * n:]
        x, y, c, chips = _me()
        k = 2 * x + y
        for w in range(n):
            for j, (px, py) in enumerate(chips):
                _remote(src[w].at[2 * px + py], land[w].at[k], send_sems.at[3 * w + j], recv_sems.at[3 * w + j],
                        (px, py, c)).start()
        token[...] = jnp.zeros_like(token)

    lands = [lax.empty(p.shape, p.dtype) for p in parts]
    sem = pltpu.SemaphoreType.DMA((3 * n,))
    outs = pl.pallas_call(
        body, name="reduce_ici_start_" + tag, in_specs=[HBM] * (2 * n),
        out_specs=[HBM] * (2 * n) + [SEM, SEM, pl.BlockSpec(memory_space=pltpu.VMEM)],
        out_shape=[pltpu.HBM(p.shape, p.dtype) for p in parts + lands] + [sem, sem, TOKEN],
        input_output_aliases={i: i for i in range(2 * n)}, compiler_params=SPLIT,
    )(*[_in_hbm(v) for v in parts + lands])
    return outs[:n], outs[n:2 * n], outs[2 * n], outs[2 * n + 1], outs[-1]


def _chip_wait(parts, lands, send_sems, recv_sems, after, tag):
    n = len(parts)

    def body(*refs):
        send, recv = refs[2 * n], refs[2 * n + 1]
        src, land = refs[2 * n + 3:3 * n + 3], refs[3 * n + 3:]
        x, y, c, chips = _me()
        for w in range(n):
            for j, (px, py) in enumerate(chips):
                got = land[w].at[2 * px + py]
                _remote(got, got, send.at[3 * w + j], recv.at[3 * w + j], (px, py, c)).wait_recv()
        for w in range(n):
            for j, (px, py) in enumerate(chips):
                sent = src[w].at[2 * px + py]
                _remote(sent, sent, send.at[3 * w + j], recv.at[3 * w + j], (px, py, c)).wait_send()

    outs = pl.pallas_call(
        body, name="reduce_ici_wait_" + tag, in_specs=[HBM] * (2 * n) + [SEM, SEM, ANY], out_specs=[HBM] * (2 * n),
        out_shape=[pltpu.HBM(p.shape, p.dtype) for p in parts + lands],
        input_output_aliases={i: i for i in range(2 * n)}, compiler_params=SPLIT,
    )(*parts, *lands, send_sems, recv_sems, after)
    chip = 2 * lax.axis_index("x") + lax.axis_index("y")
    return [lax.dynamic_update_slice(s, lax.dynamic_index_in_dim(p, chip, 0, keepdims=True), (chip, 0, 0))
            for p, s in zip(outs[:n], outs[n:])]


def _half_swap(halves, tag):
    n = len(halves)
    core = lax.axis_index("c")
    bufs = [lax.dynamic_update_slice(lax.empty((2,) + h.shape, h.dtype), h[None], (core, 0, 0)) for h in halves]

    def body(*refs):
        dst = refs[n:2 * n]
        send_sems, recv_sems = refs[2 * n:]
        x, y, c, _ = _me()
        sibling = (x, y, 1 - c)
        cps = []
        for w in range(n):
            cp = _remote(dst[w].at[c], dst[w].at[c], send_sems.at[w], recv_sems.at[w], sibling)
            cp.start()
            cps.append(cp)
        for w in range(n):
            other = dst[w].at[1 - c]
            _remote(other, other, send_sems.at[w], recv_sems.at[w], sibling).wait_recv()
        for cp in cps:
            cp.wait_send()

    outs = pl.pallas_call(
        body, name="reduce_half_swap_" + tag, in_specs=[ANY] * n, out_specs=[ANY] * n,
        out_shape=[jax.ShapeDtypeStruct(b.shape, b.dtype) for b in bufs],
        input_output_aliases={w: w for w in range(n)},
        scratch_shapes=[pltpu.SemaphoreType.DMA((n,)), pltpu.SemaphoreType.DMA((n,))],
    )(*bufs)
    return [o.reshape(2 * o.shape[1], o.shape[2]) for o in outs]


def _add_parts(a, b, name):
    _, r, c = a.shape
    tr, tc = _tile(r, 256, 8), _tile(c, 2048)

    def body(a_ref, b_ref, o_ref):
        o_ref[...] = (a_ref[...].astype(F32) + b_ref[...].astype(F32)).astype(BF16)

    blk = pl.BlockSpec((1, tr, tc), lambda j, i, l: (j, i, l))
    return pl.pallas_call(
        body, name=name, grid=(4, r // tr, c // tc), in_specs=[blk, blk], out_specs=blk,
        out_shape=jax.ShapeDtypeStruct(a.shape, BF16),
        compiler_params=_cparams(("parallel", "parallel", "parallel")),
    )(a, b)


def _sum_slots(a, name):
    _, r, c = a.shape
    tr, tc = _tile(r, 256, 8), _tile(c, 2048)

    def body(a_ref, o_ref):
        v = a_ref[...].astype(F32)
        o_ref[...] = ((v[0] + v[1]) + v[2]) + v[3]

    return pl.pallas_call(
        body, name=name, grid=(r // tr, c // tc),
        in_specs=[pl.BlockSpec((4, tr, tc), lambda i, l: (0, i, l))],
        out_specs=pl.BlockSpec((tr, tc), lambda i, l: (i, l)),
        out_shape=jax.ShapeDtypeStruct((r, c), F32),
        compiler_params=_cparams(("parallel", "parallel")),
    )(a)


def _half_of(full, axis, rows, cols, c):
    half = rows // 2
    v = lax.dynamic_index_in_dim(full.reshape(4, 2, half, cols), c, 1, keepdims=False)
    if axis == 0:
        return v
    return jnp.stack([v[_slot(axis, k)] for k in range(4)])


class _Reducer:
    def __init__(self):
        self.pending = []

    def start(self, tag, names, fulls, axes, shapes):
        c = lax.axis_index("c")
        from_sibling = _pair_exchange(fulls, axes, shapes, tag)
        parts = [_add_parts(_half_of(f, a, r, cl, c), s, name=f"reduce_add_{n}")
                 for n, f, a, (r, cl), s in zip(names, fulls, axes, shapes, from_sibling)]
        parts, lands, send, recv, token = _chip_start(parts, tag)
        self.pending.append((tag, names, parts, lands, send, recv))
        return token

    def finish(self, after):
        out = {}
        for tag, names, parts, lands, send, recv in self.pending:
            slots = _chip_wait(parts, lands, send, recv, after, tag)
            halves = [_sum_slots(s, name=f"reduce_sum_{n}") for n, s in zip(names, slots)]
            out.update(zip(names, _half_swap(halves, tag)))
        return out


def _allreduce_small(pack):
    rows = pack.shape[0]

    def body(p_ref, o_ref, slots, send_sems, recv_sems):
        x, y, c, _ = _me()
        me = 4 * x + 2 * y + c
        slots[me] = p_ref[...]
        cps = []
        for r in range(1, 8):
            peer = (x ^ (r >> 2), y ^ ((r >> 1) & 1), c ^ (r & 1))
            cp = _remote(p_ref, slots.at[me], send_sems.at[r - 1], recv_sems.at[r - 1], peer)
            cp.start()
            cps.append(cp)
        for r in range(1, 8):
            frm = me ^ r
            _remote(slots.at[frm], slots.at[frm], send_sems.at[r - 1], recv_sems.at[r - 1], (x, y, c)).wait_recv()
        for cp in cps:
            cp.wait_send()
        acc = slots[0]
        for s in range(1, 8):
            acc = acc + slots[s]
        o_ref[...] = acc

    vm = pl.BlockSpec(memory_space=pltpu.VMEM)
    return pl.pallas_call(
        body, name="allreduce_small", in_specs=[vm], out_specs=vm,
        out_shape=jax.ShapeDtypeStruct(pack.shape, F32),
        scratch_shapes=[pltpu.VMEM((8, rows, HEAD_DIM), F32), pltpu.SemaphoreType.DMA((7,)),
                        pltpu.SemaphoreType.DMA((7,))],
    )(pack)


_ROWS = ["norm_mix", "norm_ffn", "mem_norm", "fox_q_norm", "fox_k_norm", "gdn_out_norm", "mem_q_norm",
         "mem_k_norm", "fox_f_bias", "gdn_a_log", "gdn_dt_bias"]


def _pack_rows(vals):
    out = []
    for name in _ROWS:
        v = vals[name].reshape(-1)
        n = -(-v.shape[0] // HEAD_DIM) * HEAD_DIM
        out.append(jnp.pad(v, (0, n - v.shape[0])).reshape(-1, HEAD_DIM))
    return jnp.concatenate(out, axis=0)


def _unpack_rows(pack, like):
    out, r = {}, 0
    for name in _ROWS:
        n = like[name].shape[-1]
        nr = -(-n // HEAD_DIM)
        out[name] = pack[r:r + nr].reshape(1, -1)[:, :n]
        r += nr
    return out, r


def kernel(x, mem, norm_mix, w_in, fox_f_bias, fox_q_norm, fox_k_norm, gdn_conv, gdn_a_log, gdn_dt_bias, gdn_out_norm, mem_norm, w_mem_kv, mem_q_norm, mem_k_norm, w_out, norm_ffn, w_gate_up, w_down, loss_target, m_norm_mix, m_w_in, m_fox_f_bias, m_fox_q_norm, m_fox_k_norm, m_gdn_conv, m_gdn_a_log, m_gdn_dt_bias, m_gdn_out_norm, m_mem_norm, m_w_mem_kv, m_mem_q_norm, m_mem_k_norm, m_w_out, m_norm_ffn, m_w_gate_up, m_w_down, v_norm_mix, v_w_in, v_fox_f_bias, v_fox_q_norm, v_fox_k_norm, v_gdn_conv, v_gdn_a_log, v_gdn_dt_bias, v_gdn_out_norm, v_mem_norm, v_w_mem_kv, v_mem_q_norm, v_mem_k_norm, v_w_out, v_norm_ffn, v_w_gate_up, v_w_down):
    a = dict(locals())
    d = x.shape[-1]
    lay = _Layout(d)
    chip = 2 * lax.axis_index("x") + lax.axis_index("y")
    small = {n: a[n] for n in _ROWS}
    big = ["w_in", "w_mem_kv", "w_out", "w_gate_up", "w_down"]
    axes = [0, 0, 0, 1, 0]

    conv_cols = gdn_conv.shape[-1]
    conv_n = CONV_WIDTH * conv_cols
    conv_rows = -(-conv_n // HEAD_DIM)
    conv_blk = jnp.pad(gdn_conv.reshape(-1), (0, 32 * HEAD_DIM - conv_n)).reshape(32, HEAD_DIM)
    shards = [lay.regroup(w_in[0]).astype(BF16), w_mem_kv[0].astype(BF16), w_out[0].astype(BF16),
              w_gate_up[0].astype(BF16), w_down[0].astype(BF16)]
    all_shards, all_axes = shards + [conv_blk], axes + [0]
    all_shapes = [s.shape for s in all_shards]
    early, late = [0, 1, 5], [2, 3, 4]
    bufs, sems, token = _gather_start(all_shards, all_axes, [early, late])
    pick = lambda seq, idx: [seq[i] for i in idx]

    def arrive(idx, sem_pair, after, tag):
        got = _gather_wait(pick(bufs, idx), pick(all_axes, idx), pick(all_shapes, idx), sem_pair, after,
                           "gather_ici_wait_" + tag)
        return _gather_forward(got, pick(all_axes, idx), pick(all_shapes, idx), "gather_forward_" + tag)

    win, wmkv, conv_all = arrive(early, sems[0], token, "early")
    conv_full = conv_all.reshape(4, 32 * HEAD_DIM)[:, :conv_n].reshape(4, CONV_WIDTH, conv_cols)
    conv_full = jnp.transpose(conv_full, (1, 0, 2)).reshape(CONV_WIDTH, 4 * conv_cols)

    sp = dict(small)
    sp["gdn_conv"] = conv_full
    reducer = _Reducer()
    spec = {n: (ax, s.shape) for n, ax, s in zip(big, axes, shards)}

    def reduce_start(tag, grads):
        names = list(grads)
        return reducer.start(tag, names, [grads[n] for n in names], [spec[n][0] for n in names],
                             [spec[n][1] for n in names])

    loss_blk, dx, g = _local_step(x[0], mem[0], loss_target[0], win, wmkv,
                                  lambda after: arrive(late, sems[1], after, "late"), reduce_start, sp)

    gsmall = {n: g[n] for n in _ROWS}
    pack = jnp.concatenate([_pack_rows(gsmall), g["gdn_conv"].reshape(-1, HEAD_DIM), loss_blk], axis=0)
    pack = jnp.pad(pack, ((0, -pack.shape[0] % 8), (0, 0)))
    tot = _allreduce_small(pack)
    gs, r0 = _unpack_rows(tot, small)
    conv_g = tot[r0:r0 + CONV_WIDTH * 4 * conv_cols // HEAD_DIM].reshape(CONV_WIDTH, 4 * conv_cols)
    gs_conv = lax.dynamic_slice_in_dim(conv_g, chip * conv_cols, conv_cols, axis=1)
    loss = tot[r0 + CONV_WIDTH * 4 * conv_cols // HEAD_DIM, 0]
    reduced = reducer.finish(tot)
    reduced["w_in"] = lay.ungroup(reduced["w_in"])

    out = {"loss": loss, "grad_x": dx[None]}
    for n, gsh in reduced.items():
        res = _adamw(a[n][0], gsh, a["m_" + n][0], a["v_" + n][0], name="adamw_" + n)
        for pre, r in zip(["grad_", "delta_", "new_m_", "new_v_"], res):
            out[pre + n] = r[None]
    conv_pad = lambda v: jnp.pad(v.reshape(-1), (0, conv_rows * HEAD_DIM - conv_n)).reshape(conv_rows, HEAD_DIM)
    packs = []
    for src, cv in [(small, gdn_conv), (gs, gs_conv), ({n: a["m_" + n] for n in _ROWS}, m_gdn_conv),
                    ({n: a["v_" + n] for n in _ROWS}, v_gdn_conv)]:
        packs.append(jnp.concatenate([_pack_rows(src), conv_pad(cv)], axis=0))
    res = _adamw(*packs, name="adamw_small")
    for pre, r in zip(["grad_", "delta_", "new_m_", "new_v_"], res):
        vals, r1 = _unpack_rows(r, small)
        for n in _ROWS:
            out[pre + n] = vals[n]
        out[pre + "gdn_conv"] = r[r1:r1 + conv_rows].reshape(-1)[:conv_n].reshape(gdn_conv.shape)
    names = ["norm_mix", "w_in", "fox_f_bias", "fox_q_norm", "fox_k_norm", "gdn_conv", "gdn_a_log", "gdn_dt_bias",
             "gdn_out_norm", "mem_norm", "w_mem_kv", "mem_q_norm", "mem_k_norm", "w_out", "norm_ffn", "w_gate_up",
             "w_down"]
    return (out["loss"], out["grad_x"], *[out[p + n] for p in ["grad_", "delta_", "new_m_", "new_v_"] for n in names])
```

```python
import functools
import math

import jax
import jax.numpy as jnp
from jax import lax
from jax.experimental import pallas as pl
from jax.experimental.pallas import tpu as pltpu

F32, BF16 = jnp.float32, jnp.bfloat16
HEAD_DIM = 128
CHUNK = 64
N_MEM_HEADS = 4
CONV_WIDTH = 4
NORM_EPS = 1e-6
ADAM_LR, ADAM_B1, ADAM_B2, ADAM_EPS, ADAM_WD, ADAM_STEP = 0.001, 0.9, 0.999, 1e-08, 0.01, 10
VMEM_LIMIT = 48 * 1024 * 1024
NEG = -1e30
MESH = pl.DeviceIdType.MESH


def _cparams(sem=None, **kw):
    if sem is not None:
        kw["dimension_semantics"] = sem
    return pltpu.CompilerParams(vmem_limit_bytes=VMEM_LIMIT, **kw)


def _tile(n, target, mult=128):
    best = None
    d = mult
    while d <= min(n, target):
        if n % d == 0:
            best = d
        d += mult
    return best if best is not None else n


def _dot(a, b, dims, hi):
    if a.ndim == 3:
        dn = (((dims[0][0] + 1,), (dims[1][0] + 1,)), ((0,), (0,)))
    else:
        dn = (dims, ((), ()))
    if hi is not None:
        return lax.dot_general(a, b, dn, precision=hi, preferred_element_type=F32)
    return lax.dot_general(a.astype(BF16), b.astype(BF16), dn, preferred_element_type=F32)


def _make_dots(hi):
    @jax.custom_vjp
    def nn(a, b):
        return _dot(a, b, ((1,), (0,)), hi)

    @jax.custom_vjp
    def nt(a, b):
        return _dot(a, b, ((1,), (1,)), hi)

    @jax.custom_vjp
    def tn(a, b):
        return _dot(a, b, ((0,), (0,)), hi)

    nn.defvjp(lambda a, b: (nn(a, b), (a, b)), lambda r, g: (nt(g, r[1]), tn(r[0], g)))
    nt.defvjp(lambda a, b: (nt(a, b), (a, b)), lambda r, g: (nn(g, r[1]), tn(g, r[0])))
    tn.defvjp(lambda a, b: (tn(a, b), (a, b)), lambda r, g: (nt(r[1], g), nn(r[0], g)))
    return nn, nt, tn


_nn, _nt, _tn = _make_dots(None)
_nn_hi, _nt_hi, _tn_hi = _make_dots(lax.Precision.HIGHEST)
_nn_x3, _nt_x3, _tn_x3 = _make_dots(lax.Precision.HIGH)


def _sigmoid(x):
    return 1.0 / (1.0 + jnp.exp(-x))


@jax.custom_vjp
def _softplus(x):
    return jnp.maximum(x, 0.0) + jnp.log(1.0 + jnp.exp(-jnp.abs(x)))


_softplus.defvjp(lambda x: (_softplus(x), x), lambda x, g: (g * _sigmoid(x),))


def _silu(x):
    return x * _sigmoid(x)


def _rms_fn(x, gain, z=None):
    y = x * lax.rsqrt(jnp.mean(x * x, axis=-1, keepdims=True) + NORM_EPS) * gain
    if z is not None:
        y = y * _silu(z)
    return y


def _mm(a, b, *, ta=False, tb=False, out_dtype=F32, res=None, stack=None, after=None, name):
    a2, b2 = a.shape[-2:], b.shape[-2:]
    ns = b.shape[0] if stack else 1
    m = a2[1] if ta else a2[0]
    k = a2[0] if ta else a2[1]
    n = b2[0] if tb else b2[1]
    assert k == (b2[1] if tb else b2[0])
    tm, tn, tk = _mm_tiles(m, n, k, ns if stack == "sum" else 1, a.dtype.itemsize, b.dtype.itemsize,
                           jnp.dtype(out_dtype).itemsize, res is not None)
    nk = k // tk
    single = nk == 1 and stack != "sum"
    dims = ((0 if ta else 1,), (1 if tb else 0,))
    if stack == "sum":
        order = lambda g0, g1, g2, g3: (g2, g0, g1, g3)
        grid = (m // tm, n // tn, ns, nk)
    else:
        order = lambda g0, g1, g2, g3: (g0, g1, g2, g3)
        grid = (ns, m // tm, n // tn, nk)

    def body(*refs):
        if after is not None:
            refs = refs[:2 + (res is not None)] + refs[3 + (res is not None):]
        if single:
            a_ref, b_ref = refs[:2]
            r = lax.dot_general(a_ref[...].astype(BF16), b_ref[...].astype(BF16), (dims, ((), ())),
                                preferred_element_type=F32)
            if res is not None:
                r = r + refs[2][...]
            refs[-1][...] = r.astype(out_dtype)
            return
        if res is None:
            a_ref, b_ref, o_ref, acc = refs
        else:
            a_ref, b_ref, r_ref, o_ref, acc = refs
        s, _, _, kk = order(*[pl.program_id(d) for d in range(4)])
        first = kk == 0
        last = kk == nk - 1
        if stack == "sum":
            first, last = first & (s == 0), last & (s == ns - 1)

        @pl.when(first)
        def _():
            acc[...] = jnp.zeros_like(acc)

        acc[...] += lax.dot_general(a_ref[...].astype(BF16), b_ref[...].astype(BF16), (dims, ((), ())),
                                    preferred_element_type=F32)

        @pl.when(last)
        def _():
            r = acc[...]
            if res is not None:
                r = r + r_ref[...]
            o_ref[...] = r.astype(out_dtype)

    def spec(shape, idx, stacked):
        if stacked:
            return pl.BlockSpec((None,) + shape, lambda *g: (order(*g)[0],) + idx(*order(*g)))
        return pl.BlockSpec(shape, lambda *g: idx(*order(*g)))

    a_spec = (spec((tk, tm), lambda s, i, j, kk: (kk, i), stack == "sum") if ta
              else spec((tm, tk), lambda s, i, j, kk: (i, kk), stack == "sum"))
    b_spec = (spec((tn, tk), lambda s, i, j, kk: (j, kk), bool(stack)) if tb
              else spec((tk, tn), lambda s, i, j, kk: (kk, j), bool(stack)))
    o_spec = spec((tm, tn), lambda s, i, j, kk: (i, j), stack == "out")
    ins, specs = [a, b], [a_spec, b_spec]
    if res is not None:
        ins.append(res)
        specs.append(o_spec)
    if after is not None:
        ins.append(after)
        specs.append(pl.BlockSpec(after.shape, lambda *g: (0,) * after.ndim))
    sem = (("parallel", "parallel", "arbitrary", "arbitrary") if stack == "sum"
           else ("parallel", "parallel", "parallel", "arbitrary"))
    return pl.pallas_call(
        body, name=name, grid=grid, in_specs=specs, out_specs=o_spec,
        out_shape=jax.ShapeDtypeStruct(((ns,) if stack == "out" else ()) + (m, n), out_dtype),
        scratch_shapes=[] if single else [pltpu.VMEM((tm, tn), F32)],
        compiler_params=_cparams(sem),
    )(*ins)


MM_VMEM_BUDGET = 40 * 1024 * 1024


def _mm_tiles(m, n, k, ns, sa, sb, so, has_res):
    def divs(x, mult, cap):
        out = [d for d in range(mult, min(x, cap) + 1, mult) if x % d == 0]
        return out or [x]

    best = None
    for tk in divs(k, 128, 8192):
        nk = (k // tk) * ns
        for tm in divs(m, 8, 2048):
            for tn in divs(n, 128, 2048):
                vmem = 2 * (tm * tk * sa + tk * tn * sb + tm * tn * so) + (2 * tm * tn * 4 if has_res else 0)
                vmem += tm * tn * 4 if nk > 1 else 0
                if vmem > MM_VMEM_BUDGET:
                    continue
                steps = (m // tm) * (n // tn) * nk
                traffic = (m // tm) * k * n * sb * ns + (n // tn if nk > 1 else 1) * m * k * sa * ns
                cost = steps * 0.4e-6 + traffic / 2.5e12 + (nk * m * n * 8 / 6e12 if nk > 1 else 0)
                cost += 2.0 * m * n * k * ns / 7e14
                if best is None or cost < best[0]:
                    best = (cost, tm, tn, tk)
    return best[1:]


def _norm_fwd(x, xoff, gain, ncol, w, out_dtype, *, z=None, zoff=0, name):
    t = x.shape[0]
    tr = _tile(t, 256, 8)

    def body(*refs):
        if z is None:
            x_ref, g_ref, o_ref = refs
            y = _rms_fn(x_ref[...], g_ref[...])
        else:
            x_ref, g_ref, z_ref, o_ref = refs
            y = _rms_fn(x_ref[...], g_ref[...], z_ref[...])
        o_ref[...] = y.astype(out_dtype)

    ins = [x, gain]
    specs = [pl.BlockSpec((tr, w), lambda j, r: (r, xoff + j)), pl.BlockSpec((1, w), lambda j, r: (0, 0))]
    if z is not None:
        ins.append(z)
        specs.append(pl.BlockSpec((tr, w), lambda j, r: (r, zoff + j)))
    return pl.pallas_call(
        body, name=name, grid=(ncol, t // tr), in_specs=specs,
        out_specs=pl.BlockSpec((tr, w), lambda j, r: (r, j)),
        out_shape=jax.ShapeDtypeStruct((t, ncol * w), out_dtype),
        compiler_params=_cparams(("parallel", "parallel")),
    )(*ins)


def _norm_bwd(x, xoff, gain, dy, dyoff, ncol, w, *, z=None, zoff=0, res=None, name):
    t = x.shape[0]
    tr = _tile(t, 256, 8)

    def body(*refs):
        it = iter(refs)
        x_ref, g_ref = next(it), next(it)
        z_ref = next(it) if z is not None else None
        dy_ref = next(it)
        r_ref = next(it) if res is not None else None
        dx_ref = next(it)
        dz_ref = next(it) if z is not None else None
        dg_ref = next(it)

        @pl.when((pl.program_id(0) == 0) & (pl.program_id(1) == 0))
        def _():
            dg_ref[...] = jnp.zeros_like(dg_ref)

        args = (x_ref[...], g_ref[...]) + ((z_ref[...],) if z is not None else ())
        _, vjp = jax.vjp(_rms_fn, *args)
        grads = vjp(dy_ref[...].astype(F32))
        dx = grads[0]
        if res is not None:
            dx = dx + r_ref[...]
        dx_ref[...] = dx
        if z is not None:
            dz_ref[...] = grads[2]
        dg_ref[...] += grads[1]

    ins = [x, gain]
    specs = [pl.BlockSpec((tr, w), lambda j, r: (r, xoff + j)), pl.BlockSpec((1, w), lambda j, r: (0, 0))]
    if z is not None:
        ins.append(z)
        specs.append(pl.BlockSpec((tr, w), lambda j, r: (r, zoff + j)))
    ins.append(dy)
    specs.append(pl.BlockSpec((tr, w), lambda j, r: (r, dyoff + j)))
    blk = pl.BlockSpec((tr, w), lambda j, r: (r, j))
    if res is not None:
        ins.append(res)
        specs.append(blk)
    full = jax.ShapeDtypeStruct((t, ncol * w), F32)
    out_shape, out_specs = [full], [blk]
    if z is not None:
        out_shape.append(full)
        out_specs.append(blk)
    out_shape.append(jax.ShapeDtypeStruct((1, w), F32))
    out_specs.append(pl.BlockSpec((1, w), lambda j, r: (0, 0)))
    return pl.pallas_call(
        body, name=name, grid=(ncol, t // tr), in_specs=specs, out_specs=out_specs, out_shape=out_shape,
        compiler_params=_cparams(("arbitrary", "arbitrary")),
    )(*ins)


def _small_fn(x, pa, pb, nf, ng):
    lane = lax.broadcasted_iota(jnp.int32, x.shape, 1)
    zz = x + pb
    logf = -_softplus(-zz)
    g = -jnp.exp(pa) * _softplus(zz)
    beta = _sigmoid(x)
    return jnp.where(lane < nf, logf, jnp.where(lane < nf + ng, g, beta))


def _tri(n, upper):
    r = lax.broadcasted_iota(jnp.int32, (n, n), 0)
    c = lax.broadcasted_iota(jnp.int32, (n, n), 1)
    return jnp.where((c >= r) if upper else (c <= r), 1.0, 0.0).astype(F32)


def _small_fwd(p, off, pa, pb, nf, ng):
    t = p.shape[0]
    blk = HEAD_DIM
    nb = t // blk

    def body(x_ref, pa_ref, pb_ref, v_ref, c_ref):
        v_ref[...] = _small_fn(x_ref[...], pa_ref[...], pb_ref[...], nf, ng)
        tri = _tri(blk, False)

        carry = jnp.zeros((1, HEAD_DIM), F32)
        for i in range(nb):
            rows = slice(i * blk, (i + 1) * blk)
            c = _nn_hi(tri, v_ref[rows, :]) + carry
            c_ref[rows, :] = c
            carry = c[blk - 1:blk, :]

    row = pl.BlockSpec((1, HEAD_DIM), lambda i: (0, 0))
    out = pl.BlockSpec((t, HEAD_DIM), lambda i: (0, 0))
    return pl.pallas_call(
        body, name="small_fwd", grid=(1,),
        in_specs=[pl.BlockSpec((t, HEAD_DIM), lambda i: (0, off)), row, row], out_specs=[out, out],
        out_shape=[jax.ShapeDtypeStruct((t, HEAD_DIM), F32)] * 2,
        compiler_params=_cparams(("arbitrary",)),
    )(p, pa, pb)


def _small_bwd(p, off, pa, pb, dvals, dcsum, nf, ng):
    t = p.shape[0]
    blk = HEAD_DIM
    nb = t // blk

    def body(x_ref, pa_ref, pb_ref, dv_ref, dc_ref, dx_ref, dpa_ref, dpb_ref, tot_ref):
        tri = _tri(blk, True)

        carry = jnp.zeros((1, HEAD_DIM), F32)
        for i in reversed(range(nb)):
            rows = slice(i * blk, (i + 1) * blk)
            c = _nn_hi(tri, dc_ref[rows, :]) + carry
            tot_ref[rows, :] = c + dv_ref[rows, :]
            carry = c[0:1, :]
        f = functools.partial(_small_fn, nf=nf, ng=ng)
        _, vjp = jax.vjp(f, x_ref[...], pa_ref[...], pb_ref[...])
        dx, dpa, dpb = vjp(tot_ref[...])
        dx_ref[...] = dx
        dpa_ref[...] = dpa
        dpb_ref[...] = dpb

    row = pl.BlockSpec((1, HEAD_DIM), lambda i: (0, 0))
    full = pl.BlockSpec((t, HEAD_DIM), lambda i: (0, 0))
    return pl.pallas_call(
        body, name="small_bwd", grid=(1,),
        in_specs=[pl.BlockSpec((t, HEAD_DIM), lambda i: (0, off)), row, row, full, full],
        out_specs=[full, row, row],
        out_shape=[jax.ShapeDtypeStruct((t, HEAD_DIM), F32), jax.ShapeDtypeStruct((1, HEAD_DIM), F32),
                   jax.ShapeDtypeStruct((1, HEAD_DIM), F32)],
        scratch_shapes=[pltpu.VMEM((t, HEAD_DIM), F32)],
        compiler_params=_cparams(("arbitrary",)),
    )(p, pa, pb, dvals, dcsum)


def _fox_fwd(q, k, v, cc, cr, nf, tq, tk):
    t = q.shape[0]
    scale = HEAD_DIM ** -0.5
    ratio = tq // tk

    def body(q_ref, k_ref, v_ref, cc_ref, cr_ref, o_ref, lse_ref):
        i = pl.program_id(1)
        qv = q_ref[...]
        ccol = cc_ref[0]
        rows = i * tq + lax.broadcasted_iota(jnp.int32, (tq, tk), 0)
        cols0 = lax.broadcasted_iota(jnp.int32, (tq, tk), 1)

        def step(j, carry):
            m, l, acc = carry
            ks = pl.ds(pl.multiple_of(j * tk, tk), tk)
            s = lax.dot_general(qv, k_ref[ks, :], (((1,), (1,)), ((), ())), preferred_element_type=F32) * scale
            s = s + ccol - cr_ref[0, j]
            s = jnp.where(cols0 + j * tk <= rows, s, NEG)
            m_new = jnp.maximum(m, jnp.max(s, axis=1, keepdims=True))
            pr = jnp.exp(s - m_new)
            alpha = jnp.exp(m - m_new)
            l = alpha * l + jnp.sum(pr, axis=1, keepdims=True)
            acc = alpha * acc + jnp.dot(pr.astype(BF16), v_ref[ks, :], preferred_element_type=F32)
            return m_new, l, acc

        init = (jnp.full((tq, 1), NEG, F32), jnp.zeros((tq, 1), F32), jnp.zeros((tq, HEAD_DIM), F32))
        m, l, acc = lax.fori_loop(0, (i + 1) * ratio, step, init)
        o_ref[...] = acc / l
        lse_ref[0] = m + jnp.log(l)

    head_all = pl.BlockSpec((t, HEAD_DIM), lambda h, i: (0, h))
    return pl.pallas_call(
        body, name="fox_fwd", grid=(nf, t // tq),
        in_specs=[pl.BlockSpec((tq, HEAD_DIM), lambda h, i: (i, h)), head_all, head_all,
                  pl.BlockSpec((1, tq, 1), lambda h, i: (h, i, 0)),
                  pl.BlockSpec((1, t // tk, 1, tk), lambda h, i: (h, 0, 0, 0))],
        out_specs=[pl.BlockSpec((tq, HEAD_DIM), lambda h, i: (i, h)),
                   pl.BlockSpec((1, tq, 1), lambda h, i: (h, i, 0))],
        out_shape=[jax.ShapeDtypeStruct((t, nf * HEAD_DIM), F32), jax.ShapeDtypeStruct((nf, t, 1), F32)],
        compiler_params=_cparams(("parallel", "parallel")),
    )(q, k, v, cc, cr)


def _fox_bwd(q, k, v, cc, cr, o, lse, dmix, nf, tq, tk):
    t = q.shape[0]
    scale = HEAD_DIM ** -0.5
    ratio = tq // tk

    def body(q_ref, k_ref, v_ref, cc_ref, cr_ref, o_ref, lse_ref, do_ref,
             dq_ref, dk_ref, dv_ref, dcc_ref, dcr_ref):
        i = pl.program_id(1)

        @pl.when(i == 0)
        def _():
            dk_ref[...] = jnp.zeros_like(dk_ref)
            dv_ref[...] = jnp.zeros_like(dv_ref)
            dcr_ref[...] = jnp.zeros_like(dcr_ref)

        qv = q_ref[...]
        ccol = cc_ref[0]
        lse_v = lse_ref[0]
        do = do_ref[...]
        do_b = do.astype(BF16)
        delta = jnp.sum(do * o_ref[...], axis=1, keepdims=True)
        rows = i * tq + lax.broadcasted_iota(jnp.int32, (tq, tk), 0)
        cols0 = lax.broadcasted_iota(jnp.int32, (tq, tk), 1)

        def step(j, carry):
            dq, dcc = carry
            ks = pl.ds(pl.multiple_of(j * tk, tk), tk)
            kj, vj = k_ref[ks, :], v_ref[ks, :]
            s = lax.dot_general(qv, kj, (((1,), (1,)), ((), ())), preferred_element_type=F32) * scale
            s = s + ccol - cr_ref[0, j]
            pr = jnp.where(cols0 + j * tk <= rows, jnp.exp(s - lse_v), 0.0)
            dp = lax.dot_general(do_b, vj, (((1,), (1,)), ((), ())), preferred_element_type=F32)
            ds = pr * (dp - delta)
            ds_b = ds.astype(BF16)
            dq = dq + jnp.dot(ds_b, kj, preferred_element_type=F32) * scale
            dk_ref[ks, :] += lax.dot_general(ds_b, qv, (((0,), (0,)), ((), ())),
                                             preferred_element_type=F32) * scale
            dv_ref[ks, :] += lax.dot_general(pr.astype(BF16), do_b, (((0,), (0,)), ((), ())),
                                             preferred_element_type=F32)
            dcr_ref[0, j] -= jnp.sum(ds, axis=0, keepdims=True)
            return dq, dcc + jnp.sum(ds, axis=1, keepdims=True)

        init = (jnp.zeros((tq, HEAD_DIM), F32), jnp.zeros((tq, 1), F32))
        dq, dcc = lax.fori_loop(0, (i + 1) * ratio, step, init)
        dq_ref[...] = dq
        dcc_ref[0] = dcc

    head_all = pl.BlockSpec((t, HEAD_DIM), lambda h, i: (0, h))
    qblk = pl.BlockSpec((tq, HEAD_DIM), lambda h, i: (i, h))
    colv = pl.BlockSpec((1, tq, 1), lambda h, i: (h, i, 0))
    rowv = pl.BlockSpec((1, t // tk, 1, tk), lambda h, i: (h, 0, 0, 0))
    wide = jax.ShapeDtypeStruct((t, nf * HEAD_DIM), F32)
    return pl.pallas_call(
        body, name="fox_bwd", grid=(nf, t // tq),
        in_specs=[qblk, head_all, head_all, colv, rowv, qblk, colv, qblk],
        out_specs=[qblk, head_all, head_all, colv, rowv],
        out_shape=[wide, wide, wide, jax.ShapeDtypeStruct((nf, t, 1), F32),
                   jax.ShapeDtypeStruct((nf, t // tk, 1, tk), F32)],
        compiler_params=_cparams(("parallel", "arbitrary")),
    )(q, k, v, cc, cr, o, lse, dmix)


def _mem_fn(mq, mk, mv, gq, gk):
    qn = _rms_fn(mq, gq)
    kn = _rms_fn(mk, gk)
    s = _nt(qn, kn) * (HEAD_DIM ** -0.5)
    e = jnp.exp(s - lax.stop_gradient(jnp.max(s, axis=1, keepdims=True)))
    pr = e / jnp.sum(e, axis=1, keepdims=True)
    return _nn(pr, mv)


def _mem_specs(t, m, tq, qoff):
    qblk = pl.BlockSpec((tq, HEAD_DIM), lambda h, i: (i, qoff + h))
    kblk = pl.BlockSpec((m, HEAD_DIM), lambda h, i: (0, h))
    vblk = pl.BlockSpec((m, HEAD_DIM), lambda h, i: (0, N_MEM_HEADS + h))
    row = pl.BlockSpec((1, HEAD_DIM), lambda h, i: (0, 0))
    return qblk, kblk, vblk, row


def _mem_fwd(p, qoff, mkv, gq, gk, tq):
    t, m = p.shape[0], mkv.shape[0]
    qblk, kblk, vblk, row = _mem_specs(t, m, tq, qoff)

    def body(q_ref, k_ref, v_ref, gq_ref, gk_ref, o_ref):
        o_ref[...] = _mem_fn(q_ref[...], k_ref[...], v_ref[...], gq_ref[...], gk_ref[...])

    return pl.pallas_call(
        body, name="mem_fwd", grid=(N_MEM_HEADS, t // tq), in_specs=[qblk, kblk, vblk, row, row],
        out_specs=pl.BlockSpec((tq, HEAD_DIM), lambda h, i: (i, h)),
        out_shape=jax.ShapeDtypeStruct((t, N_MEM_HEADS * HEAD_DIM), F32),
        compiler_params=_cparams(("parallel", "parallel")),
    )(p, mkv, mkv, gq, gk)


def _mem_bwd(p, qoff, mkv, gq, gk, dmix, dooff, tq):
    t, m = p.shape[0], mkv.shape[0]
    qblk, kblk, vblk, row = _mem_specs(t, m, tq, qoff)

    def body(q_ref, k_ref, v_ref, gq_ref, gk_ref, do_ref, dq_ref, dkv_k_ref, dkv_v_ref, dgq_ref, dgk_ref):
        h, i = pl.program_id(0), pl.program_id(1)

        @pl.when((h == 0) & (i == 0))
        def _():
            dgq_ref[...] = jnp.zeros_like(dgq_ref)
            dgk_ref[...] = jnp.zeros_like(dgk_ref)

        @pl.when(i == 0)
        def _():
            dkv_k_ref[...] = jnp.zeros_like(dkv_k_ref)
            dkv_v_ref[...] = jnp.zeros_like(dkv_v_ref)

        _, vjp = jax.vjp(_mem_fn, q_ref[...], k_ref[...], v_ref[...], gq_ref[...], gk_ref[...])
        dq, dk, dv, dgq, dgk = vjp(do_ref[...])
        dq_ref[...] = dq
        dkv_k_ref[...] += dk
        dkv_v_ref[...] += dv
        dgq_ref[...] += dgq
        dgk_ref[...] += dgk

    oblk = pl.BlockSpec((tq, HEAD_DIM), lambda h, i: (i, h))
    kout = pl.BlockSpec((m, HEAD_DIM), lambda h, i: (0, h))
    half = jax.ShapeDtypeStruct((m, N_MEM_HEADS * HEAD_DIM), F32)
    rshape = jax.ShapeDtypeStruct((1, HEAD_DIM), F32)
    return pl.pallas_call(
        body, name="mem_bwd", grid=(N_MEM_HEADS, t // tq),
        in_specs=[qblk, kblk, vblk, row, row, pl.BlockSpec((tq, HEAD_DIM), lambda h, i: (i, dooff + h))],
        out_specs=[oblk, kout, kout, row, row],
        out_shape=[jax.ShapeDtypeStruct((t, N_MEM_HEADS * HEAD_DIM), F32), half, half, rshape, rshape],
        compiler_params=_cparams(("arbitrary", "arbitrary")),
    )(p, mkv, mkv, gq, gk, dmix)


def _shift_down(x, s):
    if s == 0:
        return x
    r = lax.broadcasted_iota(jnp.int32, x.shape, 0)
    return jnp.where(r >= s, pltpu.roll(x, s, 0), 0.0)


def _shift_up(x, s):
    if s == 0:
        return x
    n = x.shape[0]
    r = lax.broadcasted_iota(jnp.int32, x.shape, 0)
    return jnp.where(r < n - s, pltpu.roll(x, n - s, 0), 0.0)


def _conv_fn(x0, x1, x2, x3, w0, w1, w2, w3, kind):
    y = _silu(x0 * w0 + x1 * w1 + x2 * w2 + x3 * w3)
    if kind == 2:
        return y
    y = y * lax.rsqrt(jnp.sum(y * y, axis=-1, keepdims=True) + NORM_EPS)
    return y * (HEAD_DIM ** -0.5) if kind == 0 else y


def _conv_fwd(p, off, conv_w, ng):
    t = p.shape[0]

    def body(x_ref, w_ref, o_ref):
        kind = pl.program_id(0) // ng
        x = x_ref[...]
        xs = [_shift_down(x, CONV_WIDTH - 1 - j) for j in range(CONV_WIDTH)]
        ws = [w_ref[j:j + 1, :] for j in range(CONV_WIDTH)]
        for kd in range(3):
            @pl.when(kind == kd)
            def _(kd=kd):
                o_ref[...] = _conv_fn(*xs, *ws, kd)

    return pl.pallas_call(
        body, name="gdn_conv_fwd", grid=(3 * ng,),
        in_specs=[pl.BlockSpec((t, HEAD_DIM), lambda c: (0, off + c)),
                  pl.BlockSpec((CONV_WIDTH, HEAD_DIM), lambda c: (0, c))],
        out_specs=pl.BlockSpec((t, HEAD_DIM), lambda c: (0, c)),
        out_shape=jax.ShapeDtypeStruct((t, 3 * ng * HEAD_DIM), F32),
        compiler_params=_cparams(("parallel",)),
    )(p, conv_w)


def _conv_bwd(p, off, conv_w, dy, ng):
    t = p.shape[0]

    def body(x_ref, w_ref, dy_ref, dx_ref, dw_ref):
        kind = pl.program_id(0) // ng
        x = x_ref[...]
        xs = [_shift_down(x, CONV_WIDTH - 1 - j) for j in range(CONV_WIDTH)]
        ws = [w_ref[j:j + 1, :] for j in range(CONV_WIDTH)]
        for kd in range(3):
            @pl.when(kind == kd)
            def _(kd=kd):
                _, vjp = jax.vjp(functools.partial(_conv_fn, kind=kd), *xs, *ws)
                g = vjp(dy_ref[...])
                dx = _shift_up(g[0], CONV_WIDTH - 1)
                for j in range(1, CONV_WIDTH):
                    dx = dx + _shift_up(g[j], CONV_WIDTH - 1 - j)
                dx_ref[...] = dx
                for j in range(CONV_WIDTH):
                    dw_ref[j:j + 1, :] = g[CONV_WIDTH + j]

    blk = pl.BlockSpec((t, HEAD_DIM), lambda c: (0, c))
    wblk = pl.BlockSpec((CONV_WIDTH, HEAD_DIM), lambda c: (0, c))
    return pl.pallas_call(
        body, name="gdn_conv_bwd", grid=(3 * ng,),
        in_specs=[pl.BlockSpec((t, HEAD_DIM), lambda c: (0, off + c)), wblk, blk],
        out_specs=[blk, wblk],
        out_shape=[jax.ShapeDtypeStruct((t, 3 * ng * HEAD_DIM), F32),
                   jax.ShapeDtypeStruct((CONV_WIDTH, 3 * ng * HEAD_DIM), F32)],
        compiler_params=_cparams(("parallel",)),
    )(p, conv_w, dy)


def _wy_fn(q, k, v, gcol, grow, bcol):
    b, c, dk = q.shape
    r = lax.broadcasted_iota(jnp.int32, (1, c, c), 1)
    e = lax.broadcasted_iota(jnp.int32, (1, c, c), 2)
    tril, strict = e <= r, e < r
    gc_col = jnp.sum(jnp.where(tril, grow, 0.0), axis=2, keepdims=True)
    gc_row = jnp.sum(jnp.where(r <= e, gcol, 0.0), axis=1, keepdims=True)
    g_last = jnp.sum(gcol, axis=1, keepdims=True)
    decay = jnp.exp(jnp.where(tril, gc_col - gc_row, NEG))
    kb, vb = k * bcol, v * bcol
    lower = jnp.where(strict, _nt(kb, k) * decay, 0.0)
    inv = jnp.where(r == e, 1.0, 0.0) - lower
    pw = lower
    for _ in range(int(math.log2(c)) - 1):
        pw = _nn_x3(pw, pw)
        inv = inv + _nn_x3(inv, pw)
    u = _nn_x3(inv, vb)
    w = _nn_x3(inv, kb * jnp.exp(gc_col))
    attn = jnp.where(tril, _nt(q, k) * decay, 0.0)
    qg = q * jnp.exp(gc_col)
    kdec = k * jnp.exp(g_last - gc_col)
    egl = jnp.broadcast_to(jnp.exp(g_last), (b, 1, dk))
    return u, w, qg, kdec, attn, egl


def _scan_fn(u, w, qg, kdec, attn, egl, state):
    v_new = u - _nn(w, state)
    o = _nn(qg, state) + _nn(attn, v_new)
    return o, state * egl + _tn(kdec, v_new)


GDN_CHUNKS_PER_STEP = 4


def _gdn_fwd(qkv, gcol, grow, bcol, ng):
    t = qkv.shape[0]
    nch = t // CHUNK

    cb = GDN_CHUNKS_PER_STEP
    wy = _gdn_wy(qkv, gcol, grow, bcol, ng, cb)

    def body(u_ref, w_ref, qg_ref, kd_ref, at_ref, eg_ref, o_ref, st_ref, state):
        @pl.when(pl.program_id(0) == 0)
        def _():
            state[...] = jnp.zeros_like(state)

        st_ref[:, 0] = state[...]
        heads = lambda ref: jnp.stack([ref[:, h * HEAD_DIM:(h + 1) * HEAD_DIM] for h in range(ng)])
        o, new = _scan_fn(heads(u_ref), heads(w_ref), heads(qg_ref), heads(kd_ref), at_ref[:, 0], eg_ref[:, 0],
                          state[...])
        for h in range(ng):
            o_ref[:, h * HEAD_DIM:(h + 1) * HEAD_DIM] = o[h]
        state[...] = new

    w = ng * HEAD_DIM
    blk = pl.BlockSpec((CHUNK, w), lambda i: (i, 0))
    o, states = pl.pallas_call(
        body, name="gdn_scan_fwd", grid=(nch,),
        in_specs=[blk, blk, blk, blk, pl.BlockSpec((ng, 1, CHUNK, CHUNK), lambda i: (0, i, 0, 0)),
                  pl.BlockSpec((ng, 1, 1, HEAD_DIM), lambda i: (0, i, 0, 0))],
        out_specs=[blk, pl.BlockSpec((ng, 1, HEAD_DIM, HEAD_DIM), lambda i: (0, i, 0, 0))],
        out_shape=[jax.ShapeDtypeStruct((t, w), F32),
                   jax.ShapeDtypeStruct((ng, nch, HEAD_DIM, HEAD_DIM), F32)],
        scratch_shapes=[pltpu.VMEM((ng, HEAD_DIM, HEAD_DIM), F32)],
        compiler_params=_cparams(("arbitrary",)),
    )(*wy)
    return o, (wy, states)


def _wy_batch(q_ref, k_ref, v_ref, gc_ref, gr_ref, bc_ref, ng, cb):
    idx = [(c, h) for c in range(cb) for h in range(ng)]
    rows = lambda c: slice(c * CHUNK, (c + 1) * CHUNK)
    lanes = lambda h: slice(h * HEAD_DIM, (h + 1) * HEAD_DIM)
    wide = lambda ref: jnp.stack([ref[rows(c), lanes(h)] for c, h in idx])
    col = lambda ref: jnp.stack([ref[h, rows(c), :] for c, h in idx])
    return idx, (wide(q_ref), wide(k_ref), wide(v_ref), col(gc_ref), jnp.stack([gr_ref[h, c] for c, h in idx]),
                 col(bc_ref))


def _gdn_wy(qkv, gcol, grow, bcol, ng, cb):
    t = qkv.shape[0]
    nch = t // CHUNK

    def body(q_ref, k_ref, v_ref, gc_ref, gr_ref, bc_ref, u_ref, w_ref, qg_ref, kd_ref, at_ref, eg_ref):
        idx, args = _wy_batch(q_ref, k_ref, v_ref, gc_ref, gr_ref, bc_ref, ng, cb)
        u, w, qg, kd, at, eg = _wy_fn(*args)
        for b, (c, h) in enumerate(idx):
            rows, lanes = slice(c * CHUNK, (c + 1) * CHUNK), slice(h * HEAD_DIM, (h + 1) * HEAD_DIM)
            u_ref[rows, lanes] = u[b]
            w_ref[rows, lanes] = w[b]
            qg_ref[rows, lanes] = qg[b]
            kd_ref[rows, lanes] = kd[b]
            at_ref[h, c] = at[b]
            eg_ref[h, c] = eg[b]

    wd = ng * HEAD_DIM
    blk = lambda o: pl.BlockSpec((cb * CHUNK, wd), lambda i: (i, o))
    col = pl.BlockSpec((ng, cb * CHUNK, 1), lambda i: (0, i, 0))
    wide = jax.ShapeDtypeStruct((t, wd), F32)
    return pl.pallas_call(
        body, name="gdn_wy_fwd", grid=(nch // cb,),
        in_specs=[blk(0), blk(1), blk(2), col, pl.BlockSpec((ng, cb, 1, CHUNK), lambda i: (0, i, 0, 0)), col],
        out_specs=[blk(0), blk(0), blk(0), blk(0), pl.BlockSpec((ng, cb, CHUNK, CHUNK), lambda i: (0, i, 0, 0)),
                   pl.BlockSpec((ng, cb, 1, HEAD_DIM), lambda i: (0, i, 0, 0))],
        out_shape=[wide, wide, wide, wide, jax.ShapeDtypeStruct((ng, nch, CHUNK, CHUNK), F32),
                   jax.ShapeDtypeStruct((ng, nch, 1, HEAD_DIM), F32)],
        compiler_params=_cparams(("parallel",)),
    )(qkv, qkv, qkv, gcol, grow, bcol)


def _gdn_bwd(qkv, gcol, grow, bcol, saved, do, ng):
    t = qkv.shape[0]
    nch = t // CHUNK
    cb = GDN_CHUNKS_PER_STEP // 2
    wy, states = saved
    wd = ng * HEAD_DIM

    def scan_body(u_ref, w_ref, qg_ref, kd_ref, at_ref, eg_ref, st_ref, do_ref,
                  du_ref, dw_ref, dqg_ref, dkd_ref, dat_ref, deg_ref, dstate):
        @pl.when(pl.program_id(0) == 0)
        def _():
            dstate[...] = jnp.zeros_like(dstate)

        heads = lambda ref: jnp.stack([ref[:, h * HEAD_DIM:(h + 1) * HEAD_DIM] for h in range(ng)])
        _, vjp = jax.vjp(_scan_fn, heads(u_ref), heads(w_ref), heads(qg_ref), heads(kd_ref), at_ref[:, 0],
                         eg_ref[:, 0], st_ref[:, 0])
        du, dw, dqg, dkd, dat, deg, dst = vjp((heads(do_ref), dstate[...]))
        for h in range(ng):
            lanes = slice(h * HEAD_DIM, (h + 1) * HEAD_DIM)
            du_ref[:, lanes] = du[h]
            dw_ref[:, lanes] = dw[h]
            dqg_ref[:, lanes] = dqg[h]
            dkd_ref[:, lanes] = dkd[h]
        dat_ref[:, 0] = dat
        deg_ref[:, 0] = deg
        dstate[...] = dst

    rev = lambda i: nch - 1 - i
    blk = pl.BlockSpec((CHUNK, wd), lambda i: (rev(i), 0))
    atb = pl.BlockSpec((ng, 1, CHUNK, CHUNK), lambda i: (0, rev(i), 0, 0))
    egb = pl.BlockSpec((ng, 1, 1, HEAD_DIM), lambda i: (0, rev(i), 0, 0))
    wide = jax.ShapeDtypeStruct((t, wd), F32)
    at_shape = jax.ShapeDtypeStruct((ng, nch, CHUNK, CHUNK), F32)
    eg_shape = jax.ShapeDtypeStruct((ng, nch, 1, HEAD_DIM), F32)
    dwy = pl.pallas_call(
        scan_body, name="gdn_scan_bwd", grid=(nch,),
        in_specs=[blk, blk, blk, blk, atb, egb,
                  pl.BlockSpec((ng, 1, HEAD_DIM, HEAD_DIM), lambda i: (0, rev(i), 0, 0)), blk],
        out_specs=[blk, blk, blk, blk, atb, egb],
        out_shape=[wide, wide, wide, wide, at_shape, eg_shape],
        scratch_shapes=[pltpu.VMEM((ng, HEAD_DIM, HEAD_DIM), F32)],
        compiler_params=_cparams(("arbitrary",)),
    )(*wy, states, do)

    def wy_body(q_ref, k_ref, v_ref, gc_ref, gr_ref, bc_ref, du_ref, dw_ref, dqg_ref, dkd_ref, dat_ref, deg_ref,
                dq_ref, dk_ref, dv_ref, dgc_ref, dgr_ref, dbc_ref):
        idx, args = _wy_batch(q_ref, k_ref, v_ref, gc_ref, gr_ref, bc_ref, ng, cb)
        rows = lambda c: slice(c * CHUNK, (c + 1) * CHUNK)
        lanes = lambda h: slice(h * HEAD_DIM, (h + 1) * HEAD_DIM)
        wide_ct = lambda ref: jnp.stack([ref[rows(c), lanes(h)] for c, h in idx])
        cts = (wide_ct(du_ref), wide_ct(dw_ref), wide_ct(dqg_ref), wide_ct(dkd_ref),
               jnp.stack([dat_ref[h, c] for c, h in idx]), jnp.stack([deg_ref[h, c] for c, h in idx]))
        _, vjp = jax.vjp(_wy_fn, *args)
        dq, dk, dv, dgc, dgr, dbc = vjp(cts)
        for b, (c, h) in enumerate(idx):
            dq_ref[rows(c), lanes(h)] = dq[b]
            dk_ref[rows(c), lanes(h)] = dk[b]
            dv_ref[rows(c), lanes(h)] = dv[b]
            dgc_ref[h, rows(c), :] = dgc[b]
            dgr_ref[h, c] = dgr[b]
            dbc_ref[h, rows(c), :] = dbc[b]

    cblk = lambda o: pl.BlockSpec((cb * CHUNK, wd), lambda i: (i, o))
    col = pl.BlockSpec((ng, cb * CHUNK, 1), lambda i: (0, i, 0))
    rowv = pl.BlockSpec((ng, cb, 1, CHUNK), lambda i: (0, i, 0, 0))
    cshape = jax.ShapeDtypeStruct((ng, t, 1), F32)
    return pl.pallas_call(
        wy_body, name="gdn_wy_bwd", grid=(nch // cb,),
        in_specs=[cblk(0), cblk(1), cblk(2), col, rowv, col, cblk(0), cblk(0), cblk(0), cblk(0),
                  pl.BlockSpec((ng, cb, CHUNK, CHUNK), lambda i: (0, i, 0, 0)),
                  pl.BlockSpec((ng, cb, 1, HEAD_DIM), lambda i: (0, i, 0, 0))],
        out_specs=[cblk(0), cblk(0), cblk(0), col, rowv, col],
        out_shape=[wide, wide, wide, cshape, jax.ShapeDtypeStruct((ng, nch, 1, CHUNK), F32), cshape],
        compiler_params=_cparams(("parallel",)),
    )(qkv, qkv, qkv, gcol, grow, bcol, *dwy)


def _swiglu_fn(gate, up):
    return _silu(gate) * up


def _swiglu_specs(gu):
    _, t, w = gu.shape
    tr, tc = _tile(t, 512, 8), _tile(w, 1408)
    nc = w // tc
    pair = pl.BlockSpec((2, tr, tc), lambda j, r, c: (j, r, c))
    flat = pl.BlockSpec((tr, tc), lambda j, r, c: (r, j * nc + c))
    return (2, t // tr, nc), pair, flat


def _swiglu_fwd(gu):
    grid, pair, flat = _swiglu_specs(gu)

    def body(gu_ref, o_ref):
        o_ref[...] = _swiglu_fn(gu_ref[0], gu_ref[1]).astype(BF16)

    return pl.pallas_call(
        body, name="swiglu_fwd", grid=grid, in_specs=[pair], out_specs=flat,
        out_shape=jax.ShapeDtypeStruct((gu.shape[1], 2 * gu.shape[2]), BF16),
        compiler_params=_cparams(("parallel", "parallel", "parallel")),
    )(gu)


def _swiglu_bwd(gu, dact):
    grid, pair, flat = _swiglu_specs(gu)

    def body(gu_ref, d_ref, o_ref):
        _, vjp = jax.vjp(_swiglu_fn, gu_ref[0], gu_ref[1])
        dg, du = vjp(d_ref[...])
        o_ref[0] = dg.astype(BF16)
        o_ref[1] = du.astype(BF16)

    return pl.pallas_call(
        body, name="swiglu_bwd", grid=grid, in_specs=[pair, flat], out_specs=pair,
        out_shape=jax.ShapeDtypeStruct(gu.shape, BF16),
        compiler_params=_cparams(("parallel", "parallel", "parallel")),
    )(gu, dact)


def _loss_head(h2, target):
    t, d = h2.shape
    tr = _tile(t, 256, 8)

    def body(h_ref, t_ref, l_ref, d_ref):
        @pl.when(pl.program_id(0) == 0)
        def _():
            l_ref[...] = jnp.zeros_like(l_ref)

        err = h_ref[...] - t_ref[...]
        d_ref[...] = err * (1.0 / d)
        part = 0.5 * jnp.sum(jnp.mean(err * err, axis=-1, keepdims=True), axis=0, keepdims=True)
        lane = lax.broadcasted_iota(jnp.int32, (8, HEAD_DIM), 1)
        row = lax.broadcasted_iota(jnp.int32, (8, HEAD_DIM), 0)
        l_ref[...] += jnp.where((lane == 0) & (row == 0), part, 0.0)

    blk = pl.BlockSpec((tr, d), lambda r: (r, 0))
    return pl.pallas_call(
        body, name="loss_head", grid=(t // tr,), in_specs=[blk, blk],
        out_specs=[pl.BlockSpec((8, HEAD_DIM), lambda r: (0, 0)), blk],
        out_shape=[jax.ShapeDtypeStruct((8, HEAD_DIM), F32), jax.ShapeDtypeStruct((t, d), F32)],
        compiler_params=_cparams(("arbitrary",)),
    )(h2, target)


def _adamw(w, g, m, v, *, g2=None, name):
    r, c = w.shape
    tr = _tile(r, max(8, (1 << 19) // c // 8 * 8), 8)

    def body(*refs):
        if g2 is None:
            w_ref, g_ref, m_ref, v_ref, go_ref, d_ref, mo_ref, vo_ref = refs
            gr = g_ref[...]
        else:
            w_ref, g_ref, g2_ref, m_ref, v_ref, go_ref, d_ref, mo_ref, vo_ref = refs
            gr = g_ref[...] + g2_ref[...]
        mn = ADAM_B1 * m_ref[...] + (1.0 - ADAM_B1) * gr
        vn = ADAM_B2 * v_ref[...] + (1.0 - ADAM_B2) * (gr * gr)
        m_hat = mn / (1.0 - ADAM_B1 ** ADAM_STEP)
        v_hat = vn / (1.0 - ADAM_B2 ** ADAM_STEP)
        go_ref[...] = gr
        d_ref[...] = -ADAM_LR * (m_hat / (jnp.sqrt(v_hat) + ADAM_EPS) + ADAM_WD * w_ref[...])
        mo_ref[...] = mn
        vo_ref[...] = vn

    blk = pl.BlockSpec((tr, c), lambda i: (i, 0))
    n_in = 4 if g2 is None else 5
    ins = [w, g] + ([g2] if g2 is not None else []) + [m, v]
    return pl.pallas_call(
        body, name=name, grid=(r // tr,), in_specs=[blk] * n_in, out_specs=[blk] * 4,
        out_shape=[jax.ShapeDtypeStruct((r, c), F32)] * 4,
        compiler_params=_cparams(("parallel",)),
    )(*ins)


class _Layout:
    def __init__(self, d):
        nh = d // HEAD_DIM
        self.nm = N_MEM_HEADS
        self.nf = (nh - self.nm) // 2
        self.ng = nh - self.nm - self.nf
        nf, ng, nm = self.nf, self.ng, self.nm
        self.o_fq, self.o_fk, self.o_fv = 0, nf, 2 * nf
        self.o_gq = 3 * nf
        self.o_gz = 3 * nf + 3 * ng
        self.o_mq = 3 * nf + 4 * ng
        self.o_sm = self.o_mq + nm
        self.blocks = -(-(self.o_sm + 1) // 8) * 8
        self.cols = self.blocks * HEAD_DIM
        hd = HEAD_DIM
        sizes = [nf * hd, nf * hd, nf * hd, nf, 3 * ng * hd, ng * hd, ng, ng, nm * hd]
        starts = [sum(sizes[:i]) for i in range(len(sizes))]
        self.ref = list(zip(starts, sizes))
        self.in_cols = sum(sizes)

    def regroup(self, w):
        part = lambda i: w[:, self.ref[i][0]:self.ref[i][0] + self.ref[i][1]]
        pieces = [part(0), part(1), part(2), part(4), part(5), part(8), part(3), part(6), part(7)]
        pad = self.cols - self.in_cols
        return jnp.concatenate(pieces + [jnp.zeros((w.shape[0], pad), w.dtype)], axis=1)

    def ungroup(self, g):
        hd, nf, ng, nm = HEAD_DIM, self.nf, self.ng, self.nm
        sm = self.o_sm * hd
        return jnp.concatenate([
            g[:, :3 * nf * hd], g[:, sm:sm + nf], g[:, self.o_gq * hd:self.o_gz * hd],
            g[:, self.o_gz * hd:self.o_mq * hd], g[:, sm + nf:sm + nf + ng], g[:, sm + nf + ng:sm + nf + 2 * ng],
            g[:, self.o_mq * hd:self.o_sm * hd]], axis=1)


def _lane_row(pieces):
    row = jnp.zeros((1, HEAD_DIM), F32)
    for off, a in pieces:
        row = lax.dynamic_update_slice(row, a.astype(F32), (0, off))
    return row


def _local_step(x, mem, target, win, mid_weights, late_weights, reduce_start, sp):
    t, d = x.shape
    lay = _Layout(d)
    nf, ng, nm, hd = lay.nf, lay.ng, lay.nm, HEAD_DIM
    nch = t // CHUNK
    tq = _tile(t, 256)
    tk = tq

    u = _norm_fwd(x, 0, sp["norm_mix"], 1, d, BF16, name="norm_mix_fwd")
    p = _mm(u, win, name="mm_in")
    wmkv, conv_taps = mid_weights(p)
    sp = dict(sp, gdn_conv=conv_taps)
    pa = _lane_row([(nf, sp["gdn_a_log"])])
    pb = _lane_row([(0, sp["fox_f_bias"]), (nf, sp["gdn_dt_bias"])])
    vals, csum = _small_fwd(p, lay.o_sm, pa, pb, nf, ng)

    c_t = csum[:, :nf].T
    cc, cr = c_t.reshape(nf, t, 1), c_t.reshape(nf, t // tk, 1, tk)
    fq = _norm_fwd(p, lay.o_fq, sp["fox_q_norm"], nf, hd, BF16, name="fox_qnorm_fwd")
    fk = _norm_fwd(p, lay.o_fk, sp["fox_k_norm"], nf, hd, BF16, name="fox_knorm_fwd")
    fv = p[:, lay.o_fv * hd:(lay.o_fv + nf) * hd].astype(BF16)
    o_fox, lse = _fox_fwd(fq, fk, fv, cc, cr, nf, tq, tk)

    qkv = _conv_fwd(p, lay.o_gq, sp["gdn_conv"], ng)
    g_t, b_t = vals[:, nf:nf + ng].T, vals[:, nf + ng:nf + 2 * ng].T
    gcol, grow, bcol = g_t.reshape(ng, t, 1), g_t.reshape(ng, nch, 1, CHUNK), b_t.reshape(ng, t, 1)
    o_g, states = _gdn_fwd(qkv, gcol, grow, bcol, ng)
    o_gdn = _norm_fwd(o_g, 0, sp["gdn_out_norm"], ng, hd, BF16, z=p, zoff=lay.o_gz, name="gdn_out_fwd")

    mem_n = _norm_fwd(mem, 0, sp["mem_norm"], 1, d, BF16, name="mem_norm_fwd")
    mkv = _mm(mem_n, wmkv, name="mm_memkv")
    o_mem = _mem_fwd(p, lay.o_mq, mkv, sp["mem_q_norm"], sp["mem_k_norm"], tq)

    mix = jnp.concatenate([o_fox.astype(BF16), o_gdn, o_mem.astype(BF16)], axis=1)
    wout, wgu, wd = late_weights(mix)
    h1 = _mm(mix, wout, res=x, name="mm_out")
    n2 = _norm_fwd(h1, 0, sp["norm_ffn"], 1, d, BF16, name="norm_ffn_fwd")
    wgu4 = wgu.reshape(4, d, -1)
    gu = _mm(n2, wgu4, stack="out", name="mm_gate_up")
    act = _swiglu_fwd(gu)
    h2 = _mm(act, wd, res=h1, name="mm_down")
    loss_blk, dh2 = _loss_head(h2, target)

    g = {}
    dw_down = _mm(act, dh2, ta=True, out_dtype=BF16, name="mm_dw_down")
    dact = _mm(dh2, wd, tb=True, name="mm_dact")
    dgu = _swiglu_bwd(gu, dact)
    dw_gate_up = _mm(n2, dgu, ta=True, stack="out", out_dtype=BF16, name="mm_dw_gate_up").reshape(wgu.shape)
    token = reduce_start("ffn", {"w_down": dw_down, "w_gate_up": dw_gate_up})
    dn2 = _mm(dgu, wgu4, tb=True, stack="sum", name="mm_dn2")
    dh1, g["norm_ffn"] = _norm_bwd(h1, 0, sp["norm_ffn"] + token[0, 0], dn2, 0, 1, d, res=dh2,
                                   name="norm_ffn_bwd")
    dw_out = _mm(mix, dh1, ta=True, out_dtype=BF16, name="mm_dw_out")
    dmix = _mm(dh1, wout, tb=True, name="mm_dmix")

    dmq, dmk, dmv, g["mem_q_norm"], g["mem_k_norm"] = _mem_bwd(
        p, lay.o_mq, mkv, sp["mem_q_norm"], sp["mem_k_norm"], dmix, nf + ng, tq)
    dmkv = jnp.concatenate([dmk, dmv], axis=1)
    dw_mem_kv = _mm(mem_n, dmkv, ta=True, out_dtype=BF16, name="mm_dw_memkv")
    token = reduce_start("mix", {"w_out": dw_out, "w_mem_kv": dw_mem_kv})
    dmem_n = _mm(dmkv, wmkv, tb=True, name="mm_dmem")
    _, g["mem_norm"] = _norm_bwd(mem, 0, sp["mem_norm"], dmem_n, 0, 1, d, name="mem_norm_bwd")

    do_g, dgz, g["gdn_out_norm"] = _norm_bwd(o_g, 0, sp["gdn_out_norm"] + token[0, 0], dmix, nf, ng, hd, z=p,
                                             zoff=lay.o_gz, name="gdn_out_bwd")
    dq, dk, dv, dgc, dgr, dbc = _gdn_bwd(qkv, gcol, grow, bcol, states, do_g, ng)
    dgqkv, g["gdn_conv"] = _conv_bwd(p, lay.o_gq, sp["gdn_conv"], jnp.concatenate([dq, dk, dv], axis=1), ng)
    dg_t = dgc.reshape(ng, t) + dgr.reshape(ng, t)
    db_t = dbc.reshape(ng, t)

    dfq_n, dfk_n, dfv, dcc, dcr = _fox_bwd(fq, fk, fv, cc, cr, o_fox, lse, dmix, nf, tq, tk)
    dfq, g["fox_q_norm"] = _norm_bwd(p, lay.o_fq, sp["fox_q_norm"], dfq_n, 0, nf, hd, name="fox_qnorm_bwd")
    dfk, g["fox_k_norm"] = _norm_bwd(p, lay.o_fk, sp["fox_k_norm"], dfk_n, 0, nf, hd, name="fox_knorm_bwd")
    dc_t = dcc.reshape(nf, t) + dcr.reshape(nf, t)

    lanes_left = hd - nf - 2 * ng
    dvals = jnp.concatenate([jnp.zeros((t, nf), F32), dg_t.T, db_t.T, jnp.zeros((t, lanes_left), F32)], axis=1)
    dcsum = jnp.concatenate([dc_t.T, jnp.zeros((t, hd - nf), F32)], axis=1)
    dsm, dpa, dpb = _small_bwd(p, lay.o_sm, pa, pb, dvals, dcsum, nf, ng)
    g["fox_f_bias"] = dpb[:, :nf]
    g["gdn_dt_bias"] = dpb[:, nf:nf + ng]
    g["gdn_a_log"] = dpa[:, nf:nf + ng]

    pad = jnp.zeros((t, lay.cols - (lay.o_sm + 1) * hd), F32)
    dp = jnp.concatenate([dfq, dfk, dfv, dgqkv, dgz, dmq, dsm, pad], axis=1)
    token = reduce_start("in", {"w_in": _mm(u, dp, ta=True, out_dtype=BF16, name="mm_dw_in")})
    du = _mm(dp, win, tb=True, after=token, name="mm_du")
    dx, g["norm_mix"] = _norm_bwd(x, 0, sp["norm_mix"], du, 0, 1, d, res=dh1, name="norm_mix_bwd")
    return loss_blk, dx, g


ANY = pl.BlockSpec(memory_space=pl.ANY)


def _me():
    x, y, c = lax.axis_index("x"), lax.axis_index("y"), lax.axis_index("c")
    chips = [(1 - x, y), (x, 1 - y), (1 - x, 1 - y)]
    return x, y, c, chips


def _slot(axis, k):
    return k if axis == 0 else 2 * (k % 2) + k // 2


def _slab(ref, axis, rows, cols, k, h):
    half = rows // 2
    return ref.at[pl.ds(_slot(axis, k) * rows + h * half, half), :]


def _remote(src, dst, send_sem, recv_sem, dev):
    return pltpu.make_async_remote_copy(src_ref=src, dst_ref=dst, send_sem=send_sem, recv_sem=recv_sem,
                                        device_id=dev, device_id_type=MESH)


HBM = pl.BlockSpec(memory_space=pltpu.HBM)
SEM = pl.BlockSpec(memory_space=pltpu.SEMAPHORE)
SPLIT = pltpu.CompilerParams(has_side_effects=pltpu.SideEffectType.DATAFLOW_SIDE_EFFECTING)
TOKEN = jax.ShapeDtypeStruct((8, HEAD_DIM), F32)


def _in_hbm(v):
    return pltpu.with_memory_space_constraint(v, pltpu.HBM)


def _cast_place(shard, axis, name):
    r, c = shard.shape
    tr, tc = _tile(r, 512, 16), _tile(c, 2048)
    nb = r // tr
    chip = 2 * lax.axis_index("x") + lax.axis_index("y")
    slot = jnp.reshape(_slot(axis, chip), (1,)).astype(jnp.int32)

    def body(slot_ref, x_ref, o_ref):
        o_ref[...] = x_ref[...].astype(BF16)

    return pl.pallas_call(
        body, name=name,
        grid_spec=pltpu.PrefetchScalarGridSpec(
            num_scalar_prefetch=1, grid=(nb, c // tc),
            in_specs=[pl.BlockSpec((tr, tc), lambda i, l, s: (i, l))],
            out_specs=pl.BlockSpec((tr, tc), lambda i, l, s: (s[0] * nb + i, l))),
        out_shape=jax.ShapeDtypeStruct((4 * r, c), BF16),
        compiler_params=_cparams(("parallel", "parallel")),
    )(slot, shard)


def _gather_start(bufs, axes, shapes, groups):
    n = len(bufs)

    def body(*refs):
        dst = refs[n:2 * n]
        sems = refs[2 * n:2 * n + 2 * len(groups)]
        token = refs[-1]
        x, y, c, chips = _me()
        k = 2 * x + y
        for gi, ws in enumerate(groups):
            for i, w in enumerate(ws):
                r, cl = shapes[w]
                place = _slab(dst[w], axes[w], r, cl, k, c)
                for j, (px, py) in enumerate(chips):
                    _remote(place, place, sems[2 * gi].at[3 * i + j], sems[2 * gi + 1].at[3 * i + j],
                            (px, py, c)).start()
        token[...] = jnp.zeros_like(token)

    sem_shapes = [pltpu.SemaphoreType.DMA((3 * len(ws),)) for ws in groups for _ in range(2)]
    outs = pl.pallas_call(
        body, name="gather_ici_start", in_specs=[HBM] * n,
        out_specs=[HBM] * n + [SEM] * len(sem_shapes) + [pl.BlockSpec(memory_space=pltpu.VMEM)],
        out_shape=[pltpu.HBM(b.shape, b.dtype) for b in bufs] + sem_shapes + [TOKEN],
        input_output_aliases={w: w for w in range(n)}, compiler_params=SPLIT,
    )(*[_in_hbm(b) for b in bufs])
    sems = outs[n:-1]
    return outs[:n], [(sems[2 * g], sems[2 * g + 1]) for g in range(len(groups))], outs[-1]


def _gather_wait(bufs, axes, shapes, sems, after, name):
    n = len(bufs)

    def body(*refs):
        send_sems, recv_sems = refs[n], refs[n + 1]
        dst = refs[n + 3:]
        x, y, c, chips = _me()
        k = 2 * x + y
        for i in range(n):
            r, cl = shapes[i]
            for j, (px, py) in enumerate(chips):
                got = _slab(dst[i], axes[i], r, cl, 2 * px + py, c)
                _remote(got, got, send_sems.at[3 * i + j], recv_sems.at[3 * i + j], (px, py, c)).wait_recv()
        for i in range(n):
            r, cl = shapes[i]
            mine = _slab(dst[i], axes[i], r, cl, k, c)
            for j, (px, py) in enumerate(chips):
                _remote(mine, mine, send_sems.at[3 * i + j], recv_sems.at[3 * i + j], (px, py, c)).wait_send()

    return pl.pallas_call(
        body, name=name, in_specs=[HBM] * n + [SEM, SEM, ANY], out_specs=[HBM] * n,
        out_shape=[pltpu.HBM(b.shape, b.dtype) for b in bufs],
        input_output_aliases={i: i for i in range(n)}, compiler_params=SPLIT,
    )(*bufs, sems[0], sems[1], after)


def _gather_forward(bufs, axes, shapes, name):
    n = len(bufs)

    def body(*refs):
        dst = refs[n:2 * n]
        send_sems, recv_sems = refs[2 * n:]
        x, y, c, chips = _me()
        sibling = (x, y, 1 - c)
        sends = []
        for i in range(n):
            r, cl = shapes[i]
            for j, (px, py) in enumerate(chips):
                got = _slab(dst[i], axes[i], r, cl, 2 * px + py, c)
                cp = _remote(got, got, send_sems.at[3 * i + j], recv_sems.at[3 * i + j], sibling)
                cp.start()
                sends.append(cp)
        for i in range(n):
            r, cl = shapes[i]
            for j, (px, py) in enumerate(chips):
                got = _slab(dst[i], axes[i], r, cl, 2 * px + py, 1 - c)
                _remote(got, got, send_sems.at[3 * i + j], recv_sems.at[3 * i + j], sibling).wait_recv()
        for cp in sends:
            cp.wait_send()

    return pl.pallas_call(
        body, name=name, in_specs=[ANY] * n, out_specs=[ANY] * n,
        out_shape=[jax.ShapeDtypeStruct(b.shape, b.dtype) for b in bufs],
        input_output_aliases={i: i for i in range(n)},
        scratch_shapes=[pltpu.SemaphoreType.DMA((3 * n,)), pltpu.SemaphoreType.DMA((3 * n,))],
    )(*bufs)


def _pair_exchange(fulls, axes, shapes, tag):
    n = len(fulls)

    def body(*refs):
        src, dst = refs[:n], refs[n:2 * n]
        send_sems, recv_sems = refs[2 * n:]
        x, y, c, _ = _me()
        sibling = (x, y, 1 - c)
        cps = []
        for w in range(n):
            r, cl = shapes[w]
            for j in range(4):
                cp = _remote(_slab(src[w], axes[w], r, cl, j, 1 - c), dst[w].at[j],
                             send_sems.at[4 * w + j], recv_sems.at[4 * w + j], sibling)
                cp.start()
                cps.append(cp)
        for cp in cps:
            cp.wait()

    out_shape = [jax.ShapeDtypeStruct((4, r // 2, cl), f.dtype) for (r, cl), f in zip(shapes, fulls)]
    return pl.pallas_call(
        body, name="reduce_pair_exchange_" + tag, in_specs=[ANY] * n, out_specs=[ANY] * n, out_shape=out_shape,
        scratch_shapes=[pltpu.SemaphoreType.DMA((4 * n,)), pltpu.SemaphoreType.DMA((4 * n,))],
    )(*fulls)


def _chip_start(parts, tag):
    n = len(parts)

    def body(*refs):
        src, land = refs[2 * n:3 * n], refs[3 * n:4 * n]
        send_sems, recv_sems, token = refs[4 * n:]
        x, y, c, chips = _me()
        k = 2 * x + y
        for w in range(n):
            for j, (px, py) in enumerate(chips):
                _remote(src[w].at[2 * px + py], land[w].at[k], send_sems.at[3 * w + j], recv_sems.at[3 * w + j],
                        (px, py, c)).start()
        token[...] = jnp.zeros_like(token)

    lands = [lax.empty(p.shape, p.dtype) for p in parts]
    sem = pltpu.SemaphoreType.DMA((3 * n,))
    outs = pl.pallas_call(
        body, name="reduce_ici_start_" + tag, in_specs=[HBM] * (2 * n),
        out_specs=[HBM] * (2 * n) + [SEM, SEM, pl.BlockSpec(memory_space=pltpu.VMEM)],
        out_shape=[pltpu.HBM(p.shape, p.dtype) for p in parts + lands] + [sem, sem, TOKEN],
        input_output_aliases={i: i for i in range(2 * n)}, compiler_params=SPLIT,
    )(*[_in_hbm(v) for v in parts + lands])
    return outs[:n], outs[n:2 * n], outs[2 * n], outs[2 * n + 1], outs[-1]


def _chip_wait(parts, lands, send_sems, recv_sems, after, tag):
    n = len(parts)

    def body(*refs):
        send, recv = refs[2 * n], refs[2 * n + 1]
        src, land = refs[2 * n + 3:3 * n + 3], refs[3 * n + 3:]
        x, y, c, chips = _me()
        for w in range(n):
            for j, (px, py) in enumerate(chips):
                got = land[w].at[2 * px + py]
                _remote(got, got, send.at[3 * w + j], recv.at[3 * w + j], (px, py, c)).wait_recv()
        for w in range(n):
            for j, (px, py) in enumerate(chips):
                sent = src[w].at[2 * px + py]
                _remote(sent, sent, send.at[3 * w + j], recv.at[3 * w + j], (px, py, c)).wait_send()

    outs = pl.pallas_call(
        body, name="reduce_ici_wait_" + tag, in_specs=[HBM] * (2 * n) + [SEM, SEM, ANY], out_specs=[HBM] * (2 * n),
        out_shape=[pltpu.HBM(p.shape, p.dtype) for p in parts + lands],
        input_output_aliases={i: i for i in range(2 * n)}, compiler_params=SPLIT,
    )(*parts, *lands, send_sems, recv_sems, after)
    chip = 2 * lax.axis_index("x") + lax.axis_index("y")
    return [lax.dynamic_update_slice(s, lax.dynamic_index_in_dim(p, chip, 0, keepdims=True), (chip, 0, 0))
            for p, s in zip(outs[:n], outs[n:])]


def _half_swap(halves, tag):
    n = len(halves)
    core = lax.axis_index("c")
    bufs = [lax.dynamic_update_slice(lax.empty((2,) + h.shape, h.dtype), h[None], (core, 0, 0)) for h in halves]

    def body(*refs):
        dst = refs[n:2 * n]
        send_sems, recv_sems = refs[2 * n:]
        x, y, c, _ = _me()
        sibling = (x, y, 1 - c)
        cps = []
        for w in range(n):
            cp = _remote(dst[w].at[c], dst[w].at[c], send_sems.at[w], recv_sems.at[w], sibling)
            cp.start()
            cps.append(cp)
        for w in range(n):
            other = dst[w].at[1 - c]
            _remote(other, other, send_sems.at[w], recv_sems.at[w], sibling).wait_recv()
        for cp in cps:
            cp.wait_send()

    outs = pl.pallas_call(
        body, name="reduce_half_swap_" + tag, in_specs=[ANY] * n, out_specs=[ANY] * n,
        out_shape=[jax.ShapeDtypeStruct(b.shape, b.dtype) for b in bufs],
        input_output_aliases={w: w for w in range(n)},
        scratch_shapes=[pltpu.SemaphoreType.DMA((n,)), pltpu.SemaphoreType.DMA((n,))],
    )(*bufs)
    return [o.reshape(2 * o.shape[1], o.shape[2]) for o in outs]


def _add_parts(full, axis, rows, sib, name):
    _, r, c = sib.shape
    tr, tc = _tile(r, 256, 16), _tile(c, 2048)
    nb = r // tr
    core = jnp.reshape(lax.axis_index("c"), (1,)).astype(jnp.int32)

    def body(c_ref, a_ref, b_ref, o_ref):
        o_ref[0] = (a_ref[...].astype(F32) + b_ref[0].astype(F32)).astype(BF16)

    blk = pl.BlockSpec((1, tr, tc), lambda j, i, l, cr: (j, i, l))
    return pl.pallas_call(
        body, name=name,
        grid_spec=pltpu.PrefetchScalarGridSpec(
            num_scalar_prefetch=1, grid=(4, nb, c // tc),
            in_specs=[pl.BlockSpec((tr, tc), lambda j, i, l, cr: ((_slot(axis, j) * 2 + cr[0]) * nb + i, l)), blk],
            out_specs=blk),
        out_shape=jax.ShapeDtypeStruct(sib.shape, BF16),
        compiler_params=_cparams(("parallel", "parallel", "parallel")),
    )(core, full, sib)


def _sum_slots(a, name):
    _, r, c = a.shape
    tr, tc = _tile(r, 256, 8), _tile(c, 2048)

    def body(a_ref, o_ref):
        v = a_ref[...].astype(F32)
        o_ref[...] = ((v[0] + v[1]) + v[2]) + v[3]

    return pl.pallas_call(
        body, name=name, grid=(r // tr, c // tc),
        in_specs=[pl.BlockSpec((4, tr, tc), lambda i, l: (0, i, l))],
        out_specs=pl.BlockSpec((tr, tc), lambda i, l: (i, l)),
        out_shape=jax.ShapeDtypeStruct((r, c), F32),
        compiler_params=_cparams(("parallel", "parallel")),
    )(a)


class _Reducer:
    def __init__(self):
        self.pending = []

    def start(self, tag, names, fulls, axes, shapes):
        from_sibling = _pair_exchange(fulls, axes, shapes, tag)
        parts = [_add_parts(f, a, r, s, name=f"reduce_add_{n}")
                 for n, f, a, (r, cl), s in zip(names, fulls, axes, shapes, from_sibling)]
        parts, lands, send, recv, token = _chip_start(parts, tag)
        self.pending.append((tag, names, parts, lands, send, recv))
        return token

    def finish(self, after):
        out = {}
        for tag, names, parts, lands, send, recv in self.pending:
            slots = _chip_wait(parts, lands, send, recv, after, tag)
            halves = [_sum_slots(s, name=f"reduce_sum_{n}") for n, s in zip(names, slots)]
            out.update(zip(names, _half_swap(halves, tag)))
        return out


def _allreduce_small(pack):
    rows = pack.shape[0]

    def body(p_ref, o_ref, slots, send_sems, recv_sems):
        x, y, c, _ = _me()
        me = 4 * x + 2 * y + c
        slots[me] = p_ref[...]
        cps = []
        for r in range(1, 8):
            peer = (x ^ (r >> 2), y ^ ((r >> 1) & 1), c ^ (r & 1))
            cp = _remote(p_ref, slots.at[me], send_sems.at[r - 1], recv_sems.at[r - 1], peer)
            cp.start()
            cps.append(cp)
        for r in range(1, 8):
            frm = me ^ r
            _remote(slots.at[frm], slots.at[frm], send_sems.at[r - 1], recv_sems.at[r - 1], (x, y, c)).wait_recv()
        for cp in cps:
            cp.wait_send()
        acc = slots[0]
        for s in range(1, 8):
            acc = acc + slots[s]
        o_ref[...] = acc

    vm = pl.BlockSpec(memory_space=pltpu.VMEM)
    return pl.pallas_call(
        body, name="allreduce_small", in_specs=[vm], out_specs=vm,
        out_shape=jax.ShapeDtypeStruct(pack.shape, F32),
        scratch_shapes=[pltpu.VMEM((8, rows, HEAD_DIM), F32), pltpu.SemaphoreType.DMA((7,)),
                        pltpu.SemaphoreType.DMA((7,))],
    )(pack)


_ROWS = ["norm_mix", "norm_ffn", "mem_norm", "fox_q_norm", "fox_k_norm", "gdn_out_norm", "mem_q_norm",
         "mem_k_norm", "fox_f_bias", "gdn_a_log", "gdn_dt_bias"]


def _pack_rows(vals):
    out = []
    for name in _ROWS:
        v = vals[name].reshape(-1)
        n = -(-v.shape[0] // HEAD_DIM) * HEAD_DIM
        out.append(jnp.pad(v, (0, n - v.shape[0])).reshape(-1, HEAD_DIM))
    return jnp.concatenate(out, axis=0)


def _unpack_rows(pack, like):
    out, r = {}, 0
    for name in _ROWS:
        n = like[name].shape[-1]
        nr = -(-n // HEAD_DIM)
        out[name] = pack[r:r + nr].reshape(1, -1)[:, :n]
        r += nr
    return out, r


def kernel(x, mem, norm_mix, w_in, fox_f_bias, fox_q_norm, fox_k_norm, gdn_conv, gdn_a_log, gdn_dt_bias, gdn_out_norm, mem_norm, w_mem_kv, mem_q_norm, mem_k_norm, w_out, norm_ffn, w_gate_up, w_down, loss_target, m_norm_mix, m_w_in, m_fox_f_bias, m_fox_q_norm, m_fox_k_norm, m_gdn_conv, m_gdn_a_log, m_gdn_dt_bias, m_gdn_out_norm, m_mem_norm, m_w_mem_kv, m_mem_q_norm, m_mem_k_norm, m_w_out, m_norm_ffn, m_w_gate_up, m_w_down, v_norm_mix, v_w_in, v_fox_f_bias, v_fox_q_norm, v_fox_k_norm, v_gdn_conv, v_gdn_a_log, v_gdn_dt_bias, v_gdn_out_norm, v_mem_norm, v_w_mem_kv, v_mem_q_norm, v_mem_k_norm, v_w_out, v_norm_ffn, v_w_gate_up, v_w_down):
    a = dict(locals())
    d = x.shape[-1]
    lay = _Layout(d)
    chip = 2 * lax.axis_index("x") + lax.axis_index("y")
    small = {n: a[n] for n in _ROWS}
    big = ["w_in", "w_mem_kv", "w_out", "w_gate_up", "w_down"]
    axes = [0, 0, 0, 1, 0]

    conv_cols = gdn_conv.shape[-1]
    conv_n = CONV_WIDTH * conv_cols
    conv_rows = -(-conv_n // HEAD_DIM)
    conv_blk = jnp.pad(gdn_conv.reshape(-1), (0, 32 * HEAD_DIM - conv_n)).reshape(32, HEAD_DIM)
    shards = [lay.regroup(w_in[0]), w_mem_kv[0], w_out[0], w_gate_up[0], w_down[0]]
    placed = [_cast_place(s, ax, "cast_" + n) for s, ax, n in zip(shards, axes, big)]
    chip_row = chip * 32
    placed.append(lax.dynamic_update_slice(lax.empty((4 * 32, HEAD_DIM), F32), conv_blk, (chip_row, 0)))
    all_axes = axes + [0]
    all_shapes = [s.shape for s in shards] + [conv_blk.shape]
    early, mid, late = [0], [1, 5], [2, 3, 4]
    bufs, sems, token = _gather_start(placed, all_axes, all_shapes, [early, mid, late])
    pick = lambda seq, idx: [seq[i] for i in idx]

    def arrive(idx, sem_pair, after, tag):
        got = _gather_wait(pick(bufs, idx), pick(all_axes, idx), pick(all_shapes, idx), sem_pair, after,
                           "gather_ici_wait_" + tag)
        return _gather_forward(got, pick(all_axes, idx), pick(all_shapes, idx), "gather_forward_" + tag)

    (win,) = arrive(early, sems[0], token, "early")

    def mid_weights(after):
        wmkv, conv_all = arrive(mid, sems[1], after, "mid")
        taps = conv_all.reshape(4, 32 * HEAD_DIM)[:, :conv_n].reshape(4, CONV_WIDTH, conv_cols)
        return wmkv, jnp.transpose(taps, (1, 0, 2)).reshape(CONV_WIDTH, 4 * conv_cols)

    sp = dict(small)
    reducer = _Reducer()
    spec = {n: (ax, s.shape) for n, ax, s in zip(big, axes, shards)}

    def reduce_start(tag, grads):
        names = list(grads)
        return reducer.start(tag, names, [grads[n] for n in names], [spec[n][0] for n in names],
                             [spec[n][1] for n in names])

    loss_blk, dx, g = _local_step(x[0], mem[0], loss_target[0], win, mid_weights,
                                  lambda after: arrive(late, sems[2], after, "late"), reduce_start, sp)

    gsmall = {n: g[n] for n in _ROWS}
    pack = jnp.concatenate([_pack_rows(gsmall), g["gdn_conv"].reshape(-1, HEAD_DIM), loss_blk], axis=0)
    pack = jnp.pad(pack, ((0, -pack.shape[0] % 8), (0, 0)))
    tot = _allreduce_small(pack)
    gs, r0 = _unpack_rows(tot, small)
    conv_g = tot[r0:r0 + CONV_WIDTH * 4 * conv_cols // HEAD_DIM].reshape(CONV_WIDTH, 4 * conv_cols)
    gs_conv = lax.dynamic_slice_in_dim(conv_g, chip * conv_cols, conv_cols, axis=1)
    loss = tot[r0 + CONV_WIDTH * 4 * conv_cols // HEAD_DIM, 0]
    reduced = reducer.finish(tot)
    reduced["w_in"] = lay.ungroup(reduced["w_in"])

    out = {"loss": loss, "grad_x": dx[None]}
    for n, gsh in reduced.items():
        res = _adamw(a[n][0], gsh, a["m_" + n][0], a["v_" + n][0], name="adamw_" + n)
        for pre, r in zip(["grad_", "delta_", "new_m_", "new_v_"], res):
            out[pre + n] = r[None]
    conv_pad = lambda v: jnp.pad(v.reshape(-1), (0, conv_rows * HEAD_DIM - conv_n)).reshape(conv_rows, HEAD_DIM)
    packs = []
    for src, cv in [(small, gdn_conv), (gs, gs_conv), ({n: a["m_" + n] for n in _ROWS}, m_gdn_conv),
                    ({n: a["v_" + n] for n in _ROWS}, v_gdn_conv)]:
        packs.append(jnp.concatenate([_pack_rows(src), conv_pad(cv)], axis=0))
    res = _adamw(*packs, name="adamw_small")
    for pre, r in zip(["grad_", "delta_", "new_m_", "new_v_"], res):
        vals, r1 = _unpack_rows(r, small)
        for n in _ROWS:
            out[pre + n] = vals[n]
        out[pre + "gdn_conv"] = r[r1:r1 + conv_rows].reshape(-1)[:conv_n].reshape(gdn_conv.shape)
    names = ["norm_mix", "w_in", "fox_f_bias", "fox_q_norm", "fox_k_norm", "gdn_conv", "gdn_a_log", "gdn_dt_bias",
             "gdn_out_norm", "mem_norm", "w_mem_kv", "mem_q_norm", "mem_k_norm", "w_out", "norm_ffn", "w_gate_up",
             "w_down"]
    return (out["loss"], out["grad_x"], *[out[p + n] for p in ["grad_", "delta_", "new_m_", "new_v_"] for n in names])
```

```python
import functools
import math

import jax
import jax.numpy as jnp
from jax import lax
from jax.experimental import pallas as pl
from jax.experimental.pallas import tpu as pltpu

F32, BF16 = jnp.float32, jnp.bfloat16
HEAD_DIM = 128
CHUNK = 64
N_MEM_HEADS = 4
CONV_WIDTH = 4
NORM_EPS = 1e-6
ADAM_LR, ADAM_B1, ADAM_B2, ADAM_EPS, ADAM_WD, ADAM_STEP = 0.001, 0.9, 0.999, 1e-08, 0.01, 10
VMEM_LIMIT = 48 * 1024 * 1024
NEG = -1e30
MESH = pl.DeviceIdType.MESH


def _cparams(sem=None, **kw):
    if sem is not None:
        kw["dimension_semantics"] = sem
    return pltpu.CompilerParams(vmem_limit_bytes=VMEM_LIMIT, **kw)


def _tile(n, target, mult=128):
    best = None
    d = mult
    while d <= min(n, target):
        if n % d == 0:
            best = d
        d += mult
    return best if best is not None else n


def _dot(a, b, dims, hi):
    if a.ndim == 3:
        dn = (((dims[0][0] + 1,), (dims[1][0] + 1,)), ((0,), (0,)))
    else:
        dn = (dims, ((), ()))
    if hi is not None:
        return lax.dot_general(a, b, dn, precision=hi, preferred_element_type=F32)
    return lax.dot_general(a.astype(BF16), b.astype(BF16), dn, preferred_element_type=F32)


def _make_dots(hi, cotangent=None):
    @jax.custom_vjp
    def nn(a, b):
        return _dot(a, b, ((1,), (0,)), hi)

    @jax.custom_vjp
    def nt(a, b):
        return _dot(a, b, ((1,), (1,)), hi)

    @jax.custom_vjp
    def tn(a, b):
        return _dot(a, b, ((0,), (0,)), hi)

    bnn, bnt, btn = cotangent or (nn, nt, tn)
    nn.defvjp(lambda a, b: (nn(a, b), (a, b)), lambda r, g: (bnt(g, r[1]), btn(r[0], g)))
    nt.defvjp(lambda a, b: (nt(a, b), (a, b)), lambda r, g: (bnn(g, r[1]), btn(g, r[0])))
    tn.defvjp(lambda a, b: (tn(a, b), (a, b)), lambda r, g: (bnt(r[1], g), bnn(r[0], g)))
    return nn, nt, tn


_nn, _nt, _tn = _make_dots(None)
_nn_hi, _nt_hi, _tn_hi = _make_dots(lax.Precision.HIGHEST)
_nn_x3, _nt_x3, _tn_x3 = _make_dots(lax.Precision.HIGH, (_nn, _nt, _tn))


def _sigmoid(x):
    return 1.0 / (1.0 + jnp.exp(-x))


@jax.custom_vjp
def _softplus(x):
    return jnp.maximum(x, 0.0) + jnp.log(1.0 + jnp.exp(-jnp.abs(x)))


_softplus.defvjp(lambda x: (_softplus(x), x), lambda x, g: (g * _sigmoid(x),))


def _silu(x):
    return x * _sigmoid(x)


def _rms_fn(x, gain, z=None):
    y = x * lax.rsqrt(jnp.mean(x * x, axis=-1, keepdims=True) + NORM_EPS) * gain
    if z is not None:
        y = y * _silu(z)
    return y


def _mm(a, b, *, ta=False, tb=False, out_dtype=F32, res=None, stack=None, after=None, name):
    a2, b2 = a.shape[-2:], b.shape[-2:]
    ns = b.shape[0] if stack else 1
    m = a2[1] if ta else a2[0]
    k = a2[0] if ta else a2[1]
    n = b2[0] if tb else b2[1]
    assert k == (b2[1] if tb else b2[0])
    tm, tn, tk = _mm_tiles(m, n, k, ns if stack == "sum" else 1, a.dtype.itemsize, b.dtype.itemsize,
                           jnp.dtype(out_dtype).itemsize, res is not None)
    nk = k // tk
    single = nk == 1 and stack != "sum"
    dims = ((0 if ta else 1,), (1 if tb else 0,))
    if stack == "sum":
        order = lambda g0, g1, g2, g3: (g2, g0, g1, g3)
        grid = (m // tm, n // tn, ns, nk)
    else:
        order = lambda g0, g1, g2, g3: (g0, g1, g2, g3)
        grid = (ns, m // tm, n // tn, nk)

    def body(*refs):
        if after is not None:
            refs = refs[:2 + (res is not None)] + refs[3 + (res is not None):]
        if single:
            a_ref, b_ref = refs[:2]
            r = lax.dot_general(a_ref[...].astype(BF16), b_ref[...].astype(BF16), (dims, ((), ())),
                                preferred_element_type=F32)
            if res is not None:
                r = r + refs[2][...]
            refs[-1][...] = r.astype(out_dtype)
            return
        if res is None:
            a_ref, b_ref, o_ref, acc = refs
        else:
            a_ref, b_ref, r_ref, o_ref, acc = refs
        s, _, _, kk = order(*[pl.program_id(d) for d in range(4)])
        first = kk == 0
        last = kk == nk - 1
        if stack == "sum":
            first, last = first & (s == 0), last & (s == ns - 1)

        @pl.when(first)
        def _():
            acc[...] = jnp.zeros_like(acc)

        acc[...] += lax.dot_general(a_ref[...].astype(BF16), b_ref[...].astype(BF16), (dims, ((), ())),
                                    preferred_element_type=F32)

        @pl.when(last)
        def _():
            r = acc[...]
            if res is not None:
                r = r + r_ref[...]
            o_ref[...] = r.astype(out_dtype)

    def spec(shape, idx, stacked):
        if stacked:
            return pl.BlockSpec((None,) + shape, lambda *g: (order(*g)[0],) + idx(*order(*g)))
        return pl.BlockSpec(shape, lambda *g: idx(*order(*g)))

    a_spec = (spec((tk, tm), lambda s, i, j, kk: (kk, i), stack == "sum") if ta
              else spec((tm, tk), lambda s, i, j, kk: (i, kk), stack == "sum"))
    b_spec = (spec((tn, tk), lambda s, i, j, kk: (j, kk), bool(stack)) if tb
              else spec((tk, tn), lambda s, i, j, kk: (kk, j), bool(stack)))
    o_spec = spec((tm, tn), lambda s, i, j, kk: (i, j), stack == "out")
    ins, specs = [a, b], [a_spec, b_spec]
    if res is not None:
        ins.append(res)
        specs.append(o_spec)
    if after is not None:
        ins.append(after)
        specs.append(pl.BlockSpec(after.shape, lambda *g: (0,) * after.ndim))
    sem = (("parallel", "parallel", "arbitrary", "arbitrary") if stack == "sum"
           else ("parallel", "parallel", "parallel", "arbitrary"))
    return pl.pallas_call(
        body, name=name, grid=grid, in_specs=specs, out_specs=o_spec,
        out_shape=jax.ShapeDtypeStruct(((ns,) if stack == "out" else ()) + (m, n), out_dtype),
        scratch_shapes=[] if single else [pltpu.VMEM((tm, tn), F32)],
        compiler_params=_cparams(sem),
    )(*ins)


MM_VMEM_BUDGET = 40 * 1024 * 1024


def _mm_tiles(m, n, k, ns, sa, sb, so, has_res):
    def divs(x, mult, cap):
        out = [d for d in range(mult, min(x, cap) + 1, mult) if x % d == 0]
        return out or [x]

    best = None
    for tk in divs(k, 128, 8192):
        nk = (k // tk) * ns
        for tm in divs(m, 8, 2048):
            for tn in divs(n, 128, 2048):
                vmem = 2 * (tm * tk * sa + tk * tn * sb + tm * tn * so) + (2 * tm * tn * 4 if has_res else 0)
                vmem += tm * tn * 4 if nk > 1 else 0
                if vmem > MM_VMEM_BUDGET:
                    continue
                steps = (m // tm) * (n // tn) * nk
                traffic = (m // tm) * k * n * sb * ns + (n // tn if nk > 1 else 1) * m * k * sa * ns
                cost = steps * 0.4e-6 + traffic / 2.5e12 + (nk * m * n * 8 / 6e12 if nk > 1 else 0)
                cost += 2.0 * m * n * k * ns / 7e14
                if best is None or cost < best[0]:
                    best = (cost, tm, tn, tk)
    return best[1:]


def _norm_fwd(x, xoff, gain, ncol, w, out_dtype, *, z=None, zoff=0, name):
    t = x.shape[0]
    tr = _tile(t, max(256, (1 << 18) // w), 8)

    def body(*refs):
        if z is None:
            x_ref, g_ref, o_ref = refs
            y = _rms_fn(x_ref[...], g_ref[...])
        else:
            x_ref, g_ref, z_ref, o_ref = refs
            y = _rms_fn(x_ref[...], g_ref[...], z_ref[...])
        o_ref[...] = y.astype(out_dtype)

    ins = [x, gain]
    specs = [pl.BlockSpec((tr, w), lambda j, r: (r, xoff + j)), pl.BlockSpec((1, w), lambda j, r: (0, 0))]
    if z is not None:
        ins.append(z)
        specs.append(pl.BlockSpec((tr, w), lambda j, r: (r, zoff + j)))
    return pl.pallas_call(
        body, name=name, grid=(ncol, t // tr), in_specs=specs,
        out_specs=pl.BlockSpec((tr, w), lambda j, r: (r, j)),
        out_shape=jax.ShapeDtypeStruct((t, ncol * w), out_dtype),
        compiler_params=_cparams(("parallel", "parallel")),
    )(*ins)


def _norm_bwd(x, xoff, gain, dy, dyoff, ncol, w, *, z=None, zoff=0, res=None, name):
    t = x.shape[0]
    tr = _tile(t, max(256, (1 << 18) // w), 8)

    def body(*refs):
        it = iter(refs)
        x_ref, g_ref = next(it), next(it)
        z_ref = next(it) if z is not None else None
        dy_ref = next(it)
        r_ref = next(it) if res is not None else None
        dx_ref = next(it)
        dz_ref = next(it) if z is not None else None
        dg_ref = next(it)

        @pl.when((pl.program_id(0) == 0) & (pl.program_id(1) == 0))
        def _():
            dg_ref[...] = jnp.zeros_like(dg_ref)

        args = (x_ref[...], g_ref[...]) + ((z_ref[...],) if z is not None else ())
        _, vjp = jax.vjp(_rms_fn, *args)
        grads = vjp(dy_ref[...].astype(F32))
        dx = grads[0]
        if res is not None:
            dx = dx + r_ref[...]
        dx_ref[...] = dx
        if z is not None:
            dz_ref[...] = grads[2]
        dg_ref[...] += grads[1]

    ins = [x, gain]
    specs = [pl.BlockSpec((tr, w), lambda j, r: (r, xoff + j)), pl.BlockSpec((1, w), lambda j, r: (0, 0))]
    if z is not None:
        ins.append(z)
        specs.append(pl.BlockSpec((tr, w), lambda j, r: (r, zoff + j)))
    ins.append(dy)
    specs.append(pl.BlockSpec((tr, w), lambda j, r: (r, dyoff + j)))
    blk = pl.BlockSpec((tr, w), lambda j, r: (r, j))
    if res is not None:
        ins.append(res)
        specs.append(blk)
    full = jax.ShapeDtypeStruct((t, ncol * w), F32)
    out_shape, out_specs = [full], [blk]
    if z is not None:
        out_shape.append(full)
        out_specs.append(blk)
    out_shape.append(jax.ShapeDtypeStruct((1, w), F32))
    out_specs.append(pl.BlockSpec((1, w), lambda j, r: (0, 0)))
    return pl.pallas_call(
        body, name=name, grid=(ncol, t // tr), in_specs=specs, out_specs=out_specs, out_shape=out_shape,
        compiler_params=_cparams(("arbitrary", "arbitrary")),
    )(*ins)


def _small_fn(x, pa, pb, nf, ng):
    lane = lax.broadcasted_iota(jnp.int32, x.shape, 1)
    zz = x + pb
    logf = -_softplus(-zz)
    g = -jnp.exp(pa) * _softplus(zz)
    beta = _sigmoid(x)
    return jnp.where(lane < nf, logf, jnp.where(lane < nf + ng, g, beta))


def _tri(n, upper):
    r = lax.broadcasted_iota(jnp.int32, (n, n), 0)
    c = lax.broadcasted_iota(jnp.int32, (n, n), 1)
    return jnp.where((c >= r) if upper else (c <= r), 1.0, 0.0).astype(F32)


def _small_fwd(p, off, pa, pb, nf, ng):
    t = p.shape[0]
    blk = HEAD_DIM
    nb = t // blk

    def body(x_ref, pa_ref, pb_ref, v_ref, c_ref):
        v_ref[...] = _small_fn(x_ref[...], pa_ref[...], pb_ref[...], nf, ng)
        tri = _tri(blk, False)

        carry = jnp.zeros((1, HEAD_DIM), F32)
        for i in range(nb):
            rows = slice(i * blk, (i + 1) * blk)
            c = _nn_hi(tri, v_ref[rows, :]) + carry
            c_ref[rows, :] = c
            carry = c[blk - 1:blk, :]

    row = pl.BlockSpec((1, HEAD_DIM), lambda i: (0, 0))
    out = pl.BlockSpec((t, HEAD_DIM), lambda i: (0, 0))
    return pl.pallas_call(
        body, name="small_fwd", grid=(1,),
        in_specs=[pl.BlockSpec((t, HEAD_DIM), lambda i: (0, off)), row, row], out_specs=[out, out],
        out_shape=[jax.ShapeDtypeStruct((t, HEAD_DIM), F32)] * 2,
        compiler_params=_cparams(("arbitrary",)),
    )(p, pa, pb)


def _small_bwd(p, off, pa, pb, dvals, dcsum, nf, ng):
    t = p.shape[0]
    blk = HEAD_DIM
    nb = t // blk

    def body(x_ref, pa_ref, pb_ref, dv_ref, dc_ref, dx_ref, dpa_ref, dpb_ref, tot_ref):
        tri = _tri(blk, True)

        carry = jnp.zeros((1, HEAD_DIM), F32)
        for i in reversed(range(nb)):
            rows = slice(i * blk, (i + 1) * blk)
            c = _nn_hi(tri, dc_ref[rows, :]) + carry
            tot_ref[rows, :] = c + dv_ref[rows, :]
            carry = c[0:1, :]
        f = functools.partial(_small_fn, nf=nf, ng=ng)
        _, vjp = jax.vjp(f, x_ref[...], pa_ref[...], pb_ref[...])
        dx, dpa, dpb = vjp(tot_ref[...])
        dx_ref[...] = dx
        dpa_ref[...] = dpa
        dpb_ref[...] = dpb

    row = pl.BlockSpec((1, HEAD_DIM), lambda i: (0, 0))
    full = pl.BlockSpec((t, HEAD_DIM), lambda i: (0, 0))
    return pl.pallas_call(
        body, name="small_bwd", grid=(1,),
        in_specs=[pl.BlockSpec((t, HEAD_DIM), lambda i: (0, off)), row, row, full, full],
        out_specs=[full, row, row],
        out_shape=[jax.ShapeDtypeStruct((t, HEAD_DIM), F32), jax.ShapeDtypeStruct((1, HEAD_DIM), F32),
                   jax.ShapeDtypeStruct((1, HEAD_DIM), F32)],
        scratch_shapes=[pltpu.VMEM((t, HEAD_DIM), F32)],
        compiler_params=_cparams(("arbitrary",)),
    )(p, pa, pb, dvals, dcsum)


def _fox_fwd(q, k, v, cc, cr, nf, tq, tk):
    t = q.shape[0]
    scale = HEAD_DIM ** -0.5
    ratio = tq // tk

    def body(q_ref, k_ref, v_ref, cc_ref, cr_ref, o_ref, lse_ref):
        i = pl.program_id(1)
        qv = q_ref[...]
        ccol = cc_ref[0]
        rows = i * tq + lax.broadcasted_iota(jnp.int32, (tq, tk), 0)
        cols0 = lax.broadcasted_iota(jnp.int32, (tq, tk), 1)

        def step(j, carry):
            m, l, acc = carry
            ks = pl.ds(pl.multiple_of(j * tk, tk), tk)
            s = lax.dot_general(qv, k_ref[ks, :], (((1,), (1,)), ((), ())), preferred_element_type=F32) * scale
            s = s + ccol - cr_ref[0, j]
            s = jnp.where(cols0 + j * tk <= rows, s, NEG)
            m_new = jnp.maximum(m, jnp.max(s, axis=1, keepdims=True))
            pr = jnp.exp(s - m_new)
            alpha = jnp.exp(m - m_new)
            l = alpha * l + jnp.sum(pr, axis=1, keepdims=True)
            acc = alpha * acc + jnp.dot(pr.astype(BF16), v_ref[ks, :], preferred_element_type=F32)
            return m_new, l, acc

        init = (jnp.full((tq, 1), NEG, F32), jnp.zeros((tq, 1), F32), jnp.zeros((tq, HEAD_DIM), F32))
        m, l, acc = lax.fori_loop(0, (i + 1) * ratio, step, init)
        o_ref[...] = acc / l
        lse_ref[0] = m + jnp.log(l)

    head_all = pl.BlockSpec((t, HEAD_DIM), lambda h, i: (0, h))
    return pl.pallas_call(
        body, name="fox_fwd", grid=(nf, t // tq),
        in_specs=[pl.BlockSpec((tq, HEAD_DIM), lambda h, i: (i, h)), head_all, head_all,
                  pl.BlockSpec((1, tq, 1), lambda h, i: (h, i, 0)),
                  pl.BlockSpec((1, t // tk, 1, tk), lambda h, i: (h, 0, 0, 0))],
        out_specs=[pl.BlockSpec((tq, HEAD_DIM), lambda h, i: (i, h)),
                   pl.BlockSpec((1, tq, 1), lambda h, i: (h, i, 0))],
        out_shape=[jax.ShapeDtypeStruct((t, nf * HEAD_DIM), F32), jax.ShapeDtypeStruct((nf, t, 1), F32)],
        compiler_params=_cparams(("parallel", "parallel")),
    )(q, k, v, cc, cr)


def _fox_bwd(q, k, v, cc, cr, o, lse, dmix, nf, tq, tk):
    t = q.shape[0]
    scale = HEAD_DIM ** -0.5
    ratio = tq // tk

    def body(q_ref, k_ref, v_ref, cc_ref, cr_ref, o_ref, lse_ref, do_ref,
             dq_ref, dk_ref, dv_ref, dcc_ref, dcr_ref):
        i = pl.program_id(1)

        @pl.when(i == 0)
        def _():
            dk_ref[...] = jnp.zeros_like(dk_ref)
            dv_ref[...] = jnp.zeros_like(dv_ref)
            dcr_ref[...] = jnp.zeros_like(dcr_ref)

        qv = q_ref[...]
        ccol = cc_ref[0]
        lse_v = lse_ref[0]
        do = do_ref[...]
        do_b = do.astype(BF16)
        delta = jnp.sum(do * o_ref[...], axis=1, keepdims=True)
        rows = i * tq + lax.broadcasted_iota(jnp.int32, (tq, tk), 0)
        cols0 = lax.broadcasted_iota(jnp.int32, (tq, tk), 1)

        def step(j, carry):
            dq, dcc = carry
            ks = pl.ds(pl.multiple_of(j * tk, tk), tk)
            kj, vj = k_ref[ks, :], v_ref[ks, :]
            s = lax.dot_general(qv, kj, (((1,), (1,)), ((), ())), preferred_element_type=F32) * scale
            s = s + ccol - cr_ref[0, j]
            pr = jnp.where(cols0 + j * tk <= rows, jnp.exp(s - lse_v), 0.0)
            dp = lax.dot_general(do_b, vj, (((1,), (1,)), ((), ())), preferred_element_type=F32)
            ds = pr * (dp - delta)
            ds_b = ds.astype(BF16)
            dq = dq + jnp.dot(ds_b, kj, preferred_element_type=F32) * scale
            dk_ref[ks, :] += lax.dot_general(ds_b, qv, (((0,), (0,)), ((), ())),
                                             preferred_element_type=F32) * scale
            dv_ref[ks, :] += lax.dot_general(pr.astype(BF16), do_b, (((0,), (0,)), ((), ())),
                                             preferred_element_type=F32)
            dcr_ref[0, j] -= jnp.sum(ds, axis=0, keepdims=True)
            return dq, dcc + jnp.sum(ds, axis=1, keepdims=True)

        init = (jnp.zeros((tq, HEAD_DIM), F32), jnp.zeros((tq, 1), F32))
        dq, dcc = lax.fori_loop(0, (i + 1) * ratio, step, init)
        dq_ref[...] = dq
        dcc_ref[0] = dcc

    head_all = pl.BlockSpec((t, HEAD_DIM), lambda h, i: (0, h))
    qblk = pl.BlockSpec((tq, HEAD_DIM), lambda h, i: (i, h))
    colv = pl.BlockSpec((1, tq, 1), lambda h, i: (h, i, 0))
    rowv = pl.BlockSpec((1, t // tk, 1, tk), lambda h, i: (h, 0, 0, 0))
    wide = jax.ShapeDtypeStruct((t, nf * HEAD_DIM), F32)
    return pl.pallas_call(
        body, name="fox_bwd", grid=(nf, t // tq),
        in_specs=[qblk, head_all, head_all, colv, rowv, qblk, colv, qblk],
        out_specs=[qblk, head_all, head_all, colv, rowv],
        out_shape=[wide, wide, wide, jax.ShapeDtypeStruct((nf, t, 1), F32),
                   jax.ShapeDtypeStruct((nf, t // tk, 1, tk), F32)],
        compiler_params=_cparams(("parallel", "arbitrary")),
    )(q, k, v, cc, cr, o, lse, dmix)


def _mem_fn(mq, mk, mv, gq, gk):
    qn = _rms_fn(mq, gq)
    kn = _rms_fn(mk, gk)
    s = _nt(qn, kn) * (HEAD_DIM ** -0.5)
    e = jnp.exp(s - lax.stop_gradient(jnp.max(s, axis=1, keepdims=True)))
    pr = e / jnp.sum(e, axis=1, keepdims=True)
    return _nn(pr, mv)


def _mem_specs(t, m, tq, qoff):
    qblk = pl.BlockSpec((tq, HEAD_DIM), lambda h, i: (i, qoff + h))
    kblk = pl.BlockSpec((m, HEAD_DIM), lambda h, i: (0, h))
    vblk = pl.BlockSpec((m, HEAD_DIM), lambda h, i: (0, N_MEM_HEADS + h))
    row = pl.BlockSpec((1, HEAD_DIM), lambda h, i: (0, 0))
    return qblk, kblk, vblk, row


def _mem_fwd(p, qoff, mkv, gq, gk, tq):
    t, m = p.shape[0], mkv.shape[0]
    qblk, kblk, vblk, row = _mem_specs(t, m, tq, qoff)

    def body(q_ref, k_ref, v_ref, gq_ref, gk_ref, o_ref):
        o_ref[...] = _mem_fn(q_ref[...], k_ref[...], v_ref[...], gq_ref[...], gk_ref[...])

    return pl.pallas_call(
        body, name="mem_fwd", grid=(N_MEM_HEADS, t // tq), in_specs=[qblk, kblk, vblk, row, row],
        out_specs=pl.BlockSpec((tq, HEAD_DIM), lambda h, i: (i, h)),
        out_shape=jax.ShapeDtypeStruct((t, N_MEM_HEADS * HEAD_DIM), F32),
        compiler_params=_cparams(("parallel", "parallel")),
    )(p, mkv, mkv, gq, gk)


def _mem_bwd(p, qoff, mkv, gq, gk, dmix, dooff, tq):
    t, m = p.shape[0], mkv.shape[0]
    qblk, kblk, vblk, row = _mem_specs(t, m, tq, qoff)

    def body(q_ref, k_ref, v_ref, gq_ref, gk_ref, do_ref, dq_ref, dkv_k_ref, dkv_v_ref, dgq_ref, dgk_ref):
        h, i = pl.program_id(0), pl.program_id(1)

        @pl.when((h == 0) & (i == 0))
        def _():
            dgq_ref[...] = jnp.zeros_like(dgq_ref)
            dgk_ref[...] = jnp.zeros_like(dgk_ref)

        @pl.when(i == 0)
        def _():
            dkv_k_ref[...] = jnp.zeros_like(dkv_k_ref)
            dkv_v_ref[...] = jnp.zeros_like(dkv_v_ref)

        _, vjp = jax.vjp(_mem_fn, q_ref[...], k_ref[...], v_ref[...], gq_ref[...], gk_ref[...])
        dq, dk, dv, dgq, dgk = vjp(do_ref[...])
        dq_ref[...] = dq
        dkv_k_ref[...] += dk
        dkv_v_ref[...] += dv
        dgq_ref[...] += dgq
        dgk_ref[...] += dgk

    oblk = pl.BlockSpec((tq, HEAD_DIM), lambda h, i: (i, h))
    kout = pl.BlockSpec((m, HEAD_DIM), lambda h, i: (0, h))
    half = jax.ShapeDtypeStruct((m, N_MEM_HEADS * HEAD_DIM), F32)
    rshape = jax.ShapeDtypeStruct((1, HEAD_DIM), F32)
    return pl.pallas_call(
        body, name="mem_bwd", grid=(N_MEM_HEADS, t // tq),
        in_specs=[qblk, kblk, vblk, row, row, pl.BlockSpec((tq, HEAD_DIM), lambda h, i: (i, dooff + h))],
        out_specs=[oblk, kout, kout, row, row],
        out_shape=[jax.ShapeDtypeStruct((t, N_MEM_HEADS * HEAD_DIM), F32), half, half, rshape, rshape],
        compiler_params=_cparams(("arbitrary", "arbitrary")),
    )(p, mkv, mkv, gq, gk, dmix)


def _shift_down(x, s):
    if s == 0:
        return x
    r = lax.broadcasted_iota(jnp.int32, x.shape, 0)
    return jnp.where(r >= s, pltpu.roll(x, s, 0), 0.0)


def _shift_up(x, s):
    if s == 0:
        return x
    n = x.shape[0]
    r = lax.broadcasted_iota(jnp.int32, x.shape, 0)
    return jnp.where(r < n - s, pltpu.roll(x, n - s, 0), 0.0)


def _conv_fn(x0, x1, x2, x3, w0, w1, w2, w3, kind):
    y = _silu(x0 * w0 + x1 * w1 + x2 * w2 + x3 * w3)
    if kind == 2:
        return y
    y = y * lax.rsqrt(jnp.sum(y * y, axis=-1, keepdims=True) + NORM_EPS)
    return y * (HEAD_DIM ** -0.5) if kind == 0 else y


def _conv_fwd(p, off, conv_w, ng):
    t = p.shape[0]

    def body(x_ref, w_ref, o_ref):
        kind = pl.program_id(0) // ng
        x = x_ref[...]
        xs = [_shift_down(x, CONV_WIDTH - 1 - j) for j in range(CONV_WIDTH)]
        ws = [w_ref[j:j + 1, :] for j in range(CONV_WIDTH)]
        for kd in range(3):
            @pl.when(kind == kd)
            def _(kd=kd):
                o_ref[...] = _conv_fn(*xs, *ws, kd)

    return pl.pallas_call(
        body, name="gdn_conv_fwd", grid=(3 * ng,),
        in_specs=[pl.BlockSpec((t, HEAD_DIM), lambda c: (0, off + c)),
                  pl.BlockSpec((CONV_WIDTH, HEAD_DIM), lambda c: (0, c))],
        out_specs=pl.BlockSpec((t, HEAD_DIM), lambda c: (0, c)),
        out_shape=jax.ShapeDtypeStruct((t, 3 * ng * HEAD_DIM), F32),
        compiler_params=_cparams(("parallel",)),
    )(p, conv_w)


def _conv_bwd(p, off, conv_w, dy, ng):
    t = p.shape[0]

    def body(x_ref, w_ref, dy_ref, dx_ref, dw_ref):
        kind = pl.program_id(0) // ng
        x = x_ref[...]
        xs = [_shift_down(x, CONV_WIDTH - 1 - j) for j in range(CONV_WIDTH)]
        ws = [w_ref[j:j + 1, :] for j in range(CONV_WIDTH)]
        for kd in range(3):
            @pl.when(kind == kd)
            def _(kd=kd):
                _, vjp = jax.vjp(functools.partial(_conv_fn, kind=kd), *xs, *ws)
                g = vjp(dy_ref[...])
                dx = _shift_up(g[0], CONV_WIDTH - 1)
                for j in range(1, CONV_WIDTH):
                    dx = dx + _shift_up(g[j], CONV_WIDTH - 1 - j)
                dx_ref[...] = dx
                for j in range(CONV_WIDTH):
                    dw_ref[j:j + 1, :] = g[CONV_WIDTH + j]

    blk = pl.BlockSpec((t, HEAD_DIM), lambda c: (0, c))
    wblk = pl.BlockSpec((CONV_WIDTH, HEAD_DIM), lambda c: (0, c))
    return pl.pallas_call(
        body, name="gdn_conv_bwd", grid=(3 * ng,),
        in_specs=[pl.BlockSpec((t, HEAD_DIM), lambda c: (0, off + c)), wblk, blk],
        out_specs=[blk, wblk],
        out_shape=[jax.ShapeDtypeStruct((t, 3 * ng * HEAD_DIM), F32),
                   jax.ShapeDtypeStruct((CONV_WIDTH, 3 * ng * HEAD_DIM), F32)],
        compiler_params=_cparams(("parallel",)),
    )(p, conv_w, dy)


def _wy_fn(q, k, v, gcol, grow, bcol):
    b, c, dk = q.shape
    r = lax.broadcasted_iota(jnp.int32, (1, c, c), 1)
    e = lax.broadcasted_iota(jnp.int32, (1, c, c), 2)
    tril, strict = e <= r, e < r
    gc_col = jnp.sum(jnp.where(tril, grow, 0.0), axis=2, keepdims=True)
    gc_row = jnp.sum(jnp.where(r <= e, gcol, 0.0), axis=1, keepdims=True)
    g_last = jnp.sum(gcol, axis=1, keepdims=True)
    decay = jnp.exp(jnp.where(tril, gc_col - gc_row, NEG))
    kb, vb = k * bcol, v * bcol
    lower = jnp.where(strict, _nt(kb, k) * decay, 0.0)
    inv = jnp.where(r == e, 1.0, 0.0) - lower
    pw = lower
    for _ in range(int(math.log2(c)) - 1):
        pw = _nn_x3(pw, pw)
        inv = inv + _nn_x3(inv, pw)
    u = _nn_x3(inv, vb)
    w = _nn_x3(inv, kb * jnp.exp(gc_col))
    attn = jnp.where(tril, _nt(q, k) * decay, 0.0)
    qg = q * jnp.exp(gc_col)
    kdec = k * jnp.exp(g_last - gc_col)
    egl = jnp.broadcast_to(jnp.exp(g_last), (b, 1, dk))
    return u, w, qg, kdec, attn, egl


def _scan_fn(u, w, qg, kdec, attn, egl, state):
    v_new = u - _nn(w, state)
    o = _nn(qg, state) + _nn(attn, v_new)
    return o, state * egl + _tn(kdec, v_new)


GDN_CHUNKS_PER_STEP = 4


def _gdn_fwd(qkv, gcol, grow, bcol, ng):
    t = qkv.shape[0]
    nch = t // CHUNK

    cb = GDN_CHUNKS_PER_STEP
    wy = _gdn_wy(qkv, gcol, grow, bcol, ng, cb)

    def body(u_ref, w_ref, qg_ref, kd_ref, at_ref, eg_ref, o_ref, st_ref, state):
        @pl.when(pl.program_id(0) == 0)
        def _():
            state[...] = jnp.zeros_like(state)

        st_ref[:, 0] = state[...]
        heads = lambda ref: jnp.stack([ref[:, h * HEAD_DIM:(h + 1) * HEAD_DIM] for h in range(ng)])
        o, new = _scan_fn(heads(u_ref), heads(w_ref), heads(qg_ref), heads(kd_ref), at_ref[:, 0], eg_ref[:, 0],
                          state[...])
        for h in range(ng):
            o_ref[:, h * HEAD_DIM:(h + 1) * HEAD_DIM] = o[h]
        state[...] = new

    w = ng * HEAD_DIM
    blk = pl.BlockSpec((CHUNK, w), lambda i: (i, 0))
    o, states = pl.pallas_call(
        body, name="gdn_scan_fwd", grid=(nch,),
        in_specs=[blk, blk, blk, blk, pl.BlockSpec((ng, 1, CHUNK, CHUNK), lambda i: (0, i, 0, 0)),
                  pl.BlockSpec((ng, 1, 1, HEAD_DIM), lambda i: (0, i, 0, 0))],
        out_specs=[blk, pl.BlockSpec((ng, 1, HEAD_DIM, HEAD_DIM), lambda i: (0, i, 0, 0))],
        out_shape=[jax.ShapeDtypeStruct((t, w), F32),
                   jax.ShapeDtypeStruct((ng, nch, HEAD_DIM, HEAD_DIM), F32)],
        scratch_shapes=[pltpu.VMEM((ng, HEAD_DIM, HEAD_DIM), F32)],
        compiler_params=_cparams(("arbitrary",)),
    )(*wy)
    return o, (wy, states)


def _wy_batch(q_ref, k_ref, v_ref, gc_ref, gr_ref, bc_ref, ng, cb):
    idx = [(c, h) for c in range(cb) for h in range(ng)]
    rows = lambda c: slice(c * CHUNK, (c + 1) * CHUNK)
    lanes = lambda h: slice(h * HEAD_DIM, (h + 1) * HEAD_DIM)
    wide = lambda ref: jnp.stack([ref[rows(c), lanes(h)] for c, h in idx])
    col = lambda ref: jnp.stack([ref[h, rows(c), :] for c, h in idx])
    return idx, (wide(q_ref), wide(k_ref), wide(v_ref), col(gc_ref), jnp.stack([gr_ref[h, c] for c, h in idx]),
                 col(bc_ref))


def _gdn_wy(qkv, gcol, grow, bcol, ng, cb):
    t = qkv.shape[0]
    nch = t // CHUNK

    def body(q_ref, k_ref, v_ref, gc_ref, gr_ref, bc_ref, u_ref, w_ref, qg_ref, kd_ref, at_ref, eg_ref):
        idx, args = _wy_batch(q_ref, k_ref, v_ref, gc_ref, gr_ref, bc_ref, ng, cb)
        u, w, qg, kd, at, eg = _wy_fn(*args)
        for b, (c, h) in enumerate(idx):
            rows, lanes = slice(c * CHUNK, (c + 1) * CHUNK), slice(h * HEAD_DIM, (h + 1) * HEAD_DIM)
            u_ref[rows, lanes] = u[b]
            w_ref[rows, lanes] = w[b]
            qg_ref[rows, lanes] = qg[b]
            kd_ref[rows, lanes] = kd[b]
            at_ref[h, c] = at[b]
            eg_ref[h, c] = eg[b]

    wd = ng * HEAD_DIM
    blk = lambda o: pl.BlockSpec((cb * CHUNK, wd), lambda i: (i, o))
    col = pl.BlockSpec((ng, cb * CHUNK, 1), lambda i: (0, i, 0))
    wide = jax.ShapeDtypeStruct((t, wd), F32)
    return pl.pallas_call(
        body, name="gdn_wy_fwd", grid=(nch // cb,),
        in_specs=[blk(0), blk(1), blk(2), col, pl.BlockSpec((ng, cb, 1, CHUNK), lambda i: (0, i, 0, 0)), col],
        out_specs=[blk(0), blk(0), blk(0), blk(0), pl.BlockSpec((ng, cb, CHUNK, CHUNK), lambda i: (0, i, 0, 0)),
                   pl.BlockSpec((ng, cb, 1, HEAD_DIM), lambda i: (0, i, 0, 0))],
        out_shape=[wide, wide, wide, wide, jax.ShapeDtypeStruct((ng, nch, CHUNK, CHUNK), F32),
                   jax.ShapeDtypeStruct((ng, nch, 1, HEAD_DIM), F32)],
        compiler_params=_cparams(("parallel",)),
    )(qkv, qkv, qkv, gcol, grow, bcol)


def _gdn_bwd(qkv, gcol, grow, bcol, saved, do, ng):
    t = qkv.shape[0]
    nch = t // CHUNK
    cb = GDN_CHUNKS_PER_STEP // 2
    wy, states = saved
    wd = ng * HEAD_DIM

    def scan_body(u_ref, w_ref, qg_ref, kd_ref, at_ref, eg_ref, st_ref, do_ref,
                  du_ref, dw_ref, dqg_ref, dkd_ref, dat_ref, deg_ref, dstate):
        @pl.when(pl.program_id(0) == 0)
        def _():
            dstate[...] = jnp.zeros_like(dstate)

        heads = lambda ref: jnp.stack([ref[:, h * HEAD_DIM:(h + 1) * HEAD_DIM] for h in range(ng)])
        _, vjp = jax.vjp(_scan_fn, heads(u_ref), heads(w_ref), heads(qg_ref), heads(kd_ref), at_ref[:, 0],
                         eg_ref[:, 0], st_ref[:, 0])
        du, dw, dqg, dkd, dat, deg, dst = vjp((heads(do_ref), dstate[...]))
        for h in range(ng):
            lanes = slice(h * HEAD_DIM, (h + 1) * HEAD_DIM)
            du_ref[:, lanes] = du[h]
            dw_ref[:, lanes] = dw[h]
            dqg_ref[:, lanes] = dqg[h]
            dkd_ref[:, lanes] = dkd[h]
        dat_ref[:, 0] = dat
        deg_ref[:, 0] = deg
        dstate[...] = dst

    rev = lambda i: nch - 1 - i
    blk = pl.BlockSpec((CHUNK, wd), lambda i: (rev(i), 0))
    atb = pl.BlockSpec((ng, 1, CHUNK, CHUNK), lambda i: (0, rev(i), 0, 0))
    egb = pl.BlockSpec((ng, 1, 1, HEAD_DIM), lambda i: (0, rev(i), 0, 0))
    wide = jax.ShapeDtypeStruct((t, wd), F32)
    at_shape = jax.ShapeDtypeStruct((ng, nch, CHUNK, CHUNK), F32)
    eg_shape = jax.ShapeDtypeStruct((ng, nch, 1, HEAD_DIM), F32)
    dwy = pl.pallas_call(
        scan_body, name="gdn_scan_bwd", grid=(nch,),
        in_specs=[blk, blk, blk, blk, atb, egb,
                  pl.BlockSpec((ng, 1, HEAD_DIM, HEAD_DIM), lambda i: (0, rev(i), 0, 0)), blk],
        out_specs=[blk, blk, blk, blk, atb, egb],
        out_shape=[wide, wide, wide, wide, at_shape, eg_shape],
        scratch_shapes=[pltpu.VMEM((ng, HEAD_DIM, HEAD_DIM), F32)],
        compiler_params=_cparams(("arbitrary",)),
    )(*wy, states, do)

    def wy_body(q_ref, k_ref, v_ref, gc_ref, gr_ref, bc_ref, du_ref, dw_ref, dqg_ref, dkd_ref, dat_ref, deg_ref,
                dq_ref, dk_ref, dv_ref, dgc_ref, dgr_ref, dbc_ref):
        idx, args = _wy_batch(q_ref, k_ref, v_ref, gc_ref, gr_ref, bc_ref, ng, cb)
        rows = lambda c: slice(c * CHUNK, (c + 1) * CHUNK)
        lanes = lambda h: slice(h * HEAD_DIM, (h + 1) * HEAD_DIM)
        wide_ct = lambda ref: jnp.stack([ref[rows(c), lanes(h)] for c, h in idx])
        cts = (wide_ct(du_ref), wide_ct(dw_ref), wide_ct(dqg_ref), wide_ct(dkd_ref),
               jnp.stack([dat_ref[h, c] for c, h in idx]), jnp.stack([deg_ref[h, c] for c, h in idx]))
        _, vjp = jax.vjp(_wy_fn, *args)
        dq, dk, dv, dgc, dgr, dbc = vjp(cts)
        for b, (c, h) in enumerate(idx):
            dq_ref[rows(c), lanes(h)] = dq[b]
            dk_ref[rows(c), lanes(h)] = dk[b]
            dv_ref[rows(c), lanes(h)] = dv[b]
            dgc_ref[h, rows(c), :] = dgc[b]
            dgr_ref[h, c] = dgr[b]
            dbc_ref[h, rows(c), :] = dbc[b]

    cblk = lambda o: pl.BlockSpec((cb * CHUNK, wd), lambda i: (i, o))
    col = pl.BlockSpec((ng, cb * CHUNK, 1), lambda i: (0, i, 0))
    rowv = pl.BlockSpec((ng, cb, 1, CHUNK), lambda i: (0, i, 0, 0))
    cshape = jax.ShapeDtypeStruct((ng, t, 1), F32)
    return pl.pallas_call(
        wy_body, name="gdn_wy_bwd", grid=(nch // cb,),
        in_specs=[cblk(0), cblk(1), cblk(2), col, rowv, col, cblk(0), cblk(0), cblk(0), cblk(0),
                  pl.BlockSpec((ng, cb, CHUNK, CHUNK), lambda i: (0, i, 0, 0)),
                  pl.BlockSpec((ng, cb, 1, HEAD_DIM), lambda i: (0, i, 0, 0))],
        out_specs=[cblk(0), cblk(0), cblk(0), col, rowv, col],
        out_shape=[wide, wide, wide, cshape, jax.ShapeDtypeStruct((ng, nch, 1, CHUNK), F32), cshape],
        compiler_params=_cparams(("parallel",)),
    )(qkv, qkv, qkv, gcol, grow, bcol, *dwy)


def _swiglu_fn(gate, up):
    return _silu(gate) * up


def _swiglu_specs(gu):
    _, t, w = gu.shape
    tr, tc = _tile(t, 512, 8), _tile(w, 1408)
    nc = w // tc
    pair = pl.BlockSpec((2, tr, tc), lambda j, r, c: (j, r, c))
    flat = pl.BlockSpec((tr, tc), lambda j, r, c: (r, j * nc + c))
    return (2, t // tr, nc), pair, flat


def _swiglu_fwd(gu):
    grid, pair, flat = _swiglu_specs(gu)

    def body(gu_ref, o_ref):
        o_ref[...] = _swiglu_fn(gu_ref[0], gu_ref[1]).astype(BF16)

    return pl.pallas_call(
        body, name="swiglu_fwd", grid=grid, in_specs=[pair], out_specs=flat,
        out_shape=jax.ShapeDtypeStruct((gu.shape[1], 2 * gu.shape[2]), BF16),
        compiler_params=_cparams(("parallel", "parallel", "parallel")),
    )(gu)


def _swiglu_bwd(gu, dact):
    grid, pair, flat = _swiglu_specs(gu)

    def body(gu_ref, d_ref, o_ref):
        _, vjp = jax.vjp(_swiglu_fn, gu_ref[0], gu_ref[1])
        dg, du = vjp(d_ref[...])
        o_ref[0] = dg.astype(BF16)
        o_ref[1] = du.astype(BF16)

    return pl.pallas_call(
        body, name="swiglu_bwd", grid=grid, in_specs=[pair, flat], out_specs=pair,
        out_shape=jax.ShapeDtypeStruct(gu.shape, BF16),
        compiler_params=_cparams(("parallel", "parallel", "parallel")),
    )(gu, dact)


def _loss_head(h2, target):
    t, d = h2.shape
    tr = _tile(t, 256, 8)

    def body(h_ref, t_ref, l_ref, d_ref):
        @pl.when(pl.program_id(0) == 0)
        def _():
            l_ref[...] = jnp.zeros_like(l_ref)

        err = h_ref[...] - t_ref[...]
        d_ref[...] = err * (1.0 / d)
        part = 0.5 * jnp.sum(jnp.mean(err * err, axis=-1, keepdims=True), axis=0, keepdims=True)
        lane = lax.broadcasted_iota(jnp.int32, (8, HEAD_DIM), 1)
        row = lax.broadcasted_iota(jnp.int32, (8, HEAD_DIM), 0)
        l_ref[...] += jnp.where((lane == 0) & (row == 0), part, 0.0)

    blk = pl.BlockSpec((tr, d), lambda r: (r, 0))
    return pl.pallas_call(
        body, name="loss_head", grid=(t // tr,), in_specs=[blk, blk],
        out_specs=[pl.BlockSpec((8, HEAD_DIM), lambda r: (0, 0)), blk],
        out_shape=[jax.ShapeDtypeStruct((8, HEAD_DIM), F32), jax.ShapeDtypeStruct((t, d), F32)],
        compiler_params=_cparams(("arbitrary",)),
    )(h2, target)


def _adamw(w, g, m, v, *, g2=None, name):
    r, c = w.shape
    tr = _tile(r, max(8, (1 << 19) // c // 8 * 8), 8)

    def body(*refs):
        if g2 is None:
            w_ref, g_ref, m_ref, v_ref, go_ref, d_ref, mo_ref, vo_ref = refs
            gr = g_ref[...]
        else:
            w_ref, g_ref, g2_ref, m_ref, v_ref, go_ref, d_ref, mo_ref, vo_ref = refs
            gr = g_ref[...] + g2_ref[...]
        mn = ADAM_B1 * m_ref[...] + (1.0 - ADAM_B1) * gr
        vn = ADAM_B2 * v_ref[...] + (1.0 - ADAM_B2) * (gr * gr)
        m_hat = mn / (1.0 - ADAM_B1 ** ADAM_STEP)
        v_hat = vn / (1.0 - ADAM_B2 ** ADAM_STEP)
        go_ref[...] = gr
        d_ref[...] = -ADAM_LR * (m_hat / (jnp.sqrt(v_hat) + ADAM_EPS) + ADAM_WD * w_ref[...])
        mo_ref[...] = mn
        vo_ref[...] = vn

    blk = pl.BlockSpec((tr, c), lambda i: (i, 0))
    n_in = 4 if g2 is None else 5
    ins = [w, g] + ([g2] if g2 is not None else []) + [m, v]
    return pl.pallas_call(
        body, name=name, grid=(r // tr,), in_specs=[blk] * n_in, out_specs=[blk] * 4,
        out_shape=[jax.ShapeDtypeStruct((r, c), F32)] * 4,
        compiler_params=_cparams(("parallel",)),
    )(*ins)


class _Layout:
    def __init__(self, d):
        nh = d // HEAD_DIM
        self.nm = N_MEM_HEADS
        self.nf = (nh - self.nm) // 2
        self.ng = nh - self.nm - self.nf
        nf, ng, nm = self.nf, self.ng, self.nm
        self.o_fq, self.o_fk, self.o_fv = 0, nf, 2 * nf
        self.o_gq = 3 * nf
        self.o_gz = 3 * nf + 3 * ng
        self.o_mq = 3 * nf + 4 * ng
        self.o_sm = self.o_mq + nm
        self.blocks = -(-(self.o_sm + 1) // 8) * 8
        self.cols = self.blocks * HEAD_DIM
        hd = HEAD_DIM
        sizes = [nf * hd, nf * hd, nf * hd, nf, 3 * ng * hd, ng * hd, ng, ng, nm * hd]
        starts = [sum(sizes[:i]) for i in range(len(sizes))]
        self.ref = list(zip(starts, sizes))
        self.in_cols = sum(sizes)

    def regroup(self, w):
        part = lambda i: w[:, self.ref[i][0]:self.ref[i][0] + self.ref[i][1]]
        pieces = [part(0), part(1), part(2), part(4), part(5), part(8), part(3), part(6), part(7)]
        pad = self.cols - self.in_cols
        return jnp.concatenate(pieces + [jnp.zeros((w.shape[0], pad), w.dtype)], axis=1)

    def ungroup(self, g):
        hd, nf, ng, nm = HEAD_DIM, self.nf, self.ng, self.nm
        sm = self.o_sm * hd
        return jnp.concatenate([
            g[:, :3 * nf * hd], g[:, sm:sm + nf], g[:, self.o_gq * hd:self.o_gz * hd],
            g[:, self.o_gz * hd:self.o_mq * hd], g[:, sm + nf:sm + nf + ng], g[:, sm + nf + ng:sm + nf + 2 * ng],
            g[:, self.o_mq * hd:self.o_sm * hd]], axis=1)


def _lane_row(pieces):
    row = jnp.zeros((1, HEAD_DIM), F32)
    for off, a in pieces:
        row = lax.dynamic_update_slice(row, a.astype(F32), (0, off))
    return row


def _local_step(x, mem, target, weights, reduce_start, sp):
    t, d = x.shape
    lay = _Layout(d)
    nf, ng, nm, hd = lay.nf, lay.ng, lay.nm, HEAD_DIM
    nch = t // CHUNK
    tq = _tile(t, 256)
    tk = tq

    u = _norm_fwd(x, 0, sp["norm_mix"], 1, d, BF16, name="norm_mix_fwd")
    (win,) = weights("in", u)
    p = _mm(u, win, name="mm_in")
    wmkv, conv_taps = weights("mixer", p)
    sp = dict(sp, gdn_conv=conv_taps)
    pa = _lane_row([(nf, sp["gdn_a_log"])])
    pb = _lane_row([(0, sp["fox_f_bias"]), (nf, sp["gdn_dt_bias"])])
    vals, csum = _small_fwd(p, lay.o_sm, pa, pb, nf, ng)

    c_t = csum[:, :nf].T
    cc, cr = c_t.reshape(nf, t, 1), c_t.reshape(nf, t // tk, 1, tk)
    fq = _norm_fwd(p, lay.o_fq, sp["fox_q_norm"], nf, hd, BF16, name="fox_qnorm_fwd")
    fk = _norm_fwd(p, lay.o_fk, sp["fox_k_norm"], nf, hd, BF16, name="fox_knorm_fwd")
    fv = p[:, lay.o_fv * hd:(lay.o_fv + nf) * hd].astype(BF16)
    o_fox, lse = _fox_fwd(fq, fk, fv, cc, cr, nf, tq, tk)

    qkv = _conv_fwd(p, lay.o_gq, sp["gdn_conv"], ng)
    g_t, b_t = vals[:, nf:nf + ng].T, vals[:, nf + ng:nf + 2 * ng].T
    gcol, grow, bcol = g_t.reshape(ng, t, 1), g_t.reshape(ng, nch, 1, CHUNK), b_t.reshape(ng, t, 1)
    o_g, states = _gdn_fwd(qkv, gcol, grow, bcol, ng)
    o_gdn = _norm_fwd(o_g, 0, sp["gdn_out_norm"], ng, hd, BF16, z=p, zoff=lay.o_gz, name="gdn_out_fwd")

    mem_n = _norm_fwd(mem, 0, sp["mem_norm"], 1, d, BF16, name="mem_norm_fwd")
    mkv = _mm(mem_n, wmkv, name="mm_memkv")
    o_mem = _mem_fwd(p, lay.o_mq, mkv, sp["mem_q_norm"], sp["mem_k_norm"], tq)

    mix = jnp.concatenate([o_fox.astype(BF16), o_gdn, o_mem.astype(BF16)], axis=1)
    (wout,) = weights("out", mix)
    h1 = _mm(mix, wout, res=x, name="mm_out")
    n2 = _norm_fwd(h1, 0, sp["norm_ffn"], 1, d, BF16, name="norm_ffn_fwd")
    (wgu,) = weights("gate_up", n2)
    wgu4 = wgu.reshape(4, d, -1)
    gu = _mm(n2, wgu4, stack="out", name="mm_gate_up")
    act = _swiglu_fwd(gu)
    (wd,) = weights("down", act)
    h2 = _mm(act, wd, res=h1, name="mm_down")
    loss_blk, dh2 = _loss_head(h2, target)

    g = {}
    dw_down = _mm(act, dh2, ta=True, out_dtype=BF16, name="mm_dw_down")
    dact = _mm(dh2, wd, tb=True, name="mm_dact")
    dgu = _swiglu_bwd(gu, dact)
    dw_gate_up = _mm(n2, dgu, ta=True, stack="out", out_dtype=BF16, name="mm_dw_gate_up").reshape(wgu.shape)
    token = reduce_start("ffn", {"w_down": dw_down, "w_gate_up": dw_gate_up})
    dn2 = _mm(dgu, wgu4, tb=True, stack="sum", name="mm_dn2")
    dh1, g["norm_ffn"] = _norm_bwd(h1, 0, sp["norm_ffn"] + token[0, 0], dn2, 0, 1, d, res=dh2,
                                   name="norm_ffn_bwd")
    dw_out = _mm(mix, dh1, ta=True, out_dtype=BF16, name="mm_dw_out")
    dmix = _mm(dh1, wout, tb=True, name="mm_dmix")

    dmq, dmk, dmv, g["mem_q_norm"], g["mem_k_norm"] = _mem_bwd(
        p, lay.o_mq, mkv, sp["mem_q_norm"], sp["mem_k_norm"], dmix, nf + ng, tq)
    dmkv = jnp.concatenate([dmk, dmv], axis=1)
    dw_mem_kv = _mm(mem_n, dmkv, ta=True, out_dtype=BF16, name="mm_dw_memkv")
    token = reduce_start("mix", {"w_out": dw_out, "w_mem_kv": dw_mem_kv})
    dmem_n = _mm(dmkv, wmkv, tb=True, name="mm_dmem")
    _, g["mem_norm"] = _norm_bwd(mem, 0, sp["mem_norm"], dmem_n, 0, 1, d, name="mem_norm_bwd")

    do_g, dgz, g["gdn_out_norm"] = _norm_bwd(o_g, 0, sp["gdn_out_norm"] + token[0, 0], dmix, nf, ng, hd, z=p,
                                             zoff=lay.o_gz, name="gdn_out_bwd")
    dq, dk, dv, dgc, dgr, dbc = _gdn_bwd(qkv, gcol, grow, bcol, states, do_g, ng)
    dgqkv, g["gdn_conv"] = _conv_bwd(p, lay.o_gq, sp["gdn_conv"], jnp.concatenate([dq, dk, dv], axis=1), ng)
    dg_t = dgc.reshape(ng, t) + dgr.reshape(ng, t)
    db_t = dbc.reshape(ng, t)

    dfq_n, dfk_n, dfv, dcc, dcr = _fox_bwd(fq, fk, fv, cc, cr, o_fox, lse, dmix, nf, tq, tk)
    dfq, g["fox_q_norm"] = _norm_bwd(p, lay.o_fq, sp["fox_q_norm"], dfq_n, 0, nf, hd, name="fox_qnorm_bwd")
    dfk, g["fox_k_norm"] = _norm_bwd(p, lay.o_fk, sp["fox_k_norm"], dfk_n, 0, nf, hd, name="fox_knorm_bwd")
    dc_t = dcc.reshape(nf, t) + dcr.reshape(nf, t)

    lanes_left = hd - nf - 2 * ng
    dvals = jnp.concatenate([jnp.zeros((t, nf), F32), dg_t.T, db_t.T, jnp.zeros((t, lanes_left), F32)], axis=1)
    dcsum = jnp.concatenate([dc_t.T, jnp.zeros((t, hd - nf), F32)], axis=1)
    dsm, dpa, dpb = _small_bwd(p, lay.o_sm, pa, pb, dvals, dcsum, nf, ng)
    g["fox_f_bias"] = dpb[:, :nf]
    g["gdn_dt_bias"] = dpb[:, nf:nf + ng]
    g["gdn_a_log"] = dpa[:, nf:nf + ng]

    pad = jnp.zeros((t, lay.cols - (lay.o_sm + 1) * hd), F32)
    dp = jnp.concatenate([dfq, dfk, dfv, dgqkv, dgz, dmq, dsm, pad], axis=1)
    token = reduce_start("in", {"w_in": _mm(u, dp, ta=True, out_dtype=BF16, name="mm_dw_in")})
    du = _mm(dp, win, tb=True, after=token, name="mm_du")
    dx, g["norm_mix"] = _norm_bwd(x, 0, sp["norm_mix"], du, 0, 1, d, res=dh1, name="norm_mix_bwd")
    return loss_blk, dx, g


ANY = pl.BlockSpec(memory_space=pl.ANY)


def _me():
    x, y, c = lax.axis_index("x"), lax.axis_index("y"), lax.axis_index("c")
    chips = [(1 - x, y), (x, 1 - y), (1 - x, 1 - y)]
    return x, y, c, chips


def _slot(axis, k):
    return k if axis == 0 else 2 * (k % 2) + k // 2


def _slab(ref, axis, rows, cols, k, h):
    half = rows // 2
    return ref.at[pl.ds(_slot(axis, k) * rows + h * half, half), :]


def _remote(src, dst, send_sem, recv_sem, dev):
    return pltpu.make_async_remote_copy(src_ref=src, dst_ref=dst, send_sem=send_sem, recv_sem=recv_sem,
                                        device_id=dev, device_id_type=MESH)


HBM = pl.BlockSpec(memory_space=pltpu.HBM)
SEM = pl.BlockSpec(memory_space=pltpu.SEMAPHORE)
SPLIT = pltpu.CompilerParams(has_side_effects=pltpu.SideEffectType.DATAFLOW_SIDE_EFFECTING)
TOKEN = jax.ShapeDtypeStruct((8, HEAD_DIM), F32)


def _in_hbm(v):
    return pltpu.with_memory_space_constraint(v, pltpu.HBM)


def _cast_place(shard, axis, name):
    r, c = shard.shape
    tr, tc = _tile(r, 512, 16), _tile(c, 2048)
    nb = r // tr
    chip = 2 * lax.axis_index("x") + lax.axis_index("y")
    slot = jnp.reshape(_slot(axis, chip), (1,)).astype(jnp.int32)

    def body(slot_ref, x_ref, o_ref):
        o_ref[...] = x_ref[...].astype(BF16)

    return pl.pallas_call(
        body, name=name,
        grid_spec=pltpu.PrefetchScalarGridSpec(
            num_scalar_prefetch=1, grid=(nb, c // tc),
            in_specs=[pl.BlockSpec((tr, tc), lambda i, l, s: (i, l))],
            out_specs=pl.BlockSpec((tr, tc), lambda i, l, s: (s[0] * nb + i, l))),
        out_shape=jax.ShapeDtypeStruct((4 * r, c), BF16),
        compiler_params=_cparams(("parallel", "parallel")),
    )(slot, shard)


def _gather_start(bufs, axes, shapes, groups):
    n = len(bufs)

    def body(*refs):
        dst = refs[n:2 * n]
        sems = refs[2 * n:2 * n + 2 * len(groups)]
        token = refs[-1]
        x, y, c, chips = _me()
        k = 2 * x + y
        for gi, ws in enumerate(groups):
            for i, w in enumerate(ws):
                r, cl = shapes[w]
                place = _slab(dst[w], axes[w], r, cl, k, c)
                for j, (px, py) in enumerate(chips):
                    _remote(place, place, sems[2 * gi].at[3 * i + j], sems[2 * gi + 1].at[3 * i + j],
                            (px, py, c)).start()
        token[...] = jnp.zeros_like(token)

    sem_shapes = [pltpu.SemaphoreType.DMA((3 * len(ws),)) for ws in groups for _ in range(2)]
    outs = pl.pallas_call(
        body, name="gather_ici_start", in_specs=[HBM] * n,
        out_specs=[HBM] * n + [SEM] * len(sem_shapes) + [pl.BlockSpec(memory_space=pltpu.VMEM)],
        out_shape=[pltpu.HBM(b.shape, b.dtype) for b in bufs] + sem_shapes + [TOKEN],
        input_output_aliases={w: w for w in range(n)}, compiler_params=SPLIT,
    )(*[_in_hbm(b) for b in bufs])
    sems = outs[n:-1]
    return outs[:n], [(sems[2 * g], sems[2 * g + 1]) for g in range(len(groups))], outs[-1]


def _gather_wait(bufs, axes, shapes, sems, after, name):
    n = len(bufs)

    def body(*refs):
        send_sems, recv_sems = refs[n], refs[n + 1]
        dst = refs[n + 3:]
        x, y, c, chips = _me()
        k = 2 * x + y
        for i in range(n):
            r, cl = shapes[i]
            for j, (px, py) in enumerate(chips):
                got = _slab(dst[i], axes[i], r, cl, 2 * px + py, c)
                _remote(got, got, send_sems.at[3 * i + j], recv_sems.at[3 * i + j], (px, py, c)).wait_recv()
        for i in range(n):
            r, cl = shapes[i]
            mine = _slab(dst[i], axes[i], r, cl, k, c)
            for j, (px, py) in enumerate(chips):
                _remote(mine, mine, send_sems.at[3 * i + j], recv_sems.at[3 * i + j], (px, py, c)).wait_send()

    return pl.pallas_call(
        body, name=name, in_specs=[HBM] * n + [SEM, SEM, ANY], out_specs=[HBM] * n,
        out_shape=[pltpu.HBM(b.shape, b.dtype) for b in bufs],
        input_output_aliases={i: i for i in range(n)}, compiler_params=SPLIT,
    )(*bufs, sems[0], sems[1], after)


def _gather_forward(bufs, axes, shapes, name):
    n = len(bufs)

    def body(*refs):
        dst = refs[n:2 * n]
        send_sems, recv_sems = refs[2 * n:]
        x, y, c, chips = _me()
        sibling = (x, y, 1 - c)
        sends = []
        for i in range(n):
            r, cl = shapes[i]
            for j, (px, py) in enumerate(chips):
                got = _slab(dst[i], axes[i], r, cl, 2 * px + py, c)
                cp = _remote(got, got, send_sems.at[3 * i + j], recv_sems.at[3 * i + j], sibling)
                cp.start()
                sends.append(cp)
        for i in range(n):
            r, cl = shapes[i]
            for j, (px, py) in enumerate(chips):
                got = _slab(dst[i], axes[i], r, cl, 2 * px + py, 1 - c)
                _remote(got, got, send_sems.at[3 * i + j], recv_sems.at[3 * i + j], sibling).wait_recv()
        for cp in sends:
            cp.wait_send()

    return pl.pallas_call(
        body, name=name, in_specs=[ANY] * n, out_specs=[ANY] * n,
        out_shape=[jax.ShapeDtypeStruct(b.shape, b.dtype) for b in bufs],
        input_output_aliases={i: i for i in range(n)},
        scratch_shapes=[pltpu.SemaphoreType.DMA((3 * n,)), pltpu.SemaphoreType.DMA((3 * n,))],
    )(*bufs)


def _pair_exchange(fulls, axes, shapes, tag):
    n = len(fulls)

    def body(*refs):
        src, dst = refs[:n], refs[n:2 * n]
        send_sems, recv_sems = refs[2 * n:]
        x, y, c, _ = _me()
        sibling = (x, y, 1 - c)
        cps = []
        for w in range(n):
            r, cl = shapes[w]
            for j in range(4):
                cp = _remote(_slab(src[w], axes[w], r, cl, j, 1 - c), dst[w].at[j],
                             send_sems.at[4 * w + j], recv_sems.at[4 * w + j], sibling)
                cp.start()
                cps.append(cp)
        for cp in cps:
            cp.wait()

    out_shape = [jax.ShapeDtypeStruct((4, r // 2, cl), f.dtype) for (r, cl), f in zip(shapes, fulls)]
    return pl.pallas_call(
        body, name="reduce_pair_exchange_" + tag, in_specs=[ANY] * n, out_specs=[ANY] * n, out_shape=out_shape,
        scratch_shapes=[pltpu.SemaphoreType.DMA((4 * n,)), pltpu.SemaphoreType.DMA((4 * n,))],
    )(*fulls)


def _chip_start(parts, tag):
    n = len(parts)

    def body(*refs):
        src, land = refs[2 * n:3 * n], refs[3 * n:4 * n]
        send_sems, recv_sems, token = refs[4 * n:]
        x, y, c, chips = _me()
        k = 2 * x + y
        for w in range(n):
            for j, (px, py) in enumerate(chips):
                _remote(src[w].at[2 * px + py], land[w].at[k], send_sems.at[3 * w + j], recv_sems.at[3 * w + j],
                        (px, py, c)).start()
        token[...] = jnp.zeros_like(token)

    lands = [lax.empty(p.shape, p.dtype) for p in parts]
    sem = pltpu.SemaphoreType.DMA((3 * n,))
    outs = pl.pallas_call(
        body, name="reduce_ici_start_" + tag, in_specs=[HBM] * (2 * n),
        out_specs=[HBM] * (2 * n) + [SEM, SEM, pl.BlockSpec(memory_space=pltpu.VMEM)],
        out_shape=[pltpu.HBM(p.shape, p.dtype) for p in parts + lands] + [sem, sem, TOKEN],
        input_output_aliases={i: i for i in range(2 * n)}, compiler_params=SPLIT,
    )(*[_in_hbm(v) for v in parts + lands])
    return outs[:n], outs[n:2 * n], outs[2 * n], outs[2 * n + 1], outs[-1]


def _chip_wait(parts, lands, send_sems, recv_sems, after, tag):
    n = len(parts)

    def body(*refs):
        send, recv = refs[2 * n], refs[2 * n + 1]
        src, land = refs[2 * n + 3:3 * n + 3], refs[3 * n + 3:]
        x, y, c, chips = _me()
        for w in range(n):
            for j, (px, py) in enumerate(chips):
                got = land[w].at[2 * px + py]
                _remote(got, got, send.at[3 * w + j], recv.at[3 * w + j], (px, py, c)).wait_recv()
        for w in range(n):
            for j, (px, py) in enumerate(chips):
                sent = src[w].at[2 * px + py]
                _remote(sent, sent, send.at[3 * w + j], recv.at[3 * w + j], (px, py, c)).wait_send()

    outs = pl.pallas_call(
        body, name="reduce_ici_wait_" + tag, in_specs=[HBM] * (2 * n) + [SEM, SEM, ANY], out_specs=[HBM] * (2 * n),
        out_shape=[pltpu.HBM(p.shape, p.dtype) for p in parts + lands],
        input_output_aliases={i: i for i in range(2 * n)}, compiler_params=SPLIT,
    )(*parts, *lands, send_sems, recv_sems, after)
    chip = 2 * lax.axis_index("x") + lax.axis_index("y")
    return [lax.dynamic_update_slice(s, lax.dynamic_index_in_dim(p, chip, 0, keepdims=True), (chip, 0, 0))
            for p, s in zip(outs[:n], outs[n:])]


def _half_swap(halves, tag):
    n = len(halves)
    core = lax.axis_index("c")
    bufs = [lax.dynamic_update_slice(lax.empty((2,) + h.shape, h.dtype), h[None], (core, 0, 0)) for h in halves]

    def body(*refs):
        dst = refs[n:2 * n]
        send_sems, recv_sems = refs[2 * n:]
        x, y, c, _ = _me()
        sibling = (x, y, 1 - c)
        cps = []
        for w in range(n):
            cp = _remote(dst[w].at[c], dst[w].at[c], send_sems.at[w], recv_sems.at[w], sibling)
            cp.start()
            cps.append(cp)
        for w in range(n):
            other = dst[w].at[1 - c]
            _remote(other, other, send_sems.at[w], recv_sems.at[w], sibling).wait_recv()
        for cp in cps:
            cp.wait_send()

    outs = pl.pallas_call(
        body, name="reduce_half_swap_" + tag, in_specs=[ANY] * n, out_specs=[ANY] * n,
        out_shape=[jax.ShapeDtypeStruct(b.shape, b.dtype) for b in bufs],
        input_output_aliases={w: w for w in range(n)},
        scratch_shapes=[pltpu.SemaphoreType.DMA((n,)), pltpu.SemaphoreType.DMA((n,))],
    )(*bufs)
    return [o.reshape(2 * o.shape[1], o.shape[2]) for o in outs]


def _add_parts(full, axis, rows, sib, name):
    _, r, c = sib.shape
    tr, tc = _tile(r, 256, 16), _tile(c, 2048)
    nb = r // tr
    core = jnp.reshape(lax.axis_index("c"), (1,)).astype(jnp.int32)

    def body(c_ref, a_ref, b_ref, o_ref):
        o_ref[0] = (a_ref[...].astype(F32) + b_ref[0].astype(F32)).astype(BF16)

    blk = pl.BlockSpec((1, tr, tc), lambda j, i, l, cr: (j, i, l))
    return pl.pallas_call(
        body, name=name,
        grid_spec=pltpu.PrefetchScalarGridSpec(
            num_scalar_prefetch=1, grid=(4, nb, c // tc),
            in_specs=[pl.BlockSpec((tr, tc), lambda j, i, l, cr: ((_slot(axis, j) * 2 + cr[0]) * nb + i, l)), blk],
            out_specs=blk),
        out_shape=jax.ShapeDtypeStruct(sib.shape, BF16),
        compiler_params=_cparams(("parallel", "parallel", "parallel")),
    )(core, full, sib)


def _sum_slots(a, name):
    _, r, c = a.shape
    tr, tc = _tile(r, 256, 8), _tile(c, 2048)

    def body(a_ref, o_ref):
        v = a_ref[...].astype(F32)
        o_ref[...] = ((v[0] + v[1]) + v[2]) + v[3]

    return pl.pallas_call(
        body, name=name, grid=(r // tr, c // tc),
        in_specs=[pl.BlockSpec((4, tr, tc), lambda i, l: (0, i, l))],
        out_specs=pl.BlockSpec((tr, tc), lambda i, l: (i, l)),
        out_shape=jax.ShapeDtypeStruct((r, c), F32),
        compiler_params=_cparams(("parallel", "parallel")),
    )(a)


class _Reducer:
    def __init__(self):
        self.pending = []

    def start(self, tag, names, fulls, axes, shapes):
        from_sibling = _pair_exchange(fulls, axes, shapes, tag)
        parts = [_add_parts(f, a, r, s, name=f"reduce_add_{n}")
                 for n, f, a, (r, cl), s in zip(names, fulls, axes, shapes, from_sibling)]
        parts, lands, send, recv, token = _chip_start(parts, tag)
        self.pending.append((tag, names, parts, lands, send, recv))
        return token

    def finish(self, after):
        out = {}
        for tag, names, parts, lands, send, recv in self.pending:
            slots = _chip_wait(parts, lands, send, recv, after, tag)
            halves = [_sum_slots(s, name=f"reduce_sum_{n}") for n, s in zip(names, slots)]
            out.update(zip(names, _half_swap(halves, tag)))
        return out


def _allreduce_small(pack):
    rows = pack.shape[0]

    def body(p_ref, o_ref, slots, send_sems, recv_sems):
        x, y, c, _ = _me()
        me = 4 * x + 2 * y + c
        slots[me] = p_ref[...]
        cps = []
        for r in range(1, 8):
            peer = (x ^ (r >> 2), y ^ ((r >> 1) & 1), c ^ (r & 1))
            cp = _remote(p_ref, slots.at[me], send_sems.at[r - 1], recv_sems.at[r - 1], peer)
            cp.start()
            cps.append(cp)
        for r in range(1, 8):
            frm = me ^ r
            _remote(slots.at[frm], slots.at[frm], send_sems.at[r - 1], recv_sems.at[r - 1], (x, y, c)).wait_recv()
        for cp in cps:
            cp.wait_send()
        acc = slots[0]
        for s in range(1, 8):
            acc = acc + slots[s]
        o_ref[...] = acc

    vm = pl.BlockSpec(memory_space=pltpu.VMEM)
    return pl.pallas_call(
        body, name="allreduce_small", in_specs=[vm], out_specs=vm,
        out_shape=jax.ShapeDtypeStruct(pack.shape, F32),
        scratch_shapes=[pltpu.VMEM((8, rows, HEAD_DIM), F32), pltpu.SemaphoreType.DMA((7,)),
                        pltpu.SemaphoreType.DMA((7,))],
    )(pack)


_ROWS = ["norm_mix", "norm_ffn", "mem_norm", "fox_q_norm", "fox_k_norm", "gdn_out_norm", "mem_q_norm",
         "mem_k_norm", "fox_f_bias", "gdn_a_log", "gdn_dt_bias"]


def _pack_rows(vals):
    out = []
    for name in _ROWS:
        v = vals[name].reshape(-1)
        n = -(-v.shape[0] // HEAD_DIM) * HEAD_DIM
        out.append(jnp.pad(v, (0, n - v.shape[0])).reshape(-1, HEAD_DIM))
    return jnp.concatenate(out, axis=0)


def _unpack_rows(pack, like):
    out, r = {}, 0
    for name in _ROWS:
        n = like[name].shape[-1]
        nr = -(-n // HEAD_DIM)
        out[name] = pack[r:r + nr].reshape(1, -1)[:, :n]
        r += nr
    return out, r


def kernel(x, mem, norm_mix, w_in, fox_f_bias, fox_q_norm, fox_k_norm, gdn_conv, gdn_a_log, gdn_dt_bias, gdn_out_norm, mem_norm, w_mem_kv, mem_q_norm, mem_k_norm, w_out, norm_ffn, w_gate_up, w_down, loss_target, m_norm_mix, m_w_in, m_fox_f_bias, m_fox_q_norm, m_fox_k_norm, m_gdn_conv, m_gdn_a_log, m_gdn_dt_bias, m_gdn_out_norm, m_mem_norm, m_w_mem_kv, m_mem_q_norm, m_mem_k_norm, m_w_out, m_norm_ffn, m_w_gate_up, m_w_down, v_norm_mix, v_w_in, v_fox_f_bias, v_fox_q_norm, v_fox_k_norm, v_gdn_conv, v_gdn_a_log, v_gdn_dt_bias, v_gdn_out_norm, v_mem_norm, v_w_mem_kv, v_mem_q_norm, v_mem_k_norm, v_w_out, v_norm_ffn, v_w_gate_up, v_w_down):
    a = dict(locals())
    d = x.shape[-1]
    lay = _Layout(d)
    chip = 2 * lax.axis_index("x") + lax.axis_index("y")
    small = {n: a[n] for n in _ROWS}
    big = ["w_in", "w_mem_kv", "w_out", "w_gate_up", "w_down"]
    axes = [0, 0, 0, 1, 0]

    conv_cols = gdn_conv.shape[-1]
    conv_n = CONV_WIDTH * conv_cols
    conv_rows = -(-conv_n // HEAD_DIM)
    conv_blk = jnp.pad(gdn_conv.reshape(-1), (0, 32 * HEAD_DIM - conv_n)).reshape(32, HEAD_DIM)
    shards = [lay.regroup(w_in[0]), w_mem_kv[0], w_out[0], w_gate_up[0], w_down[0]]
    placed = [_cast_place(s, ax, "cast_" + n) for s, ax, n in zip(shards, axes, big)]
    chip_row = chip * 32
    placed.append(lax.dynamic_update_slice(lax.empty((4 * 32, HEAD_DIM), F32), conv_blk, (chip_row, 0)))
    all_axes = axes + [0]
    all_shapes = [s.shape for s in shards] + [conv_blk.shape]
    groups = {"in": [0], "mixer": [1, 5], "out": [2], "gate_up": [3], "down": [4]}
    bufs, sems, token = _gather_start(placed, all_axes, all_shapes, list(groups.values()))
    pick = lambda seq, idx: [seq[i] for i in idx]

    def weights(tag, after):
        idx, sem_pair = groups[tag], sems[list(groups).index(tag)]
        got = _gather_wait(pick(bufs, idx), pick(all_axes, idx), pick(all_shapes, idx), sem_pair,
                           token if after is None else after, "gather_ici_wait_" + tag)
        got = _gather_forward(got, pick(all_axes, idx), pick(all_shapes, idx), "gather_forward_" + tag)
        if tag != "mixer":
            return got
        taps = got[1].reshape(4, 32 * HEAD_DIM)[:, :conv_n].reshape(4, CONV_WIDTH, conv_cols)
        return got[0], jnp.transpose(taps, (1, 0, 2)).reshape(CONV_WIDTH, 4 * conv_cols)

    sp = dict(small)
    reducer = _Reducer()
    spec = {n: (ax, s.shape) for n, ax, s in zip(big, axes, shards)}

    def reduce_start(tag, grads):
        names = list(grads)
        return reducer.start(tag, names, [grads[n] for n in names], [spec[n][0] for n in names],
                             [spec[n][1] for n in names])

    loss_blk, dx, g = _local_step(x[0], mem[0], loss_target[0], weights, reduce_start, sp)

    gsmall = {n: g[n] for n in _ROWS}
    pack = jnp.concatenate([_pack_rows(gsmall), g["gdn_conv"].reshape(-1, HEAD_DIM), loss_blk], axis=0)
    pack = jnp.pad(pack, ((0, -pack.shape[0] % 8), (0, 0)))
    tot = _allreduce_small(pack)
    gs, r0 = _unpack_rows(tot, small)
    conv_g = tot[r0:r0 + CONV_WIDTH * 4 * conv_cols // HEAD_DIM].reshape(CONV_WIDTH, 4 * conv_cols)
    gs_conv = lax.dynamic_slice_in_dim(conv_g, chip * conv_cols, conv_cols, axis=1)
    loss = tot[r0 + CONV_WIDTH * 4 * conv_cols // HEAD_DIM, 0]
    reduced = reducer.finish(tot)
    reduced["w_in"] = lay.ungroup(reduced["w_in"])

    out = {"loss": loss, "grad_x": dx[None]}
    for n, gsh in reduced.items():
        res = _adamw(a[n][0], gsh, a["m_" + n][0], a["v_" + n][0], name="adamw_" + n)
        for pre, r in zip(["grad_", "delta_", "new_m_", "new_v_"], res):
            out[pre + n] = r[None]
    conv_pad = lambda v: jnp.pad(v.reshape(-1), (0, conv_rows * HEAD_DIM - conv_n)).reshape(conv_rows, HEAD_DIM)
    packs = []
    for src, cv in [(small, gdn_conv), (gs, gs_conv), ({n: a["m_" + n] for n in _ROWS}, m_gdn_conv),
                    ({n: a["v_" + n] for n in _ROWS}, v_gdn_conv)]:
        packs.append(jnp.concatenate([_pack_rows(src), conv_pad(cv)], axis=0))
    res = _adamw(*packs, name="adamw_small")
    for pre, r in zip(["grad_", "delta_", "new_m_", "new_v_"], res):
        vals, r1 = _unpack_rows(r, small)
        for n in _ROWS:
            out[pre + n] = vals[n]
        out[pre + "gdn_conv"] = r[r1:r1 + conv_rows].reshape(-1)[:conv_n].reshape(gdn_conv.shape)
    names = ["norm_mix", "w_in", "fox_f_bias", "fox_q_norm", "fox_k_norm", "gdn_conv", "gdn_a_log", "gdn_dt_bias",
             "gdn_out_norm", "mem_norm", "w_mem_kv", "mem_q_norm", "mem_k_norm", "w_out", "norm_ffn", "w_gate_up",
             "w_down"]
    return (out["loss"], out["grad_x"], *[out[p + n] for p in ["grad_", "delta_", "new_m_", "new_v_"] for n in names])
```

```python
import functools
import math

import jax
import jax.numpy as jnp
from jax import lax
from jax.experimental import pallas as pl
from jax.experimental.pallas import tpu as pltpu

F32, BF16 = jnp.float32, jnp.bfloat16
HEAD_DIM = 128
CHUNK = 64
N_MEM_HEADS = 4
CONV_WIDTH = 4
NORM_EPS = 1e-6
ADAM_LR, ADAM_B1, ADAM_B2, ADAM_EPS, ADAM_WD, ADAM_STEP = 0.001, 0.9, 0.999, 1e-08, 0.01, 10
VMEM_LIMIT = 48 * 1024 * 1024
NEG = -1e30
MESH = pl.DeviceIdType.MESH


def _cparams(sem=None, **kw):
    if sem is not None:
        kw["dimension_semantics"] = sem
    return pltpu.CompilerParams(vmem_limit_bytes=VMEM_LIMIT, **kw)


def _tile(n, target, mult=128):
    best = None
    d = mult
    while d <= min(n, target):
        if n % d == 0:
            best = d
        d += mult
    return best if best is not None else n


def _dot(a, b, dims, hi):
    if a.ndim == 3:
        dn = (((dims[0][0] + 1,), (dims[1][0] + 1,)), ((0,), (0,)))
    else:
        dn = (dims, ((), ()))
    if hi is not None:
        return lax.dot_general(a, b, dn, precision=hi, preferred_element_type=F32)
    return lax.dot_general(a.astype(BF16), b.astype(BF16), dn, preferred_element_type=F32)


def _make_dots(hi, cotangent=None):
    @jax.custom_vjp
    def nn(a, b):
        return _dot(a, b, ((1,), (0,)), hi)

    @jax.custom_vjp
    def nt(a, b):
        return _dot(a, b, ((1,), (1,)), hi)

    @jax.custom_vjp
    def tn(a, b):
        return _dot(a, b, ((0,), (0,)), hi)

    bnn, bnt, btn = cotangent or (nn, nt, tn)
    nn.defvjp(lambda a, b: (nn(a, b), (a, b)), lambda r, g: (bnt(g, r[1]), btn(r[0], g)))
    nt.defvjp(lambda a, b: (nt(a, b), (a, b)), lambda r, g: (bnn(g, r[1]), btn(g, r[0])))
    tn.defvjp(lambda a, b: (tn(a, b), (a, b)), lambda r, g: (bnt(r[1], g), bnn(r[0], g)))
    return nn, nt, tn


_nn, _nt, _tn = _make_dots(None)
_nn_hi, _nt_hi, _tn_hi = _make_dots(lax.Precision.HIGHEST)
_nn_x3, _nt_x3, _tn_x3 = _make_dots(lax.Precision.HIGH, (_nn, _nt, _tn))


def _sigmoid(x):
    return 1.0 / (1.0 + jnp.exp(-x))


@jax.custom_vjp
def _softplus(x):
    return jnp.maximum(x, 0.0) + jnp.log(1.0 + jnp.exp(-jnp.abs(x)))


_softplus.defvjp(lambda x: (_softplus(x), x), lambda x, g: (g * _sigmoid(x),))


def _silu(x):
    return x * _sigmoid(x)


def _rms_fn(x, gain, z=None):
    y = x * lax.rsqrt(jnp.mean(x * x, axis=-1, keepdims=True) + NORM_EPS) * gain
    if z is not None:
        y = y * _silu(z)
    return y


def _mm(a, b, *, ta=False, tb=False, out_dtype=F32, res=None, stack=None, after=None, name):
    a2, b2 = a.shape[-2:], b.shape[-2:]
    ns = b.shape[0] if stack else 1
    m = a2[1] if ta else a2[0]
    k = a2[0] if ta else a2[1]
    n = b2[0] if tb else b2[1]
    assert k == (b2[1] if tb else b2[0])
    tm, tn, tk = _mm_tiles(m, n, k, ns if stack == "sum" else 1, a.dtype.itemsize, b.dtype.itemsize,
                           jnp.dtype(out_dtype).itemsize, res is not None)
    nk = k // tk
    single = nk == 1 and stack != "sum"
    dims = ((0 if ta else 1,), (1 if tb else 0,))
    if stack == "sum":
        order = lambda g0, g1, g2, g3: (g2, g0, g1, g3)
        grid = (m // tm, n // tn, ns, nk)
    else:
        order = lambda g0, g1, g2, g3: (g0, g1, g2, g3)
        grid = (ns, m // tm, n // tn, nk)

    def body(*refs):
        if after is not None:
            refs = refs[:2 + (res is not None)] + refs[3 + (res is not None):]
        if single:
            a_ref, b_ref = refs[:2]
            r = lax.dot_general(a_ref[...].astype(BF16), b_ref[...].astype(BF16), (dims, ((), ())),
                                preferred_element_type=F32)
            if res is not None:
                r = r + refs[2][...]
            refs[-1][...] = r.astype(out_dtype)
            return
        if res is None:
            a_ref, b_ref, o_ref, acc = refs
        else:
            a_ref, b_ref, r_ref, o_ref, acc = refs
        s, _, _, kk = order(*[pl.program_id(d) for d in range(4)])
        first = kk == 0
        last = kk == nk - 1
        if stack == "sum":
            first, last = first & (s == 0), last & (s == ns - 1)

        @pl.when(first)
        def _():
            acc[...] = jnp.zeros_like(acc)

        acc[...] += lax.dot_general(a_ref[...].astype(BF16), b_ref[...].astype(BF16), (dims, ((), ())),
                                    preferred_element_type=F32)

        @pl.when(last)
        def _():
            r = acc[...]
            if res is not None:
                r = r + r_ref[...]
            o_ref[...] = r.astype(out_dtype)

    def spec(shape, idx, stacked):
        if stacked:
            return pl.BlockSpec((None,) + shape, lambda *g: (order(*g)[0],) + idx(*order(*g)))
        return pl.BlockSpec(shape, lambda *g: idx(*order(*g)))

    a_spec = (spec((tk, tm), lambda s, i, j, kk: (kk, i), stack == "sum") if ta
              else spec((tm, tk), lambda s, i, j, kk: (i, kk), stack == "sum"))
    b_spec = (spec((tn, tk), lambda s, i, j, kk: (j, kk), bool(stack)) if tb
              else spec((tk, tn), lambda s, i, j, kk: (kk, j), bool(stack)))
    o_spec = spec((tm, tn), lambda s, i, j, kk: (i, j), stack == "out")
    ins, specs = [a, b], [a_spec, b_spec]
    if res is not None:
        ins.append(res)
        specs.append(o_spec)
    if after is not None:
        ins.append(after)
        specs.append(pl.BlockSpec(after.shape, lambda *g: (0,) * after.ndim))
    sem = (("parallel", "parallel", "arbitrary", "arbitrary") if stack == "sum"
           else ("parallel", "parallel", "parallel", "arbitrary"))
    return pl.pallas_call(
        body, name=name, grid=grid, in_specs=specs, out_specs=o_spec,
        out_shape=jax.ShapeDtypeStruct(((ns,) if stack == "out" else ()) + (m, n), out_dtype),
        scratch_shapes=[] if single else [pltpu.VMEM((tm, tn), F32)],
        compiler_params=_cparams(sem),
    )(*ins)


MM_VMEM_BUDGET = 40 * 1024 * 1024


def _mm_tiles(m, n, k, ns, sa, sb, so, has_res):
    def divs(x, mult, cap):
        out = [d for d in range(mult, min(x, cap) + 1, mult) if x % d == 0]
        return out or [x]

    best = None
    for tk in divs(k, 128, 8192):
        nk = (k // tk) * ns
        for tm in divs(m, 8, 2048):
            for tn in divs(n, 128, 2048):
                vmem = 2 * (tm * tk * sa + tk * tn * sb + tm * tn * so) + (2 * tm * tn * 4 if has_res else 0)
                vmem += tm * tn * 4 if nk > 1 else 0
                if vmem > MM_VMEM_BUDGET:
                    continue
                steps = (m // tm) * (n // tn) * nk
                traffic = (m // tm) * k * n * sb * ns + (n // tn if nk > 1 else 1) * m * k * sa * ns
                cost = steps * 0.4e-6 + traffic / 2.5e12 + (nk * m * n * 8 / 6e12 if nk > 1 else 0)
                cost += 2.0 * m * n * k * ns / 7e14
                if best is None or cost < best[0]:
                    best = (cost, tm, tn, tk)
    return best[1:]


def _norm_fwd(x, xoff, gain, ncol, w, out_dtype, *, z=None, zoff=0, into=None, into_off=0, name):
    t = x.shape[0]
    tr = _tile(t, max(256, (1 << 18) // w), 8)

    def body(*refs):
        x_ref, g_ref, o_ref = refs[0], refs[1], refs[-1]
        y = _rms_fn(x_ref[...], g_ref[...]) if z is None else _rms_fn(x_ref[...], g_ref[...], refs[2][...])
        o_ref[...] = y.astype(out_dtype)

    ins = [x, gain]
    specs = [pl.BlockSpec((tr, w), lambda j, r: (r, xoff + j)), pl.BlockSpec((1, w), lambda j, r: (0, 0))]
    if z is not None:
        ins.append(z)
        specs.append(pl.BlockSpec((tr, w), lambda j, r: (r, zoff + j)))
    aliases = {}
    if into is not None:
        aliases = {len(ins): 0}
        ins.append(into)
        specs.append(pl.BlockSpec(memory_space=pl.ANY))
    return pl.pallas_call(
        body, name=name, grid=(ncol, t // tr), in_specs=specs,
        out_specs=pl.BlockSpec((tr, w), lambda j, r: (r, into_off + j)),
        out_shape=jax.ShapeDtypeStruct((t, ncol * w) if into is None else into.shape, out_dtype),
        input_output_aliases=aliases, compiler_params=_cparams(("parallel", "parallel")),
    )(*ins)


def _norm_bwd(x, xoff, gain, dy, dyoff, ncol, w, *, z=None, zoff=0, res=None, name):
    t = x.shape[0]
    tr = _tile(t, max(256, (1 << 18) // w), 8)

    def body(*refs):
        it = iter(refs)
        x_ref, g_ref = next(it), next(it)
        z_ref = next(it) if z is not None else None
        dy_ref = next(it)
        r_ref = next(it) if res is not None else None
        dx_ref = next(it)
        dz_ref = next(it) if z is not None else None
        dg_ref = next(it)

        @pl.when((pl.program_id(0) == 0) & (pl.program_id(1) == 0))
        def _():
            dg_ref[...] = jnp.zeros_like(dg_ref)

        args = (x_ref[...], g_ref[...]) + ((z_ref[...],) if z is not None else ())
        _, vjp = jax.vjp(_rms_fn, *args)
        grads = vjp(dy_ref[...].astype(F32))
        dx = grads[0]
        if res is not None:
            dx = dx + r_ref[...]
        dx_ref[...] = dx
        if z is not None:
            dz_ref[...] = grads[2]
        dg_ref[...] += grads[1]

    ins = [x, gain]
    specs = [pl.BlockSpec((tr, w), lambda j, r: (r, xoff + j)), pl.BlockSpec((1, w), lambda j, r: (0, 0))]
    if z is not None:
        ins.append(z)
        specs.append(pl.BlockSpec((tr, w), lambda j, r: (r, zoff + j)))
    ins.append(dy)
    specs.append(pl.BlockSpec((tr, w), lambda j, r: (r, dyoff + j)))
    blk = pl.BlockSpec((tr, w), lambda j, r: (r, j))
    if res is not None:
        ins.append(res)
        specs.append(blk)
    full = jax.ShapeDtypeStruct((t, ncol * w), F32)
    out_shape, out_specs = [full], [blk]
    if z is not None:
        out_shape.append(full)
        out_specs.append(blk)
    out_shape.append(jax.ShapeDtypeStruct((1, w), F32))
    out_specs.append(pl.BlockSpec((1, w), lambda j, r: (0, 0)))
    return pl.pallas_call(
        body, name=name, grid=(ncol, t // tr), in_specs=specs, out_specs=out_specs, out_shape=out_shape,
        compiler_params=_cparams(("arbitrary", "arbitrary")),
    )(*ins)


def _small_fn(x, pa, pb, nf, ng):
    lane = lax.broadcasted_iota(jnp.int32, x.shape, 1)
    zz = x + pb
    logf = -_softplus(-zz)
    g = -jnp.exp(pa) * _softplus(zz)
    beta = _sigmoid(x)
    return jnp.where(lane < nf, logf, jnp.where(lane < nf + ng, g, beta))


def _tri(n, upper):
    r = lax.broadcasted_iota(jnp.int32, (n, n), 0)
    c = lax.broadcasted_iota(jnp.int32, (n, n), 1)
    return jnp.where((c >= r) if upper else (c <= r), 1.0, 0.0).astype(F32)


def _small_fwd(p, off, pa, pb, nf, ng):
    t = p.shape[0]
    blk = HEAD_DIM
    nb = t // blk

    def body(x_ref, pa_ref, pb_ref, v_ref, c_ref):
        v_ref[...] = _small_fn(x_ref[...], pa_ref[...], pb_ref[...], nf, ng)
        tri = _tri(blk, False)

        carry = jnp.zeros((1, HEAD_DIM), F32)
        for i in range(nb):
            rows = slice(i * blk, (i + 1) * blk)
            c = _nn_hi(tri, v_ref[rows, :]) + carry
            c_ref[rows, :] = c
            carry = c[blk - 1:blk, :]

    row = pl.BlockSpec((1, HEAD_DIM), lambda i: (0, 0))
    out = pl.BlockSpec((t, HEAD_DIM), lambda i: (0, 0))
    return pl.pallas_call(
        body, name="small_fwd", grid=(1,),
        in_specs=[pl.BlockSpec((t, HEAD_DIM), lambda i: (0, off)), row, row], out_specs=[out, out],
        out_shape=[jax.ShapeDtypeStruct((t, HEAD_DIM), F32)] * 2,
        compiler_params=_cparams(("arbitrary",)),
    )(p, pa, pb)


def _small_bwd(p, off, pa, pb, dvals, dcsum, nf, ng):
    t = p.shape[0]
    blk = HEAD_DIM
    nb = t // blk

    def body(x_ref, pa_ref, pb_ref, dv_ref, dc_ref, dx_ref, dpa_ref, dpb_ref, tot_ref):
        tri = _tri(blk, True)

        carry = jnp.zeros((1, HEAD_DIM), F32)
        for i in reversed(range(nb)):
            rows = slice(i * blk, (i + 1) * blk)
            c = _nn_hi(tri, dc_ref[rows, :]) + carry
            tot_ref[rows, :] = c + dv_ref[rows, :]
            carry = c[0:1, :]
        f = functools.partial(_small_fn, nf=nf, ng=ng)
        _, vjp = jax.vjp(f, x_ref[...], pa_ref[...], pb_ref[...])
        dx, dpa, dpb = vjp(tot_ref[...])
        dx_ref[...] = dx
        dpa_ref[...] = dpa
        dpb_ref[...] = dpb

    row = pl.BlockSpec((1, HEAD_DIM), lambda i: (0, 0))
    full = pl.BlockSpec((t, HEAD_DIM), lambda i: (0, 0))
    return pl.pallas_call(
        body, name="small_bwd", grid=(1,),
        in_specs=[pl.BlockSpec((t, HEAD_DIM), lambda i: (0, off)), row, row, full, full],
        out_specs=[full, row, row],
        out_shape=[jax.ShapeDtypeStruct((t, HEAD_DIM), F32), jax.ShapeDtypeStruct((1, HEAD_DIM), F32),
                   jax.ShapeDtypeStruct((1, HEAD_DIM), F32)],
        scratch_shapes=[pltpu.VMEM((t, HEAD_DIM), F32)],
        compiler_params=_cparams(("arbitrary",)),
    )(p, pa, pb, dvals, dcsum)


def _fox_fwd(q, k, v, cc, cr, nf, tq, tk, d_mix):
    t = q.shape[0]
    scale = HEAD_DIM ** -0.5
    ratio = tq // tk

    def body(q_ref, k_ref, v_ref, cc_ref, cr_ref, o_ref, lse_ref, mix_ref):
        i = pl.program_id(1)
        qv = q_ref[...]
        ccol = cc_ref[0]
        rows = i * tq + lax.broadcasted_iota(jnp.int32, (tq, tk), 0)
        cols0 = lax.broadcasted_iota(jnp.int32, (tq, tk), 1)

        def step(j, carry):
            m, l, acc = carry
            ks = pl.ds(pl.multiple_of(j * tk, tk), tk)
            s = lax.dot_general(qv, k_ref[ks, :], (((1,), (1,)), ((), ())), preferred_element_type=F32) * scale
            s = s + ccol - cr_ref[0, j]
            s = jnp.where(cols0 + j * tk <= rows, s, NEG)
            m_new = jnp.maximum(m, jnp.max(s, axis=1, keepdims=True))
            pr = jnp.exp(s - m_new)
            alpha = jnp.exp(m - m_new)
            l = alpha * l + jnp.sum(pr, axis=1, keepdims=True)
            acc = alpha * acc + jnp.dot(pr.astype(BF16), v_ref[ks, :], preferred_element_type=F32)
            return m_new, l, acc

        init = (jnp.full((tq, 1), NEG, F32), jnp.zeros((tq, 1), F32), jnp.zeros((tq, HEAD_DIM), F32))
        m, l, acc = lax.fori_loop(0, (i + 1) * ratio, step, init)
        o_ref[...] = acc / l
        mix_ref[...] = (acc / l).astype(BF16)
        lse_ref[0] = m + jnp.log(l)

    head_all = pl.BlockSpec((t, HEAD_DIM), lambda h, i: (0, h))
    return pl.pallas_call(
        body, name="fox_fwd", grid=(nf, t // tq),
        in_specs=[pl.BlockSpec((tq, HEAD_DIM), lambda h, i: (i, h)), head_all, head_all,
                  pl.BlockSpec((1, tq, 1), lambda h, i: (h, i, 0)),
                  pl.BlockSpec((1, t // tk, 1, tk), lambda h, i: (h, 0, 0, 0))],
        out_specs=[pl.BlockSpec((tq, HEAD_DIM), lambda h, i: (i, h)),
                   pl.BlockSpec((1, tq, 1), lambda h, i: (h, i, 0)),
                   pl.BlockSpec((tq, HEAD_DIM), lambda h, i: (i, h))],
        out_shape=[jax.ShapeDtypeStruct((t, nf * HEAD_DIM), F32), jax.ShapeDtypeStruct((nf, t, 1), F32),
                   jax.ShapeDtypeStruct((t, d_mix), BF16)],
        compiler_params=_cparams(("parallel", "parallel")),
    )(q, k, v, cc, cr)


def _fox_bwd(q, k, v, cc, cr, o, lse, dmix, nf, tq, tk):
    t = q.shape[0]
    scale = HEAD_DIM ** -0.5
    ratio = tq // tk

    def body(q_ref, k_ref, v_ref, cc_ref, cr_ref, o_ref, lse_ref, do_ref,
             dq_ref, dk_ref, dv_ref, dcc_ref, dcr_ref):
        i = pl.program_id(1)

        @pl.when(i == 0)
        def _():
            dk_ref[...] = jnp.zeros_like(dk_ref)
            dv_ref[...] = jnp.zeros_like(dv_ref)
            dcr_ref[...] = jnp.zeros_like(dcr_ref)

        qv = q_ref[...]
        ccol = cc_ref[0]
        lse_v = lse_ref[0]
        do = do_ref[...]
        do_b = do.astype(BF16)
        delta = jnp.sum(do * o_ref[...], axis=1, keepdims=True)
        rows = i * tq + lax.broadcasted_iota(jnp.int32, (tq, tk), 0)
        cols0 = lax.broadcasted_iota(jnp.int32, (tq, tk), 1)

        def step(j, carry):
            dq, dcc = carry
            ks = pl.ds(pl.multiple_of(j * tk, tk), tk)
            kj, vj = k_ref[ks, :], v_ref[ks, :]
            s = lax.dot_general(qv, kj, (((1,), (1,)), ((), ())), preferred_element_type=F32) * scale
            s = s + ccol - cr_ref[0, j]
            pr = jnp.where(cols0 + j * tk <= rows, jnp.exp(s - lse_v), 0.0)
            dp = lax.dot_general(do_b, vj, (((1,), (1,)), ((), ())), preferred_element_type=F32)
            ds = pr * (dp - delta)
            ds_b = ds.astype(BF16)
            dq = dq + jnp.dot(ds_b, kj, preferred_element_type=F32) * scale
            dk_ref[ks, :] += lax.dot_general(ds_b, qv, (((0,), (0,)), ((), ())),
                                             preferred_element_type=F32) * scale
            dv_ref[ks, :] += lax.dot_general(pr.astype(BF16), do_b, (((0,), (0,)), ((), ())),
                                             preferred_element_type=F32)
            dcr_ref[0, j] -= jnp.sum(ds, axis=0, keepdims=True)
            return dq, dcc + jnp.sum(ds, axis=1, keepdims=True)

        init = (jnp.zeros((tq, HEAD_DIM), F32), jnp.zeros((tq, 1), F32))
        dq, dcc = lax.fori_loop(0, (i + 1) * ratio, step, init)
        dq_ref[...] = dq
        dcc_ref[0] = dcc

    head_all = pl.BlockSpec((t, HEAD_DIM), lambda h, i: (0, h))
    qblk = pl.BlockSpec((tq, HEAD_DIM), lambda h, i: (i, h))
    colv = pl.BlockSpec((1, tq, 1), lambda h, i: (h, i, 0))
    rowv = pl.BlockSpec((1, t // tk, 1, tk), lambda h, i: (h, 0, 0, 0))
    wide = jax.ShapeDtypeStruct((t, nf * HEAD_DIM), F32)
    return pl.pallas_call(
        body, name="fox_bwd", grid=(nf, t // tq),
        in_specs=[qblk, head_all, head_all, colv, rowv, qblk, colv, qblk],
        out_specs=[qblk, head_all, head_all, colv, rowv],
        out_shape=[wide, wide, wide, jax.ShapeDtypeStruct((nf, t, 1), F32),
                   jax.ShapeDtypeStruct((nf, t // tk, 1, tk), F32)],
        compiler_params=_cparams(("parallel", "arbitrary")),
    )(q, k, v, cc, cr, o, lse, dmix)


def _mem_fn(mq, mk, mv, gq, gk):
    qn = _rms_fn(mq, gq)
    kn = _rms_fn(mk, gk)
    s = _nt(qn, kn) * (HEAD_DIM ** -0.5)
    e = jnp.exp(s - lax.stop_gradient(jnp.max(s, axis=1, keepdims=True)))
    pr = e / jnp.sum(e, axis=1, keepdims=True)
    return _nn(pr, mv)


def _mem_specs(t, m, tq, qoff):
    qblk = pl.BlockSpec((tq, HEAD_DIM), lambda h, i: (i, qoff + h))
    kblk = pl.BlockSpec((m, HEAD_DIM), lambda h, i: (0, h))
    vblk = pl.BlockSpec((m, HEAD_DIM), lambda h, i: (0, N_MEM_HEADS + h))
    row = pl.BlockSpec((1, HEAD_DIM), lambda h, i: (0, 0))
    return qblk, kblk, vblk, row


def _mem_fwd(p, qoff, mkv, gq, gk, tq, into, into_off):
    t, m = p.shape[0], mkv.shape[0]
    qblk, kblk, vblk, row = _mem_specs(t, m, tq, qoff)

    def body(q_ref, k_ref, v_ref, gq_ref, gk_ref, _, o_ref):
        o_ref[...] = _mem_fn(q_ref[...], k_ref[...], v_ref[...], gq_ref[...], gk_ref[...]).astype(BF16)

    return pl.pallas_call(
        body, name="mem_fwd", grid=(N_MEM_HEADS, t // tq),
        in_specs=[qblk, kblk, vblk, row, row, pl.BlockSpec(memory_space=pl.ANY)],
        out_specs=pl.BlockSpec((tq, HEAD_DIM), lambda h, i: (i, into_off + h)),
        out_shape=jax.ShapeDtypeStruct(into.shape, BF16), input_output_aliases={5: 0},
        compiler_params=_cparams(("parallel", "parallel")),
    )(p, mkv, mkv, gq, gk, into)


def _mem_bwd(p, qoff, mkv, gq, gk, dmix, dooff, tq):
    t, m = p.shape[0], mkv.shape[0]
    qblk, kblk, vblk, row = _mem_specs(t, m, tq, qoff)

    def body(q_ref, k_ref, v_ref, gq_ref, gk_ref, do_ref, dq_ref, dkv_k_ref, dkv_v_ref, dgq_ref, dgk_ref):
        h, i = pl.program_id(0), pl.program_id(1)

        @pl.when((h == 0) & (i == 0))
        def _():
            dgq_ref[...] = jnp.zeros_like(dgq_ref)
            dgk_ref[...] = jnp.zeros_like(dgk_ref)

        @pl.when(i == 0)
        def _():
            dkv_k_ref[...] = jnp.zeros_like(dkv_k_ref)
            dkv_v_ref[...] = jnp.zeros_like(dkv_v_ref)

        _, vjp = jax.vjp(_mem_fn, q_ref[...], k_ref[...], v_ref[...], gq_ref[...], gk_ref[...])
        dq, dk, dv, dgq, dgk = vjp(do_ref[...])
        dq_ref[...] = dq
        dkv_k_ref[...] += dk
        dkv_v_ref[...] += dv
        dgq_ref[...] += dgq
        dgk_ref[...] += dgk

    oblk = pl.BlockSpec((tq, HEAD_DIM), lambda h, i: (i, h))
    kout = pl.BlockSpec((m, HEAD_DIM), lambda h, i: (0, h))
    half = jax.ShapeDtypeStruct((m, N_MEM_HEADS * HEAD_DIM), F32)
    rshape = jax.ShapeDtypeStruct((1, HEAD_DIM), F32)
    return pl.pallas_call(
        body, name="mem_bwd", grid=(N_MEM_HEADS, t // tq),
        in_specs=[qblk, kblk, vblk, row, row, pl.BlockSpec((tq, HEAD_DIM), lambda h, i: (i, dooff + h))],
        out_specs=[oblk, kout, kout, row, row],
        out_shape=[jax.ShapeDtypeStruct((t, N_MEM_HEADS * HEAD_DIM), F32), half, half, rshape, rshape],
        compiler_params=_cparams(("arbitrary", "arbitrary")),
    )(p, mkv, mkv, gq, gk, dmix)


def _shift_down(x, s):
    if s == 0:
        return x
    r = lax.broadcasted_iota(jnp.int32, x.shape, 0)
    return jnp.where(r >= s, pltpu.roll(x, s, 0), 0.0)


def _shift_up(x, s):
    if s == 0:
        return x
    n = x.shape[0]
    r = lax.broadcasted_iota(jnp.int32, x.shape, 0)
    return jnp.where(r < n - s, pltpu.roll(x, n - s, 0), 0.0)


def _conv_fn(x0, x1, x2, x3, w0, w1, w2, w3, kind):
    y = _silu(x0 * w0 + x1 * w1 + x2 * w2 + x3 * w3)
    if kind == 2:
        return y
    y = y * lax.rsqrt(jnp.sum(y * y, axis=-1, keepdims=True) + NORM_EPS)
    return y * (HEAD_DIM ** -0.5) if kind == 0 else y


def _conv_fwd(p, off, conv_w, ng):
    t = p.shape[0]

    def body(x_ref, w_ref, o_ref):
        kind = pl.program_id(0) // ng
        x = x_ref[...]
        xs = [_shift_down(x, CONV_WIDTH - 1 - j) for j in range(CONV_WIDTH)]
        ws = [w_ref[j:j + 1, :] for j in range(CONV_WIDTH)]
        for kd in range(3):
            @pl.when(kind == kd)
            def _(kd=kd):
                o_ref[...] = _conv_fn(*xs, *ws, kd)

    return pl.pallas_call(
        body, name="gdn_conv_fwd", grid=(3 * ng,),
        in_specs=[pl.BlockSpec((t, HEAD_DIM), lambda c: (0, off + c)),
                  pl.BlockSpec((CONV_WIDTH, HEAD_DIM), lambda c: (0, c))],
        out_specs=pl.BlockSpec((t, HEAD_DIM), lambda c: (0, c)),
        out_shape=jax.ShapeDtypeStruct((t, 3 * ng * HEAD_DIM), F32),
        compiler_params=_cparams(("parallel",)),
    )(p, conv_w)


def _conv_bwd(p, off, conv_w, dys, ng):
    t = p.shape[0]

    def body(x_ref, w_ref, dq_ref, dk_ref, dv_ref, dx_ref, dw_ref):
        kind = pl.program_id(0) // ng
        dy_refs = (dq_ref, dk_ref, dv_ref)
        x = x_ref[...]
        xs = [_shift_down(x, CONV_WIDTH - 1 - j) for j in range(CONV_WIDTH)]
        ws = [w_ref[j:j + 1, :] for j in range(CONV_WIDTH)]
        for kd in range(3):
            @pl.when(kind == kd)
            def _(kd=kd):
                _, vjp = jax.vjp(functools.partial(_conv_fn, kind=kd), *xs, *ws)
                g = vjp(dy_refs[kd][...])
                dx = _shift_up(g[0], CONV_WIDTH - 1)
                for j in range(1, CONV_WIDTH):
                    dx = dx + _shift_up(g[j], CONV_WIDTH - 1 - j)
                dx_ref[...] = dx
                for j in range(CONV_WIDTH):
                    dw_ref[j:j + 1, :] = g[CONV_WIDTH + j]

    blk = pl.BlockSpec((t, HEAD_DIM), lambda c: (0, c))
    head = pl.BlockSpec((t, HEAD_DIM), lambda c: (0, c % ng))
    wblk = pl.BlockSpec((CONV_WIDTH, HEAD_DIM), lambda c: (0, c))
    return pl.pallas_call(
        body, name="gdn_conv_bwd", grid=(3 * ng,),
        in_specs=[pl.BlockSpec((t, HEAD_DIM), lambda c: (0, off + c)), wblk] + [head] * 3,
        out_specs=[blk, wblk],
        out_shape=[jax.ShapeDtypeStruct((t, 3 * ng * HEAD_DIM), F32),
                   jax.ShapeDtypeStruct((CONV_WIDTH, 3 * ng * HEAD_DIM), F32)],
        compiler_params=_cparams(("parallel",)),
    )(p, conv_w, *dys)


def _wy_fn(q, k, v, gcol, grow, bcol):
    b, c, dk = q.shape
    r = lax.broadcasted_iota(jnp.int32, (1, c, c), 1)
    e = lax.broadcasted_iota(jnp.int32, (1, c, c), 2)
    tril, strict = e <= r, e < r
    gc_col = jnp.sum(jnp.where(tril, grow, 0.0), axis=2, keepdims=True)
    gc_row = jnp.sum(jnp.where(r <= e, gcol, 0.0), axis=1, keepdims=True)
    g_last = jnp.sum(gcol, axis=1, keepdims=True)
    decay = jnp.exp(jnp.where(tril, gc_col - gc_row, NEG))
    kb, vb = k * bcol, v * bcol
    lower = jnp.where(strict, _nt(kb, k) * decay, 0.0)
    inv = jnp.where(r == e, 1.0, 0.0) - lower
    pw = lower
    for _ in range(int(math.log2(c)) - 1):
        pw = _nn_x3(pw, pw)
        inv = inv + _nn_x3(inv, pw)
    u = _nn_x3(inv, vb)
    w = _nn_x3(inv, kb * jnp.exp(gc_col))
    attn = jnp.where(tril, _nt(q, k) * decay, 0.0)
    qg = q * jnp.exp(gc_col)
    kdec = k * jnp.exp(g_last - gc_col)
    egl = jnp.broadcast_to(jnp.exp(g_last), (b, 1, dk))
    return u, w, qg, kdec, attn, egl


def _scan_fn(u, w, qg, kdec, attn, egl, state):
    v_new = u - _nn(w, state)
    o = _nn(qg, state) + _nn(attn, v_new)
    return o, state * egl + _tn(kdec, v_new)


GDN_CHUNKS_PER_STEP = 4


def _gdn_fwd(qkv, gcol, grow, bcol, ng):
    t = qkv.shape[0]
    nch = t // CHUNK

    cb = GDN_CHUNKS_PER_STEP
    wy = _gdn_wy(qkv, gcol, grow, bcol, ng, cb)

    def body(u_ref, w_ref, qg_ref, kd_ref, at_ref, eg_ref, o_ref, st_ref, state):
        @pl.when(pl.program_id(0) == 0)
        def _():
            state[...] = jnp.zeros_like(state)

        st_ref[:, 0] = state[...]
        heads = lambda ref: jnp.stack([ref[:, h * HEAD_DIM:(h + 1) * HEAD_DIM] for h in range(ng)])
        o, new = _scan_fn(heads(u_ref), heads(w_ref), heads(qg_ref), heads(kd_ref), at_ref[:, 0], eg_ref[:, 0],
                          state[...])
        for h in range(ng):
            o_ref[:, h * HEAD_DIM:(h + 1) * HEAD_DIM] = o[h]
        state[...] = new

    w = ng * HEAD_DIM
    blk = pl.BlockSpec((CHUNK, w), lambda i: (i, 0))
    o, states = pl.pallas_call(
        body, name="gdn_scan_fwd", grid=(nch,),
        in_specs=[blk, blk, blk, blk, pl.BlockSpec((ng, 1, CHUNK, CHUNK), lambda i: (0, i, 0, 0)),
                  pl.BlockSpec((ng, 1, 1, HEAD_DIM), lambda i: (0, i, 0, 0))],
        out_specs=[blk, pl.BlockSpec((ng, 1, HEAD_DIM, HEAD_DIM), lambda i: (0, i, 0, 0))],
        out_shape=[jax.ShapeDtypeStruct((t, w), F32),
                   jax.ShapeDtypeStruct((ng, nch, HEAD_DIM, HEAD_DIM), F32)],
        scratch_shapes=[pltpu.VMEM((ng, HEAD_DIM, HEAD_DIM), F32)],
        compiler_params=_cparams(("arbitrary",)),
    )(*wy)
    return o, (wy, states)


def _wy_batch(q_ref, k_ref, v_ref, gc_ref, gr_ref, bc_ref, ng, cb):
    idx = [(c, h) for c in range(cb) for h in range(ng)]
    rows = lambda c: slice(c * CHUNK, (c + 1) * CHUNK)
    lanes = lambda h: slice(h * HEAD_DIM, (h + 1) * HEAD_DIM)
    wide = lambda ref: jnp.stack([ref[rows(c), lanes(h)] for c, h in idx])
    col = lambda ref: jnp.stack([ref[h, rows(c), :] for c, h in idx])
    return idx, (wide(q_ref), wide(k_ref), wide(v_ref), col(gc_ref), jnp.stack([gr_ref[h, c] for c, h in idx]),
                 col(bc_ref))


def _gdn_wy(qkv, gcol, grow, bcol, ng, cb):
    t = qkv.shape[0]
    nch = t // CHUNK

    def body(q_ref, k_ref, v_ref, gc_ref, gr_ref, bc_ref, u_ref, w_ref, qg_ref, kd_ref, at_ref, eg_ref):
        idx, args = _wy_batch(q_ref, k_ref, v_ref, gc_ref, gr_ref, bc_ref, ng, cb)
        u, w, qg, kd, at, eg = _wy_fn(*args)
        for b, (c, h) in enumerate(idx):
            rows, lanes = slice(c * CHUNK, (c + 1) * CHUNK), slice(h * HEAD_DIM, (h + 1) * HEAD_DIM)
            u_ref[rows, lanes] = u[b]
            w_ref[rows, lanes] = w[b]
            qg_ref[rows, lanes] = qg[b]
            kd_ref[rows, lanes] = kd[b]
            at_ref[h, c] = at[b]
            eg_ref[h, c] = eg[b]

    wd = ng * HEAD_DIM
    blk = lambda o: pl.BlockSpec((cb * CHUNK, wd), lambda i: (i, o))
    col = pl.BlockSpec((ng, cb * CHUNK, 1), lambda i: (0, i, 0))
    wide = jax.ShapeDtypeStruct((t, wd), F32)
    return pl.pallas_call(
        body, name="gdn_wy_fwd", grid=(nch // cb,),
        in_specs=[blk(0), blk(1), blk(2), col, pl.BlockSpec((ng, cb, 1, CHUNK), lambda i: (0, i, 0, 0)), col],
        out_specs=[blk(0), blk(0), blk(0), blk(0), pl.BlockSpec((ng, cb, CHUNK, CHUNK), lambda i: (0, i, 0, 0)),
                   pl.BlockSpec((ng, cb, 1, HEAD_DIM), lambda i: (0, i, 0, 0))],
        out_shape=[wide, wide, wide, wide, jax.ShapeDtypeStruct((ng, nch, CHUNK, CHUNK), F32),
                   jax.ShapeDtypeStruct((ng, nch, 1, HEAD_DIM), F32)],
        compiler_params=_cparams(("parallel",)),
    )(qkv, qkv, qkv, gcol, grow, bcol)


def _gdn_bwd(qkv, gcol, grow, bcol, saved, do, ng):
    t = qkv.shape[0]
    nch = t // CHUNK
    cb = GDN_CHUNKS_PER_STEP // 2
    wy, states = saved
    wd = ng * HEAD_DIM

    def scan_body(u_ref, w_ref, qg_ref, kd_ref, at_ref, eg_ref, st_ref, do_ref,
                  du_ref, dw_ref, dqg_ref, dkd_ref, dat_ref, deg_ref, dstate):
        @pl.when(pl.program_id(0) == 0)
        def _():
            dstate[...] = jnp.zeros_like(dstate)

        heads = lambda ref: jnp.stack([ref[:, h * HEAD_DIM:(h + 1) * HEAD_DIM] for h in range(ng)])
        _, vjp = jax.vjp(_scan_fn, heads(u_ref), heads(w_ref), heads(qg_ref), heads(kd_ref), at_ref[:, 0],
                         eg_ref[:, 0], st_ref[:, 0])
        du, dw, dqg, dkd, dat, deg, dst = vjp((heads(do_ref), dstate[...]))
        for h in range(ng):
            lanes = slice(h * HEAD_DIM, (h + 1) * HEAD_DIM)
            du_ref[:, lanes] = du[h]
            dw_ref[:, lanes] = dw[h]
            dqg_ref[:, lanes] = dqg[h]
            dkd_ref[:, lanes] = dkd[h]
        dat_ref[:, 0] = dat
        deg_ref[:, 0] = deg
        dstate[...] = dst

    rev = lambda i: nch - 1 - i
    blk = pl.BlockSpec((CHUNK, wd), lambda i: (rev(i), 0))
    atb = pl.BlockSpec((ng, 1, CHUNK, CHUNK), lambda i: (0, rev(i), 0, 0))
    egb = pl.BlockSpec((ng, 1, 1, HEAD_DIM), lambda i: (0, rev(i), 0, 0))
    wide = jax.ShapeDtypeStruct((t, wd), F32)
    at_shape = jax.ShapeDtypeStruct((ng, nch, CHUNK, CHUNK), F32)
    eg_shape = jax.ShapeDtypeStruct((ng, nch, 1, HEAD_DIM), F32)
    dwy = pl.pallas_call(
        scan_body, name="gdn_scan_bwd", grid=(nch,),
        in_specs=[blk, blk, blk, blk, atb, egb,
                  pl.BlockSpec((ng, 1, HEAD_DIM, HEAD_DIM), lambda i: (0, rev(i), 0, 0)), blk],
        out_specs=[blk, blk, blk, blk, atb, egb],
        out_shape=[wide, wide, wide, wide, at_shape, eg_shape],
        scratch_shapes=[pltpu.VMEM((ng, HEAD_DIM, HEAD_DIM), F32)],
        compiler_params=_cparams(("arbitrary",)),
    )(*wy, states, do)

    def wy_body(q_ref, k_ref, v_ref, gc_ref, gr_ref, bc_ref, du_ref, dw_ref, dqg_ref, dkd_ref, dat_ref, deg_ref,
                dq_ref, dk_ref, dv_ref, dgc_ref, dgr_ref, dbc_ref):
        idx, args = _wy_batch(q_ref, k_ref, v_ref, gc_ref, gr_ref, bc_ref, ng, cb)
        rows = lambda c: slice(c * CHUNK, (c + 1) * CHUNK)
        lanes = lambda h: slice(h * HEAD_DIM, (h + 1) * HEAD_DIM)
        wide_ct = lambda ref: jnp.stack([ref[rows(c), lanes(h)] for c, h in idx])
        cts = (wide_ct(du_ref), wide_ct(dw_ref), wide_ct(dqg_ref), wide_ct(dkd_ref),
               jnp.stack([dat_ref[h, c] for c, h in idx]), jnp.stack([deg_ref[h, c] for c, h in idx]))
        _, vjp = jax.vjp(_wy_fn, *args)
        dq, dk, dv, dgc, dgr, dbc = vjp(cts)
        for b, (c, h) in enumerate(idx):
            dq_ref[rows(c), lanes(h)] = dq[b]
            dk_ref[rows(c), lanes(h)] = dk[b]
            dv_ref[rows(c), lanes(h)] = dv[b]
            dgc_ref[h, rows(c), :] = dgc[b]
            dgr_ref[h, c] = dgr[b]
            dbc_ref[h, rows(c), :] = dbc[b]

    cblk = lambda o: pl.BlockSpec((cb * CHUNK, wd), lambda i: (i, o))
    col = pl.BlockSpec((ng, cb * CHUNK, 1), lambda i: (0, i, 0))
    rowv = pl.BlockSpec((ng, cb, 1, CHUNK), lambda i: (0, i, 0, 0))
    cshape = jax.ShapeDtypeStruct((ng, t, 1), F32)
    return pl.pallas_call(
        wy_body, name="gdn_wy_bwd", grid=(nch // cb,),
        in_specs=[cblk(0), cblk(1), cblk(2), col, rowv, col, cblk(0), cblk(0), cblk(0), cblk(0),
                  pl.BlockSpec((ng, cb, CHUNK, CHUNK), lambda i: (0, i, 0, 0)),
                  pl.BlockSpec((ng, cb, 1, HEAD_DIM), lambda i: (0, i, 0, 0))],
        out_specs=[cblk(0), cblk(0), cblk(0), col, rowv, col],
        out_shape=[wide, wide, wide, cshape, jax.ShapeDtypeStruct((ng, nch, 1, CHUNK), F32), cshape],
        compiler_params=_cparams(("parallel",)),
    )(qkv, qkv, qkv, gcol, grow, bcol, *dwy)


def _swiglu_fn(gate, up):
    return _silu(gate) * up


FFN_TN = 256


def _ffn_up(n2, wgu4):
    _, d, w = wgu4.shape
    t = n2.shape[0]
    tn = _tile(w, FFN_TN)
    nb = w // tn

    def body(a_ref, b_ref, gu_ref, act_ref):
        av = a_ref[...]
        gate = jnp.dot(av, b_ref[0], preferred_element_type=F32)
        up = jnp.dot(av, b_ref[1], preferred_element_type=F32)
        gu_ref[0] = gate.astype(BF16)
        gu_ref[1] = up.astype(BF16)
        act_ref[...] = _swiglu_fn(gate, up).astype(BF16)

    return pl.pallas_call(
        body, name="ffn_up", grid=(2, nb),
        in_specs=[pl.BlockSpec((t, d), lambda j, l: (0, 0)), pl.BlockSpec((2, d, tn), lambda j, l: (j, 0, l))],
        out_specs=[pl.BlockSpec((2, t, tn), lambda j, l: (j, 0, l)),
                   pl.BlockSpec((t, tn), lambda j, l: (0, j * nb + l))],
        out_shape=[jax.ShapeDtypeStruct((4, t, w), BF16), jax.ShapeDtypeStruct((t, 2 * w), BF16)],
        compiler_params=_cparams(("parallel", "parallel")),
    )(n2, wgu4)


def _ffn_dact(dh2, wd, gu):
    _, t, w = gu.shape
    d = dh2.shape[1]
    tn = _tile(w, FFN_TN)
    nb = w // tn

    def body(a_ref, b_ref, gu_ref, o_ref):
        dact = lax.dot_general(a_ref[...], b_ref[...], (((1,), (1,)), ((), ())), preferred_element_type=F32)
        _, vjp = jax.vjp(_swiglu_fn, gu_ref[0].astype(F32), gu_ref[1].astype(F32))
        dg, du = vjp(dact)
        o_ref[0] = dg.astype(BF16)
        o_ref[1] = du.astype(BF16)

    pair = pl.BlockSpec((2, t, tn), lambda j, l: (j, 0, l))
    return pl.pallas_call(
        body, name="ffn_dact", grid=(2, nb),
        in_specs=[pl.BlockSpec((t, d), lambda j, l: (0, 0)), pl.BlockSpec((tn, d), lambda j, l: (j * nb + l, 0)),
                  pair],
        out_specs=pair, out_shape=jax.ShapeDtypeStruct(gu.shape, BF16),
        compiler_params=_cparams(("parallel", "parallel")),
    )(dh2, wd, gu)


def _loss_head(h2, target):
    t, d = h2.shape
    tr = _tile(t, 256, 8)

    def body(h_ref, t_ref, l_ref, d_ref, db_ref):
        @pl.when(pl.program_id(0) == 0)
        def _():
            l_ref[...] = jnp.zeros_like(l_ref)

        err = h_ref[...] - t_ref[...]
        d_ref[...] = err * (1.0 / d)
        db_ref[...] = (err * (1.0 / d)).astype(BF16)
        part = 0.5 * jnp.sum(jnp.mean(err * err, axis=-1, keepdims=True), axis=0, keepdims=True)
        lane = lax.broadcasted_iota(jnp.int32, (8, HEAD_DIM), 1)
        row = lax.broadcasted_iota(jnp.int32, (8, HEAD_DIM), 0)
        l_ref[...] += jnp.where((lane == 0) & (row == 0), part, 0.0)

    blk = pl.BlockSpec((tr, d), lambda r: (r, 0))
    return pl.pallas_call(
        body, name="loss_head", grid=(t // tr,), in_specs=[blk, blk],
        out_specs=[pl.BlockSpec((8, HEAD_DIM), lambda r: (0, 0)), blk, blk],
        out_shape=[jax.ShapeDtypeStruct((8, HEAD_DIM), F32), jax.ShapeDtypeStruct((t, d), F32),
                   jax.ShapeDtypeStruct((t, d), BF16)],
        compiler_params=_cparams(("arbitrary",)),
    )(h2, target)


def _adamw(w, g, m, v, *, g_fn=None, name):
    r, c = w.shape
    tr = _tile(r, max(8, (1 << 19) // c // 8 * 8), 8)

    def body(w_ref, g_ref, m_ref, v_ref, go_ref, d_ref, mo_ref, vo_ref):
        gr = g_ref[...] if g_fn is None else g_fn(g_ref[...])
        mn = ADAM_B1 * m_ref[...] + (1.0 - ADAM_B1) * gr
        vn = ADAM_B2 * v_ref[...] + (1.0 - ADAM_B2) * (gr * gr)
        m_hat = mn / (1.0 - ADAM_B1 ** ADAM_STEP)
        v_hat = vn / (1.0 - ADAM_B2 ** ADAM_STEP)
        go_ref[...] = gr
        d_ref[...] = -ADAM_LR * (m_hat / (jnp.sqrt(v_hat) + ADAM_EPS) + ADAM_WD * w_ref[...])
        mo_ref[...] = mn
        vo_ref[...] = vn

    blk = pl.BlockSpec((tr, c), lambda i: (i, 0))
    gblk = pl.BlockSpec((tr, g.shape[1]), lambda i: (i, 0))
    return pl.pallas_call(
        body, name=name, grid=(r // tr,), in_specs=[blk, gblk, blk, blk], out_specs=[blk] * 4,
        out_shape=[jax.ShapeDtypeStruct((r, c), F32)] * 4,
        compiler_params=_cparams(("parallel",)),
    )(w, g, m, v)


class _Layout:
    def __init__(self, d):
        nh = d // HEAD_DIM
        self.nm = N_MEM_HEADS
        self.nf = (nh - self.nm) // 2
        self.ng = nh - self.nm - self.nf
        nf, ng, nm = self.nf, self.ng, self.nm
        self.o_fq, self.o_fk, self.o_fv = 0, nf, 2 * nf
        self.o_gq = 3 * nf
        self.o_gz = 3 * nf + 3 * ng
        self.o_mq = 3 * nf + 4 * ng
        self.o_sm = self.o_mq + nm
        self.blocks = -(-(self.o_sm + 1) // 8) * 8
        self.cols = self.blocks * HEAD_DIM
        hd = HEAD_DIM
        sizes = [nf * hd, nf * hd, nf * hd, nf, 3 * ng * hd, ng * hd, ng, ng, nm * hd]
        starts = [sum(sizes[:i]) for i in range(len(sizes))]
        self.ref = list(zip(starts, sizes))
        self.in_cols = sum(sizes)

    def regroup(self, w):
        part = lambda i: w[:, self.ref[i][0]:self.ref[i][0] + self.ref[i][1]]
        pieces = [part(0), part(1), part(2), part(4), part(5), part(8), part(3), part(6), part(7)]
        pad = self.cols - self.in_cols
        return jnp.concatenate(pieces + [jnp.zeros((w.shape[0], pad), w.dtype)], axis=1)

    def ungroup(self, g):
        hd, nf, ng, nm = HEAD_DIM, self.nf, self.ng, self.nm
        sm = self.o_sm * hd
        return jnp.concatenate([
            g[:, :3 * nf * hd], g[:, sm:sm + nf], g[:, self.o_gq * hd:self.o_gz * hd],
            g[:, self.o_gz * hd:self.o_mq * hd], g[:, sm + nf:sm + nf + ng], g[:, sm + nf + ng:sm + nf + 2 * ng],
            g[:, self.o_mq * hd:self.o_sm * hd]], axis=1)


def _lane_row(pieces):
    row = jnp.zeros((1, HEAD_DIM), F32)
    for off, a in pieces:
        row = lax.dynamic_update_slice(row, a.astype(F32), (0, off))
    return row


def _local_step(x, mem, target, weights, reduce_start, sp):
    t, d = x.shape
    lay = _Layout(d)
    nf, ng, nm, hd = lay.nf, lay.ng, lay.nm, HEAD_DIM
    nch = t // CHUNK
    tq = _tile(t, 256)
    tk = tq

    u = _norm_fwd(x, 0, sp["norm_mix"], 1, d, BF16, name="norm_mix_fwd")
    (win,) = weights("in", u)
    p = _mm(u, win, name="mm_in")
    wmkv, conv_taps = weights("mixer", p)
    sp = dict(sp, gdn_conv=conv_taps)
    pa = _lane_row([(nf, sp["gdn_a_log"])])
    pb = _lane_row([(0, sp["fox_f_bias"]), (nf, sp["gdn_dt_bias"])])
    vals, csum = _small_fwd(p, lay.o_sm, pa, pb, nf, ng)

    c_t = csum[:, :nf].T
    cc, cr = c_t.reshape(nf, t, 1), c_t.reshape(nf, t // tk, 1, tk)
    fq = _norm_fwd(p, lay.o_fq, sp["fox_q_norm"], nf, hd, BF16, name="fox_qnorm_fwd")
    fk = _norm_fwd(p, lay.o_fk, sp["fox_k_norm"], nf, hd, BF16, name="fox_knorm_fwd")
    fv = p[:, lay.o_fv * hd:(lay.o_fv + nf) * hd].astype(BF16)
    o_fox, lse, mix = _fox_fwd(fq, fk, fv, cc, cr, nf, tq, tk, d)

    qkv = _conv_fwd(p, lay.o_gq, sp["gdn_conv"], ng)
    g_t, b_t = vals[:, nf:nf + ng].T, vals[:, nf + ng:nf + 2 * ng].T
    gcol, grow, bcol = g_t.reshape(ng, t, 1), g_t.reshape(ng, nch, 1, CHUNK), b_t.reshape(ng, t, 1)
    o_g, states = _gdn_fwd(qkv, gcol, grow, bcol, ng)
    mix = _norm_fwd(o_g, 0, sp["gdn_out_norm"], ng, hd, BF16, z=p, zoff=lay.o_gz, into=mix, into_off=nf,
                    name="gdn_out_fwd")

    mem_n = _norm_fwd(mem, 0, sp["mem_norm"], 1, d, BF16, name="mem_norm_fwd")
    mkv = _mm(mem_n, wmkv, name="mm_memkv")
    mix = _mem_fwd(p, lay.o_mq, mkv, sp["mem_q_norm"], sp["mem_k_norm"], tq, mix, nf + ng)
    (wout,) = weights("out", mix)
    h1 = _mm(mix, wout, res=x, name="mm_out")
    n2 = _norm_fwd(h1, 0, sp["norm_ffn"], 1, d, BF16, name="norm_ffn_fwd")
    (wgu,) = weights("gate_up", n2)
    wgu4 = wgu.reshape(4, d, -1)
    gu, act = _ffn_up(n2, wgu4)
    (wd,) = weights("down", act)
    h2 = _mm(act, wd, res=h1, name="mm_down")
    loss_blk, dh2, dh2_b = _loss_head(h2, target)

    g = {}
    dw_down = _mm(act, dh2_b, ta=True, out_dtype=BF16, name="mm_dw_down")
    dgu = _ffn_dact(dh2_b, wd, gu)
    dw_gate_up = _mm(n2, dgu, ta=True, stack="out", out_dtype=BF16, name="mm_dw_gate_up").reshape(wgu.shape)
    token = reduce_start("ffn", {"w_down": dw_down, "w_gate_up": dw_gate_up})
    dn2 = _mm(dgu, wgu4, tb=True, stack="sum", name="mm_dn2")
    dh1, g["norm_ffn"] = _norm_bwd(h1, 0, sp["norm_ffn"] + token[0, 0], dn2, 0, 1, d, res=dh2,
                                   name="norm_ffn_bwd")
    dw_out = _mm(mix, dh1, ta=True, out_dtype=BF16, name="mm_dw_out")
    dmix = _mm(dh1, wout, tb=True, name="mm_dmix")

    dmq, dmk, dmv, g["mem_q_norm"], g["mem_k_norm"] = _mem_bwd(
        p, lay.o_mq, mkv, sp["mem_q_norm"], sp["mem_k_norm"], dmix, nf + ng, tq)
    dmkv = jnp.concatenate([dmk, dmv], axis=1)
    dw_mem_kv = _mm(mem_n, dmkv, ta=True, out_dtype=BF16, name="mm_dw_memkv")
    token = reduce_start("mix", {"w_out": dw_out, "w_mem_kv": dw_mem_kv})
    dmem_n = _mm(dmkv, wmkv, tb=True, name="mm_dmem")
    _, g["mem_norm"] = _norm_bwd(mem, 0, sp["mem_norm"], dmem_n, 0, 1, d, name="mem_norm_bwd")

    do_g, dgz, g["gdn_out_norm"] = _norm_bwd(o_g, 0, sp["gdn_out_norm"] + token[0, 0], dmix, nf, ng, hd, z=p,
                                             zoff=lay.o_gz, name="gdn_out_bwd")
    dq, dk, dv, dgc, dgr, dbc = _gdn_bwd(qkv, gcol, grow, bcol, states, do_g, ng)
    dgqkv, g["gdn_conv"] = _conv_bwd(p, lay.o_gq, sp["gdn_conv"], (dq, dk, dv), ng)
    dg_t = dgc.reshape(ng, t) + dgr.reshape(ng, t)
    db_t = dbc.reshape(ng, t)

    dfq_n, dfk_n, dfv, dcc, dcr = _fox_bwd(fq, fk, fv, cc, cr, o_fox, lse, dmix, nf, tq, tk)
    dfq, g["fox_q_norm"] = _norm_bwd(p, lay.o_fq, sp["fox_q_norm"], dfq_n, 0, nf, hd, name="fox_qnorm_bwd")
    dfk, g["fox_k_norm"] = _norm_bwd(p, lay.o_fk, sp["fox_k_norm"], dfk_n, 0, nf, hd, name="fox_knorm_bwd")
    dc_t = dcc.reshape(nf, t) + dcr.reshape(nf, t)

    lanes_left = hd - nf - 2 * ng
    dvals = jnp.concatenate([jnp.zeros((t, nf), F32), dg_t.T, db_t.T, jnp.zeros((t, lanes_left), F32)], axis=1)
    dcsum = jnp.concatenate([dc_t.T, jnp.zeros((t, hd - nf), F32)], axis=1)
    dsm, dpa, dpb = _small_bwd(p, lay.o_sm, pa, pb, dvals, dcsum, nf, ng)
    g["fox_f_bias"] = dpb[:, :nf]
    g["gdn_dt_bias"] = dpb[:, nf:nf + ng]
    g["gdn_a_log"] = dpa[:, nf:nf + ng]

    pad = jnp.zeros((t, lay.cols - (lay.o_sm + 1) * hd), F32)
    dp = jnp.concatenate([dfq, dfk, dfv, dgqkv, dgz, dmq, dsm, pad], axis=1)
    token = reduce_start("in", {"w_in": _mm(u, dp, ta=True, out_dtype=BF16, name="mm_dw_in")})
    du = _mm(dp, win, tb=True, after=token, name="mm_du")
    dx, g["norm_mix"] = _norm_bwd(x, 0, sp["norm_mix"], du, 0, 1, d, res=dh1, name="norm_mix_bwd")
    return loss_blk, dx, g


ANY = pl.BlockSpec(memory_space=pl.ANY)


def _me():
    x, y, c = lax.axis_index("x"), lax.axis_index("y"), lax.axis_index("c")
    chips = [(1 - x, y), (x, 1 - y), (1 - x, 1 - y)]
    return x, y, c, chips


def _slot(axis, k):
    return k if axis == 0 else 2 * (k % 2) + k // 2


def _slab(ref, axis, rows, cols, k, h):
    half = rows // 2
    return ref.at[pl.ds(_slot(axis, k) * rows + h * half, half), :]


def _remote(src, dst, send_sem, recv_sem, dev):
    return pltpu.make_async_remote_copy(src_ref=src, dst_ref=dst, send_sem=send_sem, recv_sem=recv_sem,
                                        device_id=dev, device_id_type=MESH)


HBM = pl.BlockSpec(memory_space=pltpu.HBM)
SEM = pl.BlockSpec(memory_space=pltpu.SEMAPHORE)
SPLIT = pltpu.CompilerParams(has_side_effects=pltpu.SideEffectType.DATAFLOW_SIDE_EFFECTING)
TOKEN = jax.ShapeDtypeStruct((8, HEAD_DIM), F32)


def _in_hbm(v):
    return pltpu.with_memory_space_constraint(v, pltpu.HBM)


def _cast_place(shard, axis, name, col_fn=None, out_cols=None):
    r, c = shard.shape
    oc = out_cols or c
    tr = _tile(r, 512 if col_fn is None else 64, 16)
    tc = _tile(c, 2048) if col_fn is None else c
    otc = tc if col_fn is None else oc
    nb = r // tr
    chip = 2 * lax.axis_index("x") + lax.axis_index("y")
    slot = jnp.reshape(_slot(axis, chip), (1,)).astype(jnp.int32)

    def body(slot_ref, x_ref, o_ref):
        x = x_ref[...]
        o_ref[...] = (x if col_fn is None else col_fn(x)).astype(BF16)

    return pl.pallas_call(
        body, name=name,
        grid_spec=pltpu.PrefetchScalarGridSpec(
            num_scalar_prefetch=1, grid=(nb, c // tc),
            in_specs=[pl.BlockSpec((tr, tc), lambda i, l, s: (i, l))],
            out_specs=pl.BlockSpec((tr, otc), lambda i, l, s: (s[0] * nb + i, l))),
        out_shape=jax.ShapeDtypeStruct((4 * r, oc), BF16),
        compiler_params=_cparams(("parallel", "parallel")),
    )(slot, shard)


def _gather_start(bufs, axes, shapes, groups, name):
    n = len(bufs)

    def body(*refs):
        dst = refs[n:2 * n]
        sems = refs[2 * n:2 * n + 2 * len(groups)]
        token = refs[-1]
        x, y, c, chips = _me()
        k = 2 * x + y
        for gi, ws in enumerate(groups):
            for i, w in enumerate(ws):
                r, cl = shapes[w]
                place = _slab(dst[w], axes[w], r, cl, k, c)
                for j, (px, py) in enumerate(chips):
                    _remote(place, place, sems[2 * gi].at[3 * i + j], sems[2 * gi + 1].at[3 * i + j],
                            (px, py, c)).start()
        token[...] = jnp.zeros_like(token)

    sem_shapes = [pltpu.SemaphoreType.DMA((3 * len(ws),)) for ws in groups for _ in range(2)]
    outs = pl.pallas_call(
        body, name=name, in_specs=[HBM] * n,
        out_specs=[HBM] * n + [SEM] * len(sem_shapes) + [pl.BlockSpec(memory_space=pltpu.VMEM)],
        out_shape=[pltpu.HBM(b.shape, b.dtype) for b in bufs] + sem_shapes + [TOKEN],
        input_output_aliases={w: w for w in range(n)}, compiler_params=SPLIT,
    )(*[_in_hbm(b) for b in bufs])
    sems = outs[n:-1]
    return outs[:n], [(sems[2 * g], sems[2 * g + 1]) for g in range(len(groups))], outs[-1]


def _gather_wait(bufs, axes, shapes, sems, after, name):
    n = len(bufs)

    def body(*refs):
        send_sems, recv_sems = refs[n], refs[n + 1]
        dst = refs[n + 3:]
        x, y, c, chips = _me()
        k = 2 * x + y
        for i in range(n):
            r, cl = shapes[i]
            for j, (px, py) in enumerate(chips):
                got = _slab(dst[i], axes[i], r, cl, 2 * px + py, c)
                _remote(got, got, send_sems.at[3 * i + j], recv_sems.at[3 * i + j], (px, py, c)).wait_recv()
        for i in range(n):
            r, cl = shapes[i]
            mine = _slab(dst[i], axes[i], r, cl, k, c)
            for j, (px, py) in enumerate(chips):
                _remote(mine, mine, send_sems.at[3 * i + j], recv_sems.at[3 * i + j], (px, py, c)).wait_send()

    return pl.pallas_call(
        body, name=name, in_specs=[HBM] * n + [SEM, SEM, ANY], out_specs=[HBM] * n,
        out_shape=[pltpu.HBM(b.shape, b.dtype) for b in bufs],
        input_output_aliases={i: i for i in range(n)}, compiler_params=SPLIT,
    )(*bufs, sems[0], sems[1], after)


def _gather_forward(bufs, axes, shapes, name):
    n = len(bufs)

    def body(*refs):
        dst = refs[n:2 * n]
        send_sems, recv_sems = refs[2 * n:]
        x, y, c, chips = _me()
        sibling = (x, y, 1 - c)
        sends = []
        for i in range(n):
            r, cl = shapes[i]
            for j, (px, py) in enumerate(chips):
                got = _slab(dst[i], axes[i], r, cl, 2 * px + py, c)
                cp = _remote(got, got, send_sems.at[3 * i + j], recv_sems.at[3 * i + j], sibling)
                cp.start()
                sends.append(cp)
        for i in range(n):
            r, cl = shapes[i]
            for j, (px, py) in enumerate(chips):
                got = _slab(dst[i], axes[i], r, cl, 2 * px + py, 1 - c)
                _remote(got, got, send_sems.at[3 * i + j], recv_sems.at[3 * i + j], sibling).wait_recv()
        for cp in sends:
            cp.wait_send()

    return pl.pallas_call(
        body, name=name, in_specs=[ANY] * n, out_specs=[ANY] * n,
        out_shape=[jax.ShapeDtypeStruct(b.shape, b.dtype) for b in bufs],
        input_output_aliases={i: i for i in range(n)},
        scratch_shapes=[pltpu.SemaphoreType.DMA((3 * n,)), pltpu.SemaphoreType.DMA((3 * n,))],
    )(*bufs)


def _pair_exchange(fulls, axes, shapes, tag):
    n = len(fulls)

    def body(*refs):
        src, dst = refs[:n], refs[n:2 * n]
        send_sems, recv_sems = refs[2 * n:]
        x, y, c, _ = _me()
        sibling = (x, y, 1 - c)
        cps = []
        for w in range(n):
            r, cl = shapes[w]
            for j in range(4):
                cp = _remote(_slab(src[w], axes[w], r, cl, j, 1 - c), dst[w].at[j],
                             send_sems.at[4 * w + j], recv_sems.at[4 * w + j], sibling)
                cp.start()
                cps.append(cp)
        for cp in cps:
            cp.wait()

    out_shape = [jax.ShapeDtypeStruct((4, r // 2, cl), f.dtype) for (r, cl), f in zip(shapes, fulls)]
    return pl.pallas_call(
        body, name="reduce_pair_exchange_" + tag, in_specs=[ANY] * n, out_specs=[ANY] * n, out_shape=out_shape,
        scratch_shapes=[pltpu.SemaphoreType.DMA((4 * n,)), pltpu.SemaphoreType.DMA((4 * n,))],
    )(*fulls)


def _chip_start(parts, tag):
    n = len(parts)

    def body(*refs):
        src, land = refs[2 * n:3 * n], refs[3 * n:4 * n]
        send_sems, recv_sems, token = refs[4 * n:]
        x, y, c, chips = _me()
        k = 2 * x + y
        for w in range(n):
            for j, (px, py) in enumerate(chips):
                _remote(src[w].at[2 * px + py], land[w].at[k], send_sems.at[3 * w + j], recv_sems.at[3 * w + j],
                        (px, py, c)).start()
        token[...] = jnp.zeros_like(token)

    lands = [lax.empty(p.shape, p.dtype) for p in parts]
    sem = pltpu.SemaphoreType.DMA((3 * n,))
    outs = pl.pallas_call(
        body, name="reduce_ici_start_" + tag, in_specs=[HBM] * (2 * n),
        out_specs=[HBM] * (2 * n) + [SEM, SEM, pl.BlockSpec(memory_space=pltpu.VMEM)],
        out_shape=[pltpu.HBM(p.shape, p.dtype) for p in parts + lands] + [sem, sem, TOKEN],
        input_output_aliases={i: i for i in range(2 * n)}, compiler_params=SPLIT,
    )(*[_in_hbm(v) for v in parts + lands])
    return outs[:n], outs[n:2 * n], outs[2 * n], outs[2 * n + 1], outs[-1]


def _chip_wait(parts, lands, send_sems, recv_sems, after, tag):
    n = len(parts)

    def body(*refs):
        send, recv = refs[2 * n], refs[2 * n + 1]
        src, land = refs[2 * n + 3:3 * n + 3], refs[3 * n + 3:]
        x, y, c, chips = _me()
        for w in range(n):
            for j, (px, py) in enumerate(chips):
                got = land[w].at[2 * px + py]
                _remote(got, got, send.at[3 * w + j], recv.at[3 * w + j], (px, py, c)).wait_recv()
        for w in range(n):
            for j, (px, py) in enumerate(chips):
                sent = src[w].at[2 * px + py]
                _remote(sent, sent, send.at[3 * w + j], recv.at[3 * w + j], (px, py, c)).wait_send()

    outs = pl.pallas_call(
        body, name="reduce_ici_wait_" + tag, in_specs=[HBM] * (2 * n) + [SEM, SEM, ANY], out_specs=[HBM] * (2 * n),
        out_shape=[pltpu.HBM(p.shape, p.dtype) for p in parts + lands],
        input_output_aliases={i: i for i in range(2 * n)}, compiler_params=SPLIT,
    )(*parts, *lands, send_sems, recv_sems, after)
    chip = 2 * lax.axis_index("x") + lax.axis_index("y")
    return [lax.dynamic_update_slice(s, lax.dynamic_index_in_dim(p, chip, 0, keepdims=True), (chip, 0, 0))
            for p, s in zip(outs[:n], outs[n:])]


def _half_swap(halves, tag):
    n = len(halves)
    core = lax.axis_index("c")
    bufs = [lax.dynamic_update_slice(lax.empty((2,) + h.shape, h.dtype), h[None], (core, 0, 0)) for h in halves]

    def body(*refs):
        dst = refs[n:2 * n]
        send_sems, recv_sems = refs[2 * n:]
        x, y, c, _ = _me()
        sibling = (x, y, 1 - c)
        cps = []
        for w in range(n):
            cp = _remote(dst[w].at[c], dst[w].at[c], send_sems.at[w], recv_sems.at[w], sibling)
            cp.start()
            cps.append(cp)
        for w in range(n):
            other = dst[w].at[1 - c]
            _remote(other, other, send_sems.at[w], recv_sems.at[w], sibling).wait_recv()
        for cp in cps:
            cp.wait_send()

    outs = pl.pallas_call(
        body, name="reduce_half_swap_" + tag, in_specs=[ANY] * n, out_specs=[ANY] * n,
        out_shape=[jax.ShapeDtypeStruct(b.shape, b.dtype) for b in bufs],
        input_output_aliases={w: w for w in range(n)},
        scratch_shapes=[pltpu.SemaphoreType.DMA((n,)), pltpu.SemaphoreType.DMA((n,))],
    )(*bufs)
    return [o.reshape(2 * o.shape[1], o.shape[2]) for o in outs]


def _add_parts(full, axis, rows, sib, name):
    _, r, c = sib.shape
    tr, tc = _tile(r, 256, 16), _tile(c, 2048)
    nb = r // tr
    core = jnp.reshape(lax.axis_index("c"), (1,)).astype(jnp.int32)

    def body(c_ref, a_ref, b_ref, o_ref):
        o_ref[0] = (a_ref[...].astype(F32) + b_ref[0].astype(F32)).astype(BF16)

    blk = pl.BlockSpec((1, tr, tc), lambda j, i, l, cr: (j, i, l))
    return pl.pallas_call(
        body, name=name,
        grid_spec=pltpu.PrefetchScalarGridSpec(
            num_scalar_prefetch=1, grid=(4, nb, c // tc),
            in_specs=[pl.BlockSpec((tr, tc), lambda j, i, l, cr: ((_slot(axis, j) * 2 + cr[0]) * nb + i, l)), blk],
            out_specs=blk),
        out_shape=jax.ShapeDtypeStruct(sib.shape, BF16),
        compiler_params=_cparams(("parallel", "parallel", "parallel")),
    )(core, full, sib)


def _sum_slots(a, name):
    _, r, c = a.shape
    tr, tc = _tile(r, 256, 8), _tile(c, 2048)

    def body(a_ref, o_ref):
        v = a_ref[...].astype(F32)
        o_ref[...] = ((v[0] + v[1]) + v[2]) + v[3]

    return pl.pallas_call(
        body, name=name, grid=(r // tr, c // tc),
        in_specs=[pl.BlockSpec((4, tr, tc), lambda i, l: (0, i, l))],
        out_specs=pl.BlockSpec((tr, tc), lambda i, l: (i, l)),
        out_shape=jax.ShapeDtypeStruct((r, c), F32),
        compiler_params=_cparams(("parallel", "parallel")),
    )(a)


class _Reducer:
    def __init__(self):
        self.pending = []

    def start(self, tag, names, fulls, axes, shapes):
        from_sibling = _pair_exchange(fulls, axes, shapes, tag)
        parts = [_add_parts(f, a, r, s, name=f"reduce_add_{n}")
                 for n, f, a, (r, cl), s in zip(names, fulls, axes, shapes, from_sibling)]
        parts, lands, send, recv, token = _chip_start(parts, tag)
        self.pending.append((tag, names, parts, lands, send, recv))
        return token

    def finish(self, after):
        out = {}
        for tag, names, parts, lands, send, recv in self.pending:
            slots = _chip_wait(parts, lands, send, recv, after, tag)
            halves = [_sum_slots(s, name=f"reduce_sum_{n}") for n, s in zip(names, slots)]
            out.update(zip(names, _half_swap(halves, tag)))
        return out


def _allreduce_small(pack):
    rows = pack.shape[0]

    def body(p_ref, o_ref, slots, send_sems, recv_sems):
        x, y, c, _ = _me()
        me = 4 * x + 2 * y + c
        slots[me] = p_ref[...]
        cps = []
        for r in range(1, 8):
            peer = (x ^ (r >> 2), y ^ ((r >> 1) & 1), c ^ (r & 1))
            cp = _remote(p_ref, slots.at[me], send_sems.at[r - 1], recv_sems.at[r - 1], peer)
            cp.start()
            cps.append(cp)
        for r in range(1, 8):
            frm = me ^ r
            _remote(slots.at[frm], slots.at[frm], send_sems.at[r - 1], recv_sems.at[r - 1], (x, y, c)).wait_recv()
        for cp in cps:
            cp.wait_send()
        acc = slots[0]
        for s in range(1, 8):
            acc = acc + slots[s]
        o_ref[...] = acc

    vm = pl.BlockSpec(memory_space=pltpu.VMEM)
    return pl.pallas_call(
        body, name="allreduce_small", in_specs=[vm], out_specs=vm,
        out_shape=jax.ShapeDtypeStruct(pack.shape, F32),
        scratch_shapes=[pltpu.VMEM((8, rows, HEAD_DIM), F32), pltpu.SemaphoreType.DMA((7,)),
                        pltpu.SemaphoreType.DMA((7,))],
    )(pack)


_ROWS = ["norm_mix", "norm_ffn", "mem_norm", "fox_q_norm", "fox_k_norm", "gdn_out_norm", "mem_q_norm",
         "mem_k_norm", "fox_f_bias", "gdn_a_log", "gdn_dt_bias"]


def _pack_rows(vals):
    out = []
    for name in _ROWS:
        v = vals[name].reshape(-1)
        n = -(-v.shape[0] // HEAD_DIM) * HEAD_DIM
        out.append(jnp.pad(v, (0, n - v.shape[0])).reshape(-1, HEAD_DIM))
    return jnp.concatenate(out, axis=0)


def _unpack_rows(pack, like):
    out, r = {}, 0
    for name in _ROWS:
        n = like[name].shape[-1]
        nr = -(-n // HEAD_DIM)
        out[name] = pack[r:r + nr].reshape(1, -1)[:, :n]
        r += nr
    return out, r


def kernel(x, mem, norm_mix, w_in, fox_f_bias, fox_q_norm, fox_k_norm, gdn_conv, gdn_a_log, gdn_dt_bias, gdn_out_norm, mem_norm, w_mem_kv, mem_q_norm, mem_k_norm, w_out, norm_ffn, w_gate_up, w_down, loss_target, m_norm_mix, m_w_in, m_fox_f_bias, m_fox_q_norm, m_fox_k_norm, m_gdn_conv, m_gdn_a_log, m_gdn_dt_bias, m_gdn_out_norm, m_mem_norm, m_w_mem_kv, m_mem_q_norm, m_mem_k_norm, m_w_out, m_norm_ffn, m_w_gate_up, m_w_down, v_norm_mix, v_w_in, v_fox_f_bias, v_fox_q_norm, v_fox_k_norm, v_gdn_conv, v_gdn_a_log, v_gdn_dt_bias, v_gdn_out_norm, v_mem_norm, v_w_mem_kv, v_mem_q_norm, v_mem_k_norm, v_w_out, v_norm_ffn, v_w_gate_up, v_w_down):
    a = dict(locals())
    d = x.shape[-1]
    lay = _Layout(d)
    chip = 2 * lax.axis_index("x") + lax.axis_index("y")
    small = {n: a[n] for n in _ROWS}
    big = ["w_in", "w_mem_kv", "w_out", "w_gate_up", "w_down"]
    axes = [0, 0, 0, 1, 0]

    conv_cols = gdn_conv.shape[-1]
    conv_n = CONV_WIDTH * conv_cols
    conv_rows = -(-conv_n // HEAD_DIM)
    conv_blk = jnp.pad(gdn_conv.reshape(-1), (0, 32 * HEAD_DIM - conv_n)).reshape(32, HEAD_DIM)
    axis_of = dict(zip(big, axes), conv=0)
    shape_of = {n: a[n].shape[1:] for n in big}
    shape_of["w_in"] = (w_in.shape[1], lay.cols)
    shape_of["conv"] = conv_blk.shape
    placed = {n: _cast_place(a[n][0], axis_of[n], "cast_" + n) for n in big[1:]}
    placed["w_in"] = _cast_place(w_in[0], 0, "cast_w_in", lay.regroup, lay.cols)
    placed["conv"] = lax.dynamic_update_slice(lax.empty((4 * 32, HEAD_DIM), F32), conv_blk, (chip * 32, 0))
    grouped = {"in": ["w_in"], "mixer": ["w_mem_kv", "conv"], "out": ["w_out"], "gate_up": ["w_gate_up"],
               "down": ["w_down"]}
    inflight = {}

    def start(tags, name):
        names = [n for t in tags for n in grouped[t]]
        bufs, sems, _ = _gather_start([placed[n] for n in names], [axis_of[n] for n in names],
                                      [shape_of[n] for n in names],
                                      [[names.index(n) for n in grouped[t]] for t in tags], name)
        for t, pair in zip(tags, sems):
            inflight[t] = ([bufs[names.index(n)] for n in grouped[t]], pair)

    start(["in"], "gather_ici_start_in")
    start(["mixer", "out", "gate_up", "down"], "gather_ici_start_rest")

    def weights(tag, after):
        bufs, sem_pair = inflight[tag]
        ax, shp = [axis_of[n] for n in grouped[tag]], [shape_of[n] for n in grouped[tag]]
        got = _gather_wait(bufs, ax, shp, sem_pair, after, "gather_ici_wait_" + tag)
        got = _gather_forward(got, ax, shp, "gather_forward_" + tag)
        if tag != "mixer":
            return got
        taps = got[1].reshape(4, 32 * HEAD_DIM)[:, :conv_n].reshape(4, CONV_WIDTH, conv_cols)
        return got[0], jnp.transpose(taps, (1, 0, 2)).reshape(CONV_WIDTH, 4 * conv_cols)

    sp = dict(small)
    reducer = _Reducer()
    spec = {n: (axis_of[n], shape_of[n]) for n in big}

    def reduce_start(tag, grads):
        names = list(grads)
        return reducer.start(tag, names, [grads[n] for n in names], [spec[n][0] for n in names],
                             [spec[n][1] for n in names])

    loss_blk, dx, g = _local_step(x[0], mem[0], loss_target[0], weights, reduce_start, sp)

    gsmall = {n: g[n] for n in _ROWS}
    pack = jnp.concatenate([_pack_rows(gsmall), g["gdn_conv"].reshape(-1, HEAD_DIM), loss_blk], axis=0)
    pack = jnp.pad(pack, ((0, -pack.shape[0] % 8), (0, 0)))
    tot = _allreduce_small(pack)
    gs, r0 = _unpack_rows(tot, small)
    conv_g = tot[r0:r0 + CONV_WIDTH * 4 * conv_cols // HEAD_DIM].reshape(CONV_WIDTH, 4 * conv_cols)
    gs_conv = lax.dynamic_slice_in_dim(conv_g, chip * conv_cols, conv_cols, axis=1)
    loss = tot[r0 + CONV_WIDTH * 4 * conv_cols // HEAD_DIM, 0]
    reduced = reducer.finish(tot)

    out = {"loss": loss, "grad_x": dx[None]}
    for n, gsh in reduced.items():
        res = _adamw(a[n][0], gsh, a["m_" + n][0], a["v_" + n][0], g_fn=lay.ungroup if n == "w_in" else None,
                     name="adamw_" + n)
        for pre, r in zip(["grad_", "delta_", "new_m_", "new_v_"], res):
            out[pre + n] = r[None]
    conv_pad = lambda v: jnp.pad(v.reshape(-1), (0, conv_rows * HEAD_DIM - conv_n)).reshape(conv_rows, HEAD_DIM)
    packs = []
    for src, cv in [(small, gdn_conv), (gs, gs_conv), ({n: a["m_" + n] for n in _ROWS}, m_gdn_conv),
                    ({n: a["v_" + n] for n in _ROWS}, v_gdn_conv)]:
        packs.append(jnp.concatenate([_pack_rows(src), conv_pad(cv)], axis=0))
    res = _adamw(*packs, name="adamw_small")
    for pre, r in zip(["grad_", "delta_", "new_m_", "new_v_"], res):
        vals, r1 = _unpack_rows(r, small)
        for n in _ROWS:
            out[pre + n] = vals[n]
        out[pre + "gdn_conv"] = r[r1:r1 + conv_rows].reshape(-1)[:conv_n].reshape(gdn_conv.shape)
    names = ["norm_mix", "w_in", "fox_f_bias", "fox_q_norm", "fox_k_norm", "gdn_conv", "gdn_a_log", "gdn_dt_bias",
             "gdn_out_norm", "mem_norm", "w_mem_kv", "mem_q_norm", "mem_k_norm", "w_out", "norm_ffn", "w_gate_up",
             "w_down"]
    return (out["loss"], out["grad_x"], *[out[p + n] for p in ["grad_", "delta_", "new_m_", "new_v_"] for n in names])
```

```python
import functools
import math

import jax
import jax.numpy as jnp
from jax import lax
from jax.experimental import pallas as pl
from jax.experimental.pallas import tpu as pltpu

F32, BF16 = jnp.float32, jnp.bfloat16
HEAD_DIM = 128
CHUNK = 64
N_MEM_HEADS = 4
CONV_WIDTH = 4
NORM_EPS = 1e-6
ADAM_LR, ADAM_B1, ADAM_B2, ADAM_EPS, ADAM_WD, ADAM_STEP = 0.001, 0.9, 0.999, 1e-08, 0.01, 10
VMEM_LIMIT = 48 * 1024 * 1024
NEG = -1e30
MESH = pl.DeviceIdType.MESH


def _cparams(sem=None, **kw):
    if sem is not None:
        kw["dimension_semantics"] = sem
    return pltpu.CompilerParams(vmem_limit_bytes=VMEM_LIMIT, **kw)


def _tile(n, target, mult=128):
    best = None
    d = mult
    while d <= min(n, target):
        if n % d == 0:
            best = d
        d += mult
    return best if best is not None else n


def _dot(a, b, dims, hi):
    if a.ndim == 3:
        dn = (((dims[0][0] + 1,), (dims[1][0] + 1,)), ((0,), (0,)))
    else:
        dn = (dims, ((), ()))
    if hi is not None:
        return lax.dot_general(a, b, dn, precision=hi, preferred_element_type=F32)
    return lax.dot_general(a.astype(BF16), b.astype(BF16), dn, preferred_element_type=F32)


def _make_dots(hi, cotangent=None):
    @jax.custom_vjp
    def nn(a, b):
        return _dot(a, b, ((1,), (0,)), hi)

    @jax.custom_vjp
    def nt(a, b):
        return _dot(a, b, ((1,), (1,)), hi)

    @jax.custom_vjp
    def tn(a, b):
        return _dot(a, b, ((0,), (0,)), hi)

    bnn, bnt, btn = cotangent or (nn, nt, tn)
    nn.defvjp(lambda a, b: (nn(a, b), (a, b)), lambda r, g: (bnt(g, r[1]), btn(r[0], g)))
    nt.defvjp(lambda a, b: (nt(a, b), (a, b)), lambda r, g: (bnn(g, r[1]), btn(g, r[0])))
    tn.defvjp(lambda a, b: (tn(a, b), (a, b)), lambda r, g: (bnt(r[1], g), bnn(r[0], g)))
    return nn, nt, tn


_nn, _nt, _tn = _make_dots(None)
_nn_hi, _nt_hi, _tn_hi = _make_dots(lax.Precision.HIGHEST)
_nn_x3, _nt_x3, _tn_x3 = _make_dots(lax.Precision.HIGH, (_nn, _nt, _tn))


def _sigmoid(x):
    return 1.0 / (1.0 + jnp.exp(-x))


@jax.custom_vjp
def _softplus(x):
    return jnp.maximum(x, 0.0) + jnp.log(1.0 + jnp.exp(-jnp.abs(x)))


_softplus.defvjp(lambda x: (_softplus(x), x), lambda x, g: (g * _sigmoid(x),))


def _silu(x):
    return x * _sigmoid(x)


def _rms_fn(x, gain, z=None):
    y = x * lax.rsqrt(jnp.mean(x * x, axis=-1, keepdims=True) + NORM_EPS) * gain
    if z is not None:
        y = y * _silu(z)
    return y


def _mm(a, b, *, ta=False, tb=False, out_dtype=F32, res=None, stack=None, after=None, name):
    a2, b2 = a.shape[-2:], b.shape[-2:]
    ns = b.shape[0] if stack else 1
    m = a2[1] if ta else a2[0]
    k = a2[0] if ta else a2[1]
    n = b2[0] if tb else b2[1]
    assert k == (b2[1] if tb else b2[0])
    tm, tn, tk = _mm_tiles(m, n, k, ns if stack == "sum" else 1, a.dtype.itemsize, b.dtype.itemsize,
                           jnp.dtype(out_dtype).itemsize, res is not None)
    nk = k // tk
    single = nk == 1 and stack != "sum"
    dims = ((0 if ta else 1,), (1 if tb else 0,))
    if stack == "sum":
        order = lambda g0, g1, g2, g3: (g2, g0, g1, g3)
        grid = (m // tm, n // tn, ns, nk)
    else:
        order = lambda g0, g1, g2, g3: (g0, g1, g2, g3)
        grid = (ns, m // tm, n // tn, nk)

    def body(*refs):
        if after is not None:
            refs = refs[:2 + (res is not None)] + refs[3 + (res is not None):]
        if single:
            a_ref, b_ref = refs[:2]
            r = lax.dot_general(a_ref[...].astype(BF16), b_ref[...].astype(BF16), (dims, ((), ())),
                                preferred_element_type=F32)
            if res is not None:
                r = r + refs[2][...]
            refs[-1][...] = r.astype(out_dtype)
            return
        if res is None:
            a_ref, b_ref, o_ref, acc = refs
        else:
            a_ref, b_ref, r_ref, o_ref, acc = refs
        s, _, _, kk = order(*[pl.program_id(d) for d in range(4)])
        first = kk == 0
        last = kk == nk - 1
        if stack == "sum":
            first, last = first & (s == 0), last & (s == ns - 1)

        @pl.when(first)
        def _():
            acc[...] = jnp.zeros_like(acc)

        acc[...] += lax.dot_general(a_ref[...].astype(BF16), b_ref[...].astype(BF16), (dims, ((), ())),
                                    preferred_element_type=F32)

        @pl.when(last)
        def _():
            r = acc[...]
            if res is not None:
                r = r + r_ref[...]
            o_ref[...] = r.astype(out_dtype)

    def spec(shape, idx, stacked):
        if stacked:
            return pl.BlockSpec((None,) + shape, lambda *g: (order(*g)[0],) + idx(*order(*g)))
        return pl.BlockSpec(shape, lambda *g: idx(*order(*g)))

    a_spec = (spec((tk, tm), lambda s, i, j, kk: (kk, i), stack == "sum") if ta
              else spec((tm, tk), lambda s, i, j, kk: (i, kk), stack == "sum"))
    b_spec = (spec((tn, tk), lambda s, i, j, kk: (j, kk), bool(stack)) if tb
              else spec((tk, tn), lambda s, i, j, kk: (kk, j), bool(stack)))
    o_spec = spec((tm, tn), lambda s, i, j, kk: (i, j), stack == "out")
    ins, specs = [a, b], [a_spec, b_spec]
    if res is not None:
        ins.append(res)
        specs.append(o_spec)
    if after is not None:
        ins.append(after)
        specs.append(pl.BlockSpec(after.shape, lambda *g: (0,) * after.ndim))
    sem = (("parallel", "parallel", "arbitrary", "arbitrary") if stack == "sum"
           else ("parallel", "parallel", "parallel", "arbitrary"))
    return pl.pallas_call(
        body, name=name, grid=grid, in_specs=specs, out_specs=o_spec,
        out_shape=jax.ShapeDtypeStruct(((ns,) if stack == "out" else ()) + (m, n), out_dtype),
        scratch_shapes=[] if single else [pltpu.VMEM((tm, tn), F32)],
        compiler_params=_cparams(sem),
    )(*ins)


MM_VMEM_BUDGET = 40 * 1024 * 1024


def _mm_tiles(m, n, k, ns, sa, sb, so, has_res):
    def divs(x, mult, cap):
        out = [d for d in range(mult, min(x, cap) + 1, mult) if x % d == 0]
        return out or [x]

    best = None
    for tk in divs(k, 128, 8192):
        nk = (k // tk) * ns
        for tm in divs(m, 8, 2048):
            for tn in divs(n, 128, 2048):
                vmem = 2 * (tm * tk * sa + tk * tn * sb + tm * tn * so) + (2 * tm * tn * 4 if has_res else 0)
                vmem += tm * tn * 4 if nk > 1 else 0
                if vmem > MM_VMEM_BUDGET:
                    continue
                steps = (m // tm) * (n // tn) * nk
                traffic = (m // tm) * k * n * sb * ns + (n // tn if nk > 1 else 1) * m * k * sa * ns
                cost = steps * 0.4e-6 + traffic / 2.5e12 + (nk * m * n * 8 / 6e12 if nk > 1 else 0)
                cost += 2.0 * m * n * k * ns / 7e14
                if best is None or cost < best[0]:
                    best = (cost, tm, tn, tk)
    return best[1:]


def _norm_fwd(x, xoff, gain, ncol, w, out_dtype, *, z=None, zoff=0, into=None, into_off=0, name):
    t = x.shape[0]
    tr = _tile(t, max(256, (1 << 18) // w), 8)

    def body(*refs):
        x_ref, g_ref, o_ref = refs[0], refs[1], refs[-1]
        y = _rms_fn(x_ref[...], g_ref[...]) if z is None else _rms_fn(x_ref[...], g_ref[...], refs[2][...])
        o_ref[...] = y.astype(out_dtype)

    ins = [x, gain]
    specs = [pl.BlockSpec((tr, w), lambda j, r: (r, xoff + j)), pl.BlockSpec((1, w), lambda j, r: (0, 0))]
    if z is not None:
        ins.append(z)
        specs.append(pl.BlockSpec((tr, w), lambda j, r: (r, zoff + j)))
    aliases = {}
    if into is not None:
        aliases = {len(ins): 0}
        ins.append(into)
        specs.append(pl.BlockSpec(memory_space=pl.ANY))
    return pl.pallas_call(
        body, name=name, grid=(ncol, t // tr), in_specs=specs,
        out_specs=pl.BlockSpec((tr, w), lambda j, r: (r, into_off + j)),
        out_shape=jax.ShapeDtypeStruct((t, ncol * w) if into is None else into.shape, out_dtype),
        input_output_aliases=aliases, compiler_params=_cparams(("parallel", "parallel")),
    )(*ins)


def _norm_bwd(x, xoff, gain, dy, dyoff, ncol, w, *, z=None, zoff=0, res=None, name):
    t = x.shape[0]
    tr = _tile(t, max(256, (1 << 18) // w), 8)

    def body(*refs):
        it = iter(refs)
        x_ref, g_ref = next(it), next(it)
        z_ref = next(it) if z is not None else None
        dy_ref = next(it)
        r_ref = next(it) if res is not None else None
        dx_ref = next(it)
        dz_ref = next(it) if z is not None else None
        dg_ref = next(it)

        @pl.when((pl.program_id(0) == 0) & (pl.program_id(1) == 0))
        def _():
            dg_ref[...] = jnp.zeros_like(dg_ref)

        args = (x_ref[...], g_ref[...]) + ((z_ref[...],) if z is not None else ())
        _, vjp = jax.vjp(_rms_fn, *args)
        grads = vjp(dy_ref[...].astype(F32))
        dx = grads[0]
        if res is not None:
            dx = dx + r_ref[...]
        dx_ref[...] = dx
        if z is not None:
            dz_ref[...] = grads[2]
        dg_ref[...] += grads[1]

    ins = [x, gain]
    specs = [pl.BlockSpec((tr, w), lambda j, r: (r, xoff + j)), pl.BlockSpec((1, w), lambda j, r: (0, 0))]
    if z is not None:
        ins.append(z)
        specs.append(pl.BlockSpec((tr, w), lambda j, r: (r, zoff + j)))
    ins.append(dy)
    specs.append(pl.BlockSpec((tr, w), lambda j, r: (r, dyoff + j)))
    blk = pl.BlockSpec((tr, w), lambda j, r: (r, j))
    if res is not None:
        ins.append(res)
        specs.append(blk)
    full = jax.ShapeDtypeStruct((t, ncol * w), F32)
    out_shape, out_specs = [full], [blk]
    if z is not None:
        out_shape.append(full)
        out_specs.append(blk)
    out_shape.append(jax.ShapeDtypeStruct((1, w), F32))
    out_specs.append(pl.BlockSpec((1, w), lambda j, r: (0, 0)))
    return pl.pallas_call(
        body, name=name, grid=(ncol, t // tr), in_specs=specs, out_specs=out_specs, out_shape=out_shape,
        compiler_params=_cparams(("arbitrary", "arbitrary")),
    )(*ins)


def _small_fn(x, pa, pb, nf, ng):
    lane = lax.broadcasted_iota(jnp.int32, x.shape, 1)
    zz = x + pb
    logf = -_softplus(-zz)
    g = -jnp.exp(pa) * _softplus(zz)
    beta = _sigmoid(x)
    return jnp.where(lane < nf, logf, jnp.where(lane < nf + ng, g, beta))


def _tri(n, upper):
    r = lax.broadcasted_iota(jnp.int32, (n, n), 0)
    c = lax.broadcasted_iota(jnp.int32, (n, n), 1)
    return jnp.where((c >= r) if upper else (c <= r), 1.0, 0.0).astype(F32)


def _small_fwd(p, off, pa, pb, nf, ng):
    t = p.shape[0]
    blk = HEAD_DIM
    nb = t // blk

    def body(x_ref, pa_ref, pb_ref, v_ref, c_ref):
        v_ref[...] = _small_fn(x_ref[...], pa_ref[...], pb_ref[...], nf, ng)
        tri = _tri(blk, False)

        carry = jnp.zeros((1, HEAD_DIM), F32)
        for i in range(nb):
            rows = slice(i * blk, (i + 1) * blk)
            c = _nn_hi(tri, v_ref[rows, :]) + carry
            c_ref[rows, :] = c
            carry = c[blk - 1:blk, :]

    row = pl.BlockSpec((1, HEAD_DIM), lambda i: (0, 0))
    out = pl.BlockSpec((t, HEAD_DIM), lambda i: (0, 0))
    return pl.pallas_call(
        body, name="small_fwd", grid=(1,),
        in_specs=[pl.BlockSpec((t, HEAD_DIM), lambda i: (0, off)), row, row], out_specs=[out, out],
        out_shape=[jax.ShapeDtypeStruct((t, HEAD_DIM), F32)] * 2,
        compiler_params=_cparams(("arbitrary",)),
    )(p, pa, pb)


def _small_bwd(p, off, pa, pb, dvals, dcsum, nf, ng):
    t = p.shape[0]
    blk = HEAD_DIM
    nb = t // blk

    def body(x_ref, pa_ref, pb_ref, dv_ref, dc_ref, dx_ref, dpa_ref, dpb_ref, tot_ref):
        tri = _tri(blk, True)

        carry = jnp.zeros((1, HEAD_DIM), F32)
        for i in reversed(range(nb)):
            rows = slice(i * blk, (i + 1) * blk)
            c = _nn_hi(tri, dc_ref[rows, :]) + carry
            tot_ref[rows, :] = c + dv_ref[rows, :]
            carry = c[0:1, :]
        f = functools.partial(_small_fn, nf=nf, ng=ng)
        _, vjp = jax.vjp(f, x_ref[...], pa_ref[...], pb_ref[...])
        dx, dpa, dpb = vjp(tot_ref[...])
        dx_ref[...] = dx
        dpa_ref[...] = dpa
        dpb_ref[...] = dpb

    row = pl.BlockSpec((1, HEAD_DIM), lambda i: (0, 0))
    full = pl.BlockSpec((t, HEAD_DIM), lambda i: (0, 0))
    return pl.pallas_call(
        body, name="small_bwd", grid=(1,),
        in_specs=[pl.BlockSpec((t, HEAD_DIM), lambda i: (0, off)), row, row, full, full],
        out_specs=[full, row, row],
        out_shape=[jax.ShapeDtypeStruct((t, HEAD_DIM), F32), jax.ShapeDtypeStruct((1, HEAD_DIM), F32),
                   jax.ShapeDtypeStruct((1, HEAD_DIM), F32)],
        scratch_shapes=[pltpu.VMEM((t, HEAD_DIM), F32)],
        compiler_params=_cparams(("arbitrary",)),
    )(p, pa, pb, dvals, dcsum)


def _fox_fwd(q, k, v, cc, cr, nf, tq, tk, d_mix):
    t = q.shape[0]
    scale = HEAD_DIM ** -0.5
    ratio = tq // tk

    def body(q_ref, k_ref, v_ref, cc_ref, cr_ref, o_ref, lse_ref, mix_ref):
        i = pl.program_id(1)
        qv = q_ref[...]
        ccol = cc_ref[0]
        rows = i * tq + lax.broadcasted_iota(jnp.int32, (tq, tk), 0)
        cols0 = lax.broadcasted_iota(jnp.int32, (tq, tk), 1)

        def step(j, carry):
            m, l, acc = carry
            ks = pl.ds(pl.multiple_of(j * tk, tk), tk)
            s = lax.dot_general(qv, k_ref[ks, :], (((1,), (1,)), ((), ())), preferred_element_type=F32) * scale
            s = s + ccol - cr_ref[0, j]
            s = jnp.where(cols0 + j * tk <= rows, s, NEG)
            m_new = jnp.maximum(m, jnp.max(s, axis=1, keepdims=True))
            pr = jnp.exp(s - m_new)
            alpha = jnp.exp(m - m_new)
            l = alpha * l + jnp.sum(pr, axis=1, keepdims=True)
            acc = alpha * acc + jnp.dot(pr.astype(BF16), v_ref[ks, :], preferred_element_type=F32)
            return m_new, l, acc

        init = (jnp.full((tq, 1), NEG, F32), jnp.zeros((tq, 1), F32), jnp.zeros((tq, HEAD_DIM), F32))
        m, l, acc = lax.fori_loop(0, (i + 1) * ratio, step, init)
        o_ref[...] = acc / l
        mix_ref[...] = (acc / l).astype(BF16)
        lse_ref[0] = m + jnp.log(l)

    head_all = pl.BlockSpec((t, HEAD_DIM), lambda h, i: (0, h))
    return pl.pallas_call(
        body, name="fox_fwd", grid=(nf, t // tq),
        in_specs=[pl.BlockSpec((tq, HEAD_DIM), lambda h, i: (i, h)), head_all, head_all,
                  pl.BlockSpec((1, tq, 1), lambda h, i: (h, i, 0)),
                  pl.BlockSpec((1, t // tk, 1, tk), lambda h, i: (h, 0, 0, 0))],
        out_specs=[pl.BlockSpec((tq, HEAD_DIM), lambda h, i: (i, h)),
                   pl.BlockSpec((1, tq, 1), lambda h, i: (h, i, 0)),
                   pl.BlockSpec((tq, HEAD_DIM), lambda h, i: (i, h))],
        out_shape=[jax.ShapeDtypeStruct((t, nf * HEAD_DIM), F32), jax.ShapeDtypeStruct((nf, t, 1), F32),
                   jax.ShapeDtypeStruct((t, d_mix), BF16)],
        compiler_params=_cparams(("parallel", "parallel")),
    )(q, k, v, cc, cr)


def _fox_bwd(q, k, v, cc, cr, o, lse, dmix, nf, tq, tk):
    t = q.shape[0]
    scale = HEAD_DIM ** -0.5
    ratio = tq // tk

    def body(q_ref, k_ref, v_ref, cc_ref, cr_ref, o_ref, lse_ref, do_ref,
             dq_ref, dk_ref, dv_ref, dcc_ref, dcr_ref):
        i = pl.program_id(1)

        @pl.when(i == 0)
        def _():
            dk_ref[...] = jnp.zeros_like(dk_ref)
            dv_ref[...] = jnp.zeros_like(dv_ref)
            dcr_ref[...] = jnp.zeros_like(dcr_ref)

        qv = q_ref[...]
        ccol = cc_ref[0]
        lse_v = lse_ref[0]
        do = do_ref[...]
        do_b = do.astype(BF16)
        delta = jnp.sum(do * o_ref[...], axis=1, keepdims=True)
        rows = i * tq + lax.broadcasted_iota(jnp.int32, (tq, tk), 0)
        cols0 = lax.broadcasted_iota(jnp.int32, (tq, tk), 1)

        def step(j, carry):
            dq, dcc = carry
            ks = pl.ds(pl.multiple_of(j * tk, tk), tk)
            kj, vj = k_ref[ks, :], v_ref[ks, :]
            s = lax.dot_general(qv, kj, (((1,), (1,)), ((), ())), preferred_element_type=F32) * scale
            s = s + ccol - cr_ref[0, j]
            pr = jnp.where(cols0 + j * tk <= rows, jnp.exp(s - lse_v), 0.0)
            dp = lax.dot_general(do_b, vj, (((1,), (1,)), ((), ())), preferred_element_type=F32)
            ds = pr * (dp - delta)
            ds_b = ds.astype(BF16)
            dq = dq + jnp.dot(ds_b, kj, preferred_element_type=F32) * scale
            dk_ref[ks, :] += lax.dot_general(ds_b, qv, (((0,), (0,)), ((), ())),
                                             preferred_element_type=F32) * scale
            dv_ref[ks, :] += lax.dot_general(pr.astype(BF16), do_b, (((0,), (0,)), ((), ())),
                                             preferred_element_type=F32)
            dcr_ref[0, j] -= jnp.sum(ds, axis=0, keepdims=True)
            return dq, dcc + jnp.sum(ds, axis=1, keepdims=True)

        init = (jnp.zeros((tq, HEAD_DIM), F32), jnp.zeros((tq, 1), F32))
        dq, dcc = lax.fori_loop(0, (i + 1) * ratio, step, init)
        dq_ref[...] = dq
        dcc_ref[0] = dcc

    head_all = pl.BlockSpec((t, HEAD_DIM), lambda h, i: (0, h))
    qblk = pl.BlockSpec((tq, HEAD_DIM), lambda h, i: (i, h))
    colv = pl.BlockSpec((1, tq, 1), lambda h, i: (h, i, 0))
    rowv = pl.BlockSpec((1, t // tk, 1, tk), lambda h, i: (h, 0, 0, 0))
    wide = jax.ShapeDtypeStruct((t, nf * HEAD_DIM), F32)
    return pl.pallas_call(
        body, name="fox_bwd", grid=(nf, t // tq),
        in_specs=[qblk, head_all, head_all, colv, rowv, qblk, colv, qblk],
        out_specs=[qblk, head_all, head_all, colv, rowv],
        out_shape=[wide, wide, wide, jax.ShapeDtypeStruct((nf, t, 1), F32),
                   jax.ShapeDtypeStruct((nf, t // tk, 1, tk), F32)],
        compiler_params=_cparams(("parallel", "arbitrary")),
    )(q, k, v, cc, cr, o, lse, dmix)


def _mem_fn(mq, mk, mv, gq, gk):
    qn = _rms_fn(mq, gq)
    kn = _rms_fn(mk, gk)
    s = _nt(qn, kn) * (HEAD_DIM ** -0.5)
    e = jnp.exp(s - lax.stop_gradient(jnp.max(s, axis=1, keepdims=True)))
    pr = e / jnp.sum(e, axis=1, keepdims=True)
    return _nn(pr, mv)


def _mem_specs(t, m, tq, qoff):
    qblk = pl.BlockSpec((tq, HEAD_DIM), lambda h, i: (i, qoff + h))
    kblk = pl.BlockSpec((m, HEAD_DIM), lambda h, i: (0, h))
    vblk = pl.BlockSpec((m, HEAD_DIM), lambda h, i: (0, N_MEM_HEADS + h))
    row = pl.BlockSpec((1, HEAD_DIM), lambda h, i: (0, 0))
    return qblk, kblk, vblk, row


def _mem_fwd(p, qoff, mkv, gq, gk, tq, into, into_off):
    t, m = p.shape[0], mkv.shape[0]
    qblk, kblk, vblk, row = _mem_specs(t, m, tq, qoff)

    def body(q_ref, k_ref, v_ref, gq_ref, gk_ref, _, o_ref):
        o_ref[...] = _mem_fn(q_ref[...], k_ref[...], v_ref[...], gq_ref[...], gk_ref[...]).astype(BF16)

    return pl.pallas_call(
        body, name="mem_fwd", grid=(N_MEM_HEADS, t // tq),
        in_specs=[qblk, kblk, vblk, row, row, pl.BlockSpec(memory_space=pl.ANY)],
        out_specs=pl.BlockSpec((tq, HEAD_DIM), lambda h, i: (i, into_off + h)),
        out_shape=jax.ShapeDtypeStruct(into.shape, BF16), input_output_aliases={5: 0},
        compiler_params=_cparams(("parallel", "parallel")),
    )(p, mkv, mkv, gq, gk, into)


def _mem_bwd(p, qoff, mkv, gq, gk, dmix, dooff, tq):
    t, m = p.shape[0], mkv.shape[0]
    qblk, kblk, vblk, row = _mem_specs(t, m, tq, qoff)

    def body(q_ref, k_ref, v_ref, gq_ref, gk_ref, do_ref, dq_ref, dkv_k_ref, dkv_v_ref, dgq_ref, dgk_ref):
        h, i = pl.program_id(0), pl.program_id(1)

        @pl.when((h == 0) & (i == 0))
        def _():
            dgq_ref[...] = jnp.zeros_like(dgq_ref)
            dgk_ref[...] = jnp.zeros_like(dgk_ref)

        @pl.when(i == 0)
        def _():
            dkv_k_ref[...] = jnp.zeros_like(dkv_k_ref)
            dkv_v_ref[...] = jnp.zeros_like(dkv_v_ref)

        _, vjp = jax.vjp(_mem_fn, q_ref[...], k_ref[...], v_ref[...], gq_ref[...], gk_ref[...])
        dq, dk, dv, dgq, dgk = vjp(do_ref[...])
        dq_ref[...] = dq
        dkv_k_ref[...] += dk
        dkv_v_ref[...] += dv
        dgq_ref[...] += dgq
        dgk_ref[...] += dgk

    oblk = pl.BlockSpec((tq, HEAD_DIM), lambda h, i: (i, h))
    kout = pl.BlockSpec((m, HEAD_DIM), lambda h, i: (0, h))
    half = jax.ShapeDtypeStruct((m, N_MEM_HEADS * HEAD_DIM), F32)
    rshape = jax.ShapeDtypeStruct((1, HEAD_DIM), F32)
    return pl.pallas_call(
        body, name="mem_bwd", grid=(N_MEM_HEADS, t // tq),
        in_specs=[qblk, kblk, vblk, row, row, pl.BlockSpec((tq, HEAD_DIM), lambda h, i: (i, dooff + h))],
        out_specs=[oblk, kout, kout, row, row],
        out_shape=[jax.ShapeDtypeStruct((t, N_MEM_HEADS * HEAD_DIM), F32), half, half, rshape, rshape],
        compiler_params=_cparams(("arbitrary", "arbitrary")),
    )(p, mkv, mkv, gq, gk, dmix)


def _shift_down(x, s):
    if s == 0:
        return x
    r = lax.broadcasted_iota(jnp.int32, x.shape, 0)
    return jnp.where(r >= s, pltpu.roll(x, s, 0), 0.0)


def _shift_up(x, s):
    if s == 0:
        return x
    n = x.shape[0]
    r = lax.broadcasted_iota(jnp.int32, x.shape, 0)
    return jnp.where(r < n - s, pltpu.roll(x, n - s, 0), 0.0)


def _conv_fn(x0, x1, x2, x3, w0, w1, w2, w3, kind):
    y = _silu(x0 * w0 + x1 * w1 + x2 * w2 + x3 * w3)
    if kind == 2:
        return y
    y = y * lax.rsqrt(jnp.sum(y * y, axis=-1, keepdims=True) + NORM_EPS)
    return y * (HEAD_DIM ** -0.5) if kind == 0 else y


def _conv_fwd(p, off, conv_w, ng):
    t = p.shape[0]

    def body(x_ref, w_ref, o_ref):
        kind = pl.program_id(0) // ng
        x = x_ref[...]
        xs = [_shift_down(x, CONV_WIDTH - 1 - j) for j in range(CONV_WIDTH)]
        ws = [w_ref[j:j + 1, :] for j in range(CONV_WIDTH)]
        for kd in range(3):
            @pl.when(kind == kd)
            def _(kd=kd):
                o_ref[...] = _conv_fn(*xs, *ws, kd)

    return pl.pallas_call(
        body, name="gdn_conv_fwd", grid=(3 * ng,),
        in_specs=[pl.BlockSpec((t, HEAD_DIM), lambda c: (0, off + c)),
                  pl.BlockSpec((CONV_WIDTH, HEAD_DIM), lambda c: (0, c))],
        out_specs=pl.BlockSpec((t, HEAD_DIM), lambda c: (0, c)),
        out_shape=jax.ShapeDtypeStruct((t, 3 * ng * HEAD_DIM), F32),
        compiler_params=_cparams(("parallel",)),
    )(p, conv_w)


def _conv_bwd(p, off, conv_w, dys, ng):
    t = p.shape[0]

    def body(x_ref, w_ref, dq_ref, dk_ref, dv_ref, dx_ref, dw_ref):
        kind = pl.program_id(0) // ng
        dy_refs = (dq_ref, dk_ref, dv_ref)
        x = x_ref[...]
        xs = [_shift_down(x, CONV_WIDTH - 1 - j) for j in range(CONV_WIDTH)]
        ws = [w_ref[j:j + 1, :] for j in range(CONV_WIDTH)]
        for kd in range(3):
            @pl.when(kind == kd)
            def _(kd=kd):
                _, vjp = jax.vjp(functools.partial(_conv_fn, kind=kd), *xs, *ws)
                g = vjp(dy_refs[kd][...])
                dx = _shift_up(g[0], CONV_WIDTH - 1)
                for j in range(1, CONV_WIDTH):
                    dx = dx + _shift_up(g[j], CONV_WIDTH - 1 - j)
                dx_ref[...] = dx
                for j in range(CONV_WIDTH):
                    dw_ref[j:j + 1, :] = g[CONV_WIDTH + j]

    blk = pl.BlockSpec((t, HEAD_DIM), lambda c: (0, c))
    head = pl.BlockSpec((t, HEAD_DIM), lambda c: (0, c % ng))
    wblk = pl.BlockSpec((CONV_WIDTH, HEAD_DIM), lambda c: (0, c))
    return pl.pallas_call(
        body, name="gdn_conv_bwd", grid=(3 * ng,),
        in_specs=[pl.BlockSpec((t, HEAD_DIM), lambda c: (0, off + c)), wblk] + [head] * 3,
        out_specs=[blk, wblk],
        out_shape=[jax.ShapeDtypeStruct((t, 3 * ng * HEAD_DIM), F32),
                   jax.ShapeDtypeStruct((CONV_WIDTH, 3 * ng * HEAD_DIM), F32)],
        compiler_params=_cparams(("parallel",)),
    )(p, conv_w, *dys)


def _wy_fn(q, k, v, gcol, grow, bcol):
    b, c, dk = q.shape
    r = lax.broadcasted_iota(jnp.int32, (1, c, c), 1)
    e = lax.broadcasted_iota(jnp.int32, (1, c, c), 2)
    tril, strict = e <= r, e < r
    gc_col = jnp.sum(jnp.where(tril, grow, 0.0), axis=2, keepdims=True)
    gc_row = jnp.sum(jnp.where(r <= e, gcol, 0.0), axis=1, keepdims=True)
    g_last = jnp.sum(gcol, axis=1, keepdims=True)
    decay = jnp.exp(jnp.where(tril, gc_col - gc_row, NEG))
    kb, vb = k * bcol, v * bcol
    lower = jnp.where(strict, _nt(kb, k) * decay, 0.0)
    inv = jnp.where(r == e, 1.0, 0.0) - lower
    pw = lower
    for _ in range(int(math.log2(c)) - 1):
        pw = _nn_x3(pw, pw)
        inv = inv + _nn_x3(inv, pw)
    u = _nn_x3(inv, vb)
    w = _nn_x3(inv, kb * jnp.exp(gc_col))
    attn = jnp.where(tril, _nt(q, k) * decay, 0.0)
    qg = q * jnp.exp(gc_col)
    kdec = k * jnp.exp(g_last - gc_col)
    egl = jnp.broadcast_to(jnp.exp(g_last), (b, 1, dk))
    return u, w, qg, kdec, attn, egl


def _scan_fn(u, w, qg, kdec, attn, egl, state):
    v_new = u - _nn(w, state)
    o = _nn(qg, state) + _nn(attn, v_new)
    return o, state * egl + _tn(kdec, v_new)


GDN_CHUNKS_PER_STEP = 4


def _gdn_fwd(qkv, gcol, grow, bcol, ng):
    t = qkv.shape[0]
    nch = t // CHUNK

    cb = GDN_CHUNKS_PER_STEP
    wy = _gdn_wy(qkv, gcol, grow, bcol, ng, cb)

    def body(u_ref, w_ref, qg_ref, kd_ref, at_ref, eg_ref, o_ref, st_ref, state):
        @pl.when(pl.program_id(0) == 0)
        def _():
            state[...] = jnp.zeros_like(state)

        st_ref[:, 0] = state[...]
        heads = lambda ref: jnp.stack([ref[:, h * HEAD_DIM:(h + 1) * HEAD_DIM] for h in range(ng)])
        o, new = _scan_fn(heads(u_ref), heads(w_ref), heads(qg_ref), heads(kd_ref), at_ref[:, 0], eg_ref[:, 0],
                          state[...])
        for h in range(ng):
            o_ref[:, h * HEAD_DIM:(h + 1) * HEAD_DIM] = o[h]
        state[...] = new

    w = ng * HEAD_DIM
    blk = pl.BlockSpec((CHUNK, w), lambda i: (i, 0))
    o, states = pl.pallas_call(
        body, name="gdn_scan_fwd", grid=(nch,),
        in_specs=[blk, blk, blk, blk, pl.BlockSpec((ng, 1, CHUNK, CHUNK), lambda i: (0, i, 0, 0)),
                  pl.BlockSpec((ng, 1, 1, HEAD_DIM), lambda i: (0, i, 0, 0))],
        out_specs=[blk, pl.BlockSpec((ng, 1, HEAD_DIM, HEAD_DIM), lambda i: (0, i, 0, 0))],
        out_shape=[jax.ShapeDtypeStruct((t, w), F32),
                   jax.ShapeDtypeStruct((ng, nch, HEAD_DIM, HEAD_DIM), F32)],
        scratch_shapes=[pltpu.VMEM((ng, HEAD_DIM, HEAD_DIM), F32)],
        compiler_params=_cparams(("arbitrary",)),
    )(*wy)
    return o, (wy, states)


def _wy_batch(q_ref, k_ref, v_ref, gc_ref, gr_ref, bc_ref, ng, cb):
    idx = [(c, h) for c in range(cb) for h in range(ng)]
    rows = lambda c: slice(c * CHUNK, (c + 1) * CHUNK)
    lanes = lambda h: slice(h * HEAD_DIM, (h + 1) * HEAD_DIM)
    wide = lambda ref: jnp.stack([ref[rows(c), lanes(h)] for c, h in idx])
    col = lambda ref: jnp.stack([ref[h, rows(c), :] for c, h in idx])
    return idx, (wide(q_ref), wide(k_ref), wide(v_ref), col(gc_ref), jnp.stack([gr_ref[h, c] for c, h in idx]),
                 col(bc_ref))


def _gdn_wy(qkv, gcol, grow, bcol, ng, cb):
    t = qkv.shape[0]
    nch = t // CHUNK

    def body(q_ref, k_ref, v_ref, gc_ref, gr_ref, bc_ref, u_ref, w_ref, qg_ref, kd_ref, at_ref, eg_ref):
        idx, args = _wy_batch(q_ref, k_ref, v_ref, gc_ref, gr_ref, bc_ref, ng, cb)
        u, w, qg, kd, at, eg = _wy_fn(*args)
        for b, (c, h) in enumerate(idx):
            rows, lanes = slice(c * CHUNK, (c + 1) * CHUNK), slice(h * HEAD_DIM, (h + 1) * HEAD_DIM)
            u_ref[rows, lanes] = u[b]
            w_ref[rows, lanes] = w[b]
            qg_ref[rows, lanes] = qg[b]
            kd_ref[rows, lanes] = kd[b]
            at_ref[h, c] = at[b]
            eg_ref[h, c] = eg[b]

    wd = ng * HEAD_DIM
    blk = lambda o: pl.BlockSpec((cb * CHUNK, wd), lambda i: (i, o))
    col = pl.BlockSpec((ng, cb * CHUNK, 1), lambda i: (0, i, 0))
    wide = jax.ShapeDtypeStruct((t, wd), F32)
    return pl.pallas_call(
        body, name="gdn_wy_fwd", grid=(nch // cb,),
        in_specs=[blk(0), blk(1), blk(2), col, pl.BlockSpec((ng, cb, 1, CHUNK), lambda i: (0, i, 0, 0)), col],
        out_specs=[blk(0), blk(0), blk(0), blk(0), pl.BlockSpec((ng, cb, CHUNK, CHUNK), lambda i: (0, i, 0, 0)),
                   pl.BlockSpec((ng, cb, 1, HEAD_DIM), lambda i: (0, i, 0, 0))],
        out_shape=[wide, wide, wide, wide, jax.ShapeDtypeStruct((ng, nch, CHUNK, CHUNK), F32),
                   jax.ShapeDtypeStruct((ng, nch, 1, HEAD_DIM), F32)],
        compiler_params=_cparams(("parallel",)),
    )(qkv, qkv, qkv, gcol, grow, bcol)


def _gdn_bwd(qkv, gcol, grow, bcol, saved, do, ng):
    t = qkv.shape[0]
    nch = t // CHUNK
    cb = GDN_CHUNKS_PER_STEP // 2
    wy, states = saved
    wd = ng * HEAD_DIM

    def scan_body(u_ref, w_ref, qg_ref, kd_ref, at_ref, eg_ref, st_ref, do_ref,
                  du_ref, dw_ref, dqg_ref, dkd_ref, dat_ref, deg_ref, dstate):
        @pl.when(pl.program_id(0) == 0)
        def _():
            dstate[...] = jnp.zeros_like(dstate)

        heads = lambda ref: jnp.stack([ref[:, h * HEAD_DIM:(h + 1) * HEAD_DIM] for h in range(ng)])
        _, vjp = jax.vjp(_scan_fn, heads(u_ref), heads(w_ref), heads(qg_ref), heads(kd_ref), at_ref[:, 0],
                         eg_ref[:, 0], st_ref[:, 0])
        du, dw, dqg, dkd, dat, deg, dst = vjp((heads(do_ref), dstate[...]))
        for h in range(ng):
            lanes = slice(h * HEAD_DIM, (h + 1) * HEAD_DIM)
            du_ref[:, lanes] = du[h]
            dw_ref[:, lanes] = dw[h]
            dqg_ref[:, lanes] = dqg[h]
            dkd_ref[:, lanes] = dkd[h]
        dat_ref[:, 0] = dat
        deg_ref[:, 0] = deg
        dstate[...] = dst

    rev = lambda i: nch - 1 - i
    blk = pl.BlockSpec((CHUNK, wd), lambda i: (rev(i), 0))
    atb = pl.BlockSpec((ng, 1, CHUNK, CHUNK), lambda i: (0, rev(i), 0, 0))
    egb = pl.BlockSpec((ng, 1, 1, HEAD_DIM), lambda i: (0, rev(i), 0, 0))
    wide = jax.ShapeDtypeStruct((t, wd), F32)
    at_shape = jax.ShapeDtypeStruct((ng, nch, CHUNK, CHUNK), F32)
    eg_shape = jax.ShapeDtypeStruct((ng, nch, 1, HEAD_DIM), F32)
    dwy = pl.pallas_call(
        scan_body, name="gdn_scan_bwd", grid=(nch,),
        in_specs=[blk, blk, blk, blk, atb, egb,
                  pl.BlockSpec((ng, 1, HEAD_DIM, HEAD_DIM), lambda i: (0, rev(i), 0, 0)), blk],
        out_specs=[blk, blk, blk, blk, atb, egb],
        out_shape=[wide, wide, wide, wide, at_shape, eg_shape],
        scratch_shapes=[pltpu.VMEM((ng, HEAD_DIM, HEAD_DIM), F32)],
        compiler_params=_cparams(("arbitrary",)),
    )(*wy, states, do)

    def wy_body(q_ref, k_ref, v_ref, gc_ref, gr_ref, bc_ref, du_ref, dw_ref, dqg_ref, dkd_ref, dat_ref, deg_ref,
                dq_ref, dk_ref, dv_ref, dgc_ref, dgr_ref, dbc_ref):
        idx, args = _wy_batch(q_ref, k_ref, v_ref, gc_ref, gr_ref, bc_ref, ng, cb)
        rows = lambda c: slice(c * CHUNK, (c + 1) * CHUNK)
        lanes = lambda h: slice(h * HEAD_DIM, (h + 1) * HEAD_DIM)
        wide_ct = lambda ref: jnp.stack([ref[rows(c), lanes(h)] for c, h in idx])
        cts = (wide_ct(du_ref), wide_ct(dw_ref), wide_ct(dqg_ref), wide_ct(dkd_ref),
               jnp.stack([dat_ref[h, c] for c, h in idx]), jnp.stack([deg_ref[h, c] for c, h in idx]))
        _, vjp = jax.vjp(_wy_fn, *args)
        dq, dk, dv, dgc, dgr, dbc = vjp(cts)
        for b, (c, h) in enumerate(idx):
            dq_ref[rows(c), lanes(h)] = dq[b]
            dk_ref[rows(c), lanes(h)] = dk[b]
            dv_ref[rows(c), lanes(h)] = dv[b]
            dgc_ref[h, rows(c), :] = dgc[b]
            dgr_ref[h, c] = dgr[b]
            dbc_ref[h, rows(c), :] = dbc[b]

    cblk = lambda o: pl.BlockSpec((cb * CHUNK, wd), lambda i: (i, o))
    col = pl.BlockSpec((ng, cb * CHUNK, 1), lambda i: (0, i, 0))
    rowv = pl.BlockSpec((ng, cb, 1, CHUNK), lambda i: (0, i, 0, 0))
    cshape = jax.ShapeDtypeStruct((ng, t, 1), F32)
    return pl.pallas_call(
        wy_body, name="gdn_wy_bwd", grid=(nch // cb,),
        in_specs=[cblk(0), cblk(1), cblk(2), col, rowv, col, cblk(0), cblk(0), cblk(0), cblk(0),
                  pl.BlockSpec((ng, cb, CHUNK, CHUNK), lambda i: (0, i, 0, 0)),
                  pl.BlockSpec((ng, cb, 1, HEAD_DIM), lambda i: (0, i, 0, 0))],
        out_specs=[cblk(0), cblk(0), cblk(0), col, rowv, col],
        out_shape=[wide, wide, wide, cshape, jax.ShapeDtypeStruct((ng, nch, 1, CHUNK), F32), cshape],
        compiler_params=_cparams(("parallel",)),
    )(qkv, qkv, qkv, gcol, grow, bcol, *dwy)


def _swiglu_fn(gate, up):
    return _silu(gate) * up


FFN_TN = 256


def _ffn_up(n2, wgu4):
    _, d, w = wgu4.shape
    t = n2.shape[0]
    tn = _tile(w, FFN_TN)
    nb = w // tn

    def body(a_ref, b_ref, gu_ref, act_ref):
        av = a_ref[...]
        gate = jnp.dot(av, b_ref[0], preferred_element_type=F32)
        up = jnp.dot(av, b_ref[1], preferred_element_type=F32)
        gu_ref[0] = gate.astype(BF16)
        gu_ref[1] = up.astype(BF16)
        act_ref[...] = _swiglu_fn(gate, up).astype(BF16)

    return pl.pallas_call(
        body, name="ffn_up", grid=(2, nb),
        in_specs=[pl.BlockSpec((t, d), lambda j, l: (0, 0)), pl.BlockSpec((2, d, tn), lambda j, l: (j, 0, l))],
        out_specs=[pl.BlockSpec((2, t, tn), lambda j, l: (j, 0, l)),
                   pl.BlockSpec((t, tn), lambda j, l: (0, j * nb + l))],
        out_shape=[jax.ShapeDtypeStruct((4, t, w), BF16), jax.ShapeDtypeStruct((t, 2 * w), BF16)],
        compiler_params=_cparams(("parallel", "parallel")),
    )(n2, wgu4)


def _ffn_dact(dh2, wd, gu, after):
    _, t, w = gu.shape
    d = dh2.shape[1]
    tn = _tile(w, FFN_TN)
    nb = w // tn

    def body(a_ref, b_ref, gu_ref, _, o_ref):
        dact = lax.dot_general(a_ref[...], b_ref[...], (((1,), (1,)), ((), ())), preferred_element_type=F32)
        _, vjp = jax.vjp(_swiglu_fn, gu_ref[0].astype(F32), gu_ref[1].astype(F32))
        dg, du = vjp(dact)
        o_ref[0] = dg.astype(BF16)
        o_ref[1] = du.astype(BF16)

    pair = pl.BlockSpec((2, t, tn), lambda j, l: (j, 0, l))
    return pl.pallas_call(
        body, name="ffn_dact", grid=(2, nb),
        in_specs=[pl.BlockSpec((t, d), lambda j, l: (0, 0)), pl.BlockSpec((tn, d), lambda j, l: (j * nb + l, 0)),
                  pair, pl.BlockSpec(after.shape, lambda j, l: (0, 0))],
        out_specs=pair, out_shape=jax.ShapeDtypeStruct(gu.shape, BF16),
        compiler_params=_cparams(("parallel", "parallel")),
    )(dh2, wd, gu, after)


def _loss_head(h2, target):
    t, d = h2.shape
    tr = _tile(t, 256, 8)

    def body(h_ref, t_ref, l_ref, d_ref, db_ref):
        @pl.when(pl.program_id(0) == 0)
        def _():
            l_ref[...] = jnp.zeros_like(l_ref)

        err = h_ref[...] - t_ref[...]
        d_ref[...] = err * (1.0 / d)
        db_ref[...] = (err * (1.0 / d)).astype(BF16)
        part = 0.5 * jnp.sum(jnp.mean(err * err, axis=-1, keepdims=True), axis=0, keepdims=True)
        lane = lax.broadcasted_iota(jnp.int32, (8, HEAD_DIM), 1)
        row = lax.broadcasted_iota(jnp.int32, (8, HEAD_DIM), 0)
        l_ref[...] += jnp.where((lane == 0) & (row == 0), part, 0.0)

    blk = pl.BlockSpec((tr, d), lambda r: (r, 0))
    return pl.pallas_call(
        body, name="loss_head", grid=(t // tr,), in_specs=[blk, blk],
        out_specs=[pl.BlockSpec((8, HEAD_DIM), lambda r: (0, 0)), blk, blk],
        out_shape=[jax.ShapeDtypeStruct((8, HEAD_DIM), F32), jax.ShapeDtypeStruct((t, d), F32),
                   jax.ShapeDtypeStruct((t, d), BF16)],
        compiler_params=_cparams(("arbitrary",)),
    )(h2, target)


def _adamw(w, g, m, v, *, g_fn=None, name):
    r, c = w.shape
    tr = _tile(r, max(8, (1 << 19) // c // 8 * 8), 8)

    def body(w_ref, g_ref, m_ref, v_ref, go_ref, d_ref, mo_ref, vo_ref):
        gr = g_ref[...] if g_fn is None else g_fn(g_ref[...])
        mn = ADAM_B1 * m_ref[...] + (1.0 - ADAM_B1) * gr
        vn = ADAM_B2 * v_ref[...] + (1.0 - ADAM_B2) * (gr * gr)
        m_hat = mn / (1.0 - ADAM_B1 ** ADAM_STEP)
        v_hat = vn / (1.0 - ADAM_B2 ** ADAM_STEP)
        go_ref[...] = gr
        d_ref[...] = -ADAM_LR * (m_hat / (jnp.sqrt(v_hat) + ADAM_EPS) + ADAM_WD * w_ref[...])
        mo_ref[...] = mn
        vo_ref[...] = vn

    blk = pl.BlockSpec((tr, c), lambda i: (i, 0))
    gblk = pl.BlockSpec((tr, g.shape[1]), lambda i: (i, 0))
    return pl.pallas_call(
        body, name=name, grid=(r // tr,), in_specs=[blk, gblk, blk, blk], out_specs=[blk] * 4,
        out_shape=[jax.ShapeDtypeStruct((r, c), F32)] * 4,
        compiler_params=_cparams(("parallel",)),
    )(w, g, m, v)


class _Layout:
    def __init__(self, d):
        nh = d // HEAD_DIM
        self.nm = N_MEM_HEADS
        self.nf = (nh - self.nm) // 2
        self.ng = nh - self.nm - self.nf
        nf, ng, nm = self.nf, self.ng, self.nm
        self.o_fq, self.o_fk, self.o_fv = 0, nf, 2 * nf
        self.o_gq = 3 * nf
        self.o_gz = 3 * nf + 3 * ng
        self.o_mq = 3 * nf + 4 * ng
        self.o_sm = self.o_mq + nm
        self.blocks = -(-(self.o_sm + 1) // 8) * 8
        self.cols = self.blocks * HEAD_DIM
        hd = HEAD_DIM
        sizes = [nf * hd, nf * hd, nf * hd, nf, 3 * ng * hd, ng * hd, ng, ng, nm * hd]
        starts = [sum(sizes[:i]) for i in range(len(sizes))]
        self.ref = list(zip(starts, sizes))
        self.in_cols = sum(sizes)

    def regroup(self, w):
        part = lambda i: w[:, self.ref[i][0]:self.ref[i][0] + self.ref[i][1]]
        pieces = [part(0), part(1), part(2), part(4), part(5), part(8), part(3), part(6), part(7)]
        pad = self.cols - self.in_cols
        return jnp.concatenate(pieces + [jnp.zeros((w.shape[0], pad), w.dtype)], axis=1)

    def ungroup(self, g):
        hd, nf, ng, nm = HEAD_DIM, self.nf, self.ng, self.nm
        sm = self.o_sm * hd
        return jnp.concatenate([
            g[:, :3 * nf * hd], g[:, sm:sm + nf], g[:, self.o_gq * hd:self.o_gz * hd],
            g[:, self.o_gz * hd:self.o_mq * hd], g[:, sm + nf:sm + nf + ng], g[:, sm + nf + ng:sm + nf + 2 * ng],
            g[:, self.o_mq * hd:self.o_sm * hd]], axis=1)


def _lane_row(pieces):
    row = jnp.zeros((1, HEAD_DIM), F32)
    for off, a in pieces:
        row = lax.dynamic_update_slice(row, a.astype(F32), (0, off))
    return row


def _local_step(x, mem, target, prefetch, weights, reducer, sp):
    t, d = x.shape
    lay = _Layout(d)
    nf, ng, nm, hd = lay.nf, lay.ng, lay.nm, HEAD_DIM
    nch = t // CHUNK
    tq = _tile(t, 256)
    tk = tq

    u = _norm_fwd(x, 0, sp["norm_mix"], 1, d, BF16, name="norm_mix_fwd")
    prefetch("in", u)
    (win,) = weights("in", u)
    prefetch("mixer", win)
    p = _mm(u, win, name="mm_in")
    wmkv, conv_taps = weights("mixer", p)
    sp = dict(sp, gdn_conv=conv_taps)
    pa = _lane_row([(nf, sp["gdn_a_log"])])
    pb = _lane_row([(0, sp["fox_f_bias"]), (nf, sp["gdn_dt_bias"])])
    vals, csum = _small_fwd(p, lay.o_sm, pa, pb, nf, ng)

    c_t = csum[:, :nf].T
    cc, cr = c_t.reshape(nf, t, 1), c_t.reshape(nf, t // tk, 1, tk)
    fq = _norm_fwd(p, lay.o_fq, sp["fox_q_norm"], nf, hd, BF16, name="fox_qnorm_fwd")
    fk = _norm_fwd(p, lay.o_fk, sp["fox_k_norm"], nf, hd, BF16, name="fox_knorm_fwd")
    fv = p[:, lay.o_fv * hd:(lay.o_fv + nf) * hd].astype(BF16)
    o_fox, lse, mix = _fox_fwd(fq, fk, fv, cc, cr, nf, tq, tk, d)
    prefetch("out", lse)

    qkv = _conv_fwd(p, lay.o_gq, sp["gdn_conv"], ng)
    g_t, b_t = vals[:, nf:nf + ng].T, vals[:, nf + ng:nf + 2 * ng].T
    gcol, grow, bcol = g_t.reshape(ng, t, 1), g_t.reshape(ng, nch, 1, CHUNK), b_t.reshape(ng, t, 1)
    o_g, states = _gdn_fwd(qkv, gcol, grow, bcol, ng)
    mix = _norm_fwd(o_g, 0, sp["gdn_out_norm"], ng, hd, BF16, z=p, zoff=lay.o_gz, into=mix, into_off=nf,
                    name="gdn_out_fwd")

    mem_n = _norm_fwd(mem, 0, sp["mem_norm"], 1, d, BF16, name="mem_norm_fwd")
    mkv = _mm(mem_n, wmkv, name="mm_memkv")
    mix = _mem_fwd(p, lay.o_mq, mkv, sp["mem_q_norm"], sp["mem_k_norm"], tq, mix, nf + ng)
    prefetch("gate_up", mix)
    (wout,) = weights("out", mix)
    h1 = _mm(mix, wout, res=x, name="mm_out")
    n2 = _norm_fwd(h1, 0, sp["norm_ffn"], 1, d, BF16, name="norm_ffn_fwd")
    prefetch("down", n2)
    (wgu,) = weights("gate_up", n2)
    wgu4 = wgu.reshape(4, d, -1)
    gu, act = _ffn_up(n2, wgu4)
    (wd,) = weights("down", act)
    h2 = _mm(act, wd, res=h1, name="mm_down")
    loss_blk, dh2, dh2_b = _loss_head(h2, target)

    g = {}
    token = reducer.pair("w_down", _mm(act, dh2_b, ta=True, out_dtype=BF16, name="mm_dw_down"))
    dgu = _ffn_dact(dh2_b, wd, gu, token)
    dw_gate_up = _mm(n2, dgu, ta=True, stack="out", out_dtype=BF16, name="mm_dw_gate_up").reshape(wgu.shape)
    token = reducer.pair("w_gate_up", dw_gate_up)
    dn2 = _mm(dgu, wgu4, tb=True, stack="sum", after=token, name="mm_dn2")
    token = reducer.ship("ffn", ["w_down", "w_gate_up"], dn2)
    dh1, g["norm_ffn"] = _norm_bwd(h1, 0, sp["norm_ffn"] + token[0, 0], dn2, 0, 1, d, res=dh2,
                                   name="norm_ffn_bwd")
    token = reducer.pair("w_out", _mm(mix, dh1, ta=True, out_dtype=BF16, name="mm_dw_out"))
    dmix = _mm(dh1, wout, tb=True, after=token, name="mm_dmix")

    dmq, dmk, dmv, g["mem_q_norm"], g["mem_k_norm"] = _mem_bwd(
        p, lay.o_mq, mkv, sp["mem_q_norm"], sp["mem_k_norm"], dmix, nf + ng, tq)
    dmkv = jnp.concatenate([dmk, dmv], axis=1)
    token = reducer.pair("w_mem_kv", _mm(mem_n, dmkv, ta=True, out_dtype=BF16, name="mm_dw_memkv"))
    dmem_n = _mm(dmkv, wmkv, tb=True, after=token, name="mm_dmem")
    token = reducer.ship("mix", ["w_out", "w_mem_kv"], dmem_n)
    _, g["mem_norm"] = _norm_bwd(mem, 0, sp["mem_norm"], dmem_n, 0, 1, d, name="mem_norm_bwd")

    do_g, dgz, g["gdn_out_norm"] = _norm_bwd(o_g, 0, sp["gdn_out_norm"] + token[0, 0], dmix, nf, ng, hd, z=p,
                                             zoff=lay.o_gz, name="gdn_out_bwd")
    dq, dk, dv, dgc, dgr, dbc = _gdn_bwd(qkv, gcol, grow, bcol, states, do_g, ng)
    dgqkv, g["gdn_conv"] = _conv_bwd(p, lay.o_gq, sp["gdn_conv"], (dq, dk, dv), ng)
    dg_t = dgc.reshape(ng, t) + dgr.reshape(ng, t)
    db_t = dbc.reshape(ng, t)

    dfq_n, dfk_n, dfv, dcc, dcr = _fox_bwd(fq, fk, fv, cc, cr, o_fox, lse, dmix, nf, tq, tk)
    dfq, g["fox_q_norm"] = _norm_bwd(p, lay.o_fq, sp["fox_q_norm"], dfq_n, 0, nf, hd, name="fox_qnorm_bwd")
    dfk, g["fox_k_norm"] = _norm_bwd(p, lay.o_fk, sp["fox_k_norm"], dfk_n, 0, nf, hd, name="fox_knorm_bwd")
    dc_t = dcc.reshape(nf, t) + dcr.reshape(nf, t)

    lanes_left = hd - nf - 2 * ng
    dvals = jnp.concatenate([jnp.zeros((t, nf), F32), dg_t.T, db_t.T, jnp.zeros((t, lanes_left), F32)], axis=1)
    dcsum = jnp.concatenate([dc_t.T, jnp.zeros((t, hd - nf), F32)], axis=1)
    dsm, dpa, dpb = _small_bwd(p, lay.o_sm, pa, pb, dvals, dcsum, nf, ng)
    g["fox_f_bias"] = dpb[:, :nf]
    g["gdn_dt_bias"] = dpb[:, nf:nf + ng]
    g["gdn_a_log"] = dpa[:, nf:nf + ng]

    pad = jnp.zeros((t, lay.cols - (lay.o_sm + 1) * hd), F32)
    dp = jnp.concatenate([dfq, dfk, dfv, dgqkv, dgz, dmq, dsm, pad], axis=1)
    token = reducer.start("in", {"w_in": _mm(u, dp, ta=True, out_dtype=BF16, name="mm_dw_in")})
    du = _mm(dp, win, tb=True, after=token, name="mm_du")
    dx, g["norm_mix"] = _norm_bwd(x, 0, sp["norm_mix"], du, 0, 1, d, res=dh1, name="norm_mix_bwd")
    return loss_blk, dx, g


ANY = pl.BlockSpec(memory_space=pl.ANY)


def _me():
    x, y, c = lax.axis_index("x"), lax.axis_index("y"), lax.axis_index("c")
    chips = [(1 - x, y), (x, 1 - y), (1 - x, 1 - y)]
    return x, y, c, chips


def _slot(axis, k):
    return k if axis == 0 else 2 * (k % 2) + k // 2


def _slab(ref, axis, rows, cols, k, h):
    half = rows // 2
    return ref.at[pl.ds(_slot(axis, k) * rows + h * half, half), :]


def _remote(src, dst, send_sem, recv_sem, dev):
    return pltpu.make_async_remote_copy(src_ref=src, dst_ref=dst, send_sem=send_sem, recv_sem=recv_sem,
                                        device_id=dev, device_id_type=MESH)


HBM = pl.BlockSpec(memory_space=pltpu.HBM)
SEM = pl.BlockSpec(memory_space=pltpu.SEMAPHORE)
SPLIT = pltpu.CompilerParams(has_side_effects=pltpu.SideEffectType.DATAFLOW_SIDE_EFFECTING)
TOKEN = jax.ShapeDtypeStruct((8, HEAD_DIM), F32)


def _in_hbm(v):
    return pltpu.with_memory_space_constraint(v, pltpu.HBM)


def _cast_place(shard, axis, name, col_fn=None, out_cols=None):
    r, c = shard.shape
    oc = out_cols or c
    tr = _tile(r, 512 if col_fn is None else 64, 16)
    tc = _tile(c, 2048) if col_fn is None else c
    otc = tc if col_fn is None else oc
    nb = r // tr
    chip = 2 * lax.axis_index("x") + lax.axis_index("y")
    slot = jnp.reshape(_slot(axis, chip), (1,)).astype(jnp.int32)

    def body(slot_ref, x_ref, o_ref):
        x = x_ref[...]
        o_ref[...] = (x if col_fn is None else col_fn(x)).astype(BF16)

    return pl.pallas_call(
        body, name=name,
        grid_spec=pltpu.PrefetchScalarGridSpec(
            num_scalar_prefetch=1, grid=(nb, c // tc),
            in_specs=[pl.BlockSpec((tr, tc), lambda i, l, s: (i, l))],
            out_specs=pl.BlockSpec((tr, otc), lambda i, l, s: (s[0] * nb + i, l))),
        out_shape=jax.ShapeDtypeStruct((4 * r, oc), BF16),
        compiler_params=_cparams(("parallel", "parallel")),
    )(slot, shard)


def _gather_start(bufs, axes, shapes, groups, name):
    n = len(bufs)

    def body(*refs):
        dst = refs[n:2 * n]
        sems = refs[2 * n:2 * n + 2 * len(groups)]
        token = refs[-1]
        x, y, c, chips = _me()
        k = 2 * x + y
        for gi, ws in enumerate(groups):
            for i, w in enumerate(ws):
                r, cl = shapes[w]
                place = _slab(dst[w], axes[w], r, cl, k, c)
                for j, (px, py) in enumerate(chips):
                    _remote(place, place, sems[2 * gi].at[3 * i + j], sems[2 * gi + 1].at[3 * i + j],
                            (px, py, c)).start()
        token[...] = jnp.zeros_like(token)

    sem_shapes = [pltpu.SemaphoreType.DMA((3 * len(ws),)) for ws in groups for _ in range(2)]
    outs = pl.pallas_call(
        body, name=name, in_specs=[HBM] * n,
        out_specs=[HBM] * n + [SEM] * len(sem_shapes) + [pl.BlockSpec(memory_space=pltpu.VMEM)],
        out_shape=[pltpu.HBM(b.shape, b.dtype) for b in bufs] + sem_shapes + [TOKEN],
        input_output_aliases={w: w for w in range(n)}, compiler_params=SPLIT,
    )(*[_in_hbm(b) for b in bufs])
    sems = outs[n:-1]
    return outs[:n], [(sems[2 * g], sems[2 * g + 1]) for g in range(len(groups))], outs[-1]


def _gather_wait(bufs, axes, shapes, sems, after, name):
    n = len(bufs)

    def body(*refs):
        send_sems, recv_sems = refs[n], refs[n + 1]
        dst = refs[n + 3:]
        x, y, c, chips = _me()
        k = 2 * x + y
        for i in range(n):
            r, cl = shapes[i]
            for j, (px, py) in enumerate(chips):
                got = _slab(dst[i], axes[i], r, cl, 2 * px + py, c)
                _remote(got, got, send_sems.at[3 * i + j], recv_sems.at[3 * i + j], (px, py, c)).wait_recv()
        for i in range(n):
            r, cl = shapes[i]
            mine = _slab(dst[i], axes[i], r, cl, k, c)
            for j, (px, py) in enumerate(chips):
                _remote(mine, mine, send_sems.at[3 * i + j], recv_sems.at[3 * i + j], (px, py, c)).wait_send()

    return pl.pallas_call(
        body, name=name, in_specs=[HBM] * n + [SEM, SEM, ANY], out_specs=[HBM] * n,
        out_shape=[pltpu.HBM(b.shape, b.dtype) for b in bufs],
        input_output_aliases={i: i for i in range(n)}, compiler_params=SPLIT,
    )(*bufs, sems[0], sems[1], after)


def _gather_forward(bufs, axes, shapes, name):
    n = len(bufs)

    def body(*refs):
        dst = refs[n:2 * n]
        send_sems, recv_sems = refs[2 * n:]
        x, y, c, chips = _me()
        sibling = (x, y, 1 - c)
        sends = []
        for i in range(n):
            r, cl = shapes[i]
            for j, (px, py) in enumerate(chips):
                got = _slab(dst[i], axes[i], r, cl, 2 * px + py, c)
                cp = _remote(got, got, send_sems.at[3 * i + j], recv_sems.at[3 * i + j], sibling)
                cp.start()
                sends.append(cp)
        for i in range(n):
            r, cl = shapes[i]
            for j, (px, py) in enumerate(chips):
                got = _slab(dst[i], axes[i], r, cl, 2 * px + py, 1 - c)
                _remote(got, got, send_sems.at[3 * i + j], recv_sems.at[3 * i + j], sibling).wait_recv()
        for cp in sends:
            cp.wait_send()

    return pl.pallas_call(
        body, name=name, in_specs=[ANY] * n, out_specs=[ANY] * n,
        out_shape=[jax.ShapeDtypeStruct(b.shape, b.dtype) for b in bufs],
        input_output_aliases={i: i for i in range(n)},
        scratch_shapes=[pltpu.SemaphoreType.DMA((3 * n,)), pltpu.SemaphoreType.DMA((3 * n,))],
    )(*bufs)


def _split_start(name, arrays, geometry, count):
    n = len(arrays)

    def body(*refs):
        send, recv, token = refs[2 * n:]
        for i, (src, dst, _, dev) in enumerate(geometry(refs[n:2 * n])):
            _remote(src, dst, send.at[i], recv.at[i], dev).start()
        token[...] = jnp.zeros_like(token)

    sem = pltpu.SemaphoreType.DMA((count,))
    outs = pl.pallas_call(
        body, name=name, in_specs=[HBM] * n,
        out_specs=[HBM] * n + [SEM, SEM, pl.BlockSpec(memory_space=pltpu.VMEM)],
        out_shape=[pltpu.HBM(v.shape, v.dtype) for v in arrays] + [sem, sem, TOKEN],
        input_output_aliases={i: i for i in range(n)}, compiler_params=SPLIT,
    )(*[_in_hbm(v) for v in arrays])
    return list(outs[:n]), (outs[n], outs[n + 1]), outs[-1]


def _split_wait(name, arrays, sems, after, geometry):
    n = len(arrays)

    def body(*refs):
        send, recv = refs[n], refs[n + 1]
        copies = geometry(refs[n + 3:])
        for i, (_, _, land, dev) in enumerate(copies):
            _remote(land, land, send.at[i], recv.at[i], dev).wait_recv()
        for i, (src, _, _, dev) in enumerate(copies):
            _remote(src, src, send.at[i], recv.at[i], dev).wait_send()

    return list(pl.pallas_call(
        body, name=name, in_specs=[HBM] * n + [SEM, SEM, ANY], out_specs=[HBM] * n,
        out_shape=[pltpu.HBM(v.shape, v.dtype) for v in arrays],
        input_output_aliases={i: i for i in range(n)}, compiler_params=SPLIT,
    )(*arrays, sems[0], sems[1], after))


def _forward_geometry(axes, shapes):
    def geometry(bufs):
        x, y, c, chips = _me()
        out = []
        for i, buf in enumerate(bufs):
            r, cl = shapes[i]
            for px, py in chips:
                got = _slab(buf, axes[i], r, cl, 2 * px + py, c)
                out.append((got, got, _slab(buf, axes[i], r, cl, 2 * px + py, 1 - c), (x, y, 1 - c)))
        return out
    return geometry


def _pair_geometry(axes, shapes):
    def geometry(refs):
        n = len(refs) // 2
        x, y, c, _ = _me()
        out = []
        for w in range(n):
            r, cl = shapes[w]
            for j in range(4):
                land = refs[n + w].at[j]
                out.append((_slab(refs[w], axes[w], r, cl, j, 1 - c), land, land, (x, y, 1 - c)))
        return out
    return geometry


def _pair_exchange(fulls, axes, shapes, tag):
    n = len(fulls)

    def body(*refs):
        src, dst = refs[:n], refs[n:2 * n]
        send_sems, recv_sems = refs[2 * n:]
        x, y, c, _ = _me()
        sibling = (x, y, 1 - c)
        cps = []
        for w in range(n):
            r, cl = shapes[w]
            for j in range(4):
                cp = _remote(_slab(src[w], axes[w], r, cl, j, 1 - c), dst[w].at[j],
                             send_sems.at[4 * w + j], recv_sems.at[4 * w + j], sibling)
                cp.start()
                cps.append(cp)
        for cp in cps:
            cp.wait()

    out_shape = [jax.ShapeDtypeStruct((4, r // 2, cl), f.dtype) for (r, cl), f in zip(shapes, fulls)]
    return pl.pallas_call(
        body, name="reduce_pair_exchange_" + tag, in_specs=[ANY] * n, out_specs=[ANY] * n, out_shape=out_shape,
        scratch_shapes=[pltpu.SemaphoreType.DMA((4 * n,)), pltpu.SemaphoreType.DMA((4 * n,))],
    )(*fulls)


def _chip_start(parts, tag):
    n = len(parts)

    def body(*refs):
        src, land = refs[2 * n:3 * n], refs[3 * n:4 * n]
        send_sems, recv_sems, token = refs[4 * n:]
        x, y, c, chips = _me()
        k = 2 * x + y
        for w in range(n):
            for j, (px, py) in enumerate(chips):
                _remote(src[w].at[2 * px + py], land[w].at[k], send_sems.at[3 * w + j], recv_sems.at[3 * w + j],
                        (px, py, c)).start()
        token[...] = jnp.zeros_like(token)

    lands = [lax.empty(p.shape, p.dtype) for p in parts]
    sem = pltpu.SemaphoreType.DMA((3 * n,))
    outs = pl.pallas_call(
        body, name="reduce_ici_start_" + tag, in_specs=[HBM] * (2 * n),
        out_specs=[HBM] * (2 * n) + [SEM, SEM, pl.BlockSpec(memory_space=pltpu.VMEM)],
        out_shape=[pltpu.HBM(p.shape, p.dtype) for p in parts + lands] + [sem, sem, TOKEN],
        input_output_aliases={i: i for i in range(2 * n)}, compiler_params=SPLIT,
    )(*[_in_hbm(v) for v in parts + lands])
    return outs[:n], outs[n:2 * n], outs[2 * n], outs[2 * n + 1], outs[-1]


def _chip_wait(parts, lands, send_sems, recv_sems, after, tag):
    n = len(parts)

    def body(*refs):
        send, recv = refs[2 * n], refs[2 * n + 1]
        src, land = refs[2 * n + 3:3 * n + 3], refs[3 * n + 3:]
        x, y, c, chips = _me()
        for w in range(n):
            for j, (px, py) in enumerate(chips):
                got = land[w].at[2 * px + py]
                _remote(got, got, send.at[3 * w + j], recv.at[3 * w + j], (px, py, c)).wait_recv()
        for w in range(n):
            for j, (px, py) in enumerate(chips):
                sent = src[w].at[2 * px + py]
                _remote(sent, sent, send.at[3 * w + j], recv.at[3 * w + j], (px, py, c)).wait_send()

    outs = pl.pallas_call(
        body, name="reduce_ici_wait_" + tag, in_specs=[HBM] * (2 * n) + [SEM, SEM, ANY], out_specs=[HBM] * (2 * n),
        out_shape=[pltpu.HBM(p.shape, p.dtype) for p in parts + lands],
        input_output_aliases={i: i for i in range(2 * n)}, compiler_params=SPLIT,
    )(*parts, *lands, send_sems, recv_sems, after)
    chip = 2 * lax.axis_index("x") + lax.axis_index("y")
    return [lax.dynamic_update_slice(s, lax.dynamic_index_in_dim(p, chip, 0, keepdims=True), (chip, 0, 0))
            for p, s in zip(outs[:n], outs[n:])]


def _half_swap(halves, tag):
    n = len(halves)
    core = lax.axis_index("c")
    bufs = [lax.dynamic_update_slice(lax.empty((2,) + h.shape, h.dtype), h[None], (core, 0, 0)) for h in halves]

    def body(*refs):
        dst = refs[n:2 * n]
        send_sems, recv_sems = refs[2 * n:]
        x, y, c, _ = _me()
        sibling = (x, y, 1 - c)
        cps = []
        for w in range(n):
            cp = _remote(dst[w].at[c], dst[w].at[c], send_sems.at[w], recv_sems.at[w], sibling)
            cp.start()
            cps.append(cp)
        for w in range(n):
            other = dst[w].at[1 - c]
            _remote(other, other, send_sems.at[w], recv_sems.at[w], sibling).wait_recv()
        for cp in cps:
            cp.wait_send()

    outs = pl.pallas_call(
        body, name="reduce_half_swap_" + tag, in_specs=[ANY] * n, out_specs=[ANY] * n,
        out_shape=[jax.ShapeDtypeStruct(b.shape, b.dtype) for b in bufs],
        input_output_aliases={w: w for w in range(n)},
        scratch_shapes=[pltpu.SemaphoreType.DMA((n,)), pltpu.SemaphoreType.DMA((n,))],
    )(*bufs)
    return [o.reshape(2 * o.shape[1], o.shape[2]) for o in outs]


def _add_parts(full, axis, rows, sib, name):
    _, r, c = sib.shape
    tr, tc = _tile(r, 256, 16), _tile(c, 2048)
    nb = r // tr
    core = jnp.reshape(lax.axis_index("c"), (1,)).astype(jnp.int32)

    def body(c_ref, a_ref, b_ref, o_ref):
        o_ref[0] = (a_ref[...].astype(F32) + b_ref[0].astype(F32)).astype(BF16)

    blk = pl.BlockSpec((1, tr, tc), lambda j, i, l, cr: (j, i, l))
    return pl.pallas_call(
        body, name=name,
        grid_spec=pltpu.PrefetchScalarGridSpec(
            num_scalar_prefetch=1, grid=(4, nb, c // tc),
            in_specs=[pl.BlockSpec((tr, tc), lambda j, i, l, cr: ((_slot(axis, j) * 2 + cr[0]) * nb + i, l)), blk],
            out_specs=blk),
        out_shape=jax.ShapeDtypeStruct(sib.shape, BF16),
        compiler_params=_cparams(("parallel", "parallel", "parallel")),
    )(core, full, sib)


def _sum_slots(a, name):
    _, r, c = a.shape
    tr, tc = _tile(r, 256, 8), _tile(c, 2048)

    def body(a_ref, o_ref):
        v = a_ref[...].astype(F32)
        o_ref[...] = ((v[0] + v[1]) + v[2]) + v[3]

    return pl.pallas_call(
        body, name=name, grid=(r // tr, c // tc),
        in_specs=[pl.BlockSpec((4, tr, tc), lambda i, l: (0, i, l))],
        out_specs=pl.BlockSpec((tr, tc), lambda i, l: (i, l)),
        out_shape=jax.ShapeDtypeStruct((r, c), F32),
        compiler_params=_cparams(("parallel", "parallel")),
    )(a)


class _Reducer:
    def __init__(self, spec):
        self.spec = spec
        self.paired = {}
        self.pending = []

    def pair(self, name, full):
        ax, shp = self.spec[name]
        land = lax.empty((4, shp[0] // 2, shp[1]), full.dtype)
        arrays, sems, token = _split_start("reduce_pair_start_" + name, [full, land], _pair_geometry([ax], [shp]), 4)
        self.paired[name] = (arrays, sems)
        return token

    def ship(self, tag, names, after):
        parts = []
        for n in names:
            ax, shp = self.spec[n]
            arrays, sems = self.paired.pop(n)
            full, sib = _split_wait("reduce_pair_wait_" + n, arrays, sems, after, _pair_geometry([ax], [shp]))
            parts.append(_add_parts(full, ax, shp[0], sib, name=f"reduce_add_{n}"))
        parts, lands, send, recv, token = _chip_start(parts, tag)
        self.pending.append((tag, names, parts, lands, send, recv))
        return token

    def start(self, tag, grads):
        names = list(grads)
        fulls, axes = [grads[n] for n in names], [self.spec[n][0] for n in names]
        shapes = [self.spec[n][1] for n in names]
        from_sibling = _pair_exchange(fulls, axes, shapes, tag)
        parts = [_add_parts(f, a, r, s, name=f"reduce_add_{n}")
                 for n, f, a, (r, cl), s in zip(names, fulls, axes, shapes, from_sibling)]
        parts, lands, send, recv, token = _chip_start(parts, tag)
        self.pending.append((tag, names, parts, lands, send, recv))
        return token

    def finish(self, after):
        out = {}
        for tag, names, parts, lands, send, recv in self.pending:
            slots = _chip_wait(parts, lands, send, recv, after, tag)
            halves = [_sum_slots(s, name=f"reduce_sum_{n}") for n, s in zip(names, slots)]
            out.update(zip(names, _half_swap(halves, tag)))
        return out


def _allreduce_small(pack):
    rows = pack.shape[0]

    def body(p_ref, o_ref, slots, send_sems, recv_sems):
        x, y, c, _ = _me()
        me = 4 * x + 2 * y + c
        slots[me] = p_ref[...]
        cps = []
        for r in range(1, 8):
            peer = (x ^ (r >> 2), y ^ ((r >> 1) & 1), c ^ (r & 1))
            cp = _remote(p_ref, slots.at[me], send_sems.at[r - 1], recv_sems.at[r - 1], peer)
            cp.start()
            cps.append(cp)
        for r in range(1, 8):
            frm = me ^ r
            _remote(slots.at[frm], slots.at[frm], send_sems.at[r - 1], recv_sems.at[r - 1], (x, y, c)).wait_recv()
        for cp in cps:
            cp.wait_send()
        acc = slots[0]
        for s in range(1, 8):
            acc = acc + slots[s]
        o_ref[...] = acc

    vm = pl.BlockSpec(memory_space=pltpu.VMEM)
    return pl.pallas_call(
        body, name="allreduce_small", in_specs=[vm], out_specs=vm,
        out_shape=jax.ShapeDtypeStruct(pack.shape, F32),
        scratch_shapes=[pltpu.VMEM((8, rows, HEAD_DIM), F32), pltpu.SemaphoreType.DMA((7,)),
                        pltpu.SemaphoreType.DMA((7,))],
    )(pack)


_ROWS = ["norm_mix", "norm_ffn", "mem_norm", "fox_q_norm", "fox_k_norm", "gdn_out_norm", "mem_q_norm",
         "mem_k_norm", "fox_f_bias", "gdn_a_log", "gdn_dt_bias"]


def _pack_rows(vals):
    out = []
    for name in _ROWS:
        v = vals[name].reshape(-1)
        n = -(-v.shape[0] // HEAD_DIM) * HEAD_DIM
        out.append(jnp.pad(v, (0, n - v.shape[0])).reshape(-1, HEAD_DIM))
    return jnp.concatenate(out, axis=0)


def _unpack_rows(pack, like):
    out, r = {}, 0
    for name in _ROWS:
        n = like[name].shape[-1]
        nr = -(-n // HEAD_DIM)
        out[name] = pack[r:r + nr].reshape(1, -1)[:, :n]
        r += nr
    return out, r


def kernel(x, mem, norm_mix, w_in, fox_f_bias, fox_q_norm, fox_k_norm, gdn_conv, gdn_a_log, gdn_dt_bias, gdn_out_norm, mem_norm, w_mem_kv, mem_q_norm, mem_k_norm, w_out, norm_ffn, w_gate_up, w_down, loss_target, m_norm_mix, m_w_in, m_fox_f_bias, m_fox_q_norm, m_fox_k_norm, m_gdn_conv, m_gdn_a_log, m_gdn_dt_bias, m_gdn_out_norm, m_mem_norm, m_w_mem_kv, m_mem_q_norm, m_mem_k_norm, m_w_out, m_norm_ffn, m_w_gate_up, m_w_down, v_norm_mix, v_w_in, v_fox_f_bias, v_fox_q_norm, v_fox_k_norm, v_gdn_conv, v_gdn_a_log, v_gdn_dt_bias, v_gdn_out_norm, v_mem_norm, v_w_mem_kv, v_mem_q_norm, v_mem_k_norm, v_w_out, v_norm_ffn, v_w_gate_up, v_w_down):
    a = dict(locals())
    d = x.shape[-1]
    lay = _Layout(d)
    chip = 2 * lax.axis_index("x") + lax.axis_index("y")
    small = {n: a[n] for n in _ROWS}
    big = ["w_in", "w_mem_kv", "w_out", "w_gate_up", "w_down"]
    axes = [0, 0, 0, 1, 0]

    conv_cols = gdn_conv.shape[-1]
    conv_n = CONV_WIDTH * conv_cols
    conv_rows = -(-conv_n // HEAD_DIM)
    conv_blk = jnp.pad(gdn_conv.reshape(-1), (0, 32 * HEAD_DIM - conv_n)).reshape(32, HEAD_DIM)
    axis_of = dict(zip(big, axes), conv=0)
    shape_of = {n: a[n].shape[1:] for n in big}
    shape_of["w_in"] = (w_in.shape[1], lay.cols)
    shape_of["conv"] = conv_blk.shape
    placed = {n: _cast_place(a[n][0], axis_of[n], "cast_" + n) for n in big[1:]}
    placed["w_in"] = _cast_place(w_in[0], 0, "cast_w_in", lay.regroup, lay.cols)
    placed["conv"] = lax.dynamic_update_slice(lax.empty((4 * 32, HEAD_DIM), F32), conv_blk, (chip * 32, 0))
    grouped = {"in": ["w_in"], "mixer": ["w_mem_kv", "conv"], "out": ["w_out"], "gate_up": ["w_gate_up"],
               "down": ["w_down"]}
    inflight = {}

    def start(tags, name):
        names = [n for t in tags for n in grouped[t]]
        bufs, sems, _ = _gather_start([placed[n] for n in names], [axis_of[n] for n in names],
                                      [shape_of[n] for n in names],
                                      [[names.index(n) for n in grouped[t]] for t in tags], name)
        for t, pair in zip(tags, sems):
            inflight[t] = ([bufs[names.index(n)] for n in grouped[t]], pair)

    start(["in"], "gather_ici_start_in")
    start(["mixer", "out", "gate_up", "down"], "gather_ici_start_rest")

    forwarding = {}

    def prefetch(tag, after):
        bufs, sem_pair = inflight.pop(tag)
        ax, shp = [axis_of[n] for n in grouped[tag]], [shape_of[n] for n in grouped[tag]]
        got = _gather_wait(bufs, ax, shp, sem_pair, after, "gather_ici_wait_" + tag)
        geometry = _forward_geometry(ax, shp)
        got, sems, _ = _split_start("gather_forward_start_" + tag, got, geometry, 3 * len(got))
        forwarding[tag] = (got, sems, geometry)

    def weights(tag, after):
        got, sems, geometry = forwarding.pop(tag)
        got = _split_wait("gather_forward_wait_" + tag, got, sems, after, geometry)
        if tag != "mixer":
            return got
        taps = got[1].reshape(4, 32 * HEAD_DIM)[:, :conv_n].reshape(4, CONV_WIDTH, conv_cols)
        return got[0], jnp.transpose(taps, (1, 0, 2)).reshape(CONV_WIDTH, 4 * conv_cols)

    sp = dict(small)
    reducer = _Reducer({n: (axis_of[n], shape_of[n]) for n in big})
    loss_blk, dx, g = _local_step(x[0], mem[0], loss_target[0], prefetch, weights, reducer, sp)

    gsmall = {n: g[n] for n in _ROWS}
    pack = jnp.concatenate([_pack_rows(gsmall), g["gdn_conv"].reshape(-1, HEAD_DIM), loss_blk], axis=0)
    pack = jnp.pad(pack, ((0, -pack.shape[0] % 8), (0, 0)))
    tot = _allreduce_small(pack)
    gs, r0 = _unpack_rows(tot, small)
    conv_g = tot[r0:r0 + CONV_WIDTH * 4 * conv_cols // HEAD_DIM].reshape(CONV_WIDTH, 4 * conv_cols)
    gs_conv = lax.dynamic_slice_in_dim(conv_g, chip * conv_cols, conv_cols, axis=1)
    loss = tot[r0 + CONV_WIDTH * 4 * conv_cols // HEAD_DIM, 0]
    reduced = reducer.finish(tot)

    out = {"loss": loss, "grad_x": dx[None]}
    for n, gsh in reduced.items():
        res = _adamw(a[n][0], gsh, a["m_" + n][0], a["v_" + n][0], g_fn=lay.ungroup if n == "w_in" else None,
                     name="adamw_" + n)
        for pre, r in zip(["grad_", "delta_", "new_m_", "new_v_"], res):
            out[pre + n] = r[None]
    conv_pad = lambda v: jnp.pad(v.reshape(-1), (0, conv_rows * HEAD_DIM - conv_n)).reshape(conv_rows, HEAD_DIM)
    packs = []
    for src, cv in [(small, gdn_conv), (gs, gs_conv), ({n: a["m_" + n] for n in _ROWS}, m_gdn_conv),
                    ({n: a["v_" + n] for n in _ROWS}, v_gdn_conv)]:
        packs.append(jnp.concatenate([_pack_rows(src), conv_pad(cv)], axis=0))
    res = _adamw(*packs, name="adamw_small")
    for pre, r in zip(["grad_", "delta_", "new_m_", "new_v_"], res):
        vals, r1 = _unpack_rows(r, small)
        for n in _ROWS:
            out[pre + n] = vals[n]
        out[pre + "gdn_conv"] = r[r1:r1 + conv_rows].reshape(-1)[:conv_n].reshape(gdn_conv.shape)
    names = ["norm_mix", "w_in", "fox_f_bias", "fox_q_norm", "fox_k_norm", "gdn_conv", "gdn_a_log", "gdn_dt_bias",
             "gdn_out_norm", "mem_norm", "w_mem_kv", "mem_q_norm", "mem_k_norm", "w_out", "norm_ffn", "w_gate_up",
             "w_down"]
    return (out["loss"], out["grad_x"], *[out[p + n] for p in ["grad_", "delta_", "new_m_", "new_v_"] for n in names])
```

```python
import functools
import math

import jax
import jax.numpy as jnp
from jax import lax
from jax.experimental import pallas as pl
from jax.experimental.pallas import tpu as pltpu

F32, BF16 = jnp.float32, jnp.bfloat16
HEAD_DIM = 128
CHUNK = 64
N_MEM_HEADS = 4
CONV_WIDTH = 4
NORM_EPS = 1e-6
ADAM_LR, ADAM_B1, ADAM_B2, ADAM_EPS, ADAM_WD, ADAM_STEP = 0.001, 0.9, 0.999, 1e-08, 0.01, 10
VMEM_LIMIT = 48 * 1024 * 1024
NEG = -1e30
MESH = pl.DeviceIdType.MESH


def _cparams(sem=None, **kw):
    if sem is not None:
        kw["dimension_semantics"] = sem
    return pltpu.CompilerParams(vmem_limit_bytes=VMEM_LIMIT, **kw)


def _tile(n, target, mult=128):
    best = None
    d = mult
    while d <= min(n, target):
        if n % d == 0:
            best = d
        d += mult
    return best if best is not None else n


def _dot(a, b, dims, hi):
    if a.ndim == 3:
        dn = (((dims[0][0] + 1,), (dims[1][0] + 1,)), ((0,), (0,)))
    else:
        dn = (dims, ((), ()))
    if hi is not None:
        return lax.dot_general(a, b, dn, precision=hi, preferred_element_type=F32)
    return lax.dot_general(a.astype(BF16), b.astype(BF16), dn, preferred_element_type=F32)


def _make_dots(hi, cotangent=None):
    @jax.custom_vjp
    def nn(a, b):
        return _dot(a, b, ((1,), (0,)), hi)

    @jax.custom_vjp
    def nt(a, b):
        return _dot(a, b, ((1,), (1,)), hi)

    @jax.custom_vjp
    def tn(a, b):
        return _dot(a, b, ((0,), (0,)), hi)

    bnn, bnt, btn = cotangent or (nn, nt, tn)
    nn.defvjp(lambda a, b: (nn(a, b), (a, b)), lambda r, g: (bnt(g, r[1]), btn(r[0], g)))
    nt.defvjp(lambda a, b: (nt(a, b), (a, b)), lambda r, g: (bnn(g, r[1]), btn(g, r[0])))
    tn.defvjp(lambda a, b: (tn(a, b), (a, b)), lambda r, g: (bnt(r[1], g), bnn(r[0], g)))
    return nn, nt, tn


_nn, _nt, _tn = _make_dots(None)
_nn_hi, _nt_hi, _tn_hi = _make_dots(lax.Precision.HIGHEST)
_nn_x3, _nt_x3, _tn_x3 = _make_dots(lax.Precision.HIGH, (_nn, _nt, _tn))


def _sigmoid(x):
    return 1.0 / (1.0 + jnp.exp(-x))


@jax.custom_vjp
def _softplus(x):
    return jnp.maximum(x, 0.0) + jnp.log(1.0 + jnp.exp(-jnp.abs(x)))


_softplus.defvjp(lambda x: (_softplus(x), x), lambda x, g: (g * _sigmoid(x),))


def _silu(x):
    return x * _sigmoid(x)


def _rms_fn(x, gain, z=None):
    y = x * lax.rsqrt(jnp.mean(x * x, axis=-1, keepdims=True) + NORM_EPS) * gain
    if z is not None:
        y = y * _silu(z)
    return y


def _mm(a, b, *, ta=False, tb=False, out_dtype=F32, res=None, stack=None, after=None, name):
    a2, b2 = a.shape[-2:], b.shape[-2:]
    ns = b.shape[0] if stack else 1
    m = a2[1] if ta else a2[0]
    k = a2[0] if ta else a2[1]
    n = b2[0] if tb else b2[1]
    assert k == (b2[1] if tb else b2[0])
    tm, tn, tk = _mm_tiles(m, n, k, ns if stack == "sum" else 1, a.dtype.itemsize, b.dtype.itemsize,
                           jnp.dtype(out_dtype).itemsize, res is not None)
    nk = k // tk
    single = nk == 1 and stack != "sum"
    dims = ((0 if ta else 1,), (1 if tb else 0,))
    if stack == "sum":
        order = lambda g0, g1, g2, g3: (g2, g0, g1, g3)
        grid = (m // tm, n // tn, ns, nk)
    else:
        order = lambda g0, g1, g2, g3: (g0, g1, g2, g3)
        grid = (ns, m // tm, n // tn, nk)

    def body(*refs):
        if after is not None:
            refs = refs[:2 + (res is not None)] + refs[3 + (res is not None):]
        if single:
            a_ref, b_ref = refs[:2]
            r = lax.dot_general(a_ref[...].astype(BF16), b_ref[...].astype(BF16), (dims, ((), ())),
                                preferred_element_type=F32)
            if res is not None:
                r = r + refs[2][...]
            refs[-1][...] = r.astype(out_dtype)
            return
        if res is None:
            a_ref, b_ref, o_ref, acc = refs
        else:
            a_ref, b_ref, r_ref, o_ref, acc = refs
        s, _, _, kk = order(*[pl.program_id(d) for d in range(4)])
        first = kk == 0
        last = kk == nk - 1
        if stack == "sum":
            first, last = first & (s == 0), last & (s == ns - 1)

        @pl.when(first)
        def _():
            acc[...] = jnp.zeros_like(acc)

        acc[...] += lax.dot_general(a_ref[...].astype(BF16), b_ref[...].astype(BF16), (dims, ((), ())),
                                    preferred_element_type=F32)

        @pl.when(last)
        def _():
            r = acc[...]
            if res is not None:
                r = r + r_ref[...]
            o_ref[...] = r.astype(out_dtype)

    def spec(shape, idx, stacked):
        if stacked:
            return pl.BlockSpec((None,) + shape, lambda *g: (order(*g)[0],) + idx(*order(*g)))
        return pl.BlockSpec(shape, lambda *g: idx(*order(*g)))

    a_spec = (spec((tk, tm), lambda s, i, j, kk: (kk, i), stack == "sum") if ta
              else spec((tm, tk), lambda s, i, j, kk: (i, kk), stack == "sum"))
    b_spec = (spec((tn, tk), lambda s, i, j, kk: (j, kk), bool(stack)) if tb
              else spec((tk, tn), lambda s, i, j, kk: (kk, j), bool(stack)))
    o_spec = spec((tm, tn), lambda s, i, j, kk: (i, j), stack == "out")
    ins, specs = [a, b], [a_spec, b_spec]
    if res is not None:
        ins.append(res)
        specs.append(o_spec)
    if after is not None:
        ins.append(after)
        specs.append(pl.BlockSpec(after.shape, lambda *g: (0,) * after.ndim))
    sem = (("parallel", "parallel", "arbitrary", "arbitrary") if stack == "sum"
           else ("parallel", "parallel", "parallel", "arbitrary"))
    return pl.pallas_call(
        body, name=name, grid=grid, in_specs=specs, out_specs=o_spec,
        out_shape=jax.ShapeDtypeStruct(((ns,) if stack == "out" else ()) + (m, n), out_dtype),
        scratch_shapes=[] if single else [pltpu.VMEM((tm, tn), F32)],
        compiler_params=_cparams(sem),
    )(*ins)


MM_VMEM_BUDGET = 40 * 1024 * 1024


def _mm_tiles(m, n, k, ns, sa, sb, so, has_res):
    def divs(x, mult, cap):
        out = [d for d in range(mult, min(x, cap) + 1, mult) if x % d == 0]
        return out or [x]

    best = None
    for tk in divs(k, 128, 8192):
        nk = (k // tk) * ns
        for tm in divs(m, 8, 2048):
            for tn in divs(n, 128, 2048):
                vmem = 2 * (tm * tk * sa + tk * tn * sb + tm * tn * so) + (2 * tm * tn * 4 if has_res else 0)
                vmem += tm * tn * 4 if nk > 1 else 0
                if vmem > MM_VMEM_BUDGET:
                    continue
                steps = (m // tm) * (n // tn) * nk
                traffic = (m // tm) * k * n * sb * ns + (n // tn if nk > 1 else 1) * m * k * sa * ns
                cost = steps * 0.4e-6 + traffic / 2.5e12 + (nk * m * n * 8 / 6e12 if nk > 1 else 0)
                cost += 2.0 * m * n * k * ns / 7e14
                if best is None or cost < best[0]:
                    best = (cost, tm, tn, tk)
    return best[1:]


def _norm_fwd(x, xoff, gain, ncol, w, out_dtype, *, z=None, zoff=0, into=None, into_off=0, name):
    t = x.shape[0]
    tr = _tile(t, max(256, (1 << 18) // w), 8)

    def body(*refs):
        x_ref, g_ref, o_ref = refs[0], refs[1], refs[-1]
        y = _rms_fn(x_ref[...], g_ref[...]) if z is None else _rms_fn(x_ref[...], g_ref[...], refs[2][...])
        o_ref[...] = y.astype(out_dtype)

    ins = [x, gain]
    specs = [pl.BlockSpec((tr, w), lambda j, r: (r, xoff + j)), pl.BlockSpec((1, w), lambda j, r: (0, 0))]
    if z is not None:
        ins.append(z)
        specs.append(pl.BlockSpec((tr, w), lambda j, r: (r, zoff + j)))
    aliases = {}
    if into is not None:
        aliases = {len(ins): 0}
        ins.append(into)
        specs.append(pl.BlockSpec(memory_space=pl.ANY))
    return pl.pallas_call(
        body, name=name, grid=(ncol, t // tr), in_specs=specs,
        out_specs=pl.BlockSpec((tr, w), lambda j, r: (r, into_off + j)),
        out_shape=jax.ShapeDtypeStruct((t, ncol * w) if into is None else into.shape, out_dtype),
        input_output_aliases=aliases, compiler_params=_cparams(("parallel", "parallel")),
    )(*ins)


def _norm_bwd(x, xoff, gain, dy, dyoff, ncol, w, *, z=None, zoff=0, res=None, name):
    t = x.shape[0]
    tr = _tile(t, max(256, (1 << 18) // w), 8)

    def body(*refs):
        it = iter(refs)
        x_ref, g_ref = next(it), next(it)
        z_ref = next(it) if z is not None else None
        dy_ref = next(it)
        r_ref = next(it) if res is not None else None
        dx_ref = next(it)
        dz_ref = next(it) if z is not None else None
        dg_ref = next(it)

        @pl.when((pl.program_id(0) == 0) & (pl.program_id(1) == 0))
        def _():
            dg_ref[...] = jnp.zeros_like(dg_ref)

        args = (x_ref[...], g_ref[...]) + ((z_ref[...],) if z is not None else ())
        _, vjp = jax.vjp(_rms_fn, *args)
        grads = vjp(dy_ref[...].astype(F32))
        dx = grads[0]
        if res is not None:
            dx = dx + r_ref[...]
        dx_ref[...] = dx
        if z is not None:
            dz_ref[...] = grads[2]
        dg_ref[...] += grads[1]

    ins = [x, gain]
    specs = [pl.BlockSpec((tr, w), lambda j, r: (r, xoff + j)), pl.BlockSpec((1, w), lambda j, r: (0, 0))]
    if z is not None:
        ins.append(z)
        specs.append(pl.BlockSpec((tr, w), lambda j, r: (r, zoff + j)))
    ins.append(dy)
    specs.append(pl.BlockSpec((tr, w), lambda j, r: (r, dyoff + j)))
    blk = pl.BlockSpec((tr, w), lambda j, r: (r, j))
    if res is not None:
        ins.append(res)
        specs.append(blk)
    full = jax.ShapeDtypeStruct((t, ncol * w), F32)
    out_shape, out_specs = [full], [blk]
    if z is not None:
        out_shape.append(full)
        out_specs.append(blk)
    out_shape.append(jax.ShapeDtypeStruct((1, w), F32))
    out_specs.append(pl.BlockSpec((1, w), lambda j, r: (0, 0)))
    return pl.pallas_call(
        body, name=name, grid=(ncol, t // tr), in_specs=specs, out_specs=out_specs, out_shape=out_shape,
        compiler_params=_cparams(("arbitrary", "arbitrary")),
    )(*ins)


def _small_fn(x, pa, pb, nf, ng):
    lane = lax.broadcasted_iota(jnp.int32, x.shape, 1)
    zz = x + pb
    logf = -_softplus(-zz)
    g = -jnp.exp(pa) * _softplus(zz)
    beta = _sigmoid(x)
    return jnp.where(lane < nf, logf, jnp.where(lane < nf + ng, g, beta))


def _tri(n, upper):
    r = lax.broadcasted_iota(jnp.int32, (n, n), 0)
    c = lax.broadcasted_iota(jnp.int32, (n, n), 1)
    return jnp.where((c >= r) if upper else (c <= r), 1.0, 0.0).astype(F32)


def _small_fwd(p, off, pa, pb, nf, ng):
    t = p.shape[0]
    blk = HEAD_DIM
    nb = t // blk

    def body(x_ref, pa_ref, pb_ref, v_ref, c_ref):
        v_ref[...] = _small_fn(x_ref[...], pa_ref[...], pb_ref[...], nf, ng)
        tri = _tri(blk, False)

        carry = jnp.zeros((1, HEAD_DIM), F32)
        for i in range(nb):
            rows = slice(i * blk, (i + 1) * blk)
            c = _nn_hi(tri, v_ref[rows, :]) + carry
            c_ref[rows, :] = c
            carry = c[blk - 1:blk, :]

    row = pl.BlockSpec((1, HEAD_DIM), lambda i: (0, 0))
    out = pl.BlockSpec((t, HEAD_DIM), lambda i: (0, 0))
    return pl.pallas_call(
        body, name="small_fwd", grid=(1,),
        in_specs=[pl.BlockSpec((t, HEAD_DIM), lambda i: (0, off)), row, row], out_specs=[out, out],
        out_shape=[jax.ShapeDtypeStruct((t, HEAD_DIM), F32)] * 2,
        compiler_params=_cparams(("arbitrary",)),
    )(p, pa, pb)


def _small_bwd(p, off, pa, pb, dvals, dcsum, nf, ng):
    t = p.shape[0]
    blk = HEAD_DIM
    nb = t // blk

    def body(x_ref, pa_ref, pb_ref, dv_ref, dc_ref, dx_ref, dpa_ref, dpb_ref, tot_ref):
        tri = _tri(blk, True)

        carry = jnp.zeros((1, HEAD_DIM), F32)
        for i in reversed(range(nb)):
            rows = slice(i * blk, (i + 1) * blk)
            c = _nn_hi(tri, dc_ref[rows, :]) + carry
            tot_ref[rows, :] = c + dv_ref[rows, :]
            carry = c[0:1, :]
        f = functools.partial(_small_fn, nf=nf, ng=ng)
        _, vjp = jax.vjp(f, x_ref[...], pa_ref[...], pb_ref[...])
        dx, dpa, dpb = vjp(tot_ref[...])
        dx_ref[...] = dx
        dpa_ref[...] = dpa
        dpb_ref[...] = dpb

    row = pl.BlockSpec((1, HEAD_DIM), lambda i: (0, 0))
    full = pl.BlockSpec((t, HEAD_DIM), lambda i: (0, 0))
    return pl.pallas_call(
        body, name="small_bwd", grid=(1,),
        in_specs=[pl.BlockSpec((t, HEAD_DIM), lambda i: (0, off)), row, row, full, full],
        out_specs=[full, row, row],
        out_shape=[jax.ShapeDtypeStruct((t, HEAD_DIM), F32), jax.ShapeDtypeStruct((1, HEAD_DIM), F32),
                   jax.ShapeDtypeStruct((1, HEAD_DIM), F32)],
        scratch_shapes=[pltpu.VMEM((t, HEAD_DIM), F32)],
        compiler_params=_cparams(("arbitrary",)),
    )(p, pa, pb, dvals, dcsum)


def _fox_heads(nf, most):
    return next(h for h in range(most, 0, -1) if nf % h == 0)


def _fox_fwd(q, k, v, cc, cr, nf, tq, tk, d_mix):
    t = q.shape[0]
    scale = HEAD_DIM ** -0.5
    assert tq == tk

    vt = jnp.transpose(v.reshape(t // tk, tk, nf, HEAD_DIM), (2, 0, 3, 1))

    hp = _fox_heads(nf, 3)
    lanes = lambda h: slice(h * HEAD_DIM, (h + 1) * HEAD_DIM)

    def body(q_ref, k_ref, vt_ref, cc_ref, cr_ref, o_ref, lse_ref, mix_ref):
        i = pl.program_id(1)
        qs = [q_ref[:, lanes(h)] for h in range(hp)]
        cqs = [cr_ref[h, i] for h in range(hp)]
        ones = jnp.ones((8, tk), BF16)
        diff = lax.broadcasted_iota(jnp.int32, (tk, tq), 0) - lax.broadcasted_iota(jnp.int32, (tk, tq), 1)

        def scores(h, j):
            ks = pl.ds(pl.multiple_of(j * tk, tk), tk)
            return lax.dot_general(k_ref[ks, lanes(h)], qs[h], (((1,), (1,)), ((), ())),
                                   preferred_element_type=F32)

        def tile(h, j, m, l, acc, s, masked):
            ks = pl.ds(pl.multiple_of(j * tk, tk), tk)
            s = s * scale + cqs[h] - cc_ref[h, ks, :]
            if masked:
                s = jnp.where(diff <= 0, s, NEG)
            m_new = jnp.maximum(m, jnp.max(s, axis=0, keepdims=True))
            pr = jnp.exp(s - m_new).astype(BF16)
            alpha = jnp.exp(m - m_new)
            l = alpha * l + jnp.dot(ones, pr, preferred_element_type=F32)[:1]
            acc = alpha * acc + jnp.dot(vt_ref[h, j], pr, preferred_element_type=F32)
            return m_new, l, acc

        def step(j, carry):
            nxt = [scores(h, j + 1) for h in range(hp)]
            return tuple(tile(h, j, *carry[h], False) + (nxt[h],) for h in range(hp))

        init = tuple((jnp.full((1, tq), NEG, F32), jnp.zeros((1, tq), F32), jnp.zeros((HEAD_DIM, tq), F32),
                      scores(h, 0)) for h in range(hp))
        carry = lax.fori_loop(0, i, step, init)
        for h in range(hp):
            m, l, acc = tile(h, i, *carry[h], True)
            o = jnp.transpose(acc / l)
            o_ref[:, lanes(h)] = o
            mix_ref[:, lanes(h)] = o.astype(BF16)
            lse_ref[h, 0] = m + jnp.log(l)

    w = hp * HEAD_DIM
    qblk = pl.BlockSpec((tq, w), lambda h, i: (i, h))
    return pl.pallas_call(
        body, name="fox_fwd", grid=(nf // hp, t // tq),
        in_specs=[qblk, pl.BlockSpec((t, w), lambda h, i: (0, h)),
                  pl.BlockSpec((hp, t // tk, HEAD_DIM, tk), lambda h, i: (h, 0, 0, 0)),
                  pl.BlockSpec((hp, t, 1), lambda h, i: (h, 0, 0)),
                  pl.BlockSpec((hp, t // tk, 1, tk), lambda h, i: (h, 0, 0, 0))],
        out_specs=[qblk, pl.BlockSpec((hp, 1, 1, tq), lambda h, i: (h, i, 0, 0)), qblk],
        out_shape=[jax.ShapeDtypeStruct((t, nf * HEAD_DIM), F32), jax.ShapeDtypeStruct((nf, t // tq, 1, tq), F32),
                   jax.ShapeDtypeStruct((t, d_mix), BF16)],
        compiler_params=_cparams(("parallel", "parallel")),
    )(q, k, vt, cc, cr)


def _fox_bwd(q, k, v, cc, cr, o, lse, dmix, nf, tq, tk):
    t = q.shape[0]
    scale = HEAD_DIM ** -0.5
    assert tq == tk
    hp = _fox_heads(nf, 3)
    lanes = lambda h: slice(h * HEAD_DIM, (h + 1) * HEAD_DIM)
    kt = jnp.transpose(k.reshape(t // tk, tk, nf, HEAD_DIM), (2, 0, 3, 1))

    def body(q_ref, k_ref, kt_ref, v_ref, cc_ref, cr_ref, o_ref, lse_ref, do_ref,
             dq_ref, dk_ref, dv_ref, dcq_ref, dck_ref):
        i = pl.program_id(1)

        @pl.when(i == 0)
        def _():
            dk_ref[...] = jnp.zeros_like(dk_ref)
            dv_ref[...] = jnp.zeros_like(dv_ref)
            dck_ref[...] = jnp.zeros_like(dck_ref)

        diff = lax.broadcasted_iota(jnp.int32, (tk, tq), 0) - lax.broadcasted_iota(jnp.int32, (tk, tq), 1)
        qs = [q_ref[:, lanes(h)] for h in range(hp)]
        dos = [do_ref[:, lanes(h)] for h in range(hp)]
        do_b = [d.astype(BF16) for d in dos]
        cqs = [cr_ref[h, i] for h in range(hp)]
        lses = [lse_ref[h, 0] for h in range(hp)]
        deltas = [jnp.sum(jnp.transpose(dos[h] * o_ref[:, lanes(h)]), axis=0, keepdims=True) for h in range(hp)]

        def products(h, j):
            ks = pl.ds(pl.multiple_of(j * tk, tk), tk)
            nt = (((1,), (1,)), ((), ()))
            return (lax.dot_general(k_ref[ks, lanes(h)], qs[h], nt, preferred_element_type=F32),
                    lax.dot_general(v_ref[ks, lanes(h)], do_b[h], nt, preferred_element_type=F32))

        def tile(h, j, dqt, dcq, s, dp, masked):
            ks = pl.ds(pl.multiple_of(j * tk, tk), tk)
            pr = jnp.exp(s * scale + cqs[h] - cc_ref[h, ks, :] - lses[h])
            if masked:
                pr = jnp.where(diff <= 0, pr, 0.0)
            ds = pr * (dp - deltas[h])
            ds_b = ds.astype(BF16)
            dqt = dqt + jnp.dot(kt_ref[h, j], ds_b, preferred_element_type=F32)
            dk_ref[ks, lanes(h)] += jnp.dot(ds_b, qs[h], preferred_element_type=F32) * scale
            dv_ref[ks, lanes(h)] += jnp.dot(pr.astype(BF16), do_b[h], preferred_element_type=F32)
            dck_ref[h, ks, :] -= jnp.sum(ds, axis=1, keepdims=True)
            return dqt, dcq + jnp.sum(ds, axis=0, keepdims=True)

        def step(j, carry):
            nxt = [products(h, j + 1) for h in range(hp)]
            return tuple(tile(h, j, *carry[h], False) + nxt[h] for h in range(hp))

        init = tuple((jnp.zeros((HEAD_DIM, tq), F32), jnp.zeros((1, tq), F32)) + products(h, 0) for h in range(hp))
        carry = lax.fori_loop(0, i, step, init)
        for h in range(hp):
            dqt, dcq = tile(h, i, *carry[h], True)
            dq_ref[:, lanes(h)] = jnp.transpose(dqt) * scale
            dcq_ref[h, 0] = dcq

    w = hp * HEAD_DIM
    head_all = pl.BlockSpec((t, w), lambda h, i: (0, h))
    qblk = pl.BlockSpec((tq, w), lambda h, i: (i, h))
    colv = pl.BlockSpec((hp, t, 1), lambda h, i: (h, 0, 0))
    rows_all = pl.BlockSpec((hp, t // tk, 1, tk), lambda h, i: (h, 0, 0, 0))
    row_blk = pl.BlockSpec((hp, 1, 1, tq), lambda h, i: (h, i, 0, 0))
    wide = jax.ShapeDtypeStruct((t, nf * HEAD_DIM), F32)
    return pl.pallas_call(
        body, name="fox_bwd", grid=(nf // hp, t // tq),
        in_specs=[qblk, head_all, pl.BlockSpec((hp, t // tk, HEAD_DIM, tk), lambda h, i: (h, 0, 0, 0)), head_all,
                  colv, rows_all, qblk, row_blk, qblk],
        out_specs=[qblk, head_all, head_all, row_blk, colv],
        out_shape=[wide, wide, wide, jax.ShapeDtypeStruct((nf, t // tq, 1, tq), F32),
                   jax.ShapeDtypeStruct((nf, t, 1), F32)],
        compiler_params=_cparams(("parallel", "arbitrary")),
    )(q, k, kt, v, cc, cr, o, lse, dmix)


def _mem_fn(mq, mk, mv, gq, gk):
    qn = _rms_fn(mq, gq)
    kn = _rms_fn(mk, gk)
    s = _nt(qn, kn) * (HEAD_DIM ** -0.5)
    e = jnp.exp(s - lax.stop_gradient(jnp.max(s, axis=1, keepdims=True)))
    pr = e / jnp.sum(e, axis=1, keepdims=True)
    return _nn(pr, mv)


def _mem_specs(t, m, tq, qoff):
    qblk = pl.BlockSpec((tq, HEAD_DIM), lambda h, i: (i, qoff + h))
    kblk = pl.BlockSpec((m, HEAD_DIM), lambda h, i: (0, h))
    vblk = pl.BlockSpec((m, HEAD_DIM), lambda h, i: (0, N_MEM_HEADS + h))
    row = pl.BlockSpec((1, HEAD_DIM), lambda h, i: (0, 0))
    return qblk, kblk, vblk, row


def _mem_fwd(p, qoff, mkv, gq, gk, tq, into, into_off):
    t, m = p.shape[0], mkv.shape[0]
    qblk, kblk, vblk, row = _mem_specs(t, m, tq, qoff)

    def body(q_ref, k_ref, v_ref, gq_ref, gk_ref, _, o_ref):
        o_ref[...] = _mem_fn(q_ref[...], k_ref[...], v_ref[...], gq_ref[...], gk_ref[...]).astype(BF16)

    return pl.pallas_call(
        body, name="mem_fwd", grid=(N_MEM_HEADS, t // tq),
        in_specs=[qblk, kblk, vblk, row, row, pl.BlockSpec(memory_space=pl.ANY)],
        out_specs=pl.BlockSpec((tq, HEAD_DIM), lambda h, i: (i, into_off + h)),
        out_shape=jax.ShapeDtypeStruct(into.shape, BF16), input_output_aliases={5: 0},
        compiler_params=_cparams(("parallel", "parallel")),
    )(p, mkv, mkv, gq, gk, into)


def _mem_bwd(p, qoff, mkv, gq, gk, dmix, dooff, tq):
    t, m = p.shape[0], mkv.shape[0]
    qblk, kblk, vblk, row = _mem_specs(t, m, tq, qoff)

    def body(q_ref, k_ref, v_ref, gq_ref, gk_ref, do_ref, dq_ref, dkv_k_ref, dkv_v_ref, dgq_ref, dgk_ref):
        h, i = pl.program_id(0), pl.program_id(1)

        @pl.when((h == 0) & (i == 0))
        def _():
            dgq_ref[...] = jnp.zeros_like(dgq_ref)
            dgk_ref[...] = jnp.zeros_like(dgk_ref)

        @pl.when(i == 0)
        def _():
            dkv_k_ref[...] = jnp.zeros_like(dkv_k_ref)
            dkv_v_ref[...] = jnp.zeros_like(dkv_v_ref)

        _, vjp = jax.vjp(_mem_fn, q_ref[...], k_ref[...], v_ref[...], gq_ref[...], gk_ref[...])
        dq, dk, dv, dgq, dgk = vjp(do_ref[...])
        dq_ref[...] = dq
        dkv_k_ref[...] += dk
        dkv_v_ref[...] += dv
        dgq_ref[...] += dgq
        dgk_ref[...] += dgk

    oblk = pl.BlockSpec((tq, HEAD_DIM), lambda h, i: (i, h))
    kout = pl.BlockSpec((m, HEAD_DIM), lambda h, i: (0, h))
    half = jax.ShapeDtypeStruct((m, N_MEM_HEADS * HEAD_DIM), F32)
    rshape = jax.ShapeDtypeStruct((1, HEAD_DIM), F32)
    return pl.pallas_call(
        body, name="mem_bwd", grid=(N_MEM_HEADS, t // tq),
        in_specs=[qblk, kblk, vblk, row, row, pl.BlockSpec((tq, HEAD_DIM), lambda h, i: (i, dooff + h))],
        out_specs=[oblk, kout, kout, row, row],
        out_shape=[jax.ShapeDtypeStruct((t, N_MEM_HEADS * HEAD_DIM), F32), half, half, rshape, rshape],
        compiler_params=_cparams(("arbitrary", "arbitrary")),
    )(p, mkv, mkv, gq, gk, dmix)


def _shift_down(x, s):
    if s == 0:
        return x
    r = lax.broadcasted_iota(jnp.int32, x.shape, 0)
    return jnp.where(r >= s, pltpu.roll(x, s, 0), 0.0)


def _shift_up(x, s):
    if s == 0:
        return x
    n = x.shape[0]
    r = lax.broadcasted_iota(jnp.int32, x.shape, 0)
    return jnp.where(r < n - s, pltpu.roll(x, n - s, 0), 0.0)


def _conv_fn(x0, x1, x2, x3, w0, w1, w2, w3, kind):
    y = _silu(x0 * w0 + x1 * w1 + x2 * w2 + x3 * w3)
    if kind == 2:
        return y
    y = y * lax.rsqrt(jnp.sum(y * y, axis=-1, keepdims=True) + NORM_EPS)
    return y * (HEAD_DIM ** -0.5) if kind == 0 else y


def _conv_fwd(p, off, conv_w, ng):
    t = p.shape[0]

    def body(x_ref, w_ref, o_ref):
        kind = pl.program_id(0) // ng
        x = x_ref[...]
        xs = [_shift_down(x, CONV_WIDTH - 1 - j) for j in range(CONV_WIDTH)]
        ws = [w_ref[j:j + 1, :] for j in range(CONV_WIDTH)]
        for kd in range(3):
            @pl.when(kind == kd)
            def _(kd=kd):
                o_ref[...] = _conv_fn(*xs, *ws, kd)

    return pl.pallas_call(
        body, name="gdn_conv_fwd", grid=(3 * ng,),
        in_specs=[pl.BlockSpec((t, HEAD_DIM), lambda c: (0, off + c)),
                  pl.BlockSpec((CONV_WIDTH, HEAD_DIM), lambda c: (0, c))],
        out_specs=pl.BlockSpec((t, HEAD_DIM), lambda c: (0, c)),
        out_shape=jax.ShapeDtypeStruct((t, 3 * ng * HEAD_DIM), F32),
        compiler_params=_cparams(("parallel",)),
    )(p, conv_w)


def _conv_bwd(p, off, conv_w, dys, ng):
    t = p.shape[0]

    def body(x_ref, w_ref, dq_ref, dk_ref, dv_ref, dx_ref, dw_ref):
        kind = pl.program_id(0) // ng
        dy_refs = (dq_ref, dk_ref, dv_ref)
        x = x_ref[...]
        xs = [_shift_down(x, CONV_WIDTH - 1 - j) for j in range(CONV_WIDTH)]
        ws = [w_ref[j:j + 1, :] for j in range(CONV_WIDTH)]
        for kd in range(3):
            @pl.when(kind == kd)
            def _(kd=kd):
                _, vjp = jax.vjp(functools.partial(_conv_fn, kind=kd), *xs, *ws)
                g = vjp(dy_refs[kd][...])
                dx = _shift_up(g[0], CONV_WIDTH - 1)
                for j in range(1, CONV_WIDTH):
                    dx = dx + _shift_up(g[j], CONV_WIDTH - 1 - j)
                dx_ref[...] = dx
                for j in range(CONV_WIDTH):
                    dw_ref[j:j + 1, :] = g[CONV_WIDTH + j]

    blk = pl.BlockSpec((t, HEAD_DIM), lambda c: (0, c))
    head = pl.BlockSpec((t, HEAD_DIM), lambda c: (0, c % ng))
    wblk = pl.BlockSpec((CONV_WIDTH, HEAD_DIM), lambda c: (0, c))
    return pl.pallas_call(
        body, name="gdn_conv_bwd", grid=(3 * ng,),
        in_specs=[pl.BlockSpec((t, HEAD_DIM), lambda c: (0, off + c)), wblk] + [head] * 3,
        out_specs=[blk, wblk],
        out_shape=[jax.ShapeDtypeStruct((t, 3 * ng * HEAD_DIM), F32),
                   jax.ShapeDtypeStruct((CONV_WIDTH, 3 * ng * HEAD_DIM), F32)],
        compiler_params=_cparams(("parallel",)),
    )(p, conv_w, *dys)


def _wy_fn(q, k, v, gcol, grow, bcol):
    b, c, dk = q.shape
    r = lax.broadcasted_iota(jnp.int32, (1, c, c), 1)
    e = lax.broadcasted_iota(jnp.int32, (1, c, c), 2)
    tril, strict = e <= r, e < r
    gc_col = jnp.sum(jnp.where(tril, grow, 0.0), axis=2, keepdims=True)
    gc_row = jnp.sum(jnp.where(r <= e, gcol, 0.0), axis=1, keepdims=True)
    g_last = jnp.sum(gcol, axis=1, keepdims=True)
    decay = jnp.exp(jnp.where(tril, gc_col - gc_row, NEG))
    kb, vb = k * bcol, v * bcol
    lower = jnp.where(strict, _nt(kb, k) * decay, 0.0)
    inv = jnp.where(r == e, 1.0, 0.0) - lower
    pw = lower
    for _ in range(int(math.log2(c)) - 1):
        pw = _nn_x3(pw, pw)
        inv = inv + _nn_x3(inv, pw)
    u = _nn_x3(inv, vb)
    w = _nn_x3(inv, kb * jnp.exp(gc_col))
    attn = jnp.where(tril, _nt(q, k) * decay, 0.0)
    qg = q * jnp.exp(gc_col)
    kdec = k * jnp.exp(g_last - gc_col)
    egl = jnp.broadcast_to(jnp.exp(g_last), (b, 1, dk))
    return u, w, qg, kdec, attn, egl


def _scan_fn(u, w, qg, kdec, attn, egl, state):
    v_new = u - _nn(w, state)
    o = _nn(qg, state) + _nn(attn, v_new)
    return o, state * egl + _tn(kdec, v_new)


GDN_CHUNKS_PER_STEP = 4


def _gdn_fwd(qkv, gcol, grow, bcol, ng):
    t = qkv.shape[0]
    nch = t // CHUNK

    cb = GDN_CHUNKS_PER_STEP
    wy = _gdn_wy(qkv, gcol, grow, bcol, ng, cb)

    def body(u_ref, w_ref, qg_ref, kd_ref, at_ref, eg_ref, o_ref, st_ref, state):
        @pl.when(pl.program_id(0) == 0)
        def _():
            state[...] = jnp.zeros_like(state)

        st_ref[:, 0] = state[...]
        heads = lambda ref: jnp.stack([ref[:, h * HEAD_DIM:(h + 1) * HEAD_DIM] for h in range(ng)])
        o, new = _scan_fn(heads(u_ref), heads(w_ref), heads(qg_ref), heads(kd_ref), at_ref[:, 0], eg_ref[:, 0],
                          state[...])
        for h in range(ng):
            o_ref[:, h * HEAD_DIM:(h + 1) * HEAD_DIM] = o[h]
        state[...] = new

    w = ng * HEAD_DIM
    blk = pl.BlockSpec((CHUNK, w), lambda i: (i, 0))
    o, states = pl.pallas_call(
        body, name="gdn_scan_fwd", grid=(nch,),
        in_specs=[blk, blk, blk, blk, pl.BlockSpec((ng, 1, CHUNK, CHUNK), lambda i: (0, i, 0, 0)),
                  pl.BlockSpec((ng, 1, 1, HEAD_DIM), lambda i: (0, i, 0, 0))],
        out_specs=[blk, pl.BlockSpec((ng, 1, HEAD_DIM, HEAD_DIM), lambda i: (0, i, 0, 0))],
        out_shape=[jax.ShapeDtypeStruct((t, w), F32),
                   jax.ShapeDtypeStruct((ng, nch, HEAD_DIM, HEAD_DIM), F32)],
        scratch_shapes=[pltpu.VMEM((ng, HEAD_DIM, HEAD_DIM), F32)],
        compiler_params=_cparams(("arbitrary",)),
    )(*wy)
    return o, (wy, states)


def _wy_batch(q_ref, k_ref, v_ref, gc_ref, gr_ref, bc_ref, ng, cb):
    idx = [(c, h) for c in range(cb) for h in range(ng)]
    rows = lambda c: slice(c * CHUNK, (c + 1) * CHUNK)
    lanes = lambda h: slice(h * HEAD_DIM, (h + 1) * HEAD_DIM)
    wide = lambda ref: jnp.stack([ref[rows(c), lanes(h)] for c, h in idx])
    col = lambda ref: jnp.stack([ref[h, rows(c), :] for c, h in idx])
    return idx, (wide(q_ref), wide(k_ref), wide(v_ref), col(gc_ref), jnp.stack([gr_ref[h, c] for c, h in idx]),
                 col(bc_ref))


def _gdn_wy(qkv, gcol, grow, bcol, ng, cb):
    t = qkv.shape[0]
    nch = t // CHUNK

    def body(q_ref, k_ref, v_ref, gc_ref, gr_ref, bc_ref, u_ref, w_ref, qg_ref, kd_ref, at_ref, eg_ref):
        idx, args = _wy_batch(q_ref, k_ref, v_ref, gc_ref, gr_ref, bc_ref, ng, cb)
        u, w, qg, kd, at, eg = _wy_fn(*args)
        for b, (c, h) in enumerate(idx):
            rows, lanes = slice(c * CHUNK, (c + 1) * CHUNK), slice(h * HEAD_DIM, (h + 1) * HEAD_DIM)
            u_ref[rows, lanes] = u[b]
            w_ref[rows, lanes] = w[b]
            qg_ref[rows, lanes] = qg[b]
            kd_ref[rows, lanes] = kd[b]
            at_ref[h, c] = at[b]
            eg_ref[h, c] = eg[b]

    wd = ng * HEAD_DIM
    blk = lambda o: pl.BlockSpec((cb * CHUNK, wd), lambda i: (i, o))
    col = pl.BlockSpec((ng, cb * CHUNK, 1), lambda i: (0, i, 0))
    wide = jax.ShapeDtypeStruct((t, wd), F32)
    return pl.pallas_call(
        body, name="gdn_wy_fwd", grid=(nch // cb,),
        in_specs=[blk(0), blk(1), blk(2), col, pl.BlockSpec((ng, cb, 1, CHUNK), lambda i: (0, i, 0, 0)), col],
        out_specs=[blk(0), blk(0), blk(0), blk(0), pl.BlockSpec((ng, cb, CHUNK, CHUNK), lambda i: (0, i, 0, 0)),
                   pl.BlockSpec((ng, cb, 1, HEAD_DIM), lambda i: (0, i, 0, 0))],
        out_shape=[wide, wide, wide, wide, jax.ShapeDtypeStruct((ng, nch, CHUNK, CHUNK), F32),
                   jax.ShapeDtypeStruct((ng, nch, 1, HEAD_DIM), F32)],
        compiler_params=_cparams(("parallel",)),
    )(qkv, qkv, qkv, gcol, grow, bcol)


def _gdn_bwd(qkv, gcol, grow, bcol, saved, do, ng):
    t = qkv.shape[0]
    nch = t // CHUNK
    cb = GDN_CHUNKS_PER_STEP // 2
    wy, states = saved
    wd = ng * HEAD_DIM

    def scan_body(u_ref, w_ref, qg_ref, kd_ref, at_ref, eg_ref, st_ref, do_ref,
                  du_ref, dw_ref, dqg_ref, dkd_ref, dat_ref, deg_ref, dstate):
        @pl.when(pl.program_id(0) == 0)
        def _():
            dstate[...] = jnp.zeros_like(dstate)

        heads = lambda ref: jnp.stack([ref[:, h * HEAD_DIM:(h + 1) * HEAD_DIM] for h in range(ng)])
        _, vjp = jax.vjp(_scan_fn, heads(u_ref), heads(w_ref), heads(qg_ref), heads(kd_ref), at_ref[:, 0],
                         eg_ref[:, 0], st_ref[:, 0])
        du, dw, dqg, dkd, dat, deg, dst = vjp((heads(do_ref), dstate[...]))
        for h in range(ng):
            lanes = slice(h * HEAD_DIM, (h + 1) * HEAD_DIM)
            du_ref[:, lanes] = du[h]
            dw_ref[:, lanes] = dw[h]
            dqg_ref[:, lanes] = dqg[h]
            dkd_ref[:, lanes] = dkd[h]
        dat_ref[:, 0] = dat
        deg_ref[:, 0] = deg
        dstate[...] = dst

    rev = lambda i: nch - 1 - i
    blk = pl.BlockSpec((CHUNK, wd), lambda i: (rev(i), 0))
    atb = pl.BlockSpec((ng, 1, CHUNK, CHUNK), lambda i: (0, rev(i), 0, 0))
    egb = pl.BlockSpec((ng, 1, 1, HEAD_DIM), lambda i: (0, rev(i), 0, 0))
    wide = jax.ShapeDtypeStruct((t, wd), F32)
    at_shape = jax.ShapeDtypeStruct((ng, nch, CHUNK, CHUNK), F32)
    eg_shape = jax.ShapeDtypeStruct((ng, nch, 1, HEAD_DIM), F32)
    dwy = pl.pallas_call(
        scan_body, name="gdn_scan_bwd", grid=(nch,),
        in_specs=[blk, blk, blk, blk, atb, egb,
                  pl.BlockSpec((ng, 1, HEAD_DIM, HEAD_DIM), lambda i: (0, rev(i), 0, 0)), blk],
        out_specs=[blk, blk, blk, blk, atb, egb],
        out_shape=[wide, wide, wide, wide, at_shape, eg_shape],
        scratch_shapes=[pltpu.VMEM((ng, HEAD_DIM, HEAD_DIM), F32)],
        compiler_params=_cparams(("arbitrary",)),
    )(*wy, states, do)

    def wy_body(q_ref, k_ref, v_ref, gc_ref, gr_ref, bc_ref, du_ref, dw_ref, dqg_ref, dkd_ref, dat_ref, deg_ref,
                dq_ref, dk_ref, dv_ref, dgc_ref, dgr_ref, dbc_ref):
        idx, args = _wy_batch(q_ref, k_ref, v_ref, gc_ref, gr_ref, bc_ref, ng, cb)
        rows = lambda c: slice(c * CHUNK, (c + 1) * CHUNK)
        lanes = lambda h: slice(h * HEAD_DIM, (h + 1) * HEAD_DIM)
        wide_ct = lambda ref: jnp.stack([ref[rows(c), lanes(h)] for c, h in idx])
        cts = (wide_ct(du_ref), wide_ct(dw_ref), wide_ct(dqg_ref), wide_ct(dkd_ref),
               jnp.stack([dat_ref[h, c] for c, h in idx]), jnp.stack([deg_ref[h, c] for c, h in idx]))
        _, vjp = jax.vjp(_wy_fn, *args)
        dq, dk, dv, dgc, dgr, dbc = vjp(cts)
        for b, (c, h) in enumerate(idx):
            dq_ref[rows(c), lanes(h)] = dq[b]
            dk_ref[rows(c), lanes(h)] = dk[b]
            dv_ref[rows(c), lanes(h)] = dv[b]
            dgc_ref[h, rows(c), :] = dgc[b]
            dgr_ref[h, c] = dgr[b]
            dbc_ref[h, rows(c), :] = dbc[b]

    cblk = lambda o: pl.BlockSpec((cb * CHUNK, wd), lambda i: (i, o))
    col = pl.BlockSpec((ng, cb * CHUNK, 1), lambda i: (0, i, 0))
    rowv = pl.BlockSpec((ng, cb, 1, CHUNK), lambda i: (0, i, 0, 0))
    cshape = jax.ShapeDtypeStruct((ng, t, 1), F32)
    return pl.pallas_call(
        wy_body, name="gdn_wy_bwd", grid=(nch // cb,),
        in_specs=[cblk(0), cblk(1), cblk(2), col, rowv, col, cblk(0), cblk(0), cblk(0), cblk(0),
                  pl.BlockSpec((ng, cb, CHUNK, CHUNK), lambda i: (0, i, 0, 0)),
                  pl.BlockSpec((ng, cb, 1, HEAD_DIM), lambda i: (0, i, 0, 0))],
        out_specs=[cblk(0), cblk(0), cblk(0), col, rowv, col],
        out_shape=[wide, wide, wide, cshape, jax.ShapeDtypeStruct((ng, nch, 1, CHUNK), F32), cshape],
        compiler_params=_cparams(("parallel",)),
    )(qkv, qkv, qkv, gcol, grow, bcol, *dwy)


def _swiglu_fn(gate, up):
    return _silu(gate) * up


FFN_TN = 256


def _ffn_up(n2, wgu4):
    _, d, w = wgu4.shape
    t = n2.shape[0]
    tn = _tile(w, FFN_TN)
    nb = w // tn

    def body(a_ref, b_ref, gu_ref, act_ref):
        av = a_ref[...]
        gate = jnp.dot(av, b_ref[0], preferred_element_type=F32)
        up = jnp.dot(av, b_ref[1], preferred_element_type=F32)
        gu_ref[0] = gate.astype(BF16)
        gu_ref[1] = up.astype(BF16)
        act_ref[...] = _swiglu_fn(gate, up).astype(BF16)

    return pl.pallas_call(
        body, name="ffn_up", grid=(2, nb),
        in_specs=[pl.BlockSpec((t, d), lambda j, l: (0, 0)), pl.BlockSpec((2, d, tn), lambda j, l: (j, 0, l))],
        out_specs=[pl.BlockSpec((2, t, tn), lambda j, l: (j, 0, l)),
                   pl.BlockSpec((t, tn), lambda j, l: (0, j * nb + l))],
        out_shape=[jax.ShapeDtypeStruct((4, t, w), BF16), jax.ShapeDtypeStruct((t, 2 * w), BF16)],
        compiler_params=_cparams(("parallel", "parallel")),
    )(n2, wgu4)


def _ffn_dact(dh2, wd, gu, after):
    _, t, w = gu.shape
    d = dh2.shape[1]
    tn = _tile(w, FFN_TN)
    nb = w // tn

    def body(a_ref, b_ref, gu_ref, _, o_ref):
        dact = lax.dot_general(a_ref[...], b_ref[...], (((1,), (1,)), ((), ())), preferred_element_type=F32)
        _, vjp = jax.vjp(_swiglu_fn, gu_ref[0].astype(F32), gu_ref[1].astype(F32))
        dg, du = vjp(dact)
        o_ref[0] = dg.astype(BF16)
        o_ref[1] = du.astype(BF16)

    pair = pl.BlockSpec((2, t, tn), lambda j, l: (j, 0, l))
    return pl.pallas_call(
        body, name="ffn_dact", grid=(2, nb),
        in_specs=[pl.BlockSpec((t, d), lambda j, l: (0, 0)), pl.BlockSpec((tn, d), lambda j, l: (j * nb + l, 0)),
                  pair, pl.BlockSpec(after.shape, lambda j, l: (0, 0))],
        out_specs=pair, out_shape=jax.ShapeDtypeStruct(gu.shape, BF16),
        compiler_params=_cparams(("parallel", "parallel")),
    )(dh2, wd, gu, after)


def _loss_head(h2, target):
    t, d = h2.shape
    tr = _tile(t, 256, 8)

    def body(h_ref, t_ref, l_ref, d_ref, db_ref):
        @pl.when(pl.program_id(0) == 0)
        def _():
            l_ref[...] = jnp.zeros_like(l_ref)

        err = h_ref[...] - t_ref[...]
        d_ref[...] = err * (1.0 / d)
        db_ref[...] = (err * (1.0 / d)).astype(BF16)
        part = 0.5 * jnp.sum(jnp.mean(err * err, axis=-1, keepdims=True), axis=0, keepdims=True)
        lane = lax.broadcasted_iota(jnp.int32, (8, HEAD_DIM), 1)
        row = lax.broadcasted_iota(jnp.int32, (8, HEAD_DIM), 0)
        l_ref[...] += jnp.where((lane == 0) & (row == 0), part, 0.0)

    blk = pl.BlockSpec((tr, d), lambda r: (r, 0))
    return pl.pallas_call(
        body, name="loss_head", grid=(t // tr,), in_specs=[blk, blk],
        out_specs=[pl.BlockSpec((8, HEAD_DIM), lambda r: (0, 0)), blk, blk],
        out_shape=[jax.ShapeDtypeStruct((8, HEAD_DIM), F32), jax.ShapeDtypeStruct((t, d), F32),
                   jax.ShapeDtypeStruct((t, d), BF16)],
        compiler_params=_cparams(("arbitrary",)),
    )(h2, target)


def _adamw(w, g, m, v, *, g_fn=None, name):
    r, c = w.shape
    tr = _tile(r, max(8, (1 << 19) // c // 8 * 8), 8)

    def body(w_ref, g_ref, m_ref, v_ref, go_ref, d_ref, mo_ref, vo_ref):
        gr = g_ref[...] if g_fn is None else g_fn(g_ref[...])
        mn = ADAM_B1 * m_ref[...] + (1.0 - ADAM_B1) * gr
        vn = ADAM_B2 * v_ref[...] + (1.0 - ADAM_B2) * (gr * gr)
        m_hat = mn / (1.0 - ADAM_B1 ** ADAM_STEP)
        v_hat = vn / (1.0 - ADAM_B2 ** ADAM_STEP)
        go_ref[...] = gr
        d_ref[...] = -ADAM_LR * (m_hat / (jnp.sqrt(v_hat) + ADAM_EPS) + ADAM_WD * w_ref[...])
        mo_ref[...] = mn
        vo_ref[...] = vn

    blk = pl.BlockSpec((tr, c), lambda i: (i, 0))
    gblk = pl.BlockSpec((tr, g.shape[1]), lambda i: (i, 0))
    return pl.pallas_call(
        body, name=name, grid=(r // tr,), in_specs=[blk, gblk, blk, blk], out_specs=[blk] * 4,
        out_shape=[jax.ShapeDtypeStruct((r, c), F32)] * 4,
        compiler_params=_cparams(("parallel",)),
    )(w, g, m, v)


class _Layout:
    def __init__(self, d):
        nh = d // HEAD_DIM
        self.nm = N_MEM_HEADS
        self.nf = (nh - self.nm) // 2
        self.ng = nh - self.nm - self.nf
        nf, ng, nm = self.nf, self.ng, self.nm
        self.o_fq, self.o_fk, self.o_fv = 0, nf, 2 * nf
        self.o_gq = 3 * nf
        self.o_gz = 3 * nf + 3 * ng
        self.o_mq = 3 * nf + 4 * ng
        self.o_sm = self.o_mq + nm
        self.blocks = -(-(self.o_sm + 1) // 8) * 8
        self.cols = self.blocks * HEAD_DIM
        hd = HEAD_DIM
        sizes = [nf * hd, nf * hd, nf * hd, nf, 3 * ng * hd, ng * hd, ng, ng, nm * hd]
        starts = [sum(sizes[:i]) for i in range(len(sizes))]
        self.ref = list(zip(starts, sizes))
        self.in_cols = sum(sizes)

    def regroup(self, w):
        part = lambda i: w[:, self.ref[i][0]:self.ref[i][0] + self.ref[i][1]]
        pieces = [part(0), part(1), part(2), part(4), part(5), part(8), part(3), part(6), part(7)]
        pad = self.cols - self.in_cols
        return jnp.concatenate(pieces + [jnp.zeros((w.shape[0], pad), w.dtype)], axis=1)

    def ungroup(self, g):
        hd, nf, ng, nm = HEAD_DIM, self.nf, self.ng, self.nm
        sm = self.o_sm * hd
        return jnp.concatenate([
            g[:, :3 * nf * hd], g[:, sm:sm + nf], g[:, self.o_gq * hd:self.o_gz * hd],
            g[:, self.o_gz * hd:self.o_mq * hd], g[:, sm + nf:sm + nf + ng], g[:, sm + nf + ng:sm + nf + 2 * ng],
            g[:, self.o_mq * hd:self.o_sm * hd]], axis=1)


def _lane_row(pieces):
    row = jnp.zeros((1, HEAD_DIM), F32)
    for off, a in pieces:
        row = lax.dynamic_update_slice(row, a.astype(F32), (0, off))
    return row


def _local_step(x, mem, target, prefetch, weights, reducer, sp):
    t, d = x.shape
    lay = _Layout(d)
    nf, ng, nm, hd = lay.nf, lay.ng, lay.nm, HEAD_DIM
    nch = t // CHUNK
    tq = _tile(t, 256)
    tk = tq

    u = _norm_fwd(x, 0, sp["norm_mix"], 1, d, BF16, name="norm_mix_fwd")
    prefetch("in", u)
    (win,) = weights("in", u)
    prefetch("mixer", win)
    p = _mm(u, win, name="mm_in")
    wmkv, conv_taps = weights("mixer", p)
    sp = dict(sp, gdn_conv=conv_taps)
    pa = _lane_row([(nf, sp["gdn_a_log"])])
    pb = _lane_row([(0, sp["fox_f_bias"]), (nf, sp["gdn_dt_bias"])])
    vals, csum = _small_fwd(p, lay.o_sm, pa, pb, nf, ng)

    c_t = csum[:, :nf].T
    cc, cr = c_t.reshape(nf, t, 1), c_t.reshape(nf, t // tk, 1, tk)
    fq = _norm_fwd(p, lay.o_fq, sp["fox_q_norm"], nf, hd, BF16, name="fox_qnorm_fwd")
    fk = _norm_fwd(p, lay.o_fk, sp["fox_k_norm"], nf, hd, BF16, name="fox_knorm_fwd")
    fv = p[:, lay.o_fv * hd:(lay.o_fv + nf) * hd].astype(BF16)
    o_fox, lse, mix = _fox_fwd(fq, fk, fv, cc, cr, nf, tq, tk, d)
    prefetch("out", lse)

    qkv = _conv_fwd(p, lay.o_gq, sp["gdn_conv"], ng)
    g_t, b_t = vals[:, nf:nf + ng].T, vals[:, nf + ng:nf + 2 * ng].T
    gcol, grow, bcol = g_t.reshape(ng, t, 1), g_t.reshape(ng, nch, 1, CHUNK), b_t.reshape(ng, t, 1)
    o_g, states = _gdn_fwd(qkv, gcol, grow, bcol, ng)
    mix = _norm_fwd(o_g, 0, sp["gdn_out_norm"], ng, hd, BF16, z=p, zoff=lay.o_gz, into=mix, into_off=nf,
                    name="gdn_out_fwd")

    mem_n = _norm_fwd(mem, 0, sp["mem_norm"], 1, d, BF16, name="mem_norm_fwd")
    mkv = _mm(mem_n, wmkv, name="mm_memkv")
    mix = _mem_fwd(p, lay.o_mq, mkv, sp["mem_q_norm"], sp["mem_k_norm"], tq, mix, nf + ng)
    prefetch("gate_up", mix)
    (wout,) = weights("out", mix)
    h1 = _mm(mix, wout, res=x, name="mm_out")
    n2 = _norm_fwd(h1, 0, sp["norm_ffn"], 1, d, BF16, name="norm_ffn_fwd")
    prefetch("down", n2)
    (wgu,) = weights("gate_up", n2)
    wgu4 = wgu.reshape(4, d, -1)
    gu, act = _ffn_up(n2, wgu4)
    (wd,) = weights("down", act)
    h2 = _mm(act, wd, res=h1, name="mm_down")
    loss_blk, dh2, dh2_b = _loss_head(h2, target)

    g = {}
    token = reducer.pair("w_down", _mm(act, dh2_b, ta=True, out_dtype=BF16, name="mm_dw_down"))
    dgu = _ffn_dact(dh2_b, wd, gu, token)
    dw_gate_up = _mm(n2, dgu, ta=True, stack="out", out_dtype=BF16, name="mm_dw_gate_up").reshape(wgu.shape)
    token = reducer.pair("w_gate_up", dw_gate_up)
    dn2 = _mm(dgu, wgu4, tb=True, stack="sum", after=token, name="mm_dn2")
    token = reducer.ship("ffn", ["w_down", "w_gate_up"], dn2)
    dh1, g["norm_ffn"] = _norm_bwd(h1, 0, sp["norm_ffn"] + token[0, 0], dn2, 0, 1, d, res=dh2,
                                   name="norm_ffn_bwd")
    token = reducer.pair("w_out", _mm(mix, dh1, ta=True, out_dtype=BF16, name="mm_dw_out"))
    dmix = _mm(dh1, wout, tb=True, after=token, name="mm_dmix")

    dmq, dmk, dmv, g["mem_q_norm"], g["mem_k_norm"] = _mem_bwd(
        p, lay.o_mq, mkv, sp["mem_q_norm"], sp["mem_k_norm"], dmix, nf + ng, tq)
    dmkv = jnp.concatenate([dmk, dmv], axis=1)
    token = reducer.pair("w_mem_kv", _mm(mem_n, dmkv, ta=True, out_dtype=BF16, name="mm_dw_memkv"))
    dmem_n = _mm(dmkv, wmkv, tb=True, after=token, name="mm_dmem")
    token = reducer.ship("mix", ["w_out", "w_mem_kv"], dmem_n)
    _, g["mem_norm"] = _norm_bwd(mem, 0, sp["mem_norm"], dmem_n, 0, 1, d, name="mem_norm_bwd")

    do_g, dgz, g["gdn_out_norm"] = _norm_bwd(o_g, 0, sp["gdn_out_norm"] + token[0, 0], dmix, nf, ng, hd, z=p,
                                             zoff=lay.o_gz, name="gdn_out_bwd")
    dq, dk, dv, dgc, dgr, dbc = _gdn_bwd(qkv, gcol, grow, bcol, states, do_g, ng)
    dgqkv, g["gdn_conv"] = _conv_bwd(p, lay.o_gq, sp["gdn_conv"], (dq, dk, dv), ng)
    dg_t = dgc.reshape(ng, t) + dgr.reshape(ng, t)
    db_t = dbc.reshape(ng, t)

    dfq_n, dfk_n, dfv, dcc, dcr = _fox_bwd(fq, fk, fv, cc, cr, o_fox, lse, dmix, nf, tq, tk)
    dfq, g["fox_q_norm"] = _norm_bwd(p, lay.o_fq, sp["fox_q_norm"], dfq_n, 0, nf, hd, name="fox_qnorm_bwd")
    dfk, g["fox_k_norm"] = _norm_bwd(p, lay.o_fk, sp["fox_k_norm"], dfk_n, 0, nf, hd, name="fox_knorm_bwd")
    dc_t = dcc.reshape(nf, t) + dcr.reshape(nf, t)

    lanes_left = hd - nf - 2 * ng
    dvals = jnp.concatenate([jnp.zeros((t, nf), F32), dg_t.T, db_t.T, jnp.zeros((t, lanes_left), F32)], axis=1)
    dcsum = jnp.concatenate([dc_t.T, jnp.zeros((t, hd - nf), F32)], axis=1)
    dsm, dpa, dpb = _small_bwd(p, lay.o_sm, pa, pb, dvals, dcsum, nf, ng)
    g["fox_f_bias"] = dpb[:, :nf]
    g["gdn_dt_bias"] = dpb[:, nf:nf + ng]
    g["gdn_a_log"] = dpa[:, nf:nf + ng]

    pad = jnp.zeros((t, lay.cols - (lay.o_sm + 1) * hd), F32)
    dp = jnp.concatenate([dfq, dfk, dfv, dgqkv, dgz, dmq, dsm, pad], axis=1).astype(BF16)
    token = reducer.start("in", {"w_in": _mm(u, dp, ta=True, out_dtype=BF16, name="mm_dw_in")})
    du = _mm(dp, win, tb=True, after=token, name="mm_du")
    dx, g["norm_mix"] = _norm_bwd(x, 0, sp["norm_mix"], du, 0, 1, d, res=dh1, name="norm_mix_bwd")
    return loss_blk, dx, g


ANY = pl.BlockSpec(memory_space=pl.ANY)


def _me():
    x, y, c = lax.axis_index("x"), lax.axis_index("y"), lax.axis_index("c")
    chips = [(1 - x, y), (x, 1 - y), (1 - x, 1 - y)]
    return x, y, c, chips


def _slot(axis, k):
    return k if axis == 0 else 2 * (k % 2) + k // 2


def _slab(ref, axis, rows, cols, k, h):
    half = rows // 2
    return ref.at[pl.ds(_slot(axis, k) * rows + h * half, half), :]


def _remote(src, dst, send_sem, recv_sem, dev):
    return pltpu.make_async_remote_copy(src_ref=src, dst_ref=dst, send_sem=send_sem, recv_sem=recv_sem,
                                        device_id=dev, device_id_type=MESH)


HBM = pl.BlockSpec(memory_space=pltpu.HBM)
SEM = pl.BlockSpec(memory_space=pltpu.SEMAPHORE)
SPLIT = pltpu.CompilerParams(has_side_effects=pltpu.SideEffectType.DATAFLOW_SIDE_EFFECTING)
TOKEN = jax.ShapeDtypeStruct((8, HEAD_DIM), F32)


def _in_hbm(v):
    return pltpu.with_memory_space_constraint(v, pltpu.HBM)


def _cast_place(shard, axis, name, col_fn=None, out_cols=None):
    r, c = shard.shape
    oc = out_cols or c
    tr = _tile(r, 512 if col_fn is None else 64, 16)
    tc = _tile(c, 2048) if col_fn is None else c
    otc = tc if col_fn is None else oc
    nb = r // tr
    chip = 2 * lax.axis_index("x") + lax.axis_index("y")
    slot = jnp.reshape(_slot(axis, chip), (1,)).astype(jnp.int32)

    def body(slot_ref, x_ref, o_ref):
        x = x_ref[...]
        o_ref[...] = (x if col_fn is None else col_fn(x)).astype(BF16)

    return pl.pallas_call(
        body, name=name,
        grid_spec=pltpu.PrefetchScalarGridSpec(
            num_scalar_prefetch=1, grid=(nb, c // tc),
            in_specs=[pl.BlockSpec((tr, tc), lambda i, l, s: (i, l))],
            out_specs=pl.BlockSpec((tr, otc), lambda i, l, s: (s[0] * nb + i, l))),
        out_shape=jax.ShapeDtypeStruct((4 * r, oc), BF16),
        compiler_params=_cparams(("parallel", "parallel")),
    )(slot, shard)


def _gather_start(bufs, axes, shapes, groups, name):
    n = len(bufs)

    def body(*refs):
        dst = refs[n:2 * n]
        sems = refs[2 * n:2 * n + 2 * len(groups)]
        token = refs[-1]
        x, y, c, chips = _me()
        k = 2 * x + y
        for gi, ws in enumerate(groups):
            for i, w in enumerate(ws):
                r, cl = shapes[w]
                place = _slab(dst[w], axes[w], r, cl, k, c)
                for j, (px, py) in enumerate(chips):
                    _remote(place, place, sems[2 * gi].at[3 * i + j], sems[2 * gi + 1].at[3 * i + j],
                            (px, py, c)).start()
        token[...] = jnp.zeros_like(token)

    sem_shapes = [pltpu.SemaphoreType.DMA((3 * len(ws),)) for ws in groups for _ in range(2)]
    outs = pl.pallas_call(
        body, name=name, in_specs=[HBM] * n,
        out_specs=[HBM] * n + [SEM] * len(sem_shapes) + [pl.BlockSpec(memory_space=pltpu.VMEM)],
        out_shape=[pltpu.HBM(b.shape, b.dtype) for b in bufs] + sem_shapes + [TOKEN],
        input_output_aliases={w: w for w in range(n)}, compiler_params=SPLIT,
    )(*[_in_hbm(b) for b in bufs])
    sems = outs[n:-1]
    return outs[:n], [(sems[2 * g], sems[2 * g + 1]) for g in range(len(groups))], outs[-1]


def _gather_wait(bufs, axes, shapes, sems, after, name):
    n = len(bufs)

    def body(*refs):
        send_sems, recv_sems = refs[n], refs[n + 1]
        dst = refs[n + 3:]
        x, y, c, chips = _me()
        k = 2 * x + y
        for i in range(n):
            r, cl = shapes[i]
            for j, (px, py) in enumerate(chips):
                got = _slab(dst[i], axes[i], r, cl, 2 * px + py, c)
                _remote(got, got, send_sems.at[3 * i + j], recv_sems.at[3 * i + j], (px, py, c)).wait_recv()
        for i in range(n):
            r, cl = shapes[i]
            mine = _slab(dst[i], axes[i], r, cl, k, c)
            for j, (px, py) in enumerate(chips):
                _remote(mine, mine, send_sems.at[3 * i + j], recv_sems.at[3 * i + j], (px, py, c)).wait_send()

    return pl.pallas_call(
        body, name=name, in_specs=[HBM] * n + [SEM, SEM, ANY], out_specs=[HBM] * n,
        out_shape=[pltpu.HBM(b.shape, b.dtype) for b in bufs],
        input_output_aliases={i: i for i in range(n)}, compiler_params=SPLIT,
    )(*bufs, sems[0], sems[1], after)


def _gather_forward(bufs, axes, shapes, name):
    n = len(bufs)

    def body(*refs):
        dst = refs[n:2 * n]
        send_sems, recv_sems = refs[2 * n:]
        x, y, c, chips = _me()
        sibling = (x, y, 1 - c)
        sends = []
        for i in range(n):
            r, cl = shapes[i]
            for j, (px, py) in enumerate(chips):
                got = _slab(dst[i], axes[i], r, cl, 2 * px + py, c)
                cp = _remote(got, got, send_sems.at[3 * i + j], recv_sems.at[3 * i + j], sibling)
                cp.start()
                sends.append(cp)
        for i in range(n):
            r, cl = shapes[i]
            for j, (px, py) in enumerate(chips):
                got = _slab(dst[i], axes[i], r, cl, 2 * px + py, 1 - c)
                _remote(got, got, send_sems.at[3 * i + j], recv_sems.at[3 * i + j], sibling).wait_recv()
        for cp in sends:
            cp.wait_send()

    return pl.pallas_call(
        body, name=name, in_specs=[ANY] * n, out_specs=[ANY] * n,
        out_shape=[jax.ShapeDtypeStruct(b.shape, b.dtype) for b in bufs],
        input_output_aliases={i: i for i in range(n)},
        scratch_shapes=[pltpu.SemaphoreType.DMA((3 * n,)), pltpu.SemaphoreType.DMA((3 * n,))],
    )(*bufs)


def _split_start(name, arrays, geometry, count):
    n = len(arrays)

    def body(*refs):
        send, recv, token = refs[2 * n:]
        for i, (src, dst, _, dev) in enumerate(geometry(refs[n:2 * n])):
            _remote(src, dst, send.at[i], recv.at[i], dev).start()
        token[...] = jnp.zeros_like(token)

    sem = pltpu.SemaphoreType.DMA((count,))
    outs = pl.pallas_call(
        body, name=name, in_specs=[HBM] * n,
        out_specs=[HBM] * n + [SEM, SEM, pl.BlockSpec(memory_space=pltpu.VMEM)],
        out_shape=[pltpu.HBM(v.shape, v.dtype) for v in arrays] + [sem, sem, TOKEN],
        input_output_aliases={i: i for i in range(n)}, compiler_params=SPLIT,
    )(*[_in_hbm(v) for v in arrays])
    return list(outs[:n]), (outs[n], outs[n + 1]), outs[-1]


def _split_wait(name, arrays, sems, after, geometry):
    n = len(arrays)

    def body(*refs):
        send, recv = refs[n], refs[n + 1]
        copies = geometry(refs[n + 3:])
        for i, (_, _, land, dev) in enumerate(copies):
            _remote(land, land, send.at[i], recv.at[i], dev).wait_recv()
        for i, (src, _, _, dev) in enumerate(copies):
            _remote(src, src, send.at[i], recv.at[i], dev).wait_send()

    return list(pl.pallas_call(
        body, name=name, in_specs=[HBM] * n + [SEM, SEM, ANY], out_specs=[HBM] * n,
        out_shape=[pltpu.HBM(v.shape, v.dtype) for v in arrays],
        input_output_aliases={i: i for i in range(n)}, compiler_params=SPLIT,
    )(*arrays, sems[0], sems[1], after))


def _forward_geometry(axes, shapes):
    def geometry(bufs):
        x, y, c, chips = _me()
        out = []
        for i, buf in enumerate(bufs):
            r, cl = shapes[i]
            for px, py in chips:
                got = _slab(buf, axes[i], r, cl, 2 * px + py, c)
                out.append((got, got, _slab(buf, axes[i], r, cl, 2 * px + py, 1 - c), (x, y, 1 - c)))
        return out
    return geometry


def _pair_geometry(axes, shapes):
    def geometry(refs):
        n = len(refs) // 2
        x, y, c, _ = _me()
        out = []
        for w in range(n):
            r, cl = shapes[w]
            for j in range(4):
                land = refs[n + w].at[j]
                out.append((_slab(refs[w], axes[w], r, cl, j, 1 - c), land, land, (x, y, 1 - c)))
        return out
    return geometry


def _pair_exchange(fulls, axes, shapes, tag):
    n = len(fulls)

    def body(*refs):
        src, dst = refs[:n], refs[n:2 * n]
        send_sems, recv_sems = refs[2 * n:]
        x, y, c, _ = _me()
        sibling = (x, y, 1 - c)
        cps = []
        for w in range(n):
            r, cl = shapes[w]
            for j in range(4):
                cp = _remote(_slab(src[w], axes[w], r, cl, j, 1 - c), dst[w].at[j],
                             send_sems.at[4 * w + j], recv_sems.at[4 * w + j], sibling)
                cp.start()
                cps.append(cp)
        for cp in cps:
            cp.wait()

    out_shape = [jax.ShapeDtypeStruct((4, r // 2, cl), f.dtype) for (r, cl), f in zip(shapes, fulls)]
    return pl.pallas_call(
        body, name="reduce_pair_exchange_" + tag, in_specs=[ANY] * n, out_specs=[ANY] * n, out_shape=out_shape,
        scratch_shapes=[pltpu.SemaphoreType.DMA((4 * n,)), pltpu.SemaphoreType.DMA((4 * n,))],
    )(*fulls)


def _chip_start(parts, tag):
    n = len(parts)

    def body(*refs):
        src, land = refs[2 * n:3 * n], refs[3 * n:4 * n]
        send_sems, recv_sems, token = refs[4 * n:]
        x, y, c, chips = _me()
        k = 2 * x + y
        for w in range(n):
            for j, (px, py) in enumerate(chips):
                _remote(src[w].at[2 * px + py], land[w].at[k], send_sems.at[3 * w + j], recv_sems.at[3 * w + j],
                        (px, py, c)).start()
        token[...] = jnp.zeros_like(token)

    lands = [lax.empty(p.shape, p.dtype) for p in parts]
    sem = pltpu.SemaphoreType.DMA((3 * n,))
    outs = pl.pallas_call(
        body, name="reduce_ici_start_" + tag, in_specs=[HBM] * (2 * n),
        out_specs=[HBM] * (2 * n) + [SEM, SEM, pl.BlockSpec(memory_space=pltpu.VMEM)],
        out_shape=[pltpu.HBM(p.shape, p.dtype) for p in parts + lands] + [sem, sem, TOKEN],
        input_output_aliases={i: i for i in range(2 * n)}, compiler_params=SPLIT,
    )(*[_in_hbm(v) for v in parts + lands])
    return outs[:n], outs[n:2 * n], outs[2 * n], outs[2 * n + 1], outs[-1]


def _chip_wait(parts, lands, send_sems, recv_sems, after, tag):
    n = len(parts)

    def body(*refs):
        send, recv = refs[2 * n], refs[2 * n + 1]
        src, land = refs[2 * n + 3:3 * n + 3], refs[3 * n + 3:]
        x, y, c, chips = _me()
        for w in range(n):
            for j, (px, py) in enumerate(chips):
                got = land[w].at[2 * px + py]
                _remote(got, got, send.at[3 * w + j], recv.at[3 * w + j], (px, py, c)).wait_recv()
        for w in range(n):
            for j, (px, py) in enumerate(chips):
                sent = src[w].at[2 * px + py]
                _remote(sent, sent, send.at[3 * w + j], recv.at[3 * w + j], (px, py, c)).wait_send()

    outs = pl.pallas_call(
        body, name="reduce_ici_wait_" + tag, in_specs=[HBM] * (2 * n) + [SEM, SEM, ANY], out_specs=[HBM] * (2 * n),
        out_shape=[pltpu.HBM(p.shape, p.dtype) for p in parts + lands],
        input_output_aliases={i: i for i in range(2 * n)}, compiler_params=SPLIT,
    )(*parts, *lands, send_sems, recv_sems, after)
    chip = 2 * lax.axis_index("x") + lax.axis_index("y")
    return [lax.dynamic_update_slice(s, lax.dynamic_index_in_dim(p, chip, 0, keepdims=True), (chip, 0, 0))
            for p, s in zip(outs[:n], outs[n:])]


def _half_swap(halves, tag):
    n = len(halves)
    core = lax.axis_index("c")
    bufs = [lax.dynamic_update_slice(lax.empty((2,) + h.shape, h.dtype), h[None], (core, 0, 0)) for h in halves]

    def body(*refs):
        dst = refs[n:2 * n]
        send_sems, recv_sems = refs[2 * n:]
        x, y, c, _ = _me()
        sibling = (x, y, 1 - c)
        cps = []
        for w in range(n):
            cp = _remote(dst[w].at[c], dst[w].at[c], send_sems.at[w], recv_sems.at[w], sibling)
            cp.start()
            cps.append(cp)
        for w in range(n):
            other = dst[w].at[1 - c]
            _remote(other, other, send_sems.at[w], recv_sems.at[w], sibling).wait_recv()
        for cp in cps:
            cp.wait_send()

    outs = pl.pallas_call(
        body, name="reduce_half_swap_" + tag, in_specs=[ANY] * n, out_specs=[ANY] * n,
        out_shape=[jax.ShapeDtypeStruct(b.shape, b.dtype) for b in bufs],
        input_output_aliases={w: w for w in range(n)},
        scratch_shapes=[pltpu.SemaphoreType.DMA((n,)), pltpu.SemaphoreType.DMA((n,))],
    )(*bufs)
    return [o.reshape(2 * o.shape[1], o.shape[2]) for o in outs]


def _add_parts(full, axis, rows, sib, name):
    _, r, c = sib.shape
    tr, tc = _tile(r, 256, 16), _tile(c, 2048)
    nb = r // tr
    core = jnp.reshape(lax.axis_index("c"), (1,)).astype(jnp.int32)

    def body(c_ref, a_ref, b_ref, o_ref):
        o_ref[0] = (a_ref[...].astype(F32) + b_ref[0].astype(F32)).astype(BF16)

    blk = pl.BlockSpec((1, tr, tc), lambda j, i, l, cr: (j, i, l))
    return pl.pallas_call(
        body, name=name,
        grid_spec=pltpu.PrefetchScalarGridSpec(
            num_scalar_prefetch=1, grid=(4, nb, c // tc),
            in_specs=[pl.BlockSpec((tr, tc), lambda j, i, l, cr: ((_slot(axis, j) * 2 + cr[0]) * nb + i, l)), blk],
            out_specs=blk),
        out_shape=jax.ShapeDtypeStruct(sib.shape, BF16),
        compiler_params=_cparams(("parallel", "parallel", "parallel")),
    )(core, full, sib)


def _sum_slots(a, name):
    _, r, c = a.shape
    tr, tc = _tile(r, 256, 8), _tile(c, 2048)

    def body(a_ref, o_ref):
        v = a_ref[...].astype(F32)
        o_ref[...] = ((v[0] + v[1]) + v[2]) + v[3]

    return pl.pallas_call(
        body, name=name, grid=(r // tr, c // tc),
        in_specs=[pl.BlockSpec((4, tr, tc), lambda i, l: (0, i, l))],
        out_specs=pl.BlockSpec((tr, tc), lambda i, l: (i, l)),
        out_shape=jax.ShapeDtypeStruct((r, c), F32),
        compiler_params=_cparams(("parallel", "parallel")),
    )(a)


class _Reducer:
    def __init__(self, spec):
        self.spec = spec
        self.paired = {}
        self.pending = []

    def pair(self, name, full):
        ax, shp = self.spec[name]
        land = lax.empty((4, shp[0] // 2, shp[1]), full.dtype)
        arrays, sems, token = _split_start("reduce_pair_start_" + name, [full, land], _pair_geometry([ax], [shp]), 4)
        self.paired[name] = (arrays, sems)
        return token

    def ship(self, tag, names, after):
        parts = []
        for n in names:
            ax, shp = self.spec[n]
            arrays, sems = self.paired.pop(n)
            full, sib = _split_wait("reduce_pair_wait_" + n, arrays, sems, after, _pair_geometry([ax], [shp]))
            parts.append(_add_parts(full, ax, shp[0], sib, name=f"reduce_add_{n}"))
        parts, lands, send, recv, token = _chip_start(parts, tag)
        self.pending.append((tag, names, parts, lands, send, recv))
        return token

    def start(self, tag, grads):
        names = list(grads)
        fulls, axes = [grads[n] for n in names], [self.spec[n][0] for n in names]
        shapes = [self.spec[n][1] for n in names]
        from_sibling = _pair_exchange(fulls, axes, shapes, tag)
        parts = [_add_parts(f, a, r, s, name=f"reduce_add_{n}")
                 for n, f, a, (r, cl), s in zip(names, fulls, axes, shapes, from_sibling)]
        parts, lands, send, recv, token = _chip_start(parts, tag)
        self.pending.append((tag, names, parts, lands, send, recv))
        return token

    def finish(self, after):
        out = {}
        for tag, names, parts, lands, send, recv in self.pending:
            slots = _chip_wait(parts, lands, send, recv, after, tag)
            halves = [_sum_slots(s, name=f"reduce_sum_{n}") for n, s in zip(names, slots)]
            out.update(zip(names, _half_swap(halves, tag)))
        return out


def _allreduce_small(pack):
    rows = pack.shape[0]

    def body(p_ref, o_ref, slots, send_sems, recv_sems):
        x, y, c, _ = _me()
        me = 4 * x + 2 * y + c
        slots[me] = p_ref[...]
        cps = []
        for r in range(1, 8):
            peer = (x ^ (r >> 2), y ^ ((r >> 1) & 1), c ^ (r & 1))
            cp = _remote(p_ref, slots.at[me], send_sems.at[r - 1], recv_sems.at[r - 1], peer)
            cp.start()
            cps.append(cp)
        for r in range(1, 8):
            frm = me ^ r
            _remote(slots.at[frm], slots.at[frm], send_sems.at[r - 1], recv_sems.at[r - 1], (x, y, c)).wait_recv()
        for cp in cps:
            cp.wait_send()
        acc = slots[0]
        for s in range(1, 8):
            acc = acc + slots[s]
        o_ref[...] = acc

    vm = pl.BlockSpec(memory_space=pltpu.VMEM)
    return pl.pallas_call(
        body, name="allreduce_small", in_specs=[vm], out_specs=vm,
        out_shape=jax.ShapeDtypeStruct(pack.shape, F32),
        scratch_shapes=[pltpu.VMEM((8, rows, HEAD_DIM), F32), pltpu.SemaphoreType.DMA((7,)),
                        pltpu.SemaphoreType.DMA((7,))],
    )(pack)


_ROWS = ["norm_mix", "norm_ffn", "mem_norm", "fox_q_norm", "fox_k_norm", "gdn_out_norm", "mem_q_norm",
         "mem_k_norm", "fox_f_bias", "gdn_a_log", "gdn_dt_bias"]


def _pack_rows(vals):
    out = []
    for name in _ROWS:
        v = vals[name].reshape(-1)
        n = -(-v.shape[0] // HEAD_DIM) * HEAD_DIM
        out.append(jnp.pad(v, (0, n - v.shape[0])).reshape(-1, HEAD_DIM))
    return jnp.concatenate(out, axis=0)


def _unpack_rows(pack, like):
    out, r = {}, 0
    for name in _ROWS:
        n = like[name].shape[-1]
        nr = -(-n // HEAD_DIM)
        out[name] = pack[r:r + nr].reshape(1, -1)[:, :n]
        r += nr
    return out, r


def kernel(x, mem, norm_mix, w_in, fox_f_bias, fox_q_norm, fox_k_norm, gdn_conv, gdn_a_log, gdn_dt_bias, gdn_out_norm, mem_norm, w_mem_kv, mem_q_norm, mem_k_norm, w_out, norm_ffn, w_gate_up, w_down, loss_target, m_norm_mix, m_w_in, m_fox_f_bias, m_fox_q_norm, m_fox_k_norm, m_gdn_conv, m_gdn_a_log, m_gdn_dt_bias, m_gdn_out_norm, m_mem_norm, m_w_mem_kv, m_mem_q_norm, m_mem_k_norm, m_w_out, m_norm_ffn, m_w_gate_up, m_w_down, v_norm_mix, v_w_in, v_fox_f_bias, v_fox_q_norm, v_fox_k_norm, v_gdn_conv, v_gdn_a_log, v_gdn_dt_bias, v_gdn_out_norm, v_mem_norm, v_w_mem_kv, v_mem_q_norm, v_mem_k_norm, v_w_out, v_norm_ffn, v_w_gate_up, v_w_down):
    a = dict(locals())
    d = x.shape[-1]
    lay = _Layout(d)
    chip = 2 * lax.axis_index("x") + lax.axis_index("y")
    small = {n: a[n] for n in _ROWS}
    big = ["w_in", "w_mem_kv", "w_out", "w_gate_up", "w_down"]
    axes = [0, 0, 0, 1, 0]

    conv_cols = gdn_conv.shape[-1]
    conv_n = CONV_WIDTH * conv_cols
    conv_rows = -(-conv_n // HEAD_DIM)
    conv_blk = jnp.pad(gdn_conv.reshape(-1), (0, 32 * HEAD_DIM - conv_n)).reshape(32, HEAD_DIM)
    axis_of = dict(zip(big, axes), conv=0)
    shape_of = {n: a[n].shape[1:] for n in big}
    shape_of["w_in"] = (w_in.shape[1], lay.cols)
    shape_of["conv"] = conv_blk.shape
    placed = {n: _cast_place(a[n][0], axis_of[n], "cast_" + n) for n in big[1:]}
    placed["w_in"] = _cast_place(w_in[0], 0, "cast_w_in", lay.regroup, lay.cols)
    placed["conv"] = lax.dynamic_update_slice(lax.empty((4 * 32, HEAD_DIM), F32), conv_blk, (chip * 32, 0))
    grouped = {"in": ["w_in"], "mixer": ["w_mem_kv", "conv"], "out": ["w_out"], "gate_up": ["w_gate_up"],
               "down": ["w_down"]}
    inflight = {}

    def start(tags, name):
        names = [n for t in tags for n in grouped[t]]
        bufs, sems, _ = _gather_start([placed[n] for n in names], [axis_of[n] for n in names],
                                      [shape_of[n] for n in names],
                                      [[names.index(n) for n in grouped[t]] for t in tags], name)
        for t, pair in zip(tags, sems):
            inflight[t] = ([bufs[names.index(n)] for n in grouped[t]], pair)

    start(["in"], "gather_ici_start_in")
    start(["mixer", "out", "gate_up", "down"], "gather_ici_start_rest")

    forwarding = {}

    def prefetch(tag, after):
        bufs, sem_pair = inflight.pop(tag)
        ax, shp = [axis_of[n] for n in grouped[tag]], [shape_of[n] for n in grouped[tag]]
        got = _gather_wait(bufs, ax, shp, sem_pair, after, "gather_ici_wait_" + tag)
        geometry = _forward_geometry(ax, shp)
        got, sems, _ = _split_start("gather_forward_start_" + tag, got, geometry, 3 * len(got))
        forwarding[tag] = (got, sems, geometry)

    def weights(tag, after):
        got, sems, geometry = forwarding.pop(tag)
        got = _split_wait("gather_forward_wait_" + tag, got, sems, after, geometry)
        if tag != "mixer":
            return got
        taps = got[1].reshape(4, 32 * HEAD_DIM)[:, :conv_n].reshape(4, CONV_WIDTH, conv_cols)
        return got[0], jnp.transpose(taps, (1, 0, 2)).reshape(CONV_WIDTH, 4 * conv_cols)

    sp = dict(small)
    reducer = _Reducer({n: (axis_of[n], shape_of[n]) for n in big})
    loss_blk, dx, g = _local_step(x[0], mem[0], loss_target[0], prefetch, weights, reducer, sp)

    gsmall = {n: g[n] for n in _ROWS}
    pack = jnp.concatenate([_pack_rows(gsmall), g["gdn_conv"].reshape(-1, HEAD_DIM), loss_blk], axis=0)
    pack = jnp.pad(pack, ((0, -pack.shape[0] % 8), (0, 0)))
    tot = _allreduce_small(pack)
    gs, r0 = _unpack_rows(tot, small)
    conv_g = tot[r0:r0 + CONV_WIDTH * 4 * conv_cols // HEAD_DIM].reshape(CONV_WIDTH, 4 * conv_cols)
    gs_conv = lax.dynamic_slice_in_dim(conv_g, chip * conv_cols, conv_cols, axis=1)
    loss = tot[r0 + CONV_WIDTH * 4 * conv_cols // HEAD_DIM, 0]
    reduced = reducer.finish(tot)

    out = {"loss": loss, "grad_x": dx[None]}
    for n, gsh in reduced.items():
        res = _adamw(a[n][0], gsh, a["m_" + n][0], a["v_" + n][0], g_fn=lay.ungroup if n == "w_in" else None,
                     name="adamw_" + n)
        for pre, r in zip(["grad_", "delta_", "new_m_", "new_v_"], res):
            out[pre + n] = r[None]
    conv_pad = lambda v: jnp.pad(v.reshape(-1), (0, conv_rows * HEAD_DIM - conv_n)).reshape(conv_rows, HEAD_DIM)
    packs = []
    for src, cv in [(small, gdn_conv), (gs, gs_conv), ({n: a["m_" + n] for n in _ROWS}, m_gdn_conv),
                    ({n: a["v_" + n] for n in _ROWS}, v_gdn_conv)]:
        packs.append(jnp.concatenate([_pack_rows(src), conv_pad(cv)], axis=0))
    res = _adamw(*packs, name="adamw_small")
    for pre, r in zip(["grad_", "delta_", "new_m_", "new_v_"], res):
        vals, r1 = _unpack_rows(r, small)
        for n in _ROWS:
            out[pre + n] = vals[n]
        out[pre + "gdn_conv"] = r[r1:r1 + conv_rows].reshape(-1)[:conv_n].reshape(gdn_conv.shape)
    names = ["norm_mix", "w_in", "fox_f_bias", "fox_q_norm", "fox_k_norm", "gdn_conv", "gdn_a_log", "gdn_dt_bias",
             "gdn_out_norm", "mem_norm", "w_mem_kv", "mem_q_norm", "mem_k_norm", "w_out", "norm_ffn", "w_gate_up",
             "w_down"]
    return (out["loss"], out["grad_x"], *[out[p + n] for p in ["grad_", "delta_", "new_m_", "new_v_"] for n in names])
```

```python
import functools
import math

import jax
import jax.numpy as jnp
from jax import lax
from jax.experimental import pallas as pl
from jax.experimental.pallas import tpu as pltpu

F32, BF16 = jnp.float32, jnp.bfloat16
HEAD_DIM = 128
CHUNK = 64
N_MEM_HEADS = 4
CONV_WIDTH = 4
NORM_EPS = 1e-6
ADAM_LR, ADAM_B1, ADAM_B2, ADAM_EPS, ADAM_WD, ADAM_STEP = 0.001, 0.9, 0.999, 1e-08, 0.01, 10
VMEM_LIMIT = 48 * 1024 * 1024
NEG = -1e30
MESH = pl.DeviceIdType.MESH


def _cparams(sem=None, **kw):
    if sem is not None:
        kw["dimension_semantics"] = sem
    return pltpu.CompilerParams(vmem_limit_bytes=VMEM_LIMIT, **kw)


def _tile(n, target, mult=128):
    best = None
    d = mult
    while d <= min(n, target):
        if n % d == 0:
            best = d
        d += mult
    return best if best is not None else n


def _dot(a, b, dims, hi):
    if a.ndim == 3:
        dn = (((dims[0][0] + 1,), (dims[1][0] + 1,)), ((0,), (0,)))
    else:
        dn = (dims, ((), ()))
    if hi is not None:
        return lax.dot_general(a, b, dn, precision=hi, preferred_element_type=F32)
    return lax.dot_general(a.astype(BF16), b.astype(BF16), dn, preferred_element_type=F32)


def _make_dots(hi, cotangent=None):
    @jax.custom_vjp
    def nn(a, b):
        return _dot(a, b, ((1,), (0,)), hi)

    @jax.custom_vjp
    def nt(a, b):
        return _dot(a, b, ((1,), (1,)), hi)

    @jax.custom_vjp
    def tn(a, b):
        return _dot(a, b, ((0,), (0,)), hi)

    bnn, bnt, btn = cotangent or (nn, nt, tn)
    nn.defvjp(lambda a, b: (nn(a, b), (a, b)), lambda r, g: (bnt(g, r[1]), btn(r[0], g)))
    nt.defvjp(lambda a, b: (nt(a, b), (a, b)), lambda r, g: (bnn(g, r[1]), btn(g, r[0])))
    tn.defvjp(lambda a, b: (tn(a, b), (a, b)), lambda r, g: (bnt(r[1], g), bnn(r[0], g)))
    return nn, nt, tn


_nn, _nt, _tn = _make_dots(None)
_nn_hi, _nt_hi, _tn_hi = _make_dots(lax.Precision.HIGHEST)
_nn_x3, _nt_x3, _tn_x3 = _make_dots(lax.Precision.HIGH, (_nn, _nt, _tn))


def _sigmoid(x):
    return 1.0 / (1.0 + jnp.exp(-x))


@jax.custom_vjp
def _softplus(x):
    return jnp.maximum(x, 0.0) + jnp.log(1.0 + jnp.exp(-jnp.abs(x)))


_softplus.defvjp(lambda x: (_softplus(x), x), lambda x, g: (g * _sigmoid(x),))


def _silu(x):
    return x * _sigmoid(x)


def _rms_fn(x, gain, z=None):
    y = x * lax.rsqrt(jnp.mean(x * x, axis=-1, keepdims=True) + NORM_EPS) * gain
    if z is not None:
        y = y * _silu(z)
    return y


def _mm(a, b, *, ta=False, tb=False, out_dtype=F32, res=None, stack=None, after=None, name):
    a2, b2 = a.shape[-2:], b.shape[-2:]
    ns = b.shape[0] if stack else 1
    m = a2[1] if ta else a2[0]
    k = a2[0] if ta else a2[1]
    n = b2[0] if tb else b2[1]
    assert k == (b2[1] if tb else b2[0])
    tm, tn, tk = _mm_tiles(m, n, k, ns if stack == "sum" else 1, a.dtype.itemsize, b.dtype.itemsize,
                           jnp.dtype(out_dtype).itemsize, res is not None)
    nk = k // tk
    single = nk == 1 and stack != "sum"
    dims = ((0 if ta else 1,), (1 if tb else 0,))
    if stack == "sum":
        order = lambda g0, g1, g2, g3: (g2, g0, g1, g3)
        grid = (m // tm, n // tn, ns, nk)
    else:
        order = lambda g0, g1, g2, g3: (g0, g1, g2, g3)
        grid = (ns, m // tm, n // tn, nk)

    def body(*refs):
        if after is not None:
            refs = refs[:2 + (res is not None)] + refs[3 + (res is not None):]
        if single:
            a_ref, b_ref = refs[:2]
            r = lax.dot_general(a_ref[...].astype(BF16), b_ref[...].astype(BF16), (dims, ((), ())),
                                preferred_element_type=F32)
            if res is not None:
                r = r + refs[2][...]
            refs[-1][...] = r.astype(out_dtype)
            return
        if res is None:
            a_ref, b_ref, o_ref, acc = refs
        else:
            a_ref, b_ref, r_ref, o_ref, acc = refs
        s, _, _, kk = order(*[pl.program_id(d) for d in range(4)])
        first = kk == 0
        last = kk == nk - 1
        if stack == "sum":
            first, last = first & (s == 0), last & (s == ns - 1)

        @pl.when(first)
        def _():
            acc[...] = jnp.zeros_like(acc)

        acc[...] += lax.dot_general(a_ref[...].astype(BF16), b_ref[...].astype(BF16), (dims, ((), ())),
                                    preferred_element_type=F32)

        @pl.when(last)
        def _():
            r = acc[...]
            if res is not None:
                r = r + r_ref[...]
            o_ref[...] = r.astype(out_dtype)

    def spec(shape, idx, stacked):
        if stacked:
            return pl.BlockSpec((None,) + shape, lambda *g: (order(*g)[0],) + idx(*order(*g)))
        return pl.BlockSpec(shape, lambda *g: idx(*order(*g)))

    a_spec = (spec((tk, tm), lambda s, i, j, kk: (kk, i), stack == "sum") if ta
              else spec((tm, tk), lambda s, i, j, kk: (i, kk), stack == "sum"))
    b_spec = (spec((tn, tk), lambda s, i, j, kk: (j, kk), bool(stack)) if tb
              else spec((tk, tn), lambda s, i, j, kk: (kk, j), bool(stack)))
    o_spec = spec((tm, tn), lambda s, i, j, kk: (i, j), stack == "out")
    ins, specs = [a, b], [a_spec, b_spec]
    if res is not None:
        ins.append(res)
        specs.append(o_spec)
    if after is not None:
        ins.append(after)
        specs.append(pl.BlockSpec(after.shape, lambda *g: (0,) * after.ndim))
    sem = (("parallel", "parallel", "arbitrary", "arbitrary") if stack == "sum"
           else ("parallel", "parallel", "parallel", "arbitrary"))
    return pl.pallas_call(
        body, name=name, grid=grid, in_specs=specs, out_specs=o_spec,
        out_shape=jax.ShapeDtypeStruct(((ns,) if stack == "out" else ()) + (m, n), out_dtype),
        scratch_shapes=[] if single else [pltpu.VMEM((tm, tn), F32)],
        compiler_params=_cparams(sem),
    )(*ins)


MM_VMEM_BUDGET = 40 * 1024 * 1024


def _mm_tiles(m, n, k, ns, sa, sb, so, has_res):
    def divs(x, mult, cap):
        out = [d for d in range(mult, min(x, cap) + 1, mult) if x % d == 0]
        return out or [x]

    best = None
    for tk in divs(k, 128, 8192):
        nk = (k // tk) * ns
        for tm in divs(m, 8, 2048):
            for tn in divs(n, 128, 2048):
                vmem = 2 * (tm * tk * sa + tk * tn * sb + tm * tn * so) + (2 * tm * tn * 4 if has_res else 0)
                vmem += tm * tn * 4 if nk > 1 else 0
                if vmem > MM_VMEM_BUDGET:
                    continue
                steps = (m // tm) * (n // tn) * nk
                traffic = (m // tm) * k * n * sb * ns + (n // tn if nk > 1 else 1) * m * k * sa * ns
                cost = steps * 0.4e-6 + traffic / 2.5e12 + (nk * m * n * 8 / 6e12 if nk > 1 else 0)
                cost += 2.0 * m * n * k * ns / 7e14
                if best is None or cost < best[0]:
                    best = (cost, tm, tn, tk)
    return best[1:]


def _norm_fwd(x, xoff, gain, ncol, w, out_dtype, *, z=None, zoff=0, into=None, into_off=0, name):
    t = x.shape[0]
    tr = _tile(t, max(256, (1 << 18) // w), 8)

    def body(*refs):
        x_ref, g_ref, o_ref = refs[0], refs[1], refs[-1]
        y = _rms_fn(x_ref[...], g_ref[...]) if z is None else _rms_fn(x_ref[...], g_ref[...], refs[2][...])
        o_ref[...] = y.astype(out_dtype)

    ins = [x, gain]
    specs = [pl.BlockSpec((tr, w), lambda j, r: (r, xoff + j)), pl.BlockSpec((1, w), lambda j, r: (0, 0))]
    if z is not None:
        ins.append(z)
        specs.append(pl.BlockSpec((tr, w), lambda j, r: (r, zoff + j)))
    aliases = {}
    if into is not None:
        aliases = {len(ins): 0}
        ins.append(into)
        specs.append(pl.BlockSpec(memory_space=pl.ANY))
    return pl.pallas_call(
        body, name=name, grid=(ncol, t // tr), in_specs=specs,
        out_specs=pl.BlockSpec((tr, w), lambda j, r: (r, into_off + j)),
        out_shape=jax.ShapeDtypeStruct((t, ncol * w) if into is None else into.shape, out_dtype),
        input_output_aliases=aliases, compiler_params=_cparams(("parallel", "parallel")),
    )(*ins)


def _norm_bwd(x, xoff, gain, dy, dyoff, ncol, w, *, z=None, zoff=0, res=None, name):
    t = x.shape[0]
    tr = _tile(t, max(256, (1 << 18) // w), 8)

    def body(*refs):
        it = iter(refs)
        x_ref, g_ref = next(it), next(it)
        z_ref = next(it) if z is not None else None
        dy_ref = next(it)
        r_ref = next(it) if res is not None else None
        dx_ref = next(it)
        dz_ref = next(it) if z is not None else None
        dg_ref = next(it)

        @pl.when((pl.program_id(0) == 0) & (pl.program_id(1) == 0))
        def _():
            dg_ref[...] = jnp.zeros_like(dg_ref)

        args = (x_ref[...], g_ref[...]) + ((z_ref[...],) if z is not None else ())
        _, vjp = jax.vjp(_rms_fn, *args)
        grads = vjp(dy_ref[...].astype(F32))
        dx = grads[0]
        if res is not None:
            dx = dx + r_ref[...]
        dx_ref[...] = dx
        if z is not None:
            dz_ref[...] = grads[2]
        dg_ref[...] += grads[1]

    ins = [x, gain]
    specs = [pl.BlockSpec((tr, w), lambda j, r: (r, xoff + j)), pl.BlockSpec((1, w), lambda j, r: (0, 0))]
    if z is not None:
        ins.append(z)
        specs.append(pl.BlockSpec((tr, w), lambda j, r: (r, zoff + j)))
    ins.append(dy)
    specs.append(pl.BlockSpec((tr, w), lambda j, r: (r, dyoff + j)))
    blk = pl.BlockSpec((tr, w), lambda j, r: (r, j))
    if res is not None:
        ins.append(res)
        specs.append(blk)
    full = jax.ShapeDtypeStruct((t, ncol * w), F32)
    out_shape, out_specs = [full], [blk]
    if z is not None:
        out_shape.append(full)
        out_specs.append(blk)
    out_shape.append(jax.ShapeDtypeStruct((1, w), F32))
    out_specs.append(pl.BlockSpec((1, w), lambda j, r: (0, 0)))
    return pl.pallas_call(
        body, name=name, grid=(ncol, t // tr), in_specs=specs, out_specs=out_specs, out_shape=out_shape,
        compiler_params=_cparams(("arbitrary", "arbitrary")),
    )(*ins)


def _small_fn(x, pa, pb, nf, ng):
    lane = lax.broadcasted_iota(jnp.int32, x.shape, 1)
    zz = x + pb
    logf = -_softplus(-zz)
    g = -jnp.exp(pa) * _softplus(zz)
    beta = _sigmoid(x)
    return jnp.where(lane < nf, logf, jnp.where(lane < nf + ng, g, beta))


def _tri(n, upper):
    r = lax.broadcasted_iota(jnp.int32, (n, n), 0)
    c = lax.broadcasted_iota(jnp.int32, (n, n), 1)
    return jnp.where((c >= r) if upper else (c <= r), 1.0, 0.0).astype(F32)


def _small_fwd(p, off, pa, pb, nf, ng):
    t = p.shape[0]
    blk = HEAD_DIM
    nb = t // blk

    def body(x_ref, pa_ref, pb_ref, v_ref, c_ref):
        v_ref[...] = _small_fn(x_ref[...], pa_ref[...], pb_ref[...], nf, ng)
        tri = _tri(blk, False)

        carry = jnp.zeros((1, HEAD_DIM), F32)
        for i in range(nb):
            rows = slice(i * blk, (i + 1) * blk)
            c = _nn_hi(tri, v_ref[rows, :]) + carry
            c_ref[rows, :] = c
            carry = c[blk - 1:blk, :]

    row = pl.BlockSpec((1, HEAD_DIM), lambda i: (0, 0))
    out = pl.BlockSpec((t, HEAD_DIM), lambda i: (0, 0))
    return pl.pallas_call(
        body, name="small_fwd", grid=(1,),
        in_specs=[pl.BlockSpec((t, HEAD_DIM), lambda i: (0, off)), row, row], out_specs=[out, out],
        out_shape=[jax.ShapeDtypeStruct((t, HEAD_DIM), F32)] * 2,
        compiler_params=_cparams(("arbitrary",)),
    )(p, pa, pb)


def _small_bwd(p, off, pa, pb, dvals, dcsum, nf, ng):
    t = p.shape[0]
    blk = HEAD_DIM
    nb = t // blk

    def body(x_ref, pa_ref, pb_ref, dv_ref, dc_ref, dx_ref, dpa_ref, dpb_ref, tot_ref):
        tri = _tri(blk, True)

        carry = jnp.zeros((1, HEAD_DIM), F32)
        for i in reversed(range(nb)):
            rows = slice(i * blk, (i + 1) * blk)
            c = _nn_hi(tri, dc_ref[rows, :]) + carry
            tot_ref[rows, :] = c + dv_ref[rows, :]
            carry = c[0:1, :]
        f = functools.partial(_small_fn, nf=nf, ng=ng)
        _, vjp = jax.vjp(f, x_ref[...], pa_ref[...], pb_ref[...])
        dx, dpa, dpb = vjp(tot_ref[...])
        dx_ref[...] = dx
        dpa_ref[...] = dpa
        dpb_ref[...] = dpb

    row = pl.BlockSpec((1, HEAD_DIM), lambda i: (0, 0))
    full = pl.BlockSpec((t, HEAD_DIM), lambda i: (0, 0))
    return pl.pallas_call(
        body, name="small_bwd", grid=(1,),
        in_specs=[pl.BlockSpec((t, HEAD_DIM), lambda i: (0, off)), row, row, full, full],
        out_specs=[full, row, row],
        out_shape=[jax.ShapeDtypeStruct((t, HEAD_DIM), F32), jax.ShapeDtypeStruct((1, HEAD_DIM), F32),
                   jax.ShapeDtypeStruct((1, HEAD_DIM), F32)],
        scratch_shapes=[pltpu.VMEM((t, HEAD_DIM), F32)],
        compiler_params=_cparams(("arbitrary",)),
    )(p, pa, pb, dvals, dcsum)


def _fox_heads(nf, most):
    return next(h for h in range(most, 0, -1) if nf % h == 0)


def _fox_fwd(q, k, v, cc, cr, nf, tq, tk, d_mix):
    t = q.shape[0]
    scale = HEAD_DIM ** -0.5
    assert tq == tk

    vt = jnp.transpose(v.reshape(t // tk, tk, nf, HEAD_DIM), (2, 0, 3, 1))

    hp = _fox_heads(nf, 3)
    lanes = lambda h: slice(h * HEAD_DIM, (h + 1) * HEAD_DIM)

    def body(q_ref, k_ref, vt_ref, cc_ref, cr_ref, o_ref, lse_ref, mix_ref):
        i = pl.program_id(1)
        qs = [q_ref[:, lanes(h)] for h in range(hp)]
        cqs = [cr_ref[h, i] for h in range(hp)]
        ones = jnp.ones((8, tk), BF16)
        diff = lax.broadcasted_iota(jnp.int32, (tk, tq), 0) - lax.broadcasted_iota(jnp.int32, (tk, tq), 1)

        def scores(h, j):
            ks = pl.ds(pl.multiple_of(j * tk, tk), tk)
            return lax.dot_general(k_ref[ks, lanes(h)], qs[h], (((1,), (1,)), ((), ())),
                                   preferred_element_type=F32)

        def tile(h, j, m, l, acc, s, masked):
            ks = pl.ds(pl.multiple_of(j * tk, tk), tk)
            s = s * scale + cqs[h] - cc_ref[h, ks, :]
            if masked:
                s = jnp.where(diff <= 0, s, NEG)
            m_new = jnp.maximum(m, jnp.max(s, axis=0, keepdims=True))
            pr = jnp.exp(s - m_new).astype(BF16)
            alpha = jnp.exp(m - m_new)
            l = alpha * l + jnp.dot(ones, pr, preferred_element_type=F32)[:1]
            acc = alpha * acc + jnp.dot(vt_ref[h, j], pr, preferred_element_type=F32)
            return m_new, l, acc

        def step(j, carry):
            nxt = [scores(h, j + 1) for h in range(hp)]
            return tuple(tile(h, j, *carry[h], False) + (nxt[h],) for h in range(hp))

        init = tuple((jnp.full((1, tq), NEG, F32), jnp.zeros((1, tq), F32), jnp.zeros((HEAD_DIM, tq), F32),
                      scores(h, 0)) for h in range(hp))
        carry = lax.fori_loop(0, i, step, init)
        for h in range(hp):
            m, l, acc = tile(h, i, *carry[h], True)
            o = jnp.transpose(acc / l)
            o_ref[:, lanes(h)] = o
            mix_ref[:, lanes(h)] = o.astype(BF16)
            lse_ref[h, 0] = m + jnp.log(l)

    w = hp * HEAD_DIM
    qblk = pl.BlockSpec((tq, w), lambda h, i: (i, h))
    return pl.pallas_call(
        body, name="fox_fwd", grid=(nf // hp, t // tq),
        in_specs=[qblk, pl.BlockSpec((t, w), lambda h, i: (0, h)),
                  pl.BlockSpec((hp, t // tk, HEAD_DIM, tk), lambda h, i: (h, 0, 0, 0)),
                  pl.BlockSpec((hp, t, 1), lambda h, i: (h, 0, 0)),
                  pl.BlockSpec((hp, t // tk, 1, tk), lambda h, i: (h, 0, 0, 0))],
        out_specs=[qblk, pl.BlockSpec((hp, 1, 1, tq), lambda h, i: (h, i, 0, 0)), qblk],
        out_shape=[jax.ShapeDtypeStruct((t, nf * HEAD_DIM), F32), jax.ShapeDtypeStruct((nf, t // tq, 1, tq), F32),
                   jax.ShapeDtypeStruct((t, d_mix), BF16)],
        compiler_params=_cparams(("parallel", "parallel")),
    )(q, k, vt, cc, cr)


def _fox_bwd(q, k, v, cc, cr, o, lse, dmix, nf, tq, tk):
    t = q.shape[0]
    scale = HEAD_DIM ** -0.5
    assert tq == tk
    hp = _fox_heads(nf, 3)
    lanes = lambda h: slice(h * HEAD_DIM, (h + 1) * HEAD_DIM)
    kt = jnp.transpose(k.reshape(t // tk, tk, nf, HEAD_DIM), (2, 0, 3, 1))

    def body(q_ref, k_ref, kt_ref, v_ref, cc_ref, cr_ref, o_ref, lse_ref, do_ref,
             dq_ref, dk_ref, dv_ref, dcq_ref, dck_ref):
        i = pl.program_id(1)

        @pl.when(i == 0)
        def _():
            dk_ref[...] = jnp.zeros_like(dk_ref)
            dv_ref[...] = jnp.zeros_like(dv_ref)
            dck_ref[...] = jnp.zeros_like(dck_ref)

        diff = lax.broadcasted_iota(jnp.int32, (tk, tq), 0) - lax.broadcasted_iota(jnp.int32, (tk, tq), 1)
        qs = [q_ref[:, lanes(h)] for h in range(hp)]
        dos = [do_ref[:, lanes(h)] for h in range(hp)]
        do_b = [d.astype(BF16) for d in dos]
        cqs = [cr_ref[h, i] for h in range(hp)]
        lses = [lse_ref[h, 0] for h in range(hp)]
        deltas = [jnp.sum(jnp.transpose(dos[h] * o_ref[:, lanes(h)]), axis=0, keepdims=True) for h in range(hp)]

        def products(h, j):
            ks = pl.ds(pl.multiple_of(j * tk, tk), tk)
            nt = (((1,), (1,)), ((), ()))
            return (lax.dot_general(k_ref[ks, lanes(h)], qs[h], nt, preferred_element_type=F32),
                    lax.dot_general(v_ref[ks, lanes(h)], do_b[h], nt, preferred_element_type=F32))

        def tile(h, j, dqt, dcq, s, dp, masked):
            ks = pl.ds(pl.multiple_of(j * tk, tk), tk)
            pr = jnp.exp(s * scale + cqs[h] - cc_ref[h, ks, :] - lses[h])
            if masked:
                pr = jnp.where(diff <= 0, pr, 0.0)
            ds = pr * (dp - deltas[h])
            ds_b = ds.astype(BF16)
            dqt = dqt + jnp.dot(kt_ref[h, j], ds_b, preferred_element_type=F32)
            dk_ref[ks, lanes(h)] += jnp.dot(ds_b, qs[h], preferred_element_type=F32) * scale
            dv_ref[ks, lanes(h)] += jnp.dot(pr.astype(BF16), do_b[h], preferred_element_type=F32)
            dck_ref[h, ks, :] -= jnp.sum(ds, axis=1, keepdims=True)
            return dqt, dcq + jnp.sum(ds, axis=0, keepdims=True)

        def step(j, carry):
            nxt = [products(h, j + 1) for h in range(hp)]
            return tuple(tile(h, j, *carry[h], False) + nxt[h] for h in range(hp))

        init = tuple((jnp.zeros((HEAD_DIM, tq), F32), jnp.zeros((1, tq), F32)) + products(h, 0) for h in range(hp))
        carry = lax.fori_loop(0, i, step, init)
        for h in range(hp):
            dqt, dcq = tile(h, i, *carry[h], True)
            dq_ref[:, lanes(h)] = jnp.transpose(dqt) * scale
            dcq_ref[h, 0] = dcq

    w = hp * HEAD_DIM
    head_all = pl.BlockSpec((t, w), lambda h, i: (0, h))
    qblk = pl.BlockSpec((tq, w), lambda h, i: (i, h))
    colv = pl.BlockSpec((hp, t, 1), lambda h, i: (h, 0, 0))
    rows_all = pl.BlockSpec((hp, t // tk, 1, tk), lambda h, i: (h, 0, 0, 0))
    row_blk = pl.BlockSpec((hp, 1, 1, tq), lambda h, i: (h, i, 0, 0))
    wide = jax.ShapeDtypeStruct((t, nf * HEAD_DIM), F32)
    return pl.pallas_call(
        body, name="fox_bwd", grid=(nf // hp, t // tq),
        in_specs=[qblk, head_all, pl.BlockSpec((hp, t // tk, HEAD_DIM, tk), lambda h, i: (h, 0, 0, 0)), head_all,
                  colv, rows_all, qblk, row_blk, qblk],
        out_specs=[qblk, head_all, head_all, row_blk, colv],
        out_shape=[wide, wide, wide, jax.ShapeDtypeStruct((nf, t // tq, 1, tq), F32),
                   jax.ShapeDtypeStruct((nf, t, 1), F32)],
        compiler_params=_cparams(("parallel", "arbitrary")),
    )(q, k, kt, v, cc, cr, o, lse, dmix)


def _mem_fn(mq, mk, mv, gq, gk):
    qn = _rms_fn(mq, gq)
    kn = _rms_fn(mk, gk)
    s = _nt(qn, kn) * (HEAD_DIM ** -0.5)
    e = jnp.exp(s - lax.stop_gradient(jnp.max(s, axis=1, keepdims=True)))
    pr = e / jnp.sum(e, axis=1, keepdims=True)
    return _nn(pr, mv)


def _mem_specs(t, m, tq, qoff):
    qblk = pl.BlockSpec((tq, HEAD_DIM), lambda h, i: (i, qoff + h))
    kblk = pl.BlockSpec((m, HEAD_DIM), lambda h, i: (0, h))
    vblk = pl.BlockSpec((m, HEAD_DIM), lambda h, i: (0, N_MEM_HEADS + h))
    row = pl.BlockSpec((1, HEAD_DIM), lambda h, i: (0, 0))
    return qblk, kblk, vblk, row


def _mem_fwd(p, qoff, mkv, gq, gk, tq, into, into_off):
    t, m = p.shape[0], mkv.shape[0]
    qblk, kblk, vblk, row = _mem_specs(t, m, tq, qoff)

    def body(q_ref, k_ref, v_ref, gq_ref, gk_ref, _, o_ref):
        o_ref[...] = _mem_fn(q_ref[...], k_ref[...], v_ref[...], gq_ref[...], gk_ref[...]).astype(BF16)

    return pl.pallas_call(
        body, name="mem_fwd", grid=(N_MEM_HEADS, t // tq),
        in_specs=[qblk, kblk, vblk, row, row, pl.BlockSpec(memory_space=pl.ANY)],
        out_specs=pl.BlockSpec((tq, HEAD_DIM), lambda h, i: (i, into_off + h)),
        out_shape=jax.ShapeDtypeStruct(into.shape, BF16), input_output_aliases={5: 0},
        compiler_params=_cparams(("parallel", "parallel")),
    )(p, mkv, mkv, gq, gk, into)


def _mem_bwd(p, qoff, mkv, gq, gk, dmix, dooff, tq):
    t, m = p.shape[0], mkv.shape[0]
    qblk, kblk, vblk, row = _mem_specs(t, m, tq, qoff)

    def body(q_ref, k_ref, v_ref, gq_ref, gk_ref, do_ref, dq_ref, dkv_k_ref, dkv_v_ref, dgq_ref, dgk_ref):
        h, i = pl.program_id(0), pl.program_id(1)

        @pl.when((h == 0) & (i == 0))
        def _():
            dgq_ref[...] = jnp.zeros_like(dgq_ref)
            dgk_ref[...] = jnp.zeros_like(dgk_ref)

        @pl.when(i == 0)
        def _():
            dkv_k_ref[...] = jnp.zeros_like(dkv_k_ref)
            dkv_v_ref[...] = jnp.zeros_like(dkv_v_ref)

        _, vjp = jax.vjp(_mem_fn, q_ref[...], k_ref[...], v_ref[...], gq_ref[...], gk_ref[...])
        dq, dk, dv, dgq, dgk = vjp(do_ref[...])
        dq_ref[...] = dq
        dkv_k_ref[...] += dk
        dkv_v_ref[...] += dv
        dgq_ref[...] += dgq
        dgk_ref[...] += dgk

    oblk = pl.BlockSpec((tq, HEAD_DIM), lambda h, i: (i, h))
    kout = pl.BlockSpec((m, HEAD_DIM), lambda h, i: (0, h))
    half = jax.ShapeDtypeStruct((m, N_MEM_HEADS * HEAD_DIM), F32)
    rshape = jax.ShapeDtypeStruct((1, HEAD_DIM), F32)
    return pl.pallas_call(
        body, name="mem_bwd", grid=(N_MEM_HEADS, t // tq),
        in_specs=[qblk, kblk, vblk, row, row, pl.BlockSpec((tq, HEAD_DIM), lambda h, i: (i, dooff + h))],
        out_specs=[oblk, kout, kout, row, row],
        out_shape=[jax.ShapeDtypeStruct((t, N_MEM_HEADS * HEAD_DIM), F32), half, half, rshape, rshape],
        compiler_params=_cparams(("arbitrary", "arbitrary")),
    )(p, mkv, mkv, gq, gk, dmix)


def _shift_down(x, s):
    if s == 0:
        return x
    r = lax.broadcasted_iota(jnp.int32, x.shape, 0)
    return jnp.where(r >= s, pltpu.roll(x, s, 0), 0.0)


def _shift_up(x, s):
    if s == 0:
        return x
    n = x.shape[0]
    r = lax.broadcasted_iota(jnp.int32, x.shape, 0)
    return jnp.where(r < n - s, pltpu.roll(x, n - s, 0), 0.0)


def _conv_fn(x0, x1, x2, x3, w0, w1, w2, w3, kind):
    y = _silu(x0 * w0 + x1 * w1 + x2 * w2 + x3 * w3)
    if kind == 2:
        return y
    y = y * lax.rsqrt(jnp.sum(y * y, axis=-1, keepdims=True) + NORM_EPS)
    return y * (HEAD_DIM ** -0.5) if kind == 0 else y


def _conv_fwd(p, off, conv_w, ng):
    t = p.shape[0]

    def body(x_ref, w_ref, o_ref):
        kind = pl.program_id(0) // ng
        x = x_ref[...]
        xs = [_shift_down(x, CONV_WIDTH - 1 - j) for j in range(CONV_WIDTH)]
        ws = [w_ref[j:j + 1, :] for j in range(CONV_WIDTH)]
        for kd in range(3):
            @pl.when(kind == kd)
            def _(kd=kd):
                o_ref[...] = _conv_fn(*xs, *ws, kd)

    return pl.pallas_call(
        body, name="gdn_conv_fwd", grid=(3 * ng,),
        in_specs=[pl.BlockSpec((t, HEAD_DIM), lambda c: (0, off + c)),
                  pl.BlockSpec((CONV_WIDTH, HEAD_DIM), lambda c: (0, c))],
        out_specs=pl.BlockSpec((t, HEAD_DIM), lambda c: (0, c)),
        out_shape=jax.ShapeDtypeStruct((t, 3 * ng * HEAD_DIM), F32),
        compiler_params=_cparams(("parallel",)),
    )(p, conv_w)


def _conv_bwd(p, off, conv_w, dys, ng):
    t = p.shape[0]

    def body(x_ref, w_ref, dq_ref, dk_ref, dv_ref, dx_ref, dw_ref):
        kind = pl.program_id(0) // ng
        dy_refs = (dq_ref, dk_ref, dv_ref)
        x = x_ref[...]
        xs = [_shift_down(x, CONV_WIDTH - 1 - j) for j in range(CONV_WIDTH)]
        ws = [w_ref[j:j + 1, :] for j in range(CONV_WIDTH)]
        for kd in range(3):
            @pl.when(kind == kd)
            def _(kd=kd):
                _, vjp = jax.vjp(functools.partial(_conv_fn, kind=kd), *xs, *ws)
                g = vjp(dy_refs[kd][...])
                dx = _shift_up(g[0], CONV_WIDTH - 1)
                for j in range(1, CONV_WIDTH):
                    dx = dx + _shift_up(g[j], CONV_WIDTH - 1 - j)
                dx_ref[...] = dx
                for j in range(CONV_WIDTH):
                    dw_ref[j:j + 1, :] = g[CONV_WIDTH + j]

    blk = pl.BlockSpec((t, HEAD_DIM), lambda c: (0, c))
    head = pl.BlockSpec((t, HEAD_DIM), lambda c: (0, c % ng))
    wblk = pl.BlockSpec((CONV_WIDTH, HEAD_DIM), lambda c: (0, c))
    return pl.pallas_call(
        body, name="gdn_conv_bwd", grid=(3 * ng,),
        in_specs=[pl.BlockSpec((t, HEAD_DIM), lambda c: (0, off + c)), wblk] + [head] * 3,
        out_specs=[blk, wblk],
        out_shape=[jax.ShapeDtypeStruct((t, 3 * ng * HEAD_DIM), F32),
                   jax.ShapeDtypeStruct((CONV_WIDTH, 3 * ng * HEAD_DIM), F32)],
        compiler_params=_cparams(("parallel",)),
    )(p, conv_w, *dys)


def _wy_fn(q, k, v, gcol, grow, bcol):
    b, c, dk = q.shape
    r = lax.broadcasted_iota(jnp.int32, (1, c, c), 1)
    e = lax.broadcasted_iota(jnp.int32, (1, c, c), 2)
    tril, strict = e <= r, e < r
    gc_col = jnp.sum(jnp.where(tril, grow, 0.0), axis=2, keepdims=True)
    gc_row = jnp.sum(jnp.where(r <= e, gcol, 0.0), axis=1, keepdims=True)
    g_last = jnp.sum(gcol, axis=1, keepdims=True)
    decay = jnp.exp(jnp.where(tril, gc_col - gc_row, NEG))
    kb, vb = k * bcol, v * bcol
    lower = jnp.where(strict, _nt(kb, k) * decay, 0.0)
    inv = jnp.where(r == e, 1.0, 0.0) - lower
    pw = lower
    for _ in range(int(math.log2(c)) - 1):
        pw = _nn_x3(pw, pw)
        inv = inv + _nn_x3(inv, pw)
    u = _nn_x3(inv, vb)
    w = _nn_x3(inv, kb * jnp.exp(gc_col))
    attn = jnp.where(tril, _nt(q, k) * decay, 0.0)
    qg = q * jnp.exp(gc_col)
    kdec = k * jnp.exp(g_last - gc_col)
    egl = jnp.broadcast_to(jnp.exp(g_last), (b, 1, dk))
    return u, w, qg, kdec, attn, egl


def _scan_fn(u, w, qg, kdec, attn, egl, state):
    v_new = u - _nn(w, state)
    o = _nn(qg, state) + _nn(attn, v_new)
    return o, state * egl + _tn(kdec, v_new)


GDN_CHUNKS_PER_STEP = 4


def _gdn_fwd(qkv, gcol, grow, bcol, ng):
    t = qkv.shape[0]
    nch = t // CHUNK

    cb = GDN_CHUNKS_PER_STEP
    wy = _gdn_wy(qkv, gcol, grow, bcol, ng, cb)

    def body(u_ref, w_ref, qg_ref, kd_ref, at_ref, eg_ref, o_ref, st_ref, state):
        @pl.when(pl.program_id(0) == 0)
        def _():
            state[...] = jnp.zeros_like(state)

        st_ref[:, 0] = state[...]
        heads = lambda ref: jnp.stack([ref[:, h * HEAD_DIM:(h + 1) * HEAD_DIM] for h in range(ng)])
        o, new = _scan_fn(heads(u_ref), heads(w_ref), heads(qg_ref), heads(kd_ref), at_ref[:, 0], eg_ref[:, 0],
                          state[...])
        for h in range(ng):
            o_ref[:, h * HEAD_DIM:(h + 1) * HEAD_DIM] = o[h]
        state[...] = new

    w = ng * HEAD_DIM
    blk = pl.BlockSpec((CHUNK, w), lambda i: (i, 0))
    o, states = pl.pallas_call(
        body, name="gdn_scan_fwd", grid=(nch,),
        in_specs=[blk, blk, blk, blk, pl.BlockSpec((ng, 1, CHUNK, CHUNK), lambda i: (0, i, 0, 0)),
                  pl.BlockSpec((ng, 1, 1, HEAD_DIM), lambda i: (0, i, 0, 0))],
        out_specs=[blk, pl.BlockSpec((ng, 1, HEAD_DIM, HEAD_DIM), lambda i: (0, i, 0, 0))],
        out_shape=[jax.ShapeDtypeStruct((t, w), F32),
                   jax.ShapeDtypeStruct((ng, nch, HEAD_DIM, HEAD_DIM), F32)],
        scratch_shapes=[pltpu.VMEM((ng, HEAD_DIM, HEAD_DIM), F32)],
        compiler_params=_cparams(("arbitrary",)),
    )(*wy)
    return o, (wy, states)


def _wy_batch(q_ref, k_ref, v_ref, gc_ref, gr_ref, bc_ref, ng, cb):
    idx = [(c, h) for c in range(cb) for h in range(ng)]
    rows = lambda c: slice(c * CHUNK, (c + 1) * CHUNK)
    lanes = lambda h: slice(h * HEAD_DIM, (h + 1) * HEAD_DIM)
    wide = lambda ref: jnp.stack([ref[rows(c), lanes(h)] for c, h in idx])
    col = lambda ref: jnp.stack([ref[h, rows(c), :] for c, h in idx])
    return idx, (wide(q_ref), wide(k_ref), wide(v_ref), col(gc_ref), jnp.stack([gr_ref[h, c] for c, h in idx]),
                 col(bc_ref))


def _gdn_wy(qkv, gcol, grow, bcol, ng, cb):
    t = qkv.shape[0]
    nch = t // CHUNK

    def body(q_ref, k_ref, v_ref, gc_ref, gr_ref, bc_ref, u_ref, w_ref, qg_ref, kd_ref, at_ref, eg_ref):
        idx, args = _wy_batch(q_ref, k_ref, v_ref, gc_ref, gr_ref, bc_ref, ng, cb)
        u, w, qg, kd, at, eg = _wy_fn(*args)
        for b, (c, h) in enumerate(idx):
            rows, lanes = slice(c * CHUNK, (c + 1) * CHUNK), slice(h * HEAD_DIM, (h + 1) * HEAD_DIM)
            u_ref[rows, lanes] = u[b]
            w_ref[rows, lanes] = w[b]
            qg_ref[rows, lanes] = qg[b]
            kd_ref[rows, lanes] = kd[b]
            at_ref[h, c] = at[b]
            eg_ref[h, c] = eg[b]

    wd = ng * HEAD_DIM
    blk = lambda o: pl.BlockSpec((cb * CHUNK, wd), lambda i: (i, o))
    col = pl.BlockSpec((ng, cb * CHUNK, 1), lambda i: (0, i, 0))
    wide = jax.ShapeDtypeStruct((t, wd), F32)
    return pl.pallas_call(
        body, name="gdn_wy_fwd", grid=(nch // cb,),
        in_specs=[blk(0), blk(1), blk(2), col, pl.BlockSpec((ng, cb, 1, CHUNK), lambda i: (0, i, 0, 0)), col],
        out_specs=[blk(0), blk(0), blk(0), blk(0), pl.BlockSpec((ng, cb, CHUNK, CHUNK), lambda i: (0, i, 0, 0)),
                   pl.BlockSpec((ng, cb, 1, HEAD_DIM), lambda i: (0, i, 0, 0))],
        out_shape=[wide, wide, wide, wide, jax.ShapeDtypeStruct((ng, nch, CHUNK, CHUNK), F32),
                   jax.ShapeDtypeStruct((ng, nch, 1, HEAD_DIM), F32)],
        compiler_params=_cparams(("parallel",)),
    )(qkv, qkv, qkv, gcol, grow, bcol)


def _gdn_bwd(qkv, gcol, grow, bcol, saved, do, ng):
    t = qkv.shape[0]
    nch = t // CHUNK
    cb = GDN_CHUNKS_PER_STEP // 2
    wy, states = saved
    wd = ng * HEAD_DIM

    def scan_body(u_ref, w_ref, qg_ref, kd_ref, at_ref, eg_ref, st_ref, do_ref,
                  du_ref, dw_ref, dqg_ref, dkd_ref, dat_ref, deg_ref, dstate):
        @pl.when(pl.program_id(0) == 0)
        def _():
            dstate[...] = jnp.zeros_like(dstate)

        heads = lambda ref: jnp.stack([ref[:, h * HEAD_DIM:(h + 1) * HEAD_DIM] for h in range(ng)])
        _, vjp = jax.vjp(_scan_fn, heads(u_ref), heads(w_ref), heads(qg_ref), heads(kd_ref), at_ref[:, 0],
                         eg_ref[:, 0], st_ref[:, 0])
        du, dw, dqg, dkd, dat, deg, dst = vjp((heads(do_ref), dstate[...]))
        for h in range(ng):
            lanes = slice(h * HEAD_DIM, (h + 1) * HEAD_DIM)
            du_ref[:, lanes] = du[h]
            dw_ref[:, lanes] = dw[h]
            dqg_ref[:, lanes] = dqg[h]
            dkd_ref[:, lanes] = dkd[h]
        dat_ref[:, 0] = dat
        deg_ref[:, 0] = deg
        dstate[...] = dst

    rev = lambda i: nch - 1 - i
    blk = pl.BlockSpec((CHUNK, wd), lambda i: (rev(i), 0))
    atb = pl.BlockSpec((ng, 1, CHUNK, CHUNK), lambda i: (0, rev(i), 0, 0))
    egb = pl.BlockSpec((ng, 1, 1, HEAD_DIM), lambda i: (0, rev(i), 0, 0))
    wide = jax.ShapeDtypeStruct((t, wd), F32)
    at_shape = jax.ShapeDtypeStruct((ng, nch, CHUNK, CHUNK), F32)
    eg_shape = jax.ShapeDtypeStruct((ng, nch, 1, HEAD_DIM), F32)
    dwy = pl.pallas_call(
        scan_body, name="gdn_scan_bwd", grid=(nch,),
        in_specs=[blk, blk, blk, blk, atb, egb,
                  pl.BlockSpec((ng, 1, HEAD_DIM, HEAD_DIM), lambda i: (0, rev(i), 0, 0)), blk],
        out_specs=[blk, blk, blk, blk, atb, egb],
        out_shape=[wide, wide, wide, wide, at_shape, eg_shape],
        scratch_shapes=[pltpu.VMEM((ng, HEAD_DIM, HEAD_DIM), F32)],
        compiler_params=_cparams(("arbitrary",)),
    )(*wy, states, do)

    def wy_body(q_ref, k_ref, v_ref, gc_ref, gr_ref, bc_ref, du_ref, dw_ref, dqg_ref, dkd_ref, dat_ref, deg_ref,
                dq_ref, dk_ref, dv_ref, dgc_ref, dgr_ref, dbc_ref):
        idx, args = _wy_batch(q_ref, k_ref, v_ref, gc_ref, gr_ref, bc_ref, ng, cb)
        rows = lambda c: slice(c * CHUNK, (c + 1) * CHUNK)
        lanes = lambda h: slice(h * HEAD_DIM, (h + 1) * HEAD_DIM)
        wide_ct = lambda ref: jnp.stack([ref[rows(c), lanes(h)] for c, h in idx])
        cts = (wide_ct(du_ref), wide_ct(dw_ref), wide_ct(dqg_ref), wide_ct(dkd_ref),
               jnp.stack([dat_ref[h, c] for c, h in idx]), jnp.stack([deg_ref[h, c] for c, h in idx]))
        _, vjp = jax.vjp(_wy_fn, *args)
        dq, dk, dv, dgc, dgr, dbc = vjp(cts)
        for b, (c, h) in enumerate(idx):
            dq_ref[rows(c), lanes(h)] = dq[b]
            dk_ref[rows(c), lanes(h)] = dk[b]
            dv_ref[rows(c), lanes(h)] = dv[b]
            dgc_ref[h, rows(c), :] = dgc[b]
            dgr_ref[h, c] = dgr[b]
            dbc_ref[h, rows(c), :] = dbc[b]

    cblk = lambda o: pl.BlockSpec((cb * CHUNK, wd), lambda i: (i, o))
    col = pl.BlockSpec((ng, cb * CHUNK, 1), lambda i: (0, i, 0))
    rowv = pl.BlockSpec((ng, cb, 1, CHUNK), lambda i: (0, i, 0, 0))
    cshape = jax.ShapeDtypeStruct((ng, t, 1), F32)
    return pl.pallas_call(
        wy_body, name="gdn_wy_bwd", grid=(nch // cb,),
        in_specs=[cblk(0), cblk(1), cblk(2), col, rowv, col, cblk(0), cblk(0), cblk(0), cblk(0),
                  pl.BlockSpec((ng, cb, CHUNK, CHUNK), lambda i: (0, i, 0, 0)),
                  pl.BlockSpec((ng, cb, 1, HEAD_DIM), lambda i: (0, i, 0, 0))],
        out_specs=[cblk(0), cblk(0), cblk(0), col, rowv, col],
        out_shape=[wide, wide, wide, cshape, jax.ShapeDtypeStruct((ng, nch, 1, CHUNK), F32), cshape],
        compiler_params=_cparams(("parallel",)),
    )(qkv, qkv, qkv, gcol, grow, bcol, *dwy)


def _swiglu_fn(gate, up):
    return _silu(gate) * up


FFN_TN = 256


def _ffn_up(n2, wgu4):
    _, d, w = wgu4.shape
    t = n2.shape[0]
    tn = _tile(w, FFN_TN)
    nb = w // tn

    def body(a_ref, b_ref, gu_ref, act_ref):
        av = a_ref[...]
        gate = jnp.dot(av, b_ref[0], preferred_element_type=F32)
        up = jnp.dot(av, b_ref[1], preferred_element_type=F32)
        gu_ref[0] = gate.astype(BF16)
        gu_ref[1] = up.astype(BF16)
        act_ref[...] = _swiglu_fn(gate, up).astype(BF16)

    return pl.pallas_call(
        body, name="ffn_up", grid=(2, nb),
        in_specs=[pl.BlockSpec((t, d), lambda j, l: (0, 0)), pl.BlockSpec((2, d, tn), lambda j, l: (j, 0, l))],
        out_specs=[pl.BlockSpec((2, t, tn), lambda j, l: (j, 0, l)),
                   pl.BlockSpec((t, tn), lambda j, l: (0, j * nb + l))],
        out_shape=[jax.ShapeDtypeStruct((4, t, w), BF16), jax.ShapeDtypeStruct((t, 2 * w), BF16)],
        compiler_params=_cparams(("parallel", "parallel")),
    )(n2, wgu4)


def _ffn_dact(dh2, wd, gu, after):
    _, t, w = gu.shape
    d = dh2.shape[1]
    tn = _tile(w, FFN_TN)
    nb = w // tn

    def body(a_ref, b_ref, gu_ref, _, o_ref):
        dact = lax.dot_general(a_ref[...], b_ref[...], (((1,), (1,)), ((), ())), preferred_element_type=F32)
        _, vjp = jax.vjp(_swiglu_fn, gu_ref[0].astype(F32), gu_ref[1].astype(F32))
        dg, du = vjp(dact)
        o_ref[0] = dg.astype(BF16)
        o_ref[1] = du.astype(BF16)

    pair = pl.BlockSpec((2, t, tn), lambda j, l: (j, 0, l))
    return pl.pallas_call(
        body, name="ffn_dact", grid=(2, nb),
        in_specs=[pl.BlockSpec((t, d), lambda j, l: (0, 0)), pl.BlockSpec((tn, d), lambda j, l: (j * nb + l, 0)),
                  pair, pl.BlockSpec(after.shape, lambda j, l: (0, 0))],
        out_specs=pair, out_shape=jax.ShapeDtypeStruct(gu.shape, BF16),
        compiler_params=_cparams(("parallel", "parallel")),
    )(dh2, wd, gu, after)


def _loss_head(h2, target):
    t, d = h2.shape
    tr = _tile(t, 256, 8)

    def body(h_ref, t_ref, l_ref, d_ref, db_ref):
        @pl.when(pl.program_id(0) == 0)
        def _():
            l_ref[...] = jnp.zeros_like(l_ref)

        err = h_ref[...] - t_ref[...]
        d_ref[...] = err * (1.0 / d)
        db_ref[...] = (err * (1.0 / d)).astype(BF16)
        part = 0.5 * jnp.sum(jnp.mean(err * err, axis=-1, keepdims=True), axis=0, keepdims=True)
        lane = lax.broadcasted_iota(jnp.int32, (8, HEAD_DIM), 1)
        row = lax.broadcasted_iota(jnp.int32, (8, HEAD_DIM), 0)
        l_ref[...] += jnp.where((lane == 0) & (row == 0), part, 0.0)

    blk = pl.BlockSpec((tr, d), lambda r: (r, 0))
    return pl.pallas_call(
        body, name="loss_head", grid=(t // tr,), in_specs=[blk, blk],
        out_specs=[pl.BlockSpec((8, HEAD_DIM), lambda r: (0, 0)), blk, blk],
        out_shape=[jax.ShapeDtypeStruct((8, HEAD_DIM), F32), jax.ShapeDtypeStruct((t, d), F32),
                   jax.ShapeDtypeStruct((t, d), BF16)],
        compiler_params=_cparams(("arbitrary",)),
    )(h2, target)


def _adamw(w, g, m, v, *, g_fn=None, name):
    r, c = w.shape
    tr = _tile(r, max(8, (1 << 19) // c // 8 * 8), 8)

    def body(w_ref, g_ref, m_ref, v_ref, go_ref, d_ref, mo_ref, vo_ref):
        gr = g_ref[...] if g_fn is None else g_fn(g_ref[...])
        mn = ADAM_B1 * m_ref[...] + (1.0 - ADAM_B1) * gr
        vn = ADAM_B2 * v_ref[...] + (1.0 - ADAM_B2) * (gr * gr)
        m_hat = mn / (1.0 - ADAM_B1 ** ADAM_STEP)
        v_hat = vn / (1.0 - ADAM_B2 ** ADAM_STEP)
        go_ref[...] = gr
        d_ref[...] = -ADAM_LR * (m_hat / (jnp.sqrt(v_hat) + ADAM_EPS) + ADAM_WD * w_ref[...])
        mo_ref[...] = mn
        vo_ref[...] = vn

    blk = pl.BlockSpec((tr, c), lambda i: (i, 0))
    gblk = pl.BlockSpec((tr, g.shape[1]), lambda i: (i, 0))
    return pl.pallas_call(
        body, name=name, grid=(r // tr,), in_specs=[blk, gblk, blk, blk], out_specs=[blk] * 4,
        out_shape=[jax.ShapeDtypeStruct((r, c), F32)] * 4,
        compiler_params=_cparams(("parallel",)),
    )(w, g, m, v)


class _Layout:
    def __init__(self, d):
        nh = d // HEAD_DIM
        self.nm = N_MEM_HEADS
        self.nf = (nh - self.nm) // 2
        self.ng = nh - self.nm - self.nf
        nf, ng, nm = self.nf, self.ng, self.nm
        self.o_fq, self.o_fk, self.o_fv = 0, nf, 2 * nf
        self.o_gq = 3 * nf
        self.o_gz = 3 * nf + 3 * ng
        self.o_mq = 3 * nf + 4 * ng
        self.o_sm = self.o_mq + nm
        self.blocks = -(-(self.o_sm + 1) // 8) * 8
        self.cols = self.blocks * HEAD_DIM
        hd = HEAD_DIM
        sizes = [nf * hd, nf * hd, nf * hd, nf, 3 * ng * hd, ng * hd, ng, ng, nm * hd]
        starts = [sum(sizes[:i]) for i in range(len(sizes))]
        self.ref = list(zip(starts, sizes))
        self.in_cols = sum(sizes)

    def regroup(self, w):
        part = lambda i: w[:, self.ref[i][0]:self.ref[i][0] + self.ref[i][1]]
        pieces = [part(0), part(1), part(2), part(4), part(5), part(8), part(3), part(6), part(7)]
        pad = self.cols - self.in_cols
        return jnp.concatenate(pieces + [jnp.zeros((w.shape[0], pad), w.dtype)], axis=1)

    def ungroup(self, g):
        hd, nf, ng, nm = HEAD_DIM, self.nf, self.ng, self.nm
        sm = self.o_sm * hd
        return jnp.concatenate([
            g[:, :3 * nf * hd], g[:, sm:sm + nf], g[:, self.o_gq * hd:self.o_gz * hd],
            g[:, self.o_gz * hd:self.o_mq * hd], g[:, sm + nf:sm + nf + ng], g[:, sm + nf + ng:sm + nf + 2 * ng],
            g[:, self.o_mq * hd:self.o_sm * hd]], axis=1)


def _lane_row(pieces):
    row = jnp.zeros((1, HEAD_DIM), F32)
    for off, a in pieces:
        row = lax.dynamic_update_slice(row, a.astype(F32), (0, off))
    return row


def _local_step(x, mem, target, prefetch, weights, reducer, sp):
    t, d = x.shape
    lay = _Layout(d)
    nf, ng, nm, hd = lay.nf, lay.ng, lay.nm, HEAD_DIM
    nch = t // CHUNK
    tq = _tile(t, 256)
    tk = tq

    u = _norm_fwd(x, 0, sp["norm_mix"], 1, d, BF16, name="norm_mix_fwd")
    prefetch("in", u)
    (win,) = weights("in", u)
    prefetch("mixer", win)
    p = _mm(u, win, name="mm_in")
    wmkv, conv_taps = weights("mixer", p)
    sp = dict(sp, gdn_conv=conv_taps)
    pa = _lane_row([(nf, sp["gdn_a_log"])])
    pb = _lane_row([(0, sp["fox_f_bias"]), (nf, sp["gdn_dt_bias"])])
    vals, csum = _small_fwd(p, lay.o_sm, pa, pb, nf, ng)

    c_t = csum[:, :nf].T
    cc, cr = c_t.reshape(nf, t, 1), c_t.reshape(nf, t // tk, 1, tk)
    fq = _norm_fwd(p, lay.o_fq, sp["fox_q_norm"], nf, hd, BF16, name="fox_qnorm_fwd")
    fk = _norm_fwd(p, lay.o_fk, sp["fox_k_norm"], nf, hd, BF16, name="fox_knorm_fwd")
    fv = p[:, lay.o_fv * hd:(lay.o_fv + nf) * hd].astype(BF16)
    o_fox, lse, mix = _fox_fwd(fq, fk, fv, cc, cr, nf, tq, tk, d)

    qkv = _conv_fwd(p, lay.o_gq, sp["gdn_conv"], ng)
    g_t, b_t = vals[:, nf:nf + ng].T, vals[:, nf + ng:nf + 2 * ng].T
    gcol, grow, bcol = g_t.reshape(ng, t, 1), g_t.reshape(ng, nch, 1, CHUNK), b_t.reshape(ng, t, 1)
    o_g, states = _gdn_fwd(qkv, gcol, grow, bcol, ng)
    mix = _norm_fwd(o_g, 0, sp["gdn_out_norm"], ng, hd, BF16, z=p, zoff=lay.o_gz, into=mix, into_off=nf,
                    name="gdn_out_fwd")
    prefetch("out", mix)

    mem_n = _norm_fwd(mem, 0, sp["mem_norm"], 1, d, BF16, name="mem_norm_fwd")
    mkv = _mm(mem_n, wmkv, name="mm_memkv")
    mix = _mem_fwd(p, lay.o_mq, mkv, sp["mem_q_norm"], sp["mem_k_norm"], tq, mix, nf + ng)
    prefetch("gate_up", mix)
    (wout,) = weights("out", mix)
    h1 = _mm(mix, wout, res=x, name="mm_out")
    n2 = _norm_fwd(h1, 0, sp["norm_ffn"], 1, d, BF16, name="norm_ffn_fwd")
    (wgu,) = weights("gate_up", n2)
    wgu4 = wgu.reshape(4, d, -1)
    gu, act = _ffn_up(n2, wgu4)
    prefetch("down", act)
    (wd,) = weights("down", act)
    h2 = _mm(act, wd, res=h1, name="mm_down")
    loss_blk, dh2, dh2_b = _loss_head(h2, target)

    g = {}
    token = reducer.pair("w_down", _mm(act, dh2_b, ta=True, out_dtype=BF16, name="mm_dw_down"))
    dgu = _ffn_dact(dh2_b, wd, gu, token)
    dw_gate_up = _mm(n2, dgu, ta=True, stack="out", out_dtype=BF16, name="mm_dw_gate_up").reshape(wgu.shape)
    token = reducer.pair("w_gate_up", dw_gate_up)
    dn2 = _mm(dgu, wgu4, tb=True, stack="sum", after=token, name="mm_dn2")
    token = reducer.ship("ffn", ["w_down", "w_gate_up"], dn2)
    dh1, g["norm_ffn"] = _norm_bwd(h1, 0, sp["norm_ffn"] + token[0, 0], dn2, 0, 1, d, res=dh2,
                                   name="norm_ffn_bwd")
    token = reducer.pair("w_out", _mm(mix, dh1, ta=True, out_dtype=BF16, name="mm_dw_out"))
    dmix = _mm(dh1, wout, tb=True, after=token, name="mm_dmix")

    dmq, dmk, dmv, g["mem_q_norm"], g["mem_k_norm"] = _mem_bwd(
        p, lay.o_mq, mkv, sp["mem_q_norm"], sp["mem_k_norm"], dmix, nf + ng, tq)
    dmkv = jnp.concatenate([dmk, dmv], axis=1)
    token = reducer.pair("w_mem_kv", _mm(mem_n, dmkv, ta=True, out_dtype=BF16, name="mm_dw_memkv"))
    dmem_n = _mm(dmkv, wmkv, tb=True, after=token, name="mm_dmem")
    token = reducer.ship("mix", ["w_out", "w_mem_kv"], dmem_n)
    _, g["mem_norm"] = _norm_bwd(mem, 0, sp["mem_norm"], dmem_n, 0, 1, d, name="mem_norm_bwd")

    do_g, dgz, g["gdn_out_norm"] = _norm_bwd(o_g, 0, sp["gdn_out_norm"] + token[0, 0], dmix, nf, ng, hd, z=p,
                                             zoff=lay.o_gz, name="gdn_out_bwd")
    dq, dk, dv, dgc, dgr, dbc = _gdn_bwd(qkv, gcol, grow, bcol, states, do_g, ng)
    dgqkv, g["gdn_conv"] = _conv_bwd(p, lay.o_gq, sp["gdn_conv"], (dq, dk, dv), ng)
    dg_t = dgc.reshape(ng, t) + dgr.reshape(ng, t)
    db_t = dbc.reshape(ng, t)

    dfq_n, dfk_n, dfv, dcc, dcr = _fox_bwd(fq, fk, fv, cc, cr, o_fox, lse, dmix, nf, tq, tk)
    dfq, g["fox_q_norm"] = _norm_bwd(p, lay.o_fq, sp["fox_q_norm"], dfq_n, 0, nf, hd, name="fox_qnorm_bwd")
    dfk, g["fox_k_norm"] = _norm_bwd(p, lay.o_fk, sp["fox_k_norm"], dfk_n, 0, nf, hd, name="fox_knorm_bwd")
    dc_t = dcc.reshape(nf, t) + dcr.reshape(nf, t)

    lanes_left = hd - nf - 2 * ng
    dvals = jnp.concatenate([jnp.zeros((t, nf), F32), dg_t.T, db_t.T, jnp.zeros((t, lanes_left), F32)], axis=1)
    dcsum = jnp.concatenate([dc_t.T, jnp.zeros((t, hd - nf), F32)], axis=1)
    dsm, dpa, dpb = _small_bwd(p, lay.o_sm, pa, pb, dvals, dcsum, nf, ng)
    g["fox_f_bias"] = dpb[:, :nf]
    g["gdn_dt_bias"] = dpb[:, nf:nf + ng]
    g["gdn_a_log"] = dpa[:, nf:nf + ng]

    pad = jnp.zeros((t, lay.cols - (lay.o_sm + 1) * hd), F32)
    dp = jnp.concatenate([dfq, dfk, dfv, dgqkv, dgz, dmq, dsm, pad], axis=1).astype(BF16)
    token = reducer.start("in", {"w_in": _mm(u, dp, ta=True, out_dtype=BF16, name="mm_dw_in")})
    du = _mm(dp, win, tb=True, after=token, name="mm_du")
    dx, g["norm_mix"] = _norm_bwd(x, 0, sp["norm_mix"], du, 0, 1, d, res=dh1, name="norm_mix_bwd")
    return loss_blk, dx, g


ANY = pl.BlockSpec(memory_space=pl.ANY)


def _me():
    x, y, c = lax.axis_index("x"), lax.axis_index("y"), lax.axis_index("c")
    chips = [(1 - x, y), (x, 1 - y), (1 - x, 1 - y)]
    return x, y, c, chips


def _slot(axis, k):
    return k if axis == 0 else 2 * (k % 2) + k // 2


def _slab(ref, axis, rows, cols, k, h):
    half = rows // 2
    return ref.at[pl.ds(_slot(axis, k) * rows + h * half, half), :]


def _remote(src, dst, send_sem, recv_sem, dev):
    return pltpu.make_async_remote_copy(src_ref=src, dst_ref=dst, send_sem=send_sem, recv_sem=recv_sem,
                                        device_id=dev, device_id_type=MESH)


HBM = pl.BlockSpec(memory_space=pltpu.HBM)
SEM = pl.BlockSpec(memory_space=pltpu.SEMAPHORE)
SPLIT = pltpu.CompilerParams(has_side_effects=pltpu.SideEffectType.DATAFLOW_SIDE_EFFECTING)
TOKEN = jax.ShapeDtypeStruct((8, HEAD_DIM), F32)


def _in_hbm(v):
    return pltpu.with_memory_space_constraint(v, pltpu.HBM)


def _cast_place(shard, axis, name, col_fn=None, out_cols=None):
    r, c = shard.shape
    oc = out_cols or c
    tr = _tile(r, 512 if col_fn is None else 64, 16)
    tc = _tile(c, 2048) if col_fn is None else c
    otc = tc if col_fn is None else oc
    nb = r // tr
    chip = 2 * lax.axis_index("x") + lax.axis_index("y")
    slot = jnp.reshape(_slot(axis, chip), (1,)).astype(jnp.int32)

    def body(slot_ref, x_ref, o_ref):
        x = x_ref[...]
        o_ref[...] = (x if col_fn is None else col_fn(x)).astype(BF16)

    return pl.pallas_call(
        body, name=name,
        grid_spec=pltpu.PrefetchScalarGridSpec(
            num_scalar_prefetch=1, grid=(nb, c // tc),
            in_specs=[pl.BlockSpec((tr, tc), lambda i, l, s: (i, l))],
            out_specs=pl.BlockSpec((tr, otc), lambda i, l, s: (s[0] * nb + i, l))),
        out_shape=jax.ShapeDtypeStruct((4 * r, oc), BF16),
        compiler_params=_cparams(("parallel", "parallel")),
    )(slot, shard)


def _gather_start(bufs, axes, shapes, groups, name):
    n = len(bufs)

    def body(*refs):
        dst = refs[n:2 * n]
        sems = refs[2 * n:2 * n + 2 * len(groups)]
        token = refs[-1]
        x, y, c, chips = _me()
        k = 2 * x + y
        for gi, ws in enumerate(groups):
            for i, w in enumerate(ws):
                r, cl = shapes[w]
                place = _slab(dst[w], axes[w], r, cl, k, c)
                for j, (px, py) in enumerate(chips):
                    _remote(place, place, sems[2 * gi].at[3 * i + j], sems[2 * gi + 1].at[3 * i + j],
                            (px, py, c)).start()
        token[...] = jnp.zeros_like(token)

    sem_shapes = [pltpu.SemaphoreType.DMA((3 * len(ws),)) for ws in groups for _ in range(2)]
    outs = pl.pallas_call(
        body, name=name, in_specs=[HBM] * n,
        out_specs=[HBM] * n + [SEM] * len(sem_shapes) + [pl.BlockSpec(memory_space=pltpu.VMEM)],
        out_shape=[pltpu.HBM(b.shape, b.dtype) for b in bufs] + sem_shapes + [TOKEN],
        input_output_aliases={w: w for w in range(n)}, compiler_params=SPLIT,
    )(*[_in_hbm(b) for b in bufs])
    sems = outs[n:-1]
    return outs[:n], [(sems[2 * g], sems[2 * g + 1]) for g in range(len(groups))], outs[-1]


def _gather_wait(bufs, axes, shapes, sems, after, name):
    n = len(bufs)

    def body(*refs):
        send_sems, recv_sems = refs[n], refs[n + 1]
        dst = refs[n + 3:]
        x, y, c, chips = _me()
        k = 2 * x + y
        for i in range(n):
            r, cl = shapes[i]
            for j, (px, py) in enumerate(chips):
                got = _slab(dst[i], axes[i], r, cl, 2 * px + py, c)
                _remote(got, got, send_sems.at[3 * i + j], recv_sems.at[3 * i + j], (px, py, c)).wait_recv()
        for i in range(n):
            r, cl = shapes[i]
            mine = _slab(dst[i], axes[i], r, cl, k, c)
            for j, (px, py) in enumerate(chips):
                _remote(mine, mine, send_sems.at[3 * i + j], recv_sems.at[3 * i + j], (px, py, c)).wait_send()

    return pl.pallas_call(
        body, name=name, in_specs=[HBM] * n + [SEM, SEM, ANY], out_specs=[HBM] * n,
        out_shape=[pltpu.HBM(b.shape, b.dtype) for b in bufs],
        input_output_aliases={i: i for i in range(n)}, compiler_params=SPLIT,
    )(*bufs, sems[0], sems[1], after)


def _gather_forward(bufs, axes, shapes, name):
    n = len(bufs)

    def body(*refs):
        dst = refs[n:2 * n]
        send_sems, recv_sems = refs[2 * n:]
        x, y, c, chips = _me()
        sibling = (x, y, 1 - c)
        sends = []
        for i in range(n):
            r, cl = shapes[i]
            for j, (px, py) in enumerate(chips):
                got = _slab(dst[i], axes[i], r, cl, 2 * px + py, c)
                cp = _remote(got, got, send_sems.at[3 * i + j], recv_sems.at[3 * i + j], sibling)
                cp.start()
                sends.append(cp)
        for i in range(n):
            r, cl = shapes[i]
            for j, (px, py) in enumerate(chips):
                got = _slab(dst[i], axes[i], r, cl, 2 * px + py, 1 - c)
                _remote(got, got, send_sems.at[3 * i + j], recv_sems.at[3 * i + j], sibling).wait_recv()
        for cp in sends:
            cp.wait_send()

    return pl.pallas_call(
        body, name=name, in_specs=[ANY] * n, out_specs=[ANY] * n,
        out_shape=[jax.ShapeDtypeStruct(b.shape, b.dtype) for b in bufs],
        input_output_aliases={i: i for i in range(n)},
        scratch_shapes=[pltpu.SemaphoreType.DMA((3 * n,)), pltpu.SemaphoreType.DMA((3 * n,))],
    )(*bufs)


def _split_start(name, arrays, geometry, count):
    n = len(arrays)

    def body(*refs):
        send, recv, token = refs[2 * n:]
        for i, (src, dst, _, dev) in enumerate(geometry(refs[n:2 * n])):
            _remote(src, dst, send.at[i], recv.at[i], dev).start()
        token[...] = jnp.zeros_like(token)

    sem = pltpu.SemaphoreType.DMA((count,))
    outs = pl.pallas_call(
        body, name=name, in_specs=[HBM] * n,
        out_specs=[HBM] * n + [SEM, SEM, pl.BlockSpec(memory_space=pltpu.VMEM)],
        out_shape=[pltpu.HBM(v.shape, v.dtype) for v in arrays] + [sem, sem, TOKEN],
        input_output_aliases={i: i for i in range(n)}, compiler_params=SPLIT,
    )(*[_in_hbm(v) for v in arrays])
    return list(outs[:n]), (outs[n], outs[n + 1]), outs[-1]


def _split_wait(name, arrays, sems, after, geometry):
    n = len(arrays)

    def body(*refs):
        send, recv = refs[n], refs[n + 1]
        copies = geometry(refs[n + 3:])
        for i, (_, _, land, dev) in enumerate(copies):
            _remote(land, land, send.at[i], recv.at[i], dev).wait_recv()
        for i, (src, _, _, dev) in enumerate(copies):
            _remote(src, src, send.at[i], recv.at[i], dev).wait_send()

    return list(pl.pallas_call(
        body, name=name, in_specs=[HBM] * n + [SEM, SEM, ANY], out_specs=[HBM] * n,
        out_shape=[pltpu.HBM(v.shape, v.dtype) for v in arrays],
        input_output_aliases={i: i for i in range(n)}, compiler_params=SPLIT,
    )(*arrays, sems[0], sems[1], after))


def _forward_geometry(axes, shapes):
    def geometry(bufs):
        x, y, c, chips = _me()
        out = []
        for i, buf in enumerate(bufs):
            r, cl = shapes[i]
            for px, py in chips:
                got = _slab(buf, axes[i], r, cl, 2 * px + py, c)
                out.append((got, got, _slab(buf, axes[i], r, cl, 2 * px + py, 1 - c), (x, y, 1 - c)))
        return out
    return geometry


def _pair_geometry(axes, shapes):
    def geometry(refs):
        n = len(refs) // 2
        x, y, c, _ = _me()
        out = []
        for w in range(n):
            r, cl = shapes[w]
            for j in range(4):
                land = refs[n + w].at[j]
                out.append((_slab(refs[w], axes[w], r, cl, j, 1 - c), land, land, (x, y, 1 - c)))
        return out
    return geometry


def _pair_exchange(fulls, axes, shapes, tag):
    n = len(fulls)

    def body(*refs):
        src, dst = refs[:n], refs[n:2 * n]
        send_sems, recv_sems = refs[2 * n:]
        x, y, c, _ = _me()
        sibling = (x, y, 1 - c)
        cps = []
        for w in range(n):
            r, cl = shapes[w]
            for j in range(4):
                cp = _remote(_slab(src[w], axes[w], r, cl, j, 1 - c), dst[w].at[j],
                             send_sems.at[4 * w + j], recv_sems.at[4 * w + j], sibling)
                cp.start()
                cps.append(cp)
        for cp in cps:
            cp.wait()

    out_shape = [jax.ShapeDtypeStruct((4, r // 2, cl), f.dtype) for (r, cl), f in zip(shapes, fulls)]
    return pl.pallas_call(
        body, name="reduce_pair_exchange_" + tag, in_specs=[ANY] * n, out_specs=[ANY] * n, out_shape=out_shape,
        scratch_shapes=[pltpu.SemaphoreType.DMA((4 * n,)), pltpu.SemaphoreType.DMA((4 * n,))],
    )(*fulls)


def _chip_start(parts, tag):
    n = len(parts)

    def body(*refs):
        src, land = refs[2 * n:3 * n], refs[3 * n:4 * n]
        send_sems, recv_sems, token = refs[4 * n:]
        x, y, c, chips = _me()
        k = 2 * x + y
        for w in range(n):
            for j, (px, py) in enumerate(chips):
                _remote(src[w].at[2 * px + py], land[w].at[k], send_sems.at[3 * w + j], recv_sems.at[3 * w + j],
                        (px, py, c)).start()
        token[...] = jnp.zeros_like(token)

    lands = [lax.empty(p.shape, p.dtype) for p in parts]
    sem = pltpu.SemaphoreType.DMA((3 * n,))
    outs = pl.pallas_call(
        body, name="reduce_ici_start_" + tag, in_specs=[HBM] * (2 * n),
        out_specs=[HBM] * (2 * n) + [SEM, SEM, pl.BlockSpec(memory_space=pltpu.VMEM)],
        out_shape=[pltpu.HBM(p.shape, p.dtype) for p in parts + lands] + [sem, sem, TOKEN],
        input_output_aliases={i: i for i in range(2 * n)}, compiler_params=SPLIT,
    )(*[_in_hbm(v) for v in parts + lands])
    return outs[:n], outs[n:2 * n], outs[2 * n], outs[2 * n + 1], outs[-1]


def _chip_wait(parts, lands, send_sems, recv_sems, after, tag):
    n = len(parts)

    def body(*refs):
        send, recv = refs[2 * n], refs[2 * n + 1]
        src, land = refs[2 * n + 3:3 * n + 3], refs[3 * n + 3:]
        x, y, c, chips = _me()
        for w in range(n):
            for j, (px, py) in enumerate(chips):
                got = land[w].at[2 * px + py]
                _remote(got, got, send.at[3 * w + j], recv.at[3 * w + j], (px, py, c)).wait_recv()
        for w in range(n):
            for j, (px, py) in enumerate(chips):
                sent = src[w].at[2 * px + py]
                _remote(sent, sent, send.at[3 * w + j], recv.at[3 * w + j], (px, py, c)).wait_send()

    outs = pl.pallas_call(
        body, name="reduce_ici_wait_" + tag, in_specs=[HBM] * (2 * n) + [SEM, SEM, ANY], out_specs=[HBM] * (2 * n),
        out_shape=[pltpu.HBM(p.shape, p.dtype) for p in parts + lands],
        input_output_aliases={i: i for i in range(2 * n)}, compiler_params=SPLIT,
    )(*parts, *lands, send_sems, recv_sems, after)
    chip = 2 * lax.axis_index("x") + lax.axis_index("y")
    return [lax.dynamic_update_slice(s, lax.dynamic_index_in_dim(p, chip, 0, keepdims=True), (chip, 0, 0))
            for p, s in zip(outs[:n], outs[n:])]


def _half_swap(halves, tag):
    n = len(halves)
    core = lax.axis_index("c")
    bufs = [lax.dynamic_update_slice(lax.empty((2,) + h.shape, h.dtype), h[None], (core, 0, 0)) for h in halves]

    def body(*refs):
        dst = refs[n:2 * n]
        send_sems, recv_sems = refs[2 * n:]
        x, y, c, _ = _me()
        sibling = (x, y, 1 - c)
        cps = []
        for w in range(n):
            cp = _remote(dst[w].at[c], dst[w].at[c], send_sems.at[w], recv_sems.at[w], sibling)
            cp.start()
            cps.append(cp)
        for w in range(n):
            other = dst[w].at[1 - c]
            _remote(other, other, send_sems.at[w], recv_sems.at[w], sibling).wait_recv()
        for cp in cps:
            cp.wait_send()

    outs = pl.pallas_call(
        body, name="reduce_half_swap_" + tag, in_specs=[ANY] * n, out_specs=[ANY] * n,
        out_shape=[jax.ShapeDtypeStruct(b.shape, b.dtype) for b in bufs],
        input_output_aliases={w: w for w in range(n)},
        scratch_shapes=[pltpu.SemaphoreType.DMA((n,)), pltpu.SemaphoreType.DMA((n,))],
    )(*bufs)
    return [o.reshape(2 * o.shape[1], o.shape[2]) for o in outs]


def _add_parts(full, axis, rows, sib, name):
    _, r, c = sib.shape
    tr, tc = _tile(r, 256, 16), _tile(c, 2048)
    nb = r // tr
    core = jnp.reshape(lax.axis_index("c"), (1,)).astype(jnp.int32)

    def body(c_ref, a_ref, b_ref, o_ref):
        o_ref[0] = (a_ref[...].astype(F32) + b_ref[0].astype(F32)).astype(BF16)

    blk = pl.BlockSpec((1, tr, tc), lambda j, i, l, cr: (j, i, l))
    return pl.pallas_call(
        body, name=name,
        grid_spec=pltpu.PrefetchScalarGridSpec(
            num_scalar_prefetch=1, grid=(4, nb, c // tc),
            in_specs=[pl.BlockSpec((tr, tc), lambda j, i, l, cr: ((_slot(axis, j) * 2 + cr[0]) * nb + i, l)), blk],
            out_specs=blk),
        out_shape=jax.ShapeDtypeStruct(sib.shape, BF16),
        compiler_params=_cparams(("parallel", "parallel", "parallel")),
    )(core, full, sib)


def _sum_slots(a, name):
    _, r, c = a.shape
    tr, tc = _tile(r, 256, 8), _tile(c, 2048)

    def body(a_ref, o_ref):
        v = a_ref[...].astype(F32)
        o_ref[...] = ((v[0] + v[1]) + v[2]) + v[3]

    return pl.pallas_call(
        body, name=name, grid=(r // tr, c // tc),
        in_specs=[pl.BlockSpec((4, tr, tc), lambda i, l: (0, i, l))],
        out_specs=pl.BlockSpec((tr, tc), lambda i, l: (i, l)),
        out_shape=jax.ShapeDtypeStruct((r, c), F32),
        compiler_params=_cparams(("parallel", "parallel")),
    )(a)


class _Reducer:
    def __init__(self, spec):
        self.spec = spec
        self.paired = {}
        self.pending = []

    def pair(self, name, full):
        ax, shp = self.spec[name]
        land = lax.empty((4, shp[0] // 2, shp[1]), full.dtype)
        arrays, sems, token = _split_start("reduce_pair_start_" + name, [full, land], _pair_geometry([ax], [shp]), 4)
        self.paired[name] = (arrays, sems)
        return token

    def ship(self, tag, names, after):
        parts = []
        for n in names:
            ax, shp = self.spec[n]
            arrays, sems = self.paired.pop(n)
            full, sib = _split_wait("reduce_pair_wait_" + n, arrays, sems, after, _pair_geometry([ax], [shp]))
            parts.append(_add_parts(full, ax, shp[0], sib, name=f"reduce_add_{n}"))
        parts, lands, send, recv, token = _chip_start(parts, tag)
        self.pending.append((tag, names, parts, lands, send, recv))
        return token

    def start(self, tag, grads):
        names = list(grads)
        fulls, axes = [grads[n] for n in names], [self.spec[n][0] for n in names]
        shapes = [self.spec[n][1] for n in names]
        from_sibling = _pair_exchange(fulls, axes, shapes, tag)
        parts = [_add_parts(f, a, r, s, name=f"reduce_add_{n}")
                 for n, f, a, (r, cl), s in zip(names, fulls, axes, shapes, from_sibling)]
        parts, lands, send, recv, token = _chip_start(parts, tag)
        self.pending.append((tag, names, parts, lands, send, recv))
        return token

    def finish(self, after):
        out = {}
        for tag, names, parts, lands, send, recv in self.pending:
            slots = _chip_wait(parts, lands, send, recv, after, tag)
            halves = [_sum_slots(s, name=f"reduce_sum_{n}") for n, s in zip(names, slots)]
            out.update(zip(names, _half_swap(halves, tag)))
        return out


def _allreduce_small(pack):
    rows = pack.shape[0]

    def body(p_ref, o_ref, slots, send_sems, recv_sems):
        x, y, c, _ = _me()
        me = 4 * x + 2 * y + c
        slots[me] = p_ref[...]
        cps = []
        for r in range(1, 8):
            peer = (x ^ (r >> 2), y ^ ((r >> 1) & 1), c ^ (r & 1))
            cp = _remote(p_ref, slots.at[me], send_sems.at[r - 1], recv_sems.at[r - 1], peer)
            cp.start()
            cps.append(cp)
        for r in range(1, 8):
            frm = me ^ r
            _remote(slots.at[frm], slots.at[frm], send_sems.at[r - 1], recv_sems.at[r - 1], (x, y, c)).wait_recv()
        for cp in cps:
            cp.wait_send()
        acc = slots[0]
        for s in range(1, 8):
            acc = acc + slots[s]
        o_ref[...] = acc

    vm = pl.BlockSpec(memory_space=pltpu.VMEM)
    return pl.pallas_call(
        body, name="allreduce_small", in_specs=[vm], out_specs=vm,
        out_shape=jax.ShapeDtypeStruct(pack.shape, F32),
        scratch_shapes=[pltpu.VMEM((8, rows, HEAD_DIM), F32), pltpu.SemaphoreType.DMA((7,)),
                        pltpu.SemaphoreType.DMA((7,))],
    )(pack)


_ROWS = ["norm_mix", "norm_ffn", "mem_norm", "fox_q_norm", "fox_k_norm", "gdn_out_norm", "mem_q_norm",
         "mem_k_norm", "fox_f_bias", "gdn_a_log", "gdn_dt_bias"]


def _pack_rows(vals):
    out = []
    for name in _ROWS:
        v = vals[name].reshape(-1)
        n = -(-v.shape[0] // HEAD_DIM) * HEAD_DIM
        out.append(jnp.pad(v, (0, n - v.shape[0])).reshape(-1, HEAD_DIM))
    return jnp.concatenate(out, axis=0)


def _unpack_rows(pack, like):
    out, r = {}, 0
    for name in _ROWS:
        n = like[name].shape[-1]
        nr = -(-n // HEAD_DIM)
        out[name] = pack[r:r + nr].reshape(1, -1)[:, :n]
        r += nr
    return out, r


def kernel(x, mem, norm_mix, w_in, fox_f_bias, fox_q_norm, fox_k_norm, gdn_conv, gdn_a_log, gdn_dt_bias, gdn_out_norm, mem_norm, w_mem_kv, mem_q_norm, mem_k_norm, w_out, norm_ffn, w_gate_up, w_down, loss_target, m_norm_mix, m_w_in, m_fox_f_bias, m_fox_q_norm, m_fox_k_norm, m_gdn_conv, m_gdn_a_log, m_gdn_dt_bias, m_gdn_out_norm, m_mem_norm, m_w_mem_kv, m_mem_q_norm, m_mem_k_norm, m_w_out, m_norm_ffn, m_w_gate_up, m_w_down, v_norm_mix, v_w_in, v_fox_f_bias, v_fox_q_norm, v_fox_k_norm, v_gdn_conv, v_gdn_a_log, v_gdn_dt_bias, v_gdn_out_norm, v_mem_norm, v_w_mem_kv, v_mem_q_norm, v_mem_k_norm, v_w_out, v_norm_ffn, v_w_gate_up, v_w_down):
    a = dict(locals())
    d = x.shape[-1]
    lay = _Layout(d)
    chip = 2 * lax.axis_index("x") + lax.axis_index("y")
    small = {n: a[n] for n in _ROWS}
    big = ["w_in", "w_mem_kv", "w_out", "w_gate_up", "w_down"]
    axes = [0, 0, 0, 1, 0]

    conv_cols = gdn_conv.shape[-1]
    conv_n = CONV_WIDTH * conv_cols
    conv_rows = -(-conv_n // HEAD_DIM)
    conv_blk = jnp.pad(gdn_conv.reshape(-1), (0, 32 * HEAD_DIM - conv_n)).reshape(32, HEAD_DIM)
    axis_of = dict(zip(big, axes), conv=0)
    shape_of = {n: a[n].shape[1:] for n in big}
    shape_of["w_in"] = (w_in.shape[1], lay.cols)
    shape_of["conv"] = conv_blk.shape
    placed = {n: _cast_place(a[n][0], axis_of[n], "cast_" + n) for n in big[1:]}
    placed["w_in"] = _cast_place(w_in[0], 0, "cast_w_in", lay.regroup, lay.cols)
    placed["conv"] = lax.dynamic_update_slice(lax.empty((4 * 32, HEAD_DIM), F32), conv_blk, (chip * 32, 0))
    grouped = {"in": ["w_in"], "mixer": ["w_mem_kv", "conv"], "out": ["w_out"], "gate_up": ["w_gate_up"],
               "down": ["w_down"]}
    inflight = {}

    def start(tags, name):
        names = [n for t in tags for n in grouped[t]]
        bufs, sems, _ = _gather_start([placed[n] for n in names], [axis_of[n] for n in names],
                                      [shape_of[n] for n in names],
                                      [[names.index(n) for n in grouped[t]] for t in tags], name)
        for t, pair in zip(tags, sems):
            inflight[t] = ([bufs[names.index(n)] for n in grouped[t]], pair)

    start(["in"], "gather_ici_start_in")
    start(["mixer", "out", "gate_up", "down"], "gather_ici_start_rest")

    forwarding = {}

    def prefetch(tag, after):
        bufs, sem_pair = inflight.pop(tag)
        ax, shp = [axis_of[n] for n in grouped[tag]], [shape_of[n] for n in grouped[tag]]
        got = _gather_wait(bufs, ax, shp, sem_pair, after, "gather_ici_wait_" + tag)
        geometry = _forward_geometry(ax, shp)
        got, sems, _ = _split_start("gather_forward_start_" + tag, got, geometry, 3 * len(got))
        forwarding[tag] = (got, sems, geometry)

    def weights(tag, after):
        got, sems, geometry = forwarding.pop(tag)
        got = _split_wait("gather_forward_wait_" + tag, got, sems, after, geometry)
        if tag != "mixer":
            return got
        taps = got[1].reshape(4, 32 * HEAD_DIM)[:, :conv_n].reshape(4, CONV_WIDTH, conv_cols)
        return got[0], jnp.transpose(taps, (1, 0, 2)).reshape(CONV_WIDTH, 4 * conv_cols)

    sp = dict(small)
    reducer = _Reducer({n: (axis_of[n], shape_of[n]) for n in big})
    loss_blk, dx, g = _local_step(x[0], mem[0], loss_target[0], prefetch, weights, reducer, sp)

    gsmall = {n: g[n] for n in _ROWS}
    pack = jnp.concatenate([_pack_rows(gsmall), g["gdn_conv"].reshape(-1, HEAD_DIM), loss_blk], axis=0)
    pack = jnp.pad(pack, ((0, -pack.shape[0] % 8), (0, 0)))
    tot = _allreduce_small(pack)
    gs, r0 = _unpack_rows(tot, small)
    conv_g = tot[r0:r0 + CONV_WIDTH * 4 * conv_cols // HEAD_DIM].reshape(CONV_WIDTH, 4 * conv_cols)
    gs_conv = lax.dynamic_slice_in_dim(conv_g, chip * conv_cols, conv_cols, axis=1)
    loss = tot[r0 + CONV_WIDTH * 4 * conv_cols // HEAD_DIM, 0]
    reduced = reducer.finish(tot)

    out = {"loss": loss, "grad_x": dx[None]}
    for n, gsh in reduced.items():
        res = _adamw(a[n][0], gsh, a["m_" + n][0], a["v_" + n][0], g_fn=lay.ungroup if n == "w_in" else None,
                     name="adamw_" + n)
        for pre, r in zip(["grad_", "delta_", "new_m_", "new_v_"], res):
            out[pre + n] = r[None]
    conv_pad = lambda v: jnp.pad(v.reshape(-1), (0, conv_rows * HEAD_DIM - conv_n)).reshape(conv_rows, HEAD_DIM)
    packs = []
    for src, cv in [(small, gdn_conv), (gs, gs_conv), ({n: a["m_" + n] for n in _ROWS}, m_gdn_conv),
                    ({n: a["v_" + n] for n in _ROWS}, v_gdn_conv)]:
        packs.append(jnp.concatenate([_pack_rows(src), conv_pad(cv)], axis=0))
    res = _adamw(*packs, name="adamw_small")
    for pre, r in zip(["grad_", "delta_", "new_m_", "new_v_"], res):
        vals, r1 = _unpack_rows(r, small)
        for n in _ROWS:
            out[pre + n] = vals[n]
        out[pre + "gdn_conv"] = r[r1:r1 + conv_rows].reshape(-1)[:conv_n].reshape(gdn_conv.shape)
    names = ["norm_mix", "w_in", "fox_f_bias", "fox_q_norm", "fox_k_norm", "gdn_conv", "gdn_a_log", "gdn_dt_bias",
             "gdn_out_norm", "mem_norm", "w_mem_kv", "mem_q_norm", "mem_k_norm", "w_out", "norm_ffn", "w_gate_up",
             "w_down"]
    return (out["loss"], out["grad_x"], *[out[p + n] for p in ["grad_", "delta_", "new_m_", "new_v_"] for n in names])
```

```python
import functools
import math

import jax
import jax.numpy as jnp
from jax import lax
from jax.experimental import pallas as pl
from jax.experimental.pallas import tpu as pltpu

F32, BF16 = jnp.float32, jnp.bfloat16
HEAD_DIM = 128
CHUNK = 64
N_MEM_HEADS = 4
CONV_WIDTH = 4
NORM_EPS = 1e-6
ADAM_LR, ADAM_B1, ADAM_B2, ADAM_EPS, ADAM_WD, ADAM_STEP = 0.001, 0.9, 0.999, 1e-08, 0.01, 10
VMEM_LIMIT = 48 * 1024 * 1024
NEG = -1e30
MESH = pl.DeviceIdType.MESH


def _cparams(sem=None, **kw):
    if sem is not None:
        kw["dimension_semantics"] = sem
    return pltpu.CompilerParams(vmem_limit_bytes=VMEM_LIMIT, **kw)


def _tile(n, target, mult=128):
    best = None
    d = mult
    while d <= min(n, target):
        if n % d == 0:
            best = d
        d += mult
    return best if best is not None else n


def _dot(a, b, dims, hi):
    if a.ndim == 3:
        dn = (((dims[0][0] + 1,), (dims[1][0] + 1,)), ((0,), (0,)))
    else:
        dn = (dims, ((), ()))
    if hi is not None:
        return lax.dot_general(a, b, dn, precision=hi, preferred_element_type=F32)
    return lax.dot_general(a.astype(BF16), b.astype(BF16), dn, preferred_element_type=F32)


def _make_dots(hi, cotangent=None):
    @jax.custom_vjp
    def nn(a, b):
        return _dot(a, b, ((1,), (0,)), hi)

    @jax.custom_vjp
    def nt(a, b):
        return _dot(a, b, ((1,), (1,)), hi)

    @jax.custom_vjp
    def tn(a, b):
        return _dot(a, b, ((0,), (0,)), hi)

    bnn, bnt, btn = cotangent or (nn, nt, tn)
    nn.defvjp(lambda a, b: (nn(a, b), (a, b)), lambda r, g: (bnt(g, r[1]), btn(r[0], g)))
    nt.defvjp(lambda a, b: (nt(a, b), (a, b)), lambda r, g: (bnn(g, r[1]), btn(g, r[0])))
    tn.defvjp(lambda a, b: (tn(a, b), (a, b)), lambda r, g: (bnt(r[1], g), bnn(r[0], g)))
    return nn, nt, tn


_nn, _nt, _tn = _make_dots(None)
_nn_hi, _nt_hi, _tn_hi = _make_dots(lax.Precision.HIGHEST)
_nn_x3, _nt_x3, _tn_x3 = _make_dots(lax.Precision.HIGH, (_nn, _nt, _tn))


def _sigmoid(x):
    return 1.0 / (1.0 + jnp.exp(-x))


@jax.custom_vjp
def _softplus(x):
    return jnp.maximum(x, 0.0) + jnp.log(1.0 + jnp.exp(-jnp.abs(x)))


_softplus.defvjp(lambda x: (_softplus(x), x), lambda x, g: (g * _sigmoid(x),))


def _silu(x):
    return x * _sigmoid(x)


def _rms_fn(x, gain, z=None):
    y = x * lax.rsqrt(jnp.mean(x * x, axis=-1, keepdims=True) + NORM_EPS) * gain
    if z is not None:
        y = y * _silu(z)
    return y


def _mm(a, b, *, ta=False, tb=False, out_dtype=F32, res=None, stack=None, after=None, name):
    a2, b2 = a.shape[-2:], b.shape[-2:]
    ns = b.shape[0] if stack else 1
    m = a2[1] if ta else a2[0]
    k = a2[0] if ta else a2[1]
    n = b2[0] if tb else b2[1]
    assert k == (b2[1] if tb else b2[0])
    tm, tn, tk = _mm_tiles(m, n, k, ns if stack == "sum" else 1, a.dtype.itemsize, b.dtype.itemsize,
                           jnp.dtype(out_dtype).itemsize, res is not None)
    nk = k // tk
    single = nk == 1 and stack != "sum"
    dims = ((0 if ta else 1,), (1 if tb else 0,))
    if stack == "sum":
        order = lambda g0, g1, g2, g3: (g2, g0, g1, g3)
        grid = (m // tm, n // tn, ns, nk)
    else:
        order = lambda g0, g1, g2, g3: (g0, g1, g2, g3)
        grid = (ns, m // tm, n // tn, nk)

    def body(*refs):
        if after is not None:
            refs = refs[:2 + (res is not None)] + refs[3 + (res is not None):]
        if single:
            a_ref, b_ref = refs[:2]
            r = lax.dot_general(a_ref[...].astype(BF16), b_ref[...].astype(BF16), (dims, ((), ())),
                                preferred_element_type=F32)
            if res is not None:
                r = r + refs[2][...]
            refs[-1][...] = r.astype(out_dtype)
            return
        if res is None:
            a_ref, b_ref, o_ref, acc = refs
        else:
            a_ref, b_ref, r_ref, o_ref, acc = refs
        s, _, _, kk = order(*[pl.program_id(d) for d in range(4)])
        first = kk == 0
        last = kk == nk - 1
        if stack == "sum":
            first, last = first & (s == 0), last & (s == ns - 1)

        @pl.when(first)
        def _():
            acc[...] = jnp.zeros_like(acc)

        acc[...] += lax.dot_general(a_ref[...].astype(BF16), b_ref[...].astype(BF16), (dims, ((), ())),
                                    preferred_element_type=F32)

        @pl.when(last)
        def _():
            r = acc[...]
            if res is not None:
                r = r + r_ref[...]
            o_ref[...] = r.astype(out_dtype)

    def spec(shape, idx, stacked):
        if stacked:
            return pl.BlockSpec((None,) + shape, lambda *g: (order(*g)[0],) + idx(*order(*g)))
        return pl.BlockSpec(shape, lambda *g: idx(*order(*g)))

    a_spec = (spec((tk, tm), lambda s, i, j, kk: (kk, i), stack == "sum") if ta
              else spec((tm, tk), lambda s, i, j, kk: (i, kk), stack == "sum"))
    b_spec = (spec((tn, tk), lambda s, i, j, kk: (j, kk), bool(stack)) if tb
              else spec((tk, tn), lambda s, i, j, kk: (kk, j), bool(stack)))
    o_spec = spec((tm, tn), lambda s, i, j, kk: (i, j), stack == "out")
    ins, specs = [a, b], [a_spec, b_spec]
    if res is not None:
        ins.append(res)
        specs.append(o_spec)
    if after is not None:
        ins.append(after)
        specs.append(pl.BlockSpec(after.shape, lambda *g: (0,) * after.ndim))
    sem = (("parallel", "parallel", "arbitrary", "arbitrary") if stack == "sum"
           else ("parallel", "parallel", "parallel", "arbitrary"))
    return pl.pallas_call(
        body, name=name, grid=grid, in_specs=specs, out_specs=o_spec,
        out_shape=jax.ShapeDtypeStruct(((ns,) if stack == "out" else ()) + (m, n), out_dtype),
        scratch_shapes=[] if single else [pltpu.VMEM((tm, tn), F32)],
        compiler_params=_cparams(sem),
    )(*ins)


MM_VMEM_BUDGET = 40 * 1024 * 1024


def _mm_tiles(m, n, k, ns, sa, sb, so, has_res):
    def divs(x, mult, cap):
        out = [d for d in range(mult, min(x, cap) + 1, mult) if x % d == 0]
        return out or [x]

    best = None
    for tk in divs(k, 128, 8192):
        nk = (k // tk) * ns
        for tm in divs(m, 8, 2048):
            for tn in divs(n, 128, 2048):
                vmem = 2 * (tm * tk * sa + tk * tn * sb + tm * tn * so) + (2 * tm * tn * 4 if has_res else 0)
                vmem += tm * tn * 4 if nk > 1 else 0
                if vmem > MM_VMEM_BUDGET:
                    continue
                steps = (m // tm) * (n // tn) * nk
                traffic = (m // tm) * k * n * sb * ns + (n // tn if nk > 1 else 1) * m * k * sa * ns
                cost = steps * 0.4e-6 + traffic / 2.5e12 + (nk * m * n * 8 / 6e12 if nk > 1 else 0)
                cost += 2.0 * m * n * k * ns / 7e14
                if best is None or cost < best[0]:
                    best = (cost, tm, tn, tk)
    return best[1:]


def _norm_fwd(x, xoff, gain, ncol, w, out_dtype, *, z=None, zoff=0, into=None, into_off=0, name):
    t = x.shape[0]
    tr = _tile(t, max(256, (1 << 18) // w), 8)

    def body(*refs):
        x_ref, g_ref, o_ref = refs[0], refs[1], refs[-1]
        y = _rms_fn(x_ref[...], g_ref[...]) if z is None else _rms_fn(x_ref[...], g_ref[...], refs[2][...])
        o_ref[...] = y.astype(out_dtype)

    ins = [x, gain]
    specs = [pl.BlockSpec((tr, w), lambda j, r: (r, xoff + j)), pl.BlockSpec((1, w), lambda j, r: (0, 0))]
    if z is not None:
        ins.append(z)
        specs.append(pl.BlockSpec((tr, w), lambda j, r: (r, zoff + j)))
    aliases = {}
    if into is not None:
        aliases = {len(ins): 0}
        ins.append(into)
        specs.append(pl.BlockSpec(memory_space=pl.ANY))
    return pl.pallas_call(
        body, name=name, grid=(ncol, t // tr), in_specs=specs,
        out_specs=pl.BlockSpec((tr, w), lambda j, r: (r, into_off + j)),
        out_shape=jax.ShapeDtypeStruct((t, ncol * w) if into is None else into.shape, out_dtype),
        input_output_aliases=aliases, compiler_params=_cparams(("parallel", "parallel")),
    )(*ins)


def _norm_bwd(x, xoff, gain, dy, dyoff, ncol, w, *, z=None, zoff=0, res=None, name):
    t = x.shape[0]
    tr = _tile(t, max(256, (1 << 18) // w), 8)

    def body(*refs):
        it = iter(refs)
        x_ref, g_ref = next(it), next(it)
        z_ref = next(it) if z is not None else None
        dy_ref = next(it)
        r_ref = next(it) if res is not None else None
        dx_ref = next(it)
        dz_ref = next(it) if z is not None else None
        dg_ref = next(it)

        @pl.when((pl.program_id(0) == 0) & (pl.program_id(1) == 0))
        def _():
            dg_ref[...] = jnp.zeros_like(dg_ref)

        args = (x_ref[...], g_ref[...]) + ((z_ref[...],) if z is not None else ())
        _, vjp = jax.vjp(_rms_fn, *args)
        grads = vjp(dy_ref[...].astype(F32))
        dx = grads[0]
        if res is not None:
            dx = dx + r_ref[...]
        dx_ref[...] = dx
        if z is not None:
            dz_ref[...] = grads[2]
        dg_ref[...] += grads[1]

    ins = [x, gain]
    specs = [pl.BlockSpec((tr, w), lambda j, r: (r, xoff + j)), pl.BlockSpec((1, w), lambda j, r: (0, 0))]
    if z is not None:
        ins.append(z)
        specs.append(pl.BlockSpec((tr, w), lambda j, r: (r, zoff + j)))
    ins.append(dy)
    specs.append(pl.BlockSpec((tr, w), lambda j, r: (r, dyoff + j)))
    blk = pl.BlockSpec((tr, w), lambda j, r: (r, j))
    if res is not None:
        ins.append(res)
        specs.append(blk)
    full = jax.ShapeDtypeStruct((t, ncol * w), F32)
    out_shape, out_specs = [full], [blk]
    if z is not None:
        out_shape.append(full)
        out_specs.append(blk)
    out_shape.append(jax.ShapeDtypeStruct((1, w), F32))
    out_specs.append(pl.BlockSpec((1, w), lambda j, r: (0, 0)))
    return pl.pallas_call(
        body, name=name, grid=(ncol, t // tr), in_specs=specs, out_specs=out_specs, out_shape=out_shape,
        compiler_params=_cparams(("arbitrary", "arbitrary")),
    )(*ins)


def _small_fn(x, pa, pb, nf, ng):
    lane = lax.broadcasted_iota(jnp.int32, x.shape, 1)
    zz = x + pb
    logf = -_softplus(-zz)
    g = -jnp.exp(pa) * _softplus(zz)
    beta = _sigmoid(x)
    return jnp.where(lane < nf, logf, jnp.where(lane < nf + ng, g, beta))


def _tri(n, upper):
    r = lax.broadcasted_iota(jnp.int32, (n, n), 0)
    c = lax.broadcasted_iota(jnp.int32, (n, n), 1)
    return jnp.where((c >= r) if upper else (c <= r), 1.0, 0.0).astype(F32)


def _small_fwd(p, off, pa, pb, nf, ng):
    t = p.shape[0]
    blk = HEAD_DIM
    nb = t // blk

    def body(x_ref, pa_ref, pb_ref, v_ref, c_ref):
        v_ref[...] = _small_fn(x_ref[...], pa_ref[...], pb_ref[...], nf, ng)
        tri = _tri(blk, False)

        carry = jnp.zeros((1, HEAD_DIM), F32)
        for i in range(nb):
            rows = slice(i * blk, (i + 1) * blk)
            c = _nn_hi(tri, v_ref[rows, :]) + carry
            c_ref[rows, :] = c
            carry = c[blk - 1:blk, :]

    row = pl.BlockSpec((1, HEAD_DIM), lambda i: (0, 0))
    out = pl.BlockSpec((t, HEAD_DIM), lambda i: (0, 0))
    return pl.pallas_call(
        body, name="small_fwd", grid=(1,),
        in_specs=[pl.BlockSpec((t, HEAD_DIM), lambda i: (0, off)), row, row], out_specs=[out, out],
        out_shape=[jax.ShapeDtypeStruct((t, HEAD_DIM), F32)] * 2,
        compiler_params=_cparams(("arbitrary",)),
    )(p, pa, pb)


def _small_bwd(p, off, pa, pb, dvals, dcsum, nf, ng):
    t = p.shape[0]
    blk = HEAD_DIM
    nb = t // blk

    def body(x_ref, pa_ref, pb_ref, dv_ref, dc_ref, dx_ref, dpa_ref, dpb_ref, tot_ref):
        tri = _tri(blk, True)

        carry = jnp.zeros((1, HEAD_DIM), F32)
        for i in reversed(range(nb)):
            rows = slice(i * blk, (i + 1) * blk)
            c = _nn_hi(tri, dc_ref[rows, :]) + carry
            tot_ref[rows, :] = c + dv_ref[rows, :]
            carry = c[0:1, :]
        f = functools.partial(_small_fn, nf=nf, ng=ng)
        _, vjp = jax.vjp(f, x_ref[...], pa_ref[...], pb_ref[...])
        dx, dpa, dpb = vjp(tot_ref[...])
        dx_ref[...] = dx
        dpa_ref[...] = dpa
        dpb_ref[...] = dpb

    row = pl.BlockSpec((1, HEAD_DIM), lambda i: (0, 0))
    full = pl.BlockSpec((t, HEAD_DIM), lambda i: (0, 0))
    return pl.pallas_call(
        body, name="small_bwd", grid=(1,),
        in_specs=[pl.BlockSpec((t, HEAD_DIM), lambda i: (0, off)), row, row, full, full],
        out_specs=[full, row, row],
        out_shape=[jax.ShapeDtypeStruct((t, HEAD_DIM), F32), jax.ShapeDtypeStruct((1, HEAD_DIM), F32),
                   jax.ShapeDtypeStruct((1, HEAD_DIM), F32)],
        scratch_shapes=[pltpu.VMEM((t, HEAD_DIM), F32)],
        compiler_params=_cparams(("arbitrary",)),
    )(p, pa, pb, dvals, dcsum)


def _fox_heads(nf, most):
    return next(h for h in range(most, 0, -1) if nf % h == 0)


def _fox_fwd(q, k, v, cc, cr, nf, tq, tk, d_mix):
    t = q.shape[0]
    scale = HEAD_DIM ** -0.5
    assert tq == tk

    vt = jnp.transpose(v.reshape(t // tk, tk, nf, HEAD_DIM), (2, 0, 3, 1))

    hp = _fox_heads(nf, 3)
    lanes = lambda h: slice(h * HEAD_DIM, (h + 1) * HEAD_DIM)

    def body(q_ref, k_ref, vt_ref, cc_ref, cr_ref, o_ref, lse_ref, mix_ref):
        i = pl.program_id(1)
        qs = [q_ref[:, lanes(h)] for h in range(hp)]
        cqs = [cr_ref[h, i] for h in range(hp)]
        ones = jnp.ones((8, tk), BF16)
        diff = lax.broadcasted_iota(jnp.int32, (tk, tq), 0) - lax.broadcasted_iota(jnp.int32, (tk, tq), 1)

        def scores(h, j):
            ks = pl.ds(pl.multiple_of(j * tk, tk), tk)
            return lax.dot_general(k_ref[ks, lanes(h)], qs[h], (((1,), (1,)), ((), ())),
                                   preferred_element_type=F32)

        def tile(h, j, m, l, acc, s, masked):
            ks = pl.ds(pl.multiple_of(j * tk, tk), tk)
            s = s * scale + cqs[h] - cc_ref[h, ks, :]
            if masked:
                s = jnp.where(diff <= 0, s, NEG)
            m_new = jnp.maximum(m, jnp.max(s, axis=0, keepdims=True))
            pr = jnp.exp(s - m_new).astype(BF16)
            alpha = jnp.exp(m - m_new)
            l = alpha * l + jnp.dot(ones, pr, preferred_element_type=F32)[:1]
            acc = alpha * acc + jnp.dot(vt_ref[h, j], pr, preferred_element_type=F32)
            return m_new, l, acc

        def step(j, carry):
            nxt = [scores(h, j + 1) for h in range(hp)]
            return tuple(tile(h, j, *carry[h], False) + (nxt[h],) for h in range(hp))

        init = tuple((jnp.full((1, tq), NEG, F32), jnp.zeros((1, tq), F32), jnp.zeros((HEAD_DIM, tq), F32),
                      scores(h, 0)) for h in range(hp))
        carry = lax.fori_loop(0, i, step, init)
        for h in range(hp):
            m, l, acc = tile(h, i, *carry[h], True)
            o = jnp.transpose(acc / l)
            o_ref[:, lanes(h)] = o
            mix_ref[:, lanes(h)] = o.astype(BF16)
            lse_ref[h, 0] = m + jnp.log(l)

    w = hp * HEAD_DIM
    qblk = pl.BlockSpec((tq, w), lambda h, i: (i, h))
    return pl.pallas_call(
        body, name="fox_fwd", grid=(nf // hp, t // tq),
        in_specs=[qblk, pl.BlockSpec((t, w), lambda h, i: (0, h)),
                  pl.BlockSpec((hp, t // tk, HEAD_DIM, tk), lambda h, i: (h, 0, 0, 0)),
                  pl.BlockSpec((hp, t, 1), lambda h, i: (h, 0, 0)),
                  pl.BlockSpec((hp, t // tk, 1, tk), lambda h, i: (h, 0, 0, 0))],
        out_specs=[qblk, pl.BlockSpec((hp, 1, 1, tq), lambda h, i: (h, i, 0, 0)), qblk],
        out_shape=[jax.ShapeDtypeStruct((t, nf * HEAD_DIM), F32), jax.ShapeDtypeStruct((nf, t // tq, 1, tq), F32),
                   jax.ShapeDtypeStruct((t, d_mix), BF16)],
        compiler_params=_cparams(("parallel", "parallel")),
    )(q, k, vt, cc, cr)


def _fox_bwd(q, k, v, cc, cr, o, lse, dmix, nf, tq, tk):
    t = q.shape[0]
    scale = HEAD_DIM ** -0.5
    assert tq == tk
    hp = _fox_heads(nf, 3)
    lanes = lambda h: slice(h * HEAD_DIM, (h + 1) * HEAD_DIM)
    kt = jnp.transpose(k.reshape(t // tk, tk, nf, HEAD_DIM), (2, 0, 3, 1))

    def body(q_ref, k_ref, kt_ref, v_ref, cc_ref, cr_ref, o_ref, lse_ref, do_ref,
             dq_ref, dk_ref, dv_ref, dcq_ref, dck_ref):
        i = pl.program_id(1)

        @pl.when(i == 0)
        def _():
            dk_ref[...] = jnp.zeros_like(dk_ref)
            dv_ref[...] = jnp.zeros_like(dv_ref)
            dck_ref[...] = jnp.zeros_like(dck_ref)

        diff = lax.broadcasted_iota(jnp.int32, (tk, tq), 0) - lax.broadcasted_iota(jnp.int32, (tk, tq), 1)
        qs = [q_ref[:, lanes(h)] for h in range(hp)]
        dos = [do_ref[:, lanes(h)] for h in range(hp)]
        do_b = [d.astype(BF16) for d in dos]
        cqs = [cr_ref[h, i] for h in range(hp)]
        lses = [lse_ref[h, 0] for h in range(hp)]
        deltas = [jnp.sum(jnp.transpose(dos[h] * o_ref[:, lanes(h)]), axis=0, keepdims=True) for h in range(hp)]

        def products(h, j):
            ks = pl.ds(pl.multiple_of(j * tk, tk), tk)
            nt = (((1,), (1,)), ((), ()))
            return (lax.dot_general(k_ref[ks, lanes(h)], qs[h], nt, preferred_element_type=F32),
                    lax.dot_general(v_ref[ks, lanes(h)], do_b[h], nt, preferred_element_type=F32))

        def tile(h, j, dqt, dcq, s, dp, masked):
            ks = pl.ds(pl.multiple_of(j * tk, tk), tk)
            pr = jnp.exp(s * scale + cqs[h] - cc_ref[h, ks, :] - lses[h])
            if masked:
                pr = jnp.where(diff <= 0, pr, 0.0)
            ds = pr * (dp - deltas[h])
            ds_b = ds.astype(BF16)
            dqt = dqt + jnp.dot(kt_ref[h, j], ds_b, preferred_element_type=F32)
            dk_ref[ks, lanes(h)] += jnp.dot(ds_b, qs[h], preferred_element_type=F32) * scale
            dv_ref[ks, lanes(h)] += jnp.dot(pr.astype(BF16), do_b[h], preferred_element_type=F32)
            dck_ref[h, ks, :] -= jnp.sum(ds, axis=1, keepdims=True)
            return dqt, dcq + jnp.sum(ds, axis=0, keepdims=True)

        def step(j, carry):
            nxt = [products(h, j + 1) for h in range(hp)]
            return tuple(tile(h, j, *carry[h], False) + nxt[h] for h in range(hp))

        init = tuple((jnp.zeros((HEAD_DIM, tq), F32), jnp.zeros((1, tq), F32)) + products(h, 0) for h in range(hp))
        carry = lax.fori_loop(0, i, step, init)
        for h in range(hp):
            dqt, dcq = tile(h, i, *carry[h], True)
            dq_ref[:, lanes(h)] = jnp.transpose(dqt) * scale
            dcq_ref[h, 0] = dcq

    w = hp * HEAD_DIM
    head_all = pl.BlockSpec((t, w), lambda h, i: (0, h))
    qblk = pl.BlockSpec((tq, w), lambda h, i: (i, h))
    colv = pl.BlockSpec((hp, t, 1), lambda h, i: (h, 0, 0))
    rows_all = pl.BlockSpec((hp, t // tk, 1, tk), lambda h, i: (h, 0, 0, 0))
    row_blk = pl.BlockSpec((hp, 1, 1, tq), lambda h, i: (h, i, 0, 0))
    wide = jax.ShapeDtypeStruct((t, nf * HEAD_DIM), F32)
    return pl.pallas_call(
        body, name="fox_bwd", grid=(nf // hp, t // tq),
        in_specs=[qblk, head_all, pl.BlockSpec((hp, t // tk, HEAD_DIM, tk), lambda h, i: (h, 0, 0, 0)), head_all,
                  colv, rows_all, qblk, row_blk, qblk],
        out_specs=[qblk, head_all, head_all, row_blk, colv],
        out_shape=[wide, wide, wide, jax.ShapeDtypeStruct((nf, t // tq, 1, tq), F32),
                   jax.ShapeDtypeStruct((nf, t, 1), F32)],
        compiler_params=_cparams(("parallel", "arbitrary")),
    )(q, k, kt, v, cc, cr, o, lse, dmix)


def _mem_fn(mq, mk, mv, gq, gk):
    qn = _rms_fn(mq, gq)
    kn = _rms_fn(mk, gk)
    s = _nt(qn, kn) * (HEAD_DIM ** -0.5)
    e = jnp.exp(s - lax.stop_gradient(jnp.max(s, axis=1, keepdims=True)))
    pr = e / jnp.sum(e, axis=1, keepdims=True)
    return _nn(pr, mv)


def _mem_specs(t, m, tq, qoff):
    qblk = pl.BlockSpec((tq, HEAD_DIM), lambda h, i: (i, qoff + h))
    kblk = pl.BlockSpec((m, HEAD_DIM), lambda h, i: (0, h))
    vblk = pl.BlockSpec((m, HEAD_DIM), lambda h, i: (0, N_MEM_HEADS + h))
    row = pl.BlockSpec((1, HEAD_DIM), lambda h, i: (0, 0))
    return qblk, kblk, vblk, row


def _mem_fwd(p, qoff, mkv, gq, gk, tq, into, into_off):
    t, m = p.shape[0], mkv.shape[0]
    qblk, kblk, vblk, row = _mem_specs(t, m, tq, qoff)

    def body(q_ref, k_ref, v_ref, gq_ref, gk_ref, _, o_ref):
        o_ref[...] = _mem_fn(q_ref[...], k_ref[...], v_ref[...], gq_ref[...], gk_ref[...]).astype(BF16)

    return pl.pallas_call(
        body, name="mem_fwd", grid=(N_MEM_HEADS, t // tq),
        in_specs=[qblk, kblk, vblk, row, row, pl.BlockSpec(memory_space=pl.ANY)],
        out_specs=pl.BlockSpec((tq, HEAD_DIM), lambda h, i: (i, into_off + h)),
        out_shape=jax.ShapeDtypeStruct(into.shape, BF16), input_output_aliases={5: 0},
        compiler_params=_cparams(("parallel", "parallel")),
    )(p, mkv, mkv, gq, gk, into)


def _mem_bwd(p, qoff, mkv, gq, gk, dmix, dooff, tq):
    t, m = p.shape[0], mkv.shape[0]
    qblk, kblk, vblk, row = _mem_specs(t, m, tq, qoff)

    def body(q_ref, k_ref, v_ref, gq_ref, gk_ref, do_ref, dq_ref, dkv_k_ref, dkv_v_ref, dgq_ref, dgk_ref):
        h, i = pl.program_id(0), pl.program_id(1)

        @pl.when((h == 0) & (i == 0))
        def _():
            dgq_ref[...] = jnp.zeros_like(dgq_ref)
            dgk_ref[...] = jnp.zeros_like(dgk_ref)

        @pl.when(i == 0)
        def _():
            dkv_k_ref[...] = jnp.zeros_like(dkv_k_ref)
            dkv_v_ref[...] = jnp.zeros_like(dkv_v_ref)

        _, vjp = jax.vjp(_mem_fn, q_ref[...], k_ref[...], v_ref[...], gq_ref[...], gk_ref[...])
        dq, dk, dv, dgq, dgk = vjp(do_ref[...])
        dq_ref[...] = dq
        dkv_k_ref[...] += dk
        dkv_v_ref[...] += dv
        dgq_ref[...] += dgq
        dgk_ref[...] += dgk

    oblk = pl.BlockSpec((tq, HEAD_DIM), lambda h, i: (i, h))
    kout = pl.BlockSpec((m, HEAD_DIM), lambda h, i: (0, h))
    half = jax.ShapeDtypeStruct((m, N_MEM_HEADS * HEAD_DIM), F32)
    rshape = jax.ShapeDtypeStruct((1, HEAD_DIM), F32)
    return pl.pallas_call(
        body, name="mem_bwd", grid=(N_MEM_HEADS, t // tq),
        in_specs=[qblk, kblk, vblk, row, row, pl.BlockSpec((tq, HEAD_DIM), lambda h, i: (i, dooff + h))],
        out_specs=[oblk, kout, kout, row, row],
        out_shape=[jax.ShapeDtypeStruct((t, N_MEM_HEADS * HEAD_DIM), F32), half, half, rshape, rshape],
        compiler_params=_cparams(("arbitrary", "arbitrary")),
    )(p, mkv, mkv, gq, gk, dmix)


def _shift_down(x, s):
    if s == 0:
        return x
    r = lax.broadcasted_iota(jnp.int32, x.shape, 0)
    return jnp.where(r >= s, pltpu.roll(x, s, 0), 0.0)


def _shift_up(x, s):
    if s == 0:
        return x
    n = x.shape[0]
    r = lax.broadcasted_iota(jnp.int32, x.shape, 0)
    return jnp.where(r < n - s, pltpu.roll(x, n - s, 0), 0.0)


def _conv_fn(x0, x1, x2, x3, w0, w1, w2, w3, kind):
    y = _silu(x0 * w0 + x1 * w1 + x2 * w2 + x3 * w3)
    if kind == 2:
        return y
    y = y * lax.rsqrt(jnp.sum(y * y, axis=-1, keepdims=True) + NORM_EPS)
    return y * (HEAD_DIM ** -0.5) if kind == 0 else y


def _conv_fwd(p, off, conv_w, ng):
    t = p.shape[0]

    def body(x_ref, w_ref, o_ref):
        kind = pl.program_id(0) // ng
        x = x_ref[...]
        xs = [_shift_down(x, CONV_WIDTH - 1 - j) for j in range(CONV_WIDTH)]
        ws = [w_ref[j:j + 1, :] for j in range(CONV_WIDTH)]
        for kd in range(3):
            @pl.when(kind == kd)
            def _(kd=kd):
                o_ref[...] = _conv_fn(*xs, *ws, kd)

    return pl.pallas_call(
        body, name="gdn_conv_fwd", grid=(3 * ng,),
        in_specs=[pl.BlockSpec((t, HEAD_DIM), lambda c: (0, off + c)),
                  pl.BlockSpec((CONV_WIDTH, HEAD_DIM), lambda c: (0, c))],
        out_specs=pl.BlockSpec((t, HEAD_DIM), lambda c: (0, c)),
        out_shape=jax.ShapeDtypeStruct((t, 3 * ng * HEAD_DIM), F32),
        compiler_params=_cparams(("parallel",)),
    )(p, conv_w)


def _conv_bwd(p, off, conv_w, dys, ng):
    t = p.shape[0]

    def body(x_ref, w_ref, dq_ref, dk_ref, dv_ref, dx_ref, dw_ref):
        kind = pl.program_id(0) // ng
        dy_refs = (dq_ref, dk_ref, dv_ref)
        x = x_ref[...]
        xs = [_shift_down(x, CONV_WIDTH - 1 - j) for j in range(CONV_WIDTH)]
        ws = [w_ref[j:j + 1, :] for j in range(CONV_WIDTH)]
        for kd in range(3):
            @pl.when(kind == kd)
            def _(kd=kd):
                _, vjp = jax.vjp(functools.partial(_conv_fn, kind=kd), *xs, *ws)
                g = vjp(dy_refs[kd][...])
                dx = _shift_up(g[0], CONV_WIDTH - 1)
                for j in range(1, CONV_WIDTH):
                    dx = dx + _shift_up(g[j], CONV_WIDTH - 1 - j)
                dx_ref[...] = dx
                for j in range(CONV_WIDTH):
                    dw_ref[j:j + 1, :] = g[CONV_WIDTH + j]

    blk = pl.BlockSpec((t, HEAD_DIM), lambda c: (0, c))
    head = pl.BlockSpec((t, HEAD_DIM), lambda c: (0, c % ng))
    wblk = pl.BlockSpec((CONV_WIDTH, HEAD_DIM), lambda c: (0, c))
    return pl.pallas_call(
        body, name="gdn_conv_bwd", grid=(3 * ng,),
        in_specs=[pl.BlockSpec((t, HEAD_DIM), lambda c: (0, off + c)), wblk] + [head] * 3,
        out_specs=[blk, wblk],
        out_shape=[jax.ShapeDtypeStruct((t, 3 * ng * HEAD_DIM), F32),
                   jax.ShapeDtypeStruct((CONV_WIDTH, 3 * ng * HEAD_DIM), F32)],
        compiler_params=_cparams(("parallel",)),
    )(p, conv_w, *dys)


def _lower_inverse(lower):
    c = lower.shape[-1]
    r = lax.broadcasted_iota(jnp.int32, (1, c, c), 1)
    e = lax.broadcasted_iota(jnp.int32, (1, c, c), 2)
    hi = lax.Precision.HIGH
    inv = jnp.where(r == e, 1.0, 0.0) - lower
    pw = lower
    for _ in range(int(math.log2(c)) - 1):
        pw = _dot(pw, pw, ((1,), (0,)), hi)
        inv = inv + _dot(inv, pw, ((1,), (0,)), hi)
    return inv


@jax.custom_vjp
def _solve(lower, inv, vb, kbg):
    hi = lax.Precision.HIGH
    return _dot(inv, vb, ((1,), (0,)), hi), _dot(inv, kbg, ((1,), (0,)), hi)


def _solve_fwd(lower, inv, vb, kbg):
    u, w = _solve(lower, inv, vb, kbg)
    return (u, w), (inv, u, w)


def _solve_bwd(res, cts):
    inv, u, w = res
    dvb, dkbg = _tn(inv, cts[0]), _tn(inv, cts[1])
    return -(_nt(dvb, u) + _nt(dkbg, w)), jnp.zeros_like(inv), dvb, dkbg


_solve.defvjp(_solve_fwd, _solve_bwd)


def _wy_fn(q, k, v, gcol, grow, bcol, inv=None):
    b, c, dk = q.shape
    r = lax.broadcasted_iota(jnp.int32, (1, c, c), 1)
    e = lax.broadcasted_iota(jnp.int32, (1, c, c), 2)
    tril, strict = e <= r, e < r
    gc_col = jnp.sum(jnp.where(tril, grow, 0.0), axis=2, keepdims=True)
    gc_row = jnp.sum(jnp.where(r <= e, gcol, 0.0), axis=1, keepdims=True)
    g_last = jnp.sum(gcol, axis=1, keepdims=True)
    decay = jnp.exp(jnp.where(tril, gc_col - gc_row, NEG))
    kb, vb = k * bcol, v * bcol
    lower = jnp.where(strict, _nt(kb, k) * decay, 0.0)
    if inv is None:
        inv = _lower_inverse(lower)
    u, w = _solve(lower, inv, vb, kb * jnp.exp(gc_col))
    attn = jnp.where(tril, _nt(q, k) * decay, 0.0)
    qg = q * jnp.exp(gc_col)
    kdec = k * jnp.exp(g_last - gc_col)
    egl = jnp.broadcast_to(jnp.exp(g_last), (b, 1, dk))
    return u, w, qg, kdec, attn, egl, inv


def _scan_fn(u, w, qg, kdec, attn, egl, state):
    v_new = u - _nn(w, state)
    o = _nn(qg, state) + _nn(attn, v_new)
    return o, state * egl + _tn(kdec, v_new)


GDN_CHUNKS_PER_STEP = 4


def _gdn_fwd(qkv, gcol, grow, bcol, ng):
    t = qkv.shape[0]
    nch = t // CHUNK

    cb = GDN_CHUNKS_PER_STEP
    *wy, inv = _gdn_wy(qkv, gcol, grow, bcol, ng, cb)

    def body(u_ref, w_ref, qg_ref, kd_ref, at_ref, eg_ref, o_ref, st_ref, state):
        @pl.when(pl.program_id(0) == 0)
        def _():
            state[...] = jnp.zeros_like(state)

        st_ref[:, 0] = state[...]
        heads = lambda ref: jnp.stack([ref[:, h * HEAD_DIM:(h + 1) * HEAD_DIM] for h in range(ng)])
        o, new = _scan_fn(heads(u_ref), heads(w_ref), heads(qg_ref), heads(kd_ref), at_ref[:, 0], eg_ref[:, 0],
                          state[...])
        for h in range(ng):
            o_ref[:, h * HEAD_DIM:(h + 1) * HEAD_DIM] = o[h]
        state[...] = new

    w = ng * HEAD_DIM
    blk = pl.BlockSpec((CHUNK, w), lambda i: (i, 0))
    o, states = pl.pallas_call(
        body, name="gdn_scan_fwd", grid=(nch,),
        in_specs=[blk, blk, blk, blk, pl.BlockSpec((ng, 1, CHUNK, CHUNK), lambda i: (0, i, 0, 0)),
                  pl.BlockSpec((ng, 1, 1, HEAD_DIM), lambda i: (0, i, 0, 0))],
        out_specs=[blk, pl.BlockSpec((ng, 1, HEAD_DIM, HEAD_DIM), lambda i: (0, i, 0, 0))],
        out_shape=[jax.ShapeDtypeStruct((t, w), F32),
                   jax.ShapeDtypeStruct((ng, nch, HEAD_DIM, HEAD_DIM), F32)],
        scratch_shapes=[pltpu.VMEM((ng, HEAD_DIM, HEAD_DIM), F32)],
        compiler_params=_cparams(("arbitrary",)),
    )(*wy)
    return o, (wy, inv, states)


def _wy_batch(q_ref, k_ref, v_ref, gc_ref, gr_ref, bc_ref, ng, cb):
    idx = [(c, h) for c in range(cb) for h in range(ng)]
    rows = lambda c: slice(c * CHUNK, (c + 1) * CHUNK)
    lanes = lambda h: slice(h * HEAD_DIM, (h + 1) * HEAD_DIM)
    wide = lambda ref: jnp.stack([ref[rows(c), lanes(h)] for c, h in idx])
    col = lambda ref: jnp.stack([ref[h, rows(c), :] for c, h in idx])
    return idx, (wide(q_ref), wide(k_ref), wide(v_ref), col(gc_ref), jnp.stack([gr_ref[h, c] for c, h in idx]),
                 col(bc_ref))


def _gdn_wy(qkv, gcol, grow, bcol, ng, cb):
    t = qkv.shape[0]
    nch = t // CHUNK

    def body(q_ref, k_ref, v_ref, gc_ref, gr_ref, bc_ref, u_ref, w_ref, qg_ref, kd_ref, at_ref, eg_ref, inv_ref):
        idx, args = _wy_batch(q_ref, k_ref, v_ref, gc_ref, gr_ref, bc_ref, ng, cb)
        u, w, qg, kd, at, eg, inv = _wy_fn(*args)
        for b, (c, h) in enumerate(idx):
            rows, lanes = slice(c * CHUNK, (c + 1) * CHUNK), slice(h * HEAD_DIM, (h + 1) * HEAD_DIM)
            u_ref[rows, lanes] = u[b]
            w_ref[rows, lanes] = w[b]
            qg_ref[rows, lanes] = qg[b]
            kd_ref[rows, lanes] = kd[b]
            at_ref[h, c] = at[b]
            eg_ref[h, c] = eg[b]
            inv_ref[h, c] = inv[b]

    wd = ng * HEAD_DIM
    blk = lambda o: pl.BlockSpec((cb * CHUNK, wd), lambda i: (i, o))
    col = pl.BlockSpec((ng, cb * CHUNK, 1), lambda i: (0, i, 0))
    sq = pl.BlockSpec((ng, cb, CHUNK, CHUNK), lambda i: (0, i, 0, 0))
    wide = jax.ShapeDtypeStruct((t, wd), F32)
    sq_shape = jax.ShapeDtypeStruct((ng, nch, CHUNK, CHUNK), F32)
    return pl.pallas_call(
        body, name="gdn_wy_fwd", grid=(nch // cb,),
        in_specs=[blk(0), blk(1), blk(2), col, pl.BlockSpec((ng, cb, 1, CHUNK), lambda i: (0, i, 0, 0)), col],
        out_specs=[blk(0), blk(0), blk(0), blk(0), sq, pl.BlockSpec((ng, cb, 1, HEAD_DIM), lambda i: (0, i, 0, 0)),
                   sq],
        out_shape=[wide, wide, wide, wide, sq_shape, jax.ShapeDtypeStruct((ng, nch, 1, HEAD_DIM), F32), sq_shape],
        compiler_params=_cparams(("parallel",)),
    )(qkv, qkv, qkv, gcol, grow, bcol)


def _gdn_bwd(qkv, gcol, grow, bcol, saved, do, ng):
    t = qkv.shape[0]
    nch = t // CHUNK
    cb = GDN_CHUNKS_PER_STEP // 2
    wy, inv, states = saved
    wd = ng * HEAD_DIM

    def scan_body(u_ref, w_ref, qg_ref, kd_ref, at_ref, eg_ref, st_ref, do_ref,
                  du_ref, dw_ref, dqg_ref, dkd_ref, dat_ref, deg_ref, dstate):
        @pl.when(pl.program_id(0) == 0)
        def _():
            dstate[...] = jnp.zeros_like(dstate)

        heads = lambda ref: jnp.stack([ref[:, h * HEAD_DIM:(h + 1) * HEAD_DIM] for h in range(ng)])
        _, vjp = jax.vjp(_scan_fn, heads(u_ref), heads(w_ref), heads(qg_ref), heads(kd_ref), at_ref[:, 0],
                         eg_ref[:, 0], st_ref[:, 0])
        du, dw, dqg, dkd, dat, deg, dst = vjp((heads(do_ref), dstate[...]))
        for h in range(ng):
            lanes = slice(h * HEAD_DIM, (h + 1) * HEAD_DIM)
            du_ref[:, lanes] = du[h]
            dw_ref[:, lanes] = dw[h]
            dqg_ref[:, lanes] = dqg[h]
            dkd_ref[:, lanes] = dkd[h]
        dat_ref[:, 0] = dat
        deg_ref[:, 0] = deg
        dstate[...] = dst

    rev = lambda i: nch - 1 - i
    blk = pl.BlockSpec((CHUNK, wd), lambda i: (rev(i), 0))
    atb = pl.BlockSpec((ng, 1, CHUNK, CHUNK), lambda i: (0, rev(i), 0, 0))
    egb = pl.BlockSpec((ng, 1, 1, HEAD_DIM), lambda i: (0, rev(i), 0, 0))
    wide = jax.ShapeDtypeStruct((t, wd), F32)
    at_shape = jax.ShapeDtypeStruct((ng, nch, CHUNK, CHUNK), F32)
    eg_shape = jax.ShapeDtypeStruct((ng, nch, 1, HEAD_DIM), F32)
    dwy = pl.pallas_call(
        scan_body, name="gdn_scan_bwd", grid=(nch,),
        in_specs=[blk, blk, blk, blk, atb, egb,
                  pl.BlockSpec((ng, 1, HEAD_DIM, HEAD_DIM), lambda i: (0, rev(i), 0, 0)), blk],
        out_specs=[blk, blk, blk, blk, atb, egb],
        out_shape=[wide, wide, wide, wide, at_shape, eg_shape],
        scratch_shapes=[pltpu.VMEM((ng, HEAD_DIM, HEAD_DIM), F32)],
        compiler_params=_cparams(("arbitrary",)),
    )(*wy, states, do)

    def wy_body(q_ref, k_ref, v_ref, gc_ref, gr_ref, bc_ref, du_ref, dw_ref, dqg_ref, dkd_ref, dat_ref, deg_ref,
                inv_ref, dq_ref, dk_ref, dv_ref, dgc_ref, dgr_ref, dbc_ref):
        idx, args = _wy_batch(q_ref, k_ref, v_ref, gc_ref, gr_ref, bc_ref, ng, cb)
        kept = jnp.stack([inv_ref[h, c] for c, h in idx])
        rows = lambda c: slice(c * CHUNK, (c + 1) * CHUNK)
        lanes = lambda h: slice(h * HEAD_DIM, (h + 1) * HEAD_DIM)
        wide_ct = lambda ref: jnp.stack([ref[rows(c), lanes(h)] for c, h in idx])
        cts = (wide_ct(du_ref), wide_ct(dw_ref), wide_ct(dqg_ref), wide_ct(dkd_ref),
               jnp.stack([dat_ref[h, c] for c, h in idx]), jnp.stack([deg_ref[h, c] for c, h in idx]))
        _, vjp = jax.vjp(lambda *a: _wy_fn(*a, inv=kept)[:6], *args)
        dq, dk, dv, dgc, dgr, dbc = vjp(cts)
        for b, (c, h) in enumerate(idx):
            dq_ref[rows(c), lanes(h)] = dq[b]
            dk_ref[rows(c), lanes(h)] = dk[b]
            dv_ref[rows(c), lanes(h)] = dv[b]
            dgc_ref[h, rows(c), :] = dgc[b]
            dgr_ref[h, c] = dgr[b]
            dbc_ref[h, rows(c), :] = dbc[b]

    cblk = lambda o: pl.BlockSpec((cb * CHUNK, wd), lambda i: (i, o))
    col = pl.BlockSpec((ng, cb * CHUNK, 1), lambda i: (0, i, 0))
    rowv = pl.BlockSpec((ng, cb, 1, CHUNK), lambda i: (0, i, 0, 0))
    cshape = jax.ShapeDtypeStruct((ng, t, 1), F32)
    return pl.pallas_call(
        wy_body, name="gdn_wy_bwd", grid=(nch // cb,),
        in_specs=[cblk(0), cblk(1), cblk(2), col, rowv, col, cblk(0), cblk(0), cblk(0), cblk(0),
                  pl.BlockSpec((ng, cb, CHUNK, CHUNK), lambda i: (0, i, 0, 0)),
                  pl.BlockSpec((ng, cb, 1, HEAD_DIM), lambda i: (0, i, 0, 0)),
                  pl.BlockSpec((ng, cb, CHUNK, CHUNK), lambda i: (0, i, 0, 0))],
        out_specs=[cblk(0), cblk(0), cblk(0), col, rowv, col],
        out_shape=[wide, wide, wide, cshape, jax.ShapeDtypeStruct((ng, nch, 1, CHUNK), F32), cshape],
        compiler_params=_cparams(("parallel",)),
    )(qkv, qkv, qkv, gcol, grow, bcol, *dwy, inv)


def _swiglu_fn(gate, up):
    return _silu(gate) * up


FFN_TN = 256


def _ffn_up(n2, wgu4):
    _, d, w = wgu4.shape
    t = n2.shape[0]
    tn = _tile(w, FFN_TN)
    nb = w // tn

    def body(a_ref, b_ref, gu_ref, act_ref):
        av = a_ref[...]
        gate = jnp.dot(av, b_ref[0], preferred_element_type=F32)
        up = jnp.dot(av, b_ref[1], preferred_element_type=F32)
        gu_ref[0] = gate.astype(BF16)
        gu_ref[1] = up.astype(BF16)
        act_ref[...] = _swiglu_fn(gate, up).astype(BF16)

    return pl.pallas_call(
        body, name="ffn_up", grid=(2, nb),
        in_specs=[pl.BlockSpec((t, d), lambda j, l: (0, 0)), pl.BlockSpec((2, d, tn), lambda j, l: (j, 0, l))],
        out_specs=[pl.BlockSpec((2, t, tn), lambda j, l: (j, 0, l)),
                   pl.BlockSpec((t, tn), lambda j, l: (0, j * nb + l))],
        out_shape=[jax.ShapeDtypeStruct((4, t, w), BF16), jax.ShapeDtypeStruct((t, 2 * w), BF16)],
        compiler_params=_cparams(("parallel", "parallel")),
    )(n2, wgu4)


def _ffn_dact(dh2, wd, gu, after):
    _, t, w = gu.shape
    d = dh2.shape[1]
    tn = _tile(w, FFN_TN)
    nb = w // tn

    def body(a_ref, b_ref, gu_ref, _, o_ref):
        dact = lax.dot_general(a_ref[...], b_ref[...], (((1,), (1,)), ((), ())), preferred_element_type=F32)
        _, vjp = jax.vjp(_swiglu_fn, gu_ref[0].astype(F32), gu_ref[1].astype(F32))
        dg, du = vjp(dact)
        o_ref[0] = dg.astype(BF16)
        o_ref[1] = du.astype(BF16)

    pair = pl.BlockSpec((2, t, tn), lambda j, l: (j, 0, l))
    return pl.pallas_call(
        body, name="ffn_dact", grid=(2, nb),
        in_specs=[pl.BlockSpec((t, d), lambda j, l: (0, 0)), pl.BlockSpec((tn, d), lambda j, l: (j * nb + l, 0)),
                  pair, pl.BlockSpec(after.shape, lambda j, l: (0, 0))],
        out_specs=pair, out_shape=jax.ShapeDtypeStruct(gu.shape, BF16),
        compiler_params=_cparams(("parallel", "parallel")),
    )(dh2, wd, gu, after)


def _loss_head(h2, target):
    t, d = h2.shape
    tr = _tile(t, 256, 8)

    def body(h_ref, t_ref, l_ref, d_ref, db_ref):
        @pl.when(pl.program_id(0) == 0)
        def _():
            l_ref[...] = jnp.zeros_like(l_ref)

        err = h_ref[...] - t_ref[...]
        d_ref[...] = err * (1.0 / d)
        db_ref[...] = (err * (1.0 / d)).astype(BF16)
        part = 0.5 * jnp.sum(jnp.mean(err * err, axis=-1, keepdims=True), axis=0, keepdims=True)
        lane = lax.broadcasted_iota(jnp.int32, (8, HEAD_DIM), 1)
        row = lax.broadcasted_iota(jnp.int32, (8, HEAD_DIM), 0)
        l_ref[...] += jnp.where((lane == 0) & (row == 0), part, 0.0)

    blk = pl.BlockSpec((tr, d), lambda r: (r, 0))
    return pl.pallas_call(
        body, name="loss_head", grid=(t // tr,), in_specs=[blk, blk],
        out_specs=[pl.BlockSpec((8, HEAD_DIM), lambda r: (0, 0)), blk, blk],
        out_shape=[jax.ShapeDtypeStruct((8, HEAD_DIM), F32), jax.ShapeDtypeStruct((t, d), F32),
                   jax.ShapeDtypeStruct((t, d), BF16)],
        compiler_params=_cparams(("arbitrary",)),
    )(h2, target)


def _adamw(w, g, m, v, *, g_fn=None, name):
    r, c = w.shape
    tr = _tile(r, max(8, (1 << 19) // c // 8 * 8), 8)

    def body(w_ref, g_ref, m_ref, v_ref, go_ref, d_ref, mo_ref, vo_ref):
        gr = g_ref[...] if g_fn is None else g_fn(g_ref[...])
        mn = ADAM_B1 * m_ref[...] + (1.0 - ADAM_B1) * gr
        vn = ADAM_B2 * v_ref[...] + (1.0 - ADAM_B2) * (gr * gr)
        m_hat = mn / (1.0 - ADAM_B1 ** ADAM_STEP)
        v_hat = vn / (1.0 - ADAM_B2 ** ADAM_STEP)
        go_ref[...] = gr
        d_ref[...] = -ADAM_LR * (m_hat / (jnp.sqrt(v_hat) + ADAM_EPS) + ADAM_WD * w_ref[...])
        mo_ref[...] = mn
        vo_ref[...] = vn

    blk = pl.BlockSpec((tr, c), lambda i: (i, 0))
    gblk = pl.BlockSpec((tr, g.shape[1]), lambda i: (i, 0))
    return pl.pallas_call(
        body, name=name, grid=(r // tr,), in_specs=[blk, gblk, blk, blk], out_specs=[blk] * 4,
        out_shape=[jax.ShapeDtypeStruct((r, c), F32)] * 4,
        compiler_params=_cparams(("parallel",)),
    )(w, g, m, v)


class _Layout:
    def __init__(self, d):
        nh = d // HEAD_DIM
        self.nm = N_MEM_HEADS
        self.nf = (nh - self.nm) // 2
        self.ng = nh - self.nm - self.nf
        nf, ng, nm = self.nf, self.ng, self.nm
        self.o_fq, self.o_fk, self.o_fv = 0, nf, 2 * nf
        self.o_gq = 3 * nf
        self.o_gz = 3 * nf + 3 * ng
        self.o_mq = 3 * nf + 4 * ng
        self.o_sm = self.o_mq + nm
        self.blocks = -(-(self.o_sm + 1) // 8) * 8
        self.cols = self.blocks * HEAD_DIM
        hd = HEAD_DIM
        sizes = [nf * hd, nf * hd, nf * hd, nf, 3 * ng * hd, ng * hd, ng, ng, nm * hd]
        starts = [sum(sizes[:i]) for i in range(len(sizes))]
        self.ref = list(zip(starts, sizes))
        self.in_cols = sum(sizes)

    def regroup(self, w):
        part = lambda i: w[:, self.ref[i][0]:self.ref[i][0] + self.ref[i][1]]
        pieces = [part(0), part(1), part(2), part(4), part(5), part(8), part(3), part(6), part(7)]
        pad = self.cols - self.in_cols
        return jnp.concatenate(pieces + [jnp.zeros((w.shape[0], pad), w.dtype)], axis=1)

    def ungroup(self, g):
        hd, nf, ng, nm = HEAD_DIM, self.nf, self.ng, self.nm
        sm = self.o_sm * hd
        return jnp.concatenate([
            g[:, :3 * nf * hd], g[:, sm:sm + nf], g[:, self.o_gq * hd:self.o_gz * hd],
            g[:, self.o_gz * hd:self.o_mq * hd], g[:, sm + nf:sm + nf + ng], g[:, sm + nf + ng:sm + nf + 2 * ng],
            g[:, self.o_mq * hd:self.o_sm * hd]], axis=1)


def _lane_row(pieces):
    row = jnp.zeros((1, HEAD_DIM), F32)
    for off, a in pieces:
        row = lax.dynamic_update_slice(row, a.astype(F32), (0, off))
    return row


def _local_step(x, mem, target, prefetch, weights, reducer, sp):
    t, d = x.shape
    lay = _Layout(d)
    nf, ng, nm, hd = lay.nf, lay.ng, lay.nm, HEAD_DIM
    nch = t // CHUNK
    tq = _tile(t, 256)
    tk = tq

    u = _norm_fwd(x, 0, sp["norm_mix"], 1, d, BF16, name="norm_mix_fwd")
    prefetch("in", u)
    (win,) = weights("in", u)
    prefetch("mixer", win)
    p = _mm(u, win, name="mm_in")
    wmkv, conv_taps = weights("mixer", p)
    sp = dict(sp, gdn_conv=conv_taps)
    pa = _lane_row([(nf, sp["gdn_a_log"])])
    pb = _lane_row([(0, sp["fox_f_bias"]), (nf, sp["gdn_dt_bias"])])
    vals, csum = _small_fwd(p, lay.o_sm, pa, pb, nf, ng)

    c_t = csum[:, :nf].T
    cc, cr = c_t.reshape(nf, t, 1), c_t.reshape(nf, t // tk, 1, tk)
    fq = _norm_fwd(p, lay.o_fq, sp["fox_q_norm"], nf, hd, BF16, name="fox_qnorm_fwd")
    fk = _norm_fwd(p, lay.o_fk, sp["fox_k_norm"], nf, hd, BF16, name="fox_knorm_fwd")
    fv = p[:, lay.o_fv * hd:(lay.o_fv + nf) * hd].astype(BF16)
    o_fox, lse, mix = _fox_fwd(fq, fk, fv, cc, cr, nf, tq, tk, d)

    qkv = _conv_fwd(p, lay.o_gq, sp["gdn_conv"], ng)
    g_t, b_t = vals[:, nf:nf + ng].T, vals[:, nf + ng:nf + 2 * ng].T
    gcol, grow, bcol = g_t.reshape(ng, t, 1), g_t.reshape(ng, nch, 1, CHUNK), b_t.reshape(ng, t, 1)
    o_g, states = _gdn_fwd(qkv, gcol, grow, bcol, ng)
    mix = _norm_fwd(o_g, 0, sp["gdn_out_norm"], ng, hd, BF16, z=p, zoff=lay.o_gz, into=mix, into_off=nf,
                    name="gdn_out_fwd")
    prefetch("out", mix)

    mem_n = _norm_fwd(mem, 0, sp["mem_norm"], 1, d, BF16, name="mem_norm_fwd")
    mkv = _mm(mem_n, wmkv, name="mm_memkv")
    mix = _mem_fwd(p, lay.o_mq, mkv, sp["mem_q_norm"], sp["mem_k_norm"], tq, mix, nf + ng)
    prefetch("gate_up", mix)
    (wout,) = weights("out", mix)
    h1 = _mm(mix, wout, res=x, name="mm_out")
    n2 = _norm_fwd(h1, 0, sp["norm_ffn"], 1, d, BF16, name="norm_ffn_fwd")
    (wgu,) = weights("gate_up", n2)
    wgu4 = wgu.reshape(4, d, -1)
    gu, act = _ffn_up(n2, wgu4)
    prefetch("down", act)
    (wd,) = weights("down", act)
    h2 = _mm(act, wd, res=h1, name="mm_down")
    loss_blk, dh2, dh2_b = _loss_head(h2, target)

    g = {}
    token = reducer.pair("w_down", _mm(act, dh2_b, ta=True, out_dtype=BF16, name="mm_dw_down"))
    dgu = _ffn_dact(dh2_b, wd, gu, token)
    dw_gate_up = _mm(n2, dgu, ta=True, stack="out", out_dtype=BF16, name="mm_dw_gate_up").reshape(wgu.shape)
    token = reducer.pair("w_gate_up", dw_gate_up)
    dn2 = _mm(dgu, wgu4, tb=True, stack="sum", after=token, name="mm_dn2")
    token = reducer.ship("ffn", ["w_down", "w_gate_up"], dn2)
    dh1, g["norm_ffn"] = _norm_bwd(h1, 0, sp["norm_ffn"] + token[0, 0], dn2, 0, 1, d, res=dh2,
                                   name="norm_ffn_bwd")
    token = reducer.pair("w_out", _mm(mix, dh1, ta=True, out_dtype=BF16, name="mm_dw_out"))
    dmix = _mm(dh1, wout, tb=True, after=token, name="mm_dmix")

    dmq, dmk, dmv, g["mem_q_norm"], g["mem_k_norm"] = _mem_bwd(
        p, lay.o_mq, mkv, sp["mem_q_norm"], sp["mem_k_norm"], dmix, nf + ng, tq)
    dmkv = jnp.concatenate([dmk, dmv], axis=1)
    token = reducer.pair("w_mem_kv", _mm(mem_n, dmkv, ta=True, out_dtype=BF16, name="mm_dw_memkv"))
    dmem_n = _mm(dmkv, wmkv, tb=True, after=token, name="mm_dmem")
    token = reducer.ship("mix", ["w_out", "w_mem_kv"], dmem_n)
    _, g["mem_norm"] = _norm_bwd(mem, 0, sp["mem_norm"], dmem_n, 0, 1, d, name="mem_norm_bwd")

    do_g, dgz, g["gdn_out_norm"] = _norm_bwd(o_g, 0, sp["gdn_out_norm"] + token[0, 0], dmix, nf, ng, hd, z=p,
                                             zoff=lay.o_gz, name="gdn_out_bwd")
    dq, dk, dv, dgc, dgr, dbc = _gdn_bwd(qkv, gcol, grow, bcol, states, do_g, ng)
    dgqkv, g["gdn_conv"] = _conv_bwd(p, lay.o_gq, sp["gdn_conv"], (dq, dk, dv), ng)
    dg_t = dgc.reshape(ng, t) + dgr.reshape(ng, t)
    db_t = dbc.reshape(ng, t)

    dfq_n, dfk_n, dfv, dcc, dcr = _fox_bwd(fq, fk, fv, cc, cr, o_fox, lse, dmix, nf, tq, tk)
    dfq, g["fox_q_norm"] = _norm_bwd(p, lay.o_fq, sp["fox_q_norm"], dfq_n, 0, nf, hd, name="fox_qnorm_bwd")
    dfk, g["fox_k_norm"] = _norm_bwd(p, lay.o_fk, sp["fox_k_norm"], dfk_n, 0, nf, hd, name="fox_knorm_bwd")
    dc_t = dcc.reshape(nf, t) + dcr.reshape(nf, t)

    lanes_left = hd - nf - 2 * ng
    dvals = jnp.concatenate([jnp.zeros((t, nf), F32), dg_t.T, db_t.T, jnp.zeros((t, lanes_left), F32)], axis=1)
    dcsum = jnp.concatenate([dc_t.T, jnp.zeros((t, hd - nf), F32)], axis=1)
    dsm, dpa, dpb = _small_bwd(p, lay.o_sm, pa, pb, dvals, dcsum, nf, ng)
    g["fox_f_bias"] = dpb[:, :nf]
    g["gdn_dt_bias"] = dpb[:, nf:nf + ng]
    g["gdn_a_log"] = dpa[:, nf:nf + ng]

    pad = jnp.zeros((t, lay.cols - (lay.o_sm + 1) * hd), F32)
    dp = jnp.concatenate([dfq, dfk, dfv, dgqkv, dgz, dmq, dsm, pad], axis=1).astype(BF16)
    token = reducer.start("in", {"w_in": _mm(u, dp, ta=True, out_dtype=BF16, name="mm_dw_in")})
    du = _mm(dp, win, tb=True, after=token, name="mm_du")
    dx, g["norm_mix"] = _norm_bwd(x, 0, sp["norm_mix"], du, 0, 1, d, res=dh1, name="norm_mix_bwd")
    return loss_blk, dx, g


ANY = pl.BlockSpec(memory_space=pl.ANY)


def _me():
    x, y, c = lax.axis_index("x"), lax.axis_index("y"), lax.axis_index("c")
    chips = [(1 - x, y), (x, 1 - y), (1 - x, 1 - y)]
    return x, y, c, chips


def _slot(axis, k):
    return k if axis == 0 else 2 * (k % 2) + k // 2


def _slab(ref, axis, rows, cols, k, h):
    half = rows // 2
    return ref.at[pl.ds(_slot(axis, k) * rows + h * half, half), :]


def _remote(src, dst, send_sem, recv_sem, dev):
    return pltpu.make_async_remote_copy(src_ref=src, dst_ref=dst, send_sem=send_sem, recv_sem=recv_sem,
                                        device_id=dev, device_id_type=MESH)


HBM = pl.BlockSpec(memory_space=pltpu.HBM)
SEM = pl.BlockSpec(memory_space=pltpu.SEMAPHORE)
SPLIT = pltpu.CompilerParams(has_side_effects=pltpu.SideEffectType.DATAFLOW_SIDE_EFFECTING)
TOKEN = jax.ShapeDtypeStruct((8, HEAD_DIM), F32)


def _in_hbm(v):
    return pltpu.with_memory_space_constraint(v, pltpu.HBM)


def _cast_place(shard, axis, name, col_fn=None, out_cols=None):
    r, c = shard.shape
    oc = out_cols or c
    tr = _tile(r, 512 if col_fn is None else 64, 16)
    tc = _tile(c, 2048) if col_fn is None else c
    otc = tc if col_fn is None else oc
    nb = r // tr
    chip = 2 * lax.axis_index("x") + lax.axis_index("y")
    slot = jnp.reshape(_slot(axis, chip), (1,)).astype(jnp.int32)

    def body(slot_ref, x_ref, o_ref):
        x = x_ref[...]
        o_ref[...] = (x if col_fn is None else col_fn(x)).astype(BF16)

    return pl.pallas_call(
        body, name=name,
        grid_spec=pltpu.PrefetchScalarGridSpec(
            num_scalar_prefetch=1, grid=(nb, c // tc),
            in_specs=[pl.BlockSpec((tr, tc), lambda i, l, s: (i, l))],
            out_specs=pl.BlockSpec((tr, otc), lambda i, l, s: (s[0] * nb + i, l))),
        out_shape=jax.ShapeDtypeStruct((4 * r, oc), BF16),
        compiler_params=_cparams(("parallel", "parallel")),
    )(slot, shard)


def _gather_start(bufs, axes, shapes, groups, name):
    n = len(bufs)

    def body(*refs):
        dst = refs[n:2 * n]
        sems = refs[2 * n:2 * n + 2 * len(groups)]
        token = refs[-1]
        x, y, c, chips = _me()
        k = 2 * x + y
        for gi, ws in enumerate(groups):
            for i, w in enumerate(ws):
                r, cl = shapes[w]
                place = _slab(dst[w], axes[w], r, cl, k, c)
                for j, (px, py) in enumerate(chips):
                    _remote(place, place, sems[2 * gi].at[3 * i + j], sems[2 * gi + 1].at[3 * i + j],
                            (px, py, c)).start()
        token[...] = jnp.zeros_like(token)

    sem_shapes = [pltpu.SemaphoreType.DMA((3 * len(ws),)) for ws in groups for _ in range(2)]
    outs = pl.pallas_call(
        body, name=name, in_specs=[HBM] * n,
        out_specs=[HBM] * n + [SEM] * len(sem_shapes) + [pl.BlockSpec(memory_space=pltpu.VMEM)],
        out_shape=[pltpu.HBM(b.shape, b.dtype) for b in bufs] + sem_shapes + [TOKEN],
        input_output_aliases={w: w for w in range(n)}, compiler_params=SPLIT,
    )(*[_in_hbm(b) for b in bufs])
    sems = outs[n:-1]
    return outs[:n], [(sems[2 * g], sems[2 * g + 1]) for g in range(len(groups))], outs[-1]


def _gather_wait(bufs, axes, shapes, sems, after, name):
    n = len(bufs)

    def body(*refs):
        send_sems, recv_sems = refs[n], refs[n + 1]
        dst = refs[n + 3:]
        x, y, c, chips = _me()
        k = 2 * x + y
        for i in range(n):
            r, cl = shapes[i]
            for j, (px, py) in enumerate(chips):
                got = _slab(dst[i], axes[i], r, cl, 2 * px + py, c)
                _remote(got, got, send_sems.at[3 * i + j], recv_sems.at[3 * i + j], (px, py, c)).wait_recv()
        for i in range(n):
            r, cl = shapes[i]
            mine = _slab(dst[i], axes[i], r, cl, k, c)
            for j, (px, py) in enumerate(chips):
                _remote(mine, mine, send_sems.at[3 * i + j], recv_sems.at[3 * i + j], (px, py, c)).wait_send()

    return pl.pallas_call(
        body, name=name, in_specs=[HBM] * n + [SEM, SEM, ANY], out_specs=[HBM] * n,
        out_shape=[pltpu.HBM(b.shape, b.dtype) for b in bufs],
        input_output_aliases={i: i for i in range(n)}, compiler_params=SPLIT,
    )(*bufs, sems[0], sems[1], after)


def _gather_forward(bufs, axes, shapes, name):
    n = len(bufs)

    def body(*refs):
        dst = refs[n:2 * n]
        send_sems, recv_sems = refs[2 * n:]
        x, y, c, chips = _me()
        sibling = (x, y, 1 - c)
        sends = []
        for i in range(n):
            r, cl = shapes[i]
            for j, (px, py) in enumerate(chips):
                got = _slab(dst[i], axes[i], r, cl, 2 * px + py, c)
                cp = _remote(got, got, send_sems.at[3 * i + j], recv_sems.at[3 * i + j], sibling)
                cp.start()
                sends.append(cp)
        for i in range(n):
            r, cl = shapes[i]
            for j, (px, py) in enumerate(chips):
                got = _slab(dst[i], axes[i], r, cl, 2 * px + py, 1 - c)
                _remote(got, got, send_sems.at[3 * i + j], recv_sems.at[3 * i + j], sibling).wait_recv()
        for cp in sends:
            cp.wait_send()

    return pl.pallas_call(
        body, name=name, in_specs=[ANY] * n, out_specs=[ANY] * n,
        out_shape=[jax.ShapeDtypeStruct(b.shape, b.dtype) for b in bufs],
        input_output_aliases={i: i for i in range(n)},
        scratch_shapes=[pltpu.SemaphoreType.DMA((3 * n,)), pltpu.SemaphoreType.DMA((3 * n,))],
    )(*bufs)


def _split_start(name, arrays, geometry, count):
    n = len(arrays)

    def body(*refs):
        send, recv, token = refs[2 * n:]
        for i, (src, dst, _, dev) in enumerate(geometry(refs[n:2 * n])):
            _remote(src, dst, send.at[i], recv.at[i], dev).start()
        token[...] = jnp.zeros_like(token)

    sem = pltpu.SemaphoreType.DMA((count,))
    outs = pl.pallas_call(
        body, name=name, in_specs=[HBM] * n,
        out_specs=[HBM] * n + [SEM, SEM, pl.BlockSpec(memory_space=pltpu.VMEM)],
        out_shape=[pltpu.HBM(v.shape, v.dtype) for v in arrays] + [sem, sem, TOKEN],
        input_output_aliases={i: i for i in range(n)}, compiler_params=SPLIT,
    )(*[_in_hbm(v) for v in arrays])
    return list(outs[:n]), (outs[n], outs[n + 1]), outs[-1]


def _split_wait(name, arrays, sems, after, geometry):
    n = len(arrays)

    def body(*refs):
        send, recv = refs[n], refs[n + 1]
        copies = geometry(refs[n + 3:])
        for i, (_, _, land, dev) in enumerate(copies):
            _remote(land, land, send.at[i], recv.at[i], dev).wait_recv()
        for i, (src, _, _, dev) in enumerate(copies):
            _remote(src, src, send.at[i], recv.at[i], dev).wait_send()

    return list(pl.pallas_call(
        body, name=name, in_specs=[HBM] * n + [SEM, SEM, ANY], out_specs=[HBM] * n,
        out_shape=[pltpu.HBM(v.shape, v.dtype) for v in arrays],
        input_output_aliases={i: i for i in range(n)}, compiler_params=SPLIT,
    )(*arrays, sems[0], sems[1], after))


def _forward_geometry(axes, shapes):
    def geometry(bufs):
        x, y, c, chips = _me()
        out = []
        for i, buf in enumerate(bufs):
            r, cl = shapes[i]
            for px, py in chips:
                got = _slab(buf, axes[i], r, cl, 2 * px + py, c)
                out.append((got, got, _slab(buf, axes[i], r, cl, 2 * px + py, 1 - c), (x, y, 1 - c)))
        return out
    return geometry


def _pair_geometry(axes, shapes):
    def geometry(refs):
        n = len(refs) // 2
        x, y, c, _ = _me()
        out = []
        for w in range(n):
            r, cl = shapes[w]
            for j in range(4):
                land = refs[n + w].at[j]
                out.append((_slab(refs[w], axes[w], r, cl, j, 1 - c), land, land, (x, y, 1 - c)))
        return out
    return geometry


def _pair_exchange(fulls, axes, shapes, tag):
    n = len(fulls)

    def body(*refs):
        src, dst = refs[:n], refs[n:2 * n]
        send_sems, recv_sems = refs[2 * n:]
        x, y, c, _ = _me()
        sibling = (x, y, 1 - c)
        cps = []
        for w in range(n):
            r, cl = shapes[w]
            for j in range(4):
                cp = _remote(_slab(src[w], axes[w], r, cl, j, 1 - c), dst[w].at[j],
                             send_sems.at[4 * w + j], recv_sems.at[4 * w + j], sibling)
                cp.start()
                cps.append(cp)
        for cp in cps:
            cp.wait()

    out_shape = [jax.ShapeDtypeStruct((4, r // 2, cl), f.dtype) for (r, cl), f in zip(shapes, fulls)]
    return pl.pallas_call(
        body, name="reduce_pair_exchange_" + tag, in_specs=[ANY] * n, out_specs=[ANY] * n, out_shape=out_shape,
        scratch_shapes=[pltpu.SemaphoreType.DMA((4 * n,)), pltpu.SemaphoreType.DMA((4 * n,))],
    )(*fulls)


def _chip_start(parts, tag):
    n = len(parts)

    def body(*refs):
        src, land = refs[2 * n:3 * n], refs[3 * n:4 * n]
        send_sems, recv_sems, token = refs[4 * n:]
        x, y, c, chips = _me()
        k = 2 * x + y
        for w in range(n):
            for j, (px, py) in enumerate(chips):
                _remote(src[w].at[2 * px + py], land[w].at[k], send_sems.at[3 * w + j], recv_sems.at[3 * w + j],
                        (px, py, c)).start()
        token[...] = jnp.zeros_like(token)

    lands = [lax.empty(p.shape, p.dtype) for p in parts]
    sem = pltpu.SemaphoreType.DMA((3 * n,))
    outs = pl.pallas_call(
        body, name="reduce_ici_start_" + tag, in_specs=[HBM] * (2 * n),
        out_specs=[HBM] * (2 * n) + [SEM, SEM, pl.BlockSpec(memory_space=pltpu.VMEM)],
        out_shape=[pltpu.HBM(p.shape, p.dtype) for p in parts + lands] + [sem, sem, TOKEN],
        input_output_aliases={i: i for i in range(2 * n)}, compiler_params=SPLIT,
    )(*[_in_hbm(v) for v in parts + lands])
    return outs[:n], outs[n:2 * n], outs[2 * n], outs[2 * n + 1], outs[-1]


def _chip_wait(parts, lands, send_sems, recv_sems, after, tag):
    n = len(parts)

    def body(*refs):
        send, recv = refs[2 * n], refs[2 * n + 1]
        src, land = refs[2 * n + 3:3 * n + 3], refs[3 * n + 3:]
        x, y, c, chips = _me()
        for w in range(n):
            for j, (px, py) in enumerate(chips):
                got = land[w].at[2 * px + py]
                _remote(got, got, send.at[3 * w + j], recv.at[3 * w + j], (px, py, c)).wait_recv()
        for w in range(n):
            for j, (px, py) in enumerate(chips):
                sent = src[w].at[2 * px + py]
                _remote(sent, sent, send.at[3 * w + j], recv.at[3 * w + j], (px, py, c)).wait_send()

    outs = pl.pallas_call(
        body, name="reduce_ici_wait_" + tag, in_specs=[HBM] * (2 * n) + [SEM, SEM, ANY], out_specs=[HBM] * (2 * n),
        out_shape=[pltpu.HBM(p.shape, p.dtype) for p in parts + lands],
        input_output_aliases={i: i for i in range(2 * n)}, compiler_params=SPLIT,
    )(*parts, *lands, send_sems, recv_sems, after)
    chip = 2 * lax.axis_index("x") + lax.axis_index("y")
    return [lax.dynamic_update_slice(s, lax.dynamic_index_in_dim(p, chip, 0, keepdims=True), (chip, 0, 0))
            for p, s in zip(outs[:n], outs[n:])]


def _half_swap(halves, tag):
    n = len(halves)
    core = lax.axis_index("c")
    bufs = [lax.dynamic_update_slice(lax.empty((2,) + h.shape, h.dtype), h[None], (core, 0, 0)) for h in halves]

    def body(*refs):
        dst = refs[n:2 * n]
        send_sems, recv_sems = refs[2 * n:]
        x, y, c, _ = _me()
        sibling = (x, y, 1 - c)
        cps = []
        for w in range(n):
            cp = _remote(dst[w].at[c], dst[w].at[c], send_sems.at[w], recv_sems.at[w], sibling)
            cp.start()
            cps.append(cp)
        for w in range(n):
            other = dst[w].at[1 - c]
            _remote(other, other, send_sems.at[w], recv_sems.at[w], sibling).wait_recv()
        for cp in cps:
            cp.wait_send()

    outs = pl.pallas_call(
        body, name="reduce_half_swap_" + tag, in_specs=[ANY] * n, out_specs=[ANY] * n,
        out_shape=[jax.ShapeDtypeStruct(b.shape, b.dtype) for b in bufs],
        input_output_aliases={w: w for w in range(n)},
        scratch_shapes=[pltpu.SemaphoreType.DMA((n,)), pltpu.SemaphoreType.DMA((n,))],
    )(*bufs)
    return [o.reshape(2 * o.shape[1], o.shape[2]) for o in outs]


def _add_parts(full, axis, rows, sib, name):
    _, r, c = sib.shape
    tr, tc = _tile(r, 256, 16), _tile(c, 2048)
    nb = r // tr
    core = jnp.reshape(lax.axis_index("c"), (1,)).astype(jnp.int32)

    def body(c_ref, a_ref, b_ref, o_ref):
        o_ref[0] = (a_ref[...].astype(F32) + b_ref[0].astype(F32)).astype(BF16)

    blk = pl.BlockSpec((1, tr, tc), lambda j, i, l, cr: (j, i, l))
    return pl.pallas_call(
        body, name=name,
        grid_spec=pltpu.PrefetchScalarGridSpec(
            num_scalar_prefetch=1, grid=(4, nb, c // tc),
            in_specs=[pl.BlockSpec((tr, tc), lambda j, i, l, cr: ((_slot(axis, j) * 2 + cr[0]) * nb + i, l)), blk],
            out_specs=blk),
        out_shape=jax.ShapeDtypeStruct(sib.shape, BF16),
        compiler_params=_cparams(("parallel", "parallel", "parallel")),
    )(core, full, sib)


def _sum_slots(a, name):
    _, r, c = a.shape
    tr, tc = _tile(r, 256, 8), _tile(c, 2048)

    def body(a_ref, o_ref):
        v = a_ref[...].astype(F32)
        o_ref[...] = ((v[0] + v[1]) + v[2]) + v[3]

    return pl.pallas_call(
        body, name=name, grid=(r // tr, c // tc),
        in_specs=[pl.BlockSpec((4, tr, tc), lambda i, l: (0, i, l))],
        out_specs=pl.BlockSpec((tr, tc), lambda i, l: (i, l)),
        out_shape=jax.ShapeDtypeStruct((r, c), F32),
        compiler_params=_cparams(("parallel", "parallel")),
    )(a)


class _Reducer:
    def __init__(self, spec):
        self.spec = spec
        self.paired = {}
        self.pending = []

    def pair(self, name, full):
        ax, shp = self.spec[name]
        land = lax.empty((4, shp[0] // 2, shp[1]), full.dtype)
        arrays, sems, token = _split_start("reduce_pair_start_" + name, [full, land], _pair_geometry([ax], [shp]), 4)
        self.paired[name] = (arrays, sems)
        return token

    def ship(self, tag, names, after):
        parts = []
        for n in names:
            ax, shp = self.spec[n]
            arrays, sems = self.paired.pop(n)
            full, sib = _split_wait("reduce_pair_wait_" + n, arrays, sems, after, _pair_geometry([ax], [shp]))
            parts.append(_add_parts(full, ax, shp[0], sib, name=f"reduce_add_{n}"))
        parts, lands, send, recv, token = _chip_start(parts, tag)
        self.pending.append((tag, names, parts, lands, send, recv))
        return token

    def start(self, tag, grads):
        names = list(grads)
        fulls, axes = [grads[n] for n in names], [self.spec[n][0] for n in names]
        shapes = [self.spec[n][1] for n in names]
        from_sibling = _pair_exchange(fulls, axes, shapes, tag)
        parts = [_add_parts(f, a, r, s, name=f"reduce_add_{n}")
                 for n, f, a, (r, cl), s in zip(names, fulls, axes, shapes, from_sibling)]
        parts, lands, send, recv, token = _chip_start(parts, tag)
        self.pending.append((tag, names, parts, lands, send, recv))
        return token

    def finish(self, after, tags):
        out = {}
        for tag, names, parts, lands, send, recv in [p for p in self.pending if p[0] in tags]:
            slots = _chip_wait(parts, lands, send, recv, after, tag)
            halves = [_sum_slots(s, name=f"reduce_sum_{n}") for n, s in zip(names, slots)]
            out.update(zip(names, _half_swap(halves, tag)))
        return out


def _allreduce_small(pack, after):
    rows = pack.shape[0]

    def body(p_ref, _, o_ref, slots, send_sems, recv_sems):
        x, y, c, _ = _me()
        me = 4 * x + 2 * y + c
        slots[me] = p_ref[...]
        cps = []
        for r in range(1, 8):
            peer = (x ^ (r >> 2), y ^ ((r >> 1) & 1), c ^ (r & 1))
            cp = _remote(p_ref, slots.at[me], send_sems.at[r - 1], recv_sems.at[r - 1], peer)
            cp.start()
            cps.append(cp)
        for r in range(1, 8):
            frm = me ^ r
            _remote(slots.at[frm], slots.at[frm], send_sems.at[r - 1], recv_sems.at[r - 1], (x, y, c)).wait_recv()
        for cp in cps:
            cp.wait_send()
        acc = slots[0]
        for s in range(1, 8):
            acc = acc + slots[s]
        o_ref[...] = acc

    vm = pl.BlockSpec(memory_space=pltpu.VMEM)
    return pl.pallas_call(
        body, name="allreduce_small", in_specs=[vm, ANY], out_specs=vm,
        out_shape=jax.ShapeDtypeStruct(pack.shape, F32),
        scratch_shapes=[pltpu.VMEM((8, rows, HEAD_DIM), F32), pltpu.SemaphoreType.DMA((7,)),
                        pltpu.SemaphoreType.DMA((7,))],
    )(pack, after)


_ROWS = ["norm_mix", "norm_ffn", "mem_norm", "fox_q_norm", "fox_k_norm", "gdn_out_norm", "mem_q_norm",
         "mem_k_norm", "fox_f_bias", "gdn_a_log", "gdn_dt_bias"]


def _pack_rows(vals):
    out = []
    for name in _ROWS:
        v = vals[name].reshape(-1)
        n = -(-v.shape[0] // HEAD_DIM) * HEAD_DIM
        out.append(jnp.pad(v, (0, n - v.shape[0])).reshape(-1, HEAD_DIM))
    return jnp.concatenate(out, axis=0)


def _unpack_rows(pack, like):
    out, r = {}, 0
    for name in _ROWS:
        n = like[name].shape[-1]
        nr = -(-n // HEAD_DIM)
        out[name] = pack[r:r + nr].reshape(1, -1)[:, :n]
        r += nr
    return out, r


def kernel(x, mem, norm_mix, w_in, fox_f_bias, fox_q_norm, fox_k_norm, gdn_conv, gdn_a_log, gdn_dt_bias, gdn_out_norm, mem_norm, w_mem_kv, mem_q_norm, mem_k_norm, w_out, norm_ffn, w_gate_up, w_down, loss_target, m_norm_mix, m_w_in, m_fox_f_bias, m_fox_q_norm, m_fox_k_norm, m_gdn_conv, m_gdn_a_log, m_gdn_dt_bias, m_gdn_out_norm, m_mem_norm, m_w_mem_kv, m_mem_q_norm, m_mem_k_norm, m_w_out, m_norm_ffn, m_w_gate_up, m_w_down, v_norm_mix, v_w_in, v_fox_f_bias, v_fox_q_norm, v_fox_k_norm, v_gdn_conv, v_gdn_a_log, v_gdn_dt_bias, v_gdn_out_norm, v_mem_norm, v_w_mem_kv, v_mem_q_norm, v_mem_k_norm, v_w_out, v_norm_ffn, v_w_gate_up, v_w_down):
    a = dict(locals())
    d = x.shape[-1]
    lay = _Layout(d)
    chip = 2 * lax.axis_index("x") + lax.axis_index("y")
    small = {n: a[n] for n in _ROWS}
    big = ["w_in", "w_mem_kv", "w_out", "w_gate_up", "w_down"]
    axes = [0, 0, 0, 1, 0]

    conv_cols = gdn_conv.shape[-1]
    conv_n = CONV_WIDTH * conv_cols
    conv_rows = -(-conv_n // HEAD_DIM)
    conv_blk = jnp.pad(gdn_conv.reshape(-1), (0, 32 * HEAD_DIM - conv_n)).reshape(32, HEAD_DIM)
    axis_of = dict(zip(big, axes), conv=0)
    shape_of = {n: a[n].shape[1:] for n in big}
    shape_of["w_in"] = (w_in.shape[1], lay.cols)
    shape_of["conv"] = conv_blk.shape
    placed = {n: _cast_place(a[n][0], axis_of[n], "cast_" + n) for n in big[1:]}
    placed["w_in"] = _cast_place(w_in[0], 0, "cast_w_in", lay.regroup, lay.cols)
    placed["conv"] = lax.dynamic_update_slice(lax.empty((4 * 32, HEAD_DIM), F32), conv_blk, (chip * 32, 0))
    grouped = {"in": ["w_in"], "mixer": ["w_mem_kv", "conv"], "out": ["w_out"], "gate_up": ["w_gate_up"],
               "down": ["w_down"]}
    inflight = {}

    def start(tags, name):
        names = [n for t in tags for n in grouped[t]]
        bufs, sems, _ = _gather_start([placed[n] for n in names], [axis_of[n] for n in names],
                                      [shape_of[n] for n in names],
                                      [[names.index(n) for n in grouped[t]] for t in tags], name)
        for t, pair in zip(tags, sems):
            inflight[t] = ([bufs[names.index(n)] for n in grouped[t]], pair)

    start(["in"], "gather_ici_start_in")
    start(["mixer", "out", "gate_up", "down"], "gather_ici_start_rest")

    forwarding = {}

    def prefetch(tag, after):
        bufs, sem_pair = inflight.pop(tag)
        ax, shp = [axis_of[n] for n in grouped[tag]], [shape_of[n] for n in grouped[tag]]
        got = _gather_wait(bufs, ax, shp, sem_pair, after, "gather_ici_wait_" + tag)
        geometry = _forward_geometry(ax, shp)
        got, sems, _ = _split_start("gather_forward_start_" + tag, got, geometry, 3 * len(got))
        forwarding[tag] = (got, sems, geometry)

    def weights(tag, after):
        got, sems, geometry = forwarding.pop(tag)
        got = _split_wait("gather_forward_wait_" + tag, got, sems, after, geometry)
        if tag != "mixer":
            return got
        taps = got[1].reshape(4, 32 * HEAD_DIM)[:, :conv_n].reshape(4, CONV_WIDTH, conv_cols)
        return got[0], jnp.transpose(taps, (1, 0, 2)).reshape(CONV_WIDTH, 4 * conv_cols)

    sp = dict(small)
    reducer = _Reducer({n: (axis_of[n], shape_of[n]) for n in big})
    loss_blk, dx, g = _local_step(x[0], mem[0], loss_target[0], prefetch, weights, reducer, sp)

    gsmall = {n: g[n] for n in _ROWS}
    pack = jnp.concatenate([_pack_rows(gsmall), g["gdn_conv"].reshape(-1, HEAD_DIM), loss_blk], axis=0)
    pack = jnp.pad(pack, ((0, -pack.shape[0] % 8), (0, 0)))
    out = {"grad_x": dx[None]}

    def adamw_shards(reduced):
        for n, gsh in reduced.items():
            res = _adamw(a[n][0], gsh, a["m_" + n][0], a["v_" + n][0], g_fn=lay.ungroup if n == "w_in" else None,
                         name="adamw_" + n)
            for pre, r in zip(["grad_", "delta_", "new_m_", "new_v_"], res):
                out[pre + n] = r[None]
        return res[0]

    done = adamw_shards(reducer.finish(dx, ("ffn", "mix")))
    tot = _allreduce_small(pack, done)
    gs, r0 = _unpack_rows(tot, small)
    conv_g = tot[r0:r0 + CONV_WIDTH * 4 * conv_cols // HEAD_DIM].reshape(CONV_WIDTH, 4 * conv_cols)
    gs_conv = lax.dynamic_slice_in_dim(conv_g, chip * conv_cols, conv_cols, axis=1)
    out["loss"] = tot[r0 + CONV_WIDTH * 4 * conv_cols // HEAD_DIM, 0]
    adamw_shards(reducer.finish(tot, ("in",)))
    conv_pad = lambda v: jnp.pad(v.reshape(-1), (0, conv_rows * HEAD_DIM - conv_n)).reshape(conv_rows, HEAD_DIM)
    packs = []
    for src, cv in [(small, gdn_conv), (gs, gs_conv), ({n: a["m_" + n] for n in _ROWS}, m_gdn_conv),
                    ({n: a["v_" + n] for n in _ROWS}, v_gdn_conv)]:
        packs.append(jnp.concatenate([_pack_rows(src), conv_pad(cv)], axis=0))
    res = _adamw(*packs, name="adamw_small")
    for pre, r in zip(["grad_", "delta_", "new_m_", "new_v_"], res):
        vals, r1 = _unpack_rows(r, small)
        for n in _ROWS:
            out[pre + n] = vals[n]
        out[pre + "gdn_conv"] = r[r1:r1 + conv_rows].reshape(-1)[:conv_n].reshape(gdn_conv.shape)
    names = ["norm_mix", "w_in", "fox_f_bias", "fox_q_norm", "fox_k_norm", "gdn_conv", "gdn_a_log", "gdn_dt_bias",
             "gdn_out_norm", "mem_norm", "w_mem_kv", "mem_q_norm", "mem_k_norm", "w_out", "norm_ffn", "w_gate_up",
             "w_down"]
    return (out["loss"], out["grad_x"], *[out[p + n] for p in ["grad_", "delta_", "new_m_", "new_v_"] for n in names])
```

```python
import functools
import math

import jax
import jax.numpy as jnp
from jax import lax
from jax.experimental import pallas as pl
from jax.experimental.pallas import tpu as pltpu

F32, BF16 = jnp.float32, jnp.bfloat16
HEAD_DIM = 128
CHUNK = 64
N_MEM_HEADS = 4
CONV_WIDTH = 4
NORM_EPS = 1e-6
ADAM_LR, ADAM_B1, ADAM_B2, ADAM_EPS, ADAM_WD, ADAM_STEP = 0.001, 0.9, 0.999, 1e-08, 0.01, 10
VMEM_LIMIT = 48 * 1024 * 1024
NEG = -1e30
MESH = pl.DeviceIdType.MESH


def _cparams(sem=None, **kw):
    if sem is not None:
        kw["dimension_semantics"] = sem
    return pltpu.CompilerParams(vmem_limit_bytes=VMEM_LIMIT, **kw)


def _tile(n, target, mult=128):
    best = None
    d = mult
    while d <= min(n, target):
        if n % d == 0:
            best = d
        d += mult
    return best if best is not None else n


def _dot(a, b, dims, hi):
    if a.ndim == 3:
        dn = (((dims[0][0] + 1,), (dims[1][0] + 1,)), ((0,), (0,)))
    else:
        dn = (dims, ((), ()))
    if hi is not None:
        return lax.dot_general(a, b, dn, precision=hi, preferred_element_type=F32)
    return lax.dot_general(a.astype(BF16), b.astype(BF16), dn, preferred_element_type=F32)


def _make_dots(hi, cotangent=None):
    @jax.custom_vjp
    def nn(a, b):
        return _dot(a, b, ((1,), (0,)), hi)

    @jax.custom_vjp
    def nt(a, b):
        return _dot(a, b, ((1,), (1,)), hi)

    @jax.custom_vjp
    def tn(a, b):
        return _dot(a, b, ((0,), (0,)), hi)

    bnn, bnt, btn = cotangent or (nn, nt, tn)
    nn.defvjp(lambda a, b: (nn(a, b), (a, b)), lambda r, g: (bnt(g, r[1]), btn(r[0], g)))
    nt.defvjp(lambda a, b: (nt(a, b), (a, b)), lambda r, g: (bnn(g, r[1]), btn(g, r[0])))
    tn.defvjp(lambda a, b: (tn(a, b), (a, b)), lambda r, g: (bnt(r[1], g), bnn(r[0], g)))
    return nn, nt, tn


_nn, _nt, _tn = _make_dots(None)
_nn_hi, _nt_hi, _tn_hi = _make_dots(lax.Precision.HIGHEST)
_nn_x3, _nt_x3, _tn_x3 = _make_dots(lax.Precision.HIGH, (_nn, _nt, _tn))


def _sigmoid(x):
    return 1.0 / (1.0 + jnp.exp(-x))


@jax.custom_vjp
def _softplus(x):
    return jnp.maximum(x, 0.0) + jnp.log(1.0 + jnp.exp(-jnp.abs(x)))


_softplus.defvjp(lambda x: (_softplus(x), x), lambda x, g: (g * _sigmoid(x),))


def _silu(x):
    return x * _sigmoid(x)


def _rms_fn(x, gain, z=None):
    y = x * lax.rsqrt(jnp.mean(x * x, axis=-1, keepdims=True) + NORM_EPS) * gain
    if z is not None:
        y = y * _silu(z)
    return y


def _mm(a, b, *, ta=False, tb=False, out_dtype=F32, res=None, stack=None, after=None, name):
    a2, b2 = a.shape[-2:], b.shape[-2:]
    ns = b.shape[0] if stack else 1
    m = a2[1] if ta else a2[0]
    k = a2[0] if ta else a2[1]
    n = b2[0] if tb else b2[1]
    assert k == (b2[1] if tb else b2[0])
    tm, tn, tk = _mm_tiles(m, n, k, ns if stack == "sum" else 1, a.dtype.itemsize, b.dtype.itemsize,
                           jnp.dtype(out_dtype).itemsize, res is not None)
    nk = k // tk
    single = nk == 1 and stack != "sum"
    dims = ((0 if ta else 1,), (1 if tb else 0,))
    if stack == "sum":
        order = lambda g0, g1, g2, g3: (g2, g0, g1, g3)
        grid = (m // tm, n // tn, ns, nk)
    else:
        order = lambda g0, g1, g2, g3: (g0, g1, g2, g3)
        grid = (ns, m // tm, n // tn, nk)

    def body(*refs):
        if after is not None:
            refs = refs[:2 + (res is not None)] + refs[3 + (res is not None):]
        if single:
            a_ref, b_ref = refs[:2]
            r = lax.dot_general(a_ref[...].astype(BF16), b_ref[...].astype(BF16), (dims, ((), ())),
                                preferred_element_type=F32)
            if res is not None:
                r = r + refs[2][...]
            refs[-1][...] = r.astype(out_dtype)
            return
        if res is None:
            a_ref, b_ref, o_ref, acc = refs
        else:
            a_ref, b_ref, r_ref, o_ref, acc = refs
        s, _, _, kk = order(*[pl.program_id(d) for d in range(4)])
        first = kk == 0
        last = kk == nk - 1
        if stack == "sum":
            first, last = first & (s == 0), last & (s == ns - 1)

        @pl.when(first)
        def _():
            acc[...] = jnp.zeros_like(acc)

        acc[...] += lax.dot_general(a_ref[...].astype(BF16), b_ref[...].astype(BF16), (dims, ((), ())),
                                    preferred_element_type=F32)

        @pl.when(last)
        def _():
            r = acc[...]
            if res is not None:
                r = r + r_ref[...]
            o_ref[...] = r.astype(out_dtype)

    def spec(shape, idx, stacked):
        if stacked:
            return pl.BlockSpec((None,) + shape, lambda *g: (order(*g)[0],) + idx(*order(*g)))
        return pl.BlockSpec(shape, lambda *g: idx(*order(*g)))

    a_spec = (spec((tk, tm), lambda s, i, j, kk: (kk, i), stack == "sum") if ta
              else spec((tm, tk), lambda s, i, j, kk: (i, kk), stack == "sum"))
    b_spec = (spec((tn, tk), lambda s, i, j, kk: (j, kk), bool(stack)) if tb
              else spec((tk, tn), lambda s, i, j, kk: (kk, j), bool(stack)))
    o_spec = spec((tm, tn), lambda s, i, j, kk: (i, j), stack == "out")
    ins, specs = [a, b], [a_spec, b_spec]
    if res is not None:
        ins.append(res)
        specs.append(o_spec)
    if after is not None:
        ins.append(after)
        specs.append(pl.BlockSpec(after.shape, lambda *g: (0,) * after.ndim))
    sem = (("parallel", "parallel", "arbitrary", "arbitrary") if stack == "sum"
           else ("parallel", "parallel", "parallel", "arbitrary"))
    return pl.pallas_call(
        body, name=name, grid=grid, in_specs=specs, out_specs=o_spec,
        out_shape=jax.ShapeDtypeStruct(((ns,) if stack == "out" else ()) + (m, n), out_dtype),
        scratch_shapes=[] if single else [pltpu.VMEM((tm, tn), F32)],
        compiler_params=_cparams(sem),
    )(*ins)


MM_VMEM_BUDGET = 40 * 1024 * 1024


def _mm_tiles(m, n, k, ns, sa, sb, so, has_res):
    def divs(x, mult, cap):
        out = [d for d in range(mult, min(x, cap) + 1, mult) if x % d == 0]
        return out or [x]

    best = None
    for tk in divs(k, 128, 8192):
        nk = (k // tk) * ns
        for tm in divs(m, 8, 2048):
            for tn in divs(n, 128, 2048):
                vmem = 2 * (tm * tk * sa + tk * tn * sb + tm * tn * so) + (2 * tm * tn * 4 if has_res else 0)
                vmem += tm * tn * 4 if nk > 1 else 0
                if vmem > MM_VMEM_BUDGET:
                    continue
                steps = (m // tm) * (n // tn) * nk
                traffic = (m // tm) * k * n * sb * ns + (n // tn if nk > 1 else 1) * m * k * sa * ns
                cost = steps * 0.4e-6 + traffic / 2.5e12 + (nk * m * n * 8 / 6e12 if nk > 1 else 0)
                cost += 2.0 * m * n * k * ns / 7e14
                if best is None or cost < best[0]:
                    best = (cost, tm, tn, tk)
    return best[1:]


def _norm_fwd(x, xoff, gain, ncol, w, out_dtype, *, z=None, zoff=0, into=None, into_off=0, name):
    t = x.shape[0]
    tr = _tile(t, max(256, (1 << 18) // w), 8)

    def body(*refs):
        x_ref, g_ref, o_ref = refs[0], refs[1], refs[-1]
        y = _rms_fn(x_ref[...], g_ref[...]) if z is None else _rms_fn(x_ref[...], g_ref[...], refs[2][...])
        o_ref[...] = y.astype(out_dtype)

    ins = [x, gain]
    specs = [pl.BlockSpec((tr, w), lambda j, r: (r, xoff + j)), pl.BlockSpec((1, w), lambda j, r: (0, 0))]
    if z is not None:
        ins.append(z)
        specs.append(pl.BlockSpec((tr, w), lambda j, r: (r, zoff + j)))
    aliases = {}
    if into is not None:
        aliases = {len(ins): 0}
        ins.append(into)
        specs.append(pl.BlockSpec(memory_space=pl.ANY))
    return pl.pallas_call(
        body, name=name, grid=(ncol, t // tr), in_specs=specs,
        out_specs=pl.BlockSpec((tr, w), lambda j, r: (r, into_off + j)),
        out_shape=jax.ShapeDtypeStruct((t, ncol * w) if into is None else into.shape, out_dtype),
        input_output_aliases=aliases, compiler_params=_cparams(("parallel", "parallel")),
    )(*ins)


def _norm_bwd(x, xoff, gain, dy, dyoff, ncol, w, *, z=None, zoff=0, res=None, name):
    t = x.shape[0]
    tr = _tile(t, max(256, (1 << 18) // w), 8)

    def body(*refs):
        it = iter(refs)
        x_ref, g_ref = next(it), next(it)
        z_ref = next(it) if z is not None else None
        dy_ref = next(it)
        r_ref = next(it) if res is not None else None
        dx_ref = next(it)
        dz_ref = next(it) if z is not None else None
        dg_ref = next(it)

        @pl.when((pl.program_id(0) == 0) & (pl.program_id(1) == 0))
        def _():
            dg_ref[...] = jnp.zeros_like(dg_ref)

        args = (x_ref[...], g_ref[...]) + ((z_ref[...],) if z is not None else ())
        _, vjp = jax.vjp(_rms_fn, *args)
        grads = vjp(dy_ref[...].astype(F32))
        dx = grads[0]
        if res is not None:
            dx = dx + r_ref[...]
        dx_ref[...] = dx
        if z is not None:
            dz_ref[...] = grads[2]
        dg_ref[...] += grads[1]

    ins = [x, gain]
    specs = [pl.BlockSpec((tr, w), lambda j, r: (r, xoff + j)), pl.BlockSpec((1, w), lambda j, r: (0, 0))]
    if z is not None:
        ins.append(z)
        specs.append(pl.BlockSpec((tr, w), lambda j, r: (r, zoff + j)))
    ins.append(dy)
    specs.append(pl.BlockSpec((tr, w), lambda j, r: (r, dyoff + j)))
    blk = pl.BlockSpec((tr, w), lambda j, r: (r, j))
    if res is not None:
        ins.append(res)
        specs.append(blk)
    full = jax.ShapeDtypeStruct((t, ncol * w), F32)
    out_shape, out_specs = [full], [blk]
    if z is not None:
        out_shape.append(full)
        out_specs.append(blk)
    out_shape.append(jax.ShapeDtypeStruct((1, w), F32))
    out_specs.append(pl.BlockSpec((1, w), lambda j, r: (0, 0)))
    return pl.pallas_call(
        body, name=name, grid=(ncol, t // tr), in_specs=specs, out_specs=out_specs, out_shape=out_shape,
        compiler_params=_cparams(("arbitrary", "arbitrary")),
    )(*ins)


def _small_fn(x, pa, pb, nf, ng):
    lane = lax.broadcasted_iota(jnp.int32, x.shape, 1)
    zz = x + pb
    logf = -_softplus(-zz)
    g = -jnp.exp(pa) * _softplus(zz)
    beta = _sigmoid(x)
    return jnp.where(lane < nf, logf, jnp.where(lane < nf + ng, g, beta))


def _tri(n, upper):
    r = lax.broadcasted_iota(jnp.int32, (n, n), 0)
    c = lax.broadcasted_iota(jnp.int32, (n, n), 1)
    return jnp.where((c >= r) if upper else (c <= r), 1.0, 0.0).astype(F32)


def _small_fwd(p, off, pa, pb, nf, ng):
    t = p.shape[0]
    blk = HEAD_DIM
    nb = t // blk

    def body(x_ref, pa_ref, pb_ref, v_ref, c_ref):
        v_ref[...] = _small_fn(x_ref[...], pa_ref[...], pb_ref[...], nf, ng)
        tri = _tri(blk, False)

        carry = jnp.zeros((1, HEAD_DIM), F32)
        for i in range(nb):
            rows = slice(i * blk, (i + 1) * blk)
            c = _nn_hi(tri, v_ref[rows, :]) + carry
            c_ref[rows, :] = c
            carry = c[blk - 1:blk, :]

    row = pl.BlockSpec((1, HEAD_DIM), lambda i: (0, 0))
    out = pl.BlockSpec((t, HEAD_DIM), lambda i: (0, 0))
    return pl.pallas_call(
        body, name="small_fwd", grid=(1,),
        in_specs=[pl.BlockSpec((t, HEAD_DIM), lambda i: (0, off)), row, row], out_specs=[out, out],
        out_shape=[jax.ShapeDtypeStruct((t, HEAD_DIM), F32)] * 2,
        compiler_params=_cparams(("arbitrary",)),
    )(p, pa, pb)


def _small_bwd(p, off, pa, pb, dvals, dcsum, nf, ng):
    t = p.shape[0]
    blk = HEAD_DIM
    nb = t // blk

    def body(x_ref, pa_ref, pb_ref, dv_ref, dc_ref, dx_ref, dpa_ref, dpb_ref, tot_ref):
        tri = _tri(blk, True)

        carry = jnp.zeros((1, HEAD_DIM), F32)
        for i in reversed(range(nb)):
            rows = slice(i * blk, (i + 1) * blk)
            c = _nn_hi(tri, dc_ref[rows, :]) + carry
            tot_ref[rows, :] = c + dv_ref[rows, :]
            carry = c[0:1, :]
        f = functools.partial(_small_fn, nf=nf, ng=ng)
        _, vjp = jax.vjp(f, x_ref[...], pa_ref[...], pb_ref[...])
        dx, dpa, dpb = vjp(tot_ref[...])
        dx_ref[...] = dx
        dpa_ref[...] = dpa
        dpb_ref[...] = dpb

    row = pl.BlockSpec((1, HEAD_DIM), lambda i: (0, 0))
    full = pl.BlockSpec((t, HEAD_DIM), lambda i: (0, 0))
    return pl.pallas_call(
        body, name="small_bwd", grid=(1,),
        in_specs=[pl.BlockSpec((t, HEAD_DIM), lambda i: (0, off)), row, row, full, full],
        out_specs=[full, row, row],
        out_shape=[jax.ShapeDtypeStruct((t, HEAD_DIM), F32), jax.ShapeDtypeStruct((1, HEAD_DIM), F32),
                   jax.ShapeDtypeStruct((1, HEAD_DIM), F32)],
        scratch_shapes=[pltpu.VMEM((t, HEAD_DIM), F32)],
        compiler_params=_cparams(("arbitrary",)),
    )(p, pa, pb, dvals, dcsum)


def _fox_heads(nf, most):
    return next(h for h in range(most, 0, -1) if nf % h == 0)


def _fox_fwd(q, k, v, cc, cr, nf, tq, tk, d_mix):
    t = q.shape[0]
    scale = HEAD_DIM ** -0.5
    assert tq == tk

    vt = jnp.transpose(v.reshape(t // tk, tk, nf, HEAD_DIM), (2, 0, 3, 1))

    hp = _fox_heads(nf, 3)
    lanes = lambda h: slice(h * HEAD_DIM, (h + 1) * HEAD_DIM)

    def body(q_ref, k_ref, vt_ref, cc_ref, cr_ref, o_ref, lse_ref, mix_ref):
        i = pl.program_id(1)
        qs = [q_ref[:, lanes(h)] for h in range(hp)]
        cqs = [cr_ref[h, i] for h in range(hp)]
        ones = jnp.ones((8, tk), BF16)
        diff = lax.broadcasted_iota(jnp.int32, (tk, tq), 0) - lax.broadcasted_iota(jnp.int32, (tk, tq), 1)

        def scores(h, j):
            ks = pl.ds(pl.multiple_of(j * tk, tk), tk)
            return lax.dot_general(k_ref[ks, lanes(h)], qs[h], (((1,), (1,)), ((), ())),
                                   preferred_element_type=F32)

        def tile(h, j, m, l, acc, s, masked):
            ks = pl.ds(pl.multiple_of(j * tk, tk), tk)
            s = s * scale + cqs[h] - cc_ref[h, ks, :]
            if masked:
                s = jnp.where(diff <= 0, s, NEG)
            m_new = jnp.maximum(m, jnp.max(s, axis=0, keepdims=True))
            pr = jnp.exp(s - m_new).astype(BF16)
            alpha = jnp.exp(m - m_new)
            l = alpha * l + jnp.dot(ones, pr, preferred_element_type=F32)[:1]
            acc = alpha * acc + jnp.dot(vt_ref[h, j], pr, preferred_element_type=F32)
            return m_new, l, acc

        def step(j, carry):
            nxt = [scores(h, j + 1) for h in range(hp)]
            return tuple(tile(h, j, *carry[h], False) + (nxt[h],) for h in range(hp))

        init = tuple((jnp.full((1, tq), NEG, F32), jnp.zeros((1, tq), F32), jnp.zeros((HEAD_DIM, tq), F32),
                      scores(h, 0)) for h in range(hp))
        carry = lax.fori_loop(0, i, step, init)
        for h in range(hp):
            m, l, acc = tile(h, i, *carry[h], True)
            o = jnp.transpose(acc / l)
            o_ref[:, lanes(h)] = o
            mix_ref[:, lanes(h)] = o.astype(BF16)
            lse_ref[h, 0] = m + jnp.log(l)

    w = hp * HEAD_DIM
    qblk = pl.BlockSpec((tq, w), lambda h, i: (i, h))
    return pl.pallas_call(
        body, name="fox_fwd", grid=(nf // hp, t // tq),
        in_specs=[qblk, pl.BlockSpec((t, w), lambda h, i: (0, h)),
                  pl.BlockSpec((hp, t // tk, HEAD_DIM, tk), lambda h, i: (h, 0, 0, 0)),
                  pl.BlockSpec((hp, t, 1), lambda h, i: (h, 0, 0)),
                  pl.BlockSpec((hp, t // tk, 1, tk), lambda h, i: (h, 0, 0, 0))],
        out_specs=[qblk, pl.BlockSpec((hp, 1, 1, tq), lambda h, i: (h, i, 0, 0)), qblk],
        out_shape=[jax.ShapeDtypeStruct((t, nf * HEAD_DIM), F32), jax.ShapeDtypeStruct((nf, t // tq, 1, tq), F32),
                   jax.ShapeDtypeStruct((t, d_mix), BF16)],
        compiler_params=_cparams(("parallel", "parallel")),
    )(q, k, vt, cc, cr)


def _fox_bwd(q, k, v, cc, cr, o, lse, dmix, nf, tq, tk):
    t = q.shape[0]
    scale = HEAD_DIM ** -0.5
    assert tq == tk
    hp = _fox_heads(nf, 3)
    lanes = lambda h: slice(h * HEAD_DIM, (h + 1) * HEAD_DIM)
    kt = jnp.transpose(k.reshape(t // tk, tk, nf, HEAD_DIM), (2, 0, 3, 1))

    def body(q_ref, k_ref, kt_ref, v_ref, cc_ref, cr_ref, o_ref, lse_ref, do_ref,
             dq_ref, dk_ref, dv_ref, dcq_ref, dck_ref):
        i = pl.program_id(1)

        @pl.when(i == 0)
        def _():
            dk_ref[...] = jnp.zeros_like(dk_ref)
            dv_ref[...] = jnp.zeros_like(dv_ref)
            dck_ref[...] = jnp.zeros_like(dck_ref)

        diff = lax.broadcasted_iota(jnp.int32, (tk, tq), 0) - lax.broadcasted_iota(jnp.int32, (tk, tq), 1)
        qs = [q_ref[:, lanes(h)] for h in range(hp)]
        dos = [do_ref[:, lanes(h)] for h in range(hp)]
        do_b = [d.astype(BF16) for d in dos]
        cqs = [cr_ref[h, i] for h in range(hp)]
        lses = [lse_ref[h, 0] for h in range(hp)]
        deltas = [jnp.sum(jnp.transpose(dos[h] * o_ref[:, lanes(h)]), axis=0, keepdims=True) for h in range(hp)]

        def products(h, j):
            ks = pl.ds(pl.multiple_of(j * tk, tk), tk)
            nt = (((1,), (1,)), ((), ()))
            return (lax.dot_general(k_ref[ks, lanes(h)], qs[h], nt, preferred_element_type=F32),
                    lax.dot_general(v_ref[ks, lanes(h)], do_b[h], nt, preferred_element_type=F32))

        def tile(h, j, dqt, dcq, s, dp, masked):
            ks = pl.ds(pl.multiple_of(j * tk, tk), tk)
            pr = jnp.exp(s * scale + cqs[h] - cc_ref[h, ks, :] - lses[h])
            if masked:
                pr = jnp.where(diff <= 0, pr, 0.0)
            ds = pr * (dp - deltas[h])
            ds_b = ds.astype(BF16)
            dqt = dqt + jnp.dot(kt_ref[h, j], ds_b, preferred_element_type=F32)
            dk_ref[ks, lanes(h)] += jnp.dot(ds_b, qs[h], preferred_element_type=F32) * scale
            dv_ref[ks, lanes(h)] += jnp.dot(pr.astype(BF16), do_b[h], preferred_element_type=F32)
            dck_ref[h, ks, :] -= jnp.sum(ds, axis=1, keepdims=True)
            return dqt, dcq + jnp.sum(ds, axis=0, keepdims=True)

        def step(j, carry):
            nxt = [products(h, j + 1) for h in range(hp)]
            return tuple(tile(h, j, *carry[h], False) + nxt[h] for h in range(hp))

        init = tuple((jnp.zeros((HEAD_DIM, tq), F32), jnp.zeros((1, tq), F32)) + products(h, 0) for h in range(hp))
        carry = lax.fori_loop(0, i, step, init)
        for h in range(hp):
            dqt, dcq = tile(h, i, *carry[h], True)
            dq_ref[:, lanes(h)] = jnp.transpose(dqt) * scale
            dcq_ref[h, 0] = dcq

    w = hp * HEAD_DIM
    head_all = pl.BlockSpec((t, w), lambda h, i: (0, h))
    qblk = pl.BlockSpec((tq, w), lambda h, i: (i, h))
    colv = pl.BlockSpec((hp, t, 1), lambda h, i: (h, 0, 0))
    rows_all = pl.BlockSpec((hp, t // tk, 1, tk), lambda h, i: (h, 0, 0, 0))
    row_blk = pl.BlockSpec((hp, 1, 1, tq), lambda h, i: (h, i, 0, 0))
    wide = jax.ShapeDtypeStruct((t, nf * HEAD_DIM), F32)
    return pl.pallas_call(
        body, name="fox_bwd", grid=(nf // hp, t // tq),
        in_specs=[qblk, head_all, pl.BlockSpec((hp, t // tk, HEAD_DIM, tk), lambda h, i: (h, 0, 0, 0)), head_all,
                  colv, rows_all, qblk, row_blk, qblk],
        out_specs=[qblk, head_all, head_all, row_blk, colv],
        out_shape=[wide, wide, wide, jax.ShapeDtypeStruct((nf, t // tq, 1, tq), F32),
                   jax.ShapeDtypeStruct((nf, t, 1), F32)],
        compiler_params=_cparams(("parallel", "arbitrary")),
    )(q, k, kt, v, cc, cr, o, lse, dmix)


def _mem_fn(mq, mk, mv, gq, gk):
    qn = _rms_fn(mq, gq)
    kn = _rms_fn(mk, gk)
    s = _nt(qn, kn) * (HEAD_DIM ** -0.5)
    e = jnp.exp(s - lax.stop_gradient(jnp.max(s, axis=1, keepdims=True)))
    pr = e / jnp.sum(e, axis=1, keepdims=True)
    return _nn(pr, mv)


def _mem_specs(t, m, tq, qoff):
    qblk = pl.BlockSpec((tq, HEAD_DIM), lambda h, i: (i, qoff + h))
    kblk = pl.BlockSpec((m, HEAD_DIM), lambda h, i: (0, h))
    vblk = pl.BlockSpec((m, HEAD_DIM), lambda h, i: (0, N_MEM_HEADS + h))
    row = pl.BlockSpec((1, HEAD_DIM), lambda h, i: (0, 0))
    return qblk, kblk, vblk, row


def _mem_fwd(p, qoff, mkv, gq, gk, tq, into, into_off):
    t, m = p.shape[0], mkv.shape[0]
    qblk, kblk, vblk, row = _mem_specs(t, m, tq, qoff)

    def body(q_ref, k_ref, v_ref, gq_ref, gk_ref, _, o_ref):
        o_ref[...] = _mem_fn(q_ref[...], k_ref[...], v_ref[...], gq_ref[...], gk_ref[...]).astype(BF16)

    return pl.pallas_call(
        body, name="mem_fwd", grid=(N_MEM_HEADS, t // tq),
        in_specs=[qblk, kblk, vblk, row, row, pl.BlockSpec(memory_space=pl.ANY)],
        out_specs=pl.BlockSpec((tq, HEAD_DIM), lambda h, i: (i, into_off + h)),
        out_shape=jax.ShapeDtypeStruct(into.shape, BF16), input_output_aliases={5: 0},
        compiler_params=_cparams(("parallel", "parallel")),
    )(p, mkv, mkv, gq, gk, into)


def _mem_bwd(p, qoff, mkv, gq, gk, dmix, dooff, tq):
    t, m = p.shape[0], mkv.shape[0]
    qblk, kblk, vblk, row = _mem_specs(t, m, tq, qoff)

    def body(q_ref, k_ref, v_ref, gq_ref, gk_ref, do_ref, dq_ref, dkv_k_ref, dkv_v_ref, dgq_ref, dgk_ref):
        h, i = pl.program_id(0), pl.program_id(1)

        @pl.when((h == 0) & (i == 0))
        def _():
            dgq_ref[...] = jnp.zeros_like(dgq_ref)
            dgk_ref[...] = jnp.zeros_like(dgk_ref)

        @pl.when(i == 0)
        def _():
            dkv_k_ref[...] = jnp.zeros_like(dkv_k_ref)
            dkv_v_ref[...] = jnp.zeros_like(dkv_v_ref)

        _, vjp = jax.vjp(_mem_fn, q_ref[...], k_ref[...], v_ref[...], gq_ref[...], gk_ref[...])
        dq, dk, dv, dgq, dgk = vjp(do_ref[...])
        dq_ref[...] = dq
        dkv_k_ref[...] += dk
        dkv_v_ref[...] += dv
        dgq_ref[...] += dgq
        dgk_ref[...] += dgk

    oblk = pl.BlockSpec((tq, HEAD_DIM), lambda h, i: (i, h))
    kout = pl.BlockSpec((m, HEAD_DIM), lambda h, i: (0, h))
    half = jax.ShapeDtypeStruct((m, N_MEM_HEADS * HEAD_DIM), F32)
    rshape = jax.ShapeDtypeStruct((1, HEAD_DIM), F32)
    return pl.pallas_call(
        body, name="mem_bwd", grid=(N_MEM_HEADS, t // tq),
        in_specs=[qblk, kblk, vblk, row, row, pl.BlockSpec((tq, HEAD_DIM), lambda h, i: (i, dooff + h))],
        out_specs=[oblk, kout, kout, row, row],
        out_shape=[jax.ShapeDtypeStruct((t, N_MEM_HEADS * HEAD_DIM), F32), half, half, rshape, rshape],
        compiler_params=_cparams(("arbitrary", "arbitrary")),
    )(p, mkv, mkv, gq, gk, dmix)


def _shift_down(x, s):
    if s == 0:
        return x
    r = lax.broadcasted_iota(jnp.int32, x.shape, 0)
    return jnp.where(r >= s, pltpu.roll(x, s, 0), 0.0)


def _shift_up(x, s):
    if s == 0:
        return x
    n = x.shape[0]
    r = lax.broadcasted_iota(jnp.int32, x.shape, 0)
    return jnp.where(r < n - s, pltpu.roll(x, n - s, 0), 0.0)


def _conv_fn(x0, x1, x2, x3, w0, w1, w2, w3, kind):
    y = _silu(x0 * w0 + x1 * w1 + x2 * w2 + x3 * w3)
    if kind == 2:
        return y
    y = y * lax.rsqrt(jnp.sum(y * y, axis=-1, keepdims=True) + NORM_EPS)
    return y * (HEAD_DIM ** -0.5) if kind == 0 else y


def _conv_fwd(p, off, conv_w, ng):
    t = p.shape[0]

    def body(x_ref, w_ref, o_ref):
        kind = pl.program_id(0) // ng
        x = x_ref[...]
        xs = [_shift_down(x, CONV_WIDTH - 1 - j) for j in range(CONV_WIDTH)]
        ws = [w_ref[j:j + 1, :] for j in range(CONV_WIDTH)]
        for kd in range(3):
            @pl.when(kind == kd)
            def _(kd=kd):
                o_ref[...] = _conv_fn(*xs, *ws, kd)

    return pl.pallas_call(
        body, name="gdn_conv_fwd", grid=(3 * ng,),
        in_specs=[pl.BlockSpec((t, HEAD_DIM), lambda c: (0, off + c)),
                  pl.BlockSpec((CONV_WIDTH, HEAD_DIM), lambda c: (0, c))],
        out_specs=pl.BlockSpec((t, HEAD_DIM), lambda c: (0, c)),
        out_shape=jax.ShapeDtypeStruct((t, 3 * ng * HEAD_DIM), F32),
        compiler_params=_cparams(("parallel",)),
    )(p, conv_w)


def _conv_bwd(p, off, conv_w, dys, ng):
    t = p.shape[0]

    def body(x_ref, w_ref, dq_ref, dk_ref, dv_ref, dx_ref, dw_ref):
        kind = pl.program_id(0) // ng
        dy_refs = (dq_ref, dk_ref, dv_ref)
        x = x_ref[...]
        xs = [_shift_down(x, CONV_WIDTH - 1 - j) for j in range(CONV_WIDTH)]
        ws = [w_ref[j:j + 1, :] for j in range(CONV_WIDTH)]
        for kd in range(3):
            @pl.when(kind == kd)
            def _(kd=kd):
                _, vjp = jax.vjp(functools.partial(_conv_fn, kind=kd), *xs, *ws)
                g = vjp(dy_refs[kd][...])
                dx = _shift_up(g[0], CONV_WIDTH - 1)
                for j in range(1, CONV_WIDTH):
                    dx = dx + _shift_up(g[j], CONV_WIDTH - 1 - j)
                dx_ref[...] = dx
                for j in range(CONV_WIDTH):
                    dw_ref[j:j + 1, :] = g[CONV_WIDTH + j]

    blk = pl.BlockSpec((t, HEAD_DIM), lambda c: (0, c))
    head = pl.BlockSpec((t, HEAD_DIM), lambda c: (0, c % ng))
    wblk = pl.BlockSpec((CONV_WIDTH, HEAD_DIM), lambda c: (0, c))
    return pl.pallas_call(
        body, name="gdn_conv_bwd", grid=(3 * ng,),
        in_specs=[pl.BlockSpec((t, HEAD_DIM), lambda c: (0, off + c)), wblk] + [head] * 3,
        out_specs=[blk, wblk],
        out_shape=[jax.ShapeDtypeStruct((t, 3 * ng * HEAD_DIM), F32),
                   jax.ShapeDtypeStruct((CONV_WIDTH, 3 * ng * HEAD_DIM), F32)],
        compiler_params=_cparams(("parallel",)),
    )(p, conv_w, *dys)


def _lower_inverse(lower):
    c = lower.shape[-1]
    r = lax.broadcasted_iota(jnp.int32, (1, c, c), 1)
    e = lax.broadcasted_iota(jnp.int32, (1, c, c), 2)
    hi = lax.Precision.HIGH
    inv = jnp.where(r == e, 1.0, 0.0) - lower
    pw = lower
    for _ in range(int(math.log2(c)) - 1):
        pw = _dot(pw, pw, ((1,), (0,)), hi)
        inv = inv + _dot(inv, pw, ((1,), (0,)), hi)
    return inv


@jax.custom_vjp
def _solve(lower, inv, vb, kbg):
    hi = lax.Precision.HIGH
    return _dot(inv, vb, ((1,), (0,)), hi), _dot(inv, kbg, ((1,), (0,)), hi)


def _solve_fwd(lower, inv, vb, kbg):
    u, w = _solve(lower, inv, vb, kbg)
    return (u, w), (inv, u, w)


def _solve_bwd(res, cts):
    inv, u, w = res
    dvb, dkbg = _tn(inv, cts[0]), _tn(inv, cts[1])
    return -(_nt(dvb, u) + _nt(dkbg, w)), jnp.zeros_like(inv), dvb, dkbg


_solve.defvjp(_solve_fwd, _solve_bwd)


def _wy_fn(q, k, v, gcol, grow, bcol, inv=None):
    b, c, dk = q.shape
    r = lax.broadcasted_iota(jnp.int32, (1, c, c), 1)
    e = lax.broadcasted_iota(jnp.int32, (1, c, c), 2)
    tril, strict = e <= r, e < r
    gc_col = jnp.sum(jnp.where(tril, grow, 0.0), axis=2, keepdims=True)
    gc_row = jnp.sum(jnp.where(r <= e, gcol, 0.0), axis=1, keepdims=True)
    g_last = jnp.sum(gcol, axis=1, keepdims=True)
    decay = jnp.exp(jnp.where(tril, gc_col - gc_row, NEG))
    kb, vb = k * bcol, v * bcol
    lower = jnp.where(strict, _nt(kb, k) * decay, 0.0)
    if inv is None:
        inv = _lower_inverse(lower)
    u, w = _solve(lower, inv, vb, kb * jnp.exp(gc_col))
    attn = jnp.where(tril, _nt(q, k) * decay, 0.0)
    qg = q * jnp.exp(gc_col)
    kdec = k * jnp.exp(g_last - gc_col)
    egl = jnp.broadcast_to(jnp.exp(g_last), (b, 1, dk))
    return u, w, qg, kdec, attn, egl, inv


def _scan_fn(u, w, qg, kdec, attn, egl, state):
    v_new = u - _nn(w, state)
    o = _nn(qg, state) + _nn(attn, v_new)
    return o, state * egl + _tn(kdec, v_new)


GDN_CHUNKS_PER_STEP = 4


def _gdn_fwd(qkv, gcol, grow, bcol, ng):
    t = qkv.shape[0]
    nch = t // CHUNK

    cb = GDN_CHUNKS_PER_STEP
    *wy, inv = _gdn_wy(qkv, gcol, grow, bcol, ng, cb)

    def body(u_ref, w_ref, qg_ref, kd_ref, at_ref, eg_ref, o_ref, st_ref, state):
        @pl.when(pl.program_id(0) == 0)
        def _():
            state[...] = jnp.zeros_like(state)

        st_ref[:, 0] = state[...]
        heads = lambda ref: jnp.stack([ref[:, h * HEAD_DIM:(h + 1) * HEAD_DIM] for h in range(ng)])
        o, new = _scan_fn(heads(u_ref), heads(w_ref), heads(qg_ref), heads(kd_ref), at_ref[:, 0], eg_ref[:, 0],
                          state[...])
        for h in range(ng):
            o_ref[:, h * HEAD_DIM:(h + 1) * HEAD_DIM] = o[h]
        state[...] = new

    w = ng * HEAD_DIM
    blk = pl.BlockSpec((CHUNK, w), lambda i: (i, 0))
    o, states = pl.pallas_call(
        body, name="gdn_scan_fwd", grid=(nch,),
        in_specs=[blk, blk, blk, blk, pl.BlockSpec((ng, 1, CHUNK, CHUNK), lambda i: (0, i, 0, 0)),
                  pl.BlockSpec((ng, 1, 1, HEAD_DIM), lambda i: (0, i, 0, 0))],
        out_specs=[blk, pl.BlockSpec((ng, 1, HEAD_DIM, HEAD_DIM), lambda i: (0, i, 0, 0))],
        out_shape=[jax.ShapeDtypeStruct((t, w), F32),
                   jax.ShapeDtypeStruct((ng, nch, HEAD_DIM, HEAD_DIM), F32)],
        scratch_shapes=[pltpu.VMEM((ng, HEAD_DIM, HEAD_DIM), F32)],
        compiler_params=_cparams(("arbitrary",)),
    )(*wy)
    return o, (wy, inv, states)


def _wy_batch(q_ref, k_ref, v_ref, gc_ref, gr_ref, bc_ref, ng, cb):
    idx = [(c, h) for c in range(cb) for h in range(ng)]
    rows = lambda c: slice(c * CHUNK, (c + 1) * CHUNK)
    lanes = lambda h: slice(h * HEAD_DIM, (h + 1) * HEAD_DIM)
    wide = lambda ref: jnp.stack([ref[rows(c), lanes(h)] for c, h in idx])
    col = lambda ref: jnp.stack([ref[h, rows(c), :] for c, h in idx])
    return idx, (wide(q_ref), wide(k_ref), wide(v_ref), col(gc_ref), jnp.stack([gr_ref[h, c] for c, h in idx]),
                 col(bc_ref))


def _gdn_wy(qkv, gcol, grow, bcol, ng, cb):
    t = qkv.shape[0]
    nch = t // CHUNK

    def body(q_ref, k_ref, v_ref, gc_ref, gr_ref, bc_ref, u_ref, w_ref, qg_ref, kd_ref, at_ref, eg_ref, inv_ref):
        idx, args = _wy_batch(q_ref, k_ref, v_ref, gc_ref, gr_ref, bc_ref, ng, cb)
        u, w, qg, kd, at, eg, inv = _wy_fn(*args)
        for b, (c, h) in enumerate(idx):
            rows, lanes = slice(c * CHUNK, (c + 1) * CHUNK), slice(h * HEAD_DIM, (h + 1) * HEAD_DIM)
            u_ref[rows, lanes] = u[b]
            w_ref[rows, lanes] = w[b]
            qg_ref[rows, lanes] = qg[b]
            kd_ref[rows, lanes] = kd[b]
            at_ref[h, c] = at[b]
            eg_ref[h, c] = eg[b]
            inv_ref[h, c] = inv[b]

    wd = ng * HEAD_DIM
    blk = lambda o: pl.BlockSpec((cb * CHUNK, wd), lambda i: (i, o))
    col = pl.BlockSpec((ng, cb * CHUNK, 1), lambda i: (0, i, 0))
    sq = pl.BlockSpec((ng, cb, CHUNK, CHUNK), lambda i: (0, i, 0, 0))
    wide = jax.ShapeDtypeStruct((t, wd), F32)
    sq_shape = jax.ShapeDtypeStruct((ng, nch, CHUNK, CHUNK), F32)
    return pl.pallas_call(
        body, name="gdn_wy_fwd", grid=(nch // cb,),
        in_specs=[blk(0), blk(1), blk(2), col, pl.BlockSpec((ng, cb, 1, CHUNK), lambda i: (0, i, 0, 0)), col],
        out_specs=[blk(0), blk(0), blk(0), blk(0), sq, pl.BlockSpec((ng, cb, 1, HEAD_DIM), lambda i: (0, i, 0, 0)),
                   sq],
        out_shape=[wide, wide, wide, wide, sq_shape, jax.ShapeDtypeStruct((ng, nch, 1, HEAD_DIM), F32), sq_shape],
        compiler_params=_cparams(("parallel",)),
    )(qkv, qkv, qkv, gcol, grow, bcol)


def _gdn_bwd(qkv, gcol, grow, bcol, saved, do, ng):
    t = qkv.shape[0]
    nch = t // CHUNK
    cb = GDN_CHUNKS_PER_STEP // 2
    wy, inv, states = saved
    wd = ng * HEAD_DIM

    def scan_body(u_ref, w_ref, qg_ref, kd_ref, at_ref, eg_ref, st_ref, do_ref,
                  du_ref, dw_ref, dqg_ref, dkd_ref, dat_ref, deg_ref, dstate):
        @pl.when(pl.program_id(0) == 0)
        def _():
            dstate[...] = jnp.zeros_like(dstate)

        heads = lambda ref: jnp.stack([ref[:, h * HEAD_DIM:(h + 1) * HEAD_DIM] for h in range(ng)])
        _, vjp = jax.vjp(_scan_fn, heads(u_ref), heads(w_ref), heads(qg_ref), heads(kd_ref), at_ref[:, 0],
                         eg_ref[:, 0], st_ref[:, 0])
        du, dw, dqg, dkd, dat, deg, dst = vjp((heads(do_ref), dstate[...]))
        for h in range(ng):
            lanes = slice(h * HEAD_DIM, (h + 1) * HEAD_DIM)
            du_ref[:, lanes] = du[h]
            dw_ref[:, lanes] = dw[h]
            dqg_ref[:, lanes] = dqg[h]
            dkd_ref[:, lanes] = dkd[h]
        dat_ref[:, 0] = dat
        deg_ref[:, 0] = deg
        dstate[...] = dst

    rev = lambda i: nch - 1 - i
    blk = pl.BlockSpec((CHUNK, wd), lambda i: (rev(i), 0))
    atb = pl.BlockSpec((ng, 1, CHUNK, CHUNK), lambda i: (0, rev(i), 0, 0))
    egb = pl.BlockSpec((ng, 1, 1, HEAD_DIM), lambda i: (0, rev(i), 0, 0))
    wide = jax.ShapeDtypeStruct((t, wd), F32)
    at_shape = jax.ShapeDtypeStruct((ng, nch, CHUNK, CHUNK), F32)
    eg_shape = jax.ShapeDtypeStruct((ng, nch, 1, HEAD_DIM), F32)
    dwy = pl.pallas_call(
        scan_body, name="gdn_scan_bwd", grid=(nch,),
        in_specs=[blk, blk, blk, blk, atb, egb,
                  pl.BlockSpec((ng, 1, HEAD_DIM, HEAD_DIM), lambda i: (0, rev(i), 0, 0)), blk],
        out_specs=[blk, blk, blk, blk, atb, egb],
        out_shape=[wide, wide, wide, wide, at_shape, eg_shape],
        scratch_shapes=[pltpu.VMEM((ng, HEAD_DIM, HEAD_DIM), F32)],
        compiler_params=_cparams(("arbitrary",)),
    )(*wy, states, do)

    def wy_body(q_ref, k_ref, v_ref, gc_ref, gr_ref, bc_ref, du_ref, dw_ref, dqg_ref, dkd_ref, dat_ref, deg_ref,
                inv_ref, dq_ref, dk_ref, dv_ref, dgc_ref, dgr_ref, dbc_ref):
        idx, args = _wy_batch(q_ref, k_ref, v_ref, gc_ref, gr_ref, bc_ref, ng, cb)
        kept = jnp.stack([inv_ref[h, c] for c, h in idx])
        rows = lambda c: slice(c * CHUNK, (c + 1) * CHUNK)
        lanes = lambda h: slice(h * HEAD_DIM, (h + 1) * HEAD_DIM)
        wide_ct = lambda ref: jnp.stack([ref[rows(c), lanes(h)] for c, h in idx])
        cts = (wide_ct(du_ref), wide_ct(dw_ref), wide_ct(dqg_ref), wide_ct(dkd_ref),
               jnp.stack([dat_ref[h, c] for c, h in idx]), jnp.stack([deg_ref[h, c] for c, h in idx]))
        _, vjp = jax.vjp(lambda *a: _wy_fn(*a, inv=kept)[:6], *args)
        dq, dk, dv, dgc, dgr, dbc = vjp(cts)
        for b, (c, h) in enumerate(idx):
            dq_ref[rows(c), lanes(h)] = dq[b]
            dk_ref[rows(c), lanes(h)] = dk[b]
            dv_ref[rows(c), lanes(h)] = dv[b]
            dgc_ref[h, rows(c), :] = dgc[b]
            dgr_ref[h, c] = dgr[b]
            dbc_ref[h, rows(c), :] = dbc[b]

    cblk = lambda o: pl.BlockSpec((cb * CHUNK, wd), lambda i: (i, o))
    col = pl.BlockSpec((ng, cb * CHUNK, 1), lambda i: (0, i, 0))
    rowv = pl.BlockSpec((ng, cb, 1, CHUNK), lambda i: (0, i, 0, 0))
    cshape = jax.ShapeDtypeStruct((ng, t, 1), F32)
    return pl.pallas_call(
        wy_body, name="gdn_wy_bwd", grid=(nch // cb,),
        in_specs=[cblk(0), cblk(1), cblk(2), col, rowv, col, cblk(0), cblk(0), cblk(0), cblk(0),
                  pl.BlockSpec((ng, cb, CHUNK, CHUNK), lambda i: (0, i, 0, 0)),
                  pl.BlockSpec((ng, cb, 1, HEAD_DIM), lambda i: (0, i, 0, 0)),
                  pl.BlockSpec((ng, cb, CHUNK, CHUNK), lambda i: (0, i, 0, 0))],
        out_specs=[cblk(0), cblk(0), cblk(0), col, rowv, col],
        out_shape=[wide, wide, wide, cshape, jax.ShapeDtypeStruct((ng, nch, 1, CHUNK), F32), cshape],
        compiler_params=_cparams(("parallel",)),
    )(qkv, qkv, qkv, gcol, grow, bcol, *dwy, inv)


def _swiglu_fn(gate, up):
    return _silu(gate) * up


FFN_TN = 256


def _ffn_up(n2, wgu4):
    _, d, w = wgu4.shape
    t = n2.shape[0]
    tn = _tile(w, FFN_TN)
    nb = w // tn

    def body(a_ref, b_ref, gu_ref, act_ref):
        av = a_ref[...]
        gate = jnp.dot(av, b_ref[0], preferred_element_type=F32)
        up = jnp.dot(av, b_ref[1], preferred_element_type=F32)
        gu_ref[0] = gate.astype(BF16)
        gu_ref[1] = up.astype(BF16)
        act_ref[...] = _swiglu_fn(gate, up).astype(BF16)

    return pl.pallas_call(
        body, name="ffn_up", grid=(2, nb),
        in_specs=[pl.BlockSpec((t, d), lambda j, l: (0, 0)), pl.BlockSpec((2, d, tn), lambda j, l: (j, 0, l))],
        out_specs=[pl.BlockSpec((2, t, tn), lambda j, l: (j, 0, l)),
                   pl.BlockSpec((t, tn), lambda j, l: (0, j * nb + l))],
        out_shape=[jax.ShapeDtypeStruct((4, t, w), BF16), jax.ShapeDtypeStruct((t, 2 * w), BF16)],
        compiler_params=_cparams(("parallel", "parallel")),
    )(n2, wgu4)


def _ffn_dact(dh2, wd, gu, after):
    _, t, w = gu.shape
    d = dh2.shape[1]
    tn = _tile(w, FFN_TN)
    nb = w // tn

    def body(a_ref, b_ref, gu_ref, _, o_ref):
        dact = lax.dot_general(a_ref[...], b_ref[...], (((1,), (1,)), ((), ())), preferred_element_type=F32)
        _, vjp = jax.vjp(_swiglu_fn, gu_ref[0].astype(F32), gu_ref[1].astype(F32))
        dg, du = vjp(dact)
        o_ref[0] = dg.astype(BF16)
        o_ref[1] = du.astype(BF16)

    pair = pl.BlockSpec((2, t, tn), lambda j, l: (j, 0, l))
    return pl.pallas_call(
        body, name="ffn_dact", grid=(2, nb),
        in_specs=[pl.BlockSpec((t, d), lambda j, l: (0, 0)), pl.BlockSpec((tn, d), lambda j, l: (j * nb + l, 0)),
                  pair, pl.BlockSpec(after.shape, lambda j, l: (0, 0))],
        out_specs=pair, out_shape=jax.ShapeDtypeStruct(gu.shape, BF16),
        compiler_params=_cparams(("parallel", "parallel")),
    )(dh2, wd, gu, after)


def _loss_head(h2, target):
    t, d = h2.shape
    tr = _tile(t, 256, 8)

    def body(h_ref, t_ref, l_ref, d_ref, db_ref):
        @pl.when(pl.program_id(0) == 0)
        def _():
            l_ref[...] = jnp.zeros_like(l_ref)

        err = h_ref[...] - t_ref[...]
        d_ref[...] = err * (1.0 / d)
        db_ref[...] = (err * (1.0 / d)).astype(BF16)
        part = 0.5 * jnp.sum(jnp.mean(err * err, axis=-1, keepdims=True), axis=0, keepdims=True)
        lane = lax.broadcasted_iota(jnp.int32, (8, HEAD_DIM), 1)
        row = lax.broadcasted_iota(jnp.int32, (8, HEAD_DIM), 0)
        l_ref[...] += jnp.where((lane == 0) & (row == 0), part, 0.0)

    blk = pl.BlockSpec((tr, d), lambda r: (r, 0))
    return pl.pallas_call(
        body, name="loss_head", grid=(t // tr,), in_specs=[blk, blk],
        out_specs=[pl.BlockSpec((8, HEAD_DIM), lambda r: (0, 0)), blk, blk],
        out_shape=[jax.ShapeDtypeStruct((8, HEAD_DIM), F32), jax.ShapeDtypeStruct((t, d), F32),
                   jax.ShapeDtypeStruct((t, d), BF16)],
        compiler_params=_cparams(("arbitrary",)),
    )(h2, target)


def _adamw(w, g, m, v, *, g_fn=None, name):
    r, c = w.shape
    tr = _tile(r, max(8, (1 << 19) // c // 8 * 8), 8)
    gs = g if isinstance(g, tuple) else (g,)

    def body(w_ref, *refs):
        g_refs, (m_ref, v_ref, go_ref, d_ref, mo_ref, vo_ref) = refs[:len(gs)], refs[len(gs):]
        gr = g_refs[0][...] if g_fn is None else g_fn(*[ref[...] for ref in g_refs])
        mn = ADAM_B1 * m_ref[...] + (1.0 - ADAM_B1) * gr
        vn = ADAM_B2 * v_ref[...] + (1.0 - ADAM_B2) * (gr * gr)
        m_hat = mn / (1.0 - ADAM_B1 ** ADAM_STEP)
        v_hat = vn / (1.0 - ADAM_B2 ** ADAM_STEP)
        go_ref[...] = gr
        d_ref[...] = -ADAM_LR * (m_hat / (jnp.sqrt(v_hat) + ADAM_EPS) + ADAM_WD * w_ref[...])
        mo_ref[...] = mn
        vo_ref[...] = vn

    blk = pl.BlockSpec((tr, c), lambda i: (i, 0))
    gblks = [pl.BlockSpec((tr, gi.shape[1]), lambda i: (i, 0)) for gi in gs]
    return pl.pallas_call(
        body, name=name, grid=(r // tr,), in_specs=[blk] + gblks + [blk, blk], out_specs=[blk] * 4,
        out_shape=[jax.ShapeDtypeStruct((r, c), F32)] * 4,
        compiler_params=_cparams(("parallel",)),
    )(w, *gs, m, v)


class _Layout:
    def __init__(self, d):
        nh = d // HEAD_DIM
        self.nm = N_MEM_HEADS
        self.nf = (nh - self.nm) // 2
        self.ng = nh - self.nm - self.nf
        nf, ng, nm, hd = self.nf, self.ng, self.nm, HEAD_DIM
        self.o_fq, self.o_fk, self.o_fv, self.o_sm = 0, nf, 2 * nf, 3 * nf
        self.o_gq, self.o_gz, self.o_mq = 0, 3 * ng, 4 * ng
        self.cols_a = -(-(3 * nf + 1) // 4) * 4 * hd
        self.cols_b = -(-(4 * ng + nm) // 4) * 4 * hd
        self.cols = self.cols_a + self.cols_b
        sizes = [nf * hd, nf * hd, nf * hd, nf, 3 * ng * hd, ng * hd, ng, ng, nm * hd]
        starts = [sum(sizes[:i]) for i in range(len(sizes))]
        self.ref = list(zip(starts, sizes))
        self.in_cols = sum(sizes)

    def regroup(self, w):
        part = lambda i: w[:, self.ref[i][0]:self.ref[i][0] + self.ref[i][1]]
        a = [part(0), part(1), part(2), part(3), part(6), part(7)]
        b = [part(4), part(5), part(8)]
        pads = [self.cols_a - sum(p.shape[1] for p in a), self.cols_b - sum(p.shape[1] for p in b)]
        fill = [[jnp.zeros((w.shape[0], n), w.dtype)] if n else [] for n in pads]
        return jnp.concatenate(a + fill[0] + b + fill[1], axis=1)

    def ungroup(self, g):
        hd, nf, ng, nm = HEAD_DIM, self.nf, self.ng, self.nm
        sm, b0 = self.o_sm * hd, self.cols_a
        return jnp.concatenate([
            g[:, :3 * nf * hd], g[:, sm:sm + nf], g[:, b0:b0 + 3 * ng * hd],
            g[:, b0 + self.o_gz * hd:b0 + self.o_mq * hd], g[:, sm + nf:sm + nf + ng],
            g[:, sm + nf + ng:sm + nf + 2 * ng], g[:, b0 + self.o_mq * hd:b0 + (self.o_mq + nm) * hd]], axis=1)


def _lane_row(pieces):
    row = jnp.zeros((1, HEAD_DIM), F32)
    for off, a in pieces:
        row = lax.dynamic_update_slice(row, a.astype(F32), (0, off))
    return row


def _local_step(x, mem, target, prefetch, weights, reducer, sp):
    t, d = x.shape
    lay = _Layout(d)
    nf, ng, nm, hd = lay.nf, lay.ng, lay.nm, HEAD_DIM
    nch = t // CHUNK
    tq = _tile(t, 256)
    tk = tq

    u = _norm_fwd(x, 0, sp["norm_mix"], 1, d, BF16, name="norm_mix_fwd")
    prefetch("in_a", u)
    (win_a,) = weights("in_a", u)
    p_a = _mm(u, win_a, name="mm_in_a")
    pa = _lane_row([(nf, sp["gdn_a_log"])])
    pb = _lane_row([(0, sp["fox_f_bias"]), (nf, sp["gdn_dt_bias"])])
    vals, csum = _small_fwd(p_a, lay.o_sm, pa, pb, nf, ng)

    c_t = csum[:, :nf].T
    cc, cr = c_t.reshape(nf, t, 1), c_t.reshape(nf, t // tk, 1, tk)
    fq = _norm_fwd(p_a, lay.o_fq, sp["fox_q_norm"], nf, hd, BF16, name="fox_qnorm_fwd")
    fk = _norm_fwd(p_a, lay.o_fk, sp["fox_k_norm"], nf, hd, BF16, name="fox_knorm_fwd")
    fv = p_a[:, lay.o_fv * hd:(lay.o_fv + nf) * hd].astype(BF16)
    o_fox, lse, mix = _fox_fwd(fq, fk, fv, cc, cr, nf, tq, tk, d)

    prefetch("in_b", lse)
    (win_b,) = weights("in_b", lse)
    prefetch("mixer", win_b)
    p = _mm(u, win_b, name="mm_in_b")
    wmkv, conv_taps = weights("mixer", p)
    sp = dict(sp, gdn_conv=conv_taps)
    qkv = _conv_fwd(p, lay.o_gq, sp["gdn_conv"], ng)
    g_t, b_t = vals[:, nf:nf + ng].T, vals[:, nf + ng:nf + 2 * ng].T
    gcol, grow, bcol = g_t.reshape(ng, t, 1), g_t.reshape(ng, nch, 1, CHUNK), b_t.reshape(ng, t, 1)
    o_g, states = _gdn_fwd(qkv, gcol, grow, bcol, ng)
    mix = _norm_fwd(o_g, 0, sp["gdn_out_norm"], ng, hd, BF16, z=p, zoff=lay.o_gz, into=mix, into_off=nf,
                    name="gdn_out_fwd")
    prefetch("out", mix)

    mem_n = _norm_fwd(mem, 0, sp["mem_norm"], 1, d, BF16, name="mem_norm_fwd")
    mkv = _mm(mem_n, wmkv, name="mm_memkv")
    mix = _mem_fwd(p, lay.o_mq, mkv, sp["mem_q_norm"], sp["mem_k_norm"], tq, mix, nf + ng)
    prefetch("gate_up", mix)
    (wout,) = weights("out", mix)
    h1 = _mm(mix, wout, res=x, name="mm_out")
    n2 = _norm_fwd(h1, 0, sp["norm_ffn"], 1, d, BF16, name="norm_ffn_fwd")
    (wgu,) = weights("gate_up", n2)
    wgu4 = wgu.reshape(4, d, -1)
    gu, act = _ffn_up(n2, wgu4)
    prefetch("down", act)
    (wd,) = weights("down", act)
    h2 = _mm(act, wd, res=h1, name="mm_down")
    loss_blk, dh2, dh2_b = _loss_head(h2, target)

    g = {}
    token = reducer.pair("w_down", _mm(act, dh2_b, ta=True, out_dtype=BF16, name="mm_dw_down"))
    dgu = _ffn_dact(dh2_b, wd, gu, token)
    dw_gate_up = _mm(n2, dgu, ta=True, stack="out", out_dtype=BF16, name="mm_dw_gate_up").reshape(wgu.shape)
    token = reducer.pair("w_gate_up", dw_gate_up)
    dn2 = _mm(dgu, wgu4, tb=True, stack="sum", after=token, name="mm_dn2")
    token = reducer.ship("ffn", ["w_down", "w_gate_up"], dn2)
    dh1, g["norm_ffn"] = _norm_bwd(h1, 0, sp["norm_ffn"] + token[0, 0], dn2, 0, 1, d, res=dh2,
                                   name="norm_ffn_bwd")
    token = reducer.pair("w_out", _mm(mix, dh1, ta=True, out_dtype=BF16, name="mm_dw_out"))
    dmix = _mm(dh1, wout, tb=True, after=token, name="mm_dmix")

    dmq, dmk, dmv, g["mem_q_norm"], g["mem_k_norm"] = _mem_bwd(
        p, lay.o_mq, mkv, sp["mem_q_norm"], sp["mem_k_norm"], dmix, nf + ng, tq)
    dmkv = jnp.concatenate([dmk, dmv], axis=1)
    token = reducer.pair("w_mem_kv", _mm(mem_n, dmkv, ta=True, out_dtype=BF16, name="mm_dw_memkv"))
    dmem_n = _mm(dmkv, wmkv, tb=True, after=token, name="mm_dmem")
    token = reducer.ship("mix", ["w_out", "w_mem_kv"], dmem_n)
    _, g["mem_norm"] = _norm_bwd(mem, 0, sp["mem_norm"], dmem_n, 0, 1, d, name="mem_norm_bwd")

    do_g, dgz, g["gdn_out_norm"] = _norm_bwd(o_g, 0, sp["gdn_out_norm"] + token[0, 0], dmix, nf, ng, hd, z=p,
                                             zoff=lay.o_gz, name="gdn_out_bwd")
    dq, dk, dv, dgc, dgr, dbc = _gdn_bwd(qkv, gcol, grow, bcol, states, do_g, ng)
    dgqkv, g["gdn_conv"] = _conv_bwd(p, lay.o_gq, sp["gdn_conv"], (dq, dk, dv), ng)
    dg_t = dgc.reshape(ng, t) + dgr.reshape(ng, t)
    db_t = dbc.reshape(ng, t)

    dfq_n, dfk_n, dfv, dcc, dcr = _fox_bwd(fq, fk, fv, cc, cr, o_fox, lse, dmix, nf, tq, tk)
    dfq, g["fox_q_norm"] = _norm_bwd(p_a, lay.o_fq, sp["fox_q_norm"], dfq_n, 0, nf, hd, name="fox_qnorm_bwd")
    dfk, g["fox_k_norm"] = _norm_bwd(p_a, lay.o_fk, sp["fox_k_norm"], dfk_n, 0, nf, hd, name="fox_knorm_bwd")
    dc_t = dcc.reshape(nf, t) + dcr.reshape(nf, t)

    lanes_left = hd - nf - 2 * ng
    dvals = jnp.concatenate([jnp.zeros((t, nf), F32), dg_t.T, db_t.T, jnp.zeros((t, lanes_left), F32)], axis=1)
    dcsum = jnp.concatenate([dc_t.T, jnp.zeros((t, hd - nf), F32)], axis=1)
    dsm, dpa, dpb = _small_bwd(p_a, lay.o_sm, pa, pb, dvals, dcsum, nf, ng)
    g["fox_f_bias"] = dpb[:, :nf]
    g["gdn_dt_bias"] = dpb[:, nf:nf + ng]
    g["gdn_a_log"] = dpa[:, nf:nf + ng]

    zeros = lambda n: jnp.zeros((t, n), F32)
    dp_a = jnp.concatenate([dfq, dfk, dfv, dsm, zeros(lay.cols_a - (lay.o_sm + 1) * hd)], axis=1).astype(BF16)
    dp_b = jnp.concatenate([dgqkv, dgz, dmq, zeros(lay.cols_b - (lay.o_mq + nm) * hd)], axis=1).astype(BF16)
    token = reducer.start("in", {"w_in_a": _mm(u, dp_a, ta=True, out_dtype=BF16, name="mm_dw_in_a"),
                                 "w_in_b": _mm(u, dp_b, ta=True, out_dtype=BF16, name="mm_dw_in_b")})
    du = _mm(dp_a, win_a, tb=True, after=token, name="mm_du_a")
    du = _mm(dp_b, win_b, tb=True, res=du, name="mm_du_b")
    dx, g["norm_mix"] = _norm_bwd(x, 0, sp["norm_mix"], du, 0, 1, d, res=dh1, name="norm_mix_bwd")
    return loss_blk, dx, g


ANY = pl.BlockSpec(memory_space=pl.ANY)


def _me():
    x, y, c = lax.axis_index("x"), lax.axis_index("y"), lax.axis_index("c")
    chips = [(1 - x, y), (x, 1 - y), (1 - x, 1 - y)]
    return x, y, c, chips


def _slot(axis, k):
    return k if axis == 0 else 2 * (k % 2) + k // 2


def _slab(ref, axis, rows, cols, k, h):
    half = rows // 2
    return ref.at[pl.ds(_slot(axis, k) * rows + h * half, half), :]


def _remote(src, dst, send_sem, recv_sem, dev):
    return pltpu.make_async_remote_copy(src_ref=src, dst_ref=dst, send_sem=send_sem, recv_sem=recv_sem,
                                        device_id=dev, device_id_type=MESH)


HBM = pl.BlockSpec(memory_space=pltpu.HBM)
SEM = pl.BlockSpec(memory_space=pltpu.SEMAPHORE)
SPLIT = pltpu.CompilerParams(has_side_effects=pltpu.SideEffectType.DATAFLOW_SIDE_EFFECTING)
TOKEN = jax.ShapeDtypeStruct((8, HEAD_DIM), F32)


def _in_hbm(v):
    return pltpu.with_memory_space_constraint(v, pltpu.HBM)


def _cast_place(shard, axis, name, col_fn=None, out_cols=None):
    r, c = shard.shape
    oc = out_cols or c
    tr = _tile(r, 512 if col_fn is None else 64, 16)
    tc = _tile(c, 2048) if col_fn is None else c
    otc = tc if col_fn is None else oc
    nb = r // tr
    chip = 2 * lax.axis_index("x") + lax.axis_index("y")
    slot = jnp.reshape(_slot(axis, chip), (1,)).astype(jnp.int32)

    def body(slot_ref, x_ref, o_ref):
        x = x_ref[...]
        o_ref[...] = (x if col_fn is None else col_fn(x)).astype(BF16)

    return pl.pallas_call(
        body, name=name,
        grid_spec=pltpu.PrefetchScalarGridSpec(
            num_scalar_prefetch=1, grid=(nb, c // tc),
            in_specs=[pl.BlockSpec((tr, tc), lambda i, l, s: (i, l))],
            out_specs=pl.BlockSpec((tr, otc), lambda i, l, s: (s[0] * nb + i, l))),
        out_shape=jax.ShapeDtypeStruct((4 * r, oc), BF16),
        compiler_params=_cparams(("parallel", "parallel")),
    )(slot, shard)


def _gather_start(bufs, axes, shapes, groups, name):
    n = len(bufs)

    def body(*refs):
        dst = refs[n:2 * n]
        sems = refs[2 * n:2 * n + 2 * len(groups)]
        token = refs[-1]
        x, y, c, chips = _me()
        k = 2 * x + y
        for gi, ws in enumerate(groups):
            for i, w in enumerate(ws):
                r, cl = shapes[w]
                place = _slab(dst[w], axes[w], r, cl, k, c)
                for j, (px, py) in enumerate(chips):
                    _remote(place, place, sems[2 * gi].at[3 * i + j], sems[2 * gi + 1].at[3 * i + j],
                            (px, py, c)).start()
        token[...] = jnp.zeros_like(token)

    sem_shapes = [pltpu.SemaphoreType.DMA((3 * len(ws),)) for ws in groups for _ in range(2)]
    outs = pl.pallas_call(
        body, name=name, in_specs=[HBM] * n,
        out_specs=[HBM] * n + [SEM] * len(sem_shapes) + [pl.BlockSpec(memory_space=pltpu.VMEM)],
        out_shape=[pltpu.HBM(b.shape, b.dtype) for b in bufs] + sem_shapes + [TOKEN],
        input_output_aliases={w: w for w in range(n)}, compiler_params=SPLIT,
    )(*[_in_hbm(b) for b in bufs])
    sems = outs[n:-1]
    return outs[:n], [(sems[2 * g], sems[2 * g + 1]) for g in range(len(groups))], outs[-1]


def _gather_wait(bufs, axes, shapes, sems, after, name):
    n = len(bufs)

    def body(*refs):
        send_sems, recv_sems = refs[n], refs[n + 1]
        dst = refs[n + 3:]
        x, y, c, chips = _me()
        k = 2 * x + y
        for i in range(n):
            r, cl = shapes[i]
            for j, (px, py) in enumerate(chips):
                got = _slab(dst[i], axes[i], r, cl, 2 * px + py, c)
                _remote(got, got, send_sems.at[3 * i + j], recv_sems.at[3 * i + j], (px, py, c)).wait_recv()
        for i in range(n):
            r, cl = shapes[i]
            mine = _slab(dst[i], axes[i], r, cl, k, c)
            for j, (px, py) in enumerate(chips):
                _remote(mine, mine, send_sems.at[3 * i + j], recv_sems.at[3 * i + j], (px, py, c)).wait_send()

    return pl.pallas_call(
        body, name=name, in_specs=[HBM] * n + [SEM, SEM, ANY], out_specs=[HBM] * n,
        out_shape=[pltpu.HBM(b.shape, b.dtype) for b in bufs],
        input_output_aliases={i: i for i in range(n)}, compiler_params=SPLIT,
    )(*bufs, sems[0], sems[1], after)


def _gather_forward(bufs, axes, shapes, name):
    n = len(bufs)

    def body(*refs):
        dst = refs[n:2 * n]
        send_sems, recv_sems = refs[2 * n:]
        x, y, c, chips = _me()
        sibling = (x, y, 1 - c)
        sends = []
        for i in range(n):
            r, cl = shapes[i]
            for j, (px, py) in enumerate(chips):
                got = _slab(dst[i], axes[i], r, cl, 2 * px + py, c)
                cp = _remote(got, got, send_sems.at[3 * i + j], recv_sems.at[3 * i + j], sibling)
                cp.start()
                sends.append(cp)
        for i in range(n):
            r, cl = shapes[i]
            for j, (px, py) in enumerate(chips):
                got = _slab(dst[i], axes[i], r, cl, 2 * px + py, 1 - c)
                _remote(got, got, send_sems.at[3 * i + j], recv_sems.at[3 * i + j], sibling).wait_recv()
        for cp in sends:
            cp.wait_send()

    return pl.pallas_call(
        body, name=name, in_specs=[ANY] * n, out_specs=[ANY] * n,
        out_shape=[jax.ShapeDtypeStruct(b.shape, b.dtype) for b in bufs],
        input_output_aliases={i: i for i in range(n)},
        scratch_shapes=[pltpu.SemaphoreType.DMA((3 * n,)), pltpu.SemaphoreType.DMA((3 * n,))],
    )(*bufs)


def _split_start(name, arrays, geometry, count):
    n = len(arrays)

    def body(*refs):
        send, recv, token = refs[2 * n:]
        for i, (src, dst, _, dev) in enumerate(geometry(refs[n:2 * n])):
            _remote(src, dst, send.at[i], recv.at[i], dev).start()
        token[...] = jnp.zeros_like(token)

    sem = pltpu.SemaphoreType.DMA((count,))
    outs = pl.pallas_call(
        body, name=name, in_specs=[HBM] * n,
        out_specs=[HBM] * n + [SEM, SEM, pl.BlockSpec(memory_space=pltpu.VMEM)],
        out_shape=[pltpu.HBM(v.shape, v.dtype) for v in arrays] + [sem, sem, TOKEN],
        input_output_aliases={i: i for i in range(n)}, compiler_params=SPLIT,
    )(*[_in_hbm(v) for v in arrays])
    return list(outs[:n]), (outs[n], outs[n + 1]), outs[-1]


def _split_wait(name, arrays, sems, after, geometry):
    n = len(arrays)

    def body(*refs):
        send, recv = refs[n], refs[n + 1]
        copies = geometry(refs[n + 3:])
        for i, (_, _, land, dev) in enumerate(copies):
            _remote(land, land, send.at[i], recv.at[i], dev).wait_recv()
        for i, (src, _, _, dev) in enumerate(copies):
            _remote(src, src, send.at[i], recv.at[i], dev).wait_send()

    return list(pl.pallas_call(
        body, name=name, in_specs=[HBM] * n + [SEM, SEM, ANY], out_specs=[HBM] * n,
        out_shape=[pltpu.HBM(v.shape, v.dtype) for v in arrays],
        input_output_aliases={i: i for i in range(n)}, compiler_params=SPLIT,
    )(*arrays, sems[0], sems[1], after))


def _forward_geometry(axes, shapes):
    def geometry(bufs):
        x, y, c, chips = _me()
        out = []
        for i, buf in enumerate(bufs):
            r, cl = shapes[i]
            for px, py in chips:
                got = _slab(buf, axes[i], r, cl, 2 * px + py, c)
                out.append((got, got, _slab(buf, axes[i], r, cl, 2 * px + py, 1 - c), (x, y, 1 - c)))
        return out
    return geometry


def _pair_geometry(axes, shapes):
    def geometry(refs):
        n = len(refs) // 2
        x, y, c, _ = _me()
        out = []
        for w in range(n):
            r, cl = shapes[w]
            for j in range(4):
                land = refs[n + w].at[j]
                out.append((_slab(refs[w], axes[w], r, cl, j, 1 - c), land, land, (x, y, 1 - c)))
        return out
    return geometry


def _pair_exchange(fulls, axes, shapes, tag):
    n = len(fulls)

    def body(*refs):
        src, dst = refs[:n], refs[n:2 * n]
        send_sems, recv_sems = refs[2 * n:]
        x, y, c, _ = _me()
        sibling = (x, y, 1 - c)
        cps = []
        for w in range(n):
            r, cl = shapes[w]
            for j in range(4):
                cp = _remote(_slab(src[w], axes[w], r, cl, j, 1 - c), dst[w].at[j],
                             send_sems.at[4 * w + j], recv_sems.at[4 * w + j], sibling)
                cp.start()
                cps.append(cp)
        for cp in cps:
            cp.wait()

    out_shape = [jax.ShapeDtypeStruct((4, r // 2, cl), f.dtype) for (r, cl), f in zip(shapes, fulls)]
    return pl.pallas_call(
        body, name="reduce_pair_exchange_" + tag, in_specs=[ANY] * n, out_specs=[ANY] * n, out_shape=out_shape,
        scratch_shapes=[pltpu.SemaphoreType.DMA((4 * n,)), pltpu.SemaphoreType.DMA((4 * n,))],
    )(*fulls)


def _chip_start(parts, tag):
    n = len(parts)

    def body(*refs):
        src, land = refs[2 * n:3 * n], refs[3 * n:4 * n]
        send_sems, recv_sems, token = refs[4 * n:]
        x, y, c, chips = _me()
        k = 2 * x + y
        for w in range(n):
            for j, (px, py) in enumerate(chips):
                _remote(src[w].at[2 * px + py], land[w].at[k], send_sems.at[3 * w + j], recv_sems.at[3 * w + j],
                        (px, py, c)).start()
        token[...] = jnp.zeros_like(token)

    lands = [lax.empty(p.shape, p.dtype) for p in parts]
    sem = pltpu.SemaphoreType.DMA((3 * n,))
    outs = pl.pallas_call(
        body, name="reduce_ici_start_" + tag, in_specs=[HBM] * (2 * n),
        out_specs=[HBM] * (2 * n) + [SEM, SEM, pl.BlockSpec(memory_space=pltpu.VMEM)],
        out_shape=[pltpu.HBM(p.shape, p.dtype) for p in parts + lands] + [sem, sem, TOKEN],
        input_output_aliases={i: i for i in range(2 * n)}, compiler_params=SPLIT,
    )(*[_in_hbm(v) for v in parts + lands])
    return outs[:n], outs[n:2 * n], outs[2 * n], outs[2 * n + 1], outs[-1]


def _chip_wait(parts, lands, send_sems, recv_sems, after, tag):
    n = len(parts)

    def body(*refs):
        send, recv = refs[2 * n], refs[2 * n + 1]
        src, land = refs[2 * n + 3:3 * n + 3], refs[3 * n + 3:]
        x, y, c, chips = _me()
        for w in range(n):
            for j, (px, py) in enumerate(chips):
                got = land[w].at[2 * px + py]
                _remote(got, got, send.at[3 * w + j], recv.at[3 * w + j], (px, py, c)).wait_recv()
        for w in range(n):
            for j, (px, py) in enumerate(chips):
                sent = src[w].at[2 * px + py]
                _remote(sent, sent, send.at[3 * w + j], recv.at[3 * w + j], (px, py, c)).wait_send()

    outs = pl.pallas_call(
        body, name="reduce_ici_wait_" + tag, in_specs=[HBM] * (2 * n) + [SEM, SEM, ANY], out_specs=[HBM] * (2 * n),
        out_shape=[pltpu.HBM(p.shape, p.dtype) for p in parts + lands],
        input_output_aliases={i: i for i in range(2 * n)}, compiler_params=SPLIT,
    )(*parts, *lands, send_sems, recv_sems, after)
    chip = 2 * lax.axis_index("x") + lax.axis_index("y")
    return [lax.dynamic_update_slice(s, lax.dynamic_index_in_dim(p, chip, 0, keepdims=True), (chip, 0, 0))
            for p, s in zip(outs[:n], outs[n:])]


def _half_swap(halves, tag):
    n = len(halves)
    core = lax.axis_index("c")
    bufs = [lax.dynamic_update_slice(lax.empty((2,) + h.shape, h.dtype), h[None], (core, 0, 0)) for h in halves]

    def body(*refs):
        dst = refs[n:2 * n]
        send_sems, recv_sems = refs[2 * n:]
        x, y, c, _ = _me()
        sibling = (x, y, 1 - c)
        cps = []
        for w in range(n):
            cp = _remote(dst[w].at[c], dst[w].at[c], send_sems.at[w], recv_sems.at[w], sibling)
            cp.start()
            cps.append(cp)
        for w in range(n):
            other = dst[w].at[1 - c]
            _remote(other, other, send_sems.at[w], recv_sems.at[w], sibling).wait_recv()
        for cp in cps:
            cp.wait_send()

    outs = pl.pallas_call(
        body, name="reduce_half_swap_" + tag, in_specs=[ANY] * n, out_specs=[ANY] * n,
        out_shape=[jax.ShapeDtypeStruct(b.shape, b.dtype) for b in bufs],
        input_output_aliases={w: w for w in range(n)},
        scratch_shapes=[pltpu.SemaphoreType.DMA((n,)), pltpu.SemaphoreType.DMA((n,))],
    )(*bufs)
    return [o.reshape(2 * o.shape[1], o.shape[2]) for o in outs]


def _add_parts(full, axis, rows, sib, name):
    _, r, c = sib.shape
    tr, tc = _tile(r, 256, 16), _tile(c, 2048)
    nb = r // tr
    core = jnp.reshape(lax.axis_index("c"), (1,)).astype(jnp.int32)

    def body(c_ref, a_ref, b_ref, o_ref):
        o_ref[0] = (a_ref[...].astype(F32) + b_ref[0].astype(F32)).astype(BF16)

    blk = pl.BlockSpec((1, tr, tc), lambda j, i, l, cr: (j, i, l))
    return pl.pallas_call(
        body, name=name,
        grid_spec=pltpu.PrefetchScalarGridSpec(
            num_scalar_prefetch=1, grid=(4, nb, c // tc),
            in_specs=[pl.BlockSpec((tr, tc), lambda j, i, l, cr: ((_slot(axis, j) * 2 + cr[0]) * nb + i, l)), blk],
            out_specs=blk),
        out_shape=jax.ShapeDtypeStruct(sib.shape, BF16),
        compiler_params=_cparams(("parallel", "parallel", "parallel")),
    )(core, full, sib)


def _sum_slots(a, name):
    _, r, c = a.shape
    tr, tc = _tile(r, 256, 8), _tile(c, 2048)

    def body(a_ref, o_ref):
        v = a_ref[...].astype(F32)
        o_ref[...] = ((v[0] + v[1]) + v[2]) + v[3]

    return pl.pallas_call(
        body, name=name, grid=(r // tr, c // tc),
        in_specs=[pl.BlockSpec((4, tr, tc), lambda i, l: (0, i, l))],
        out_specs=pl.BlockSpec((tr, tc), lambda i, l: (i, l)),
        out_shape=jax.ShapeDtypeStruct((r, c), F32),
        compiler_params=_cparams(("parallel", "parallel")),
    )(a)


class _Reducer:
    def __init__(self, spec):
        self.spec = spec
        self.paired = {}
        self.pending = []

    def pair(self, name, full):
        ax, shp = self.spec[name]
        land = lax.empty((4, shp[0] // 2, shp[1]), full.dtype)
        arrays, sems, token = _split_start("reduce_pair_start_" + name, [full, land], _pair_geometry([ax], [shp]), 4)
        self.paired[name] = (arrays, sems)
        return token

    def ship(self, tag, names, after):
        parts = []
        for n in names:
            ax, shp = self.spec[n]
            arrays, sems = self.paired.pop(n)
            full, sib = _split_wait("reduce_pair_wait_" + n, arrays, sems, after, _pair_geometry([ax], [shp]))
            parts.append(_add_parts(full, ax, shp[0], sib, name=f"reduce_add_{n}"))
        parts, lands, send, recv, token = _chip_start(parts, tag)
        self.pending.append((tag, names, parts, lands, send, recv))
        return token

    def start(self, tag, grads):
        names = list(grads)
        fulls, axes = [grads[n] for n in names], [self.spec[n][0] for n in names]
        shapes = [self.spec[n][1] for n in names]
        from_sibling = _pair_exchange(fulls, axes, shapes, tag)
        parts = [_add_parts(f, a, r, s, name=f"reduce_add_{n}")
                 for n, f, a, (r, cl), s in zip(names, fulls, axes, shapes, from_sibling)]
        parts, lands, send, recv, token = _chip_start(parts, tag)
        self.pending.append((tag, names, parts, lands, send, recv))
        return token

    def finish(self, after, tags):
        out = {}
        for tag, names, parts, lands, send, recv in [p for p in self.pending if p[0] in tags]:
            slots = _chip_wait(parts, lands, send, recv, after, tag)
            halves = [_sum_slots(s, name=f"reduce_sum_{n}") for n, s in zip(names, slots)]
            out.update(zip(names, _half_swap(halves, tag)))
        return out


def _allreduce_small(pack, after):
    rows = pack.shape[0]

    def body(p_ref, _, o_ref, slots, send_sems, recv_sems):
        x, y, c, _ = _me()
        me = 4 * x + 2 * y + c
        slots[me] = p_ref[...]
        cps = []
        for r in range(1, 8):
            peer = (x ^ (r >> 2), y ^ ((r >> 1) & 1), c ^ (r & 1))
            cp = _remote(p_ref, slots.at[me], send_sems.at[r - 1], recv_sems.at[r - 1], peer)
            cp.start()
            cps.append(cp)
        for r in range(1, 8):
            frm = me ^ r
            _remote(slots.at[frm], slots.at[frm], send_sems.at[r - 1], recv_sems.at[r - 1], (x, y, c)).wait_recv()
        for cp in cps:
            cp.wait_send()
        acc = slots[0]
        for s in range(1, 8):
            acc = acc + slots[s]
        o_ref[...] = acc

    vm = pl.BlockSpec(memory_space=pltpu.VMEM)
    return pl.pallas_call(
        body, name="allreduce_small", in_specs=[vm, ANY], out_specs=vm,
        out_shape=jax.ShapeDtypeStruct(pack.shape, F32),
        scratch_shapes=[pltpu.VMEM((8, rows, HEAD_DIM), F32), pltpu.SemaphoreType.DMA((7,)),
                        pltpu.SemaphoreType.DMA((7,))],
    )(pack, after)


_ROWS = ["norm_mix", "norm_ffn", "mem_norm", "fox_q_norm", "fox_k_norm", "gdn_out_norm", "mem_q_norm",
         "mem_k_norm", "fox_f_bias", "gdn_a_log", "gdn_dt_bias"]


def _pack_rows(vals):
    out = []
    for name in _ROWS:
        v = vals[name].reshape(-1)
        n = -(-v.shape[0] // HEAD_DIM) * HEAD_DIM
        out.append(jnp.pad(v, (0, n - v.shape[0])).reshape(-1, HEAD_DIM))
    return jnp.concatenate(out, axis=0)


def _unpack_rows(pack, like):
    out, r = {}, 0
    for name in _ROWS:
        n = like[name].shape[-1]
        nr = -(-n // HEAD_DIM)
        out[name] = pack[r:r + nr].reshape(1, -1)[:, :n]
        r += nr
    return out, r


def kernel(x, mem, norm_mix, w_in, fox_f_bias, fox_q_norm, fox_k_norm, gdn_conv, gdn_a_log, gdn_dt_bias, gdn_out_norm, mem_norm, w_mem_kv, mem_q_norm, mem_k_norm, w_out, norm_ffn, w_gate_up, w_down, loss_target, m_norm_mix, m_w_in, m_fox_f_bias, m_fox_q_norm, m_fox_k_norm, m_gdn_conv, m_gdn_a_log, m_gdn_dt_bias, m_gdn_out_norm, m_mem_norm, m_w_mem_kv, m_mem_q_norm, m_mem_k_norm, m_w_out, m_norm_ffn, m_w_gate_up, m_w_down, v_norm_mix, v_w_in, v_fox_f_bias, v_fox_q_norm, v_fox_k_norm, v_gdn_conv, v_gdn_a_log, v_gdn_dt_bias, v_gdn_out_norm, v_mem_norm, v_w_mem_kv, v_mem_q_norm, v_mem_k_norm, v_w_out, v_norm_ffn, v_w_gate_up, v_w_down):
    a = dict(locals())
    d = x.shape[-1]
    lay = _Layout(d)
    chip = 2 * lax.axis_index("x") + lax.axis_index("y")
    small = {n: a[n] for n in _ROWS}
    big = ["w_in", "w_mem_kv", "w_out", "w_gate_up", "w_down"]
    axes = [0, 0, 0, 1, 0]

    conv_cols = gdn_conv.shape[-1]
    conv_n = CONV_WIDTH * conv_cols
    conv_rows = -(-conv_n // HEAD_DIM)
    conv_blk = jnp.pad(gdn_conv.reshape(-1), (0, 32 * HEAD_DIM - conv_n)).reshape(32, HEAD_DIM)
    axis_of = dict(zip(big, axes), conv=0, w_in_a=0, w_in_b=0)
    shape_of = {n: a[n].shape[1:] for n in big[1:]}
    shape_of.update(w_in_a=(w_in.shape[1], lay.cols_a), w_in_b=(w_in.shape[1], lay.cols_b), conv=conv_blk.shape)
    placed = {n: _cast_place(a[n][0], axis_of[n], "cast_" + n) for n in big[1:]}
    placed["w_in_a"] = _cast_place(w_in[0], 0, "cast_w_in_a", lambda v: lay.regroup(v)[:, :lay.cols_a], lay.cols_a)
    placed["w_in_b"] = _cast_place(w_in[0], 0, "cast_w_in_b", lambda v: lay.regroup(v)[:, lay.cols_a:], lay.cols_b)
    placed["conv"] = lax.dynamic_update_slice(lax.empty((4 * 32, HEAD_DIM), F32), conv_blk, (chip * 32, 0))
    grouped = {"in_a": ["w_in_a"], "in_b": ["w_in_b"], "mixer": ["w_mem_kv", "conv"], "out": ["w_out"],
               "gate_up": ["w_gate_up"], "down": ["w_down"]}
    inflight = {}

    def start(tags, name):
        names = [n for t in tags for n in grouped[t]]
        bufs, sems, _ = _gather_start([placed[n] for n in names], [axis_of[n] for n in names],
                                      [shape_of[n] for n in names],
                                      [[names.index(n) for n in grouped[t]] for t in tags], name)
        for t, pair in zip(tags, sems):
            inflight[t] = ([bufs[names.index(n)] for n in grouped[t]], pair)

    start(["in_a"], "gather_ici_start_in")
    start(["in_b", "mixer", "out", "gate_up", "down"], "gather_ici_start_rest")

    forwarding = {}

    def prefetch(tag, after):
        bufs, sem_pair = inflight.pop(tag)
        ax, shp = [axis_of[n] for n in grouped[tag]], [shape_of[n] for n in grouped[tag]]
        got = _gather_wait(bufs, ax, shp, sem_pair, after, "gather_ici_wait_" + tag)
        geometry = _forward_geometry(ax, shp)
        got, sems, _ = _split_start("gather_forward_start_" + tag, got, geometry, 3 * len(got))
        forwarding[tag] = (got, sems, geometry)

    def weights(tag, after):
        got, sems, geometry = forwarding.pop(tag)
        got = _split_wait("gather_forward_wait_" + tag, got, sems, after, geometry)
        if tag != "mixer":
            return got
        taps = got[1].reshape(4, 32 * HEAD_DIM)[:, :conv_n].reshape(4, CONV_WIDTH, conv_cols)
        return got[0], jnp.transpose(taps, (1, 0, 2)).reshape(CONV_WIDTH, 4 * conv_cols)

    sp = dict(small)
    reducer = _Reducer({n: (axis_of[n], shape_of[n]) for n in big[1:] + ["w_in_a", "w_in_b"]})
    loss_blk, dx, g = _local_step(x[0], mem[0], loss_target[0], prefetch, weights, reducer, sp)

    gsmall = {n: g[n] for n in _ROWS}
    pack = jnp.concatenate([_pack_rows(gsmall), g["gdn_conv"].reshape(-1, HEAD_DIM), loss_blk], axis=0)
    pack = jnp.pad(pack, ((0, -pack.shape[0] % 8), (0, 0)))
    out = {"grad_x": dx[None]}

    def adamw_shards(reduced):
        if "w_in_a" in reduced:
            reduced = {"w_in": (reduced["w_in_a"], reduced["w_in_b"])}
        for n, gsh in reduced.items():
            join = (lambda ga, gb: lay.ungroup(jnp.concatenate([ga, gb], axis=1))) if n == "w_in" else None
            res = _adamw(a[n][0], gsh, a["m_" + n][0], a["v_" + n][0], g_fn=join, name="adamw_" + n)
            for pre, r in zip(["grad_", "delta_", "new_m_", "new_v_"], res):
                out[pre + n] = r[None]
        return res[0]

    done = adamw_shards(reducer.finish(dx, ("ffn", "mix")))
    tot = _allreduce_small(pack, done)
    gs, r0 = _unpack_rows(tot, small)
    conv_g = tot[r0:r0 + CONV_WIDTH * 4 * conv_cols // HEAD_DIM].reshape(CONV_WIDTH, 4 * conv_cols)
    gs_conv = lax.dynamic_slice_in_dim(conv_g, chip * conv_cols, conv_cols, axis=1)
    out["loss"] = tot[r0 + CONV_WIDTH * 4 * conv_cols // HEAD_DIM, 0]
    adamw_shards(reducer.finish(tot, ("in",)))
    conv_pad = lambda v: jnp.pad(v.reshape(-1), (0, conv_rows * HEAD_DIM - conv_n)).reshape(conv_rows, HEAD_DIM)
    packs = []
    for src, cv in [(small, gdn_conv), (gs, gs_conv), ({n: a["m_" + n] for n in _ROWS}, m_gdn_conv),
                    ({n: a["v_" + n] for n in _ROWS}, v_gdn_conv)]:
        packs.append(jnp.concatenate([_pack_rows(src), conv_pad(cv)], axis=0))
    res = _adamw(*packs, name="adamw_small")
    for pre, r in zip(["grad_", "delta_", "new_m_", "new_v_"], res):
        vals, r1 = _unpack_rows(r, small)
        for n in _ROWS:
            out[pre + n] = vals[n]
        out[pre + "gdn_conv"] = r[r1:r1 + conv_rows].reshape(-1)[:conv_n].reshape(gdn_conv.shape)
    names = ["norm_mix", "w_in", "fox_f_bias", "fox_q_norm", "fox_k_norm", "gdn_conv", "gdn_a_log", "gdn_dt_bias",
             "gdn_out_norm", "mem_norm", "w_mem_kv", "mem_q_norm", "mem_k_norm", "w_out", "norm_ffn", "w_gate_up",
             "w_down"]
    return (out["loss"], out["grad_x"], *[out[p + n] for p in ["grad_", "delta_", "new_m_", "new_v_"] for n in names])
```

```python
import functools
import math

import jax
import jax.numpy as jnp
from jax import lax
from jax.experimental import pallas as pl
from jax.experimental.pallas import tpu as pltpu

F32, BF16 = jnp.float32, jnp.bfloat16
HEAD_DIM = 128
CHUNK = 64
N_MEM_HEADS = 4
CONV_WIDTH = 4
NORM_EPS = 1e-6
ADAM_LR, ADAM_B1, ADAM_B2, ADAM_EPS, ADAM_WD, ADAM_STEP = 0.001, 0.9, 0.999, 1e-08, 0.01, 10
VMEM_LIMIT = 48 * 1024 * 1024
NEG = -1e30
MESH = pl.DeviceIdType.MESH


def _cparams(sem=None, **kw):
    if sem is not None:
        kw["dimension_semantics"] = sem
    return pltpu.CompilerParams(vmem_limit_bytes=VMEM_LIMIT, **kw)


def _tile(n, target, mult=128):
    best = None
    d = mult
    while d <= min(n, target):
        if n % d == 0:
            best = d
        d += mult
    return best if best is not None else n


def _dot(a, b, dims, hi):
    if a.ndim == 3:
        dn = (((dims[0][0] + 1,), (dims[1][0] + 1,)), ((0,), (0,)))
    else:
        dn = (dims, ((), ()))
    if hi is not None:
        return lax.dot_general(a, b, dn, precision=hi, preferred_element_type=F32)
    return lax.dot_general(a.astype(BF16), b.astype(BF16), dn, preferred_element_type=F32)


def _make_dots(hi, cotangent=None):
    @jax.custom_vjp
    def nn(a, b):
        return _dot(a, b, ((1,), (0,)), hi)

    @jax.custom_vjp
    def nt(a, b):
        return _dot(a, b, ((1,), (1,)), hi)

    @jax.custom_vjp
    def tn(a, b):
        return _dot(a, b, ((0,), (0,)), hi)

    bnn, bnt, btn = cotangent or (nn, nt, tn)
    nn.defvjp(lambda a, b: (nn(a, b), (a, b)), lambda r, g: (bnt(g, r[1]), btn(r[0], g)))
    nt.defvjp(lambda a, b: (nt(a, b), (a, b)), lambda r, g: (bnn(g, r[1]), btn(g, r[0])))
    tn.defvjp(lambda a, b: (tn(a, b), (a, b)), lambda r, g: (bnt(r[1], g), bnn(r[0], g)))
    return nn, nt, tn


_nn, _nt, _tn = _make_dots(None)
_nn_hi, _nt_hi, _tn_hi = _make_dots(lax.Precision.HIGHEST)
_nn_x3, _nt_x3, _tn_x3 = _make_dots(lax.Precision.HIGH, (_nn, _nt, _tn))


def _sigmoid(x):
    return 1.0 / (1.0 + jnp.exp(-x))


@jax.custom_vjp
def _softplus(x):
    return jnp.maximum(x, 0.0) + jnp.log(1.0 + jnp.exp(-jnp.abs(x)))


_softplus.defvjp(lambda x: (_softplus(x), x), lambda x, g: (g * _sigmoid(x),))


def _silu(x):
    return x * _sigmoid(x)


def _rms_fn(x, gain, z=None):
    y = x * lax.rsqrt(jnp.mean(x * x, axis=-1, keepdims=True) + NORM_EPS) * gain
    if z is not None:
        y = y * _silu(z)
    return y


def _mm(a, b, *, ta=False, tb=False, out_dtype=F32, res=None, stack=None, after=None, name):
    a2, b2 = a.shape[-2:], b.shape[-2:]
    ns = b.shape[0] if stack else 1
    m = a2[1] if ta else a2[0]
    k = a2[0] if ta else a2[1]
    n = b2[0] if tb else b2[1]
    assert k == (b2[1] if tb else b2[0])
    tm, tn, tk = _mm_tiles(m, n, k, ns if stack == "sum" else 1, a.dtype.itemsize, b.dtype.itemsize,
                           jnp.dtype(out_dtype).itemsize, res is not None)
    nk = k // tk
    single = nk == 1 and stack != "sum"
    dims = ((0 if ta else 1,), (1 if tb else 0,))
    if stack == "sum":
        order = lambda g0, g1, g2, g3: (g2, g0, g1, g3)
        grid = (m // tm, n // tn, ns, nk)
    else:
        order = lambda g0, g1, g2, g3: (g0, g1, g2, g3)
        grid = (ns, m // tm, n // tn, nk)

    def body(*refs):
        if after is not None:
            refs = refs[:2 + (res is not None)] + refs[3 + (res is not None):]
        if single:
            a_ref, b_ref = refs[:2]
            r = lax.dot_general(a_ref[...].astype(BF16), b_ref[...].astype(BF16), (dims, ((), ())),
                                preferred_element_type=F32)
            if res is not None:
                r = r + refs[2][...]
            refs[-1][...] = r.astype(out_dtype)
            return
        if res is None:
            a_ref, b_ref, o_ref, acc = refs
        else:
            a_ref, b_ref, r_ref, o_ref, acc = refs
        s, _, _, kk = order(*[pl.program_id(d) for d in range(4)])
        first = kk == 0
        last = kk == nk - 1
        if stack == "sum":
            first, last = first & (s == 0), last & (s == ns - 1)

        @pl.when(first)
        def _():
            acc[...] = jnp.zeros_like(acc)

        acc[...] += lax.dot_general(a_ref[...].astype(BF16), b_ref[...].astype(BF16), (dims, ((), ())),
                                    preferred_element_type=F32)

        @pl.when(last)
        def _():
            r = acc[...]
            if res is not None:
                r = r + r_ref[...]
            o_ref[...] = r.astype(out_dtype)

    def spec(shape, idx, stacked):
        if stacked:
            return pl.BlockSpec((None,) + shape, lambda *g: (order(*g)[0],) + idx(*order(*g)))
        return pl.BlockSpec(shape, lambda *g: idx(*order(*g)))

    a_spec = (spec((tk, tm), lambda s, i, j, kk: (kk, i), stack == "sum") if ta
              else spec((tm, tk), lambda s, i, j, kk: (i, kk), stack == "sum"))
    b_spec = (spec((tn, tk), lambda s, i, j, kk: (j, kk), bool(stack)) if tb
              else spec((tk, tn), lambda s, i, j, kk: (kk, j), bool(stack)))
    o_spec = spec((tm, tn), lambda s, i, j, kk: (i, j), stack == "out")
    ins, specs = [a, b], [a_spec, b_spec]
    if res is not None:
        ins.append(res)
        specs.append(o_spec)
    if after is not None:
        ins.append(after)
        specs.append(pl.BlockSpec(after.shape, lambda *g: (0,) * after.ndim))
    sem = (("parallel", "parallel", "arbitrary", "arbitrary") if stack == "sum"
           else ("parallel", "parallel", "parallel", "arbitrary"))
    return pl.pallas_call(
        body, name=name, grid=grid, in_specs=specs, out_specs=o_spec,
        out_shape=jax.ShapeDtypeStruct(((ns,) if stack == "out" else ()) + (m, n), out_dtype),
        scratch_shapes=[] if single else [pltpu.VMEM((tm, tn), F32)],
        compiler_params=_cparams(sem),
    )(*ins)


MM_VMEM_BUDGET = 40 * 1024 * 1024


def _mm_tiles(m, n, k, ns, sa, sb, so, has_res):
    def divs(x, mult, cap):
        out = [d for d in range(mult, min(x, cap) + 1, mult) if x % d == 0]
        return out or [x]

    best = None
    for tk in divs(k, 128, 8192):
        nk = (k // tk) * ns
        for tm in divs(m, 8, 2048):
            for tn in divs(n, 128, 2048):
                vmem = 2 * (tm * tk * sa + tk * tn * sb + tm * tn * so) + (2 * tm * tn * 4 if has_res else 0)
                vmem += tm * tn * 4 if nk > 1 else 0
                if vmem > MM_VMEM_BUDGET:
                    continue
                steps = (m // tm) * (n // tn) * nk
                traffic = (m // tm) * k * n * sb * ns + (n // tn if nk > 1 else 1) * m * k * sa * ns
                cost = steps * 0.4e-6 + traffic / 2.5e12 + (nk * m * n * 8 / 6e12 if nk > 1 else 0)
                cost += 2.0 * m * n * k * ns / 7e14
                if best is None or cost < best[0]:
                    best = (cost, tm, tn, tk)
    return best[1:]


def _norm_fwd(x, xoff, gain, ncol, w, out_dtype, *, z=None, zoff=0, into=None, into_off=0, name):
    t = x.shape[0]
    tr = _tile(t, max(256, (1 << 18) // w), 8)

    def body(*refs):
        x_ref, g_ref, o_ref = refs[0], refs[1], refs[-1]
        y = _rms_fn(x_ref[...], g_ref[...]) if z is None else _rms_fn(x_ref[...], g_ref[...], refs[2][...])
        o_ref[...] = y.astype(out_dtype)

    ins = [x, gain]
    specs = [pl.BlockSpec((tr, w), lambda j, r: (r, xoff + j)), pl.BlockSpec((1, w), lambda j, r: (0, 0))]
    if z is not None:
        ins.append(z)
        specs.append(pl.BlockSpec((tr, w), lambda j, r: (r, zoff + j)))
    aliases = {}
    if into is not None:
        aliases = {len(ins): 0}
        ins.append(into)
        specs.append(pl.BlockSpec(memory_space=pl.ANY))
    return pl.pallas_call(
        body, name=name, grid=(ncol, t // tr), in_specs=specs,
        out_specs=pl.BlockSpec((tr, w), lambda j, r: (r, into_off + j)),
        out_shape=jax.ShapeDtypeStruct((t, ncol * w) if into is None else into.shape, out_dtype),
        input_output_aliases=aliases, compiler_params=_cparams(("parallel", "parallel")),
    )(*ins)


def _norm_bwd(x, xoff, gain, dy, dyoff, ncol, w, *, z=None, zoff=0, res=None, name):
    t = x.shape[0]
    tr = _tile(t, max(256, (1 << 18) // w), 8)

    def body(*refs):
        it = iter(refs)
        x_ref, g_ref = next(it), next(it)
        z_ref = next(it) if z is not None else None
        dy_ref = next(it)
        r_ref = next(it) if res is not None else None
        dx_ref = next(it)
        dz_ref = next(it) if z is not None else None
        dg_ref = next(it)

        @pl.when((pl.program_id(0) == 0) & (pl.program_id(1) == 0))
        def _():
            dg_ref[...] = jnp.zeros_like(dg_ref)

        args = (x_ref[...], g_ref[...]) + ((z_ref[...],) if z is not None else ())
        _, vjp = jax.vjp(_rms_fn, *args)
        grads = vjp(dy_ref[...].astype(F32))
        dx = grads[0]
        if res is not None:
            dx = dx + r_ref[...]
        dx_ref[...] = dx
        if z is not None:
            dz_ref[...] = grads[2]
        dg_ref[...] += grads[1]

    ins = [x, gain]
    specs = [pl.BlockSpec((tr, w), lambda j, r: (r, xoff + j)), pl.BlockSpec((1, w), lambda j, r: (0, 0))]
    if z is not None:
        ins.append(z)
        specs.append(pl.BlockSpec((tr, w), lambda j, r: (r, zoff + j)))
    ins.append(dy)
    specs.append(pl.BlockSpec((tr, w), lambda j, r: (r, dyoff + j)))
    blk = pl.BlockSpec((tr, w), lambda j, r: (r, j))
    if res is not None:
        ins.append(res)
        specs.append(blk)
    full = jax.ShapeDtypeStruct((t, ncol * w), F32)
    out_shape, out_specs = [full], [blk]
    if z is not None:
        out_shape.append(full)
        out_specs.append(blk)
    out_shape.append(jax.ShapeDtypeStruct((1, w), F32))
    out_specs.append(pl.BlockSpec((1, w), lambda j, r: (0, 0)))
    return pl.pallas_call(
        body, name=name, grid=(ncol, t // tr), in_specs=specs, out_specs=out_specs, out_shape=out_shape,
        compiler_params=_cparams(("arbitrary", "arbitrary")),
    )(*ins)


def _small_fn(x, pa, pb, nf, ng):
    lane = lax.broadcasted_iota(jnp.int32, x.shape, 1)
    zz = x + pb
    logf = -_softplus(-zz)
    g = -jnp.exp(pa) * _softplus(zz)
    beta = _sigmoid(x)
    return jnp.where(lane < nf, logf, jnp.where(lane < nf + ng, g, beta))


def _tri(n, upper):
    r = lax.broadcasted_iota(jnp.int32, (n, n), 0)
    c = lax.broadcasted_iota(jnp.int32, (n, n), 1)
    return jnp.where((c >= r) if upper else (c <= r), 1.0, 0.0).astype(F32)


def _small_fwd(p, off, pa, pb, nf, ng):
    t = p.shape[0]
    blk = HEAD_DIM
    nb = t // blk

    def body(x_ref, pa_ref, pb_ref, v_ref, c_ref):
        v_ref[...] = _small_fn(x_ref[...], pa_ref[...], pb_ref[...], nf, ng)
        tri = _tri(blk, False)

        carry = jnp.zeros((1, HEAD_DIM), F32)
        for i in range(nb):
            rows = slice(i * blk, (i + 1) * blk)
            c = _nn_hi(tri, v_ref[rows, :]) + carry
            c_ref[rows, :] = c
            carry = c[blk - 1:blk, :]

    row = pl.BlockSpec((1, HEAD_DIM), lambda i: (0, 0))
    out = pl.BlockSpec((t, HEAD_DIM), lambda i: (0, 0))
    return pl.pallas_call(
        body, name="small_fwd", grid=(1,),
        in_specs=[pl.BlockSpec((t, HEAD_DIM), lambda i: (0, off)), row, row], out_specs=[out, out],
        out_shape=[jax.ShapeDtypeStruct((t, HEAD_DIM), F32)] * 2,
        compiler_params=_cparams(("arbitrary",)),
    )(p, pa, pb)


def _small_bwd(p, off, pa, pb, dvals, dcsum, nf, ng):
    t = p.shape[0]
    blk = HEAD_DIM
    nb = t // blk

    def body(x_ref, pa_ref, pb_ref, dv_ref, dc_ref, dx_ref, dpa_ref, dpb_ref, tot_ref):
        tri = _tri(blk, True)

        carry = jnp.zeros((1, HEAD_DIM), F32)
        for i in reversed(range(nb)):
            rows = slice(i * blk, (i + 1) * blk)
            c = _nn_hi(tri, dc_ref[rows, :]) + carry
            tot_ref[rows, :] = c + dv_ref[rows, :]
            carry = c[0:1, :]
        f = functools.partial(_small_fn, nf=nf, ng=ng)
        _, vjp = jax.vjp(f, x_ref[...], pa_ref[...], pb_ref[...])
        dx, dpa, dpb = vjp(tot_ref[...])
        dx_ref[...] = dx
        dpa_ref[...] = dpa
        dpb_ref[...] = dpb

    row = pl.BlockSpec((1, HEAD_DIM), lambda i: (0, 0))
    full = pl.BlockSpec((t, HEAD_DIM), lambda i: (0, 0))
    return pl.pallas_call(
        body, name="small_bwd", grid=(1,),
        in_specs=[pl.BlockSpec((t, HEAD_DIM), lambda i: (0, off)), row, row, full, full],
        out_specs=[full, row, row],
        out_shape=[jax.ShapeDtypeStruct((t, HEAD_DIM), F32), jax.ShapeDtypeStruct((1, HEAD_DIM), F32),
                   jax.ShapeDtypeStruct((1, HEAD_DIM), F32)],
        scratch_shapes=[pltpu.VMEM((t, HEAD_DIM), F32)],
        compiler_params=_cparams(("arbitrary",)),
    )(p, pa, pb, dvals, dcsum)


def _fox_heads(nf, most):
    return next(h for h in range(most, 0, -1) if nf % h == 0)


def _fox_fwd(q, k, v, cc, cr, nf, tq, tk, d_mix):
    t = q.shape[0]
    scale = HEAD_DIM ** -0.5
    assert tq == tk

    vt = jnp.transpose(v.reshape(t // tk, tk, nf, HEAD_DIM), (2, 0, 3, 1))

    hp = _fox_heads(nf, 3)
    lanes = lambda h: slice(h * HEAD_DIM, (h + 1) * HEAD_DIM)

    def body(q_ref, k_ref, vt_ref, cc_ref, cr_ref, o_ref, lse_ref, mix_ref):
        i = pl.program_id(1)
        qs = [q_ref[:, lanes(h)] for h in range(hp)]
        cqs = [cr_ref[h, i] for h in range(hp)]
        ones = jnp.ones((8, tk), BF16)
        diff = lax.broadcasted_iota(jnp.int32, (tk, tq), 0) - lax.broadcasted_iota(jnp.int32, (tk, tq), 1)

        def scores(h, j):
            ks = pl.ds(pl.multiple_of(j * tk, tk), tk)
            return lax.dot_general(k_ref[ks, lanes(h)], qs[h], (((1,), (1,)), ((), ())),
                                   preferred_element_type=F32)

        def tile(h, j, m, l, acc, s, masked):
            ks = pl.ds(pl.multiple_of(j * tk, tk), tk)
            s = s * scale + cqs[h] - cc_ref[h, ks, :]
            if masked:
                s = jnp.where(diff <= 0, s, NEG)
            m_new = jnp.maximum(m, jnp.max(s, axis=0, keepdims=True))
            pr = jnp.exp(s - m_new).astype(BF16)
            alpha = jnp.exp(m - m_new)
            l = alpha * l + jnp.dot(ones, pr, preferred_element_type=F32)[:1]
            acc = alpha * acc + jnp.dot(vt_ref[h, j], pr, preferred_element_type=F32)
            return m_new, l, acc

        def step(j, carry):
            nxt = [scores(h, j + 1) for h in range(hp)]
            return tuple(tile(h, j, *carry[h], False) + (nxt[h],) for h in range(hp))

        init = tuple((jnp.full((1, tq), NEG, F32), jnp.zeros((1, tq), F32), jnp.zeros((HEAD_DIM, tq), F32),
                      scores(h, 0)) for h in range(hp))
        carry = lax.fori_loop(0, i, step, init)
        for h in range(hp):
            m, l, acc = tile(h, i, *carry[h], True)
            o = jnp.transpose(acc / l)
            o_ref[:, lanes(h)] = o
            mix_ref[:, lanes(h)] = o.astype(BF16)
            lse_ref[h, 0] = m + jnp.log(l)

    w = hp * HEAD_DIM
    qblk = pl.BlockSpec((tq, w), lambda h, i: (i, h))
    return pl.pallas_call(
        body, name="fox_fwd", grid=(nf // hp, t // tq),
        in_specs=[qblk, pl.BlockSpec((t, w), lambda h, i: (0, h)),
                  pl.BlockSpec((hp, t // tk, HEAD_DIM, tk), lambda h, i: (h, 0, 0, 0)),
                  pl.BlockSpec((hp, t, 1), lambda h, i: (h, 0, 0)),
                  pl.BlockSpec((hp, t // tk, 1, tk), lambda h, i: (h, 0, 0, 0))],
        out_specs=[qblk, pl.BlockSpec((hp, 1, 1, tq), lambda h, i: (h, i, 0, 0)), qblk],
        out_shape=[jax.ShapeDtypeStruct((t, nf * HEAD_DIM), F32), jax.ShapeDtypeStruct((nf, t // tq, 1, tq), F32),
                   jax.ShapeDtypeStruct((t, d_mix), BF16)],
        compiler_params=_cparams(("parallel", "parallel")),
    )(q, k, vt, cc, cr)


def _fox_bwd(q, k, v, cc, cr, o, lse, dmix, nf, tq, tk):
    t = q.shape[0]
    scale = HEAD_DIM ** -0.5
    assert tq == tk
    hp = _fox_heads(nf, 3)
    lanes = lambda h: slice(h * HEAD_DIM, (h + 1) * HEAD_DIM)
    kt = jnp.transpose(k.reshape(t // tk, tk, nf, HEAD_DIM), (2, 0, 3, 1))

    def body(q_ref, k_ref, kt_ref, v_ref, cc_ref, cr_ref, o_ref, lse_ref, do_ref,
             dq_ref, dk_ref, dv_ref, dcq_ref, dck_ref):
        i = pl.program_id(1)

        @pl.when(i == 0)
        def _():
            dk_ref[...] = jnp.zeros_like(dk_ref)
            dv_ref[...] = jnp.zeros_like(dv_ref)
            dck_ref[...] = jnp.zeros_like(dck_ref)

        diff = lax.broadcasted_iota(jnp.int32, (tk, tq), 0) - lax.broadcasted_iota(jnp.int32, (tk, tq), 1)
        qs = [q_ref[:, lanes(h)] for h in range(hp)]
        dos = [do_ref[:, lanes(h)] for h in range(hp)]
        do_b = [d.astype(BF16) for d in dos]
        cqs = [cr_ref[h, i] for h in range(hp)]
        lses = [lse_ref[h, 0] for h in range(hp)]
        deltas = [jnp.sum(jnp.transpose(dos[h] * o_ref[:, lanes(h)]), axis=0, keepdims=True) for h in range(hp)]

        def products(h, j):
            ks = pl.ds(pl.multiple_of(j * tk, tk), tk)
            nt = (((1,), (1,)), ((), ()))
            return (lax.dot_general(k_ref[ks, lanes(h)], qs[h], nt, preferred_element_type=F32),
                    lax.dot_general(v_ref[ks, lanes(h)], do_b[h], nt, preferred_element_type=F32))

        def tile(h, j, dqt, dcq, s, dp, masked):
            ks = pl.ds(pl.multiple_of(j * tk, tk), tk)
            pr = jnp.exp(s * scale + cqs[h] - cc_ref[h, ks, :] - lses[h])
            if masked:
                pr = jnp.where(diff <= 0, pr, 0.0)
            ds = pr * (dp - deltas[h])
            ds_b = ds.astype(BF16)
            dqt = dqt + jnp.dot(kt_ref[h, j], ds_b, preferred_element_type=F32)
            dk_ref[ks, lanes(h)] += jnp.dot(ds_b, qs[h], preferred_element_type=F32) * scale
            dv_ref[ks, lanes(h)] += jnp.dot(pr.astype(BF16), do_b[h], preferred_element_type=F32)
            dck_ref[h, ks, :] -= jnp.sum(ds, axis=1, keepdims=True)
            return dqt, dcq + jnp.sum(ds, axis=0, keepdims=True)

        def step(j, carry):
            nxt = [products(h, j + 1) for h in range(hp)]
            return tuple(tile(h, j, *carry[h], False) + nxt[h] for h in range(hp))

        init = tuple((jnp.zeros((HEAD_DIM, tq), F32), jnp.zeros((1, tq), F32)) + products(h, 0) for h in range(hp))
        carry = lax.fori_loop(0, i, step, init)
        for h in range(hp):
            dqt, dcq = tile(h, i, *carry[h], True)
            dq_ref[:, lanes(h)] = jnp.transpose(dqt) * scale
            dcq_ref[h, 0] = dcq

    w = hp * HEAD_DIM
    head_all = pl.BlockSpec((t, w), lambda h, i: (0, h))
    qblk = pl.BlockSpec((tq, w), lambda h, i: (i, h))
    colv = pl.BlockSpec((hp, t, 1), lambda h, i: (h, 0, 0))
    rows_all = pl.BlockSpec((hp, t // tk, 1, tk), lambda h, i: (h, 0, 0, 0))
    row_blk = pl.BlockSpec((hp, 1, 1, tq), lambda h, i: (h, i, 0, 0))
    wide = jax.ShapeDtypeStruct((t, nf * HEAD_DIM), F32)
    return pl.pallas_call(
        body, name="fox_bwd", grid=(nf // hp, t // tq),
        in_specs=[qblk, head_all, pl.BlockSpec((hp, t // tk, HEAD_DIM, tk), lambda h, i: (h, 0, 0, 0)), head_all,
                  colv, rows_all, qblk, row_blk, qblk],
        out_specs=[qblk, head_all, head_all, row_blk, colv],
        out_shape=[wide, wide, wide, jax.ShapeDtypeStruct((nf, t // tq, 1, tq), F32),
                   jax.ShapeDtypeStruct((nf, t, 1), F32)],
        compiler_params=_cparams(("parallel", "arbitrary")),
    )(q, k, kt, v, cc, cr, o, lse, dmix)


def _mem_fn(mq, mk, mv, gq, gk):
    qn = _rms_fn(mq, gq)
    kn = _rms_fn(mk, gk)
    s = _nt(qn, kn) * (HEAD_DIM ** -0.5)
    e = jnp.exp(s - lax.stop_gradient(jnp.max(s, axis=1, keepdims=True)))
    pr = e / jnp.sum(e, axis=1, keepdims=True)
    return _nn(pr, mv)


def _mem_specs(t, m, tq, qoff):
    qblk = pl.BlockSpec((tq, HEAD_DIM), lambda h, i: (i, qoff + h))
    kblk = pl.BlockSpec((m, HEAD_DIM), lambda h, i: (0, h))
    vblk = pl.BlockSpec((m, HEAD_DIM), lambda h, i: (0, N_MEM_HEADS + h))
    row = pl.BlockSpec((1, HEAD_DIM), lambda h, i: (0, 0))
    return qblk, kblk, vblk, row


def _mem_fwd(p, qoff, mkv, gq, gk, tq, into, into_off):
    t, m = p.shape[0], mkv.shape[0]
    qblk, kblk, vblk, row = _mem_specs(t, m, tq, qoff)

    def body(q_ref, k_ref, v_ref, gq_ref, gk_ref, _, o_ref):
        o_ref[...] = _mem_fn(q_ref[...], k_ref[...], v_ref[...], gq_ref[...], gk_ref[...]).astype(BF16)

    return pl.pallas_call(
        body, name="mem_fwd", grid=(N_MEM_HEADS, t // tq),
        in_specs=[qblk, kblk, vblk, row, row, pl.BlockSpec(memory_space=pl.ANY)],
        out_specs=pl.BlockSpec((tq, HEAD_DIM), lambda h, i: (i, into_off + h)),
        out_shape=jax.ShapeDtypeStruct(into.shape, BF16), input_output_aliases={5: 0},
        compiler_params=_cparams(("parallel", "parallel")),
    )(p, mkv, mkv, gq, gk, into)


def _mem_bwd(p, qoff, mkv, gq, gk, dmix, dooff, tq):
    t, m = p.shape[0], mkv.shape[0]
    qblk, kblk, vblk, row = _mem_specs(t, m, tq, qoff)

    def body(q_ref, k_ref, v_ref, gq_ref, gk_ref, do_ref, dq_ref, dkv_k_ref, dkv_v_ref, dgq_ref, dgk_ref):
        h, i = pl.program_id(0), pl.program_id(1)

        @pl.when((h == 0) & (i == 0))
        def _():
            dgq_ref[...] = jnp.zeros_like(dgq_ref)
            dgk_ref[...] = jnp.zeros_like(dgk_ref)

        @pl.when(i == 0)
        def _():
            dkv_k_ref[...] = jnp.zeros_like(dkv_k_ref)
            dkv_v_ref[...] = jnp.zeros_like(dkv_v_ref)

        _, vjp = jax.vjp(_mem_fn, q_ref[...], k_ref[...], v_ref[...], gq_ref[...], gk_ref[...])
        dq, dk, dv, dgq, dgk = vjp(do_ref[...])
        dq_ref[...] = dq
        dkv_k_ref[...] += dk
        dkv_v_ref[...] += dv
        dgq_ref[...] += dgq
        dgk_ref[...] += dgk

    oblk = pl.BlockSpec((tq, HEAD_DIM), lambda h, i: (i, h))
    kout = pl.BlockSpec((m, HEAD_DIM), lambda h, i: (0, h))
    half = jax.ShapeDtypeStruct((m, N_MEM_HEADS * HEAD_DIM), F32)
    rshape = jax.ShapeDtypeStruct((1, HEAD_DIM), F32)
    return pl.pallas_call(
        body, name="mem_bwd", grid=(N_MEM_HEADS, t // tq),
        in_specs=[qblk, kblk, vblk, row, row, pl.BlockSpec((tq, HEAD_DIM), lambda h, i: (i, dooff + h))],
        out_specs=[oblk, kout, kout, row, row],
        out_shape=[jax.ShapeDtypeStruct((t, N_MEM_HEADS * HEAD_DIM), F32), half, half, rshape, rshape],
        compiler_params=_cparams(("arbitrary", "arbitrary")),
    )(p, mkv, mkv, gq, gk, dmix)


def _shift_down(x, s):
    if s == 0:
        return x
    r = lax.broadcasted_iota(jnp.int32, x.shape, 0)
    return jnp.where(r >= s, pltpu.roll(x, s, 0), 0.0)


def _shift_up(x, s):
    if s == 0:
        return x
    n = x.shape[0]
    r = lax.broadcasted_iota(jnp.int32, x.shape, 0)
    return jnp.where(r < n - s, pltpu.roll(x, n - s, 0), 0.0)


def _conv_fn(x0, x1, x2, x3, w0, w1, w2, w3, kind):
    y = _silu(x0 * w0 + x1 * w1 + x2 * w2 + x3 * w3)
    if kind == 2:
        return y
    y = y * lax.rsqrt(jnp.sum(y * y, axis=-1, keepdims=True) + NORM_EPS)
    return y * (HEAD_DIM ** -0.5) if kind == 0 else y


def _conv_fwd(p, off, conv_w, ng):
    t = p.shape[0]

    def body(x_ref, w_ref, o_ref):
        kind = pl.program_id(0) // ng
        x = x_ref[...]
        xs = [_shift_down(x, CONV_WIDTH - 1 - j) for j in range(CONV_WIDTH)]
        ws = [w_ref[j:j + 1, :] for j in range(CONV_WIDTH)]
        for kd in range(3):
            @pl.when(kind == kd)
            def _(kd=kd):
                o_ref[...] = _conv_fn(*xs, *ws, kd)

    return pl.pallas_call(
        body, name="gdn_conv_fwd", grid=(3 * ng,),
        in_specs=[pl.BlockSpec((t, HEAD_DIM), lambda c: (0, off + c)),
                  pl.BlockSpec((CONV_WIDTH, HEAD_DIM), lambda c: (0, c))],
        out_specs=pl.BlockSpec((t, HEAD_DIM), lambda c: (0, c)),
        out_shape=jax.ShapeDtypeStruct((t, 3 * ng * HEAD_DIM), F32),
        compiler_params=_cparams(("parallel",)),
    )(p, conv_w)


def _conv_bwd(p, off, conv_w, dys, ng):
    t = p.shape[0]

    def body(x_ref, w_ref, dq_ref, dk_ref, dv_ref, dx_ref, dw_ref):
        kind = pl.program_id(0) // ng
        dy_refs = (dq_ref, dk_ref, dv_ref)
        x = x_ref[...]
        xs = [_shift_down(x, CONV_WIDTH - 1 - j) for j in range(CONV_WIDTH)]
        ws = [w_ref[j:j + 1, :] for j in range(CONV_WIDTH)]
        for kd in range(3):
            @pl.when(kind == kd)
            def _(kd=kd):
                _, vjp = jax.vjp(functools.partial(_conv_fn, kind=kd), *xs, *ws)
                g = vjp(dy_refs[kd][...])
                dx = _shift_up(g[0], CONV_WIDTH - 1)
                for j in range(1, CONV_WIDTH):
                    dx = dx + _shift_up(g[j], CONV_WIDTH - 1 - j)
                dx_ref[...] = dx
                for j in range(CONV_WIDTH):
                    dw_ref[j:j + 1, :] = g[CONV_WIDTH + j]

    blk = pl.BlockSpec((t, HEAD_DIM), lambda c: (0, c))
    head = pl.BlockSpec((t, HEAD_DIM), lambda c: (0, c % ng))
    wblk = pl.BlockSpec((CONV_WIDTH, HEAD_DIM), lambda c: (0, c))
    return pl.pallas_call(
        body, name="gdn_conv_bwd", grid=(3 * ng,),
        in_specs=[pl.BlockSpec((t, HEAD_DIM), lambda c: (0, off + c)), wblk] + [head] * 3,
        out_specs=[blk, wblk],
        out_shape=[jax.ShapeDtypeStruct((t, 3 * ng * HEAD_DIM), F32),
                   jax.ShapeDtypeStruct((CONV_WIDTH, 3 * ng * HEAD_DIM), F32)],
        compiler_params=_cparams(("parallel",)),
    )(p, conv_w, *dys)


def _lower_inverse(lower):
    c = lower.shape[-1]
    r = lax.broadcasted_iota(jnp.int32, (1, c, c), 1)
    e = lax.broadcasted_iota(jnp.int32, (1, c, c), 2)
    hi = lax.Precision.HIGH
    inv = jnp.where(r == e, 1.0, 0.0) - lower
    pw = lower
    for _ in range(int(math.log2(c)) - 1):
        pw = _dot(pw, pw, ((1,), (0,)), hi)
        inv = inv + _dot(inv, pw, ((1,), (0,)), hi)
    return inv


@jax.custom_vjp
def _solve(lower, inv, vb, kbg):
    hi = lax.Precision.HIGH
    return _dot(inv, vb, ((1,), (0,)), hi), _dot(inv, kbg, ((1,), (0,)), hi)


def _solve_fwd(lower, inv, vb, kbg):
    u, w = _solve(lower, inv, vb, kbg)
    return (u, w), (inv, u, w)


def _solve_bwd(res, cts):
    inv, u, w = res
    dvb, dkbg = _tn(inv, cts[0]), _tn(inv, cts[1])
    return -(_nt(dvb, u) + _nt(dkbg, w)), jnp.zeros_like(inv), dvb, dkbg


_solve.defvjp(_solve_fwd, _solve_bwd)


def _wy_fn(q, k, v, gcol, grow, bcol, inv=None):
    b, c, dk = q.shape
    r = lax.broadcasted_iota(jnp.int32, (1, c, c), 1)
    e = lax.broadcasted_iota(jnp.int32, (1, c, c), 2)
    tril, strict = e <= r, e < r
    gc_col = jnp.sum(jnp.where(tril, grow, 0.0), axis=2, keepdims=True)
    gc_row = jnp.sum(jnp.where(r <= e, gcol, 0.0), axis=1, keepdims=True)
    g_last = jnp.sum(gcol, axis=1, keepdims=True)
    decay = jnp.exp(jnp.where(tril, gc_col - gc_row, NEG))
    kb, vb = k * bcol, v * bcol
    lower = jnp.where(strict, _nt(kb, k) * decay, 0.0)
    if inv is None:
        inv = _lower_inverse(lower)
    u, w = _solve(lower, inv, vb, kb * jnp.exp(gc_col))
    attn = jnp.where(tril, _nt(q, k) * decay, 0.0)
    qg = q * jnp.exp(gc_col)
    kdec = k * jnp.exp(g_last - gc_col)
    egl = jnp.broadcast_to(jnp.exp(g_last), (b, 1, dk))
    return u, w, qg, kdec, attn, egl, inv


def _scan_fn(u, w, qg, kdec, attn, egl, state):
    v_new = u - _nn(w, state)
    o = _nn(qg, state) + _nn(attn, v_new)
    return o, state * egl + _tn(kdec, v_new)


GDN_CHUNKS_PER_STEP = 4


def _gdn_fwd(qkv, gcol, grow, bcol, ng):
    t = qkv.shape[0]
    nch = t // CHUNK

    cb = GDN_CHUNKS_PER_STEP
    *wy, inv = _gdn_wy(qkv, gcol, grow, bcol, ng, cb)

    def body(u_ref, w_ref, qg_ref, kd_ref, at_ref, eg_ref, o_ref, st_ref, state):
        @pl.when(pl.program_id(0) == 0)
        def _():
            state[...] = jnp.zeros_like(state)

        st_ref[:, 0] = state[...]
        heads = lambda ref: jnp.stack([ref[:, h * HEAD_DIM:(h + 1) * HEAD_DIM] for h in range(ng)])
        o, new = _scan_fn(heads(u_ref), heads(w_ref), heads(qg_ref), heads(kd_ref), at_ref[:, 0], eg_ref[:, 0],
                          state[...])
        for h in range(ng):
            o_ref[:, h * HEAD_DIM:(h + 1) * HEAD_DIM] = o[h]
        state[...] = new

    w = ng * HEAD_DIM
    blk = pl.BlockSpec((CHUNK, w), lambda i: (i, 0))
    o, states = pl.pallas_call(
        body, name="gdn_scan_fwd", grid=(nch,),
        in_specs=[blk, blk, blk, blk, pl.BlockSpec((ng, 1, CHUNK, CHUNK), lambda i: (0, i, 0, 0)),
                  pl.BlockSpec((ng, 1, 1, HEAD_DIM), lambda i: (0, i, 0, 0))],
        out_specs=[blk, pl.BlockSpec((ng, 1, HEAD_DIM, HEAD_DIM), lambda i: (0, i, 0, 0))],
        out_shape=[jax.ShapeDtypeStruct((t, w), F32),
                   jax.ShapeDtypeStruct((ng, nch, HEAD_DIM, HEAD_DIM), F32)],
        scratch_shapes=[pltpu.VMEM((ng, HEAD_DIM, HEAD_DIM), F32)],
        compiler_params=_cparams(("arbitrary",)),
    )(*wy)
    return o, (wy, inv, states)


def _wy_batch(q_ref, k_ref, v_ref, gc_ref, gr_ref, bc_ref, ng, cb):
    idx = [(c, h) for c in range(cb) for h in range(ng)]
    rows = lambda c: slice(c * CHUNK, (c + 1) * CHUNK)
    lanes = lambda h: slice(h * HEAD_DIM, (h + 1) * HEAD_DIM)
    wide = lambda ref: jnp.stack([ref[rows(c), lanes(h)] for c, h in idx])
    col = lambda ref: jnp.stack([ref[h, rows(c), :] for c, h in idx])
    return idx, (wide(q_ref), wide(k_ref), wide(v_ref), col(gc_ref), jnp.stack([gr_ref[h, c] for c, h in idx]),
                 col(bc_ref))


def _gdn_wy(qkv, gcol, grow, bcol, ng, cb):
    t = qkv.shape[0]
    nch = t // CHUNK

    def body(q_ref, k_ref, v_ref, gc_ref, gr_ref, bc_ref, u_ref, w_ref, qg_ref, kd_ref, at_ref, eg_ref, inv_ref):
        idx, args = _wy_batch(q_ref, k_ref, v_ref, gc_ref, gr_ref, bc_ref, ng, cb)
        u, w, qg, kd, at, eg, inv = _wy_fn(*args)
        for b, (c, h) in enumerate(idx):
            rows, lanes = slice(c * CHUNK, (c + 1) * CHUNK), slice(h * HEAD_DIM, (h + 1) * HEAD_DIM)
            u_ref[rows, lanes] = u[b]
            w_ref[rows, lanes] = w[b]
            qg_ref[rows, lanes] = qg[b]
            kd_ref[rows, lanes] = kd[b]
            at_ref[h, c] = at[b]
            eg_ref[h, c] = eg[b]
            inv_ref[h, c] = inv[b]

    wd = ng * HEAD_DIM
    blk = lambda o: pl.BlockSpec((cb * CHUNK, wd), lambda i: (i, o))
    col = pl.BlockSpec((ng, cb * CHUNK, 1), lambda i: (0, i, 0))
    sq = pl.BlockSpec((ng, cb, CHUNK, CHUNK), lambda i: (0, i, 0, 0))
    wide = jax.ShapeDtypeStruct((t, wd), F32)
    sq_shape = jax.ShapeDtypeStruct((ng, nch, CHUNK, CHUNK), F32)
    return pl.pallas_call(
        body, name="gdn_wy_fwd", grid=(nch // cb,),
        in_specs=[blk(0), blk(1), blk(2), col, pl.BlockSpec((ng, cb, 1, CHUNK), lambda i: (0, i, 0, 0)), col],
        out_specs=[blk(0), blk(0), blk(0), blk(0), sq, pl.BlockSpec((ng, cb, 1, HEAD_DIM), lambda i: (0, i, 0, 0)),
                   sq],
        out_shape=[wide, wide, wide, wide, sq_shape, jax.ShapeDtypeStruct((ng, nch, 1, HEAD_DIM), F32), sq_shape],
        compiler_params=_cparams(("parallel",)),
    )(qkv, qkv, qkv, gcol, grow, bcol)


def _gdn_bwd(qkv, gcol, grow, bcol, saved, do, ng):
    t = qkv.shape[0]
    nch = t // CHUNK
    cb = GDN_CHUNKS_PER_STEP // 2
    wy, inv, states = saved
    wd = ng * HEAD_DIM

    def scan_body(u_ref, w_ref, qg_ref, kd_ref, at_ref, eg_ref, st_ref, do_ref,
                  du_ref, dw_ref, dqg_ref, dkd_ref, dat_ref, deg_ref, dstate):
        @pl.when(pl.program_id(0) == 0)
        def _():
            dstate[...] = jnp.zeros_like(dstate)

        heads = lambda ref: jnp.stack([ref[:, h * HEAD_DIM:(h + 1) * HEAD_DIM] for h in range(ng)])
        _, vjp = jax.vjp(_scan_fn, heads(u_ref), heads(w_ref), heads(qg_ref), heads(kd_ref), at_ref[:, 0],
                         eg_ref[:, 0], st_ref[:, 0])
        du, dw, dqg, dkd, dat, deg, dst = vjp((heads(do_ref), dstate[...]))
        for h in range(ng):
            lanes = slice(h * HEAD_DIM, (h + 1) * HEAD_DIM)
            du_ref[:, lanes] = du[h]
            dw_ref[:, lanes] = dw[h]
            dqg_ref[:, lanes] = dqg[h]
            dkd_ref[:, lanes] = dkd[h]
        dat_ref[:, 0] = dat
        deg_ref[:, 0] = deg
        dstate[...] = dst

    rev = lambda i: nch - 1 - i
    blk = pl.BlockSpec((CHUNK, wd), lambda i: (rev(i), 0))
    atb = pl.BlockSpec((ng, 1, CHUNK, CHUNK), lambda i: (0, rev(i), 0, 0))
    egb = pl.BlockSpec((ng, 1, 1, HEAD_DIM), lambda i: (0, rev(i), 0, 0))
    wide = jax.ShapeDtypeStruct((t, wd), F32)
    at_shape = jax.ShapeDtypeStruct((ng, nch, CHUNK, CHUNK), F32)
    eg_shape = jax.ShapeDtypeStruct((ng, nch, 1, HEAD_DIM), F32)
    dwy = pl.pallas_call(
        scan_body, name="gdn_scan_bwd", grid=(nch,),
        in_specs=[blk, blk, blk, blk, atb, egb,
                  pl.BlockSpec((ng, 1, HEAD_DIM, HEAD_DIM), lambda i: (0, rev(i), 0, 0)), blk],
        out_specs=[blk, blk, blk, blk, atb, egb],
        out_shape=[wide, wide, wide, wide, at_shape, eg_shape],
        scratch_shapes=[pltpu.VMEM((ng, HEAD_DIM, HEAD_DIM), F32)],
        compiler_params=_cparams(("arbitrary",)),
    )(*wy, states, do)

    def wy_body(q_ref, k_ref, v_ref, gc_ref, gr_ref, bc_ref, du_ref, dw_ref, dqg_ref, dkd_ref, dat_ref, deg_ref,
                inv_ref, dq_ref, dk_ref, dv_ref, dgc_ref, dgr_ref, dbc_ref):
        idx, args = _wy_batch(q_ref, k_ref, v_ref, gc_ref, gr_ref, bc_ref, ng, cb)
        kept = jnp.stack([inv_ref[h, c] for c, h in idx])
        rows = lambda c: slice(c * CHUNK, (c + 1) * CHUNK)
        lanes = lambda h: slice(h * HEAD_DIM, (h + 1) * HEAD_DIM)
        wide_ct = lambda ref: jnp.stack([ref[rows(c), lanes(h)] for c, h in idx])
        cts = (wide_ct(du_ref), wide_ct(dw_ref), wide_ct(dqg_ref), wide_ct(dkd_ref),
               jnp.stack([dat_ref[h, c] for c, h in idx]), jnp.stack([deg_ref[h, c] for c, h in idx]))
        _, vjp = jax.vjp(lambda *a: _wy_fn(*a, inv=kept)[:6], *args)
        dq, dk, dv, dgc, dgr, dbc = vjp(cts)
        for b, (c, h) in enumerate(idx):
            dq_ref[rows(c), lanes(h)] = dq[b]
            dk_ref[rows(c), lanes(h)] = dk[b]
            dv_ref[rows(c), lanes(h)] = dv[b]
            dgc_ref[h, rows(c), :] = dgc[b]
            dgr_ref[h, c] = dgr[b]
            dbc_ref[h, rows(c), :] = dbc[b]

    cblk = lambda o: pl.BlockSpec((cb * CHUNK, wd), lambda i: (i, o))
    col = pl.BlockSpec((ng, cb * CHUNK, 1), lambda i: (0, i, 0))
    rowv = pl.BlockSpec((ng, cb, 1, CHUNK), lambda i: (0, i, 0, 0))
    cshape = jax.ShapeDtypeStruct((ng, t, 1), F32)
    return pl.pallas_call(
        wy_body, name="gdn_wy_bwd", grid=(nch // cb,),
        in_specs=[cblk(0), cblk(1), cblk(2), col, rowv, col, cblk(0), cblk(0), cblk(0), cblk(0),
                  pl.BlockSpec((ng, cb, CHUNK, CHUNK), lambda i: (0, i, 0, 0)),
                  pl.BlockSpec((ng, cb, 1, HEAD_DIM), lambda i: (0, i, 0, 0)),
                  pl.BlockSpec((ng, cb, CHUNK, CHUNK), lambda i: (0, i, 0, 0))],
        out_specs=[cblk(0), cblk(0), cblk(0), col, rowv, col],
        out_shape=[wide, wide, wide, cshape, jax.ShapeDtypeStruct((ng, nch, 1, CHUNK), F32), cshape],
        compiler_params=_cparams(("parallel",)),
    )(qkv, qkv, qkv, gcol, grow, bcol, *dwy, inv)


def _swiglu_fn(gate, up):
    return _silu(gate) * up


FFN_TN = 256


def _ffn_up(n2, wgu4):
    _, d, w = wgu4.shape
    t = n2.shape[0]
    tn = _tile(w, FFN_TN)
    nb = w // tn

    def body(a_ref, b_ref, gu_ref, act_ref):
        av = a_ref[...]
        gate = jnp.dot(av, b_ref[0], preferred_element_type=F32)
        up = jnp.dot(av, b_ref[1], preferred_element_type=F32)
        gu_ref[0] = gate.astype(BF16)
        gu_ref[1] = up.astype(BF16)
        act_ref[...] = _swiglu_fn(gate, up).astype(BF16)

    return pl.pallas_call(
        body, name="ffn_up", grid=(2, nb),
        in_specs=[pl.BlockSpec((t, d), lambda j, l: (0, 0)), pl.BlockSpec((2, d, tn), lambda j, l: (j, 0, l))],
        out_specs=[pl.BlockSpec((2, t, tn), lambda j, l: (j, 0, l)),
                   pl.BlockSpec((t, tn), lambda j, l: (0, j * nb + l))],
        out_shape=[jax.ShapeDtypeStruct((4, t, w), BF16), jax.ShapeDtypeStruct((t, 2 * w), BF16)],
        compiler_params=_cparams(("parallel", "parallel")),
    )(n2, wgu4)


def _ffn_dact(dh2, wd, gu, after):
    _, t, w = gu.shape
    d = dh2.shape[1]
    tn = _tile(w, FFN_TN)
    nb = w // tn

    def body(a_ref, b_ref, gu_ref, _, o_ref):
        dact = lax.dot_general(a_ref[...], b_ref[...], (((1,), (1,)), ((), ())), preferred_element_type=F32)
        _, vjp = jax.vjp(_swiglu_fn, gu_ref[0].astype(F32), gu_ref[1].astype(F32))
        dg, du = vjp(dact)
        o_ref[0] = dg.astype(BF16)
        o_ref[1] = du.astype(BF16)

    pair = pl.BlockSpec((2, t, tn), lambda j, l: (j, 0, l))
    return pl.pallas_call(
        body, name="ffn_dact", grid=(2, nb),
        in_specs=[pl.BlockSpec((t, d), lambda j, l: (0, 0)), pl.BlockSpec((tn, d), lambda j, l: (j * nb + l, 0)),
                  pair, pl.BlockSpec(after.shape, lambda j, l: (0, 0))],
        out_specs=pair, out_shape=jax.ShapeDtypeStruct(gu.shape, BF16),
        compiler_params=_cparams(("parallel", "parallel")),
    )(dh2, wd, gu, after)


def _loss_head(h2, target):
    t, d = h2.shape
    tr = _tile(t, 256, 8)

    def body(h_ref, t_ref, l_ref, d_ref, db_ref):
        @pl.when(pl.program_id(0) == 0)
        def _():
            l_ref[...] = jnp.zeros_like(l_ref)

        err = h_ref[...] - t_ref[...]
        d_ref[...] = err * (1.0 / d)
        db_ref[...] = (err * (1.0 / d)).astype(BF16)
        part = 0.5 * jnp.sum(jnp.mean(err * err, axis=-1, keepdims=True), axis=0, keepdims=True)
        lane = lax.broadcasted_iota(jnp.int32, (8, HEAD_DIM), 1)
        row = lax.broadcasted_iota(jnp.int32, (8, HEAD_DIM), 0)
        l_ref[...] += jnp.where((lane == 0) & (row == 0), part, 0.0)

    blk = pl.BlockSpec((tr, d), lambda r: (r, 0))
    return pl.pallas_call(
        body, name="loss_head", grid=(t // tr,), in_specs=[blk, blk],
        out_specs=[pl.BlockSpec((8, HEAD_DIM), lambda r: (0, 0)), blk, blk],
        out_shape=[jax.ShapeDtypeStruct((8, HEAD_DIM), F32), jax.ShapeDtypeStruct((t, d), F32),
                   jax.ShapeDtypeStruct((t, d), BF16)],
        compiler_params=_cparams(("arbitrary",)),
    )(h2, target)


def _adamw(w, g, m, v, *, g_fn=None, name):
    r, c = w.shape
    tr = _tile(r, max(8, (1 << 19) // c // 8 * 8), 8)
    gs = g if isinstance(g, tuple) else (g,)

    def body(w_ref, *refs):
        g_refs, (m_ref, v_ref, go_ref, d_ref, mo_ref, vo_ref) = refs[:len(gs)], refs[len(gs):]
        gr = g_refs[0][...] if g_fn is None else g_fn(*[ref[...] for ref in g_refs])
        mn = ADAM_B1 * m_ref[...] + (1.0 - ADAM_B1) * gr
        vn = ADAM_B2 * v_ref[...] + (1.0 - ADAM_B2) * (gr * gr)
        m_hat = mn / (1.0 - ADAM_B1 ** ADAM_STEP)
        v_hat = vn / (1.0 - ADAM_B2 ** ADAM_STEP)
        go_ref[...] = gr
        d_ref[...] = -ADAM_LR * (m_hat / (jnp.sqrt(v_hat) + ADAM_EPS) + ADAM_WD * w_ref[...])
        mo_ref[...] = mn
        vo_ref[...] = vn

    blk = pl.BlockSpec((tr, c), lambda i: (i, 0))
    gblks = [pl.BlockSpec((tr, gi.shape[1]), lambda i: (i, 0)) for gi in gs]
    return pl.pallas_call(
        body, name=name, grid=(r // tr,), in_specs=[blk] + gblks + [blk, blk], out_specs=[blk] * 4,
        out_shape=[jax.ShapeDtypeStruct((r, c), F32)] * 4,
        compiler_params=_cparams(("parallel",)),
    )(w, *gs, m, v)


class _Layout:
    def __init__(self, d):
        nh = d // HEAD_DIM
        self.nm = N_MEM_HEADS
        self.nf = (nh - self.nm) // 2
        self.ng = nh - self.nm - self.nf
        nf, ng, nm, hd = self.nf, self.ng, self.nm, HEAD_DIM
        self.o_fq, self.o_fk, self.o_fv, self.o_sm = 0, nf, 2 * nf, 3 * nf
        self.o_gq, self.o_gz, self.o_mq = 0, 3 * ng, 4 * ng
        self.cols_a = -(-(3 * nf + 1) // 4) * 4 * hd
        self.cols_b = -(-(4 * ng + nm) // 4) * 4 * hd
        self.cols = self.cols_a + self.cols_b
        sizes = [nf * hd, nf * hd, nf * hd, nf, 3 * ng * hd, ng * hd, ng, ng, nm * hd]
        starts = [sum(sizes[:i]) for i in range(len(sizes))]
        self.ref = list(zip(starts, sizes))
        self.in_cols = sum(sizes)

    def regroup(self, w):
        part = lambda i: w[:, self.ref[i][0]:self.ref[i][0] + self.ref[i][1]]
        a = [part(0), part(1), part(2), part(3), part(6), part(7)]
        b = [part(4), part(5), part(8)]
        pads = [self.cols_a - sum(p.shape[1] for p in a), self.cols_b - sum(p.shape[1] for p in b)]
        fill = [[jnp.zeros((w.shape[0], n), w.dtype)] if n else [] for n in pads]
        return jnp.concatenate(a + fill[0] + b + fill[1], axis=1)

    def ungroup(self, g):
        hd, nf, ng, nm = HEAD_DIM, self.nf, self.ng, self.nm
        sm, b0 = self.o_sm * hd, self.cols_a
        return jnp.concatenate([
            g[:, :3 * nf * hd], g[:, sm:sm + nf], g[:, b0:b0 + 3 * ng * hd],
            g[:, b0 + self.o_gz * hd:b0 + self.o_mq * hd], g[:, sm + nf:sm + nf + ng],
            g[:, sm + nf + ng:sm + nf + 2 * ng], g[:, b0 + self.o_mq * hd:b0 + (self.o_mq + nm) * hd]], axis=1)


def _lane_row(pieces):
    row = jnp.zeros((1, HEAD_DIM), F32)
    for off, a in pieces:
        row = lax.dynamic_update_slice(row, a.astype(F32), (0, off))
    return row


def _local_step(x, mem, target, prefetch, weights, reducer, sp):
    t, d = x.shape
    lay = _Layout(d)
    nf, ng, nm, hd = lay.nf, lay.ng, lay.nm, HEAD_DIM
    nch = t // CHUNK
    tq = _tile(t, 256)
    tk = tq

    u = _norm_fwd(x, 0, sp["norm_mix"], 1, d, BF16, name="norm_mix_fwd")
    prefetch("in_a", u)
    (win_a,) = weights("in_a", u)
    p_a = _mm(u, win_a, name="mm_in_a")
    pa = _lane_row([(nf, sp["gdn_a_log"])])
    pb = _lane_row([(0, sp["fox_f_bias"]), (nf, sp["gdn_dt_bias"])])
    vals, csum = _small_fwd(p_a, lay.o_sm, pa, pb, nf, ng)

    c_t = csum[:, :nf].T
    cc, cr = c_t.reshape(nf, t, 1), c_t.reshape(nf, t // tk, 1, tk)
    fq = _norm_fwd(p_a, lay.o_fq, sp["fox_q_norm"], nf, hd, BF16, name="fox_qnorm_fwd")
    fk = _norm_fwd(p_a, lay.o_fk, sp["fox_k_norm"], nf, hd, BF16, name="fox_knorm_fwd")
    fv = p_a[:, lay.o_fv * hd:(lay.o_fv + nf) * hd].astype(BF16)
    o_fox, lse, mix = _fox_fwd(fq, fk, fv, cc, cr, nf, tq, tk, d)

    prefetch("in_b", lse)
    (win_b,) = weights("in_b", lse)
    prefetch("mixer", win_b)
    p = _mm(u, win_b, name="mm_in_b")
    wmkv, conv_taps = weights("mixer", p)
    sp = dict(sp, gdn_conv=conv_taps)
    qkv = _conv_fwd(p, lay.o_gq, sp["gdn_conv"], ng)
    g_t, b_t = vals[:, nf:nf + ng].T, vals[:, nf + ng:nf + 2 * ng].T
    gcol, grow, bcol = g_t.reshape(ng, t, 1), g_t.reshape(ng, nch, 1, CHUNK), b_t.reshape(ng, t, 1)
    o_g, states = _gdn_fwd(qkv, gcol, grow, bcol, ng)
    mix = _norm_fwd(o_g, 0, sp["gdn_out_norm"], ng, hd, BF16, z=p, zoff=lay.o_gz, into=mix, into_off=nf,
                    name="gdn_out_fwd")
    prefetch("out", mix)

    mem_n = _norm_fwd(mem, 0, sp["mem_norm"], 1, d, BF16, name="mem_norm_fwd")
    mkv = _mm(mem_n, wmkv, name="mm_memkv")
    mix = _mem_fwd(p, lay.o_mq, mkv, sp["mem_q_norm"], sp["mem_k_norm"], tq, mix, nf + ng)
    prefetch("gate_up", mix)
    (wout,) = weights("out", mix)
    h1 = _mm(mix, wout, res=x, name="mm_out")
    n2 = _norm_fwd(h1, 0, sp["norm_ffn"], 1, d, BF16, name="norm_ffn_fwd")
    (wgu,) = weights("gate_up", n2)
    wgu4 = wgu.reshape(4, d, -1)
    gu, act = _ffn_up(n2, wgu4)
    prefetch("down", act)
    (wd,) = weights("down", act)
    h2 = _mm(act, wd, res=h1, name="mm_down")
    loss_blk, dh2, dh2_b = _loss_head(h2, target)

    g = {}
    token = reducer.pair("w_down", _mm(act, dh2_b, ta=True, out_dtype=BF16, name="mm_dw_down"))
    dgu = _ffn_dact(dh2_b, wd, gu, token)
    dw_gate_up = _mm(n2, dgu, ta=True, stack="out", out_dtype=BF16, name="mm_dw_gate_up").reshape(wgu.shape)
    token = reducer.pair("w_gate_up", dw_gate_up)
    dn2 = _mm(dgu, wgu4, tb=True, stack="sum", after=token, name="mm_dn2")
    token = reducer.ship("ffn", ["w_down", "w_gate_up"], dn2)
    dh1, g["norm_ffn"] = _norm_bwd(h1, 0, sp["norm_ffn"] + token[0, 0], dn2, 0, 1, d, res=dh2,
                                   name="norm_ffn_bwd")
    token = reducer.pair("w_out", _mm(mix, dh1, ta=True, out_dtype=BF16, name="mm_dw_out"))
    dmix = _mm(dh1, wout, tb=True, after=token, name="mm_dmix")

    dmq, dmk, dmv, g["mem_q_norm"], g["mem_k_norm"] = _mem_bwd(
        p, lay.o_mq, mkv, sp["mem_q_norm"], sp["mem_k_norm"], dmix, nf + ng, tq)
    dmkv = jnp.concatenate([dmk, dmv], axis=1)
    token = reducer.pair("w_mem_kv", _mm(mem_n, dmkv, ta=True, out_dtype=BF16, name="mm_dw_memkv"))
    dmem_n = _mm(dmkv, wmkv, tb=True, after=token, name="mm_dmem")
    token = reducer.ship("mix", ["w_out", "w_mem_kv"], dmem_n)
    _, g["mem_norm"] = _norm_bwd(mem, 0, sp["mem_norm"], dmem_n, 0, 1, d, name="mem_norm_bwd")

    do_g, dgz, g["gdn_out_norm"] = _norm_bwd(o_g, 0, sp["gdn_out_norm"] + token[0, 0], dmix, nf, ng, hd, z=p,
                                             zoff=lay.o_gz, name="gdn_out_bwd")
    dq, dk, dv, dgc, dgr, dbc = _gdn_bwd(qkv, gcol, grow, bcol, states, do_g, ng)
    dgqkv, g["gdn_conv"] = _conv_bwd(p, lay.o_gq, sp["gdn_conv"], (dq, dk, dv), ng)
    dg_t = dgc.reshape(ng, t) + dgr.reshape(ng, t)
    db_t = dbc.reshape(ng, t)

    dfq_n, dfk_n, dfv, dcc, dcr = _fox_bwd(fq, fk, fv, cc, cr, o_fox, lse, dmix, nf, tq, tk)
    dfq, g["fox_q_norm"] = _norm_bwd(p_a, lay.o_fq, sp["fox_q_norm"], dfq_n, 0, nf, hd, name="fox_qnorm_bwd")
    dfk, g["fox_k_norm"] = _norm_bwd(p_a, lay.o_fk, sp["fox_k_norm"], dfk_n, 0, nf, hd, name="fox_knorm_bwd")
    dc_t = dcc.reshape(nf, t) + dcr.reshape(nf, t)

    lanes_left = hd - nf - 2 * ng
    dvals = jnp.concatenate([jnp.zeros((t, nf), F32), dg_t.T, db_t.T, jnp.zeros((t, lanes_left), F32)], axis=1)
    dcsum = jnp.concatenate([dc_t.T, jnp.zeros((t, hd - nf), F32)], axis=1)
    dsm, dpa, dpb = _small_bwd(p_a, lay.o_sm, pa, pb, dvals, dcsum, nf, ng)
    g["fox_f_bias"] = dpb[:, :nf]
    g["gdn_dt_bias"] = dpb[:, nf:nf + ng]
    g["gdn_a_log"] = dpa[:, nf:nf + ng]

    zeros = lambda n: jnp.zeros((t, n), F32)
    dp_a = jnp.concatenate([dfq, dfk, dfv, dsm, zeros(lay.cols_a - (lay.o_sm + 1) * hd)], axis=1).astype(BF16)
    dp_b = jnp.concatenate([dgqkv, dgz, dmq, zeros(lay.cols_b - (lay.o_mq + nm) * hd)], axis=1).astype(BF16)
    token = reducer.start("in", {"w_in_a": _mm(u, dp_a, ta=True, out_dtype=BF16, name="mm_dw_in_a"),
                                 "w_in_b": _mm(u, dp_b, ta=True, out_dtype=BF16, name="mm_dw_in_b")})
    du = _mm(dp_a, win_a, tb=True, after=token, name="mm_du_a")
    du = _mm(dp_b, win_b, tb=True, res=du, name="mm_du_b")
    dx, g["norm_mix"] = _norm_bwd(x, 0, sp["norm_mix"], du, 0, 1, d, res=dh1, name="norm_mix_bwd")
    return loss_blk, dx, g


ANY = pl.BlockSpec(memory_space=pl.ANY)


def _me():
    x, y, c = lax.axis_index("x"), lax.axis_index("y"), lax.axis_index("c")
    chips = [(1 - x, y), (x, 1 - y), (1 - x, 1 - y)]
    return x, y, c, chips


def _slot(axis, k):
    return k if axis == 0 else 2 * (k % 2) + k // 2


def _slab(ref, axis, rows, cols, k, h):
    half = rows // 2
    return ref.at[pl.ds(_slot(axis, k) * rows + h * half, half), :]


def _remote(src, dst, send_sem, recv_sem, dev):
    return pltpu.make_async_remote_copy(src_ref=src, dst_ref=dst, send_sem=send_sem, recv_sem=recv_sem,
                                        device_id=dev, device_id_type=MESH)


HBM = pl.BlockSpec(memory_space=pltpu.HBM)
SEM = pl.BlockSpec(memory_space=pltpu.SEMAPHORE)
SPLIT = pltpu.CompilerParams(has_side_effects=pltpu.SideEffectType.DATAFLOW_SIDE_EFFECTING)
TOKEN = jax.ShapeDtypeStruct((8, HEAD_DIM), F32)


def _in_hbm(v):
    return pltpu.with_memory_space_constraint(v, pltpu.HBM)


def _cast_place(shard, axis, name, col_fn=None, out_cols=None):
    r, c = shard.shape
    oc = out_cols or c
    tr = _tile(r, 512 if col_fn is None else 64, 16)
    tc = _tile(c, 2048) if col_fn is None else c
    otc = tc if col_fn is None else oc
    nb = r // tr
    chip = 2 * lax.axis_index("x") + lax.axis_index("y")
    slot = jnp.reshape(_slot(axis, chip), (1,)).astype(jnp.int32)

    def body(slot_ref, x_ref, o_ref):
        x = x_ref[...]
        o_ref[...] = (x if col_fn is None else col_fn(x)).astype(BF16)

    return pl.pallas_call(
        body, name=name,
        grid_spec=pltpu.PrefetchScalarGridSpec(
            num_scalar_prefetch=1, grid=(nb, c // tc),
            in_specs=[pl.BlockSpec((tr, tc), lambda i, l, s: (i, l))],
            out_specs=pl.BlockSpec((tr, otc), lambda i, l, s: (s[0] * nb + i, l))),
        out_shape=jax.ShapeDtypeStruct((4 * r, oc), BF16),
        compiler_params=_cparams(("parallel", "parallel")),
    )(slot, shard)


def _gather_start(bufs, axes, shapes, groups, name):
    n = len(bufs)

    def body(*refs):
        dst = refs[n:2 * n]
        sems = refs[2 * n:2 * n + 2 * len(groups)]
        token = refs[-1]
        x, y, c, chips = _me()
        k = 2 * x + y
        for gi, ws in enumerate(groups):
            for i, w in enumerate(ws):
                r, cl = shapes[w]
                place = _slab(dst[w], axes[w], r, cl, k, c)
                for j, (px, py) in enumerate(chips):
                    _remote(place, place, sems[2 * gi].at[3 * i + j], sems[2 * gi + 1].at[3 * i + j],
                            (px, py, c)).start()
        token[...] = jnp.zeros_like(token)

    sem_shapes = [pltpu.SemaphoreType.DMA((3 * len(ws),)) for ws in groups for _ in range(2)]
    outs = pl.pallas_call(
        body, name=name, in_specs=[HBM] * n,
        out_specs=[HBM] * n + [SEM] * len(sem_shapes) + [pl.BlockSpec(memory_space=pltpu.VMEM)],
        out_shape=[pltpu.HBM(b.shape, b.dtype) for b in bufs] + sem_shapes + [TOKEN],
        input_output_aliases={w: w for w in range(n)}, compiler_params=SPLIT,
    )(*[_in_hbm(b) for b in bufs])
    sems = outs[n:-1]
    return outs[:n], [(sems[2 * g], sems[2 * g + 1]) for g in range(len(groups))], outs[-1]


def _gather_wait(bufs, axes, shapes, sems, after, name):
    n = len(bufs)

    def body(*refs):
        send_sems, recv_sems = refs[n], refs[n + 1]
        dst = refs[n + 3:]
        x, y, c, chips = _me()
        k = 2 * x + y
        for i in range(n):
            r, cl = shapes[i]
            for j, (px, py) in enumerate(chips):
                got = _slab(dst[i], axes[i], r, cl, 2 * px + py, c)
                _remote(got, got, send_sems.at[3 * i + j], recv_sems.at[3 * i + j], (px, py, c)).wait_recv()
        for i in range(n):
            r, cl = shapes[i]
            mine = _slab(dst[i], axes[i], r, cl, k, c)
            for j, (px, py) in enumerate(chips):
                _remote(mine, mine, send_sems.at[3 * i + j], recv_sems.at[3 * i + j], (px, py, c)).wait_send()

    return pl.pallas_call(
        body, name=name, in_specs=[HBM] * n + [SEM, SEM, ANY], out_specs=[HBM] * n,
        out_shape=[pltpu.HBM(b.shape, b.dtype) for b in bufs],
        input_output_aliases={i: i for i in range(n)}, compiler_params=SPLIT,
    )(*bufs, sems[0], sems[1], after)


def _gather_forward(bufs, axes, shapes, name):
    n = len(bufs)

    def body(*refs):
        dst = refs[n:2 * n]
        send_sems, recv_sems = refs[2 * n:]
        x, y, c, chips = _me()
        sibling = (x, y, 1 - c)
        sends = []
        for i in range(n):
            r, cl = shapes[i]
            for j, (px, py) in enumerate(chips):
                got = _slab(dst[i], axes[i], r, cl, 2 * px + py, c)
                cp = _remote(got, got, send_sems.at[3 * i + j], recv_sems.at[3 * i + j], sibling)
                cp.start()
                sends.append(cp)
        for i in range(n):
            r, cl = shapes[i]
            for j, (px, py) in enumerate(chips):
                got = _slab(dst[i], axes[i], r, cl, 2 * px + py, 1 - c)
                _remote(got, got, send_sems.at[3 * i + j], recv_sems.at[3 * i + j], sibling).wait_recv()
        for cp in sends:
            cp.wait_send()

    return pl.pallas_call(
        body, name=name, in_specs=[ANY] * n, out_specs=[ANY] * n,
        out_shape=[jax.ShapeDtypeStruct(b.shape, b.dtype) for b in bufs],
        input_output_aliases={i: i for i in range(n)},
        scratch_shapes=[pltpu.SemaphoreType.DMA((3 * n,)), pltpu.SemaphoreType.DMA((3 * n,))],
    )(*bufs)


def _split_start(name, arrays, geometry, count):
    n = len(arrays)

    def body(*refs):
        send, recv, token = refs[2 * n:]
        for i, (src, dst, _, dev) in enumerate(geometry(refs[n:2 * n])):
            _remote(src, dst, send.at[i], recv.at[i], dev).start()
        token[...] = jnp.zeros_like(token)

    sem = pltpu.SemaphoreType.DMA((count,))
    outs = pl.pallas_call(
        body, name=name, in_specs=[HBM] * n,
        out_specs=[HBM] * n + [SEM, SEM, pl.BlockSpec(memory_space=pltpu.VMEM)],
        out_shape=[pltpu.HBM(v.shape, v.dtype) for v in arrays] + [sem, sem, TOKEN],
        input_output_aliases={i: i for i in range(n)}, compiler_params=SPLIT,
    )(*[_in_hbm(v) for v in arrays])
    return list(outs[:n]), (outs[n], outs[n + 1]), outs[-1]


def _split_wait(name, arrays, sems, after, geometry):
    n = len(arrays)

    def body(*refs):
        send, recv = refs[n], refs[n + 1]
        copies = geometry(refs[n + 3:])
        for i, (_, _, land, dev) in enumerate(copies):
            _remote(land, land, send.at[i], recv.at[i], dev).wait_recv()
        for i, (src, _, _, dev) in enumerate(copies):
            _remote(src, src, send.at[i], recv.at[i], dev).wait_send()

    return list(pl.pallas_call(
        body, name=name, in_specs=[HBM] * n + [SEM, SEM, ANY], out_specs=[HBM] * n,
        out_shape=[pltpu.HBM(v.shape, v.dtype) for v in arrays],
        input_output_aliases={i: i for i in range(n)}, compiler_params=SPLIT,
    )(*arrays, sems[0], sems[1], after))


def _forward_geometry(axes, shapes):
    def geometry(bufs):
        x, y, c, chips = _me()
        out = []
        for i, buf in enumerate(bufs):
            r, cl = shapes[i]
            for px, py in chips:
                got = _slab(buf, axes[i], r, cl, 2 * px + py, c)
                out.append((got, got, _slab(buf, axes[i], r, cl, 2 * px + py, 1 - c), (x, y, 1 - c)))
        return out
    return geometry


def _pair_geometry(axes, shapes):
    def geometry(refs):
        n = len(refs) // 2
        x, y, c, _ = _me()
        out = []
        for w in range(n):
            r, cl = shapes[w]
            for j in range(4):
                land = refs[n + w].at[j]
                out.append((_slab(refs[w], axes[w], r, cl, j, 1 - c), land, land, (x, y, 1 - c)))
        return out
    return geometry


def _pair_exchange(fulls, axes, shapes, tag):
    n = len(fulls)

    def body(*refs):
        src, dst = refs[:n], refs[n:2 * n]
        send_sems, recv_sems = refs[2 * n:]
        x, y, c, _ = _me()
        sibling = (x, y, 1 - c)
        cps = []
        for w in range(n):
            r, cl = shapes[w]
            for j in range(4):
                cp = _remote(_slab(src[w], axes[w], r, cl, j, 1 - c), dst[w].at[j],
                             send_sems.at[4 * w + j], recv_sems.at[4 * w + j], sibling)
                cp.start()
                cps.append(cp)
        for cp in cps:
            cp.wait()

    out_shape = [jax.ShapeDtypeStruct((4, r // 2, cl), f.dtype) for (r, cl), f in zip(shapes, fulls)]
    return pl.pallas_call(
        body, name="reduce_pair_exchange_" + tag, in_specs=[ANY] * n, out_specs=[ANY] * n, out_shape=out_shape,
        scratch_shapes=[pltpu.SemaphoreType.DMA((4 * n,)), pltpu.SemaphoreType.DMA((4 * n,))],
    )(*fulls)


def _chip_start(parts, tag):
    n = len(parts)

    def body(*refs):
        src, land = refs[2 * n:3 * n], refs[3 * n:4 * n]
        send_sems, recv_sems, token = refs[4 * n:]
        x, y, c, chips = _me()
        k = 2 * x + y
        for w in range(n):
            for j, (px, py) in enumerate(chips):
                _remote(src[w].at[2 * px + py], land[w].at[k], send_sems.at[3 * w + j], recv_sems.at[3 * w + j],
                        (px, py, c)).start()
        token[...] = jnp.zeros_like(token)

    lands = [lax.empty(p.shape, p.dtype) for p in parts]
    sem = pltpu.SemaphoreType.DMA((3 * n,))
    outs = pl.pallas_call(
        body, name="reduce_ici_start_" + tag, in_specs=[HBM] * (2 * n),
        out_specs=[HBM] * (2 * n) + [SEM, SEM, pl.BlockSpec(memory_space=pltpu.VMEM)],
        out_shape=[pltpu.HBM(p.shape, p.dtype) for p in parts + lands] + [sem, sem, TOKEN],
        input_output_aliases={i: i for i in range(2 * n)}, compiler_params=SPLIT,
    )(*[_in_hbm(v) for v in parts + lands])
    return outs[:n], outs[n:2 * n], outs[2 * n], outs[2 * n + 1], outs[-1]


def _chip_wait(parts, lands, send_sems, recv_sems, after, tag):
    n = len(parts)

    def body(*refs):
        send, recv = refs[2 * n], refs[2 * n + 1]
        src, land = refs[2 * n + 3:3 * n + 3], refs[3 * n + 3:]
        x, y, c, chips = _me()
        for w in range(n):
            for j, (px, py) in enumerate(chips):
                got = land[w].at[2 * px + py]
                _remote(got, got, send.at[3 * w + j], recv.at[3 * w + j], (px, py, c)).wait_recv()
        for w in range(n):
            for j, (px, py) in enumerate(chips):
                sent = src[w].at[2 * px + py]
                _remote(sent, sent, send.at[3 * w + j], recv.at[3 * w + j], (px, py, c)).wait_send()

    outs = pl.pallas_call(
        body, name="reduce_ici_wait_" + tag, in_specs=[HBM] * (2 * n) + [SEM, SEM, ANY], out_specs=[HBM] * (2 * n),
        out_shape=[pltpu.HBM(p.shape, p.dtype) for p in parts + lands],
        input_output_aliases={i: i for i in range(2 * n)}, compiler_params=SPLIT,
    )(*parts, *lands, send_sems, recv_sems, after)
    chip = 2 * lax.axis_index("x") + lax.axis_index("y")
    return [lax.dynamic_update_slice(s, lax.dynamic_index_in_dim(p, chip, 0, keepdims=True), (chip, 0, 0))
            for p, s in zip(outs[:n], outs[n:])]


def _half_swap(halves, tag):
    n = len(halves)
    core = lax.axis_index("c")
    bufs = [lax.dynamic_update_slice(lax.empty((2,) + h.shape, h.dtype), h[None], (core, 0, 0)) for h in halves]

    def body(*refs):
        dst = refs[n:2 * n]
        send_sems, recv_sems = refs[2 * n:]
        x, y, c, _ = _me()
        sibling = (x, y, 1 - c)
        cps = []
        for w in range(n):
            cp = _remote(dst[w].at[c], dst[w].at[c], send_sems.at[w], recv_sems.at[w], sibling)
            cp.start()
            cps.append(cp)
        for w in range(n):
            other = dst[w].at[1 - c]
            _remote(other, other, send_sems.at[w], recv_sems.at[w], sibling).wait_recv()
        for cp in cps:
            cp.wait_send()

    outs = pl.pallas_call(
        body, name="reduce_half_swap_" + tag, in_specs=[ANY] * n, out_specs=[ANY] * n,
        out_shape=[jax.ShapeDtypeStruct(b.shape, b.dtype) for b in bufs],
        input_output_aliases={w: w for w in range(n)},
        scratch_shapes=[pltpu.SemaphoreType.DMA((n,)), pltpu.SemaphoreType.DMA((n,))],
    )(*bufs)
    return [o.reshape(2 * o.shape[1], o.shape[2]) for o in outs]


def _add_parts(full, axis, rows, sib, name):
    _, r, c = sib.shape
    tr, tc = _tile(r, 256, 16), _tile(c, 2048)
    nb = r // tr
    core = jnp.reshape(lax.axis_index("c"), (1,)).astype(jnp.int32)

    def body(c_ref, a_ref, b_ref, o_ref):
        o_ref[0] = (a_ref[...].astype(F32) + b_ref[0].astype(F32)).astype(BF16)

    blk = pl.BlockSpec((1, tr, tc), lambda j, i, l, cr: (j, i, l))
    return pl.pallas_call(
        body, name=name,
        grid_spec=pltpu.PrefetchScalarGridSpec(
            num_scalar_prefetch=1, grid=(4, nb, c // tc),
            in_specs=[pl.BlockSpec((tr, tc), lambda j, i, l, cr: ((_slot(axis, j) * 2 + cr[0]) * nb + i, l)), blk],
            out_specs=blk),
        out_shape=jax.ShapeDtypeStruct(sib.shape, BF16),
        compiler_params=_cparams(("parallel", "parallel", "parallel")),
    )(core, full, sib)


def _sum_slots(a, name):
    _, r, c = a.shape
    tr, tc = _tile(r, 256, 8), _tile(c, 2048)

    def body(a_ref, o_ref):
        v = a_ref[...].astype(F32)
        o_ref[...] = ((v[0] + v[1]) + v[2]) + v[3]

    return pl.pallas_call(
        body, name=name, grid=(r // tr, c // tc),
        in_specs=[pl.BlockSpec((4, tr, tc), lambda i, l: (0, i, l))],
        out_specs=pl.BlockSpec((tr, tc), lambda i, l: (i, l)),
        out_shape=jax.ShapeDtypeStruct((r, c), F32),
        compiler_params=_cparams(("parallel", "parallel")),
    )(a)


class _Reducer:
    def __init__(self, spec):
        self.spec = spec
        self.paired = {}
        self.pending = []

    def pair(self, name, full):
        ax, shp = self.spec[name]
        land = lax.empty((4, shp[0] // 2, shp[1]), full.dtype)
        arrays, sems, token = _split_start("reduce_pair_start_" + name, [full, land], _pair_geometry([ax], [shp]), 4)
        self.paired[name] = (arrays, sems)
        return token

    def ship(self, tag, names, after):
        parts = []
        for n in names:
            ax, shp = self.spec[n]
            arrays, sems = self.paired.pop(n)
            full, sib = _split_wait("reduce_pair_wait_" + n, arrays, sems, after, _pair_geometry([ax], [shp]))
            parts.append(_add_parts(full, ax, shp[0], sib, name=f"reduce_add_{n}"))
        parts, lands, send, recv, token = _chip_start(parts, tag)
        self.pending.append((tag, names, parts, lands, send, recv))
        return token

    def start(self, tag, grads):
        names = list(grads)
        fulls, axes = [grads[n] for n in names], [self.spec[n][0] for n in names]
        shapes = [self.spec[n][1] for n in names]
        from_sibling = _pair_exchange(fulls, axes, shapes, tag)
        parts = [_add_parts(f, a, r, s, name=f"reduce_add_{n}")
                 for n, f, a, (r, cl), s in zip(names, fulls, axes, shapes, from_sibling)]
        parts, lands, send, recv, token = _chip_start(parts, tag)
        self.pending.append((tag, names, parts, lands, send, recv))
        return token

    def finish(self, after, tags):
        out = {}
        for tag, names, parts, lands, send, recv in [p for p in self.pending if p[0] in tags]:
            slots = _chip_wait(parts, lands, send, recv, after, tag)
            halves = [_sum_slots(s, name=f"reduce_sum_{n}") for n, s in zip(names, slots)]
            out.update(zip(names, _half_swap(halves, tag)))
        return out


def _allreduce_small(pack, after):
    rows = pack.shape[0]

    def body(p_ref, _, o_ref, slots, send_sems, recv_sems):
        x, y, c, _ = _me()
        me = 4 * x + 2 * y + c
        slots[me] = p_ref[...]
        cps = []
        for r in range(1, 8):
            peer = (x ^ (r >> 2), y ^ ((r >> 1) & 1), c ^ (r & 1))
            cp = _remote(p_ref, slots.at[me], send_sems.at[r - 1], recv_sems.at[r - 1], peer)
            cp.start()
            cps.append(cp)
        for r in range(1, 8):
            frm = me ^ r
            _remote(slots.at[frm], slots.at[frm], send_sems.at[r - 1], recv_sems.at[r - 1], (x, y, c)).wait_recv()
        for cp in cps:
            cp.wait_send()
        acc = slots[0]
        for s in range(1, 8):
            acc = acc + slots[s]
        o_ref[...] = acc

    vm = pl.BlockSpec(memory_space=pltpu.VMEM)
    return pl.pallas_call(
        body, name="allreduce_small", in_specs=[vm, ANY], out_specs=vm,
        out_shape=jax.ShapeDtypeStruct(pack.shape, F32),
        scratch_shapes=[pltpu.VMEM((8, rows, HEAD_DIM), F32), pltpu.SemaphoreType.DMA((7,)),
                        pltpu.SemaphoreType.DMA((7,))],
    )(pack, after)


_ROWS = ["norm_mix", "norm_ffn", "mem_norm", "fox_q_norm", "fox_k_norm", "gdn_out_norm", "mem_q_norm",
         "mem_k_norm", "fox_f_bias", "gdn_a_log", "gdn_dt_bias"]


def _pack_rows(vals):
    out = []
    for name in _ROWS:
        v = vals[name].reshape(-1)
        n = -(-v.shape[0] // HEAD_DIM) * HEAD_DIM
        out.append(jnp.pad(v, (0, n - v.shape[0])).reshape(-1, HEAD_DIM))
    return jnp.concatenate(out, axis=0)


def _unpack_rows(pack, like):
    out, r = {}, 0
    for name in _ROWS:
        n = like[name].shape[-1]
        nr = -(-n // HEAD_DIM)
        out[name] = pack[r:r + nr].reshape(1, -1)[:, :n]
        r += nr
    return out, r


def kernel(x, mem, norm_mix, w_in, fox_f_bias, fox_q_norm, fox_k_norm, gdn_conv, gdn_a_log, gdn_dt_bias, gdn_out_norm, mem_norm, w_mem_kv, mem_q_norm, mem_k_norm, w_out, norm_ffn, w_gate_up, w_down, loss_target, m_norm_mix, m_w_in, m_fox_f_bias, m_fox_q_norm, m_fox_k_norm, m_gdn_conv, m_gdn_a_log, m_gdn_dt_bias, m_gdn_out_norm, m_mem_norm, m_w_mem_kv, m_mem_q_norm, m_mem_k_norm, m_w_out, m_norm_ffn, m_w_gate_up, m_w_down, v_norm_mix, v_w_in, v_fox_f_bias, v_fox_q_norm, v_fox_k_norm, v_gdn_conv, v_gdn_a_log, v_gdn_dt_bias, v_gdn_out_norm, v_mem_norm, v_w_mem_kv, v_mem_q_norm, v_mem_k_norm, v_w_out, v_norm_ffn, v_w_gate_up, v_w_down):
    a = dict(locals())
    d = x.shape[-1]
    lay = _Layout(d)
    chip = 2 * lax.axis_index("x") + lax.axis_index("y")
    small = {n: a[n] for n in _ROWS}
    big = ["w_in", "w_mem_kv", "w_out", "w_gate_up", "w_down"]
    axes = [0, 0, 0, 1, 0]

    conv_cols = gdn_conv.shape[-1]
    conv_n = CONV_WIDTH * conv_cols
    conv_rows = -(-conv_n // HEAD_DIM)
    conv_blk = jnp.pad(gdn_conv.reshape(-1), (0, 32 * HEAD_DIM - conv_n)).reshape(32, HEAD_DIM)
    axis_of = dict(zip(big, axes), conv=0, w_in_a=0, w_in_b=0)
    shape_of = {n: a[n].shape[1:] for n in big[1:]}
    shape_of.update(w_in_a=(w_in.shape[1], lay.cols_a), w_in_b=(w_in.shape[1], lay.cols_b), conv=conv_blk.shape)
    placed = {n: _cast_place(a[n][0], axis_of[n], "cast_" + n) for n in big[1:]}
    placed["w_in_a"] = _cast_place(w_in[0], 0, "cast_w_in_a", lambda v: lay.regroup(v)[:, :lay.cols_a], lay.cols_a)
    placed["w_in_b"] = _cast_place(w_in[0], 0, "cast_w_in_b", lambda v: lay.regroup(v)[:, lay.cols_a:], lay.cols_b)
    placed["conv"] = lax.dynamic_update_slice(lax.empty((4 * 32, HEAD_DIM), F32), conv_blk, (chip * 32, 0))
    grouped = {"in_a": ["w_in_a"], "in_b": ["w_in_b"], "mixer": ["w_mem_kv", "conv"], "out": ["w_out"],
               "gate_up": ["w_gate_up"], "down": ["w_down"]}
    inflight = {}

    def start(tags, name):
        names = [n for t in tags for n in grouped[t]]
        bufs, sems, token = _gather_start([placed[n] for n in names], [axis_of[n] for n in names],
                                          [shape_of[n] for n in names],
                                          [[names.index(n) for n in grouped[t]] for t in tags], name)
        for t, pair in zip(tags, sems):
            inflight[t] = ([bufs[names.index(n)] for n in grouped[t]], pair)
        return token

    start(["in_a"], "gather_ici_start_in")
    all_started = start(["in_b", "mixer", "out", "gate_up", "down"], "gather_ici_start_rest")

    forwarding = {}

    def prefetch(tag, after):
        bufs, sem_pair = inflight.pop(tag)
        ax, shp = [axis_of[n] for n in grouped[tag]], [shape_of[n] for n in grouped[tag]]
        got = _gather_wait(bufs, ax, shp, sem_pair, all_started if tag == "in_a" else after,
                           "gather_ici_wait_" + tag)
        geometry = _forward_geometry(ax, shp)
        got, sems, _ = _split_start("gather_forward_start_" + tag, got, geometry, 3 * len(got))
        forwarding[tag] = (got, sems, geometry)

    def weights(tag, after):
        got, sems, geometry = forwarding.pop(tag)
        got = _split_wait("gather_forward_wait_" + tag, got, sems, after, geometry)
        if tag != "mixer":
            return got
        taps = got[1].reshape(4, 32 * HEAD_DIM)[:, :conv_n].reshape(4, CONV_WIDTH, conv_cols)
        return got[0], jnp.transpose(taps, (1, 0, 2)).reshape(CONV_WIDTH, 4 * conv_cols)

    sp = dict(small)
    reducer = _Reducer({n: (axis_of[n], shape_of[n]) for n in big[1:] + ["w_in_a", "w_in_b"]})
    loss_blk, dx, g = _local_step(x[0], mem[0], loss_target[0], prefetch, weights, reducer, sp)

    gsmall = {n: g[n] for n in _ROWS}
    pack = jnp.concatenate([_pack_rows(gsmall), g["gdn_conv"].reshape(-1, HEAD_DIM), loss_blk], axis=0)
    pack = jnp.pad(pack, ((0, -pack.shape[0] % 8), (0, 0)))
    out = {"grad_x": dx[None]}

    def adamw_shards(reduced):
        if "w_in_a" in reduced:
            reduced = {"w_in": (reduced["w_in_a"], reduced["w_in_b"])}
        for n, gsh in reduced.items():
            join = (lambda ga, gb: lay.ungroup(jnp.concatenate([ga, gb], axis=1))) if n == "w_in" else None
            res = _adamw(a[n][0], gsh, a["m_" + n][0], a["v_" + n][0], g_fn=join, name="adamw_" + n)
            for pre, r in zip(["grad_", "delta_", "new_m_", "new_v_"], res):
                out[pre + n] = r[None]
        return res[0]

    done = adamw_shards(reducer.finish(dx, ("ffn", "mix")))
    tot = _allreduce_small(pack, done)
    gs, r0 = _unpack_rows(tot, small)
    conv_g = tot[r0:r0 + CONV_WIDTH * 4 * conv_cols // HEAD_DIM].reshape(CONV_WIDTH, 4 * conv_cols)
    gs_conv = lax.dynamic_slice_in_dim(conv_g, chip * conv_cols, conv_cols, axis=1)
    out["loss"] = tot[r0 + CONV_WIDTH * 4 * conv_cols // HEAD_DIM, 0]
    adamw_shards(reducer.finish(tot, ("in",)))
    conv_pad = lambda v: jnp.pad(v.reshape(-1), (0, conv_rows * HEAD_DIM - conv_n)).reshape(conv_rows, HEAD_DIM)
    packs = []
    for src, cv in [(small, gdn_conv), (gs, gs_conv), ({n: a["m_" + n] for n in _ROWS}, m_gdn_conv),
                    ({n: a["v_" + n] for n in _ROWS}, v_gdn_conv)]:
        packs.append(jnp.concatenate([_pack_rows(src), conv_pad(cv)], axis=0))
    res = _adamw(*packs, name="adamw_small")
    for pre, r in zip(["grad_", "delta_", "new_m_", "new_v_"], res):
        vals, r1 = _unpack_rows(r, small)
        for n in _ROWS:
            out[pre + n] = vals[n]
        out[pre + "gdn_conv"] = r[r1:r1 + conv_rows].reshape(-1)[:conv_n].reshape(gdn_conv.shape)
    names = ["norm_mix", "w_in", "fox_f_bias", "fox_q_norm", "fox_k_norm", "gdn_conv", "gdn_a_log", "gdn_dt_bias",
             "gdn_out_norm", "mem_norm", "w_mem_kv", "mem_q_norm", "mem_k_norm", "w_out", "norm_ffn", "w_gate_up",
             "w_down"]
    return (out["loss"], out["grad_x"], *[out[p + n] for p in ["grad_", "delta_", "new_m_", "new_v_"] for n in names])
```

```python
import functools
import math

import jax
import jax.numpy as jnp
from jax import lax
from jax.experimental import pallas as pl
from jax.experimental.pallas import tpu as pltpu

F32, BF16 = jnp.float32, jnp.bfloat16
HEAD_DIM = 128
CHUNK = 64
N_MEM_HEADS = 4
CONV_WIDTH = 4
NORM_EPS = 1e-6
ADAM_LR, ADAM_B1, ADAM_B2, ADAM_EPS, ADAM_WD, ADAM_STEP = 0.001, 0.9, 0.999, 1e-08, 0.01, 10
VMEM_LIMIT = 48 * 1024 * 1024
NEG = -1e30
MESH = pl.DeviceIdType.MESH


def _cparams(sem=None, **kw):
    if sem is not None:
        kw["dimension_semantics"] = sem
    return pltpu.CompilerParams(vmem_limit_bytes=VMEM_LIMIT, **kw)


def _tile(n, target, mult=128):
    best = None
    d = mult
    while d <= min(n, target):
        if n % d == 0:
            best = d
        d += mult
    return best if best is not None else n


def _dot(a, b, dims, hi):
    if a.ndim == 3:
        dn = (((dims[0][0] + 1,), (dims[1][0] + 1,)), ((0,), (0,)))
    else:
        dn = (dims, ((), ()))
    if hi is not None:
        return lax.dot_general(a, b, dn, precision=hi, preferred_element_type=F32)
    return lax.dot_general(a.astype(BF16), b.astype(BF16), dn, preferred_element_type=F32)


def _make_dots(hi, cotangent=None):
    @jax.custom_vjp
    def nn(a, b):
        return _dot(a, b, ((1,), (0,)), hi)

    @jax.custom_vjp
    def nt(a, b):
        return _dot(a, b, ((1,), (1,)), hi)

    @jax.custom_vjp
    def tn(a, b):
        return _dot(a, b, ((0,), (0,)), hi)

    bnn, bnt, btn = cotangent or (nn, nt, tn)
    nn.defvjp(lambda a, b: (nn(a, b), (a, b)), lambda r, g: (bnt(g, r[1]), btn(r[0], g)))
    nt.defvjp(lambda a, b: (nt(a, b), (a, b)), lambda r, g: (bnn(g, r[1]), btn(g, r[0])))
    tn.defvjp(lambda a, b: (tn(a, b), (a, b)), lambda r, g: (bnt(r[1], g), bnn(r[0], g)))
    return nn, nt, tn


_nn, _nt, _tn = _make_dots(None)
_nn_hi, _nt_hi, _tn_hi = _make_dots(lax.Precision.HIGHEST)
_nn_x3, _nt_x3, _tn_x3 = _make_dots(lax.Precision.HIGH, (_nn, _nt, _tn))


def _sigmoid(x):
    return 1.0 / (1.0 + jnp.exp(-x))


@jax.custom_vjp
def _softplus(x):
    return jnp.maximum(x, 0.0) + jnp.log(1.0 + jnp.exp(-jnp.abs(x)))


_softplus.defvjp(lambda x: (_softplus(x), x), lambda x, g: (g * _sigmoid(x),))


def _silu(x):
    return x * _sigmoid(x)


def _rms_fn(x, gain, z=None):
    y = x * lax.rsqrt(jnp.mean(x * x, axis=-1, keepdims=True) + NORM_EPS) * gain
    if z is not None:
        y = y * _silu(z)
    return y


def _mm(a, b, *, ta=False, tb=False, out_dtype=F32, res=None, stack=None, after=None, name):
    a2, b2 = a.shape[-2:], b.shape[-2:]
    ns = b.shape[0] if stack else 1
    m = a2[1] if ta else a2[0]
    k = a2[0] if ta else a2[1]
    n = b2[0] if tb else b2[1]
    assert k == (b2[1] if tb else b2[0])
    tm, tn, tk = _mm_tiles(m, n, k, ns if stack == "sum" else 1, a.dtype.itemsize, b.dtype.itemsize,
                           jnp.dtype(out_dtype).itemsize, res is not None)
    nk = k // tk
    single = nk == 1 and stack != "sum"
    dims = ((0 if ta else 1,), (1 if tb else 0,))
    if stack == "sum":
        order = lambda g0, g1, g2, g3: (g2, g0, g1, g3)
        grid = (m // tm, n // tn, ns, nk)
    else:
        order = lambda g0, g1, g2, g3: (g0, g1, g2, g3)
        grid = (ns, m // tm, n // tn, nk)

    def body(*refs):
        if after is not None:
            refs = refs[:2 + (res is not None)] + refs[3 + (res is not None):]
        if single:
            a_ref, b_ref = refs[:2]
            r = lax.dot_general(a_ref[...].astype(BF16), b_ref[...].astype(BF16), (dims, ((), ())),
                                preferred_element_type=F32)
            if res is not None:
                r = r + refs[2][...]
            refs[-1][...] = r.astype(out_dtype)
            return
        if res is None:
            a_ref, b_ref, o_ref, acc = refs
        else:
            a_ref, b_ref, r_ref, o_ref, acc = refs
        s, _, _, kk = order(*[pl.program_id(d) for d in range(4)])
        first = kk == 0
        last = kk == nk - 1
        if stack == "sum":
            first, last = first & (s == 0), last & (s == ns - 1)

        @pl.when(first)
        def _():
            acc[...] = jnp.zeros_like(acc)

        acc[...] += lax.dot_general(a_ref[...].astype(BF16), b_ref[...].astype(BF16), (dims, ((), ())),
                                    preferred_element_type=F32)

        @pl.when(last)
        def _():
            r = acc[...]
            if res is not None:
                r = r + r_ref[...]
            o_ref[...] = r.astype(out_dtype)

    def spec(shape, idx, stacked):
        if stacked:
            return pl.BlockSpec((None,) + shape, lambda *g: (order(*g)[0],) + idx(*order(*g)))
        return pl.BlockSpec(shape, lambda *g: idx(*order(*g)))

    a_spec = (spec((tk, tm), lambda s, i, j, kk: (kk, i), stack == "sum") if ta
              else spec((tm, tk), lambda s, i, j, kk: (i, kk), stack == "sum"))
    b_spec = (spec((tn, tk), lambda s, i, j, kk: (j, kk), bool(stack)) if tb
              else spec((tk, tn), lambda s, i, j, kk: (kk, j), bool(stack)))
    o_spec = spec((tm, tn), lambda s, i, j, kk: (i, j), stack == "out")
    ins, specs = [a, b], [a_spec, b_spec]
    if res is not None:
        ins.append(res)
        specs.append(o_spec)
    if after is not None:
        ins.append(after)
        specs.append(pl.BlockSpec(after.shape, lambda *g: (0,) * after.ndim))
    sem = (("parallel", "parallel", "arbitrary", "arbitrary") if stack == "sum"
           else ("parallel", "parallel", "parallel", "arbitrary"))
    return pl.pallas_call(
        body, name=name, grid=grid, in_specs=specs, out_specs=o_spec,
        out_shape=jax.ShapeDtypeStruct(((ns,) if stack == "out" else ()) + (m, n), out_dtype),
        scratch_shapes=[] if single else [pltpu.VMEM((tm, tn), F32)],
        compiler_params=_cparams(sem),
    )(*ins)


MM_VMEM_BUDGET = 40 * 1024 * 1024


def _mm_tiles(m, n, k, ns, sa, sb, so, has_res):
    def divs(x, mult, cap):
        out = [d for d in range(mult, min(x, cap) + 1, mult) if x % d == 0]
        return out or [x]

    best = None
    for tk in divs(k, 128, 8192):
        nk = (k // tk) * ns
        for tm in divs(m, 8, 2048):
            for tn in divs(n, 128, 2048):
                vmem = 2 * (tm * tk * sa + tk * tn * sb + tm * tn * so) + (2 * tm * tn * 4 if has_res else 0)
                vmem += tm * tn * 4 if nk > 1 else 0
                if vmem > MM_VMEM_BUDGET:
                    continue
                steps = (m // tm) * (n // tn) * nk
                traffic = (m // tm) * k * n * sb * ns + (n // tn if nk > 1 else 1) * m * k * sa * ns
                cost = steps * 0.4e-6 + traffic / 2.5e12 + (nk * m * n * 8 / 6e12 if nk > 1 else 0)
                cost += 2.0 * m * n * k * ns / 7e14
                if best is None or cost < best[0]:
                    best = (cost, tm, tn, tk)
    return best[1:]


def _norm_fwd(x, xoff, gain, ncol, w, out_dtype, *, z=None, zoff=0, into=None, into_off=0, name):
    t = x.shape[0]
    tr = _tile(t, max(256, (1 << 18) // w), 8)

    def body(*refs):
        x_ref, g_ref, o_ref = refs[0], refs[1], refs[-1]
        y = _rms_fn(x_ref[...], g_ref[...]) if z is None else _rms_fn(x_ref[...], g_ref[...], refs[2][...])
        o_ref[...] = y.astype(out_dtype)

    ins = [x, gain]
    specs = [pl.BlockSpec((tr, w), lambda j, r: (r, xoff + j)), pl.BlockSpec((1, w), lambda j, r: (0, 0))]
    if z is not None:
        ins.append(z)
        specs.append(pl.BlockSpec((tr, w), lambda j, r: (r, zoff + j)))
    aliases = {}
    if into is not None:
        aliases = {len(ins): 0}
        ins.append(into)
        specs.append(pl.BlockSpec(memory_space=pl.ANY))
    return pl.pallas_call(
        body, name=name, grid=(ncol, t // tr), in_specs=specs,
        out_specs=pl.BlockSpec((tr, w), lambda j, r: (r, into_off + j)),
        out_shape=jax.ShapeDtypeStruct((t, ncol * w) if into is None else into.shape, out_dtype),
        input_output_aliases=aliases, compiler_params=_cparams(("parallel", "parallel")),
    )(*ins)


def _norm_bwd(x, xoff, gain, dy, dyoff, ncol, w, *, z=None, zoff=0, res=None, name):
    t = x.shape[0]
    tr = _tile(t, max(256, (1 << 18) // w), 8)

    def body(*refs):
        it = iter(refs)
        x_ref, g_ref = next(it), next(it)
        z_ref = next(it) if z is not None else None
        dy_ref = next(it)
        r_ref = next(it) if res is not None else None
        dx_ref = next(it)
        dz_ref = next(it) if z is not None else None
        dg_ref = next(it)

        @pl.when((pl.program_id(0) == 0) & (pl.program_id(1) == 0))
        def _():
            dg_ref[...] = jnp.zeros_like(dg_ref)

        args = (x_ref[...], g_ref[...]) + ((z_ref[...],) if z is not None else ())
        _, vjp = jax.vjp(_rms_fn, *args)
        grads = vjp(dy_ref[...].astype(F32))
        dx = grads[0]
        if res is not None:
            dx = dx + r_ref[...]
        dx_ref[...] = dx
        if z is not None:
            dz_ref[...] = grads[2]
        dg_ref[...] += grads[1]

    ins = [x, gain]
    specs = [pl.BlockSpec((tr, w), lambda j, r: (r, xoff + j)), pl.BlockSpec((1, w), lambda j, r: (0, 0))]
    if z is not None:
        ins.append(z)
        specs.append(pl.BlockSpec((tr, w), lambda j, r: (r, zoff + j)))
    ins.append(dy)
    specs.append(pl.BlockSpec((tr, w), lambda j, r: (r, dyoff + j)))
    blk = pl.BlockSpec((tr, w), lambda j, r: (r, j))
    if res is not None:
        ins.append(res)
        specs.append(blk)
    full = jax.ShapeDtypeStruct((t, ncol * w), F32)
    out_shape, out_specs = [full], [blk]
    if z is not None:
        out_shape.append(full)
        out_specs.append(blk)
    out_shape.append(jax.ShapeDtypeStruct((1, w), F32))
    out_specs.append(pl.BlockSpec((1, w), lambda j, r: (0, 0)))
    return pl.pallas_call(
        body, name=name, grid=(ncol, t // tr), in_specs=specs, out_specs=out_specs, out_shape=out_shape,
        compiler_params=_cparams(("arbitrary", "arbitrary")),
    )(*ins)


def _small_fn(x, pa, pb, nf, ng):
    lane = lax.broadcasted_iota(jnp.int32, x.shape, 1)
    zz = x + pb
    logf = -_softplus(-zz)
    g = -jnp.exp(pa) * _softplus(zz)
    beta = _sigmoid(x)
    return jnp.where(lane < nf, logf, jnp.where(lane < nf + ng, g, beta))


def _tri(n, upper):
    r = lax.broadcasted_iota(jnp.int32, (n, n), 0)
    c = lax.broadcasted_iota(jnp.int32, (n, n), 1)
    return jnp.where((c >= r) if upper else (c <= r), 1.0, 0.0).astype(F32)


def _small_fwd(p, off, pa, pb, nf, ng):
    t = p.shape[0]
    blk = HEAD_DIM
    nb = t // blk

    def body(x_ref, pa_ref, pb_ref, v_ref, c_ref):
        v_ref[...] = _small_fn(x_ref[...], pa_ref[...], pb_ref[...], nf, ng)
        tri = _tri(blk, False)

        carry = jnp.zeros((1, HEAD_DIM), F32)
        for i in range(nb):
            rows = slice(i * blk, (i + 1) * blk)
            c = _nn_hi(tri, v_ref[rows, :]) + carry
            c_ref[rows, :] = c
            carry = c[blk - 1:blk, :]

    row = pl.BlockSpec((1, HEAD_DIM), lambda i: (0, 0))
    out = pl.BlockSpec((t, HEAD_DIM), lambda i: (0, 0))
    return pl.pallas_call(
        body, name="small_fwd", grid=(1,),
        in_specs=[pl.BlockSpec((t, HEAD_DIM), lambda i: (0, off)), row, row], out_specs=[out, out],
        out_shape=[jax.ShapeDtypeStruct((t, HEAD_DIM), F32)] * 2,
        compiler_params=_cparams(("arbitrary",)),
    )(p, pa, pb)


def _small_bwd(p, off, pa, pb, dvals, dcsum, nf, ng):
    t = p.shape[0]
    blk = HEAD_DIM
    nb = t // blk

    def body(x_ref, pa_ref, pb_ref, dv_ref, dc_ref, dx_ref, dpa_ref, dpb_ref, tot_ref):
        tri = _tri(blk, True)

        carry = jnp.zeros((1, HEAD_DIM), F32)
        for i in reversed(range(nb)):
            rows = slice(i * blk, (i + 1) * blk)
            c = _nn_hi(tri, dc_ref[rows, :]) + carry
            tot_ref[rows, :] = c + dv_ref[rows, :]
            carry = c[0:1, :]
        f = functools.partial(_small_fn, nf=nf, ng=ng)
        _, vjp = jax.vjp(f, x_ref[...], pa_ref[...], pb_ref[...])
        dx, dpa, dpb = vjp(tot_ref[...])
        dx_ref[...] = dx
        dpa_ref[...] = dpa
        dpb_ref[...] = dpb

    row = pl.BlockSpec((1, HEAD_DIM), lambda i: (0, 0))
    full = pl.BlockSpec((t, HEAD_DIM), lambda i: (0, 0))
    return pl.pallas_call(
        body, name="small_bwd", grid=(1,),
        in_specs=[pl.BlockSpec((t, HEAD_DIM), lambda i: (0, off)), row, row, full, full],
        out_specs=[full, row, row],
        out_shape=[jax.ShapeDtypeStruct((t, HEAD_DIM), F32), jax.ShapeDtypeStruct((1, HEAD_DIM), F32),
                   jax.ShapeDtypeStruct((1, HEAD_DIM), F32)],
        scratch_shapes=[pltpu.VMEM((t, HEAD_DIM), F32)],
        compiler_params=_cparams(("arbitrary",)),
    )(p, pa, pb, dvals, dcsum)


def _fox_heads(nf, most):
    return next(h for h in range(most, 0, -1) if nf % h == 0)


def _fox_fwd(q, k, v, cc, cr, nf, tq, tk, d_mix):
    t = q.shape[0]
    scale = HEAD_DIM ** -0.5
    assert tq == tk

    vt = jnp.transpose(v.reshape(t // tk, tk, nf, HEAD_DIM), (2, 0, 3, 1))

    hp = _fox_heads(nf, 3)
    lanes = lambda h: slice(h * HEAD_DIM, (h + 1) * HEAD_DIM)

    def body(q_ref, k_ref, vt_ref, cc_ref, cr_ref, o_ref, lse_ref, mix_ref):
        i = pl.program_id(1)
        qs = [q_ref[:, lanes(h)] for h in range(hp)]
        cqs = [cr_ref[h, i] for h in range(hp)]
        ones = jnp.ones((8, tk), BF16)
        diff = lax.broadcasted_iota(jnp.int32, (tk, tq), 0) - lax.broadcasted_iota(jnp.int32, (tk, tq), 1)

        def scores(h, j):
            ks = pl.ds(pl.multiple_of(j * tk, tk), tk)
            return lax.dot_general(k_ref[ks, lanes(h)], qs[h], (((1,), (1,)), ((), ())),
                                   preferred_element_type=F32)

        def tile(h, j, m, l, acc, s, masked):
            ks = pl.ds(pl.multiple_of(j * tk, tk), tk)
            s = s * scale + cqs[h] - cc_ref[h, ks, :]
            if masked:
                s = jnp.where(diff <= 0, s, NEG)
            m_new = jnp.maximum(m, jnp.max(s, axis=0, keepdims=True))
            pr = jnp.exp(s - m_new).astype(BF16)
            alpha = jnp.exp(m - m_new)
            l = alpha * l + jnp.dot(ones, pr, preferred_element_type=F32)[:1]
            acc = alpha * acc + jnp.dot(vt_ref[h, j], pr, preferred_element_type=F32)
            return m_new, l, acc

        def step(j, carry):
            nxt = [scores(h, j + 1) for h in range(hp)]
            return tuple(tile(h, j, *carry[h], False) + (nxt[h],) for h in range(hp))

        init = tuple((jnp.full((1, tq), NEG, F32), jnp.zeros((1, tq), F32), jnp.zeros((HEAD_DIM, tq), F32),
                      scores(h, 0)) for h in range(hp))
        carry = lax.fori_loop(0, i, step, init)
        for h in range(hp):
            m, l, acc = tile(h, i, *carry[h], True)
            o = jnp.transpose(acc / l)
            o_ref[:, lanes(h)] = o
            mix_ref[:, lanes(h)] = o.astype(BF16)
            lse_ref[h, 0] = m + jnp.log(l)

    w = hp * HEAD_DIM
    qblk = pl.BlockSpec((tq, w), lambda h, i: (i, h))
    return pl.pallas_call(
        body, name="fox_fwd", grid=(nf // hp, t // tq),
        in_specs=[qblk, pl.BlockSpec((t, w), lambda h, i: (0, h)),
                  pl.BlockSpec((hp, t // tk, HEAD_DIM, tk), lambda h, i: (h, 0, 0, 0)),
                  pl.BlockSpec((hp, t, 1), lambda h, i: (h, 0, 0)),
                  pl.BlockSpec((hp, t // tk, 1, tk), lambda h, i: (h, 0, 0, 0))],
        out_specs=[qblk, pl.BlockSpec((hp, 1, 1, tq), lambda h, i: (h, i, 0, 0)), qblk],
        out_shape=[jax.ShapeDtypeStruct((t, nf * HEAD_DIM), F32), jax.ShapeDtypeStruct((nf, t // tq, 1, tq), F32),
                   jax.ShapeDtypeStruct((t, d_mix), BF16)],
        compiler_params=_cparams(("parallel", "parallel")),
    )(q, k, vt, cc, cr)


def _fox_bwd(q, k, v, cc, cr, o, lse, dmix, nf, tq, tk):
    t = q.shape[0]
    scale = HEAD_DIM ** -0.5
    assert tq == tk
    hp = _fox_heads(nf, 3)
    lanes = lambda h: slice(h * HEAD_DIM, (h + 1) * HEAD_DIM)
    kt = jnp.transpose(k.reshape(t // tk, tk, nf, HEAD_DIM), (2, 0, 3, 1))

    def body(q_ref, k_ref, kt_ref, v_ref, cc_ref, cr_ref, o_ref, lse_ref, do_ref,
             dq_ref, dk_ref, dv_ref, dcq_ref, dck_ref):
        i = pl.program_id(1)

        @pl.when(i == 0)
        def _():
            dk_ref[...] = jnp.zeros_like(dk_ref)
            dv_ref[...] = jnp.zeros_like(dv_ref)
            dck_ref[...] = jnp.zeros_like(dck_ref)

        diff = lax.broadcasted_iota(jnp.int32, (tk, tq), 0) - lax.broadcasted_iota(jnp.int32, (tk, tq), 1)
        qs = [q_ref[:, lanes(h)] for h in range(hp)]
        dos = [do_ref[:, lanes(h)] for h in range(hp)]
        do_b = [d.astype(BF16) for d in dos]
        cqs = [cr_ref[h, i] for h in range(hp)]
        lses = [lse_ref[h, 0] for h in range(hp)]
        deltas = [jnp.sum(jnp.transpose(dos[h] * o_ref[:, lanes(h)]), axis=0, keepdims=True) for h in range(hp)]

        def products(h, j):
            ks = pl.ds(pl.multiple_of(j * tk, tk), tk)
            nt = (((1,), (1,)), ((), ()))
            return (lax.dot_general(k_ref[ks, lanes(h)], qs[h], nt, preferred_element_type=F32),
                    lax.dot_general(v_ref[ks, lanes(h)], do_b[h], nt, preferred_element_type=F32))

        def tile(h, j, dqt, dcq, s, dp, masked):
            ks = pl.ds(pl.multiple_of(j * tk, tk), tk)
            pr = jnp.exp(s * scale + cqs[h] - cc_ref[h, ks, :] - lses[h])
            if masked:
                pr = jnp.where(diff <= 0, pr, 0.0)
            ds = pr * (dp - deltas[h])
            ds_b = ds.astype(BF16)
            dqt = dqt + jnp.dot(kt_ref[h, j], ds_b, preferred_element_type=F32)
            dk_ref[ks, lanes(h)] += jnp.dot(ds_b, qs[h], preferred_element_type=F32) * scale
            dv_ref[ks, lanes(h)] += jnp.dot(pr.astype(BF16), do_b[h], preferred_element_type=F32)
            dck_ref[h, ks, :] -= jnp.sum(ds, axis=1, keepdims=True)
            return dqt, dcq + jnp.sum(ds, axis=0, keepdims=True)

        def step(j, carry):
            nxt = [products(h, j + 1) for h in range(hp)]
            return tuple(tile(h, j, *carry[h], False) + nxt[h] for h in range(hp))

        init = tuple((jnp.zeros((HEAD_DIM, tq), F32), jnp.zeros((1, tq), F32)) + products(h, 0) for h in range(hp))
        carry = lax.fori_loop(0, i, step, init)
        for h in range(hp):
            dqt, dcq = tile(h, i, *carry[h], True)
            dq_ref[:, lanes(h)] = jnp.transpose(dqt) * scale
            dcq_ref[h, 0] = dcq

    w = hp * HEAD_DIM
    head_all = pl.BlockSpec((t, w), lambda h, i: (0, h))
    qblk = pl.BlockSpec((tq, w), lambda h, i: (i, h))
    colv = pl.BlockSpec((hp, t, 1), lambda h, i: (h, 0, 0))
    rows_all = pl.BlockSpec((hp, t // tk, 1, tk), lambda h, i: (h, 0, 0, 0))
    row_blk = pl.BlockSpec((hp, 1, 1, tq), lambda h, i: (h, i, 0, 0))
    wide = jax.ShapeDtypeStruct((t, nf * HEAD_DIM), F32)
    return pl.pallas_call(
        body, name="fox_bwd", grid=(nf // hp, t // tq),
        in_specs=[qblk, head_all, pl.BlockSpec((hp, t // tk, HEAD_DIM, tk), lambda h, i: (h, 0, 0, 0)), head_all,
                  colv, rows_all, qblk, row_blk, qblk],
        out_specs=[qblk, head_all, head_all, row_blk, colv],
        out_shape=[wide, wide, wide, jax.ShapeDtypeStruct((nf, t // tq, 1, tq), F32),
                   jax.ShapeDtypeStruct((nf, t, 1), F32)],
        compiler_params=_cparams(("parallel", "arbitrary")),
    )(q, k, kt, v, cc, cr, o, lse, dmix)


def _mem_fn(mq, mk, mv, gq, gk):
    qn = _rms_fn(mq, gq)
    kn = _rms_fn(mk, gk)
    s = _nt(qn, kn) * (HEAD_DIM ** -0.5)
    e = jnp.exp(s - lax.stop_gradient(jnp.max(s, axis=1, keepdims=True)))
    pr = e / jnp.sum(e, axis=1, keepdims=True)
    return _nn(pr, mv)


def _mem_specs(t, m, tq, qoff):
    qblk = pl.BlockSpec((tq, HEAD_DIM), lambda h, i: (i, qoff + h))
    kblk = pl.BlockSpec((m, HEAD_DIM), lambda h, i: (0, h))
    vblk = pl.BlockSpec((m, HEAD_DIM), lambda h, i: (0, N_MEM_HEADS + h))
    row = pl.BlockSpec((1, HEAD_DIM), lambda h, i: (0, 0))
    return qblk, kblk, vblk, row


def _mem_fwd(p, qoff, mkv, gq, gk, tq, into, into_off):
    t, m = p.shape[0], mkv.shape[0]
    qblk, kblk, vblk, row = _mem_specs(t, m, tq, qoff)

    def body(q_ref, k_ref, v_ref, gq_ref, gk_ref, _, o_ref):
        o_ref[...] = _mem_fn(q_ref[...], k_ref[...], v_ref[...], gq_ref[...], gk_ref[...]).astype(BF16)

    return pl.pallas_call(
        body, name="mem_fwd", grid=(N_MEM_HEADS, t // tq),
        in_specs=[qblk, kblk, vblk, row, row, pl.BlockSpec(memory_space=pl.ANY)],
        out_specs=pl.BlockSpec((tq, HEAD_DIM), lambda h, i: (i, into_off + h)),
        out_shape=jax.ShapeDtypeStruct(into.shape, BF16), input_output_aliases={5: 0},
        compiler_params=_cparams(("parallel", "parallel")),
    )(p, mkv, mkv, gq, gk, into)


def _mem_bwd(p, qoff, mkv, gq, gk, dmix, dooff, tq):
    t, m = p.shape[0], mkv.shape[0]
    qblk, kblk, vblk, row = _mem_specs(t, m, tq, qoff)

    def body(q_ref, k_ref, v_ref, gq_ref, gk_ref, do_ref, dq_ref, dkv_k_ref, dkv_v_ref, dgq_ref, dgk_ref):
        h, i = pl.program_id(0), pl.program_id(1)

        @pl.when((h == 0) & (i == 0))
        def _():
            dgq_ref[...] = jnp.zeros_like(dgq_ref)
            dgk_ref[...] = jnp.zeros_like(dgk_ref)

        @pl.when(i == 0)
        def _():
            dkv_k_ref[...] = jnp.zeros_like(dkv_k_ref)
            dkv_v_ref[...] = jnp.zeros_like(dkv_v_ref)

        _, vjp = jax.vjp(_mem_fn, q_ref[...], k_ref[...], v_ref[...], gq_ref[...], gk_ref[...])
        dq, dk, dv, dgq, dgk = vjp(do_ref[...])
        dq_ref[...] = dq
        dkv_k_ref[...] += dk
        dkv_v_ref[...] += dv
        dgq_ref[...] += dgq
        dgk_ref[...] += dgk

    oblk = pl.BlockSpec((tq, HEAD_DIM), lambda h, i: (i, h))
    kout = pl.BlockSpec((m, HEAD_DIM), lambda h, i: (0, h))
    half = jax.ShapeDtypeStruct((m, N_MEM_HEADS * HEAD_DIM), F32)
    rshape = jax.ShapeDtypeStruct((1, HEAD_DIM), F32)
    return pl.pallas_call(
        body, name="mem_bwd", grid=(N_MEM_HEADS, t // tq),
        in_specs=[qblk, kblk, vblk, row, row, pl.BlockSpec((tq, HEAD_DIM), lambda h, i: (i, dooff + h))],
        out_specs=[oblk, kout, kout, row, row],
        out_shape=[jax.ShapeDtypeStruct((t, N_MEM_HEADS * HEAD_DIM), F32), half, half, rshape, rshape],
        compiler_params=_cparams(("arbitrary", "arbitrary")),
    )(p, mkv, mkv, gq, gk, dmix)


def _shift_down(x, s):
    if s == 0:
        return x
    r = lax.broadcasted_iota(jnp.int32, x.shape, 0)
    return jnp.where(r >= s, pltpu.roll(x, s, 0), 0.0)


def _shift_up(x, s):
    if s == 0:
        return x
    n = x.shape[0]
    r = lax.broadcasted_iota(jnp.int32, x.shape, 0)
    return jnp.where(r < n - s, pltpu.roll(x, n - s, 0), 0.0)


def _conv_fn(x0, x1, x2, x3, w0, w1, w2, w3, kind):
    y = _silu(x0 * w0 + x1 * w1 + x2 * w2 + x3 * w3)
    if kind == 2:
        return y
    y = y * lax.rsqrt(jnp.sum(y * y, axis=-1, keepdims=True) + NORM_EPS)
    return y * (HEAD_DIM ** -0.5) if kind == 0 else y


def _conv_fwd(p, off, conv_w, ng):
    t = p.shape[0]

    def body(x_ref, w_ref, o_ref):
        kind = pl.program_id(0) // ng
        x = x_ref[...]
        xs = [_shift_down(x, CONV_WIDTH - 1 - j) for j in range(CONV_WIDTH)]
        ws = [w_ref[j:j + 1, :] for j in range(CONV_WIDTH)]
        for kd in range(3):
            @pl.when(kind == kd)
            def _(kd=kd):
                o_ref[...] = _conv_fn(*xs, *ws, kd)

    return pl.pallas_call(
        body, name="gdn_conv_fwd", grid=(3 * ng,),
        in_specs=[pl.BlockSpec((t, HEAD_DIM), lambda c: (0, off + c)),
                  pl.BlockSpec((CONV_WIDTH, HEAD_DIM), lambda c: (0, c))],
        out_specs=pl.BlockSpec((t, HEAD_DIM), lambda c: (0, c)),
        out_shape=jax.ShapeDtypeStruct((t, 3 * ng * HEAD_DIM), F32),
        compiler_params=_cparams(("parallel",)),
    )(p, conv_w)


def _conv_bwd(p, off, conv_w, dys, ng):
    t = p.shape[0]

    def body(x_ref, w_ref, dq_ref, dk_ref, dv_ref, dx_ref, dw_ref):
        kind = pl.program_id(0) // ng
        dy_refs = (dq_ref, dk_ref, dv_ref)
        x = x_ref[...]
        xs = [_shift_down(x, CONV_WIDTH - 1 - j) for j in range(CONV_WIDTH)]
        ws = [w_ref[j:j + 1, :] for j in range(CONV_WIDTH)]
        for kd in range(3):
            @pl.when(kind == kd)
            def _(kd=kd):
                _, vjp = jax.vjp(functools.partial(_conv_fn, kind=kd), *xs, *ws)
                g = vjp(dy_refs[kd][...])
                dx = _shift_up(g[0], CONV_WIDTH - 1)
                for j in range(1, CONV_WIDTH):
                    dx = dx + _shift_up(g[j], CONV_WIDTH - 1 - j)
                dx_ref[...] = dx
                for j in range(CONV_WIDTH):
                    dw_ref[j:j + 1, :] = g[CONV_WIDTH + j]

    blk = pl.BlockSpec((t, HEAD_DIM), lambda c: (0, c))
    head = lambda k: pl.BlockSpec((t, HEAD_DIM), lambda c: (0, jnp.where(c // ng == k, c % ng, 0)))
    wblk = pl.BlockSpec((CONV_WIDTH, HEAD_DIM), lambda c: (0, c))
    return pl.pallas_call(
        body, name="gdn_conv_bwd", grid=(3 * ng,),
        in_specs=[pl.BlockSpec((t, HEAD_DIM), lambda c: (0, off + c)), wblk] + [head(k) for k in range(3)],
        out_specs=[blk, wblk],
        out_shape=[jax.ShapeDtypeStruct((t, 3 * ng * HEAD_DIM), F32),
                   jax.ShapeDtypeStruct((CONV_WIDTH, 3 * ng * HEAD_DIM), F32)],
        compiler_params=_cparams(("parallel",)),
    )(p, conv_w, *dys)


def _lower_inverse(lower):
    c = lower.shape[-1]
    r = lax.broadcasted_iota(jnp.int32, (1, c, c), 1)
    e = lax.broadcasted_iota(jnp.int32, (1, c, c), 2)
    hi = lax.Precision.HIGH
    inv = jnp.where(r == e, 1.0, 0.0) - lower
    pw = lower
    for _ in range(int(math.log2(c)) - 1):
        pw = _dot(pw, pw, ((1,), (0,)), hi)
        inv = inv + _dot(inv, pw, ((1,), (0,)), hi)
    return inv


@jax.custom_vjp
def _solve(lower, inv, vb, kbg):
    hi = lax.Precision.HIGH
    return _dot(inv, vb, ((1,), (0,)), hi), _dot(inv, kbg, ((1,), (0,)), hi)


def _solve_fwd(lower, inv, vb, kbg):
    u, w = _solve(lower, inv, vb, kbg)
    return (u, w), (inv, u, w)


def _solve_bwd(res, cts):
    inv, u, w = res
    dvb, dkbg = _tn(inv, cts[0]), _tn(inv, cts[1])
    return -(_nt(dvb, u) + _nt(dkbg, w)), jnp.zeros_like(inv), dvb, dkbg


_solve.defvjp(_solve_fwd, _solve_bwd)


def _wy_fn(q, k, v, gcol, grow, bcol, inv=None):
    b, c, dk = q.shape
    r = lax.broadcasted_iota(jnp.int32, (1, c, c), 1)
    e = lax.broadcasted_iota(jnp.int32, (1, c, c), 2)
    tril, strict = e <= r, e < r
    gc_col = jnp.sum(jnp.where(tril, grow, 0.0), axis=2, keepdims=True)
    gc_row = jnp.sum(jnp.where(r <= e, gcol, 0.0), axis=1, keepdims=True)
    g_last = jnp.sum(gcol, axis=1, keepdims=True)
    decay = jnp.exp(jnp.where(tril, gc_col - gc_row, NEG))
    kb, vb = k * bcol, v * bcol
    lower = jnp.where(strict, _nt(kb, k) * decay, 0.0)
    if inv is None:
        inv = _lower_inverse(lower)
    u, w = _solve(lower, inv, vb, kb * jnp.exp(gc_col))
    attn = jnp.where(tril, _nt(q, k) * decay, 0.0)
    qg = q * jnp.exp(gc_col)
    kdec = k * jnp.exp(g_last - gc_col)
    egl = jnp.broadcast_to(jnp.exp(g_last), (b, 1, dk))
    return u, w, qg, kdec, attn, egl, inv


def _scan_fn(u, w, qg, kdec, attn, egl, state):
    v_new = u - _nn(w, state)
    o = _nn(qg, state) + _nn(attn, v_new)
    return o, state * egl + _tn(kdec, v_new)


GDN_CHUNKS_PER_STEP = 4


def _gdn_fwd(qkv, gcol, grow, bcol, ng):
    t = qkv.shape[0]
    nch = t // CHUNK

    cb = GDN_CHUNKS_PER_STEP
    *wy, inv = _gdn_wy(qkv, gcol, grow, bcol, ng, cb)

    def body(u_ref, w_ref, qg_ref, kd_ref, at_ref, eg_ref, o_ref, st_ref, state):
        @pl.when(pl.program_id(0) == 0)
        def _():
            state[...] = jnp.zeros_like(state)

        st_ref[:, 0] = state[...]
        heads = lambda ref: jnp.stack([ref[:, h * HEAD_DIM:(h + 1) * HEAD_DIM] for h in range(ng)])
        o, new = _scan_fn(heads(u_ref), heads(w_ref), heads(qg_ref), heads(kd_ref), at_ref[:, 0], eg_ref[:, 0],
                          state[...])
        for h in range(ng):
            o_ref[:, h * HEAD_DIM:(h + 1) * HEAD_DIM] = o[h]
        state[...] = new

    w = ng * HEAD_DIM
    blk = pl.BlockSpec((CHUNK, w), lambda i: (i, 0))
    o, states = pl.pallas_call(
        body, name="gdn_scan_fwd", grid=(nch,),
        in_specs=[blk, blk, blk, blk, pl.BlockSpec((ng, 1, CHUNK, CHUNK), lambda i: (0, i, 0, 0)),
                  pl.BlockSpec((ng, 1, 1, HEAD_DIM), lambda i: (0, i, 0, 0))],
        out_specs=[blk, pl.BlockSpec((ng, 1, HEAD_DIM, HEAD_DIM), lambda i: (0, i, 0, 0))],
        out_shape=[jax.ShapeDtypeStruct((t, w), F32),
                   jax.ShapeDtypeStruct((ng, nch, HEAD_DIM, HEAD_DIM), F32)],
        scratch_shapes=[pltpu.VMEM((ng, HEAD_DIM, HEAD_DIM), F32)],
        compiler_params=_cparams(("arbitrary",)),
    )(*wy)
    return o, (wy, inv, states)


def _wy_batch(q_ref, k_ref, v_ref, gc_ref, gr_ref, bc_ref, ng, cb):
    idx = [(c, h) for c in range(cb) for h in range(ng)]
    rows = lambda c: slice(c * CHUNK, (c + 1) * CHUNK)
    lanes = lambda h: slice(h * HEAD_DIM, (h + 1) * HEAD_DIM)
    wide = lambda ref: jnp.stack([ref[rows(c), lanes(h)] for c, h in idx])
    col = lambda ref: jnp.stack([ref[h, rows(c), :] for c, h in idx])
    return idx, (wide(q_ref), wide(k_ref), wide(v_ref), col(gc_ref), jnp.stack([gr_ref[h, c] for c, h in idx]),
                 col(bc_ref))


def _gdn_wy(qkv, gcol, grow, bcol, ng, cb):
    t = qkv.shape[0]
    nch = t // CHUNK

    def body(q_ref, k_ref, v_ref, gc_ref, gr_ref, bc_ref, u_ref, w_ref, qg_ref, kd_ref, at_ref, eg_ref, inv_ref):
        idx, args = _wy_batch(q_ref, k_ref, v_ref, gc_ref, gr_ref, bc_ref, ng, cb)
        u, w, qg, kd, at, eg, inv = _wy_fn(*args)
        for b, (c, h) in enumerate(idx):
            rows, lanes = slice(c * CHUNK, (c + 1) * CHUNK), slice(h * HEAD_DIM, (h + 1) * HEAD_DIM)
            u_ref[rows, lanes] = u[b]
            w_ref[rows, lanes] = w[b]
            qg_ref[rows, lanes] = qg[b]
            kd_ref[rows, lanes] = kd[b]
            at_ref[h, c] = at[b]
            eg_ref[h, c] = eg[b]
            inv_ref[h, c] = inv[b]

    wd = ng * HEAD_DIM
    blk = lambda o: pl.BlockSpec((cb * CHUNK, wd), lambda i: (i, o))
    col = pl.BlockSpec((ng, cb * CHUNK, 1), lambda i: (0, i, 0))
    sq = pl.BlockSpec((ng, cb, CHUNK, CHUNK), lambda i: (0, i, 0, 0))
    wide = jax.ShapeDtypeStruct((t, wd), F32)
    sq_shape = jax.ShapeDtypeStruct((ng, nch, CHUNK, CHUNK), F32)
    return pl.pallas_call(
        body, name="gdn_wy_fwd", grid=(nch // cb,),
        in_specs=[blk(0), blk(1), blk(2), col, pl.BlockSpec((ng, cb, 1, CHUNK), lambda i: (0, i, 0, 0)), col],
        out_specs=[blk(0), blk(0), blk(0), blk(0), sq, pl.BlockSpec((ng, cb, 1, HEAD_DIM), lambda i: (0, i, 0, 0)),
                   sq],
        out_shape=[wide, wide, wide, wide, sq_shape, jax.ShapeDtypeStruct((ng, nch, 1, HEAD_DIM), F32), sq_shape],
        compiler_params=_cparams(("parallel",)),
    )(qkv, qkv, qkv, gcol, grow, bcol)


def _gdn_bwd(qkv, gcol, grow, bcol, saved, do, ng):
    t = qkv.shape[0]
    nch = t // CHUNK
    cb = GDN_CHUNKS_PER_STEP // 2
    wy, inv, states = saved
    wd = ng * HEAD_DIM

    def scan_body(u_ref, w_ref, qg_ref, kd_ref, at_ref, eg_ref, st_ref, do_ref,
                  du_ref, dw_ref, dqg_ref, dkd_ref, dat_ref, deg_ref, dstate):
        @pl.when(pl.program_id(0) == 0)
        def _():
            dstate[...] = jnp.zeros_like(dstate)

        heads = lambda ref: jnp.stack([ref[:, h * HEAD_DIM:(h + 1) * HEAD_DIM] for h in range(ng)])
        _, vjp = jax.vjp(_scan_fn, heads(u_ref), heads(w_ref), heads(qg_ref), heads(kd_ref), at_ref[:, 0],
                         eg_ref[:, 0], st_ref[:, 0])
        du, dw, dqg, dkd, dat, deg, dst = vjp((heads(do_ref), dstate[...]))
        for h in range(ng):
            lanes = slice(h * HEAD_DIM, (h + 1) * HEAD_DIM)
            du_ref[:, lanes] = du[h]
            dw_ref[:, lanes] = dw[h]
            dqg_ref[:, lanes] = dqg[h]
            dkd_ref[:, lanes] = dkd[h]
        dat_ref[:, 0] = dat
        deg_ref[:, 0] = deg
        dstate[...] = dst

    rev = lambda i: nch - 1 - i
    blk = pl.BlockSpec((CHUNK, wd), lambda i: (rev(i), 0))
    atb = pl.BlockSpec((ng, 1, CHUNK, CHUNK), lambda i: (0, rev(i), 0, 0))
    egb = pl.BlockSpec((ng, 1, 1, HEAD_DIM), lambda i: (0, rev(i), 0, 0))
    wide = jax.ShapeDtypeStruct((t, wd), F32)
    at_shape = jax.ShapeDtypeStruct((ng, nch, CHUNK, CHUNK), F32)
    eg_shape = jax.ShapeDtypeStruct((ng, nch, 1, HEAD_DIM), F32)
    dwy = pl.pallas_call(
        scan_body, name="gdn_scan_bwd", grid=(nch,),
        in_specs=[blk, blk, blk, blk, atb, egb,
                  pl.BlockSpec((ng, 1, HEAD_DIM, HEAD_DIM), lambda i: (0, rev(i), 0, 0)), blk],
        out_specs=[blk, blk, blk, blk, atb, egb],
        out_shape=[wide, wide, wide, wide, at_shape, eg_shape],
        scratch_shapes=[pltpu.VMEM((ng, HEAD_DIM, HEAD_DIM), F32)],
        compiler_params=_cparams(("arbitrary",)),
    )(*wy, states, do)

    def wy_body(q_ref, k_ref, v_ref, gc_ref, gr_ref, bc_ref, du_ref, dw_ref, dqg_ref, dkd_ref, dat_ref, deg_ref,
                inv_ref, dq_ref, dk_ref, dv_ref, dgc_ref, dgr_ref, dbc_ref):
        idx, args = _wy_batch(q_ref, k_ref, v_ref, gc_ref, gr_ref, bc_ref, ng, cb)
        kept = jnp.stack([inv_ref[h, c] for c, h in idx])
        rows = lambda c: slice(c * CHUNK, (c + 1) * CHUNK)
        lanes = lambda h: slice(h * HEAD_DIM, (h + 1) * HEAD_DIM)
        wide_ct = lambda ref: jnp.stack([ref[rows(c), lanes(h)] for c, h in idx])
        cts = (wide_ct(du_ref), wide_ct(dw_ref), wide_ct(dqg_ref), wide_ct(dkd_ref),
               jnp.stack([dat_ref[h, c] for c, h in idx]), jnp.stack([deg_ref[h, c] for c, h in idx]))
        _, vjp = jax.vjp(lambda *a: _wy_fn(*a, inv=kept)[:6], *args)
        dq, dk, dv, dgc, dgr, dbc = vjp(cts)
        for b, (c, h) in enumerate(idx):
            dq_ref[rows(c), lanes(h)] = dq[b]
            dk_ref[rows(c), lanes(h)] = dk[b]
            dv_ref[rows(c), lanes(h)] = dv[b]
            dgc_ref[h, rows(c), :] = dgc[b]
            dgr_ref[h, c] = dgr[b]
            dbc_ref[h, rows(c), :] = dbc[b]

    cblk = lambda o: pl.BlockSpec((cb * CHUNK, wd), lambda i: (i, o))
    col = pl.BlockSpec((ng, cb * CHUNK, 1), lambda i: (0, i, 0))
    rowv = pl.BlockSpec((ng, cb, 1, CHUNK), lambda i: (0, i, 0, 0))
    cshape = jax.ShapeDtypeStruct((ng, t, 1), F32)
    return pl.pallas_call(
        wy_body, name="gdn_wy_bwd", grid=(nch // cb,),
        in_specs=[cblk(0), cblk(1), cblk(2), col, rowv, col, cblk(0), cblk(0), cblk(0), cblk(0),
                  pl.BlockSpec((ng, cb, CHUNK, CHUNK), lambda i: (0, i, 0, 0)),
                  pl.BlockSpec((ng, cb, 1, HEAD_DIM), lambda i: (0, i, 0, 0)),
                  pl.BlockSpec((ng, cb, CHUNK, CHUNK), lambda i: (0, i, 0, 0))],
        out_specs=[cblk(0), cblk(0), cblk(0), col, rowv, col],
        out_shape=[wide, wide, wide, cshape, jax.ShapeDtypeStruct((ng, nch, 1, CHUNK), F32), cshape],
        compiler_params=_cparams(("parallel",)),
    )(qkv, qkv, qkv, gcol, grow, bcol, *dwy, inv)


def _swiglu_fn(gate, up):
    return _silu(gate) * up


FFN_TN = 256


def _ffn_up(n2, wgu4):
    _, d, w = wgu4.shape
    t = n2.shape[0]
    tn = _tile(w, FFN_TN)
    nb = w // tn

    def body(a_ref, b_ref, gu_ref, act_ref):
        av = a_ref[...]
        gate = jnp.dot(av, b_ref[0], preferred_element_type=F32)
        up = jnp.dot(av, b_ref[1], preferred_element_type=F32)
        gu_ref[0] = gate.astype(BF16)
        gu_ref[1] = up.astype(BF16)
        act_ref[...] = _swiglu_fn(gate, up).astype(BF16)

    return pl.pallas_call(
        body, name="ffn_up", grid=(2, nb),
        in_specs=[pl.BlockSpec((t, d), lambda j, l: (0, 0)), pl.BlockSpec((2, d, tn), lambda j, l: (j, 0, l))],
        out_specs=[pl.BlockSpec((2, t, tn), lambda j, l: (j, 0, l)),
                   pl.BlockSpec((t, tn), lambda j, l: (0, j * nb + l))],
        out_shape=[jax.ShapeDtypeStruct((4, t, w), BF16), jax.ShapeDtypeStruct((t, 2 * w), BF16)],
        compiler_params=_cparams(("parallel", "parallel")),
    )(n2, wgu4)


def _ffn_dact(dh2, wd, gu, after):
    _, t, w = gu.shape
    d = dh2.shape[1]
    tn = _tile(w, FFN_TN)
    nb = w // tn

    def body(a_ref, b_ref, gu_ref, _, o_ref):
        dact = lax.dot_general(a_ref[...], b_ref[...], (((1,), (1,)), ((), ())), preferred_element_type=F32)
        _, vjp = jax.vjp(_swiglu_fn, gu_ref[0].astype(F32), gu_ref[1].astype(F32))
        dg, du = vjp(dact)
        o_ref[0] = dg.astype(BF16)
        o_ref[1] = du.astype(BF16)

    pair = pl.BlockSpec((2, t, tn), lambda j, l: (j, 0, l))
    return pl.pallas_call(
        body, name="ffn_dact", grid=(2, nb),
        in_specs=[pl.BlockSpec((t, d), lambda j, l: (0, 0)), pl.BlockSpec((tn, d), lambda j, l: (j * nb + l, 0)),
                  pair, pl.BlockSpec(after.shape, lambda j, l: (0, 0))],
        out_specs=pair, out_shape=jax.ShapeDtypeStruct(gu.shape, BF16),
        compiler_params=_cparams(("parallel", "parallel")),
    )(dh2, wd, gu, after)


def _loss_head(h2, target):
    t, d = h2.shape
    tr = _tile(t, 256, 8)

    def body(h_ref, t_ref, l_ref, d_ref, db_ref):
        @pl.when(pl.program_id(0) == 0)
        def _():
            l_ref[...] = jnp.zeros_like(l_ref)

        err = h_ref[...] - t_ref[...]
        d_ref[...] = err * (1.0 / d)
        db_ref[...] = (err * (1.0 / d)).astype(BF16)
        part = 0.5 * jnp.sum(jnp.mean(err * err, axis=-1, keepdims=True), axis=0, keepdims=True)
        lane = lax.broadcasted_iota(jnp.int32, (8, HEAD_DIM), 1)
        row = lax.broadcasted_iota(jnp.int32, (8, HEAD_DIM), 0)
        l_ref[...] += jnp.where((lane == 0) & (row == 0), part, 0.0)

    blk = pl.BlockSpec((tr, d), lambda r: (r, 0))
    return pl.pallas_call(
        body, name="loss_head", grid=(t // tr,), in_specs=[blk, blk],
        out_specs=[pl.BlockSpec((8, HEAD_DIM), lambda r: (0, 0)), blk, blk],
        out_shape=[jax.ShapeDtypeStruct((8, HEAD_DIM), F32), jax.ShapeDtypeStruct((t, d), F32),
                   jax.ShapeDtypeStruct((t, d), BF16)],
        compiler_params=_cparams(("arbitrary",)),
    )(h2, target)


def _adamw(w, g, m, v, *, g_fn=None, name):
    r, c = w.shape
    tr = _tile(r, max(8, (1 << 19) // c // 8 * 8), 8)
    gs = g if isinstance(g, tuple) else (g,)

    def body(w_ref, *refs):
        g_refs, (m_ref, v_ref, go_ref, d_ref, mo_ref, vo_ref) = refs[:len(gs)], refs[len(gs):]
        gr = g_refs[0][...] if g_fn is None else g_fn(*[ref[...] for ref in g_refs])
        mn = ADAM_B1 * m_ref[...] + (1.0 - ADAM_B1) * gr
        vn = ADAM_B2 * v_ref[...] + (1.0 - ADAM_B2) * (gr * gr)
        m_hat = mn / (1.0 - ADAM_B1 ** ADAM_STEP)
        v_hat = vn / (1.0 - ADAM_B2 ** ADAM_STEP)
        go_ref[...] = gr
        d_ref[...] = -ADAM_LR * (m_hat / (jnp.sqrt(v_hat) + ADAM_EPS) + ADAM_WD * w_ref[...])
        mo_ref[...] = mn
        vo_ref[...] = vn

    blk = pl.BlockSpec((tr, c), lambda i: (i, 0))
    gblks = [pl.BlockSpec((tr, gi.shape[1]), lambda i: (i, 0)) for gi in gs]
    return pl.pallas_call(
        body, name=name, grid=(r // tr,), in_specs=[blk] + gblks + [blk, blk], out_specs=[blk] * 4,
        out_shape=[jax.ShapeDtypeStruct((r, c), F32)] * 4,
        compiler_params=_cparams(("parallel",)),
    )(w, *gs, m, v)


class _Layout:
    def __init__(self, d):
        nh = d // HEAD_DIM
        self.nm = N_MEM_HEADS
        self.nf = (nh - self.nm) // 2
        self.ng = nh - self.nm - self.nf
        nf, ng, nm, hd = self.nf, self.ng, self.nm, HEAD_DIM
        self.o_fq, self.o_fk, self.o_fv, self.o_sm = 0, nf, 2 * nf, 3 * nf
        self.o_gq, self.o_gz, self.o_mq = 0, 3 * ng, 4 * ng
        self.cols_a = -(-(3 * nf + 1) // 4) * 4 * hd
        self.cols_b = -(-(4 * ng + nm) // 4) * 4 * hd
        self.cols = self.cols_a + self.cols_b
        sizes = [nf * hd, nf * hd, nf * hd, nf, 3 * ng * hd, ng * hd, ng, ng, nm * hd]
        starts = [sum(sizes[:i]) for i in range(len(sizes))]
        self.ref = list(zip(starts, sizes))
        self.in_cols = sum(sizes)

    def regroup(self, w):
        part = lambda i: w[:, self.ref[i][0]:self.ref[i][0] + self.ref[i][1]]
        a = [part(0), part(1), part(2), part(3), part(6), part(7)]
        b = [part(4), part(5), part(8)]
        pads = [self.cols_a - sum(p.shape[1] for p in a), self.cols_b - sum(p.shape[1] for p in b)]
        fill = [[jnp.zeros((w.shape[0], n), w.dtype)] if n else [] for n in pads]
        return jnp.concatenate(a + fill[0] + b + fill[1], axis=1)

    def ungroup(self, g):
        hd, nf, ng, nm = HEAD_DIM, self.nf, self.ng, self.nm
        sm, b0 = self.o_sm * hd, self.cols_a
        return jnp.concatenate([
            g[:, :3 * nf * hd], g[:, sm:sm + nf], g[:, b0:b0 + 3 * ng * hd],
            g[:, b0 + self.o_gz * hd:b0 + self.o_mq * hd], g[:, sm + nf:sm + nf + ng],
            g[:, sm + nf + ng:sm + nf + 2 * ng], g[:, b0 + self.o_mq * hd:b0 + (self.o_mq + nm) * hd]], axis=1)


def _lane_row(pieces):
    row = jnp.zeros((1, HEAD_DIM), F32)
    for off, a in pieces:
        row = lax.dynamic_update_slice(row, a.astype(F32), (0, off))
    return row


def _local_step(x, mem, target, prefetch, weights, reducer, sp):
    t, d = x.shape
    lay = _Layout(d)
    nf, ng, nm, hd = lay.nf, lay.ng, lay.nm, HEAD_DIM
    nch = t // CHUNK
    tq = _tile(t, 256)
    tk = tq

    u = _norm_fwd(x, 0, sp["norm_mix"], 1, d, BF16, name="norm_mix_fwd")
    prefetch("in_a", u)
    (win_a,) = weights("in_a", u)
    p_a = _mm(u, win_a, name="mm_in_a")
    pa = _lane_row([(nf, sp["gdn_a_log"])])
    pb = _lane_row([(0, sp["fox_f_bias"]), (nf, sp["gdn_dt_bias"])])
    vals, csum = _small_fwd(p_a, lay.o_sm, pa, pb, nf, ng)

    c_t = csum[:, :nf].T
    cc, cr = c_t.reshape(nf, t, 1), c_t.reshape(nf, t // tk, 1, tk)
    fq = _norm_fwd(p_a, lay.o_fq, sp["fox_q_norm"], nf, hd, BF16, name="fox_qnorm_fwd")
    fk = _norm_fwd(p_a, lay.o_fk, sp["fox_k_norm"], nf, hd, BF16, name="fox_knorm_fwd")
    fv = p_a[:, lay.o_fv * hd:(lay.o_fv + nf) * hd].astype(BF16)
    o_fox, lse, mix = _fox_fwd(fq, fk, fv, cc, cr, nf, tq, tk, d)

    prefetch("in_b", lse)
    (win_b,) = weights("in_b", lse)
    prefetch("mixer", win_b)
    p = _mm(u, win_b, name="mm_in_b")
    wmkv, conv_taps = weights("mixer", p)
    sp = dict(sp, gdn_conv=conv_taps)
    qkv = _conv_fwd(p, lay.o_gq, sp["gdn_conv"], ng)
    g_t, b_t = vals[:, nf:nf + ng].T, vals[:, nf + ng:nf + 2 * ng].T
    gcol, grow, bcol = g_t.reshape(ng, t, 1), g_t.reshape(ng, nch, 1, CHUNK), b_t.reshape(ng, t, 1)
    o_g, states = _gdn_fwd(qkv, gcol, grow, bcol, ng)
    mix = _norm_fwd(o_g, 0, sp["gdn_out_norm"], ng, hd, BF16, z=p, zoff=lay.o_gz, into=mix, into_off=nf,
                    name="gdn_out_fwd")
    prefetch("out", mix)

    mem_n = _norm_fwd(mem, 0, sp["mem_norm"], 1, d, BF16, name="mem_norm_fwd")
    mkv = _mm(mem_n, wmkv, name="mm_memkv")
    tq_mem = _tile(t, 1024)
    mix = _mem_fwd(p, lay.o_mq, mkv, sp["mem_q_norm"], sp["mem_k_norm"], tq_mem, mix, nf + ng)
    prefetch("gate_up", mix)
    (wout,) = weights("out", mix)
    h1 = _mm(mix, wout, res=x, name="mm_out")
    n2 = _norm_fwd(h1, 0, sp["norm_ffn"], 1, d, BF16, name="norm_ffn_fwd")
    (wgu,) = weights("gate_up", n2)
    wgu4 = wgu.reshape(4, d, -1)
    gu, act = _ffn_up(n2, wgu4)
    prefetch("down", act)
    (wd,) = weights("down", act)
    h2 = _mm(act, wd, res=h1, name="mm_down")
    loss_blk, dh2, dh2_b = _loss_head(h2, target)

    g = {}
    token = reducer.pair("w_down", _mm(act, dh2_b, ta=True, out_dtype=BF16, name="mm_dw_down"))
    dgu = _ffn_dact(dh2_b, wd, gu, token)
    dw_gate_up = _mm(n2, dgu, ta=True, stack="out", out_dtype=BF16, name="mm_dw_gate_up").reshape(wgu.shape)
    token = reducer.pair("w_gate_up", dw_gate_up)
    dn2 = _mm(dgu, wgu4, tb=True, stack="sum", after=token, name="mm_dn2")
    token = reducer.ship("ffn", ["w_down", "w_gate_up"], dn2)
    dh1, g["norm_ffn"] = _norm_bwd(h1, 0, sp["norm_ffn"] + token[0, 0], dn2, 0, 1, d, res=dh2,
                                   name="norm_ffn_bwd")
    token = reducer.pair("w_out", _mm(mix, dh1, ta=True, out_dtype=BF16, name="mm_dw_out"))
    dmix = _mm(dh1, wout, tb=True, after=token, name="mm_dmix")

    dmq, dmk, dmv, g["mem_q_norm"], g["mem_k_norm"] = _mem_bwd(
        p, lay.o_mq, mkv, sp["mem_q_norm"], sp["mem_k_norm"], dmix, nf + ng, tq_mem)
    dmkv = jnp.concatenate([dmk, dmv], axis=1)
    token = reducer.pair("w_mem_kv", _mm(mem_n, dmkv, ta=True, out_dtype=BF16, name="mm_dw_memkv"))
    dmem_n = _mm(dmkv, wmkv, tb=True, after=token, name="mm_dmem")
    token = reducer.ship("mix", ["w_out", "w_mem_kv"], dmem_n)
    _, g["mem_norm"] = _norm_bwd(mem, 0, sp["mem_norm"], dmem_n, 0, 1, d, name="mem_norm_bwd")

    do_g, dgz, g["gdn_out_norm"] = _norm_bwd(o_g, 0, sp["gdn_out_norm"] + token[0, 0], dmix, nf, ng, hd, z=p,
                                             zoff=lay.o_gz, name="gdn_out_bwd")
    dq, dk, dv, dgc, dgr, dbc = _gdn_bwd(qkv, gcol, grow, bcol, states, do_g, ng)
    dgqkv, g["gdn_conv"] = _conv_bwd(p, lay.o_gq, sp["gdn_conv"], (dq, dk, dv), ng)
    dg_t = dgc.reshape(ng, t) + dgr.reshape(ng, t)
    db_t = dbc.reshape(ng, t)

    dfq_n, dfk_n, dfv, dcc, dcr = _fox_bwd(fq, fk, fv, cc, cr, o_fox, lse, dmix, nf, tq, tk)
    dfq, g["fox_q_norm"] = _norm_bwd(p_a, lay.o_fq, sp["fox_q_norm"], dfq_n, 0, nf, hd, name="fox_qnorm_bwd")
    dfk, g["fox_k_norm"] = _norm_bwd(p_a, lay.o_fk, sp["fox_k_norm"], dfk_n, 0, nf, hd, name="fox_knorm_bwd")
    dc_t = dcc.reshape(nf, t) + dcr.reshape(nf, t)

    lanes_left = hd - nf - 2 * ng
    dvals = jnp.concatenate([jnp.zeros((t, nf), F32), dg_t.T, db_t.T, jnp.zeros((t, lanes_left), F32)], axis=1)
    dcsum = jnp.concatenate([dc_t.T, jnp.zeros((t, hd - nf), F32)], axis=1)
    dsm, dpa, dpb = _small_bwd(p_a, lay.o_sm, pa, pb, dvals, dcsum, nf, ng)
    g["fox_f_bias"] = dpb[:, :nf]
    g["gdn_dt_bias"] = dpb[:, nf:nf + ng]
    g["gdn_a_log"] = dpa[:, nf:nf + ng]

    zeros = lambda n: jnp.zeros((t, n), F32)
    dp_a = jnp.concatenate([dfq, dfk, dfv, dsm, zeros(lay.cols_a - (lay.o_sm + 1) * hd)], axis=1).astype(BF16)
    dp_b = jnp.concatenate([dgqkv, dgz, dmq, zeros(lay.cols_b - (lay.o_mq + nm) * hd)], axis=1).astype(BF16)
    token = reducer.start("in", {"w_in_a": _mm(u, dp_a, ta=True, out_dtype=BF16, name="mm_dw_in_a"),
                                 "w_in_b": _mm(u, dp_b, ta=True, out_dtype=BF16, name="mm_dw_in_b")})
    du = _mm(dp_a, win_a, tb=True, after=token, name="mm_du_a")
    du = _mm(dp_b, win_b, tb=True, res=du, name="mm_du_b")
    dx, g["norm_mix"] = _norm_bwd(x, 0, sp["norm_mix"], du, 0, 1, d, res=dh1, name="norm_mix_bwd")
    return loss_blk, dx, g


ANY = pl.BlockSpec(memory_space=pl.ANY)


def _me():
    x, y, c = lax.axis_index("x"), lax.axis_index("y"), lax.axis_index("c")
    chips = [(1 - x, y), (x, 1 - y), (1 - x, 1 - y)]
    return x, y, c, chips


def _slot(axis, k):
    return k if axis == 0 else 2 * (k % 2) + k // 2


def _slab(ref, axis, rows, cols, k, h):
    half = rows // 2
    return ref.at[pl.ds(_slot(axis, k) * rows + h * half, half), :]


def _remote(src, dst, send_sem, recv_sem, dev):
    return pltpu.make_async_remote_copy(src_ref=src, dst_ref=dst, send_sem=send_sem, recv_sem=recv_sem,
                                        device_id=dev, device_id_type=MESH)


HBM = pl.BlockSpec(memory_space=pltpu.HBM)
SEM = pl.BlockSpec(memory_space=pltpu.SEMAPHORE)
SPLIT = pltpu.CompilerParams(has_side_effects=pltpu.SideEffectType.DATAFLOW_SIDE_EFFECTING)
TOKEN = jax.ShapeDtypeStruct((8, HEAD_DIM), F32)


def _in_hbm(v):
    return pltpu.with_memory_space_constraint(v, pltpu.HBM)


def _cast_place(shard, axis, name, col_fn=None, out_cols=None, after=None):
    r, c = shard.shape
    oc = out_cols or c
    tr = _tile(r, 512 if col_fn is None else 64, 16)
    tc = _tile(c, 2048) if col_fn is None else c
    otc = tc if col_fn is None else oc
    nb = r // tr
    chip = 2 * lax.axis_index("x") + lax.axis_index("y")
    slot = jnp.reshape(_slot(axis, chip), (1,)).astype(jnp.int32)

    def body(slot_ref, x_ref, *rest):
        x = x_ref[...]
        rest[-1][...] = (x if col_fn is None else col_fn(x)).astype(BF16)

    extra = [] if after is None else [after]
    return pl.pallas_call(
        body, name=name,
        grid_spec=pltpu.PrefetchScalarGridSpec(
            num_scalar_prefetch=1, grid=(nb, c // tc),
            in_specs=[pl.BlockSpec((tr, tc), lambda i, l, s: (i, l))] + [ANY] * len(extra),
            out_specs=pl.BlockSpec((tr, otc), lambda i, l, s: (s[0] * nb + i, l))),
        out_shape=jax.ShapeDtypeStruct((4 * r, oc), BF16),
        compiler_params=_cparams(("parallel", "parallel")),
    )(slot, shard, *extra)


def _gather_start(bufs, axes, shapes, groups, name):
    n = len(bufs)

    def body(*refs):
        dst = refs[n:2 * n]
        sems = refs[2 * n:2 * n + 2 * len(groups)]
        token = refs[-1]
        x, y, c, chips = _me()
        k = 2 * x + y
        for gi, ws in enumerate(groups):
            for i, w in enumerate(ws):
                r, cl = shapes[w]
                place = _slab(dst[w], axes[w], r, cl, k, c)
                for j, (px, py) in enumerate(chips):
                    _remote(place, place, sems[2 * gi].at[3 * i + j], sems[2 * gi + 1].at[3 * i + j],
                            (px, py, c)).start()
        token[...] = jnp.zeros_like(token)

    sem_shapes = [pltpu.SemaphoreType.DMA((3 * len(ws),)) for ws in groups for _ in range(2)]
    outs = pl.pallas_call(
        body, name=name, in_specs=[HBM] * n,
        out_specs=[HBM] * n + [SEM] * len(sem_shapes) + [pl.BlockSpec(memory_space=pltpu.VMEM)],
        out_shape=[pltpu.HBM(b.shape, b.dtype) for b in bufs] + sem_shapes + [TOKEN],
        input_output_aliases={w: w for w in range(n)}, compiler_params=SPLIT,
    )(*[_in_hbm(b) for b in bufs])
    sems = outs[n:-1]
    return outs[:n], [(sems[2 * g], sems[2 * g + 1]) for g in range(len(groups))], outs[-1]


def _gather_wait(bufs, axes, shapes, sems, after, name):
    n = len(bufs)

    def body(*refs):
        send_sems, recv_sems = refs[n], refs[n + 1]
        dst = refs[n + 3:]
        x, y, c, chips = _me()
        k = 2 * x + y
        for i in range(n):
            r, cl = shapes[i]
            for j, (px, py) in enumerate(chips):
                got = _slab(dst[i], axes[i], r, cl, 2 * px + py, c)
                _remote(got, got, send_sems.at[3 * i + j], recv_sems.at[3 * i + j], (px, py, c)).wait_recv()
        for i in range(n):
            r, cl = shapes[i]
            mine = _slab(dst[i], axes[i], r, cl, k, c)
            for j, (px, py) in enumerate(chips):
                _remote(mine, mine, send_sems.at[3 * i + j], recv_sems.at[3 * i + j], (px, py, c)).wait_send()

    return pl.pallas_call(
        body, name=name, in_specs=[HBM] * n + [SEM, SEM, ANY], out_specs=[HBM] * n,
        out_shape=[pltpu.HBM(b.shape, b.dtype) for b in bufs],
        input_output_aliases={i: i for i in range(n)}, compiler_params=SPLIT,
    )(*bufs, sems[0], sems[1], after)


def _gather_forward(bufs, axes, shapes, name):
    n = len(bufs)

    def body(*refs):
        dst = refs[n:2 * n]
        send_sems, recv_sems = refs[2 * n:]
        x, y, c, chips = _me()
        sibling = (x, y, 1 - c)
        sends = []
        for i in range(n):
            r, cl = shapes[i]
            for j, (px, py) in enumerate(chips):
                got = _slab(dst[i], axes[i], r, cl, 2 * px + py, c)
                cp = _remote(got, got, send_sems.at[3 * i + j], recv_sems.at[3 * i + j], sibling)
                cp.start()
                sends.append(cp)
        for i in range(n):
            r, cl = shapes[i]
            for j, (px, py) in enumerate(chips):
                got = _slab(dst[i], axes[i], r, cl, 2 * px + py, 1 - c)
                _remote(got, got, send_sems.at[3 * i + j], recv_sems.at[3 * i + j], sibling).wait_recv()
        for cp in sends:
            cp.wait_send()

    return pl.pallas_call(
        body, name=name, in_specs=[ANY] * n, out_specs=[ANY] * n,
        out_shape=[jax.ShapeDtypeStruct(b.shape, b.dtype) for b in bufs],
        input_output_aliases={i: i for i in range(n)},
        scratch_shapes=[pltpu.SemaphoreType.DMA((3 * n,)), pltpu.SemaphoreType.DMA((3 * n,))],
    )(*bufs)


def _split_start(name, arrays, geometry, count):
    n = len(arrays)

    def body(*refs):
        send, recv, token = refs[2 * n:]
        for i, (src, dst, _, dev) in enumerate(geometry(refs[n:2 * n])):
            _remote(src, dst, send.at[i], recv.at[i], dev).start()
        token[...] = jnp.zeros_like(token)

    sem = pltpu.SemaphoreType.DMA((count,))
    outs = pl.pallas_call(
        body, name=name, in_specs=[HBM] * n,
        out_specs=[HBM] * n + [SEM, SEM, pl.BlockSpec(memory_space=pltpu.VMEM)],
        out_shape=[pltpu.HBM(v.shape, v.dtype) for v in arrays] + [sem, sem, TOKEN],
        input_output_aliases={i: i for i in range(n)}, compiler_params=SPLIT,
    )(*[_in_hbm(v) for v in arrays])
    return list(outs[:n]), (outs[n], outs[n + 1]), outs[-1]


def _split_wait(name, arrays, sems, after, geometry):
    n = len(arrays)

    def body(*refs):
        send, recv = refs[n], refs[n + 1]
        copies = geometry(refs[n + 3:])
        for i, (_, _, land, dev) in enumerate(copies):
            _remote(land, land, send.at[i], recv.at[i], dev).wait_recv()
        for i, (src, _, _, dev) in enumerate(copies):
            _remote(src, src, send.at[i], recv.at[i], dev).wait_send()

    return list(pl.pallas_call(
        body, name=name, in_specs=[HBM] * n + [SEM, SEM, ANY], out_specs=[HBM] * n,
        out_shape=[pltpu.HBM(v.shape, v.dtype) for v in arrays],
        input_output_aliases={i: i for i in range(n)}, compiler_params=SPLIT,
    )(*arrays, sems[0], sems[1], after))


def _forward_geometry(axes, shapes):
    def geometry(bufs):
        x, y, c, chips = _me()
        out = []
        for i, buf in enumerate(bufs):
            r, cl = shapes[i]
            for px, py in chips:
                got = _slab(buf, axes[i], r, cl, 2 * px + py, c)
                out.append((got, got, _slab(buf, axes[i], r, cl, 2 * px + py, 1 - c), (x, y, 1 - c)))
        return out
    return geometry


def _pair_geometry(axes, shapes):
    def geometry(refs):
        n = len(refs) // 2
        x, y, c, _ = _me()
        out = []
        for w in range(n):
            r, cl = shapes[w]
            for j in range(4):
                land = refs[n + w].at[j]
                out.append((_slab(refs[w], axes[w], r, cl, j, 1 - c), land, land, (x, y, 1 - c)))
        return out
    return geometry


def _pair_exchange(fulls, axes, shapes, tag):
    n = len(fulls)

    def body(*refs):
        src, dst = refs[:n], refs[n:2 * n]
        send_sems, recv_sems = refs[2 * n:]
        x, y, c, _ = _me()
        sibling = (x, y, 1 - c)
        cps = []
        for w in range(n):
            r, cl = shapes[w]
            for j in range(4):
                cp = _remote(_slab(src[w], axes[w], r, cl, j, 1 - c), dst[w].at[j],
                             send_sems.at[4 * w + j], recv_sems.at[4 * w + j], sibling)
                cp.start()
                cps.append(cp)
        for cp in cps:
            cp.wait()

    out_shape = [jax.ShapeDtypeStruct((4, r // 2, cl), f.dtype) for (r, cl), f in zip(shapes, fulls)]
    return pl.pallas_call(
        body, name="reduce_pair_exchange_" + tag, in_specs=[ANY] * n, out_specs=[ANY] * n, out_shape=out_shape,
        scratch_shapes=[pltpu.SemaphoreType.DMA((4 * n,)), pltpu.SemaphoreType.DMA((4 * n,))],
    )(*fulls)


def _chip_start(parts, tag):
    n = len(parts)

    def body(*refs):
        src, land = refs[2 * n:3 * n], refs[3 * n:4 * n]
        send_sems, recv_sems, token = refs[4 * n:]
        x, y, c, chips = _me()
        k = 2 * x + y
        for w in range(n):
            for j, (px, py) in enumerate(chips):
                _remote(src[w].at[2 * px + py], land[w].at[k], send_sems.at[3 * w + j], recv_sems.at[3 * w + j],
                        (px, py, c)).start()
        token[...] = jnp.zeros_like(token)

    lands = [lax.empty(p.shape, p.dtype) for p in parts]
    sem = pltpu.SemaphoreType.DMA((3 * n,))
    outs = pl.pallas_call(
        body, name="reduce_ici_start_" + tag, in_specs=[HBM] * (2 * n),
        out_specs=[HBM] * (2 * n) + [SEM, SEM, pl.BlockSpec(memory_space=pltpu.VMEM)],
        out_shape=[pltpu.HBM(p.shape, p.dtype) for p in parts + lands] + [sem, sem, TOKEN],
        input_output_aliases={i: i for i in range(2 * n)}, compiler_params=SPLIT,
    )(*[_in_hbm(v) for v in parts + lands])
    return outs[:n], outs[n:2 * n], outs[2 * n], outs[2 * n + 1], outs[-1]


def _chip_wait(parts, lands, send_sems, recv_sems, after, tag):
    n = len(parts)

    def body(*refs):
        send, recv = refs[2 * n], refs[2 * n + 1]
        src, land = refs[2 * n + 3:3 * n + 3], refs[3 * n + 3:]
        x, y, c, chips = _me()
        for w in range(n):
            for j, (px, py) in enumerate(chips):
                got = land[w].at[2 * px + py]
                _remote(got, got, send.at[3 * w + j], recv.at[3 * w + j], (px, py, c)).wait_recv()
        for w in range(n):
            for j, (px, py) in enumerate(chips):
                sent = src[w].at[2 * px + py]
                _remote(sent, sent, send.at[3 * w + j], recv.at[3 * w + j], (px, py, c)).wait_send()

    outs = pl.pallas_call(
        body, name="reduce_ici_wait_" + tag, in_specs=[HBM] * (2 * n) + [SEM, SEM, ANY], out_specs=[HBM] * (2 * n),
        out_shape=[pltpu.HBM(p.shape, p.dtype) for p in parts + lands],
        input_output_aliases={i: i for i in range(2 * n)}, compiler_params=SPLIT,
    )(*parts, *lands, send_sems, recv_sems, after)
    chip = 2 * lax.axis_index("x") + lax.axis_index("y")
    return [lax.dynamic_update_slice(s, lax.dynamic_index_in_dim(p, chip, 0, keepdims=True), (chip, 0, 0))
            for p, s in zip(outs[:n], outs[n:])]


def _half_swap(halves, tag):
    n = len(halves)
    core = lax.axis_index("c")
    bufs = [lax.dynamic_update_slice(lax.empty((2,) + h.shape, h.dtype), h[None], (core, 0, 0)) for h in halves]

    def body(*refs):
        dst = refs[n:2 * n]
        send_sems, recv_sems = refs[2 * n:]
        x, y, c, _ = _me()
        sibling = (x, y, 1 - c)
        cps = []
        for w in range(n):
            cp = _remote(dst[w].at[c], dst[w].at[c], send_sems.at[w], recv_sems.at[w], sibling)
            cp.start()
            cps.append(cp)
        for w in range(n):
            other = dst[w].at[1 - c]
            _remote(other, other, send_sems.at[w], recv_sems.at[w], sibling).wait_recv()
        for cp in cps:
            cp.wait_send()

    outs = pl.pallas_call(
        body, name="reduce_half_swap_" + tag, in_specs=[ANY] * n, out_specs=[ANY] * n,
        out_shape=[jax.ShapeDtypeStruct(b.shape, b.dtype) for b in bufs],
        input_output_aliases={w: w for w in range(n)},
        scratch_shapes=[pltpu.SemaphoreType.DMA((n,)), pltpu.SemaphoreType.DMA((n,))],
    )(*bufs)
    return [o.reshape(2 * o.shape[1], o.shape[2]) for o in outs]


def _add_parts(full, axis, rows, sib, name):
    _, r, c = sib.shape
    tr, tc = _tile(r, 256, 16), _tile(c, 2048)
    nb = r // tr
    core = jnp.reshape(lax.axis_index("c"), (1,)).astype(jnp.int32)

    def body(c_ref, a_ref, b_ref, o_ref):
        o_ref[0] = (a_ref[...].astype(F32) + b_ref[0].astype(F32)).astype(BF16)

    blk = pl.BlockSpec((1, tr, tc), lambda j, i, l, cr: (j, i, l))
    return pl.pallas_call(
        body, name=name,
        grid_spec=pltpu.PrefetchScalarGridSpec(
            num_scalar_prefetch=1, grid=(4, nb, c // tc),
            in_specs=[pl.BlockSpec((tr, tc), lambda j, i, l, cr: ((_slot(axis, j) * 2 + cr[0]) * nb + i, l)), blk],
            out_specs=blk),
        out_shape=jax.ShapeDtypeStruct(sib.shape, BF16),
        compiler_params=_cparams(("parallel", "parallel", "parallel")),
    )(core, full, sib)


def _sum_slots(a, name):
    _, r, c = a.shape
    tr, tc = _tile(r, 256, 8), _tile(c, 2048)

    def body(a_ref, o_ref):
        v = a_ref[...].astype(F32)
        o_ref[...] = ((v[0] + v[1]) + v[2]) + v[3]

    return pl.pallas_call(
        body, name=name, grid=(r // tr, c // tc),
        in_specs=[pl.BlockSpec((4, tr, tc), lambda i, l: (0, i, l))],
        out_specs=pl.BlockSpec((tr, tc), lambda i, l: (i, l)),
        out_shape=jax.ShapeDtypeStruct((r, c), F32),
        compiler_params=_cparams(("parallel", "parallel")),
    )(a)


class _Reducer:
    def __init__(self, spec):
        self.spec = spec
        self.paired = {}
        self.pending = []

    def pair(self, name, full):
        ax, shp = self.spec[name]
        land = lax.empty((4, shp[0] // 2, shp[1]), full.dtype)
        arrays, sems, token = _split_start("reduce_pair_start_" + name, [full, land], _pair_geometry([ax], [shp]), 4)
        self.paired[name] = (arrays, sems)
        return token

    def ship(self, tag, names, after):
        parts = []
        for n in names:
            ax, shp = self.spec[n]
            arrays, sems = self.paired.pop(n)
            full, sib = _split_wait("reduce_pair_wait_" + n, arrays, sems, after, _pair_geometry([ax], [shp]))
            parts.append(_add_parts(full, ax, shp[0], sib, name=f"reduce_add_{n}"))
        parts, lands, send, recv, token = _chip_start(parts, tag)
        self.pending.append((tag, names, parts, lands, send, recv))
        return token

    def start(self, tag, grads):
        names = list(grads)
        fulls, axes = [grads[n] for n in names], [self.spec[n][0] for n in names]
        shapes = [self.spec[n][1] for n in names]
        from_sibling = _pair_exchange(fulls, axes, shapes, tag)
        parts = [_add_parts(f, a, r, s, name=f"reduce_add_{n}")
                 for n, f, a, (r, cl), s in zip(names, fulls, axes, shapes, from_sibling)]
        parts, lands, send, recv, token = _chip_start(parts, tag)
        self.pending.append((tag, names, parts, lands, send, recv))
        return token

    def finish(self, after, tags):
        out = {}
        for tag, names, parts, lands, send, recv in [p for p in self.pending if p[0] in tags]:
            slots = _chip_wait(parts, lands, send, recv, after, tag)
            halves = [_sum_slots(s, name=f"reduce_sum_{n}") for n, s in zip(names, slots)]
            out.update(zip(names, _half_swap(halves, tag)))
        return out


def _allreduce_small(pack, after):
    rows = pack.shape[0]

    def body(p_ref, _, o_ref, slots, send_sems, recv_sems):
        x, y, c, _ = _me()
        me = 4 * x + 2 * y + c
        slots[me] = p_ref[...]
        cps = []
        for r in range(1, 8):
            peer = (x ^ (r >> 2), y ^ ((r >> 1) & 1), c ^ (r & 1))
            cp = _remote(p_ref, slots.at[me], send_sems.at[r - 1], recv_sems.at[r - 1], peer)
            cp.start()
            cps.append(cp)
        for r in range(1, 8):
            frm = me ^ r
            _remote(slots.at[frm], slots.at[frm], send_sems.at[r - 1], recv_sems.at[r - 1], (x, y, c)).wait_recv()
        for cp in cps:
            cp.wait_send()
        acc = slots[0]
        for s in range(1, 8):
            acc = acc + slots[s]
        o_ref[...] = acc

    vm = pl.BlockSpec(memory_space=pltpu.VMEM)
    return pl.pallas_call(
        body, name="allreduce_small", in_specs=[vm, ANY], out_specs=vm,
        out_shape=jax.ShapeDtypeStruct(pack.shape, F32),
        scratch_shapes=[pltpu.VMEM((8, rows, HEAD_DIM), F32), pltpu.SemaphoreType.DMA((7,)),
                        pltpu.SemaphoreType.DMA((7,))],
    )(pack, after)


_ROWS = ["norm_mix", "norm_ffn", "mem_norm", "fox_q_norm", "fox_k_norm", "gdn_out_norm", "mem_q_norm",
         "mem_k_norm", "fox_f_bias", "gdn_a_log", "gdn_dt_bias"]


def _pack_rows(vals):
    out = []
    for name in _ROWS:
        v = vals[name].reshape(-1)
        n = -(-v.shape[0] // HEAD_DIM) * HEAD_DIM
        out.append(jnp.pad(v, (0, n - v.shape[0])).reshape(-1, HEAD_DIM))
    return jnp.concatenate(out, axis=0)


def _unpack_rows(pack, like):
    out, r = {}, 0
    for name in _ROWS:
        n = like[name].shape[-1]
        nr = -(-n // HEAD_DIM)
        out[name] = pack[r:r + nr].reshape(1, -1)[:, :n]
        r += nr
    return out, r


def kernel(x, mem, norm_mix, w_in, fox_f_bias, fox_q_norm, fox_k_norm, gdn_conv, gdn_a_log, gdn_dt_bias, gdn_out_norm, mem_norm, w_mem_kv, mem_q_norm, mem_k_norm, w_out, norm_ffn, w_gate_up, w_down, loss_target, m_norm_mix, m_w_in, m_fox_f_bias, m_fox_q_norm, m_fox_k_norm, m_gdn_conv, m_gdn_a_log, m_gdn_dt_bias, m_gdn_out_norm, m_mem_norm, m_w_mem_kv, m_mem_q_norm, m_mem_k_norm, m_w_out, m_norm_ffn, m_w_gate_up, m_w_down, v_norm_mix, v_w_in, v_fox_f_bias, v_fox_q_norm, v_fox_k_norm, v_gdn_conv, v_gdn_a_log, v_gdn_dt_bias, v_gdn_out_norm, v_mem_norm, v_w_mem_kv, v_mem_q_norm, v_mem_k_norm, v_w_out, v_norm_ffn, v_w_gate_up, v_w_down):
    a = dict(locals())
    d = x.shape[-1]
    lay = _Layout(d)
    chip = 2 * lax.axis_index("x") + lax.axis_index("y")
    small = {n: a[n] for n in _ROWS}
    big = ["w_in", "w_mem_kv", "w_out", "w_gate_up", "w_down"]
    axes = [0, 0, 0, 1, 0]

    conv_cols = gdn_conv.shape[-1]
    conv_n = CONV_WIDTH * conv_cols
    conv_rows = -(-conv_n // HEAD_DIM)
    conv_blk = jnp.pad(gdn_conv.reshape(-1), (0, 32 * HEAD_DIM - conv_n)).reshape(32, HEAD_DIM)
    axis_of = dict(zip(big, axes), conv=0, w_in_a=0, w_in_b=0)
    shape_of = {n: a[n].shape[1:] for n in big[1:]}
    shape_of.update(w_in_a=(w_in.shape[1], lay.cols_a), w_in_b=(w_in.shape[1], lay.cols_b), conv=conv_blk.shape)
    placed = {"w_in_a": _cast_place(w_in[0], 0, "cast_w_in_a", lambda v: lay.regroup(v)[:, :lay.cols_a], lay.cols_a),
              "conv": lax.dynamic_update_slice(lax.empty((4 * 32, HEAD_DIM), F32), conv_blk, (chip * 32, 0))}
    grouped = {"in_a": ["w_in_a"], "in_b": ["w_in_b"], "mixer": ["w_mem_kv", "conv"], "out": ["w_out"],
               "gate_up": ["w_gate_up"], "down": ["w_down"]}
    inflight = {}

    def start(tags, name):
        names = [n for t in tags for n in grouped[t]]
        bufs, sems, token = _gather_start([placed[n] for n in names], [axis_of[n] for n in names],
                                          [shape_of[n] for n in names],
                                          [[names.index(n) for n in grouped[t]] for t in tags], name)
        for t, pair in zip(tags, sems):
            inflight[t] = ([bufs[names.index(n)] for n in grouped[t]], pair)
        return token

    first = start(["in_a"], "gather_ici_start_in")
    placed["w_in_b"] = _cast_place(w_in[0], 0, "cast_w_in_b", lambda v: lay.regroup(v)[:, lay.cols_a:], lay.cols_b,
                                   after=first)
    placed.update({n: _cast_place(a[n][0], axis_of[n], "cast_" + n, after=first) for n in big[1:]})
    all_started = start(["in_b", "mixer", "out", "gate_up", "down"], "gather_ici_start_rest")

    forwarding = {}

    def prefetch(tag, after):
        bufs, sem_pair = inflight.pop(tag)
        ax, shp = [axis_of[n] for n in grouped[tag]], [shape_of[n] for n in grouped[tag]]
        got = _gather_wait(bufs, ax, shp, sem_pair, all_started if tag == "in_a" else after,
                           "gather_ici_wait_" + tag)
        geometry = _forward_geometry(ax, shp)
        got, sems, _ = _split_start("gather_forward_start_" + tag, got, geometry, 3 * len(got))
        forwarding[tag] = (got, sems, geometry)

    def weights(tag, after):
        got, sems, geometry = forwarding.pop(tag)
        got = _split_wait("gather_forward_wait_" + tag, got, sems, after, geometry)
        if tag != "mixer":
            return got
        taps = got[1].reshape(4, 32 * HEAD_DIM)[:, :conv_n].reshape(4, CONV_WIDTH, conv_cols)
        return got[0], jnp.transpose(taps, (1, 0, 2)).reshape(CONV_WIDTH, 4 * conv_cols)

    sp = dict(small)
    reducer = _Reducer({n: (axis_of[n], shape_of[n]) for n in big[1:] + ["w_in_a", "w_in_b"]})
    loss_blk, dx, g = _local_step(x[0], mem[0], loss_target[0], prefetch, weights, reducer, sp)

    gsmall = {n: g[n] for n in _ROWS}
    pack = jnp.concatenate([_pack_rows(gsmall), g["gdn_conv"].reshape(-1, HEAD_DIM), loss_blk], axis=0)
    pack = jnp.pad(pack, ((0, -pack.shape[0] % 8), (0, 0)))
    out = {"grad_x": dx[None]}

    def adamw_shards(reduced):
        if "w_in_a" in reduced:
            reduced = {"w_in": (reduced["w_in_a"], reduced["w_in_b"])}
        for n, gsh in reduced.items():
            join = (lambda ga, gb: lay.ungroup(jnp.concatenate([ga, gb], axis=1))) if n == "w_in" else None
            res = _adamw(a[n][0], gsh, a["m_" + n][0], a["v_" + n][0], g_fn=join, name="adamw_" + n)
            for pre, r in zip(["grad_", "delta_", "new_m_", "new_v_"], res):
                out[pre + n] = r[None]
        return res[0]

    done = adamw_shards(reducer.finish(dx, ("ffn", "mix")))
    tot = _allreduce_small(pack, done)
    gs, r0 = _unpack_rows(tot, small)
    conv_g = tot[r0:r0 + CONV_WIDTH * 4 * conv_cols // HEAD_DIM].reshape(CONV_WIDTH, 4 * conv_cols)
    gs_conv = lax.dynamic_slice_in_dim(conv_g, chip * conv_cols, conv_cols, axis=1)
    out["loss"] = tot[r0 + CONV_WIDTH * 4 * conv_cols // HEAD_DIM, 0]
    adamw_shards(reducer.finish(tot, ("in",)))
    conv_pad = lambda v: jnp.pad(v.reshape(-1), (0, conv_rows * HEAD_DIM - conv_n)).reshape(conv_rows, HEAD_DIM)
    packs = []
    for src, cv in [(small, gdn_conv), (gs, gs_conv), ({n: a["m_" + n] for n in _ROWS}, m_gdn_conv),
                    ({n: a["v_" + n] for n in _ROWS}, v_gdn_conv)]:
        packs.append(jnp.concatenate([_pack_rows(src), conv_pad(cv)], axis=0))
    res = _adamw(*packs, name="adamw_small")
    for pre, r in zip(["grad_", "delta_", "new_m_", "new_v_"], res):
        vals, r1 = _unpack_rows(r, small)
        for n in _ROWS:
            out[pre + n] = vals[n]
        out[pre + "gdn_conv"] = r[r1:r1 + conv_rows].reshape(-1)[:conv_n].reshape(gdn_conv.shape)
    names = ["norm_mix", "w_in", "fox_f_bias", "fox_q_norm", "fox_k_norm", "gdn_conv", "gdn_a_log", "gdn_dt_bias",
             "gdn_out_norm", "mem_norm", "w_mem_kv", "mem_q_norm", "mem_k_norm", "w_out", "norm_ffn", "w_gate_up",
             "w_down"]
    return (out["loss"], out["grad_x"], *[out[p + n] for p in ["grad_", "delta_", "new_m_", "new_v_"] for n in names])
```

```python
import functools
import math

import jax
import jax.numpy as jnp
from jax import lax
from jax.experimental import pallas as pl
from jax.experimental.pallas import tpu as pltpu

F32, BF16 = jnp.float32, jnp.bfloat16
HEAD_DIM = 128
CHUNK = 64
N_MEM_HEADS = 4
CONV_WIDTH = 4
NORM_EPS = 1e-6
ADAM_LR, ADAM_B1, ADAM_B2, ADAM_EPS, ADAM_WD, ADAM_STEP = 0.001, 0.9, 0.999, 1e-08, 0.01, 10
VMEM_LIMIT = 48 * 1024 * 1024
NEG = -1e30
MESH = pl.DeviceIdType.MESH


def _cparams(sem=None, **kw):
    if sem is not None:
        kw["dimension_semantics"] = sem
    return pltpu.CompilerParams(vmem_limit_bytes=VMEM_LIMIT, **kw)


def _tile(n, target, mult=128):
    best = None
    d = mult
    while d <= min(n, target):
        if n % d == 0:
            best = d
        d += mult
    return best if best is not None else n


def _dot(a, b, dims, hi):
    if a.ndim == 3:
        dn = (((dims[0][0] + 1,), (dims[1][0] + 1,)), ((0,), (0,)))
    else:
        dn = (dims, ((), ()))
    if hi is not None:
        return lax.dot_general(a, b, dn, precision=hi, preferred_element_type=F32)
    return lax.dot_general(a.astype(BF16), b.astype(BF16), dn, preferred_element_type=F32)


def _make_dots(hi, cotangent=None):
    @jax.custom_vjp
    def nn(a, b):
        return _dot(a, b, ((1,), (0,)), hi)

    @jax.custom_vjp
    def nt(a, b):
        return _dot(a, b, ((1,), (1,)), hi)

    @jax.custom_vjp
    def tn(a, b):
        return _dot(a, b, ((0,), (0,)), hi)

    bnn, bnt, btn = cotangent or (nn, nt, tn)
    nn.defvjp(lambda a, b: (nn(a, b), (a, b)), lambda r, g: (bnt(g, r[1]), btn(r[0], g)))
    nt.defvjp(lambda a, b: (nt(a, b), (a, b)), lambda r, g: (bnn(g, r[1]), btn(g, r[0])))
    tn.defvjp(lambda a, b: (tn(a, b), (a, b)), lambda r, g: (bnt(r[1], g), bnn(r[0], g)))
    return nn, nt, tn


_nn, _nt, _tn = _make_dots(None)
_nn_hi, _nt_hi, _tn_hi = _make_dots(lax.Precision.HIGHEST)
_nn_x3, _nt_x3, _tn_x3 = _make_dots(lax.Precision.HIGH, (_nn, _nt, _tn))


def _sigmoid(x):
    return jax.nn.sigmoid(x)


@jax.custom_vjp
def _softplus(x):
    return jnp.maximum(x, 0.0) + jnp.log(1.0 + jnp.exp(-jnp.abs(x)))


_softplus.defvjp(lambda x: (_softplus(x), x), lambda x, g: (g * _sigmoid(x),))


def _silu(x):
    return x * _sigmoid(x)


def _rms_fn(x, gain, z=None):
    y = x * lax.rsqrt(jnp.mean(x * x, axis=-1, keepdims=True) + NORM_EPS) * gain
    if z is not None:
        y = y * _silu(z)
    return y


def _mm(a, b, *, ta=False, tb=False, out_dtype=F32, res=None, stack=None, after=None, name):
    a2, b2 = a.shape[-2:], b.shape[-2:]
    ns = b.shape[0] if stack else 1
    m = a2[1] if ta else a2[0]
    k = a2[0] if ta else a2[1]
    n = b2[0] if tb else b2[1]
    assert k == (b2[1] if tb else b2[0])
    tm, tn, tk = _mm_tiles(m, n, k, ns if stack == "sum" else 1, a.dtype.itemsize, b.dtype.itemsize,
                           jnp.dtype(out_dtype).itemsize, res is not None)
    nk = k // tk
    single = nk == 1 and stack != "sum"
    dims = ((0 if ta else 1,), (1 if tb else 0,))
    if stack == "sum":
        order = lambda g0, g1, g2, g3: (g2, g0, g1, g3)
        grid = (m // tm, n // tn, ns, nk)
    else:
        order = lambda g0, g1, g2, g3: (g0, g1, g2, g3)
        grid = (ns, m // tm, n // tn, nk)

    def body(*refs):
        if after is not None:
            refs = refs[:2 + (res is not None)] + refs[3 + (res is not None):]
        if single:
            a_ref, b_ref = refs[:2]
            r = lax.dot_general(a_ref[...].astype(BF16), b_ref[...].astype(BF16), (dims, ((), ())),
                                preferred_element_type=F32)
            if res is not None:
                r = r + refs[2][...]
            refs[-1][...] = r.astype(out_dtype)
            return
        if res is None:
            a_ref, b_ref, o_ref, acc = refs
        else:
            a_ref, b_ref, r_ref, o_ref, acc = refs
        s, _, _, kk = order(*[pl.program_id(d) for d in range(4)])
        first = kk == 0
        last = kk == nk - 1
        if stack == "sum":
            first, last = first & (s == 0), last & (s == ns - 1)

        @pl.when(first)
        def _():
            acc[...] = jnp.zeros_like(acc)

        acc[...] += lax.dot_general(a_ref[...].astype(BF16), b_ref[...].astype(BF16), (dims, ((), ())),
                                    preferred_element_type=F32)

        @pl.when(last)
        def _():
            r = acc[...]
            if res is not None:
                r = r + r_ref[...]
            o_ref[...] = r.astype(out_dtype)

    def spec(shape, idx, stacked):
        if stacked:
            return pl.BlockSpec((None,) + shape, lambda *g: (order(*g)[0],) + idx(*order(*g)))
        return pl.BlockSpec(shape, lambda *g: idx(*order(*g)))

    a_spec = (spec((tk, tm), lambda s, i, j, kk: (kk, i), stack == "sum") if ta
              else spec((tm, tk), lambda s, i, j, kk: (i, kk), stack == "sum"))
    b_spec = (spec((tn, tk), lambda s, i, j, kk: (j, kk), bool(stack)) if tb
              else spec((tk, tn), lambda s, i, j, kk: (kk, j), bool(stack)))
    o_spec = spec((tm, tn), lambda s, i, j, kk: (i, j), stack == "out")
    ins, specs = [a, b], [a_spec, b_spec]
    if res is not None:
        ins.append(res)
        specs.append(o_spec)
    if after is not None:
        ins.append(after)
        specs.append(pl.BlockSpec(after.shape, lambda *g: (0,) * after.ndim))
    sem = (("parallel", "parallel", "arbitrary", "arbitrary") if stack == "sum"
           else ("parallel", "parallel", "parallel", "arbitrary"))
    return pl.pallas_call(
        body, name=name, grid=grid, in_specs=specs, out_specs=o_spec,
        out_shape=jax.ShapeDtypeStruct(((ns,) if stack == "out" else ()) + (m, n), out_dtype),
        scratch_shapes=[] if single else [pltpu.VMEM((tm, tn), F32)],
        compiler_params=_cparams(sem),
    )(*ins)


MM_VMEM_BUDGET = 40 * 1024 * 1024
MXU_WIDTH = 256


def _mm_tiles(m, n, k, ns, sa, sb, so, has_res):
    def divs(x, mult, cap):
        out = [d for d in range(mult, min(x, cap) + 1, mult) if x % d == 0]
        return out or [x]

    best = None
    for tk in divs(k, 128, 8192):
        nk = (k // tk) * ns
        for tm in divs(m, 8, 2048):
            for tn in divs(n, 128, 2048):
                vmem = 2 * (tm * tk * sa + tk * tn * sb + tm * tn * so) + (2 * tm * tn * 4 if has_res else 0)
                vmem += tm * tn * 4 if nk > 1 else 0
                if vmem > MM_VMEM_BUDGET:
                    continue
                steps = (m // tm) * (n // tn) * nk
                traffic = (m // tm) * k * n * sb * ns + (n // tn if nk > 1 else 1) * m * k * sa * ns
                cost = steps * 0.4e-6 + traffic / 2.5e12 + (nk * m * n * 8 / 6e12 if nk > 1 else 0)
                cost += 2.0 * m * n * k * ns / 7e14 * (-(-tn // MXU_WIDTH) * MXU_WIDTH / tn)
                if best is None or cost < best[0]:
                    best = (cost, tm, tn, tk)
    return best[1:]


def _norm_fwd(x, xoff, gain, ncol, w, out_dtype, *, z=None, zoff=0, into=None, into_off=0, name):
    t = x.shape[0]
    tr = _tile(t, max(256, (1 << 18) // w), 8)

    def body(*refs):
        x_ref, g_ref, o_ref = refs[0], refs[1], refs[-1]
        y = _rms_fn(x_ref[...], g_ref[...]) if z is None else _rms_fn(x_ref[...], g_ref[...], refs[2][...])
        o_ref[...] = y.astype(out_dtype)

    ins = [x, gain]
    specs = [pl.BlockSpec((tr, w), lambda j, r: (r, xoff + j)), pl.BlockSpec((1, w), lambda j, r: (0, 0))]
    if z is not None:
        ins.append(z)
        specs.append(pl.BlockSpec((tr, w), lambda j, r: (r, zoff + j)))
    aliases = {}
    if into is not None:
        aliases = {len(ins): 0}
        ins.append(into)
        specs.append(pl.BlockSpec(memory_space=pl.ANY))
    return pl.pallas_call(
        body, name=name, grid=(ncol, t // tr), in_specs=specs,
        out_specs=pl.BlockSpec((tr, w), lambda j, r: (r, into_off + j)),
        out_shape=jax.ShapeDtypeStruct((t, ncol * w) if into is None else into.shape, out_dtype),
        input_output_aliases=aliases, compiler_params=_cparams(("parallel", "parallel")),
    )(*ins)


def _norm_bwd(x, xoff, gain, dy, dyoff, ncol, w, *, z=None, zoff=0, res=None, name):
    t = x.shape[0]
    tr = _tile(t, max(256, (1 << 18) // w), 8)

    def body(*refs):
        it = iter(refs)
        x_ref, g_ref = next(it), next(it)
        z_ref = next(it) if z is not None else None
        dy_ref = next(it)
        r_ref = next(it) if res is not None else None
        dx_ref = next(it)
        dz_ref = next(it) if z is not None else None
        dg_ref = next(it)

        @pl.when((pl.program_id(0) == 0) & (pl.program_id(1) == 0))
        def _():
            dg_ref[...] = jnp.zeros_like(dg_ref)

        args = (x_ref[...], g_ref[...]) + ((z_ref[...],) if z is not None else ())
        _, vjp = jax.vjp(_rms_fn, *args)
        grads = vjp(dy_ref[...].astype(F32))
        dx = grads[0]
        if res is not None:
            dx = dx + r_ref[...]
        dx_ref[...] = dx
        if z is not None:
            dz_ref[...] = grads[2]
        dg_ref[...] += grads[1]

    ins = [x, gain]
    specs = [pl.BlockSpec((tr, w), lambda j, r: (r, xoff + j)), pl.BlockSpec((1, w), lambda j, r: (0, 0))]
    if z is not None:
        ins.append(z)
        specs.append(pl.BlockSpec((tr, w), lambda j, r: (r, zoff + j)))
    ins.append(dy)
    specs.append(pl.BlockSpec((tr, w), lambda j, r: (r, dyoff + j)))
    blk = pl.BlockSpec((tr, w), lambda j, r: (r, j))
    if res is not None:
        ins.append(res)
        specs.append(blk)
    full = jax.ShapeDtypeStruct((t, ncol * w), F32)
    out_shape, out_specs = [full], [blk]
    if z is not None:
        out_shape.append(full)
        out_specs.append(blk)
    out_shape.append(jax.ShapeDtypeStruct((1, w), F32))
    out_specs.append(pl.BlockSpec((1, w), lambda j, r: (0, 0)))
    return pl.pallas_call(
        body, name=name, grid=(ncol, t // tr), in_specs=specs, out_specs=out_specs, out_shape=out_shape,
        compiler_params=_cparams(("arbitrary", "arbitrary")),
    )(*ins)


def _small_fn(x, pa, pb, nf, ng):
    lane = lax.broadcasted_iota(jnp.int32, x.shape, 1)
    zz = x + pb
    logf = -_softplus(-zz)
    g = -jnp.exp(pa) * _softplus(zz)
    beta = _sigmoid(x)
    return jnp.where(lane < nf, logf, jnp.where(lane < nf + ng, g, beta))


def _tri(n, upper):
    r = lax.broadcasted_iota(jnp.int32, (n, n), 0)
    c = lax.broadcasted_iota(jnp.int32, (n, n), 1)
    return jnp.where((c >= r) if upper else (c <= r), 1.0, 0.0).astype(F32)


def _small_fwd(p, off, pa, pb, nf, ng):
    t = p.shape[0]
    blk = HEAD_DIM
    nb = t // blk

    def body(x_ref, pa_ref, pb_ref, v_ref, c_ref):
        v_ref[...] = _small_fn(x_ref[...], pa_ref[...], pb_ref[...], nf, ng)
        tri = _tri(blk, False)

        carry = jnp.zeros((1, HEAD_DIM), F32)
        for i in range(nb):
            rows = slice(i * blk, (i + 1) * blk)
            c = _nn_hi(tri, v_ref[rows, :]) + carry
            c_ref[rows, :] = c
            carry = c[blk - 1:blk, :]

    row = pl.BlockSpec((1, HEAD_DIM), lambda i: (0, 0))
    out = pl.BlockSpec((t, HEAD_DIM), lambda i: (0, 0))
    return pl.pallas_call(
        body, name="small_fwd", grid=(1,),
        in_specs=[pl.BlockSpec((t, HEAD_DIM), lambda i: (0, off)), row, row], out_specs=[out, out],
        out_shape=[jax.ShapeDtypeStruct((t, HEAD_DIM), F32)] * 2,
        compiler_params=_cparams(("arbitrary",)),
    )(p, pa, pb)


def _small_bwd(p, off, pa, pb, dvals, dcsum, nf, ng):
    t = p.shape[0]
    blk = HEAD_DIM
    nb = t // blk

    def body(x_ref, pa_ref, pb_ref, dv_ref, dc_ref, dx_ref, dpa_ref, dpb_ref, tot_ref):
        tri = _tri(blk, True)

        carry = jnp.zeros((1, HEAD_DIM), F32)
        for i in reversed(range(nb)):
            rows = slice(i * blk, (i + 1) * blk)
            c = _nn_hi(tri, dc_ref[rows, :]) + carry
            tot_ref[rows, :] = c + dv_ref[rows, :]
            carry = c[0:1, :]
        f = functools.partial(_small_fn, nf=nf, ng=ng)
        _, vjp = jax.vjp(f, x_ref[...], pa_ref[...], pb_ref[...])
        dx, dpa, dpb = vjp(tot_ref[...])
        dx_ref[...] = dx
        dpa_ref[...] = dpa
        dpb_ref[...] = dpb

    row = pl.BlockSpec((1, HEAD_DIM), lambda i: (0, 0))
    full = pl.BlockSpec((t, HEAD_DIM), lambda i: (0, 0))
    return pl.pallas_call(
        body, name="small_bwd", grid=(1,),
        in_specs=[pl.BlockSpec((t, HEAD_DIM), lambda i: (0, off)), row, row, full, full],
        out_specs=[full, row, row],
        out_shape=[jax.ShapeDtypeStruct((t, HEAD_DIM), F32), jax.ShapeDtypeStruct((1, HEAD_DIM), F32),
                   jax.ShapeDtypeStruct((1, HEAD_DIM), F32)],
        scratch_shapes=[pltpu.VMEM((t, HEAD_DIM), F32)],
        compiler_params=_cparams(("arbitrary",)),
    )(p, pa, pb, dvals, dcsum)


def _fox_heads(nf, most):
    return next(h for h in range(most, 0, -1) if nf % h == 0)


def _fox_fwd(q, k, v, cc, cr, nf, tq, tk, d_mix):
    t = q.shape[0]
    scale = HEAD_DIM ** -0.5
    assert tq == tk

    vt = jnp.transpose(v.reshape(t // tk, tk, nf, HEAD_DIM), (2, 0, 3, 1))

    hp = _fox_heads(nf, 3)
    lanes = lambda h: slice(h * HEAD_DIM, (h + 1) * HEAD_DIM)

    def body(q_ref, k_ref, vt_ref, cc_ref, cr_ref, o_ref, lse_ref, mix_ref):
        i = pl.program_id(1)
        qs = [q_ref[:, lanes(h)] for h in range(hp)]
        cqs = [cr_ref[h, i] for h in range(hp)]
        ones = jnp.ones((8, tk), BF16)
        diff = lax.broadcasted_iota(jnp.int32, (tk, tq), 0) - lax.broadcasted_iota(jnp.int32, (tk, tq), 1)

        def scores(h, j):
            ks = pl.ds(pl.multiple_of(j * tk, tk), tk)
            return lax.dot_general(k_ref[ks, lanes(h)], qs[h], (((1,), (1,)), ((), ())),
                                   preferred_element_type=F32)

        def tile(h, j, m, l, acc, s, masked):
            ks = pl.ds(pl.multiple_of(j * tk, tk), tk)
            s = s * scale + cqs[h] - cc_ref[h, ks, :]
            if masked:
                s = jnp.where(diff <= 0, s, NEG)
            m_new = jnp.maximum(m, jnp.max(s, axis=0, keepdims=True))
            pr = jnp.exp(s - m_new).astype(BF16)
            alpha = jnp.exp(m - m_new)
            l = alpha * l + jnp.dot(ones, pr, preferred_element_type=F32)[:1]
            acc = alpha * acc + jnp.dot(vt_ref[h, j], pr, preferred_element_type=F32)
            return m_new, l, acc

        def step(j, carry):
            nxt = [scores(h, j + 1) for h in range(hp)]
            return tuple(tile(h, j, *carry[h], False) + (nxt[h],) for h in range(hp))

        init = tuple((jnp.full((1, tq), NEG, F32), jnp.zeros((1, tq), F32), jnp.zeros((HEAD_DIM, tq), F32),
                      scores(h, 0)) for h in range(hp))
        carry = lax.fori_loop(0, i, step, init)
        for h in range(hp):
            m, l, acc = tile(h, i, *carry[h], True)
            o = jnp.transpose(acc / l)
            o_ref[:, lanes(h)] = o
            mix_ref[:, lanes(h)] = o.astype(BF16)
            lse_ref[h, 0] = m + jnp.log(l)

    w = hp * HEAD_DIM
    qblk = pl.BlockSpec((tq, w), lambda h, i: (i, h))
    return pl.pallas_call(
        body, name="fox_fwd", grid=(nf // hp, t // tq),
        in_specs=[qblk, pl.BlockSpec((t, w), lambda h, i: (0, h)),
                  pl.BlockSpec((hp, t // tk, HEAD_DIM, tk), lambda h, i: (h, 0, 0, 0)),
                  pl.BlockSpec((hp, t, 1), lambda h, i: (h, 0, 0)),
                  pl.BlockSpec((hp, t // tk, 1, tk), lambda h, i: (h, 0, 0, 0))],
        out_specs=[qblk, pl.BlockSpec((hp, 1, 1, tq), lambda h, i: (h, i, 0, 0)), qblk],
        out_shape=[jax.ShapeDtypeStruct((t, nf * HEAD_DIM), F32), jax.ShapeDtypeStruct((nf, t // tq, 1, tq), F32),
                   jax.ShapeDtypeStruct((t, d_mix), BF16)],
        compiler_params=_cparams(("parallel", "parallel")),
    )(q, k, vt, cc, cr)


def _fox_bwd(q, k, v, cc, cr, o, lse, dmix, nf, tq, tk):
    t = q.shape[0]
    scale = HEAD_DIM ** -0.5
    assert tq == tk
    hp = _fox_heads(nf, 3)
    lanes = lambda h: slice(h * HEAD_DIM, (h + 1) * HEAD_DIM)
    kt = jnp.transpose(k.reshape(t // tk, tk, nf, HEAD_DIM), (2, 0, 3, 1))

    def body(q_ref, k_ref, kt_ref, v_ref, cc_ref, cr_ref, o_ref, lse_ref, do_ref,
             dq_ref, dk_ref, dv_ref, dcq_ref, dck_ref):
        i = pl.program_id(1)

        @pl.when(i == 0)
        def _():
            dk_ref[...] = jnp.zeros_like(dk_ref)
            dv_ref[...] = jnp.zeros_like(dv_ref)
            dck_ref[...] = jnp.zeros_like(dck_ref)

        diff = lax.broadcasted_iota(jnp.int32, (tk, tq), 0) - lax.broadcasted_iota(jnp.int32, (tk, tq), 1)
        qs = [q_ref[:, lanes(h)] for h in range(hp)]
        dos = [do_ref[:, lanes(h)] for h in range(hp)]
        do_b = [d.astype(BF16) for d in dos]
        cqs = [cr_ref[h, i] for h in range(hp)]
        lses = [lse_ref[h, 0] for h in range(hp)]
        deltas = [jnp.sum(jnp.transpose(dos[h] * o_ref[:, lanes(h)]), axis=0, keepdims=True) for h in range(hp)]

        def products(h, j):
            ks = pl.ds(pl.multiple_of(j * tk, tk), tk)
            nt = (((1,), (1,)), ((), ()))
            return (lax.dot_general(k_ref[ks, lanes(h)], qs[h], nt, preferred_element_type=F32),
                    lax.dot_general(v_ref[ks, lanes(h)], do_b[h], nt, preferred_element_type=F32))

        def tile(h, j, dqt, dcq, s, dp, masked):
            ks = pl.ds(pl.multiple_of(j * tk, tk), tk)
            pr = jnp.exp(s * scale + cqs[h] - cc_ref[h, ks, :] - lses[h])
            if masked:
                pr = jnp.where(diff <= 0, pr, 0.0)
            ds = pr * (dp - deltas[h])
            ds_b = ds.astype(BF16)
            dqt = dqt + jnp.dot(kt_ref[h, j], ds_b, preferred_element_type=F32)
            dk_ref[ks, lanes(h)] += jnp.dot(ds_b, qs[h], preferred_element_type=F32) * scale
            dv_ref[ks, lanes(h)] += jnp.dot(pr.astype(BF16), do_b[h], preferred_element_type=F32)
            dck_ref[h, ks, :] -= jnp.sum(ds, axis=1, keepdims=True)
            return dqt, dcq + jnp.sum(ds, axis=0, keepdims=True)

        def step(j, carry):
            nxt = [products(h, j + 1) for h in range(hp)]
            return tuple(tile(h, j, *carry[h], False) + nxt[h] for h in range(hp))

        init = tuple((jnp.zeros((HEAD_DIM, tq), F32), jnp.zeros((1, tq), F32)) + products(h, 0) for h in range(hp))
        carry = lax.fori_loop(0, i, step, init)
        for h in range(hp):
            dqt, dcq = tile(h, i, *carry[h], True)
            dq_ref[:, lanes(h)] = jnp.transpose(dqt) * scale
            dcq_ref[h, 0] = dcq

    w = hp * HEAD_DIM
    head_all = pl.BlockSpec((t, w), lambda h, i: (0, h))
    qblk = pl.BlockSpec((tq, w), lambda h, i: (i, h))
    colv = pl.BlockSpec((hp, t, 1), lambda h, i: (h, 0, 0))
    rows_all = pl.BlockSpec((hp, t // tk, 1, tk), lambda h, i: (h, 0, 0, 0))
    row_blk = pl.BlockSpec((hp, 1, 1, tq), lambda h, i: (h, i, 0, 0))
    wide = jax.ShapeDtypeStruct((t, nf * HEAD_DIM), F32)
    return pl.pallas_call(
        body, name="fox_bwd", grid=(nf // hp, t // tq),
        in_specs=[qblk, head_all, pl.BlockSpec((hp, t // tk, HEAD_DIM, tk), lambda h, i: (h, 0, 0, 0)), head_all,
                  colv, rows_all, qblk, row_blk, qblk],
        out_specs=[qblk, head_all, head_all, row_blk, colv],
        out_shape=[wide, wide, wide, jax.ShapeDtypeStruct((nf, t // tq, 1, tq), F32),
                   jax.ShapeDtypeStruct((nf, t, 1), F32)],
        compiler_params=_cparams(("parallel", "arbitrary")),
    )(q, k, kt, v, cc, cr, o, lse, dmix)


def _mem_fn(mq, mk, mv, gq, gk):
    qn = _rms_fn(mq, gq)
    kn = _rms_fn(mk, gk)
    s = _nt(qn, kn) * (HEAD_DIM ** -0.5)
    e = jnp.exp(s - lax.stop_gradient(jnp.max(s, axis=1, keepdims=True)))
    pr = e / jnp.sum(e, axis=1, keepdims=True)
    return _nn(pr, mv)


def _mem_specs(t, m, tq, qoff):
    qblk = pl.BlockSpec((tq, HEAD_DIM), lambda h, i: (i, qoff + h))
    kblk = pl.BlockSpec((m, HEAD_DIM), lambda h, i: (0, h))
    vblk = pl.BlockSpec((m, HEAD_DIM), lambda h, i: (0, N_MEM_HEADS + h))
    row = pl.BlockSpec((1, HEAD_DIM), lambda h, i: (0, 0))
    return qblk, kblk, vblk, row


def _mem_fwd(p, qoff, mkv, gq, gk, tq, into, into_off):
    t, m = p.shape[0], mkv.shape[0]
    qblk, kblk, vblk, row = _mem_specs(t, m, tq, qoff)

    def body(q_ref, k_ref, v_ref, gq_ref, gk_ref, _, o_ref):
        o_ref[...] = _mem_fn(q_ref[...], k_ref[...], v_ref[...], gq_ref[...], gk_ref[...]).astype(BF16)

    return pl.pallas_call(
        body, name="mem_fwd", grid=(N_MEM_HEADS, t // tq),
        in_specs=[qblk, kblk, vblk, row, row, pl.BlockSpec(memory_space=pl.ANY)],
        out_specs=pl.BlockSpec((tq, HEAD_DIM), lambda h, i: (i, into_off + h)),
        out_shape=jax.ShapeDtypeStruct(into.shape, BF16), input_output_aliases={5: 0},
        compiler_params=_cparams(("parallel", "parallel")),
    )(p, mkv, mkv, gq, gk, into)


def _mem_bwd(p, qoff, mkv, gq, gk, dmix, dooff, tq):
    t, m = p.shape[0], mkv.shape[0]
    qblk, kblk, vblk, row = _mem_specs(t, m, tq, qoff)

    def body(q_ref, k_ref, v_ref, gq_ref, gk_ref, do_ref, dq_ref, dkv_k_ref, dkv_v_ref, dgq_ref, dgk_ref):
        h, i = pl.program_id(0), pl.program_id(1)

        @pl.when((h == 0) & (i == 0))
        def _():
            dgq_ref[...] = jnp.zeros_like(dgq_ref)
            dgk_ref[...] = jnp.zeros_like(dgk_ref)

        @pl.when(i == 0)
        def _():
            dkv_k_ref[...] = jnp.zeros_like(dkv_k_ref)
            dkv_v_ref[...] = jnp.zeros_like(dkv_v_ref)

        _, vjp = jax.vjp(_mem_fn, q_ref[...], k_ref[...], v_ref[...], gq_ref[...], gk_ref[...])
        dq, dk, dv, dgq, dgk = vjp(do_ref[...])
        dq_ref[...] = dq
        dkv_k_ref[...] += dk
        dkv_v_ref[...] += dv
        dgq_ref[...] += dgq
        dgk_ref[...] += dgk

    oblk = pl.BlockSpec((tq, HEAD_DIM), lambda h, i: (i, h))
    kout = pl.BlockSpec((m, HEAD_DIM), lambda h, i: (0, h))
    half = jax.ShapeDtypeStruct((m, N_MEM_HEADS * HEAD_DIM), F32)
    rshape = jax.ShapeDtypeStruct((1, HEAD_DIM), F32)
    return pl.pallas_call(
        body, name="mem_bwd", grid=(N_MEM_HEADS, t // tq),
        in_specs=[qblk, kblk, vblk, row, row, pl.BlockSpec((tq, HEAD_DIM), lambda h, i: (i, dooff + h))],
        out_specs=[oblk, kout, kout, row, row],
        out_shape=[jax.ShapeDtypeStruct((t, N_MEM_HEADS * HEAD_DIM), F32), half, half, rshape, rshape],
        compiler_params=_cparams(("arbitrary", "arbitrary")),
    )(p, mkv, mkv, gq, gk, dmix)


def _shift_down(x, s):
    if s == 0:
        return x
    r = lax.broadcasted_iota(jnp.int32, x.shape, 0)
    return jnp.where(r >= s, pltpu.roll(x, s, 0), 0.0)


def _shift_up(x, s):
    if s == 0:
        return x
    n = x.shape[0]
    r = lax.broadcasted_iota(jnp.int32, x.shape, 0)
    return jnp.where(r < n - s, pltpu.roll(x, n - s, 0), 0.0)


def _conv_fn(x0, x1, x2, x3, w0, w1, w2, w3, kind):
    y = _silu(x0 * w0 + x1 * w1 + x2 * w2 + x3 * w3)
    if kind == 2:
        return y
    y = y * lax.rsqrt(jnp.sum(y * y, axis=-1, keepdims=True) + NORM_EPS)
    return y * (HEAD_DIM ** -0.5) if kind == 0 else y


def _conv_fwd(p, off, conv_w, ng):
    t = p.shape[0]

    def body(x_ref, w_ref, o_ref):
        kind = pl.program_id(0) // ng
        x = x_ref[...]
        xs = [_shift_down(x, CONV_WIDTH - 1 - j) for j in range(CONV_WIDTH)]
        ws = [w_ref[j:j + 1, :] for j in range(CONV_WIDTH)]
        for kd in range(3):
            @pl.when(kind == kd)
            def _(kd=kd):
                o_ref[...] = _conv_fn(*xs, *ws, kd)

    return pl.pallas_call(
        body, name="gdn_conv_fwd", grid=(3 * ng,),
        in_specs=[pl.BlockSpec((t, HEAD_DIM), lambda c: (0, off + c)),
                  pl.BlockSpec((CONV_WIDTH, HEAD_DIM), lambda c: (0, c))],
        out_specs=pl.BlockSpec((t, HEAD_DIM), lambda c: (0, c)),
        out_shape=jax.ShapeDtypeStruct((t, 3 * ng * HEAD_DIM), F32),
        compiler_params=_cparams(("parallel",)),
    )(p, conv_w)


def _conv_bwd(p, off, conv_w, dys, ng):
    t = p.shape[0]

    def body(x_ref, w_ref, dq_ref, dk_ref, dv_ref, dx_ref, dw_ref):
        kind = pl.program_id(0) // ng
        dy_refs = (dq_ref, dk_ref, dv_ref)
        x = x_ref[...]
        xs = [_shift_down(x, CONV_WIDTH - 1 - j) for j in range(CONV_WIDTH)]
        ws = [w_ref[j:j + 1, :] for j in range(CONV_WIDTH)]
        for kd in range(3):
            @pl.when(kind == kd)
            def _(kd=kd):
                _, vjp = jax.vjp(functools.partial(_conv_fn, kind=kd), *xs, *ws)
                g = vjp(dy_refs[kd][...])
                dx = _shift_up(g[0], CONV_WIDTH - 1)
                for j in range(1, CONV_WIDTH):
                    dx = dx + _shift_up(g[j], CONV_WIDTH - 1 - j)
                dx_ref[...] = dx
                for j in range(CONV_WIDTH):
                    dw_ref[j:j + 1, :] = g[CONV_WIDTH + j]

    blk = pl.BlockSpec((t, HEAD_DIM), lambda c: (0, c))
    head = lambda k: pl.BlockSpec((t, HEAD_DIM), lambda c: (0, jnp.where(c // ng == k, c % ng, 0)))
    wblk = pl.BlockSpec((CONV_WIDTH, HEAD_DIM), lambda c: (0, c))
    return pl.pallas_call(
        body, name="gdn_conv_bwd", grid=(3 * ng,),
        in_specs=[pl.BlockSpec((t, HEAD_DIM), lambda c: (0, off + c)), wblk] + [head(k) for k in range(3)],
        out_specs=[blk, wblk],
        out_shape=[jax.ShapeDtypeStruct((t, 3 * ng * HEAD_DIM), F32),
                   jax.ShapeDtypeStruct((CONV_WIDTH, 3 * ng * HEAD_DIM), F32)],
        compiler_params=_cparams(("parallel",)),
    )(p, conv_w, *dys)


def _lower_inverse(lower):
    c = lower.shape[-1]
    r = lax.broadcasted_iota(jnp.int32, (1, c, c), 1)
    e = lax.broadcasted_iota(jnp.int32, (1, c, c), 2)
    hi = lax.Precision.HIGH
    inv = jnp.where(r == e, 1.0, 0.0) - lower
    pw = lower
    for _ in range(int(math.log2(c)) - 1):
        pw = _dot(pw, pw, ((1,), (0,)), hi)
        inv = inv + _dot(inv, pw, ((1,), (0,)), hi)
    return inv


@jax.custom_vjp
def _solve(lower, inv, vb, kbg):
    hi = lax.Precision.HIGH
    return _dot(inv, vb, ((1,), (0,)), hi), _dot(inv, kbg, ((1,), (0,)), hi)


def _solve_fwd(lower, inv, vb, kbg):
    u, w = _solve(lower, inv, vb, kbg)
    return (u, w), (inv, u, w)


def _solve_bwd(res, cts):
    inv, u, w = res
    dvb, dkbg = _tn(inv, cts[0]), _tn(inv, cts[1])
    return -(_nt(dvb, u) + _nt(dkbg, w)), jnp.zeros_like(inv), dvb, dkbg


_solve.defvjp(_solve_fwd, _solve_bwd)


def _wy_fn(q, k, v, gcol, grow, bcol, inv=None):
    b, c, dk = q.shape
    r = lax.broadcasted_iota(jnp.int32, (1, c, c), 1)
    e = lax.broadcasted_iota(jnp.int32, (1, c, c), 2)
    tril, strict = e <= r, e < r
    gc_col = jnp.sum(jnp.where(tril, grow, 0.0), axis=2, keepdims=True)
    gc_row = jnp.sum(jnp.where(r <= e, gcol, 0.0), axis=1, keepdims=True)
    g_last = jnp.sum(gcol, axis=1, keepdims=True)
    decay = jnp.exp(jnp.where(tril, gc_col - gc_row, NEG))
    kb, vb = k * bcol, v * bcol
    lower = jnp.where(strict, _nt(kb, k) * decay, 0.0)
    if inv is None:
        inv = _lower_inverse(lower)
    u, w = _solve(lower, inv, vb, kb * jnp.exp(gc_col))
    attn = jnp.where(tril, _nt(q, k) * decay, 0.0)
    qg = q * jnp.exp(gc_col)
    kdec = k * jnp.exp(g_last - gc_col)
    egl = jnp.broadcast_to(jnp.exp(g_last), (b, 1, dk))
    return u, w, qg, kdec, attn, egl, inv


def _scan_fn(u, w, qg, kdec, attn, egl, state):
    v_new = u - _nn(w, state)
    o = _nn(qg, state) + _nn(attn, v_new)
    return o, state * egl + _tn(kdec, v_new)


GDN_CHUNKS_PER_STEP = 4


def _gdn_fwd(qkv, gcol, grow, bcol, ng):
    t = qkv.shape[0]
    nch = t // CHUNK

    cb = GDN_CHUNKS_PER_STEP
    *wy, inv = _gdn_wy(qkv, gcol, grow, bcol, ng, cb)

    def body(u_ref, w_ref, qg_ref, kd_ref, at_ref, eg_ref, o_ref, st_ref, state):
        @pl.when(pl.program_id(0) == 0)
        def _():
            state[...] = jnp.zeros_like(state)

        st_ref[:, 0] = state[...]
        heads = lambda ref: jnp.stack([ref[:, h * HEAD_DIM:(h + 1) * HEAD_DIM] for h in range(ng)])
        o, new = _scan_fn(heads(u_ref), heads(w_ref), heads(qg_ref), heads(kd_ref), at_ref[:, 0], eg_ref[:, 0],
                          state[...])
        for h in range(ng):
            o_ref[:, h * HEAD_DIM:(h + 1) * HEAD_DIM] = o[h]
        state[...] = new

    w = ng * HEAD_DIM
    blk = pl.BlockSpec((CHUNK, w), lambda i: (i, 0))
    o, states = pl.pallas_call(
        body, name="gdn_scan_fwd", grid=(nch,),
        in_specs=[blk, blk, blk, blk, pl.BlockSpec((ng, 1, CHUNK, CHUNK), lambda i: (0, i, 0, 0)),
                  pl.BlockSpec((ng, 1, 1, HEAD_DIM), lambda i: (0, i, 0, 0))],
        out_specs=[blk, pl.BlockSpec((ng, 1, HEAD_DIM, HEAD_DIM), lambda i: (0, i, 0, 0))],
        out_shape=[jax.ShapeDtypeStruct((t, w), F32),
                   jax.ShapeDtypeStruct((ng, nch, HEAD_DIM, HEAD_DIM), F32)],
        scratch_shapes=[pltpu.VMEM((ng, HEAD_DIM, HEAD_DIM), F32)],
        compiler_params=_cparams(("arbitrary",)),
    )(*wy)
    return o, (wy, inv, states)


def _wy_batch(q_ref, k_ref, v_ref, gc_ref, gr_ref, bc_ref, ng, cb):
    idx = [(c, h) for c in range(cb) for h in range(ng)]
    rows = lambda c: slice(c * CHUNK, (c + 1) * CHUNK)
    lanes = lambda h: slice(h * HEAD_DIM, (h + 1) * HEAD_DIM)
    wide = lambda ref: jnp.stack([ref[rows(c), lanes(h)] for c, h in idx])
    col = lambda ref: jnp.stack([ref[h, rows(c), :] for c, h in idx])
    return idx, (wide(q_ref), wide(k_ref), wide(v_ref), col(gc_ref), jnp.stack([gr_ref[h, c] for c, h in idx]),
                 col(bc_ref))


def _gdn_wy(qkv, gcol, grow, bcol, ng, cb):
    t = qkv.shape[0]
    nch = t // CHUNK

    def body(q_ref, k_ref, v_ref, gc_ref, gr_ref, bc_ref, u_ref, w_ref, qg_ref, kd_ref, at_ref, eg_ref, inv_ref):
        idx, args = _wy_batch(q_ref, k_ref, v_ref, gc_ref, gr_ref, bc_ref, ng, cb)
        u, w, qg, kd, at, eg, inv = _wy_fn(*args)
        for b, (c, h) in enumerate(idx):
            rows, lanes = slice(c * CHUNK, (c + 1) * CHUNK), slice(h * HEAD_DIM, (h + 1) * HEAD_DIM)
            u_ref[rows, lanes] = u[b]
            w_ref[rows, lanes] = w[b]
            qg_ref[rows, lanes] = qg[b]
            kd_ref[rows, lanes] = kd[b]
            at_ref[h, c] = at[b]
            eg_ref[h, c] = eg[b]
            inv_ref[h, c] = inv[b]

    wd = ng * HEAD_DIM
    blk = lambda o: pl.BlockSpec((cb * CHUNK, wd), lambda i: (i, o))
    col = pl.BlockSpec((ng, cb * CHUNK, 1), lambda i: (0, i, 0))
    sq = pl.BlockSpec((ng, cb, CHUNK, CHUNK), lambda i: (0, i, 0, 0))
    wide = jax.ShapeDtypeStruct((t, wd), F32)
    sq_shape = jax.ShapeDtypeStruct((ng, nch, CHUNK, CHUNK), F32)
    return pl.pallas_call(
        body, name="gdn_wy_fwd", grid=(nch // cb,),
        in_specs=[blk(0), blk(1), blk(2), col, pl.BlockSpec((ng, cb, 1, CHUNK), lambda i: (0, i, 0, 0)), col],
        out_specs=[blk(0), blk(0), blk(0), blk(0), sq, pl.BlockSpec((ng, cb, 1, HEAD_DIM), lambda i: (0, i, 0, 0)),
                   sq],
        out_shape=[wide, wide, wide, wide, sq_shape, jax.ShapeDtypeStruct((ng, nch, 1, HEAD_DIM), F32), sq_shape],
        compiler_params=_cparams(("parallel",)),
    )(qkv, qkv, qkv, gcol, grow, bcol)


def _gdn_bwd(qkv, gcol, grow, bcol, saved, do, ng):
    t = qkv.shape[0]
    nch = t // CHUNK
    cb = GDN_CHUNKS_PER_STEP // 2
    wy, inv, states = saved
    wd = ng * HEAD_DIM

    def scan_body(u_ref, w_ref, qg_ref, kd_ref, at_ref, eg_ref, st_ref, do_ref,
                  du_ref, dw_ref, dqg_ref, dkd_ref, dat_ref, deg_ref, dstate):
        @pl.when(pl.program_id(0) == 0)
        def _():
            dstate[...] = jnp.zeros_like(dstate)

        heads = lambda ref: jnp.stack([ref[:, h * HEAD_DIM:(h + 1) * HEAD_DIM] for h in range(ng)])
        _, vjp = jax.vjp(_scan_fn, heads(u_ref), heads(w_ref), heads(qg_ref), heads(kd_ref), at_ref[:, 0],
                         eg_ref[:, 0], st_ref[:, 0])
        du, dw, dqg, dkd, dat, deg, dst = vjp((heads(do_ref), dstate[...]))
        for h in range(ng):
            lanes = slice(h * HEAD_DIM, (h + 1) * HEAD_DIM)
            du_ref[:, lanes] = du[h]
            dw_ref[:, lanes] = dw[h]
            dqg_ref[:, lanes] = dqg[h]
            dkd_ref[:, lanes] = dkd[h]
        dat_ref[:, 0] = dat
        deg_ref[:, 0] = deg
        dstate[...] = dst

    rev = lambda i: nch - 1 - i
    blk = pl.BlockSpec((CHUNK, wd), lambda i: (rev(i), 0))
    atb = pl.BlockSpec((ng, 1, CHUNK, CHUNK), lambda i: (0, rev(i), 0, 0))
    egb = pl.BlockSpec((ng, 1, 1, HEAD_DIM), lambda i: (0, rev(i), 0, 0))
    wide = jax.ShapeDtypeStruct((t, wd), F32)
    at_shape = jax.ShapeDtypeStruct((ng, nch, CHUNK, CHUNK), F32)
    eg_shape = jax.ShapeDtypeStruct((ng, nch, 1, HEAD_DIM), F32)
    dwy = pl.pallas_call(
        scan_body, name="gdn_scan_bwd", grid=(nch,),
        in_specs=[blk, blk, blk, blk, atb, egb,
                  pl.BlockSpec((ng, 1, HEAD_DIM, HEAD_DIM), lambda i: (0, rev(i), 0, 0)), blk],
        out_specs=[blk, blk, blk, blk, atb, egb],
        out_shape=[wide, wide, wide, wide, at_shape, eg_shape],
        scratch_shapes=[pltpu.VMEM((ng, HEAD_DIM, HEAD_DIM), F32)],
        compiler_params=_cparams(("arbitrary",)),
    )(*wy, states, do)

    def wy_body(q_ref, k_ref, v_ref, gc_ref, gr_ref, bc_ref, du_ref, dw_ref, dqg_ref, dkd_ref, dat_ref, deg_ref,
                inv_ref, dq_ref, dk_ref, dv_ref, dgc_ref, dgr_ref, dbc_ref):
        idx, args = _wy_batch(q_ref, k_ref, v_ref, gc_ref, gr_ref, bc_ref, ng, cb)
        kept = jnp.stack([inv_ref[h, c] for c, h in idx])
        rows = lambda c: slice(c * CHUNK, (c + 1) * CHUNK)
        lanes = lambda h: slice(h * HEAD_DIM, (h + 1) * HEAD_DIM)
        wide_ct = lambda ref: jnp.stack([ref[rows(c), lanes(h)] for c, h in idx])
        cts = (wide_ct(du_ref), wide_ct(dw_ref), wide_ct(dqg_ref), wide_ct(dkd_ref),
               jnp.stack([dat_ref[h, c] for c, h in idx]), jnp.stack([deg_ref[h, c] for c, h in idx]))
        _, vjp = jax.vjp(lambda *a: _wy_fn(*a, inv=kept)[:6], *args)
        dq, dk, dv, dgc, dgr, dbc = vjp(cts)
        for b, (c, h) in enumerate(idx):
            dq_ref[rows(c), lanes(h)] = dq[b]
            dk_ref[rows(c), lanes(h)] = dk[b]
            dv_ref[rows(c), lanes(h)] = dv[b]
            dgc_ref[h, rows(c), :] = dgc[b]
            dgr_ref[h, c] = dgr[b]
            dbc_ref[h, rows(c), :] = dbc[b]

    cblk = lambda o: pl.BlockSpec((cb * CHUNK, wd), lambda i: (i, o))
    col = pl.BlockSpec((ng, cb * CHUNK, 1), lambda i: (0, i, 0))
    rowv = pl.BlockSpec((ng, cb, 1, CHUNK), lambda i: (0, i, 0, 0))
    cshape = jax.ShapeDtypeStruct((ng, t, 1), F32)
    return pl.pallas_call(
        wy_body, name="gdn_wy_bwd", grid=(nch // cb,),
        in_specs=[cblk(0), cblk(1), cblk(2), col, rowv, col, cblk(0), cblk(0), cblk(0), cblk(0),
                  pl.BlockSpec((ng, cb, CHUNK, CHUNK), lambda i: (0, i, 0, 0)),
                  pl.BlockSpec((ng, cb, 1, HEAD_DIM), lambda i: (0, i, 0, 0)),
                  pl.BlockSpec((ng, cb, CHUNK, CHUNK), lambda i: (0, i, 0, 0))],
        out_specs=[cblk(0), cblk(0), cblk(0), col, rowv, col],
        out_shape=[wide, wide, wide, cshape, jax.ShapeDtypeStruct((ng, nch, 1, CHUNK), F32), cshape],
        compiler_params=_cparams(("parallel",)),
    )(qkv, qkv, qkv, gcol, grow, bcol, *dwy, inv)


def _swiglu_fn(gate, up):
    return _silu(gate) * up


FFN_TN = 256


def _ffn_up(n2, wgu4):
    _, d, w = wgu4.shape
    t = n2.shape[0]
    tn = _tile(w, FFN_TN)
    nb = w // tn

    def body(a_ref, b_ref, gu_ref, act_ref):
        av = a_ref[...]
        gate = jnp.dot(av, b_ref[0], preferred_element_type=F32)
        up = jnp.dot(av, b_ref[1], preferred_element_type=F32)
        gu_ref[0] = gate.astype(BF16)
        gu_ref[1] = up.astype(BF16)
        act_ref[...] = _swiglu_fn(gate, up).astype(BF16)

    return pl.pallas_call(
        body, name="ffn_up", grid=(2, nb),
        in_specs=[pl.BlockSpec((t, d), lambda j, l: (0, 0)), pl.BlockSpec((2, d, tn), lambda j, l: (j, 0, l))],
        out_specs=[pl.BlockSpec((2, t, tn), lambda j, l: (j, 0, l)),
                   pl.BlockSpec((t, tn), lambda j, l: (0, j * nb + l))],
        out_shape=[jax.ShapeDtypeStruct((4, t, w), BF16), jax.ShapeDtypeStruct((t, 2 * w), BF16)],
        compiler_params=_cparams(("parallel", "parallel")),
    )(n2, wgu4)


def _ffn_dact(dh2, wd, gu, after):
    _, t, w = gu.shape
    d = dh2.shape[1]
    tn = _tile(w, FFN_TN)
    nb = w // tn

    def body(a_ref, b_ref, gu_ref, _, o_ref):
        dact = lax.dot_general(a_ref[...], b_ref[...], (((1,), (1,)), ((), ())), preferred_element_type=F32)
        _, vjp = jax.vjp(_swiglu_fn, gu_ref[0].astype(F32), gu_ref[1].astype(F32))
        dg, du = vjp(dact)
        o_ref[0] = dg.astype(BF16)
        o_ref[1] = du.astype(BF16)

    pair = pl.BlockSpec((2, t, tn), lambda j, l: (j, 0, l))
    return pl.pallas_call(
        body, name="ffn_dact", grid=(2, nb),
        in_specs=[pl.BlockSpec((t, d), lambda j, l: (0, 0)), pl.BlockSpec((tn, d), lambda j, l: (j * nb + l, 0)),
                  pair, pl.BlockSpec(after.shape, lambda j, l: (0, 0))],
        out_specs=pair, out_shape=jax.ShapeDtypeStruct(gu.shape, BF16),
        compiler_params=_cparams(("parallel", "parallel")),
    )(dh2, wd, gu, after)


def _loss_head(h2, target):
    t, d = h2.shape
    tr = _tile(t, 256, 8)

    def body(h_ref, t_ref, l_ref, d_ref, db_ref):
        @pl.when(pl.program_id(0) == 0)
        def _():
            l_ref[...] = jnp.zeros_like(l_ref)

        err = h_ref[...] - t_ref[...]
        d_ref[...] = err * (1.0 / d)
        db_ref[...] = (err * (1.0 / d)).astype(BF16)
        part = 0.5 * jnp.sum(jnp.mean(err * err, axis=-1, keepdims=True), axis=0, keepdims=True)
        lane = lax.broadcasted_iota(jnp.int32, (8, HEAD_DIM), 1)
        row = lax.broadcasted_iota(jnp.int32, (8, HEAD_DIM), 0)
        l_ref[...] += jnp.where((lane == 0) & (row == 0), part, 0.0)

    blk = pl.BlockSpec((tr, d), lambda r: (r, 0))
    return pl.pallas_call(
        body, name="loss_head", grid=(t // tr,), in_specs=[blk, blk],
        out_specs=[pl.BlockSpec((8, HEAD_DIM), lambda r: (0, 0)), blk, blk],
        out_shape=[jax.ShapeDtypeStruct((8, HEAD_DIM), F32), jax.ShapeDtypeStruct((t, d), F32),
                   jax.ShapeDtypeStruct((t, d), BF16)],
        compiler_params=_cparams(("arbitrary",)),
    )(h2, target)


def _adamw(w, g, m, v, *, g_fn=None, name):
    r, c = w.shape
    tr = _tile(r, max(8, (1 << 19) // c // 8 * 8), 8)
    gs = g if isinstance(g, tuple) else (g,)

    def body(w_ref, *refs):
        g_refs, (m_ref, v_ref, go_ref, d_ref, mo_ref, vo_ref) = refs[:len(gs)], refs[len(gs):]
        gr = g_refs[0][...] if g_fn is None else g_fn(*[ref[...] for ref in g_refs])
        mn = ADAM_B1 * m_ref[...] + (1.0 - ADAM_B1) * gr
        vn = ADAM_B2 * v_ref[...] + (1.0 - ADAM_B2) * (gr * gr)
        m_hat = mn / (1.0 - ADAM_B1 ** ADAM_STEP)
        v_hat = vn / (1.0 - ADAM_B2 ** ADAM_STEP)
        go_ref[...] = gr
        d_ref[...] = -ADAM_LR * (m_hat / (jnp.sqrt(v_hat) + ADAM_EPS) + ADAM_WD * w_ref[...])
        mo_ref[...] = mn
        vo_ref[...] = vn

    blk = pl.BlockSpec((tr, c), lambda i: (i, 0))
    gblks = [pl.BlockSpec((tr, gi.shape[1]), lambda i: (i, 0)) for gi in gs]
    return pl.pallas_call(
        body, name=name, grid=(r // tr,), in_specs=[blk] + gblks + [blk, blk], out_specs=[blk] * 4,
        out_shape=[jax.ShapeDtypeStruct((r, c), F32)] * 4,
        compiler_params=_cparams(("parallel",)),
    )(w, *gs, m, v)


class _Layout:
    def __init__(self, d):
        nh = d // HEAD_DIM
        self.nm = N_MEM_HEADS
        self.nf = (nh - self.nm) // 2
        self.ng = nh - self.nm - self.nf
        nf, ng, nm, hd = self.nf, self.ng, self.nm, HEAD_DIM
        self.o_fq, self.o_fk, self.o_fv, self.o_sm = 0, nf, 2 * nf, 3 * nf
        self.o_gq, self.o_gz, self.o_mq = 0, 3 * ng, 4 * ng
        self.cols_a = -(-(3 * nf + 1) // 4) * 4 * hd
        self.cols_b = -(-(4 * ng + nm) // 4) * 4 * hd
        self.cols = self.cols_a + self.cols_b
        sizes = [nf * hd, nf * hd, nf * hd, nf, 3 * ng * hd, ng * hd, ng, ng, nm * hd]
        starts = [sum(sizes[:i]) for i in range(len(sizes))]
        self.ref = list(zip(starts, sizes))
        self.in_cols = sum(sizes)

    def regroup(self, w):
        part = lambda i: w[:, self.ref[i][0]:self.ref[i][0] + self.ref[i][1]]
        a = [part(0), part(1), part(2), part(3), part(6), part(7)]
        b = [part(4), part(5), part(8)]
        pads = [self.cols_a - sum(p.shape[1] for p in a), self.cols_b - sum(p.shape[1] for p in b)]
        fill = [[jnp.zeros((w.shape[0], n), w.dtype)] if n else [] for n in pads]
        return jnp.concatenate(a + fill[0] + b + fill[1], axis=1)

    def ungroup(self, g):
        hd, nf, ng, nm = HEAD_DIM, self.nf, self.ng, self.nm
        sm, b0 = self.o_sm * hd, self.cols_a
        return jnp.concatenate([
            g[:, :3 * nf * hd], g[:, sm:sm + nf], g[:, b0:b0 + 3 * ng * hd],
            g[:, b0 + self.o_gz * hd:b0 + self.o_mq * hd], g[:, sm + nf:sm + nf + ng],
            g[:, sm + nf + ng:sm + nf + 2 * ng], g[:, b0 + self.o_mq * hd:b0 + (self.o_mq + nm) * hd]], axis=1)


def _lane_row(pieces):
    row = jnp.zeros((1, HEAD_DIM), F32)
    for off, a in pieces:
        row = lax.dynamic_update_slice(row, a.astype(F32), (0, off))
    return row


def _local_step(x, mem, target, prefetch, weights, reducer, sp):
    t, d = x.shape
    lay = _Layout(d)
    nf, ng, nm, hd = lay.nf, lay.ng, lay.nm, HEAD_DIM
    nch = t // CHUNK
    tq = _tile(t, 256)
    tk = tq

    u = _norm_fwd(x, 0, sp["norm_mix"], 1, d, BF16, name="norm_mix_fwd")
    prefetch("in_a", u)
    (win_a,) = weights("in_a", u)
    p_a = _mm(u, win_a, name="mm_in_a")
    pa = _lane_row([(nf, sp["gdn_a_log"])])
    pb = _lane_row([(0, sp["fox_f_bias"]), (nf, sp["gdn_dt_bias"])])
    vals, csum = _small_fwd(p_a, lay.o_sm, pa, pb, nf, ng)

    c_t = csum[:, :nf].T
    cc, cr = c_t.reshape(nf, t, 1), c_t.reshape(nf, t // tk, 1, tk)
    fq = _norm_fwd(p_a, lay.o_fq, sp["fox_q_norm"], nf, hd, BF16, name="fox_qnorm_fwd")
    fk = _norm_fwd(p_a, lay.o_fk, sp["fox_k_norm"], nf, hd, BF16, name="fox_knorm_fwd")
    fv = p_a[:, lay.o_fv * hd:(lay.o_fv + nf) * hd].astype(BF16)
    o_fox, lse, mix = _fox_fwd(fq, fk, fv, cc, cr, nf, tq, tk, d)

    prefetch("in_b", lse)
    (win_b,) = weights("in_b", lse)
    prefetch("mixer", win_b)
    p = _mm(u, win_b, name="mm_in_b")
    wmkv, conv_taps = weights("mixer", p)
    sp = dict(sp, gdn_conv=conv_taps)
    qkv = _conv_fwd(p, lay.o_gq, sp["gdn_conv"], ng)
    g_t, b_t = vals[:, nf:nf + ng].T, vals[:, nf + ng:nf + 2 * ng].T
    gcol, grow, bcol = g_t.reshape(ng, t, 1), g_t.reshape(ng, nch, 1, CHUNK), b_t.reshape(ng, t, 1)
    o_g, states = _gdn_fwd(qkv, gcol, grow, bcol, ng)
    mix = _norm_fwd(o_g, 0, sp["gdn_out_norm"], ng, hd, BF16, z=p, zoff=lay.o_gz, into=mix, into_off=nf,
                    name="gdn_out_fwd")
    prefetch("out", mix)

    mem_n = _norm_fwd(mem, 0, sp["mem_norm"], 1, d, BF16, name="mem_norm_fwd")
    mkv = _mm(mem_n, wmkv, name="mm_memkv")
    tq_mem = _tile(t, 1024)
    mix = _mem_fwd(p, lay.o_mq, mkv, sp["mem_q_norm"], sp["mem_k_norm"], tq_mem, mix, nf + ng)
    prefetch("gate_up", mix)
    (wout,) = weights("out", mix)
    h1 = _mm(mix, wout, res=x, name="mm_out")
    n2 = _norm_fwd(h1, 0, sp["norm_ffn"], 1, d, BF16, name="norm_ffn_fwd")
    (wgu,) = weights("gate_up", n2)
    wgu4 = wgu.reshape(4, d, -1)
    gu, act = _ffn_up(n2, wgu4)
    prefetch("down", act)
    (wd,) = weights("down", act)
    h2 = _mm(act, wd, res=h1, name="mm_down")
    loss_blk, dh2, dh2_b = _loss_head(h2, target)

    g = {}
    token = reducer.pair("w_down", _mm(act, dh2_b, ta=True, out_dtype=BF16, name="mm_dw_down"))
    dgu = _ffn_dact(dh2_b, wd, gu, token)
    dw_gate_up = _mm(n2, dgu, ta=True, stack="out", out_dtype=BF16, name="mm_dw_gate_up").reshape(wgu.shape)
    token = reducer.pair("w_gate_up", dw_gate_up)
    dn2 = _mm(dgu, wgu4, tb=True, stack="sum", after=token, name="mm_dn2")
    token = reducer.ship("ffn", ["w_down", "w_gate_up"], dn2)
    dh1, g["norm_ffn"] = _norm_bwd(h1, 0, sp["norm_ffn"] + token[0, 0], dn2, 0, 1, d, res=dh2,
                                   name="norm_ffn_bwd")
    token = reducer.pair("w_out", _mm(mix, dh1, ta=True, out_dtype=BF16, name="mm_dw_out"))
    dmix = _mm(dh1, wout, tb=True, after=token, name="mm_dmix")

    dmq, dmk, dmv, g["mem_q_norm"], g["mem_k_norm"] = _mem_bwd(
        p, lay.o_mq, mkv, sp["mem_q_norm"], sp["mem_k_norm"], dmix, nf + ng, tq_mem)
    dmkv = jnp.concatenate([dmk, dmv], axis=1)
    token = reducer.pair("w_mem_kv", _mm(mem_n, dmkv, ta=True, out_dtype=BF16, name="mm_dw_memkv"))
    dmem_n = _mm(dmkv, wmkv, tb=True, after=token, name="mm_dmem")
    token = reducer.ship("mix", ["w_out", "w_mem_kv"], dmem_n)
    _, g["mem_norm"] = _norm_bwd(mem, 0, sp["mem_norm"], dmem_n, 0, 1, d, name="mem_norm_bwd")

    do_g, dgz, g["gdn_out_norm"] = _norm_bwd(o_g, 0, sp["gdn_out_norm"] + token[0, 0], dmix, nf, ng, hd, z=p,
                                             zoff=lay.o_gz, name="gdn_out_bwd")
    dq, dk, dv, dgc, dgr, dbc = _gdn_bwd(qkv, gcol, grow, bcol, states, do_g, ng)
    dgqkv, g["gdn_conv"] = _conv_bwd(p, lay.o_gq, sp["gdn_conv"], (dq, dk, dv), ng)
    dg_t = dgc.reshape(ng, t) + dgr.reshape(ng, t)
    db_t = dbc.reshape(ng, t)

    dfq_n, dfk_n, dfv, dcc, dcr = _fox_bwd(fq, fk, fv, cc, cr, o_fox, lse, dmix, nf, tq, tk)
    dfq, g["fox_q_norm"] = _norm_bwd(p_a, lay.o_fq, sp["fox_q_norm"], dfq_n, 0, nf, hd, name="fox_qnorm_bwd")
    dfk, g["fox_k_norm"] = _norm_bwd(p_a, lay.o_fk, sp["fox_k_norm"], dfk_n, 0, nf, hd, name="fox_knorm_bwd")
    dc_t = dcc.reshape(nf, t) + dcr.reshape(nf, t)

    lanes_left = hd - nf - 2 * ng
    dvals = jnp.concatenate([jnp.zeros((t, nf), F32), dg_t.T, db_t.T, jnp.zeros((t, lanes_left), F32)], axis=1)
    dcsum = jnp.concatenate([dc_t.T, jnp.zeros((t, hd - nf), F32)], axis=1)
    dsm, dpa, dpb = _small_bwd(p_a, lay.o_sm, pa, pb, dvals, dcsum, nf, ng)
    g["fox_f_bias"] = dpb[:, :nf]
    g["gdn_dt_bias"] = dpb[:, nf:nf + ng]
    g["gdn_a_log"] = dpa[:, nf:nf + ng]

    zeros = lambda n: jnp.zeros((t, n), F32)
    dp_a = jnp.concatenate([dfq, dfk, dfv, dsm, zeros(lay.cols_a - (lay.o_sm + 1) * hd)], axis=1).astype(BF16)
    dp_b = jnp.concatenate([dgqkv, dgz, dmq, zeros(lay.cols_b - (lay.o_mq + nm) * hd)], axis=1).astype(BF16)
    token = reducer.start("in", {"w_in_a": _mm(u, dp_a, ta=True, out_dtype=BF16, name="mm_dw_in_a"),
                                 "w_in_b": _mm(u, dp_b, ta=True, out_dtype=BF16, name="mm_dw_in_b")})
    du = _mm(dp_a, win_a, tb=True, after=token, name="mm_du_a")
    du = _mm(dp_b, win_b, tb=True, res=du, name="mm_du_b")
    dx, g["norm_mix"] = _norm_bwd(x, 0, sp["norm_mix"], du, 0, 1, d, res=dh1, name="norm_mix_bwd")
    return loss_blk, dx, g


ANY = pl.BlockSpec(memory_space=pl.ANY)


def _me():
    x, y, c = lax.axis_index("x"), lax.axis_index("y"), lax.axis_index("c")
    chips = [(1 - x, y), (x, 1 - y), (1 - x, 1 - y)]
    return x, y, c, chips


def _slot(axis, k):
    return k if axis == 0 else 2 * (k % 2) + k // 2


def _slab(ref, axis, rows, cols, k, h):
    half = rows // 2
    return ref.at[pl.ds(_slot(axis, k) * rows + h * half, half), :]


def _remote(src, dst, send_sem, recv_sem, dev):
    return pltpu.make_async_remote_copy(src_ref=src, dst_ref=dst, send_sem=send_sem, recv_sem=recv_sem,
                                        device_id=dev, device_id_type=MESH)


HBM = pl.BlockSpec(memory_space=pltpu.HBM)
SEM = pl.BlockSpec(memory_space=pltpu.SEMAPHORE)
SPLIT = pltpu.CompilerParams(has_side_effects=pltpu.SideEffectType.DATAFLOW_SIDE_EFFECTING)
TOKEN = jax.ShapeDtypeStruct((8, HEAD_DIM), F32)


def _in_hbm(v):
    return pltpu.with_memory_space_constraint(v, pltpu.HBM)


def _cast_place(shard, axis, name, col_fn=None, out_cols=None, after=None):
    r, c = shard.shape
    oc = out_cols or c
    tr = _tile(r, 512 if col_fn is None else 64, 16)
    tc = _tile(c, 2048) if col_fn is None else c
    otc = tc if col_fn is None else oc
    nb = r // tr
    chip = 2 * lax.axis_index("x") + lax.axis_index("y")
    slot = jnp.reshape(_slot(axis, chip), (1,)).astype(jnp.int32)

    def body(slot_ref, x_ref, *rest):
        x = x_ref[...]
        rest[-1][...] = (x if col_fn is None else col_fn(x)).astype(BF16)

    extra = [] if after is None else [after]
    return pl.pallas_call(
        body, name=name,
        grid_spec=pltpu.PrefetchScalarGridSpec(
            num_scalar_prefetch=1, grid=(nb, c // tc),
            in_specs=[pl.BlockSpec((tr, tc), lambda i, l, s: (i, l))] + [ANY] * len(extra),
            out_specs=pl.BlockSpec((tr, otc), lambda i, l, s: (s[0] * nb + i, l))),
        out_shape=jax.ShapeDtypeStruct((4 * r, oc), BF16),
        compiler_params=_cparams(("parallel", "parallel")),
    )(slot, shard, *extra)


def _gather_start(bufs, axes, shapes, groups, name):
    n = len(bufs)

    def body(*refs):
        dst = refs[n:2 * n]
        sems = refs[2 * n:2 * n + 2 * len(groups)]
        token = refs[-1]
        x, y, c, chips = _me()
        k = 2 * x + y
        for gi, ws in enumerate(groups):
            for i, w in enumerate(ws):
                r, cl = shapes[w]
                place = _slab(dst[w], axes[w], r, cl, k, c)
                for j, (px, py) in enumerate(chips):
                    _remote(place, place, sems[2 * gi].at[3 * i + j], sems[2 * gi + 1].at[3 * i + j],
                            (px, py, c)).start()
        token[...] = jnp.zeros_like(token)

    sem_shapes = [pltpu.SemaphoreType.DMA((3 * len(ws),)) for ws in groups for _ in range(2)]
    outs = pl.pallas_call(
        body, name=name, in_specs=[HBM] * n,
        out_specs=[HBM] * n + [SEM] * len(sem_shapes) + [pl.BlockSpec(memory_space=pltpu.VMEM)],
        out_shape=[pltpu.HBM(b.shape, b.dtype) for b in bufs] + sem_shapes + [TOKEN],
        input_output_aliases={w: w for w in range(n)}, compiler_params=SPLIT,
    )(*[_in_hbm(b) for b in bufs])
    sems = outs[n:-1]
    return outs[:n], [(sems[2 * g], sems[2 * g + 1]) for g in range(len(groups))], outs[-1]


def _gather_wait(bufs, axes, shapes, sems, after, name):
    n = len(bufs)

    def body(*refs):
        send_sems, recv_sems = refs[n], refs[n + 1]
        dst = refs[n + 3:]
        x, y, c, chips = _me()
        k = 2 * x + y
        for i in range(n):
            r, cl = shapes[i]
            for j, (px, py) in enumerate(chips):
                got = _slab(dst[i], axes[i], r, cl, 2 * px + py, c)
                _remote(got, got, send_sems.at[3 * i + j], recv_sems.at[3 * i + j], (px, py, c)).wait_recv()
        for i in range(n):
            r, cl = shapes[i]
            mine = _slab(dst[i], axes[i], r, cl, k, c)
            for j, (px, py) in enumerate(chips):
                _remote(mine, mine, send_sems.at[3 * i + j], recv_sems.at[3 * i + j], (px, py, c)).wait_send()

    return pl.pallas_call(
        body, name=name, in_specs=[HBM] * n + [SEM, SEM, ANY], out_specs=[HBM] * n,
        out_shape=[pltpu.HBM(b.shape, b.dtype) for b in bufs],
        input_output_aliases={i: i for i in range(n)}, compiler_params=SPLIT,
    )(*bufs, sems[0], sems[1], after)


def _gather_forward(bufs, axes, shapes, name):
    n = len(bufs)

    def body(*refs):
        dst = refs[n:2 * n]
        send_sems, recv_sems = refs[2 * n:]
        x, y, c, chips = _me()
        sibling = (x, y, 1 - c)
        sends = []
        for i in range(n):
            r, cl = shapes[i]
            for j, (px, py) in enumerate(chips):
                got = _slab(dst[i], axes[i], r, cl, 2 * px + py, c)
                cp = _remote(got, got, send_sems.at[3 * i + j], recv_sems.at[3 * i + j], sibling)
                cp.start()
                sends.append(cp)
        for i in range(n):
            r, cl = shapes[i]
            for j, (px, py) in enumerate(chips):
                got = _slab(dst[i], axes[i], r, cl, 2 * px + py, 1 - c)
                _remote(got, got, send_sems.at[3 * i + j], recv_sems.at[3 * i + j], sibling).wait_recv()
        for cp in sends:
            cp.wait_send()

    return pl.pallas_call(
        body, name=name, in_specs=[ANY] * n, out_specs=[ANY] * n,
        out_shape=[jax.ShapeDtypeStruct(b.shape, b.dtype) for b in bufs],
        input_output_aliases={i: i for i in range(n)},
        scratch_shapes=[pltpu.SemaphoreType.DMA((3 * n,)), pltpu.SemaphoreType.DMA((3 * n,))],
    )(*bufs)


def _split_start(name, arrays, geometry, count):
    n = len(arrays)

    def body(*refs):
        send, recv, token = refs[2 * n:]
        for i, (src, dst, _, dev) in enumerate(geometry(refs[n:2 * n])):
            _remote(src, dst, send.at[i], recv.at[i], dev).start()
        token[...] = jnp.zeros_like(token)

    sem = pltpu.SemaphoreType.DMA((count,))
    outs = pl.pallas_call(
        body, name=name, in_specs=[HBM] * n,
        out_specs=[HBM] * n + [SEM, SEM, pl.BlockSpec(memory_space=pltpu.VMEM)],
        out_shape=[pltpu.HBM(v.shape, v.dtype) for v in arrays] + [sem, sem, TOKEN],
        input_output_aliases={i: i for i in range(n)}, compiler_params=SPLIT,
    )(*[_in_hbm(v) for v in arrays])
    return list(outs[:n]), (outs[n], outs[n + 1]), outs[-1]


def _split_wait(name, arrays, sems, after, geometry):
    n = len(arrays)

    def body(*refs):
        send, recv = refs[n], refs[n + 1]
        copies = geometry(refs[n + 3:])
        for i, (_, _, land, dev) in enumerate(copies):
            _remote(land, land, send.at[i], recv.at[i], dev).wait_recv()
        for i, (src, _, _, dev) in enumerate(copies):
            _remote(src, src, send.at[i], recv.at[i], dev).wait_send()

    return list(pl.pallas_call(
        body, name=name, in_specs=[HBM] * n + [SEM, SEM, ANY], out_specs=[HBM] * n,
        out_shape=[pltpu.HBM(v.shape, v.dtype) for v in arrays],
        input_output_aliases={i: i for i in range(n)}, compiler_params=SPLIT,
    )(*arrays, sems[0], sems[1], after))


def _forward_geometry(axes, shapes):
    def geometry(bufs):
        x, y, c, chips = _me()
        out = []
        for i, buf in enumerate(bufs):
            r, cl = shapes[i]
            for px, py in chips:
                got = _slab(buf, axes[i], r, cl, 2 * px + py, c)
                out.append((got, got, _slab(buf, axes[i], r, cl, 2 * px + py, 1 - c), (x, y, 1 - c)))
        return out
    return geometry


def _pair_geometry(axes, shapes):
    def geometry(refs):
        n = len(refs) // 2
        x, y, c, _ = _me()
        out = []
        for w in range(n):
            r, cl = shapes[w]
            for j in range(4):
                land = refs[n + w].at[j]
                out.append((_slab(refs[w], axes[w], r, cl, j, 1 - c), land, land, (x, y, 1 - c)))
        return out
    return geometry


def _pair_exchange(fulls, axes, shapes, tag):
    n = len(fulls)

    def body(*refs):
        src, dst = refs[:n], refs[n:2 * n]
        send_sems, recv_sems = refs[2 * n:]
        x, y, c, _ = _me()
        sibling = (x, y, 1 - c)
        cps = []
        for w in range(n):
            r, cl = shapes[w]
            for j in range(4):
                cp = _remote(_slab(src[w], axes[w], r, cl, j, 1 - c), dst[w].at[j],
                             send_sems.at[4 * w + j], recv_sems.at[4 * w + j], sibling)
                cp.start()
                cps.append(cp)
        for cp in cps:
            cp.wait()

    out_shape = [jax.ShapeDtypeStruct((4, r // 2, cl), f.dtype) for (r, cl), f in zip(shapes, fulls)]
    return pl.pallas_call(
        body, name="reduce_pair_exchange_" + tag, in_specs=[ANY] * n, out_specs=[ANY] * n, out_shape=out_shape,
        scratch_shapes=[pltpu.SemaphoreType.DMA((4 * n,)), pltpu.SemaphoreType.DMA((4 * n,))],
    )(*fulls)


def _chip_start(parts, tag):
    n = len(parts)

    def body(*refs):
        src, land = refs[2 * n:3 * n], refs[3 * n:4 * n]
        send_sems, recv_sems, token = refs[4 * n:]
        x, y, c, chips = _me()
        k = 2 * x + y
        for w in range(n):
            for j, (px, py) in enumerate(chips):
                _remote(src[w].at[2 * px + py], land[w].at[k], send_sems.at[3 * w + j], recv_sems.at[3 * w + j],
                        (px, py, c)).start()
        token[...] = jnp.zeros_like(token)

    lands = [lax.empty(p.shape, p.dtype) for p in parts]
    sem = pltpu.SemaphoreType.DMA((3 * n,))
    outs = pl.pallas_call(
        body, name="reduce_ici_start_" + tag, in_specs=[HBM] * (2 * n),
        out_specs=[HBM] * (2 * n) + [SEM, SEM, pl.BlockSpec(memory_space=pltpu.VMEM)],
        out_shape=[pltpu.HBM(p.shape, p.dtype) for p in parts + lands] + [sem, sem, TOKEN],
        input_output_aliases={i: i for i in range(2 * n)}, compiler_params=SPLIT,
    )(*[_in_hbm(v) for v in parts + lands])
    return outs[:n], outs[n:2 * n], outs[2 * n], outs[2 * n + 1], outs[-1]


def _chip_wait(parts, lands, send_sems, recv_sems, after, tag):
    n = len(parts)

    def body(*refs):
        send, recv = refs[2 * n], refs[2 * n + 1]
        src, land = refs[2 * n + 3:3 * n + 3], refs[3 * n + 3:]
        x, y, c, chips = _me()
        for w in range(n):
            for j, (px, py) in enumerate(chips):
                got = land[w].at[2 * px + py]
                _remote(got, got, send.at[3 * w + j], recv.at[3 * w + j], (px, py, c)).wait_recv()
        for w in range(n):
            for j, (px, py) in enumerate(chips):
                sent = src[w].at[2 * px + py]
                _remote(sent, sent, send.at[3 * w + j], recv.at[3 * w + j], (px, py, c)).wait_send()

    outs = pl.pallas_call(
        body, name="reduce_ici_wait_" + tag, in_specs=[HBM] * (2 * n) + [SEM, SEM, ANY], out_specs=[HBM] * (2 * n),
        out_shape=[pltpu.HBM(p.shape, p.dtype) for p in parts + lands],
        input_output_aliases={i: i for i in range(2 * n)}, compiler_params=SPLIT,
    )(*parts, *lands, send_sems, recv_sems, after)
    chip = 2 * lax.axis_index("x") + lax.axis_index("y")
    return [lax.dynamic_update_slice(s, lax.dynamic_index_in_dim(p, chip, 0, keepdims=True), (chip, 0, 0))
            for p, s in zip(outs[:n], outs[n:])]


def _half_swap(halves, tag):
    n = len(halves)
    core = lax.axis_index("c")
    bufs = [lax.dynamic_update_slice(lax.empty((2,) + h.shape, h.dtype), h[None], (core, 0, 0)) for h in halves]

    def body(*refs):
        dst = refs[n:2 * n]
        send_sems, recv_sems = refs[2 * n:]
        x, y, c, _ = _me()
        sibling = (x, y, 1 - c)
        cps = []
        for w in range(n):
            cp = _remote(dst[w].at[c], dst[w].at[c], send_sems.at[w], recv_sems.at[w], sibling)
            cp.start()
            cps.append(cp)
        for w in range(n):
            other = dst[w].at[1 - c]
            _remote(other, other, send_sems.at[w], recv_sems.at[w], sibling).wait_recv()
        for cp in cps:
            cp.wait_send()

    outs = pl.pallas_call(
        body, name="reduce_half_swap_" + tag, in_specs=[ANY] * n, out_specs=[ANY] * n,
        out_shape=[jax.ShapeDtypeStruct(b.shape, b.dtype) for b in bufs],
        input_output_aliases={w: w for w in range(n)},
        scratch_shapes=[pltpu.SemaphoreType.DMA((n,)), pltpu.SemaphoreType.DMA((n,))],
    )(*bufs)
    return [o.reshape(2 * o.shape[1], o.shape[2]) for o in outs]


def _add_parts(full, axis, rows, sib, name):
    _, r, c = sib.shape
    tr, tc = _tile(r, 1024, 16), _tile(c, 2048)
    nb = r // tr
    core = jnp.reshape(lax.axis_index("c"), (1,)).astype(jnp.int32)

    def body(c_ref, a_ref, b_ref, o_ref):
        o_ref[0] = (a_ref[...].astype(F32) + b_ref[0].astype(F32)).astype(BF16)

    blk = pl.BlockSpec((1, tr, tc), lambda j, i, l, cr: (j, i, l))
    return pl.pallas_call(
        body, name=name,
        grid_spec=pltpu.PrefetchScalarGridSpec(
            num_scalar_prefetch=1, grid=(4, nb, c // tc),
            in_specs=[pl.BlockSpec((tr, tc), lambda j, i, l, cr: ((_slot(axis, j) * 2 + cr[0]) * nb + i, l)), blk],
            out_specs=blk),
        out_shape=jax.ShapeDtypeStruct(sib.shape, BF16),
        compiler_params=_cparams(("parallel", "parallel", "parallel")),
    )(core, full, sib)


def _sum_slots(a, name):
    _, r, c = a.shape
    tr, tc = _tile(r, 512, 8), _tile(c, 2048)

    def body(a_ref, o_ref):
        v = a_ref[...].astype(F32)
        o_ref[...] = ((v[0] + v[1]) + v[2]) + v[3]

    return pl.pallas_call(
        body, name=name, grid=(r // tr, c // tc),
        in_specs=[pl.BlockSpec((4, tr, tc), lambda i, l: (0, i, l))],
        out_specs=pl.BlockSpec((tr, tc), lambda i, l: (i, l)),
        out_shape=jax.ShapeDtypeStruct((r, c), F32),
        compiler_params=_cparams(("parallel", "parallel")),
    )(a)


class _Reducer:
    def __init__(self, spec):
        self.spec = spec
        self.paired = {}
        self.pending = []

    def pair(self, name, full):
        ax, shp = self.spec[name]
        land = lax.empty((4, shp[0] // 2, shp[1]), full.dtype)
        arrays, sems, token = _split_start("reduce_pair_start_" + name, [full, land], _pair_geometry([ax], [shp]), 4)
        self.paired[name] = (arrays, sems)
        return token

    def ship(self, tag, names, after):
        parts = []
        for n in names:
            ax, shp = self.spec[n]
            arrays, sems = self.paired.pop(n)
            full, sib = _split_wait("reduce_pair_wait_" + n, arrays, sems, after, _pair_geometry([ax], [shp]))
            parts.append(_add_parts(full, ax, shp[0], sib, name=f"reduce_add_{n}"))
        parts, lands, send, recv, token = _chip_start(parts, tag)
        self.pending.append((tag, names, parts, lands, send, recv))
        return token

    def start(self, tag, grads):
        names = list(grads)
        fulls, axes = [grads[n] for n in names], [self.spec[n][0] for n in names]
        shapes = [self.spec[n][1] for n in names]
        from_sibling = _pair_exchange(fulls, axes, shapes, tag)
        parts = [_add_parts(f, a, r, s, name=f"reduce_add_{n}")
                 for n, f, a, (r, cl), s in zip(names, fulls, axes, shapes, from_sibling)]
        parts, lands, send, recv, token = _chip_start(parts, tag)
        self.pending.append((tag, names, parts, lands, send, recv))
        return token

    def finish(self, after, tags):
        out = {}
        for tag, names, parts, lands, send, recv in [p for p in self.pending if p[0] in tags]:
            slots = _chip_wait(parts, lands, send, recv, after, tag)
            halves = [_sum_slots(s, name=f"reduce_sum_{n}") for n, s in zip(names, slots)]
            out.update(zip(names, _half_swap(halves, tag)))
        return out


def _allreduce_small(pack, after):
    rows = pack.shape[0]

    def body(p_ref, _, o_ref, slots, send_sems, recv_sems):
        x, y, c, _ = _me()
        me = 4 * x + 2 * y + c
        slots[me] = p_ref[...]
        cps = []
        for r in range(1, 8):
            peer = (x ^ (r >> 2), y ^ ((r >> 1) & 1), c ^ (r & 1))
            cp = _remote(p_ref, slots.at[me], send_sems.at[r - 1], recv_sems.at[r - 1], peer)
            cp.start()
            cps.append(cp)
        for r in range(1, 8):
            frm = me ^ r
            _remote(slots.at[frm], slots.at[frm], send_sems.at[r - 1], recv_sems.at[r - 1], (x, y, c)).wait_recv()
        for cp in cps:
            cp.wait_send()
        acc = slots[0]
        for s in range(1, 8):
            acc = acc + slots[s]
        o_ref[...] = acc

    vm = pl.BlockSpec(memory_space=pltpu.VMEM)
    return pl.pallas_call(
        body, name="allreduce_small", in_specs=[vm, ANY], out_specs=vm,
        out_shape=jax.ShapeDtypeStruct(pack.shape, F32),
        scratch_shapes=[pltpu.VMEM((8, rows, HEAD_DIM), F32), pltpu.SemaphoreType.DMA((7,)),
                        pltpu.SemaphoreType.DMA((7,))],
    )(pack, after)


_ROWS = ["norm_mix", "norm_ffn", "mem_norm", "fox_q_norm", "fox_k_norm", "gdn_out_norm", "mem_q_norm",
         "mem_k_norm", "fox_f_bias", "gdn_a_log", "gdn_dt_bias"]


def _pack_rows(vals):
    out = []
    for name in _ROWS:
        v = vals[name].reshape(-1)
        n = -(-v.shape[0] // HEAD_DIM) * HEAD_DIM
        out.append(jnp.pad(v, (0, n - v.shape[0])).reshape(-1, HEAD_DIM))
    return jnp.concatenate(out, axis=0)


def _unpack_rows(pack, like):
    out, r = {}, 0
    for name in _ROWS:
        n = like[name].shape[-1]
        nr = -(-n // HEAD_DIM)
        out[name] = pack[r:r + nr].reshape(1, -1)[:, :n]
        r += nr
    return out, r


def kernel(x, mem, norm_mix, w_in, fox_f_bias, fox_q_norm, fox_k_norm, gdn_conv, gdn_a_log, gdn_dt_bias, gdn_out_norm, mem_norm, w_mem_kv, mem_q_norm, mem_k_norm, w_out, norm_ffn, w_gate_up, w_down, loss_target, m_norm_mix, m_w_in, m_fox_f_bias, m_fox_q_norm, m_fox_k_norm, m_gdn_conv, m_gdn_a_log, m_gdn_dt_bias, m_gdn_out_norm, m_mem_norm, m_w_mem_kv, m_mem_q_norm, m_mem_k_norm, m_w_out, m_norm_ffn, m_w_gate_up, m_w_down, v_norm_mix, v_w_in, v_fox_f_bias, v_fox_q_norm, v_fox_k_norm, v_gdn_conv, v_gdn_a_log, v_gdn_dt_bias, v_gdn_out_norm, v_mem_norm, v_w_mem_kv, v_mem_q_norm, v_mem_k_norm, v_w_out, v_norm_ffn, v_w_gate_up, v_w_down):
    a = dict(locals())
    d = x.shape[-1]
    lay = _Layout(d)
    chip = 2 * lax.axis_index("x") + lax.axis_index("y")
    small = {n: a[n] for n in _ROWS}
    big = ["w_in", "w_mem_kv", "w_out", "w_gate_up", "w_down"]
    axes = [0, 0, 0, 1, 0]

    conv_cols = gdn_conv.shape[-1]
    conv_n = CONV_WIDTH * conv_cols
    conv_rows = -(-conv_n // HEAD_DIM)
    conv_blk = jnp.pad(gdn_conv.reshape(-1), (0, 32 * HEAD_DIM - conv_n)).reshape(32, HEAD_DIM)
    axis_of = dict(zip(big, axes), conv=0, w_in_a=0, w_in_b=0)
    shape_of = {n: a[n].shape[1:] for n in big[1:]}
    shape_of.update(w_in_a=(w_in.shape[1], lay.cols_a), w_in_b=(w_in.shape[1], lay.cols_b), conv=conv_blk.shape)
    placed = {"w_in_a": _cast_place(w_in[0], 0, "cast_w_in_a", lambda v: lay.regroup(v)[:, :lay.cols_a], lay.cols_a),
              "conv": lax.dynamic_update_slice(lax.empty((4 * 32, HEAD_DIM), F32), conv_blk, (chip * 32, 0))}
    grouped = {"in_a": ["w_in_a"], "in_b": ["w_in_b"], "mixer": ["w_mem_kv", "conv"], "out": ["w_out"],
               "gate_up": ["w_gate_up"], "down": ["w_down"]}
    inflight = {}

    def start(tags, name):
        names = [n for t in tags for n in grouped[t]]
        bufs, sems, token = _gather_start([placed[n] for n in names], [axis_of[n] for n in names],
                                          [shape_of[n] for n in names],
                                          [[names.index(n) for n in grouped[t]] for t in tags], name)
        for t, pair in zip(tags, sems):
            inflight[t] = ([bufs[names.index(n)] for n in grouped[t]], pair)
        return token

    first = start(["in_a"], "gather_ici_start_in")
    placed["w_in_b"] = _cast_place(w_in[0], 0, "cast_w_in_b", lambda v: lay.regroup(v)[:, lay.cols_a:], lay.cols_b,
                                   after=first)
    placed.update({n: _cast_place(a[n][0], axis_of[n], "cast_" + n, after=first) for n in big[1:]})
    all_started = start(["in_b", "mixer", "out", "gate_up", "down"], "gather_ici_start_rest")

    forwarding = {}

    def prefetch(tag, after):
        bufs, sem_pair = inflight.pop(tag)
        ax, shp = [axis_of[n] for n in grouped[tag]], [shape_of[n] for n in grouped[tag]]
        got = _gather_wait(bufs, ax, shp, sem_pair, all_started if tag == "in_a" else after,
                           "gather_ici_wait_" + tag)
        geometry = _forward_geometry(ax, shp)
        got, sems, _ = _split_start("gather_forward_start_" + tag, got, geometry, 3 * len(got))
        forwarding[tag] = (got, sems, geometry)

    def weights(tag, after):
        got, sems, geometry = forwarding.pop(tag)
        got = _split_wait("gather_forward_wait_" + tag, got, sems, after, geometry)
        if tag != "mixer":
            return got
        taps = got[1].reshape(4, 32 * HEAD_DIM)[:, :conv_n].reshape(4, CONV_WIDTH, conv_cols)
        return got[0], jnp.transpose(taps, (1, 0, 2)).reshape(CONV_WIDTH, 4 * conv_cols)

    sp = dict(small)
    reducer = _Reducer({n: (axis_of[n], shape_of[n]) for n in big[1:] + ["w_in_a", "w_in_b"]})
    loss_blk, dx, g = _local_step(x[0], mem[0], loss_target[0], prefetch, weights, reducer, sp)

    gsmall = {n: g[n] for n in _ROWS}
    pack = jnp.concatenate([_pack_rows(gsmall), g["gdn_conv"].reshape(-1, HEAD_DIM), loss_blk], axis=0)
    pack = jnp.pad(pack, ((0, -pack.shape[0] % 8), (0, 0)))
    out = {"grad_x": dx[None]}

    def adamw_shards(reduced):
        if "w_in_a" in reduced:
            reduced = {"w_in": (reduced["w_in_a"], reduced["w_in_b"])}
        for n, gsh in reduced.items():
            join = (lambda ga, gb: lay.ungroup(jnp.concatenate([ga, gb], axis=1))) if n == "w_in" else None
            res = _adamw(a[n][0], gsh, a["m_" + n][0], a["v_" + n][0], g_fn=join, name="adamw_" + n)
            for pre, r in zip(["grad_", "delta_", "new_m_", "new_v_"], res):
                out[pre + n] = r[None]
        return res[0]

    done = adamw_shards(reducer.finish(dx, ("ffn", "mix")))
    tot = _allreduce_small(pack, done)
    gs, r0 = _unpack_rows(tot, small)
    conv_g = tot[r0:r0 + CONV_WIDTH * 4 * conv_cols // HEAD_DIM].reshape(CONV_WIDTH, 4 * conv_cols)
    gs_conv = lax.dynamic_slice_in_dim(conv_g, chip * conv_cols, conv_cols, axis=1)
    out["loss"] = tot[r0 + CONV_WIDTH * 4 * conv_cols // HEAD_DIM, 0]
    adamw_shards(reducer.finish(tot, ("in",)))
    conv_pad = lambda v: jnp.pad(v.reshape(-1), (0, conv_rows * HEAD_DIM - conv_n)).reshape(conv_rows, HEAD_DIM)
    packs = []
    for src, cv in [(small, gdn_conv), (gs, gs_conv), ({n: a["m_" + n] for n in _ROWS}, m_gdn_conv),
                    ({n: a["v_" + n] for n in _ROWS}, v_gdn_conv)]:
        packs.append(jnp.concatenate([_pack_rows(src), conv_pad(cv)], axis=0))
    res = _adamw(*packs, name="adamw_small")
    for pre, r in zip(["grad_", "delta_", "new_m_", "new_v_"], res):
        vals, r1 = _unpack_rows(r, small)
        for n in _ROWS:
            out[pre + n] = vals[n]
        out[pre + "gdn_conv"] = r[r1:r1 + conv_rows].reshape(-1)[:conv_n].reshape(gdn_conv.shape)
    names = ["norm_mix", "w_in", "fox_f_bias", "fox_q_norm", "fox_k_norm", "gdn_conv", "gdn_a_log", "gdn_dt_bias",
             "gdn_out_norm", "mem_norm", "w_mem_kv", "mem_q_norm", "mem_k_norm", "w_out", "norm_ffn", "w_gate_up",
             "w_down"]
    return (out["loss"], out["grad_x"], *[out[p + n] for p in ["grad_", "delta_", "new_m_", "new_v_"] for n in names])
```

```python
import functools
import math

import jax
import jax.numpy as jnp
from jax import lax
from jax.experimental import pallas as pl
from jax.experimental.pallas import tpu as pltpu

F32, BF16 = jnp.float32, jnp.bfloat16
HEAD_DIM = 128
CHUNK = 64
N_MEM_HEADS = 4
CONV_WIDTH = 4
NORM_EPS = 1e-6
ADAM_LR, ADAM_B1, ADAM_B2, ADAM_EPS, ADAM_WD, ADAM_STEP = 0.001, 0.9, 0.999, 1e-08, 0.01, 10
VMEM_LIMIT = 48 * 1024 * 1024
NEG = -1e30
MESH = pl.DeviceIdType.MESH


def _cparams(sem=None, **kw):
    if sem is not None:
        kw["dimension_semantics"] = sem
    return pltpu.CompilerParams(vmem_limit_bytes=VMEM_LIMIT, **kw)


def _tile(n, target, mult=128):
    best = None
    d = mult
    while d <= min(n, target):
        if n % d == 0:
            best = d
        d += mult
    return best if best is not None else n


def _dot(a, b, dims, hi):
    if a.ndim == 3:
        dn = (((dims[0][0] + 1,), (dims[1][0] + 1,)), ((0,), (0,)))
    else:
        dn = (dims, ((), ()))
    if hi is not None:
        return lax.dot_general(a, b, dn, precision=hi, preferred_element_type=F32)
    return lax.dot_general(a.astype(BF16), b.astype(BF16), dn, preferred_element_type=F32)


def _make_dots(hi, cotangent=None):
    @jax.custom_vjp
    def nn(a, b):
        return _dot(a, b, ((1,), (0,)), hi)

    @jax.custom_vjp
    def nt(a, b):
        return _dot(a, b, ((1,), (1,)), hi)

    @jax.custom_vjp
    def tn(a, b):
        return _dot(a, b, ((0,), (0,)), hi)

    bnn, bnt, btn = cotangent or (nn, nt, tn)
    nn.defvjp(lambda a, b: (nn(a, b), (a, b)), lambda r, g: (bnt(g, r[1]), btn(r[0], g)))
    nt.defvjp(lambda a, b: (nt(a, b), (a, b)), lambda r, g: (bnn(g, r[1]), btn(g, r[0])))
    tn.defvjp(lambda a, b: (tn(a, b), (a, b)), lambda r, g: (bnt(r[1], g), bnn(r[0], g)))
    return nn, nt, tn


_nn, _nt, _tn = _make_dots(None)
_nn_hi, _nt_hi, _tn_hi = _make_dots(lax.Precision.HIGHEST)
_nn_x3, _nt_x3, _tn_x3 = _make_dots(lax.Precision.HIGH, (_nn, _nt, _tn))


def _sigmoid(x):
    return jax.nn.sigmoid(x)


@jax.custom_vjp
def _softplus(x):
    return jnp.maximum(x, 0.0) + jnp.log(1.0 + jnp.exp(-jnp.abs(x)))


_softplus.defvjp(lambda x: (_softplus(x), x), lambda x, g: (g * _sigmoid(x),))


def _silu(x):
    return x * _sigmoid(x)


def _rms_fn(x, gain, z=None):
    y = x * lax.rsqrt(jnp.mean(x * x, axis=-1, keepdims=True) + NORM_EPS) * gain
    if z is not None:
        y = y * _silu(z)
    return y


def _mm(a, b, *, ta=False, tb=False, out_dtype=F32, res=None, stack=None, after=None, name):
    a2, b2 = a.shape[-2:], b.shape[-2:]
    ns = b.shape[0] if stack else 1
    m = a2[1] if ta else a2[0]
    k = a2[0] if ta else a2[1]
    n = b2[0] if tb else b2[1]
    assert k == (b2[1] if tb else b2[0])
    tm, tn, tk = _mm_tiles(m, n, k, ns if stack == "sum" else 1, a.dtype.itemsize, b.dtype.itemsize,
                           jnp.dtype(out_dtype).itemsize, res is not None)
    nk = k // tk
    single = nk == 1 and stack != "sum"
    dims = ((0 if ta else 1,), (1 if tb else 0,))
    if stack == "sum":
        order = lambda g0, g1, g2, g3: (g2, g0, g1, g3)
        grid = (m // tm, n // tn, ns, nk)
    else:
        order = lambda g0, g1, g2, g3: (g0, g1, g2, g3)
        grid = (ns, m // tm, n // tn, nk)

    def body(*refs):
        if after is not None:
            refs = refs[:2 + (res is not None)] + refs[3 + (res is not None):]
        if single:
            a_ref, b_ref = refs[:2]
            r = lax.dot_general(a_ref[...].astype(BF16), b_ref[...].astype(BF16), (dims, ((), ())),
                                preferred_element_type=F32)
            if res is not None:
                r = r + refs[2][...]
            refs[-1][...] = r.astype(out_dtype)
            return
        if res is None:
            a_ref, b_ref, o_ref, acc = refs
        else:
            a_ref, b_ref, r_ref, o_ref, acc = refs
        s, _, _, kk = order(*[pl.program_id(d) for d in range(4)])
        first = kk == 0
        last = kk == nk - 1
        if stack == "sum":
            first, last = first & (s == 0), last & (s == ns - 1)

        @pl.when(first)
        def _():
            acc[...] = jnp.zeros_like(acc)

        acc[...] += lax.dot_general(a_ref[...].astype(BF16), b_ref[...].astype(BF16), (dims, ((), ())),
                                    preferred_element_type=F32)

        @pl.when(last)
        def _():
            r = acc[...]
            if res is not None:
                r = r + r_ref[...]
            o_ref[...] = r.astype(out_dtype)

    def spec(shape, idx, stacked):
        if stacked:
            return pl.BlockSpec((None,) + shape, lambda *g: (order(*g)[0],) + idx(*order(*g)))
        return pl.BlockSpec(shape, lambda *g: idx(*order(*g)))

    a_spec = (spec((tk, tm), lambda s, i, j, kk: (kk, i), stack == "sum") if ta
              else spec((tm, tk), lambda s, i, j, kk: (i, kk), stack == "sum"))
    b_spec = (spec((tn, tk), lambda s, i, j, kk: (j, kk), bool(stack)) if tb
              else spec((tk, tn), lambda s, i, j, kk: (kk, j), bool(stack)))
    o_spec = spec((tm, tn), lambda s, i, j, kk: (i, j), stack == "out")
    ins, specs = [a, b], [a_spec, b_spec]
    if res is not None:
        ins.append(res)
        specs.append(o_spec)
    if after is not None:
        ins.append(after)
        specs.append(pl.BlockSpec(after.shape, lambda *g: (0,) * after.ndim))
    sem = (("parallel", "parallel", "arbitrary", "arbitrary") if stack == "sum"
           else ("parallel", "parallel", "parallel", "arbitrary"))
    return pl.pallas_call(
        body, name=name, grid=grid, in_specs=specs, out_specs=o_spec,
        out_shape=jax.ShapeDtypeStruct(((ns,) if stack == "out" else ()) + (m, n), out_dtype),
        scratch_shapes=[] if single else [pltpu.VMEM((tm, tn), F32)],
        compiler_params=_cparams(sem),
    )(*ins)


MM_VMEM_BUDGET = 40 * 1024 * 1024
MXU_WIDTH = 256


def _mm_tiles(m, n, k, ns, sa, sb, so, has_res):
    def divs(x, mult, cap):
        out = [d for d in range(mult, min(x, cap) + 1, mult) if x % d == 0]
        return out or [x]

    best = None
    for tk in divs(k, 128, 8192):
        nk = (k // tk) * ns
        for tm in divs(m, 8, 2048):
            for tn in divs(n, 128, 2048):
                vmem = 2 * (tm * tk * sa + tk * tn * sb + tm * tn * so) + (2 * tm * tn * 4 if has_res else 0)
                vmem += tm * tn * 4 if nk > 1 else 0
                if vmem > MM_VMEM_BUDGET:
                    continue
                steps = (m // tm) * (n // tn) * nk
                traffic = (m // tm) * k * n * sb * ns + (n // tn if nk > 1 else 1) * m * k * sa * ns
                cost = steps * 0.4e-6 + traffic / 2.5e12 + (nk * m * n * 8 / 6e12 if nk > 1 else 0)
                cost += 2.0 * m * n * k * ns / 7e14 * (-(-tn // MXU_WIDTH) * MXU_WIDTH / tn)
                if best is None or cost < best[0]:
                    best = (cost, tm, tn, tk)
    return best[1:]


def _norm_fwd(x, xoff, gain, ncol, w, out_dtype, *, z=None, zoff=0, into=None, into_off=0, name):
    t = x.shape[0]
    tr = _tile(t, max(256, (1 << 18) // w), 8)

    def body(*refs):
        x_ref, g_ref, o_ref = refs[0], refs[1], refs[-1]
        y = _rms_fn(x_ref[...], g_ref[...]) if z is None else _rms_fn(x_ref[...], g_ref[...], refs[2][...])
        o_ref[...] = y.astype(out_dtype)

    ins = [x, gain]
    specs = [pl.BlockSpec((tr, w), lambda j, r: (r, xoff + j)), pl.BlockSpec((1, w), lambda j, r: (0, 0))]
    if z is not None:
        ins.append(z)
        specs.append(pl.BlockSpec((tr, w), lambda j, r: (r, zoff + j)))
    aliases = {}
    if into is not None:
        aliases = {len(ins): 0}
        ins.append(into)
        specs.append(pl.BlockSpec(memory_space=pl.ANY))
    return pl.pallas_call(
        body, name=name, grid=(ncol, t // tr), in_specs=specs,
        out_specs=pl.BlockSpec((tr, w), lambda j, r: (r, into_off + j)),
        out_shape=jax.ShapeDtypeStruct((t, ncol * w) if into is None else into.shape, out_dtype),
        input_output_aliases=aliases, compiler_params=_cparams(("parallel", "parallel")),
    )(*ins)


def _norm_bwd(x, xoff, gain, dy, dyoff, ncol, w, *, z=None, zoff=0, res=None, name):
    t = x.shape[0]
    tr = _tile(t, max(256, (1 << 18) // w), 8)

    def body(*refs):
        it = iter(refs)
        x_ref, g_ref = next(it), next(it)
        z_ref = next(it) if z is not None else None
        dy_ref = next(it)
        r_ref = next(it) if res is not None else None
        dx_ref = next(it)
        dz_ref = next(it) if z is not None else None
        dg_ref = next(it)

        @pl.when((pl.program_id(0) == 0) & (pl.program_id(1) == 0))
        def _():
            dg_ref[...] = jnp.zeros_like(dg_ref)

        args = (x_ref[...], g_ref[...]) + ((z_ref[...],) if z is not None else ())
        _, vjp = jax.vjp(_rms_fn, *args)
        grads = vjp(dy_ref[...].astype(F32))
        dx = grads[0]
        if res is not None:
            dx = dx + r_ref[...]
        dx_ref[...] = dx
        if z is not None:
            dz_ref[...] = grads[2]
        dg_ref[...] += grads[1]

    ins = [x, gain]
    specs = [pl.BlockSpec((tr, w), lambda j, r: (r, xoff + j)), pl.BlockSpec((1, w), lambda j, r: (0, 0))]
    if z is not None:
        ins.append(z)
        specs.append(pl.BlockSpec((tr, w), lambda j, r: (r, zoff + j)))
    ins.append(dy)
    specs.append(pl.BlockSpec((tr, w), lambda j, r: (r, dyoff + j)))
    blk = pl.BlockSpec((tr, w), lambda j, r: (r, j))
    if res is not None:
        ins.append(res)
        specs.append(blk)
    full = jax.ShapeDtypeStruct((t, ncol * w), F32)
    out_shape, out_specs = [full], [blk]
    if z is not None:
        out_shape.append(full)
        out_specs.append(blk)
    out_shape.append(jax.ShapeDtypeStruct((1, w), F32))
    out_specs.append(pl.BlockSpec((1, w), lambda j, r: (0, 0)))
    return pl.pallas_call(
        body, name=name, grid=(ncol, t // tr), in_specs=specs, out_specs=out_specs, out_shape=out_shape,
        compiler_params=_cparams(("arbitrary", "arbitrary")),
    )(*ins)


def _small_fn(x, pa, pb, nf, ng):
    lane = lax.broadcasted_iota(jnp.int32, x.shape, 1)
    zz = x + pb
    logf = -_softplus(-zz)
    g = -jnp.exp(pa) * _softplus(zz)
    beta = _sigmoid(x)
    return jnp.where(lane < nf, logf, jnp.where(lane < nf + ng, g, beta))


def _tri(n, upper):
    r = lax.broadcasted_iota(jnp.int32, (n, n), 0)
    c = lax.broadcasted_iota(jnp.int32, (n, n), 1)
    return jnp.where((c >= r) if upper else (c <= r), 1.0, 0.0).astype(F32)


def _small_fwd(p, off, pa, pb, nf, ng):
    t = p.shape[0]
    blk = HEAD_DIM
    nb = t // blk

    def body(x_ref, pa_ref, pb_ref, v_ref, c_ref):
        v_ref[...] = _small_fn(x_ref[...], pa_ref[...], pb_ref[...], nf, ng)
        tri = _tri(blk, False)

        carry = jnp.zeros((1, HEAD_DIM), F32)
        for i in range(nb):
            rows = slice(i * blk, (i + 1) * blk)
            c = _nn_hi(tri, v_ref[rows, :]) + carry
            c_ref[rows, :] = c
            carry = c[blk - 1:blk, :]

    row = pl.BlockSpec((1, HEAD_DIM), lambda i: (0, 0))
    out = pl.BlockSpec((t, HEAD_DIM), lambda i: (0, 0))
    return pl.pallas_call(
        body, name="small_fwd", grid=(1,),
        in_specs=[pl.BlockSpec((t, HEAD_DIM), lambda i: (0, off)), row, row], out_specs=[out, out],
        out_shape=[jax.ShapeDtypeStruct((t, HEAD_DIM), F32)] * 2,
        compiler_params=_cparams(("arbitrary",)),
    )(p, pa, pb)


def _small_bwd(p, off, pa, pb, dvals, dcsum, nf, ng):
    t = p.shape[0]
    blk = HEAD_DIM
    nb = t // blk

    def body(x_ref, pa_ref, pb_ref, dv_ref, dc_ref, dx_ref, dpa_ref, dpb_ref, tot_ref):
        tri = _tri(blk, True)

        carry = jnp.zeros((1, HEAD_DIM), F32)
        for i in reversed(range(nb)):
            rows = slice(i * blk, (i + 1) * blk)
            c = _nn_hi(tri, dc_ref[rows, :]) + carry
            tot_ref[rows, :] = c + dv_ref[rows, :]
            carry = c[0:1, :]
        f = functools.partial(_small_fn, nf=nf, ng=ng)
        _, vjp = jax.vjp(f, x_ref[...], pa_ref[...], pb_ref[...])
        dx, dpa, dpb = vjp(tot_ref[...])
        dx_ref[...] = dx
        dpa_ref[...] = dpa
        dpb_ref[...] = dpb

    row = pl.BlockSpec((1, HEAD_DIM), lambda i: (0, 0))
    full = pl.BlockSpec((t, HEAD_DIM), lambda i: (0, 0))
    return pl.pallas_call(
        body, name="small_bwd", grid=(1,),
        in_specs=[pl.BlockSpec((t, HEAD_DIM), lambda i: (0, off)), row, row, full, full],
        out_specs=[full, row, row],
        out_shape=[jax.ShapeDtypeStruct((t, HEAD_DIM), F32), jax.ShapeDtypeStruct((1, HEAD_DIM), F32),
                   jax.ShapeDtypeStruct((1, HEAD_DIM), F32)],
        scratch_shapes=[pltpu.VMEM((t, HEAD_DIM), F32)],
        compiler_params=_cparams(("arbitrary",)),
    )(p, pa, pb, dvals, dcsum)


def _fox_heads(nf, most):
    return next(h for h in range(most, 0, -1) if nf % h == 0)


def _fox_fwd(q, k, v, cc, cr, nf, tq, tk, d_mix):
    t = q.shape[0]
    scale = HEAD_DIM ** -0.5
    assert tq == tk

    vt = jnp.transpose(v.reshape(t // tk, tk, nf, HEAD_DIM), (2, 0, 3, 1))

    hp = _fox_heads(nf, 3)
    lanes = lambda h: slice(h * HEAD_DIM, (h + 1) * HEAD_DIM)

    def body(q_ref, k_ref, vt_ref, cc_ref, cr_ref, o_ref, lse_ref, mix_ref):
        i = pl.program_id(1)
        qs = [q_ref[:, lanes(h)] for h in range(hp)]
        cqs = [cr_ref[h, i] for h in range(hp)]
        ones = jnp.ones((8, tk), BF16)
        diff = lax.broadcasted_iota(jnp.int32, (tk, tq), 0) - lax.broadcasted_iota(jnp.int32, (tk, tq), 1)

        def scores(h, j):
            ks = pl.ds(pl.multiple_of(j * tk, tk), tk)
            return lax.dot_general(k_ref[ks, lanes(h)], qs[h], (((1,), (1,)), ((), ())),
                                   preferred_element_type=F32)

        def tile(h, j, m, l, acc, s, masked):
            ks = pl.ds(pl.multiple_of(j * tk, tk), tk)
            s = s * scale + cqs[h] - cc_ref[h, ks, :]
            if masked:
                s = jnp.where(diff <= 0, s, NEG)
            m_new = jnp.maximum(m, jnp.max(s, axis=0, keepdims=True))
            pr = jnp.exp(s - m_new).astype(BF16)
            alpha = jnp.exp(m - m_new)
            l = alpha * l + jnp.dot(ones, pr, preferred_element_type=F32)[:1]
            acc = alpha * acc + jnp.dot(vt_ref[h, j], pr, preferred_element_type=F32)
            return m_new, l, acc

        def step(j, carry):
            nxt = [scores(h, j + 1) for h in range(hp)]
            return tuple(tile(h, j, *carry[h], False) + (nxt[h],) for h in range(hp))

        init = tuple((jnp.full((1, tq), NEG, F32), jnp.zeros((1, tq), F32), jnp.zeros((HEAD_DIM, tq), F32),
                      scores(h, 0)) for h in range(hp))
        carry = lax.fori_loop(0, i, step, init)
        for h in range(hp):
            m, l, acc = tile(h, i, *carry[h], True)
            o = jnp.transpose(acc / l)
            o_ref[:, lanes(h)] = o
            mix_ref[:, lanes(h)] = o.astype(BF16)
            lse_ref[h, 0] = m + jnp.log(l)

    w = hp * HEAD_DIM
    qblk = pl.BlockSpec((tq, w), lambda h, i: (i, h))
    return pl.pallas_call(
        body, name="fox_fwd", grid=(nf // hp, t // tq),
        in_specs=[qblk, pl.BlockSpec((t, w), lambda h, i: (0, h)),
                  pl.BlockSpec((hp, t // tk, HEAD_DIM, tk), lambda h, i: (h, 0, 0, 0)),
                  pl.BlockSpec((hp, t, 1), lambda h, i: (h, 0, 0)),
                  pl.BlockSpec((hp, t // tk, 1, tk), lambda h, i: (h, 0, 0, 0))],
        out_specs=[qblk, pl.BlockSpec((hp, 1, 1, tq), lambda h, i: (h, i, 0, 0)), qblk],
        out_shape=[jax.ShapeDtypeStruct((t, nf * HEAD_DIM), F32), jax.ShapeDtypeStruct((nf, t // tq, 1, tq), F32),
                   jax.ShapeDtypeStruct((t, d_mix), BF16)],
        compiler_params=_cparams(("parallel", "parallel")),
    )(q, k, vt, cc, cr)


def _fox_bwd(q, k, v, cc, cr, o, lse, dmix, nf, tq, tk):
    t = q.shape[0]
    scale = HEAD_DIM ** -0.5
    assert tq == tk
    hp = _fox_heads(nf, 3)
    lanes = lambda h: slice(h * HEAD_DIM, (h + 1) * HEAD_DIM)
    kt = jnp.transpose(k.reshape(t // tk, tk, nf, HEAD_DIM), (2, 0, 3, 1))

    def body(q_ref, k_ref, kt_ref, v_ref, cc_ref, cr_ref, o_ref, lse_ref, do_ref,
             dq_ref, dk_ref, dv_ref, dcq_ref, dck_ref):
        i = pl.program_id(1)

        @pl.when(i == 0)
        def _():
            dk_ref[...] = jnp.zeros_like(dk_ref)
            dv_ref[...] = jnp.zeros_like(dv_ref)
            dck_ref[...] = jnp.zeros_like(dck_ref)

        diff = lax.broadcasted_iota(jnp.int32, (tk, tq), 0) - lax.broadcasted_iota(jnp.int32, (tk, tq), 1)
        qs = [q_ref[:, lanes(h)] for h in range(hp)]
        dos = [do_ref[:, lanes(h)] for h in range(hp)]
        do_b = [d.astype(BF16) for d in dos]
        cqs = [cr_ref[h, i] for h in range(hp)]
        lses = [lse_ref[h, 0] for h in range(hp)]
        deltas = [jnp.sum(jnp.transpose(dos[h] * o_ref[:, lanes(h)]), axis=0, keepdims=True) for h in range(hp)]

        def products(h, j):
            ks = pl.ds(pl.multiple_of(j * tk, tk), tk)
            nt = (((1,), (1,)), ((), ()))
            return (lax.dot_general(k_ref[ks, lanes(h)], qs[h], nt, preferred_element_type=F32),
                    lax.dot_general(v_ref[ks, lanes(h)], do_b[h], nt, preferred_element_type=F32))

        def tile(h, j, dqt, dcq, s, dp, masked):
            ks = pl.ds(pl.multiple_of(j * tk, tk), tk)
            pr = jnp.exp(s * scale + cqs[h] - cc_ref[h, ks, :] - lses[h])
            if masked:
                pr = jnp.where(diff <= 0, pr, 0.0)
            ds = pr * (dp - deltas[h])
            ds_b = ds.astype(BF16)
            dqt = dqt + jnp.dot(kt_ref[h, j], ds_b, preferred_element_type=F32)
            dk_ref[ks, lanes(h)] += jnp.dot(ds_b, qs[h], preferred_element_type=F32) * scale
            dv_ref[ks, lanes(h)] += jnp.dot(pr.astype(BF16), do_b[h], preferred_element_type=F32)
            dck_ref[h, ks, :] -= jnp.sum(ds, axis=1, keepdims=True)
            return dqt, dcq + jnp.sum(ds, axis=0, keepdims=True)

        def step(j, carry):
            nxt = [products(h, j + 1) for h in range(hp)]
            return tuple(tile(h, j, *carry[h], False) + nxt[h] for h in range(hp))

        init = tuple((jnp.zeros((HEAD_DIM, tq), F32), jnp.zeros((1, tq), F32)) + products(h, 0) for h in range(hp))
        carry = lax.fori_loop(0, i, step, init)
        for h in range(hp):
            dqt, dcq = tile(h, i, *carry[h], True)
            dq_ref[:, lanes(h)] = jnp.transpose(dqt) * scale
            dcq_ref[h, 0] = dcq

    w = hp * HEAD_DIM
    head_all = pl.BlockSpec((t, w), lambda h, i: (0, h))
    qblk = pl.BlockSpec((tq, w), lambda h, i: (i, h))
    colv = pl.BlockSpec((hp, t, 1), lambda h, i: (h, 0, 0))
    rows_all = pl.BlockSpec((hp, t // tk, 1, tk), lambda h, i: (h, 0, 0, 0))
    row_blk = pl.BlockSpec((hp, 1, 1, tq), lambda h, i: (h, i, 0, 0))
    wide = jax.ShapeDtypeStruct((t, nf * HEAD_DIM), F32)
    return pl.pallas_call(
        body, name="fox_bwd", grid=(nf // hp, t // tq),
        in_specs=[qblk, head_all, pl.BlockSpec((hp, t // tk, HEAD_DIM, tk), lambda h, i: (h, 0, 0, 0)), head_all,
                  colv, rows_all, qblk, row_blk, qblk],
        out_specs=[qblk, head_all, head_all, row_blk, colv],
        out_shape=[wide, wide, wide, jax.ShapeDtypeStruct((nf, t // tq, 1, tq), F32),
                   jax.ShapeDtypeStruct((nf, t, 1), F32)],
        compiler_params=_cparams(("parallel", "arbitrary")),
    )(q, k, kt, v, cc, cr, o, lse, dmix)


def _mem_fn(mq, mk, mv, gq, gk):
    qn = _rms_fn(mq, gq)
    kn = _rms_fn(mk, gk)
    s = _nt(qn, kn) * (HEAD_DIM ** -0.5)
    e = jnp.exp(s - lax.stop_gradient(jnp.max(s, axis=1, keepdims=True)))
    pr = e / jnp.sum(e, axis=1, keepdims=True)
    return _nn(pr, mv)


def _mem_specs(t, m, tq, qoff):
    qblk = pl.BlockSpec((tq, HEAD_DIM), lambda h, i: (i, qoff + h))
    kblk = pl.BlockSpec((m, HEAD_DIM), lambda h, i: (0, h))
    vblk = pl.BlockSpec((m, HEAD_DIM), lambda h, i: (0, N_MEM_HEADS + h))
    row = pl.BlockSpec((1, HEAD_DIM), lambda h, i: (0, 0))
    return qblk, kblk, vblk, row


def _mem_fwd(p, qoff, mkv, gq, gk, tq, into, into_off):
    t, m = p.shape[0], mkv.shape[0]
    qblk, kblk, vblk, row = _mem_specs(t, m, tq, qoff)

    def body(q_ref, k_ref, v_ref, gq_ref, gk_ref, _, o_ref):
        o_ref[...] = _mem_fn(q_ref[...], k_ref[...], v_ref[...], gq_ref[...], gk_ref[...]).astype(BF16)

    return pl.pallas_call(
        body, name="mem_fwd", grid=(N_MEM_HEADS, t // tq),
        in_specs=[qblk, kblk, vblk, row, row, pl.BlockSpec(memory_space=pl.ANY)],
        out_specs=pl.BlockSpec((tq, HEAD_DIM), lambda h, i: (i, into_off + h)),
        out_shape=jax.ShapeDtypeStruct(into.shape, BF16), input_output_aliases={5: 0},
        compiler_params=_cparams(("parallel", "parallel")),
    )(p, mkv, mkv, gq, gk, into)


def _mem_bwd(p, qoff, mkv, gq, gk, dmix, dooff, tq):
    t, m = p.shape[0], mkv.shape[0]
    qblk, kblk, vblk, row = _mem_specs(t, m, tq, qoff)

    def body(q_ref, k_ref, v_ref, gq_ref, gk_ref, do_ref, dq_ref, dkv_k_ref, dkv_v_ref, dgq_ref, dgk_ref):
        h, i = pl.program_id(0), pl.program_id(1)

        @pl.when((h == 0) & (i == 0))
        def _():
            dgq_ref[...] = jnp.zeros_like(dgq_ref)
            dgk_ref[...] = jnp.zeros_like(dgk_ref)

        @pl.when(i == 0)
        def _():
            dkv_k_ref[...] = jnp.zeros_like(dkv_k_ref)
            dkv_v_ref[...] = jnp.zeros_like(dkv_v_ref)

        _, vjp = jax.vjp(_mem_fn, q_ref[...], k_ref[...], v_ref[...], gq_ref[...], gk_ref[...])
        dq, dk, dv, dgq, dgk = vjp(do_ref[...])
        dq_ref[...] = dq
        dkv_k_ref[...] += dk
        dkv_v_ref[...] += dv
        dgq_ref[...] += dgq
        dgk_ref[...] += dgk

    oblk = pl.BlockSpec((tq, HEAD_DIM), lambda h, i: (i, h))
    kout = pl.BlockSpec((m, HEAD_DIM), lambda h, i: (0, h))
    half = jax.ShapeDtypeStruct((m, N_MEM_HEADS * HEAD_DIM), F32)
    rshape = jax.ShapeDtypeStruct((1, HEAD_DIM), F32)
    return pl.pallas_call(
        body, name="mem_bwd", grid=(N_MEM_HEADS, t // tq),
        in_specs=[qblk, kblk, vblk, row, row, pl.BlockSpec((tq, HEAD_DIM), lambda h, i: (i, dooff + h))],
        out_specs=[oblk, kout, kout, row, row],
        out_shape=[jax.ShapeDtypeStruct((t, N_MEM_HEADS * HEAD_DIM), F32), half, half, rshape, rshape],
        compiler_params=_cparams(("arbitrary", "arbitrary")),
    )(p, mkv, mkv, gq, gk, dmix)


def _shift_down(x, s):
    if s == 0:
        return x
    r = lax.broadcasted_iota(jnp.int32, x.shape, 0)
    return jnp.where(r >= s, pltpu.roll(x, s, 0), 0.0)


def _shift_up(x, s):
    if s == 0:
        return x
    n = x.shape[0]
    r = lax.broadcasted_iota(jnp.int32, x.shape, 0)
    return jnp.where(r < n - s, pltpu.roll(x, n - s, 0), 0.0)


def _conv_fn(x0, x1, x2, x3, w0, w1, w2, w3, kind):
    y = _silu(x0 * w0 + x1 * w1 + x2 * w2 + x3 * w3)
    if kind == 2:
        return y
    y = y * lax.rsqrt(jnp.sum(y * y, axis=-1, keepdims=True) + NORM_EPS)
    return y * (HEAD_DIM ** -0.5) if kind == 0 else y


def _conv_fwd(p, off, conv_w, ng):
    t = p.shape[0]

    def body(x_ref, w_ref, o_ref):
        kind = pl.program_id(0) // ng
        x = x_ref[...]
        xs = [_shift_down(x, CONV_WIDTH - 1 - j) for j in range(CONV_WIDTH)]
        ws = [w_ref[j:j + 1, :] for j in range(CONV_WIDTH)]
        for kd in range(3):
            @pl.when(kind == kd)
            def _(kd=kd):
                o_ref[...] = _conv_fn(*xs, *ws, kd)

    return pl.pallas_call(
        body, name="gdn_conv_fwd", grid=(3 * ng,),
        in_specs=[pl.BlockSpec((t, HEAD_DIM), lambda c: (0, off + c)),
                  pl.BlockSpec((CONV_WIDTH, HEAD_DIM), lambda c: (0, c))],
        out_specs=pl.BlockSpec((t, HEAD_DIM), lambda c: (0, c)),
        out_shape=jax.ShapeDtypeStruct((t, 3 * ng * HEAD_DIM), F32),
        compiler_params=_cparams(("parallel",)),
    )(p, conv_w)


def _conv_bwd(p, off, conv_w, dys, ng):
    t = p.shape[0]

    def body(x_ref, w_ref, dq_ref, dk_ref, dv_ref, dx_ref, dw_ref):
        kind = pl.program_id(0) // ng
        dy_refs = (dq_ref, dk_ref, dv_ref)
        x = x_ref[...]
        xs = [_shift_down(x, CONV_WIDTH - 1 - j) for j in range(CONV_WIDTH)]
        ws = [w_ref[j:j + 1, :] for j in range(CONV_WIDTH)]
        for kd in range(3):
            @pl.when(kind == kd)
            def _(kd=kd):
                _, vjp = jax.vjp(functools.partial(_conv_fn, kind=kd), *xs, *ws)
                g = vjp(dy_refs[kd][...])
                dx = _shift_up(g[0], CONV_WIDTH - 1)
                for j in range(1, CONV_WIDTH):
                    dx = dx + _shift_up(g[j], CONV_WIDTH - 1 - j)
                dx_ref[...] = dx
                for j in range(CONV_WIDTH):
                    dw_ref[j:j + 1, :] = g[CONV_WIDTH + j]

    blk = pl.BlockSpec((t, HEAD_DIM), lambda c: (0, c))
    head = lambda k: pl.BlockSpec((t, HEAD_DIM), lambda c: (0, jnp.where(c // ng == k, c % ng, 0)))
    wblk = pl.BlockSpec((CONV_WIDTH, HEAD_DIM), lambda c: (0, c))
    return pl.pallas_call(
        body, name="gdn_conv_bwd", grid=(3 * ng,),
        in_specs=[pl.BlockSpec((t, HEAD_DIM), lambda c: (0, off + c)), wblk] + [head(k) for k in range(3)],
        out_specs=[blk, wblk],
        out_shape=[jax.ShapeDtypeStruct((t, 3 * ng * HEAD_DIM), F32),
                   jax.ShapeDtypeStruct((CONV_WIDTH, 3 * ng * HEAD_DIM), F32)],
        compiler_params=_cparams(("parallel",)),
    )(p, conv_w, *dys)


def _lower_inverse(lower):
    c = lower.shape[-1]
    r = lax.broadcasted_iota(jnp.int32, (1, c, c), 1)
    e = lax.broadcasted_iota(jnp.int32, (1, c, c), 2)
    hi = lax.Precision.HIGH
    inv = jnp.where(r == e, 1.0, 0.0) - lower
    pw = lower
    for _ in range(int(math.log2(c)) - 1):
        pw = _dot(pw, pw, ((1,), (0,)), hi)
        inv = inv + _dot(inv, pw, ((1,), (0,)), hi)
    return inv


@jax.custom_vjp
def _solve(lower, inv, vb, kbg):
    hi = lax.Precision.HIGH
    return _dot(inv, vb, ((1,), (0,)), hi), _dot(inv, kbg, ((1,), (0,)), hi)


def _solve_fwd(lower, inv, vb, kbg):
    u, w = _solve(lower, inv, vb, kbg)
    return (u, w), (inv, u, w)


def _solve_bwd(res, cts):
    inv, u, w = res
    dvb, dkbg = _tn(inv, cts[0]), _tn(inv, cts[1])
    return -(_nt(dvb, u) + _nt(dkbg, w)), jnp.zeros_like(inv), dvb, dkbg


_solve.defvjp(_solve_fwd, _solve_bwd)


def _wy_fn(q, k, v, gcol, grow, bcol, inv=None):
    b, c, dk = q.shape
    r = lax.broadcasted_iota(jnp.int32, (1, c, c), 1)
    e = lax.broadcasted_iota(jnp.int32, (1, c, c), 2)
    tril, strict = e <= r, e < r
    gc_col = jnp.sum(jnp.where(tril, grow, 0.0), axis=2, keepdims=True)
    gc_row = jnp.sum(jnp.where(r <= e, gcol, 0.0), axis=1, keepdims=True)
    g_last = jnp.sum(gcol, axis=1, keepdims=True)
    decay = jnp.exp(jnp.where(tril, gc_col - gc_row, NEG))
    kb, vb = k * bcol, v * bcol
    lower = jnp.where(strict, _nt(kb, k) * decay, 0.0)
    if inv is None:
        inv = _lower_inverse(lower)
    u, w = _solve(lower, inv, vb, kb * jnp.exp(gc_col))
    attn = jnp.where(tril, _nt(q, k) * decay, 0.0)
    qg = q * jnp.exp(gc_col)
    kdec = k * jnp.exp(g_last - gc_col)
    egl = jnp.broadcast_to(jnp.exp(g_last), (b, 1, dk))
    return u, w, qg, kdec, attn, egl, inv


def _scan_fn(u, w, qg, kdec, attn, egl, state):
    v_new = u - _nn(w, state)
    o = _nn(qg, state) + _nn(attn, v_new)
    return o, state * egl + _tn(kdec, v_new)


GDN_CHUNKS_PER_STEP = 4


def _gdn_fwd(qkv, gcol, grow, bcol, ng):
    t = qkv.shape[0]
    nch = t // CHUNK

    cb = GDN_CHUNKS_PER_STEP
    *wy, inv = _gdn_wy(qkv, gcol, grow, bcol, ng, cb)

    def body(u_ref, w_ref, qg_ref, kd_ref, at_ref, eg_ref, o_ref, st_ref, state):
        @pl.when(pl.program_id(0) == 0)
        def _():
            state[...] = jnp.zeros_like(state)

        st_ref[:, 0] = state[...]
        heads = lambda ref: jnp.stack([ref[:, h * HEAD_DIM:(h + 1) * HEAD_DIM] for h in range(ng)])
        o, new = _scan_fn(heads(u_ref), heads(w_ref), heads(qg_ref), heads(kd_ref), at_ref[:, 0], eg_ref[:, 0],
                          state[...])
        for h in range(ng):
            o_ref[:, h * HEAD_DIM:(h + 1) * HEAD_DIM] = o[h]
        state[...] = new

    w = ng * HEAD_DIM
    blk = pl.BlockSpec((CHUNK, w), lambda i: (i, 0))
    o, states = pl.pallas_call(
        body, name="gdn_scan_fwd", grid=(nch,),
        in_specs=[blk, blk, blk, blk, pl.BlockSpec((ng, 1, CHUNK, CHUNK), lambda i: (0, i, 0, 0)),
                  pl.BlockSpec((ng, 1, 1, HEAD_DIM), lambda i: (0, i, 0, 0))],
        out_specs=[blk, pl.BlockSpec((ng, 1, HEAD_DIM, HEAD_DIM), lambda i: (0, i, 0, 0))],
        out_shape=[jax.ShapeDtypeStruct((t, w), F32),
                   jax.ShapeDtypeStruct((ng, nch, HEAD_DIM, HEAD_DIM), F32)],
        scratch_shapes=[pltpu.VMEM((ng, HEAD_DIM, HEAD_DIM), F32)],
        compiler_params=_cparams(("arbitrary",)),
    )(*wy)
    return o, (wy, inv, states)


def _wy_batch(q_ref, k_ref, v_ref, gc_ref, gr_ref, bc_ref, ng, cb):
    idx = [(c, h) for c in range(cb) for h in range(ng)]
    rows = lambda c: slice(c * CHUNK, (c + 1) * CHUNK)
    lanes = lambda h: slice(h * HEAD_DIM, (h + 1) * HEAD_DIM)
    wide = lambda ref: jnp.stack([ref[rows(c), lanes(h)] for c, h in idx])
    col = lambda ref: jnp.stack([ref[h, rows(c), :] for c, h in idx])
    return idx, (wide(q_ref), wide(k_ref), wide(v_ref), col(gc_ref), jnp.stack([gr_ref[h, c] for c, h in idx]),
                 col(bc_ref))


def _gdn_wy(qkv, gcol, grow, bcol, ng, cb):
    t = qkv.shape[0]
    nch = t // CHUNK

    def body(q_ref, k_ref, v_ref, gc_ref, gr_ref, bc_ref, u_ref, w_ref, qg_ref, kd_ref, at_ref, eg_ref, inv_ref):
        idx, args = _wy_batch(q_ref, k_ref, v_ref, gc_ref, gr_ref, bc_ref, ng, cb)
        u, w, qg, kd, at, eg, inv = _wy_fn(*args)
        for b, (c, h) in enumerate(idx):
            rows, lanes = slice(c * CHUNK, (c + 1) * CHUNK), slice(h * HEAD_DIM, (h + 1) * HEAD_DIM)
            u_ref[rows, lanes] = u[b]
            w_ref[rows, lanes] = w[b]
            qg_ref[rows, lanes] = qg[b]
            kd_ref[rows, lanes] = kd[b]
            at_ref[h, c] = at[b]
            eg_ref[h, c] = eg[b]
            inv_ref[h, c] = inv[b]

    wd = ng * HEAD_DIM
    blk = lambda o: pl.BlockSpec((cb * CHUNK, wd), lambda i: (i, o))
    col = pl.BlockSpec((ng, cb * CHUNK, 1), lambda i: (0, i, 0))
    sq = pl.BlockSpec((ng, cb, CHUNK, CHUNK), lambda i: (0, i, 0, 0))
    wide = jax.ShapeDtypeStruct((t, wd), F32)
    sq_shape = jax.ShapeDtypeStruct((ng, nch, CHUNK, CHUNK), F32)
    return pl.pallas_call(
        body, name="gdn_wy_fwd", grid=(nch // cb,),
        in_specs=[blk(0), blk(1), blk(2), col, pl.BlockSpec((ng, cb, 1, CHUNK), lambda i: (0, i, 0, 0)), col],
        out_specs=[blk(0), blk(0), blk(0), blk(0), sq, pl.BlockSpec((ng, cb, 1, HEAD_DIM), lambda i: (0, i, 0, 0)),
                   sq],
        out_shape=[wide, wide, wide, wide, sq_shape, jax.ShapeDtypeStruct((ng, nch, 1, HEAD_DIM), F32), sq_shape],
        compiler_params=_cparams(("parallel",)),
    )(qkv, qkv, qkv, gcol, grow, bcol)


def _gdn_bwd(qkv, gcol, grow, bcol, saved, do, ng):
    t = qkv.shape[0]
    nch = t // CHUNK
    cb = GDN_CHUNKS_PER_STEP // 2
    wy, inv, states = saved
    wd = ng * HEAD_DIM

    def scan_body(u_ref, w_ref, qg_ref, kd_ref, at_ref, eg_ref, st_ref, do_ref,
                  du_ref, dw_ref, dqg_ref, dkd_ref, dat_ref, deg_ref, dstate):
        @pl.when(pl.program_id(0) == 0)
        def _():
            dstate[...] = jnp.zeros_like(dstate)

        heads = lambda ref: jnp.stack([ref[:, h * HEAD_DIM:(h + 1) * HEAD_DIM] for h in range(ng)])
        _, vjp = jax.vjp(_scan_fn, heads(u_ref), heads(w_ref), heads(qg_ref), heads(kd_ref), at_ref[:, 0],
                         eg_ref[:, 0], st_ref[:, 0])
        du, dw, dqg, dkd, dat, deg, dst = vjp((heads(do_ref), dstate[...]))
        for h in range(ng):
            lanes = slice(h * HEAD_DIM, (h + 1) * HEAD_DIM)
            du_ref[:, lanes] = du[h]
            dw_ref[:, lanes] = dw[h]
            dqg_ref[:, lanes] = dqg[h]
            dkd_ref[:, lanes] = dkd[h]
        dat_ref[:, 0] = dat
        deg_ref[:, 0] = deg
        dstate[...] = dst

    rev = lambda i: nch - 1 - i
    blk = pl.BlockSpec((CHUNK, wd), lambda i: (rev(i), 0))
    atb = pl.BlockSpec((ng, 1, CHUNK, CHUNK), lambda i: (0, rev(i), 0, 0))
    egb = pl.BlockSpec((ng, 1, 1, HEAD_DIM), lambda i: (0, rev(i), 0, 0))
    wide = jax.ShapeDtypeStruct((t, wd), F32)
    at_shape = jax.ShapeDtypeStruct((ng, nch, CHUNK, CHUNK), F32)
    eg_shape = jax.ShapeDtypeStruct((ng, nch, 1, HEAD_DIM), F32)
    dwy = pl.pallas_call(
        scan_body, name="gdn_scan_bwd", grid=(nch,),
        in_specs=[blk, blk, blk, blk, atb, egb,
                  pl.BlockSpec((ng, 1, HEAD_DIM, HEAD_DIM), lambda i: (0, rev(i), 0, 0)), blk],
        out_specs=[blk, blk, blk, blk, atb, egb],
        out_shape=[wide, wide, wide, wide, at_shape, eg_shape],
        scratch_shapes=[pltpu.VMEM((ng, HEAD_DIM, HEAD_DIM), F32)],
        compiler_params=_cparams(("arbitrary",)),
    )(*wy, states, do)

    def wy_body(q_ref, k_ref, v_ref, gc_ref, gr_ref, bc_ref, du_ref, dw_ref, dqg_ref, dkd_ref, dat_ref, deg_ref,
                inv_ref, dq_ref, dk_ref, dv_ref, dgc_ref, dgr_ref, dbc_ref):
        idx, args = _wy_batch(q_ref, k_ref, v_ref, gc_ref, gr_ref, bc_ref, ng, cb)
        kept = jnp.stack([inv_ref[h, c] for c, h in idx])
        rows = lambda c: slice(c * CHUNK, (c + 1) * CHUNK)
        lanes = lambda h: slice(h * HEAD_DIM, (h + 1) * HEAD_DIM)
        wide_ct = lambda ref: jnp.stack([ref[rows(c), lanes(h)] for c, h in idx])
        cts = (wide_ct(du_ref), wide_ct(dw_ref), wide_ct(dqg_ref), wide_ct(dkd_ref),
               jnp.stack([dat_ref[h, c] for c, h in idx]), jnp.stack([deg_ref[h, c] for c, h in idx]))
        _, vjp = jax.vjp(lambda *a: _wy_fn(*a, inv=kept)[:6], *args)
        dq, dk, dv, dgc, dgr, dbc = vjp(cts)
        for b, (c, h) in enumerate(idx):
            dq_ref[rows(c), lanes(h)] = dq[b]
            dk_ref[rows(c), lanes(h)] = dk[b]
            dv_ref[rows(c), lanes(h)] = dv[b]
            dgc_ref[h, rows(c), :] = dgc[b]
            dgr_ref[h, c] = dgr[b]
            dbc_ref[h, rows(c), :] = dbc[b]

    cblk = lambda o: pl.BlockSpec((cb * CHUNK, wd), lambda i: (i, o))
    col = pl.BlockSpec((ng, cb * CHUNK, 1), lambda i: (0, i, 0))
    rowv = pl.BlockSpec((ng, cb, 1, CHUNK), lambda i: (0, i, 0, 0))
    cshape = jax.ShapeDtypeStruct((ng, t, 1), F32)
    return pl.pallas_call(
        wy_body, name="gdn_wy_bwd", grid=(nch // cb,),
        in_specs=[cblk(0), cblk(1), cblk(2), col, rowv, col, cblk(0), cblk(0), cblk(0), cblk(0),
                  pl.BlockSpec((ng, cb, CHUNK, CHUNK), lambda i: (0, i, 0, 0)),
                  pl.BlockSpec((ng, cb, 1, HEAD_DIM), lambda i: (0, i, 0, 0)),
                  pl.BlockSpec((ng, cb, CHUNK, CHUNK), lambda i: (0, i, 0, 0))],
        out_specs=[cblk(0), cblk(0), cblk(0), col, rowv, col],
        out_shape=[wide, wide, wide, cshape, jax.ShapeDtypeStruct((ng, nch, 1, CHUNK), F32), cshape],
        compiler_params=_cparams(("parallel",)),
    )(qkv, qkv, qkv, gcol, grow, bcol, *dwy, inv)


def _swiglu_fn(gate, up):
    return _silu(gate) * up


FFN_TN = 256


def _ffn_up(n2, wgu4):
    _, d, w = wgu4.shape
    t = n2.shape[0]
    tn = _tile(w, FFN_TN)
    nb = w // tn

    def body(a_ref, b_ref, gu_ref, act_ref):
        av = a_ref[...]
        gate = jnp.dot(av, b_ref[0], preferred_element_type=F32)
        up = jnp.dot(av, b_ref[1], preferred_element_type=F32)
        gu_ref[0] = gate.astype(BF16)
        gu_ref[1] = up.astype(BF16)
        act_ref[...] = _swiglu_fn(gate, up).astype(BF16)

    return pl.pallas_call(
        body, name="ffn_up", grid=(2, nb),
        in_specs=[pl.BlockSpec((t, d), lambda j, l: (0, 0)), pl.BlockSpec((2, d, tn), lambda j, l: (j, 0, l))],
        out_specs=[pl.BlockSpec((2, t, tn), lambda j, l: (j, 0, l)),
                   pl.BlockSpec((t, tn), lambda j, l: (0, j * nb + l))],
        out_shape=[jax.ShapeDtypeStruct((4, t, w), BF16), jax.ShapeDtypeStruct((t, 2 * w), BF16)],
        compiler_params=_cparams(("parallel", "parallel")),
    )(n2, wgu4)


def _ffn_dact(dh2, wd, gu, after):
    _, t, w = gu.shape
    d = dh2.shape[1]
    tn = _tile(w, FFN_TN)
    nb = w // tn

    def body(a_ref, b_ref, gu_ref, _, o_ref):
        dact = lax.dot_general(a_ref[...], b_ref[...], (((1,), (1,)), ((), ())), preferred_element_type=F32)
        _, vjp = jax.vjp(_swiglu_fn, gu_ref[0].astype(F32), gu_ref[1].astype(F32))
        dg, du = vjp(dact)
        o_ref[0] = dg.astype(BF16)
        o_ref[1] = du.astype(BF16)

    pair = pl.BlockSpec((2, t, tn), lambda j, l: (j, 0, l))
    return pl.pallas_call(
        body, name="ffn_dact", grid=(2, nb),
        in_specs=[pl.BlockSpec((t, d), lambda j, l: (0, 0)), pl.BlockSpec((tn, d), lambda j, l: (j * nb + l, 0)),
                  pair, pl.BlockSpec(after.shape, lambda j, l: (0, 0))],
        out_specs=pair, out_shape=jax.ShapeDtypeStruct(gu.shape, BF16),
        compiler_params=_cparams(("parallel", "parallel")),
    )(dh2, wd, gu, after)


def _loss_head(h2, target):
    t, d = h2.shape
    tr = _tile(t, 256, 8)

    def body(h_ref, t_ref, l_ref, d_ref, db_ref):
        @pl.when(pl.program_id(0) == 0)
        def _():
            l_ref[...] = jnp.zeros_like(l_ref)

        err = h_ref[...] - t_ref[...]
        d_ref[...] = err * (1.0 / d)
        db_ref[...] = (err * (1.0 / d)).astype(BF16)
        part = 0.5 * jnp.sum(jnp.mean(err * err, axis=-1, keepdims=True), axis=0, keepdims=True)
        lane = lax.broadcasted_iota(jnp.int32, (8, HEAD_DIM), 1)
        row = lax.broadcasted_iota(jnp.int32, (8, HEAD_DIM), 0)
        l_ref[...] += jnp.where((lane == 0) & (row == 0), part, 0.0)

    blk = pl.BlockSpec((tr, d), lambda r: (r, 0))
    return pl.pallas_call(
        body, name="loss_head", grid=(t // tr,), in_specs=[blk, blk],
        out_specs=[pl.BlockSpec((8, HEAD_DIM), lambda r: (0, 0)), blk, blk],
        out_shape=[jax.ShapeDtypeStruct((8, HEAD_DIM), F32), jax.ShapeDtypeStruct((t, d), F32),
                   jax.ShapeDtypeStruct((t, d), BF16)],
        compiler_params=_cparams(("arbitrary",)),
    )(h2, target)


def _adamw(w, g, m, v, *, g_fn=None, name):
    r, c = w.shape
    tr = _tile(r, max(8, (1 << 19) // c // 8 * 8), 8)
    gs = g if isinstance(g, tuple) else (g,)

    def body(w_ref, *refs):
        g_refs, (m_ref, v_ref, go_ref, d_ref, mo_ref, vo_ref) = refs[:len(gs)], refs[len(gs):]
        gr = g_refs[0][...] if g_fn is None else g_fn(*[ref[...] for ref in g_refs])
        mn = ADAM_B1 * m_ref[...] + (1.0 - ADAM_B1) * gr
        vn = ADAM_B2 * v_ref[...] + (1.0 - ADAM_B2) * (gr * gr)
        m_hat = mn / (1.0 - ADAM_B1 ** ADAM_STEP)
        v_hat = vn / (1.0 - ADAM_B2 ** ADAM_STEP)
        go_ref[...] = gr
        d_ref[...] = -ADAM_LR * (m_hat / (jnp.sqrt(v_hat) + ADAM_EPS) + ADAM_WD * w_ref[...])
        mo_ref[...] = mn
        vo_ref[...] = vn

    blk = pl.BlockSpec((tr, c), lambda i: (i, 0))
    gblks = [pl.BlockSpec((tr, gi.shape[1]), lambda i: (i, 0)) for gi in gs]
    return pl.pallas_call(
        body, name=name, grid=(r // tr,), in_specs=[blk] + gblks + [blk, blk], out_specs=[blk] * 4,
        out_shape=[jax.ShapeDtypeStruct((r, c), F32)] * 4,
        compiler_params=_cparams(("parallel",)),
    )(w, *gs, m, v)


class _Layout:
    def __init__(self, d):
        nh = d // HEAD_DIM
        self.nm = N_MEM_HEADS
        self.nf = (nh - self.nm) // 2
        self.ng = nh - self.nm - self.nf
        nf, ng, nm, hd = self.nf, self.ng, self.nm, HEAD_DIM
        self.o_fq, self.o_fk, self.o_fv, self.o_sm = 0, nf, 2 * nf, 3 * nf
        self.o_gq, self.o_gz, self.o_mq = 0, 3 * ng, 4 * ng
        self.cols_a = -(-(3 * nf + 1) // 4) * 4 * hd
        self.cols_b = -(-(4 * ng + nm) // 4) * 4 * hd
        self.cols = self.cols_a + self.cols_b
        sizes = [nf * hd, nf * hd, nf * hd, nf, 3 * ng * hd, ng * hd, ng, ng, nm * hd]
        starts = [sum(sizes[:i]) for i in range(len(sizes))]
        self.ref = list(zip(starts, sizes))
        self.in_cols = sum(sizes)

    def regroup(self, w):
        part = lambda i: w[:, self.ref[i][0]:self.ref[i][0] + self.ref[i][1]]
        a = [part(0), part(1), part(2), part(3), part(6), part(7)]
        b = [part(4), part(5), part(8)]
        pads = [self.cols_a - sum(p.shape[1] for p in a), self.cols_b - sum(p.shape[1] for p in b)]
        fill = [[jnp.zeros((w.shape[0], n), w.dtype)] if n else [] for n in pads]
        return jnp.concatenate(a + fill[0] + b + fill[1], axis=1)

    def ungroup(self, g):
        hd, nf, ng, nm = HEAD_DIM, self.nf, self.ng, self.nm
        sm, b0 = self.o_sm * hd, self.cols_a
        return jnp.concatenate([
            g[:, :3 * nf * hd], g[:, sm:sm + nf], g[:, b0:b0 + 3 * ng * hd],
            g[:, b0 + self.o_gz * hd:b0 + self.o_mq * hd], g[:, sm + nf:sm + nf + ng],
            g[:, sm + nf + ng:sm + nf + 2 * ng], g[:, b0 + self.o_mq * hd:b0 + (self.o_mq + nm) * hd]], axis=1)


def _lane_row(pieces):
    row = jnp.zeros((1, HEAD_DIM), F32)
    for off, a in pieces:
        row = lax.dynamic_update_slice(row, a.astype(F32), (0, off))
    return row


def _local_step(x, mem, target, prefetch, weights, reducer, sp):
    t, d = x.shape
    lay = _Layout(d)
    nf, ng, nm, hd = lay.nf, lay.ng, lay.nm, HEAD_DIM
    nch = t // CHUNK
    tq = _tile(t, 256)
    tk = tq

    u = _norm_fwd(x, 0, sp["norm_mix"], 1, d, BF16, name="norm_mix_fwd")
    prefetch("in_a", u)
    (win_a,) = weights("in_a", u)
    p_a = _mm(u, win_a, name="mm_in_a")
    pa = _lane_row([(nf, sp["gdn_a_log"])])
    pb = _lane_row([(0, sp["fox_f_bias"]), (nf, sp["gdn_dt_bias"])])
    vals, csum = _small_fwd(p_a, lay.o_sm, pa, pb, nf, ng)

    c_t = csum[:, :nf].T
    cc, cr = c_t.reshape(nf, t, 1), c_t.reshape(nf, t // tk, 1, tk)
    fq = _norm_fwd(p_a, lay.o_fq, sp["fox_q_norm"], nf, hd, BF16, name="fox_qnorm_fwd")
    fk = _norm_fwd(p_a, lay.o_fk, sp["fox_k_norm"], nf, hd, BF16, name="fox_knorm_fwd")
    fv = p_a[:, lay.o_fv * hd:(lay.o_fv + nf) * hd].astype(BF16)
    o_fox, lse, mix = _fox_fwd(fq, fk, fv, cc, cr, nf, tq, tk, d)

    prefetch("in_b", lse)
    (win_b,) = weights("in_b", lse)
    prefetch("mixer", win_b)
    p = _mm(u, win_b, name="mm_in_b")
    wmkv, conv_taps = weights("mixer", p)
    sp = dict(sp, gdn_conv=conv_taps)
    qkv = _conv_fwd(p, lay.o_gq, sp["gdn_conv"], ng)
    g_t, b_t = vals[:, nf:nf + ng].T, vals[:, nf + ng:nf + 2 * ng].T
    gcol, grow, bcol = g_t.reshape(ng, t, 1), g_t.reshape(ng, nch, 1, CHUNK), b_t.reshape(ng, t, 1)
    o_g, states = _gdn_fwd(qkv, gcol, grow, bcol, ng)
    mix = _norm_fwd(o_g, 0, sp["gdn_out_norm"], ng, hd, BF16, z=p, zoff=lay.o_gz, into=mix, into_off=nf,
                    name="gdn_out_fwd")
    prefetch("out", mix)

    mem_n = _norm_fwd(mem, 0, sp["mem_norm"], 1, d, BF16, name="mem_norm_fwd")
    mkv = _mm(mem_n, wmkv, name="mm_memkv")
    tq_mem = _tile(t, 1024)
    mix = _mem_fwd(p, lay.o_mq, mkv, sp["mem_q_norm"], sp["mem_k_norm"], tq_mem, mix, nf + ng)
    prefetch("gate_up", mix)
    (wout,) = weights("out", mix)
    h1 = _mm(mix, wout, res=x, name="mm_out")
    n2 = _norm_fwd(h1, 0, sp["norm_ffn"], 1, d, BF16, name="norm_ffn_fwd")
    (wgu,) = weights("gate_up", n2)
    wgu4 = wgu.reshape(4, d, -1)
    gu, act = _ffn_up(n2, wgu4)
    prefetch("down", act)
    (wd,) = weights("down", act)
    h2 = _mm(act, wd, res=h1, name="mm_down")
    loss_blk, dh2, dh2_b = _loss_head(h2, target)

    g = {}
    token = reducer.pair("w_down", _mm(act, dh2_b, ta=True, out_dtype=BF16, name="mm_dw_down"))
    dgu = _ffn_dact(dh2_b, wd, gu, token)
    dw_gate_up = _mm(n2, dgu, ta=True, stack="out", out_dtype=BF16, name="mm_dw_gate_up").reshape(wgu.shape)
    token = reducer.pair("w_gate_up", dw_gate_up)
    dn2 = _mm(dgu, wgu4, tb=True, stack="sum", after=token, name="mm_dn2")
    token = reducer.ship("ffn", ["w_down", "w_gate_up"], dn2)
    dh1, g["norm_ffn"] = _norm_bwd(h1, 0, sp["norm_ffn"] + token[0, 0], dn2, 0, 1, d, res=dh2,
                                   name="norm_ffn_bwd")
    token = reducer.pair("w_out", _mm(mix, dh1, ta=True, out_dtype=BF16, name="mm_dw_out"))
    dmix = _mm(dh1, wout, tb=True, after=token, name="mm_dmix")

    dmq, dmk, dmv, g["mem_q_norm"], g["mem_k_norm"] = _mem_bwd(
        p, lay.o_mq, mkv, sp["mem_q_norm"], sp["mem_k_norm"], dmix, nf + ng, tq_mem)
    dmkv = jnp.concatenate([dmk, dmv], axis=1)
    token = reducer.pair("w_mem_kv", _mm(mem_n, dmkv, ta=True, out_dtype=BF16, name="mm_dw_memkv"))
    dmem_n = _mm(dmkv, wmkv, tb=True, after=token, name="mm_dmem")
    token = reducer.ship("mix", ["w_out", "w_mem_kv"], dmem_n)
    _, g["mem_norm"] = _norm_bwd(mem, 0, sp["mem_norm"], dmem_n, 0, 1, d, name="mem_norm_bwd")

    do_g, dgz, g["gdn_out_norm"] = _norm_bwd(o_g, 0, sp["gdn_out_norm"] + token[0, 0], dmix, nf, ng, hd, z=p,
                                             zoff=lay.o_gz, name="gdn_out_bwd")
    dq, dk, dv, dgc, dgr, dbc = _gdn_bwd(qkv, gcol, grow, bcol, states, do_g, ng)
    dgqkv, g["gdn_conv"] = _conv_bwd(p, lay.o_gq, sp["gdn_conv"], (dq, dk, dv), ng)
    dg_t = dgc.reshape(ng, t) + dgr.reshape(ng, t)
    db_t = dbc.reshape(ng, t)

    dfq_n, dfk_n, dfv, dcc, dcr = _fox_bwd(fq, fk, fv, cc, cr, o_fox, lse, dmix, nf, tq, tk)
    dfq, g["fox_q_norm"] = _norm_bwd(p_a, lay.o_fq, sp["fox_q_norm"], dfq_n, 0, nf, hd, name="fox_qnorm_bwd")
    dfk, g["fox_k_norm"] = _norm_bwd(p_a, lay.o_fk, sp["fox_k_norm"], dfk_n, 0, nf, hd, name="fox_knorm_bwd")
    dc_t = dcc.reshape(nf, t) + dcr.reshape(nf, t)

    lanes_left = hd - nf - 2 * ng
    dvals = jnp.concatenate([jnp.zeros((t, nf), F32), dg_t.T, db_t.T, jnp.zeros((t, lanes_left), F32)], axis=1)
    dcsum = jnp.concatenate([dc_t.T, jnp.zeros((t, hd - nf), F32)], axis=1)
    dsm, dpa, dpb = _small_bwd(p_a, lay.o_sm, pa, pb, dvals, dcsum, nf, ng)
    g["fox_f_bias"] = dpb[:, :nf]
    g["gdn_dt_bias"] = dpb[:, nf:nf + ng]
    g["gdn_a_log"] = dpa[:, nf:nf + ng]

    zeros = lambda n: jnp.zeros((t, n), F32)
    dp_a = jnp.concatenate([dfq, dfk, dfv, dsm, zeros(lay.cols_a - (lay.o_sm + 1) * hd)], axis=1).astype(BF16)
    dp_b = jnp.concatenate([dgqkv, dgz, dmq, zeros(lay.cols_b - (lay.o_mq + nm) * hd)], axis=1).astype(BF16)
    token = reducer.pair("w_in_a", _mm(u, dp_a, ta=True, out_dtype=BF16, name="mm_dw_in_a"))
    token = reducer.pair("w_in_b", _mm(u, dp_b, ta=True, out_dtype=BF16, after=token, name="mm_dw_in_b"))
    du = _mm(dp_a, win_a, tb=True, after=token, name="mm_du_a")
    token = reducer.ship("in", ["w_in_a", "w_in_b"], du)
    du = _mm(dp_b, win_b, tb=True, res=du, after=token, name="mm_du_b")
    dx, g["norm_mix"] = _norm_bwd(x, 0, sp["norm_mix"], du, 0, 1, d, res=dh1, name="norm_mix_bwd")
    return loss_blk, dx, g


ANY = pl.BlockSpec(memory_space=pl.ANY)


def _me():
    x, y, c = lax.axis_index("x"), lax.axis_index("y"), lax.axis_index("c")
    chips = [(1 - x, y), (x, 1 - y), (1 - x, 1 - y)]
    return x, y, c, chips


def _slot(axis, k):
    return k if axis == 0 else 2 * (k % 2) + k // 2


def _slab(ref, axis, rows, cols, k, h):
    half = rows // 2
    return ref.at[pl.ds(_slot(axis, k) * rows + h * half, half), :]


def _remote(src, dst, send_sem, recv_sem, dev):
    return pltpu.make_async_remote_copy(src_ref=src, dst_ref=dst, send_sem=send_sem, recv_sem=recv_sem,
                                        device_id=dev, device_id_type=MESH)


HBM = pl.BlockSpec(memory_space=pltpu.HBM)
SEM = pl.BlockSpec(memory_space=pltpu.SEMAPHORE)
SPLIT = pltpu.CompilerParams(has_side_effects=pltpu.SideEffectType.DATAFLOW_SIDE_EFFECTING)
TOKEN = jax.ShapeDtypeStruct((8, HEAD_DIM), F32)


def _in_hbm(v):
    return pltpu.with_memory_space_constraint(v, pltpu.HBM)


def _cast_place(shard, axis, name, col_fn=None, out_cols=None, after=None):
    r, c = shard.shape
    oc = out_cols or c
    tr = _tile(r, 512 if col_fn is None else 64, 16)
    tc = _tile(c, 2048) if col_fn is None else c
    otc = tc if col_fn is None else oc
    nb = r // tr
    chip = 2 * lax.axis_index("x") + lax.axis_index("y")
    slot = jnp.reshape(_slot(axis, chip), (1,)).astype(jnp.int32)

    def body(slot_ref, x_ref, *rest):
        x = x_ref[...]
        rest[-1][...] = (x if col_fn is None else col_fn(x)).astype(BF16)

    extra = [] if after is None else [after]
    return pl.pallas_call(
        body, name=name,
        grid_spec=pltpu.PrefetchScalarGridSpec(
            num_scalar_prefetch=1, grid=(nb, c // tc),
            in_specs=[pl.BlockSpec((tr, tc), lambda i, l, s: (i, l))] + [ANY] * len(extra),
            out_specs=pl.BlockSpec((tr, otc), lambda i, l, s: (s[0] * nb + i, l))),
        out_shape=jax.ShapeDtypeStruct((4 * r, oc), BF16),
        compiler_params=_cparams(("parallel", "parallel")),
    )(slot, shard, *extra)


def _gather_start(bufs, axes, shapes, groups, name):
    n = len(bufs)

    def body(*refs):
        dst = refs[n:2 * n]
        sems = refs[2 * n:2 * n + 2 * len(groups)]
        token = refs[-1]
        x, y, c, chips = _me()
        k = 2 * x + y
        for gi, ws in enumerate(groups):
            for i, w in enumerate(ws):
                r, cl = shapes[w]
                place = _slab(dst[w], axes[w], r, cl, k, c)
                for j, (px, py) in enumerate(chips):
                    _remote(place, place, sems[2 * gi].at[3 * i + j], sems[2 * gi + 1].at[3 * i + j],
                            (px, py, c)).start()
        token[...] = jnp.zeros_like(token)

    sem_shapes = [pltpu.SemaphoreType.DMA((3 * len(ws),)) for ws in groups for _ in range(2)]
    outs = pl.pallas_call(
        body, name=name, in_specs=[HBM] * n,
        out_specs=[HBM] * n + [SEM] * len(sem_shapes) + [pl.BlockSpec(memory_space=pltpu.VMEM)],
        out_shape=[pltpu.HBM(b.shape, b.dtype) for b in bufs] + sem_shapes + [TOKEN],
        input_output_aliases={w: w for w in range(n)}, compiler_params=SPLIT,
    )(*[_in_hbm(b) for b in bufs])
    sems = outs[n:-1]
    return outs[:n], [(sems[2 * g], sems[2 * g + 1]) for g in range(len(groups))], outs[-1]


def _gather_wait(bufs, axes, shapes, sems, after, name):
    n = len(bufs)

    def body(*refs):
        send_sems, recv_sems = refs[n], refs[n + 1]
        dst = refs[n + 3:]
        x, y, c, chips = _me()
        k = 2 * x + y
        for i in range(n):
            r, cl = shapes[i]
            for j, (px, py) in enumerate(chips):
                got = _slab(dst[i], axes[i], r, cl, 2 * px + py, c)
                _remote(got, got, send_sems.at[3 * i + j], recv_sems.at[3 * i + j], (px, py, c)).wait_recv()
        for i in range(n):
            r, cl = shapes[i]
            mine = _slab(dst[i], axes[i], r, cl, k, c)
            for j, (px, py) in enumerate(chips):
                _remote(mine, mine, send_sems.at[3 * i + j], recv_sems.at[3 * i + j], (px, py, c)).wait_send()

    return pl.pallas_call(
        body, name=name, in_specs=[HBM] * n + [SEM, SEM, ANY], out_specs=[HBM] * n,
        out_shape=[pltpu.HBM(b.shape, b.dtype) for b in bufs],
        input_output_aliases={i: i for i in range(n)}, compiler_params=SPLIT,
    )(*bufs, sems[0], sems[1], after)


def _gather_forward(bufs, axes, shapes, name):
    n = len(bufs)

    def body(*refs):
        dst = refs[n:2 * n]
        send_sems, recv_sems = refs[2 * n:]
        x, y, c, chips = _me()
        sibling = (x, y, 1 - c)
        sends = []
        for i in range(n):
            r, cl = shapes[i]
            for j, (px, py) in enumerate(chips):
                got = _slab(dst[i], axes[i], r, cl, 2 * px + py, c)
                cp = _remote(got, got, send_sems.at[3 * i + j], recv_sems.at[3 * i + j], sibling)
                cp.start()
                sends.append(cp)
        for i in range(n):
            r, cl = shapes[i]
            for j, (px, py) in enumerate(chips):
                got = _slab(dst[i], axes[i], r, cl, 2 * px + py, 1 - c)
                _remote(got, got, send_sems.at[3 * i + j], recv_sems.at[3 * i + j], sibling).wait_recv()
        for cp in sends:
            cp.wait_send()

    return pl.pallas_call(
        body, name=name, in_specs=[ANY] * n, out_specs=[ANY] * n,
        out_shape=[jax.ShapeDtypeStruct(b.shape, b.dtype) for b in bufs],
        input_output_aliases={i: i for i in range(n)},
        scratch_shapes=[pltpu.SemaphoreType.DMA((3 * n,)), pltpu.SemaphoreType.DMA((3 * n,))],
    )(*bufs)


def _split_start(name, arrays, geometry, count):
    n = len(arrays)

    def body(*refs):
        send, recv, token = refs[2 * n:]
        for i, (src, dst, _, dev) in enumerate(geometry(refs[n:2 * n])):
            _remote(src, dst, send.at[i], recv.at[i], dev).start()
        token[...] = jnp.zeros_like(token)

    sem = pltpu.SemaphoreType.DMA((count,))
    outs = pl.pallas_call(
        body, name=name, in_specs=[HBM] * n,
        out_specs=[HBM] * n + [SEM, SEM, pl.BlockSpec(memory_space=pltpu.VMEM)],
        out_shape=[pltpu.HBM(v.shape, v.dtype) for v in arrays] + [sem, sem, TOKEN],
        input_output_aliases={i: i for i in range(n)}, compiler_params=SPLIT,
    )(*[_in_hbm(v) for v in arrays])
    return list(outs[:n]), (outs[n], outs[n + 1]), outs[-1]


def _split_wait(name, arrays, sems, after, geometry):
    n = len(arrays)

    def body(*refs):
        send, recv = refs[n], refs[n + 1]
        copies = geometry(refs[n + 3:])
        for i, (_, _, land, dev) in enumerate(copies):
            _remote(land, land, send.at[i], recv.at[i], dev).wait_recv()
        for i, (src, _, _, dev) in enumerate(copies):
            _remote(src, src, send.at[i], recv.at[i], dev).wait_send()

    return list(pl.pallas_call(
        body, name=name, in_specs=[HBM] * n + [SEM, SEM, ANY], out_specs=[HBM] * n,
        out_shape=[pltpu.HBM(v.shape, v.dtype) for v in arrays],
        input_output_aliases={i: i for i in range(n)}, compiler_params=SPLIT,
    )(*arrays, sems[0], sems[1], after))


def _forward_geometry(axes, shapes):
    def geometry(bufs):
        x, y, c, chips = _me()
        out = []
        for i, buf in enumerate(bufs):
            r, cl = shapes[i]
            for px, py in chips:
                got = _slab(buf, axes[i], r, cl, 2 * px + py, c)
                out.append((got, got, _slab(buf, axes[i], r, cl, 2 * px + py, 1 - c), (x, y, 1 - c)))
        return out
    return geometry


def _pair_geometry(axes, shapes):
    def geometry(refs):
        n = len(refs) // 2
        x, y, c, _ = _me()
        out = []
        for w in range(n):
            r, cl = shapes[w]
            for j in range(4):
                land = refs[n + w].at[j]
                out.append((_slab(refs[w], axes[w], r, cl, j, 1 - c), land, land, (x, y, 1 - c)))
        return out
    return geometry


def _swap_geometry(bufs):
    x, y, c, _ = _me()
    return [(b.at[c], b.at[c], b.at[1 - c], (x, y, 1 - c)) for b in bufs]


def _pair_exchange(fulls, axes, shapes, tag):
    n = len(fulls)

    def body(*refs):
        src, dst = refs[:n], refs[n:2 * n]
        send_sems, recv_sems = refs[2 * n:]
        x, y, c, _ = _me()
        sibling = (x, y, 1 - c)
        cps = []
        for w in range(n):
            r, cl = shapes[w]
            for j in range(4):
                cp = _remote(_slab(src[w], axes[w], r, cl, j, 1 - c), dst[w].at[j],
                             send_sems.at[4 * w + j], recv_sems.at[4 * w + j], sibling)
                cp.start()
                cps.append(cp)
        for cp in cps:
            cp.wait()

    out_shape = [jax.ShapeDtypeStruct((4, r // 2, cl), f.dtype) for (r, cl), f in zip(shapes, fulls)]
    return pl.pallas_call(
        body, name="reduce_pair_exchange_" + tag, in_specs=[ANY] * n, out_specs=[ANY] * n, out_shape=out_shape,
        scratch_shapes=[pltpu.SemaphoreType.DMA((4 * n,)), pltpu.SemaphoreType.DMA((4 * n,))],
    )(*fulls)


def _chip_start(parts, tag):
    n = len(parts)

    def body(*refs):
        src, land = refs[2 * n:3 * n], refs[3 * n:4 * n]
        send_sems, recv_sems, token = refs[4 * n:]
        x, y, c, chips = _me()
        k = 2 * x + y
        for w in range(n):
            for j, (px, py) in enumerate(chips):
                _remote(src[w].at[2 * px + py], land[w].at[k], send_sems.at[3 * w + j], recv_sems.at[3 * w + j],
                        (px, py, c)).start()
        token[...] = jnp.zeros_like(token)

    lands = [lax.empty(p.shape, p.dtype) for p in parts]
    sem = pltpu.SemaphoreType.DMA((3 * n,))
    outs = pl.pallas_call(
        body, name="reduce_ici_start_" + tag, in_specs=[HBM] * (2 * n),
        out_specs=[HBM] * (2 * n) + [SEM, SEM, pl.BlockSpec(memory_space=pltpu.VMEM)],
        out_shape=[pltpu.HBM(p.shape, p.dtype) for p in parts + lands] + [sem, sem, TOKEN],
        input_output_aliases={i: i for i in range(2 * n)}, compiler_params=SPLIT,
    )(*[_in_hbm(v) for v in parts + lands])
    return outs[:n], outs[n:2 * n], outs[2 * n], outs[2 * n + 1], outs[-1]


def _chip_wait(parts, lands, send_sems, recv_sems, after, tag):
    n = len(parts)

    def body(*refs):
        send, recv = refs[2 * n], refs[2 * n + 1]
        src, land = refs[2 * n + 3:3 * n + 3], refs[3 * n + 3:]
        x, y, c, chips = _me()
        for w in range(n):
            for j, (px, py) in enumerate(chips):
                got = land[w].at[2 * px + py]
                _remote(got, got, send.at[3 * w + j], recv.at[3 * w + j], (px, py, c)).wait_recv()
        for w in range(n):
            for j, (px, py) in enumerate(chips):
                sent = src[w].at[2 * px + py]
                _remote(sent, sent, send.at[3 * w + j], recv.at[3 * w + j], (px, py, c)).wait_send()

    outs = pl.pallas_call(
        body, name="reduce_ici_wait_" + tag, in_specs=[HBM] * (2 * n) + [SEM, SEM, ANY], out_specs=[HBM] * (2 * n),
        out_shape=[pltpu.HBM(p.shape, p.dtype) for p in parts + lands],
        input_output_aliases={i: i for i in range(2 * n)}, compiler_params=SPLIT,
    )(*parts, *lands, send_sems, recv_sems, after)
    chip = 2 * lax.axis_index("x") + lax.axis_index("y")
    return [lax.dynamic_update_slice(s, lax.dynamic_index_in_dim(p, chip, 0, keepdims=True), (chip, 0, 0))
            for p, s in zip(outs[:n], outs[n:])]


def _half_swap(halves, tag):
    n = len(halves)
    core = lax.axis_index("c")
    bufs = [lax.dynamic_update_slice(lax.empty((2,) + h.shape, h.dtype), h[None], (core, 0, 0)) for h in halves]

    def body(*refs):
        dst = refs[n:2 * n]
        send_sems, recv_sems = refs[2 * n:]
        x, y, c, _ = _me()
        sibling = (x, y, 1 - c)
        cps = []
        for w in range(n):
            cp = _remote(dst[w].at[c], dst[w].at[c], send_sems.at[w], recv_sems.at[w], sibling)
            cp.start()
            cps.append(cp)
        for w in range(n):
            other = dst[w].at[1 - c]
            _remote(other, other, send_sems.at[w], recv_sems.at[w], sibling).wait_recv()
        for cp in cps:
            cp.wait_send()

    outs = pl.pallas_call(
        body, name="reduce_half_swap_" + tag, in_specs=[ANY] * n, out_specs=[ANY] * n,
        out_shape=[jax.ShapeDtypeStruct(b.shape, b.dtype) for b in bufs],
        input_output_aliases={w: w for w in range(n)},
        scratch_shapes=[pltpu.SemaphoreType.DMA((n,)), pltpu.SemaphoreType.DMA((n,))],
    )(*bufs)
    return [o.reshape(2 * o.shape[1], o.shape[2]) for o in outs]


def _add_parts(full, axis, rows, sib, name):
    _, r, c = sib.shape
    tr, tc = _tile(r, 1024, 16), _tile(c, 2048)
    nb = r // tr
    core = jnp.reshape(lax.axis_index("c"), (1,)).astype(jnp.int32)

    def body(c_ref, a_ref, b_ref, o_ref):
        o_ref[0] = (a_ref[...].astype(F32) + b_ref[0].astype(F32)).astype(BF16)

    blk = pl.BlockSpec((1, tr, tc), lambda j, i, l, cr: (j, i, l))
    return pl.pallas_call(
        body, name=name,
        grid_spec=pltpu.PrefetchScalarGridSpec(
            num_scalar_prefetch=1, grid=(4, nb, c // tc),
            in_specs=[pl.BlockSpec((tr, tc), lambda j, i, l, cr: ((_slot(axis, j) * 2 + cr[0]) * nb + i, l)), blk],
            out_specs=blk),
        out_shape=jax.ShapeDtypeStruct(sib.shape, BF16),
        compiler_params=_cparams(("parallel", "parallel", "parallel")),
    )(core, full, sib)


def _sum_slots(a, name):
    _, r, c = a.shape
    tr, tc = _tile(r, 512, 8), _tile(c, 2048)

    def body(a_ref, o_ref):
        v = a_ref[...].astype(F32)
        o_ref[...] = ((v[0] + v[1]) + v[2]) + v[3]

    return pl.pallas_call(
        body, name=name, grid=(r // tr, c // tc),
        in_specs=[pl.BlockSpec((4, tr, tc), lambda i, l: (0, i, l))],
        out_specs=pl.BlockSpec((tr, tc), lambda i, l: (i, l)),
        out_shape=jax.ShapeDtypeStruct((r, c), F32),
        compiler_params=_cparams(("parallel", "parallel")),
    )(a)


class _Reducer:
    def __init__(self, spec):
        self.spec = spec
        self.paired = {}
        self.pending = []

    def pair(self, name, full):
        ax, shp = self.spec[name]
        land = lax.empty((4, shp[0] // 2, shp[1]), full.dtype)
        arrays, sems, token = _split_start("reduce_pair_start_" + name, [full, land], _pair_geometry([ax], [shp]), 4)
        self.paired[name] = (arrays, sems)
        return token

    def ship(self, tag, names, after):
        parts = []
        for n in names:
            ax, shp = self.spec[n]
            arrays, sems = self.paired.pop(n)
            full, sib = _split_wait("reduce_pair_wait_" + n, arrays, sems, after, _pair_geometry([ax], [shp]))
            parts.append(_add_parts(full, ax, shp[0], sib, name=f"reduce_add_{n}"))
        parts, lands, send, recv, token = _chip_start(parts, tag)
        self.pending.append((tag, names, parts, lands, send, recv))
        return token

    def start(self, tag, grads):
        names = list(grads)
        fulls, axes = [grads[n] for n in names], [self.spec[n][0] for n in names]
        shapes = [self.spec[n][1] for n in names]
        from_sibling = _pair_exchange(fulls, axes, shapes, tag)
        parts = [_add_parts(f, a, r, s, name=f"reduce_add_{n}")
                 for n, f, a, (r, cl), s in zip(names, fulls, axes, shapes, from_sibling)]
        parts, lands, send, recv, token = _chip_start(parts, tag)
        self.pending.append((tag, names, parts, lands, send, recv))
        return token

    def finish(self, after, tags):
        out = {}
        for tag, names, parts, lands, send, recv in [p for p in self.pending if p[0] in tags]:
            slots = _chip_wait(parts, lands, send, recv, after, tag)
            halves = [_sum_slots(s, name=f"reduce_sum_{n}") for n, s in zip(names, slots)]
            out.update(zip(names, _half_swap(halves, tag)))
        return out

    def finish_start(self, after, tag):
        (_, names, parts, lands, send, recv), = [p for p in self.pending if p[0] == tag]
        slots = _chip_wait(parts, lands, send, recv, after, tag)
        halves = [_sum_slots(s, name=f"reduce_sum_{n}") for n, s in zip(names, slots)]
        core = lax.axis_index("c")
        bufs = [lax.dynamic_update_slice(lax.empty((2,) + h.shape, h.dtype), h[None], (core, 0, 0)) for h in halves]
        bufs, sems, _ = _split_start("reduce_half_swap_start_" + tag, bufs, _swap_geometry, len(bufs))
        return tag, names, bufs, sems

    def swap_wait(self, started, after):
        tag, names, bufs, sems = started
        outs = _split_wait("reduce_half_swap_wait_" + tag, bufs, sems, after, _swap_geometry)
        return dict(zip(names, [o.reshape(2 * o.shape[1], o.shape[2]) for o in outs]))


def _allreduce_small(pack, after):
    rows = pack.shape[0]

    def body(p_ref, _, o_ref, slots, send_sems, recv_sems):
        x, y, c, _ = _me()
        me = 4 * x + 2 * y + c
        slots[me] = p_ref[...]
        cps = []
        for r in range(1, 8):
            peer = (x ^ (r >> 2), y ^ ((r >> 1) & 1), c ^ (r & 1))
            cp = _remote(p_ref, slots.at[me], send_sems.at[r - 1], recv_sems.at[r - 1], peer)
            cp.start()
            cps.append(cp)
        for r in range(1, 8):
            frm = me ^ r
            _remote(slots.at[frm], slots.at[frm], send_sems.at[r - 1], recv_sems.at[r - 1], (x, y, c)).wait_recv()
        for cp in cps:
            cp.wait_send()
        acc = slots[0]
        for s in range(1, 8):
            acc = acc + slots[s]
        o_ref[...] = acc

    vm = pl.BlockSpec(memory_space=pltpu.VMEM)
    return pl.pallas_call(
        body, name="allreduce_small", in_specs=[vm, ANY], out_specs=vm,
        out_shape=jax.ShapeDtypeStruct(pack.shape, F32),
        scratch_shapes=[pltpu.VMEM((8, rows, HEAD_DIM), F32), pltpu.SemaphoreType.DMA((7,)),
                        pltpu.SemaphoreType.DMA((7,))],
    )(pack, after)


_ROWS = ["norm_mix", "norm_ffn", "mem_norm", "fox_q_norm", "fox_k_norm", "gdn_out_norm", "mem_q_norm",
         "mem_k_norm", "fox_f_bias", "gdn_a_log", "gdn_dt_bias"]


def _pack_rows(vals):
    out = []
    for name in _ROWS:
        v = vals[name].reshape(-1)
        n = -(-v.shape[0] // HEAD_DIM) * HEAD_DIM
        out.append(jnp.pad(v, (0, n - v.shape[0])).reshape(-1, HEAD_DIM))
    return jnp.concatenate(out, axis=0)


def _unpack_rows(pack, like):
    out, r = {}, 0
    for name in _ROWS:
        n = like[name].shape[-1]
        nr = -(-n // HEAD_DIM)
        out[name] = pack[r:r + nr].reshape(1, -1)[:, :n]
        r += nr
    return out, r


def kernel(x, mem, norm_mix, w_in, fox_f_bias, fox_q_norm, fox_k_norm, gdn_conv, gdn_a_log, gdn_dt_bias, gdn_out_norm, mem_norm, w_mem_kv, mem_q_norm, mem_k_norm, w_out, norm_ffn, w_gate_up, w_down, loss_target, m_norm_mix, m_w_in, m_fox_f_bias, m_fox_q_norm, m_fox_k_norm, m_gdn_conv, m_gdn_a_log, m_gdn_dt_bias, m_gdn_out_norm, m_mem_norm, m_w_mem_kv, m_mem_q_norm, m_mem_k_norm, m_w_out, m_norm_ffn, m_w_gate_up, m_w_down, v_norm_mix, v_w_in, v_fox_f_bias, v_fox_q_norm, v_fox_k_norm, v_gdn_conv, v_gdn_a_log, v_gdn_dt_bias, v_gdn_out_norm, v_mem_norm, v_w_mem_kv, v_mem_q_norm, v_mem_k_norm, v_w_out, v_norm_ffn, v_w_gate_up, v_w_down):
    a = dict(locals())
    d = x.shape[-1]
    lay = _Layout(d)
    chip = 2 * lax.axis_index("x") + lax.axis_index("y")
    small = {n: a[n] for n in _ROWS}
    big = ["w_in", "w_mem_kv", "w_out", "w_gate_up", "w_down"]
    axes = [0, 0, 0, 1, 0]

    conv_cols = gdn_conv.shape[-1]
    conv_n = CONV_WIDTH * conv_cols
    conv_rows = -(-conv_n // HEAD_DIM)
    conv_blk = jnp.pad(gdn_conv.reshape(-1), (0, 32 * HEAD_DIM - conv_n)).reshape(32, HEAD_DIM)
    axis_of = dict(zip(big, axes), conv=0, w_in_a=0, w_in_b=0)
    shape_of = {n: a[n].shape[1:] for n in big[1:]}
    shape_of.update(w_in_a=(w_in.shape[1], lay.cols_a), w_in_b=(w_in.shape[1], lay.cols_b), conv=conv_blk.shape)
    placed = {"w_in_a": _cast_place(w_in[0], 0, "cast_w_in_a", lambda v: lay.regroup(v)[:, :lay.cols_a], lay.cols_a),
              "conv": lax.dynamic_update_slice(lax.empty((4 * 32, HEAD_DIM), F32), conv_blk, (chip * 32, 0))}
    grouped = {"in_a": ["w_in_a"], "in_b": ["w_in_b"], "mixer": ["w_mem_kv", "conv"], "out": ["w_out"],
               "gate_up": ["w_gate_up"], "down": ["w_down"]}
    inflight = {}

    def start(tags, name):
        names = [n for t in tags for n in grouped[t]]
        bufs, sems, token = _gather_start([placed[n] for n in names], [axis_of[n] for n in names],
                                          [shape_of[n] for n in names],
                                          [[names.index(n) for n in grouped[t]] for t in tags], name)
        for t, pair in zip(tags, sems):
            inflight[t] = ([bufs[names.index(n)] for n in grouped[t]], pair)
        return token

    first = start(["in_a"], "gather_ici_start_in")
    placed["w_in_b"] = _cast_place(w_in[0], 0, "cast_w_in_b", lambda v: lay.regroup(v)[:, lay.cols_a:], lay.cols_b,
                                   after=first)
    placed.update({n: _cast_place(a[n][0], axis_of[n], "cast_" + n, after=first) for n in big[1:]})
    all_started = start(["in_b", "mixer", "out", "gate_up", "down"], "gather_ici_start_rest")

    forwarding = {}

    def prefetch(tag, after):
        bufs, sem_pair = inflight.pop(tag)
        ax, shp = [axis_of[n] for n in grouped[tag]], [shape_of[n] for n in grouped[tag]]
        got = _gather_wait(bufs, ax, shp, sem_pair, all_started if tag == "in_a" else after,
                           "gather_ici_wait_" + tag)
        geometry = _forward_geometry(ax, shp)
        got, sems, _ = _split_start("gather_forward_start_" + tag, got, geometry, 3 * len(got))
        forwarding[tag] = (got, sems, geometry)

    def weights(tag, after):
        got, sems, geometry = forwarding.pop(tag)
        got = _split_wait("gather_forward_wait_" + tag, got, sems, after, geometry)
        if tag != "mixer":
            return got
        taps = got[1].reshape(4, 32 * HEAD_DIM)[:, :conv_n].reshape(4, CONV_WIDTH, conv_cols)
        return got[0], jnp.transpose(taps, (1, 0, 2)).reshape(CONV_WIDTH, 4 * conv_cols)

    sp = dict(small)
    reducer = _Reducer({n: (axis_of[n], shape_of[n]) for n in big[1:] + ["w_in_a", "w_in_b"]})
    loss_blk, dx, g = _local_step(x[0], mem[0], loss_target[0], prefetch, weights, reducer, sp)

    gsmall = {n: g[n] for n in _ROWS}
    pack = jnp.concatenate([_pack_rows(gsmall), g["gdn_conv"].reshape(-1, HEAD_DIM), loss_blk], axis=0)
    pack = jnp.pad(pack, ((0, -pack.shape[0] % 8), (0, 0)))
    out = {"grad_x": dx[None]}

    def adamw_shards(reduced):
        if "w_in_a" in reduced:
            reduced = {"w_in": (reduced["w_in_a"], reduced["w_in_b"])}
        for n, gsh in reduced.items():
            join = (lambda ga, gb: lay.ungroup(jnp.concatenate([ga, gb], axis=1))) if n == "w_in" else None
            res = _adamw(a[n][0], gsh, a["m_" + n][0], a["v_" + n][0], g_fn=join, name="adamw_" + n)
            for pre, r in zip(["grad_", "delta_", "new_m_", "new_v_"], res):
                out[pre + n] = r[None]
        return res[0]

    mix_swap = reducer.finish_start(dx, "mix")
    ffn_swap = reducer.finish_start(mix_swap[2][0], "ffn")
    done = adamw_shards(reducer.swap_wait(mix_swap, ffn_swap[2][0]))
    done = adamw_shards(reducer.swap_wait(ffn_swap, done))
    tot = _allreduce_small(pack, done)
    gs, r0 = _unpack_rows(tot, small)
    conv_g = tot[r0:r0 + CONV_WIDTH * 4 * conv_cols // HEAD_DIM].reshape(CONV_WIDTH, 4 * conv_cols)
    gs_conv = lax.dynamic_slice_in_dim(conv_g, chip * conv_cols, conv_cols, axis=1)
    out["loss"] = tot[r0 + CONV_WIDTH * 4 * conv_cols // HEAD_DIM, 0]
    adamw_shards(reducer.finish(tot, ("in",)))
    conv_pad = lambda v: jnp.pad(v.reshape(-1), (0, conv_rows * HEAD_DIM - conv_n)).reshape(conv_rows, HEAD_DIM)
    packs = []
    for src, cv in [(small, gdn_conv), (gs, gs_conv), ({n: a["m_" + n] for n in _ROWS}, m_gdn_conv),
                    ({n: a["v_" + n] for n in _ROWS}, v_gdn_conv)]:
        packs.append(jnp.concatenate([_pack_rows(src), conv_pad(cv)], axis=0))
    res = _adamw(*packs, name="adamw_small")
    for pre, r in zip(["grad_", "delta_", "new_m_", "new_v_"], res):
        vals, r1 = _unpack_rows(r, small)
        for n in _ROWS:
            out[pre + n] = vals[n]
        out[pre + "gdn_conv"] = r[r1:r1 + conv_rows].reshape(-1)[:conv_n].reshape(gdn_conv.shape)
    names = ["norm_mix", "w_in", "fox_f_bias", "fox_q_norm", "fox_k_norm", "gdn_conv", "gdn_a_log", "gdn_dt_bias",
             "gdn_out_norm", "mem_norm", "w_mem_kv", "mem_q_norm", "mem_k_norm", "w_out", "norm_ffn", "w_gate_up",
             "w_down"]
    return (out["loss"], out["grad_x"], *[out[p + n] for p in ["grad_", "delta_", "new_m_", "new_v_"] for n in names])
```

```python
import functools
import math

import jax
import jax.numpy as jnp
from jax import lax
from jax.experimental import pallas as pl
from jax.experimental.pallas import tpu as pltpu

F32, BF16 = jnp.float32, jnp.bfloat16
HEAD_DIM = 128
CHUNK = 64
N_MEM_HEADS = 4
CONV_WIDTH = 4
NORM_EPS = 1e-6
ADAM_LR, ADAM_B1, ADAM_B2, ADAM_EPS, ADAM_WD, ADAM_STEP = 0.001, 0.9, 0.999, 1e-08, 0.01, 10
VMEM_LIMIT = 48 * 1024 * 1024
NEG = -1e30
MESH = pl.DeviceIdType.MESH


def _cparams(sem=None, **kw):
    if sem is not None:
        kw["dimension_semantics"] = sem
    return pltpu.CompilerParams(vmem_limit_bytes=VMEM_LIMIT, **kw)


def _tile(n, target, mult=128):
    best = None
    d = mult
    while d <= min(n, target):
        if n % d == 0:
            best = d
        d += mult
    return best if best is not None else n


def _dot(a, b, dims, hi):
    if a.ndim == 3:
        dn = (((dims[0][0] + 1,), (dims[1][0] + 1,)), ((0,), (0,)))
    else:
        dn = (dims, ((), ()))
    if hi is not None:
        return lax.dot_general(a, b, dn, precision=hi, preferred_element_type=F32)
    return lax.dot_general(a.astype(BF16), b.astype(BF16), dn, preferred_element_type=F32)


def _make_dots(hi, cotangent=None):
    @jax.custom_vjp
    def nn(a, b):
        return _dot(a, b, ((1,), (0,)), hi)

    @jax.custom_vjp
    def nt(a, b):
        return _dot(a, b, ((1,), (1,)), hi)

    @jax.custom_vjp
    def tn(a, b):
        return _dot(a, b, ((0,), (0,)), hi)

    bnn, bnt, btn = cotangent or (nn, nt, tn)
    nn.defvjp(lambda a, b: (nn(a, b), (a, b)), lambda r, g: (bnt(g, r[1]), btn(r[0], g)))
    nt.defvjp(lambda a, b: (nt(a, b), (a, b)), lambda r, g: (bnn(g, r[1]), btn(g, r[0])))
    tn.defvjp(lambda a, b: (tn(a, b), (a, b)), lambda r, g: (bnt(r[1], g), bnn(r[0], g)))
    return nn, nt, tn


_nn, _nt, _tn = _make_dots(None)
_nn_hi, _nt_hi, _tn_hi = _make_dots(lax.Precision.HIGHEST)
_nn_x3, _nt_x3, _tn_x3 = _make_dots(lax.Precision.HIGH, (_nn, _nt, _tn))


def _sigmoid(x):
    return jax.nn.sigmoid(x)


@jax.custom_vjp
def _softplus(x):
    return jnp.maximum(x, 0.0) + jnp.log(1.0 + jnp.exp(-jnp.abs(x)))


_softplus.defvjp(lambda x: (_softplus(x), x), lambda x, g: (g * _sigmoid(x),))


def _silu(x):
    return x * _sigmoid(x)


def _rms_fn(x, gain, z=None):
    y = x * lax.rsqrt(jnp.mean(x * x, axis=-1, keepdims=True) + NORM_EPS) * gain
    if z is not None:
        y = y * _silu(z)
    return y


def _mm(a, b, *, ta=False, tb=False, out_dtype=F32, res=None, stack=None, after=None, name):
    a2, b2 = a.shape[-2:], b.shape[-2:]
    ns = b.shape[0] if stack else 1
    m = a2[1] if ta else a2[0]
    k = a2[0] if ta else a2[1]
    n = b2[0] if tb else b2[1]
    assert k == (b2[1] if tb else b2[0])
    tm, tn, tk = _mm_tiles(m, n, k, ns if stack == "sum" else 1, a.dtype.itemsize, b.dtype.itemsize,
                           jnp.dtype(out_dtype).itemsize, res is not None)
    nk = k // tk
    single = nk == 1 and stack != "sum"
    dims = ((0 if ta else 1,), (1 if tb else 0,))
    if stack == "sum":
        order = lambda g0, g1, g2, g3: (g2, g0, g1, g3)
        grid = (m // tm, n // tn, ns, nk)
    else:
        order = lambda g0, g1, g2, g3: (g0, g1, g2, g3)
        grid = (ns, m // tm, n // tn, nk)

    def body(*refs):
        if after is not None:
            refs = refs[:2 + (res is not None)] + refs[3 + (res is not None):]
        if single:
            a_ref, b_ref = refs[:2]
            r = lax.dot_general(a_ref[...].astype(BF16), b_ref[...].astype(BF16), (dims, ((), ())),
                                preferred_element_type=F32)
            if res is not None:
                r = r + refs[2][...]
            refs[-1][...] = r.astype(out_dtype)
            return
        if res is None:
            a_ref, b_ref, o_ref, acc = refs
        else:
            a_ref, b_ref, r_ref, o_ref, acc = refs
        s, _, _, kk = order(*[pl.program_id(d) for d in range(4)])
        first = kk == 0
        last = kk == nk - 1
        if stack == "sum":
            first, last = first & (s == 0), last & (s == ns - 1)

        @pl.when(first)
        def _():
            acc[...] = jnp.zeros_like(acc)

        acc[...] += lax.dot_general(a_ref[...].astype(BF16), b_ref[...].astype(BF16), (dims, ((), ())),
                                    preferred_element_type=F32)

        @pl.when(last)
        def _():
            r = acc[...]
            if res is not None:
                r = r + r_ref[...]
            o_ref[...] = r.astype(out_dtype)

    def spec(shape, idx, stacked):
        if stacked:
            return pl.BlockSpec((None,) + shape, lambda *g: (order(*g)[0],) + idx(*order(*g)))
        return pl.BlockSpec(shape, lambda *g: idx(*order(*g)))

    a_spec = (spec((tk, tm), lambda s, i, j, kk: (kk, i), stack == "sum") if ta
              else spec((tm, tk), lambda s, i, j, kk: (i, kk), stack == "sum"))
    b_spec = (spec((tn, tk), lambda s, i, j, kk: (j, kk), bool(stack)) if tb
              else spec((tk, tn), lambda s, i, j, kk: (kk, j), bool(stack)))
    o_spec = spec((tm, tn), lambda s, i, j, kk: (i, j), stack == "out")
    ins, specs = [a, b], [a_spec, b_spec]
    if res is not None:
        ins.append(res)
        specs.append(o_spec)
    if after is not None:
        ins.append(after)
        specs.append(pl.BlockSpec(after.shape, lambda *g: (0,) * after.ndim))
    sem = (("parallel", "parallel", "arbitrary", "arbitrary") if stack == "sum"
           else ("parallel", "parallel", "parallel", "arbitrary"))
    return pl.pallas_call(
        body, name=name, grid=grid, in_specs=specs, out_specs=o_spec,
        out_shape=jax.ShapeDtypeStruct(((ns,) if stack == "out" else ()) + (m, n), out_dtype),
        scratch_shapes=[] if single else [pltpu.VMEM((tm, tn), F32)],
        compiler_params=_cparams(sem),
    )(*ins)


MM_VMEM_BUDGET = 40 * 1024 * 1024
MXU_WIDTH = 256


def _mm_tiles(m, n, k, ns, sa, sb, so, has_res):
    def divs(x, mult, cap):
        out = [d for d in range(mult, min(x, cap) + 1, mult) if x % d == 0]
        return out or [x]

    best = None
    for tk in divs(k, 128, 8192):
        nk = (k // tk) * ns
        for tm in divs(m, 8, 2048):
            for tn in divs(n, 128, 2048):
                vmem = 2 * (tm * tk * sa + tk * tn * sb + tm * tn * so) + (2 * tm * tn * 4 if has_res else 0)
                vmem += tm * tn * 4 if nk > 1 else 0
                if vmem > MM_VMEM_BUDGET:
                    continue
                steps = (m // tm) * (n // tn) * nk
                traffic = (m // tm) * k * n * sb * ns + (n // tn if nk > 1 else 1) * m * k * sa * ns
                cost = steps * 0.4e-6 + traffic / 2.5e12 + (nk * m * n * 8 / 6e12 if nk > 1 else 0)
                cost += 2.0 * m * n * k * ns / 7e14 * (-(-tn // MXU_WIDTH) * MXU_WIDTH / tn)
                if best is None or cost < best[0]:
                    best = (cost, tm, tn, tk)
    return best[1:]


def _norm_fwd(x, xoff, gain, ncol, w, out_dtype, *, z=None, zoff=0, into=None, into_off=0, name):
    t = x.shape[0]
    tr = _tile(t, max(256, (1 << 18) // w), 8)

    def body(*refs):
        x_ref, g_ref, o_ref = refs[0], refs[1], refs[-1]
        y = _rms_fn(x_ref[...], g_ref[...]) if z is None else _rms_fn(x_ref[...], g_ref[...], refs[2][...])
        o_ref[...] = y.astype(out_dtype)

    ins = [x, gain]
    specs = [pl.BlockSpec((tr, w), lambda j, r: (r, xoff + j)), pl.BlockSpec((1, w), lambda j, r: (0, 0))]
    if z is not None:
        ins.append(z)
        specs.append(pl.BlockSpec((tr, w), lambda j, r: (r, zoff + j)))
    aliases = {}
    if into is not None:
        aliases = {len(ins): 0}
        ins.append(into)
        specs.append(pl.BlockSpec(memory_space=pl.ANY))
    return pl.pallas_call(
        body, name=name, grid=(ncol, t // tr), in_specs=specs,
        out_specs=pl.BlockSpec((tr, w), lambda j, r: (r, into_off + j)),
        out_shape=jax.ShapeDtypeStruct((t, ncol * w) if into is None else into.shape, out_dtype),
        input_output_aliases=aliases, compiler_params=_cparams(("parallel", "parallel")),
    )(*ins)


def _norm_bwd(x, xoff, gain, dy, dyoff, ncol, w, *, z=None, zoff=0, res=None, name):
    t = x.shape[0]
    tr = _tile(t, max(256, (1 << 18) // w), 8)

    def body(*refs):
        it = iter(refs)
        x_ref, g_ref = next(it), next(it)
        z_ref = next(it) if z is not None else None
        dy_ref = next(it)
        r_ref = next(it) if res is not None else None
        dx_ref = next(it)
        dz_ref = next(it) if z is not None else None
        dg_ref = next(it)

        @pl.when((pl.program_id(0) == 0) & (pl.program_id(1) == 0))
        def _():
            dg_ref[...] = jnp.zeros_like(dg_ref)

        args = (x_ref[...], g_ref[...]) + ((z_ref[...],) if z is not None else ())
        _, vjp = jax.vjp(_rms_fn, *args)
        grads = vjp(dy_ref[...].astype(F32))
        dx = grads[0]
        if res is not None:
            dx = dx + r_ref[...]
        dx_ref[...] = dx
        if z is not None:
            dz_ref[...] = grads[2]
        dg_ref[...] += grads[1]

    ins = [x, gain]
    specs = [pl.BlockSpec((tr, w), lambda j, r: (r, xoff + j)), pl.BlockSpec((1, w), lambda j, r: (0, 0))]
    if z is not None:
        ins.append(z)
        specs.append(pl.BlockSpec((tr, w), lambda j, r: (r, zoff + j)))
    ins.append(dy)
    specs.append(pl.BlockSpec((tr, w), lambda j, r: (r, dyoff + j)))
    blk = pl.BlockSpec((tr, w), lambda j, r: (r, j))
    if res is not None:
        ins.append(res)
        specs.append(blk)
    full = jax.ShapeDtypeStruct((t, ncol * w), F32)
    out_shape, out_specs = [full], [blk]
    if z is not None:
        out_shape.append(full)
        out_specs.append(blk)
    out_shape.append(jax.ShapeDtypeStruct((1, w), F32))
    out_specs.append(pl.BlockSpec((1, w), lambda j, r: (0, 0)))
    return pl.pallas_call(
        body, name=name, grid=(ncol, t // tr), in_specs=specs, out_specs=out_specs, out_shape=out_shape,
        compiler_params=_cparams(("arbitrary", "arbitrary")),
    )(*ins)


def _small_fn(x, pa, pb, nf, ng):
    lane = lax.broadcasted_iota(jnp.int32, x.shape, 1)
    zz = x + pb
    logf = -_softplus(-zz)
    g = -jnp.exp(pa) * _softplus(zz)
    beta = _sigmoid(x)
    return jnp.where(lane < nf, logf, jnp.where(lane < nf + ng, g, beta))


def _tri(n, upper):
    r = lax.broadcasted_iota(jnp.int32, (n, n), 0)
    c = lax.broadcasted_iota(jnp.int32, (n, n), 1)
    return jnp.where((c >= r) if upper else (c <= r), 1.0, 0.0).astype(F32)


def _small_fwd(p, off, pa, pb, nf, ng):
    t = p.shape[0]
    blk = HEAD_DIM
    nb = t // blk

    def body(x_ref, pa_ref, pb_ref, v_ref, c_ref):
        v_ref[...] = _small_fn(x_ref[...], pa_ref[...], pb_ref[...], nf, ng)
        tri = _tri(blk, False)

        carry = jnp.zeros((1, HEAD_DIM), F32)
        for i in range(nb):
            rows = slice(i * blk, (i + 1) * blk)
            c = _nn_hi(tri, v_ref[rows, :]) + carry
            c_ref[rows, :] = c
            carry = c[blk - 1:blk, :]

    row = pl.BlockSpec((1, HEAD_DIM), lambda i: (0, 0))
    out = pl.BlockSpec((t, HEAD_DIM), lambda i: (0, 0))
    return pl.pallas_call(
        body, name="small_fwd", grid=(1,),
        in_specs=[pl.BlockSpec((t, HEAD_DIM), lambda i: (0, off)), row, row], out_specs=[out, out],
        out_shape=[jax.ShapeDtypeStruct((t, HEAD_DIM), F32)] * 2,
        compiler_params=_cparams(("arbitrary",)),
    )(p, pa, pb)


def _small_bwd(p, off, pa, pb, dvals, dcsum, nf, ng):
    t = p.shape[0]
    blk = HEAD_DIM
    nb = t // blk

    def body(x_ref, pa_ref, pb_ref, dv_ref, dc_ref, dx_ref, dpa_ref, dpb_ref, tot_ref):
        tri = _tri(blk, True)

        carry = jnp.zeros((1, HEAD_DIM), F32)
        for i in reversed(range(nb)):
            rows = slice(i * blk, (i + 1) * blk)
            c = _nn_hi(tri, dc_ref[rows, :]) + carry
            tot_ref[rows, :] = c + dv_ref[rows, :]
            carry = c[0:1, :]
        f = functools.partial(_small_fn, nf=nf, ng=ng)
        _, vjp = jax.vjp(f, x_ref[...], pa_ref[...], pb_ref[...])
        dx, dpa, dpb = vjp(tot_ref[...])
        dx_ref[...] = dx
        dpa_ref[...] = dpa
        dpb_ref[...] = dpb

    row = pl.BlockSpec((1, HEAD_DIM), lambda i: (0, 0))
    full = pl.BlockSpec((t, HEAD_DIM), lambda i: (0, 0))
    return pl.pallas_call(
        body, name="small_bwd", grid=(1,),
        in_specs=[pl.BlockSpec((t, HEAD_DIM), lambda i: (0, off)), row, row, full, full],
        out_specs=[full, row, row],
        out_shape=[jax.ShapeDtypeStruct((t, HEAD_DIM), F32), jax.ShapeDtypeStruct((1, HEAD_DIM), F32),
                   jax.ShapeDtypeStruct((1, HEAD_DIM), F32)],
        scratch_shapes=[pltpu.VMEM((t, HEAD_DIM), F32)],
        compiler_params=_cparams(("arbitrary",)),
    )(p, pa, pb, dvals, dcsum)


def _fox_heads(nf, most):
    return next(h for h in range(most, 0, -1) if nf % h == 0)


def _fox_fwd(q, k, v, cc, cr, nf, tq, tk, d_mix):
    t = q.shape[0]
    scale = HEAD_DIM ** -0.5
    assert tq == tk

    vt = jnp.transpose(v.reshape(t // tk, tk, nf, HEAD_DIM), (2, 0, 3, 1))

    hp = _fox_heads(nf, 3)
    lanes = lambda h: slice(h * HEAD_DIM, (h + 1) * HEAD_DIM)

    def body(q_ref, k_ref, vt_ref, cc_ref, cr_ref, o_ref, lse_ref, mix_ref):
        i = pl.program_id(1)
        qs = [q_ref[:, lanes(h)] for h in range(hp)]
        cqs = [cr_ref[h, i] for h in range(hp)]
        ones = jnp.ones((8, tk), BF16)
        diff = lax.broadcasted_iota(jnp.int32, (tk, tq), 0) - lax.broadcasted_iota(jnp.int32, (tk, tq), 1)

        def scores(h, j):
            ks = pl.ds(pl.multiple_of(j * tk, tk), tk)
            return lax.dot_general(k_ref[ks, lanes(h)], qs[h], (((1,), (1,)), ((), ())),
                                   preferred_element_type=F32)

        def tile(h, j, m, l, acc, s, masked):
            ks = pl.ds(pl.multiple_of(j * tk, tk), tk)
            s = s * scale + cqs[h] - cc_ref[h, ks, :]
            if masked:
                s = jnp.where(diff <= 0, s, NEG)
            m_new = jnp.maximum(m, jnp.max(s, axis=0, keepdims=True))
            pr = jnp.exp(s - m_new).astype(BF16)
            alpha = jnp.exp(m - m_new)
            l = alpha * l + jnp.dot(ones, pr, preferred_element_type=F32)[:1]
            acc = alpha * acc + jnp.dot(vt_ref[h, j], pr, preferred_element_type=F32)
            return m_new, l, acc

        def step(j, carry):
            nxt = [scores(h, j + 1) for h in range(hp)]
            return tuple(tile(h, j, *carry[h], False) + (nxt[h],) for h in range(hp))

        init = tuple((jnp.full((1, tq), NEG, F32), jnp.zeros((1, tq), F32), jnp.zeros((HEAD_DIM, tq), F32),
                      scores(h, 0)) for h in range(hp))
        carry = lax.fori_loop(0, i, step, init)
        for h in range(hp):
            m, l, acc = tile(h, i, *carry[h], True)
            o = jnp.transpose(acc / l)
            o_ref[:, lanes(h)] = o
            mix_ref[:, lanes(h)] = o.astype(BF16)
            lse_ref[h, 0] = m + jnp.log(l)

    w = hp * HEAD_DIM
    qblk = pl.BlockSpec((tq, w), lambda h, i: (i, h))
    return pl.pallas_call(
        body, name="fox_fwd", grid=(nf // hp, t // tq),
        in_specs=[qblk, pl.BlockSpec((t, w), lambda h, i: (0, h)),
                  pl.BlockSpec((hp, t // tk, HEAD_DIM, tk), lambda h, i: (h, 0, 0, 0)),
                  pl.BlockSpec((hp, t, 1), lambda h, i: (h, 0, 0)),
                  pl.BlockSpec((hp, t // tk, 1, tk), lambda h, i: (h, 0, 0, 0))],
        out_specs=[qblk, pl.BlockSpec((hp, 1, 1, tq), lambda h, i: (h, i, 0, 0)), qblk],
        out_shape=[jax.ShapeDtypeStruct((t, nf * HEAD_DIM), F32), jax.ShapeDtypeStruct((nf, t // tq, 1, tq), F32),
                   jax.ShapeDtypeStruct((t, d_mix), BF16)],
        compiler_params=_cparams(("parallel", "parallel")),
    )(q, k, vt, cc, cr)


def _fox_bwd(q, k, v, cc, cr, o, lse, dmix, nf, tq, tk):
    t = q.shape[0]
    scale = HEAD_DIM ** -0.5
    assert tq == tk
    hp = _fox_heads(nf, 3)
    lanes = lambda h: slice(h * HEAD_DIM, (h + 1) * HEAD_DIM)
    kt = jnp.transpose(k.reshape(t // tk, tk, nf, HEAD_DIM), (2, 0, 3, 1))

    def body(q_ref, k_ref, kt_ref, v_ref, cc_ref, cr_ref, o_ref, lse_ref, do_ref,
             dq_ref, dk_ref, dv_ref, dcq_ref, dck_ref):
        i = pl.program_id(1)

        @pl.when(i == 0)
        def _():
            dk_ref[...] = jnp.zeros_like(dk_ref)
            dv_ref[...] = jnp.zeros_like(dv_ref)
            dck_ref[...] = jnp.zeros_like(dck_ref)

        diff = lax.broadcasted_iota(jnp.int32, (tk, tq), 0) - lax.broadcasted_iota(jnp.int32, (tk, tq), 1)
        qs = [q_ref[:, lanes(h)] for h in range(hp)]
        dos = [do_ref[:, lanes(h)] for h in range(hp)]
        do_b = [d.astype(BF16) for d in dos]
        cqs = [cr_ref[h, i] for h in range(hp)]
        lses = [lse_ref[h, 0] for h in range(hp)]
        deltas = [jnp.sum(jnp.transpose(dos[h] * o_ref[:, lanes(h)]), axis=0, keepdims=True) for h in range(hp)]

        def products(h, j):
            ks = pl.ds(pl.multiple_of(j * tk, tk), tk)
            nt = (((1,), (1,)), ((), ()))
            return (lax.dot_general(k_ref[ks, lanes(h)], qs[h], nt, preferred_element_type=F32),
                    lax.dot_general(v_ref[ks, lanes(h)], do_b[h], nt, preferred_element_type=F32))

        def tile(h, j, dqt, dcq, s, dp, masked):
            ks = pl.ds(pl.multiple_of(j * tk, tk), tk)
            pr = jnp.exp(s * scale + cqs[h] - cc_ref[h, ks, :] - lses[h])
            if masked:
                pr = jnp.where(diff <= 0, pr, 0.0)
            ds = pr * (dp - deltas[h])
            ds_b = ds.astype(BF16)
            dqt = dqt + jnp.dot(kt_ref[h, j], ds_b, preferred_element_type=F32)
            dk_ref[ks, lanes(h)] += jnp.dot(ds_b, qs[h], preferred_element_type=F32) * scale
            dv_ref[ks, lanes(h)] += jnp.dot(pr.astype(BF16), do_b[h], preferred_element_type=F32)
            dck_ref[h, ks, :] -= jnp.sum(ds, axis=1, keepdims=True)
            return dqt, dcq + jnp.sum(ds, axis=0, keepdims=True)

        def step(j, carry):
            nxt = [products(h, j + 1) for h in range(hp)]
            return tuple(tile(h, j, *carry[h], False) + nxt[h] for h in range(hp))

        init = tuple((jnp.zeros((HEAD_DIM, tq), F32), jnp.zeros((1, tq), F32)) + products(h, 0) for h in range(hp))
        carry = lax.fori_loop(0, i, step, init)
        for h in range(hp):
            dqt, dcq = tile(h, i, *carry[h], True)
            dq_ref[:, lanes(h)] = jnp.transpose(dqt) * scale
            dcq_ref[h, 0] = dcq

    w = hp * HEAD_DIM
    head_all = pl.BlockSpec((t, w), lambda h, i: (0, h))
    qblk = pl.BlockSpec((tq, w), lambda h, i: (i, h))
    colv = pl.BlockSpec((hp, t, 1), lambda h, i: (h, 0, 0))
    rows_all = pl.BlockSpec((hp, t // tk, 1, tk), lambda h, i: (h, 0, 0, 0))
    row_blk = pl.BlockSpec((hp, 1, 1, tq), lambda h, i: (h, i, 0, 0))
    wide = jax.ShapeDtypeStruct((t, nf * HEAD_DIM), F32)
    return pl.pallas_call(
        body, name="fox_bwd", grid=(nf // hp, t // tq),
        in_specs=[qblk, head_all, pl.BlockSpec((hp, t // tk, HEAD_DIM, tk), lambda h, i: (h, 0, 0, 0)), head_all,
                  colv, rows_all, qblk, row_blk, qblk],
        out_specs=[qblk, head_all, head_all, row_blk, colv],
        out_shape=[wide, wide, wide, jax.ShapeDtypeStruct((nf, t // tq, 1, tq), F32),
                   jax.ShapeDtypeStruct((nf, t, 1), F32)],
        compiler_params=_cparams(("parallel", "arbitrary")),
    )(q, k, kt, v, cc, cr, o, lse, dmix)


def _mem_fn(mq, mk, mv, gq, gk):
    qn = _rms_fn(mq, gq)
    kn = _rms_fn(mk, gk)
    s = _nt(qn, kn) * (HEAD_DIM ** -0.5)
    e = jnp.exp(s - lax.stop_gradient(jnp.max(s, axis=1, keepdims=True)))
    pr = e / jnp.sum(e, axis=1, keepdims=True)
    return _nn(pr, mv)


def _mem_specs(t, m, tq, qoff):
    qblk = pl.BlockSpec((tq, HEAD_DIM), lambda h, i: (i, qoff + h))
    kblk = pl.BlockSpec((m, HEAD_DIM), lambda h, i: (0, h))
    vblk = pl.BlockSpec((m, HEAD_DIM), lambda h, i: (0, N_MEM_HEADS + h))
    row = pl.BlockSpec((1, HEAD_DIM), lambda h, i: (0, 0))
    return qblk, kblk, vblk, row


def _mem_fwd(p, qoff, mkv, gq, gk, tq, into, into_off):
    t, m = p.shape[0], mkv.shape[0]
    qblk, kblk, vblk, row = _mem_specs(t, m, tq, qoff)

    def body(q_ref, k_ref, v_ref, gq_ref, gk_ref, _, o_ref):
        o_ref[...] = _mem_fn(q_ref[...], k_ref[...], v_ref[...], gq_ref[...], gk_ref[...]).astype(BF16)

    return pl.pallas_call(
        body, name="mem_fwd", grid=(N_MEM_HEADS, t // tq),
        in_specs=[qblk, kblk, vblk, row, row, pl.BlockSpec(memory_space=pl.ANY)],
        out_specs=pl.BlockSpec((tq, HEAD_DIM), lambda h, i: (i, into_off + h)),
        out_shape=jax.ShapeDtypeStruct(into.shape, BF16), input_output_aliases={5: 0},
        compiler_params=_cparams(("parallel", "parallel")),
    )(p, mkv, mkv, gq, gk, into)


def _mem_bwd(p, qoff, mkv, gq, gk, dmix, dooff, tq):
    t, m = p.shape[0], mkv.shape[0]
    qblk, kblk, vblk, row = _mem_specs(t, m, tq, qoff)

    def body(q_ref, k_ref, v_ref, gq_ref, gk_ref, do_ref, dq_ref, dkv_k_ref, dkv_v_ref, dgq_ref, dgk_ref):
        h, i = pl.program_id(0), pl.program_id(1)

        @pl.when((h == 0) & (i == 0))
        def _():
            dgq_ref[...] = jnp.zeros_like(dgq_ref)
            dgk_ref[...] = jnp.zeros_like(dgk_ref)

        @pl.when(i == 0)
        def _():
            dkv_k_ref[...] = jnp.zeros_like(dkv_k_ref)
            dkv_v_ref[...] = jnp.zeros_like(dkv_v_ref)

        _, vjp = jax.vjp(_mem_fn, q_ref[...], k_ref[...], v_ref[...], gq_ref[...], gk_ref[...])
        dq, dk, dv, dgq, dgk = vjp(do_ref[...])
        dq_ref[...] = dq
        dkv_k_ref[...] += dk
        dkv_v_ref[...] += dv
        dgq_ref[...] += dgq
        dgk_ref[...] += dgk

    oblk = pl.BlockSpec((tq, HEAD_DIM), lambda h, i: (i, h))
    kout = pl.BlockSpec((m, HEAD_DIM), lambda h, i: (0, h))
    half = jax.ShapeDtypeStruct((m, N_MEM_HEADS * HEAD_DIM), F32)
    rshape = jax.ShapeDtypeStruct((1, HEAD_DIM), F32)
    return pl.pallas_call(
        body, name="mem_bwd", grid=(N_MEM_HEADS, t // tq),
        in_specs=[qblk, kblk, vblk, row, row, pl.BlockSpec((tq, HEAD_DIM), lambda h, i: (i, dooff + h))],
        out_specs=[oblk, kout, kout, row, row],
        out_shape=[jax.ShapeDtypeStruct((t, N_MEM_HEADS * HEAD_DIM), F32), half, half, rshape, rshape],
        compiler_params=_cparams(("arbitrary", "arbitrary")),
    )(p, mkv, mkv, gq, gk, dmix)


def _shift_down(x, s):
    if s == 0:
        return x
    r = lax.broadcasted_iota(jnp.int32, x.shape, 0)
    return jnp.where(r >= s, pltpu.roll(x, s, 0), 0.0)


def _shift_up(x, s):
    if s == 0:
        return x
    n = x.shape[0]
    r = lax.broadcasted_iota(jnp.int32, x.shape, 0)
    return jnp.where(r < n - s, pltpu.roll(x, n - s, 0), 0.0)


def _conv_fn(x0, x1, x2, x3, w0, w1, w2, w3, kind):
    y = _silu(x0 * w0 + x1 * w1 + x2 * w2 + x3 * w3)
    if kind == 2:
        return y
    y = y * lax.rsqrt(jnp.sum(y * y, axis=-1, keepdims=True) + NORM_EPS)
    return y * (HEAD_DIM ** -0.5) if kind == 0 else y


def _conv_fwd(p, off, conv_w, ng):
    t = p.shape[0]

    def body(x_ref, w_ref, o_ref):
        kind = pl.program_id(0) // ng
        x = x_ref[...]
        xs = [_shift_down(x, CONV_WIDTH - 1 - j) for j in range(CONV_WIDTH)]
        ws = [w_ref[j:j + 1, :] for j in range(CONV_WIDTH)]
        for kd in range(3):
            @pl.when(kind == kd)
            def _(kd=kd):
                o_ref[...] = _conv_fn(*xs, *ws, kd)

    return pl.pallas_call(
        body, name="gdn_conv_fwd", grid=(3 * ng,),
        in_specs=[pl.BlockSpec((t, HEAD_DIM), lambda c: (0, off + c)),
                  pl.BlockSpec((CONV_WIDTH, HEAD_DIM), lambda c: (0, c))],
        out_specs=pl.BlockSpec((t, HEAD_DIM), lambda c: (0, c)),
        out_shape=jax.ShapeDtypeStruct((t, 3 * ng * HEAD_DIM), F32),
        compiler_params=_cparams(("parallel",)),
    )(p, conv_w)


def _conv_bwd(p, off, conv_w, dys, ng):
    t = p.shape[0]

    def body(x_ref, w_ref, dq_ref, dk_ref, dv_ref, dx_ref, dw_ref):
        kind = pl.program_id(0) // ng
        dy_refs = (dq_ref, dk_ref, dv_ref)
        x = x_ref[...]
        xs = [_shift_down(x, CONV_WIDTH - 1 - j) for j in range(CONV_WIDTH)]
        ws = [w_ref[j:j + 1, :] for j in range(CONV_WIDTH)]
        for kd in range(3):
            @pl.when(kind == kd)
            def _(kd=kd):
                _, vjp = jax.vjp(functools.partial(_conv_fn, kind=kd), *xs, *ws)
                g = vjp(dy_refs[kd][...])
                dx = _shift_up(g[0], CONV_WIDTH - 1)
                for j in range(1, CONV_WIDTH):
                    dx = dx + _shift_up(g[j], CONV_WIDTH - 1 - j)
                dx_ref[...] = dx
                for j in range(CONV_WIDTH):
                    dw_ref[j:j + 1, :] = g[CONV_WIDTH + j]

    blk = pl.BlockSpec((t, HEAD_DIM), lambda c: (0, c))
    head = lambda k: pl.BlockSpec((t, HEAD_DIM), lambda c: (0, jnp.where(c // ng == k, c % ng, 0)))
    wblk = pl.BlockSpec((CONV_WIDTH, HEAD_DIM), lambda c: (0, c))
    return pl.pallas_call(
        body, name="gdn_conv_bwd", grid=(3 * ng,),
        in_specs=[pl.BlockSpec((t, HEAD_DIM), lambda c: (0, off + c)), wblk] + [head(k) for k in range(3)],
        out_specs=[blk, wblk],
        out_shape=[jax.ShapeDtypeStruct((t, 3 * ng * HEAD_DIM), F32),
                   jax.ShapeDtypeStruct((CONV_WIDTH, 3 * ng * HEAD_DIM), F32)],
        compiler_params=_cparams(("parallel",)),
    )(p, conv_w, *dys)


def _lower_inverse(lower):
    c = lower.shape[-1]
    r = lax.broadcasted_iota(jnp.int32, (1, c, c), 1)
    e = lax.broadcasted_iota(jnp.int32, (1, c, c), 2)
    hi = lax.Precision.HIGH
    inv = jnp.where(r == e, 1.0, 0.0) - lower
    pw = lower
    for _ in range(int(math.log2(c)) - 1):
        pw = _dot(pw, pw, ((1,), (0,)), hi)
        inv = inv + _dot(inv, pw, ((1,), (0,)), hi)
    return inv


@jax.custom_vjp
def _solve(lower, inv, vb, kbg):
    hi = lax.Precision.HIGH
    return _dot(inv, vb, ((1,), (0,)), hi), _dot(inv, kbg, ((1,), (0,)), hi)


def _solve_fwd(lower, inv, vb, kbg):
    u, w = _solve(lower, inv, vb, kbg)
    return (u, w), (inv, u, w)


def _solve_bwd(res, cts):
    inv, u, w = res
    dvb, dkbg = _tn(inv, cts[0]), _tn(inv, cts[1])
    return -(_nt(dvb, u) + _nt(dkbg, w)), jnp.zeros_like(inv), dvb, dkbg


_solve.defvjp(_solve_fwd, _solve_bwd)


def _wy_fn(q, k, v, gcol, grow, bcol, inv=None):
    b, c, dk = q.shape
    r = lax.broadcasted_iota(jnp.int32, (1, c, c), 1)
    e = lax.broadcasted_iota(jnp.int32, (1, c, c), 2)
    tril, strict = e <= r, e < r
    gc_col = jnp.sum(jnp.where(tril, grow, 0.0), axis=2, keepdims=True)
    gc_row = jnp.sum(jnp.where(r <= e, gcol, 0.0), axis=1, keepdims=True)
    g_last = jnp.sum(gcol, axis=1, keepdims=True)
    decay = jnp.exp(jnp.where(tril, gc_col - gc_row, NEG))
    kb, vb = k * bcol, v * bcol
    lower = jnp.where(strict, _nt(kb, k) * decay, 0.0)
    if inv is None:
        inv = _lower_inverse(lower)
    u, w = _solve(lower, inv, vb, kb * jnp.exp(gc_col))
    attn = jnp.where(tril, _nt(q, k) * decay, 0.0)
    qg = q * jnp.exp(gc_col)
    kdec = k * jnp.exp(g_last - gc_col)
    egl = jnp.broadcast_to(jnp.exp(g_last), (b, 1, dk))
    return u, w, qg, kdec, attn, egl, inv


def _scan_fn(u, w, qg, kdec, attn, egl, state):
    v_new = u - _nn(w, state)
    o = _nn(qg, state) + _nn(attn, v_new)
    return o, state * egl + _tn(kdec, v_new)


GDN_CHUNKS_PER_STEP = 4


def _gdn_fwd(qkv, gcol, grow, bcol, ng):
    t = qkv.shape[0]
    nch = t // CHUNK

    cb = GDN_CHUNKS_PER_STEP
    *wy, inv = _gdn_wy(qkv, gcol, grow, bcol, ng, cb)

    def body(u_ref, w_ref, qg_ref, kd_ref, at_ref, eg_ref, o_ref, st_ref, state):
        @pl.when(pl.program_id(0) == 0)
        def _():
            state[...] = jnp.zeros_like(state)

        st_ref[:, 0] = state[...]
        heads = lambda ref: jnp.stack([ref[:, h * HEAD_DIM:(h + 1) * HEAD_DIM] for h in range(ng)])
        o, new = _scan_fn(heads(u_ref), heads(w_ref), heads(qg_ref), heads(kd_ref), at_ref[:, 0], eg_ref[:, 0],
                          state[...])
        for h in range(ng):
            o_ref[:, h * HEAD_DIM:(h + 1) * HEAD_DIM] = o[h]
        state[...] = new

    w = ng * HEAD_DIM
    blk = pl.BlockSpec((CHUNK, w), lambda i: (i, 0))
    o, states = pl.pallas_call(
        body, name="gdn_scan_fwd", grid=(nch,),
        in_specs=[blk, blk, blk, blk, pl.BlockSpec((ng, 1, CHUNK, CHUNK), lambda i: (0, i, 0, 0)),
                  pl.BlockSpec((ng, 1, 1, HEAD_DIM), lambda i: (0, i, 0, 0))],
        out_specs=[blk, pl.BlockSpec((ng, 1, HEAD_DIM, HEAD_DIM), lambda i: (0, i, 0, 0))],
        out_shape=[jax.ShapeDtypeStruct((t, w), F32),
                   jax.ShapeDtypeStruct((ng, nch, HEAD_DIM, HEAD_DIM), F32)],
        scratch_shapes=[pltpu.VMEM((ng, HEAD_DIM, HEAD_DIM), F32)],
        compiler_params=_cparams(("arbitrary",)),
    )(*wy)
    return o, (wy, inv, states)


def _wy_batch(q_ref, k_ref, v_ref, gc_ref, gr_ref, bc_ref, ng, cb):
    idx = [(c, h) for c in range(cb) for h in range(ng)]
    rows = lambda c: slice(c * CHUNK, (c + 1) * CHUNK)
    lanes = lambda h: slice(h * HEAD_DIM, (h + 1) * HEAD_DIM)
    wide = lambda ref: jnp.stack([ref[rows(c), lanes(h)] for c, h in idx])
    col = lambda ref: jnp.stack([ref[h, rows(c), :] for c, h in idx])
    return idx, (wide(q_ref), wide(k_ref), wide(v_ref), col(gc_ref), jnp.stack([gr_ref[h, c] for c, h in idx]),
                 col(bc_ref))


def _gdn_wy(qkv, gcol, grow, bcol, ng, cb):
    t = qkv.shape[0]
    nch = t // CHUNK

    def body(q_ref, k_ref, v_ref, gc_ref, gr_ref, bc_ref, u_ref, w_ref, qg_ref, kd_ref, at_ref, eg_ref, inv_ref):
        idx, args = _wy_batch(q_ref, k_ref, v_ref, gc_ref, gr_ref, bc_ref, ng, cb)
        u, w, qg, kd, at, eg, inv = _wy_fn(*args)
        for b, (c, h) in enumerate(idx):
            rows, lanes = slice(c * CHUNK, (c + 1) * CHUNK), slice(h * HEAD_DIM, (h + 1) * HEAD_DIM)
            u_ref[rows, lanes] = u[b]
            w_ref[rows, lanes] = w[b]
            qg_ref[rows, lanes] = qg[b]
            kd_ref[rows, lanes] = kd[b]
            at_ref[h, c] = at[b]
            eg_ref[h, c] = eg[b]
            inv_ref[h, c] = inv[b]

    wd = ng * HEAD_DIM
    blk = lambda o: pl.BlockSpec((cb * CHUNK, wd), lambda i: (i, o))
    col = pl.BlockSpec((ng, cb * CHUNK, 1), lambda i: (0, i, 0))
    sq = pl.BlockSpec((ng, cb, CHUNK, CHUNK), lambda i: (0, i, 0, 0))
    wide = jax.ShapeDtypeStruct((t, wd), F32)
    sq_shape = jax.ShapeDtypeStruct((ng, nch, CHUNK, CHUNK), F32)
    return pl.pallas_call(
        body, name="gdn_wy_fwd", grid=(nch // cb,),
        in_specs=[blk(0), blk(1), blk(2), col, pl.BlockSpec((ng, cb, 1, CHUNK), lambda i: (0, i, 0, 0)), col],
        out_specs=[blk(0), blk(0), blk(0), blk(0), sq, pl.BlockSpec((ng, cb, 1, HEAD_DIM), lambda i: (0, i, 0, 0)),
                   sq],
        out_shape=[wide, wide, wide, wide, sq_shape, jax.ShapeDtypeStruct((ng, nch, 1, HEAD_DIM), F32), sq_shape],
        compiler_params=_cparams(("parallel",)),
    )(qkv, qkv, qkv, gcol, grow, bcol)


def _gdn_bwd(qkv, gcol, grow, bcol, saved, do, ng):
    t = qkv.shape[0]
    nch = t // CHUNK
    cb = GDN_CHUNKS_PER_STEP // 2
    wy, inv, states = saved
    wd = ng * HEAD_DIM

    def scan_body(u_ref, w_ref, qg_ref, kd_ref, at_ref, eg_ref, st_ref, do_ref,
                  du_ref, dw_ref, dqg_ref, dkd_ref, dat_ref, deg_ref, dstate):
        @pl.when(pl.program_id(0) == 0)
        def _():
            dstate[...] = jnp.zeros_like(dstate)

        heads = lambda ref: jnp.stack([ref[:, h * HEAD_DIM:(h + 1) * HEAD_DIM] for h in range(ng)])
        _, vjp = jax.vjp(_scan_fn, heads(u_ref), heads(w_ref), heads(qg_ref), heads(kd_ref), at_ref[:, 0],
                         eg_ref[:, 0], st_ref[:, 0])
        du, dw, dqg, dkd, dat, deg, dst = vjp((heads(do_ref), dstate[...]))
        for h in range(ng):
            lanes = slice(h * HEAD_DIM, (h + 1) * HEAD_DIM)
            du_ref[:, lanes] = du[h]
            dw_ref[:, lanes] = dw[h]
            dqg_ref[:, lanes] = dqg[h]
            dkd_ref[:, lanes] = dkd[h]
        dat_ref[:, 0] = dat
        deg_ref[:, 0] = deg
        dstate[...] = dst

    rev = lambda i: nch - 1 - i
    blk = pl.BlockSpec((CHUNK, wd), lambda i: (rev(i), 0))
    atb = pl.BlockSpec((ng, 1, CHUNK, CHUNK), lambda i: (0, rev(i), 0, 0))
    egb = pl.BlockSpec((ng, 1, 1, HEAD_DIM), lambda i: (0, rev(i), 0, 0))
    wide = jax.ShapeDtypeStruct((t, wd), F32)
    at_shape = jax.ShapeDtypeStruct((ng, nch, CHUNK, CHUNK), F32)
    eg_shape = jax.ShapeDtypeStruct((ng, nch, 1, HEAD_DIM), F32)
    dwy = pl.pallas_call(
        scan_body, name="gdn_scan_bwd", grid=(nch,),
        in_specs=[blk, blk, blk, blk, atb, egb,
                  pl.BlockSpec((ng, 1, HEAD_DIM, HEAD_DIM), lambda i: (0, rev(i), 0, 0)), blk],
        out_specs=[blk, blk, blk, blk, atb, egb],
        out_shape=[wide, wide, wide, wide, at_shape, eg_shape],
        scratch_shapes=[pltpu.VMEM((ng, HEAD_DIM, HEAD_DIM), F32)],
        compiler_params=_cparams(("arbitrary",)),
    )(*wy, states, do)

    def wy_body(q_ref, k_ref, v_ref, gc_ref, gr_ref, bc_ref, du_ref, dw_ref, dqg_ref, dkd_ref, dat_ref, deg_ref,
                inv_ref, dq_ref, dk_ref, dv_ref, dgc_ref, dgr_ref, dbc_ref):
        idx, args = _wy_batch(q_ref, k_ref, v_ref, gc_ref, gr_ref, bc_ref, ng, cb)
        kept = jnp.stack([inv_ref[h, c] for c, h in idx])
        rows = lambda c: slice(c * CHUNK, (c + 1) * CHUNK)
        lanes = lambda h: slice(h * HEAD_DIM, (h + 1) * HEAD_DIM)
        wide_ct = lambda ref: jnp.stack([ref[rows(c), lanes(h)] for c, h in idx])
        cts = (wide_ct(du_ref), wide_ct(dw_ref), wide_ct(dqg_ref), wide_ct(dkd_ref),
               jnp.stack([dat_ref[h, c] for c, h in idx]), jnp.stack([deg_ref[h, c] for c, h in idx]))
        _, vjp = jax.vjp(lambda *a: _wy_fn(*a, inv=kept)[:6], *args)
        dq, dk, dv, dgc, dgr, dbc = vjp(cts)
        for b, (c, h) in enumerate(idx):
            dq_ref[rows(c), lanes(h)] = dq[b]
            dk_ref[rows(c), lanes(h)] = dk[b]
            dv_ref[rows(c), lanes(h)] = dv[b]
            dgc_ref[h, rows(c), :] = dgc[b]
            dgr_ref[h, c] = dgr[b]
            dbc_ref[h, rows(c), :] = dbc[b]

    cblk = lambda o: pl.BlockSpec((cb * CHUNK, wd), lambda i: (i, o))
    col = pl.BlockSpec((ng, cb * CHUNK, 1), lambda i: (0, i, 0))
    rowv = pl.BlockSpec((ng, cb, 1, CHUNK), lambda i: (0, i, 0, 0))
    cshape = jax.ShapeDtypeStruct((ng, t, 1), F32)
    return pl.pallas_call(
        wy_body, name="gdn_wy_bwd", grid=(nch // cb,),
        in_specs=[cblk(0), cblk(1), cblk(2), col, rowv, col, cblk(0), cblk(0), cblk(0), cblk(0),
                  pl.BlockSpec((ng, cb, CHUNK, CHUNK), lambda i: (0, i, 0, 0)),
                  pl.BlockSpec((ng, cb, 1, HEAD_DIM), lambda i: (0, i, 0, 0)),
                  pl.BlockSpec((ng, cb, CHUNK, CHUNK), lambda i: (0, i, 0, 0))],
        out_specs=[cblk(0), cblk(0), cblk(0), col, rowv, col],
        out_shape=[wide, wide, wide, cshape, jax.ShapeDtypeStruct((ng, nch, 1, CHUNK), F32), cshape],
        compiler_params=_cparams(("parallel",)),
    )(qkv, qkv, qkv, gcol, grow, bcol, *dwy, inv)


def _swiglu_fn(gate, up):
    return _silu(gate) * up


FFN_TN = 256


def _ffn_up(n2, wgu4):
    _, d, w = wgu4.shape
    t = n2.shape[0]
    tn = _tile(w, FFN_TN)
    nb = w // tn

    def body(a_ref, b_ref, gu_ref, act_ref):
        av = a_ref[...]
        gate = jnp.dot(av, b_ref[0], preferred_element_type=F32)
        up = jnp.dot(av, b_ref[1], preferred_element_type=F32)
        gu_ref[0] = gate.astype(BF16)
        gu_ref[1] = up.astype(BF16)
        act_ref[...] = _swiglu_fn(gate, up).astype(BF16)

    return pl.pallas_call(
        body, name="ffn_up", grid=(2, nb),
        in_specs=[pl.BlockSpec((t, d), lambda j, l: (0, 0)), pl.BlockSpec((2, d, tn), lambda j, l: (j, 0, l))],
        out_specs=[pl.BlockSpec((2, t, tn), lambda j, l: (j, 0, l)),
                   pl.BlockSpec((t, tn), lambda j, l: (0, j * nb + l))],
        out_shape=[jax.ShapeDtypeStruct((4, t, w), BF16), jax.ShapeDtypeStruct((t, 2 * w), BF16)],
        compiler_params=_cparams(("parallel", "parallel")),
    )(n2, wgu4)


def _ffn_dact(dh2, wd, gu, after):
    _, t, w = gu.shape
    d = dh2.shape[1]
    tn = _tile(w, FFN_TN)
    nb = w // tn

    def body(a_ref, b_ref, gu_ref, _, o_ref):
        dact = lax.dot_general(a_ref[...], b_ref[...], (((1,), (1,)), ((), ())), preferred_element_type=F32)
        _, vjp = jax.vjp(_swiglu_fn, gu_ref[0].astype(F32), gu_ref[1].astype(F32))
        dg, du = vjp(dact)
        o_ref[0] = dg.astype(BF16)
        o_ref[1] = du.astype(BF16)

    pair = pl.BlockSpec((2, t, tn), lambda j, l: (j, 0, l))
    return pl.pallas_call(
        body, name="ffn_dact", grid=(2, nb),
        in_specs=[pl.BlockSpec((t, d), lambda j, l: (0, 0)), pl.BlockSpec((tn, d), lambda j, l: (j * nb + l, 0)),
                  pair, pl.BlockSpec(after.shape, lambda j, l: (0, 0))],
        out_specs=pair, out_shape=jax.ShapeDtypeStruct(gu.shape, BF16),
        compiler_params=_cparams(("parallel", "parallel")),
    )(dh2, wd, gu, after)


def _loss_head(h2, target):
    t, d = h2.shape
    tr = _tile(t, 256, 8)

    def body(h_ref, t_ref, l_ref, d_ref, db_ref):
        @pl.when(pl.program_id(0) == 0)
        def _():
            l_ref[...] = jnp.zeros_like(l_ref)

        err = h_ref[...] - t_ref[...]
        d_ref[...] = err * (1.0 / d)
        db_ref[...] = (err * (1.0 / d)).astype(BF16)
        part = 0.5 * jnp.sum(jnp.mean(err * err, axis=-1, keepdims=True), axis=0, keepdims=True)
        lane = lax.broadcasted_iota(jnp.int32, (8, HEAD_DIM), 1)
        row = lax.broadcasted_iota(jnp.int32, (8, HEAD_DIM), 0)
        l_ref[...] += jnp.where((lane == 0) & (row == 0), part, 0.0)

    blk = pl.BlockSpec((tr, d), lambda r: (r, 0))
    return pl.pallas_call(
        body, name="loss_head", grid=(t // tr,), in_specs=[blk, blk],
        out_specs=[pl.BlockSpec((8, HEAD_DIM), lambda r: (0, 0)), blk, blk],
        out_shape=[jax.ShapeDtypeStruct((8, HEAD_DIM), F32), jax.ShapeDtypeStruct((t, d), F32),
                   jax.ShapeDtypeStruct((t, d), BF16)],
        compiler_params=_cparams(("arbitrary",)),
    )(h2, target)


def _adamw(w, g, m, v, *, g_fn=None, name):
    r, c = w.shape
    tr = _tile(r, max(8, (1 << 19) // c // 8 * 8), 8)
    gs = g if isinstance(g, tuple) else (g,)

    def body(w_ref, *refs):
        g_refs, (m_ref, v_ref, go_ref, d_ref, mo_ref, vo_ref) = refs[:len(gs)], refs[len(gs):]
        gr = g_refs[0][...] if g_fn is None else g_fn(*[ref[...] for ref in g_refs])
        mn = ADAM_B1 * m_ref[...] + (1.0 - ADAM_B1) * gr
        vn = ADAM_B2 * v_ref[...] + (1.0 - ADAM_B2) * (gr * gr)
        m_hat = mn / (1.0 - ADAM_B1 ** ADAM_STEP)
        v_hat = vn / (1.0 - ADAM_B2 ** ADAM_STEP)
        go_ref[...] = gr
        d_ref[...] = -ADAM_LR * (m_hat / (jnp.sqrt(v_hat) + ADAM_EPS) + ADAM_WD * w_ref[...])
        mo_ref[...] = mn
        vo_ref[...] = vn

    blk = pl.BlockSpec((tr, c), lambda i: (i, 0))
    gblks = [pl.BlockSpec((tr, gi.shape[1]), lambda i: (i, 0)) for gi in gs]
    return pl.pallas_call(
        body, name=name, grid=(r // tr,), in_specs=[blk] + gblks + [blk, blk], out_specs=[blk] * 4,
        out_shape=[jax.ShapeDtypeStruct((r, c), F32)] * 4,
        compiler_params=_cparams(("parallel",)),
    )(w, *gs, m, v)


class _Layout:
    def __init__(self, d):
        nh = d // HEAD_DIM
        self.nm = N_MEM_HEADS
        self.nf = (nh - self.nm) // 2
        self.ng = nh - self.nm - self.nf
        nf, ng, nm, hd = self.nf, self.ng, self.nm, HEAD_DIM
        self.o_fq, self.o_fk, self.o_fv, self.o_sm = 0, nf, 2 * nf, 3 * nf
        self.o_gq, self.o_gz, self.o_mq = 0, 3 * ng, 4 * ng
        self.cols_a = -(-(3 * nf + 1) // 4) * 4 * hd
        self.cols_b = -(-(4 * ng + nm) // 4) * 4 * hd
        self.cols = self.cols_a + self.cols_b
        sizes = [nf * hd, nf * hd, nf * hd, nf, 3 * ng * hd, ng * hd, ng, ng, nm * hd]
        starts = [sum(sizes[:i]) for i in range(len(sizes))]
        self.ref = list(zip(starts, sizes))
        self.in_cols = sum(sizes)

    def regroup(self, w):
        part = lambda i: w[:, self.ref[i][0]:self.ref[i][0] + self.ref[i][1]]
        a = [part(0), part(1), part(2), part(3), part(6), part(7)]
        b = [part(4), part(5), part(8)]
        pads = [self.cols_a - sum(p.shape[1] for p in a), self.cols_b - sum(p.shape[1] for p in b)]
        fill = [[jnp.zeros((w.shape[0], n), w.dtype)] if n else [] for n in pads]
        return jnp.concatenate(a + fill[0] + b + fill[1], axis=1)

    def ungroup(self, g):
        hd, nf, ng, nm = HEAD_DIM, self.nf, self.ng, self.nm
        sm, b0 = self.o_sm * hd, self.cols_a
        return jnp.concatenate([
            g[:, :3 * nf * hd], g[:, sm:sm + nf], g[:, b0:b0 + 3 * ng * hd],
            g[:, b0 + self.o_gz * hd:b0 + self.o_mq * hd], g[:, sm + nf:sm + nf + ng],
            g[:, sm + nf + ng:sm + nf + 2 * ng], g[:, b0 + self.o_mq * hd:b0 + (self.o_mq + nm) * hd]], axis=1)


def _lane_row(pieces):
    row = jnp.zeros((1, HEAD_DIM), F32)
    for off, a in pieces:
        row = lax.dynamic_update_slice(row, a.astype(F32), (0, off))
    return row


def _local_step(x, mem, target, prefetch, weights, reducer, sp):
    t, d = x.shape
    lay = _Layout(d)
    nf, ng, nm, hd = lay.nf, lay.ng, lay.nm, HEAD_DIM
    nch = t // CHUNK
    tq = _tile(t, 256)
    tk = tq

    u = _norm_fwd(x, 0, sp["norm_mix"], 1, d, BF16, name="norm_mix_fwd")
    prefetch("in_a", u)
    (win_a,) = weights("in_a", u)
    p_a = _mm(u, win_a, name="mm_in_a")
    pa = _lane_row([(nf, sp["gdn_a_log"])])
    pb = _lane_row([(0, sp["fox_f_bias"]), (nf, sp["gdn_dt_bias"])])
    vals, csum = _small_fwd(p_a, lay.o_sm, pa, pb, nf, ng)

    c_t = csum[:, :nf].T
    cc, cr = c_t.reshape(nf, t, 1), c_t.reshape(nf, t // tk, 1, tk)
    fq = _norm_fwd(p_a, lay.o_fq, sp["fox_q_norm"], nf, hd, BF16, name="fox_qnorm_fwd")
    fk = _norm_fwd(p_a, lay.o_fk, sp["fox_k_norm"], nf, hd, BF16, name="fox_knorm_fwd")
    fv = p_a[:, lay.o_fv * hd:(lay.o_fv + nf) * hd].astype(BF16)
    o_fox, lse, mix = _fox_fwd(fq, fk, fv, cc, cr, nf, tq, tk, d)

    prefetch("in_b", lse)
    (win_b,) = weights("in_b", lse)
    prefetch("mixer", win_b)
    p = _mm(u, win_b, name="mm_in_b")
    wmkv, conv_taps = weights("mixer", p)
    sp = dict(sp, gdn_conv=conv_taps)
    qkv = _conv_fwd(p, lay.o_gq, sp["gdn_conv"], ng)
    g_t, b_t = vals[:, nf:nf + ng].T, vals[:, nf + ng:nf + 2 * ng].T
    gcol, grow, bcol = g_t.reshape(ng, t, 1), g_t.reshape(ng, nch, 1, CHUNK), b_t.reshape(ng, t, 1)
    o_g, states = _gdn_fwd(qkv, gcol, grow, bcol, ng)
    mix = _norm_fwd(o_g, 0, sp["gdn_out_norm"], ng, hd, BF16, z=p, zoff=lay.o_gz, into=mix, into_off=nf,
                    name="gdn_out_fwd")
    prefetch("out", mix)

    mem_n = _norm_fwd(mem, 0, sp["mem_norm"], 1, d, BF16, name="mem_norm_fwd")
    mkv = _mm(mem_n, wmkv, name="mm_memkv")
    tq_mem = _tile(t, 1024)
    mix = _mem_fwd(p, lay.o_mq, mkv, sp["mem_q_norm"], sp["mem_k_norm"], tq_mem, mix, nf + ng)
    prefetch("gate_up", mix)
    (wout,) = weights("out", mix)
    h1 = _mm(mix, wout, res=x, name="mm_out")
    n2 = _norm_fwd(h1, 0, sp["norm_ffn"], 1, d, BF16, name="norm_ffn_fwd")
    (wgu,) = weights("gate_up", n2)
    wgu4 = wgu.reshape(4, d, -1)
    gu, act = _ffn_up(n2, wgu4)
    prefetch("down", act)
    (wd,) = weights("down", act)
    h2 = _mm(act, wd, res=h1, name="mm_down")
    loss_blk, dh2, dh2_b = _loss_head(h2, target)

    g = {}
    token = reducer.pair("w_down", _mm(act, dh2_b, ta=True, out_dtype=BF16, name="mm_dw_down"))
    dgu = _ffn_dact(dh2_b, wd, gu, token)
    dw_gate_up = _mm(n2, dgu, ta=True, stack="out", out_dtype=BF16, name="mm_dw_gate_up").reshape(wgu.shape)
    token = reducer.pair("w_gate_up", dw_gate_up)
    dn2 = _mm(dgu, wgu4, tb=True, stack="sum", after=token, name="mm_dn2")
    token = reducer.ship("ffn", ["w_down", "w_gate_up"], dn2)
    dh1, g["norm_ffn"] = _norm_bwd(h1, 0, sp["norm_ffn"] + token[0, 0], dn2, 0, 1, d, res=dh2,
                                   name="norm_ffn_bwd")
    token = reducer.pair("w_out", _mm(mix, dh1, ta=True, out_dtype=BF16, name="mm_dw_out"))
    dmix = _mm(dh1, wout, tb=True, after=token, name="mm_dmix")

    dmq, dmk, dmv, g["mem_q_norm"], g["mem_k_norm"] = _mem_bwd(
        p, lay.o_mq, mkv, sp["mem_q_norm"], sp["mem_k_norm"], dmix, nf + ng, tq_mem)
    dmkv = jnp.concatenate([dmk, dmv], axis=1)
    token = reducer.pair("w_mem_kv", _mm(mem_n, dmkv, ta=True, out_dtype=BF16, name="mm_dw_memkv"))
    dmem_n = _mm(dmkv, wmkv, tb=True, after=token, name="mm_dmem")
    token = reducer.ship("mix", ["w_out", "w_mem_kv"], dmem_n)
    _, g["mem_norm"] = _norm_bwd(mem, 0, sp["mem_norm"], dmem_n, 0, 1, d, name="mem_norm_bwd")

    do_g, dgz, g["gdn_out_norm"] = _norm_bwd(o_g, 0, sp["gdn_out_norm"] + token[0, 0], dmix, nf, ng, hd, z=p,
                                             zoff=lay.o_gz, name="gdn_out_bwd")
    dq, dk, dv, dgc, dgr, dbc = _gdn_bwd(qkv, gcol, grow, bcol, states, do_g, ng)
    dgqkv, g["gdn_conv"] = _conv_bwd(p, lay.o_gq, sp["gdn_conv"], (dq, dk, dv), ng)
    dg_t = dgc.reshape(ng, t) + dgr.reshape(ng, t)
    db_t = dbc.reshape(ng, t)

    dfq_n, dfk_n, dfv, dcc, dcr = _fox_bwd(fq, fk, fv, cc, cr, o_fox, lse, dmix, nf, tq, tk)
    dfq, g["fox_q_norm"] = _norm_bwd(p_a, lay.o_fq, sp["fox_q_norm"], dfq_n, 0, nf, hd, name="fox_qnorm_bwd")
    dfk, g["fox_k_norm"] = _norm_bwd(p_a, lay.o_fk, sp["fox_k_norm"], dfk_n, 0, nf, hd, name="fox_knorm_bwd")
    dc_t = dcc.reshape(nf, t) + dcr.reshape(nf, t)

    lanes_left = hd - nf - 2 * ng
    dvals = jnp.concatenate([jnp.zeros((t, nf), F32), dg_t.T, db_t.T, jnp.zeros((t, lanes_left), F32)], axis=1)
    dcsum = jnp.concatenate([dc_t.T, jnp.zeros((t, hd - nf), F32)], axis=1)
    dsm, dpa, dpb = _small_bwd(p_a, lay.o_sm, pa, pb, dvals, dcsum, nf, ng)
    g["fox_f_bias"] = dpb[:, :nf]
    g["gdn_dt_bias"] = dpb[:, nf:nf + ng]
    g["gdn_a_log"] = dpa[:, nf:nf + ng]

    zeros = lambda n: jnp.zeros((t, n), F32)
    dp_a = jnp.concatenate([dfq, dfk, dfv, dsm, zeros(lay.cols_a - (lay.o_sm + 1) * hd)], axis=1).astype(BF16)
    dp_b = jnp.concatenate([dgqkv, dgz, dmq, zeros(lay.cols_b - (lay.o_mq + nm) * hd)], axis=1).astype(BF16)
    token = reducer.pair("w_in_a", _mm(u, dp_a, ta=True, out_dtype=BF16, name="mm_dw_in_a"))
    token = reducer.pair("w_in_b", _mm(u, dp_b, ta=True, out_dtype=BF16, after=token, name="mm_dw_in_b"))
    du = _mm(dp_a, win_a, tb=True, after=token, name="mm_du_a")
    token = reducer.ship("in", ["w_in_a", "w_in_b"], du)
    du = _mm(dp_b, win_b, tb=True, res=du, after=token, name="mm_du_b")
    dx, g["norm_mix"] = _norm_bwd(x, 0, sp["norm_mix"], du, 0, 1, d, res=dh1, name="norm_mix_bwd")
    return loss_blk, dx, g


ANY = pl.BlockSpec(memory_space=pl.ANY)


def _me():
    x, y, c = lax.axis_index("x"), lax.axis_index("y"), lax.axis_index("c")
    chips = [(1 - x, y), (x, 1 - y), (1 - x, 1 - y)]
    return x, y, c, chips


def _slot(axis, k):
    return k if axis == 0 else 2 * (k % 2) + k // 2


def _slab(ref, axis, rows, cols, k, h):
    half = rows // 2
    return ref.at[pl.ds(_slot(axis, k) * rows + h * half, half), :]


def _remote(src, dst, send_sem, recv_sem, dev):
    return pltpu.make_async_remote_copy(src_ref=src, dst_ref=dst, send_sem=send_sem, recv_sem=recv_sem,
                                        device_id=dev, device_id_type=MESH)


HBM = pl.BlockSpec(memory_space=pltpu.HBM)
SEM = pl.BlockSpec(memory_space=pltpu.SEMAPHORE)
SPLIT = pltpu.CompilerParams(has_side_effects=pltpu.SideEffectType.DATAFLOW_SIDE_EFFECTING)
TOKEN = jax.ShapeDtypeStruct((8, HEAD_DIM), F32)


def _in_hbm(v):
    return pltpu.with_memory_space_constraint(v, pltpu.HBM)


def _cast_place(shard, axis, name, col_fn=None, out_cols=None, after=None):
    r, c = shard.shape
    oc = out_cols or c
    tr = _tile(r, 512 if col_fn is None else 64, 16)
    tc = _tile(c, 2048) if col_fn is None else c
    otc = tc if col_fn is None else oc
    nb = r // tr
    chip = 2 * lax.axis_index("x") + lax.axis_index("y")
    slot = jnp.reshape(_slot(axis, chip), (1,)).astype(jnp.int32)

    def body(slot_ref, x_ref, *rest):
        x = x_ref[...]
        rest[-1][...] = (x if col_fn is None else col_fn(x)).astype(BF16)

    extra = [] if after is None else [after]
    return pl.pallas_call(
        body, name=name,
        grid_spec=pltpu.PrefetchScalarGridSpec(
            num_scalar_prefetch=1, grid=(nb, c // tc),
            in_specs=[pl.BlockSpec((tr, tc), lambda i, l, s: (i, l))] + [ANY] * len(extra),
            out_specs=pl.BlockSpec((tr, otc), lambda i, l, s: (s[0] * nb + i, l))),
        out_shape=jax.ShapeDtypeStruct((4 * r, oc), BF16),
        compiler_params=_cparams(("parallel", "parallel")),
    )(slot, shard, *extra)


def _gather_start(bufs, axes, shapes, groups, name):
    n = len(bufs)

    def body(*refs):
        dst = refs[n:2 * n]
        sems = refs[2 * n:2 * n + 2 * len(groups)]
        token = refs[-1]
        x, y, c, chips = _me()
        k = 2 * x + y
        for gi, ws in enumerate(groups):
            for i, w in enumerate(ws):
                r, cl = shapes[w]
                place = _slab(dst[w], axes[w], r, cl, k, c)
                for j, (px, py) in enumerate(chips):
                    _remote(place, place, sems[2 * gi].at[3 * i + j], sems[2 * gi + 1].at[3 * i + j],
                            (px, py, c)).start()
        token[...] = jnp.zeros_like(token)

    sem_shapes = [pltpu.SemaphoreType.DMA((3 * len(ws),)) for ws in groups for _ in range(2)]
    outs = pl.pallas_call(
        body, name=name, in_specs=[HBM] * n,
        out_specs=[HBM] * n + [SEM] * len(sem_shapes) + [pl.BlockSpec(memory_space=pltpu.VMEM)],
        out_shape=[pltpu.HBM(b.shape, b.dtype) for b in bufs] + sem_shapes + [TOKEN],
        input_output_aliases={w: w for w in range(n)}, compiler_params=SPLIT,
    )(*[_in_hbm(b) for b in bufs])
    sems = outs[n:-1]
    return outs[:n], [(sems[2 * g], sems[2 * g + 1]) for g in range(len(groups))], outs[-1]


def _gather_wait(bufs, axes, shapes, sems, after, name):
    n = len(bufs)

    def body(*refs):
        send_sems, recv_sems = refs[n], refs[n + 1]
        dst = refs[n + 3:]
        x, y, c, chips = _me()
        k = 2 * x + y
        for i in range(n):
            r, cl = shapes[i]
            for j, (px, py) in enumerate(chips):
                got = _slab(dst[i], axes[i], r, cl, 2 * px + py, c)
                _remote(got, got, send_sems.at[3 * i + j], recv_sems.at[3 * i + j], (px, py, c)).wait_recv()
        for i in range(n):
            r, cl = shapes[i]
            mine = _slab(dst[i], axes[i], r, cl, k, c)
            for j, (px, py) in enumerate(chips):
                _remote(mine, mine, send_sems.at[3 * i + j], recv_sems.at[3 * i + j], (px, py, c)).wait_send()

    return pl.pallas_call(
        body, name=name, in_specs=[HBM] * n + [SEM, SEM, ANY], out_specs=[HBM] * n,
        out_shape=[pltpu.HBM(b.shape, b.dtype) for b in bufs],
        input_output_aliases={i: i for i in range(n)}, compiler_params=SPLIT,
    )(*bufs, sems[0], sems[1], after)


def _gather_forward(bufs, axes, shapes, name):
    n = len(bufs)

    def body(*refs):
        dst = refs[n:2 * n]
        send_sems, recv_sems = refs[2 * n:]
        x, y, c, chips = _me()
        sibling = (x, y, 1 - c)
        sends = []
        for i in range(n):
            r, cl = shapes[i]
            for j, (px, py) in enumerate(chips):
                got = _slab(dst[i], axes[i], r, cl, 2 * px + py, c)
                cp = _remote(got, got, send_sems.at[3 * i + j], recv_sems.at[3 * i + j], sibling)
                cp.start()
                sends.append(cp)
        for i in range(n):
            r, cl = shapes[i]
            for j, (px, py) in enumerate(chips):
                got = _slab(dst[i], axes[i], r, cl, 2 * px + py, 1 - c)
                _remote(got, got, send_sems.at[3 * i + j], recv_sems.at[3 * i + j], sibling).wait_recv()
        for cp in sends:
            cp.wait_send()

    return pl.pallas_call(
        body, name=name, in_specs=[ANY] * n, out_specs=[ANY] * n,
        out_shape=[jax.ShapeDtypeStruct(b.shape, b.dtype) for b in bufs],
        input_output_aliases={i: i for i in range(n)},
        scratch_shapes=[pltpu.SemaphoreType.DMA((3 * n,)), pltpu.SemaphoreType.DMA((3 * n,))],
    )(*bufs)


def _split_start(name, arrays, geometry, count):
    n = len(arrays)

    def body(*refs):
        send, recv, token = refs[2 * n:]
        for i, (src, dst, _, dev) in enumerate(geometry(refs[n:2 * n])):
            _remote(src, dst, send.at[i], recv.at[i], dev).start()
        token[...] = jnp.zeros_like(token)

    sem = pltpu.SemaphoreType.DMA((count,))
    outs = pl.pallas_call(
        body, name=name, in_specs=[HBM] * n,
        out_specs=[HBM] * n + [SEM, SEM, pl.BlockSpec(memory_space=pltpu.VMEM)],
        out_shape=[pltpu.HBM(v.shape, v.dtype) for v in arrays] + [sem, sem, TOKEN],
        input_output_aliases={i: i for i in range(n)}, compiler_params=SPLIT,
    )(*[_in_hbm(v) for v in arrays])
    return list(outs[:n]), (outs[n], outs[n + 1]), outs[-1]


def _split_wait(name, arrays, sems, after, geometry):
    n = len(arrays)

    def body(*refs):
        send, recv = refs[n], refs[n + 1]
        copies = geometry(refs[n + 3:])
        for i, (_, _, land, dev) in enumerate(copies):
            _remote(land, land, send.at[i], recv.at[i], dev).wait_recv()
        for i, (src, _, _, dev) in enumerate(copies):
            _remote(src, src, send.at[i], recv.at[i], dev).wait_send()

    return list(pl.pallas_call(
        body, name=name, in_specs=[HBM] * n + [SEM, SEM, ANY], out_specs=[HBM] * n,
        out_shape=[pltpu.HBM(v.shape, v.dtype) for v in arrays],
        input_output_aliases={i: i for i in range(n)}, compiler_params=SPLIT,
    )(*arrays, sems[0], sems[1], after))


def _forward_geometry(axes, shapes):
    def geometry(bufs):
        x, y, c, chips = _me()
        out = []
        for i, buf in enumerate(bufs):
            r, cl = shapes[i]
            for px, py in chips:
                got = _slab(buf, axes[i], r, cl, 2 * px + py, c)
                out.append((got, got, _slab(buf, axes[i], r, cl, 2 * px + py, 1 - c), (x, y, 1 - c)))
        return out
    return geometry


def _pair_geometry(axes, shapes):
    def geometry(refs):
        n = len(refs) // 2
        x, y, c, _ = _me()
        out = []
        for w in range(n):
            r, cl = shapes[w]
            for j in range(4):
                land = refs[n + w].at[j]
                out.append((_slab(refs[w], axes[w], r, cl, j, 1 - c), land, land, (x, y, 1 - c)))
        return out
    return geometry


def _after_all(name, token, *arrays):
    def body(*refs):
        refs[-1][...] = jnp.zeros_like(refs[-1])

    return pl.pallas_call(
        body, name=name, in_specs=[ANY] * (1 + len(arrays)), out_specs=pl.BlockSpec(memory_space=pltpu.VMEM),
        out_shape=TOKEN,
    )(token, *arrays)


def _swap_geometry(bufs):
    x, y, c, _ = _me()
    return [(b.at[c], b.at[c], b.at[1 - c], (x, y, 1 - c)) for b in bufs]


def _pair_exchange(fulls, axes, shapes, tag):
    n = len(fulls)

    def body(*refs):
        src, dst = refs[:n], refs[n:2 * n]
        send_sems, recv_sems = refs[2 * n:]
        x, y, c, _ = _me()
        sibling = (x, y, 1 - c)
        cps = []
        for w in range(n):
            r, cl = shapes[w]
            for j in range(4):
                cp = _remote(_slab(src[w], axes[w], r, cl, j, 1 - c), dst[w].at[j],
                             send_sems.at[4 * w + j], recv_sems.at[4 * w + j], sibling)
                cp.start()
                cps.append(cp)
        for cp in cps:
            cp.wait()

    out_shape = [jax.ShapeDtypeStruct((4, r // 2, cl), f.dtype) for (r, cl), f in zip(shapes, fulls)]
    return pl.pallas_call(
        body, name="reduce_pair_exchange_" + tag, in_specs=[ANY] * n, out_specs=[ANY] * n, out_shape=out_shape,
        scratch_shapes=[pltpu.SemaphoreType.DMA((4 * n,)), pltpu.SemaphoreType.DMA((4 * n,))],
    )(*fulls)


def _chip_start(parts, tag):
    n = len(parts)

    def body(*refs):
        src, land = refs[2 * n:3 * n], refs[3 * n:4 * n]
        send_sems, recv_sems, token = refs[4 * n:]
        x, y, c, chips = _me()
        k = 2 * x + y
        for w in range(n):
            for j, (px, py) in enumerate(chips):
                _remote(src[w].at[2 * px + py], land[w].at[k], send_sems.at[3 * w + j], recv_sems.at[3 * w + j],
                        (px, py, c)).start()
        token[...] = jnp.zeros_like(token)

    lands = [lax.empty(p.shape, p.dtype) for p in parts]
    sem = pltpu.SemaphoreType.DMA((3 * n,))
    outs = pl.pallas_call(
        body, name="reduce_ici_start_" + tag, in_specs=[HBM] * (2 * n),
        out_specs=[HBM] * (2 * n) + [SEM, SEM, pl.BlockSpec(memory_space=pltpu.VMEM)],
        out_shape=[pltpu.HBM(p.shape, p.dtype) for p in parts + lands] + [sem, sem, TOKEN],
        input_output_aliases={i: i for i in range(2 * n)}, compiler_params=SPLIT,
    )(*[_in_hbm(v) for v in parts + lands])
    return outs[:n], outs[n:2 * n], outs[2 * n], outs[2 * n + 1], outs[-1]


def _chip_wait(parts, lands, send_sems, recv_sems, after, tag):
    n = len(parts)

    def body(*refs):
        send, recv = refs[2 * n], refs[2 * n + 1]
        src, land = refs[2 * n + 3:3 * n + 3], refs[3 * n + 3:]
        x, y, c, chips = _me()
        for w in range(n):
            for j, (px, py) in enumerate(chips):
                got = land[w].at[2 * px + py]
                _remote(got, got, send.at[3 * w + j], recv.at[3 * w + j], (px, py, c)).wait_recv()
        for w in range(n):
            for j, (px, py) in enumerate(chips):
                sent = src[w].at[2 * px + py]
                _remote(sent, sent, send.at[3 * w + j], recv.at[3 * w + j], (px, py, c)).wait_send()

    outs = pl.pallas_call(
        body, name="reduce_ici_wait_" + tag, in_specs=[HBM] * (2 * n) + [SEM, SEM, ANY], out_specs=[HBM] * (2 * n),
        out_shape=[pltpu.HBM(p.shape, p.dtype) for p in parts + lands],
        input_output_aliases={i: i for i in range(2 * n)}, compiler_params=SPLIT,
    )(*parts, *lands, send_sems, recv_sems, after)
    chip = 2 * lax.axis_index("x") + lax.axis_index("y")
    return [lax.dynamic_update_slice(s, lax.dynamic_index_in_dim(p, chip, 0, keepdims=True), (chip, 0, 0))
            for p, s in zip(outs[:n], outs[n:])]


def _half_swap(halves, tag):
    n = len(halves)
    core = lax.axis_index("c")
    bufs = [lax.dynamic_update_slice(lax.empty((2,) + h.shape, h.dtype), h[None], (core, 0, 0)) for h in halves]

    def body(*refs):
        dst = refs[n:2 * n]
        send_sems, recv_sems = refs[2 * n:]
        x, y, c, _ = _me()
        sibling = (x, y, 1 - c)
        cps = []
        for w in range(n):
            cp = _remote(dst[w].at[c], dst[w].at[c], send_sems.at[w], recv_sems.at[w], sibling)
            cp.start()
            cps.append(cp)
        for w in range(n):
            other = dst[w].at[1 - c]
            _remote(other, other, send_sems.at[w], recv_sems.at[w], sibling).wait_recv()
        for cp in cps:
            cp.wait_send()

    outs = pl.pallas_call(
        body, name="reduce_half_swap_" + tag, in_specs=[ANY] * n, out_specs=[ANY] * n,
        out_shape=[jax.ShapeDtypeStruct(b.shape, b.dtype) for b in bufs],
        input_output_aliases={w: w for w in range(n)},
        scratch_shapes=[pltpu.SemaphoreType.DMA((n,)), pltpu.SemaphoreType.DMA((n,))],
    )(*bufs)
    return [o.reshape(2 * o.shape[1], o.shape[2]) for o in outs]


def _add_parts(full, axis, rows, sib, name):
    _, r, c = sib.shape
    tr, tc = _tile(r, 1024, 16), _tile(c, 2048)
    nb = r // tr
    core = jnp.reshape(lax.axis_index("c"), (1,)).astype(jnp.int32)

    def body(c_ref, a_ref, b_ref, o_ref):
        o_ref[0] = (a_ref[...].astype(F32) + b_ref[0].astype(F32)).astype(BF16)

    blk = pl.BlockSpec((1, tr, tc), lambda j, i, l, cr: (j, i, l))
    return pl.pallas_call(
        body, name=name,
        grid_spec=pltpu.PrefetchScalarGridSpec(
            num_scalar_prefetch=1, grid=(4, nb, c // tc),
            in_specs=[pl.BlockSpec((tr, tc), lambda j, i, l, cr: ((_slot(axis, j) * 2 + cr[0]) * nb + i, l)), blk],
            out_specs=blk),
        out_shape=jax.ShapeDtypeStruct(sib.shape, BF16),
        compiler_params=_cparams(("parallel", "parallel", "parallel")),
    )(core, full, sib)


def _sum_slots(a, name):
    _, r, c = a.shape
    tr, tc = _tile(r, 512, 8), _tile(c, 2048)

    def body(a_ref, o_ref):
        v = a_ref[...].astype(F32)
        o_ref[...] = ((v[0] + v[1]) + v[2]) + v[3]

    return pl.pallas_call(
        body, name=name, grid=(r // tr, c // tc),
        in_specs=[pl.BlockSpec((4, tr, tc), lambda i, l: (0, i, l))],
        out_specs=pl.BlockSpec((tr, tc), lambda i, l: (i, l)),
        out_shape=jax.ShapeDtypeStruct((r, c), F32),
        compiler_params=_cparams(("parallel", "parallel")),
    )(a)


class _Reducer:
    def __init__(self, spec):
        self.spec = spec
        self.paired = {}
        self.pending = []

    def pair(self, name, full):
        ax, shp = self.spec[name]
        land = lax.empty((4, shp[0] // 2, shp[1]), full.dtype)
        arrays, sems, token = _split_start("reduce_pair_start_" + name, [full, land], _pair_geometry([ax], [shp]), 4)
        self.paired[name] = (arrays, sems)
        return token

    def ship(self, tag, names, after):
        parts = []
        for n in names:
            ax, shp = self.spec[n]
            arrays, sems = self.paired.pop(n)
            full, sib = _split_wait("reduce_pair_wait_" + n, arrays, sems, after, _pair_geometry([ax], [shp]))
            parts.append(_add_parts(full, ax, shp[0], sib, name=f"reduce_add_{n}"))
        parts, lands, send, recv, token = _chip_start(parts, tag)
        self.pending.append((tag, names, parts, lands, send, recv))
        return token

    def start(self, tag, grads):
        names = list(grads)
        fulls, axes = [grads[n] for n in names], [self.spec[n][0] for n in names]
        shapes = [self.spec[n][1] for n in names]
        from_sibling = _pair_exchange(fulls, axes, shapes, tag)
        parts = [_add_parts(f, a, r, s, name=f"reduce_add_{n}")
                 for n, f, a, (r, cl), s in zip(names, fulls, axes, shapes, from_sibling)]
        parts, lands, send, recv, token = _chip_start(parts, tag)
        self.pending.append((tag, names, parts, lands, send, recv))
        return token

    def finish(self, after, tags):
        out = {}
        for tag, names, parts, lands, send, recv in [p for p in self.pending if p[0] in tags]:
            slots = _chip_wait(parts, lands, send, recv, after, tag)
            halves = [_sum_slots(s, name=f"reduce_sum_{n}") for n, s in zip(names, slots)]
            out.update(zip(names, _half_swap(halves, tag)))
        return out

    def finish_start(self, after, tag):
        (_, names, parts, lands, send, recv), = [p for p in self.pending if p[0] == tag]
        slots = _chip_wait(parts, lands, send, recv, after, tag)
        halves = [_sum_slots(s, name=f"reduce_sum_{n}") for n, s in zip(names, slots)]
        core = lax.axis_index("c")
        bufs = [lax.dynamic_update_slice(lax.empty((2,) + h.shape, h.dtype), h[None], (core, 0, 0)) for h in halves]
        bufs, sems, _ = _split_start("reduce_half_swap_start_" + tag, bufs, _swap_geometry, len(bufs))
        return tag, names, bufs, sems

    def swap_wait(self, started, after):
        tag, names, bufs, sems = started
        outs = _split_wait("reduce_half_swap_wait_" + tag, bufs, sems, after, _swap_geometry)
        return dict(zip(names, [o.reshape(2 * o.shape[1], o.shape[2]) for o in outs]))


def _allreduce_small(pack, after):
    rows = pack.shape[0]

    def body(p_ref, _, o_ref, slots, send_sems, recv_sems):
        x, y, c, _ = _me()
        me = 4 * x + 2 * y + c
        slots[me] = p_ref[...]
        cps = []
        for r in range(1, 8):
            peer = (x ^ (r >> 2), y ^ ((r >> 1) & 1), c ^ (r & 1))
            cp = _remote(p_ref, slots.at[me], send_sems.at[r - 1], recv_sems.at[r - 1], peer)
            cp.start()
            cps.append(cp)
        for r in range(1, 8):
            frm = me ^ r
            _remote(slots.at[frm], slots.at[frm], send_sems.at[r - 1], recv_sems.at[r - 1], (x, y, c)).wait_recv()
        for cp in cps:
            cp.wait_send()
        acc = slots[0]
        for s in range(1, 8):
            acc = acc + slots[s]
        o_ref[...] = acc

    vm = pl.BlockSpec(memory_space=pltpu.VMEM)
    return pl.pallas_call(
        body, name="allreduce_small", in_specs=[vm, ANY], out_specs=vm,
        out_shape=jax.ShapeDtypeStruct(pack.shape, F32),
        scratch_shapes=[pltpu.VMEM((8, rows, HEAD_DIM), F32), pltpu.SemaphoreType.DMA((7,)),
                        pltpu.SemaphoreType.DMA((7,))],
    )(pack, after)


_ROWS = ["norm_mix", "norm_ffn", "mem_norm", "fox_q_norm", "fox_k_norm", "gdn_out_norm", "mem_q_norm",
         "mem_k_norm", "fox_f_bias", "gdn_a_log", "gdn_dt_bias"]


def _pack_rows(vals):
    out = []
    for name in _ROWS:
        v = vals[name].reshape(-1)
        n = -(-v.shape[0] // HEAD_DIM) * HEAD_DIM
        out.append(jnp.pad(v, (0, n - v.shape[0])).reshape(-1, HEAD_DIM))
    return jnp.concatenate(out, axis=0)


def _unpack_rows(pack, like):
    out, r = {}, 0
    for name in _ROWS:
        n = like[name].shape[-1]
        nr = -(-n // HEAD_DIM)
        out[name] = pack[r:r + nr].reshape(1, -1)[:, :n]
        r += nr
    return out, r


def kernel(x, mem, norm_mix, w_in, fox_f_bias, fox_q_norm, fox_k_norm, gdn_conv, gdn_a_log, gdn_dt_bias, gdn_out_norm, mem_norm, w_mem_kv, mem_q_norm, mem_k_norm, w_out, norm_ffn, w_gate_up, w_down, loss_target, m_norm_mix, m_w_in, m_fox_f_bias, m_fox_q_norm, m_fox_k_norm, m_gdn_conv, m_gdn_a_log, m_gdn_dt_bias, m_gdn_out_norm, m_mem_norm, m_w_mem_kv, m_mem_q_norm, m_mem_k_norm, m_w_out, m_norm_ffn, m_w_gate_up, m_w_down, v_norm_mix, v_w_in, v_fox_f_bias, v_fox_q_norm, v_fox_k_norm, v_gdn_conv, v_gdn_a_log, v_gdn_dt_bias, v_gdn_out_norm, v_mem_norm, v_w_mem_kv, v_mem_q_norm, v_mem_k_norm, v_w_out, v_norm_ffn, v_w_gate_up, v_w_down):
    a = dict(locals())
    d = x.shape[-1]
    lay = _Layout(d)
    chip = 2 * lax.axis_index("x") + lax.axis_index("y")
    small = {n: a[n] for n in _ROWS}
    big = ["w_in", "w_mem_kv", "w_out", "w_gate_up", "w_down"]
    axes = [0, 0, 0, 1, 0]

    conv_cols = gdn_conv.shape[-1]
    conv_n = CONV_WIDTH * conv_cols
    conv_rows = -(-conv_n // HEAD_DIM)
    conv_blk = jnp.pad(gdn_conv.reshape(-1), (0, 32 * HEAD_DIM - conv_n)).reshape(32, HEAD_DIM)
    axis_of = dict(zip(big, axes), conv=0, w_in_a=0, w_in_b=0)
    shape_of = {n: a[n].shape[1:] for n in big[1:]}
    shape_of.update(w_in_a=(w_in.shape[1], lay.cols_a), w_in_b=(w_in.shape[1], lay.cols_b), conv=conv_blk.shape)
    placed = {"w_in_a": _cast_place(w_in[0], 0, "cast_w_in_a", lambda v: lay.regroup(v)[:, :lay.cols_a], lay.cols_a),
              "conv": lax.dynamic_update_slice(lax.empty((4 * 32, HEAD_DIM), F32), conv_blk, (chip * 32, 0))}
    grouped = {"in_a": ["w_in_a"], "in_b": ["w_in_b"], "mixer": ["w_mem_kv", "conv"], "out": ["w_out"],
               "gate_up": ["w_gate_up"], "down": ["w_down"]}
    inflight = {}

    def start(tags, name):
        names = [n for t in tags for n in grouped[t]]
        bufs, sems, token = _gather_start([placed[n] for n in names], [axis_of[n] for n in names],
                                          [shape_of[n] for n in names],
                                          [[names.index(n) for n in grouped[t]] for t in tags], name)
        for t, pair in zip(tags, sems):
            inflight[t] = ([bufs[names.index(n)] for n in grouped[t]], pair)
        return token

    first = start(["in_a"], "gather_ici_start_in")
    placed["w_in_b"] = _cast_place(w_in[0], 0, "cast_w_in_b", lambda v: lay.regroup(v)[:, lay.cols_a:], lay.cols_b,
                                   after=first)
    placed.update({n: _cast_place(a[n][0], axis_of[n], "cast_" + n, after=first) for n in big[1:]})
    all_started = start(["in_b", "mixer", "out", "gate_up", "down"], "gather_ici_start_rest")
    all_started = _after_all("moments_ready", all_started, m_w_in[0], v_w_in[0])

    forwarding = {}

    def prefetch(tag, after):
        bufs, sem_pair = inflight.pop(tag)
        ax, shp = [axis_of[n] for n in grouped[tag]], [shape_of[n] for n in grouped[tag]]
        got = _gather_wait(bufs, ax, shp, sem_pair, all_started if tag == "in_a" else after,
                           "gather_ici_wait_" + tag)
        geometry = _forward_geometry(ax, shp)
        got, sems, _ = _split_start("gather_forward_start_" + tag, got, geometry, 3 * len(got))
        forwarding[tag] = (got, sems, geometry)

    def weights(tag, after):
        got, sems, geometry = forwarding.pop(tag)
        got = _split_wait("gather_forward_wait_" + tag, got, sems, after, geometry)
        if tag != "mixer":
            return got
        taps = got[1].reshape(4, 32 * HEAD_DIM)[:, :conv_n].reshape(4, CONV_WIDTH, conv_cols)
        return got[0], jnp.transpose(taps, (1, 0, 2)).reshape(CONV_WIDTH, 4 * conv_cols)

    sp = dict(small)
    reducer = _Reducer({n: (axis_of[n], shape_of[n]) for n in big[1:] + ["w_in_a", "w_in_b"]})
    loss_blk, dx, g = _local_step(x[0], mem[0], loss_target[0], prefetch, weights, reducer, sp)

    gsmall = {n: g[n] for n in _ROWS}
    pack = jnp.concatenate([_pack_rows(gsmall), g["gdn_conv"].reshape(-1, HEAD_DIM), loss_blk], axis=0)
    pack = jnp.pad(pack, ((0, -pack.shape[0] % 8), (0, 0)))
    out = {"grad_x": dx[None]}

    def adamw_shards(reduced):
        if "w_in_a" in reduced:
            reduced = {"w_in": (reduced["w_in_a"], reduced["w_in_b"])}
        for n, gsh in reduced.items():
            join = (lambda ga, gb: lay.ungroup(jnp.concatenate([ga, gb], axis=1))) if n == "w_in" else None
            res = _adamw(a[n][0], gsh, a["m_" + n][0], a["v_" + n][0], g_fn=join, name="adamw_" + n)
            for pre, r in zip(["grad_", "delta_", "new_m_", "new_v_"], res):
                out[pre + n] = r[None]
        return res[0]

    mix_swap = reducer.finish_start(dx, "mix")
    ffn_swap = reducer.finish_start(mix_swap[2][0], "ffn")
    done = adamw_shards(reducer.swap_wait(mix_swap, ffn_swap[2][0]))
    done = adamw_shards(reducer.swap_wait(ffn_swap, done))
    tot = _allreduce_small(pack, done)
    gs, r0 = _unpack_rows(tot, small)
    conv_g = tot[r0:r0 + CONV_WIDTH * 4 * conv_cols // HEAD_DIM].reshape(CONV_WIDTH, 4 * conv_cols)
    gs_conv = lax.dynamic_slice_in_dim(conv_g, chip * conv_cols, conv_cols, axis=1)
    out["loss"] = tot[r0 + CONV_WIDTH * 4 * conv_cols // HEAD_DIM, 0]
    adamw_shards(reducer.finish(tot, ("in",)))
    conv_pad = lambda v: jnp.pad(v.reshape(-1), (0, conv_rows * HEAD_DIM - conv_n)).reshape(conv_rows, HEAD_DIM)
    packs = []
    for src, cv in [(small, gdn_conv), (gs, gs_conv), ({n: a["m_" + n] for n in _ROWS}, m_gdn_conv),
                    ({n: a["v_" + n] for n in _ROWS}, v_gdn_conv)]:
        packs.append(jnp.concatenate([_pack_rows(src), conv_pad(cv)], axis=0))
    res = _adamw(*packs, name="adamw_small")
    for pre, r in zip(["grad_", "delta_", "new_m_", "new_v_"], res):
        vals, r1 = _unpack_rows(r, small)
        for n in _ROWS:
            out[pre + n] = vals[n]
        out[pre + "gdn_conv"] = r[r1:r1 + conv_rows].reshape(-1)[:conv_n].reshape(gdn_conv.shape)
    names = ["norm_mix", "w_in", "fox_f_bias", "fox_q_norm", "fox_k_norm", "gdn_conv", "gdn_a_log", "gdn_dt_bias",
             "gdn_out_norm", "mem_norm", "w_mem_kv", "mem_q_norm", "mem_k_norm", "w_out", "norm_ffn", "w_gate_up",
             "w_down"]
    return (out["loss"], out["grad_x"], *[out[p + n] for p in ["grad_", "delta_", "new_m_", "new_v_"] for n in names])
```

```python
import functools
import math

import jax
import jax.numpy as jnp
from jax import lax
from jax.experimental import pallas as pl
from jax.experimental.pallas import tpu as pltpu

F32, BF16 = jnp.float32, jnp.bfloat16
HEAD_DIM = 128
CHUNK = 64
N_MEM_HEADS = 4
CONV_WIDTH = 4
NORM_EPS = 1e-6
ADAM_LR, ADAM_B1, ADAM_B2, ADAM_EPS, ADAM_WD, ADAM_STEP = 0.001, 0.9, 0.999, 1e-08, 0.01, 10
VMEM_LIMIT = 48 * 1024 * 1024
NEG = -1e30
MESH = pl.DeviceIdType.MESH


def _cparams(sem=None, **kw):
    if sem is not None:
        kw["dimension_semantics"] = sem
    return pltpu.CompilerParams(vmem_limit_bytes=VMEM_LIMIT, **kw)


def _tile(n, target, mult=128):
    best = None
    d = mult
    while d <= min(n, target):
        if n % d == 0:
            best = d
        d += mult
    return best if best is not None else n


def _dot(a, b, dims, hi):
    if a.ndim == 3:
        dn = (((dims[0][0] + 1,), (dims[1][0] + 1,)), ((0,), (0,)))
    else:
        dn = (dims, ((), ()))
    if hi is not None:
        return lax.dot_general(a, b, dn, precision=hi, preferred_element_type=F32)
    return lax.dot_general(a.astype(BF16), b.astype(BF16), dn, preferred_element_type=F32)


def _make_dots(hi, cotangent=None):
    @jax.custom_vjp
    def nn(a, b):
        return _dot(a, b, ((1,), (0,)), hi)

    @jax.custom_vjp
    def nt(a, b):
        return _dot(a, b, ((1,), (1,)), hi)

    @jax.custom_vjp
    def tn(a, b):
        return _dot(a, b, ((0,), (0,)), hi)

    bnn, bnt, btn = cotangent or (nn, nt, tn)
    nn.defvjp(lambda a, b: (nn(a, b), (a, b)), lambda r, g: (bnt(g, r[1]), btn(r[0], g)))
    nt.defvjp(lambda a, b: (nt(a, b), (a, b)), lambda r, g: (bnn(g, r[1]), btn(g, r[0])))
    tn.defvjp(lambda a, b: (tn(a, b), (a, b)), lambda r, g: (bnt(r[1], g), bnn(r[0], g)))
    return nn, nt, tn


_nn, _nt, _tn = _make_dots(None)
_nn_hi, _nt_hi, _tn_hi = _make_dots(lax.Precision.HIGHEST)
_nn_x3, _nt_x3, _tn_x3 = _make_dots(lax.Precision.HIGH, (_nn, _nt, _tn))


def _sigmoid(x):
    return jax.nn.sigmoid(x)


@jax.custom_vjp
def _softplus(x):
    return jnp.maximum(x, 0.0) + jnp.log(1.0 + jnp.exp(-jnp.abs(x)))


_softplus.defvjp(lambda x: (_softplus(x), x), lambda x, g: (g * _sigmoid(x),))


def _silu(x):
    return x * _sigmoid(x)


def _rms_fn(x, gain, z=None):
    y = x * lax.rsqrt(jnp.mean(x * x, axis=-1, keepdims=True) + NORM_EPS) * gain
    if z is not None:
        y = y * _silu(z)
    return y


def _mm(a, b, *, ta=False, tb=False, out_dtype=F32, res=None, stack=None, after=None, name):
    a2, b2 = a.shape[-2:], b.shape[-2:]
    ns = b.shape[0] if stack else 1
    m = a2[1] if ta else a2[0]
    k = a2[0] if ta else a2[1]
    n = b2[0] if tb else b2[1]
    assert k == (b2[1] if tb else b2[0])
    tm, tn, tk = _mm_tiles(m, n, k, ns if stack == "sum" else 1, a.dtype.itemsize, b.dtype.itemsize,
                           jnp.dtype(out_dtype).itemsize, res is not None)
    nk = k // tk
    single = nk == 1 and stack != "sum"
    dims = ((0 if ta else 1,), (1 if tb else 0,))
    if stack == "sum":
        order = lambda g0, g1, g2, g3: (g2, g0, g1, g3)
        grid = (m // tm, n // tn, ns, nk)
    else:
        order = lambda g0, g1, g2, g3: (g0, g1, g2, g3)
        grid = (ns, m // tm, n // tn, nk)

    def body(*refs):
        if after is not None:
            refs = refs[:2 + (res is not None)] + refs[3 + (res is not None):]
        if single:
            a_ref, b_ref = refs[:2]
            r = lax.dot_general(a_ref[...].astype(BF16), b_ref[...].astype(BF16), (dims, ((), ())),
                                preferred_element_type=F32)
            if res is not None:
                r = r + refs[2][...]
            refs[-1][...] = r.astype(out_dtype)
            return
        if res is None:
            a_ref, b_ref, o_ref, acc = refs
        else:
            a_ref, b_ref, r_ref, o_ref, acc = refs
        s, _, _, kk = order(*[pl.program_id(d) for d in range(4)])
        first = kk == 0
        last = kk == nk - 1
        if stack == "sum":
            first, last = first & (s == 0), last & (s == ns - 1)

        @pl.when(first)
        def _():
            acc[...] = jnp.zeros_like(acc)

        acc[...] += lax.dot_general(a_ref[...].astype(BF16), b_ref[...].astype(BF16), (dims, ((), ())),
                                    preferred_element_type=F32)

        @pl.when(last)
        def _():
            r = acc[...]
            if res is not None:
                r = r + r_ref[...]
            o_ref[...] = r.astype(out_dtype)

    def spec(shape, idx, stacked):
        if stacked:
            return pl.BlockSpec((None,) + shape, lambda *g: (order(*g)[0],) + idx(*order(*g)))
        return pl.BlockSpec(shape, lambda *g: idx(*order(*g)))

    a_spec = (spec((tk, tm), lambda s, i, j, kk: (kk, i), stack == "sum") if ta
              else spec((tm, tk), lambda s, i, j, kk: (i, kk), stack == "sum"))
    b_spec = (spec((tn, tk), lambda s, i, j, kk: (j, kk), bool(stack)) if tb
              else spec((tk, tn), lambda s, i, j, kk: (kk, j), bool(stack)))
    o_spec = spec((tm, tn), lambda s, i, j, kk: (i, j), stack == "out")
    ins, specs = [a, b], [a_spec, b_spec]
    if res is not None:
        ins.append(res)
        specs.append(o_spec)
    if after is not None:
        ins.append(after)
        specs.append(pl.BlockSpec(after.shape, lambda *g: (0,) * after.ndim))
    sem = (("parallel", "parallel", "arbitrary", "arbitrary") if stack == "sum"
           else ("parallel", "parallel", "parallel", "arbitrary"))
    return pl.pallas_call(
        body, name=name, grid=grid, in_specs=specs, out_specs=o_spec,
        out_shape=jax.ShapeDtypeStruct(((ns,) if stack == "out" else ()) + (m, n), out_dtype),
        scratch_shapes=[] if single else [pltpu.VMEM((tm, tn), F32)],
        compiler_params=_cparams(sem),
    )(*ins)


MM_VMEM_BUDGET = 40 * 1024 * 1024
MXU_WIDTH = 256


def _mm_tiles(m, n, k, ns, sa, sb, so, has_res):
    def divs(x, mult, cap):
        out = [d for d in range(mult, min(x, cap) + 1, mult) if x % d == 0]
        return out or [x]

    best = None
    for tk in divs(k, 128, 8192):
        nk = (k // tk) * ns
        for tm in divs(m, 8, 2048):
            for tn in divs(n, 128, 2048):
                vmem = 2 * (tm * tk * sa + tk * tn * sb + tm * tn * so) + (2 * tm * tn * 4 if has_res else 0)
                vmem += tm * tn * 4 if nk > 1 else 0
                if vmem > MM_VMEM_BUDGET:
                    continue
                steps = (m // tm) * (n // tn) * nk
                traffic = (m // tm) * k * n * sb * ns + (n // tn if nk > 1 else 1) * m * k * sa * ns
                cost = steps * 0.4e-6 + traffic / 2.5e12 + (nk * m * n * 8 / 6e12 if nk > 1 else 0)
                cost += 2.0 * m * n * k * ns / 7e14 * (-(-tn // MXU_WIDTH) * MXU_WIDTH / tn)
                if best is None or cost < best[0]:
                    best = (cost, tm, tn, tk)
    return best[1:]


def _norm_fwd(x, xoff, gain, ncol, w, out_dtype, *, z=None, zoff=0, into=None, into_off=0, name):
    t = x.shape[0]
    tr = _tile(t, max(256, (1 << 18) // w), 8)

    def body(*refs):
        x_ref, g_ref, o_ref = refs[0], refs[1], refs[-1]
        y = _rms_fn(x_ref[...], g_ref[...]) if z is None else _rms_fn(x_ref[...], g_ref[...], refs[2][...])
        o_ref[...] = y.astype(out_dtype)

    ins = [x, gain]
    specs = [pl.BlockSpec((tr, w), lambda j, r: (r, xoff + j)), pl.BlockSpec((1, w), lambda j, r: (0, 0))]
    if z is not None:
        ins.append(z)
        specs.append(pl.BlockSpec((tr, w), lambda j, r: (r, zoff + j)))
    aliases = {}
    if into is not None:
        aliases = {len(ins): 0}
        ins.append(into)
        specs.append(pl.BlockSpec(memory_space=pl.ANY))
    return pl.pallas_call(
        body, name=name, grid=(ncol, t // tr), in_specs=specs,
        out_specs=pl.BlockSpec((tr, w), lambda j, r: (r, into_off + j)),
        out_shape=jax.ShapeDtypeStruct((t, ncol * w) if into is None else into.shape, out_dtype),
        input_output_aliases=aliases, compiler_params=_cparams(("parallel", "parallel")),
    )(*ins)


def _norm_bwd(x, xoff, gain, dy, dyoff, ncol, w, *, z=None, zoff=0, res=None, name):
    t = x.shape[0]
    tr = _tile(t, max(256, (1 << 18) // w), 8)

    def body(*refs):
        it = iter(refs)
        x_ref, g_ref = next(it), next(it)
        z_ref = next(it) if z is not None else None
        dy_ref = next(it)
        r_ref = next(it) if res is not None else None
        dx_ref = next(it)
        dz_ref = next(it) if z is not None else None
        dg_ref = next(it)

        @pl.when((pl.program_id(0) == 0) & (pl.program_id(1) == 0))
        def _():
            dg_ref[...] = jnp.zeros_like(dg_ref)

        args = (x_ref[...], g_ref[...]) + ((z_ref[...],) if z is not None else ())
        _, vjp = jax.vjp(_rms_fn, *args)
        grads = vjp(dy_ref[...].astype(F32))
        dx = grads[0]
        if res is not None:
            dx = dx + r_ref[...]
        dx_ref[...] = dx
        if z is not None:
            dz_ref[...] = grads[2]
        dg_ref[...] += grads[1]

    ins = [x, gain]
    specs = [pl.BlockSpec((tr, w), lambda j, r: (r, xoff + j)), pl.BlockSpec((1, w), lambda j, r: (0, 0))]
    if z is not None:
        ins.append(z)
        specs.append(pl.BlockSpec((tr, w), lambda j, r: (r, zoff + j)))
    ins.append(dy)
    specs.append(pl.BlockSpec((tr, w), lambda j, r: (r, dyoff + j)))
    blk = pl.BlockSpec((tr, w), lambda j, r: (r, j))
    if res is not None:
        ins.append(res)
        specs.append(blk)
    full = jax.ShapeDtypeStruct((t, ncol * w), F32)
    out_shape, out_specs = [full], [blk]
    if z is not None:
        out_shape.append(full)
        out_specs.append(blk)
    out_shape.append(jax.ShapeDtypeStruct((1, w), F32))
    out_specs.append(pl.BlockSpec((1, w), lambda j, r: (0, 0)))
    return pl.pallas_call(
        body, name=name, grid=(ncol, t // tr), in_specs=specs, out_specs=out_specs, out_shape=out_shape,
        compiler_params=_cparams(("arbitrary", "arbitrary")),
    )(*ins)


def _small_fn(x, pa, pb, nf, ng):
    lane = lax.broadcasted_iota(jnp.int32, x.shape, 1)
    zz = x + pb
    logf = -_softplus(-zz)
    g = -jnp.exp(pa) * _softplus(zz)
    beta = _sigmoid(x)
    return jnp.where(lane < nf, logf, jnp.where(lane < nf + ng, g, beta))


def _tri(n, upper):
    r = lax.broadcasted_iota(jnp.int32, (n, n), 0)
    c = lax.broadcasted_iota(jnp.int32, (n, n), 1)
    return jnp.where((c >= r) if upper else (c <= r), 1.0, 0.0).astype(F32)


def _small_fwd(p, off, pa, pb, nf, ng):
    t = p.shape[0]
    blk = HEAD_DIM
    nb = t // blk

    def body(x_ref, pa_ref, pb_ref, v_ref, c_ref):
        v_ref[...] = _small_fn(x_ref[...], pa_ref[...], pb_ref[...], nf, ng)
        tri = _tri(blk, False)

        carry = jnp.zeros((1, HEAD_DIM), F32)
        for i in range(nb):
            rows = slice(i * blk, (i + 1) * blk)
            c = _nn_hi(tri, v_ref[rows, :]) + carry
            c_ref[rows, :] = c
            carry = c[blk - 1:blk, :]

    row = pl.BlockSpec((1, HEAD_DIM), lambda i: (0, 0))
    out = pl.BlockSpec((t, HEAD_DIM), lambda i: (0, 0))
    return pl.pallas_call(
        body, name="small_fwd", grid=(1,),
        in_specs=[pl.BlockSpec((t, HEAD_DIM), lambda i: (0, off)), row, row], out_specs=[out, out],
        out_shape=[jax.ShapeDtypeStruct((t, HEAD_DIM), F32)] * 2,
        compiler_params=_cparams(("arbitrary",)),
    )(p, pa, pb)


def _small_bwd(p, off, pa, pb, dvals, dcsum, nf, ng):
    t = p.shape[0]
    blk = HEAD_DIM
    nb = t // blk

    def body(x_ref, pa_ref, pb_ref, dv_ref, dc_ref, dx_ref, dpa_ref, dpb_ref, tot_ref):
        tri = _tri(blk, True)

        carry = jnp.zeros((1, HEAD_DIM), F32)
        for i in reversed(range(nb)):
            rows = slice(i * blk, (i + 1) * blk)
            c = _nn_hi(tri, dc_ref[rows, :]) + carry
            tot_ref[rows, :] = c + dv_ref[rows, :]
            carry = c[0:1, :]
        f = functools.partial(_small_fn, nf=nf, ng=ng)
        _, vjp = jax.vjp(f, x_ref[...], pa_ref[...], pb_ref[...])
        dx, dpa, dpb = vjp(tot_ref[...])
        dx_ref[...] = dx
        dpa_ref[...] = dpa
        dpb_ref[...] = dpb

    row = pl.BlockSpec((1, HEAD_DIM), lambda i: (0, 0))
    full = pl.BlockSpec((t, HEAD_DIM), lambda i: (0, 0))
    return pl.pallas_call(
        body, name="small_bwd", grid=(1,),
        in_specs=[pl.BlockSpec((t, HEAD_DIM), lambda i: (0, off)), row, row, full, full],
        out_specs=[full, row, row],
        out_shape=[jax.ShapeDtypeStruct((t, HEAD_DIM), F32), jax.ShapeDtypeStruct((1, HEAD_DIM), F32),
                   jax.ShapeDtypeStruct((1, HEAD_DIM), F32)],
        scratch_shapes=[pltpu.VMEM((t, HEAD_DIM), F32)],
        compiler_params=_cparams(("arbitrary",)),
    )(p, pa, pb, dvals, dcsum)


def _fox_heads(nf, most):
    return next(h for h in range(most, 0, -1) if nf % h == 0)


def _fox_fwd(q, k, v, cc, cr, nf, tq, tk, d_mix):
    t = q.shape[0]
    scale = HEAD_DIM ** -0.5
    assert tq == tk

    vt = jnp.transpose(v.reshape(t // tk, tk, nf, HEAD_DIM), (2, 0, 3, 1))

    hp = _fox_heads(nf, 3)
    lanes = lambda h: slice(h * HEAD_DIM, (h + 1) * HEAD_DIM)

    def body(q_ref, k_ref, vt_ref, cc_ref, cr_ref, o_ref, lse_ref, mix_ref):
        i = pl.program_id(1)
        qs = [q_ref[:, lanes(h)] for h in range(hp)]
        cqs = [cr_ref[h, i] for h in range(hp)]
        ones = jnp.ones((8, tk), BF16)
        diff = lax.broadcasted_iota(jnp.int32, (tk, tq), 0) - lax.broadcasted_iota(jnp.int32, (tk, tq), 1)

        def scores(h, j):
            ks = pl.ds(pl.multiple_of(j * tk, tk), tk)
            return lax.dot_general(k_ref[ks, lanes(h)], qs[h], (((1,), (1,)), ((), ())),
                                   preferred_element_type=F32)

        def tile(h, j, m, l, acc, s, masked):
            ks = pl.ds(pl.multiple_of(j * tk, tk), tk)
            s = s * scale + cqs[h] - cc_ref[h, ks, :]
            if masked:
                s = jnp.where(diff <= 0, s, NEG)
            m_new = jnp.maximum(m, jnp.max(s, axis=0, keepdims=True))
            pr = jnp.exp(s - m_new).astype(BF16)
            alpha = jnp.exp(m - m_new)
            l = alpha * l + jnp.dot(ones, pr, preferred_element_type=F32)[:1]
            acc = alpha * acc + jnp.dot(vt_ref[h, j], pr, preferred_element_type=F32)
            return m_new, l, acc

        def step(j, carry):
            nxt = [scores(h, j + 1) for h in range(hp)]
            return tuple(tile(h, j, *carry[h], False) + (nxt[h],) for h in range(hp))

        init = tuple((jnp.full((1, tq), NEG, F32), jnp.zeros((1, tq), F32), jnp.zeros((HEAD_DIM, tq), F32),
                      scores(h, 0)) for h in range(hp))
        carry = lax.fori_loop(0, i, step, init)
        for h in range(hp):
            m, l, acc = tile(h, i, *carry[h], True)
            o = jnp.transpose(acc / l)
            o_ref[:, lanes(h)] = o
            mix_ref[:, lanes(h)] = o.astype(BF16)
            lse_ref[h, 0] = m + jnp.log(l)

    w = hp * HEAD_DIM
    qblk = pl.BlockSpec((tq, w), lambda h, i: (i, h))
    return pl.pallas_call(
        body, name="fox_fwd", grid=(nf // hp, t // tq),
        in_specs=[qblk, pl.BlockSpec((t, w), lambda h, i: (0, h)),
                  pl.BlockSpec((hp, t // tk, HEAD_DIM, tk), lambda h, i: (h, 0, 0, 0)),
                  pl.BlockSpec((hp, t, 1), lambda h, i: (h, 0, 0)),
                  pl.BlockSpec((hp, t // tk, 1, tk), lambda h, i: (h, 0, 0, 0))],
        out_specs=[qblk, pl.BlockSpec((hp, 1, 1, tq), lambda h, i: (h, i, 0, 0)), qblk],
        out_shape=[jax.ShapeDtypeStruct((t, nf * HEAD_DIM), F32), jax.ShapeDtypeStruct((nf, t // tq, 1, tq), F32),
                   jax.ShapeDtypeStruct((t, d_mix), BF16)],
        compiler_params=_cparams(("parallel", "parallel")),
    )(q, k, vt, cc, cr)


def _fox_bwd(q, k, v, cc, cr, o, lse, dmix, nf, tq, tk):
    t = q.shape[0]
    scale = HEAD_DIM ** -0.5
    assert tq == tk
    hp = _fox_heads(nf, 3)
    lanes = lambda h: slice(h * HEAD_DIM, (h + 1) * HEAD_DIM)
    kt = jnp.transpose(k.reshape(t // tk, tk, nf, HEAD_DIM), (2, 0, 3, 1))

    def body(q_ref, k_ref, kt_ref, v_ref, cc_ref, cr_ref, o_ref, lse_ref, do_ref,
             dq_ref, dk_ref, dv_ref, dcq_ref, dck_ref):
        i = pl.program_id(1)

        @pl.when(i == 0)
        def _():
            dk_ref[...] = jnp.zeros_like(dk_ref)
            dv_ref[...] = jnp.zeros_like(dv_ref)
            dck_ref[...] = jnp.zeros_like(dck_ref)

        diff = lax.broadcasted_iota(jnp.int32, (tk, tq), 0) - lax.broadcasted_iota(jnp.int32, (tk, tq), 1)
        qs = [q_ref[:, lanes(h)] for h in range(hp)]
        dos = [do_ref[:, lanes(h)] for h in range(hp)]
        do_b = [d.astype(BF16) for d in dos]
        cqs = [cr_ref[h, i] for h in range(hp)]
        lses = [lse_ref[h, 0] for h in range(hp)]
        deltas = [jnp.sum(jnp.transpose(dos[h] * o_ref[:, lanes(h)]), axis=0, keepdims=True) for h in range(hp)]

        def products(h, j):
            ks = pl.ds(pl.multiple_of(j * tk, tk), tk)
            nt = (((1,), (1,)), ((), ()))
            return (lax.dot_general(k_ref[ks, lanes(h)], qs[h], nt, preferred_element_type=F32),
                    lax.dot_general(v_ref[ks, lanes(h)], do_b[h], nt, preferred_element_type=F32))

        def tile(h, j, dqt, dcq, s, dp, masked):
            ks = pl.ds(pl.multiple_of(j * tk, tk), tk)
            pr = jnp.exp(s * scale + cqs[h] - cc_ref[h, ks, :] - lses[h])
            if masked:
                pr = jnp.where(diff <= 0, pr, 0.0)
            ds = pr * (dp - deltas[h])
            ds_b = ds.astype(BF16)
            dqt = dqt + jnp.dot(kt_ref[h, j], ds_b, preferred_element_type=F32)
            dk_ref[ks, lanes(h)] += jnp.dot(ds_b, qs[h], preferred_element_type=F32) * scale
            dv_ref[ks, lanes(h)] += jnp.dot(pr.astype(BF16), do_b[h], preferred_element_type=F32)
            dck_ref[h, ks, :] -= jnp.sum(ds, axis=1, keepdims=True)
            return dqt, dcq + jnp.sum(ds, axis=0, keepdims=True)

        def step(j, carry):
            nxt = [products(h, j + 1) for h in range(hp)]
            return tuple(tile(h, j, *carry[h], False) + nxt[h] for h in range(hp))

        init = tuple((jnp.zeros((HEAD_DIM, tq), F32), jnp.zeros((1, tq), F32)) + products(h, 0) for h in range(hp))
        carry = lax.fori_loop(0, i, step, init)
        for h in range(hp):
            dqt, dcq = tile(h, i, *carry[h], True)
            dq_ref[:, lanes(h)] = jnp.transpose(dqt) * scale
            dcq_ref[h, 0] = dcq

    w = hp * HEAD_DIM
    head_all = pl.BlockSpec((t, w), lambda h, i: (0, h))
    qblk = pl.BlockSpec((tq, w), lambda h, i: (i, h))
    colv = pl.BlockSpec((hp, t, 1), lambda h, i: (h, 0, 0))
    rows_all = pl.BlockSpec((hp, t // tk, 1, tk), lambda h, i: (h, 0, 0, 0))
    row_blk = pl.BlockSpec((hp, 1, 1, tq), lambda h, i: (h, i, 0, 0))
    wide = jax.ShapeDtypeStruct((t, nf * HEAD_DIM), F32)
    return pl.pallas_call(
        body, name="fox_bwd", grid=(nf // hp, t // tq),
        in_specs=[qblk, head_all, pl.BlockSpec((hp, t // tk, HEAD_DIM, tk), lambda h, i: (h, 0, 0, 0)), head_all,
                  colv, rows_all, qblk, row_blk, qblk],
        out_specs=[qblk, head_all, head_all, row_blk, colv],
        out_shape=[wide, wide, wide, jax.ShapeDtypeStruct((nf, t // tq, 1, tq), F32),
                   jax.ShapeDtypeStruct((nf, t, 1), F32)],
        compiler_params=_cparams(("parallel", "arbitrary")),
    )(q, k, kt, v, cc, cr, o, lse, dmix)


def _mem_fn(mq, mk, mv, gq, gk):
    qn = _rms_fn(mq, gq)
    kn = _rms_fn(mk, gk)
    s = _nt(qn, kn) * (HEAD_DIM ** -0.5)
    e = jnp.exp(s - lax.stop_gradient(jnp.max(s, axis=1, keepdims=True)))
    pr = e / jnp.sum(e, axis=1, keepdims=True)
    return _nn(pr, mv)


def _mem_specs(t, m, tq, qoff):
    qblk = pl.BlockSpec((tq, HEAD_DIM), lambda h, i: (i, qoff + h))
    kblk = pl.BlockSpec((m, HEAD_DIM), lambda h, i: (0, h))
    vblk = pl.BlockSpec((m, HEAD_DIM), lambda h, i: (0, N_MEM_HEADS + h))
    row = pl.BlockSpec((1, HEAD_DIM), lambda h, i: (0, 0))
    return qblk, kblk, vblk, row


def _mem_fwd(p, qoff, mkv, gq, gk, tq, into, into_off):
    t, m = p.shape[0], mkv.shape[0]
    qblk, kblk, vblk, row = _mem_specs(t, m, tq, qoff)

    def body(q_ref, k_ref, v_ref, gq_ref, gk_ref, _, o_ref):
        o_ref[...] = _mem_fn(q_ref[...], k_ref[...], v_ref[...], gq_ref[...], gk_ref[...]).astype(BF16)

    return pl.pallas_call(
        body, name="mem_fwd", grid=(N_MEM_HEADS, t // tq),
        in_specs=[qblk, kblk, vblk, row, row, pl.BlockSpec(memory_space=pl.ANY)],
        out_specs=pl.BlockSpec((tq, HEAD_DIM), lambda h, i: (i, into_off + h)),
        out_shape=jax.ShapeDtypeStruct(into.shape, BF16), input_output_aliases={5: 0},
        compiler_params=_cparams(("parallel", "parallel")),
    )(p, mkv, mkv, gq, gk, into)


def _mem_bwd(p, qoff, mkv, gq, gk, dmix, dooff, tq):
    t, m = p.shape[0], mkv.shape[0]
    qblk, kblk, vblk, row = _mem_specs(t, m, tq, qoff)

    def body(q_ref, k_ref, v_ref, gq_ref, gk_ref, do_ref, dq_ref, dkv_k_ref, dkv_v_ref, dgq_ref, dgk_ref):
        h, i = pl.program_id(0), pl.program_id(1)

        @pl.when((h == 0) & (i == 0))
        def _():
            dgq_ref[...] = jnp.zeros_like(dgq_ref)
            dgk_ref[...] = jnp.zeros_like(dgk_ref)

        @pl.when(i == 0)
        def _():
            dkv_k_ref[...] = jnp.zeros_like(dkv_k_ref)
            dkv_v_ref[...] = jnp.zeros_like(dkv_v_ref)

        _, vjp = jax.vjp(_mem_fn, q_ref[...], k_ref[...], v_ref[...], gq_ref[...], gk_ref[...])
        dq, dk, dv, dgq, dgk = vjp(do_ref[...])
        dq_ref[...] = dq
        dkv_k_ref[...] += dk
        dkv_v_ref[...] += dv
        dgq_ref[...] += dgq
        dgk_ref[...] += dgk

    oblk = pl.BlockSpec((tq, HEAD_DIM), lambda h, i: (i, h))
    kout = pl.BlockSpec((m, HEAD_DIM), lambda h, i: (0, h))
    half = jax.ShapeDtypeStruct((m, N_MEM_HEADS * HEAD_DIM), F32)
    rshape = jax.ShapeDtypeStruct((1, HEAD_DIM), F32)
    return pl.pallas_call(
        body, name="mem_bwd", grid=(N_MEM_HEADS, t // tq),
        in_specs=[qblk, kblk, vblk, row, row, pl.BlockSpec((tq, HEAD_DIM), lambda h, i: (i, dooff + h))],
        out_specs=[oblk, kout, kout, row, row],
        out_shape=[jax.ShapeDtypeStruct((t, N_MEM_HEADS * HEAD_DIM), F32), half, half, rshape, rshape],
        compiler_params=_cparams(("arbitrary", "arbitrary")),
    )(p, mkv, mkv, gq, gk, dmix)


def _shift_down(x, s):
    if s == 0:
        return x
    r = lax.broadcasted_iota(jnp.int32, x.shape, 0)
    return jnp.where(r >= s, pltpu.roll(x, s, 0), 0.0)


def _shift_up(x, s):
    if s == 0:
        return x
    n = x.shape[0]
    r = lax.broadcasted_iota(jnp.int32, x.shape, 0)
    return jnp.where(r < n - s, pltpu.roll(x, n - s, 0), 0.0)


def _conv_fn(x0, x1, x2, x3, w0, w1, w2, w3, kind):
    y = _silu(x0 * w0 + x1 * w1 + x2 * w2 + x3 * w3)
    if kind == 2:
        return y
    y = y * lax.rsqrt(jnp.sum(y * y, axis=-1, keepdims=True) + NORM_EPS)
    return y * (HEAD_DIM ** -0.5) if kind == 0 else y


def _conv_fwd(p, off, conv_w, ng):
    t = p.shape[0]

    def body(x_ref, w_ref, o_ref):
        kind = pl.program_id(0) // ng
        x = x_ref[...]
        xs = [_shift_down(x, CONV_WIDTH - 1 - j) for j in range(CONV_WIDTH)]
        ws = [w_ref[j:j + 1, :] for j in range(CONV_WIDTH)]
        for kd in range(3):
            @pl.when(kind == kd)
            def _(kd=kd):
                o_ref[...] = _conv_fn(*xs, *ws, kd)

    return pl.pallas_call(
        body, name="gdn_conv_fwd", grid=(3 * ng,),
        in_specs=[pl.BlockSpec((t, HEAD_DIM), lambda c: (0, off + c)),
                  pl.BlockSpec((CONV_WIDTH, HEAD_DIM), lambda c: (0, c))],
        out_specs=pl.BlockSpec((t, HEAD_DIM), lambda c: (0, c)),
        out_shape=jax.ShapeDtypeStruct((t, 3 * ng * HEAD_DIM), F32),
        compiler_params=_cparams(("parallel",)),
    )(p, conv_w)


def _conv_bwd(p, off, conv_w, dys, ng):
    t = p.shape[0]

    def body(x_ref, w_ref, dq_ref, dk_ref, dv_ref, dx_ref, dw_ref):
        kind = pl.program_id(0) // ng
        dy_refs = (dq_ref, dk_ref, dv_ref)
        x = x_ref[...]
        xs = [_shift_down(x, CONV_WIDTH - 1 - j) for j in range(CONV_WIDTH)]
        ws = [w_ref[j:j + 1, :] for j in range(CONV_WIDTH)]
        for kd in range(3):
            @pl.when(kind == kd)
            def _(kd=kd):
                _, vjp = jax.vjp(functools.partial(_conv_fn, kind=kd), *xs, *ws)
                g = vjp(dy_refs[kd][...])
                dx = _shift_up(g[0], CONV_WIDTH - 1)
                for j in range(1, CONV_WIDTH):
                    dx = dx + _shift_up(g[j], CONV_WIDTH - 1 - j)
                dx_ref[...] = dx
                for j in range(CONV_WIDTH):
                    dw_ref[j:j + 1, :] = g[CONV_WIDTH + j]

    blk = pl.BlockSpec((t, HEAD_DIM), lambda c: (0, c))
    head = lambda k: pl.BlockSpec((t, HEAD_DIM), lambda c: (0, jnp.where(c // ng == k, c % ng, 0)))
    wblk = pl.BlockSpec((CONV_WIDTH, HEAD_DIM), lambda c: (0, c))
    return pl.pallas_call(
        body, name="gdn_conv_bwd", grid=(3 * ng,),
        in_specs=[pl.BlockSpec((t, HEAD_DIM), lambda c: (0, off + c)), wblk] + [head(k) for k in range(3)],
        out_specs=[blk, wblk],
        out_shape=[jax.ShapeDtypeStruct((t, 3 * ng * HEAD_DIM), F32),
                   jax.ShapeDtypeStruct((CONV_WIDTH, 3 * ng * HEAD_DIM), F32)],
        compiler_params=_cparams(("parallel",)),
    )(p, conv_w, *dys)


def _lower_inverse(lower):
    c = lower.shape[-1]
    r = lax.broadcasted_iota(jnp.int32, (1, c, c), 1)
    e = lax.broadcasted_iota(jnp.int32, (1, c, c), 2)
    hi = lax.Precision.HIGH
    inv = jnp.where(r == e, 1.0, 0.0) - lower
    pw = lower
    for _ in range(int(math.log2(c)) - 1):
        pw = _dot(pw, pw, ((1,), (0,)), hi)
        inv = inv + _dot(inv, pw, ((1,), (0,)), hi)
    return inv


@jax.custom_vjp
def _solve(lower, inv, vb, kbg):
    hi = lax.Precision.HIGH
    return _dot(inv, vb, ((1,), (0,)), hi), _dot(inv, kbg, ((1,), (0,)), hi)


def _solve_fwd(lower, inv, vb, kbg):
    u, w = _solve(lower, inv, vb, kbg)
    return (u, w), (inv, u, w)


def _solve_bwd(res, cts):
    inv, u, w = res
    dvb, dkbg = _tn(inv, cts[0]), _tn(inv, cts[1])
    return -(_nt(dvb, u) + _nt(dkbg, w)), jnp.zeros_like(inv), dvb, dkbg


_solve.defvjp(_solve_fwd, _solve_bwd)


def _wy_fn(q, k, v, gcol, grow, bcol, inv=None):
    b, c, dk = q.shape
    r = lax.broadcasted_iota(jnp.int32, (1, c, c), 1)
    e = lax.broadcasted_iota(jnp.int32, (1, c, c), 2)
    tril, strict = e <= r, e < r
    gc_col = jnp.sum(jnp.where(tril, grow, 0.0), axis=2, keepdims=True)
    gc_row = jnp.sum(jnp.where(r <= e, gcol, 0.0), axis=1, keepdims=True)
    g_last = jnp.sum(gcol, axis=1, keepdims=True)
    decay = jnp.exp(jnp.where(tril, gc_col - gc_row, NEG))
    kb, vb = k * bcol, v * bcol
    lower = jnp.where(strict, _nt(kb, k) * decay, 0.0)
    if inv is None:
        inv = _lower_inverse(lower)
    u, w = _solve(lower, inv, vb, kb * jnp.exp(gc_col))
    attn = jnp.where(tril, _nt(q, k) * decay, 0.0)
    qg = q * jnp.exp(gc_col)
    kdec = k * jnp.exp(g_last - gc_col)
    egl = jnp.broadcast_to(jnp.exp(g_last), (b, 1, dk))
    return u, w, qg, kdec, attn, egl, inv


def _scan_fn(u, w, qg, kdec, attn, egl, state):
    v_new = u - _nn(w, state)
    o = _nn(qg, state) + _nn(attn, v_new)
    return o, state * egl + _tn(kdec, v_new)


GDN_CHUNKS_PER_STEP = 4


def _gdn_fwd(qkv, vals, grow, nf, ng):
    t = qkv.shape[0]
    nch = t // CHUNK

    cb = GDN_CHUNKS_PER_STEP
    *wy, inv = _gdn_wy(qkv, vals, grow, nf, ng, cb)

    def body(u_ref, w_ref, qg_ref, kd_ref, at_ref, eg_ref, o_ref, st_ref, state):
        @pl.when(pl.program_id(0) == 0)
        def _():
            state[...] = jnp.zeros_like(state)

        st_ref[:, 0] = state[...]
        heads = lambda ref: jnp.stack([ref[:, h * HEAD_DIM:(h + 1) * HEAD_DIM] for h in range(ng)])
        o, new = _scan_fn(heads(u_ref), heads(w_ref), heads(qg_ref), heads(kd_ref), at_ref[:, 0], eg_ref[:, 0],
                          state[...])
        for h in range(ng):
            o_ref[:, h * HEAD_DIM:(h + 1) * HEAD_DIM] = o[h]
        state[...] = new

    w = ng * HEAD_DIM
    blk = pl.BlockSpec((CHUNK, w), lambda i: (i, 0))
    o, states = pl.pallas_call(
        body, name="gdn_scan_fwd", grid=(nch,),
        in_specs=[blk, blk, blk, blk, pl.BlockSpec((ng, 1, CHUNK, CHUNK), lambda i: (0, i, 0, 0)),
                  pl.BlockSpec((ng, 1, 1, HEAD_DIM), lambda i: (0, i, 0, 0))],
        out_specs=[blk, pl.BlockSpec((ng, 1, HEAD_DIM, HEAD_DIM), lambda i: (0, i, 0, 0))],
        out_shape=[jax.ShapeDtypeStruct((t, w), F32),
                   jax.ShapeDtypeStruct((ng, nch, HEAD_DIM, HEAD_DIM), F32)],
        scratch_shapes=[pltpu.VMEM((ng, HEAD_DIM, HEAD_DIM), F32)],
        compiler_params=_cparams(("arbitrary",)),
    )(*wy)
    return o, (wy, inv, states)


def _wy_batch(q_ref, k_ref, v_ref, vals_ref, gr_ref, nf, ng, cb):
    idx = [(c, h) for c in range(cb) for h in range(ng)]
    rows = lambda c: slice(c * CHUNK, (c + 1) * CHUNK)
    lanes = lambda h: slice(h * HEAD_DIM, (h + 1) * HEAD_DIM)
    wide = lambda ref: jnp.stack([ref[rows(c), lanes(h)] for c, h in idx])
    col = lambda lane0: jnp.stack([vals_ref[rows(c), lane0 + h:lane0 + h + 1] for c, h in idx])
    return idx, (wide(q_ref), wide(k_ref), wide(v_ref), col(nf), jnp.stack([gr_ref[h, c] for c, h in idx]),
                 col(nf + ng))


def _gdn_wy(qkv, vals, grow, nf, ng, cb):
    t = qkv.shape[0]
    nch = t // CHUNK

    def body(q_ref, k_ref, v_ref, vals_ref, gr_ref, u_ref, w_ref, qg_ref, kd_ref, at_ref, eg_ref, inv_ref):
        idx, args = _wy_batch(q_ref, k_ref, v_ref, vals_ref, gr_ref, nf, ng, cb)
        u, w, qg, kd, at, eg, inv = _wy_fn(*args)
        for b, (c, h) in enumerate(idx):
            rows, lanes = slice(c * CHUNK, (c + 1) * CHUNK), slice(h * HEAD_DIM, (h + 1) * HEAD_DIM)
            u_ref[rows, lanes] = u[b]
            w_ref[rows, lanes] = w[b]
            qg_ref[rows, lanes] = qg[b]
            kd_ref[rows, lanes] = kd[b]
            at_ref[h, c] = at[b]
            eg_ref[h, c] = eg[b]
            inv_ref[h, c] = inv[b]

    wd = ng * HEAD_DIM
    blk = lambda o: pl.BlockSpec((cb * CHUNK, wd), lambda i: (i, o))
    col = pl.BlockSpec((cb * CHUNK, HEAD_DIM), lambda i: (i, 0))
    sq = pl.BlockSpec((ng, cb, CHUNK, CHUNK), lambda i: (0, i, 0, 0))
    wide = jax.ShapeDtypeStruct((t, wd), F32)
    sq_shape = jax.ShapeDtypeStruct((ng, nch, CHUNK, CHUNK), F32)
    return pl.pallas_call(
        body, name="gdn_wy_fwd", grid=(nch // cb,),
        in_specs=[blk(0), blk(1), blk(2), col, pl.BlockSpec((ng, cb, 1, CHUNK), lambda i: (0, i, 0, 0))],
        out_specs=[blk(0), blk(0), blk(0), blk(0), sq, pl.BlockSpec((ng, cb, 1, HEAD_DIM), lambda i: (0, i, 0, 0)),
                   sq],
        out_shape=[wide, wide, wide, wide, sq_shape, jax.ShapeDtypeStruct((ng, nch, 1, HEAD_DIM), F32), sq_shape],
        compiler_params=_cparams(("parallel",)),
    )(qkv, qkv, qkv, vals, grow)


def _gdn_bwd(qkv, vals, grow, saved, do, nf, ng):
    t = qkv.shape[0]
    nch = t // CHUNK
    cb = GDN_CHUNKS_PER_STEP // 2
    wy, inv, states = saved
    wd = ng * HEAD_DIM

    def scan_body(u_ref, w_ref, qg_ref, kd_ref, at_ref, eg_ref, st_ref, do_ref,
                  du_ref, dw_ref, dqg_ref, dkd_ref, dat_ref, deg_ref, dstate):
        @pl.when(pl.program_id(0) == 0)
        def _():
            dstate[...] = jnp.zeros_like(dstate)

        heads = lambda ref: jnp.stack([ref[:, h * HEAD_DIM:(h + 1) * HEAD_DIM] for h in range(ng)])
        _, vjp = jax.vjp(_scan_fn, heads(u_ref), heads(w_ref), heads(qg_ref), heads(kd_ref), at_ref[:, 0],
                         eg_ref[:, 0], st_ref[:, 0])
        du, dw, dqg, dkd, dat, deg, dst = vjp((heads(do_ref), dstate[...]))
        for h in range(ng):
            lanes = slice(h * HEAD_DIM, (h + 1) * HEAD_DIM)
            du_ref[:, lanes] = du[h]
            dw_ref[:, lanes] = dw[h]
            dqg_ref[:, lanes] = dqg[h]
            dkd_ref[:, lanes] = dkd[h]
        dat_ref[:, 0] = dat
        deg_ref[:, 0] = deg
        dstate[...] = dst

    rev = lambda i: nch - 1 - i
    blk = pl.BlockSpec((CHUNK, wd), lambda i: (rev(i), 0))
    atb = pl.BlockSpec((ng, 1, CHUNK, CHUNK), lambda i: (0, rev(i), 0, 0))
    egb = pl.BlockSpec((ng, 1, 1, HEAD_DIM), lambda i: (0, rev(i), 0, 0))
    wide = jax.ShapeDtypeStruct((t, wd), F32)
    at_shape = jax.ShapeDtypeStruct((ng, nch, CHUNK, CHUNK), F32)
    eg_shape = jax.ShapeDtypeStruct((ng, nch, 1, HEAD_DIM), F32)
    dwy = pl.pallas_call(
        scan_body, name="gdn_scan_bwd", grid=(nch,),
        in_specs=[blk, blk, blk, blk, atb, egb,
                  pl.BlockSpec((ng, 1, HEAD_DIM, HEAD_DIM), lambda i: (0, rev(i), 0, 0)), blk],
        out_specs=[blk, blk, blk, blk, atb, egb],
        out_shape=[wide, wide, wide, wide, at_shape, eg_shape],
        scratch_shapes=[pltpu.VMEM((ng, HEAD_DIM, HEAD_DIM), F32)],
        compiler_params=_cparams(("arbitrary",)),
    )(*wy, states, do)

    def wy_body(q_ref, k_ref, v_ref, vals_ref, gr_ref, du_ref, dw_ref, dqg_ref, dkd_ref, dat_ref, deg_ref,
                inv_ref, dq_ref, dk_ref, dv_ref, dvals_ref, dgr_ref):
        idx, args = _wy_batch(q_ref, k_ref, v_ref, vals_ref, gr_ref, nf, ng, cb)
        lane = lax.broadcasted_iota(jnp.int32, (CHUNK, HEAD_DIM), 1)
        kept = jnp.stack([inv_ref[h, c] for c, h in idx])
        rows = lambda c: slice(c * CHUNK, (c + 1) * CHUNK)
        lanes = lambda h: slice(h * HEAD_DIM, (h + 1) * HEAD_DIM)
        wide_ct = lambda ref: jnp.stack([ref[rows(c), lanes(h)] for c, h in idx])
        cts = (wide_ct(du_ref), wide_ct(dw_ref), wide_ct(dqg_ref), wide_ct(dkd_ref),
               jnp.stack([dat_ref[h, c] for c, h in idx]), jnp.stack([deg_ref[h, c] for c, h in idx]))
        _, vjp = jax.vjp(lambda *a: _wy_fn(*a, inv=kept)[:6], *args)
        dq, dk, dv, dgc, dgr, dbc = vjp(cts)
        for b, (c, h) in enumerate(idx):
            dq_ref[rows(c), lanes(h)] = dq[b]
            dk_ref[rows(c), lanes(h)] = dk[b]
            dv_ref[rows(c), lanes(h)] = dv[b]
            dgr_ref[h, c] = dgr[b]
        for c in range(cb):
            acc = jnp.zeros((CHUNK, HEAD_DIM), F32)
            for h in range(ng):
                acc = jnp.where(lane == nf + h, dgc[c * ng + h], acc)
                acc = jnp.where(lane == nf + ng + h, dbc[c * ng + h], acc)
            dvals_ref[rows(c), :] = acc

    cblk = lambda o: pl.BlockSpec((cb * CHUNK, wd), lambda i: (i, o))
    col = pl.BlockSpec((cb * CHUNK, HEAD_DIM), lambda i: (i, 0))
    rowv = pl.BlockSpec((ng, cb, 1, CHUNK), lambda i: (0, i, 0, 0))
    return pl.pallas_call(
        wy_body, name="gdn_wy_bwd", grid=(nch // cb,),
        in_specs=[cblk(0), cblk(1), cblk(2), col, rowv, cblk(0), cblk(0), cblk(0), cblk(0),
                  pl.BlockSpec((ng, cb, CHUNK, CHUNK), lambda i: (0, i, 0, 0)),
                  pl.BlockSpec((ng, cb, 1, HEAD_DIM), lambda i: (0, i, 0, 0)),
                  pl.BlockSpec((ng, cb, CHUNK, CHUNK), lambda i: (0, i, 0, 0))],
        out_specs=[cblk(0), cblk(0), cblk(0), col, rowv],
        out_shape=[wide, wide, wide, jax.ShapeDtypeStruct((t, HEAD_DIM), F32),
                   jax.ShapeDtypeStruct((ng, nch, 1, CHUNK), F32)],
        compiler_params=_cparams(("parallel",)),
    )(qkv, qkv, qkv, vals, grow, *dwy, inv)


def _swiglu_fn(gate, up):
    return _silu(gate) * up


FFN_TN = 256


def _ffn_up(n2, wgu4):
    _, d, w = wgu4.shape
    t = n2.shape[0]
    tn = _tile(w, FFN_TN)
    nb = w // tn

    def body(a_ref, b_ref, gu_ref, act_ref):
        av = a_ref[...]
        gate = jnp.dot(av, b_ref[0], preferred_element_type=F32)
        up = jnp.dot(av, b_ref[1], preferred_element_type=F32)
        gu_ref[0] = gate.astype(BF16)
        gu_ref[1] = up.astype(BF16)
        act_ref[...] = _swiglu_fn(gate, up).astype(BF16)

    return pl.pallas_call(
        body, name="ffn_up", grid=(2, nb),
        in_specs=[pl.BlockSpec((t, d), lambda j, l: (0, 0)), pl.BlockSpec((2, d, tn), lambda j, l: (j, 0, l))],
        out_specs=[pl.BlockSpec((2, t, tn), lambda j, l: (j, 0, l)),
                   pl.BlockSpec((t, tn), lambda j, l: (0, j * nb + l))],
        out_shape=[jax.ShapeDtypeStruct((4, t, w), BF16), jax.ShapeDtypeStruct((t, 2 * w), BF16)],
        compiler_params=_cparams(("parallel", "parallel")),
    )(n2, wgu4)


def _ffn_dact(dh2, wd, gu, after):
    _, t, w = gu.shape
    d = dh2.shape[1]
    tn = _tile(w, FFN_TN)
    nb = w // tn

    def body(a_ref, b_ref, gu_ref, _, o_ref):
        dact = lax.dot_general(a_ref[...], b_ref[...], (((1,), (1,)), ((), ())), preferred_element_type=F32)
        _, vjp = jax.vjp(_swiglu_fn, gu_ref[0].astype(F32), gu_ref[1].astype(F32))
        dg, du = vjp(dact)
        o_ref[0] = dg.astype(BF16)
        o_ref[1] = du.astype(BF16)

    pair = pl.BlockSpec((2, t, tn), lambda j, l: (j, 0, l))
    return pl.pallas_call(
        body, name="ffn_dact", grid=(2, nb),
        in_specs=[pl.BlockSpec((t, d), lambda j, l: (0, 0)), pl.BlockSpec((tn, d), lambda j, l: (j * nb + l, 0)),
                  pair, pl.BlockSpec(after.shape, lambda j, l: (0, 0))],
        out_specs=pair, out_shape=jax.ShapeDtypeStruct(gu.shape, BF16),
        compiler_params=_cparams(("parallel", "parallel")),
    )(dh2, wd, gu, after)


def _loss_head(h2, target):
    t, d = h2.shape
    tr = _tile(t, 256, 8)

    def body(h_ref, t_ref, l_ref, d_ref, db_ref):
        @pl.when(pl.program_id(0) == 0)
        def _():
            l_ref[...] = jnp.zeros_like(l_ref)

        err = h_ref[...] - t_ref[...]
        d_ref[...] = err * (1.0 / d)
        db_ref[...] = (err * (1.0 / d)).astype(BF16)
        part = 0.5 * jnp.sum(jnp.mean(err * err, axis=-1, keepdims=True), axis=0, keepdims=True)
        lane = lax.broadcasted_iota(jnp.int32, (8, HEAD_DIM), 1)
        row = lax.broadcasted_iota(jnp.int32, (8, HEAD_DIM), 0)
        l_ref[...] += jnp.where((lane == 0) & (row == 0), part, 0.0)

    blk = pl.BlockSpec((tr, d), lambda r: (r, 0))
    return pl.pallas_call(
        body, name="loss_head", grid=(t // tr,), in_specs=[blk, blk],
        out_specs=[pl.BlockSpec((8, HEAD_DIM), lambda r: (0, 0)), blk, blk],
        out_shape=[jax.ShapeDtypeStruct((8, HEAD_DIM), F32), jax.ShapeDtypeStruct((t, d), F32),
                   jax.ShapeDtypeStruct((t, d), BF16)],
        compiler_params=_cparams(("arbitrary",)),
    )(h2, target)


def _adamw(w, g, m, v, *, g_fn=None, name):
    r, c = w.shape
    tr = _tile(r, max(8, (1 << 19) // c // 8 * 8), 8)
    gs = g if isinstance(g, tuple) else (g,)

    def body(w_ref, *refs):
        g_refs, (m_ref, v_ref, go_ref, d_ref, mo_ref, vo_ref) = refs[:len(gs)], refs[len(gs):]
        gr = g_refs[0][...] if g_fn is None else g_fn(*[ref[...] for ref in g_refs])
        mn = ADAM_B1 * m_ref[...] + (1.0 - ADAM_B1) * gr
        vn = ADAM_B2 * v_ref[...] + (1.0 - ADAM_B2) * (gr * gr)
        m_hat = mn / (1.0 - ADAM_B1 ** ADAM_STEP)
        v_hat = vn / (1.0 - ADAM_B2 ** ADAM_STEP)
        go_ref[...] = gr
        d_ref[...] = -ADAM_LR * (m_hat / (jnp.sqrt(v_hat) + ADAM_EPS) + ADAM_WD * w_ref[...])
        mo_ref[...] = mn
        vo_ref[...] = vn

    blk = pl.BlockSpec((tr, c), lambda i: (i, 0))
    gblks = [pl.BlockSpec((tr, gi.shape[1]), lambda i: (i, 0)) for gi in gs]
    return pl.pallas_call(
        body, name=name, grid=(r // tr,), in_specs=[blk] + gblks + [blk, blk], out_specs=[blk] * 4,
        out_shape=[jax.ShapeDtypeStruct((r, c), F32)] * 4,
        compiler_params=_cparams(("parallel",)),
    )(w, *gs, m, v)


class _Layout:
    def __init__(self, d):
        nh = d // HEAD_DIM
        self.nm = N_MEM_HEADS
        self.nf = (nh - self.nm) // 2
        self.ng = nh - self.nm - self.nf
        nf, ng, nm, hd = self.nf, self.ng, self.nm, HEAD_DIM
        self.o_fq, self.o_fk, self.o_fv, self.o_sm = 0, nf, 2 * nf, 3 * nf
        self.o_gq, self.o_gz, self.o_mq = 0, 3 * ng, 4 * ng
        self.cols_a = -(-(3 * nf + 1) // 4) * 4 * hd
        self.cols_b = -(-(4 * ng + nm) // 4) * 4 * hd
        self.cols = self.cols_a + self.cols_b
        sizes = [nf * hd, nf * hd, nf * hd, nf, 3 * ng * hd, ng * hd, ng, ng, nm * hd]
        starts = [sum(sizes[:i]) for i in range(len(sizes))]
        self.ref = list(zip(starts, sizes))
        self.in_cols = sum(sizes)

    def regroup(self, w):
        part = lambda i: w[:, self.ref[i][0]:self.ref[i][0] + self.ref[i][1]]
        a = [part(0), part(1), part(2), part(3), part(6), part(7)]
        b = [part(4), part(5), part(8)]
        pads = [self.cols_a - sum(p.shape[1] for p in a), self.cols_b - sum(p.shape[1] for p in b)]
        fill = [[jnp.zeros((w.shape[0], n), w.dtype)] if n else [] for n in pads]
        return jnp.concatenate(a + fill[0] + b + fill[1], axis=1)

    def ungroup(self, g):
        hd, nf, ng, nm = HEAD_DIM, self.nf, self.ng, self.nm
        sm, b0 = self.o_sm * hd, self.cols_a
        return jnp.concatenate([
            g[:, :3 * nf * hd], g[:, sm:sm + nf], g[:, b0:b0 + 3 * ng * hd],
            g[:, b0 + self.o_gz * hd:b0 + self.o_mq * hd], g[:, sm + nf:sm + nf + ng],
            g[:, sm + nf + ng:sm + nf + 2 * ng], g[:, b0 + self.o_mq * hd:b0 + (self.o_mq + nm) * hd]], axis=1)


def _lane_row(pieces):
    row = jnp.zeros((1, HEAD_DIM), F32)
    for off, a in pieces:
        row = lax.dynamic_update_slice(row, a.astype(F32), (0, off))
    return row


def _local_step(x, mem, target, prefetch, weights, reducer, sp):
    t, d = x.shape
    lay = _Layout(d)
    nf, ng, nm, hd = lay.nf, lay.ng, lay.nm, HEAD_DIM
    nch = t // CHUNK
    tq = _tile(t, 256)
    tk = tq

    u = _norm_fwd(x, 0, sp["norm_mix"], 1, d, BF16, name="norm_mix_fwd")
    prefetch("in_a", u)
    (win_a,) = weights("in_a", u)
    p_a = _mm(u, win_a, name="mm_in_a")
    pa = _lane_row([(nf, sp["gdn_a_log"])])
    pb = _lane_row([(0, sp["fox_f_bias"]), (nf, sp["gdn_dt_bias"])])
    vals, csum = _small_fwd(p_a, lay.o_sm, pa, pb, nf, ng)

    c_t = csum[:, :nf].T
    cc, cr = c_t.reshape(nf, t, 1), c_t.reshape(nf, t // tk, 1, tk)
    fq = _norm_fwd(p_a, lay.o_fq, sp["fox_q_norm"], nf, hd, BF16, name="fox_qnorm_fwd")
    fk = _norm_fwd(p_a, lay.o_fk, sp["fox_k_norm"], nf, hd, BF16, name="fox_knorm_fwd")
    fv = p_a[:, lay.o_fv * hd:(lay.o_fv + nf) * hd].astype(BF16)
    o_fox, lse, mix = _fox_fwd(fq, fk, fv, cc, cr, nf, tq, tk, d)

    prefetch("in_b", lse)
    (win_b,) = weights("in_b", lse)
    prefetch("mixer", win_b)
    p = _mm(u, win_b, name="mm_in_b")
    wmkv, conv_taps = weights("mixer", p)
    sp = dict(sp, gdn_conv=conv_taps)
    qkv = _conv_fwd(p, lay.o_gq, sp["gdn_conv"], ng)
    grow = vals[:, nf:nf + ng].T.reshape(ng, nch, 1, CHUNK)
    o_g, states = _gdn_fwd(qkv, vals, grow, nf, ng)
    mix = _norm_fwd(o_g, 0, sp["gdn_out_norm"], ng, hd, BF16, z=p, zoff=lay.o_gz, into=mix, into_off=nf,
                    name="gdn_out_fwd")
    prefetch("out", mix)

    mem_n = _norm_fwd(mem, 0, sp["mem_norm"], 1, d, BF16, name="mem_norm_fwd")
    mkv = _mm(mem_n, wmkv, name="mm_memkv")
    tq_mem = _tile(t, 1024)
    mix = _mem_fwd(p, lay.o_mq, mkv, sp["mem_q_norm"], sp["mem_k_norm"], tq_mem, mix, nf + ng)
    prefetch("gate_up", mix)
    (wout,) = weights("out", mix)
    h1 = _mm(mix, wout, res=x, name="mm_out")
    n2 = _norm_fwd(h1, 0, sp["norm_ffn"], 1, d, BF16, name="norm_ffn_fwd")
    (wgu,) = weights("gate_up", n2)
    wgu4 = wgu.reshape(4, d, -1)
    gu, act = _ffn_up(n2, wgu4)
    prefetch("down", act)
    (wd,) = weights("down", act)
    h2 = _mm(act, wd, res=h1, name="mm_down")
    loss_blk, dh2, dh2_b = _loss_head(h2, target)

    g = {}
    token = reducer.pair("w_down", _mm(act, dh2_b, ta=True, out_dtype=BF16, name="mm_dw_down"))
    dgu = _ffn_dact(dh2_b, wd, gu, token)
    dw_gate_up = _mm(n2, dgu, ta=True, stack="out", out_dtype=BF16, name="mm_dw_gate_up").reshape(wgu.shape)
    token = reducer.pair("w_gate_up", dw_gate_up)
    dn2 = _mm(dgu, wgu4, tb=True, stack="sum", after=token, name="mm_dn2")
    token = reducer.ship("ffn", ["w_down", "w_gate_up"], dn2)
    dh1, g["norm_ffn"] = _norm_bwd(h1, 0, sp["norm_ffn"] + token[0, 0], dn2, 0, 1, d, res=dh2,
                                   name="norm_ffn_bwd")
    token = reducer.pair("w_out", _mm(mix, dh1, ta=True, out_dtype=BF16, name="mm_dw_out"))
    dmix = _mm(dh1, wout, tb=True, after=token, name="mm_dmix")

    dmq, dmk, dmv, g["mem_q_norm"], g["mem_k_norm"] = _mem_bwd(
        p, lay.o_mq, mkv, sp["mem_q_norm"], sp["mem_k_norm"], dmix, nf + ng, tq_mem)
    dmkv = jnp.concatenate([dmk, dmv], axis=1)
    token = reducer.pair("w_mem_kv", _mm(mem_n, dmkv, ta=True, out_dtype=BF16, name="mm_dw_memkv"))
    dmem_n = _mm(dmkv, wmkv, tb=True, after=token, name="mm_dmem")
    token = reducer.ship("mix", ["w_out", "w_mem_kv"], dmem_n)
    _, g["mem_norm"] = _norm_bwd(mem, 0, sp["mem_norm"], dmem_n, 0, 1, d, name="mem_norm_bwd")

    do_g, dgz, g["gdn_out_norm"] = _norm_bwd(o_g, 0, sp["gdn_out_norm"] + token[0, 0], dmix, nf, ng, hd, z=p,
                                             zoff=lay.o_gz, name="gdn_out_bwd")
    dq, dk, dv, dvals, dgr = _gdn_bwd(qkv, vals, grow, states, do_g, nf, ng)
    dgqkv, g["gdn_conv"] = _conv_bwd(p, lay.o_gq, sp["gdn_conv"], (dq, dk, dv), ng)

    dfq_n, dfk_n, dfv, dcc, dcr = _fox_bwd(fq, fk, fv, cc, cr, o_fox, lse, dmix, nf, tq, tk)
    dfq, g["fox_q_norm"] = _norm_bwd(p_a, lay.o_fq, sp["fox_q_norm"], dfq_n, 0, nf, hd, name="fox_qnorm_bwd")
    dfk, g["fox_k_norm"] = _norm_bwd(p_a, lay.o_fk, sp["fox_k_norm"], dfk_n, 0, nf, hd, name="fox_knorm_bwd")
    dc_t = dcc.reshape(nf, t) + dcr.reshape(nf, t)

    dvals = dvals + jnp.pad(dgr.reshape(ng, t).T, ((0, 0), (nf, hd - nf - ng)))
    dcsum = jnp.concatenate([dc_t.T, jnp.zeros((t, hd - nf), F32)], axis=1)
    dsm, dpa, dpb = _small_bwd(p_a, lay.o_sm, pa, pb, dvals, dcsum, nf, ng)
    g["fox_f_bias"] = dpb[:, :nf]
    g["gdn_dt_bias"] = dpb[:, nf:nf + ng]
    g["gdn_a_log"] = dpa[:, nf:nf + ng]

    zeros = lambda n: jnp.zeros((t, n), F32)
    dp_a = jnp.concatenate([dfq, dfk, dfv, dsm, zeros(lay.cols_a - (lay.o_sm + 1) * hd)], axis=1).astype(BF16)
    dp_b = jnp.concatenate([dgqkv, dgz, dmq, zeros(lay.cols_b - (lay.o_mq + nm) * hd)], axis=1).astype(BF16)
    token = reducer.pair("w_in_a", _mm(u, dp_a, ta=True, out_dtype=BF16, name="mm_dw_in_a"))
    token = reducer.pair("w_in_b", _mm(u, dp_b, ta=True, out_dtype=BF16, after=token, name="mm_dw_in_b"))
    du = _mm(dp_a, win_a, tb=True, after=token, name="mm_du_a")
    token = reducer.ship("in", ["w_in_a", "w_in_b"], du)
    du = _mm(dp_b, win_b, tb=True, res=du, after=token, name="mm_du_b")
    dx, g["norm_mix"] = _norm_bwd(x, 0, sp["norm_mix"], du, 0, 1, d, res=dh1, name="norm_mix_bwd")
    return loss_blk, dx, g


ANY = pl.BlockSpec(memory_space=pl.ANY)


def _me():
    x, y, c = lax.axis_index("x"), lax.axis_index("y"), lax.axis_index("c")
    chips = [(1 - x, y), (x, 1 - y), (1 - x, 1 - y)]
    return x, y, c, chips


def _slot(axis, k):
    return k if axis == 0 else 2 * (k % 2) + k // 2


def _slab(ref, axis, rows, cols, k, h):
    half = rows // 2
    return ref.at[pl.ds(_slot(axis, k) * rows + h * half, half), :]


def _remote(src, dst, send_sem, recv_sem, dev):
    return pltpu.make_async_remote_copy(src_ref=src, dst_ref=dst, send_sem=send_sem, recv_sem=recv_sem,
                                        device_id=dev, device_id_type=MESH)


HBM = pl.BlockSpec(memory_space=pltpu.HBM)
SEM = pl.BlockSpec(memory_space=pltpu.SEMAPHORE)
SPLIT = pltpu.CompilerParams(has_side_effects=pltpu.SideEffectType.DATAFLOW_SIDE_EFFECTING)
TOKEN = jax.ShapeDtypeStruct((8, HEAD_DIM), F32)


def _in_hbm(v):
    return pltpu.with_memory_space_constraint(v, pltpu.HBM)


def _cast_place(shard, axis, name, col_fn=None, out_cols=None, after=None):
    r, c = shard.shape
    oc = out_cols or c
    tr = _tile(r, 512 if col_fn is None else 64, 16)
    tc = _tile(c, 2048) if col_fn is None else c
    otc = tc if col_fn is None else oc
    nb = r // tr
    chip = 2 * lax.axis_index("x") + lax.axis_index("y")
    slot = jnp.reshape(_slot(axis, chip), (1,)).astype(jnp.int32)

    def body(slot_ref, x_ref, *rest):
        x = x_ref[...]
        rest[-1][...] = (x if col_fn is None else col_fn(x)).astype(BF16)

    extra = [] if after is None else [after]
    return pl.pallas_call(
        body, name=name,
        grid_spec=pltpu.PrefetchScalarGridSpec(
            num_scalar_prefetch=1, grid=(nb, c // tc),
            in_specs=[pl.BlockSpec((tr, tc), lambda i, l, s: (i, l))] + [ANY] * len(extra),
            out_specs=pl.BlockSpec((tr, otc), lambda i, l, s: (s[0] * nb + i, l))),
        out_shape=jax.ShapeDtypeStruct((4 * r, oc), BF16),
        compiler_params=_cparams(("parallel", "parallel")),
    )(slot, shard, *extra)


def _gather_start(bufs, axes, shapes, groups, name):
    n = len(bufs)

    def body(*refs):
        dst = refs[n:2 * n]
        sems = refs[2 * n:2 * n + 2 * len(groups)]
        token = refs[-1]
        x, y, c, chips = _me()
        k = 2 * x + y
        for gi, ws in enumerate(groups):
            for i, w in enumerate(ws):
                r, cl = shapes[w]
                place = _slab(dst[w], axes[w], r, cl, k, c)
                for j, (px, py) in enumerate(chips):
                    _remote(place, place, sems[2 * gi].at[3 * i + j], sems[2 * gi + 1].at[3 * i + j],
                            (px, py, c)).start()
        token[...] = jnp.zeros_like(token)

    sem_shapes = [pltpu.SemaphoreType.DMA((3 * len(ws),)) for ws in groups for _ in range(2)]
    outs = pl.pallas_call(
        body, name=name, in_specs=[HBM] * n,
        out_specs=[HBM] * n + [SEM] * len(sem_shapes) + [pl.BlockSpec(memory_space=pltpu.VMEM)],
        out_shape=[pltpu.HBM(b.shape, b.dtype) for b in bufs] + sem_shapes + [TOKEN],
        input_output_aliases={w: w for w in range(n)}, compiler_params=SPLIT,
    )(*[_in_hbm(b) for b in bufs])
    sems = outs[n:-1]
    return outs[:n], [(sems[2 * g], sems[2 * g + 1]) for g in range(len(groups))], outs[-1]


def _gather_wait(bufs, axes, shapes, sems, after, name):
    n = len(bufs)

    def body(*refs):
        send_sems, recv_sems = refs[n], refs[n + 1]
        dst = refs[n + 3:]
        x, y, c, chips = _me()
        k = 2 * x + y
        for i in range(n):
            r, cl = shapes[i]
            for j, (px, py) in enumerate(chips):
                got = _slab(dst[i], axes[i], r, cl, 2 * px + py, c)
                _remote(got, got, send_sems.at[3 * i + j], recv_sems.at[3 * i + j], (px, py, c)).wait_recv()
        for i in range(n):
            r, cl = shapes[i]
            mine = _slab(dst[i], axes[i], r, cl, k, c)
            for j, (px, py) in enumerate(chips):
                _remote(mine, mine, send_sems.at[3 * i + j], recv_sems.at[3 * i + j], (px, py, c)).wait_send()

    return pl.pallas_call(
        body, name=name, in_specs=[HBM] * n + [SEM, SEM, ANY], out_specs=[HBM] * n,
        out_shape=[pltpu.HBM(b.shape, b.dtype) for b in bufs],
        input_output_aliases={i: i for i in range(n)}, compiler_params=SPLIT,
    )(*bufs, sems[0], sems[1], after)


def _gather_forward(bufs, axes, shapes, name):
    n = len(bufs)

    def body(*refs):
        dst = refs[n:2 * n]
        send_sems, recv_sems = refs[2 * n:]
        x, y, c, chips = _me()
        sibling = (x, y, 1 - c)
        sends = []
        for i in range(n):
            r, cl = shapes[i]
            for j, (px, py) in enumerate(chips):
                got = _slab(dst[i], axes[i], r, cl, 2 * px + py, c)
                cp = _remote(got, got, send_sems.at[3 * i + j], recv_sems.at[3 * i + j], sibling)
                cp.start()
                sends.append(cp)
        for i in range(n):
            r, cl = shapes[i]
            for j, (px, py) in enumerate(chips):
                got = _slab(dst[i], axes[i], r, cl, 2 * px + py, 1 - c)
                _remote(got, got, send_sems.at[3 * i + j], recv_sems.at[3 * i + j], sibling).wait_recv()
        for cp in sends:
            cp.wait_send()

    return pl.pallas_call(
        body, name=name, in_specs=[ANY] * n, out_specs=[ANY] * n,
        out_shape=[jax.ShapeDtypeStruct(b.shape, b.dtype) for b in bufs],
        input_output_aliases={i: i for i in range(n)},
        scratch_shapes=[pltpu.SemaphoreType.DMA((3 * n,)), pltpu.SemaphoreType.DMA((3 * n,))],
    )(*bufs)


def _split_start(name, arrays, geometry, count):
    n = len(arrays)

    def body(*refs):
        send, recv, token = refs[2 * n:]
        for i, (src, dst, _, dev) in enumerate(geometry(refs[n:2 * n])):
            _remote(src, dst, send.at[i], recv.at[i], dev).start()
        token[...] = jnp.zeros_like(token)

    sem = pltpu.SemaphoreType.DMA((count,))
    outs = pl.pallas_call(
        body, name=name, in_specs=[HBM] * n,
        out_specs=[HBM] * n + [SEM, SEM, pl.BlockSpec(memory_space=pltpu.VMEM)],
        out_shape=[pltpu.HBM(v.shape, v.dtype) for v in arrays] + [sem, sem, TOKEN],
        input_output_aliases={i: i for i in range(n)}, compiler_params=SPLIT,
    )(*[_in_hbm(v) for v in arrays])
    return list(outs[:n]), (outs[n], outs[n + 1]), outs[-1]


def _split_wait(name, arrays, sems, after, geometry):
    n = len(arrays)

    def body(*refs):
        send, recv = refs[n], refs[n + 1]
        copies = geometry(refs[n + 3:])
        for i, (_, _, land, dev) in enumerate(copies):
            _remote(land, land, send.at[i], recv.at[i], dev).wait_recv()
        for i, (src, _, _, dev) in enumerate(copies):
            _remote(src, src, send.at[i], recv.at[i], dev).wait_send()

    return list(pl.pallas_call(
        body, name=name, in_specs=[HBM] * n + [SEM, SEM, ANY], out_specs=[HBM] * n,
        out_shape=[pltpu.HBM(v.shape, v.dtype) for v in arrays],
        input_output_aliases={i: i for i in range(n)}, compiler_params=SPLIT,
    )(*arrays, sems[0], sems[1], after))


def _forward_geometry(axes, shapes):
    def geometry(bufs):
        x, y, c, chips = _me()
        out = []
        for i, buf in enumerate(bufs):
            r, cl = shapes[i]
            for px, py in chips:
                got = _slab(buf, axes[i], r, cl, 2 * px + py, c)
                out.append((got, got, _slab(buf, axes[i], r, cl, 2 * px + py, 1 - c), (x, y, 1 - c)))
        return out
    return geometry


def _pair_geometry(axes, shapes):
    def geometry(refs):
        n = len(refs) // 2
        x, y, c, _ = _me()
        out = []
        for w in range(n):
            r, cl = shapes[w]
            for j in range(4):
                land = refs[n + w].at[j]
                out.append((_slab(refs[w], axes[w], r, cl, j, 1 - c), land, land, (x, y, 1 - c)))
        return out
    return geometry


def _after_all(name, token, *arrays):
    def body(*refs):
        refs[-1][...] = jnp.zeros_like(refs[-1])

    return pl.pallas_call(
        body, name=name, in_specs=[ANY] * (1 + len(arrays)), out_specs=pl.BlockSpec(memory_space=pltpu.VMEM),
        out_shape=TOKEN,
    )(token, *arrays)


def _swap_geometry(bufs):
    x, y, c, _ = _me()
    return [(b.at[c], b.at[c], b.at[1 - c], (x, y, 1 - c)) for b in bufs]


def _pair_exchange(fulls, axes, shapes, tag):
    n = len(fulls)

    def body(*refs):
        src, dst = refs[:n], refs[n:2 * n]
        send_sems, recv_sems = refs[2 * n:]
        x, y, c, _ = _me()
        sibling = (x, y, 1 - c)
        cps = []
        for w in range(n):
            r, cl = shapes[w]
            for j in range(4):
                cp = _remote(_slab(src[w], axes[w], r, cl, j, 1 - c), dst[w].at[j],
                             send_sems.at[4 * w + j], recv_sems.at[4 * w + j], sibling)
                cp.start()
                cps.append(cp)
        for cp in cps:
            cp.wait()

    out_shape = [jax.ShapeDtypeStruct((4, r // 2, cl), f.dtype) for (r, cl), f in zip(shapes, fulls)]
    return pl.pallas_call(
        body, name="reduce_pair_exchange_" + tag, in_specs=[ANY] * n, out_specs=[ANY] * n, out_shape=out_shape,
        scratch_shapes=[pltpu.SemaphoreType.DMA((4 * n,)), pltpu.SemaphoreType.DMA((4 * n,))],
    )(*fulls)


def _chip_start(parts, tag):
    n = len(parts)

    def body(*refs):
        src, land = refs[2 * n:3 * n], refs[3 * n:4 * n]
        send_sems, recv_sems, token = refs[4 * n:]
        x, y, c, chips = _me()
        k = 2 * x + y
        for w in range(n):
            for j, (px, py) in enumerate(chips):
                _remote(src[w].at[2 * px + py], land[w].at[k], send_sems.at[3 * w + j], recv_sems.at[3 * w + j],
                        (px, py, c)).start()
        token[...] = jnp.zeros_like(token)

    lands = [lax.empty(p.shape, p.dtype) for p in parts]
    sem = pltpu.SemaphoreType.DMA((3 * n,))
    outs = pl.pallas_call(
        body, name="reduce_ici_start_" + tag, in_specs=[HBM] * (2 * n),
        out_specs=[HBM] * (2 * n) + [SEM, SEM, pl.BlockSpec(memory_space=pltpu.VMEM)],
        out_shape=[pltpu.HBM(p.shape, p.dtype) for p in parts + lands] + [sem, sem, TOKEN],
        input_output_aliases={i: i for i in range(2 * n)}, compiler_params=SPLIT,
    )(*[_in_hbm(v) for v in parts + lands])
    return outs[:n], outs[n:2 * n], outs[2 * n], outs[2 * n + 1], outs[-1]


def _chip_wait(parts, lands, send_sems, recv_sems, after, tag):
    n = len(parts)

    def body(*refs):
        send, recv = refs[2 * n], refs[2 * n + 1]
        src, land = refs[2 * n + 3:3 * n + 3], refs[3 * n + 3:]
        x, y, c, chips = _me()
        for w in range(n):
            for j, (px, py) in enumerate(chips):
                got = land[w].at[2 * px + py]
                _remote(got, got, send.at[3 * w + j], recv.at[3 * w + j], (px, py, c)).wait_recv()
        for w in range(n):
            for j, (px, py) in enumerate(chips):
                sent = src[w].at[2 * px + py]
                _remote(sent, sent, send.at[3 * w + j], recv.at[3 * w + j], (px, py, c)).wait_send()

    outs = pl.pallas_call(
        body, name="reduce_ici_wait_" + tag, in_specs=[HBM] * (2 * n) + [SEM, SEM, ANY], out_specs=[HBM] * (2 * n),
        out_shape=[pltpu.HBM(p.shape, p.dtype) for p in parts + lands],
        input_output_aliases={i: i for i in range(2 * n)}, compiler_params=SPLIT,
    )(*parts, *lands, send_sems, recv_sems, after)
    chip = 2 * lax.axis_index("x") + lax.axis_index("y")
    return [lax.dynamic_update_slice(s, lax.dynamic_index_in_dim(p, chip, 0, keepdims=True), (chip, 0, 0))
            for p, s in zip(outs[:n], outs[n:])]


def _half_swap(halves, tag):
    n = len(halves)
    core = lax.axis_index("c")
    bufs = [lax.dynamic_update_slice(lax.empty((2,) + h.shape, h.dtype), h[None], (core, 0, 0)) for h in halves]

    def body(*refs):
        dst = refs[n:2 * n]
        send_sems, recv_sems = refs[2 * n:]
        x, y, c, _ = _me()
        sibling = (x, y, 1 - c)
        cps = []
        for w in range(n):
            cp = _remote(dst[w].at[c], dst[w].at[c], send_sems.at[w], recv_sems.at[w], sibling)
            cp.start()
            cps.append(cp)
        for w in range(n):
            other = dst[w].at[1 - c]
            _remote(other, other, send_sems.at[w], recv_sems.at[w], sibling).wait_recv()
        for cp in cps:
            cp.wait_send()

    outs = pl.pallas_call(
        body, name="reduce_half_swap_" + tag, in_specs=[ANY] * n, out_specs=[ANY] * n,
        out_shape=[jax.ShapeDtypeStruct(b.shape, b.dtype) for b in bufs],
        input_output_aliases={w: w for w in range(n)},
        scratch_shapes=[pltpu.SemaphoreType.DMA((n,)), pltpu.SemaphoreType.DMA((n,))],
    )(*bufs)
    return [o.reshape(2 * o.shape[1], o.shape[2]) for o in outs]


def _add_parts(full, axis, rows, sib, name):
    _, r, c = sib.shape
    tr, tc = _tile(r, 1024, 16), _tile(c, 2048)
    nb = r // tr
    core = jnp.reshape(lax.axis_index("c"), (1,)).astype(jnp.int32)

    def body(c_ref, a_ref, b_ref, o_ref):
        o_ref[0] = (a_ref[...].astype(F32) + b_ref[0].astype(F32)).astype(BF16)

    blk = pl.BlockSpec((1, tr, tc), lambda j, i, l, cr: (j, i, l))
    return pl.pallas_call(
        body, name=name,
        grid_spec=pltpu.PrefetchScalarGridSpec(
            num_scalar_prefetch=1, grid=(4, nb, c // tc),
            in_specs=[pl.BlockSpec((tr, tc), lambda j, i, l, cr: ((_slot(axis, j) * 2 + cr[0]) * nb + i, l)), blk],
            out_specs=blk),
        out_shape=jax.ShapeDtypeStruct(sib.shape, BF16),
        compiler_params=_cparams(("parallel", "parallel", "parallel")),
    )(core, full, sib)


def _sum_slots(a, name):
    _, r, c = a.shape
    tr, tc = _tile(r, 512, 8), _tile(c, 2048)

    def body(a_ref, o_ref):
        v = a_ref[...].astype(F32)
        o_ref[...] = ((v[0] + v[1]) + v[2]) + v[3]

    return pl.pallas_call(
        body, name=name, grid=(r // tr, c // tc),
        in_specs=[pl.BlockSpec((4, tr, tc), lambda i, l: (0, i, l))],
        out_specs=pl.BlockSpec((tr, tc), lambda i, l: (i, l)),
        out_shape=jax.ShapeDtypeStruct((r, c), F32),
        compiler_params=_cparams(("parallel", "parallel")),
    )(a)


class _Reducer:
    def __init__(self, spec):
        self.spec = spec
        self.paired = {}
        self.pending = []

    def pair(self, name, full):
        ax, shp = self.spec[name]
        land = lax.empty((4, shp[0] // 2, shp[1]), full.dtype)
        arrays, sems, token = _split_start("reduce_pair_start_" + name, [full, land], _pair_geometry([ax], [shp]), 4)
        self.paired[name] = (arrays, sems)
        return token

    def ship(self, tag, names, after):
        parts = []
        for n in names:
            ax, shp = self.spec[n]
            arrays, sems = self.paired.pop(n)
            full, sib = _split_wait("reduce_pair_wait_" + n, arrays, sems, after, _pair_geometry([ax], [shp]))
            parts.append(_add_parts(full, ax, shp[0], sib, name=f"reduce_add_{n}"))
        parts, lands, send, recv, token = _chip_start(parts, tag)
        self.pending.append((tag, names, parts, lands, send, recv))
        return token

    def start(self, tag, grads):
        names = list(grads)
        fulls, axes = [grads[n] for n in names], [self.spec[n][0] for n in names]
        shapes = [self.spec[n][1] for n in names]
        from_sibling = _pair_exchange(fulls, axes, shapes, tag)
        parts = [_add_parts(f, a, r, s, name=f"reduce_add_{n}")
                 for n, f, a, (r, cl), s in zip(names, fulls, axes, shapes, from_sibling)]
        parts, lands, send, recv, token = _chip_start(parts, tag)
        self.pending.append((tag, names, parts, lands, send, recv))
        return token

    def finish(self, after, tags):
        out = {}
        for tag, names, parts, lands, send, recv in [p for p in self.pending if p[0] in tags]:
            slots = _chip_wait(parts, lands, send, recv, after, tag)
            halves = [_sum_slots(s, name=f"reduce_sum_{n}") for n, s in zip(names, slots)]
            out.update(zip(names, _half_swap(halves, tag)))
        return out

    def finish_start(self, after, tag):
        (_, names, parts, lands, send, recv), = [p for p in self.pending if p[0] == tag]
        slots = _chip_wait(parts, lands, send, recv, after, tag)
        halves = [_sum_slots(s, name=f"reduce_sum_{n}") for n, s in zip(names, slots)]
        core = lax.axis_index("c")
        bufs = [lax.dynamic_update_slice(lax.empty((2,) + h.shape, h.dtype), h[None], (core, 0, 0)) for h in halves]
        bufs, sems, _ = _split_start("reduce_half_swap_start_" + tag, bufs, _swap_geometry, len(bufs))
        return tag, names, bufs, sems

    def swap_wait(self, started, after):
        tag, names, bufs, sems = started
        outs = _split_wait("reduce_half_swap_wait_" + tag, bufs, sems, after, _swap_geometry)
        return dict(zip(names, [o.reshape(2 * o.shape[1], o.shape[2]) for o in outs]))


def _allreduce_small(pack, after):
    rows = pack.shape[0]

    def body(p_ref, _, o_ref, slots, send_sems, recv_sems):
        x, y, c, _ = _me()
        me = 4 * x + 2 * y + c
        slots[me] = p_ref[...]
        cps = []
        for r in range(1, 8):
            peer = (x ^ (r >> 2), y ^ ((r >> 1) & 1), c ^ (r & 1))
            cp = _remote(p_ref, slots.at[me], send_sems.at[r - 1], recv_sems.at[r - 1], peer)
            cp.start()
            cps.append(cp)
        for r in range(1, 8):
            frm = me ^ r
            _remote(slots.at[frm], slots.at[frm], send_sems.at[r - 1], recv_sems.at[r - 1], (x, y, c)).wait_recv()
        for cp in cps:
            cp.wait_send()
        acc = slots[0]
        for s in range(1, 8):
            acc = acc + slots[s]
        o_ref[...] = acc

    vm = pl.BlockSpec(memory_space=pltpu.VMEM)
    return pl.pallas_call(
        body, name="allreduce_small", in_specs=[vm, ANY], out_specs=vm,
        out_shape=jax.ShapeDtypeStruct(pack.shape, F32),
        scratch_shapes=[pltpu.VMEM((8, rows, HEAD_DIM), F32), pltpu.SemaphoreType.DMA((7,)),
                        pltpu.SemaphoreType.DMA((7,))],
    )(pack, after)


_ROWS = ["norm_mix", "norm_ffn", "mem_norm", "fox_q_norm", "fox_k_norm", "gdn_out_norm", "mem_q_norm",
         "mem_k_norm", "fox_f_bias", "gdn_a_log", "gdn_dt_bias"]


def _pack_rows(vals):
    out = []
    for name in _ROWS:
        v = vals[name].reshape(-1)
        n = -(-v.shape[0] // HEAD_DIM) * HEAD_DIM
        out.append(jnp.pad(v, (0, n - v.shape[0])).reshape(-1, HEAD_DIM))
    return jnp.concatenate(out, axis=0)


def _unpack_rows(pack, like):
    out, r = {}, 0
    for name in _ROWS:
        n = like[name].shape[-1]
        nr = -(-n // HEAD_DIM)
        out[name] = pack[r:r + nr].reshape(1, -1)[:, :n]
        r += nr
    return out, r


def kernel(x, mem, norm_mix, w_in, fox_f_bias, fox_q_norm, fox_k_norm, gdn_conv, gdn_a_log, gdn_dt_bias, gdn_out_norm, mem_norm, w_mem_kv, mem_q_norm, mem_k_norm, w_out, norm_ffn, w_gate_up, w_down, loss_target, m_norm_mix, m_w_in, m_fox_f_bias, m_fox_q_norm, m_fox_k_norm, m_gdn_conv, m_gdn_a_log, m_gdn_dt_bias, m_gdn_out_norm, m_mem_norm, m_w_mem_kv, m_mem_q_norm, m_mem_k_norm, m_w_out, m_norm_ffn, m_w_gate_up, m_w_down, v_norm_mix, v_w_in, v_fox_f_bias, v_fox_q_norm, v_fox_k_norm, v_gdn_conv, v_gdn_a_log, v_gdn_dt_bias, v_gdn_out_norm, v_mem_norm, v_w_mem_kv, v_mem_q_norm, v_mem_k_norm, v_w_out, v_norm_ffn, v_w_gate_up, v_w_down):
    a = dict(locals())
    d = x.shape[-1]
    lay = _Layout(d)
    chip = 2 * lax.axis_index("x") + lax.axis_index("y")
    small = {n: a[n] for n in _ROWS}
    big = ["w_in", "w_mem_kv", "w_out", "w_gate_up", "w_down"]
    axes = [0, 0, 0, 1, 0]

    conv_cols = gdn_conv.shape[-1]
    conv_n = CONV_WIDTH * conv_cols
    conv_rows = -(-conv_n // HEAD_DIM)
    conv_blk = jnp.pad(gdn_conv.reshape(-1), (0, 32 * HEAD_DIM - conv_n)).reshape(32, HEAD_DIM)
    axis_of = dict(zip(big, axes), conv=0, w_in_a=0, w_in_b=0)
    shape_of = {n: a[n].shape[1:] for n in big[1:]}
    shape_of.update(w_in_a=(w_in.shape[1], lay.cols_a), w_in_b=(w_in.shape[1], lay.cols_b), conv=conv_blk.shape)
    placed = {"w_in_a": _cast_place(w_in[0], 0, "cast_w_in_a", lambda v: lay.regroup(v)[:, :lay.cols_a], lay.cols_a),
              "conv": lax.dynamic_update_slice(lax.empty((4 * 32, HEAD_DIM), F32), conv_blk, (chip * 32, 0))}
    grouped = {"in_a": ["w_in_a"], "in_b": ["w_in_b"], "mixer": ["w_mem_kv", "conv"], "out": ["w_out"],
               "gate_up": ["w_gate_up"], "down": ["w_down"]}
    inflight = {}

    def start(tags, name):
        names = [n for t in tags for n in grouped[t]]
        bufs, sems, token = _gather_start([placed[n] for n in names], [axis_of[n] for n in names],
                                          [shape_of[n] for n in names],
                                          [[names.index(n) for n in grouped[t]] for t in tags], name)
        for t, pair in zip(tags, sems):
            inflight[t] = ([bufs[names.index(n)] for n in grouped[t]], pair)
        return token

    first = start(["in_a"], "gather_ici_start_in")
    placed["w_in_b"] = _cast_place(w_in[0], 0, "cast_w_in_b", lambda v: lay.regroup(v)[:, lay.cols_a:], lay.cols_b,
                                   after=first)
    placed.update({n: _cast_place(a[n][0], axis_of[n], "cast_" + n, after=first) for n in big[1:]})
    all_started = start(["in_b", "mixer", "out", "gate_up", "down"], "gather_ici_start_rest")
    all_started = _after_all("moments_ready", all_started, m_w_in[0], v_w_in[0])

    forwarding = {}

    def prefetch(tag, after):
        bufs, sem_pair = inflight.pop(tag)
        ax, shp = [axis_of[n] for n in grouped[tag]], [shape_of[n] for n in grouped[tag]]
        got = _gather_wait(bufs, ax, shp, sem_pair, all_started if tag == "in_a" else after,
                           "gather_ici_wait_" + tag)
        geometry = _forward_geometry(ax, shp)
        got, sems, _ = _split_start("gather_forward_start_" + tag, got, geometry, 3 * len(got))
        forwarding[tag] = (got, sems, geometry)

    def weights(tag, after):
        got, sems, geometry = forwarding.pop(tag)
        got = _split_wait("gather_forward_wait_" + tag, got, sems, after, geometry)
        if tag != "mixer":
            return got
        taps = got[1].reshape(4, 32 * HEAD_DIM)[:, :conv_n].reshape(4, CONV_WIDTH, conv_cols)
        return got[0], jnp.transpose(taps, (1, 0, 2)).reshape(CONV_WIDTH, 4 * conv_cols)

    sp = dict(small)
    reducer = _Reducer({n: (axis_of[n], shape_of[n]) for n in big[1:] + ["w_in_a", "w_in_b"]})
    loss_blk, dx, g = _local_step(x[0], mem[0], loss_target[0], prefetch, weights, reducer, sp)

    gsmall = {n: g[n] for n in _ROWS}
    pack = jnp.concatenate([_pack_rows(gsmall), g["gdn_conv"].reshape(-1, HEAD_DIM), loss_blk], axis=0)
    pack = jnp.pad(pack, ((0, -pack.shape[0] % 8), (0, 0)))
    out = {"grad_x": dx[None]}

    def adamw_shards(reduced):
        if "w_in_a" in reduced:
            reduced = {"w_in": (reduced["w_in_a"], reduced["w_in_b"])}
        for n, gsh in reduced.items():
            join = (lambda ga, gb: lay.ungroup(jnp.concatenate([ga, gb], axis=1))) if n == "w_in" else None
            res = _adamw(a[n][0], gsh, a["m_" + n][0], a["v_" + n][0], g_fn=join, name="adamw_" + n)
            for pre, r in zip(["grad_", "delta_", "new_m_", "new_v_"], res):
                out[pre + n] = r[None]
        return res[0]

    mix_swap = reducer.finish_start(dx, "mix")
    ffn_swap = reducer.finish_start(mix_swap[2][0], "ffn")
    done = adamw_shards(reducer.swap_wait(mix_swap, ffn_swap[2][0]))
    done = adamw_shards(reducer.swap_wait(ffn_swap, done))
    tot = _allreduce_small(pack, done)
    gs, r0 = _unpack_rows(tot, small)
    conv_g = tot[r0:r0 + CONV_WIDTH * 4 * conv_cols // HEAD_DIM].reshape(CONV_WIDTH, 4 * conv_cols)
    gs_conv = lax.dynamic_slice_in_dim(conv_g, chip * conv_cols, conv_cols, axis=1)
    out["loss"] = tot[r0 + CONV_WIDTH * 4 * conv_cols // HEAD_DIM, 0]
    adamw_shards(reducer.finish(tot, ("in",)))
    conv_pad = lambda v: jnp.pad(v.reshape(-1), (0, conv_rows * HEAD_DIM - conv_n)).reshape(conv_rows, HEAD_DIM)
    packs = []
    for src, cv in [(small, gdn_conv), (gs, gs_conv), ({n: a["m_" + n] for n in _ROWS}, m_gdn_conv),
                    ({n: a["v_" + n] for n in _ROWS}, v_gdn_conv)]:
        packs.append(jnp.concatenate([_pack_rows(src), conv_pad(cv)], axis=0))
    res = _adamw(*packs, name="adamw_small")
    for pre, r in zip(["grad_", "delta_", "new_m_", "new_v_"], res):
        vals, r1 = _unpack_rows(r, small)
        for n in _ROWS:
            out[pre + n] = vals[n]
        out[pre + "gdn_conv"] = r[r1:r1 + conv_rows].reshape(-1)[:conv_n].reshape(gdn_conv.shape)
    names = ["norm_mix", "w_in", "fox_f_bias", "fox_q_norm", "fox_k_norm", "gdn_conv", "gdn_a_log", "gdn_dt_bias",
             "gdn_out_norm", "mem_norm", "w_mem_kv", "mem_q_norm", "mem_k_norm", "w_out", "norm_ffn", "w_gate_up",
             "w_down"]
    return (out["loss"], out["grad_x"], *[out[p + n] for p in ["grad_", "delta_", "new_m_", "new_v_"] for n in names])
```

```python
import functools
import math

import jax
import jax.numpy as jnp
from jax import lax
from jax.experimental import pallas as pl
from jax.experimental.pallas import tpu as pltpu

F32, BF16 = jnp.float32, jnp.bfloat16
HEAD_DIM = 128
CHUNK = 64
N_MEM_HEADS = 4
CONV_WIDTH = 4
NORM_EPS = 1e-6
ADAM_LR, ADAM_B1, ADAM_B2, ADAM_EPS, ADAM_WD, ADAM_STEP = 0.001, 0.9, 0.999, 1e-08, 0.01, 10
VMEM_LIMIT = 48 * 1024 * 1024
NEG = -1e30
MESH = pl.DeviceIdType.MESH


def _cparams(sem=None, **kw):
    if sem is not None:
        kw["dimension_semantics"] = sem
    return pltpu.CompilerParams(vmem_limit_bytes=VMEM_LIMIT, **kw)


def _tile(n, target, mult=128):
    best = None
    d = mult
    while d <= min(n, target):
        if n % d == 0:
            best = d
        d += mult
    return best if best is not None else n


def _dot(a, b, dims, hi):
    if a.ndim == 3:
        dn = (((dims[0][0] + 1,), (dims[1][0] + 1,)), ((0,), (0,)))
    else:
        dn = (dims, ((), ()))
    if hi is not None:
        return lax.dot_general(a, b, dn, precision=hi, preferred_element_type=F32)
    return lax.dot_general(a.astype(BF16), b.astype(BF16), dn, preferred_element_type=F32)


def _make_dots(hi, cotangent=None):
    @jax.custom_vjp
    def nn(a, b):
        return _dot(a, b, ((1,), (0,)), hi)

    @jax.custom_vjp
    def nt(a, b):
        return _dot(a, b, ((1,), (1,)), hi)

    @jax.custom_vjp
    def tn(a, b):
        return _dot(a, b, ((0,), (0,)), hi)

    bnn, bnt, btn = cotangent or (nn, nt, tn)
    nn.defvjp(lambda a, b: (nn(a, b), (a, b)), lambda r, g: (bnt(g, r[1]), btn(r[0], g)))
    nt.defvjp(lambda a, b: (nt(a, b), (a, b)), lambda r, g: (bnn(g, r[1]), btn(g, r[0])))
    tn.defvjp(lambda a, b: (tn(a, b), (a, b)), lambda r, g: (bnt(r[1], g), bnn(r[0], g)))
    return nn, nt, tn


_nn, _nt, _tn = _make_dots(None)
_nn_hi, _nt_hi, _tn_hi = _make_dots(lax.Precision.HIGHEST)
_nn_x3, _nt_x3, _tn_x3 = _make_dots(lax.Precision.HIGH, (_nn, _nt, _tn))


def _sigmoid(x):
    return jax.nn.sigmoid(x)


@jax.custom_vjp
def _softplus(x):
    return jnp.maximum(x, 0.0) + jnp.log(1.0 + jnp.exp(-jnp.abs(x)))


_softplus.defvjp(lambda x: (_softplus(x), x), lambda x, g: (g * _sigmoid(x),))


def _silu(x):
    return x * _sigmoid(x)


def _rms_fn(x, gain, z=None):
    y = x * lax.rsqrt(jnp.mean(x * x, axis=-1, keepdims=True) + NORM_EPS) * gain
    if z is not None:
        y = y * _silu(z)
    return y


def _mm(a, b, *, ta=False, tb=False, out_dtype=F32, res=None, stack=None, after=None, name):
    a2, b2 = a.shape[-2:], b.shape[-2:]
    ns = b.shape[0] if stack else 1
    m = a2[1] if ta else a2[0]
    k = a2[0] if ta else a2[1]
    n = b2[0] if tb else b2[1]
    assert k == (b2[1] if tb else b2[0])
    tm, tn, tk = _mm_tiles(m, n, k, ns if stack == "sum" else 1, a.dtype.itemsize, b.dtype.itemsize,
                           jnp.dtype(out_dtype).itemsize, res is not None)
    nk = k // tk
    single = nk == 1 and stack != "sum"
    dims = ((0 if ta else 1,), (1 if tb else 0,))
    if stack == "sum":
        order = lambda g0, g1, g2, g3: (g2, g0, g1, g3)
        grid = (m // tm, n // tn, ns, nk)
    else:
        order = lambda g0, g1, g2, g3: (g0, g1, g2, g3)
        grid = (ns, m // tm, n // tn, nk)

    def body(*refs):
        if after is not None:
            refs = refs[:2 + (res is not None)] + refs[3 + (res is not None):]
        if single:
            a_ref, b_ref = refs[:2]
            r = lax.dot_general(a_ref[...].astype(BF16), b_ref[...].astype(BF16), (dims, ((), ())),
                                preferred_element_type=F32)
            if res is not None:
                r = r + refs[2][...]
            refs[-1][...] = r.astype(out_dtype)
            return
        if res is None:
            a_ref, b_ref, o_ref, acc = refs
        else:
            a_ref, b_ref, r_ref, o_ref, acc = refs
        s, _, _, kk = order(*[pl.program_id(d) for d in range(4)])
        first = kk == 0
        last = kk == nk - 1
        if stack == "sum":
            first, last = first & (s == 0), last & (s == ns - 1)

        @pl.when(first)
        def _():
            acc[...] = jnp.zeros_like(acc)

        acc[...] += lax.dot_general(a_ref[...].astype(BF16), b_ref[...].astype(BF16), (dims, ((), ())),
                                    preferred_element_type=F32)

        @pl.when(last)
        def _():
            r = acc[...]
            if res is not None:
                r = r + r_ref[...]
            o_ref[...] = r.astype(out_dtype)

    def spec(shape, idx, stacked):
        if stacked:
            return pl.BlockSpec((None,) + shape, lambda *g: (order(*g)[0],) + idx(*order(*g)))
        return pl.BlockSpec(shape, lambda *g: idx(*order(*g)))

    a_spec = (spec((tk, tm), lambda s, i, j, kk: (kk, i), stack == "sum") if ta
              else spec((tm, tk), lambda s, i, j, kk: (i, kk), stack == "sum"))
    b_spec = (spec((tn, tk), lambda s, i, j, kk: (j, kk), bool(stack)) if tb
              else spec((tk, tn), lambda s, i, j, kk: (kk, j), bool(stack)))
    o_spec = spec((tm, tn), lambda s, i, j, kk: (i, j), stack == "out")
    ins, specs = [a, b], [a_spec, b_spec]
    if res is not None:
        ins.append(res)
        specs.append(o_spec)
    if after is not None:
        ins.append(after)
        specs.append(pl.BlockSpec(after.shape, lambda *g: (0,) * after.ndim))
    sem = (("parallel", "parallel", "arbitrary", "arbitrary") if stack == "sum"
           else ("parallel", "parallel", "parallel", "arbitrary"))
    return pl.pallas_call(
        body, name=name, grid=grid, in_specs=specs, out_specs=o_spec,
        out_shape=jax.ShapeDtypeStruct(((ns,) if stack == "out" else ()) + (m, n), out_dtype),
        scratch_shapes=[] if single else [pltpu.VMEM((tm, tn), F32)],
        compiler_params=_cparams(sem),
    )(*ins)


MM_VMEM_BUDGET = 40 * 1024 * 1024
MXU_WIDTH = 256


def _mm_tiles(m, n, k, ns, sa, sb, so, has_res):
    def divs(x, mult, cap):
        out = [d for d in range(mult, min(x, cap) + 1, mult) if x % d == 0]
        return out or [x]

    best = None
    for tk in divs(k, 128, 8192):
        nk = (k // tk) * ns
        for tm in divs(m, 8, 2048):
            for tn in divs(n, 128, 2048):
                vmem = 2 * (tm * tk * sa + tk * tn * sb + tm * tn * so) + (2 * tm * tn * 4 if has_res else 0)
                vmem += tm * tn * 4 if nk > 1 else 0
                if vmem > MM_VMEM_BUDGET:
                    continue
                steps = (m // tm) * (n // tn) * nk
                traffic = (m // tm) * k * n * sb * ns + (n // tn if nk > 1 else 1) * m * k * sa * ns
                cost = steps * 0.4e-6 + traffic / 2.5e12 + (nk * m * n * 8 / 6e12 if nk > 1 else 0)
                cost += 2.0 * m * n * k * ns / 7e14 * (-(-tn // MXU_WIDTH) * MXU_WIDTH / tn)
                if best is None or cost < best[0]:
                    best = (cost, tm, tn, tk)
    return best[1:]


def _norm_fwd(x, xoff, gain, ncol, w, out_dtype, *, z=None, zoff=0, into=None, into_off=0, name):
    t = x.shape[0]
    tr = _tile(t, max(256, (1 << 18) // w), 8)

    def body(*refs):
        x_ref, g_ref, o_ref = refs[0], refs[1], refs[-1]
        y = _rms_fn(x_ref[...], g_ref[...]) if z is None else _rms_fn(x_ref[...], g_ref[...], refs[2][...])
        o_ref[...] = y.astype(out_dtype)

    ins = [x, gain]
    specs = [pl.BlockSpec((tr, w), lambda j, r: (r, xoff + j)), pl.BlockSpec((1, w), lambda j, r: (0, 0))]
    if z is not None:
        ins.append(z)
        specs.append(pl.BlockSpec((tr, w), lambda j, r: (r, zoff + j)))
    aliases = {}
    if into is not None:
        aliases = {len(ins): 0}
        ins.append(into)
        specs.append(pl.BlockSpec(memory_space=pl.ANY))
    return pl.pallas_call(
        body, name=name, grid=(ncol, t // tr), in_specs=specs,
        out_specs=pl.BlockSpec((tr, w), lambda j, r: (r, into_off + j)),
        out_shape=jax.ShapeDtypeStruct((t, ncol * w) if into is None else into.shape, out_dtype),
        input_output_aliases=aliases, compiler_params=_cparams(("parallel", "parallel")),
    )(*ins)


def _norm_bwd(x, xoff, gain, dy, dyoff, ncol, w, *, z=None, zoff=0, res=None, name):
    t = x.shape[0]
    tr = _tile(t, max(256, (1 << 18) // w), 8)

    def body(*refs):
        it = iter(refs)
        x_ref, g_ref = next(it), next(it)
        z_ref = next(it) if z is not None else None
        dy_ref = next(it)
        r_ref = next(it) if res is not None else None
        dx_ref = next(it)
        dz_ref = next(it) if z is not None else None
        dg_ref = next(it)

        @pl.when((pl.program_id(0) == 0) & (pl.program_id(1) == 0))
        def _():
            dg_ref[...] = jnp.zeros_like(dg_ref)

        args = (x_ref[...], g_ref[...]) + ((z_ref[...],) if z is not None else ())
        _, vjp = jax.vjp(_rms_fn, *args)
        grads = vjp(dy_ref[...].astype(F32))
        dx = grads[0]
        if res is not None:
            dx = dx + r_ref[...]
        dx_ref[...] = dx
        if z is not None:
            dz_ref[...] = grads[2]
        dg_ref[...] += grads[1]

    ins = [x, gain]
    specs = [pl.BlockSpec((tr, w), lambda j, r: (r, xoff + j)), pl.BlockSpec((1, w), lambda j, r: (0, 0))]
    if z is not None:
        ins.append(z)
        specs.append(pl.BlockSpec((tr, w), lambda j, r: (r, zoff + j)))
    ins.append(dy)
    specs.append(pl.BlockSpec((tr, w), lambda j, r: (r, dyoff + j)))
    blk = pl.BlockSpec((tr, w), lambda j, r: (r, j))
    if res is not None:
        ins.append(res)
        specs.append(blk)
    full = jax.ShapeDtypeStruct((t, ncol * w), F32)
    out_shape, out_specs = [full], [blk]
    if z is not None:
        out_shape.append(full)
        out_specs.append(blk)
    out_shape.append(jax.ShapeDtypeStruct((1, w), F32))
    out_specs.append(pl.BlockSpec((1, w), lambda j, r: (0, 0)))
    return pl.pallas_call(
        body, name=name, grid=(ncol, t // tr), in_specs=specs, out_specs=out_specs, out_shape=out_shape,
        compiler_params=_cparams(("arbitrary", "arbitrary")),
    )(*ins)


def _small_fn(x, pa, pb, nf, ng):
    lane = lax.broadcasted_iota(jnp.int32, x.shape, 1)
    zz = x + pb
    logf = -_softplus(-zz)
    g = -jnp.exp(pa) * _softplus(zz)
    beta = _sigmoid(x)
    return jnp.where(lane < nf, logf, jnp.where(lane < nf + ng, g, beta))


def _tri(n, upper):
    r = lax.broadcasted_iota(jnp.int32, (n, n), 0)
    c = lax.broadcasted_iota(jnp.int32, (n, n), 1)
    return jnp.where((c >= r) if upper else (c <= r), 1.0, 0.0).astype(F32)


def _small_fwd(p, off, pa, pb, nf, ng):
    t = p.shape[0]
    blk = HEAD_DIM
    nb = t // blk

    def body(x_ref, pa_ref, pb_ref, v_ref, c_ref):
        v_ref[...] = _small_fn(x_ref[...], pa_ref[...], pb_ref[...], nf, ng)
        tri = _tri(blk, False)

        carry = jnp.zeros((1, HEAD_DIM), F32)
        for i in range(nb):
            rows = slice(i * blk, (i + 1) * blk)
            c = _nn_hi(tri, v_ref[rows, :]) + carry
            c_ref[rows, :] = c
            carry = c[blk - 1:blk, :]

    row = pl.BlockSpec((1, HEAD_DIM), lambda i: (0, 0))
    out = pl.BlockSpec((t, HEAD_DIM), lambda i: (0, 0))
    return pl.pallas_call(
        body, name="small_fwd", grid=(1,),
        in_specs=[pl.BlockSpec((t, HEAD_DIM), lambda i: (0, off)), row, row], out_specs=[out, out],
        out_shape=[jax.ShapeDtypeStruct((t, HEAD_DIM), F32)] * 2,
        compiler_params=_cparams(("arbitrary",)),
    )(p, pa, pb)


def _small_bwd(p, off, pa, pb, dvals, dcsum, nf, ng):
    t = p.shape[0]
    blk = HEAD_DIM
    nb = t // blk

    def body(x_ref, pa_ref, pb_ref, dv_ref, dc_ref, dx_ref, dpa_ref, dpb_ref, tot_ref):
        tri = _tri(blk, True)

        carry = jnp.zeros((1, HEAD_DIM), F32)
        for i in reversed(range(nb)):
            rows = slice(i * blk, (i + 1) * blk)
            c = _nn_hi(tri, dc_ref[rows, :]) + carry
            tot_ref[rows, :] = c + dv_ref[rows, :]
            carry = c[0:1, :]
        f = functools.partial(_small_fn, nf=nf, ng=ng)
        _, vjp = jax.vjp(f, x_ref[...], pa_ref[...], pb_ref[...])
        dx, dpa, dpb = vjp(tot_ref[...])
        dx_ref[...] = dx
        dpa_ref[...] = dpa
        dpb_ref[...] = dpb

    row = pl.BlockSpec((1, HEAD_DIM), lambda i: (0, 0))
    full = pl.BlockSpec((t, HEAD_DIM), lambda i: (0, 0))
    return pl.pallas_call(
        body, name="small_bwd", grid=(1,),
        in_specs=[pl.BlockSpec((t, HEAD_DIM), lambda i: (0, off)), row, row, full, full],
        out_specs=[full, row, row],
        out_shape=[jax.ShapeDtypeStruct((t, HEAD_DIM), F32), jax.ShapeDtypeStruct((1, HEAD_DIM), F32),
                   jax.ShapeDtypeStruct((1, HEAD_DIM), F32)],
        scratch_shapes=[pltpu.VMEM((t, HEAD_DIM), F32)],
        compiler_params=_cparams(("arbitrary",)),
    )(p, pa, pb, dvals, dcsum)


def _fox_heads(nf, most):
    return next(h for h in range(most, 0, -1) if nf % h == 0)


def _fox_fwd(q, k, v, cc, cr, nf, tq, tk, d_mix):
    t = q.shape[0]
    scale = HEAD_DIM ** -0.5
    assert tq == tk

    vt = jnp.transpose(v.reshape(t // tk, tk, nf, HEAD_DIM), (2, 0, 3, 1))

    hp = _fox_heads(nf, 3)
    lanes = lambda h: slice(h * HEAD_DIM, (h + 1) * HEAD_DIM)

    def body(q_ref, k_ref, vt_ref, cc_ref, cr_ref, o_ref, lse_ref, mix_ref):
        i = pl.program_id(1)
        qs = [q_ref[:, lanes(h)] for h in range(hp)]
        cqs = [cr_ref[h, i] for h in range(hp)]
        ones = jnp.ones((8, tk), BF16)
        diff = lax.broadcasted_iota(jnp.int32, (tk, tq), 0) - lax.broadcasted_iota(jnp.int32, (tk, tq), 1)

        def scores(h, j):
            ks = pl.ds(pl.multiple_of(j * tk, tk), tk)
            return lax.dot_general(k_ref[ks, lanes(h)], qs[h], (((1,), (1,)), ((), ())),
                                   preferred_element_type=F32)

        def tile(h, j, m, l, acc, s, masked):
            ks = pl.ds(pl.multiple_of(j * tk, tk), tk)
            s = s * scale + cqs[h] - cc_ref[0, ks, h:h + 1]
            if masked:
                s = jnp.where(diff <= 0, s, NEG)
            m_new = jnp.maximum(m, jnp.max(s, axis=0, keepdims=True))
            pr = jnp.exp(s - m_new).astype(BF16)
            alpha = jnp.exp(m - m_new)
            l = alpha * l + jnp.dot(ones, pr, preferred_element_type=F32)[:1]
            acc = alpha * acc + jnp.dot(vt_ref[h, j], pr, preferred_element_type=F32)
            return m_new, l, acc

        def step(j, carry):
            nxt = [scores(h, j + 1) for h in range(hp)]
            return tuple(tile(h, j, *carry[h], False) + (nxt[h],) for h in range(hp))

        init = tuple((jnp.full((1, tq), NEG, F32), jnp.zeros((1, tq), F32), jnp.zeros((HEAD_DIM, tq), F32),
                      scores(h, 0)) for h in range(hp))
        carry = lax.fori_loop(0, i, step, init)
        for h in range(hp):
            m, l, acc = tile(h, i, *carry[h], True)
            o = jnp.transpose(acc / l)
            o_ref[:, lanes(h)] = o
            mix_ref[:, lanes(h)] = o.astype(BF16)
            lse_ref[h, 0] = m + jnp.log(l)

    w = hp * HEAD_DIM
    qblk = pl.BlockSpec((tq, w), lambda h, i: (i, h))
    return pl.pallas_call(
        body, name="fox_fwd", grid=(nf // hp, t // tq),
        in_specs=[qblk, pl.BlockSpec((t, w), lambda h, i: (0, h)),
                  pl.BlockSpec((hp, t // tk, HEAD_DIM, tk), lambda h, i: (h, 0, 0, 0)),
                  pl.BlockSpec((1, t, HEAD_DIM), lambda h, i: (h, 0, 0)),
                  pl.BlockSpec((hp, t // tk, 1, tk), lambda h, i: (h, 0, 0, 0))],
        out_specs=[qblk, pl.BlockSpec((hp, 1, 1, tq), lambda h, i: (h, i, 0, 0)), qblk],
        out_shape=[jax.ShapeDtypeStruct((t, nf * HEAD_DIM), F32), jax.ShapeDtypeStruct((nf, t // tq, 1, tq), F32),
                   jax.ShapeDtypeStruct((t, d_mix), BF16)],
        compiler_params=_cparams(("parallel", "parallel")),
    )(q, k, vt, cc, cr)


def _fox_bwd(q, k, v, cc, cr, o, lse, dmix, nf, tq, tk):
    t = q.shape[0]
    scale = HEAD_DIM ** -0.5
    assert tq == tk
    hp = _fox_heads(nf, 3)
    lanes = lambda h: slice(h * HEAD_DIM, (h + 1) * HEAD_DIM)
    kt = jnp.transpose(k.reshape(t // tk, tk, nf, HEAD_DIM), (2, 0, 3, 1))

    def body(q_ref, k_ref, kt_ref, v_ref, cc_ref, cr_ref, o_ref, lse_ref, do_ref,
             dq_ref, dk_ref, dv_ref, dcq_ref, dck_ref):
        i = pl.program_id(1)

        @pl.when(i == 0)
        def _():
            dk_ref[...] = jnp.zeros_like(dk_ref)
            dv_ref[...] = jnp.zeros_like(dv_ref)
            dck_ref[...] = jnp.zeros_like(dck_ref)

        diff = lax.broadcasted_iota(jnp.int32, (tk, tq), 0) - lax.broadcasted_iota(jnp.int32, (tk, tq), 1)
        lane = lax.broadcasted_iota(jnp.int32, (tk, HEAD_DIM), 1)
        qs = [q_ref[:, lanes(h)] for h in range(hp)]
        dos = [do_ref[:, lanes(h)] for h in range(hp)]
        do_b = [d.astype(BF16) for d in dos]
        cqs = [cr_ref[h, i] for h in range(hp)]
        lses = [lse_ref[h, 0] for h in range(hp)]
        deltas = [jnp.sum(jnp.transpose(dos[h] * o_ref[:, lanes(h)]), axis=0, keepdims=True) for h in range(hp)]

        def products(h, j):
            ks = pl.ds(pl.multiple_of(j * tk, tk), tk)
            nt = (((1,), (1,)), ((), ()))
            return (lax.dot_general(k_ref[ks, lanes(h)], qs[h], nt, preferred_element_type=F32),
                    lax.dot_general(v_ref[ks, lanes(h)], do_b[h], nt, preferred_element_type=F32))

        def tile(h, j, dqt, dcq, s, dp, masked):
            ks = pl.ds(pl.multiple_of(j * tk, tk), tk)
            pr = jnp.exp(s * scale + cqs[h] - cc_ref[0, ks, h:h + 1] - lses[h])
            if masked:
                pr = jnp.where(diff <= 0, pr, 0.0)
            ds = pr * (dp - deltas[h])
            ds_b = ds.astype(BF16)
            dqt = dqt + jnp.dot(kt_ref[h, j], ds_b, preferred_element_type=F32)
            dk_ref[ks, lanes(h)] += jnp.dot(ds_b, qs[h], preferred_element_type=F32) * scale
            dv_ref[ks, lanes(h)] += jnp.dot(pr.astype(BF16), do_b[h], preferred_element_type=F32)
            dck_ref[0, ks, :] -= jnp.where(lane == h, jnp.sum(ds, axis=1, keepdims=True), 0.0)
            return dqt, dcq + jnp.sum(ds, axis=0, keepdims=True)

        def step(j, carry):
            nxt = [products(h, j + 1) for h in range(hp)]
            return tuple(tile(h, j, *carry[h], False) + nxt[h] for h in range(hp))

        init = tuple((jnp.zeros((HEAD_DIM, tq), F32), jnp.zeros((1, tq), F32)) + products(h, 0) for h in range(hp))
        carry = lax.fori_loop(0, i, step, init)
        for h in range(hp):
            dqt, dcq = tile(h, i, *carry[h], True)
            dq_ref[:, lanes(h)] = jnp.transpose(dqt) * scale
            dcq_ref[h, 0] = dcq

    w = hp * HEAD_DIM
    head_all = pl.BlockSpec((t, w), lambda h, i: (0, h))
    qblk = pl.BlockSpec((tq, w), lambda h, i: (i, h))
    colv = pl.BlockSpec((1, t, HEAD_DIM), lambda h, i: (h, 0, 0))
    rows_all = pl.BlockSpec((hp, t // tk, 1, tk), lambda h, i: (h, 0, 0, 0))
    row_blk = pl.BlockSpec((hp, 1, 1, tq), lambda h, i: (h, i, 0, 0))
    wide = jax.ShapeDtypeStruct((t, nf * HEAD_DIM), F32)
    return pl.pallas_call(
        body, name="fox_bwd", grid=(nf // hp, t // tq),
        in_specs=[qblk, head_all, pl.BlockSpec((hp, t // tk, HEAD_DIM, tk), lambda h, i: (h, 0, 0, 0)), head_all,
                  colv, rows_all, qblk, row_blk, qblk],
        out_specs=[qblk, head_all, head_all, row_blk, colv],
        out_shape=[wide, wide, wide, jax.ShapeDtypeStruct((nf, t // tq, 1, tq), F32),
                   jax.ShapeDtypeStruct((nf // hp, t, HEAD_DIM), F32)],
        compiler_params=_cparams(("parallel", "arbitrary")),
    )(q, k, kt, v, cc, cr, o, lse, dmix)


def _mem_fn(mq, mk, mv, gq, gk):
    qn = _rms_fn(mq, gq)
    kn = _rms_fn(mk, gk)
    s = _nt(qn, kn) * (HEAD_DIM ** -0.5)
    e = jnp.exp(s - lax.stop_gradient(jnp.max(s, axis=1, keepdims=True)))
    pr = e / jnp.sum(e, axis=1, keepdims=True)
    return _nn(pr, mv)


def _mem_specs(t, m, tq, qoff):
    qblk = pl.BlockSpec((tq, HEAD_DIM), lambda h, i: (i, qoff + h))
    kblk = pl.BlockSpec((m, HEAD_DIM), lambda h, i: (0, h))
    vblk = pl.BlockSpec((m, HEAD_DIM), lambda h, i: (0, N_MEM_HEADS + h))
    row = pl.BlockSpec((1, HEAD_DIM), lambda h, i: (0, 0))
    return qblk, kblk, vblk, row


def _mem_fwd(p, qoff, mkv, gq, gk, tq, into, into_off):
    t, m = p.shape[0], mkv.shape[0]
    qblk, kblk, vblk, row = _mem_specs(t, m, tq, qoff)

    def body(q_ref, k_ref, v_ref, gq_ref, gk_ref, _, o_ref):
        o_ref[...] = _mem_fn(q_ref[...], k_ref[...], v_ref[...], gq_ref[...], gk_ref[...]).astype(BF16)

    return pl.pallas_call(
        body, name="mem_fwd", grid=(N_MEM_HEADS, t // tq),
        in_specs=[qblk, kblk, vblk, row, row, pl.BlockSpec(memory_space=pl.ANY)],
        out_specs=pl.BlockSpec((tq, HEAD_DIM), lambda h, i: (i, into_off + h)),
        out_shape=jax.ShapeDtypeStruct(into.shape, BF16), input_output_aliases={5: 0},
        compiler_params=_cparams(("parallel", "parallel")),
    )(p, mkv, mkv, gq, gk, into)


def _mem_bwd(p, qoff, mkv, gq, gk, dmix, dooff, tq):
    t, m = p.shape[0], mkv.shape[0]
    qblk, kblk, vblk, row = _mem_specs(t, m, tq, qoff)

    def body(q_ref, k_ref, v_ref, gq_ref, gk_ref, do_ref, dq_ref, dkv_k_ref, dkv_v_ref, dgq_ref, dgk_ref):
        h, i = pl.program_id(0), pl.program_id(1)

        @pl.when((h == 0) & (i == 0))
        def _():
            dgq_ref[...] = jnp.zeros_like(dgq_ref)
            dgk_ref[...] = jnp.zeros_like(dgk_ref)

        @pl.when(i == 0)
        def _():
            dkv_k_ref[...] = jnp.zeros_like(dkv_k_ref)
            dkv_v_ref[...] = jnp.zeros_like(dkv_v_ref)

        _, vjp = jax.vjp(_mem_fn, q_ref[...], k_ref[...], v_ref[...], gq_ref[...], gk_ref[...])
        dq, dk, dv, dgq, dgk = vjp(do_ref[...])
        dq_ref[...] = dq
        dkv_k_ref[...] += dk
        dkv_v_ref[...] += dv
        dgq_ref[...] += dgq
        dgk_ref[...] += dgk

    oblk = pl.BlockSpec((tq, HEAD_DIM), lambda h, i: (i, h))
    kout = pl.BlockSpec((m, HEAD_DIM), lambda h, i: (0, h))
    half = jax.ShapeDtypeStruct((m, N_MEM_HEADS * HEAD_DIM), F32)
    rshape = jax.ShapeDtypeStruct((1, HEAD_DIM), F32)
    return pl.pallas_call(
        body, name="mem_bwd", grid=(N_MEM_HEADS, t // tq),
        in_specs=[qblk, kblk, vblk, row, row, pl.BlockSpec((tq, HEAD_DIM), lambda h, i: (i, dooff + h))],
        out_specs=[oblk, kout, kout, row, row],
        out_shape=[jax.ShapeDtypeStruct((t, N_MEM_HEADS * HEAD_DIM), F32), half, half, rshape, rshape],
        compiler_params=_cparams(("arbitrary", "arbitrary")),
    )(p, mkv, mkv, gq, gk, dmix)


def _shift_down(x, s):
    if s == 0:
        return x
    r = lax.broadcasted_iota(jnp.int32, x.shape, 0)
    return jnp.where(r >= s, pltpu.roll(x, s, 0), 0.0)


def _shift_up(x, s):
    if s == 0:
        return x
    n = x.shape[0]
    r = lax.broadcasted_iota(jnp.int32, x.shape, 0)
    return jnp.where(r < n - s, pltpu.roll(x, n - s, 0), 0.0)


def _conv_fn(x0, x1, x2, x3, w0, w1, w2, w3, kind):
    y = _silu(x0 * w0 + x1 * w1 + x2 * w2 + x3 * w3)
    if kind == 2:
        return y
    y = y * lax.rsqrt(jnp.sum(y * y, axis=-1, keepdims=True) + NORM_EPS)
    return y * (HEAD_DIM ** -0.5) if kind == 0 else y


def _conv_fwd(p, off, conv_w, ng):
    t = p.shape[0]

    def body(x_ref, w_ref, o_ref):
        kind = pl.program_id(0) // ng
        x = x_ref[...]
        xs = [_shift_down(x, CONV_WIDTH - 1 - j) for j in range(CONV_WIDTH)]
        ws = [w_ref[j:j + 1, :] for j in range(CONV_WIDTH)]
        for kd in range(3):
            @pl.when(kind == kd)
            def _(kd=kd):
                o_ref[...] = _conv_fn(*xs, *ws, kd)

    return pl.pallas_call(
        body, name="gdn_conv_fwd", grid=(3 * ng,),
        in_specs=[pl.BlockSpec((t, HEAD_DIM), lambda c: (0, off + c)),
                  pl.BlockSpec((CONV_WIDTH, HEAD_DIM), lambda c: (0, c))],
        out_specs=pl.BlockSpec((t, HEAD_DIM), lambda c: (0, c)),
        out_shape=jax.ShapeDtypeStruct((t, 3 * ng * HEAD_DIM), F32),
        compiler_params=_cparams(("parallel",)),
    )(p, conv_w)


def _conv_bwd(p, off, conv_w, dys, ng):
    t = p.shape[0]

    def body(x_ref, w_ref, dq_ref, dk_ref, dv_ref, dx_ref, dw_ref):
        kind = pl.program_id(0) // ng
        dy_refs = (dq_ref, dk_ref, dv_ref)
        x = x_ref[...]
        xs = [_shift_down(x, CONV_WIDTH - 1 - j) for j in range(CONV_WIDTH)]
        ws = [w_ref[j:j + 1, :] for j in range(CONV_WIDTH)]
        for kd in range(3):
            @pl.when(kind == kd)
            def _(kd=kd):
                _, vjp = jax.vjp(functools.partial(_conv_fn, kind=kd), *xs, *ws)
                g = vjp(dy_refs[kd][...])
                dx = _shift_up(g[0], CONV_WIDTH - 1)
                for j in range(1, CONV_WIDTH):
                    dx = dx + _shift_up(g[j], CONV_WIDTH - 1 - j)
                dx_ref[...] = dx
                for j in range(CONV_WIDTH):
                    dw_ref[j:j + 1, :] = g[CONV_WIDTH + j]

    blk = pl.BlockSpec((t, HEAD_DIM), lambda c: (0, c))
    head = lambda k: pl.BlockSpec((t, HEAD_DIM), lambda c: (0, jnp.where(c // ng == k, c % ng, 0)))
    wblk = pl.BlockSpec((CONV_WIDTH, HEAD_DIM), lambda c: (0, c))
    return pl.pallas_call(
        body, name="gdn_conv_bwd", grid=(3 * ng,),
        in_specs=[pl.BlockSpec((t, HEAD_DIM), lambda c: (0, off + c)), wblk] + [head(k) for k in range(3)],
        out_specs=[blk, wblk],
        out_shape=[jax.ShapeDtypeStruct((t, 3 * ng * HEAD_DIM), F32),
                   jax.ShapeDtypeStruct((CONV_WIDTH, 3 * ng * HEAD_DIM), F32)],
        compiler_params=_cparams(("parallel",)),
    )(p, conv_w, *dys)


def _lower_inverse(lower):
    c = lower.shape[-1]
    r = lax.broadcasted_iota(jnp.int32, (1, c, c), 1)
    e = lax.broadcasted_iota(jnp.int32, (1, c, c), 2)
    hi = lax.Precision.HIGH
    inv = jnp.where(r == e, 1.0, 0.0) - lower
    pw = lower
    for _ in range(int(math.log2(c)) - 1):
        pw = _dot(pw, pw, ((1,), (0,)), hi)
        inv = inv + _dot(inv, pw, ((1,), (0,)), hi)
    return inv


@jax.custom_vjp
def _solve(lower, inv, vb, kbg):
    hi = lax.Precision.HIGH
    return _dot(inv, vb, ((1,), (0,)), hi), _dot(inv, kbg, ((1,), (0,)), hi)


def _solve_fwd(lower, inv, vb, kbg):
    u, w = _solve(lower, inv, vb, kbg)
    return (u, w), (inv, u, w)


def _solve_bwd(res, cts):
    inv, u, w = res
    dvb, dkbg = _tn(inv, cts[0]), _tn(inv, cts[1])
    return -(_nt(dvb, u) + _nt(dkbg, w)), jnp.zeros_like(inv), dvb, dkbg


_solve.defvjp(_solve_fwd, _solve_bwd)


def _wy_fn(q, k, v, gcol, grow, bcol, inv=None):
    b, c, dk = q.shape
    r = lax.broadcasted_iota(jnp.int32, (1, c, c), 1)
    e = lax.broadcasted_iota(jnp.int32, (1, c, c), 2)
    tril, strict = e <= r, e < r
    gc_col = jnp.sum(jnp.where(tril, grow, 0.0), axis=2, keepdims=True)
    gc_row = jnp.sum(jnp.where(r <= e, gcol, 0.0), axis=1, keepdims=True)
    g_last = jnp.sum(gcol, axis=1, keepdims=True)
    decay = jnp.exp(jnp.where(tril, gc_col - gc_row, NEG))
    kb, vb = k * bcol, v * bcol
    lower = jnp.where(strict, _nt(kb, k) * decay, 0.0)
    if inv is None:
        inv = _lower_inverse(lower)
    u, w = _solve(lower, inv, vb, kb * jnp.exp(gc_col))
    attn = jnp.where(tril, _nt(q, k) * decay, 0.0)
    qg = q * jnp.exp(gc_col)
    kdec = k * jnp.exp(g_last - gc_col)
    egl = jnp.broadcast_to(jnp.exp(g_last), (b, 1, dk))
    return u, w, qg, kdec, attn, egl, inv


def _scan_fn(u, w, qg, kdec, attn, egl, state):
    v_new = u - _nn(w, state)
    o = _nn(qg, state) + _nn(attn, v_new)
    return o, state * egl + _tn(kdec, v_new)


GDN_CHUNKS_PER_STEP = 4


def _gdn_fwd(qkv, vals, grow, nf, ng):
    t = qkv.shape[0]
    nch = t // CHUNK

    cb = GDN_CHUNKS_PER_STEP
    *wy, inv = _gdn_wy(qkv, vals, grow, nf, ng, cb)

    def body(u_ref, w_ref, qg_ref, kd_ref, at_ref, eg_ref, o_ref, st_ref, state):
        @pl.when(pl.program_id(0) == 0)
        def _():
            state[...] = jnp.zeros_like(state)

        st_ref[:, 0] = state[...]
        heads = lambda ref: jnp.stack([ref[:, h * HEAD_DIM:(h + 1) * HEAD_DIM] for h in range(ng)])
        o, new = _scan_fn(heads(u_ref), heads(w_ref), heads(qg_ref), heads(kd_ref), at_ref[:, 0], eg_ref[:, 0],
                          state[...])
        for h in range(ng):
            o_ref[:, h * HEAD_DIM:(h + 1) * HEAD_DIM] = o[h]
        state[...] = new

    w = ng * HEAD_DIM
    blk = pl.BlockSpec((CHUNK, w), lambda i: (i, 0))
    o, states = pl.pallas_call(
        body, name="gdn_scan_fwd", grid=(nch,),
        in_specs=[blk, blk, blk, blk, pl.BlockSpec((ng, 1, CHUNK, CHUNK), lambda i: (0, i, 0, 0)),
                  pl.BlockSpec((ng, 1, 1, HEAD_DIM), lambda i: (0, i, 0, 0))],
        out_specs=[blk, pl.BlockSpec((ng, 1, HEAD_DIM, HEAD_DIM), lambda i: (0, i, 0, 0))],
        out_shape=[jax.ShapeDtypeStruct((t, w), F32),
                   jax.ShapeDtypeStruct((ng, nch, HEAD_DIM, HEAD_DIM), F32)],
        scratch_shapes=[pltpu.VMEM((ng, HEAD_DIM, HEAD_DIM), F32)],
        compiler_params=_cparams(("arbitrary",)),
    )(*wy)
    return o, (wy, inv, states)


def _wy_batch(q_ref, k_ref, v_ref, vals_ref, gr_ref, nf, ng, cb):
    idx = [(c, h) for c in range(cb) for h in range(ng)]
    rows = lambda c: slice(c * CHUNK, (c + 1) * CHUNK)
    lanes = lambda h: slice(h * HEAD_DIM, (h + 1) * HEAD_DIM)
    wide = lambda ref: jnp.stack([ref[rows(c), lanes(h)] for c, h in idx])
    col = lambda lane0: jnp.stack([vals_ref[rows(c), lane0 + h:lane0 + h + 1] for c, h in idx])
    return idx, (wide(q_ref), wide(k_ref), wide(v_ref), col(nf), jnp.stack([gr_ref[h, c] for c, h in idx]),
                 col(nf + ng))


def _gdn_wy(qkv, vals, grow, nf, ng, cb):
    t = qkv.shape[0]
    nch = t // CHUNK

    def body(q_ref, k_ref, v_ref, vals_ref, gr_ref, u_ref, w_ref, qg_ref, kd_ref, at_ref, eg_ref, inv_ref):
        idx, args = _wy_batch(q_ref, k_ref, v_ref, vals_ref, gr_ref, nf, ng, cb)
        u, w, qg, kd, at, eg, inv = _wy_fn(*args)
        for b, (c, h) in enumerate(idx):
            rows, lanes = slice(c * CHUNK, (c + 1) * CHUNK), slice(h * HEAD_DIM, (h + 1) * HEAD_DIM)
            u_ref[rows, lanes] = u[b]
            w_ref[rows, lanes] = w[b]
            qg_ref[rows, lanes] = qg[b]
            kd_ref[rows, lanes] = kd[b]
            at_ref[h, c] = at[b]
            eg_ref[h, c] = eg[b]
            inv_ref[h, c] = inv[b]

    wd = ng * HEAD_DIM
    blk = lambda o: pl.BlockSpec((cb * CHUNK, wd), lambda i: (i, o))
    col = pl.BlockSpec((cb * CHUNK, HEAD_DIM), lambda i: (i, 0))
    sq = pl.BlockSpec((ng, cb, CHUNK, CHUNK), lambda i: (0, i, 0, 0))
    wide = jax.ShapeDtypeStruct((t, wd), F32)
    sq_shape = jax.ShapeDtypeStruct((ng, nch, CHUNK, CHUNK), F32)
    return pl.pallas_call(
        body, name="gdn_wy_fwd", grid=(nch // cb,),
        in_specs=[blk(0), blk(1), blk(2), col, pl.BlockSpec((ng, cb, 1, CHUNK), lambda i: (0, i, 0, 0))],
        out_specs=[blk(0), blk(0), blk(0), blk(0), sq, pl.BlockSpec((ng, cb, 1, HEAD_DIM), lambda i: (0, i, 0, 0)),
                   sq],
        out_shape=[wide, wide, wide, wide, sq_shape, jax.ShapeDtypeStruct((ng, nch, 1, HEAD_DIM), F32), sq_shape],
        compiler_params=_cparams(("parallel",)),
    )(qkv, qkv, qkv, vals, grow)


def _gdn_bwd(qkv, vals, grow, saved, do, nf, ng):
    t = qkv.shape[0]
    nch = t // CHUNK
    cb = GDN_CHUNKS_PER_STEP // 2
    wy, inv, states = saved
    wd = ng * HEAD_DIM

    def scan_body(u_ref, w_ref, qg_ref, kd_ref, at_ref, eg_ref, st_ref, do_ref,
                  du_ref, dw_ref, dqg_ref, dkd_ref, dat_ref, deg_ref, dstate):
        @pl.when(pl.program_id(0) == 0)
        def _():
            dstate[...] = jnp.zeros_like(dstate)

        heads = lambda ref: jnp.stack([ref[:, h * HEAD_DIM:(h + 1) * HEAD_DIM] for h in range(ng)])
        _, vjp = jax.vjp(_scan_fn, heads(u_ref), heads(w_ref), heads(qg_ref), heads(kd_ref), at_ref[:, 0],
                         eg_ref[:, 0], st_ref[:, 0])
        du, dw, dqg, dkd, dat, deg, dst = vjp((heads(do_ref), dstate[...]))
        for h in range(ng):
            lanes = slice(h * HEAD_DIM, (h + 1) * HEAD_DIM)
            du_ref[:, lanes] = du[h]
            dw_ref[:, lanes] = dw[h]
            dqg_ref[:, lanes] = dqg[h]
            dkd_ref[:, lanes] = dkd[h]
        dat_ref[:, 0] = dat
        deg_ref[:, 0] = deg
        dstate[...] = dst

    rev = lambda i: nch - 1 - i
    blk = pl.BlockSpec((CHUNK, wd), lambda i: (rev(i), 0))
    atb = pl.BlockSpec((ng, 1, CHUNK, CHUNK), lambda i: (0, rev(i), 0, 0))
    egb = pl.BlockSpec((ng, 1, 1, HEAD_DIM), lambda i: (0, rev(i), 0, 0))
    wide = jax.ShapeDtypeStruct((t, wd), F32)
    at_shape = jax.ShapeDtypeStruct((ng, nch, CHUNK, CHUNK), F32)
    eg_shape = jax.ShapeDtypeStruct((ng, nch, 1, HEAD_DIM), F32)
    dwy = pl.pallas_call(
        scan_body, name="gdn_scan_bwd", grid=(nch,),
        in_specs=[blk, blk, blk, blk, atb, egb,
                  pl.BlockSpec((ng, 1, HEAD_DIM, HEAD_DIM), lambda i: (0, rev(i), 0, 0)), blk],
        out_specs=[blk, blk, blk, blk, atb, egb],
        out_shape=[wide, wide, wide, wide, at_shape, eg_shape],
        scratch_shapes=[pltpu.VMEM((ng, HEAD_DIM, HEAD_DIM), F32)],
        compiler_params=_cparams(("arbitrary",)),
    )(*wy, states, do)

    def wy_body(q_ref, k_ref, v_ref, vals_ref, gr_ref, du_ref, dw_ref, dqg_ref, dkd_ref, dat_ref, deg_ref,
                inv_ref, dq_ref, dk_ref, dv_ref, dvals_ref, dgr_ref):
        idx, args = _wy_batch(q_ref, k_ref, v_ref, vals_ref, gr_ref, nf, ng, cb)
        lane = lax.broadcasted_iota(jnp.int32, (CHUNK, HEAD_DIM), 1)
        kept = jnp.stack([inv_ref[h, c] for c, h in idx])
        rows = lambda c: slice(c * CHUNK, (c + 1) * CHUNK)
        lanes = lambda h: slice(h * HEAD_DIM, (h + 1) * HEAD_DIM)
        wide_ct = lambda ref: jnp.stack([ref[rows(c), lanes(h)] for c, h in idx])
        cts = (wide_ct(du_ref), wide_ct(dw_ref), wide_ct(dqg_ref), wide_ct(dkd_ref),
               jnp.stack([dat_ref[h, c] for c, h in idx]), jnp.stack([deg_ref[h, c] for c, h in idx]))
        _, vjp = jax.vjp(lambda *a: _wy_fn(*a, inv=kept)[:6], *args)
        dq, dk, dv, dgc, dgr, dbc = vjp(cts)
        for b, (c, h) in enumerate(idx):
            dq_ref[rows(c), lanes(h)] = dq[b]
            dk_ref[rows(c), lanes(h)] = dk[b]
            dv_ref[rows(c), lanes(h)] = dv[b]
            dgr_ref[h, c] = dgr[b]
        for c in range(cb):
            acc = jnp.zeros((CHUNK, HEAD_DIM), F32)
            for h in range(ng):
                acc = jnp.where(lane == nf + h, dgc[c * ng + h], acc)
                acc = jnp.where(lane == nf + ng + h, dbc[c * ng + h], acc)
            dvals_ref[rows(c), :] = acc

    cblk = lambda o: pl.BlockSpec((cb * CHUNK, wd), lambda i: (i, o))
    col = pl.BlockSpec((cb * CHUNK, HEAD_DIM), lambda i: (i, 0))
    rowv = pl.BlockSpec((ng, cb, 1, CHUNK), lambda i: (0, i, 0, 0))
    return pl.pallas_call(
        wy_body, name="gdn_wy_bwd", grid=(nch // cb,),
        in_specs=[cblk(0), cblk(1), cblk(2), col, rowv, cblk(0), cblk(0), cblk(0), cblk(0),
                  pl.BlockSpec((ng, cb, CHUNK, CHUNK), lambda i: (0, i, 0, 0)),
                  pl.BlockSpec((ng, cb, 1, HEAD_DIM), lambda i: (0, i, 0, 0)),
                  pl.BlockSpec((ng, cb, CHUNK, CHUNK), lambda i: (0, i, 0, 0))],
        out_specs=[cblk(0), cblk(0), cblk(0), col, rowv],
        out_shape=[wide, wide, wide, jax.ShapeDtypeStruct((t, HEAD_DIM), F32),
                   jax.ShapeDtypeStruct((ng, nch, 1, CHUNK), F32)],
        compiler_params=_cparams(("parallel",)),
    )(qkv, qkv, qkv, vals, grow, *dwy, inv)


def _swiglu_fn(gate, up):
    return _silu(gate) * up


FFN_TN = 256


def _ffn_up(n2, wgu4):
    _, d, w = wgu4.shape
    t = n2.shape[0]
    tn = _tile(w, FFN_TN)
    nb = w // tn

    def body(a_ref, b_ref, gu_ref, act_ref):
        av = a_ref[...]
        gate = jnp.dot(av, b_ref[0], preferred_element_type=F32)
        up = jnp.dot(av, b_ref[1], preferred_element_type=F32)
        gu_ref[0] = gate.astype(BF16)
        gu_ref[1] = up.astype(BF16)
        act_ref[...] = _swiglu_fn(gate, up).astype(BF16)

    return pl.pallas_call(
        body, name="ffn_up", grid=(2, nb),
        in_specs=[pl.BlockSpec((t, d), lambda j, l: (0, 0)), pl.BlockSpec((2, d, tn), lambda j, l: (j, 0, l))],
        out_specs=[pl.BlockSpec((2, t, tn), lambda j, l: (j, 0, l)),
                   pl.BlockSpec((t, tn), lambda j, l: (0, j * nb + l))],
        out_shape=[jax.ShapeDtypeStruct((4, t, w), BF16), jax.ShapeDtypeStruct((t, 2 * w), BF16)],
        compiler_params=_cparams(("parallel", "parallel")),
    )(n2, wgu4)


def _ffn_dact(dh2, wd, gu, after):
    _, t, w = gu.shape
    d = dh2.shape[1]
    tn = _tile(w, FFN_TN)
    nb = w // tn

    def body(a_ref, b_ref, gu_ref, _, o_ref):
        dact = lax.dot_general(a_ref[...], b_ref[...], (((1,), (1,)), ((), ())), preferred_element_type=F32)
        _, vjp = jax.vjp(_swiglu_fn, gu_ref[0].astype(F32), gu_ref[1].astype(F32))
        dg, du = vjp(dact)
        o_ref[0] = dg.astype(BF16)
        o_ref[1] = du.astype(BF16)

    pair = pl.BlockSpec((2, t, tn), lambda j, l: (j, 0, l))
    return pl.pallas_call(
        body, name="ffn_dact", grid=(2, nb),
        in_specs=[pl.BlockSpec((t, d), lambda j, l: (0, 0)), pl.BlockSpec((tn, d), lambda j, l: (j * nb + l, 0)),
                  pair, pl.BlockSpec(after.shape, lambda j, l: (0, 0))],
        out_specs=pair, out_shape=jax.ShapeDtypeStruct(gu.shape, BF16),
        compiler_params=_cparams(("parallel", "parallel")),
    )(dh2, wd, gu, after)


def _loss_head(h2, target):
    t, d = h2.shape
    tr = _tile(t, 256, 8)

    def body(h_ref, t_ref, l_ref, d_ref, db_ref):
        @pl.when(pl.program_id(0) == 0)
        def _():
            l_ref[...] = jnp.zeros_like(l_ref)

        err = h_ref[...] - t_ref[...]
        d_ref[...] = err * (1.0 / d)
        db_ref[...] = (err * (1.0 / d)).astype(BF16)
        part = 0.5 * jnp.sum(jnp.mean(err * err, axis=-1, keepdims=True), axis=0, keepdims=True)
        lane = lax.broadcasted_iota(jnp.int32, (8, HEAD_DIM), 1)
        row = lax.broadcasted_iota(jnp.int32, (8, HEAD_DIM), 0)
        l_ref[...] += jnp.where((lane == 0) & (row == 0), part, 0.0)

    blk = pl.BlockSpec((tr, d), lambda r: (r, 0))
    return pl.pallas_call(
        body, name="loss_head", grid=(t // tr,), in_specs=[blk, blk],
        out_specs=[pl.BlockSpec((8, HEAD_DIM), lambda r: (0, 0)), blk, blk],
        out_shape=[jax.ShapeDtypeStruct((8, HEAD_DIM), F32), jax.ShapeDtypeStruct((t, d), F32),
                   jax.ShapeDtypeStruct((t, d), BF16)],
        compiler_params=_cparams(("arbitrary",)),
    )(h2, target)


def _adamw(w, g, m, v, *, g_fn=None, name):
    r, c = w.shape
    tr = _tile(r, max(8, (1 << 19) // c // 8 * 8), 8)
    gs = g if isinstance(g, tuple) else (g,)

    def body(w_ref, *refs):
        g_refs, (m_ref, v_ref, go_ref, d_ref, mo_ref, vo_ref) = refs[:len(gs)], refs[len(gs):]
        gr = g_refs[0][...] if g_fn is None else g_fn(*[ref[...] for ref in g_refs])
        mn = ADAM_B1 * m_ref[...] + (1.0 - ADAM_B1) * gr
        vn = ADAM_B2 * v_ref[...] + (1.0 - ADAM_B2) * (gr * gr)
        m_hat = mn / (1.0 - ADAM_B1 ** ADAM_STEP)
        v_hat = vn / (1.0 - ADAM_B2 ** ADAM_STEP)
        go_ref[...] = gr
        d_ref[...] = -ADAM_LR * (m_hat / (jnp.sqrt(v_hat) + ADAM_EPS) + ADAM_WD * w_ref[...])
        mo_ref[...] = mn
        vo_ref[...] = vn

    blk = pl.BlockSpec((tr, c), lambda i: (i, 0))
    gblks = [pl.BlockSpec((tr, gi.shape[1]), lambda i: (i, 0)) for gi in gs]
    return pl.pallas_call(
        body, name=name, grid=(r // tr,), in_specs=[blk] + gblks + [blk, blk], out_specs=[blk] * 4,
        out_shape=[jax.ShapeDtypeStruct((r, c), F32)] * 4,
        compiler_params=_cparams(("parallel",)),
    )(w, *gs, m, v)


class _Layout:
    def __init__(self, d):
        nh = d // HEAD_DIM
        self.nm = N_MEM_HEADS
        self.nf = (nh - self.nm) // 2
        self.ng = nh - self.nm - self.nf
        nf, ng, nm, hd = self.nf, self.ng, self.nm, HEAD_DIM
        self.o_fq, self.o_fk, self.o_fv, self.o_sm = 0, nf, 2 * nf, 3 * nf
        self.o_gq, self.o_gz, self.o_mq = 0, 3 * ng, 4 * ng
        self.cols_a = -(-(3 * nf + 1) // 4) * 4 * hd
        self.cols_b = -(-(4 * ng + nm) // 4) * 4 * hd
        self.cols = self.cols_a + self.cols_b
        sizes = [nf * hd, nf * hd, nf * hd, nf, 3 * ng * hd, ng * hd, ng, ng, nm * hd]
        starts = [sum(sizes[:i]) for i in range(len(sizes))]
        self.ref = list(zip(starts, sizes))
        self.in_cols = sum(sizes)

    def regroup(self, w):
        part = lambda i: w[:, self.ref[i][0]:self.ref[i][0] + self.ref[i][1]]
        a = [part(0), part(1), part(2), part(3), part(6), part(7)]
        b = [part(4), part(5), part(8)]
        pads = [self.cols_a - sum(p.shape[1] for p in a), self.cols_b - sum(p.shape[1] for p in b)]
        fill = [[jnp.zeros((w.shape[0], n), w.dtype)] if n else [] for n in pads]
        return jnp.concatenate(a + fill[0] + b + fill[1], axis=1)

    def ungroup(self, g):
        hd, nf, ng, nm = HEAD_DIM, self.nf, self.ng, self.nm
        sm, b0 = self.o_sm * hd, self.cols_a
        return jnp.concatenate([
            g[:, :3 * nf * hd], g[:, sm:sm + nf], g[:, b0:b0 + 3 * ng * hd],
            g[:, b0 + self.o_gz * hd:b0 + self.o_mq * hd], g[:, sm + nf:sm + nf + ng],
            g[:, sm + nf + ng:sm + nf + 2 * ng], g[:, b0 + self.o_mq * hd:b0 + (self.o_mq + nm) * hd]], axis=1)


def _lane_row(pieces):
    row = jnp.zeros((1, HEAD_DIM), F32)
    for off, a in pieces:
        row = lax.dynamic_update_slice(row, a.astype(F32), (0, off))
    return row


def _local_step(x, mem, target, prefetch, weights, reducer, sp):
    t, d = x.shape
    lay = _Layout(d)
    nf, ng, nm, hd = lay.nf, lay.ng, lay.nm, HEAD_DIM
    nch = t // CHUNK
    tq = _tile(t, 256)
    tk = tq

    u = _norm_fwd(x, 0, sp["norm_mix"], 1, d, BF16, name="norm_mix_fwd")
    prefetch("in_a", u)
    (win_a,) = weights("in_a", u)
    p_a = _mm(u, win_a, name="mm_in_a")
    pa = _lane_row([(nf, sp["gdn_a_log"])])
    pb = _lane_row([(0, sp["fox_f_bias"]), (nf, sp["gdn_dt_bias"])])
    vals, csum = _small_fwd(p_a, lay.o_sm, pa, pb, nf, ng)

    c_t = csum[:, :nf].T
    hp = _fox_heads(nf, 3)
    cr = c_t.reshape(nf, t // tk, 1, tk)
    cc = jnp.stack([jnp.pad(csum[:, g * hp:(g + 1) * hp], ((0, 0), (0, hd - hp))) for g in range(nf // hp)])
    fq = _norm_fwd(p_a, lay.o_fq, sp["fox_q_norm"], nf, hd, BF16, name="fox_qnorm_fwd")
    fk = _norm_fwd(p_a, lay.o_fk, sp["fox_k_norm"], nf, hd, BF16, name="fox_knorm_fwd")
    fv = p_a[:, lay.o_fv * hd:(lay.o_fv + nf) * hd].astype(BF16)
    o_fox, lse, mix = _fox_fwd(fq, fk, fv, cc, cr, nf, tq, tk, d)

    prefetch("in_b", lse)
    (win_b,) = weights("in_b", lse)
    prefetch("mixer", win_b)
    p = _mm(u, win_b, name="mm_in_b")
    wmkv, conv_taps = weights("mixer", p)
    sp = dict(sp, gdn_conv=conv_taps)
    qkv = _conv_fwd(p, lay.o_gq, sp["gdn_conv"], ng)
    grow = vals[:, nf:nf + ng].T.reshape(ng, nch, 1, CHUNK)
    o_g, states = _gdn_fwd(qkv, vals, grow, nf, ng)
    mix = _norm_fwd(o_g, 0, sp["gdn_out_norm"], ng, hd, BF16, z=p, zoff=lay.o_gz, into=mix, into_off=nf,
                    name="gdn_out_fwd")
    prefetch("out", mix)

    mem_n = _norm_fwd(mem, 0, sp["mem_norm"], 1, d, BF16, name="mem_norm_fwd")
    mkv = _mm(mem_n, wmkv, name="mm_memkv")
    tq_mem = _tile(t, 1024)
    mix = _mem_fwd(p, lay.o_mq, mkv, sp["mem_q_norm"], sp["mem_k_norm"], tq_mem, mix, nf + ng)
    prefetch("gate_up", mix)
    (wout,) = weights("out", mix)
    h1 = _mm(mix, wout, res=x, name="mm_out")
    n2 = _norm_fwd(h1, 0, sp["norm_ffn"], 1, d, BF16, name="norm_ffn_fwd")
    (wgu,) = weights("gate_up", n2)
    wgu4 = wgu.reshape(4, d, -1)
    gu, act = _ffn_up(n2, wgu4)
    prefetch("down", act)
    (wd,) = weights("down", act)
    h2 = _mm(act, wd, res=h1, name="mm_down")
    loss_blk, dh2, dh2_b = _loss_head(h2, target)

    g = {}
    token = reducer.pair("w_down", _mm(act, dh2_b, ta=True, out_dtype=BF16, name="mm_dw_down"))
    dgu = _ffn_dact(dh2_b, wd, gu, token)
    dw_gate_up = _mm(n2, dgu, ta=True, stack="out", out_dtype=BF16, name="mm_dw_gate_up").reshape(wgu.shape)
    token = reducer.pair("w_gate_up", dw_gate_up)
    dn2 = _mm(dgu, wgu4, tb=True, stack="sum", after=token, name="mm_dn2")
    token = reducer.ship("ffn", ["w_down", "w_gate_up"], dn2)
    dh1, g["norm_ffn"] = _norm_bwd(h1, 0, sp["norm_ffn"] + token[0, 0], dn2, 0, 1, d, res=dh2,
                                   name="norm_ffn_bwd")
    token = reducer.pair("w_out", _mm(mix, dh1, ta=True, out_dtype=BF16, name="mm_dw_out"))
    dmix = _mm(dh1, wout, tb=True, after=token, name="mm_dmix")

    dmq, dmk, dmv, g["mem_q_norm"], g["mem_k_norm"] = _mem_bwd(
        p, lay.o_mq, mkv, sp["mem_q_norm"], sp["mem_k_norm"], dmix, nf + ng, tq_mem)
    dmkv = jnp.concatenate([dmk, dmv], axis=1)
    token = reducer.pair("w_mem_kv", _mm(mem_n, dmkv, ta=True, out_dtype=BF16, name="mm_dw_memkv"))
    dmem_n = _mm(dmkv, wmkv, tb=True, after=token, name="mm_dmem")
    token = reducer.ship("mix", ["w_out", "w_mem_kv"], dmem_n)
    _, g["mem_norm"] = _norm_bwd(mem, 0, sp["mem_norm"], dmem_n, 0, 1, d, name="mem_norm_bwd")

    do_g, dgz, g["gdn_out_norm"] = _norm_bwd(o_g, 0, sp["gdn_out_norm"] + token[0, 0], dmix, nf, ng, hd, z=p,
                                             zoff=lay.o_gz, name="gdn_out_bwd")
    dq, dk, dv, dvals, dgr = _gdn_bwd(qkv, vals, grow, states, do_g, nf, ng)
    dgqkv, g["gdn_conv"] = _conv_bwd(p, lay.o_gq, sp["gdn_conv"], (dq, dk, dv), ng)

    dfq_n, dfk_n, dfv, dcc, dcr = _fox_bwd(fq, fk, fv, cc, cr, o_fox, lse, dmix, nf, tq, tk)
    dfq, g["fox_q_norm"] = _norm_bwd(p_a, lay.o_fq, sp["fox_q_norm"], dfq_n, 0, nf, hd, name="fox_qnorm_bwd")
    dfk, g["fox_k_norm"] = _norm_bwd(p_a, lay.o_fk, sp["fox_k_norm"], dfk_n, 0, nf, hd, name="fox_knorm_bwd")
    dc = dcc.reshape(nf, t).T + jnp.concatenate([dcr[g, :, :hp] for g in range(nf // hp)], axis=1)

    dvals = dvals + jnp.pad(dgr.reshape(ng, t).T, ((0, 0), (nf, hd - nf - ng)))
    dcsum = jnp.pad(dc, ((0, 0), (0, hd - nf)))
    dsm, dpa, dpb = _small_bwd(p_a, lay.o_sm, pa, pb, dvals, dcsum, nf, ng)
    g["fox_f_bias"] = dpb[:, :nf]
    g["gdn_dt_bias"] = dpb[:, nf:nf + ng]
    g["gdn_a_log"] = dpa[:, nf:nf + ng]

    zeros = lambda n: jnp.zeros((t, n), F32)
    dp_a = jnp.concatenate([dfq, dfk, dfv, dsm, zeros(lay.cols_a - (lay.o_sm + 1) * hd)], axis=1).astype(BF16)
    dp_b = jnp.concatenate([dgqkv, dgz, dmq, zeros(lay.cols_b - (lay.o_mq + nm) * hd)], axis=1).astype(BF16)
    token = reducer.pair("w_in_a", _mm(u, dp_a, ta=True, out_dtype=BF16, name="mm_dw_in_a"))
    token = reducer.pair("w_in_b", _mm(u, dp_b, ta=True, out_dtype=BF16, after=token, name="mm_dw_in_b"))
    du = _mm(dp_a, win_a, tb=True, after=token, name="mm_du_a")
    token = reducer.ship("in", ["w_in_a", "w_in_b"], du)
    du = _mm(dp_b, win_b, tb=True, res=du, after=token, name="mm_du_b")
    dx, g["norm_mix"] = _norm_bwd(x, 0, sp["norm_mix"], du, 0, 1, d, res=dh1, name="norm_mix_bwd")
    return loss_blk, dx, g


ANY = pl.BlockSpec(memory_space=pl.ANY)


def _me():
    x, y, c = lax.axis_index("x"), lax.axis_index("y"), lax.axis_index("c")
    chips = [(1 - x, y), (x, 1 - y), (1 - x, 1 - y)]
    return x, y, c, chips


def _slot(axis, k):
    return k if axis == 0 else 2 * (k % 2) + k // 2


def _slab(ref, axis, rows, cols, k, h):
    half = rows // 2
    return ref.at[pl.ds(_slot(axis, k) * rows + h * half, half), :]


def _remote(src, dst, send_sem, recv_sem, dev):
    return pltpu.make_async_remote_copy(src_ref=src, dst_ref=dst, send_sem=send_sem, recv_sem=recv_sem,
                                        device_id=dev, device_id_type=MESH)


HBM = pl.BlockSpec(memory_space=pltpu.HBM)
SEM = pl.BlockSpec(memory_space=pltpu.SEMAPHORE)
SPLIT = pltpu.CompilerParams(has_side_effects=pltpu.SideEffectType.DATAFLOW_SIDE_EFFECTING)
TOKEN = jax.ShapeDtypeStruct((8, HEAD_DIM), F32)


def _in_hbm(v):
    return pltpu.with_memory_space_constraint(v, pltpu.HBM)


def _cast_place(shard, axis, name, col_fn=None, out_cols=None, after=None):
    r, c = shard.shape
    oc = out_cols or c
    tr = _tile(r, 512 if col_fn is None else 64, 16)
    tc = _tile(c, 2048) if col_fn is None else c
    otc = tc if col_fn is None else oc
    nb = r // tr
    chip = 2 * lax.axis_index("x") + lax.axis_index("y")
    slot = jnp.reshape(_slot(axis, chip), (1,)).astype(jnp.int32)

    def body(slot_ref, x_ref, *rest):
        x = x_ref[...]
        rest[-1][...] = (x if col_fn is None else col_fn(x)).astype(BF16)

    extra = [] if after is None else [after]
    return pl.pallas_call(
        body, name=name,
        grid_spec=pltpu.PrefetchScalarGridSpec(
            num_scalar_prefetch=1, grid=(nb, c // tc),
            in_specs=[pl.BlockSpec((tr, tc), lambda i, l, s: (i, l))] + [ANY] * len(extra),
            out_specs=pl.BlockSpec((tr, otc), lambda i, l, s: (s[0] * nb + i, l))),
        out_shape=jax.ShapeDtypeStruct((4 * r, oc), BF16),
        compiler_params=_cparams(("parallel", "parallel")),
    )(slot, shard, *extra)


def _gather_start(bufs, axes, shapes, groups, name):
    n = len(bufs)

    def body(*refs):
        dst = refs[n:2 * n]
        sems = refs[2 * n:2 * n + 2 * len(groups)]
        token = refs[-1]
        x, y, c, chips = _me()
        k = 2 * x + y
        for gi, ws in enumerate(groups):
            for i, w in enumerate(ws):
                r, cl = shapes[w]
                place = _slab(dst[w], axes[w], r, cl, k, c)
                for j, (px, py) in enumerate(chips):
                    _remote(place, place, sems[2 * gi].at[3 * i + j], sems[2 * gi + 1].at[3 * i + j],
                            (px, py, c)).start()
        token[...] = jnp.zeros_like(token)

    sem_shapes = [pltpu.SemaphoreType.DMA((3 * len(ws),)) for ws in groups for _ in range(2)]
    outs = pl.pallas_call(
        body, name=name, in_specs=[HBM] * n,
        out_specs=[HBM] * n + [SEM] * len(sem_shapes) + [pl.BlockSpec(memory_space=pltpu.VMEM)],
        out_shape=[pltpu.HBM(b.shape, b.dtype) for b in bufs] + sem_shapes + [TOKEN],
        input_output_aliases={w: w for w in range(n)}, compiler_params=SPLIT,
    )(*[_in_hbm(b) for b in bufs])
    sems = outs[n:-1]
    return outs[:n], [(sems[2 * g], sems[2 * g + 1]) for g in range(len(groups))], outs[-1]


def _gather_wait(bufs, axes, shapes, sems, after, name):
    n = len(bufs)

    def body(*refs):
        send_sems, recv_sems = refs[n], refs[n + 1]
        dst = refs[n + 3:]
        x, y, c, chips = _me()
        k = 2 * x + y
        for i in range(n):
            r, cl = shapes[i]
            for j, (px, py) in enumerate(chips):
                got = _slab(dst[i], axes[i], r, cl, 2 * px + py, c)
                _remote(got, got, send_sems.at[3 * i + j], recv_sems.at[3 * i + j], (px, py, c)).wait_recv()
        for i in range(n):
            r, cl = shapes[i]
            mine = _slab(dst[i], axes[i], r, cl, k, c)
            for j, (px, py) in enumerate(chips):
                _remote(mine, mine, send_sems.at[3 * i + j], recv_sems.at[3 * i + j], (px, py, c)).wait_send()

    return pl.pallas_call(
        body, name=name, in_specs=[HBM] * n + [SEM, SEM, ANY], out_specs=[HBM] * n,
        out_shape=[pltpu.HBM(b.shape, b.dtype) for b in bufs],
        input_output_aliases={i: i for i in range(n)}, compiler_params=SPLIT,
    )(*bufs, sems[0], sems[1], after)


def _gather_forward(bufs, axes, shapes, name):
    n = len(bufs)

    def body(*refs):
        dst = refs[n:2 * n]
        send_sems, recv_sems = refs[2 * n:]
        x, y, c, chips = _me()
        sibling = (x, y, 1 - c)
        sends = []
        for i in range(n):
            r, cl = shapes[i]
            for j, (px, py) in enumerate(chips):
                got = _slab(dst[i], axes[i], r, cl, 2 * px + py, c)
                cp = _remote(got, got, send_sems.at[3 * i + j], recv_sems.at[3 * i + j], sibling)
                cp.start()
                sends.append(cp)
        for i in range(n):
            r, cl = shapes[i]
            for j, (px, py) in enumerate(chips):
                got = _slab(dst[i], axes[i], r, cl, 2 * px + py, 1 - c)
                _remote(got, got, send_sems.at[3 * i + j], recv_sems.at[3 * i + j], sibling).wait_recv()
        for cp in sends:
            cp.wait_send()

    return pl.pallas_call(
        body, name=name, in_specs=[ANY] * n, out_specs=[ANY] * n,
        out_shape=[jax.ShapeDtypeStruct(b.shape, b.dtype) for b in bufs],
        input_output_aliases={i: i for i in range(n)},
        scratch_shapes=[pltpu.SemaphoreType.DMA((3 * n,)), pltpu.SemaphoreType.DMA((3 * n,))],
    )(*bufs)


def _split_start(name, arrays, geometry, count):
    n = len(arrays)

    def body(*refs):
        send, recv, token = refs[2 * n:]
        for i, (src, dst, _, dev) in enumerate(geometry(refs[n:2 * n])):
            _remote(src, dst, send.at[i], recv.at[i], dev).start()
        token[...] = jnp.zeros_like(token)

    sem = pltpu.SemaphoreType.DMA((count,))
    outs = pl.pallas_call(
        body, name=name, in_specs=[HBM] * n,
        out_specs=[HBM] * n + [SEM, SEM, pl.BlockSpec(memory_space=pltpu.VMEM)],
        out_shape=[pltpu.HBM(v.shape, v.dtype) for v in arrays] + [sem, sem, TOKEN],
        input_output_aliases={i: i for i in range(n)}, compiler_params=SPLIT,
    )(*[_in_hbm(v) for v in arrays])
    return list(outs[:n]), (outs[n], outs[n + 1]), outs[-1]


def _split_wait(name, arrays, sems, after, geometry):
    n = len(arrays)

    def body(*refs):
        send, recv = refs[n], refs[n + 1]
        copies = geometry(refs[n + 3:])
        for i, (_, _, land, dev) in enumerate(copies):
            _remote(land, land, send.at[i], recv.at[i], dev).wait_recv()
        for i, (src, _, _, dev) in enumerate(copies):
            _remote(src, src, send.at[i], recv.at[i], dev).wait_send()

    return list(pl.pallas_call(
        body, name=name, in_specs=[HBM] * n + [SEM, SEM, ANY], out_specs=[HBM] * n,
        out_shape=[pltpu.HBM(v.shape, v.dtype) for v in arrays],
        input_output_aliases={i: i for i in range(n)}, compiler_params=SPLIT,
    )(*arrays, sems[0], sems[1], after))


def _forward_geometry(axes, shapes):
    def geometry(bufs):
        x, y, c, chips = _me()
        out = []
        for i, buf in enumerate(bufs):
            r, cl = shapes[i]
            for px, py in chips:
                got = _slab(buf, axes[i], r, cl, 2 * px + py, c)
                out.append((got, got, _slab(buf, axes[i], r, cl, 2 * px + py, 1 - c), (x, y, 1 - c)))
        return out
    return geometry


def _pair_geometry(axes, shapes):
    def geometry(refs):
        n = len(refs) // 2
        x, y, c, _ = _me()
        out = []
        for w in range(n):
            r, cl = shapes[w]
            for j in range(4):
                land = refs[n + w].at[j]
                out.append((_slab(refs[w], axes[w], r, cl, j, 1 - c), land, land, (x, y, 1 - c)))
        return out
    return geometry


def _after_all(name, token, *arrays):
    def body(*refs):
        refs[-1][...] = jnp.zeros_like(refs[-1])

    return pl.pallas_call(
        body, name=name, in_specs=[ANY] * (1 + len(arrays)), out_specs=pl.BlockSpec(memory_space=pltpu.VMEM),
        out_shape=TOKEN,
    )(token, *arrays)


def _swap_geometry(bufs):
    x, y, c, _ = _me()
    return [(b.at[c], b.at[c], b.at[1 - c], (x, y, 1 - c)) for b in bufs]


def _pair_exchange(fulls, axes, shapes, tag):
    n = len(fulls)

    def body(*refs):
        src, dst = refs[:n], refs[n:2 * n]
        send_sems, recv_sems = refs[2 * n:]
        x, y, c, _ = _me()
        sibling = (x, y, 1 - c)
        cps = []
        for w in range(n):
            r, cl = shapes[w]
            for j in range(4):
                cp = _remote(_slab(src[w], axes[w], r, cl, j, 1 - c), dst[w].at[j],
                             send_sems.at[4 * w + j], recv_sems.at[4 * w + j], sibling)
                cp.start()
                cps.append(cp)
        for cp in cps:
            cp.wait()

    out_shape = [jax.ShapeDtypeStruct((4, r // 2, cl), f.dtype) for (r, cl), f in zip(shapes, fulls)]
    return pl.pallas_call(
        body, name="reduce_pair_exchange_" + tag, in_specs=[ANY] * n, out_specs=[ANY] * n, out_shape=out_shape,
        scratch_shapes=[pltpu.SemaphoreType.DMA((4 * n,)), pltpu.SemaphoreType.DMA((4 * n,))],
    )(*fulls)


def _chip_start(parts, tag):
    n = len(parts)

    def body(*refs):
        src, land = refs[2 * n:3 * n], refs[3 * n:4 * n]
        send_sems, recv_sems, token = refs[4 * n:]
        x, y, c, chips = _me()
        k = 2 * x + y
        for w in range(n):
            for j, (px, py) in enumerate(chips):
                _remote(src[w].at[2 * px + py], land[w].at[k], send_sems.at[3 * w + j], recv_sems.at[3 * w + j],
                        (px, py, c)).start()
        token[...] = jnp.zeros_like(token)

    lands = [lax.empty(p.shape, p.dtype) for p in parts]
    sem = pltpu.SemaphoreType.DMA((3 * n,))
    outs = pl.pallas_call(
        body, name="reduce_ici_start_" + tag, in_specs=[HBM] * (2 * n),
        out_specs=[HBM] * (2 * n) + [SEM, SEM, pl.BlockSpec(memory_space=pltpu.VMEM)],
        out_shape=[pltpu.HBM(p.shape, p.dtype) for p in parts + lands] + [sem, sem, TOKEN],
        input_output_aliases={i: i for i in range(2 * n)}, compiler_params=SPLIT,
    )(*[_in_hbm(v) for v in parts + lands])
    return outs[:n], outs[n:2 * n], outs[2 * n], outs[2 * n + 1], outs[-1]


def _chip_wait(parts, lands, send_sems, recv_sems, after, tag):
    n = len(parts)

    def body(*refs):
        send, recv = refs[2 * n], refs[2 * n + 1]
        src, land = refs[2 * n + 3:3 * n + 3], refs[3 * n + 3:]
        x, y, c, chips = _me()
        for w in range(n):
            for j, (px, py) in enumerate(chips):
                got = land[w].at[2 * px + py]
                _remote(got, got, send.at[3 * w + j], recv.at[3 * w + j], (px, py, c)).wait_recv()
        for w in range(n):
            for j, (px, py) in enumerate(chips):
                sent = src[w].at[2 * px + py]
                _remote(sent, sent, send.at[3 * w + j], recv.at[3 * w + j], (px, py, c)).wait_send()

    outs = pl.pallas_call(
        body, name="reduce_ici_wait_" + tag, in_specs=[HBM] * (2 * n) + [SEM, SEM, ANY], out_specs=[HBM] * (2 * n),
        out_shape=[pltpu.HBM(p.shape, p.dtype) for p in parts + lands],
        input_output_aliases={i: i for i in range(2 * n)}, compiler_params=SPLIT,
    )(*parts, *lands, send_sems, recv_sems, after)
    chip = 2 * lax.axis_index("x") + lax.axis_index("y")
    return [lax.dynamic_update_slice(s, lax.dynamic_index_in_dim(p, chip, 0, keepdims=True), (chip, 0, 0))
            for p, s in zip(outs[:n], outs[n:])]


def _half_swap(halves, tag):
    n = len(halves)
    core = lax.axis_index("c")
    bufs = [lax.dynamic_update_slice(lax.empty((2,) + h.shape, h.dtype), h[None], (core, 0, 0)) for h in halves]

    def body(*refs):
        dst = refs[n:2 * n]
        send_sems, recv_sems = refs[2 * n:]
        x, y, c, _ = _me()
        sibling = (x, y, 1 - c)
        cps = []
        for w in range(n):
            cp = _remote(dst[w].at[c], dst[w].at[c], send_sems.at[w], recv_sems.at[w], sibling)
            cp.start()
            cps.append(cp)
        for w in range(n):
            other = dst[w].at[1 - c]
            _remote(other, other, send_sems.at[w], recv_sems.at[w], sibling).wait_recv()
        for cp in cps:
            cp.wait_send()

    outs = pl.pallas_call(
        body, name="reduce_half_swap_" + tag, in_specs=[ANY] * n, out_specs=[ANY] * n,
        out_shape=[jax.ShapeDtypeStruct(b.shape, b.dtype) for b in bufs],
        input_output_aliases={w: w for w in range(n)},
        scratch_shapes=[pltpu.SemaphoreType.DMA((n,)), pltpu.SemaphoreType.DMA((n,))],
    )(*bufs)
    return [o.reshape(2 * o.shape[1], o.shape[2]) for o in outs]


def _add_parts(full, axis, rows, sib, name):
    _, r, c = sib.shape
    tr, tc = _tile(r, 1024, 16), _tile(c, 2048)
    nb = r // tr
    core = jnp.reshape(lax.axis_index("c"), (1,)).astype(jnp.int32)

    def body(c_ref, a_ref, b_ref, o_ref):
        o_ref[0] = (a_ref[...].astype(F32) + b_ref[0].astype(F32)).astype(BF16)

    blk = pl.BlockSpec((1, tr, tc), lambda j, i, l, cr: (j, i, l))
    return pl.pallas_call(
        body, name=name,
        grid_spec=pltpu.PrefetchScalarGridSpec(
            num_scalar_prefetch=1, grid=(4, nb, c // tc),
            in_specs=[pl.BlockSpec((tr, tc), lambda j, i, l, cr: ((_slot(axis, j) * 2 + cr[0]) * nb + i, l)), blk],
            out_specs=blk),
        out_shape=jax.ShapeDtypeStruct(sib.shape, BF16),
        compiler_params=_cparams(("parallel", "parallel", "parallel")),
    )(core, full, sib)


def _sum_slots(a, name):
    _, r, c = a.shape
    tr, tc = _tile(r, 512, 8), _tile(c, 2048)

    def body(a_ref, o_ref):
        v = a_ref[...].astype(F32)
        o_ref[...] = ((v[0] + v[1]) + v[2]) + v[3]

    return pl.pallas_call(
        body, name=name, grid=(r // tr, c // tc),
        in_specs=[pl.BlockSpec((4, tr, tc), lambda i, l: (0, i, l))],
        out_specs=pl.BlockSpec((tr, tc), lambda i, l: (i, l)),
        out_shape=jax.ShapeDtypeStruct((r, c), F32),
        compiler_params=_cparams(("parallel", "parallel")),
    )(a)


class _Reducer:
    def __init__(self, spec):
        self.spec = spec
        self.paired = {}
        self.pending = []

    def pair(self, name, full):
        ax, shp = self.spec[name]
        land = lax.empty((4, shp[0] // 2, shp[1]), full.dtype)
        arrays, sems, token = _split_start("reduce_pair_start_" + name, [full, land], _pair_geometry([ax], [shp]), 4)
        self.paired[name] = (arrays, sems)
        return token

    def ship(self, tag, names, after):
        parts = []
        for n in names:
            ax, shp = self.spec[n]
            arrays, sems = self.paired.pop(n)
            full, sib = _split_wait("reduce_pair_wait_" + n, arrays, sems, after, _pair_geometry([ax], [shp]))
            parts.append(_add_parts(full, ax, shp[0], sib, name=f"reduce_add_{n}"))
        parts, lands, send, recv, token = _chip_start(parts, tag)
        self.pending.append((tag, names, parts, lands, send, recv))
        return token

    def start(self, tag, grads):
        names = list(grads)
        fulls, axes = [grads[n] for n in names], [self.spec[n][0] for n in names]
        shapes = [self.spec[n][1] for n in names]
        from_sibling = _pair_exchange(fulls, axes, shapes, tag)
        parts = [_add_parts(f, a, r, s, name=f"reduce_add_{n}")
                 for n, f, a, (r, cl), s in zip(names, fulls, axes, shapes, from_sibling)]
        parts, lands, send, recv, token = _chip_start(parts, tag)
        self.pending.append((tag, names, parts, lands, send, recv))
        return token

    def finish(self, after, tags):
        out = {}
        for tag, names, parts, lands, send, recv in [p for p in self.pending if p[0] in tags]:
            slots = _chip_wait(parts, lands, send, recv, after, tag)
            halves = [_sum_slots(s, name=f"reduce_sum_{n}") for n, s in zip(names, slots)]
            out.update(zip(names, _half_swap(halves, tag)))
        return out

    def finish_start(self, after, tag):
        (_, names, parts, lands, send, recv), = [p for p in self.pending if p[0] == tag]
        slots = _chip_wait(parts, lands, send, recv, after, tag)
        halves = [_sum_slots(s, name=f"reduce_sum_{n}") for n, s in zip(names, slots)]
        core = lax.axis_index("c")
        bufs = [lax.dynamic_update_slice(lax.empty((2,) + h.shape, h.dtype), h[None], (core, 0, 0)) for h in halves]
        bufs, sems, _ = _split_start("reduce_half_swap_start_" + tag, bufs, _swap_geometry, len(bufs))
        return tag, names, bufs, sems

    def swap_wait(self, started, after):
        tag, names, bufs, sems = started
        outs = _split_wait("reduce_half_swap_wait_" + tag, bufs, sems, after, _swap_geometry)
        return dict(zip(names, [o.reshape(2 * o.shape[1], o.shape[2]) for o in outs]))


def _allreduce_small(pack, after):
    rows = pack.shape[0]

    def body(p_ref, _, o_ref, slots, send_sems, recv_sems):
        x, y, c, _ = _me()
        me = 4 * x + 2 * y + c
        slots[me] = p_ref[...]
        cps = []
        for r in range(1, 8):
            peer = (x ^ (r >> 2), y ^ ((r >> 1) & 1), c ^ (r & 1))
            cp = _remote(p_ref, slots.at[me], send_sems.at[r - 1], recv_sems.at[r - 1], peer)
            cp.start()
            cps.append(cp)
        for r in range(1, 8):
            frm = me ^ r
            _remote(slots.at[frm], slots.at[frm], send_sems.at[r - 1], recv_sems.at[r - 1], (x, y, c)).wait_recv()
        for cp in cps:
            cp.wait_send()
        acc = slots[0]
        for s in range(1, 8):
            acc = acc + slots[s]
        o_ref[...] = acc

    vm = pl.BlockSpec(memory_space=pltpu.VMEM)
    return pl.pallas_call(
        body, name="allreduce_small", in_specs=[vm, ANY], out_specs=vm,
        out_shape=jax.ShapeDtypeStruct(pack.shape, F32),
        scratch_shapes=[pltpu.VMEM((8, rows, HEAD_DIM), F32), pltpu.SemaphoreType.DMA((7,)),
                        pltpu.SemaphoreType.DMA((7,))],
    )(pack, after)


_ROWS = ["norm_mix", "norm_ffn", "mem_norm", "fox_q_norm", "fox_k_norm", "gdn_out_norm", "mem_q_norm",
         "mem_k_norm", "fox_f_bias", "gdn_a_log", "gdn_dt_bias"]


def _pack_rows(vals):
    out = []
    for name in _ROWS:
        v = vals[name].reshape(-1)
        n = -(-v.shape[0] // HEAD_DIM) * HEAD_DIM
        out.append(jnp.pad(v, (0, n - v.shape[0])).reshape(-1, HEAD_DIM))
    return jnp.concatenate(out, axis=0)


def _unpack_rows(pack, like):
    out, r = {}, 0
    for name in _ROWS:
        n = like[name].shape[-1]
        nr = -(-n // HEAD_DIM)
        out[name] = pack[r:r + nr].reshape(1, -1)[:, :n]
        r += nr
    return out, r


def kernel(x, mem, norm_mix, w_in, fox_f_bias, fox_q_norm, fox_k_norm, gdn_conv, gdn_a_log, gdn_dt_bias, gdn_out_norm, mem_norm, w_mem_kv, mem_q_norm, mem_k_norm, w_out, norm_ffn, w_gate_up, w_down, loss_target, m_norm_mix, m_w_in, m_fox_f_bias, m_fox_q_norm, m_fox_k_norm, m_gdn_conv, m_gdn_a_log, m_gdn_dt_bias, m_gdn_out_norm, m_mem_norm, m_w_mem_kv, m_mem_q_norm, m_mem_k_norm, m_w_out, m_norm_ffn, m_w_gate_up, m_w_down, v_norm_mix, v_w_in, v_fox_f_bias, v_fox_q_norm, v_fox_k_norm, v_gdn_conv, v_gdn_a_log, v_gdn_dt_bias, v_gdn_out_norm, v_mem_norm, v_w_mem_kv, v_mem_q_norm, v_mem_k_norm, v_w_out, v_norm_ffn, v_w_gate_up, v_w_down):
    a = dict(locals())
    d = x.shape[-1]
    lay = _Layout(d)
    chip = 2 * lax.axis_index("x") + lax.axis_index("y")
    small = {n: a[n] for n in _ROWS}
    big = ["w_in", "w_mem_kv", "w_out", "w_gate_up", "w_down"]
    axes = [0, 0, 0, 1, 0]

    conv_cols = gdn_conv.shape[-1]
    conv_n = CONV_WIDTH * conv_cols
    conv_rows = -(-conv_n // HEAD_DIM)
    conv_blk = jnp.pad(gdn_conv.reshape(-1), (0, 32 * HEAD_DIM - conv_n)).reshape(32, HEAD_DIM)
    axis_of = dict(zip(big, axes), conv=0, w_in_a=0, w_in_b=0)
    shape_of = {n: a[n].shape[1:] for n in big[1:]}
    shape_of.update(w_in_a=(w_in.shape[1], lay.cols_a), w_in_b=(w_in.shape[1], lay.cols_b), conv=conv_blk.shape)
    placed = {"w_in_a": _cast_place(w_in[0], 0, "cast_w_in_a", lambda v: lay.regroup(v)[:, :lay.cols_a], lay.cols_a),
              "conv": lax.dynamic_update_slice(lax.empty((4 * 32, HEAD_DIM), F32), conv_blk, (chip * 32, 0))}
    grouped = {"in_a": ["w_in_a"], "in_b": ["w_in_b"], "mixer": ["w_mem_kv", "conv"], "out": ["w_out"],
               "gate_up": ["w_gate_up"], "down": ["w_down"]}
    inflight = {}

    def start(tags, name):
        names = [n for t in tags for n in grouped[t]]
        bufs, sems, token = _gather_start([placed[n] for n in names], [axis_of[n] for n in names],
                                          [shape_of[n] for n in names],
                                          [[names.index(n) for n in grouped[t]] for t in tags], name)
        for t, pair in zip(tags, sems):
            inflight[t] = ([bufs[names.index(n)] for n in grouped[t]], pair)
        return token

    first = start(["in_a"], "gather_ici_start_in")
    placed["w_in_b"] = _cast_place(w_in[0], 0, "cast_w_in_b", lambda v: lay.regroup(v)[:, lay.cols_a:], lay.cols_b,
                                   after=first)
    placed.update({n: _cast_place(a[n][0], axis_of[n], "cast_" + n, after=first) for n in big[1:]})
    all_started = start(["in_b", "mixer", "out", "gate_up", "down"], "gather_ici_start_rest")
    all_started = _after_all("moments_ready", all_started, m_w_in[0], v_w_in[0])

    forwarding = {}

    def prefetch(tag, after):
        bufs, sem_pair = inflight.pop(tag)
        ax, shp = [axis_of[n] for n in grouped[tag]], [shape_of[n] for n in grouped[tag]]
        got = _gather_wait(bufs, ax, shp, sem_pair, all_started if tag == "in_a" else after,
                           "gather_ici_wait_" + tag)
        geometry = _forward_geometry(ax, shp)
        got, sems, _ = _split_start("gather_forward_start_" + tag, got, geometry, 3 * len(got))
        forwarding[tag] = (got, sems, geometry)

    def weights(tag, after):
        got, sems, geometry = forwarding.pop(tag)
        got = _split_wait("gather_forward_wait_" + tag, got, sems, after, geometry)
        if tag != "mixer":
            return got
        taps = got[1].reshape(4, 32 * HEAD_DIM)[:, :conv_n].reshape(4, CONV_WIDTH, conv_cols)
        return got[0], jnp.transpose(taps, (1, 0, 2)).reshape(CONV_WIDTH, 4 * conv_cols)

    sp = dict(small)
    reducer = _Reducer({n: (axis_of[n], shape_of[n]) for n in big[1:] + ["w_in_a", "w_in_b"]})
    loss_blk, dx, g = _local_step(x[0], mem[0], loss_target[0], prefetch, weights, reducer, sp)

    gsmall = {n: g[n] for n in _ROWS}
    pack = jnp.concatenate([_pack_rows(gsmall), g["gdn_conv"].reshape(-1, HEAD_DIM), loss_blk], axis=0)
    pack = jnp.pad(pack, ((0, -pack.shape[0] % 8), (0, 0)))
    out = {"grad_x": dx[None]}

    def adamw_shards(reduced):
        if "w_in_a" in reduced:
            reduced = {"w_in": (reduced["w_in_a"], reduced["w_in_b"])}
        for n, gsh in reduced.items():
            join = (lambda ga, gb: lay.ungroup(jnp.concatenate([ga, gb], axis=1))) if n == "w_in" else None
            res = _adamw(a[n][0], gsh, a["m_" + n][0], a["v_" + n][0], g_fn=join, name="adamw_" + n)
            for pre, r in zip(["grad_", "delta_", "new_m_", "new_v_"], res):
                out[pre + n] = r[None]
        return res[0]

    mix_swap = reducer.finish_start(dx, "mix")
    ffn_swap = reducer.finish_start(mix_swap[2][0], "ffn")
    done = adamw_shards(reducer.swap_wait(mix_swap, ffn_swap[2][0]))
    done = adamw_shards(reducer.swap_wait(ffn_swap, done))
    tot = _allreduce_small(pack, done)
    gs, r0 = _unpack_rows(tot, small)
    conv_g = tot[r0:r0 + CONV_WIDTH * 4 * conv_cols // HEAD_DIM].reshape(CONV_WIDTH, 4 * conv_cols)
    gs_conv = lax.dynamic_slice_in_dim(conv_g, chip * conv_cols, conv_cols, axis=1)
    out["loss"] = tot[r0 + CONV_WIDTH * 4 * conv_cols // HEAD_DIM, 0]
    adamw_shards(reducer.finish(tot, ("in",)))
    conv_pad = lambda v: jnp.pad(v.reshape(-1), (0, conv_rows * HEAD_DIM - conv_n)).reshape(conv_rows, HEAD_DIM)
    packs = []
    for src, cv in [(small, gdn_conv), (gs, gs_conv), ({n: a["m_" + n] for n in _ROWS}, m_gdn_conv),
                    ({n: a["v_" + n] for n in _ROWS}, v_gdn_conv)]:
        packs.append(jnp.concatenate([_pack_rows(src), conv_pad(cv)], axis=0))
    res = _adamw(*packs, name="adamw_small")
    for pre, r in zip(["grad_", "delta_", "new_m_", "new_v_"], res):
        vals, r1 = _unpack_rows(r, small)
        for n in _ROWS:
            out[pre + n] = vals[n]
        out[pre + "gdn_conv"] = r[r1:r1 + conv_rows].reshape(-1)[:conv_n].reshape(gdn_conv.shape)
    names = ["norm_mix", "w_in", "fox_f_bias", "fox_q_norm", "fox_k_norm", "gdn_conv", "gdn_a_log", "gdn_dt_bias",
             "gdn_out_norm", "mem_norm", "w_mem_kv", "mem_q_norm", "mem_k_norm", "w_out", "norm_ffn", "w_gate_up",
             "w_down"]
    return (out["loss"], out["grad_x"], *[out[p + n] for p in ["grad_", "delta_", "new_m_", "new_v_"] for n in names])
```

```python
import functools
import math

import jax
import jax.numpy as jnp
from jax import lax
from jax.experimental import pallas as pl
from jax.experimental.pallas import tpu as pltpu

F32, BF16 = jnp.float32, jnp.bfloat16
HEAD_DIM = 128
CHUNK = 64
N_MEM_HEADS = 4
CONV_WIDTH = 4
NORM_EPS = 1e-6
ADAM_LR, ADAM_B1, ADAM_B2, ADAM_EPS, ADAM_WD, ADAM_STEP = 0.001, 0.9, 0.999, 1e-08, 0.01, 10
VMEM_LIMIT = 48 * 1024 * 1024
NEG = -1e30
MESH = pl.DeviceIdType.MESH


def _cparams(sem=None, **kw):
    if sem is not None:
        kw["dimension_semantics"] = sem
    return pltpu.CompilerParams(vmem_limit_bytes=VMEM_LIMIT, **kw)


def _tile(n, target, mult=128):
    best = None
    d = mult
    while d <= min(n, target):
        if n % d == 0:
            best = d
        d += mult
    return best if best is not None else n


def _dot(a, b, dims, hi):
    if a.ndim == 3:
        dn = (((dims[0][0] + 1,), (dims[1][0] + 1,)), ((0,), (0,)))
    else:
        dn = (dims, ((), ()))
    if hi is not None:
        return lax.dot_general(a, b, dn, precision=hi, preferred_element_type=F32)
    return lax.dot_general(a.astype(BF16), b.astype(BF16), dn, preferred_element_type=F32)


def _make_dots(hi, cotangent=None):
    @jax.custom_vjp
    def nn(a, b):
        return _dot(a, b, ((1,), (0,)), hi)

    @jax.custom_vjp
    def nt(a, b):
        return _dot(a, b, ((1,), (1,)), hi)

    @jax.custom_vjp
    def tn(a, b):
        return _dot(a, b, ((0,), (0,)), hi)

    bnn, bnt, btn = cotangent or (nn, nt, tn)
    nn.defvjp(lambda a, b: (nn(a, b), (a, b)), lambda r, g: (bnt(g, r[1]), btn(r[0], g)))
    nt.defvjp(lambda a, b: (nt(a, b), (a, b)), lambda r, g: (bnn(g, r[1]), btn(g, r[0])))
    tn.defvjp(lambda a, b: (tn(a, b), (a, b)), lambda r, g: (bnt(r[1], g), bnn(r[0], g)))
    return nn, nt, tn


_nn, _nt, _tn = _make_dots(None)
_nn_hi, _nt_hi, _tn_hi = _make_dots(lax.Precision.HIGHEST)


def _sigmoid(x):
    return jax.nn.sigmoid(x)


@jax.custom_vjp
def _softplus(x):
    return jnp.maximum(x, 0.0) + jnp.log(1.0 + jnp.exp(-jnp.abs(x)))


_softplus.defvjp(lambda x: (_softplus(x), x), lambda x, g: (g * _sigmoid(x),))


def _silu(x):
    return x * _sigmoid(x)


def _rms_fn(x, gain, z=None):
    y = x * lax.rsqrt(jnp.mean(x * x, axis=-1, keepdims=True) + NORM_EPS) * gain
    if z is not None:
        y = y * _silu(z)
    return y


def _mm(a, b, *, ta=False, tb=False, out_dtype=F32, res=None, stack=None, after=None, name):
    a2, b2 = a.shape[-2:], b.shape[-2:]
    ns = b.shape[0] if stack else 1
    m = a2[1] if ta else a2[0]
    k = a2[0] if ta else a2[1]
    n = b2[0] if tb else b2[1]
    assert k == (b2[1] if tb else b2[0])
    tm, tn, tk = _mm_tiles(m, n, k, ns if stack == "sum" else 1, a.dtype.itemsize, b.dtype.itemsize,
                           jnp.dtype(out_dtype).itemsize, res is not None)
    nk = k // tk
    single = nk == 1 and stack != "sum"
    dims = ((0 if ta else 1,), (1 if tb else 0,))
    if stack == "sum":
        order = lambda g0, g1, g2, g3: (g2, g0, g1, g3)
        grid = (m // tm, n // tn, ns, nk)
    else:
        order = lambda g0, g1, g2, g3: (g0, g1, g2, g3)
        grid = (ns, m // tm, n // tn, nk)

    def body(*refs):
        if after is not None:
            refs = refs[:2 + (res is not None)] + refs[3 + (res is not None):]
        if single:
            a_ref, b_ref = refs[:2]
            r = lax.dot_general(a_ref[...].astype(BF16), b_ref[...].astype(BF16), (dims, ((), ())),
                                preferred_element_type=F32)
            if res is not None:
                r = r + refs[2][...]
            refs[-1][...] = r.astype(out_dtype)
            return
        if res is None:
            a_ref, b_ref, o_ref, acc = refs
        else:
            a_ref, b_ref, r_ref, o_ref, acc = refs
        s, _, _, kk = order(*[pl.program_id(d) for d in range(4)])
        first = kk == 0
        last = kk == nk - 1
        if stack == "sum":
            first, last = first & (s == 0), last & (s == ns - 1)

        @pl.when(first)
        def _():
            acc[...] = jnp.zeros_like(acc)

        acc[...] += lax.dot_general(a_ref[...].astype(BF16), b_ref[...].astype(BF16), (dims, ((), ())),
                                    preferred_element_type=F32)

        @pl.when(last)
        def _():
            r = acc[...]
            if res is not None:
                r = r + r_ref[...]
            o_ref[...] = r.astype(out_dtype)

    def spec(shape, idx, stacked):
        if stacked:
            return pl.BlockSpec((None,) + shape, lambda *g: (order(*g)[0],) + idx(*order(*g)))
        return pl.BlockSpec(shape, lambda *g: idx(*order(*g)))

    a_spec = (spec((tk, tm), lambda s, i, j, kk: (kk, i), stack == "sum") if ta
              else spec((tm, tk), lambda s, i, j, kk: (i, kk), stack == "sum"))
    b_spec = (spec((tn, tk), lambda s, i, j, kk: (j, kk), bool(stack)) if tb
              else spec((tk, tn), lambda s, i, j, kk: (kk, j), bool(stack)))
    o_spec = spec((tm, tn), lambda s, i, j, kk: (i, j), stack == "out")
    ins, specs = [a, b], [a_spec, b_spec]
    if res is not None:
        ins.append(res)
        specs.append(o_spec)
    if after is not None:
        ins.append(after)
        specs.append(pl.BlockSpec(after.shape, lambda *g: (0,) * after.ndim))
    sem = (("parallel", "parallel", "arbitrary", "arbitrary") if stack == "sum"
           else ("parallel", "parallel", "parallel", "arbitrary"))
    return pl.pallas_call(
        body, name=name, grid=grid, in_specs=specs, out_specs=o_spec,
        out_shape=jax.ShapeDtypeStruct(((ns,) if stack == "out" else ()) + (m, n), out_dtype),
        scratch_shapes=[] if single else [pltpu.VMEM((tm, tn), F32)],
        compiler_params=_cparams(sem),
    )(*ins)


MM_VMEM_BUDGET = 40 * 1024 * 1024
MXU_WIDTH = 256


def _mm_tiles(m, n, k, ns, sa, sb, so, has_res):
    def divs(x, mult, cap):
        out = [d for d in range(mult, min(x, cap) + 1, mult) if x % d == 0]
        return out or [x]

    best = None
    for tk in divs(k, 128, 8192):
        nk = (k // tk) * ns
        for tm in divs(m, 8, 2048):
            for tn in divs(n, 128, 2048):
                vmem = 2 * (tm * tk * sa + tk * tn * sb + tm * tn * so) + (2 * tm * tn * 4 if has_res else 0)
                vmem += tm * tn * 4 if nk > 1 else 0
                if vmem > MM_VMEM_BUDGET:
                    continue
                steps = (m // tm) * (n // tn) * nk
                traffic = (m // tm) * k * n * sb * ns + (n // tn if nk > 1 else 1) * m * k * sa * ns
                cost = steps * 0.4e-6 + traffic / 2.5e12 + (nk * m * n * 8 / 6e12 if nk > 1 else 0)
                cost += 2.0 * m * n * k * ns / 7e14 * (-(-tn // MXU_WIDTH) * MXU_WIDTH / tn)
                if best is None or cost < best[0]:
                    best = (cost, tm, tn, tk)
    return best[1:]


def _norm_fwd(x, xoff, gain, ncol, w, out_dtype, *, z=None, zoff=0, into=None, into_off=0, name):
    t = x.shape[0]
    tr = _tile(t, max(256, (1 << 18) // w), 8)

    def body(*refs):
        x_ref, g_ref, o_ref = refs[0], refs[1], refs[-1]
        y = _rms_fn(x_ref[...], g_ref[...]) if z is None else _rms_fn(x_ref[...], g_ref[...], refs[2][...])
        o_ref[...] = y.astype(out_dtype)

    ins = [x, gain]
    specs = [pl.BlockSpec((tr, w), lambda j, r: (r, xoff + j)), pl.BlockSpec((1, w), lambda j, r: (0, 0))]
    if z is not None:
        ins.append(z)
        specs.append(pl.BlockSpec((tr, w), lambda j, r: (r, zoff + j)))
    aliases = {}
    if into is not None:
        aliases = {len(ins): 0}
        ins.append(into)
        specs.append(pl.BlockSpec(memory_space=pl.ANY))
    return pl.pallas_call(
        body, name=name, grid=(ncol, t // tr), in_specs=specs,
        out_specs=pl.BlockSpec((tr, w), lambda j, r: (r, into_off + j)),
        out_shape=jax.ShapeDtypeStruct((t, ncol * w) if into is None else into.shape, out_dtype),
        input_output_aliases=aliases, compiler_params=_cparams(("parallel", "parallel")),
    )(*ins)


def _norm_bwd(x, xoff, gain, dy, dyoff, ncol, w, *, z=None, zoff=0, res=None, name):
    t = x.shape[0]
    tr = _tile(t, max(256, (1 << 18) // w), 8)

    def body(*refs):
        it = iter(refs)
        x_ref, g_ref = next(it), next(it)
        z_ref = next(it) if z is not None else None
        dy_ref = next(it)
        r_ref = next(it) if res is not None else None
        dx_ref = next(it)
        dz_ref = next(it) if z is not None else None
        dg_ref = next(it)

        @pl.when((pl.program_id(0) == 0) & (pl.program_id(1) == 0))
        def _():
            dg_ref[...] = jnp.zeros_like(dg_ref)

        args = (x_ref[...], g_ref[...]) + ((z_ref[...],) if z is not None else ())
        _, vjp = jax.vjp(_rms_fn, *args)
        grads = vjp(dy_ref[...].astype(F32))
        dx = grads[0]
        if res is not None:
            dx = dx + r_ref[...]
        dx_ref[...] = dx
        if z is not None:
            dz_ref[...] = grads[2]
        dg_ref[...] += grads[1]

    ins = [x, gain]
    specs = [pl.BlockSpec((tr, w), lambda j, r: (r, xoff + j)), pl.BlockSpec((1, w), lambda j, r: (0, 0))]
    if z is not None:
        ins.append(z)
        specs.append(pl.BlockSpec((tr, w), lambda j, r: (r, zoff + j)))
    ins.append(dy)
    specs.append(pl.BlockSpec((tr, w), lambda j, r: (r, dyoff + j)))
    blk = pl.BlockSpec((tr, w), lambda j, r: (r, j))
    if res is not None:
        ins.append(res)
        specs.append(blk)
    full = jax.ShapeDtypeStruct((t, ncol * w), F32)
    out_shape, out_specs = [full], [blk]
    if z is not None:
        out_shape.append(full)
        out_specs.append(blk)
    out_shape.append(jax.ShapeDtypeStruct((1, w), F32))
    out_specs.append(pl.BlockSpec((1, w), lambda j, r: (0, 0)))
    return pl.pallas_call(
        body, name=name, grid=(ncol, t // tr), in_specs=specs, out_specs=out_specs, out_shape=out_shape,
        compiler_params=_cparams(("arbitrary", "arbitrary")),
    )(*ins)


def _small_fn(x, pa, pb, nf, ng):
    lane = lax.broadcasted_iota(jnp.int32, x.shape, 1)
    zz = x + pb
    logf = -_softplus(-zz)
    g = -jnp.exp(pa) * _softplus(zz)
    beta = _sigmoid(x)
    return jnp.where(lane < nf, logf, jnp.where(lane < nf + ng, g, beta))


def _tri(n, upper):
    r = lax.broadcasted_iota(jnp.int32, (n, n), 0)
    c = lax.broadcasted_iota(jnp.int32, (n, n), 1)
    return jnp.where((c >= r) if upper else (c <= r), 1.0, 0.0).astype(F32)


def _small_fwd(p, off, pa, pb, nf, ng):
    t = p.shape[0]
    blk = HEAD_DIM
    nb = t // blk

    def body(x_ref, pa_ref, pb_ref, v_ref, c_ref):
        v_ref[...] = _small_fn(x_ref[...], pa_ref[...], pb_ref[...], nf, ng)
        tri = _tri(blk, False)

        carry = jnp.zeros((1, HEAD_DIM), F32)
        for i in range(nb):
            rows = slice(i * blk, (i + 1) * blk)
            c = _nn_hi(tri, v_ref[rows, :]) + carry
            c_ref[rows, :] = c
            carry = c[blk - 1:blk, :]

    row = pl.BlockSpec((1, HEAD_DIM), lambda i: (0, 0))
    out = pl.BlockSpec((t, HEAD_DIM), lambda i: (0, 0))
    return pl.pallas_call(
        body, name="small_fwd", grid=(1,),
        in_specs=[pl.BlockSpec((t, HEAD_DIM), lambda i: (0, off)), row, row], out_specs=[out, out],
        out_shape=[jax.ShapeDtypeStruct((t, HEAD_DIM), F32)] * 2,
        compiler_params=_cparams(("arbitrary",)),
    )(p, pa, pb)


def _small_bwd(p, off, pa, pb, dvals, dcsum, nf, ng):
    t = p.shape[0]
    blk = HEAD_DIM
    nb = t // blk

    def body(x_ref, pa_ref, pb_ref, dv_ref, dc_ref, dx_ref, dpa_ref, dpb_ref, tot_ref):
        tri = _tri(blk, True)

        carry = jnp.zeros((1, HEAD_DIM), F32)
        for i in reversed(range(nb)):
            rows = slice(i * blk, (i + 1) * blk)
            c = _nn_hi(tri, dc_ref[rows, :]) + carry
            tot_ref[rows, :] = c + dv_ref[rows, :]
            carry = c[0:1, :]
        f = functools.partial(_small_fn, nf=nf, ng=ng)
        _, vjp = jax.vjp(f, x_ref[...], pa_ref[...], pb_ref[...])
        dx, dpa, dpb = vjp(tot_ref[...])
        dx_ref[...] = dx
        dpa_ref[...] = dpa
        dpb_ref[...] = dpb

    row = pl.BlockSpec((1, HEAD_DIM), lambda i: (0, 0))
    full = pl.BlockSpec((t, HEAD_DIM), lambda i: (0, 0))
    return pl.pallas_call(
        body, name="small_bwd", grid=(1,),
        in_specs=[pl.BlockSpec((t, HEAD_DIM), lambda i: (0, off)), row, row, full, full],
        out_specs=[full, row, row],
        out_shape=[jax.ShapeDtypeStruct((t, HEAD_DIM), F32), jax.ShapeDtypeStruct((1, HEAD_DIM), F32),
                   jax.ShapeDtypeStruct((1, HEAD_DIM), F32)],
        scratch_shapes=[pltpu.VMEM((t, HEAD_DIM), F32)],
        compiler_params=_cparams(("arbitrary",)),
    )(p, pa, pb, dvals, dcsum)


def _fox_heads(nf, most):
    return next(h for h in range(most, 0, -1) if nf % h == 0)


def _fox_fwd(q, k, v, cc, cr, nf, tq, tk, d_mix):
    t = q.shape[0]
    scale = HEAD_DIM ** -0.5
    assert tq == tk

    vt = jnp.transpose(v.reshape(t // tk, tk, nf, HEAD_DIM), (2, 0, 3, 1))

    hp = _fox_heads(nf, 3)
    lanes = lambda h: slice(h * HEAD_DIM, (h + 1) * HEAD_DIM)

    def body(q_ref, k_ref, vt_ref, cc_ref, cr_ref, o_ref, lse_ref, mix_ref):
        i = pl.program_id(1)
        qs = [q_ref[:, lanes(h)] for h in range(hp)]
        cqs = [cr_ref[h, i] for h in range(hp)]
        ones = jnp.ones((8, tk), BF16)
        diff = lax.broadcasted_iota(jnp.int32, (tk, tq), 0) - lax.broadcasted_iota(jnp.int32, (tk, tq), 1)

        def scores(h, j):
            ks = pl.ds(pl.multiple_of(j * tk, tk), tk)
            return lax.dot_general(k_ref[ks, lanes(h)], qs[h], (((1,), (1,)), ((), ())),
                                   preferred_element_type=F32)

        def tile(h, j, m, l, acc, s, masked):
            ks = pl.ds(pl.multiple_of(j * tk, tk), tk)
            s = s * scale + cqs[h] - cc_ref[0, ks, h:h + 1]
            if masked:
                s = jnp.where(diff <= 0, s, NEG)
            m_new = jnp.maximum(m, jnp.max(s, axis=0, keepdims=True))
            pr = jnp.exp(s - m_new).astype(BF16)
            alpha = jnp.exp(m - m_new)
            l = alpha * l + jnp.dot(ones, pr, preferred_element_type=F32)[:1]
            acc = alpha * acc + jnp.dot(vt_ref[h, j], pr, preferred_element_type=F32)
            return m_new, l, acc

        def step(j, carry):
            nxt = [scores(h, j + 1) for h in range(hp)]
            return tuple(tile(h, j, *carry[h], False) + (nxt[h],) for h in range(hp))

        init = tuple((jnp.full((1, tq), NEG, F32), jnp.zeros((1, tq), F32), jnp.zeros((HEAD_DIM, tq), F32),
                      scores(h, 0)) for h in range(hp))
        carry = lax.fori_loop(0, i, step, init)
        for h in range(hp):
            m, l, acc = tile(h, i, *carry[h], True)
            o = jnp.transpose(acc / l)
            o_ref[:, lanes(h)] = o
            mix_ref[:, lanes(h)] = o.astype(BF16)
            lse_ref[h, 0] = m + jnp.log(l)

    w = hp * HEAD_DIM
    qblk = pl.BlockSpec((tq, w), lambda h, i: (i, h))
    return pl.pallas_call(
        body, name="fox_fwd", grid=(nf // hp, t // tq),
        in_specs=[qblk, pl.BlockSpec((t, w), lambda h, i: (0, h)),
                  pl.BlockSpec((hp, t // tk, HEAD_DIM, tk), lambda h, i: (h, 0, 0, 0)),
                  pl.BlockSpec((1, t, HEAD_DIM), lambda h, i: (h, 0, 0)),
                  pl.BlockSpec((hp, t // tk, 1, tk), lambda h, i: (h, 0, 0, 0))],
        out_specs=[qblk, pl.BlockSpec((hp, 1, 1, tq), lambda h, i: (h, i, 0, 0)), qblk],
        out_shape=[jax.ShapeDtypeStruct((t, nf * HEAD_DIM), F32), jax.ShapeDtypeStruct((nf, t // tq, 1, tq), F32),
                   jax.ShapeDtypeStruct((t, d_mix), BF16)],
        compiler_params=_cparams(("parallel", "parallel")),
    )(q, k, vt, cc, cr)


def _fox_bwd(q, k, v, cc, cr, o, lse, dmix, nf, tq, tk):
    t = q.shape[0]
    scale = HEAD_DIM ** -0.5
    assert tq == tk
    hp = _fox_heads(nf, 3)
    lanes = lambda h: slice(h * HEAD_DIM, (h + 1) * HEAD_DIM)
    kt = jnp.transpose(k.reshape(t // tk, tk, nf, HEAD_DIM), (2, 0, 3, 1))

    def body(q_ref, k_ref, kt_ref, v_ref, cc_ref, cr_ref, o_ref, lse_ref, do_ref,
             dq_ref, dk_ref, dv_ref, dcq_ref, dck_ref):
        i = pl.program_id(1)

        @pl.when(i == 0)
        def _():
            dk_ref[...] = jnp.zeros_like(dk_ref)
            dv_ref[...] = jnp.zeros_like(dv_ref)
            dck_ref[...] = jnp.zeros_like(dck_ref)

        diff = lax.broadcasted_iota(jnp.int32, (tk, tq), 0) - lax.broadcasted_iota(jnp.int32, (tk, tq), 1)
        lane = lax.broadcasted_iota(jnp.int32, (tk, HEAD_DIM), 1)
        qs = [q_ref[:, lanes(h)] for h in range(hp)]
        dos = [do_ref[:, lanes(h)] for h in range(hp)]
        do_b = [d.astype(BF16) for d in dos]
        cqs = [cr_ref[h, i] for h in range(hp)]
        lses = [lse_ref[h, 0] for h in range(hp)]
        deltas = [jnp.sum(jnp.transpose(dos[h] * o_ref[:, lanes(h)]), axis=0, keepdims=True) for h in range(hp)]

        def products(h, j):
            ks = pl.ds(pl.multiple_of(j * tk, tk), tk)
            nt = (((1,), (1,)), ((), ()))
            return (lax.dot_general(k_ref[ks, lanes(h)], qs[h], nt, preferred_element_type=F32),
                    lax.dot_general(v_ref[ks, lanes(h)], do_b[h], nt, preferred_element_type=F32))

        def tile(h, j, dqt, dcq, s, dp, masked):
            ks = pl.ds(pl.multiple_of(j * tk, tk), tk)
            pr = jnp.exp(s * scale + cqs[h] - cc_ref[0, ks, h:h + 1] - lses[h])
            if masked:
                pr = jnp.where(diff <= 0, pr, 0.0)
            ds = pr * (dp - deltas[h])
            ds_b = ds.astype(BF16)
            dqt = dqt + jnp.dot(kt_ref[h, j], ds_b, preferred_element_type=F32)
            dk_ref[ks, lanes(h)] += jnp.dot(ds_b, qs[h], preferred_element_type=F32) * scale
            dv_ref[ks, lanes(h)] += jnp.dot(pr.astype(BF16), do_b[h], preferred_element_type=F32)
            dck_ref[0, ks, :] -= jnp.where(lane == h, jnp.sum(ds, axis=1, keepdims=True), 0.0)
            return dqt, dcq + jnp.sum(ds, axis=0, keepdims=True)

        def step(j, carry):
            nxt = [products(h, j + 1) for h in range(hp)]
            return tuple(tile(h, j, *carry[h], False) + nxt[h] for h in range(hp))

        init = tuple((jnp.zeros((HEAD_DIM, tq), F32), jnp.zeros((1, tq), F32)) + products(h, 0) for h in range(hp))
        carry = lax.fori_loop(0, i, step, init)
        for h in range(hp):
            dqt, dcq = tile(h, i, *carry[h], True)
            dq_ref[:, lanes(h)] = jnp.transpose(dqt) * scale
            dcq_ref[h, 0] = dcq

    w = hp * HEAD_DIM
    head_all = pl.BlockSpec((t, w), lambda h, i: (0, h))
    qblk = pl.BlockSpec((tq, w), lambda h, i: (i, h))
    colv = pl.BlockSpec((1, t, HEAD_DIM), lambda h, i: (h, 0, 0))
    rows_all = pl.BlockSpec((hp, t // tk, 1, tk), lambda h, i: (h, 0, 0, 0))
    row_blk = pl.BlockSpec((hp, 1, 1, tq), lambda h, i: (h, i, 0, 0))
    wide = jax.ShapeDtypeStruct((t, nf * HEAD_DIM), F32)
    return pl.pallas_call(
        body, name="fox_bwd", grid=(nf // hp, t // tq),
        in_specs=[qblk, head_all, pl.BlockSpec((hp, t // tk, HEAD_DIM, tk), lambda h, i: (h, 0, 0, 0)), head_all,
                  colv, rows_all, qblk, row_blk, qblk],
        out_specs=[qblk, head_all, head_all, row_blk, colv],
        out_shape=[wide, wide, wide, jax.ShapeDtypeStruct((nf, t // tq, 1, tq), F32),
                   jax.ShapeDtypeStruct((nf // hp, t, HEAD_DIM), F32)],
        compiler_params=_cparams(("parallel", "arbitrary")),
    )(q, k, kt, v, cc, cr, o, lse, dmix)


def _mem_fn(mq, mk, mv, gq, gk):
    qn = _rms_fn(mq, gq)
    kn = _rms_fn(mk, gk)
    s = _nt(qn, kn) * (HEAD_DIM ** -0.5)
    e = jnp.exp(s - lax.stop_gradient(jnp.max(s, axis=1, keepdims=True)))
    pr = e / jnp.sum(e, axis=1, keepdims=True)
    return _nn(pr, mv)


def _mem_specs(t, m, tq, qoff):
    qblk = pl.BlockSpec((tq, HEAD_DIM), lambda h, i: (i, qoff + h))
    kblk = pl.BlockSpec((m, HEAD_DIM), lambda h, i: (0, h))
    vblk = pl.BlockSpec((m, HEAD_DIM), lambda h, i: (0, N_MEM_HEADS + h))
    row = pl.BlockSpec((1, HEAD_DIM), lambda h, i: (0, 0))
    return qblk, kblk, vblk, row


def _mem_fwd(p, qoff, mkv, gq, gk, tq, into, into_off):
    t, m = p.shape[0], mkv.shape[0]
    qblk, kblk, vblk, row = _mem_specs(t, m, tq, qoff)

    def body(q_ref, k_ref, v_ref, gq_ref, gk_ref, _, o_ref):
        o_ref[...] = _mem_fn(q_ref[...], k_ref[...], v_ref[...], gq_ref[...], gk_ref[...]).astype(BF16)

    return pl.pallas_call(
        body, name="mem_fwd", grid=(N_MEM_HEADS, t // tq),
        in_specs=[qblk, kblk, vblk, row, row, pl.BlockSpec(memory_space=pl.ANY)],
        out_specs=pl.BlockSpec((tq, HEAD_DIM), lambda h, i: (i, into_off + h)),
        out_shape=jax.ShapeDtypeStruct(into.shape, BF16), input_output_aliases={5: 0},
        compiler_params=_cparams(("parallel", "parallel")),
    )(p, mkv, mkv, gq, gk, into)


def _mem_bwd(p, qoff, mkv, gq, gk, dmix, dooff, tq):
    t, m = p.shape[0], mkv.shape[0]
    qblk, kblk, vblk, row = _mem_specs(t, m, tq, qoff)

    def body(q_ref, k_ref, v_ref, gq_ref, gk_ref, do_ref, dq_ref, dkv_k_ref, dkv_v_ref, dgq_ref, dgk_ref):
        h, i = pl.program_id(0), pl.program_id(1)

        @pl.when((h == 0) & (i == 0))
        def _():
            dgq_ref[...] = jnp.zeros_like(dgq_ref)
            dgk_ref[...] = jnp.zeros_like(dgk_ref)

        @pl.when(i == 0)
        def _():
            dkv_k_ref[...] = jnp.zeros_like(dkv_k_ref)
            dkv_v_ref[...] = jnp.zeros_like(dkv_v_ref)

        _, vjp = jax.vjp(_mem_fn, q_ref[...], k_ref[...], v_ref[...], gq_ref[...], gk_ref[...])
        dq, dk, dv, dgq, dgk = vjp(do_ref[...])
        dq_ref[...] = dq
        dkv_k_ref[...] += dk
        dkv_v_ref[...] += dv
        dgq_ref[...] += dgq
        dgk_ref[...] += dgk

    oblk = pl.BlockSpec((tq, HEAD_DIM), lambda h, i: (i, h))
    kout = pl.BlockSpec((m, HEAD_DIM), lambda h, i: (0, h))
    half = jax.ShapeDtypeStruct((m, N_MEM_HEADS * HEAD_DIM), F32)
    rshape = jax.ShapeDtypeStruct((1, HEAD_DIM), F32)
    return pl.pallas_call(
        body, name="mem_bwd", grid=(N_MEM_HEADS, t // tq),
        in_specs=[qblk, kblk, vblk, row, row, pl.BlockSpec((tq, HEAD_DIM), lambda h, i: (i, dooff + h))],
        out_specs=[oblk, kout, kout, row, row],
        out_shape=[jax.ShapeDtypeStruct((t, N_MEM_HEADS * HEAD_DIM), F32), half, half, rshape, rshape],
        compiler_params=_cparams(("arbitrary", "arbitrary")),
    )(p, mkv, mkv, gq, gk, dmix)


def _shift_down(x, s):
    if s == 0:
        return x
    r = lax.broadcasted_iota(jnp.int32, x.shape, 0)
    return jnp.where(r >= s, pltpu.roll(x, s, 0), 0.0)


def _shift_up(x, s):
    if s == 0:
        return x
    n = x.shape[0]
    r = lax.broadcasted_iota(jnp.int32, x.shape, 0)
    return jnp.where(r < n - s, pltpu.roll(x, n - s, 0), 0.0)


def _conv_fn(x0, x1, x2, x3, w0, w1, w2, w3, kind):
    y = _silu(x0 * w0 + x1 * w1 + x2 * w2 + x3 * w3)
    if kind == 2:
        return y
    y = y * lax.rsqrt(jnp.sum(y * y, axis=-1, keepdims=True) + NORM_EPS)
    return y * (HEAD_DIM ** -0.5) if kind == 0 else y


def _conv_fwd(p, off, conv_w, ng):
    t = p.shape[0]

    def body(x_ref, w_ref, o_ref):
        kind = pl.program_id(0) // ng
        x = x_ref[...]
        xs = [_shift_down(x, CONV_WIDTH - 1 - j) for j in range(CONV_WIDTH)]
        ws = [w_ref[j:j + 1, :] for j in range(CONV_WIDTH)]
        for kd in range(3):
            @pl.when(kind == kd)
            def _(kd=kd):
                o_ref[...] = _conv_fn(*xs, *ws, kd)

    return pl.pallas_call(
        body, name="gdn_conv_fwd", grid=(3 * ng,),
        in_specs=[pl.BlockSpec((t, HEAD_DIM), lambda c: (0, off + c)),
                  pl.BlockSpec((CONV_WIDTH, HEAD_DIM), lambda c: (0, c))],
        out_specs=pl.BlockSpec((t, HEAD_DIM), lambda c: (0, c)),
        out_shape=jax.ShapeDtypeStruct((t, 3 * ng * HEAD_DIM), F32),
        compiler_params=_cparams(("parallel",)),
    )(p, conv_w)


def _conv_bwd(p, off, conv_w, dys, ng):
    t = p.shape[0]

    def body(x_ref, w_ref, dq_ref, dk_ref, dv_ref, dx_ref, dw_ref):
        kind = pl.program_id(0) // ng
        dy_refs = (dq_ref, dk_ref, dv_ref)
        x = x_ref[...]
        xs = [_shift_down(x, CONV_WIDTH - 1 - j) for j in range(CONV_WIDTH)]
        ws = [w_ref[j:j + 1, :] for j in range(CONV_WIDTH)]
        for kd in range(3):
            @pl.when(kind == kd)
            def _(kd=kd):
                _, vjp = jax.vjp(functools.partial(_conv_fn, kind=kd), *xs, *ws)
                g = vjp(dy_refs[kd][...])
                dx = _shift_up(g[0], CONV_WIDTH - 1)
                for j in range(1, CONV_WIDTH):
                    dx = dx + _shift_up(g[j], CONV_WIDTH - 1 - j)
                dx_ref[...] = dx
                for j in range(CONV_WIDTH):
                    dw_ref[j:j + 1, :] = g[CONV_WIDTH + j]

    blk = pl.BlockSpec((t, HEAD_DIM), lambda c: (0, c))
    head = lambda k: pl.BlockSpec((t, HEAD_DIM), lambda c: (0, jnp.where(c // ng == k, c % ng, 0)))
    wblk = pl.BlockSpec((CONV_WIDTH, HEAD_DIM), lambda c: (0, c))
    return pl.pallas_call(
        body, name="gdn_conv_bwd", grid=(3 * ng,),
        in_specs=[pl.BlockSpec((t, HEAD_DIM), lambda c: (0, off + c)), wblk] + [head(k) for k in range(3)],
        out_specs=[blk, wblk],
        out_shape=[jax.ShapeDtypeStruct((t, 3 * ng * HEAD_DIM), F32),
                   jax.ShapeDtypeStruct((CONV_WIDTH, 3 * ng * HEAD_DIM), F32)],
        compiler_params=_cparams(("parallel",)),
    )(p, conv_w, *dys)


def _lower_inverse(lower):
    c = lower.shape[-1]
    r = lax.broadcasted_iota(jnp.int32, (1, c, c), 1)
    e = lax.broadcasted_iota(jnp.int32, (1, c, c), 2)
    hi = lax.Precision.HIGH
    inv = jnp.where(r == e, 1.0, 0.0) - lower
    pw = lower
    for _ in range(int(math.log2(c)) - 1):
        pw = _dot(pw, pw, ((1,), (0,)), hi)
        inv = inv + _dot(inv, pw, ((1,), (0,)), hi)
    return inv


@jax.custom_vjp
def _solve(lower, inv, vb, kbg):
    hi = lax.Precision.HIGH
    return _dot(inv, vb, ((1,), (0,)), hi), _dot(inv, kbg, ((1,), (0,)), hi)


def _solve_fwd(lower, inv, vb, kbg):
    u, w = _solve(lower, inv, vb, kbg)
    return (u, w), (inv, u, w)


def _solve_bwd(res, cts):
    inv, u, w = res
    dvb, dkbg = _tn(inv, cts[0]), _tn(inv, cts[1])
    return -(_nt(dvb, u) + _nt(dkbg, w)), jnp.zeros_like(inv), dvb, dkbg


_solve.defvjp(_solve_fwd, _solve_bwd)


def _wy_fn(q, k, v, gcol, grow, bcol, inv=None):
    b, c, dk = q.shape
    r = lax.broadcasted_iota(jnp.int32, (1, c, c), 1)
    e = lax.broadcasted_iota(jnp.int32, (1, c, c), 2)
    tril, strict = e <= r, e < r
    gc_col = jnp.sum(jnp.where(tril, grow, 0.0), axis=2, keepdims=True)
    gc_row = jnp.sum(jnp.where(r <= e, gcol, 0.0), axis=1, keepdims=True)
    g_last = jnp.sum(gcol, axis=1, keepdims=True)
    decay = jnp.exp(jnp.where(tril, gc_col - gc_row, NEG))
    kb, vb = k * bcol, v * bcol
    lower = jnp.where(strict, _nt(kb, k) * decay, 0.0)
    if inv is None:
        inv = _lower_inverse(lower)
    u, w = _solve(lower, inv, vb, kb * jnp.exp(gc_col))
    attn = jnp.where(tril, _nt(q, k) * decay, 0.0)
    qg = q * jnp.exp(gc_col)
    kdec = k * jnp.exp(g_last - gc_col)
    egl = jnp.broadcast_to(jnp.exp(g_last), (b, 1, dk))
    return u, w, qg, kdec, attn, egl, inv


def _scan_fn(u, w, qg, kdec, attn, egl, state):
    v_new = u - _nn(w, state)
    o = _nn(qg, state) + _nn(attn, v_new)
    return o, state * egl + _tn(kdec, v_new)


GDN_CHUNKS_PER_STEP = 4
GDN_SCAN_CHUNKS = 2


def _gdn_fwd(qkv, vals, grow, nf, ng):
    t = qkv.shape[0]
    nch = t // CHUNK

    cb = GDN_CHUNKS_PER_STEP
    *wy, inv = _gdn_wy(qkv, vals, grow, nf, ng, cb)

    sc = GDN_SCAN_CHUNKS

    def body(u_ref, w_ref, qg_ref, kd_ref, at_ref, eg_ref, o_ref, st_ref, state):
        @pl.when(pl.program_id(0) == 0)
        def _():
            state[...] = jnp.zeros_like(state)

        for c in range(sc):
            rows = slice(c * CHUNK, (c + 1) * CHUNK)
            heads = lambda ref: jnp.stack([ref[rows, h * HEAD_DIM:(h + 1) * HEAD_DIM] for h in range(ng)])
            st_ref[:, c] = state[...]
            o, new = _scan_fn(heads(u_ref), heads(w_ref), heads(qg_ref), heads(kd_ref), at_ref[:, c], eg_ref[:, c],
                              state[...])
            for h in range(ng):
                o_ref[rows, h * HEAD_DIM:(h + 1) * HEAD_DIM] = o[h]
            state[...] = new

    w = ng * HEAD_DIM
    blk = pl.BlockSpec((sc * CHUNK, w), lambda i: (i, 0))
    o, states = pl.pallas_call(
        body, name="gdn_scan_fwd", grid=(nch // sc,),
        in_specs=[blk, blk, blk, blk, pl.BlockSpec((ng, sc, CHUNK, CHUNK), lambda i: (0, i, 0, 0)),
                  pl.BlockSpec((ng, sc, 1, HEAD_DIM), lambda i: (0, i, 0, 0))],
        out_specs=[blk, pl.BlockSpec((ng, sc, HEAD_DIM, HEAD_DIM), lambda i: (0, i, 0, 0))],
        out_shape=[jax.ShapeDtypeStruct((t, w), F32),
                   jax.ShapeDtypeStruct((ng, nch, HEAD_DIM, HEAD_DIM), F32)],
        scratch_shapes=[pltpu.VMEM((ng, HEAD_DIM, HEAD_DIM), F32)],
        compiler_params=_cparams(("arbitrary",)),
    )(*wy)
    return o, (wy, inv, states)


def _wy_batch(q_ref, k_ref, v_ref, vals_ref, gr_ref, nf, ng, cb):
    idx = [(c, h) for c in range(cb) for h in range(ng)]
    rows = lambda c: slice(c * CHUNK, (c + 1) * CHUNK)
    lanes = lambda h: slice(h * HEAD_DIM, (h + 1) * HEAD_DIM)
    wide = lambda ref: jnp.stack([ref[rows(c), lanes(h)] for c, h in idx])
    col = lambda lane0: jnp.stack([vals_ref[rows(c), lane0 + h:lane0 + h + 1] for c, h in idx])
    return idx, (wide(q_ref), wide(k_ref), wide(v_ref), col(nf), jnp.stack([gr_ref[h, c] for c, h in idx]),
                 col(nf + ng))


def _gdn_wy(qkv, vals, grow, nf, ng, cb):
    t = qkv.shape[0]
    nch = t // CHUNK

    def body(q_ref, k_ref, v_ref, vals_ref, gr_ref, u_ref, w_ref, qg_ref, kd_ref, at_ref, eg_ref, inv_ref):
        idx, args = _wy_batch(q_ref, k_ref, v_ref, vals_ref, gr_ref, nf, ng, cb)
        u, w, qg, kd, at, eg, inv = _wy_fn(*args)
        for b, (c, h) in enumerate(idx):
            rows, lanes = slice(c * CHUNK, (c + 1) * CHUNK), slice(h * HEAD_DIM, (h + 1) * HEAD_DIM)
            u_ref[rows, lanes] = u[b]
            w_ref[rows, lanes] = w[b]
            qg_ref[rows, lanes] = qg[b]
            kd_ref[rows, lanes] = kd[b]
            at_ref[h, c] = at[b]
            eg_ref[h, c] = eg[b]
            inv_ref[h, c] = inv[b]

    wd = ng * HEAD_DIM
    blk = lambda o: pl.BlockSpec((cb * CHUNK, wd), lambda i: (i, o))
    col = pl.BlockSpec((cb * CHUNK, HEAD_DIM), lambda i: (i, 0))
    sq = pl.BlockSpec((ng, cb, CHUNK, CHUNK), lambda i: (0, i, 0, 0))
    wide = jax.ShapeDtypeStruct((t, wd), F32)
    sq_shape = jax.ShapeDtypeStruct((ng, nch, CHUNK, CHUNK), F32)
    return pl.pallas_call(
        body, name="gdn_wy_fwd", grid=(nch // cb,),
        in_specs=[blk(0), blk(1), blk(2), col, pl.BlockSpec((ng, cb, 1, CHUNK), lambda i: (0, i, 0, 0))],
        out_specs=[blk(0), blk(0), blk(0), blk(0), sq, pl.BlockSpec((ng, cb, 1, HEAD_DIM), lambda i: (0, i, 0, 0)),
                   sq],
        out_shape=[wide, wide, wide, wide, sq_shape, jax.ShapeDtypeStruct((ng, nch, 1, HEAD_DIM), F32), sq_shape],
        compiler_params=_cparams(("parallel",)),
    )(qkv, qkv, qkv, vals, grow)


def _gdn_bwd(qkv, vals, grow, saved, do, nf, ng):
    t = qkv.shape[0]
    nch = t // CHUNK
    cb = GDN_CHUNKS_PER_STEP // 2
    wy, inv, states = saved
    wd = ng * HEAD_DIM

    def scan_body(u_ref, w_ref, qg_ref, kd_ref, at_ref, eg_ref, st_ref, do_ref,
                  du_ref, dw_ref, dqg_ref, dkd_ref, dat_ref, deg_ref, dstate):
        @pl.when(pl.program_id(0) == 0)
        def _():
            dstate[...] = jnp.zeros_like(dstate)

        for c in reversed(range(sc)):
            rows = slice(c * CHUNK, (c + 1) * CHUNK)
            heads = lambda ref: jnp.stack([ref[rows, h * HEAD_DIM:(h + 1) * HEAD_DIM] for h in range(ng)])
            _, vjp = jax.vjp(_scan_fn, heads(u_ref), heads(w_ref), heads(qg_ref), heads(kd_ref), at_ref[:, c],
                             eg_ref[:, c], st_ref[:, c])
            du, dw, dqg, dkd, dat, deg, dst = vjp((heads(do_ref), dstate[...]))
            for h in range(ng):
                lanes = slice(h * HEAD_DIM, (h + 1) * HEAD_DIM)
                du_ref[rows, lanes] = du[h]
                dw_ref[rows, lanes] = dw[h]
                dqg_ref[rows, lanes] = dqg[h]
                dkd_ref[rows, lanes] = dkd[h]
            dat_ref[:, c] = dat
            deg_ref[:, c] = deg
            dstate[...] = dst

    sc = GDN_SCAN_CHUNKS
    rev = lambda i: nch // sc - 1 - i
    blk = pl.BlockSpec((sc * CHUNK, wd), lambda i: (rev(i), 0))
    atb = pl.BlockSpec((ng, sc, CHUNK, CHUNK), lambda i: (0, rev(i), 0, 0))
    egb = pl.BlockSpec((ng, sc, 1, HEAD_DIM), lambda i: (0, rev(i), 0, 0))
    wide = jax.ShapeDtypeStruct((t, wd), F32)
    at_shape = jax.ShapeDtypeStruct((ng, nch, CHUNK, CHUNK), F32)
    eg_shape = jax.ShapeDtypeStruct((ng, nch, 1, HEAD_DIM), F32)
    dwy = pl.pallas_call(
        scan_body, name="gdn_scan_bwd", grid=(nch // sc,),
        in_specs=[blk, blk, blk, blk, atb, egb,
                  pl.BlockSpec((ng, sc, HEAD_DIM, HEAD_DIM), lambda i: (0, rev(i), 0, 0)), blk],
        out_specs=[blk, blk, blk, blk, atb, egb],
        out_shape=[wide, wide, wide, wide, at_shape, eg_shape],
        scratch_shapes=[pltpu.VMEM((ng, HEAD_DIM, HEAD_DIM), F32)],
        compiler_params=_cparams(("arbitrary",)),
    )(*wy, states, do)

    def wy_body(q_ref, k_ref, v_ref, vals_ref, gr_ref, du_ref, dw_ref, dqg_ref, dkd_ref, dat_ref, deg_ref,
                inv_ref, dq_ref, dk_ref, dv_ref, dvals_ref, dgr_ref):
        idx, args = _wy_batch(q_ref, k_ref, v_ref, vals_ref, gr_ref, nf, ng, cb)
        lane = lax.broadcasted_iota(jnp.int32, (CHUNK, HEAD_DIM), 1)
        kept = jnp.stack([inv_ref[h, c] for c, h in idx])
        rows = lambda c: slice(c * CHUNK, (c + 1) * CHUNK)
        lanes = lambda h: slice(h * HEAD_DIM, (h + 1) * HEAD_DIM)
        wide_ct = lambda ref: jnp.stack([ref[rows(c), lanes(h)] for c, h in idx])
        cts = (wide_ct(du_ref), wide_ct(dw_ref), wide_ct(dqg_ref), wide_ct(dkd_ref),
               jnp.stack([dat_ref[h, c] for c, h in idx]), jnp.stack([deg_ref[h, c] for c, h in idx]))
        _, vjp = jax.vjp(lambda *a: _wy_fn(*a, inv=kept)[:6], *args)
        dq, dk, dv, dgc, dgr, dbc = vjp(cts)
        for b, (c, h) in enumerate(idx):
            dq_ref[rows(c), lanes(h)] = dq[b]
            dk_ref[rows(c), lanes(h)] = dk[b]
            dv_ref[rows(c), lanes(h)] = dv[b]
            dgr_ref[h, c] = dgr[b]
        for c in range(cb):
            acc = jnp.zeros((CHUNK, HEAD_DIM), F32)
            for h in range(ng):
                acc = jnp.where(lane == nf + h, dgc[c * ng + h], acc)
                acc = jnp.where(lane == nf + ng + h, dbc[c * ng + h], acc)
            dvals_ref[rows(c), :] = acc

    cblk = lambda o: pl.BlockSpec((cb * CHUNK, wd), lambda i: (i, o))
    col = pl.BlockSpec((cb * CHUNK, HEAD_DIM), lambda i: (i, 0))
    rowv = pl.BlockSpec((ng, cb, 1, CHUNK), lambda i: (0, i, 0, 0))
    return pl.pallas_call(
        wy_body, name="gdn_wy_bwd", grid=(nch // cb,),
        in_specs=[cblk(0), cblk(1), cblk(2), col, rowv, cblk(0), cblk(0), cblk(0), cblk(0),
                  pl.BlockSpec((ng, cb, CHUNK, CHUNK), lambda i: (0, i, 0, 0)),
                  pl.BlockSpec((ng, cb, 1, HEAD_DIM), lambda i: (0, i, 0, 0)),
                  pl.BlockSpec((ng, cb, CHUNK, CHUNK), lambda i: (0, i, 0, 0))],
        out_specs=[cblk(0), cblk(0), cblk(0), col, rowv],
        out_shape=[wide, wide, wide, jax.ShapeDtypeStruct((t, HEAD_DIM), F32),
                   jax.ShapeDtypeStruct((ng, nch, 1, CHUNK), F32)],
        compiler_params=_cparams(("parallel",)),
    )(qkv, qkv, qkv, vals, grow, *dwy, inv)


def _swiglu_fn(gate, up):
    return _silu(gate) * up


FFN_TN = 512


def _ffn_up(n2, wgu4):
    _, d, w = wgu4.shape
    t = n2.shape[0]
    tn = _tile(w, FFN_TN)
    nb = w // tn

    def body(a_ref, b_ref, gu_ref, act_ref):
        av = a_ref[...]
        gate = jnp.dot(av, b_ref[0], preferred_element_type=F32)
        up = jnp.dot(av, b_ref[1], preferred_element_type=F32)
        gu_ref[0] = gate.astype(BF16)
        gu_ref[1] = up.astype(BF16)
        act_ref[...] = _swiglu_fn(gate, up).astype(BF16)

    return pl.pallas_call(
        body, name="ffn_up", grid=(2, nb),
        in_specs=[pl.BlockSpec((t, d), lambda j, l: (0, 0)), pl.BlockSpec((2, d, tn), lambda j, l: (j, 0, l))],
        out_specs=[pl.BlockSpec((2, t, tn), lambda j, l: (j, 0, l)),
                   pl.BlockSpec((t, tn), lambda j, l: (0, j * nb + l))],
        out_shape=[jax.ShapeDtypeStruct((4, t, w), BF16), jax.ShapeDtypeStruct((t, 2 * w), BF16)],
        compiler_params=_cparams(("parallel", "parallel")),
    )(n2, wgu4)


def _ffn_dact(dh2, wd, gu, after):
    _, t, w = gu.shape
    d = dh2.shape[1]
    tn = _tile(w, FFN_TN)
    nb = w // tn

    def body(a_ref, b_ref, gu_ref, _, o_ref):
        dact = lax.dot_general(a_ref[...], b_ref[...], (((1,), (1,)), ((), ())), preferred_element_type=F32)
        _, vjp = jax.vjp(_swiglu_fn, gu_ref[0].astype(F32), gu_ref[1].astype(F32))
        dg, du = vjp(dact)
        o_ref[0] = dg.astype(BF16)
        o_ref[1] = du.astype(BF16)

    pair = pl.BlockSpec((2, t, tn), lambda j, l: (j, 0, l))
    return pl.pallas_call(
        body, name="ffn_dact", grid=(2, nb),
        in_specs=[pl.BlockSpec((t, d), lambda j, l: (0, 0)), pl.BlockSpec((tn, d), lambda j, l: (j * nb + l, 0)),
                  pair, pl.BlockSpec(after.shape, lambda j, l: (0, 0))],
        out_specs=pair, out_shape=jax.ShapeDtypeStruct(gu.shape, BF16),
        compiler_params=_cparams(("parallel", "parallel")),
    )(dh2, wd, gu, after)


def _loss_head(h2, target):
    t, d = h2.shape
    tr = _tile(t, 256, 8)

    def body(h_ref, t_ref, l_ref, d_ref, db_ref):
        @pl.when(pl.program_id(0) == 0)
        def _():
            l_ref[...] = jnp.zeros_like(l_ref)

        err = h_ref[...] - t_ref[...]
        d_ref[...] = err * (1.0 / d)
        db_ref[...] = (err * (1.0 / d)).astype(BF16)
        part = 0.5 * jnp.sum(jnp.mean(err * err, axis=-1, keepdims=True), axis=0, keepdims=True)
        lane = lax.broadcasted_iota(jnp.int32, (8, HEAD_DIM), 1)
        row = lax.broadcasted_iota(jnp.int32, (8, HEAD_DIM), 0)
        l_ref[...] += jnp.where((lane == 0) & (row == 0), part, 0.0)

    blk = pl.BlockSpec((tr, d), lambda r: (r, 0))
    return pl.pallas_call(
        body, name="loss_head", grid=(t // tr,), in_specs=[blk, blk],
        out_specs=[pl.BlockSpec((8, HEAD_DIM), lambda r: (0, 0)), blk, blk],
        out_shape=[jax.ShapeDtypeStruct((8, HEAD_DIM), F32), jax.ShapeDtypeStruct((t, d), F32),
                   jax.ShapeDtypeStruct((t, d), BF16)],
        compiler_params=_cparams(("arbitrary",)),
    )(h2, target)


def _adamw(w, g, m, v, *, g_fn=None, name):
    r, c = w.shape
    tr = _tile(r, max(8, (1 << 19) // c // 8 * 8), 8)
    gs = g if isinstance(g, tuple) else (g,)

    def body(w_ref, *refs):
        g_refs, (m_ref, v_ref, go_ref, d_ref, mo_ref, vo_ref) = refs[:len(gs)], refs[len(gs):]
        gr = g_refs[0][...] if g_fn is None else g_fn(*[ref[...] for ref in g_refs])
        mn = ADAM_B1 * m_ref[...] + (1.0 - ADAM_B1) * gr
        vn = ADAM_B2 * v_ref[...] + (1.0 - ADAM_B2) * (gr * gr)
        m_hat = mn / (1.0 - ADAM_B1 ** ADAM_STEP)
        v_hat = vn / (1.0 - ADAM_B2 ** ADAM_STEP)
        go_ref[...] = gr
        d_ref[...] = -ADAM_LR * (m_hat / (jnp.sqrt(v_hat) + ADAM_EPS) + ADAM_WD * w_ref[...])
        mo_ref[...] = mn
        vo_ref[...] = vn

    blk = pl.BlockSpec((tr, c), lambda i: (i, 0))
    gblks = [pl.BlockSpec((tr, gi.shape[1]), lambda i: (i, 0)) for gi in gs]
    return pl.pallas_call(
        body, name=name, grid=(r // tr,), in_specs=[blk] + gblks + [blk, blk], out_specs=[blk] * 4,
        out_shape=[jax.ShapeDtypeStruct((r, c), F32)] * 4,
        compiler_params=_cparams(("parallel",)),
    )(w, *gs, m, v)


class _Layout:
    def __init__(self, d):
        nh = d // HEAD_DIM
        self.nm = N_MEM_HEADS
        self.nf = (nh - self.nm) // 2
        self.ng = nh - self.nm - self.nf
        nf, ng, nm, hd = self.nf, self.ng, self.nm, HEAD_DIM
        self.o_fq, self.o_fk, self.o_fv, self.o_sm = 0, nf, 2 * nf, 3 * nf
        self.o_gq, self.o_gz, self.o_mq = 0, 3 * ng, 4 * ng
        self.cols_a = -(-(3 * nf + 1) // 4) * 4 * hd
        self.cols_b = -(-(4 * ng + nm) // 4) * 4 * hd
        self.cols = self.cols_a + self.cols_b
        sizes = [nf * hd, nf * hd, nf * hd, nf, 3 * ng * hd, ng * hd, ng, ng, nm * hd]
        starts = [sum(sizes[:i]) for i in range(len(sizes))]
        self.ref = list(zip(starts, sizes))
        self.in_cols = sum(sizes)

    def regroup(self, w):
        part = lambda i: w[:, self.ref[i][0]:self.ref[i][0] + self.ref[i][1]]
        a = [part(0), part(1), part(2), part(3), part(6), part(7)]
        b = [part(4), part(5), part(8)]
        pads = [self.cols_a - sum(p.shape[1] for p in a), self.cols_b - sum(p.shape[1] for p in b)]
        fill = [[jnp.zeros((w.shape[0], n), w.dtype)] if n else [] for n in pads]
        return jnp.concatenate(a + fill[0] + b + fill[1], axis=1)

    def ungroup(self, g):
        hd, nf, ng, nm = HEAD_DIM, self.nf, self.ng, self.nm
        sm, b0 = self.o_sm * hd, self.cols_a
        return jnp.concatenate([
            g[:, :3 * nf * hd], g[:, sm:sm + nf], g[:, b0:b0 + 3 * ng * hd],
            g[:, b0 + self.o_gz * hd:b0 + self.o_mq * hd], g[:, sm + nf:sm + nf + ng],
            g[:, sm + nf + ng:sm + nf + 2 * ng], g[:, b0 + self.o_mq * hd:b0 + (self.o_mq + nm) * hd]], axis=1)


def _lane_row(pieces):
    row = jnp.zeros((1, HEAD_DIM), F32)
    for off, a in pieces:
        row = lax.dynamic_update_slice(row, a.astype(F32), (0, off))
    return row


def _local_step(x, mem, target, prefetch, weights, reducer, sp):
    t, d = x.shape
    lay = _Layout(d)
    nf, ng, nm, hd = lay.nf, lay.ng, lay.nm, HEAD_DIM
    nch = t // CHUNK
    tq = _tile(t, 256)
    tk = tq

    u = _norm_fwd(x, 0, sp["norm_mix"], 1, d, BF16, name="norm_mix_fwd")
    prefetch("in_a", u)
    (win_a,) = weights("in_a", u)
    p_a = _mm(u, win_a, name="mm_in_a")
    pa = _lane_row([(nf, sp["gdn_a_log"])])
    pb = _lane_row([(0, sp["fox_f_bias"]), (nf, sp["gdn_dt_bias"])])
    vals, csum = _small_fwd(p_a, lay.o_sm, pa, pb, nf, ng)

    c_t = csum[:, :nf].T
    hp = _fox_heads(nf, 3)
    cr = c_t.reshape(nf, t // tk, 1, tk)
    cc = jnp.stack([jnp.pad(csum[:, g * hp:(g + 1) * hp], ((0, 0), (0, hd - hp))) for g in range(nf // hp)])
    fq = _norm_fwd(p_a, lay.o_fq, sp["fox_q_norm"], nf, hd, BF16, name="fox_qnorm_fwd")
    fk = _norm_fwd(p_a, lay.o_fk, sp["fox_k_norm"], nf, hd, BF16, name="fox_knorm_fwd")
    fv = p_a[:, lay.o_fv * hd:(lay.o_fv + nf) * hd].astype(BF16)
    o_fox, lse, mix = _fox_fwd(fq, fk, fv, cc, cr, nf, tq, tk, d)

    prefetch("in_b", lse)
    (win_b,) = weights("in_b", lse)
    prefetch("mixer", win_b)
    p = _mm(u, win_b, name="mm_in_b")
    wmkv, conv_taps = weights("mixer", p)
    sp = dict(sp, gdn_conv=conv_taps)
    qkv = _conv_fwd(p, lay.o_gq, sp["gdn_conv"], ng)
    grow = vals[:, nf:nf + ng].T.reshape(ng, nch, 1, CHUNK)
    o_g, states = _gdn_fwd(qkv, vals, grow, nf, ng)
    mix = _norm_fwd(o_g, 0, sp["gdn_out_norm"], ng, hd, BF16, z=p, zoff=lay.o_gz, into=mix, into_off=nf,
                    name="gdn_out_fwd")
    prefetch("out", mix)

    mem_n = _norm_fwd(mem, 0, sp["mem_norm"], 1, d, BF16, name="mem_norm_fwd")
    mkv = _mm(mem_n, wmkv, name="mm_memkv")
    tq_mem = _tile(t, 1024)
    mix = _mem_fwd(p, lay.o_mq, mkv, sp["mem_q_norm"], sp["mem_k_norm"], tq_mem, mix, nf + ng)
    prefetch("gate_up", mix)
    (wout,) = weights("out", mix)
    h1 = _mm(mix, wout, res=x, name="mm_out")
    n2 = _norm_fwd(h1, 0, sp["norm_ffn"], 1, d, BF16, name="norm_ffn_fwd")
    (wgu,) = weights("gate_up", n2)
    wgu4 = wgu.reshape(4, d, -1)
    gu, act = _ffn_up(n2, wgu4)
    prefetch("down", act)
    (wd,) = weights("down", act)
    h2 = _mm(act, wd, res=h1, name="mm_down")
    loss_blk, dh2, dh2_b = _loss_head(h2, target)

    g = {}
    token = reducer.pair("w_down", _mm(act, dh2_b, ta=True, out_dtype=BF16, name="mm_dw_down"))
    dgu = _ffn_dact(dh2_b, wd, gu, token)
    dw_gate_up = _mm(n2, dgu, ta=True, stack="out", out_dtype=BF16, name="mm_dw_gate_up").reshape(wgu.shape)
    token = reducer.pair("w_gate_up", dw_gate_up)
    dn2 = _mm(dgu, wgu4, tb=True, stack="sum", after=token, name="mm_dn2")
    token = reducer.ship("ffn", ["w_down", "w_gate_up"], dn2)
    dh1, g["norm_ffn"] = _norm_bwd(h1, 0, sp["norm_ffn"] + token[0, 0], dn2, 0, 1, d, res=dh2,
                                   name="norm_ffn_bwd")
    token = reducer.pair("w_out", _mm(mix, dh1, ta=True, out_dtype=BF16, name="mm_dw_out"))
    dmix = _mm(dh1, wout, tb=True, after=token, name="mm_dmix")

    dmq, dmk, dmv, g["mem_q_norm"], g["mem_k_norm"] = _mem_bwd(
        p, lay.o_mq, mkv, sp["mem_q_norm"], sp["mem_k_norm"], dmix, nf + ng, tq_mem)
    dmkv = jnp.concatenate([dmk, dmv], axis=1)
    token = reducer.pair("w_mem_kv", _mm(mem_n, dmkv, ta=True, out_dtype=BF16, name="mm_dw_memkv"))
    dmem_n = _mm(dmkv, wmkv, tb=True, after=token, name="mm_dmem")
    token = reducer.ship("mix", ["w_out", "w_mem_kv"], dmem_n)
    _, g["mem_norm"] = _norm_bwd(mem, 0, sp["mem_norm"], dmem_n, 0, 1, d, name="mem_norm_bwd")

    do_g, dgz, g["gdn_out_norm"] = _norm_bwd(o_g, 0, sp["gdn_out_norm"] + token[0, 0], dmix, nf, ng, hd, z=p,
                                             zoff=lay.o_gz, name="gdn_out_bwd")
    dq, dk, dv, dvals, dgr = _gdn_bwd(qkv, vals, grow, states, do_g, nf, ng)
    dgqkv, g["gdn_conv"] = _conv_bwd(p, lay.o_gq, sp["gdn_conv"], (dq, dk, dv), ng)

    dfq_n, dfk_n, dfv, dcc, dcr = _fox_bwd(fq, fk, fv, cc, cr, o_fox, lse, dmix, nf, tq, tk)
    dfq, g["fox_q_norm"] = _norm_bwd(p_a, lay.o_fq, sp["fox_q_norm"], dfq_n, 0, nf, hd, name="fox_qnorm_bwd")
    dfk, g["fox_k_norm"] = _norm_bwd(p_a, lay.o_fk, sp["fox_k_norm"], dfk_n, 0, nf, hd, name="fox_knorm_bwd")
    dc = dcc.reshape(nf, t).T + jnp.concatenate([dcr[g, :, :hp] for g in range(nf // hp)], axis=1)

    dvals = dvals + jnp.pad(dgr.reshape(ng, t).T, ((0, 0), (nf, hd - nf - ng)))
    dcsum = jnp.pad(dc, ((0, 0), (0, hd - nf)))
    dsm, dpa, dpb = _small_bwd(p_a, lay.o_sm, pa, pb, dvals, dcsum, nf, ng)
    g["fox_f_bias"] = dpb[:, :nf]
    g["gdn_dt_bias"] = dpb[:, nf:nf + ng]
    g["gdn_a_log"] = dpa[:, nf:nf + ng]

    zeros = lambda n: jnp.zeros((t, n), F32)
    dp_a = jnp.concatenate([dfq, dfk, dfv, dsm, zeros(lay.cols_a - (lay.o_sm + 1) * hd)], axis=1).astype(BF16)
    dp_b = jnp.concatenate([dgqkv, dgz, dmq, zeros(lay.cols_b - (lay.o_mq + nm) * hd)], axis=1).astype(BF16)
    token = reducer.pair("w_in_a", _mm(u, dp_a, ta=True, out_dtype=BF16, name="mm_dw_in_a"))
    token = reducer.pair("w_in_b", _mm(u, dp_b, ta=True, out_dtype=BF16, after=token, name="mm_dw_in_b"))
    du = _mm(dp_a, win_a, tb=True, after=token, name="mm_du_a")
    token = reducer.ship("in", ["w_in_a", "w_in_b"], du)
    du = _mm(dp_b, win_b, tb=True, res=du, after=token, name="mm_du_b")
    dx, g["norm_mix"] = _norm_bwd(x, 0, sp["norm_mix"], du, 0, 1, d, res=dh1, name="norm_mix_bwd")
    return loss_blk, dx, g


ANY = pl.BlockSpec(memory_space=pl.ANY)


def _me():
    x, y, c = lax.axis_index("x"), lax.axis_index("y"), lax.axis_index("c")
    chips = [(1 - x, y), (x, 1 - y), (1 - x, 1 - y)]
    return x, y, c, chips


def _slot(axis, k):
    return k if axis == 0 else 2 * (k % 2) + k // 2


def _slab(ref, axis, rows, cols, k, h):
    half = rows // 2
    return ref.at[pl.ds(_slot(axis, k) * rows + h * half, half), :]


def _remote(src, dst, send_sem, recv_sem, dev):
    return pltpu.make_async_remote_copy(src_ref=src, dst_ref=dst, send_sem=send_sem, recv_sem=recv_sem,
                                        device_id=dev, device_id_type=MESH)


HBM = pl.BlockSpec(memory_space=pltpu.HBM)
SEM = pl.BlockSpec(memory_space=pltpu.SEMAPHORE)
SPLIT = pltpu.CompilerParams(has_side_effects=pltpu.SideEffectType.DATAFLOW_SIDE_EFFECTING)
TOKEN = jax.ShapeDtypeStruct((8, HEAD_DIM), F32)


def _in_hbm(v):
    return pltpu.with_memory_space_constraint(v, pltpu.HBM)


def _cast_place(shard, axis, name, col_fn=None, out_cols=None, after=None):
    r, c = shard.shape
    oc = out_cols or c
    tr = _tile(r, 512 if col_fn is None else 64, 16)
    tc = _tile(c, 2048) if col_fn is None else c
    otc = tc if col_fn is None else oc
    nb = r // tr
    chip = 2 * lax.axis_index("x") + lax.axis_index("y")
    slot = jnp.reshape(_slot(axis, chip), (1,)).astype(jnp.int32)

    def body(slot_ref, x_ref, *rest):
        x = x_ref[...]
        rest[-1][...] = (x if col_fn is None else col_fn(x)).astype(BF16)

    extra = [] if after is None else [after]
    return pl.pallas_call(
        body, name=name,
        grid_spec=pltpu.PrefetchScalarGridSpec(
            num_scalar_prefetch=1, grid=(nb, c // tc),
            in_specs=[pl.BlockSpec((tr, tc), lambda i, l, s: (i, l))] + [ANY] * len(extra),
            out_specs=pl.BlockSpec((tr, otc), lambda i, l, s: (s[0] * nb + i, l))),
        out_shape=jax.ShapeDtypeStruct((4 * r, oc), BF16),
        compiler_params=_cparams(("parallel", "parallel")),
    )(slot, shard, *extra)


def _gather_start(bufs, axes, shapes, groups, name):
    n = len(bufs)

    def body(*refs):
        dst = refs[n:2 * n]
        sems = refs[2 * n:2 * n + 2 * len(groups)]
        token = refs[-1]
        x, y, c, chips = _me()
        k = 2 * x + y
        for gi, ws in enumerate(groups):
            for i, w in enumerate(ws):
                r, cl = shapes[w]
                place = _slab(dst[w], axes[w], r, cl, k, c)
                for j, (px, py) in enumerate(chips):
                    _remote(place, place, sems[2 * gi].at[3 * i + j], sems[2 * gi + 1].at[3 * i + j],
                            (px, py, c)).start()
        token[...] = jnp.zeros_like(token)

    sem_shapes = [pltpu.SemaphoreType.DMA((3 * len(ws),)) for ws in groups for _ in range(2)]
    outs = pl.pallas_call(
        body, name=name, in_specs=[HBM] * n,
        out_specs=[HBM] * n + [SEM] * len(sem_shapes) + [pl.BlockSpec(memory_space=pltpu.VMEM)],
        out_shape=[pltpu.HBM(b.shape, b.dtype) for b in bufs] + sem_shapes + [TOKEN],
        input_output_aliases={w: w for w in range(n)}, compiler_params=SPLIT,
    )(*[_in_hbm(b) for b in bufs])
    sems = outs[n:-1]
    return outs[:n], [(sems[2 * g], sems[2 * g + 1]) for g in range(len(groups))], outs[-1]


def _gather_wait(bufs, axes, shapes, sems, after, name):
    n = len(bufs)

    def body(*refs):
        send_sems, recv_sems = refs[n], refs[n + 1]
        dst = refs[n + 3:]
        x, y, c, chips = _me()
        k = 2 * x + y
        for i in range(n):
            r, cl = shapes[i]
            for j, (px, py) in enumerate(chips):
                got = _slab(dst[i], axes[i], r, cl, 2 * px + py, c)
                _remote(got, got, send_sems.at[3 * i + j], recv_sems.at[3 * i + j], (px, py, c)).wait_recv()
        for i in range(n):
            r, cl = shapes[i]
            mine = _slab(dst[i], axes[i], r, cl, k, c)
            for j, (px, py) in enumerate(chips):
                _remote(mine, mine, send_sems.at[3 * i + j], recv_sems.at[3 * i + j], (px, py, c)).wait_send()

    return pl.pallas_call(
        body, name=name, in_specs=[HBM] * n + [SEM, SEM, ANY], out_specs=[HBM] * n,
        out_shape=[pltpu.HBM(b.shape, b.dtype) for b in bufs],
        input_output_aliases={i: i for i in range(n)}, compiler_params=SPLIT,
    )(*bufs, sems[0], sems[1], after)


def _split_start(name, arrays, geometry, count):
    n = len(arrays)

    def body(*refs):
        send, recv, token = refs[2 * n:]
        for i, (src, dst, _, dev) in enumerate(geometry(refs[n:2 * n])):
            _remote(src, dst, send.at[i], recv.at[i], dev).start()
        token[...] = jnp.zeros_like(token)

    sem = pltpu.SemaphoreType.DMA((count,))
    outs = pl.pallas_call(
        body, name=name, in_specs=[HBM] * n,
        out_specs=[HBM] * n + [SEM, SEM, pl.BlockSpec(memory_space=pltpu.VMEM)],
        out_shape=[pltpu.HBM(v.shape, v.dtype) for v in arrays] + [sem, sem, TOKEN],
        input_output_aliases={i: i for i in range(n)}, compiler_params=SPLIT,
    )(*[_in_hbm(v) for v in arrays])
    return list(outs[:n]), (outs[n], outs[n + 1]), outs[-1]


def _split_wait(name, arrays, sems, after, geometry):
    n = len(arrays)

    def body(*refs):
        send, recv = refs[n], refs[n + 1]
        copies = geometry(refs[n + 3:])
        for i, (_, _, land, dev) in enumerate(copies):
            _remote(land, land, send.at[i], recv.at[i], dev).wait_recv()
        for i, (src, _, _, dev) in enumerate(copies):
            _remote(src, src, send.at[i], recv.at[i], dev).wait_send()

    return list(pl.pallas_call(
        body, name=name, in_specs=[HBM] * n + [SEM, SEM, ANY], out_specs=[HBM] * n,
        out_shape=[pltpu.HBM(v.shape, v.dtype) for v in arrays],
        input_output_aliases={i: i for i in range(n)}, compiler_params=SPLIT,
    )(*arrays, sems[0], sems[1], after))


def _forward_geometry(axes, shapes):
    def geometry(bufs):
        x, y, c, chips = _me()
        out = []
        for i, buf in enumerate(bufs):
            r, cl = shapes[i]
            for px, py in chips:
                got = _slab(buf, axes[i], r, cl, 2 * px + py, c)
                out.append((got, got, _slab(buf, axes[i], r, cl, 2 * px + py, 1 - c), (x, y, 1 - c)))
        return out
    return geometry


def _pair_geometry(axes, shapes):
    def geometry(refs):
        n = len(refs) // 2
        x, y, c, _ = _me()
        out = []
        for w in range(n):
            r, cl = shapes[w]
            for j in range(4):
                land = refs[n + w].at[j]
                out.append((_slab(refs[w], axes[w], r, cl, j, 1 - c), land, land, (x, y, 1 - c)))
        return out
    return geometry


def _after_all(name, token, *arrays):
    def body(*refs):
        refs[-1][...] = jnp.zeros_like(refs[-1])

    return pl.pallas_call(
        body, name=name, in_specs=[ANY] * (1 + len(arrays)), out_specs=pl.BlockSpec(memory_space=pltpu.VMEM),
        out_shape=TOKEN,
    )(token, *arrays)


def _swap_geometry(bufs):
    x, y, c, _ = _me()
    return [(b.at[c], b.at[c], b.at[1 - c], (x, y, 1 - c)) for b in bufs]


def _chip_start(parts, tag):
    n = len(parts)

    def body(*refs):
        src, land = refs[2 * n:3 * n], refs[3 * n:4 * n]
        send_sems, recv_sems, token = refs[4 * n:]
        x, y, c, chips = _me()
        k = 2 * x + y
        for w in range(n):
            for j, (px, py) in enumerate(chips):
                _remote(src[w].at[2 * px + py], land[w].at[k], send_sems.at[3 * w + j], recv_sems.at[3 * w + j],
                        (px, py, c)).start()
        token[...] = jnp.zeros_like(token)

    lands = [lax.empty(p.shape, p.dtype) for p in parts]
    sem = pltpu.SemaphoreType.DMA((3 * n,))
    outs = pl.pallas_call(
        body, name="reduce_ici_start_" + tag, in_specs=[HBM] * (2 * n),
        out_specs=[HBM] * (2 * n) + [SEM, SEM, pl.BlockSpec(memory_space=pltpu.VMEM)],
        out_shape=[pltpu.HBM(p.shape, p.dtype) for p in parts + lands] + [sem, sem, TOKEN],
        input_output_aliases={i: i for i in range(2 * n)}, compiler_params=SPLIT,
    )(*[_in_hbm(v) for v in parts + lands])
    return outs[:n], outs[n:2 * n], outs[2 * n], outs[2 * n + 1], outs[-1]


def _chip_wait(parts, lands, send_sems, recv_sems, after, tag):
    n = len(parts)

    def body(*refs):
        send, recv = refs[2 * n], refs[2 * n + 1]
        src, land = refs[2 * n + 3:3 * n + 3], refs[3 * n + 3:]
        x, y, c, chips = _me()
        for w in range(n):
            for j, (px, py) in enumerate(chips):
                got = land[w].at[2 * px + py]
                _remote(got, got, send.at[3 * w + j], recv.at[3 * w + j], (px, py, c)).wait_recv()
        for w in range(n):
            for j, (px, py) in enumerate(chips):
                sent = src[w].at[2 * px + py]
                _remote(sent, sent, send.at[3 * w + j], recv.at[3 * w + j], (px, py, c)).wait_send()

    outs = pl.pallas_call(
        body, name="reduce_ici_wait_" + tag, in_specs=[HBM] * (2 * n) + [SEM, SEM, ANY], out_specs=[HBM] * (2 * n),
        out_shape=[pltpu.HBM(p.shape, p.dtype) for p in parts + lands],
        input_output_aliases={i: i for i in range(2 * n)}, compiler_params=SPLIT,
    )(*parts, *lands, send_sems, recv_sems, after)
    chip = 2 * lax.axis_index("x") + lax.axis_index("y")
    return [lax.dynamic_update_slice(s, lax.dynamic_index_in_dim(p, chip, 0, keepdims=True), (chip, 0, 0))
            for p, s in zip(outs[:n], outs[n:])]


def _half_swap(halves, tag):
    n = len(halves)
    core = lax.axis_index("c")
    bufs = [lax.dynamic_update_slice(lax.empty((2,) + h.shape, h.dtype), h[None], (core, 0, 0)) for h in halves]

    def body(*refs):
        dst = refs[n:2 * n]
        send_sems, recv_sems = refs[2 * n:]
        x, y, c, _ = _me()
        sibling = (x, y, 1 - c)
        cps = []
        for w in range(n):
            cp = _remote(dst[w].at[c], dst[w].at[c], send_sems.at[w], recv_sems.at[w], sibling)
            cp.start()
            cps.append(cp)
        for w in range(n):
            other = dst[w].at[1 - c]
            _remote(other, other, send_sems.at[w], recv_sems.at[w], sibling).wait_recv()
        for cp in cps:
            cp.wait_send()

    outs = pl.pallas_call(
        body, name="reduce_half_swap_" + tag, in_specs=[ANY] * n, out_specs=[ANY] * n,
        out_shape=[jax.ShapeDtypeStruct(b.shape, b.dtype) for b in bufs],
        input_output_aliases={w: w for w in range(n)},
        scratch_shapes=[pltpu.SemaphoreType.DMA((n,)), pltpu.SemaphoreType.DMA((n,))],
    )(*bufs)
    return [o.reshape(2 * o.shape[1], o.shape[2]) for o in outs]


def _add_parts(full, axis, rows, sib, name):
    _, r, c = sib.shape
    tr, tc = _tile(r, 1024, 16), _tile(c, 2048)
    nb = r // tr
    core = jnp.reshape(lax.axis_index("c"), (1,)).astype(jnp.int32)

    def body(c_ref, a_ref, b_ref, o_ref):
        o_ref[0] = (a_ref[...].astype(F32) + b_ref[0].astype(F32)).astype(BF16)

    blk = pl.BlockSpec((1, tr, tc), lambda j, i, l, cr: (j, i, l))
    return pl.pallas_call(
        body, name=name,
        grid_spec=pltpu.PrefetchScalarGridSpec(
            num_scalar_prefetch=1, grid=(4, nb, c // tc),
            in_specs=[pl.BlockSpec((tr, tc), lambda j, i, l, cr: ((_slot(axis, j) * 2 + cr[0]) * nb + i, l)), blk],
            out_specs=blk),
        out_shape=jax.ShapeDtypeStruct(sib.shape, BF16),
        compiler_params=_cparams(("parallel", "parallel", "parallel")),
    )(core, full, sib)


def _sum_slots(a, name):
    _, r, c = a.shape
    tr, tc = _tile(r, 512, 8), _tile(c, 2048)

    def body(a_ref, o_ref):
        v = a_ref[...].astype(F32)
        o_ref[...] = ((v[0] + v[1]) + v[2]) + v[3]

    return pl.pallas_call(
        body, name=name, grid=(r // tr, c // tc),
        in_specs=[pl.BlockSpec((4, tr, tc), lambda i, l: (0, i, l))],
        out_specs=pl.BlockSpec((tr, tc), lambda i, l: (i, l)),
        out_shape=jax.ShapeDtypeStruct((r, c), F32),
        compiler_params=_cparams(("parallel", "parallel")),
    )(a)


class _Reducer:
    def __init__(self, spec):
        self.spec = spec
        self.paired = {}
        self.pending = []

    def pair(self, name, full):
        ax, shp = self.spec[name]
        land = lax.empty((4, shp[0] // 2, shp[1]), full.dtype)
        arrays, sems, token = _split_start("reduce_pair_start_" + name, [full, land], _pair_geometry([ax], [shp]), 4)
        self.paired[name] = (arrays, sems)
        return token

    def ship(self, tag, names, after):
        parts = []
        for n in names:
            ax, shp = self.spec[n]
            arrays, sems = self.paired.pop(n)
            full, sib = _split_wait("reduce_pair_wait_" + n, arrays, sems, after, _pair_geometry([ax], [shp]))
            parts.append(_add_parts(full, ax, shp[0], sib, name=f"reduce_add_{n}"))
        parts, lands, send, recv, token = _chip_start(parts, tag)
        self.pending.append((tag, names, parts, lands, send, recv))
        return token

    def finish(self, after, tags):
        out = {}
        for tag, names, parts, lands, send, recv in [p for p in self.pending if p[0] in tags]:
            slots = _chip_wait(parts, lands, send, recv, after, tag)
            halves = [_sum_slots(s, name=f"reduce_sum_{n}") for n, s in zip(names, slots)]
            out.update(zip(names, _half_swap(halves, tag)))
        return out

    def finish_start(self, after, tag):
        (_, names, parts, lands, send, recv), = [p for p in self.pending if p[0] == tag]
        slots = _chip_wait(parts, lands, send, recv, after, tag)
        halves = [_sum_slots(s, name=f"reduce_sum_{n}") for n, s in zip(names, slots)]
        core = lax.axis_index("c")
        bufs = [lax.dynamic_update_slice(lax.empty((2,) + h.shape, h.dtype), h[None], (core, 0, 0)) for h in halves]
        bufs, sems, _ = _split_start("reduce_half_swap_start_" + tag, bufs, _swap_geometry, len(bufs))
        return tag, names, bufs, sems

    def swap_wait(self, started, after):
        tag, names, bufs, sems = started
        outs = _split_wait("reduce_half_swap_wait_" + tag, bufs, sems, after, _swap_geometry)
        return dict(zip(names, [o.reshape(2 * o.shape[1], o.shape[2]) for o in outs]))


def _allreduce_small(pack, after):
    rows = pack.shape[0]

    def body(p_ref, _, o_ref, slots, send_sems, recv_sems):
        x, y, c, _ = _me()
        me = 4 * x + 2 * y + c
        slots[me] = p_ref[...]
        cps = []
        for r in range(1, 8):
            peer = (x ^ (r >> 2), y ^ ((r >> 1) & 1), c ^ (r & 1))
            cp = _remote(p_ref, slots.at[me], send_sems.at[r - 1], recv_sems.at[r - 1], peer)
            cp.start()
            cps.append(cp)
        for r in range(1, 8):
            frm = me ^ r
            _remote(slots.at[frm], slots.at[frm], send_sems.at[r - 1], recv_sems.at[r - 1], (x, y, c)).wait_recv()
        for cp in cps:
            cp.wait_send()
        acc = slots[0]
        for s in range(1, 8):
            acc = acc + slots[s]
        o_ref[...] = acc

    vm = pl.BlockSpec(memory_space=pltpu.VMEM)
    return pl.pallas_call(
        body, name="allreduce_small", in_specs=[vm, ANY], out_specs=vm,
        out_shape=jax.ShapeDtypeStruct(pack.shape, F32),
        scratch_shapes=[pltpu.VMEM((8, rows, HEAD_DIM), F32), pltpu.SemaphoreType.DMA((7,)),
                        pltpu.SemaphoreType.DMA((7,))],
    )(pack, after)


_ROWS = ["norm_mix", "norm_ffn", "mem_norm", "fox_q_norm", "fox_k_norm", "gdn_out_norm", "mem_q_norm",
         "mem_k_norm", "fox_f_bias", "gdn_a_log", "gdn_dt_bias"]


def _pack_rows(vals):
    out = []
    for name in _ROWS:
        v = vals[name].reshape(-1)
        n = -(-v.shape[0] // HEAD_DIM) * HEAD_DIM
        out.append(jnp.pad(v, (0, n - v.shape[0])).reshape(-1, HEAD_DIM))
    return jnp.concatenate(out, axis=0)


def _unpack_rows(pack, like):
    out, r = {}, 0
    for name in _ROWS:
        n = like[name].shape[-1]
        nr = -(-n // HEAD_DIM)
        out[name] = pack[r:r + nr].reshape(1, -1)[:, :n]
        r += nr
    return out, r


def kernel(x, mem, norm_mix, w_in, fox_f_bias, fox_q_norm, fox_k_norm, gdn_conv, gdn_a_log, gdn_dt_bias, gdn_out_norm, mem_norm, w_mem_kv, mem_q_norm, mem_k_norm, w_out, norm_ffn, w_gate_up, w_down, loss_target, m_norm_mix, m_w_in, m_fox_f_bias, m_fox_q_norm, m_fox_k_norm, m_gdn_conv, m_gdn_a_log, m_gdn_dt_bias, m_gdn_out_norm, m_mem_norm, m_w_mem_kv, m_mem_q_norm, m_mem_k_norm, m_w_out, m_norm_ffn, m_w_gate_up, m_w_down, v_norm_mix, v_w_in, v_fox_f_bias, v_fox_q_norm, v_fox_k_norm, v_gdn_conv, v_gdn_a_log, v_gdn_dt_bias, v_gdn_out_norm, v_mem_norm, v_w_mem_kv, v_mem_q_norm, v_mem_k_norm, v_w_out, v_norm_ffn, v_w_gate_up, v_w_down):
    a = dict(locals())
    d = x.shape[-1]
    lay = _Layout(d)
    chip = 2 * lax.axis_index("x") + lax.axis_index("y")
    small = {n: a[n] for n in _ROWS}
    big = ["w_in", "w_mem_kv", "w_out", "w_gate_up", "w_down"]
    axes = [0, 0, 0, 1, 0]

    conv_cols = gdn_conv.shape[-1]
    conv_n = CONV_WIDTH * conv_cols
    conv_rows = -(-conv_n // HEAD_DIM)
    conv_blk = jnp.pad(gdn_conv.reshape(-1), (0, 32 * HEAD_DIM - conv_n)).reshape(32, HEAD_DIM)
    axis_of = dict(zip(big, axes), conv=0, w_in_a=0, w_in_b=0)
    shape_of = {n: a[n].shape[1:] for n in big[1:]}
    shape_of.update(w_in_a=(w_in.shape[1], lay.cols_a), w_in_b=(w_in.shape[1], lay.cols_b), conv=conv_blk.shape)
    placed = {"w_in_a": _cast_place(w_in[0], 0, "cast_w_in_a", lambda v: lay.regroup(v)[:, :lay.cols_a], lay.cols_a),
              "conv": lax.dynamic_update_slice(lax.empty((4 * 32, HEAD_DIM), F32), conv_blk, (chip * 32, 0))}
    grouped = {"in_a": ["w_in_a"], "in_b": ["w_in_b"], "mixer": ["w_mem_kv", "conv"], "out": ["w_out"],
               "gate_up": ["w_gate_up"], "down": ["w_down"]}
    inflight = {}

    def start(tags, name):
        names = [n for t in tags for n in grouped[t]]
        bufs, sems, token = _gather_start([placed[n] for n in names], [axis_of[n] for n in names],
                                          [shape_of[n] for n in names],
                                          [[names.index(n) for n in grouped[t]] for t in tags], name)
        for t, pair in zip(tags, sems):
            inflight[t] = ([bufs[names.index(n)] for n in grouped[t]], pair)
        return token

    first = start(["in_a"], "gather_ici_start_in")
    placed["w_in_b"] = _cast_place(w_in[0], 0, "cast_w_in_b", lambda v: lay.regroup(v)[:, lay.cols_a:], lay.cols_b,
                                   after=first)
    placed.update({n: _cast_place(a[n][0], axis_of[n], "cast_" + n, after=first) for n in big[1:]})
    all_started = start(["in_b", "mixer", "out", "gate_up", "down"], "gather_ici_start_rest")
    all_started = _after_all("moments_ready", all_started, m_w_in[0], v_w_in[0])

    forwarding = {}

    def prefetch(tag, after):
        bufs, sem_pair = inflight.pop(tag)
        ax, shp = [axis_of[n] for n in grouped[tag]], [shape_of[n] for n in grouped[tag]]
        got = _gather_wait(bufs, ax, shp, sem_pair, all_started if tag == "in_a" else after,
                           "gather_ici_wait_" + tag)
        geometry = _forward_geometry(ax, shp)
        got, sems, _ = _split_start("gather_forward_start_" + tag, got, geometry, 3 * len(got))
        forwarding[tag] = (got, sems, geometry)

    def weights(tag, after):
        got, sems, geometry = forwarding.pop(tag)
        got = _split_wait("gather_forward_wait_" + tag, got, sems, after, geometry)
        if tag != "mixer":
            return got
        taps = got[1].reshape(4, 32 * HEAD_DIM)[:, :conv_n].reshape(4, CONV_WIDTH, conv_cols)
        return got[0], jnp.transpose(taps, (1, 0, 2)).reshape(CONV_WIDTH, 4 * conv_cols)

    sp = dict(small)
    reducer = _Reducer({n: (axis_of[n], shape_of[n]) for n in big[1:] + ["w_in_a", "w_in_b"]})
    loss_blk, dx, g = _local_step(x[0], mem[0], loss_target[0], prefetch, weights, reducer, sp)

    gsmall = {n: g[n] for n in _ROWS}
    pack = jnp.concatenate([_pack_rows(gsmall), g["gdn_conv"].reshape(-1, HEAD_DIM), loss_blk], axis=0)
    pack = jnp.pad(pack, ((0, -pack.shape[0] % 8), (0, 0)))
    out = {"grad_x": dx[None]}

    def adamw_shards(reduced):
        if "w_in_a" in reduced:
            reduced = {"w_in": (reduced["w_in_a"], reduced["w_in_b"])}
        for n, gsh in reduced.items():
            join = (lambda ga, gb: lay.ungroup(jnp.concatenate([ga, gb], axis=1))) if n == "w_in" else None
            res = _adamw(a[n][0], gsh, a["m_" + n][0], a["v_" + n][0], g_fn=join, name="adamw_" + n)
            for pre, r in zip(["grad_", "delta_", "new_m_", "new_v_"], res):
                out[pre + n] = r[None]
        return res[0]

    mix_swap = reducer.finish_start(dx, "mix")
    ffn_swap = reducer.finish_start(mix_swap[2][0], "ffn")
    done = adamw_shards(reducer.swap_wait(mix_swap, ffn_swap[2][0]))
    done = adamw_shards(reducer.swap_wait(ffn_swap, done))
    tot = _allreduce_small(pack, done)
    gs, r0 = _unpack_rows(tot, small)
    conv_g = tot[r0:r0 + CONV_WIDTH * 4 * conv_cols // HEAD_DIM].reshape(CONV_WIDTH, 4 * conv_cols)
    gs_conv = lax.dynamic_slice_in_dim(conv_g, chip * conv_cols, conv_cols, axis=1)
    out["loss"] = tot[r0 + CONV_WIDTH * 4 * conv_cols // HEAD_DIM, 0]
    adamw_shards(reducer.finish(tot, ("in",)))
    conv_pad = lambda v: jnp.pad(v.reshape(-1), (0, conv_rows * HEAD_DIM - conv_n)).reshape(conv_rows, HEAD_DIM)
    packs = []
    for src, cv in [(small, gdn_conv), (gs, gs_conv), ({n: a["m_" + n] for n in _ROWS}, m_gdn_conv),
                    ({n: a["v_" + n] for n in _ROWS}, v_gdn_conv)]:
        packs.append(jnp.concatenate([_pack_rows(src), conv_pad(cv)], axis=0))
    res = _adamw(*packs, name="adamw_small")
    for pre, r in zip(["grad_", "delta_", "new_m_", "new_v_"], res):
        vals, r1 = _unpack_rows(r, small)
        for n in _ROWS:
            out[pre + n] = vals[n]
        out[pre + "gdn_conv"] = r[r1:r1 + conv_rows].reshape(-1)[:conv_n].reshape(gdn_conv.shape)
    names = ["norm_mix", "w_in", "fox_f_bias", "fox_q_norm", "fox_k_norm", "gdn_conv", "gdn_a_log", "gdn_dt_bias",
             "gdn_out_norm", "mem_norm", "w_mem_kv", "mem_q_norm", "mem_k_norm", "w_out", "norm_ffn", "w_gate_up",
             "w_down"]
    return (out["loss"], out["grad_x"], *[out[p + n] for p in ["grad_", "delta_", "new_m_", "new_v_"] for n in names])
```

```python
import functools
import math

import jax
import jax.numpy as jnp
from jax import lax
from jax.experimental import pallas as pl
from jax.experimental.pallas import tpu as pltpu

F32, BF16 = jnp.float32, jnp.bfloat16
HEAD_DIM = 128
CHUNK = 64
N_MEM_HEADS = 4
CONV_WIDTH = 4
NORM_EPS = 1e-6
ADAM_LR, ADAM_B1, ADAM_B2, ADAM_EPS, ADAM_WD, ADAM_STEP = 0.001, 0.9, 0.999, 1e-08, 0.01, 10
VMEM_LIMIT = 48 * 1024 * 1024
NEG = -1e30
MESH = pl.DeviceIdType.MESH


def _cparams(sem=None, **kw):
    if sem is not None:
        kw["dimension_semantics"] = sem
    return pltpu.CompilerParams(vmem_limit_bytes=VMEM_LIMIT, **kw)


def _tile(n, target, mult=128):
    best = None
    d = mult
    while d <= min(n, target):
        if n % d == 0:
            best = d
        d += mult
    return best if best is not None else n


def _dot(a, b, dims, hi):
    if a.ndim == 3:
        dn = (((dims[0][0] + 1,), (dims[1][0] + 1,)), ((0,), (0,)))
    else:
        dn = (dims, ((), ()))
    if hi is not None:
        return lax.dot_general(a, b, dn, precision=hi, preferred_element_type=F32)
    return lax.dot_general(a.astype(BF16), b.astype(BF16), dn, preferred_element_type=F32)


def _make_dots(hi, cotangent=None):
    @jax.custom_vjp
    def nn(a, b):
        return _dot(a, b, ((1,), (0,)), hi)

    @jax.custom_vjp
    def nt(a, b):
        return _dot(a, b, ((1,), (1,)), hi)

    @jax.custom_vjp
    def tn(a, b):
        return _dot(a, b, ((0,), (0,)), hi)

    bnn, bnt, btn = cotangent or (nn, nt, tn)
    nn.defvjp(lambda a, b: (nn(a, b), (a, b)), lambda r, g: (bnt(g, r[1]), btn(r[0], g)))
    nt.defvjp(lambda a, b: (nt(a, b), (a, b)), lambda r, g: (bnn(g, r[1]), btn(g, r[0])))
    tn.defvjp(lambda a, b: (tn(a, b), (a, b)), lambda r, g: (bnt(r[1], g), bnn(r[0], g)))
    return nn, nt, tn


_nn, _nt, _tn = _make_dots(None)
_nn_hi, _nt_hi, _tn_hi = _make_dots(lax.Precision.HIGHEST)


def _sigmoid(x):
    return jax.nn.sigmoid(x)


@jax.custom_vjp
def _softplus(x):
    return jnp.maximum(x, 0.0) + jnp.log(1.0 + jnp.exp(-jnp.abs(x)))


_softplus.defvjp(lambda x: (_softplus(x), x), lambda x, g: (g * _sigmoid(x),))


def _silu(x):
    return x * _sigmoid(x)


def _rms_fn(x, gain, z=None):
    y = x * lax.rsqrt(jnp.mean(x * x, axis=-1, keepdims=True) + NORM_EPS) * gain
    if z is not None:
        y = y * _silu(z)
    return y


def _mm(a, b, *, ta=False, tb=False, out_dtype=F32, res=None, stack=None, after=None, name):
    a2, b2 = a.shape[-2:], b.shape[-2:]
    ns = b.shape[0] if stack else 1
    m = a2[1] if ta else a2[0]
    k = a2[0] if ta else a2[1]
    n = b2[0] if tb else b2[1]
    assert k == (b2[1] if tb else b2[0])
    tm, tn, tk = _mm_tiles(m, n, k, ns if stack == "sum" else 1, a.dtype.itemsize, b.dtype.itemsize,
                           jnp.dtype(out_dtype).itemsize, res is not None)
    nk = k // tk
    single = nk == 1 and stack != "sum"
    dims = ((0 if ta else 1,), (1 if tb else 0,))
    if stack == "sum":
        order = lambda g0, g1, g2, g3: (g2, g0, g1, g3)
        grid = (m // tm, n // tn, ns, nk)
    else:
        order = lambda g0, g1, g2, g3: (g0, g1, g2, g3)
        grid = (ns, m // tm, n // tn, nk)

    def body(*refs):
        if after is not None:
            refs = refs[:2 + (res is not None)] + refs[3 + (res is not None):]
        if single:
            a_ref, b_ref = refs[:2]
            r = lax.dot_general(a_ref[...].astype(BF16), b_ref[...].astype(BF16), (dims, ((), ())),
                                preferred_element_type=F32)
            if res is not None:
                r = r + refs[2][...]
            refs[-1][...] = r.astype(out_dtype)
            return
        if res is None:
            a_ref, b_ref, o_ref, acc = refs
        else:
            a_ref, b_ref, r_ref, o_ref, acc = refs
        s, _, _, kk = order(*[pl.program_id(d) for d in range(4)])
        first = kk == 0
        last = kk == nk - 1
        if stack == "sum":
            first, last = first & (s == 0), last & (s == ns - 1)

        @pl.when(first)
        def _():
            acc[...] = jnp.zeros_like(acc)

        acc[...] += lax.dot_general(a_ref[...].astype(BF16), b_ref[...].astype(BF16), (dims, ((), ())),
                                    preferred_element_type=F32)

        @pl.when(last)
        def _():
            r = acc[...]
            if res is not None:
                r = r + r_ref[...]
            o_ref[...] = r.astype(out_dtype)

    def spec(shape, idx, stacked):
        if stacked:
            return pl.BlockSpec((None,) + shape, lambda *g: (order(*g)[0],) + idx(*order(*g)))
        return pl.BlockSpec(shape, lambda *g: idx(*order(*g)))

    a_spec = (spec((tk, tm), lambda s, i, j, kk: (kk, i), stack == "sum") if ta
              else spec((tm, tk), lambda s, i, j, kk: (i, kk), stack == "sum"))
    b_spec = (spec((tn, tk), lambda s, i, j, kk: (j, kk), bool(stack)) if tb
              else spec((tk, tn), lambda s, i, j, kk: (kk, j), bool(stack)))
    o_spec = spec((tm, tn), lambda s, i, j, kk: (i, j), stack == "out")
    ins, specs = [a, b], [a_spec, b_spec]
    if res is not None:
        ins.append(res)
        specs.append(o_spec)
    if after is not None:
        ins.append(after)
        specs.append(pl.BlockSpec(after.shape, lambda *g: (0,) * after.ndim))
    sem = (("parallel", "parallel", "arbitrary", "arbitrary") if stack == "sum"
           else ("parallel", "parallel", "parallel", "arbitrary"))
    return pl.pallas_call(
        body, name=name, grid=grid, in_specs=specs, out_specs=o_spec,
        out_shape=jax.ShapeDtypeStruct(((ns,) if stack == "out" else ()) + (m, n), out_dtype),
        scratch_shapes=[] if single else [pltpu.VMEM((tm, tn), F32)],
        compiler_params=_cparams(sem),
    )(*ins)


MM_VMEM_BUDGET = 40 * 1024 * 1024
MXU_WIDTH = 256


def _mm_tiles(m, n, k, ns, sa, sb, so, has_res):
    def divs(x, mult, cap):
        out = [d for d in range(mult, min(x, cap) + 1, mult) if x % d == 0]
        return out or [x]

    best = None
    for tk in divs(k, 128, 8192):
        nk = (k // tk) * ns
        for tm in divs(m, 8, 2048):
            for tn in divs(n, 128, 2048):
                vmem = 2 * (tm * tk * sa + tk * tn * sb + tm * tn * so) + (2 * tm * tn * 4 if has_res else 0)
                vmem += tm * tn * 4 if nk > 1 else 0
                if vmem > MM_VMEM_BUDGET:
                    continue
                steps = (m // tm) * (n // tn) * nk
                traffic = (m // tm) * k * n * sb * ns + (n // tn if nk > 1 else 1) * m * k * sa * ns
                cost = steps * 0.4e-6 + traffic / 2.5e12 + (nk * m * n * 8 / 6e12 if nk > 1 else 0)
                cost += 2.0 * m * n * k * ns / 7e14 * (-(-tn // MXU_WIDTH) * MXU_WIDTH / tn)
                if best is None or cost < best[0]:
                    best = (cost, tm, tn, tk)
    return best[1:]


def _norm_fwd(x, xoff, gain, ncol, w, out_dtype, *, z=None, zoff=0, into=None, into_off=0, name):
    t = x.shape[0]
    tr = _tile(t, max(256, (1 << 18) // w), 8)

    def body(*refs):
        x_ref, g_ref, o_ref = refs[0], refs[1], refs[-1]
        y = _rms_fn(x_ref[...], g_ref[...]) if z is None else _rms_fn(x_ref[...], g_ref[...], refs[2][...])
        o_ref[...] = y.astype(out_dtype)

    ins = [x, gain]
    specs = [pl.BlockSpec((tr, w), lambda j, r: (r, xoff + j)), pl.BlockSpec((1, w), lambda j, r: (0, 0))]
    if z is not None:
        ins.append(z)
        specs.append(pl.BlockSpec((tr, w), lambda j, r: (r, zoff + j)))
    aliases = {}
    if into is not None:
        aliases = {len(ins): 0}
        ins.append(into)
        specs.append(pl.BlockSpec(memory_space=pl.ANY))
    return pl.pallas_call(
        body, name=name, grid=(ncol, t // tr), in_specs=specs,
        out_specs=pl.BlockSpec((tr, w), lambda j, r: (r, into_off + j)),
        out_shape=jax.ShapeDtypeStruct((t, ncol * w) if into is None else into.shape, out_dtype),
        input_output_aliases=aliases, compiler_params=_cparams(("parallel", "parallel")),
    )(*ins)


def _norm_bwd(x, xoff, gain, dy, dyoff, ncol, w, *, z=None, zoff=0, res=None, name):
    t = x.shape[0]
    tr = _tile(t, max(256, (1 << 18) // w), 8)

    def body(*refs):
        it = iter(refs)
        x_ref, g_ref = next(it), next(it)
        z_ref = next(it) if z is not None else None
        dy_ref = next(it)
        r_ref = next(it) if res is not None else None
        dx_ref = next(it)
        dz_ref = next(it) if z is not None else None
        dg_ref = next(it)

        @pl.when((pl.program_id(0) == 0) & (pl.program_id(1) == 0))
        def _():
            dg_ref[...] = jnp.zeros_like(dg_ref)

        args = (x_ref[...], g_ref[...]) + ((z_ref[...],) if z is not None else ())
        _, vjp = jax.vjp(_rms_fn, *args)
        grads = vjp(dy_ref[...].astype(F32))
        dx = grads[0]
        if res is not None:
            dx = dx + r_ref[...]
        dx_ref[...] = dx
        if z is not None:
            dz_ref[...] = grads[2]
        dg_ref[...] += grads[1]

    ins = [x, gain]
    specs = [pl.BlockSpec((tr, w), lambda j, r: (r, xoff + j)), pl.BlockSpec((1, w), lambda j, r: (0, 0))]
    if z is not None:
        ins.append(z)
        specs.append(pl.BlockSpec((tr, w), lambda j, r: (r, zoff + j)))
    ins.append(dy)
    specs.append(pl.BlockSpec((tr, w), lambda j, r: (r, dyoff + j)))
    blk = pl.BlockSpec((tr, w), lambda j, r: (r, j))
    if res is not None:
        ins.append(res)
        specs.append(blk)
    full = jax.ShapeDtypeStruct((t, ncol * w), F32)
    out_shape, out_specs = [full], [blk]
    if z is not None:
        out_shape.append(full)
        out_specs.append(blk)
    out_shape.append(jax.ShapeDtypeStruct((1, w), F32))
    out_specs.append(pl.BlockSpec((1, w), lambda j, r: (0, 0)))
    return pl.pallas_call(
        body, name=name, grid=(ncol, t // tr), in_specs=specs, out_specs=out_specs, out_shape=out_shape,
        compiler_params=_cparams(("arbitrary", "arbitrary")),
    )(*ins)


def _small_fn(x, pa, pb, nf, ng):
    lane = lax.broadcasted_iota(jnp.int32, x.shape, 1)
    zz = x + pb
    logf = -_softplus(-zz)
    g = -jnp.exp(pa) * _softplus(zz)
    beta = _sigmoid(x)
    return jnp.where(lane < nf, logf, jnp.where(lane < nf + ng, g, beta))


def _tri(n, upper):
    r = lax.broadcasted_iota(jnp.int32, (n, n), 0)
    c = lax.broadcasted_iota(jnp.int32, (n, n), 1)
    return jnp.where((c >= r) if upper else (c <= r), 1.0, 0.0).astype(F32)


def _small_fwd(p, off, pa, pb, nf, ng):
    t = p.shape[0]
    blk = HEAD_DIM
    nb = t // blk

    def body(x_ref, pa_ref, pb_ref, v_ref, c_ref):
        v_ref[...] = _small_fn(x_ref[...], pa_ref[...], pb_ref[...], nf, ng)
        tri = _tri(blk, False)

        carry = jnp.zeros((1, HEAD_DIM), F32)
        for i in range(nb):
            rows = slice(i * blk, (i + 1) * blk)
            c = _nn_hi(tri, v_ref[rows, :]) + carry
            c_ref[rows, :] = c
            carry = c[blk - 1:blk, :]

    row = pl.BlockSpec((1, HEAD_DIM), lambda i: (0, 0))
    out = pl.BlockSpec((t, HEAD_DIM), lambda i: (0, 0))
    return pl.pallas_call(
        body, name="small_fwd", grid=(1,),
        in_specs=[pl.BlockSpec((t, HEAD_DIM), lambda i: (0, off)), row, row], out_specs=[out, out],
        out_shape=[jax.ShapeDtypeStruct((t, HEAD_DIM), F32)] * 2,
        compiler_params=_cparams(("arbitrary",)),
    )(p, pa, pb)


def _small_bwd(p, off, pa, pb, dvals, dcsum, nf, ng):
    t = p.shape[0]
    blk = HEAD_DIM
    nb = t // blk

    def body(x_ref, pa_ref, pb_ref, dv_ref, dc_ref, dx_ref, dpa_ref, dpb_ref, tot_ref):
        tri = _tri(blk, True)

        carry = jnp.zeros((1, HEAD_DIM), F32)
        for i in reversed(range(nb)):
            rows = slice(i * blk, (i + 1) * blk)
            c = _nn_hi(tri, dc_ref[rows, :]) + carry
            tot_ref[rows, :] = c + dv_ref[rows, :]
            carry = c[0:1, :]
        f = functools.partial(_small_fn, nf=nf, ng=ng)
        _, vjp = jax.vjp(f, x_ref[...], pa_ref[...], pb_ref[...])
        dx, dpa, dpb = vjp(tot_ref[...])
        dx_ref[...] = dx
        dpa_ref[...] = dpa
        dpb_ref[...] = dpb

    row = pl.BlockSpec((1, HEAD_DIM), lambda i: (0, 0))
    full = pl.BlockSpec((t, HEAD_DIM), lambda i: (0, 0))
    return pl.pallas_call(
        body, name="small_bwd", grid=(1,),
        in_specs=[pl.BlockSpec((t, HEAD_DIM), lambda i: (0, off)), row, row, full, full],
        out_specs=[full, row, row],
        out_shape=[jax.ShapeDtypeStruct((t, HEAD_DIM), F32), jax.ShapeDtypeStruct((1, HEAD_DIM), F32),
                   jax.ShapeDtypeStruct((1, HEAD_DIM), F32)],
        scratch_shapes=[pltpu.VMEM((t, HEAD_DIM), F32)],
        compiler_params=_cparams(("arbitrary",)),
    )(p, pa, pb, dvals, dcsum)


def _fox_heads(nf, most):
    return next(h for h in range(most, 0, -1) if nf % h == 0)


def _fox_fwd(q, k, v, cc, cr, nf, tq, tk, d_mix):
    t = q.shape[0]
    scale = HEAD_DIM ** -0.5
    assert tq == tk

    vt = jnp.transpose(v.reshape(t // tk, tk, nf, HEAD_DIM), (2, 0, 3, 1))

    hp = _fox_heads(nf, 3)
    lanes = lambda h: slice(h * HEAD_DIM, (h + 1) * HEAD_DIM)

    def body(q_ref, k_ref, vt_ref, cc_ref, cr_ref, o_ref, lse_ref, mix_ref):
        i = pl.program_id(1)
        qs = [q_ref[:, lanes(h)] for h in range(hp)]
        cqs = [cr_ref[h, i] for h in range(hp)]
        ones = jnp.ones((8, tk), BF16)
        diff = lax.broadcasted_iota(jnp.int32, (tk, tq), 0) - lax.broadcasted_iota(jnp.int32, (tk, tq), 1)

        def scores(h, j):
            ks = pl.ds(pl.multiple_of(j * tk, tk), tk)
            return lax.dot_general(k_ref[ks, lanes(h)], qs[h], (((1,), (1,)), ((), ())),
                                   preferred_element_type=F32)

        def tile(h, j, m, l, acc, s, masked):
            ks = pl.ds(pl.multiple_of(j * tk, tk), tk)
            s = s * scale + cqs[h] - cc_ref[0, ks, h:h + 1]
            if masked:
                s = jnp.where(diff <= 0, s, NEG)
            m_new = jnp.maximum(m, jnp.max(s, axis=0, keepdims=True))
            pr = jnp.exp(s - m_new).astype(BF16)
            alpha = jnp.exp(m - m_new)
            l = alpha * l + jnp.dot(ones, pr, preferred_element_type=F32)[:1]
            acc = alpha * acc + jnp.dot(vt_ref[h, j], pr, preferred_element_type=F32)
            return m_new, l, acc

        def step(j, carry):
            nxt = [scores(h, j + 1) for h in range(hp)]
            return tuple(tile(h, j, *carry[h], False) + (nxt[h],) for h in range(hp))

        init = tuple((jnp.full((1, tq), NEG, F32), jnp.zeros((1, tq), F32), jnp.zeros((HEAD_DIM, tq), F32),
                      scores(h, 0)) for h in range(hp))
        carry = lax.fori_loop(0, i, step, init)
        for h in range(hp):
            m, l, acc = tile(h, i, *carry[h], True)
            o = jnp.transpose(acc / l)
            o_ref[:, lanes(h)] = o
            mix_ref[:, lanes(h)] = o.astype(BF16)
            lse_ref[h, 0] = m + jnp.log(l)

    w = hp * HEAD_DIM
    qblk = pl.BlockSpec((tq, w), lambda h, i: (i, h))
    return pl.pallas_call(
        body, name="fox_fwd", grid=(nf // hp, t // tq),
        in_specs=[qblk, pl.BlockSpec((t, w), lambda h, i: (0, h)),
                  pl.BlockSpec((hp, t // tk, HEAD_DIM, tk), lambda h, i: (h, 0, 0, 0)),
                  pl.BlockSpec((1, t, HEAD_DIM), lambda h, i: (h, 0, 0)),
                  pl.BlockSpec((hp, t // tk, 1, tk), lambda h, i: (h, 0, 0, 0))],
        out_specs=[qblk, pl.BlockSpec((hp, 1, 1, tq), lambda h, i: (h, i, 0, 0)), qblk],
        out_shape=[jax.ShapeDtypeStruct((t, nf * HEAD_DIM), F32), jax.ShapeDtypeStruct((nf, t // tq, 1, tq), F32),
                   jax.ShapeDtypeStruct((t, d_mix), BF16)],
        compiler_params=_cparams(("parallel", "parallel")),
    )(q, k, vt, cc, cr)


def _fox_bwd(q, k, v, cc, cr, o, lse, dmix, nf, tq, tk):
    t = q.shape[0]
    scale = HEAD_DIM ** -0.5
    assert tq == tk
    hp = _fox_heads(nf, 3)
    lanes = lambda h: slice(h * HEAD_DIM, (h + 1) * HEAD_DIM)
    kt = jnp.transpose(k.reshape(t // tk, tk, nf, HEAD_DIM), (2, 0, 3, 1))

    def body(q_ref, k_ref, kt_ref, v_ref, cc_ref, cr_ref, o_ref, lse_ref, do_ref,
             dq_ref, dk_ref, dv_ref, dcq_ref, dck_ref):
        i = pl.program_id(1)

        @pl.when(i == 0)
        def _():
            dk_ref[...] = jnp.zeros_like(dk_ref)
            dv_ref[...] = jnp.zeros_like(dv_ref)
            dck_ref[...] = jnp.zeros_like(dck_ref)

        diff = lax.broadcasted_iota(jnp.int32, (tk, tq), 0) - lax.broadcasted_iota(jnp.int32, (tk, tq), 1)
        lane = lax.broadcasted_iota(jnp.int32, (tk, HEAD_DIM), 1)
        qs = [q_ref[:, lanes(h)] for h in range(hp)]
        dos = [do_ref[:, lanes(h)] for h in range(hp)]
        do_b = [d.astype(BF16) for d in dos]
        cqs = [cr_ref[h, i] for h in range(hp)]
        lses = [lse_ref[h, 0] for h in range(hp)]
        deltas = [jnp.sum(jnp.transpose(dos[h] * o_ref[:, lanes(h)]), axis=0, keepdims=True) for h in range(hp)]

        def products(h, j):
            ks = pl.ds(pl.multiple_of(j * tk, tk), tk)
            nt = (((1,), (1,)), ((), ()))
            return (lax.dot_general(k_ref[ks, lanes(h)], qs[h], nt, preferred_element_type=F32),
                    lax.dot_general(v_ref[ks, lanes(h)], do_b[h], nt, preferred_element_type=F32))

        def tile(h, j, dqt, dcq, s, dp, masked):
            ks = pl.ds(pl.multiple_of(j * tk, tk), tk)
            pr = jnp.exp(s * scale + cqs[h] - cc_ref[0, ks, h:h + 1] - lses[h])
            if masked:
                pr = jnp.where(diff <= 0, pr, 0.0)
            ds = pr * (dp - deltas[h])
            ds_b = ds.astype(BF16)
            dqt = dqt + jnp.dot(kt_ref[h, j], ds_b, preferred_element_type=F32)
            dk_ref[ks, lanes(h)] += jnp.dot(ds_b, qs[h], preferred_element_type=F32) * scale
            dv_ref[ks, lanes(h)] += jnp.dot(pr.astype(BF16), do_b[h], preferred_element_type=F32)
            dck_ref[0, ks, :] -= jnp.where(lane == h, jnp.sum(ds, axis=1, keepdims=True), 0.0)
            return dqt, dcq + jnp.sum(ds, axis=0, keepdims=True)

        def step(j, carry):
            nxt = [products(h, j + 1) for h in range(hp)]
            return tuple(tile(h, j, *carry[h], False) + nxt[h] for h in range(hp))

        init = tuple((jnp.zeros((HEAD_DIM, tq), F32), jnp.zeros((1, tq), F32)) + products(h, 0) for h in range(hp))
        carry = lax.fori_loop(0, i, step, init)
        for h in range(hp):
            dqt, dcq = tile(h, i, *carry[h], True)
            dq_ref[:, lanes(h)] = jnp.transpose(dqt) * scale
            dcq_ref[h, 0] = dcq

    w = hp * HEAD_DIM
    head_all = pl.BlockSpec((t, w), lambda h, i: (0, h))
    qblk = pl.BlockSpec((tq, w), lambda h, i: (i, h))
    colv = pl.BlockSpec((1, t, HEAD_DIM), lambda h, i: (h, 0, 0))
    rows_all = pl.BlockSpec((hp, t // tk, 1, tk), lambda h, i: (h, 0, 0, 0))
    row_blk = pl.BlockSpec((hp, 1, 1, tq), lambda h, i: (h, i, 0, 0))
    wide = jax.ShapeDtypeStruct((t, nf * HEAD_DIM), F32)
    return pl.pallas_call(
        body, name="fox_bwd", grid=(nf // hp, t // tq),
        in_specs=[qblk, head_all, pl.BlockSpec((hp, t // tk, HEAD_DIM, tk), lambda h, i: (h, 0, 0, 0)), head_all,
                  colv, rows_all, qblk, row_blk, qblk],
        out_specs=[qblk, head_all, head_all, row_blk, colv],
        out_shape=[wide, wide, wide, jax.ShapeDtypeStruct((nf, t // tq, 1, tq), F32),
                   jax.ShapeDtypeStruct((nf // hp, t, HEAD_DIM), F32)],
        compiler_params=_cparams(("parallel", "arbitrary")),
    )(q, k, kt, v, cc, cr, o, lse, dmix)


def _mem_fn(mq, mk, mv, gq, gk):
    qn = _rms_fn(mq, gq)
    kn = _rms_fn(mk, gk)
    s = _nt(qn, kn) * (HEAD_DIM ** -0.5)
    e = jnp.exp(s - lax.stop_gradient(jnp.max(s, axis=1, keepdims=True)))
    pr = e / jnp.sum(e, axis=1, keepdims=True)
    return _nn(pr, mv)


def _mem_specs(t, m, tq, qoff):
    qblk = pl.BlockSpec((tq, HEAD_DIM), lambda h, i: (i, qoff + h))
    kblk = pl.BlockSpec((m, HEAD_DIM), lambda h, i: (0, h))
    vblk = pl.BlockSpec((m, HEAD_DIM), lambda h, i: (0, N_MEM_HEADS + h))
    row = pl.BlockSpec((1, HEAD_DIM), lambda h, i: (0, 0))
    return qblk, kblk, vblk, row


def _mem_fwd(p, qoff, mkv, gq, gk, tq, into, into_off):
    t, m = p.shape[0], mkv.shape[0]
    qblk, kblk, vblk, row = _mem_specs(t, m, tq, qoff)

    def body(q_ref, k_ref, v_ref, gq_ref, gk_ref, _, o_ref):
        o_ref[...] = _mem_fn(q_ref[...], k_ref[...], v_ref[...], gq_ref[...], gk_ref[...]).astype(BF16)

    return pl.pallas_call(
        body, name="mem_fwd", grid=(N_MEM_HEADS, t // tq),
        in_specs=[qblk, kblk, vblk, row, row, pl.BlockSpec(memory_space=pl.ANY)],
        out_specs=pl.BlockSpec((tq, HEAD_DIM), lambda h, i: (i, into_off + h)),
        out_shape=jax.ShapeDtypeStruct(into.shape, BF16), input_output_aliases={5: 0},
        compiler_params=_cparams(("parallel", "parallel")),
    )(p, mkv, mkv, gq, gk, into)


def _mem_bwd(p, qoff, mkv, gq, gk, dmix, dooff, tq):
    t, m = p.shape[0], mkv.shape[0]
    qblk, kblk, vblk, row = _mem_specs(t, m, tq, qoff)

    def body(q_ref, k_ref, v_ref, gq_ref, gk_ref, do_ref, dq_ref, dkv_k_ref, dkv_v_ref, dgq_ref, dgk_ref):
        h, i = pl.program_id(0), pl.program_id(1)

        @pl.when((h == 0) & (i == 0))
        def _():
            dgq_ref[...] = jnp.zeros_like(dgq_ref)
            dgk_ref[...] = jnp.zeros_like(dgk_ref)

        @pl.when(i == 0)
        def _():
            dkv_k_ref[...] = jnp.zeros_like(dkv_k_ref)
            dkv_v_ref[...] = jnp.zeros_like(dkv_v_ref)

        _, vjp = jax.vjp(_mem_fn, q_ref[...], k_ref[...], v_ref[...], gq_ref[...], gk_ref[...])
        dq, dk, dv, dgq, dgk = vjp(do_ref[...])
        dq_ref[...] = dq
        dkv_k_ref[...] += dk
        dkv_v_ref[...] += dv
        dgq_ref[...] += dgq
        dgk_ref[...] += dgk

    oblk = pl.BlockSpec((tq, HEAD_DIM), lambda h, i: (i, h))
    kout = pl.BlockSpec((m, HEAD_DIM), lambda h, i: (0, h))
    half = jax.ShapeDtypeStruct((m, N_MEM_HEADS * HEAD_DIM), F32)
    rshape = jax.ShapeDtypeStruct((1, HEAD_DIM), F32)
    return pl.pallas_call(
        body, name="mem_bwd", grid=(N_MEM_HEADS, t // tq),
        in_specs=[qblk, kblk, vblk, row, row, pl.BlockSpec((tq, HEAD_DIM), lambda h, i: (i, dooff + h))],
        out_specs=[oblk, kout, kout, row, row],
        out_shape=[jax.ShapeDtypeStruct((t, N_MEM_HEADS * HEAD_DIM), F32), half, half, rshape, rshape],
        compiler_params=_cparams(("arbitrary", "arbitrary")),
    )(p, mkv, mkv, gq, gk, dmix)


def _shift_down(x, s):
    if s == 0:
        return x
    r = lax.broadcasted_iota(jnp.int32, x.shape, 0)
    return jnp.where(r >= s, pltpu.roll(x, s, 0), 0.0)


def _shift_up(x, s):
    if s == 0:
        return x
    n = x.shape[0]
    r = lax.broadcasted_iota(jnp.int32, x.shape, 0)
    return jnp.where(r < n - s, pltpu.roll(x, n - s, 0), 0.0)


def _conv_fn(x0, x1, x2, x3, w0, w1, w2, w3, kind):
    y = _silu(x0 * w0 + x1 * w1 + x2 * w2 + x3 * w3)
    if kind == 2:
        return y
    y = y * lax.rsqrt(jnp.sum(y * y, axis=-1, keepdims=True) + NORM_EPS)
    return y * (HEAD_DIM ** -0.5) if kind == 0 else y


def _conv_fwd(p, off, conv_w, ng):
    t = p.shape[0]

    def body(x_ref, w_ref, o_ref):
        kind = pl.program_id(0) // ng
        x = x_ref[...]
        xs = [_shift_down(x, CONV_WIDTH - 1 - j) for j in range(CONV_WIDTH)]
        ws = [w_ref[j:j + 1, :] for j in range(CONV_WIDTH)]
        for kd in range(3):
            @pl.when(kind == kd)
            def _(kd=kd):
                o_ref[...] = _conv_fn(*xs, *ws, kd)

    return pl.pallas_call(
        body, name="gdn_conv_fwd", grid=(3 * ng,),
        in_specs=[pl.BlockSpec((t, HEAD_DIM), lambda c: (0, off + c)),
                  pl.BlockSpec((CONV_WIDTH, HEAD_DIM), lambda c: (0, c))],
        out_specs=pl.BlockSpec((t, HEAD_DIM), lambda c: (0, c)),
        out_shape=jax.ShapeDtypeStruct((t, 3 * ng * HEAD_DIM), F32),
        compiler_params=_cparams(("parallel",)),
    )(p, conv_w)


def _conv_bwd(p, off, conv_w, dys, ng):
    t = p.shape[0]

    def body(x_ref, w_ref, dq_ref, dk_ref, dv_ref, dx_ref, dw_ref):
        kind = pl.program_id(0) // ng
        dy_refs = (dq_ref, dk_ref, dv_ref)
        x = x_ref[...]
        xs = [_shift_down(x, CONV_WIDTH - 1 - j) for j in range(CONV_WIDTH)]
        ws = [w_ref[j:j + 1, :] for j in range(CONV_WIDTH)]
        for kd in range(3):
            @pl.when(kind == kd)
            def _(kd=kd):
                _, vjp = jax.vjp(functools.partial(_conv_fn, kind=kd), *xs, *ws)
                g = vjp(dy_refs[kd][...])
                dx = _shift_up(g[0], CONV_WIDTH - 1)
                for j in range(1, CONV_WIDTH):
                    dx = dx + _shift_up(g[j], CONV_WIDTH - 1 - j)
                dx_ref[...] = dx
                for j in range(CONV_WIDTH):
                    dw_ref[j:j + 1, :] = g[CONV_WIDTH + j]

    blk = pl.BlockSpec((t, HEAD_DIM), lambda c: (0, c))
    head = lambda k: pl.BlockSpec((t, HEAD_DIM), lambda c: (0, jnp.where(c // ng == k, c % ng, 0)))
    wblk = pl.BlockSpec((CONV_WIDTH, HEAD_DIM), lambda c: (0, c))
    return pl.pallas_call(
        body, name="gdn_conv_bwd", grid=(3 * ng,),
        in_specs=[pl.BlockSpec((t, HEAD_DIM), lambda c: (0, off + c)), wblk] + [head(k) for k in range(3)],
        out_specs=[blk, wblk],
        out_shape=[jax.ShapeDtypeStruct((t, 3 * ng * HEAD_DIM), F32),
                   jax.ShapeDtypeStruct((CONV_WIDTH, 3 * ng * HEAD_DIM), F32)],
        compiler_params=_cparams(("parallel",)),
    )(p, conv_w, *dys)


def _lower_inverse(lower):
    c = lower.shape[-1]
    r = lax.broadcasted_iota(jnp.int32, (1, c, c), 1)
    e = lax.broadcasted_iota(jnp.int32, (1, c, c), 2)
    hi = lax.Precision.HIGH
    inv = jnp.where(r == e, 1.0, 0.0) - lower
    pw = lower
    for _ in range(int(math.log2(c)) - 1):
        pw = _dot(pw, pw, ((1,), (0,)), hi)
        inv = inv + _dot(inv, pw, ((1,), (0,)), hi)
    return inv


@jax.custom_vjp
def _solve(lower, inv, vb, kbg):
    hi = lax.Precision.HIGH
    return _dot(inv, vb, ((1,), (0,)), hi), _dot(inv, kbg, ((1,), (0,)), hi)


def _solve_fwd(lower, inv, vb, kbg):
    u, w = _solve(lower, inv, vb, kbg)
    return (u, w), (inv, u, w)


def _solve_bwd(res, cts):
    inv, u, w = res
    dvb, dkbg = _tn(inv, cts[0]), _tn(inv, cts[1])
    return -(_nt(dvb, u) + _nt(dkbg, w)), jnp.zeros_like(inv), dvb, dkbg


_solve.defvjp(_solve_fwd, _solve_bwd)


def _wy_fn(q, k, v, gcol, grow, bcol, inv=None):
    b, c, dk = q.shape
    r = lax.broadcasted_iota(jnp.int32, (1, c, c), 1)
    e = lax.broadcasted_iota(jnp.int32, (1, c, c), 2)
    tril, strict = e <= r, e < r
    gc_col = jnp.sum(jnp.where(tril, grow, 0.0), axis=2, keepdims=True)
    gc_row = jnp.sum(jnp.where(r <= e, gcol, 0.0), axis=1, keepdims=True)
    g_last = jnp.sum(gcol, axis=1, keepdims=True)
    decay = jnp.exp(jnp.where(tril, gc_col - gc_row, NEG))
    kb, vb = k * bcol, v * bcol
    lower = jnp.where(strict, _nt(kb, k) * decay, 0.0)
    if inv is None:
        inv = _lower_inverse(lower)
    u, w = _solve(lower, inv, vb, kb * jnp.exp(gc_col))
    attn = jnp.where(tril, _nt(q, k) * decay, 0.0)
    qg = q * jnp.exp(gc_col)
    kdec = k * jnp.exp(g_last - gc_col)
    egl = jnp.broadcast_to(jnp.exp(g_last), (b, 1, dk))
    return u, w, qg, kdec, attn, egl, inv


def _scan_fn(u, w, qg, kdec, attn, egl, state):
    v_new = u - _nn(w, state)
    o = _nn(qg, state) + _nn(attn, v_new)
    return o, state * egl + _tn(kdec, v_new)


GDN_CHUNKS_PER_STEP = 4
GDN_SCAN_CHUNKS = 4


def _gdn_fwd(qkv, vals, grow, nf, ng):
    t = qkv.shape[0]
    nch = t // CHUNK

    cb = GDN_CHUNKS_PER_STEP
    *wy, inv = _gdn_wy(qkv, vals, grow, nf, ng, cb)

    sc = GDN_SCAN_CHUNKS

    def body(u_ref, w_ref, qg_ref, kd_ref, at_ref, eg_ref, o_ref, st_ref, state):
        @pl.when(pl.program_id(0) == 0)
        def _():
            state[...] = jnp.zeros_like(state)

        for c in range(sc):
            rows = slice(c * CHUNK, (c + 1) * CHUNK)
            heads = lambda ref: jnp.stack([ref[rows, h * HEAD_DIM:(h + 1) * HEAD_DIM] for h in range(ng)])
            st_ref[:, c] = state[...]
            o, new = _scan_fn(heads(u_ref), heads(w_ref), heads(qg_ref), heads(kd_ref), at_ref[:, c], eg_ref[:, c],
                              state[...])
            for h in range(ng):
                o_ref[rows, h * HEAD_DIM:(h + 1) * HEAD_DIM] = o[h]
            state[...] = new

    w = ng * HEAD_DIM
    blk = pl.BlockSpec((sc * CHUNK, w), lambda i: (i, 0))
    o, states = pl.pallas_call(
        body, name="gdn_scan_fwd", grid=(nch // sc,),
        in_specs=[blk, blk, blk, blk, pl.BlockSpec((ng, sc, CHUNK, CHUNK), lambda i: (0, i, 0, 0)),
                  pl.BlockSpec((ng, sc, 1, HEAD_DIM), lambda i: (0, i, 0, 0))],
        out_specs=[blk, pl.BlockSpec((ng, sc, HEAD_DIM, HEAD_DIM), lambda i: (0, i, 0, 0))],
        out_shape=[jax.ShapeDtypeStruct((t, w), F32),
                   jax.ShapeDtypeStruct((ng, nch, HEAD_DIM, HEAD_DIM), F32)],
        scratch_shapes=[pltpu.VMEM((ng, HEAD_DIM, HEAD_DIM), F32)],
        compiler_params=_cparams(("arbitrary",)),
    )(*wy)
    return o, (wy, inv, states)


def _wy_batch(q_ref, k_ref, v_ref, vals_ref, gr_ref, nf, ng, cb):
    idx = [(c, h) for c in range(cb) for h in range(ng)]
    rows = lambda c: slice(c * CHUNK, (c + 1) * CHUNK)
    lanes = lambda h: slice(h * HEAD_DIM, (h + 1) * HEAD_DIM)
    wide = lambda ref: jnp.stack([ref[rows(c), lanes(h)] for c, h in idx])
    col = lambda lane0: jnp.stack([vals_ref[rows(c), lane0 + h:lane0 + h + 1] for c, h in idx])
    return idx, (wide(q_ref), wide(k_ref), wide(v_ref), col(nf), jnp.stack([gr_ref[h, c] for c, h in idx]),
                 col(nf + ng))


def _gdn_wy(qkv, vals, grow, nf, ng, cb):
    t = qkv.shape[0]
    nch = t // CHUNK

    def body(q_ref, k_ref, v_ref, vals_ref, gr_ref, u_ref, w_ref, qg_ref, kd_ref, at_ref, eg_ref, inv_ref):
        idx, args = _wy_batch(q_ref, k_ref, v_ref, vals_ref, gr_ref, nf, ng, cb)
        u, w, qg, kd, at, eg, inv = _wy_fn(*args)
        for b, (c, h) in enumerate(idx):
            rows, lanes = slice(c * CHUNK, (c + 1) * CHUNK), slice(h * HEAD_DIM, (h + 1) * HEAD_DIM)
            u_ref[rows, lanes] = u[b]
            w_ref[rows, lanes] = w[b]
            qg_ref[rows, lanes] = qg[b]
            kd_ref[rows, lanes] = kd[b]
            at_ref[h, c] = at[b]
            eg_ref[h, c] = eg[b]
            inv_ref[h, c] = inv[b]

    wd = ng * HEAD_DIM
    blk = lambda o: pl.BlockSpec((cb * CHUNK, wd), lambda i: (i, o))
    col = pl.BlockSpec((cb * CHUNK, HEAD_DIM), lambda i: (i, 0))
    sq = pl.BlockSpec((ng, cb, CHUNK, CHUNK), lambda i: (0, i, 0, 0))
    wide = jax.ShapeDtypeStruct((t, wd), F32)
    sq_shape = jax.ShapeDtypeStruct((ng, nch, CHUNK, CHUNK), F32)
    return pl.pallas_call(
        body, name="gdn_wy_fwd", grid=(nch // cb,),
        in_specs=[blk(0), blk(1), blk(2), col, pl.BlockSpec((ng, cb, 1, CHUNK), lambda i: (0, i, 0, 0))],
        out_specs=[blk(0), blk(0), blk(0), blk(0), sq, pl.BlockSpec((ng, cb, 1, HEAD_DIM), lambda i: (0, i, 0, 0)),
                   sq],
        out_shape=[wide, wide, wide, wide, sq_shape, jax.ShapeDtypeStruct((ng, nch, 1, HEAD_DIM), F32), sq_shape],
        compiler_params=_cparams(("parallel",)),
    )(qkv, qkv, qkv, vals, grow)


def _gdn_bwd(qkv, vals, grow, saved, do, nf, ng):
    t = qkv.shape[0]
    nch = t // CHUNK
    cb = GDN_CHUNKS_PER_STEP // 2
    wy, inv, states = saved
    wd = ng * HEAD_DIM

    def scan_body(u_ref, w_ref, qg_ref, kd_ref, at_ref, eg_ref, st_ref, do_ref,
                  du_ref, dw_ref, dqg_ref, dkd_ref, dat_ref, deg_ref, dstate):
        @pl.when(pl.program_id(0) == 0)
        def _():
            dstate[...] = jnp.zeros_like(dstate)

        for c in reversed(range(sc)):
            rows = slice(c * CHUNK, (c + 1) * CHUNK)
            heads = lambda ref: jnp.stack([ref[rows, h * HEAD_DIM:(h + 1) * HEAD_DIM] for h in range(ng)])
            _, vjp = jax.vjp(_scan_fn, heads(u_ref), heads(w_ref), heads(qg_ref), heads(kd_ref), at_ref[:, c],
                             eg_ref[:, c], st_ref[:, c])
            du, dw, dqg, dkd, dat, deg, dst = vjp((heads(do_ref), dstate[...]))
            for h in range(ng):
                lanes = slice(h * HEAD_DIM, (h + 1) * HEAD_DIM)
                du_ref[rows, lanes] = du[h]
                dw_ref[rows, lanes] = dw[h]
                dqg_ref[rows, lanes] = dqg[h]
                dkd_ref[rows, lanes] = dkd[h]
            dat_ref[:, c] = dat
            deg_ref[:, c] = deg
            dstate[...] = dst

    sc = GDN_SCAN_CHUNKS
    rev = lambda i: nch // sc - 1 - i
    blk = pl.BlockSpec((sc * CHUNK, wd), lambda i: (rev(i), 0))
    atb = pl.BlockSpec((ng, sc, CHUNK, CHUNK), lambda i: (0, rev(i), 0, 0))
    egb = pl.BlockSpec((ng, sc, 1, HEAD_DIM), lambda i: (0, rev(i), 0, 0))
    wide = jax.ShapeDtypeStruct((t, wd), F32)
    at_shape = jax.ShapeDtypeStruct((ng, nch, CHUNK, CHUNK), F32)
    eg_shape = jax.ShapeDtypeStruct((ng, nch, 1, HEAD_DIM), F32)
    dwy = pl.pallas_call(
        scan_body, name="gdn_scan_bwd", grid=(nch // sc,),
        in_specs=[blk, blk, blk, blk, atb, egb,
                  pl.BlockSpec((ng, sc, HEAD_DIM, HEAD_DIM), lambda i: (0, rev(i), 0, 0)), blk],
        out_specs=[blk, blk, blk, blk, atb, egb],
        out_shape=[wide, wide, wide, wide, at_shape, eg_shape],
        scratch_shapes=[pltpu.VMEM((ng, HEAD_DIM, HEAD_DIM), F32)],
        compiler_params=_cparams(("arbitrary",)),
    )(*wy, states, do)

    def wy_body(q_ref, k_ref, v_ref, vals_ref, gr_ref, du_ref, dw_ref, dqg_ref, dkd_ref, dat_ref, deg_ref,
                inv_ref, dq_ref, dk_ref, dv_ref, dvals_ref, dgr_ref):
        idx, args = _wy_batch(q_ref, k_ref, v_ref, vals_ref, gr_ref, nf, ng, cb)
        lane = lax.broadcasted_iota(jnp.int32, (CHUNK, HEAD_DIM), 1)
        kept = jnp.stack([inv_ref[h, c] for c, h in idx])
        rows = lambda c: slice(c * CHUNK, (c + 1) * CHUNK)
        lanes = lambda h: slice(h * HEAD_DIM, (h + 1) * HEAD_DIM)
        wide_ct = lambda ref: jnp.stack([ref[rows(c), lanes(h)] for c, h in idx])
        cts = (wide_ct(du_ref), wide_ct(dw_ref), wide_ct(dqg_ref), wide_ct(dkd_ref),
               jnp.stack([dat_ref[h, c] for c, h in idx]), jnp.stack([deg_ref[h, c] for c, h in idx]))
        _, vjp = jax.vjp(lambda *a: _wy_fn(*a, inv=kept)[:6], *args)
        dq, dk, dv, dgc, dgr, dbc = vjp(cts)
        for b, (c, h) in enumerate(idx):
            dq_ref[rows(c), lanes(h)] = dq[b]
            dk_ref[rows(c), lanes(h)] = dk[b]
            dv_ref[rows(c), lanes(h)] = dv[b]
            dgr_ref[h, c] = dgr[b]
        for c in range(cb):
            acc = jnp.zeros((CHUNK, HEAD_DIM), F32)
            for h in range(ng):
                acc = jnp.where(lane == nf + h, dgc[c * ng + h], acc)
                acc = jnp.where(lane == nf + ng + h, dbc[c * ng + h], acc)
            dvals_ref[rows(c), :] = acc

    cblk = lambda o: pl.BlockSpec((cb * CHUNK, wd), lambda i: (i, o))
    col = pl.BlockSpec((cb * CHUNK, HEAD_DIM), lambda i: (i, 0))
    rowv = pl.BlockSpec((ng, cb, 1, CHUNK), lambda i: (0, i, 0, 0))
    return pl.pallas_call(
        wy_body, name="gdn_wy_bwd", grid=(nch // cb,),
        in_specs=[cblk(0), cblk(1), cblk(2), col, rowv, cblk(0), cblk(0), cblk(0), cblk(0),
                  pl.BlockSpec((ng, cb, CHUNK, CHUNK), lambda i: (0, i, 0, 0)),
                  pl.BlockSpec((ng, cb, 1, HEAD_DIM), lambda i: (0, i, 0, 0)),
                  pl.BlockSpec((ng, cb, CHUNK, CHUNK), lambda i: (0, i, 0, 0))],
        out_specs=[cblk(0), cblk(0), cblk(0), col, rowv],
        out_shape=[wide, wide, wide, jax.ShapeDtypeStruct((t, HEAD_DIM), F32),
                   jax.ShapeDtypeStruct((ng, nch, 1, CHUNK), F32)],
        compiler_params=_cparams(("parallel",)),
    )(qkv, qkv, qkv, vals, grow, *dwy, inv)


def _swiglu_fn(gate, up):
    return _silu(gate) * up


FFN_TN = 512


def _ffn_up(n2, wgu4):
    _, d, w = wgu4.shape
    t = n2.shape[0]
    tn = _tile(w, FFN_TN)
    nb = w // tn

    def body(a_ref, b_ref, gu_ref, act_ref):
        av = a_ref[...]
        gate = jnp.dot(av, b_ref[0], preferred_element_type=F32)
        up = jnp.dot(av, b_ref[1], preferred_element_type=F32)
        gu_ref[0] = gate.astype(BF16)
        gu_ref[1] = up.astype(BF16)
        act_ref[...] = _swiglu_fn(gate, up).astype(BF16)

    return pl.pallas_call(
        body, name="ffn_up", grid=(2, nb),
        in_specs=[pl.BlockSpec((t, d), lambda j, l: (0, 0)), pl.BlockSpec((2, d, tn), lambda j, l: (j, 0, l))],
        out_specs=[pl.BlockSpec((2, t, tn), lambda j, l: (j, 0, l)),
                   pl.BlockSpec((t, tn), lambda j, l: (0, j * nb + l))],
        out_shape=[jax.ShapeDtypeStruct((4, t, w), BF16), jax.ShapeDtypeStruct((t, 2 * w), BF16)],
        compiler_params=_cparams(("parallel", "parallel")),
    )(n2, wgu4)


def _ffn_dact(dh2, wd, gu, after):
    _, t, w = gu.shape
    d = dh2.shape[1]
    tn = _tile(w, FFN_TN)
    nb = w // tn

    def body(a_ref, b_ref, gu_ref, _, o_ref):
        dact = lax.dot_general(a_ref[...], b_ref[...], (((1,), (1,)), ((), ())), preferred_element_type=F32)
        _, vjp = jax.vjp(_swiglu_fn, gu_ref[0].astype(F32), gu_ref[1].astype(F32))
        dg, du = vjp(dact)
        o_ref[0] = dg.astype(BF16)
        o_ref[1] = du.astype(BF16)

    pair = pl.BlockSpec((2, t, tn), lambda j, l: (j, 0, l))
    return pl.pallas_call(
        body, name="ffn_dact", grid=(2, nb),
        in_specs=[pl.BlockSpec((t, d), lambda j, l: (0, 0)), pl.BlockSpec((tn, d), lambda j, l: (j * nb + l, 0)),
                  pair, pl.BlockSpec(after.shape, lambda j, l: (0, 0))],
        out_specs=pair, out_shape=jax.ShapeDtypeStruct(gu.shape, BF16),
        compiler_params=_cparams(("parallel", "parallel")),
    )(dh2, wd, gu, after)


def _loss_head(h2, target):
    t, d = h2.shape
    tr = _tile(t, 256, 8)

    def body(h_ref, t_ref, l_ref, d_ref, db_ref):
        @pl.when(pl.program_id(0) == 0)
        def _():
            l_ref[...] = jnp.zeros_like(l_ref)

        err = h_ref[...] - t_ref[...]
        d_ref[...] = err * (1.0 / d)
        db_ref[...] = (err * (1.0 / d)).astype(BF16)
        part = 0.5 * jnp.sum(jnp.mean(err * err, axis=-1, keepdims=True), axis=0, keepdims=True)
        lane = lax.broadcasted_iota(jnp.int32, (8, HEAD_DIM), 1)
        row = lax.broadcasted_iota(jnp.int32, (8, HEAD_DIM), 0)
        l_ref[...] += jnp.where((lane == 0) & (row == 0), part, 0.0)

    blk = pl.BlockSpec((tr, d), lambda r: (r, 0))
    return pl.pallas_call(
        body, name="loss_head", grid=(t // tr,), in_specs=[blk, blk],
        out_specs=[pl.BlockSpec((8, HEAD_DIM), lambda r: (0, 0)), blk, blk],
        out_shape=[jax.ShapeDtypeStruct((8, HEAD_DIM), F32), jax.ShapeDtypeStruct((t, d), F32),
                   jax.ShapeDtypeStruct((t, d), BF16)],
        compiler_params=_cparams(("arbitrary",)),
    )(h2, target)


def _adamw(w, g, m, v, *, g_fn=None, name):
    r, c = w.shape
    tr = _tile(r, max(8, (1 << 19) // c // 8 * 8), 8)
    gs = g if isinstance(g, tuple) else (g,)

    def body(w_ref, *refs):
        g_refs, (m_ref, v_ref, go_ref, d_ref, mo_ref, vo_ref) = refs[:len(gs)], refs[len(gs):]
        gr = g_refs[0][...] if g_fn is None else g_fn(*[ref[...] for ref in g_refs])
        mn = ADAM_B1 * m_ref[...] + (1.0 - ADAM_B1) * gr
        vn = ADAM_B2 * v_ref[...] + (1.0 - ADAM_B2) * (gr * gr)
        m_hat = mn / (1.0 - ADAM_B1 ** ADAM_STEP)
        v_hat = vn / (1.0 - ADAM_B2 ** ADAM_STEP)
        go_ref[...] = gr
        d_ref[...] = -ADAM_LR * (m_hat / (jnp.sqrt(v_hat) + ADAM_EPS) + ADAM_WD * w_ref[...])
        mo_ref[...] = mn
        vo_ref[...] = vn

    blk = pl.BlockSpec((tr, c), lambda i: (i, 0))
    gblks = [pl.BlockSpec((tr, gi.shape[1]), lambda i: (i, 0)) for gi in gs]
    return pl.pallas_call(
        body, name=name, grid=(r // tr,), in_specs=[blk] + gblks + [blk, blk], out_specs=[blk] * 4,
        out_shape=[jax.ShapeDtypeStruct((r, c), F32)] * 4,
        compiler_params=_cparams(("parallel",)),
    )(w, *gs, m, v)


class _Layout:
    def __init__(self, d):
        nh = d // HEAD_DIM
        self.nm = N_MEM_HEADS
        self.nf = (nh - self.nm) // 2
        self.ng = nh - self.nm - self.nf
        nf, ng, nm, hd = self.nf, self.ng, self.nm, HEAD_DIM
        self.o_fq, self.o_fk, self.o_fv, self.o_sm = 0, nf, 2 * nf, 3 * nf
        self.o_gq, self.o_gz, self.o_mq = 0, 3 * ng, 4 * ng
        self.cols_a = -(-(3 * nf + 1) // 4) * 4 * hd
        self.cols_b = -(-(4 * ng + nm) // 4) * 4 * hd
        self.cols = self.cols_a + self.cols_b
        sizes = [nf * hd, nf * hd, nf * hd, nf, 3 * ng * hd, ng * hd, ng, ng, nm * hd]
        starts = [sum(sizes[:i]) for i in range(len(sizes))]
        self.ref = list(zip(starts, sizes))
        self.in_cols = sum(sizes)

    def regroup(self, w):
        part = lambda i: w[:, self.ref[i][0]:self.ref[i][0] + self.ref[i][1]]
        a = [part(0), part(1), part(2), part(3), part(6), part(7)]
        b = [part(4), part(5), part(8)]
        pads = [self.cols_a - sum(p.shape[1] for p in a), self.cols_b - sum(p.shape[1] for p in b)]
        fill = [[jnp.zeros((w.shape[0], n), w.dtype)] if n else [] for n in pads]
        return jnp.concatenate(a + fill[0] + b + fill[1], axis=1)

    def ungroup(self, g):
        hd, nf, ng, nm = HEAD_DIM, self.nf, self.ng, self.nm
        sm, b0 = self.o_sm * hd, self.cols_a
        return jnp.concatenate([
            g[:, :3 * nf * hd], g[:, sm:sm + nf], g[:, b0:b0 + 3 * ng * hd],
            g[:, b0 + self.o_gz * hd:b0 + self.o_mq * hd], g[:, sm + nf:sm + nf + ng],
            g[:, sm + nf + ng:sm + nf + 2 * ng], g[:, b0 + self.o_mq * hd:b0 + (self.o_mq + nm) * hd]], axis=1)


def _lane_row(pieces):
    row = jnp.zeros((1, HEAD_DIM), F32)
    for off, a in pieces:
        row = lax.dynamic_update_slice(row, a.astype(F32), (0, off))
    return row


def _local_step(x, mem, target, prefetch, weights, reducer, sp):
    t, d = x.shape
    lay = _Layout(d)
    nf, ng, nm, hd = lay.nf, lay.ng, lay.nm, HEAD_DIM
    nch = t // CHUNK
    tq = _tile(t, 256)
    tk = tq

    u = _norm_fwd(x, 0, sp["norm_mix"], 1, d, BF16, name="norm_mix_fwd")
    prefetch("in_a", u)
    (win_a,) = weights("in_a", u)
    p_a = _mm(u, win_a, name="mm_in_a")
    pa = _lane_row([(nf, sp["gdn_a_log"])])
    pb = _lane_row([(0, sp["fox_f_bias"]), (nf, sp["gdn_dt_bias"])])
    vals, csum = _small_fwd(p_a, lay.o_sm, pa, pb, nf, ng)

    c_t = csum[:, :nf].T
    hp = _fox_heads(nf, 3)
    cr = c_t.reshape(nf, t // tk, 1, tk)
    cc = jnp.stack([jnp.pad(csum[:, g * hp:(g + 1) * hp], ((0, 0), (0, hd - hp))) for g in range(nf // hp)])
    fq = _norm_fwd(p_a, lay.o_fq, sp["fox_q_norm"], nf, hd, BF16, name="fox_qnorm_fwd")
    fk = _norm_fwd(p_a, lay.o_fk, sp["fox_k_norm"], nf, hd, BF16, name="fox_knorm_fwd")
    fv = p_a[:, lay.o_fv * hd:(lay.o_fv + nf) * hd].astype(BF16)
    o_fox, lse, mix = _fox_fwd(fq, fk, fv, cc, cr, nf, tq, tk, d)

    prefetch("in_b", lse)
    (win_b,) = weights("in_b", lse)
    prefetch("mixer", win_b)
    p = _mm(u, win_b, name="mm_in_b")
    wmkv, conv_taps = weights("mixer", p)
    sp = dict(sp, gdn_conv=conv_taps)
    qkv = _conv_fwd(p, lay.o_gq, sp["gdn_conv"], ng)
    grow = vals[:, nf:nf + ng].T.reshape(ng, nch, 1, CHUNK)
    o_g, states = _gdn_fwd(qkv, vals, grow, nf, ng)
    mix = _norm_fwd(o_g, 0, sp["gdn_out_norm"], ng, hd, BF16, z=p, zoff=lay.o_gz, into=mix, into_off=nf,
                    name="gdn_out_fwd")
    prefetch("out", mix)

    mem_n = _norm_fwd(mem, 0, sp["mem_norm"], 1, d, BF16, name="mem_norm_fwd")
    mkv = _mm(mem_n, wmkv, name="mm_memkv")
    tq_mem = _tile(t, 1024)
    mix = _mem_fwd(p, lay.o_mq, mkv, sp["mem_q_norm"], sp["mem_k_norm"], tq_mem, mix, nf + ng)
    prefetch("gate_up", mix)
    (wout,) = weights("out", mix)
    h1 = _mm(mix, wout, res=x, name="mm_out")
    n2 = _norm_fwd(h1, 0, sp["norm_ffn"], 1, d, BF16, name="norm_ffn_fwd")
    (wgu,) = weights("gate_up", n2)
    wgu4 = wgu.reshape(4, d, -1)
    gu, act = _ffn_up(n2, wgu4)
    prefetch("down", act)
    (wd,) = weights("down", act)
    h2 = _mm(act, wd, res=h1, name="mm_down")
    loss_blk, dh2, dh2_b = _loss_head(h2, target)

    g = {}
    token = reducer.pair("w_down", _mm(act, dh2_b, ta=True, out_dtype=BF16, name="mm_dw_down"))
    dgu = _ffn_dact(dh2_b, wd, gu, token)
    dw_gate_up = _mm(n2, dgu, ta=True, stack="out", out_dtype=BF16, name="mm_dw_gate_up").reshape(wgu.shape)
    token = reducer.pair("w_gate_up", dw_gate_up)
    dn2 = _mm(dgu, wgu4, tb=True, stack="sum", after=token, name="mm_dn2")
    token = reducer.ship("ffn", ["w_down", "w_gate_up"], dn2)
    dh1, g["norm_ffn"] = _norm_bwd(h1, 0, sp["norm_ffn"] + token[0, 0], dn2, 0, 1, d, res=dh2,
                                   name="norm_ffn_bwd")
    token = reducer.pair("w_out", _mm(mix, dh1, ta=True, out_dtype=BF16, name="mm_dw_out"))
    dmix = _mm(dh1, wout, tb=True, after=token, name="mm_dmix")

    dmq, dmk, dmv, g["mem_q_norm"], g["mem_k_norm"] = _mem_bwd(
        p, lay.o_mq, mkv, sp["mem_q_norm"], sp["mem_k_norm"], dmix, nf + ng, tq_mem)
    dmkv = jnp.concatenate([dmk, dmv], axis=1)
    token = reducer.pair("w_mem_kv", _mm(mem_n, dmkv, ta=True, out_dtype=BF16, name="mm_dw_memkv"))
    dmem_n = _mm(dmkv, wmkv, tb=True, after=token, name="mm_dmem")
    token = reducer.ship("mix", ["w_out", "w_mem_kv"], dmem_n)
    _, g["mem_norm"] = _norm_bwd(mem, 0, sp["mem_norm"], dmem_n, 0, 1, d, name="mem_norm_bwd")

    do_g, dgz, g["gdn_out_norm"] = _norm_bwd(o_g, 0, sp["gdn_out_norm"] + token[0, 0], dmix, nf, ng, hd, z=p,
                                             zoff=lay.o_gz, name="gdn_out_bwd")
    dq, dk, dv, dvals, dgr = _gdn_bwd(qkv, vals, grow, states, do_g, nf, ng)
    dgqkv, g["gdn_conv"] = _conv_bwd(p, lay.o_gq, sp["gdn_conv"], (dq, dk, dv), ng)

    dfq_n, dfk_n, dfv, dcc, dcr = _fox_bwd(fq, fk, fv, cc, cr, o_fox, lse, dmix, nf, tq, tk)
    dfq, g["fox_q_norm"] = _norm_bwd(p_a, lay.o_fq, sp["fox_q_norm"], dfq_n, 0, nf, hd, name="fox_qnorm_bwd")
    dfk, g["fox_k_norm"] = _norm_bwd(p_a, lay.o_fk, sp["fox_k_norm"], dfk_n, 0, nf, hd, name="fox_knorm_bwd")
    dc = dcc.reshape(nf, t).T + jnp.concatenate([dcr[g, :, :hp] for g in range(nf // hp)], axis=1)

    dvals = dvals + jnp.pad(dgr.reshape(ng, t).T, ((0, 0), (nf, hd - nf - ng)))
    dcsum = jnp.pad(dc, ((0, 0), (0, hd - nf)))
    dsm, dpa, dpb = _small_bwd(p_a, lay.o_sm, pa, pb, dvals, dcsum, nf, ng)
    g["fox_f_bias"] = dpb[:, :nf]
    g["gdn_dt_bias"] = dpb[:, nf:nf + ng]
    g["gdn_a_log"] = dpa[:, nf:nf + ng]

    zeros = lambda n: jnp.zeros((t, n), F32)
    dp_a = jnp.concatenate([dfq, dfk, dfv, dsm, zeros(lay.cols_a - (lay.o_sm + 1) * hd)], axis=1).astype(BF16)
    dp_b = jnp.concatenate([dgqkv, dgz, dmq, zeros(lay.cols_b - (lay.o_mq + nm) * hd)], axis=1).astype(BF16)
    token = reducer.pair("w_in_a", _mm(u, dp_a, ta=True, out_dtype=BF16, name="mm_dw_in_a"))
    token = reducer.pair("w_in_b", _mm(u, dp_b, ta=True, out_dtype=BF16, after=token, name="mm_dw_in_b"))
    du = _mm(dp_a, win_a, tb=True, after=token, name="mm_du_a")
    token = reducer.ship("in", ["w_in_a", "w_in_b"], du)
    du = _mm(dp_b, win_b, tb=True, res=du, after=token, name="mm_du_b")
    dx, g["norm_mix"] = _norm_bwd(x, 0, sp["norm_mix"], du, 0, 1, d, res=dh1, name="norm_mix_bwd")
    return loss_blk, dx, g


ANY = pl.BlockSpec(memory_space=pl.ANY)


def _me():
    x, y, c = lax.axis_index("x"), lax.axis_index("y"), lax.axis_index("c")
    chips = [(1 - x, y), (x, 1 - y), (1 - x, 1 - y)]
    return x, y, c, chips


def _slot(axis, k):
    return k if axis == 0 else 2 * (k % 2) + k // 2


def _slab(ref, axis, rows, cols, k, h):
    half = rows // 2
    return ref.at[pl.ds(_slot(axis, k) * rows + h * half, half), :]


def _remote(src, dst, send_sem, recv_sem, dev):
    return pltpu.make_async_remote_copy(src_ref=src, dst_ref=dst, send_sem=send_sem, recv_sem=recv_sem,
                                        device_id=dev, device_id_type=MESH)


HBM = pl.BlockSpec(memory_space=pltpu.HBM)
SEM = pl.BlockSpec(memory_space=pltpu.SEMAPHORE)
SPLIT = pltpu.CompilerParams(has_side_effects=pltpu.SideEffectType.DATAFLOW_SIDE_EFFECTING)
TOKEN = jax.ShapeDtypeStruct((8, HEAD_DIM), F32)


def _in_hbm(v):
    return pltpu.with_memory_space_constraint(v, pltpu.HBM)


def _cast_place(shard, axis, name, col_fn=None, out_cols=None, after=None):
    r, c = shard.shape
    oc = out_cols or c
    tr = _tile(r, 512 if col_fn is None else 64, 16)
    tc = _tile(c, 2048) if col_fn is None else c
    otc = tc if col_fn is None else oc
    nb = r // tr
    chip = 2 * lax.axis_index("x") + lax.axis_index("y")
    slot = jnp.reshape(_slot(axis, chip), (1,)).astype(jnp.int32)

    def body(slot_ref, x_ref, *rest):
        x = x_ref[...]
        rest[-1][...] = (x if col_fn is None else col_fn(x)).astype(BF16)

    extra = [] if after is None else [after]
    return pl.pallas_call(
        body, name=name,
        grid_spec=pltpu.PrefetchScalarGridSpec(
            num_scalar_prefetch=1, grid=(nb, c // tc),
            in_specs=[pl.BlockSpec((tr, tc), lambda i, l, s: (i, l))] + [ANY] * len(extra),
            out_specs=pl.BlockSpec((tr, otc), lambda i, l, s: (s[0] * nb + i, l))),
        out_shape=jax.ShapeDtypeStruct((4 * r, oc), BF16),
        compiler_params=_cparams(("parallel", "parallel")),
    )(slot, shard, *extra)


def _gather_start(bufs, axes, shapes, groups, name):
    n = len(bufs)

    def body(*refs):
        dst = refs[n:2 * n]
        sems = refs[2 * n:2 * n + 2 * len(groups)]
        token = refs[-1]
        x, y, c, chips = _me()
        k = 2 * x + y
        for gi, ws in enumerate(groups):
            for i, w in enumerate(ws):
                r, cl = shapes[w]
                place = _slab(dst[w], axes[w], r, cl, k, c)
                for j, (px, py) in enumerate(chips):
                    _remote(place, place, sems[2 * gi].at[3 * i + j], sems[2 * gi + 1].at[3 * i + j],
                            (px, py, c)).start()
        token[...] = jnp.zeros_like(token)

    sem_shapes = [pltpu.SemaphoreType.DMA((3 * len(ws),)) for ws in groups for _ in range(2)]
    outs = pl.pallas_call(
        body, name=name, in_specs=[HBM] * n,
        out_specs=[HBM] * n + [SEM] * len(sem_shapes) + [pl.BlockSpec(memory_space=pltpu.VMEM)],
        out_shape=[pltpu.HBM(b.shape, b.dtype) for b in bufs] + sem_shapes + [TOKEN],
        input_output_aliases={w: w for w in range(n)}, compiler_params=SPLIT,
    )(*[_in_hbm(b) for b in bufs])
    sems = outs[n:-1]
    return outs[:n], [(sems[2 * g], sems[2 * g + 1]) for g in range(len(groups))], outs[-1]


def _gather_wait(bufs, axes, shapes, sems, after, name):
    n = len(bufs)

    def body(*refs):
        send_sems, recv_sems = refs[n], refs[n + 1]
        dst = refs[n + 3:]
        x, y, c, chips = _me()
        k = 2 * x + y
        for i in range(n):
            r, cl = shapes[i]
            for j, (px, py) in enumerate(chips):
                got = _slab(dst[i], axes[i], r, cl, 2 * px + py, c)
                _remote(got, got, send_sems.at[3 * i + j], recv_sems.at[3 * i + j], (px, py, c)).wait_recv()
        for i in range(n):
            r, cl = shapes[i]
            mine = _slab(dst[i], axes[i], r, cl, k, c)
            for j, (px, py) in enumerate(chips):
                _remote(mine, mine, send_sems.at[3 * i + j], recv_sems.at[3 * i + j], (px, py, c)).wait_send()

    return pl.pallas_call(
        body, name=name, in_specs=[HBM] * n + [SEM, SEM, ANY], out_specs=[HBM] * n,
        out_shape=[pltpu.HBM(b.shape, b.dtype) for b in bufs],
        input_output_aliases={i: i for i in range(n)}, compiler_params=SPLIT,
    )(*bufs, sems[0], sems[1], after)


def _split_start(name, arrays, geometry, count):
    n = len(arrays)

    def body(*refs):
        send, recv, token = refs[2 * n:]
        for i, (src, dst, _, dev) in enumerate(geometry(refs[n:2 * n])):
            _remote(src, dst, send.at[i], recv.at[i], dev).start()
        token[...] = jnp.zeros_like(token)

    sem = pltpu.SemaphoreType.DMA((count,))
    outs = pl.pallas_call(
        body, name=name, in_specs=[HBM] * n,
        out_specs=[HBM] * n + [SEM, SEM, pl.BlockSpec(memory_space=pltpu.VMEM)],
        out_shape=[pltpu.HBM(v.shape, v.dtype) for v in arrays] + [sem, sem, TOKEN],
        input_output_aliases={i: i for i in range(n)}, compiler_params=SPLIT,
    )(*[_in_hbm(v) for v in arrays])
    return list(outs[:n]), (outs[n], outs[n + 1]), outs[-1]


def _split_wait(name, arrays, sems, after, geometry):
    n = len(arrays)

    def body(*refs):
        send, recv = refs[n], refs[n + 1]
        copies = geometry(refs[n + 3:])
        for i, (_, _, land, dev) in enumerate(copies):
            _remote(land, land, send.at[i], recv.at[i], dev).wait_recv()
        for i, (src, _, _, dev) in enumerate(copies):
            _remote(src, src, send.at[i], recv.at[i], dev).wait_send()

    return list(pl.pallas_call(
        body, name=name, in_specs=[HBM] * n + [SEM, SEM, ANY], out_specs=[HBM] * n,
        out_shape=[pltpu.HBM(v.shape, v.dtype) for v in arrays],
        input_output_aliases={i: i for i in range(n)}, compiler_params=SPLIT,
    )(*arrays, sems[0], sems[1], after))


def _forward_geometry(axes, shapes):
    def geometry(bufs):
        x, y, c, chips = _me()
        out = []
        for i, buf in enumerate(bufs):
            r, cl = shapes[i]
            for px, py in chips:
                got = _slab(buf, axes[i], r, cl, 2 * px + py, c)
                out.append((got, got, _slab(buf, axes[i], r, cl, 2 * px + py, 1 - c), (x, y, 1 - c)))
        return out
    return geometry


def _pair_geometry(axes, shapes):
    def geometry(refs):
        n = len(refs) // 2
        x, y, c, _ = _me()
        out = []
        for w in range(n):
            r, cl = shapes[w]
            for j in range(4):
                land = refs[n + w].at[j]
                out.append((_slab(refs[w], axes[w], r, cl, j, 1 - c), land, land, (x, y, 1 - c)))
        return out
    return geometry


def _after_all(name, token, *arrays):
    def body(*refs):
        refs[-1][...] = jnp.zeros_like(refs[-1])

    return pl.pallas_call(
        body, name=name, in_specs=[ANY] * (1 + len(arrays)), out_specs=pl.BlockSpec(memory_space=pltpu.VMEM),
        out_shape=TOKEN,
    )(token, *arrays)


def _swap_geometry(bufs):
    x, y, c, _ = _me()
    return [(b.at[c], b.at[c], b.at[1 - c], (x, y, 1 - c)) for b in bufs]


def _chip_start(parts, tag):
    n = len(parts)

    def body(*refs):
        src, land = refs[2 * n:3 * n], refs[3 * n:4 * n]
        send_sems, recv_sems, token = refs[4 * n:]
        x, y, c, chips = _me()
        k = 2 * x + y
        for w in range(n):
            for j, (px, py) in enumerate(chips):
                _remote(src[w].at[2 * px + py], land[w].at[k], send_sems.at[3 * w + j], recv_sems.at[3 * w + j],
                        (px, py, c)).start()
        token[...] = jnp.zeros_like(token)

    lands = [lax.empty(p.shape, p.dtype) for p in parts]
    sem = pltpu.SemaphoreType.DMA((3 * n,))
    outs = pl.pallas_call(
        body, name="reduce_ici_start_" + tag, in_specs=[HBM] * (2 * n),
        out_specs=[HBM] * (2 * n) + [SEM, SEM, pl.BlockSpec(memory_space=pltpu.VMEM)],
        out_shape=[pltpu.HBM(p.shape, p.dtype) for p in parts + lands] + [sem, sem, TOKEN],
        input_output_aliases={i: i for i in range(2 * n)}, compiler_params=SPLIT,
    )(*[_in_hbm(v) for v in parts + lands])
    return outs[:n], outs[n:2 * n], outs[2 * n], outs[2 * n + 1], outs[-1]


def _chip_wait(parts, lands, send_sems, recv_sems, after, tag):
    n = len(parts)

    def body(*refs):
        send, recv = refs[2 * n], refs[2 * n + 1]
        src, land = refs[2 * n + 3:3 * n + 3], refs[3 * n + 3:]
        x, y, c, chips = _me()
        for w in range(n):
            for j, (px, py) in enumerate(chips):
                got = land[w].at[2 * px + py]
                _remote(got, got, send.at[3 * w + j], recv.at[3 * w + j], (px, py, c)).wait_recv()
        for w in range(n):
            for j, (px, py) in enumerate(chips):
                sent = src[w].at[2 * px + py]
                _remote(sent, sent, send.at[3 * w + j], recv.at[3 * w + j], (px, py, c)).wait_send()

    outs = pl.pallas_call(
        body, name="reduce_ici_wait_" + tag, in_specs=[HBM] * (2 * n) + [SEM, SEM, ANY], out_specs=[HBM] * (2 * n),
        out_shape=[pltpu.HBM(p.shape, p.dtype) for p in parts + lands],
        input_output_aliases={i: i for i in range(2 * n)}, compiler_params=SPLIT,
    )(*parts, *lands, send_sems, recv_sems, after)
    chip = 2 * lax.axis_index("x") + lax.axis_index("y")
    return [lax.dynamic_update_slice(s, lax.dynamic_index_in_dim(p, chip, 0, keepdims=True), (chip, 0, 0))
            for p, s in zip(outs[:n], outs[n:])]


def _half_swap(halves, tag):
    n = len(halves)
    core = lax.axis_index("c")
    bufs = [lax.dynamic_update_slice(lax.empty((2,) + h.shape, h.dtype), h[None], (core, 0, 0)) for h in halves]

    def body(*refs):
        dst = refs[n:2 * n]
        send_sems, recv_sems = refs[2 * n:]
        x, y, c, _ = _me()
        sibling = (x, y, 1 - c)
        cps = []
        for w in range(n):
            cp = _remote(dst[w].at[c], dst[w].at[c], send_sems.at[w], recv_sems.at[w], sibling)
            cp.start()
            cps.append(cp)
        for w in range(n):
            other = dst[w].at[1 - c]
            _remote(other, other, send_sems.at[w], recv_sems.at[w], sibling).wait_recv()
        for cp in cps:
            cp.wait_send()

    outs = pl.pallas_call(
        body, name="reduce_half_swap_" + tag, in_specs=[ANY] * n, out_specs=[ANY] * n,
        out_shape=[jax.ShapeDtypeStruct(b.shape, b.dtype) for b in bufs],
        input_output_aliases={w: w for w in range(n)},
        scratch_shapes=[pltpu.SemaphoreType.DMA((n,)), pltpu.SemaphoreType.DMA((n,))],
    )(*bufs)
    return [o.reshape(2 * o.shape[1], o.shape[2]) for o in outs]


def _add_parts(full, axis, rows, sib, name):
    _, r, c = sib.shape
    tr, tc = _tile(r, 1024, 16), _tile(c, 2048)
    nb = r // tr
    core = jnp.reshape(lax.axis_index("c"), (1,)).astype(jnp.int32)

    def body(c_ref, a_ref, b_ref, o_ref):
        o_ref[0] = (a_ref[...].astype(F32) + b_ref[0].astype(F32)).astype(BF16)

    blk = pl.BlockSpec((1, tr, tc), lambda j, i, l, cr: (j, i, l))
    return pl.pallas_call(
        body, name=name,
        grid_spec=pltpu.PrefetchScalarGridSpec(
            num_scalar_prefetch=1, grid=(4, nb, c // tc),
            in_specs=[pl.BlockSpec((tr, tc), lambda j, i, l, cr: ((_slot(axis, j) * 2 + cr[0]) * nb + i, l)), blk],
            out_specs=blk),
        out_shape=jax.ShapeDtypeStruct(sib.shape, BF16),
        compiler_params=_cparams(("parallel", "parallel", "parallel")),
    )(core, full, sib)


def _sum_slots(a, name):
    _, r, c = a.shape
    tr, tc = _tile(r, 512, 8), _tile(c, 2048)

    def body(a_ref, o_ref):
        v = a_ref[...].astype(F32)
        o_ref[...] = ((v[0] + v[1]) + v[2]) + v[3]

    return pl.pallas_call(
        body, name=name, grid=(r // tr, c // tc),
        in_specs=[pl.BlockSpec((4, tr, tc), lambda i, l: (0, i, l))],
        out_specs=pl.BlockSpec((tr, tc), lambda i, l: (i, l)),
        out_shape=jax.ShapeDtypeStruct((r, c), F32),
        compiler_params=_cparams(("parallel", "parallel")),
    )(a)


class _Reducer:
    def __init__(self, spec):
        self.spec = spec
        self.paired = {}
        self.pending = []

    def pair(self, name, full):
        ax, shp = self.spec[name]
        land = lax.empty((4, shp[0] // 2, shp[1]), full.dtype)
        arrays, sems, token = _split_start("reduce_pair_start_" + name, [full, land], _pair_geometry([ax], [shp]), 4)
        self.paired[name] = (arrays, sems)
        return token

    def ship(self, tag, names, after):
        parts = []
        for n in names:
            ax, shp = self.spec[n]
            arrays, sems = self.paired.pop(n)
            full, sib = _split_wait("reduce_pair_wait_" + n, arrays, sems, after, _pair_geometry([ax], [shp]))
            parts.append(_add_parts(full, ax, shp[0], sib, name=f"reduce_add_{n}"))
        parts, lands, send, recv, token = _chip_start(parts, tag)
        self.pending.append((tag, names, parts, lands, send, recv))
        return token

    def finish(self, after, tags):
        out = {}
        for tag, names, parts, lands, send, recv in [p for p in self.pending if p[0] in tags]:
            slots = _chip_wait(parts, lands, send, recv, after, tag)
            halves = [_sum_slots(s, name=f"reduce_sum_{n}") for n, s in zip(names, slots)]
            out.update(zip(names, _half_swap(halves, tag)))
        return out

    def finish_start(self, after, tag):
        (_, names, parts, lands, send, recv), = [p for p in self.pending if p[0] == tag]
        slots = _chip_wait(parts, lands, send, recv, after, tag)
        halves = [_sum_slots(s, name=f"reduce_sum_{n}") for n, s in zip(names, slots)]
        core = lax.axis_index("c")
        bufs = [lax.dynamic_update_slice(lax.empty((2,) + h.shape, h.dtype), h[None], (core, 0, 0)) for h in halves]
        bufs, sems, _ = _split_start("reduce_half_swap_start_" + tag, bufs, _swap_geometry, len(bufs))
        return tag, names, bufs, sems

    def swap_wait(self, started, after):
        tag, names, bufs, sems = started
        outs = _split_wait("reduce_half_swap_wait_" + tag, bufs, sems, after, _swap_geometry)
        return dict(zip(names, [o.reshape(2 * o.shape[1], o.shape[2]) for o in outs]))


def _allreduce_small(pack, after):
    rows = pack.shape[0]

    def body(p_ref, _, o_ref, slots, send_sems, recv_sems):
        x, y, c, _ = _me()
        me = 4 * x + 2 * y + c
        slots[me] = p_ref[...]
        cps = []
        for r in range(1, 8):
            peer = (x ^ (r >> 2), y ^ ((r >> 1) & 1), c ^ (r & 1))
            cp = _remote(p_ref, slots.at[me], send_sems.at[r - 1], recv_sems.at[r - 1], peer)
            cp.start()
            cps.append(cp)
        for r in range(1, 8):
            frm = me ^ r
            _remote(slots.at[frm], slots.at[frm], send_sems.at[r - 1], recv_sems.at[r - 1], (x, y, c)).wait_recv()
        for cp in cps:
            cp.wait_send()
        acc = slots[0]
        for s in range(1, 8):
            acc = acc + slots[s]
        o_ref[...] = acc

    vm = pl.BlockSpec(memory_space=pltpu.VMEM)
    return pl.pallas_call(
        body, name="allreduce_small", in_specs=[vm, ANY], out_specs=vm,
        out_shape=jax.ShapeDtypeStruct(pack.shape, F32),
        scratch_shapes=[pltpu.VMEM((8, rows, HEAD_DIM), F32), pltpu.SemaphoreType.DMA((7,)),
                        pltpu.SemaphoreType.DMA((7,))],
    )(pack, after)


_ROWS = ["norm_mix", "norm_ffn", "mem_norm", "fox_q_norm", "fox_k_norm", "gdn_out_norm", "mem_q_norm",
         "mem_k_norm", "fox_f_bias", "gdn_a_log", "gdn_dt_bias"]


def _pack_rows(vals):
    out = []
    for name in _ROWS:
        v = vals[name].reshape(-1)
        n = -(-v.shape[0] // HEAD_DIM) * HEAD_DIM
        out.append(jnp.pad(v, (0, n - v.shape[0])).reshape(-1, HEAD_DIM))
    return jnp.concatenate(out, axis=0)


def _unpack_rows(pack, like):
    out, r = {}, 0
    for name in _ROWS:
        n = like[name].shape[-1]
        nr = -(-n // HEAD_DIM)
        out[name] = pack[r:r + nr].reshape(1, -1)[:, :n]
        r += nr
    return out, r


def kernel(x, mem, norm_mix, w_in, fox_f_bias, fox_q_norm, fox_k_norm, gdn_conv, gdn_a_log, gdn_dt_bias, gdn_out_norm, mem_norm, w_mem_kv, mem_q_norm, mem_k_norm, w_out, norm_ffn, w_gate_up, w_down, loss_target, m_norm_mix, m_w_in, m_fox_f_bias, m_fox_q_norm, m_fox_k_norm, m_gdn_conv, m_gdn_a_log, m_gdn_dt_bias, m_gdn_out_norm, m_mem_norm, m_w_mem_kv, m_mem_q_norm, m_mem_k_norm, m_w_out, m_norm_ffn, m_w_gate_up, m_w_down, v_norm_mix, v_w_in, v_fox_f_bias, v_fox_q_norm, v_fox_k_norm, v_gdn_conv, v_gdn_a_log, v_gdn_dt_bias, v_gdn_out_norm, v_mem_norm, v_w_mem_kv, v_mem_q_norm, v_mem_k_norm, v_w_out, v_norm_ffn, v_w_gate_up, v_w_down):
    a = dict(locals())
    d = x.shape[-1]
    lay = _Layout(d)
    chip = 2 * lax.axis_index("x") + lax.axis_index("y")
    small = {n: a[n] for n in _ROWS}
    big = ["w_in", "w_mem_kv", "w_out", "w_gate_up", "w_down"]
    axes = [0, 0, 0, 1, 0]

    conv_cols = gdn_conv.shape[-1]
    conv_n = CONV_WIDTH * conv_cols
    conv_rows = -(-conv_n // HEAD_DIM)
    conv_blk = jnp.pad(gdn_conv.reshape(-1), (0, 32 * HEAD_DIM - conv_n)).reshape(32, HEAD_DIM)
    axis_of = dict(zip(big, axes), conv=0, w_in_a=0, w_in_b=0)
    shape_of = {n: a[n].shape[1:] for n in big[1:]}
    shape_of.update(w_in_a=(w_in.shape[1], lay.cols_a), w_in_b=(w_in.shape[1], lay.cols_b), conv=conv_blk.shape)
    placed = {"w_in_a": _cast_place(w_in[0], 0, "cast_w_in_a", lambda v: lay.regroup(v)[:, :lay.cols_a], lay.cols_a),
              "conv": lax.dynamic_update_slice(lax.empty((4 * 32, HEAD_DIM), F32), conv_blk, (chip * 32, 0))}
    grouped = {"in_a": ["w_in_a"], "in_b": ["w_in_b"], "mixer": ["w_mem_kv", "conv"], "out": ["w_out"],
               "gate_up": ["w_gate_up"], "down": ["w_down"]}
    inflight = {}

    def start(tags, name):
        names = [n for t in tags for n in grouped[t]]
        bufs, sems, token = _gather_start([placed[n] for n in names], [axis_of[n] for n in names],
                                          [shape_of[n] for n in names],
                                          [[names.index(n) for n in grouped[t]] for t in tags], name)
        for t, pair in zip(tags, sems):
            inflight[t] = ([bufs[names.index(n)] for n in grouped[t]], pair)
        return token

    first = start(["in_a"], "gather_ici_start_in")
    placed["w_in_b"] = _cast_place(w_in[0], 0, "cast_w_in_b", lambda v: lay.regroup(v)[:, lay.cols_a:], lay.cols_b,
                                   after=first)
    placed.update({n: _cast_place(a[n][0], axis_of[n], "cast_" + n, after=first) for n in big[1:]})
    all_started = start(["in_b", "mixer", "out", "gate_up", "down"], "gather_ici_start_rest")
    all_started = _after_all("moments_ready", all_started, m_w_in[0], v_w_in[0])

    forwarding = {}

    def prefetch(tag, after):
        bufs, sem_pair = inflight.pop(tag)
        ax, shp = [axis_of[n] for n in grouped[tag]], [shape_of[n] for n in grouped[tag]]
        got = _gather_wait(bufs, ax, shp, sem_pair, all_started if tag == "in_a" else after,
                           "gather_ici_wait_" + tag)
        geometry = _forward_geometry(ax, shp)
        got, sems, _ = _split_start("gather_forward_start_" + tag, got, geometry, 3 * len(got))
        forwarding[tag] = (got, sems, geometry)

    def weights(tag, after):
        got, sems, geometry = forwarding.pop(tag)
        got = _split_wait("gather_forward_wait_" + tag, got, sems, after, geometry)
        if tag != "mixer":
            return got
        taps = got[1].reshape(4, 32 * HEAD_DIM)[:, :conv_n].reshape(4, CONV_WIDTH, conv_cols)
        return got[0], jnp.transpose(taps, (1, 0, 2)).reshape(CONV_WIDTH, 4 * conv_cols)

    sp = dict(small)
    reducer = _Reducer({n: (axis_of[n], shape_of[n]) for n in big[1:] + ["w_in_a", "w_in_b"]})
    loss_blk, dx, g = _local_step(x[0], mem[0], loss_target[0], prefetch, weights, reducer, sp)

    gsmall = {n: g[n] for n in _ROWS}
    pack = jnp.concatenate([_pack_rows(gsmall), g["gdn_conv"].reshape(-1, HEAD_DIM), loss_blk], axis=0)
    pack = jnp.pad(pack, ((0, -pack.shape[0] % 8), (0, 0)))
    out = {"grad_x": dx[None]}

    def adamw_shards(reduced):
        if "w_in_a" in reduced:
            reduced = {"w_in": (reduced["w_in_a"], reduced["w_in_b"])}
        for n, gsh in reduced.items():
            join = (lambda ga, gb: lay.ungroup(jnp.concatenate([ga, gb], axis=1))) if n == "w_in" else None
            res = _adamw(a[n][0], gsh, a["m_" + n][0], a["v_" + n][0], g_fn=join, name="adamw_" + n)
            for pre, r in zip(["grad_", "delta_", "new_m_", "new_v_"], res):
                out[pre + n] = r[None]
        return res[0]

    mix_swap = reducer.finish_start(dx, "mix")
    ffn_swap = reducer.finish_start(mix_swap[2][0], "ffn")
    done = adamw_shards(reducer.swap_wait(mix_swap, ffn_swap[2][0]))
    done = adamw_shards(reducer.swap_wait(ffn_swap, done))
    tot = _allreduce_small(pack, done)
    gs, r0 = _unpack_rows(tot, small)
    conv_g = tot[r0:r0 + CONV_WIDTH * 4 * conv_cols // HEAD_DIM].reshape(CONV_WIDTH, 4 * conv_cols)
    gs_conv = lax.dynamic_slice_in_dim(conv_g, chip * conv_cols, conv_cols, axis=1)
    out["loss"] = tot[r0 + CONV_WIDTH * 4 * conv_cols // HEAD_DIM, 0]
    adamw_shards(reducer.finish(tot, ("in",)))
    conv_pad = lambda v: jnp.pad(v.reshape(-1), (0, conv_rows * HEAD_DIM - conv_n)).reshape(conv_rows, HEAD_DIM)
    packs = []
    for src, cv in [(small, gdn_conv), (gs, gs_conv), ({n: a["m_" + n] for n in _ROWS}, m_gdn_conv),
                    ({n: a["v_" + n] for n in _ROWS}, v_gdn_conv)]:
        packs.append(jnp.concatenate([_pack_rows(src), conv_pad(cv)], axis=0))
    res = _adamw(*packs, name="adamw_small")
    for pre, r in zip(["grad_", "delta_", "new_m_", "new_v_"], res):
        vals, r1 = _unpack_rows(r, small)
        for n in _ROWS:
            out[pre + n] = vals[n]
        out[pre + "gdn_conv"] = r[r1:r1 + conv_rows].reshape(-1)[:conv_n].reshape(gdn_conv.shape)
    names = ["norm_mix", "w_in", "fox_f_bias", "fox_q_norm", "fox_k_norm", "gdn_conv", "gdn_a_log", "gdn_dt_bias",
             "gdn_out_norm", "mem_norm", "w_mem_kv", "mem_q_norm", "mem_k_norm", "w_out", "norm_ffn", "w_gate_up",
             "w_down"]
    return (out["loss"], out["grad_x"], *[out[p + n] for p in ["grad_", "delta_", "new_m_", "new_v_"] for n in names])
```

```python
import functools
import math

import jax
import jax.numpy as jnp
from jax import lax
from jax.experimental import pallas as pl
from jax.experimental.pallas import tpu as pltpu

F32, BF16 = jnp.float32, jnp.bfloat16
HEAD_DIM = 128
CHUNK = 64
N_MEM_HEADS = 4
CONV_WIDTH = 4
NORM_EPS = 1e-6
ADAM_LR, ADAM_B1, ADAM_B2, ADAM_EPS, ADAM_WD, ADAM_STEP = 0.001, 0.9, 0.999, 1e-08, 0.01, 10
VMEM_LIMIT = 48 * 1024 * 1024
NEG = -1e30
MESH = pl.DeviceIdType.MESH


def _cparams(sem=None, **kw):
    if sem is not None:
        kw["dimension_semantics"] = sem
    return pltpu.CompilerParams(vmem_limit_bytes=VMEM_LIMIT, **kw)


def _tile(n, target, mult=128):
    best = None
    d = mult
    while d <= min(n, target):
        if n % d == 0:
            best = d
        d += mult
    return best if best is not None else n


def _dot(a, b, dims, hi):
    if a.ndim == 3:
        dn = (((dims[0][0] + 1,), (dims[1][0] + 1,)), ((0,), (0,)))
    else:
        dn = (dims, ((), ()))
    if hi is not None:
        return lax.dot_general(a, b, dn, precision=hi, preferred_element_type=F32)
    return lax.dot_general(a.astype(BF16), b.astype(BF16), dn, preferred_element_type=F32)


def _make_dots(hi, cotangent=None):
    @jax.custom_vjp
    def nn(a, b):
        return _dot(a, b, ((1,), (0,)), hi)

    @jax.custom_vjp
    def nt(a, b):
        return _dot(a, b, ((1,), (1,)), hi)

    @jax.custom_vjp
    def tn(a, b):
        return _dot(a, b, ((0,), (0,)), hi)

    bnn, bnt, btn = cotangent or (nn, nt, tn)
    nn.defvjp(lambda a, b: (nn(a, b), (a, b)), lambda r, g: (bnt(g, r[1]), btn(r[0], g)))
    nt.defvjp(lambda a, b: (nt(a, b), (a, b)), lambda r, g: (bnn(g, r[1]), btn(g, r[0])))
    tn.defvjp(lambda a, b: (tn(a, b), (a, b)), lambda r, g: (bnt(r[1], g), bnn(r[0], g)))
    return nn, nt, tn


_nn, _nt, _tn = _make_dots(None)
_nn_hi, _nt_hi, _tn_hi = _make_dots(lax.Precision.HIGHEST)


def _sigmoid(x):
    return jax.nn.sigmoid(x)


@jax.custom_vjp
def _softplus(x):
    return jnp.maximum(x, 0.0) + jnp.log(1.0 + jnp.exp(-jnp.abs(x)))


_softplus.defvjp(lambda x: (_softplus(x), x), lambda x, g: (g * _sigmoid(x),))


def _silu(x):
    return x * _sigmoid(x)


def _rms_fn(x, gain, z=None):
    y = x * lax.rsqrt(jnp.mean(x * x, axis=-1, keepdims=True) + NORM_EPS) * gain
    if z is not None:
        y = y * _silu(z)
    return y


def _mm(a, b, *, ta=False, tb=False, out_dtype=F32, res=None, stack=None, after=None, name):
    a2, b2 = a.shape[-2:], b.shape[-2:]
    ns = b.shape[0] if stack else 1
    m = a2[1] if ta else a2[0]
    k = a2[0] if ta else a2[1]
    n = b2[0] if tb else b2[1]
    assert k == (b2[1] if tb else b2[0])
    tm, tn, tk = _mm_tiles(m, n, k, ns if stack == "sum" else 1, a.dtype.itemsize, b.dtype.itemsize,
                           jnp.dtype(out_dtype).itemsize, res is not None)
    nk = k // tk
    single = nk == 1 and stack != "sum"
    dims = ((0 if ta else 1,), (1 if tb else 0,))
    if stack == "sum":
        order = lambda g0, g1, g2, g3: (g2, g0, g1, g3)
        grid = (m // tm, n // tn, ns, nk)
    else:
        order = lambda g0, g1, g2, g3: (g0, g1, g2, g3)
        grid = (ns, m // tm, n // tn, nk)

    def body(*refs):
        if after is not None:
            refs = refs[:2 + (res is not None)] + refs[3 + (res is not None):]
        if single:
            a_ref, b_ref = refs[:2]
            r = lax.dot_general(a_ref[...].astype(BF16), b_ref[...].astype(BF16), (dims, ((), ())),
                                preferred_element_type=F32)
            if res is not None:
                r = r + refs[2][...]
            refs[-1][...] = r.astype(out_dtype)
            return
        if res is None:
            a_ref, b_ref, o_ref, acc = refs
        else:
            a_ref, b_ref, r_ref, o_ref, acc = refs
        s, _, _, kk = order(*[pl.program_id(d) for d in range(4)])
        first = kk == 0
        last = kk == nk - 1
        if stack == "sum":
            first, last = first & (s == 0), last & (s == ns - 1)

        @pl.when(first)
        def _():
            acc[...] = jnp.zeros_like(acc)

        acc[...] += lax.dot_general(a_ref[...].astype(BF16), b_ref[...].astype(BF16), (dims, ((), ())),
                                    preferred_element_type=F32)

        @pl.when(last)
        def _():
            r = acc[...]
            if res is not None:
                r = r + r_ref[...]
            o_ref[...] = r.astype(out_dtype)

    def spec(shape, idx, stacked):
        if stacked:
            return pl.BlockSpec((None,) + shape, lambda *g: (order(*g)[0],) + idx(*order(*g)))
        return pl.BlockSpec(shape, lambda *g: idx(*order(*g)))

    a_spec = (spec((tk, tm), lambda s, i, j, kk: (kk, i), stack == "sum") if ta
              else spec((tm, tk), lambda s, i, j, kk: (i, kk), stack == "sum"))
    b_spec = (spec((tn, tk), lambda s, i, j, kk: (j, kk), bool(stack)) if tb
              else spec((tk, tn), lambda s, i, j, kk: (kk, j), bool(stack)))
    o_spec = spec((tm, tn), lambda s, i, j, kk: (i, j), stack == "out")
    ins, specs = [a, b], [a_spec, b_spec]
    if res is not None:
        ins.append(res)
        specs.append(o_spec)
    if after is not None:
        ins.append(after)
        specs.append(pl.BlockSpec(after.shape, lambda *g: (0,) * after.ndim))
    sem = (("parallel", "parallel", "arbitrary", "arbitrary") if stack == "sum"
           else ("parallel", "parallel", "parallel", "arbitrary"))
    return pl.pallas_call(
        body, name=name, grid=grid, in_specs=specs, out_specs=o_spec,
        out_shape=jax.ShapeDtypeStruct(((ns,) if stack == "out" else ()) + (m, n), out_dtype),
        scratch_shapes=[] if single else [pltpu.VMEM((tm, tn), F32)],
        compiler_params=_cparams(sem),
    )(*ins)


MM_VMEM_BUDGET = 40 * 1024 * 1024
MXU_WIDTH = 256


def _mm_tiles(m, n, k, ns, sa, sb, so, has_res):
    def divs(x, mult, cap):
        out = [d for d in range(mult, min(x, cap) + 1, mult) if x % d == 0]
        return out or [x]

    best = None
    for tk in divs(k, 128, 8192):
        nk = (k // tk) * ns
        for tm in divs(m, 8, 2048):
            for tn in divs(n, 128, 2048):
                vmem = 2 * (tm * tk * sa + tk * tn * sb + tm * tn * so) + (2 * tm * tn * 4 if has_res else 0)
                vmem += tm * tn * 4 if nk > 1 else 0
                if vmem > MM_VMEM_BUDGET:
                    continue
                steps = (m // tm) * (n // tn) * nk
                traffic = (m // tm) * k * n * sb * ns + (n // tn if nk > 1 else 1) * m * k * sa * ns
                cost = steps * 0.4e-6 + traffic / 2.5e12 + (nk * m * n * 8 / 6e12 if nk > 1 else 0)
                cost += 2.0 * m * n * k * ns / 7e14 * (-(-tn // MXU_WIDTH) * MXU_WIDTH / tn)
                if best is None or cost < best[0]:
                    best = (cost, tm, tn, tk)
    return best[1:]


def _norm_fwd(x, xoff, gain, ncol, w, out_dtype, *, z=None, zoff=0, into=None, into_off=0, name):
    t = x.shape[0]
    tr = _tile(t, max(256, (1 << 18) // w), 8)

    def body(*refs):
        x_ref, g_ref, o_ref = refs[0], refs[1], refs[-1]
        y = _rms_fn(x_ref[...], g_ref[...]) if z is None else _rms_fn(x_ref[...], g_ref[...], refs[2][...])
        o_ref[...] = y.astype(out_dtype)

    ins = [x, gain]
    specs = [pl.BlockSpec((tr, w), lambda j, r: (r, xoff + j)), pl.BlockSpec((1, w), lambda j, r: (0, 0))]
    if z is not None:
        ins.append(z)
        specs.append(pl.BlockSpec((tr, w), lambda j, r: (r, zoff + j)))
    aliases = {}
    if into is not None:
        aliases = {len(ins): 0}
        ins.append(into)
        specs.append(pl.BlockSpec(memory_space=pl.ANY))
    return pl.pallas_call(
        body, name=name, grid=(ncol, t // tr), in_specs=specs,
        out_specs=pl.BlockSpec((tr, w), lambda j, r: (r, into_off + j)),
        out_shape=jax.ShapeDtypeStruct((t, ncol * w) if into is None else into.shape, out_dtype),
        input_output_aliases=aliases, compiler_params=_cparams(("parallel", "parallel")),
    )(*ins)


def _norm_bwd(x, xoff, gain, dy, dyoff, ncol, w, *, z=None, zoff=0, res=None, name):
    t = x.shape[0]
    tr = _tile(t, max(256, (1 << 18) // w), 8)

    def body(*refs):
        it = iter(refs)
        x_ref, g_ref = next(it), next(it)
        z_ref = next(it) if z is not None else None
        dy_ref = next(it)
        r_ref = next(it) if res is not None else None
        dx_ref = next(it)
        dz_ref = next(it) if z is not None else None
        dg_ref = next(it)

        @pl.when((pl.program_id(0) == 0) & (pl.program_id(1) == 0))
        def _():
            dg_ref[...] = jnp.zeros_like(dg_ref)

        args = (x_ref[...], g_ref[...]) + ((z_ref[...],) if z is not None else ())
        _, vjp = jax.vjp(_rms_fn, *args)
        grads = vjp(dy_ref[...].astype(F32))
        dx = grads[0]
        if res is not None:
            dx = dx + r_ref[...]
        dx_ref[...] = dx
        if z is not None:
            dz_ref[...] = grads[2]
        dg_ref[...] += grads[1]

    ins = [x, gain]
    specs = [pl.BlockSpec((tr, w), lambda j, r: (r, xoff + j)), pl.BlockSpec((1, w), lambda j, r: (0, 0))]
    if z is not None:
        ins.append(z)
        specs.append(pl.BlockSpec((tr, w), lambda j, r: (r, zoff + j)))
    ins.append(dy)
    specs.append(pl.BlockSpec((tr, w), lambda j, r: (r, dyoff + j)))
    blk = pl.BlockSpec((tr, w), lambda j, r: (r, j))
    if res is not None:
        ins.append(res)
        specs.append(blk)
    full = jax.ShapeDtypeStruct((t, ncol * w), F32)
    out_shape, out_specs = [full], [blk]
    if z is not None:
        out_shape.append(full)
        out_specs.append(blk)
    out_shape.append(jax.ShapeDtypeStruct((1, w), F32))
    out_specs.append(pl.BlockSpec((1, w), lambda j, r: (0, 0)))
    return pl.pallas_call(
        body, name=name, grid=(ncol, t // tr), in_specs=specs, out_specs=out_specs, out_shape=out_shape,
        compiler_params=_cparams(("arbitrary", "arbitrary")),
    )(*ins)


def _small_fn(x, pa, pb, nf, ng):
    lane = lax.broadcasted_iota(jnp.int32, x.shape, 1)
    zz = x + pb
    logf = -_softplus(-zz)
    g = -jnp.exp(pa) * _softplus(zz)
    beta = _sigmoid(x)
    return jnp.where(lane < nf, logf, jnp.where(lane < nf + ng, g, beta))


def _tri(n, upper):
    r = lax.broadcasted_iota(jnp.int32, (n, n), 0)
    c = lax.broadcasted_iota(jnp.int32, (n, n), 1)
    return jnp.where((c >= r) if upper else (c <= r), 1.0, 0.0).astype(F32)


def _small_fwd(p, off, pa, pb, nf, ng):
    t = p.shape[0]
    blk = HEAD_DIM
    nb = t // blk

    def body(x_ref, pa_ref, pb_ref, v_ref, c_ref):
        v_ref[...] = _small_fn(x_ref[...], pa_ref[...], pb_ref[...], nf, ng)
        tri = _tri(blk, False)

        carry = jnp.zeros((1, HEAD_DIM), F32)
        for i in range(nb):
            rows = slice(i * blk, (i + 1) * blk)
            c = _nn_hi(tri, v_ref[rows, :]) + carry
            c_ref[rows, :] = c
            carry = c[blk - 1:blk, :]

    row = pl.BlockSpec((1, HEAD_DIM), lambda i: (0, 0))
    out = pl.BlockSpec((t, HEAD_DIM), lambda i: (0, 0))
    return pl.pallas_call(
        body, name="small_fwd", grid=(1,),
        in_specs=[pl.BlockSpec((t, HEAD_DIM), lambda i: (0, off)), row, row], out_specs=[out, out],
        out_shape=[jax.ShapeDtypeStruct((t, HEAD_DIM), F32)] * 2,
        compiler_params=_cparams(("arbitrary",)),
    )(p, pa, pb)


def _small_bwd(p, off, pa, pb, dvals, dcsum, nf, ng):
    t = p.shape[0]
    blk = HEAD_DIM
    nb = t // blk

    def body(x_ref, pa_ref, pb_ref, dv_ref, dc_ref, dx_ref, dpa_ref, dpb_ref, tot_ref):
        tri = _tri(blk, True)

        carry = jnp.zeros((1, HEAD_DIM), F32)
        for i in reversed(range(nb)):
            rows = slice(i * blk, (i + 1) * blk)
            c = _nn_hi(tri, dc_ref[rows, :]) + carry
            tot_ref[rows, :] = c + dv_ref[rows, :]
            carry = c[0:1, :]
        f = functools.partial(_small_fn, nf=nf, ng=ng)
        _, vjp = jax.vjp(f, x_ref[...], pa_ref[...], pb_ref[...])
        dx, dpa, dpb = vjp(tot_ref[...])
        dx_ref[...] = dx
        dpa_ref[...] = dpa
        dpb_ref[...] = dpb

    row = pl.BlockSpec((1, HEAD_DIM), lambda i: (0, 0))
    full = pl.BlockSpec((t, HEAD_DIM), lambda i: (0, 0))
    return pl.pallas_call(
        body, name="small_bwd", grid=(1,),
        in_specs=[pl.BlockSpec((t, HEAD_DIM), lambda i: (0, off)), row, row, full, full],
        out_specs=[full, row, row],
        out_shape=[jax.ShapeDtypeStruct((t, HEAD_DIM), F32), jax.ShapeDtypeStruct((1, HEAD_DIM), F32),
                   jax.ShapeDtypeStruct((1, HEAD_DIM), F32)],
        scratch_shapes=[pltpu.VMEM((t, HEAD_DIM), F32)],
        compiler_params=_cparams(("arbitrary",)),
    )(p, pa, pb, dvals, dcsum)


def _fox_heads(nf, most):
    return next(h for h in range(most, 0, -1) if nf % h == 0)


def _fox_fwd(q, k, v, cc, cr, nf, tq, tk, d_mix):
    t = q.shape[0]
    scale = HEAD_DIM ** -0.5
    assert tq == tk

    vt = jnp.transpose(v.reshape(t // tk, tk, nf, HEAD_DIM), (2, 0, 3, 1))

    hp = _fox_heads(nf, 3)
    lanes = lambda h: slice(h * HEAD_DIM, (h + 1) * HEAD_DIM)

    def body(q_ref, k_ref, vt_ref, cc_ref, cr_ref, o_ref, lse_ref, mix_ref):
        i = pl.program_id(1)
        qs = [q_ref[:, lanes(h)] for h in range(hp)]
        cqs = [cr_ref[h, i] for h in range(hp)]
        ones = jnp.ones((8, tk), BF16)
        diff = lax.broadcasted_iota(jnp.int32, (tk, tq), 0) - lax.broadcasted_iota(jnp.int32, (tk, tq), 1)

        def scores(h, j):
            ks = pl.ds(pl.multiple_of(j * tk, tk), tk)
            return lax.dot_general(k_ref[ks, lanes(h)], qs[h], (((1,), (1,)), ((), ())),
                                   preferred_element_type=F32)

        def tile(h, j, m, l, acc, s, masked):
            ks = pl.ds(pl.multiple_of(j * tk, tk), tk)
            s = s * scale + cqs[h] - cc_ref[0, ks, h:h + 1]
            if masked:
                s = jnp.where(diff <= 0, s, NEG)
            m_new = jnp.maximum(m, jnp.max(s, axis=0, keepdims=True))
            pr = jnp.exp(s - m_new).astype(BF16)
            alpha = jnp.exp(m - m_new)
            l = alpha * l + jnp.dot(ones, pr, preferred_element_type=F32)[:1]
            acc = alpha * acc + jnp.dot(vt_ref[h, j], pr, preferred_element_type=F32)
            return m_new, l, acc

        def step(j, carry):
            nxt = [scores(h, j + 1) for h in range(hp)]
            return tuple(tile(h, j, *carry[h], False) + (nxt[h],) for h in range(hp))

        init = tuple((jnp.full((1, tq), NEG, F32), jnp.zeros((1, tq), F32), jnp.zeros((HEAD_DIM, tq), F32),
                      scores(h, 0)) for h in range(hp))
        carry = lax.fori_loop(0, i, step, init)
        for h in range(hp):
            m, l, acc = tile(h, i, *carry[h], True)
            o = jnp.transpose(acc / l)
            o_ref[:, lanes(h)] = o
            mix_ref[:, lanes(h)] = o.astype(BF16)
            lse_ref[h, 0] = m + jnp.log(l)

    w = hp * HEAD_DIM
    qblk = pl.BlockSpec((tq, w), lambda h, i: (i, h))
    return pl.pallas_call(
        body, name="fox_fwd", grid=(nf // hp, t // tq),
        in_specs=[qblk, pl.BlockSpec((t, w), lambda h, i: (0, h)),
                  pl.BlockSpec((hp, t // tk, HEAD_DIM, tk), lambda h, i: (h, 0, 0, 0)),
                  pl.BlockSpec((1, t, HEAD_DIM), lambda h, i: (h, 0, 0)),
                  pl.BlockSpec((hp, t // tk, 1, tk), lambda h, i: (h, 0, 0, 0))],
        out_specs=[qblk, pl.BlockSpec((hp, 1, 1, tq), lambda h, i: (h, i, 0, 0)), qblk],
        out_shape=[jax.ShapeDtypeStruct((t, nf * HEAD_DIM), F32), jax.ShapeDtypeStruct((nf, t // tq, 1, tq), F32),
                   jax.ShapeDtypeStruct((t, d_mix), BF16)],
        compiler_params=_cparams(("parallel", "parallel")),
    )(q, k, vt, cc, cr)


def _fox_bwd(q, k, v, cc, cr, o, lse, dmix, nf, tq, tk):
    t = q.shape[0]
    scale = HEAD_DIM ** -0.5
    assert tq == tk
    hp = _fox_heads(nf, 3)
    lanes = lambda h: slice(h * HEAD_DIM, (h + 1) * HEAD_DIM)
    kt = jnp.transpose(k.reshape(t // tk, tk, nf, HEAD_DIM), (2, 0, 3, 1))

    def body(q_ref, k_ref, kt_ref, v_ref, cc_ref, cr_ref, o_ref, lse_ref, do_ref,
             dq_ref, dk_ref, dv_ref, dcq_ref, dck_ref):
        i = pl.program_id(1)

        @pl.when(i == 0)
        def _():
            dk_ref[...] = jnp.zeros_like(dk_ref)
            dv_ref[...] = jnp.zeros_like(dv_ref)
            dck_ref[...] = jnp.zeros_like(dck_ref)

        diff = lax.broadcasted_iota(jnp.int32, (tk, tq), 0) - lax.broadcasted_iota(jnp.int32, (tk, tq), 1)
        lane = lax.broadcasted_iota(jnp.int32, (tk, HEAD_DIM), 1)
        qs = [q_ref[:, lanes(h)] for h in range(hp)]
        dos = [do_ref[:, lanes(h)] for h in range(hp)]
        do_b = [d.astype(BF16) for d in dos]
        cqs = [cr_ref[h, i] for h in range(hp)]
        lses = [lse_ref[h, 0] for h in range(hp)]
        deltas = [jnp.sum(jnp.transpose(dos[h] * o_ref[:, lanes(h)]), axis=0, keepdims=True) for h in range(hp)]

        def products(h, j):
            ks = pl.ds(pl.multiple_of(j * tk, tk), tk)
            nt = (((1,), (1,)), ((), ()))
            return (lax.dot_general(k_ref[ks, lanes(h)], qs[h], nt, preferred_element_type=F32),
                    lax.dot_general(v_ref[ks, lanes(h)], do_b[h], nt, preferred_element_type=F32))

        def tile(h, j, dqt, dcq, s, dp, masked):
            ks = pl.ds(pl.multiple_of(j * tk, tk), tk)
            pr = jnp.exp(s * scale + cqs[h] - cc_ref[0, ks, h:h + 1] - lses[h])
            if masked:
                pr = jnp.where(diff <= 0, pr, 0.0)
            ds = pr * (dp - deltas[h])
            ds_b = ds.astype(BF16)
            dqt = dqt + jnp.dot(kt_ref[h, j], ds_b, preferred_element_type=F32)
            dk_ref[ks, lanes(h)] += jnp.dot(ds_b, qs[h], preferred_element_type=F32) * scale
            dv_ref[ks, lanes(h)] += jnp.dot(pr.astype(BF16), do_b[h], preferred_element_type=F32)
            dck_ref[0, ks, :] -= jnp.where(lane == h, jnp.sum(ds, axis=1, keepdims=True), 0.0)
            return dqt, dcq + jnp.sum(ds, axis=0, keepdims=True)

        def step(j, carry):
            nxt = [products(h, j + 1) for h in range(hp)]
            return tuple(tile(h, j, *carry[h], False) + nxt[h] for h in range(hp))

        init = tuple((jnp.zeros((HEAD_DIM, tq), F32), jnp.zeros((1, tq), F32)) + products(h, 0) for h in range(hp))
        carry = lax.fori_loop(0, i, step, init)
        for h in range(hp):
            dqt, dcq = tile(h, i, *carry[h], True)
            dq_ref[:, lanes(h)] = jnp.transpose(dqt) * scale
            dcq_ref[h, 0] = dcq

    w = hp * HEAD_DIM
    head_all = pl.BlockSpec((t, w), lambda h, i: (0, h))
    qblk = pl.BlockSpec((tq, w), lambda h, i: (i, h))
    colv = pl.BlockSpec((1, t, HEAD_DIM), lambda h, i: (h, 0, 0))
    rows_all = pl.BlockSpec((hp, t // tk, 1, tk), lambda h, i: (h, 0, 0, 0))
    row_blk = pl.BlockSpec((hp, 1, 1, tq), lambda h, i: (h, i, 0, 0))
    wide = jax.ShapeDtypeStruct((t, nf * HEAD_DIM), F32)
    return pl.pallas_call(
        body, name="fox_bwd", grid=(nf // hp, t // tq),
        in_specs=[qblk, head_all, pl.BlockSpec((hp, t // tk, HEAD_DIM, tk), lambda h, i: (h, 0, 0, 0)), head_all,
                  colv, rows_all, qblk, row_blk, qblk],
        out_specs=[qblk, head_all, head_all, row_blk, colv],
        out_shape=[wide, wide, wide, jax.ShapeDtypeStruct((nf, t // tq, 1, tq), F32),
                   jax.ShapeDtypeStruct((nf // hp, t, HEAD_DIM), F32)],
        compiler_params=_cparams(("parallel", "arbitrary")),
    )(q, k, kt, v, cc, cr, o, lse, dmix)


def _mem_fn(mq, mk, mv, gq, gk):
    qn = _rms_fn(mq, gq)
    kn = _rms_fn(mk, gk)
    s = _nt(qn, kn) * (HEAD_DIM ** -0.5)
    e = jnp.exp(s - lax.stop_gradient(jnp.max(s, axis=1, keepdims=True)))
    pr = e / jnp.sum(e, axis=1, keepdims=True)
    return _nn(pr, mv)


def _mem_specs(t, m, tq, qoff):
    qblk = pl.BlockSpec((tq, HEAD_DIM), lambda h, i: (i, qoff + h))
    kblk = pl.BlockSpec((m, HEAD_DIM), lambda h, i: (0, h))
    vblk = pl.BlockSpec((m, HEAD_DIM), lambda h, i: (0, N_MEM_HEADS + h))
    row = pl.BlockSpec((1, HEAD_DIM), lambda h, i: (0, 0))
    return qblk, kblk, vblk, row


def _mem_fwd(p, qoff, mkv, gq, gk, tq, into, into_off):
    t, m = p.shape[0], mkv.shape[0]
    qblk, kblk, vblk, row = _mem_specs(t, m, tq, qoff)

    def body(q_ref, k_ref, v_ref, gq_ref, gk_ref, _, o_ref):
        o_ref[...] = _mem_fn(q_ref[...], k_ref[...], v_ref[...], gq_ref[...], gk_ref[...]).astype(BF16)

    return pl.pallas_call(
        body, name="mem_fwd", grid=(N_MEM_HEADS, t // tq),
        in_specs=[qblk, kblk, vblk, row, row, pl.BlockSpec(memory_space=pl.ANY)],
        out_specs=pl.BlockSpec((tq, HEAD_DIM), lambda h, i: (i, into_off + h)),
        out_shape=jax.ShapeDtypeStruct(into.shape, BF16), input_output_aliases={5: 0},
        compiler_params=_cparams(("parallel", "parallel")),
    )(p, mkv, mkv, gq, gk, into)


def _mem_bwd(p, qoff, mkv, gq, gk, dmix, dooff, tq):
    t, m = p.shape[0], mkv.shape[0]
    qblk, kblk, vblk, row = _mem_specs(t, m, tq, qoff)

    def body(q_ref, k_ref, v_ref, gq_ref, gk_ref, do_ref, dq_ref, dkv_k_ref, dkv_v_ref, dgq_ref, dgk_ref):
        h, i = pl.program_id(0), pl.program_id(1)

        @pl.when((h == 0) & (i == 0))
        def _():
            dgq_ref[...] = jnp.zeros_like(dgq_ref)
            dgk_ref[...] = jnp.zeros_like(dgk_ref)

        @pl.when(i == 0)
        def _():
            dkv_k_ref[...] = jnp.zeros_like(dkv_k_ref)
            dkv_v_ref[...] = jnp.zeros_like(dkv_v_ref)

        _, vjp = jax.vjp(_mem_fn, q_ref[...], k_ref[...], v_ref[...], gq_ref[...], gk_ref[...])
        dq, dk, dv, dgq, dgk = vjp(do_ref[...])
        dq_ref[...] = dq
        dkv_k_ref[...] += dk
        dkv_v_ref[...] += dv
        dgq_ref[...] += dgq
        dgk_ref[...] += dgk

    oblk = pl.BlockSpec((tq, HEAD_DIM), lambda h, i: (i, h))
    kout = pl.BlockSpec((m, HEAD_DIM), lambda h, i: (0, h))
    half = jax.ShapeDtypeStruct((m, N_MEM_HEADS * HEAD_DIM), F32)
    rshape = jax.ShapeDtypeStruct((1, HEAD_DIM), F32)
    return pl.pallas_call(
        body, name="mem_bwd", grid=(N_MEM_HEADS, t // tq),
        in_specs=[qblk, kblk, vblk, row, row, pl.BlockSpec((tq, HEAD_DIM), lambda h, i: (i, dooff + h))],
        out_specs=[oblk, kout, kout, row, row],
        out_shape=[jax.ShapeDtypeStruct((t, N_MEM_HEADS * HEAD_DIM), F32), half, half, rshape, rshape],
        compiler_params=_cparams(("arbitrary", "arbitrary")),
    )(p, mkv, mkv, gq, gk, dmix)


def _shift_down(x, s):
    if s == 0:
        return x
    r = lax.broadcasted_iota(jnp.int32, x.shape, 0)
    return jnp.where(r >= s, pltpu.roll(x, s, 0), 0.0)


def _shift_up(x, s):
    if s == 0:
        return x
    n = x.shape[0]
    r = lax.broadcasted_iota(jnp.int32, x.shape, 0)
    return jnp.where(r < n - s, pltpu.roll(x, n - s, 0), 0.0)


def _conv_fn(x0, x1, x2, x3, w0, w1, w2, w3, kind):
    y = _silu(x0 * w0 + x1 * w1 + x2 * w2 + x3 * w3)
    if kind == 2:
        return y
    y = y * lax.rsqrt(jnp.sum(y * y, axis=-1, keepdims=True) + NORM_EPS)
    return y * (HEAD_DIM ** -0.5) if kind == 0 else y


def _conv_fwd(p, off, conv_w, ng):
    t = p.shape[0]

    def body(x_ref, w_ref, o_ref):
        kind = pl.program_id(0) // ng
        x = x_ref[...]
        xs = [_shift_down(x, CONV_WIDTH - 1 - j) for j in range(CONV_WIDTH)]
        ws = [w_ref[j:j + 1, :] for j in range(CONV_WIDTH)]
        for kd in range(3):
            @pl.when(kind == kd)
            def _(kd=kd):
                o_ref[...] = _conv_fn(*xs, *ws, kd)

    return pl.pallas_call(
        body, name="gdn_conv_fwd", grid=(3 * ng,),
        in_specs=[pl.BlockSpec((t, HEAD_DIM), lambda c: (0, off + c)),
                  pl.BlockSpec((CONV_WIDTH, HEAD_DIM), lambda c: (0, c))],
        out_specs=pl.BlockSpec((t, HEAD_DIM), lambda c: (0, c)),
        out_shape=jax.ShapeDtypeStruct((t, 3 * ng * HEAD_DIM), F32),
        compiler_params=_cparams(("parallel",)),
    )(p, conv_w)


def _conv_bwd(p, off, conv_w, dys, ng):
    t = p.shape[0]

    def body(x_ref, w_ref, dq_ref, dk_ref, dv_ref, dx_ref, dw_ref):
        kind = pl.program_id(0) // ng
        dy_refs = (dq_ref, dk_ref, dv_ref)
        x = x_ref[...]
        xs = [_shift_down(x, CONV_WIDTH - 1 - j) for j in range(CONV_WIDTH)]
        ws = [w_ref[j:j + 1, :] for j in range(CONV_WIDTH)]
        for kd in range(3):
            @pl.when(kind == kd)
            def _(kd=kd):
                _, vjp = jax.vjp(functools.partial(_conv_fn, kind=kd), *xs, *ws)
                g = vjp(dy_refs[kd][...])
                dx = _shift_up(g[0], CONV_WIDTH - 1)
                for j in range(1, CONV_WIDTH):
                    dx = dx + _shift_up(g[j], CONV_WIDTH - 1 - j)
                dx_ref[...] = dx.astype(BF16)
                for j in range(CONV_WIDTH):
                    dw_ref[j:j + 1, :] = g[CONV_WIDTH + j]

    blk = pl.BlockSpec((t, HEAD_DIM), lambda c: (0, c))
    head = lambda k: pl.BlockSpec((t, HEAD_DIM), lambda c: (0, jnp.where(c // ng == k, c % ng, 0)))
    wblk = pl.BlockSpec((CONV_WIDTH, HEAD_DIM), lambda c: (0, c))
    return pl.pallas_call(
        body, name="gdn_conv_bwd", grid=(3 * ng,),
        in_specs=[pl.BlockSpec((t, HEAD_DIM), lambda c: (0, off + c)), wblk] + [head(k) for k in range(3)],
        out_specs=[blk, wblk],
        out_shape=[jax.ShapeDtypeStruct((t, 3 * ng * HEAD_DIM), BF16),
                   jax.ShapeDtypeStruct((CONV_WIDTH, 3 * ng * HEAD_DIM), F32)],
        compiler_params=_cparams(("parallel",)),
    )(p, conv_w, *dys)


def _lower_inverse(lower):
    c = lower.shape[-1]
    r = lax.broadcasted_iota(jnp.int32, (1, c, c), 1)
    e = lax.broadcasted_iota(jnp.int32, (1, c, c), 2)
    hi = lax.Precision.HIGH
    inv = jnp.where(r == e, 1.0, 0.0) - lower
    pw = lower
    for _ in range(int(math.log2(c)) - 1):
        pw = _dot(pw, pw, ((1,), (0,)), hi)
        inv = inv + _dot(inv, pw, ((1,), (0,)), hi)
    return inv


@jax.custom_vjp
def _solve(lower, inv, vb, kbg):
    hi = lax.Precision.HIGH
    return _dot(inv, vb, ((1,), (0,)), hi), _dot(inv, kbg, ((1,), (0,)), hi)


def _solve_fwd(lower, inv, vb, kbg):
    u, w = _solve(lower, inv, vb, kbg)
    return (u, w), (inv, u, w)


def _solve_bwd(res, cts):
    inv, u, w = res
    dvb, dkbg = _tn(inv, cts[0]), _tn(inv, cts[1])
    return -(_nt(dvb, u) + _nt(dkbg, w)), jnp.zeros_like(inv), dvb, dkbg


_solve.defvjp(_solve_fwd, _solve_bwd)


def _wy_fn(q, k, v, gcol, grow, bcol, inv=None):
    b, c, dk = q.shape
    r = lax.broadcasted_iota(jnp.int32, (1, c, c), 1)
    e = lax.broadcasted_iota(jnp.int32, (1, c, c), 2)
    tril, strict = e <= r, e < r
    gc_col = jnp.sum(jnp.where(tril, grow, 0.0), axis=2, keepdims=True)
    gc_row = jnp.sum(jnp.where(r <= e, gcol, 0.0), axis=1, keepdims=True)
    g_last = jnp.sum(gcol, axis=1, keepdims=True)
    decay = jnp.exp(jnp.where(tril, gc_col - gc_row, NEG))
    kb, vb = k * bcol, v * bcol
    lower = jnp.where(strict, _nt(kb, k) * decay, 0.0)
    if inv is None:
        inv = _lower_inverse(lower)
    u, w = _solve(lower, inv, vb, kb * jnp.exp(gc_col))
    attn = jnp.where(tril, _nt(q, k) * decay, 0.0)
    qg = q * jnp.exp(gc_col)
    kdec = k * jnp.exp(g_last - gc_col)
    egl = jnp.broadcast_to(jnp.exp(g_last), (b, 1, dk))
    return u, w, qg, kdec, attn, egl, inv


def _scan_fn(u, w, qg, kdec, attn, egl, state):
    v_new = u - _nn(w, state)
    o = _nn(qg, state) + _nn(attn, v_new)
    return o, state * egl + _tn(kdec, v_new)


GDN_CHUNKS_PER_STEP = 4
GDN_SCAN_CHUNKS = 4


def _gdn_fwd(qkv, vals, grow, nf, ng):
    t = qkv.shape[0]
    nch = t // CHUNK

    cb = GDN_CHUNKS_PER_STEP
    *wy, inv = _gdn_wy(qkv, vals, grow, nf, ng, cb)

    sc = GDN_SCAN_CHUNKS

    def body(u_ref, w_ref, qg_ref, kd_ref, at_ref, eg_ref, o_ref, st_ref, state):
        @pl.when(pl.program_id(0) == 0)
        def _():
            state[...] = jnp.zeros_like(state)

        for c in range(sc):
            rows = slice(c * CHUNK, (c + 1) * CHUNK)
            heads = lambda ref: jnp.stack([ref[rows, h * HEAD_DIM:(h + 1) * HEAD_DIM] for h in range(ng)])
            st_ref[:, c] = state[...]
            o, new = _scan_fn(heads(u_ref), heads(w_ref), heads(qg_ref), heads(kd_ref), at_ref[:, c], eg_ref[:, c],
                              state[...])
            for h in range(ng):
                o_ref[rows, h * HEAD_DIM:(h + 1) * HEAD_DIM] = o[h]
            state[...] = new

    w = ng * HEAD_DIM
    blk = pl.BlockSpec((sc * CHUNK, w), lambda i: (i, 0))
    o, states = pl.pallas_call(
        body, name="gdn_scan_fwd", grid=(nch // sc,),
        in_specs=[blk, blk, blk, blk, pl.BlockSpec((ng, sc, CHUNK, CHUNK), lambda i: (0, i, 0, 0)),
                  pl.BlockSpec((ng, sc, 1, HEAD_DIM), lambda i: (0, i, 0, 0))],
        out_specs=[blk, pl.BlockSpec((ng, sc, HEAD_DIM, HEAD_DIM), lambda i: (0, i, 0, 0))],
        out_shape=[jax.ShapeDtypeStruct((t, w), F32),
                   jax.ShapeDtypeStruct((ng, nch, HEAD_DIM, HEAD_DIM), F32)],
        scratch_shapes=[pltpu.VMEM((ng, HEAD_DIM, HEAD_DIM), F32)],
        compiler_params=_cparams(("arbitrary",)),
    )(*wy)
    return o, (wy, inv, states)


def _wy_batch(q_ref, k_ref, v_ref, vals_ref, gr_ref, nf, ng, cb):
    idx = [(c, h) for c in range(cb) for h in range(ng)]
    rows = lambda c: slice(c * CHUNK, (c + 1) * CHUNK)
    lanes = lambda h: slice(h * HEAD_DIM, (h + 1) * HEAD_DIM)
    wide = lambda ref: jnp.stack([ref[rows(c), lanes(h)] for c, h in idx])
    col = lambda lane0: jnp.stack([vals_ref[rows(c), lane0 + h:lane0 + h + 1] for c, h in idx])
    return idx, (wide(q_ref), wide(k_ref), wide(v_ref), col(nf), jnp.stack([gr_ref[h, c] for c, h in idx]),
                 col(nf + ng))


def _gdn_wy(qkv, vals, grow, nf, ng, cb):
    t = qkv.shape[0]
    nch = t // CHUNK

    def body(q_ref, k_ref, v_ref, vals_ref, gr_ref, u_ref, w_ref, qg_ref, kd_ref, at_ref, eg_ref, inv_ref):
        idx, args = _wy_batch(q_ref, k_ref, v_ref, vals_ref, gr_ref, nf, ng, cb)
        u, w, qg, kd, at, eg, inv = _wy_fn(*args)
        for b, (c, h) in enumerate(idx):
            rows, lanes = slice(c * CHUNK, (c + 1) * CHUNK), slice(h * HEAD_DIM, (h + 1) * HEAD_DIM)
            u_ref[rows, lanes] = u[b]
            w_ref[rows, lanes] = w[b]
            qg_ref[rows, lanes] = qg[b]
            kd_ref[rows, lanes] = kd[b]
            at_ref[h, c] = at[b]
            eg_ref[h, c] = eg[b]
            inv_ref[h, c] = inv[b]

    wd = ng * HEAD_DIM
    blk = lambda o: pl.BlockSpec((cb * CHUNK, wd), lambda i: (i, o))
    col = pl.BlockSpec((cb * CHUNK, HEAD_DIM), lambda i: (i, 0))
    sq = pl.BlockSpec((ng, cb, CHUNK, CHUNK), lambda i: (0, i, 0, 0))
    wide = jax.ShapeDtypeStruct((t, wd), F32)
    sq_shape = jax.ShapeDtypeStruct((ng, nch, CHUNK, CHUNK), F32)
    return pl.pallas_call(
        body, name="gdn_wy_fwd", grid=(nch // cb,),
        in_specs=[blk(0), blk(1), blk(2), col, pl.BlockSpec((ng, cb, 1, CHUNK), lambda i: (0, i, 0, 0))],
        out_specs=[blk(0), blk(0), blk(0), blk(0), sq, pl.BlockSpec((ng, cb, 1, HEAD_DIM), lambda i: (0, i, 0, 0)),
                   sq],
        out_shape=[wide, wide, wide, wide, sq_shape, jax.ShapeDtypeStruct((ng, nch, 1, HEAD_DIM), F32), sq_shape],
        compiler_params=_cparams(("parallel",)),
    )(qkv, qkv, qkv, vals, grow)


def _gdn_bwd(qkv, vals, grow, saved, do, nf, ng):
    t = qkv.shape[0]
    nch = t // CHUNK
    cb = GDN_CHUNKS_PER_STEP // 2
    wy, inv, states = saved
    wd = ng * HEAD_DIM

    def scan_body(u_ref, w_ref, qg_ref, kd_ref, at_ref, eg_ref, st_ref, do_ref,
                  du_ref, dw_ref, dqg_ref, dkd_ref, dat_ref, deg_ref, dstate):
        @pl.when(pl.program_id(0) == 0)
        def _():
            dstate[...] = jnp.zeros_like(dstate)

        for c in reversed(range(sc)):
            rows = slice(c * CHUNK, (c + 1) * CHUNK)
            heads = lambda ref: jnp.stack([ref[rows, h * HEAD_DIM:(h + 1) * HEAD_DIM] for h in range(ng)])
            _, vjp = jax.vjp(_scan_fn, heads(u_ref), heads(w_ref), heads(qg_ref), heads(kd_ref), at_ref[:, c],
                             eg_ref[:, c], st_ref[:, c])
            du, dw, dqg, dkd, dat, deg, dst = vjp((heads(do_ref), dstate[...]))
            for h in range(ng):
                lanes = slice(h * HEAD_DIM, (h + 1) * HEAD_DIM)
                du_ref[rows, lanes] = du[h]
                dw_ref[rows, lanes] = dw[h]
                dqg_ref[rows, lanes] = dqg[h]
                dkd_ref[rows, lanes] = dkd[h]
            dat_ref[:, c] = dat
            deg_ref[:, c] = deg
            dstate[...] = dst

    sc = GDN_SCAN_CHUNKS
    rev = lambda i: nch // sc - 1 - i
    blk = pl.BlockSpec((sc * CHUNK, wd), lambda i: (rev(i), 0))
    atb = pl.BlockSpec((ng, sc, CHUNK, CHUNK), lambda i: (0, rev(i), 0, 0))
    egb = pl.BlockSpec((ng, sc, 1, HEAD_DIM), lambda i: (0, rev(i), 0, 0))
    wide = jax.ShapeDtypeStruct((t, wd), F32)
    at_shape = jax.ShapeDtypeStruct((ng, nch, CHUNK, CHUNK), F32)
    eg_shape = jax.ShapeDtypeStruct((ng, nch, 1, HEAD_DIM), F32)
    dwy = pl.pallas_call(
        scan_body, name="gdn_scan_bwd", grid=(nch // sc,),
        in_specs=[blk, blk, blk, blk, atb, egb,
                  pl.BlockSpec((ng, sc, HEAD_DIM, HEAD_DIM), lambda i: (0, rev(i), 0, 0)), blk],
        out_specs=[blk, blk, blk, blk, atb, egb],
        out_shape=[wide, wide, wide, wide, at_shape, eg_shape],
        scratch_shapes=[pltpu.VMEM((ng, HEAD_DIM, HEAD_DIM), F32)],
        compiler_params=_cparams(("arbitrary",)),
    )(*wy, states, do)

    def wy_body(q_ref, k_ref, v_ref, vals_ref, gr_ref, du_ref, dw_ref, dqg_ref, dkd_ref, dat_ref, deg_ref,
                inv_ref, dq_ref, dk_ref, dv_ref, dvals_ref, dgr_ref):
        idx, args = _wy_batch(q_ref, k_ref, v_ref, vals_ref, gr_ref, nf, ng, cb)
        lane = lax.broadcasted_iota(jnp.int32, (CHUNK, HEAD_DIM), 1)
        kept = jnp.stack([inv_ref[h, c] for c, h in idx])
        rows = lambda c: slice(c * CHUNK, (c + 1) * CHUNK)
        lanes = lambda h: slice(h * HEAD_DIM, (h + 1) * HEAD_DIM)
        wide_ct = lambda ref: jnp.stack([ref[rows(c), lanes(h)] for c, h in idx])
        cts = (wide_ct(du_ref), wide_ct(dw_ref), wide_ct(dqg_ref), wide_ct(dkd_ref),
               jnp.stack([dat_ref[h, c] for c, h in idx]), jnp.stack([deg_ref[h, c] for c, h in idx]))
        _, vjp = jax.vjp(lambda *a: _wy_fn(*a, inv=kept)[:6], *args)
        dq, dk, dv, dgc, dgr, dbc = vjp(cts)
        for b, (c, h) in enumerate(idx):
            dq_ref[rows(c), lanes(h)] = dq[b]
            dk_ref[rows(c), lanes(h)] = dk[b]
            dv_ref[rows(c), lanes(h)] = dv[b]
            dgr_ref[h, c] = dgr[b]
        for c in range(cb):
            acc = jnp.zeros((CHUNK, HEAD_DIM), F32)
            for h in range(ng):
                acc = jnp.where(lane == nf + h, dgc[c * ng + h], acc)
                acc = jnp.where(lane == nf + ng + h, dbc[c * ng + h], acc)
            dvals_ref[rows(c), :] = acc

    cblk = lambda o: pl.BlockSpec((cb * CHUNK, wd), lambda i: (i, o))
    col = pl.BlockSpec((cb * CHUNK, HEAD_DIM), lambda i: (i, 0))
    rowv = pl.BlockSpec((ng, cb, 1, CHUNK), lambda i: (0, i, 0, 0))
    return pl.pallas_call(
        wy_body, name="gdn_wy_bwd", grid=(nch // cb,),
        in_specs=[cblk(0), cblk(1), cblk(2), col, rowv, cblk(0), cblk(0), cblk(0), cblk(0),
                  pl.BlockSpec((ng, cb, CHUNK, CHUNK), lambda i: (0, i, 0, 0)),
                  pl.BlockSpec((ng, cb, 1, HEAD_DIM), lambda i: (0, i, 0, 0)),
                  pl.BlockSpec((ng, cb, CHUNK, CHUNK), lambda i: (0, i, 0, 0))],
        out_specs=[cblk(0), cblk(0), cblk(0), col, rowv],
        out_shape=[wide, wide, wide, jax.ShapeDtypeStruct((t, HEAD_DIM), F32),
                   jax.ShapeDtypeStruct((ng, nch, 1, CHUNK), F32)],
        compiler_params=_cparams(("parallel",)),
    )(qkv, qkv, qkv, vals, grow, *dwy, inv)


def _swiglu_fn(gate, up):
    return _silu(gate) * up


FFN_TN = 512


def _ffn_up(n2, wgu4):
    _, d, w = wgu4.shape
    t = n2.shape[0]
    tn = _tile(w, FFN_TN)
    nb = w // tn

    def body(a_ref, b_ref, gu_ref, act_ref):
        av = a_ref[...]
        gate = jnp.dot(av, b_ref[0], preferred_element_type=F32)
        up = jnp.dot(av, b_ref[1], preferred_element_type=F32)
        gu_ref[0] = gate.astype(BF16)
        gu_ref[1] = up.astype(BF16)
        act_ref[...] = _swiglu_fn(gate, up).astype(BF16)

    return pl.pallas_call(
        body, name="ffn_up", grid=(2, nb),
        in_specs=[pl.BlockSpec((t, d), lambda j, l: (0, 0)), pl.BlockSpec((2, d, tn), lambda j, l: (j, 0, l))],
        out_specs=[pl.BlockSpec((2, t, tn), lambda j, l: (j, 0, l)),
                   pl.BlockSpec((t, tn), lambda j, l: (0, j * nb + l))],
        out_shape=[jax.ShapeDtypeStruct((4, t, w), BF16), jax.ShapeDtypeStruct((t, 2 * w), BF16)],
        compiler_params=_cparams(("parallel", "parallel")),
    )(n2, wgu4)


def _ffn_dact(dh2, wd, gu, after):
    _, t, w = gu.shape
    d = dh2.shape[1]
    tn = _tile(w, FFN_TN)
    nb = w // tn

    def body(a_ref, b_ref, gu_ref, _, o_ref):
        dact = lax.dot_general(a_ref[...], b_ref[...], (((1,), (1,)), ((), ())), preferred_element_type=F32)
        _, vjp = jax.vjp(_swiglu_fn, gu_ref[0].astype(F32), gu_ref[1].astype(F32))
        dg, du = vjp(dact)
        o_ref[0] = dg.astype(BF16)
        o_ref[1] = du.astype(BF16)

    pair = pl.BlockSpec((2, t, tn), lambda j, l: (j, 0, l))
    return pl.pallas_call(
        body, name="ffn_dact", grid=(2, nb),
        in_specs=[pl.BlockSpec((t, d), lambda j, l: (0, 0)), pl.BlockSpec((tn, d), lambda j, l: (j * nb + l, 0)),
                  pair, pl.BlockSpec(after.shape, lambda j, l: (0, 0))],
        out_specs=pair, out_shape=jax.ShapeDtypeStruct(gu.shape, BF16),
        compiler_params=_cparams(("parallel", "parallel")),
    )(dh2, wd, gu, after)


def _loss_head(h2, target):
    t, d = h2.shape
    tr = _tile(t, 256, 8)

    def body(h_ref, t_ref, l_ref, d_ref, db_ref):
        @pl.when(pl.program_id(0) == 0)
        def _():
            l_ref[...] = jnp.zeros_like(l_ref)

        err = h_ref[...] - t_ref[...]
        d_ref[...] = err * (1.0 / d)
        db_ref[...] = (err * (1.0 / d)).astype(BF16)
        part = 0.5 * jnp.sum(jnp.mean(err * err, axis=-1, keepdims=True), axis=0, keepdims=True)
        lane = lax.broadcasted_iota(jnp.int32, (8, HEAD_DIM), 1)
        row = lax.broadcasted_iota(jnp.int32, (8, HEAD_DIM), 0)
        l_ref[...] += jnp.where((lane == 0) & (row == 0), part, 0.0)

    blk = pl.BlockSpec((tr, d), lambda r: (r, 0))
    return pl.pallas_call(
        body, name="loss_head", grid=(t // tr,), in_specs=[blk, blk],
        out_specs=[pl.BlockSpec((8, HEAD_DIM), lambda r: (0, 0)), blk, blk],
        out_shape=[jax.ShapeDtypeStruct((8, HEAD_DIM), F32), jax.ShapeDtypeStruct((t, d), F32),
                   jax.ShapeDtypeStruct((t, d), BF16)],
        compiler_params=_cparams(("arbitrary",)),
    )(h2, target)


def _adamw(w, g, m, v, *, g_fn=None, name):
    r, c = w.shape
    tr = _tile(r, max(8, (1 << 19) // c // 8 * 8), 8)
    gs = g if isinstance(g, tuple) else (g,)

    def body(w_ref, *refs):
        g_refs, (m_ref, v_ref, go_ref, d_ref, mo_ref, vo_ref) = refs[:len(gs)], refs[len(gs):]
        gr = g_refs[0][...] if g_fn is None else g_fn(*[ref[...] for ref in g_refs])
        mn = ADAM_B1 * m_ref[...] + (1.0 - ADAM_B1) * gr
        vn = ADAM_B2 * v_ref[...] + (1.0 - ADAM_B2) * (gr * gr)
        m_hat = mn / (1.0 - ADAM_B1 ** ADAM_STEP)
        v_hat = vn / (1.0 - ADAM_B2 ** ADAM_STEP)
        go_ref[...] = gr
        d_ref[...] = -ADAM_LR * (m_hat / (jnp.sqrt(v_hat) + ADAM_EPS) + ADAM_WD * w_ref[...])
        mo_ref[...] = mn
        vo_ref[...] = vn

    blk = pl.BlockSpec((tr, c), lambda i: (i, 0))
    gblks = [pl.BlockSpec((tr, gi.shape[1]), lambda i: (i, 0)) for gi in gs]
    return pl.pallas_call(
        body, name=name, grid=(r // tr,), in_specs=[blk] + gblks + [blk, blk], out_specs=[blk] * 4,
        out_shape=[jax.ShapeDtypeStruct((r, c), F32)] * 4,
        compiler_params=_cparams(("parallel",)),
    )(w, *gs, m, v)


class _Layout:
    def __init__(self, d):
        nh = d // HEAD_DIM
        self.nm = N_MEM_HEADS
        self.nf = (nh - self.nm) // 2
        self.ng = nh - self.nm - self.nf
        nf, ng, nm, hd = self.nf, self.ng, self.nm, HEAD_DIM
        self.o_fq, self.o_fk, self.o_fv, self.o_sm = 0, nf, 2 * nf, 3 * nf
        self.o_gq, self.o_gz, self.o_mq = 0, 3 * ng, 4 * ng
        self.cols_a = -(-(3 * nf + 1) // 4) * 4 * hd
        self.cols_b = -(-(4 * ng + nm) // 4) * 4 * hd
        self.cols = self.cols_a + self.cols_b
        sizes = [nf * hd, nf * hd, nf * hd, nf, 3 * ng * hd, ng * hd, ng, ng, nm * hd]
        starts = [sum(sizes[:i]) for i in range(len(sizes))]
        self.ref = list(zip(starts, sizes))
        self.in_cols = sum(sizes)

    def regroup(self, w):
        part = lambda i: w[:, self.ref[i][0]:self.ref[i][0] + self.ref[i][1]]
        a = [part(0), part(1), part(2), part(3), part(6), part(7)]
        b = [part(4), part(5), part(8)]
        pads = [self.cols_a - sum(p.shape[1] for p in a), self.cols_b - sum(p.shape[1] for p in b)]
        fill = [[jnp.zeros((w.shape[0], n), w.dtype)] if n else [] for n in pads]
        return jnp.concatenate(a + fill[0] + b + fill[1], axis=1)

    def ungroup(self, g):
        hd, nf, ng, nm = HEAD_DIM, self.nf, self.ng, self.nm
        sm, b0 = self.o_sm * hd, self.cols_a
        return jnp.concatenate([
            g[:, :3 * nf * hd], g[:, sm:sm + nf], g[:, b0:b0 + 3 * ng * hd],
            g[:, b0 + self.o_gz * hd:b0 + self.o_mq * hd], g[:, sm + nf:sm + nf + ng],
            g[:, sm + nf + ng:sm + nf + 2 * ng], g[:, b0 + self.o_mq * hd:b0 + (self.o_mq + nm) * hd]], axis=1)


def _lane_row(pieces):
    row = jnp.zeros((1, HEAD_DIM), F32)
    for off, a in pieces:
        row = lax.dynamic_update_slice(row, a.astype(F32), (0, off))
    return row


def _local_step(x, mem, target, prefetch, weights, reducer, sp):
    t, d = x.shape
    lay = _Layout(d)
    nf, ng, nm, hd = lay.nf, lay.ng, lay.nm, HEAD_DIM
    nch = t // CHUNK
    tq = _tile(t, 256)
    tk = tq

    u = _norm_fwd(x, 0, sp["norm_mix"], 1, d, BF16, name="norm_mix_fwd")
    prefetch("in_a", u)
    (win_a,) = weights("in_a", u)
    p_a = _mm(u, win_a, name="mm_in_a")
    pa = _lane_row([(nf, sp["gdn_a_log"])])
    pb = _lane_row([(0, sp["fox_f_bias"]), (nf, sp["gdn_dt_bias"])])
    vals, csum = _small_fwd(p_a, lay.o_sm, pa, pb, nf, ng)

    c_t = csum[:, :nf].T
    hp = _fox_heads(nf, 3)
    cr = c_t.reshape(nf, t // tk, 1, tk)
    cc = jnp.stack([jnp.pad(csum[:, g * hp:(g + 1) * hp], ((0, 0), (0, hd - hp))) for g in range(nf // hp)])
    fq = _norm_fwd(p_a, lay.o_fq, sp["fox_q_norm"], nf, hd, BF16, name="fox_qnorm_fwd")
    fk = _norm_fwd(p_a, lay.o_fk, sp["fox_k_norm"], nf, hd, BF16, name="fox_knorm_fwd")
    fv = p_a[:, lay.o_fv * hd:(lay.o_fv + nf) * hd].astype(BF16)
    o_fox, lse, mix = _fox_fwd(fq, fk, fv, cc, cr, nf, tq, tk, d)

    prefetch("in_b", lse)
    (win_b,) = weights("in_b", lse)
    prefetch("mixer", win_b)
    p = _mm(u, win_b, name="mm_in_b")
    wmkv, conv_taps = weights("mixer", p)
    sp = dict(sp, gdn_conv=conv_taps)
    qkv = _conv_fwd(p, lay.o_gq, sp["gdn_conv"], ng)
    grow = vals[:, nf:nf + ng].T.reshape(ng, nch, 1, CHUNK)
    o_g, states = _gdn_fwd(qkv, vals, grow, nf, ng)
    mix = _norm_fwd(o_g, 0, sp["gdn_out_norm"], ng, hd, BF16, z=p, zoff=lay.o_gz, into=mix, into_off=nf,
                    name="gdn_out_fwd")
    prefetch("out", mix)

    mem_n = _norm_fwd(mem, 0, sp["mem_norm"], 1, d, BF16, name="mem_norm_fwd")
    mkv = _mm(mem_n, wmkv, name="mm_memkv")
    tq_mem = _tile(t, 1024)
    mix = _mem_fwd(p, lay.o_mq, mkv, sp["mem_q_norm"], sp["mem_k_norm"], tq_mem, mix, nf + ng)
    prefetch("gate_up", mix)
    (wout,) = weights("out", mix)
    h1 = _mm(mix, wout, res=x, name="mm_out")
    n2 = _norm_fwd(h1, 0, sp["norm_ffn"], 1, d, BF16, name="norm_ffn_fwd")
    (wgu,) = weights("gate_up", n2)
    wgu4 = wgu.reshape(4, d, -1)
    gu, act = _ffn_up(n2, wgu4)
    prefetch("down", act)
    (wd,) = weights("down", act)
    h2 = _mm(act, wd, res=h1, name="mm_down")
    loss_blk, dh2, dh2_b = _loss_head(h2, target)

    g = {}
    token = reducer.pair("w_down", _mm(act, dh2_b, ta=True, out_dtype=BF16, name="mm_dw_down"))
    dgu = _ffn_dact(dh2_b, wd, gu, token)
    dw_gate_up = _mm(n2, dgu, ta=True, stack="out", out_dtype=BF16, name="mm_dw_gate_up").reshape(wgu.shape)
    token = reducer.pair("w_gate_up", dw_gate_up)
    dn2 = _mm(dgu, wgu4, tb=True, stack="sum", after=token, name="mm_dn2")
    token = reducer.ship("ffn", ["w_down", "w_gate_up"], dn2)
    dh1, g["norm_ffn"] = _norm_bwd(h1, 0, sp["norm_ffn"] + token[0, 0], dn2, 0, 1, d, res=dh2,
                                   name="norm_ffn_bwd")
    token = reducer.pair("w_out", _mm(mix, dh1, ta=True, out_dtype=BF16, name="mm_dw_out"))
    dmix = _mm(dh1, wout, tb=True, after=token, name="mm_dmix")

    dmq, dmk, dmv, g["mem_q_norm"], g["mem_k_norm"] = _mem_bwd(
        p, lay.o_mq, mkv, sp["mem_q_norm"], sp["mem_k_norm"], dmix, nf + ng, tq_mem)
    dmkv = jnp.concatenate([dmk, dmv], axis=1)
    token = reducer.pair("w_mem_kv", _mm(mem_n, dmkv, ta=True, out_dtype=BF16, name="mm_dw_memkv"))
    dmem_n = _mm(dmkv, wmkv, tb=True, after=token, name="mm_dmem")
    token = reducer.ship("mix", ["w_out", "w_mem_kv"], dmem_n)
    _, g["mem_norm"] = _norm_bwd(mem, 0, sp["mem_norm"], dmem_n, 0, 1, d, name="mem_norm_bwd")

    do_g, dgz, g["gdn_out_norm"] = _norm_bwd(o_g, 0, sp["gdn_out_norm"] + token[0, 0], dmix, nf, ng, hd, z=p,
                                             zoff=lay.o_gz, name="gdn_out_bwd")
    dq, dk, dv, dvals, dgr = _gdn_bwd(qkv, vals, grow, states, do_g, nf, ng)
    dgqkv, g["gdn_conv"] = _conv_bwd(p, lay.o_gq, sp["gdn_conv"], (dq, dk, dv), ng)

    dfq_n, dfk_n, dfv, dcc, dcr = _fox_bwd(fq, fk, fv, cc, cr, o_fox, lse, dmix, nf, tq, tk)
    dfq, g["fox_q_norm"] = _norm_bwd(p_a, lay.o_fq, sp["fox_q_norm"], dfq_n, 0, nf, hd, name="fox_qnorm_bwd")
    dfk, g["fox_k_norm"] = _norm_bwd(p_a, lay.o_fk, sp["fox_k_norm"], dfk_n, 0, nf, hd, name="fox_knorm_bwd")
    dc = dcc.reshape(nf, t).T + jnp.concatenate([dcr[g, :, :hp] for g in range(nf // hp)], axis=1)

    dvals = dvals + jnp.pad(dgr.reshape(ng, t).T, ((0, 0), (nf, hd - nf - ng)))
    dcsum = jnp.pad(dc, ((0, 0), (0, hd - nf)))
    dsm, dpa, dpb = _small_bwd(p_a, lay.o_sm, pa, pb, dvals, dcsum, nf, ng)
    g["fox_f_bias"] = dpb[:, :nf]
    g["gdn_dt_bias"] = dpb[:, nf:nf + ng]
    g["gdn_a_log"] = dpa[:, nf:nf + ng]

    zeros = lambda n: jnp.zeros((t, n), F32)
    dp_a = jnp.concatenate([dfq, dfk, dfv, dsm, zeros(lay.cols_a - (lay.o_sm + 1) * hd)], axis=1).astype(BF16)
    dp_b = jnp.concatenate([dgqkv, dgz.astype(BF16), dmq.astype(BF16),
                            jnp.zeros((t, lay.cols_b - (lay.o_mq + nm) * hd), BF16)], axis=1)
    token = reducer.pair("w_in_a", _mm(u, dp_a, ta=True, out_dtype=BF16, name="mm_dw_in_a"))
    token = reducer.pair("w_in_b", _mm(u, dp_b, ta=True, out_dtype=BF16, after=token, name="mm_dw_in_b"))
    du = _mm(dp_a, win_a, tb=True, after=token, name="mm_du_a")
    token = reducer.ship("in", ["w_in_a", "w_in_b"], du)
    du = _mm(dp_b, win_b, tb=True, res=du, after=token, name="mm_du_b")
    dx, g["norm_mix"] = _norm_bwd(x, 0, sp["norm_mix"], du, 0, 1, d, res=dh1, name="norm_mix_bwd")
    return loss_blk, dx, g


ANY = pl.BlockSpec(memory_space=pl.ANY)


def _me():
    x, y, c = lax.axis_index("x"), lax.axis_index("y"), lax.axis_index("c")
    chips = [(1 - x, y), (x, 1 - y), (1 - x, 1 - y)]
    return x, y, c, chips


def _slot(axis, k):
    return k if axis == 0 else 2 * (k % 2) + k // 2


def _slab(ref, axis, rows, cols, k, h):
    half = rows // 2
    return ref.at[pl.ds(_slot(axis, k) * rows + h * half, half), :]


def _remote(src, dst, send_sem, recv_sem, dev):
    return pltpu.make_async_remote_copy(src_ref=src, dst_ref=dst, send_sem=send_sem, recv_sem=recv_sem,
                                        device_id=dev, device_id_type=MESH)


HBM = pl.BlockSpec(memory_space=pltpu.HBM)
SEM = pl.BlockSpec(memory_space=pltpu.SEMAPHORE)
SPLIT = pltpu.CompilerParams(has_side_effects=pltpu.SideEffectType.DATAFLOW_SIDE_EFFECTING)
TOKEN = jax.ShapeDtypeStruct((8, HEAD_DIM), F32)


def _in_hbm(v):
    return pltpu.with_memory_space_constraint(v, pltpu.HBM)


def _cast_place(shard, axis, name, col_fn=None, out_cols=None, after=None):
    r, c = shard.shape
    oc = out_cols or c
    tr = _tile(r, 512 if col_fn is None else 64, 16)
    tc = _tile(c, 2048) if col_fn is None else c
    otc = tc if col_fn is None else oc
    nb = r // tr
    chip = 2 * lax.axis_index("x") + lax.axis_index("y")
    slot = jnp.reshape(_slot(axis, chip), (1,)).astype(jnp.int32)

    def body(slot_ref, x_ref, *rest):
        x = x_ref[...]
        rest[-1][...] = (x if col_fn is None else col_fn(x)).astype(BF16)

    extra = [] if after is None else [after]
    return pl.pallas_call(
        body, name=name,
        grid_spec=pltpu.PrefetchScalarGridSpec(
            num_scalar_prefetch=1, grid=(nb, c // tc),
            in_specs=[pl.BlockSpec((tr, tc), lambda i, l, s: (i, l))] + [ANY] * len(extra),
            out_specs=pl.BlockSpec((tr, otc), lambda i, l, s: (s[0] * nb + i, l))),
        out_shape=jax.ShapeDtypeStruct((4 * r, oc), BF16),
        compiler_params=_cparams(("parallel", "parallel")),
    )(slot, shard, *extra)


def _gather_start(bufs, axes, shapes, groups, name):
    n = len(bufs)

    def body(*refs):
        dst = refs[n:2 * n]
        sems = refs[2 * n:2 * n + 2 * len(groups)]
        token = refs[-1]
        x, y, c, chips = _me()
        k = 2 * x + y
        for gi, ws in enumerate(groups):
            for i, w in enumerate(ws):
                r, cl = shapes[w]
                place = _slab(dst[w], axes[w], r, cl, k, c)
                for j, (px, py) in enumerate(chips):
                    _remote(place, place, sems[2 * gi].at[3 * i + j], sems[2 * gi + 1].at[3 * i + j],
                            (px, py, c)).start()
        token[...] = jnp.zeros_like(token)

    sem_shapes = [pltpu.SemaphoreType.DMA((3 * len(ws),)) for ws in groups for _ in range(2)]
    outs = pl.pallas_call(
        body, name=name, in_specs=[HBM] * n,
        out_specs=[HBM] * n + [SEM] * len(sem_shapes) + [pl.BlockSpec(memory_space=pltpu.VMEM)],
        out_shape=[pltpu.HBM(b.shape, b.dtype) for b in bufs] + sem_shapes + [TOKEN],
        input_output_aliases={w: w for w in range(n)}, compiler_params=SPLIT,
    )(*[_in_hbm(b) for b in bufs])
    sems = outs[n:-1]
    return outs[:n], [(sems[2 * g], sems[2 * g + 1]) for g in range(len(groups))], outs[-1]


def _gather_wait(bufs, axes, shapes, sems, after, name):
    n = len(bufs)

    def body(*refs):
        send_sems, recv_sems = refs[n], refs[n + 1]
        dst = refs[n + 3:]
        x, y, c, chips = _me()
        k = 2 * x + y
        for i in range(n):
            r, cl = shapes[i]
            for j, (px, py) in enumerate(chips):
                got = _slab(dst[i], axes[i], r, cl, 2 * px + py, c)
                _remote(got, got, send_sems.at[3 * i + j], recv_sems.at[3 * i + j], (px, py, c)).wait_recv()
        for i in range(n):
            r, cl = shapes[i]
            mine = _slab(dst[i], axes[i], r, cl, k, c)
            for j, (px, py) in enumerate(chips):
                _remote(mine, mine, send_sems.at[3 * i + j], recv_sems.at[3 * i + j], (px, py, c)).wait_send()

    return pl.pallas_call(
        body, name=name, in_specs=[HBM] * n + [SEM, SEM, ANY], out_specs=[HBM] * n,
        out_shape=[pltpu.HBM(b.shape, b.dtype) for b in bufs],
        input_output_aliases={i: i for i in range(n)}, compiler_params=SPLIT,
    )(*bufs, sems[0], sems[1], after)


def _split_start(name, arrays, geometry, count):
    n = len(arrays)

    def body(*refs):
        send, recv, token = refs[2 * n:]
        for i, (src, dst, _, dev) in enumerate(geometry(refs[n:2 * n])):
            _remote(src, dst, send.at[i], recv.at[i], dev).start()
        token[...] = jnp.zeros_like(token)

    sem = pltpu.SemaphoreType.DMA((count,))
    outs = pl.pallas_call(
        body, name=name, in_specs=[HBM] * n,
        out_specs=[HBM] * n + [SEM, SEM, pl.BlockSpec(memory_space=pltpu.VMEM)],
        out_shape=[pltpu.HBM(v.shape, v.dtype) for v in arrays] + [sem, sem, TOKEN],
        input_output_aliases={i: i for i in range(n)}, compiler_params=SPLIT,
    )(*[_in_hbm(v) for v in arrays])
    return list(outs[:n]), (outs[n], outs[n + 1]), outs[-1]


def _split_wait(name, arrays, sems, after, geometry):
    n = len(arrays)

    def body(*refs):
        send, recv = refs[n], refs[n + 1]
        copies = geometry(refs[n + 3:])
        for i, (_, _, land, dev) in enumerate(copies):
            _remote(land, land, send.at[i], recv.at[i], dev).wait_recv()
        for i, (src, _, _, dev) in enumerate(copies):
            _remote(src, src, send.at[i], recv.at[i], dev).wait_send()

    return list(pl.pallas_call(
        body, name=name, in_specs=[HBM] * n + [SEM, SEM, ANY], out_specs=[HBM] * n,
        out_shape=[pltpu.HBM(v.shape, v.dtype) for v in arrays],
        input_output_aliases={i: i for i in range(n)}, compiler_params=SPLIT,
    )(*arrays, sems[0], sems[1], after))


def _forward_geometry(axes, shapes):
    def geometry(bufs):
        x, y, c, chips = _me()
        out = []
        for i, buf in enumerate(bufs):
            r, cl = shapes[i]
            for px, py in chips:
                got = _slab(buf, axes[i], r, cl, 2 * px + py, c)
                out.append((got, got, _slab(buf, axes[i], r, cl, 2 * px + py, 1 - c), (x, y, 1 - c)))
        return out
    return geometry


def _pair_geometry(axes, shapes):
    def geometry(refs):
        n = len(refs) // 2
        x, y, c, _ = _me()
        out = []
        for w in range(n):
            r, cl = shapes[w]
            for j in range(4):
                land = refs[n + w].at[j]
                out.append((_slab(refs[w], axes[w], r, cl, j, 1 - c), land, land, (x, y, 1 - c)))
        return out
    return geometry


def _after_all(name, token, *arrays):
    def body(*refs):
        refs[-1][...] = jnp.zeros_like(refs[-1])

    return pl.pallas_call(
        body, name=name, in_specs=[ANY] * (1 + len(arrays)), out_specs=pl.BlockSpec(memory_space=pltpu.VMEM),
        out_shape=TOKEN,
    )(token, *arrays)


def _swap_geometry(bufs):
    x, y, c, _ = _me()
    return [(b.at[c], b.at[c], b.at[1 - c], (x, y, 1 - c)) for b in bufs]


def _chip_start(parts, tag):
    n = len(parts)

    def body(*refs):
        src, land = refs[2 * n:3 * n], refs[3 * n:4 * n]
        send_sems, recv_sems, token = refs[4 * n:]
        x, y, c, chips = _me()
        k = 2 * x + y
        for w in range(n):
            for j, (px, py) in enumerate(chips):
                _remote(src[w].at[2 * px + py], land[w].at[k], send_sems.at[3 * w + j], recv_sems.at[3 * w + j],
                        (px, py, c)).start()
        token[...] = jnp.zeros_like(token)

    lands = [lax.empty(p.shape, p.dtype) for p in parts]
    sem = pltpu.SemaphoreType.DMA((3 * n,))
    outs = pl.pallas_call(
        body, name="reduce_ici_start_" + tag, in_specs=[HBM] * (2 * n),
        out_specs=[HBM] * (2 * n) + [SEM, SEM, pl.BlockSpec(memory_space=pltpu.VMEM)],
        out_shape=[pltpu.HBM(p.shape, p.dtype) for p in parts + lands] + [sem, sem, TOKEN],
        input_output_aliases={i: i for i in range(2 * n)}, compiler_params=SPLIT,
    )(*[_in_hbm(v) for v in parts + lands])
    return outs[:n], outs[n:2 * n], outs[2 * n], outs[2 * n + 1], outs[-1]


def _chip_wait(parts, lands, send_sems, recv_sems, after, tag):
    n = len(parts)

    def body(*refs):
        send, recv = refs[2 * n], refs[2 * n + 1]
        src, land = refs[2 * n + 3:3 * n + 3], refs[3 * n + 3:]
        x, y, c, chips = _me()
        for w in range(n):
            for j, (px, py) in enumerate(chips):
                got = land[w].at[2 * px + py]
                _remote(got, got, send.at[3 * w + j], recv.at[3 * w + j], (px, py, c)).wait_recv()
        for w in range(n):
            for j, (px, py) in enumerate(chips):
                sent = src[w].at[2 * px + py]
                _remote(sent, sent, send.at[3 * w + j], recv.at[3 * w + j], (px, py, c)).wait_send()

    outs = pl.pallas_call(
        body, name="reduce_ici_wait_" + tag, in_specs=[HBM] * (2 * n) + [SEM, SEM, ANY], out_specs=[HBM] * (2 * n),
        out_shape=[pltpu.HBM(p.shape, p.dtype) for p in parts + lands],
        input_output_aliases={i: i for i in range(2 * n)}, compiler_params=SPLIT,
    )(*parts, *lands, send_sems, recv_sems, after)
    chip = 2 * lax.axis_index("x") + lax.axis_index("y")
    return [lax.dynamic_update_slice(s, lax.dynamic_index_in_dim(p, chip, 0, keepdims=True), (chip, 0, 0))
            for p, s in zip(outs[:n], outs[n:])]


def _half_swap(halves, tag):
    n = len(halves)
    core = lax.axis_index("c")
    bufs = [lax.dynamic_update_slice(lax.empty((2,) + h.shape, h.dtype), h[None], (core, 0, 0)) for h in halves]

    def body(*refs):
        dst = refs[n:2 * n]
        send_sems, recv_sems = refs[2 * n:]
        x, y, c, _ = _me()
        sibling = (x, y, 1 - c)
        cps = []
        for w in range(n):
            cp = _remote(dst[w].at[c], dst[w].at[c], send_sems.at[w], recv_sems.at[w], sibling)
            cp.start()
            cps.append(cp)
        for w in range(n):
            other = dst[w].at[1 - c]
            _remote(other, other, send_sems.at[w], recv_sems.at[w], sibling).wait_recv()
        for cp in cps:
            cp.wait_send()

    outs = pl.pallas_call(
        body, name="reduce_half_swap_" + tag, in_specs=[ANY] * n, out_specs=[ANY] * n,
        out_shape=[jax.ShapeDtypeStruct(b.shape, b.dtype) for b in bufs],
        input_output_aliases={w: w for w in range(n)},
        scratch_shapes=[pltpu.SemaphoreType.DMA((n,)), pltpu.SemaphoreType.DMA((n,))],
    )(*bufs)
    return [o.reshape(2 * o.shape[1], o.shape[2]) for o in outs]


def _add_parts(full, axis, rows, sib, name):
    _, r, c = sib.shape
    tr, tc = _tile(r, 1024, 16), _tile(c, 2048)
    nb = r // tr
    core = jnp.reshape(lax.axis_index("c"), (1,)).astype(jnp.int32)

    def body(c_ref, a_ref, b_ref, o_ref):
        o_ref[0] = (a_ref[...].astype(F32) + b_ref[0].astype(F32)).astype(BF16)

    blk = pl.BlockSpec((1, tr, tc), lambda j, i, l, cr: (j, i, l))
    return pl.pallas_call(
        body, name=name,
        grid_spec=pltpu.PrefetchScalarGridSpec(
            num_scalar_prefetch=1, grid=(4, nb, c // tc),
            in_specs=[pl.BlockSpec((tr, tc), lambda j, i, l, cr: ((_slot(axis, j) * 2 + cr[0]) * nb + i, l)), blk],
            out_specs=blk),
        out_shape=jax.ShapeDtypeStruct(sib.shape, BF16),
        compiler_params=_cparams(("parallel", "parallel", "parallel")),
    )(core, full, sib)


def _sum_slots(a, name):
    _, r, c = a.shape
    tr, tc = _tile(r, 512, 8), _tile(c, 2048)

    def body(a_ref, o_ref):
        v = a_ref[...].astype(F32)
        o_ref[...] = ((v[0] + v[1]) + v[2]) + v[3]

    return pl.pallas_call(
        body, name=name, grid=(r // tr, c // tc),
        in_specs=[pl.BlockSpec((4, tr, tc), lambda i, l: (0, i, l))],
        out_specs=pl.BlockSpec((tr, tc), lambda i, l: (i, l)),
        out_shape=jax.ShapeDtypeStruct((r, c), F32),
        compiler_params=_cparams(("parallel", "parallel")),
    )(a)


class _Reducer:
    def __init__(self, spec):
        self.spec = spec
        self.paired = {}
        self.pending = []

    def pair(self, name, full):
        ax, shp = self.spec[name]
        land = lax.empty((4, shp[0] // 2, shp[1]), full.dtype)
        arrays, sems, token = _split_start("reduce_pair_start_" + name, [full, land], _pair_geometry([ax], [shp]), 4)
        self.paired[name] = (arrays, sems)
        return token

    def ship(self, tag, names, after):
        parts = []
        for n in names:
            ax, shp = self.spec[n]
            arrays, sems = self.paired.pop(n)
            full, sib = _split_wait("reduce_pair_wait_" + n, arrays, sems, after, _pair_geometry([ax], [shp]))
            parts.append(_add_parts(full, ax, shp[0], sib, name=f"reduce_add_{n}"))
        parts, lands, send, recv, token = _chip_start(parts, tag)
        self.pending.append((tag, names, parts, lands, send, recv))
        return token

    def finish(self, after, tags):
        out = {}
        for tag, names, parts, lands, send, recv in [p for p in self.pending if p[0] in tags]:
            slots = _chip_wait(parts, lands, send, recv, after, tag)
            halves = [_sum_slots(s, name=f"reduce_sum_{n}") for n, s in zip(names, slots)]
            out.update(zip(names, _half_swap(halves, tag)))
        return out

    def finish_start(self, after, tag):
        (_, names, parts, lands, send, recv), = [p for p in self.pending if p[0] == tag]
        slots = _chip_wait(parts, lands, send, recv, after, tag)
        halves = [_sum_slots(s, name=f"reduce_sum_{n}") for n, s in zip(names, slots)]
        core = lax.axis_index("c")
        bufs = [lax.dynamic_update_slice(lax.empty((2,) + h.shape, h.dtype), h[None], (core, 0, 0)) for h in halves]
        bufs, sems, _ = _split_start("reduce_half_swap_start_" + tag, bufs, _swap_geometry, len(bufs))
        return tag, names, bufs, sems

    def swap_wait(self, started, after):
        tag, names, bufs, sems = started
        outs = _split_wait("reduce_half_swap_wait_" + tag, bufs, sems, after, _swap_geometry)
        return dict(zip(names, [o.reshape(2 * o.shape[1], o.shape[2]) for o in outs]))


def _allreduce_small(pack, after):
    rows = pack.shape[0]

    def body(p_ref, _, o_ref, slots, send_sems, recv_sems):
        x, y, c, _ = _me()
        me = 4 * x + 2 * y + c
        slots[me] = p_ref[...]
        cps = []
        for r in range(1, 8):
            peer = (x ^ (r >> 2), y ^ ((r >> 1) & 1), c ^ (r & 1))
            cp = _remote(p_ref, slots.at[me], send_sems.at[r - 1], recv_sems.at[r - 1], peer)
            cp.start()
            cps.append(cp)
        for r in range(1, 8):
            frm = me ^ r
            _remote(slots.at[frm], slots.at[frm], send_sems.at[r - 1], recv_sems.at[r - 1], (x, y, c)).wait_recv()
        for cp in cps:
            cp.wait_send()
        acc = slots[0]
        for s in range(1, 8):
            acc = acc + slots[s]
        o_ref[...] = acc

    vm = pl.BlockSpec(memory_space=pltpu.VMEM)
    return pl.pallas_call(
        body, name="allreduce_small", in_specs=[vm, ANY], out_specs=vm,
        out_shape=jax.ShapeDtypeStruct(pack.shape, F32),
        scratch_shapes=[pltpu.VMEM((8, rows, HEAD_DIM), F32), pltpu.SemaphoreType.DMA((7,)),
                        pltpu.SemaphoreType.DMA((7,))],
    )(pack, after)


_ROWS = ["norm_mix", "norm_ffn", "mem_norm", "fox_q_norm", "fox_k_norm", "gdn_out_norm", "mem_q_norm",
         "mem_k_norm", "fox_f_bias", "gdn_a_log", "gdn_dt_bias"]


def _pack_rows(vals):
    out = []
    for name in _ROWS:
        v = vals[name].reshape(-1)
        n = -(-v.shape[0] // HEAD_DIM) * HEAD_DIM
        out.append(jnp.pad(v, (0, n - v.shape[0])).reshape(-1, HEAD_DIM))
    return jnp.concatenate(out, axis=0)


def _unpack_rows(pack, like):
    out, r = {}, 0
    for name in _ROWS:
        n = like[name].shape[-1]
        nr = -(-n // HEAD_DIM)
        out[name] = pack[r:r + nr].reshape(1, -1)[:, :n]
        r += nr
    return out, r


def kernel(x, mem, norm_mix, w_in, fox_f_bias, fox_q_norm, fox_k_norm, gdn_conv, gdn_a_log, gdn_dt_bias, gdn_out_norm, mem_norm, w_mem_kv, mem_q_norm, mem_k_norm, w_out, norm_ffn, w_gate_up, w_down, loss_target, m_norm_mix, m_w_in, m_fox_f_bias, m_fox_q_norm, m_fox_k_norm, m_gdn_conv, m_gdn_a_log, m_gdn_dt_bias, m_gdn_out_norm, m_mem_norm, m_w_mem_kv, m_mem_q_norm, m_mem_k_norm, m_w_out, m_norm_ffn, m_w_gate_up, m_w_down, v_norm_mix, v_w_in, v_fox_f_bias, v_fox_q_norm, v_fox_k_norm, v_gdn_conv, v_gdn_a_log, v_gdn_dt_bias, v_gdn_out_norm, v_mem_norm, v_w_mem_kv, v_mem_q_norm, v_mem_k_norm, v_w_out, v_norm_ffn, v_w_gate_up, v_w_down):
    a = dict(locals())
    d = x.shape[-1]
    lay = _Layout(d)
    chip = 2 * lax.axis_index("x") + lax.axis_index("y")
    small = {n: a[n] for n in _ROWS}
    big = ["w_in", "w_mem_kv", "w_out", "w_gate_up", "w_down"]
    axes = [0, 0, 0, 1, 0]

    conv_cols = gdn_conv.shape[-1]
    conv_n = CONV_WIDTH * conv_cols
    conv_rows = -(-conv_n // HEAD_DIM)
    conv_blk = jnp.pad(gdn_conv.reshape(-1), (0, 32 * HEAD_DIM - conv_n)).reshape(32, HEAD_DIM)
    axis_of = dict(zip(big, axes), conv=0, w_in_a=0, w_in_b=0)
    shape_of = {n: a[n].shape[1:] for n in big[1:]}
    shape_of.update(w_in_a=(w_in.shape[1], lay.cols_a), w_in_b=(w_in.shape[1], lay.cols_b), conv=conv_blk.shape)
    placed = {"w_in_a": _cast_place(w_in[0], 0, "cast_w_in_a", lambda v: lay.regroup(v)[:, :lay.cols_a], lay.cols_a),
              "conv": lax.dynamic_update_slice(lax.empty((4 * 32, HEAD_DIM), F32), conv_blk, (chip * 32, 0))}
    grouped = {"in_a": ["w_in_a"], "in_b": ["w_in_b"], "mixer": ["w_mem_kv", "conv"], "out": ["w_out"],
               "gate_up": ["w_gate_up"], "down": ["w_down"]}
    inflight = {}

    def start(tags, name):
        names = [n for t in tags for n in grouped[t]]
        bufs, sems, token = _gather_start([placed[n] for n in names], [axis_of[n] for n in names],
                                          [shape_of[n] for n in names],
                                          [[names.index(n) for n in grouped[t]] for t in tags], name)
        for t, pair in zip(tags, sems):
            inflight[t] = ([bufs[names.index(n)] for n in grouped[t]], pair)
        return token

    first = start(["in_a"], "gather_ici_start_in")
    placed["w_in_b"] = _cast_place(w_in[0], 0, "cast_w_in_b", lambda v: lay.regroup(v)[:, lay.cols_a:], lay.cols_b,
                                   after=first)
    placed.update({n: _cast_place(a[n][0], axis_of[n], "cast_" + n, after=first) for n in big[1:]})
    all_started = start(["in_b", "mixer", "out", "gate_up", "down"], "gather_ici_start_rest")
    all_started = _after_all("moments_ready", all_started, m_w_in[0], v_w_in[0])

    forwarding = {}

    def prefetch(tag, after):
        bufs, sem_pair = inflight.pop(tag)
        ax, shp = [axis_of[n] for n in grouped[tag]], [shape_of[n] for n in grouped[tag]]
        got = _gather_wait(bufs, ax, shp, sem_pair, all_started if tag == "in_a" else after,
                           "gather_ici_wait_" + tag)
        geometry = _forward_geometry(ax, shp)
        got, sems, _ = _split_start("gather_forward_start_" + tag, got, geometry, 3 * len(got))
        forwarding[tag] = (got, sems, geometry)

    def weights(tag, after):
        got, sems, geometry = forwarding.pop(tag)
        got = _split_wait("gather_forward_wait_" + tag, got, sems, after, geometry)
        if tag != "mixer":
            return got
        taps = got[1].reshape(4, 32 * HEAD_DIM)[:, :conv_n].reshape(4, CONV_WIDTH, conv_cols)
        return got[0], jnp.transpose(taps, (1, 0, 2)).reshape(CONV_WIDTH, 4 * conv_cols)

    sp = dict(small)
    reducer = _Reducer({n: (axis_of[n], shape_of[n]) for n in big[1:] + ["w_in_a", "w_in_b"]})
    loss_blk, dx, g = _local_step(x[0], mem[0], loss_target[0], prefetch, weights, reducer, sp)

    gsmall = {n: g[n] for n in _ROWS}
    pack = jnp.concatenate([_pack_rows(gsmall), g["gdn_conv"].reshape(-1, HEAD_DIM), loss_blk], axis=0)
    pack = jnp.pad(pack, ((0, -pack.shape[0] % 8), (0, 0)))
    out = {"grad_x": dx[None]}

    def adamw_shards(reduced):
        if "w_in_a" in reduced:
            reduced = {"w_in": (reduced["w_in_a"], reduced["w_in_b"])}
        for n, gsh in reduced.items():
            join = (lambda ga, gb: lay.ungroup(jnp.concatenate([ga, gb], axis=1))) if n == "w_in" else None
            res = _adamw(a[n][0], gsh, a["m_" + n][0], a["v_" + n][0], g_fn=join, name="adamw_" + n)
            for pre, r in zip(["grad_", "delta_", "new_m_", "new_v_"], res):
                out[pre + n] = r[None]
        return res[0]

    mix_swap = reducer.finish_start(dx, "mix")
    ffn_swap = reducer.finish_start(mix_swap[2][0], "ffn")
    done = adamw_shards(reducer.swap_wait(mix_swap, ffn_swap[2][0]))
    done = adamw_shards(reducer.swap_wait(ffn_swap, done))
    tot = _allreduce_small(pack, done)
    gs, r0 = _unpack_rows(tot, small)
    conv_g = tot[r0:r0 + CONV_WIDTH * 4 * conv_cols // HEAD_DIM].reshape(CONV_WIDTH, 4 * conv_cols)
    gs_conv = lax.dynamic_slice_in_dim(conv_g, chip * conv_cols, conv_cols, axis=1)
    out["loss"] = tot[r0 + CONV_WIDTH * 4 * conv_cols // HEAD_DIM, 0]
    adamw_shards(reducer.finish(tot, ("in",)))
    conv_pad = lambda v: jnp.pad(v.reshape(-1), (0, conv_rows * HEAD_DIM - conv_n)).reshape(conv_rows, HEAD_DIM)
    packs = []
    for src, cv in [(small, gdn_conv), (gs, gs_conv), ({n: a["m_" + n] for n in _ROWS}, m_gdn_conv),
                    ({n: a["v_" + n] for n in _ROWS}, v_gdn_conv)]:
        packs.append(jnp.concatenate([_pack_rows(src), conv_pad(cv)], axis=0))
    res = _adamw(*packs, name="adamw_small")
    for pre, r in zip(["grad_", "delta_", "new_m_", "new_v_"], res):
        vals, r1 = _unpack_rows(r, small)
        for n in _ROWS:
            out[pre + n] = vals[n]
        out[pre + "gdn_conv"] = r[r1:r1 + conv_rows].reshape(-1)[:conv_n].reshape(gdn_conv.shape)
    names = ["norm_mix", "w_in", "fox_f_bias", "fox_q_norm", "fox_k_norm", "gdn_conv", "gdn_a_log", "gdn_dt_bias",
             "gdn_out_norm", "mem_norm", "w_mem_kv", "mem_q_norm", "mem_k_norm", "w_out", "norm_ffn", "w_gate_up",
             "w_down"]
    return (out["loss"], out["grad_x"], *[out[p + n] for p in ["grad_", "delta_", "new_m_", "new_v_"] for n in names])
```

```python
import functools
import math

import jax
import jax.numpy as jnp
from jax import lax
from jax.experimental import pallas as pl
from jax.experimental.pallas import tpu as pltpu

F32, BF16 = jnp.float32, jnp.bfloat16
HEAD_DIM = 128
CHUNK = 64
N_MEM_HEADS = 4
CONV_WIDTH = 4
NORM_EPS = 1e-6
ADAM_LR, ADAM_B1, ADAM_B2, ADAM_EPS, ADAM_WD, ADAM_STEP = 0.001, 0.9, 0.999, 1e-08, 0.01, 10
VMEM_LIMIT = 48 * 1024 * 1024
NEG = -1e30
MESH = pl.DeviceIdType.MESH


def _cparams(sem=None, **kw):
    if sem is not None:
        kw["dimension_semantics"] = sem
    return pltpu.CompilerParams(vmem_limit_bytes=VMEM_LIMIT, **kw)


def _tile(n, target, mult=128):
    best = None
    d = mult
    while d <= min(n, target):
        if n % d == 0:
            best = d
        d += mult
    return best if best is not None else n


def _dot(a, b, dims, hi):
    if a.ndim == 3:
        dn = (((dims[0][0] + 1,), (dims[1][0] + 1,)), ((0,), (0,)))
    else:
        dn = (dims, ((), ()))
    if hi is not None:
        return lax.dot_general(a, b, dn, precision=hi, preferred_element_type=F32)
    return lax.dot_general(a.astype(BF16), b.astype(BF16), dn, preferred_element_type=F32)


def _make_dots(hi, cotangent=None):
    @jax.custom_vjp
    def nn(a, b):
        return _dot(a, b, ((1,), (0,)), hi)

    @jax.custom_vjp
    def nt(a, b):
        return _dot(a, b, ((1,), (1,)), hi)

    @jax.custom_vjp
    def tn(a, b):
        return _dot(a, b, ((0,), (0,)), hi)

    bnn, bnt, btn = cotangent or (nn, nt, tn)
    nn.defvjp(lambda a, b: (nn(a, b), (a, b)), lambda r, g: (bnt(g, r[1]), btn(r[0], g)))
    nt.defvjp(lambda a, b: (nt(a, b), (a, b)), lambda r, g: (bnn(g, r[1]), btn(g, r[0])))
    tn.defvjp(lambda a, b: (tn(a, b), (a, b)), lambda r, g: (bnt(r[1], g), bnn(r[0], g)))
    return nn, nt, tn


_nn, _nt, _tn = _make_dots(None)
_nn_hi, _nt_hi, _tn_hi = _make_dots(lax.Precision.HIGHEST)


def _sigmoid(x):
    return jax.nn.sigmoid(x)


@jax.custom_vjp
def _softplus(x):
    return jnp.maximum(x, 0.0) + jnp.log(1.0 + jnp.exp(-jnp.abs(x)))


_softplus.defvjp(lambda x: (_softplus(x), x), lambda x, g: (g * _sigmoid(x),))


def _silu(x):
    return x * _sigmoid(x)


def _rms_fn(x, gain, z=None):
    y = x * lax.rsqrt(jnp.mean(x * x, axis=-1, keepdims=True) + NORM_EPS) * gain
    if z is not None:
        y = y * _silu(z)
    return y


def _mm(a, b, *, ta=False, tb=False, out_dtype=F32, res=None, stack=None, after=None, name):
    a2, b2 = a.shape[-2:], b.shape[-2:]
    ns = b.shape[0] if stack else 1
    m = a2[1] if ta else a2[0]
    k = a2[0] if ta else a2[1]
    n = b2[0] if tb else b2[1]
    assert k == (b2[1] if tb else b2[0])
    tm, tn, tk = _mm_tiles(m, n, k, ns if stack == "sum" else 1, a.dtype.itemsize, b.dtype.itemsize,
                           jnp.dtype(out_dtype).itemsize, res is not None)
    nk = k // tk
    single = nk == 1 and stack != "sum"
    dims = ((0 if ta else 1,), (1 if tb else 0,))
    if stack == "sum":
        order = lambda g0, g1, g2, g3: (g2, g0, g1, g3)
        grid = (m // tm, n // tn, ns, nk)
    else:
        order = lambda g0, g1, g2, g3: (g0, g1, g2, g3)
        grid = (ns, m // tm, n // tn, nk)

    def body(*refs):
        if after is not None:
            refs = refs[:2 + (res is not None)] + refs[3 + (res is not None):]
        if single:
            a_ref, b_ref = refs[:2]
            r = lax.dot_general(a_ref[...].astype(BF16), b_ref[...].astype(BF16), (dims, ((), ())),
                                preferred_element_type=F32)
            if res is not None:
                r = r + refs[2][...]
            refs[-1][...] = r.astype(out_dtype)
            return
        if res is None:
            a_ref, b_ref, o_ref, acc = refs
        else:
            a_ref, b_ref, r_ref, o_ref, acc = refs
        s, _, _, kk = order(*[pl.program_id(d) for d in range(4)])
        first = kk == 0
        last = kk == nk - 1
        if stack == "sum":
            first, last = first & (s == 0), last & (s == ns - 1)

        @pl.when(first)
        def _():
            acc[...] = jnp.zeros_like(acc)

        acc[...] += lax.dot_general(a_ref[...].astype(BF16), b_ref[...].astype(BF16), (dims, ((), ())),
                                    preferred_element_type=F32)

        @pl.when(last)
        def _():
            r = acc[...]
            if res is not None:
                r = r + r_ref[...]
            o_ref[...] = r.astype(out_dtype)

    def spec(shape, idx, stacked):
        if stacked:
            return pl.BlockSpec((None,) + shape, lambda *g: (order(*g)[0],) + idx(*order(*g)))
        return pl.BlockSpec(shape, lambda *g: idx(*order(*g)))

    a_spec = (spec((tk, tm), lambda s, i, j, kk: (kk, i), stack == "sum") if ta
              else spec((tm, tk), lambda s, i, j, kk: (i, kk), stack == "sum"))
    b_spec = (spec((tn, tk), lambda s, i, j, kk: (j, kk), bool(stack)) if tb
              else spec((tk, tn), lambda s, i, j, kk: (kk, j), bool(stack)))
    o_spec = spec((tm, tn), lambda s, i, j, kk: (i, j), stack == "out")
    ins, specs = [a, b], [a_spec, b_spec]
    if res is not None:
        ins.append(res)
        specs.append(o_spec)
    if after is not None:
        ins.append(after)
        specs.append(pl.BlockSpec(after.shape, lambda *g: (0,) * after.ndim))
    sem = (("parallel", "parallel", "arbitrary", "arbitrary") if stack == "sum"
           else ("parallel", "parallel", "parallel", "arbitrary"))
    return pl.pallas_call(
        body, name=name, grid=grid, in_specs=specs, out_specs=o_spec,
        out_shape=jax.ShapeDtypeStruct(((ns,) if stack == "out" else ()) + (m, n), out_dtype),
        scratch_shapes=[] if single else [pltpu.VMEM((tm, tn), F32)],
        compiler_params=_cparams(sem),
    )(*ins)


MM_VMEM_BUDGET = 40 * 1024 * 1024
MXU_WIDTH = 256


def _mm_tiles(m, n, k, ns, sa, sb, so, has_res):
    def divs(x, mult, cap):
        out = [d for d in range(mult, min(x, cap) + 1, mult) if x % d == 0]
        return out or [x]

    best = None
    for tk in divs(k, 128, 8192):
        nk = (k // tk) * ns
        for tm in divs(m, 8, 2048):
            for tn in divs(n, 128, 2048):
                vmem = 2 * (tm * tk * sa + tk * tn * sb + tm * tn * so) + (2 * tm * tn * 4 if has_res else 0)
                vmem += tm * tn * 4 if nk > 1 else 0
                if vmem > MM_VMEM_BUDGET:
                    continue
                steps = (m // tm) * (n // tn) * nk
                traffic = (m // tm) * k * n * sb * ns + (n // tn if nk > 1 else 1) * m * k * sa * ns
                cost = steps * 0.4e-6 + traffic / 2.5e12 + (nk * m * n * 8 / 6e12 if nk > 1 else 0)
                cost += 2.0 * m * n * k * ns / 7e14 * (-(-tn // MXU_WIDTH) * MXU_WIDTH / tn)
                if best is None or cost < best[0]:
                    best = (cost, tm, tn, tk)
    return best[1:]


def _norm_fwd(x, xoff, gain, ncol, w, out_dtype, *, z=None, zoff=0, into=None, into_off=0, name):
    t = x.shape[0]
    tr = _tile(t, max(256, (1 << 18) // w), 8)

    def body(*refs):
        x_ref, g_ref, o_ref = refs[0], refs[1], refs[-1]
        y = _rms_fn(x_ref[...], g_ref[...]) if z is None else _rms_fn(x_ref[...], g_ref[...], refs[2][...])
        o_ref[...] = y.astype(out_dtype)

    ins = [x, gain]
    specs = [pl.BlockSpec((tr, w), lambda j, r: (r, xoff + j)), pl.BlockSpec((1, w), lambda j, r: (0, 0))]
    if z is not None:
        ins.append(z)
        specs.append(pl.BlockSpec((tr, w), lambda j, r: (r, zoff + j)))
    aliases = {}
    if into is not None:
        aliases = {len(ins): 0}
        ins.append(into)
        specs.append(pl.BlockSpec(memory_space=pl.ANY))
    return pl.pallas_call(
        body, name=name, grid=(ncol, t // tr), in_specs=specs,
        out_specs=pl.BlockSpec((tr, w), lambda j, r: (r, into_off + j)),
        out_shape=jax.ShapeDtypeStruct((t, ncol * w) if into is None else into.shape, out_dtype),
        input_output_aliases=aliases, compiler_params=_cparams(("parallel", "parallel")),
    )(*ins)


def _norm_bwd(x, xoff, gain, dy, dyoff, ncol, w, *, z=None, zoff=0, res=None, name):
    t = x.shape[0]
    tr = _tile(t, max(256, (1 << 18) // w), 8)

    def body(*refs):
        it = iter(refs)
        x_ref, g_ref = next(it), next(it)
        z_ref = next(it) if z is not None else None
        dy_ref = next(it)
        r_ref = next(it) if res is not None else None
        dx_ref = next(it)
        dz_ref = next(it) if z is not None else None
        dg_ref = next(it)

        @pl.when((pl.program_id(0) == 0) & (pl.program_id(1) == 0))
        def _():
            dg_ref[...] = jnp.zeros_like(dg_ref)

        args = (x_ref[...], g_ref[...]) + ((z_ref[...],) if z is not None else ())
        _, vjp = jax.vjp(_rms_fn, *args)
        grads = vjp(dy_ref[...].astype(F32))
        dx = grads[0]
        if res is not None:
            dx = dx + r_ref[...]
        dx_ref[...] = dx
        if z is not None:
            dz_ref[...] = grads[2]
        dg_ref[...] += grads[1]

    ins = [x, gain]
    specs = [pl.BlockSpec((tr, w), lambda j, r: (r, xoff + j)), pl.BlockSpec((1, w), lambda j, r: (0, 0))]
    if z is not None:
        ins.append(z)
        specs.append(pl.BlockSpec((tr, w), lambda j, r: (r, zoff + j)))
    ins.append(dy)
    specs.append(pl.BlockSpec((tr, w), lambda j, r: (r, dyoff + j)))
    blk = pl.BlockSpec((tr, w), lambda j, r: (r, j))
    if res is not None:
        ins.append(res)
        specs.append(blk)
    full = jax.ShapeDtypeStruct((t, ncol * w), F32)
    out_shape, out_specs = [full], [blk]
    if z is not None:
        out_shape.append(full)
        out_specs.append(blk)
    out_shape.append(jax.ShapeDtypeStruct((1, w), F32))
    out_specs.append(pl.BlockSpec((1, w), lambda j, r: (0, 0)))
    return pl.pallas_call(
        body, name=name, grid=(ncol, t // tr), in_specs=specs, out_specs=out_specs, out_shape=out_shape,
        compiler_params=_cparams(("arbitrary", "arbitrary")),
    )(*ins)


def _small_fn(x, pa, pb, nf, ng):
    lane = lax.broadcasted_iota(jnp.int32, x.shape, 1)
    zz = x + pb
    logf = -_softplus(-zz)
    g = -jnp.exp(pa) * _softplus(zz)
    beta = _sigmoid(x)
    return jnp.where(lane < nf, logf, jnp.where(lane < nf + ng, g, beta))


def _tri(n, upper):
    r = lax.broadcasted_iota(jnp.int32, (n, n), 0)
    c = lax.broadcasted_iota(jnp.int32, (n, n), 1)
    return jnp.where((c >= r) if upper else (c <= r), 1.0, 0.0).astype(F32)


def _small_fwd(p, off, pa, pb, nf, ng):
    t = p.shape[0]
    blk = HEAD_DIM
    nb = t // blk

    def body(x_ref, pa_ref, pb_ref, v_ref, c_ref):
        v_ref[...] = _small_fn(x_ref[...], pa_ref[...], pb_ref[...], nf, ng)
        tri = _tri(blk, False)

        carry = jnp.zeros((1, HEAD_DIM), F32)
        for i in range(nb):
            rows = slice(i * blk, (i + 1) * blk)
            c = _nn_hi(tri, v_ref[rows, :]) + carry
            c_ref[rows, :] = c
            carry = c[blk - 1:blk, :]

    row = pl.BlockSpec((1, HEAD_DIM), lambda i: (0, 0))
    out = pl.BlockSpec((t, HEAD_DIM), lambda i: (0, 0))
    return pl.pallas_call(
        body, name="small_fwd", grid=(1,),
        in_specs=[pl.BlockSpec((t, HEAD_DIM), lambda i: (0, off)), row, row], out_specs=[out, out],
        out_shape=[jax.ShapeDtypeStruct((t, HEAD_DIM), F32)] * 2,
        compiler_params=_cparams(("arbitrary",)),
    )(p, pa, pb)


def _small_bwd(p, off, pa, pb, dvals, dcsum, nf, ng):
    t = p.shape[0]
    blk = HEAD_DIM
    nb = t // blk

    def body(x_ref, pa_ref, pb_ref, dv_ref, dc_ref, dx_ref, dpa_ref, dpb_ref, tot_ref):
        tri = _tri(blk, True)

        carry = jnp.zeros((1, HEAD_DIM), F32)
        for i in reversed(range(nb)):
            rows = slice(i * blk, (i + 1) * blk)
            c = _nn_hi(tri, dc_ref[rows, :]) + carry
            tot_ref[rows, :] = c + dv_ref[rows, :]
            carry = c[0:1, :]
        f = functools.partial(_small_fn, nf=nf, ng=ng)
        _, vjp = jax.vjp(f, x_ref[...], pa_ref[...], pb_ref[...])
        dx, dpa, dpb = vjp(tot_ref[...])
        dx_ref[...] = dx
        dpa_ref[...] = dpa
        dpb_ref[...] = dpb

    row = pl.BlockSpec((1, HEAD_DIM), lambda i: (0, 0))
    full = pl.BlockSpec((t, HEAD_DIM), lambda i: (0, 0))
    return pl.pallas_call(
        body, name="small_bwd", grid=(1,),
        in_specs=[pl.BlockSpec((t, HEAD_DIM), lambda i: (0, off)), row, row, full, full],
        out_specs=[full, row, row],
        out_shape=[jax.ShapeDtypeStruct((t, HEAD_DIM), F32), jax.ShapeDtypeStruct((1, HEAD_DIM), F32),
                   jax.ShapeDtypeStruct((1, HEAD_DIM), F32)],
        scratch_shapes=[pltpu.VMEM((t, HEAD_DIM), F32)],
        compiler_params=_cparams(("arbitrary",)),
    )(p, pa, pb, dvals, dcsum)


def _fox_heads(nf, most):
    return next(h for h in range(most, 0, -1) if nf % h == 0)


def _fox_fwd(q, k, v, cc, cr, nf, tq, tk, d_mix):
    t = q.shape[0]
    scale = HEAD_DIM ** -0.5
    assert tq == tk

    vt = jnp.transpose(v.reshape(t // tk, tk, nf, HEAD_DIM), (2, 0, 3, 1))

    hp = _fox_heads(nf, 3)
    lanes = lambda h: slice(h * HEAD_DIM, (h + 1) * HEAD_DIM)

    def body(q_ref, k_ref, vt_ref, cc_ref, cr_ref, o_ref, lse_ref, mix_ref):
        i = pl.program_id(1)
        qs = [q_ref[:, lanes(h)] for h in range(hp)]
        cqs = [cr_ref[h, i] for h in range(hp)]
        ones = jnp.ones((8, tk), BF16)
        diff = lax.broadcasted_iota(jnp.int32, (tk, tq), 0) - lax.broadcasted_iota(jnp.int32, (tk, tq), 1)

        def scores(h, j):
            ks = pl.ds(pl.multiple_of(j * tk, tk), tk)
            return lax.dot_general(k_ref[ks, lanes(h)], qs[h], (((1,), (1,)), ((), ())),
                                   preferred_element_type=F32)

        def tile(h, j, m, l, acc, s, masked):
            ks = pl.ds(pl.multiple_of(j * tk, tk), tk)
            s = s * scale + cqs[h] - cc_ref[0, ks, h:h + 1]
            if masked:
                s = jnp.where(diff <= 0, s, NEG)
            m_new = jnp.maximum(m, jnp.max(s, axis=0, keepdims=True))
            pr = jnp.exp(s - m_new).astype(BF16)
            alpha = jnp.exp(m - m_new)
            l = alpha * l + jnp.dot(ones, pr, preferred_element_type=F32)[:1]
            acc = alpha * acc + jnp.dot(vt_ref[h, j], pr, preferred_element_type=F32)
            return m_new, l, acc

        def step(j, carry):
            nxt = [scores(h, j + 1) for h in range(hp)]
            return tuple(tile(h, j, *carry[h], False) + (nxt[h],) for h in range(hp))

        init = tuple((jnp.full((1, tq), NEG, F32), jnp.zeros((1, tq), F32), jnp.zeros((HEAD_DIM, tq), F32),
                      scores(h, 0)) for h in range(hp))
        carry = lax.fori_loop(0, i, step, init)
        for h in range(hp):
            m, l, acc = tile(h, i, *carry[h], True)
            o = jnp.transpose(acc / l)
            o_ref[:, lanes(h)] = o
            mix_ref[:, lanes(h)] = o.astype(BF16)
            lse_ref[h, 0] = m + jnp.log(l)

    w = hp * HEAD_DIM
    qblk = pl.BlockSpec((tq, w), lambda h, i: (i, h))
    return pl.pallas_call(
        body, name="fox_fwd", grid=(nf // hp, t // tq),
        in_specs=[qblk, pl.BlockSpec((t, w), lambda h, i: (0, h)),
                  pl.BlockSpec((hp, t // tk, HEAD_DIM, tk), lambda h, i: (h, 0, 0, 0)),
                  pl.BlockSpec((1, t, HEAD_DIM), lambda h, i: (h, 0, 0)),
                  pl.BlockSpec((hp, t // tk, 1, tk), lambda h, i: (h, 0, 0, 0))],
        out_specs=[qblk, pl.BlockSpec((hp, 1, 1, tq), lambda h, i: (h, i, 0, 0)), qblk],
        out_shape=[jax.ShapeDtypeStruct((t, nf * HEAD_DIM), F32), jax.ShapeDtypeStruct((nf, t // tq, 1, tq), F32),
                   jax.ShapeDtypeStruct((t, d_mix), BF16)],
        compiler_params=_cparams(("parallel", "parallel")),
    )(q, k, vt, cc, cr)


def _fox_bwd(q, k, v, cc, cr, o, lse, dmix, nf, tq, tk):
    t = q.shape[0]
    scale = HEAD_DIM ** -0.5
    assert tq == tk
    hp = _fox_heads(nf, 3)
    lanes = lambda h: slice(h * HEAD_DIM, (h + 1) * HEAD_DIM)
    kt = jnp.transpose(k.reshape(t // tk, tk, nf, HEAD_DIM), (2, 0, 3, 1))

    def body(q_ref, k_ref, kt_ref, v_ref, cc_ref, cr_ref, o_ref, lse_ref, do_ref,
             dq_ref, dk_ref, dv_ref, dcq_ref, dck_ref):
        i = pl.program_id(1)

        @pl.when(i == 0)
        def _():
            dk_ref[...] = jnp.zeros_like(dk_ref)
            dv_ref[...] = jnp.zeros_like(dv_ref)
            dck_ref[...] = jnp.zeros_like(dck_ref)

        diff = lax.broadcasted_iota(jnp.int32, (tk, tq), 0) - lax.broadcasted_iota(jnp.int32, (tk, tq), 1)
        lane = lax.broadcasted_iota(jnp.int32, (tk, HEAD_DIM), 1)
        qs = [q_ref[:, lanes(h)] for h in range(hp)]
        dos = [do_ref[:, lanes(h)] for h in range(hp)]
        do_b = [d.astype(BF16) for d in dos]
        cqs = [cr_ref[h, i] for h in range(hp)]
        lses = [lse_ref[h, 0] for h in range(hp)]
        deltas = [jnp.sum(jnp.transpose(dos[h] * o_ref[:, lanes(h)]), axis=0, keepdims=True) for h in range(hp)]

        def products(h, j):
            ks = pl.ds(pl.multiple_of(j * tk, tk), tk)
            nt = (((1,), (1,)), ((), ()))
            return (lax.dot_general(k_ref[ks, lanes(h)], qs[h], nt, preferred_element_type=F32),
                    lax.dot_general(v_ref[ks, lanes(h)], do_b[h], nt, preferred_element_type=F32))

        def tile(h, j, dqt, dcq, s, dp, masked):
            ks = pl.ds(pl.multiple_of(j * tk, tk), tk)
            pr = jnp.exp(s * scale + cqs[h] - cc_ref[0, ks, h:h + 1] - lses[h])
            if masked:
                pr = jnp.where(diff <= 0, pr, 0.0)
            ds = pr * (dp - deltas[h])
            ds_b = ds.astype(BF16)
            dqt = dqt + jnp.dot(kt_ref[h, j], ds_b, preferred_element_type=F32)
            dk_ref[ks, lanes(h)] += jnp.dot(ds_b, qs[h], preferred_element_type=F32) * scale
            dv_ref[ks, lanes(h)] += jnp.dot(pr.astype(BF16), do_b[h], preferred_element_type=F32)
            dck_ref[0, ks, :] -= jnp.where(lane == h, jnp.sum(ds, axis=1, keepdims=True), 0.0)
            return dqt, dcq + jnp.sum(ds, axis=0, keepdims=True)

        def step(j, carry):
            nxt = [products(h, j + 1) for h in range(hp)]
            return tuple(tile(h, j, *carry[h], False) + nxt[h] for h in range(hp))

        init = tuple((jnp.zeros((HEAD_DIM, tq), F32), jnp.zeros((1, tq), F32)) + products(h, 0) for h in range(hp))
        carry = lax.fori_loop(0, i, step, init)
        for h in range(hp):
            dqt, dcq = tile(h, i, *carry[h], True)
            dq_ref[:, lanes(h)] = jnp.transpose(dqt) * scale
            dcq_ref[h, 0] = dcq

    w = hp * HEAD_DIM
    head_all = pl.BlockSpec((t, w), lambda h, i: (0, h))
    qblk = pl.BlockSpec((tq, w), lambda h, i: (i, h))
    colv = pl.BlockSpec((1, t, HEAD_DIM), lambda h, i: (h, 0, 0))
    rows_all = pl.BlockSpec((hp, t // tk, 1, tk), lambda h, i: (h, 0, 0, 0))
    row_blk = pl.BlockSpec((hp, 1, 1, tq), lambda h, i: (h, i, 0, 0))
    wide = jax.ShapeDtypeStruct((t, nf * HEAD_DIM), F32)
    return pl.pallas_call(
        body, name="fox_bwd", grid=(nf // hp, t // tq),
        in_specs=[qblk, head_all, pl.BlockSpec((hp, t // tk, HEAD_DIM, tk), lambda h, i: (h, 0, 0, 0)), head_all,
                  colv, rows_all, qblk, row_blk, qblk],
        out_specs=[qblk, head_all, head_all, row_blk, colv],
        out_shape=[wide, wide, wide, jax.ShapeDtypeStruct((nf, t // tq, 1, tq), F32),
                   jax.ShapeDtypeStruct((nf // hp, t, HEAD_DIM), F32)],
        compiler_params=_cparams(("parallel", "arbitrary")),
    )(q, k, kt, v, cc, cr, o, lse, dmix)


def _mem_fn(mq, mk, mv, gq, gk):
    qn = _rms_fn(mq, gq)
    kn = _rms_fn(mk, gk)
    s = _nt(qn, kn) * (HEAD_DIM ** -0.5)
    e = jnp.exp(s - lax.stop_gradient(jnp.max(s, axis=1, keepdims=True)))
    pr = e / jnp.sum(e, axis=1, keepdims=True)
    return _nn(pr, mv)


def _mem_specs(t, m, tq, qoff):
    qblk = pl.BlockSpec((tq, HEAD_DIM), lambda h, i: (i, qoff + h))
    kblk = pl.BlockSpec((m, HEAD_DIM), lambda h, i: (0, h))
    vblk = pl.BlockSpec((m, HEAD_DIM), lambda h, i: (0, N_MEM_HEADS + h))
    row = pl.BlockSpec((1, HEAD_DIM), lambda h, i: (0, 0))
    return qblk, kblk, vblk, row


def _mem_fwd(p, qoff, mkv, gq, gk, tq, into, into_off):
    t, m = p.shape[0], mkv.shape[0]
    qblk, kblk, vblk, row = _mem_specs(t, m, tq, qoff)

    def body(q_ref, k_ref, v_ref, gq_ref, gk_ref, _, o_ref):
        o_ref[...] = _mem_fn(q_ref[...], k_ref[...], v_ref[...], gq_ref[...], gk_ref[...]).astype(BF16)

    return pl.pallas_call(
        body, name="mem_fwd", grid=(N_MEM_HEADS, t // tq),
        in_specs=[qblk, kblk, vblk, row, row, pl.BlockSpec(memory_space=pl.ANY)],
        out_specs=pl.BlockSpec((tq, HEAD_DIM), lambda h, i: (i, into_off + h)),
        out_shape=jax.ShapeDtypeStruct(into.shape, BF16), input_output_aliases={5: 0},
        compiler_params=_cparams(("parallel", "parallel")),
    )(p, mkv, mkv, gq, gk, into)


def _mem_bwd(p, qoff, mkv, gq, gk, dmix, dooff, tq):
    t, m = p.shape[0], mkv.shape[0]
    qblk, kblk, vblk, row = _mem_specs(t, m, tq, qoff)

    def body(q_ref, k_ref, v_ref, gq_ref, gk_ref, do_ref, dq_ref, dkv_k_ref, dkv_v_ref, dgq_ref, dgk_ref):
        h, i = pl.program_id(0), pl.program_id(1)

        @pl.when((h == 0) & (i == 0))
        def _():
            dgq_ref[...] = jnp.zeros_like(dgq_ref)
            dgk_ref[...] = jnp.zeros_like(dgk_ref)

        @pl.when(i == 0)
        def _():
            dkv_k_ref[...] = jnp.zeros_like(dkv_k_ref)
            dkv_v_ref[...] = jnp.zeros_like(dkv_v_ref)

        _, vjp = jax.vjp(_mem_fn, q_ref[...], k_ref[...], v_ref[...], gq_ref[...], gk_ref[...])
        dq, dk, dv, dgq, dgk = vjp(do_ref[...])
        dq_ref[...] = dq
        dkv_k_ref[...] += dk
        dkv_v_ref[...] += dv
        dgq_ref[...] += dgq
        dgk_ref[...] += dgk

    oblk = pl.BlockSpec((tq, HEAD_DIM), lambda h, i: (i, h))
    kout = pl.BlockSpec((m, HEAD_DIM), lambda h, i: (0, h))
    half = jax.ShapeDtypeStruct((m, N_MEM_HEADS * HEAD_DIM), F32)
    rshape = jax.ShapeDtypeStruct((1, HEAD_DIM), F32)
    return pl.pallas_call(
        body, name="mem_bwd", grid=(N_MEM_HEADS, t // tq),
        in_specs=[qblk, kblk, vblk, row, row, pl.BlockSpec((tq, HEAD_DIM), lambda h, i: (i, dooff + h))],
        out_specs=[oblk, kout, kout, row, row],
        out_shape=[jax.ShapeDtypeStruct((t, N_MEM_HEADS * HEAD_DIM), F32), half, half, rshape, rshape],
        compiler_params=_cparams(("arbitrary", "arbitrary")),
    )(p, mkv, mkv, gq, gk, dmix)


def _shift_down(x, s):
    if s == 0:
        return x
    r = lax.broadcasted_iota(jnp.int32, x.shape, 0)
    return jnp.where(r >= s, pltpu.roll(x, s, 0), 0.0)


def _shift_up(x, s):
    if s == 0:
        return x
    n = x.shape[0]
    r = lax.broadcasted_iota(jnp.int32, x.shape, 0)
    return jnp.where(r < n - s, pltpu.roll(x, n - s, 0), 0.0)


def _conv_fn(x0, x1, x2, x3, w0, w1, w2, w3, kind):
    y = _silu(x0 * w0 + x1 * w1 + x2 * w2 + x3 * w3)
    if kind == 2:
        return y
    y = y * lax.rsqrt(jnp.sum(y * y, axis=-1, keepdims=True) + NORM_EPS)
    return y * (HEAD_DIM ** -0.5) if kind == 0 else y


def _conv_fwd(p, off, conv_w, ng):
    t = p.shape[0]

    def body(x_ref, w_ref, o_ref):
        kind = pl.program_id(0) // ng
        x = x_ref[...]
        xs = [_shift_down(x, CONV_WIDTH - 1 - j) for j in range(CONV_WIDTH)]
        ws = [w_ref[j:j + 1, :] for j in range(CONV_WIDTH)]
        for kd in range(3):
            @pl.when(kind == kd)
            def _(kd=kd):
                o_ref[...] = _conv_fn(*xs, *ws, kd)

    return pl.pallas_call(
        body, name="gdn_conv_fwd", grid=(3 * ng,),
        in_specs=[pl.BlockSpec((t, HEAD_DIM), lambda c: (0, off + c)),
                  pl.BlockSpec((CONV_WIDTH, HEAD_DIM), lambda c: (0, c))],
        out_specs=pl.BlockSpec((t, HEAD_DIM), lambda c: (0, c)),
        out_shape=jax.ShapeDtypeStruct((t, 3 * ng * HEAD_DIM), F32),
        compiler_params=_cparams(("parallel",)),
    )(p, conv_w)


def _conv_bwd(p, off, conv_w, dys, ng):
    t = p.shape[0]

    def body(x_ref, w_ref, dq_ref, dk_ref, dv_ref, dx_ref, dw_ref):
        kind = pl.program_id(0) // ng
        dy_refs = (dq_ref, dk_ref, dv_ref)
        x = x_ref[...]
        xs = [_shift_down(x, CONV_WIDTH - 1 - j) for j in range(CONV_WIDTH)]
        ws = [w_ref[j:j + 1, :] for j in range(CONV_WIDTH)]
        for kd in range(3):
            @pl.when(kind == kd)
            def _(kd=kd):
                _, vjp = jax.vjp(functools.partial(_conv_fn, kind=kd), *xs, *ws)
                g = vjp(dy_refs[kd][...])
                dx = _shift_up(g[0], CONV_WIDTH - 1)
                for j in range(1, CONV_WIDTH):
                    dx = dx + _shift_up(g[j], CONV_WIDTH - 1 - j)
                dx_ref[...] = dx.astype(BF16)
                for j in range(CONV_WIDTH):
                    dw_ref[j:j + 1, :] = g[CONV_WIDTH + j]

    blk = pl.BlockSpec((t, HEAD_DIM), lambda c: (0, off + c))
    head = lambda k: pl.BlockSpec((t, HEAD_DIM), lambda c: (0, jnp.where(c // ng == k, c % ng, 0)))
    wblk = pl.BlockSpec((CONV_WIDTH, HEAD_DIM), lambda c: (0, c))
    return pl.pallas_call(
        body, name="gdn_conv_bwd", grid=(3 * ng,),
        in_specs=[blk, wblk] + [head(k) for k in range(3)],
        out_specs=[blk, wblk],
        out_shape=[jax.ShapeDtypeStruct(p.shape, BF16),
                   jax.ShapeDtypeStruct((CONV_WIDTH, 3 * ng * HEAD_DIM), F32)],
        compiler_params=_cparams(("parallel",)),
    )(p, conv_w, *dys)


def _lower_inverse(lower):
    c = lower.shape[-1]
    r = lax.broadcasted_iota(jnp.int32, (1, c, c), 1)
    e = lax.broadcasted_iota(jnp.int32, (1, c, c), 2)
    hi = lax.Precision.HIGH
    inv = jnp.where(r == e, 1.0, 0.0) - lower
    pw = lower
    for _ in range(int(math.log2(c)) - 1):
        pw = _dot(pw, pw, ((1,), (0,)), hi)
        inv = inv + _dot(inv, pw, ((1,), (0,)), hi)
    return inv


@jax.custom_vjp
def _solve(lower, inv, vb, kbg):
    hi = lax.Precision.HIGH
    return _dot(inv, vb, ((1,), (0,)), hi), _dot(inv, kbg, ((1,), (0,)), hi)


def _solve_fwd(lower, inv, vb, kbg):
    u, w = _solve(lower, inv, vb, kbg)
    return (u, w), (inv, u, w)


def _solve_bwd(res, cts):
    inv, u, w = res
    dvb, dkbg = _tn(inv, cts[0]), _tn(inv, cts[1])
    return -(_nt(dvb, u) + _nt(dkbg, w)), jnp.zeros_like(inv), dvb, dkbg


_solve.defvjp(_solve_fwd, _solve_bwd)


def _wy_fn(q, k, v, gcol, grow, bcol, inv=None):
    b, c, dk = q.shape
    r = lax.broadcasted_iota(jnp.int32, (1, c, c), 1)
    e = lax.broadcasted_iota(jnp.int32, (1, c, c), 2)
    tril, strict = e <= r, e < r
    gc_col = jnp.sum(jnp.where(tril, grow, 0.0), axis=2, keepdims=True)
    gc_row = jnp.sum(jnp.where(r <= e, gcol, 0.0), axis=1, keepdims=True)
    g_last = jnp.sum(gcol, axis=1, keepdims=True)
    decay = jnp.exp(jnp.where(tril, gc_col - gc_row, NEG))
    kb, vb = k * bcol, v * bcol
    lower = jnp.where(strict, _nt(kb, k) * decay, 0.0)
    if inv is None:
        inv = _lower_inverse(lower)
    u, w = _solve(lower, inv, vb, kb * jnp.exp(gc_col))
    attn = jnp.where(tril, _nt(q, k) * decay, 0.0)
    qg = q * jnp.exp(gc_col)
    kdec = k * jnp.exp(g_last - gc_col)
    egl = jnp.broadcast_to(jnp.exp(g_last), (b, 1, dk))
    return u, w, qg, kdec, attn, egl, inv


def _scan_fn(u, w, qg, kdec, attn, egl, state):
    v_new = u - _nn(w, state)
    o = _nn(qg, state) + _nn(attn, v_new)
    return o, state * egl + _tn(kdec, v_new)


GDN_CHUNKS_PER_STEP = 4
GDN_SCAN_CHUNKS = 4


def _gdn_fwd(qkv, vals, grow, nf, ng):
    t = qkv.shape[0]
    nch = t // CHUNK

    cb = GDN_CHUNKS_PER_STEP
    *wy, inv = _gdn_wy(qkv, vals, grow, nf, ng, cb)

    sc = GDN_SCAN_CHUNKS

    def body(u_ref, w_ref, qg_ref, kd_ref, at_ref, eg_ref, o_ref, st_ref, state):
        @pl.when(pl.program_id(0) == 0)
        def _():
            state[...] = jnp.zeros_like(state)

        for c in range(sc):
            rows = slice(c * CHUNK, (c + 1) * CHUNK)
            heads = lambda ref: jnp.stack([ref[rows, h * HEAD_DIM:(h + 1) * HEAD_DIM] for h in range(ng)])
            st_ref[:, c] = state[...]
            o, new = _scan_fn(heads(u_ref), heads(w_ref), heads(qg_ref), heads(kd_ref), at_ref[:, c], eg_ref[:, c],
                              state[...])
            for h in range(ng):
                o_ref[rows, h * HEAD_DIM:(h + 1) * HEAD_DIM] = o[h]
            state[...] = new

    w = ng * HEAD_DIM
    blk = pl.BlockSpec((sc * CHUNK, w), lambda i: (i, 0))
    o, states = pl.pallas_call(
        body, name="gdn_scan_fwd", grid=(nch // sc,),
        in_specs=[blk, blk, blk, blk, pl.BlockSpec((ng, sc, CHUNK, CHUNK), lambda i: (0, i, 0, 0)),
                  pl.BlockSpec((ng, sc, 1, HEAD_DIM), lambda i: (0, i, 0, 0))],
        out_specs=[blk, pl.BlockSpec((ng, sc, HEAD_DIM, HEAD_DIM), lambda i: (0, i, 0, 0))],
        out_shape=[jax.ShapeDtypeStruct((t, w), F32),
                   jax.ShapeDtypeStruct((ng, nch, HEAD_DIM, HEAD_DIM), F32)],
        scratch_shapes=[pltpu.VMEM((ng, HEAD_DIM, HEAD_DIM), F32)],
        compiler_params=_cparams(("arbitrary",)),
    )(*wy)
    return o, (wy, inv, states)


def _wy_batch(q_ref, k_ref, v_ref, vals_ref, gr_ref, nf, ng, cb):
    idx = [(c, h) for c in range(cb) for h in range(ng)]
    rows = lambda c: slice(c * CHUNK, (c + 1) * CHUNK)
    lanes = lambda h: slice(h * HEAD_DIM, (h + 1) * HEAD_DIM)
    wide = lambda ref: jnp.stack([ref[rows(c), lanes(h)] for c, h in idx])
    col = lambda lane0: jnp.stack([vals_ref[rows(c), lane0 + h:lane0 + h + 1] for c, h in idx])
    return idx, (wide(q_ref), wide(k_ref), wide(v_ref), col(nf), jnp.stack([gr_ref[h, c] for c, h in idx]),
                 col(nf + ng))


def _gdn_wy(qkv, vals, grow, nf, ng, cb):
    t = qkv.shape[0]
    nch = t // CHUNK

    def body(q_ref, k_ref, v_ref, vals_ref, gr_ref, u_ref, w_ref, qg_ref, kd_ref, at_ref, eg_ref, inv_ref):
        idx, args = _wy_batch(q_ref, k_ref, v_ref, vals_ref, gr_ref, nf, ng, cb)
        u, w, qg, kd, at, eg, inv = _wy_fn(*args)
        for b, (c, h) in enumerate(idx):
            rows, lanes = slice(c * CHUNK, (c + 1) * CHUNK), slice(h * HEAD_DIM, (h + 1) * HEAD_DIM)
            u_ref[rows, lanes] = u[b]
            w_ref[rows, lanes] = w[b]
            qg_ref[rows, lanes] = qg[b]
            kd_ref[rows, lanes] = kd[b]
            at_ref[h, c] = at[b]
            eg_ref[h, c] = eg[b]
            inv_ref[h, c] = inv[b]

    wd = ng * HEAD_DIM
    blk = lambda o: pl.BlockSpec((cb * CHUNK, wd), lambda i: (i, o))
    col = pl.BlockSpec((cb * CHUNK, HEAD_DIM), lambda i: (i, 0))
    sq = pl.BlockSpec((ng, cb, CHUNK, CHUNK), lambda i: (0, i, 0, 0))
    wide = jax.ShapeDtypeStruct((t, wd), F32)
    sq_shape = jax.ShapeDtypeStruct((ng, nch, CHUNK, CHUNK), F32)
    return pl.pallas_call(
        body, name="gdn_wy_fwd", grid=(nch // cb,),
        in_specs=[blk(0), blk(1), blk(2), col, pl.BlockSpec((ng, cb, 1, CHUNK), lambda i: (0, i, 0, 0))],
        out_specs=[blk(0), blk(0), blk(0), blk(0), sq, pl.BlockSpec((ng, cb, 1, HEAD_DIM), lambda i: (0, i, 0, 0)),
                   sq],
        out_shape=[wide, wide, wide, wide, sq_shape, jax.ShapeDtypeStruct((ng, nch, 1, HEAD_DIM), F32), sq_shape],
        compiler_params=_cparams(("parallel",)),
    )(qkv, qkv, qkv, vals, grow)


def _gdn_bwd(qkv, vals, grow, saved, do, nf, ng):
    t = qkv.shape[0]
    nch = t // CHUNK
    cb = GDN_CHUNKS_PER_STEP // 2
    wy, inv, states = saved
    wd = ng * HEAD_DIM

    def scan_body(u_ref, w_ref, qg_ref, kd_ref, at_ref, eg_ref, st_ref, do_ref,
                  du_ref, dw_ref, dqg_ref, dkd_ref, dat_ref, deg_ref, dstate):
        @pl.when(pl.program_id(0) == 0)
        def _():
            dstate[...] = jnp.zeros_like(dstate)

        for c in reversed(range(sc)):
            rows = slice(c * CHUNK, (c + 1) * CHUNK)
            heads = lambda ref: jnp.stack([ref[rows, h * HEAD_DIM:(h + 1) * HEAD_DIM] for h in range(ng)])
            _, vjp = jax.vjp(_scan_fn, heads(u_ref), heads(w_ref), heads(qg_ref), heads(kd_ref), at_ref[:, c],
                             eg_ref[:, c], st_ref[:, c])
            du, dw, dqg, dkd, dat, deg, dst = vjp((heads(do_ref), dstate[...]))
            for h in range(ng):
                lanes = slice(h * HEAD_DIM, (h + 1) * HEAD_DIM)
                du_ref[rows, lanes] = du[h]
                dw_ref[rows, lanes] = dw[h]
                dqg_ref[rows, lanes] = dqg[h]
                dkd_ref[rows, lanes] = dkd[h]
            dat_ref[:, c] = dat
            deg_ref[:, c] = deg
            dstate[...] = dst

    sc = GDN_SCAN_CHUNKS
    rev = lambda i: nch // sc - 1 - i
    blk = pl.BlockSpec((sc * CHUNK, wd), lambda i: (rev(i), 0))
    atb = pl.BlockSpec((ng, sc, CHUNK, CHUNK), lambda i: (0, rev(i), 0, 0))
    egb = pl.BlockSpec((ng, sc, 1, HEAD_DIM), lambda i: (0, rev(i), 0, 0))
    wide = jax.ShapeDtypeStruct((t, wd), F32)
    at_shape = jax.ShapeDtypeStruct((ng, nch, CHUNK, CHUNK), F32)
    eg_shape = jax.ShapeDtypeStruct((ng, nch, 1, HEAD_DIM), F32)
    dwy = pl.pallas_call(
        scan_body, name="gdn_scan_bwd", grid=(nch // sc,),
        in_specs=[blk, blk, blk, blk, atb, egb,
                  pl.BlockSpec((ng, sc, HEAD_DIM, HEAD_DIM), lambda i: (0, rev(i), 0, 0)), blk],
        out_specs=[blk, blk, blk, blk, atb, egb],
        out_shape=[wide, wide, wide, wide, at_shape, eg_shape],
        scratch_shapes=[pltpu.VMEM((ng, HEAD_DIM, HEAD_DIM), F32)],
        compiler_params=_cparams(("arbitrary",)),
    )(*wy, states, do)

    def wy_body(q_ref, k_ref, v_ref, vals_ref, gr_ref, du_ref, dw_ref, dqg_ref, dkd_ref, dat_ref, deg_ref,
                inv_ref, dq_ref, dk_ref, dv_ref, dvals_ref, dgr_ref):
        idx, args = _wy_batch(q_ref, k_ref, v_ref, vals_ref, gr_ref, nf, ng, cb)
        lane = lax.broadcasted_iota(jnp.int32, (CHUNK, HEAD_DIM), 1)
        kept = jnp.stack([inv_ref[h, c] for c, h in idx])
        rows = lambda c: slice(c * CHUNK, (c + 1) * CHUNK)
        lanes = lambda h: slice(h * HEAD_DIM, (h + 1) * HEAD_DIM)
        wide_ct = lambda ref: jnp.stack([ref[rows(c), lanes(h)] for c, h in idx])
        cts = (wide_ct(du_ref), wide_ct(dw_ref), wide_ct(dqg_ref), wide_ct(dkd_ref),
               jnp.stack([dat_ref[h, c] for c, h in idx]), jnp.stack([deg_ref[h, c] for c, h in idx]))
        _, vjp = jax.vjp(lambda *a: _wy_fn(*a, inv=kept)[:6], *args)
        dq, dk, dv, dgc, dgr, dbc = vjp(cts)
        for b, (c, h) in enumerate(idx):
            dq_ref[rows(c), lanes(h)] = dq[b]
            dk_ref[rows(c), lanes(h)] = dk[b]
            dv_ref[rows(c), lanes(h)] = dv[b]
            dgr_ref[h, c] = dgr[b]
        for c in range(cb):
            acc = jnp.zeros((CHUNK, HEAD_DIM), F32)
            for h in range(ng):
                acc = jnp.where(lane == nf + h, dgc[c * ng + h], acc)
                acc = jnp.where(lane == nf + ng + h, dbc[c * ng + h], acc)
            dvals_ref[rows(c), :] = acc

    cblk = lambda o: pl.BlockSpec((cb * CHUNK, wd), lambda i: (i, o))
    col = pl.BlockSpec((cb * CHUNK, HEAD_DIM), lambda i: (i, 0))
    rowv = pl.BlockSpec((ng, cb, 1, CHUNK), lambda i: (0, i, 0, 0))
    return pl.pallas_call(
        wy_body, name="gdn_wy_bwd", grid=(nch // cb,),
        in_specs=[cblk(0), cblk(1), cblk(2), col, rowv, cblk(0), cblk(0), cblk(0), cblk(0),
                  pl.BlockSpec((ng, cb, CHUNK, CHUNK), lambda i: (0, i, 0, 0)),
                  pl.BlockSpec((ng, cb, 1, HEAD_DIM), lambda i: (0, i, 0, 0)),
                  pl.BlockSpec((ng, cb, CHUNK, CHUNK), lambda i: (0, i, 0, 0))],
        out_specs=[cblk(0), cblk(0), cblk(0), col, rowv],
        out_shape=[wide, wide, wide, jax.ShapeDtypeStruct((t, HEAD_DIM), F32),
                   jax.ShapeDtypeStruct((ng, nch, 1, CHUNK), F32)],
        compiler_params=_cparams(("parallel",)),
    )(qkv, qkv, qkv, vals, grow, *dwy, inv)


def _swiglu_fn(gate, up):
    return _silu(gate) * up


FFN_TN = 512


def _ffn_up(n2, wgu4):
    _, d, w = wgu4.shape
    t = n2.shape[0]
    tn = _tile(w, FFN_TN)
    nb = w // tn

    def body(a_ref, b_ref, gu_ref, act_ref):
        av = a_ref[...]
        gate = jnp.dot(av, b_ref[0], preferred_element_type=F32)
        up = jnp.dot(av, b_ref[1], preferred_element_type=F32)
        gu_ref[0] = gate.astype(BF16)
        gu_ref[1] = up.astype(BF16)
        act_ref[...] = _swiglu_fn(gate, up).astype(BF16)

    return pl.pallas_call(
        body, name="ffn_up", grid=(2, nb),
        in_specs=[pl.BlockSpec((t, d), lambda j, l: (0, 0)), pl.BlockSpec((2, d, tn), lambda j, l: (j, 0, l))],
        out_specs=[pl.BlockSpec((2, t, tn), lambda j, l: (j, 0, l)),
                   pl.BlockSpec((t, tn), lambda j, l: (0, j * nb + l))],
        out_shape=[jax.ShapeDtypeStruct((4, t, w), BF16), jax.ShapeDtypeStruct((t, 2 * w), BF16)],
        compiler_params=_cparams(("parallel", "parallel")),
    )(n2, wgu4)


def _ffn_dact(dh2, wd, gu, after):
    _, t, w = gu.shape
    d = dh2.shape[1]
    tn = _tile(w, FFN_TN)
    nb = w // tn

    def body(a_ref, b_ref, gu_ref, _, o_ref):
        dact = lax.dot_general(a_ref[...], b_ref[...], (((1,), (1,)), ((), ())), preferred_element_type=F32)
        _, vjp = jax.vjp(_swiglu_fn, gu_ref[0].astype(F32), gu_ref[1].astype(F32))
        dg, du = vjp(dact)
        o_ref[0] = dg.astype(BF16)
        o_ref[1] = du.astype(BF16)

    pair = pl.BlockSpec((2, t, tn), lambda j, l: (j, 0, l))
    return pl.pallas_call(
        body, name="ffn_dact", grid=(2, nb),
        in_specs=[pl.BlockSpec((t, d), lambda j, l: (0, 0)), pl.BlockSpec((tn, d), lambda j, l: (j * nb + l, 0)),
                  pair, pl.BlockSpec(after.shape, lambda j, l: (0, 0))],
        out_specs=pair, out_shape=jax.ShapeDtypeStruct(gu.shape, BF16),
        compiler_params=_cparams(("parallel", "parallel")),
    )(dh2, wd, gu, after)


def _loss_head(h2, target):
    t, d = h2.shape
    tr = _tile(t, 256, 8)

    def body(h_ref, t_ref, l_ref, d_ref, db_ref):
        @pl.when(pl.program_id(0) == 0)
        def _():
            l_ref[...] = jnp.zeros_like(l_ref)

        err = h_ref[...] - t_ref[...]
        d_ref[...] = err * (1.0 / d)
        db_ref[...] = (err * (1.0 / d)).astype(BF16)
        part = 0.5 * jnp.sum(jnp.mean(err * err, axis=-1, keepdims=True), axis=0, keepdims=True)
        lane = lax.broadcasted_iota(jnp.int32, (8, HEAD_DIM), 1)
        row = lax.broadcasted_iota(jnp.int32, (8, HEAD_DIM), 0)
        l_ref[...] += jnp.where((lane == 0) & (row == 0), part, 0.0)

    blk = pl.BlockSpec((tr, d), lambda r: (r, 0))
    return pl.pallas_call(
        body, name="loss_head", grid=(t // tr,), in_specs=[blk, blk],
        out_specs=[pl.BlockSpec((8, HEAD_DIM), lambda r: (0, 0)), blk, blk],
        out_shape=[jax.ShapeDtypeStruct((8, HEAD_DIM), F32), jax.ShapeDtypeStruct((t, d), F32),
                   jax.ShapeDtypeStruct((t, d), BF16)],
        compiler_params=_cparams(("arbitrary",)),
    )(h2, target)


def _adamw(w, g, m, v, *, g_fn=None, name):
    r, c = w.shape
    tr = _tile(r, max(8, (1 << 19) // c // 8 * 8), 8)
    gs = g if isinstance(g, tuple) else (g,)

    def body(w_ref, *refs):
        g_refs, (m_ref, v_ref, go_ref, d_ref, mo_ref, vo_ref) = refs[:len(gs)], refs[len(gs):]
        gr = g_refs[0][...] if g_fn is None else g_fn(*[ref[...] for ref in g_refs])
        mn = ADAM_B1 * m_ref[...] + (1.0 - ADAM_B1) * gr
        vn = ADAM_B2 * v_ref[...] + (1.0 - ADAM_B2) * (gr * gr)
        m_hat = mn / (1.0 - ADAM_B1 ** ADAM_STEP)
        v_hat = vn / (1.0 - ADAM_B2 ** ADAM_STEP)
        go_ref[...] = gr
        d_ref[...] = -ADAM_LR * (m_hat / (jnp.sqrt(v_hat) + ADAM_EPS) + ADAM_WD * w_ref[...])
        mo_ref[...] = mn
        vo_ref[...] = vn

    blk = pl.BlockSpec((tr, c), lambda i: (i, 0))
    gblks = [pl.BlockSpec((tr, gi.shape[1]), lambda i: (i, 0)) for gi in gs]
    return pl.pallas_call(
        body, name=name, grid=(r // tr,), in_specs=[blk] + gblks + [blk, blk], out_specs=[blk] * 4,
        out_shape=[jax.ShapeDtypeStruct((r, c), F32)] * 4,
        compiler_params=_cparams(("parallel",)),
    )(w, *gs, m, v)


class _Layout:
    def __init__(self, d):
        nh = d // HEAD_DIM
        self.nm = N_MEM_HEADS
        self.nf = (nh - self.nm) // 2
        self.ng = nh - self.nm - self.nf
        nf, ng, nm, hd = self.nf, self.ng, self.nm, HEAD_DIM
        self.o_fq, self.o_fk, self.o_fv, self.o_sm = 0, nf, 2 * nf, 3 * nf
        self.o_gq, self.o_gz, self.o_mq = 0, 3 * ng, 4 * ng
        self.cols_a = -(-(3 * nf + 1) // 4) * 4 * hd
        self.cols_b = -(-(4 * ng + nm) // 4) * 4 * hd
        self.cols = self.cols_a + self.cols_b
        sizes = [nf * hd, nf * hd, nf * hd, nf, 3 * ng * hd, ng * hd, ng, ng, nm * hd]
        starts = [sum(sizes[:i]) for i in range(len(sizes))]
        self.ref = list(zip(starts, sizes))
        self.in_cols = sum(sizes)

    def regroup(self, w):
        part = lambda i: w[:, self.ref[i][0]:self.ref[i][0] + self.ref[i][1]]
        a = [part(0), part(1), part(2), part(3), part(6), part(7)]
        b = [part(4), part(5), part(8)]
        pads = [self.cols_a - sum(p.shape[1] for p in a), self.cols_b - sum(p.shape[1] for p in b)]
        fill = [[jnp.zeros((w.shape[0], n), w.dtype)] if n else [] for n in pads]
        return jnp.concatenate(a + fill[0] + b + fill[1], axis=1)

    def ungroup(self, g):
        hd, nf, ng, nm = HEAD_DIM, self.nf, self.ng, self.nm
        sm, b0 = self.o_sm * hd, self.cols_a
        return jnp.concatenate([
            g[:, :3 * nf * hd], g[:, sm:sm + nf], g[:, b0:b0 + 3 * ng * hd],
            g[:, b0 + self.o_gz * hd:b0 + self.o_mq * hd], g[:, sm + nf:sm + nf + ng],
            g[:, sm + nf + ng:sm + nf + 2 * ng], g[:, b0 + self.o_mq * hd:b0 + (self.o_mq + nm) * hd]], axis=1)


def _lane_row(pieces):
    row = jnp.zeros((1, HEAD_DIM), F32)
    for off, a in pieces:
        row = lax.dynamic_update_slice(row, a.astype(F32), (0, off))
    return row


def _local_step(x, mem, target, prefetch, weights, reducer, sp):
    t, d = x.shape
    lay = _Layout(d)
    nf, ng, nm, hd = lay.nf, lay.ng, lay.nm, HEAD_DIM
    nch = t // CHUNK
    tq = _tile(t, 256)
    tk = tq

    u = _norm_fwd(x, 0, sp["norm_mix"], 1, d, BF16, name="norm_mix_fwd")
    prefetch("in_a", u)
    (win_a,) = weights("in_a", u)
    p_a = _mm(u, win_a, name="mm_in_a")
    pa = _lane_row([(nf, sp["gdn_a_log"])])
    pb = _lane_row([(0, sp["fox_f_bias"]), (nf, sp["gdn_dt_bias"])])
    vals, csum = _small_fwd(p_a, lay.o_sm, pa, pb, nf, ng)

    c_t = csum[:, :nf].T
    hp = _fox_heads(nf, 3)
    cr = c_t.reshape(nf, t // tk, 1, tk)
    cc = jnp.stack([jnp.pad(csum[:, g * hp:(g + 1) * hp], ((0, 0), (0, hd - hp))) for g in range(nf // hp)])
    fq = _norm_fwd(p_a, lay.o_fq, sp["fox_q_norm"], nf, hd, BF16, name="fox_qnorm_fwd")
    fk = _norm_fwd(p_a, lay.o_fk, sp["fox_k_norm"], nf, hd, BF16, name="fox_knorm_fwd")
    fv = p_a[:, lay.o_fv * hd:(lay.o_fv + nf) * hd].astype(BF16)
    o_fox, lse, mix = _fox_fwd(fq, fk, fv, cc, cr, nf, tq, tk, d)

    prefetch("in_b", lse)
    (win_b,) = weights("in_b", lse)
    prefetch("mixer", win_b)
    p = _mm(u, win_b, name="mm_in_b")
    wmkv, conv_taps = weights("mixer", p)
    sp = dict(sp, gdn_conv=conv_taps)
    qkv = _conv_fwd(p, lay.o_gq, sp["gdn_conv"], ng)
    grow = vals[:, nf:nf + ng].T.reshape(ng, nch, 1, CHUNK)
    o_g, states = _gdn_fwd(qkv, vals, grow, nf, ng)
    mix = _norm_fwd(o_g, 0, sp["gdn_out_norm"], ng, hd, BF16, z=p, zoff=lay.o_gz, into=mix, into_off=nf,
                    name="gdn_out_fwd")
    prefetch("out", mix)

    mem_n = _norm_fwd(mem, 0, sp["mem_norm"], 1, d, BF16, name="mem_norm_fwd")
    mkv = _mm(mem_n, wmkv, name="mm_memkv")
    tq_mem = _tile(t, 1024)
    mix = _mem_fwd(p, lay.o_mq, mkv, sp["mem_q_norm"], sp["mem_k_norm"], tq_mem, mix, nf + ng)
    prefetch("gate_up", mix)
    (wout,) = weights("out", mix)
    h1 = _mm(mix, wout, res=x, name="mm_out")
    n2 = _norm_fwd(h1, 0, sp["norm_ffn"], 1, d, BF16, name="norm_ffn_fwd")
    (wgu,) = weights("gate_up", n2)
    wgu4 = wgu.reshape(4, d, -1)
    gu, act = _ffn_up(n2, wgu4)
    prefetch("down", act)
    (wd,) = weights("down", act)
    h2 = _mm(act, wd, res=h1, name="mm_down")
    loss_blk, dh2, dh2_b = _loss_head(h2, target)

    g = {}
    token = reducer.pair("w_down", _mm(act, dh2_b, ta=True, out_dtype=BF16, name="mm_dw_down"))
    dgu = _ffn_dact(dh2_b, wd, gu, token)
    dw_gate_up = _mm(n2, dgu, ta=True, stack="out", out_dtype=BF16, name="mm_dw_gate_up").reshape(wgu.shape)
    token = reducer.pair("w_gate_up", dw_gate_up)
    dn2 = _mm(dgu, wgu4, tb=True, stack="sum", after=token, name="mm_dn2")
    token = reducer.ship("ffn", ["w_down", "w_gate_up"], dn2)
    dh1, g["norm_ffn"] = _norm_bwd(h1, 0, sp["norm_ffn"] + token[0, 0], dn2, 0, 1, d, res=dh2,
                                   name="norm_ffn_bwd")
    token = reducer.pair("w_out", _mm(mix, dh1, ta=True, out_dtype=BF16, name="mm_dw_out"))
    dmix = _mm(dh1, wout, tb=True, after=token, name="mm_dmix")

    dmq, dmk, dmv, g["mem_q_norm"], g["mem_k_norm"] = _mem_bwd(
        p, lay.o_mq, mkv, sp["mem_q_norm"], sp["mem_k_norm"], dmix, nf + ng, tq_mem)
    dmkv = jnp.concatenate([dmk, dmv], axis=1)
    token = reducer.pair("w_mem_kv", _mm(mem_n, dmkv, ta=True, out_dtype=BF16, name="mm_dw_memkv"))
    dmem_n = _mm(dmkv, wmkv, tb=True, after=token, name="mm_dmem")
    token = reducer.ship("mix", ["w_out", "w_mem_kv"], dmem_n)
    _, g["mem_norm"] = _norm_bwd(mem, 0, sp["mem_norm"], dmem_n, 0, 1, d, name="mem_norm_bwd")

    do_g, dgz, g["gdn_out_norm"] = _norm_bwd(o_g, 0, sp["gdn_out_norm"] + token[0, 0], dmix, nf, ng, hd, z=p,
                                             zoff=lay.o_gz, name="gdn_out_bwd")
    dq, dk, dv, dvals, dgr = _gdn_bwd(qkv, vals, grow, states, do_g, nf, ng)
    dgqkv, g["gdn_conv"] = _conv_bwd(p, lay.o_gq, sp["gdn_conv"], (dq, dk, dv), ng)

    dfq_n, dfk_n, dfv, dcc, dcr = _fox_bwd(fq, fk, fv, cc, cr, o_fox, lse, dmix, nf, tq, tk)
    dfq, g["fox_q_norm"] = _norm_bwd(p_a, lay.o_fq, sp["fox_q_norm"], dfq_n, 0, nf, hd, name="fox_qnorm_bwd")
    dfk, g["fox_k_norm"] = _norm_bwd(p_a, lay.o_fk, sp["fox_k_norm"], dfk_n, 0, nf, hd, name="fox_knorm_bwd")
    dc = dcc.reshape(nf, t).T + jnp.concatenate([dcr[g, :, :hp] for g in range(nf // hp)], axis=1)

    dvals = dvals + jnp.pad(dgr.reshape(ng, t).T, ((0, 0), (nf, hd - nf - ng)))
    dcsum = jnp.pad(dc, ((0, 0), (0, hd - nf)))
    dsm, dpa, dpb = _small_bwd(p_a, lay.o_sm, pa, pb, dvals, dcsum, nf, ng)
    g["fox_f_bias"] = dpb[:, :nf]
    g["gdn_dt_bias"] = dpb[:, nf:nf + ng]
    g["gdn_a_log"] = dpa[:, nf:nf + ng]

    zeros = lambda n: jnp.zeros((t, n), F32)
    dp_a = jnp.concatenate([dfq, dfk, dfv, dsm, zeros(lay.cols_a - (lay.o_sm + 1) * hd)], axis=1).astype(BF16)
    assert lay.o_gq == 0 and lay.o_gz == 3 * ng and lay.o_mq == lay.o_gz + ng
    rest = jnp.concatenate([dgz.astype(BF16), dmq.astype(BF16),
                            jnp.zeros((t, lay.cols_b - (lay.o_mq + nm) * hd), BF16)], axis=1)
    dp_b = lax.dynamic_update_slice(dgqkv, rest, (0, lay.o_gz * hd))
    token = reducer.pair("w_in_a", _mm(u, dp_a, ta=True, out_dtype=BF16, name="mm_dw_in_a"))
    token = reducer.pair("w_in_b", _mm(u, dp_b, ta=True, out_dtype=BF16, after=token, name="mm_dw_in_b"))
    du = _mm(dp_a, win_a, tb=True, after=token, name="mm_du_a")
    token = reducer.ship("in", ["w_in_a", "w_in_b"], du)
    du = _mm(dp_b, win_b, tb=True, res=du, after=token, name="mm_du_b")
    dx, g["norm_mix"] = _norm_bwd(x, 0, sp["norm_mix"], du, 0, 1, d, res=dh1, name="norm_mix_bwd")
    return loss_blk, dx, g


ANY = pl.BlockSpec(memory_space=pl.ANY)


def _me():
    x, y, c = lax.axis_index("x"), lax.axis_index("y"), lax.axis_index("c")
    chips = [(1 - x, y), (x, 1 - y), (1 - x, 1 - y)]
    return x, y, c, chips


def _slot(axis, k):
    return k if axis == 0 else 2 * (k % 2) + k // 2


def _slab(ref, axis, rows, cols, k, h):
    half = rows // 2
    return ref.at[pl.ds(_slot(axis, k) * rows + h * half, half), :]


def _remote(src, dst, send_sem, recv_sem, dev):
    return pltpu.make_async_remote_copy(src_ref=src, dst_ref=dst, send_sem=send_sem, recv_sem=recv_sem,
                                        device_id=dev, device_id_type=MESH)


HBM = pl.BlockSpec(memory_space=pltpu.HBM)
SEM = pl.BlockSpec(memory_space=pltpu.SEMAPHORE)
SPLIT = pltpu.CompilerParams(has_side_effects=pltpu.SideEffectType.DATAFLOW_SIDE_EFFECTING)
TOKEN = jax.ShapeDtypeStruct((8, HEAD_DIM), F32)


def _in_hbm(v):
    return pltpu.with_memory_space_constraint(v, pltpu.HBM)


def _cast_place(shard, axis, name, col_fn=None, out_cols=None, after=None):
    r, c = shard.shape
    oc = out_cols or c
    tr = _tile(r, 512 if col_fn is None else 64, 16)
    tc = _tile(c, 2048) if col_fn is None else c
    otc = tc if col_fn is None else oc
    nb = r // tr
    chip = 2 * lax.axis_index("x") + lax.axis_index("y")
    slot = jnp.reshape(_slot(axis, chip), (1,)).astype(jnp.int32)

    def body(slot_ref, x_ref, *rest):
        x = x_ref[...]
        rest[-1][...] = (x if col_fn is None else col_fn(x)).astype(BF16)

    extra = [] if after is None else [after]
    return pl.pallas_call(
        body, name=name,
        grid_spec=pltpu.PrefetchScalarGridSpec(
            num_scalar_prefetch=1, grid=(nb, c // tc),
            in_specs=[pl.BlockSpec((tr, tc), lambda i, l, s: (i, l))] + [ANY] * len(extra),
            out_specs=pl.BlockSpec((tr, otc), lambda i, l, s: (s[0] * nb + i, l))),
        out_shape=jax.ShapeDtypeStruct((4 * r, oc), BF16),
        compiler_params=_cparams(("parallel", "parallel")),
    )(slot, shard, *extra)


def _gather_start(bufs, axes, shapes, groups, name):
    n = len(bufs)

    def body(*refs):
        dst = refs[n:2 * n]
        sems = refs[2 * n:2 * n + 2 * len(groups)]
        token = refs[-1]
        x, y, c, chips = _me()
        k = 2 * x + y
        for gi, ws in enumerate(groups):
            for i, w in enumerate(ws):
                r, cl = shapes[w]
                place = _slab(dst[w], axes[w], r, cl, k, c)
                for j, (px, py) in enumerate(chips):
                    _remote(place, place, sems[2 * gi].at[3 * i + j], sems[2 * gi + 1].at[3 * i + j],
                            (px, py, c)).start()
        token[...] = jnp.zeros_like(token)

    sem_shapes = [pltpu.SemaphoreType.DMA((3 * len(ws),)) for ws in groups for _ in range(2)]
    outs = pl.pallas_call(
        body, name=name, in_specs=[HBM] * n,
        out_specs=[HBM] * n + [SEM] * len(sem_shapes) + [pl.BlockSpec(memory_space=pltpu.VMEM)],
        out_shape=[pltpu.HBM(b.shape, b.dtype) for b in bufs] + sem_shapes + [TOKEN],
        input_output_aliases={w: w for w in range(n)}, compiler_params=SPLIT,
    )(*[_in_hbm(b) for b in bufs])
    sems = outs[n:-1]
    return outs[:n], [(sems[2 * g], sems[2 * g + 1]) for g in range(len(groups))], outs[-1]


def _gather_wait(bufs, axes, shapes, sems, after, name):
    n = len(bufs)

    def body(*refs):
        send_sems, recv_sems = refs[n], refs[n + 1]
        dst = refs[n + 3:]
        x, y, c, chips = _me()
        k = 2 * x + y
        for i in range(n):
            r, cl = shapes[i]
            for j, (px, py) in enumerate(chips):
                got = _slab(dst[i], axes[i], r, cl, 2 * px + py, c)
                _remote(got, got, send_sems.at[3 * i + j], recv_sems.at[3 * i + j], (px, py, c)).wait_recv()
        for i in range(n):
            r, cl = shapes[i]
            mine = _slab(dst[i], axes[i], r, cl, k, c)
            for j, (px, py) in enumerate(chips):
                _remote(mine, mine, send_sems.at[3 * i + j], recv_sems.at[3 * i + j], (px, py, c)).wait_send()

    return pl.pallas_call(
        body, name=name, in_specs=[HBM] * n + [SEM, SEM, ANY], out_specs=[HBM] * n,
        out_shape=[pltpu.HBM(b.shape, b.dtype) for b in bufs],
        input_output_aliases={i: i for i in range(n)}, compiler_params=SPLIT,
    )(*bufs, sems[0], sems[1], after)


def _split_start(name, arrays, geometry, count):
    n = len(arrays)

    def body(*refs):
        send, recv, token = refs[2 * n:]
        for i, (src, dst, _, dev) in enumerate(geometry(refs[n:2 * n])):
            _remote(src, dst, send.at[i], recv.at[i], dev).start()
        token[...] = jnp.zeros_like(token)

    sem = pltpu.SemaphoreType.DMA((count,))
    outs = pl.pallas_call(
        body, name=name, in_specs=[HBM] * n,
        out_specs=[HBM] * n + [SEM, SEM, pl.BlockSpec(memory_space=pltpu.VMEM)],
        out_shape=[pltpu.HBM(v.shape, v.dtype) for v in arrays] + [sem, sem, TOKEN],
        input_output_aliases={i: i for i in range(n)}, compiler_params=SPLIT,
    )(*[_in_hbm(v) for v in arrays])
    return list(outs[:n]), (outs[n], outs[n + 1]), outs[-1]


def _split_wait(name, arrays, sems, after, geometry):
    n = len(arrays)

    def body(*refs):
        send, recv = refs[n], refs[n + 1]
        copies = geometry(refs[n + 3:])
        for i, (_, _, land, dev) in enumerate(copies):
            _remote(land, land, send.at[i], recv.at[i], dev).wait_recv()
        for i, (src, _, _, dev) in enumerate(copies):
            _remote(src, src, send.at[i], recv.at[i], dev).wait_send()

    return list(pl.pallas_call(
        body, name=name, in_specs=[HBM] * n + [SEM, SEM, ANY], out_specs=[HBM] * n,
        out_shape=[pltpu.HBM(v.shape, v.dtype) for v in arrays],
        input_output_aliases={i: i for i in range(n)}, compiler_params=SPLIT,
    )(*arrays, sems[0], sems[1], after))


def _forward_geometry(axes, shapes):
    def geometry(bufs):
        x, y, c, chips = _me()
        out = []
        for i, buf in enumerate(bufs):
            r, cl = shapes[i]
            for px, py in chips:
                got = _slab(buf, axes[i], r, cl, 2 * px + py, c)
                out.append((got, got, _slab(buf, axes[i], r, cl, 2 * px + py, 1 - c), (x, y, 1 - c)))
        return out
    return geometry


def _pair_geometry(axes, shapes):
    def geometry(refs):
        n = len(refs) // 2
        x, y, c, _ = _me()
        out = []
        for w in range(n):
            r, cl = shapes[w]
            for j in range(4):
                land = refs[n + w].at[j]
                out.append((_slab(refs[w], axes[w], r, cl, j, 1 - c), land, land, (x, y, 1 - c)))
        return out
    return geometry


def _after_all(name, token, *arrays):
    def body(*refs):
        refs[-1][...] = jnp.zeros_like(refs[-1])

    return pl.pallas_call(
        body, name=name, in_specs=[ANY] * (1 + len(arrays)), out_specs=pl.BlockSpec(memory_space=pltpu.VMEM),
        out_shape=TOKEN,
    )(token, *arrays)


def _swap_geometry(bufs):
    x, y, c, _ = _me()
    return [(b.at[c], b.at[c], b.at[1 - c], (x, y, 1 - c)) for b in bufs]


def _chip_start(parts, tag):
    n = len(parts)

    def body(*refs):
        src, land = refs[2 * n:3 * n], refs[3 * n:4 * n]
        send_sems, recv_sems, token = refs[4 * n:]
        x, y, c, chips = _me()
        k = 2 * x + y
        for w in range(n):
            for j, (px, py) in enumerate(chips):
                _remote(src[w].at[2 * px + py], land[w].at[k], send_sems.at[3 * w + j], recv_sems.at[3 * w + j],
                        (px, py, c)).start()
        token[...] = jnp.zeros_like(token)

    lands = [lax.empty(p.shape, p.dtype) for p in parts]
    sem = pltpu.SemaphoreType.DMA((3 * n,))
    outs = pl.pallas_call(
        body, name="reduce_ici_start_" + tag, in_specs=[HBM] * (2 * n),
        out_specs=[HBM] * (2 * n) + [SEM, SEM, pl.BlockSpec(memory_space=pltpu.VMEM)],
        out_shape=[pltpu.HBM(p.shape, p.dtype) for p in parts + lands] + [sem, sem, TOKEN],
        input_output_aliases={i: i for i in range(2 * n)}, compiler_params=SPLIT,
    )(*[_in_hbm(v) for v in parts + lands])
    return outs[:n], outs[n:2 * n], outs[2 * n], outs[2 * n + 1], outs[-1]


def _chip_wait(parts, lands, send_sems, recv_sems, after, tag):
    n = len(parts)

    def body(*refs):
        send, recv = refs[2 * n], refs[2 * n + 1]
        src, land = refs[2 * n + 3:3 * n + 3], refs[3 * n + 3:]
        x, y, c, chips = _me()
        for w in range(n):
            for j, (px, py) in enumerate(chips):
                got = land[w].at[2 * px + py]
                _remote(got, got, send.at[3 * w + j], recv.at[3 * w + j], (px, py, c)).wait_recv()
        for w in range(n):
            for j, (px, py) in enumerate(chips):
                sent = src[w].at[2 * px + py]
                _remote(sent, sent, send.at[3 * w + j], recv.at[3 * w + j], (px, py, c)).wait_send()

    outs = pl.pallas_call(
        body, name="reduce_ici_wait_" + tag, in_specs=[HBM] * (2 * n) + [SEM, SEM, ANY], out_specs=[HBM] * (2 * n),
        out_shape=[pltpu.HBM(p.shape, p.dtype) for p in parts + lands],
        input_output_aliases={i: i for i in range(2 * n)}, compiler_params=SPLIT,
    )(*parts, *lands, send_sems, recv_sems, after)
    chip = 2 * lax.axis_index("x") + lax.axis_index("y")
    return [lax.dynamic_update_slice(s, lax.dynamic_index_in_dim(p, chip, 0, keepdims=True), (chip, 0, 0))
            for p, s in zip(outs[:n], outs[n:])]


def _half_swap(halves, tag):
    n = len(halves)
    core = lax.axis_index("c")
    bufs = [lax.dynamic_update_slice(lax.empty((2,) + h.shape, h.dtype), h[None], (core, 0, 0)) for h in halves]

    def body(*refs):
        dst = refs[n:2 * n]
        send_sems, recv_sems = refs[2 * n:]
        x, y, c, _ = _me()
        sibling = (x, y, 1 - c)
        cps = []
        for w in range(n):
            cp = _remote(dst[w].at[c], dst[w].at[c], send_sems.at[w], recv_sems.at[w], sibling)
            cp.start()
            cps.append(cp)
        for w in range(n):
            other = dst[w].at[1 - c]
            _remote(other, other, send_sems.at[w], recv_sems.at[w], sibling).wait_recv()
        for cp in cps:
            cp.wait_send()

    outs = pl.pallas_call(
        body, name="reduce_half_swap_" + tag, in_specs=[ANY] * n, out_specs=[ANY] * n,
        out_shape=[jax.ShapeDtypeStruct(b.shape, b.dtype) for b in bufs],
        input_output_aliases={w: w for w in range(n)},
        scratch_shapes=[pltpu.SemaphoreType.DMA((n,)), pltpu.SemaphoreType.DMA((n,))],
    )(*bufs)
    return [o.reshape(2 * o.shape[1], o.shape[2]) for o in outs]


def _add_parts(full, axis, rows, sib, name):
    _, r, c = sib.shape
    tr, tc = _tile(r, 1024, 16), _tile(c, 2048)
    nb = r // tr
    core = jnp.reshape(lax.axis_index("c"), (1,)).astype(jnp.int32)

    def body(c_ref, a_ref, b_ref, o_ref):
        o_ref[0] = (a_ref[...].astype(F32) + b_ref[0].astype(F32)).astype(BF16)

    blk = pl.BlockSpec((1, tr, tc), lambda j, i, l, cr: (j, i, l))
    return pl.pallas_call(
        body, name=name,
        grid_spec=pltpu.PrefetchScalarGridSpec(
            num_scalar_prefetch=1, grid=(4, nb, c // tc),
            in_specs=[pl.BlockSpec((tr, tc), lambda j, i, l, cr: ((_slot(axis, j) * 2 + cr[0]) * nb + i, l)), blk],
            out_specs=blk),
        out_shape=jax.ShapeDtypeStruct(sib.shape, BF16),
        compiler_params=_cparams(("parallel", "parallel", "parallel")),
    )(core, full, sib)


def _sum_slots(a, name):
    _, r, c = a.shape
    tr, tc = _tile(r, 512, 8), _tile(c, 2048)

    def body(a_ref, o_ref):
        v = a_ref[...].astype(F32)
        o_ref[...] = ((v[0] + v[1]) + v[2]) + v[3]

    return pl.pallas_call(
        body, name=name, grid=(r // tr, c // tc),
        in_specs=[pl.BlockSpec((4, tr, tc), lambda i, l: (0, i, l))],
        out_specs=pl.BlockSpec((tr, tc), lambda i, l: (i, l)),
        out_shape=jax.ShapeDtypeStruct((r, c), F32),
        compiler_params=_cparams(("parallel", "parallel")),
    )(a)


class _Reducer:
    def __init__(self, spec):
        self.spec = spec
        self.paired = {}
        self.pending = []

    def pair(self, name, full):
        ax, shp = self.spec[name]
        land = lax.empty((4, shp[0] // 2, shp[1]), full.dtype)
        arrays, sems, token = _split_start("reduce_pair_start_" + name, [full, land], _pair_geometry([ax], [shp]), 4)
        self.paired[name] = (arrays, sems)
        return token

    def ship(self, tag, names, after):
        parts = []
        for n in names:
            ax, shp = self.spec[n]
            arrays, sems = self.paired.pop(n)
            full, sib = _split_wait("reduce_pair_wait_" + n, arrays, sems, after, _pair_geometry([ax], [shp]))
            parts.append(_add_parts(full, ax, shp[0], sib, name=f"reduce_add_{n}"))
        parts, lands, send, recv, token = _chip_start(parts, tag)
        self.pending.append((tag, names, parts, lands, send, recv))
        return token

    def finish(self, after, tags):
        out = {}
        for tag, names, parts, lands, send, recv in [p for p in self.pending if p[0] in tags]:
            slots = _chip_wait(parts, lands, send, recv, after, tag)
            halves = [_sum_slots(s, name=f"reduce_sum_{n}") for n, s in zip(names, slots)]
            out.update(zip(names, _half_swap(halves, tag)))
        return out

    def finish_start(self, after, tag):
        (_, names, parts, lands, send, recv), = [p for p in self.pending if p[0] == tag]
        slots = _chip_wait(parts, lands, send, recv, after, tag)
        halves = [_sum_slots(s, name=f"reduce_sum_{n}") for n, s in zip(names, slots)]
        core = lax.axis_index("c")
        bufs = [lax.dynamic_update_slice(lax.empty((2,) + h.shape, h.dtype), h[None], (core, 0, 0)) for h in halves]
        bufs, sems, _ = _split_start("reduce_half_swap_start_" + tag, bufs, _swap_geometry, len(bufs))
        return tag, names, bufs, sems

    def swap_wait(self, started, after):
        tag, names, bufs, sems = started
        outs = _split_wait("reduce_half_swap_wait_" + tag, bufs, sems, after, _swap_geometry)
        return dict(zip(names, [o.reshape(2 * o.shape[1], o.shape[2]) for o in outs]))


def _allreduce_small(pack, after):
    rows = pack.shape[0]

    def body(p_ref, _, o_ref, slots, send_sems, recv_sems):
        x, y, c, _ = _me()
        me = 4 * x + 2 * y + c
        slots[me] = p_ref[...]
        cps = []
        for r in range(1, 8):
            peer = (x ^ (r >> 2), y ^ ((r >> 1) & 1), c ^ (r & 1))
            cp = _remote(p_ref, slots.at[me], send_sems.at[r - 1], recv_sems.at[r - 1], peer)
            cp.start()
            cps.append(cp)
        for r in range(1, 8):
            frm = me ^ r
            _remote(slots.at[frm], slots.at[frm], send_sems.at[r - 1], recv_sems.at[r - 1], (x, y, c)).wait_recv()
        for cp in cps:
            cp.wait_send()
        acc = slots[0]
        for s in range(1, 8):
            acc = acc + slots[s]
        o_ref[...] = acc

    vm = pl.BlockSpec(memory_space=pltpu.VMEM)
    return pl.pallas_call(
        body, name="allreduce_small", in_specs=[vm, ANY], out_specs=vm,
        out_shape=jax.ShapeDtypeStruct(pack.shape, F32),
        scratch_shapes=[pltpu.VMEM((8, rows, HEAD_DIM), F32), pltpu.SemaphoreType.DMA((7,)),
                        pltpu.SemaphoreType.DMA((7,))],
    )(pack, after)


_ROWS = ["norm_mix", "norm_ffn", "mem_norm", "fox_q_norm", "fox_k_norm", "gdn_out_norm", "mem_q_norm",
         "mem_k_norm", "fox_f_bias", "gdn_a_log", "gdn_dt_bias"]


def _pack_rows(vals):
    out = []
    for name in _ROWS:
        v = vals[name].reshape(-1)
        n = -(-v.shape[0] // HEAD_DIM) * HEAD_DIM
        out.append(jnp.pad(v, (0, n - v.shape[0])).reshape(-1, HEAD_DIM))
    return jnp.concatenate(out, axis=0)


def _unpack_rows(pack, like):
    out, r = {}, 0
    for name in _ROWS:
        n = like[name].shape[-1]
        nr = -(-n // HEAD_DIM)
        out[name] = pack[r:r + nr].reshape(1, -1)[:, :n]
        r += nr
    return out, r


def kernel(x, mem, norm_mix, w_in, fox_f_bias, fox_q_norm, fox_k_norm, gdn_conv, gdn_a_log, gdn_dt_bias, gdn_out_norm, mem_norm, w_mem_kv, mem_q_norm, mem_k_norm, w_out, norm_ffn, w_gate_up, w_down, loss_target, m_norm_mix, m_w_in, m_fox_f_bias, m_fox_q_norm, m_fox_k_norm, m_gdn_conv, m_gdn_a_log, m_gdn_dt_bias, m_gdn_out_norm, m_mem_norm, m_w_mem_kv, m_mem_q_norm, m_mem_k_norm, m_w_out, m_norm_ffn, m_w_gate_up, m_w_down, v_norm_mix, v_w_in, v_fox_f_bias, v_fox_q_norm, v_fox_k_norm, v_gdn_conv, v_gdn_a_log, v_gdn_dt_bias, v_gdn_out_norm, v_mem_norm, v_w_mem_kv, v_mem_q_norm, v_mem_k_norm, v_w_out, v_norm_ffn, v_w_gate_up, v_w_down):
    a = dict(locals())
    d = x.shape[-1]
    lay = _Layout(d)
    chip = 2 * lax.axis_index("x") + lax.axis_index("y")
    small = {n: a[n] for n in _ROWS}
    big = ["w_in", "w_mem_kv", "w_out", "w_gate_up", "w_down"]
    axes = [0, 0, 0, 1, 0]

    conv_cols = gdn_conv.shape[-1]
    conv_n = CONV_WIDTH * conv_cols
    conv_rows = -(-conv_n // HEAD_DIM)
    conv_blk = jnp.pad(gdn_conv.reshape(-1), (0, 32 * HEAD_DIM - conv_n)).reshape(32, HEAD_DIM)
    axis_of = dict(zip(big, axes), conv=0, w_in_a=0, w_in_b=0)
    shape_of = {n: a[n].shape[1:] for n in big[1:]}
    shape_of.update(w_in_a=(w_in.shape[1], lay.cols_a), w_in_b=(w_in.shape[1], lay.cols_b), conv=conv_blk.shape)
    placed = {"w_in_a": _cast_place(w_in[0], 0, "cast_w_in_a", lambda v: lay.regroup(v)[:, :lay.cols_a], lay.cols_a),
              "conv": lax.dynamic_update_slice(lax.empty((4 * 32, HEAD_DIM), F32), conv_blk, (chip * 32, 0))}
    grouped = {"in_a": ["w_in_a"], "in_b": ["w_in_b"], "mixer": ["w_mem_kv", "conv"], "out": ["w_out"],
               "gate_up": ["w_gate_up"], "down": ["w_down"]}
    inflight = {}

    def start(tags, name):
        names = [n for t in tags for n in grouped[t]]
        bufs, sems, token = _gather_start([placed[n] for n in names], [axis_of[n] for n in names],
                                          [shape_of[n] for n in names],
                                          [[names.index(n) for n in grouped[t]] for t in tags], name)
        for t, pair in zip(tags, sems):
            inflight[t] = ([bufs[names.index(n)] for n in grouped[t]], pair)
        return token

    first = start(["in_a"], "gather_ici_start_in")
    placed["w_in_b"] = _cast_place(w_in[0], 0, "cast_w_in_b", lambda v: lay.regroup(v)[:, lay.cols_a:], lay.cols_b,
                                   after=first)
    placed.update({n: _cast_place(a[n][0], axis_of[n], "cast_" + n, after=first) for n in big[1:]})
    all_started = start(["in_b", "mixer", "out", "gate_up", "down"], "gather_ici_start_rest")
    all_started = _after_all("moments_ready", all_started, m_w_in[0], v_w_in[0])

    forwarding = {}

    def prefetch(tag, after):
        bufs, sem_pair = inflight.pop(tag)
        ax, shp = [axis_of[n] for n in grouped[tag]], [shape_of[n] for n in grouped[tag]]
        got = _gather_wait(bufs, ax, shp, sem_pair, all_started if tag == "in_a" else after,
                           "gather_ici_wait_" + tag)
        geometry = _forward_geometry(ax, shp)
        got, sems, _ = _split_start("gather_forward_start_" + tag, got, geometry, 3 * len(got))
        forwarding[tag] = (got, sems, geometry)

    def weights(tag, after):
        got, sems, geometry = forwarding.pop(tag)
        got = _split_wait("gather_forward_wait_" + tag, got, sems, after, geometry)
        if tag != "mixer":
            return got
        taps = got[1].reshape(4, 32 * HEAD_DIM)[:, :conv_n].reshape(4, CONV_WIDTH, conv_cols)
        return got[0], jnp.transpose(taps, (1, 0, 2)).reshape(CONV_WIDTH, 4 * conv_cols)

    sp = dict(small)
    reducer = _Reducer({n: (axis_of[n], shape_of[n]) for n in big[1:] + ["w_in_a", "w_in_b"]})
    loss_blk, dx, g = _local_step(x[0], mem[0], loss_target[0], prefetch, weights, reducer, sp)

    gsmall = {n: g[n] for n in _ROWS}
    pack = jnp.concatenate([_pack_rows(gsmall), g["gdn_conv"].reshape(-1, HEAD_DIM), loss_blk], axis=0)
    pack = jnp.pad(pack, ((0, -pack.shape[0] % 8), (0, 0)))
    out = {"grad_x": dx[None]}

    def adamw_shards(reduced):
        if "w_in_a" in reduced:
            reduced = {"w_in": (reduced["w_in_a"], reduced["w_in_b"])}
        for n, gsh in reduced.items():
            join = (lambda ga, gb: lay.ungroup(jnp.concatenate([ga, gb], axis=1))) if n == "w_in" else None
            res = _adamw(a[n][0], gsh, a["m_" + n][0], a["v_" + n][0], g_fn=join, name="adamw_" + n)
            for pre, r in zip(["grad_", "delta_", "new_m_", "new_v_"], res):
                out[pre + n] = r[None]
        return res[0]

    mix_swap = reducer.finish_start(dx, "mix")
    ffn_swap = reducer.finish_start(mix_swap[2][0], "ffn")
    done = adamw_shards(reducer.swap_wait(mix_swap, ffn_swap[2][0]))
    done = adamw_shards(reducer.swap_wait(ffn_swap, done))
    tot = _allreduce_small(pack, done)
    gs, r0 = _unpack_rows(tot, small)
    conv_g = tot[r0:r0 + CONV_WIDTH * 4 * conv_cols // HEAD_DIM].reshape(CONV_WIDTH, 4 * conv_cols)
    gs_conv = lax.dynamic_slice_in_dim(conv_g, chip * conv_cols, conv_cols, axis=1)
    out["loss"] = tot[r0 + CONV_WIDTH * 4 * conv_cols // HEAD_DIM, 0]
    adamw_shards(reducer.finish(tot, ("in",)))
    conv_pad = lambda v: jnp.pad(v.reshape(-1), (0, conv_rows * HEAD_DIM - conv_n)).reshape(conv_rows, HEAD_DIM)
    packs = []
    for src, cv in [(small, gdn_conv), (gs, gs_conv), ({n: a["m_" + n] for n in _ROWS}, m_gdn_conv),
                    ({n: a["v_" + n] for n in _ROWS}, v_gdn_conv)]:
        packs.append(jnp.concatenate([_pack_rows(src), conv_pad(cv)], axis=0))
    res = _adamw(*packs, name="adamw_small")
    for pre, r in zip(["grad_", "delta_", "new_m_", "new_v_"], res):
        vals, r1 = _unpack_rows(r, small)
        for n in _ROWS:
            out[pre + n] = vals[n]
        out[pre + "gdn_conv"] = r[r1:r1 + conv_rows].reshape(-1)[:conv_n].reshape(gdn_conv.shape)
    names = ["norm_mix", "w_in", "fox_f_bias", "fox_q_norm", "fox_k_norm", "gdn_conv", "gdn_a_log", "gdn_dt_bias",
             "gdn_out_norm", "mem_norm", "w_mem_kv", "mem_q_norm", "mem_k_norm", "w_out", "norm_ffn", "w_gate_up",
             "w_down"]
    return (out["loss"], out["grad_x"], *[out[p + n] for p in ["grad_", "delta_", "new_m_", "new_v_"] for n in names])
```

```python
import functools
import math

import jax
import jax.numpy as jnp
from jax import lax
from jax.experimental import pallas as pl
from jax.experimental.pallas import tpu as pltpu

F32, BF16 = jnp.float32, jnp.bfloat16
HEAD_DIM = 128
CHUNK = 64
N_MEM_HEADS = 4
CONV_WIDTH = 4
NORM_EPS = 1e-6
ADAM_LR, ADAM_B1, ADAM_B2, ADAM_EPS, ADAM_WD, ADAM_STEP = 0.001, 0.9, 0.999, 1e-08, 0.01, 10
VMEM_LIMIT = 48 * 1024 * 1024
NEG = -1e30
MESH = pl.DeviceIdType.MESH


def _cparams(sem=None, **kw):
    if sem is not None:
        kw["dimension_semantics"] = sem
    return pltpu.CompilerParams(vmem_limit_bytes=VMEM_LIMIT, **kw)


def _tile(n, target, mult=128):
    best = None
    d = mult
    while d <= min(n, target):
        if n % d == 0:
            best = d
        d += mult
    return best if best is not None else n


def _dot(a, b, dims, hi):
    if a.ndim == 3:
        dn = (((dims[0][0] + 1,), (dims[1][0] + 1,)), ((0,), (0,)))
    else:
        dn = (dims, ((), ()))
    if hi is not None:
        return lax.dot_general(a, b, dn, precision=hi, preferred_element_type=F32)
    return lax.dot_general(a.astype(BF16), b.astype(BF16), dn, preferred_element_type=F32)


def _make_dots(hi, cotangent=None):
    @jax.custom_vjp
    def nn(a, b):
        return _dot(a, b, ((1,), (0,)), hi)

    @jax.custom_vjp
    def nt(a, b):
        return _dot(a, b, ((1,), (1,)), hi)

    @jax.custom_vjp
    def tn(a, b):
        return _dot(a, b, ((0,), (0,)), hi)

    bnn, bnt, btn = cotangent or (nn, nt, tn)
    nn.defvjp(lambda a, b: (nn(a, b), (a, b)), lambda r, g: (bnt(g, r[1]), btn(r[0], g)))
    nt.defvjp(lambda a, b: (nt(a, b), (a, b)), lambda r, g: (bnn(g, r[1]), btn(g, r[0])))
    tn.defvjp(lambda a, b: (tn(a, b), (a, b)), lambda r, g: (bnt(r[1], g), bnn(r[0], g)))
    return nn, nt, tn


_nn, _nt, _tn = _make_dots(None)
_nn_hi, _nt_hi, _tn_hi = _make_dots(lax.Precision.HIGHEST)


def _sigmoid(x):
    return jax.nn.sigmoid(x)


@jax.custom_vjp
def _softplus(x):
    return jnp.maximum(x, 0.0) + jnp.log(1.0 + jnp.exp(-jnp.abs(x)))


_softplus.defvjp(lambda x: (_softplus(x), x), lambda x, g: (g * _sigmoid(x),))


def _silu(x):
    return x * _sigmoid(x)


def _rms_fn(x, gain, z=None):
    y = x * lax.rsqrt(jnp.mean(x * x, axis=-1, keepdims=True) + NORM_EPS) * gain
    if z is not None:
        y = y * _silu(z)
    return y


def _mm(a, b, *, ta=False, tb=False, out_dtype=F32, res=None, stack=None, after=None, name):
    a2, b2 = a.shape[-2:], b.shape[-2:]
    ns = b.shape[0] if stack else 1
    m = a2[1] if ta else a2[0]
    k = a2[0] if ta else a2[1]
    n = b2[0] if tb else b2[1]
    assert k == (b2[1] if tb else b2[0])
    tm, tn, tk = _mm_tiles(m, n, k, ns if stack == "sum" else 1, a.dtype.itemsize, b.dtype.itemsize,
                           jnp.dtype(out_dtype).itemsize, res is not None)
    nk = k // tk
    single = nk == 1 and stack != "sum"
    dims = ((0 if ta else 1,), (1 if tb else 0,))
    if stack == "sum":
        order = lambda g0, g1, g2, g3: (g2, g0, g1, g3)
        grid = (m // tm, n // tn, ns, nk)
    else:
        order = lambda g0, g1, g2, g3: (g0, g1, g2, g3)
        grid = (ns, m // tm, n // tn, nk)

    def body(*refs):
        if after is not None:
            refs = refs[:2 + (res is not None)] + refs[3 + (res is not None):]
        if single:
            a_ref, b_ref = refs[:2]
            r = lax.dot_general(a_ref[...].astype(BF16), b_ref[...].astype(BF16), (dims, ((), ())),
                                preferred_element_type=F32)
            if res is not None:
                r = r + refs[2][...]
            refs[-1][...] = r.astype(out_dtype)
            return
        if res is None:
            a_ref, b_ref, o_ref, acc = refs
        else:
            a_ref, b_ref, r_ref, o_ref, acc = refs
        s, _, _, kk = order(*[pl.program_id(d) for d in range(4)])
        first = kk == 0
        last = kk == nk - 1
        if stack == "sum":
            first, last = first & (s == 0), last & (s == ns - 1)

        @pl.when(first)
        def _():
            acc[...] = jnp.zeros_like(acc)

        acc[...] += lax.dot_general(a_ref[...].astype(BF16), b_ref[...].astype(BF16), (dims, ((), ())),
                                    preferred_element_type=F32)

        @pl.when(last)
        def _():
            r = acc[...]
            if res is not None:
                r = r + r_ref[...]
            o_ref[...] = r.astype(out_dtype)

    def spec(shape, idx, stacked):
        if stacked:
            return pl.BlockSpec((None,) + shape, lambda *g: (order(*g)[0],) + idx(*order(*g)))
        return pl.BlockSpec(shape, lambda *g: idx(*order(*g)))

    a_spec = (spec((tk, tm), lambda s, i, j, kk: (kk, i), stack == "sum") if ta
              else spec((tm, tk), lambda s, i, j, kk: (i, kk), stack == "sum"))
    b_spec = (spec((tn, tk), lambda s, i, j, kk: (j, kk), bool(stack)) if tb
              else spec((tk, tn), lambda s, i, j, kk: (kk, j), bool(stack)))
    o_spec = spec((tm, tn), lambda s, i, j, kk: (i, j), stack == "out")
    ins, specs = [a, b], [a_spec, b_spec]
    if res is not None:
        ins.append(res)
        specs.append(o_spec)
    if after is not None:
        ins.append(after)
        specs.append(pl.BlockSpec(after.shape, lambda *g: (0,) * after.ndim))
    sem = (("parallel", "parallel", "arbitrary", "arbitrary") if stack == "sum"
           else ("parallel", "parallel", "parallel", "arbitrary"))
    return pl.pallas_call(
        body, name=name, grid=grid, in_specs=specs, out_specs=o_spec,
        out_shape=jax.ShapeDtypeStruct(((ns,) if stack == "out" else ()) + (m, n), out_dtype),
        scratch_shapes=[] if single else [pltpu.VMEM((tm, tn), F32)],
        compiler_params=_cparams(sem),
    )(*ins)


MM_VMEM_BUDGET = 40 * 1024 * 1024
MXU_WIDTH = 256


def _mm_tiles(m, n, k, ns, sa, sb, so, has_res):
    def divs(x, mult, cap):
        out = [d for d in range(mult, min(x, cap) + 1, mult) if x % d == 0]
        return out or [x]

    best = None
    for tk in divs(k, 128, 8192):
        nk = (k // tk) * ns
        for tm in divs(m, 8, 2048):
            for tn in divs(n, 128, 2048):
                vmem = 2 * (tm * tk * sa + tk * tn * sb + tm * tn * so) + (2 * tm * tn * 4 if has_res else 0)
                vmem += tm * tn * 4 if nk > 1 else 0
                if vmem > MM_VMEM_BUDGET:
                    continue
                steps = (m // tm) * (n // tn) * nk
                traffic = (m // tm) * k * n * sb * ns + (n // tn if nk > 1 else 1) * m * k * sa * ns
                cost = steps * 0.4e-6 + traffic / 2.5e12 + (nk * m * n * 8 / 6e12 if nk > 1 else 0)
                cost += 2.0 * m * n * k * ns / 7e14 * (-(-tn // MXU_WIDTH) * MXU_WIDTH / tn)
                if best is None or cost < best[0]:
                    best = (cost, tm, tn, tk)
    return best[1:]


def _norm_fwd(x, xoff, gain, ncol, w, out_dtype, *, z=None, zoff=0, into=None, into_off=0, name):
    t = x.shape[0]
    tr = _tile(t, max(256, (1 << 18) // w), 8)

    def body(*refs):
        x_ref, g_ref, o_ref = refs[0], refs[1], refs[-1]
        y = _rms_fn(x_ref[...], g_ref[...]) if z is None else _rms_fn(x_ref[...], g_ref[...], refs[2][...])
        o_ref[...] = y.astype(out_dtype)

    ins = [x, gain]
    specs = [pl.BlockSpec((tr, w), lambda j, r: (r, xoff + j)), pl.BlockSpec((1, w), lambda j, r: (0, 0))]
    if z is not None:
        ins.append(z)
        specs.append(pl.BlockSpec((tr, w), lambda j, r: (r, zoff + j)))
    aliases = {}
    if into is not None:
        aliases = {len(ins): 0}
        ins.append(into)
        specs.append(pl.BlockSpec(memory_space=pl.ANY))
    return pl.pallas_call(
        body, name=name, grid=(ncol, t // tr), in_specs=specs,
        out_specs=pl.BlockSpec((tr, w), lambda j, r: (r, into_off + j)),
        out_shape=jax.ShapeDtypeStruct((t, ncol * w) if into is None else into.shape, out_dtype),
        input_output_aliases=aliases, compiler_params=_cparams(("parallel", "parallel")),
    )(*ins)


def _norm_bwd(x, xoff, gain, dy, dyoff, ncol, w, *, z=None, zoff=0, res=None, out_dtype=F32, name):
    t = x.shape[0]
    tr = _tile(t, max(256, (1 << 18) // w), 8)

    def body(*refs):
        it = iter(refs)
        x_ref, g_ref = next(it), next(it)
        z_ref = next(it) if z is not None else None
        dy_ref = next(it)
        r_ref = next(it) if res is not None else None
        dx_ref = next(it)
        dz_ref = next(it) if z is not None else None
        dg_ref = next(it)

        @pl.when((pl.program_id(0) == 0) & (pl.program_id(1) == 0))
        def _():
            dg_ref[...] = jnp.zeros_like(dg_ref)

        args = (x_ref[...], g_ref[...]) + ((z_ref[...],) if z is not None else ())
        _, vjp = jax.vjp(_rms_fn, *args)
        grads = vjp(dy_ref[...].astype(F32))
        dx = grads[0]
        if res is not None:
            dx = dx + r_ref[...]
        dx_ref[...] = dx.astype(out_dtype)
        if z is not None:
            dz_ref[...] = grads[2]
        dg_ref[...] += grads[1]

    ins = [x, gain]
    specs = [pl.BlockSpec((tr, w), lambda j, r: (r, xoff + j)), pl.BlockSpec((1, w), lambda j, r: (0, 0))]
    if z is not None:
        ins.append(z)
        specs.append(pl.BlockSpec((tr, w), lambda j, r: (r, zoff + j)))
    ins.append(dy)
    specs.append(pl.BlockSpec((tr, w), lambda j, r: (r, dyoff + j)))
    blk = pl.BlockSpec((tr, w), lambda j, r: (r, j))
    if res is not None:
        ins.append(res)
        specs.append(blk)
    full = jax.ShapeDtypeStruct((t, ncol * w), F32)
    out_shape, out_specs = [jax.ShapeDtypeStruct((t, ncol * w), out_dtype)], [blk]
    if z is not None:
        out_shape.append(full)
        out_specs.append(blk)
    out_shape.append(jax.ShapeDtypeStruct((1, w), F32))
    out_specs.append(pl.BlockSpec((1, w), lambda j, r: (0, 0)))
    return pl.pallas_call(
        body, name=name, grid=(ncol, t // tr), in_specs=specs, out_specs=out_specs, out_shape=out_shape,
        compiler_params=_cparams(("arbitrary", "arbitrary")),
    )(*ins)


def _small_fn(x, pa, pb, nf, ng):
    lane = lax.broadcasted_iota(jnp.int32, x.shape, 1)
    zz = x + pb
    logf = -_softplus(-zz)
    g = -jnp.exp(pa) * _softplus(zz)
    beta = _sigmoid(x)
    return jnp.where(lane < nf, logf, jnp.where(lane < nf + ng, g, beta))


def _tri(n, upper):
    r = lax.broadcasted_iota(jnp.int32, (n, n), 0)
    c = lax.broadcasted_iota(jnp.int32, (n, n), 1)
    return jnp.where((c >= r) if upper else (c <= r), 1.0, 0.0).astype(F32)


def _small_fwd(p, off, pa, pb, nf, ng):
    t = p.shape[0]
    blk = HEAD_DIM
    nb = t // blk

    def body(x_ref, pa_ref, pb_ref, v_ref, c_ref):
        v_ref[...] = _small_fn(x_ref[...], pa_ref[...], pb_ref[...], nf, ng)
        tri = _tri(blk, False)

        carry = jnp.zeros((1, HEAD_DIM), F32)
        for i in range(nb):
            rows = slice(i * blk, (i + 1) * blk)
            c = _nn_hi(tri, v_ref[rows, :]) + carry
            c_ref[rows, :] = c
            carry = c[blk - 1:blk, :]

    row = pl.BlockSpec((1, HEAD_DIM), lambda i: (0, 0))
    out = pl.BlockSpec((t, HEAD_DIM), lambda i: (0, 0))
    return pl.pallas_call(
        body, name="small_fwd", grid=(1,),
        in_specs=[pl.BlockSpec((t, HEAD_DIM), lambda i: (0, off)), row, row], out_specs=[out, out],
        out_shape=[jax.ShapeDtypeStruct((t, HEAD_DIM), F32)] * 2,
        compiler_params=_cparams(("arbitrary",)),
    )(p, pa, pb)


def _small_bwd(p, off, pa, pb, dvals, dcsum, nf, ng):
    t = p.shape[0]
    blk = HEAD_DIM
    nb = t // blk

    def body(x_ref, pa_ref, pb_ref, dv_ref, dc_ref, dx_ref, dpa_ref, dpb_ref, tot_ref):
        tri = _tri(blk, True)

        carry = jnp.zeros((1, HEAD_DIM), F32)
        for i in reversed(range(nb)):
            rows = slice(i * blk, (i + 1) * blk)
            c = _nn_hi(tri, dc_ref[rows, :]) + carry
            tot_ref[rows, :] = c + dv_ref[rows, :]
            carry = c[0:1, :]
        f = functools.partial(_small_fn, nf=nf, ng=ng)
        _, vjp = jax.vjp(f, x_ref[...], pa_ref[...], pb_ref[...])
        dx, dpa, dpb = vjp(tot_ref[...])
        dx_ref[...] = dx
        dpa_ref[...] = dpa
        dpb_ref[...] = dpb

    row = pl.BlockSpec((1, HEAD_DIM), lambda i: (0, 0))
    full = pl.BlockSpec((t, HEAD_DIM), lambda i: (0, 0))
    return pl.pallas_call(
        body, name="small_bwd", grid=(1,),
        in_specs=[pl.BlockSpec((t, HEAD_DIM), lambda i: (0, off)), row, row, full, full],
        out_specs=[full, row, row],
        out_shape=[jax.ShapeDtypeStruct((t, HEAD_DIM), F32), jax.ShapeDtypeStruct((1, HEAD_DIM), F32),
                   jax.ShapeDtypeStruct((1, HEAD_DIM), F32)],
        scratch_shapes=[pltpu.VMEM((t, HEAD_DIM), F32)],
        compiler_params=_cparams(("arbitrary",)),
    )(p, pa, pb, dvals, dcsum)


def _fox_heads(nf, most):
    return next(h for h in range(most, 0, -1) if nf % h == 0)


def _fox_fwd(q, k, v, cc, cr, nf, tq, tk, d_mix):
    t = q.shape[0]
    scale = HEAD_DIM ** -0.5
    assert tq == tk

    vt = jnp.transpose(v.reshape(t // tk, tk, nf, HEAD_DIM), (2, 0, 3, 1))

    hp = _fox_heads(nf, 3)
    lanes = lambda h: slice(h * HEAD_DIM, (h + 1) * HEAD_DIM)

    def body(q_ref, k_ref, vt_ref, cc_ref, cr_ref, o_ref, lse_ref, mix_ref):
        i = pl.program_id(1)
        qs = [q_ref[:, lanes(h)] for h in range(hp)]
        cqs = [cr_ref[h, i] for h in range(hp)]
        ones = jnp.ones((8, tk), BF16)
        diff = lax.broadcasted_iota(jnp.int32, (tk, tq), 0) - lax.broadcasted_iota(jnp.int32, (tk, tq), 1)

        def scores(h, j):
            ks = pl.ds(pl.multiple_of(j * tk, tk), tk)
            return lax.dot_general(k_ref[ks, lanes(h)], qs[h], (((1,), (1,)), ((), ())),
                                   preferred_element_type=F32)

        def tile(h, j, m, l, acc, s, masked):
            ks = pl.ds(pl.multiple_of(j * tk, tk), tk)
            s = s * scale + cqs[h] - cc_ref[0, ks, h:h + 1]
            if masked:
                s = jnp.where(diff <= 0, s, NEG)
            m_new = jnp.maximum(m, jnp.max(s, axis=0, keepdims=True))
            pr = jnp.exp(s - m_new).astype(BF16)
            alpha = jnp.exp(m - m_new)
            l = alpha * l + jnp.dot(ones, pr, preferred_element_type=F32)[:1]
            acc = alpha * acc + jnp.dot(vt_ref[h, j], pr, preferred_element_type=F32)
            return m_new, l, acc

        def step(j, carry):
            nxt = [scores(h, j + 1) for h in range(hp)]
            return tuple(tile(h, j, *carry[h], False) + (nxt[h],) for h in range(hp))

        init = tuple((jnp.full((1, tq), NEG, F32), jnp.zeros((1, tq), F32), jnp.zeros((HEAD_DIM, tq), F32),
                      scores(h, 0)) for h in range(hp))
        carry = lax.fori_loop(0, i, step, init)
        for h in range(hp):
            m, l, acc = tile(h, i, *carry[h], True)
            o = jnp.transpose(acc / l)
            o_ref[:, lanes(h)] = o
            mix_ref[:, lanes(h)] = o.astype(BF16)
            lse_ref[h, 0] = m + jnp.log(l)

    w = hp * HEAD_DIM
    qblk = pl.BlockSpec((tq, w), lambda h, i: (i, h))
    return pl.pallas_call(
        body, name="fox_fwd", grid=(nf // hp, t // tq),
        in_specs=[qblk, pl.BlockSpec((t, w), lambda h, i: (0, h)),
                  pl.BlockSpec((hp, t // tk, HEAD_DIM, tk), lambda h, i: (h, 0, 0, 0)),
                  pl.BlockSpec((1, t, HEAD_DIM), lambda h, i: (h, 0, 0)),
                  pl.BlockSpec((hp, t // tk, 1, tk), lambda h, i: (h, 0, 0, 0))],
        out_specs=[qblk, pl.BlockSpec((hp, 1, 1, tq), lambda h, i: (h, i, 0, 0)), qblk],
        out_shape=[jax.ShapeDtypeStruct((t, nf * HEAD_DIM), F32), jax.ShapeDtypeStruct((nf, t // tq, 1, tq), F32),
                   jax.ShapeDtypeStruct((t, d_mix), BF16)],
        compiler_params=_cparams(("parallel", "parallel")),
    )(q, k, vt, cc, cr)


def _fox_bwd(q, k, v, cc, cr, o, lse, dmix, nf, tq, tk):
    t = q.shape[0]
    scale = HEAD_DIM ** -0.5
    assert tq == tk
    hp = _fox_heads(nf, 3)
    lanes = lambda h: slice(h * HEAD_DIM, (h + 1) * HEAD_DIM)
    kt = jnp.transpose(k.reshape(t // tk, tk, nf, HEAD_DIM), (2, 0, 3, 1))

    def body(q_ref, k_ref, kt_ref, v_ref, cc_ref, cr_ref, o_ref, lse_ref, do_ref,
             dq_ref, dk_ref, dv_ref, dcq_ref, dck_ref):
        i = pl.program_id(1)

        @pl.when(i == 0)
        def _():
            dk_ref[...] = jnp.zeros_like(dk_ref)
            dv_ref[...] = jnp.zeros_like(dv_ref)
            dck_ref[...] = jnp.zeros_like(dck_ref)

        diff = lax.broadcasted_iota(jnp.int32, (tk, tq), 0) - lax.broadcasted_iota(jnp.int32, (tk, tq), 1)
        lane = lax.broadcasted_iota(jnp.int32, (tk, HEAD_DIM), 1)
        qs = [q_ref[:, lanes(h)] for h in range(hp)]
        dos = [do_ref[:, lanes(h)] for h in range(hp)]
        do_b = [d.astype(BF16) for d in dos]
        cqs = [cr_ref[h, i] for h in range(hp)]
        lses = [lse_ref[h, 0] for h in range(hp)]
        deltas = [jnp.sum(jnp.transpose(dos[h] * o_ref[:, lanes(h)]), axis=0, keepdims=True) for h in range(hp)]

        def products(h, j):
            ks = pl.ds(pl.multiple_of(j * tk, tk), tk)
            nt = (((1,), (1,)), ((), ()))
            return (lax.dot_general(k_ref[ks, lanes(h)], qs[h], nt, preferred_element_type=F32),
                    lax.dot_general(v_ref[ks, lanes(h)], do_b[h], nt, preferred_element_type=F32))

        def tile(h, j, dqt, dcq, s, dp, masked):
            ks = pl.ds(pl.multiple_of(j * tk, tk), tk)
            pr = jnp.exp(s * scale + cqs[h] - cc_ref[0, ks, h:h + 1] - lses[h])
            if masked:
                pr = jnp.where(diff <= 0, pr, 0.0)
            ds = pr * (dp - deltas[h])
            ds_b = ds.astype(BF16)
            dqt = dqt + jnp.dot(kt_ref[h, j], ds_b, preferred_element_type=F32)
            dk_ref[ks, lanes(h)] += jnp.dot(ds_b, qs[h], preferred_element_type=F32) * scale
            dv_ref[ks, lanes(h)] += jnp.dot(pr.astype(BF16), do_b[h], preferred_element_type=F32)
            dck_ref[0, ks, :] -= jnp.where(lane == h, jnp.sum(ds, axis=1, keepdims=True), 0.0)
            return dqt, dcq + jnp.sum(ds, axis=0, keepdims=True)

        def step(j, carry):
            nxt = [products(h, j + 1) for h in range(hp)]
            return tuple(tile(h, j, *carry[h], False) + nxt[h] for h in range(hp))

        init = tuple((jnp.zeros((HEAD_DIM, tq), F32), jnp.zeros((1, tq), F32)) + products(h, 0) for h in range(hp))
        carry = lax.fori_loop(0, i, step, init)
        for h in range(hp):
            dqt, dcq = tile(h, i, *carry[h], True)
            dq_ref[:, lanes(h)] = jnp.transpose(dqt) * scale
            dcq_ref[h, 0] = dcq

    w = hp * HEAD_DIM
    head_all = pl.BlockSpec((t, w), lambda h, i: (0, h))
    qblk = pl.BlockSpec((tq, w), lambda h, i: (i, h))
    colv = pl.BlockSpec((1, t, HEAD_DIM), lambda h, i: (h, 0, 0))
    rows_all = pl.BlockSpec((hp, t // tk, 1, tk), lambda h, i: (h, 0, 0, 0))
    row_blk = pl.BlockSpec((hp, 1, 1, tq), lambda h, i: (h, i, 0, 0))
    wide = jax.ShapeDtypeStruct((t, nf * HEAD_DIM), F32)
    return pl.pallas_call(
        body, name="fox_bwd", grid=(nf // hp, t // tq),
        in_specs=[qblk, head_all, pl.BlockSpec((hp, t // tk, HEAD_DIM, tk), lambda h, i: (h, 0, 0, 0)), head_all,
                  colv, rows_all, qblk, row_blk, qblk],
        out_specs=[qblk, head_all, head_all, row_blk, colv],
        out_shape=[wide, wide, wide, jax.ShapeDtypeStruct((nf, t // tq, 1, tq), F32),
                   jax.ShapeDtypeStruct((nf // hp, t, HEAD_DIM), F32)],
        compiler_params=_cparams(("parallel", "arbitrary")),
    )(q, k, kt, v, cc, cr, o, lse, dmix)


def _mem_fn(mq, mk, mv, gq, gk):
    qn = _rms_fn(mq, gq)
    kn = _rms_fn(mk, gk)
    s = _nt(qn, kn) * (HEAD_DIM ** -0.5)
    e = jnp.exp(s - lax.stop_gradient(jnp.max(s, axis=1, keepdims=True)))
    pr = e / jnp.sum(e, axis=1, keepdims=True)
    return _nn(pr, mv)


def _mem_specs(t, m, tq, qoff):
    qblk = pl.BlockSpec((tq, HEAD_DIM), lambda h, i: (i, qoff + h))
    kblk = pl.BlockSpec((m, HEAD_DIM), lambda h, i: (0, h))
    vblk = pl.BlockSpec((m, HEAD_DIM), lambda h, i: (0, N_MEM_HEADS + h))
    row = pl.BlockSpec((1, HEAD_DIM), lambda h, i: (0, 0))
    return qblk, kblk, vblk, row


def _mem_fwd(p, qoff, mkv, gq, gk, tq, into, into_off):
    t, m = p.shape[0], mkv.shape[0]
    qblk, kblk, vblk, row = _mem_specs(t, m, tq, qoff)

    def body(q_ref, k_ref, v_ref, gq_ref, gk_ref, _, o_ref):
        o_ref[...] = _mem_fn(q_ref[...], k_ref[...], v_ref[...], gq_ref[...], gk_ref[...]).astype(BF16)

    return pl.pallas_call(
        body, name="mem_fwd", grid=(N_MEM_HEADS, t // tq),
        in_specs=[qblk, kblk, vblk, row, row, pl.BlockSpec(memory_space=pl.ANY)],
        out_specs=pl.BlockSpec((tq, HEAD_DIM), lambda h, i: (i, into_off + h)),
        out_shape=jax.ShapeDtypeStruct(into.shape, BF16), input_output_aliases={5: 0},
        compiler_params=_cparams(("parallel", "parallel")),
    )(p, mkv, mkv, gq, gk, into)


def _mem_bwd(p, qoff, mkv, gq, gk, dmix, dooff, tq):
    t, m = p.shape[0], mkv.shape[0]
    qblk, kblk, vblk, row = _mem_specs(t, m, tq, qoff)

    def body(q_ref, k_ref, v_ref, gq_ref, gk_ref, do_ref, dq_ref, dkv_k_ref, dkv_v_ref, dgq_ref, dgk_ref):
        h, i = pl.program_id(0), pl.program_id(1)

        @pl.when((h == 0) & (i == 0))
        def _():
            dgq_ref[...] = jnp.zeros_like(dgq_ref)
            dgk_ref[...] = jnp.zeros_like(dgk_ref)

        @pl.when(i == 0)
        def _():
            dkv_k_ref[...] = jnp.zeros_like(dkv_k_ref)
            dkv_v_ref[...] = jnp.zeros_like(dkv_v_ref)

        _, vjp = jax.vjp(_mem_fn, q_ref[...], k_ref[...], v_ref[...], gq_ref[...], gk_ref[...])
        dq, dk, dv, dgq, dgk = vjp(do_ref[...])
        dq_ref[...] = dq
        dkv_k_ref[...] += dk
        dkv_v_ref[...] += dv
        dgq_ref[...] += dgq
        dgk_ref[...] += dgk

    oblk = pl.BlockSpec((tq, HEAD_DIM), lambda h, i: (i, h))
    kout = pl.BlockSpec((m, HEAD_DIM), lambda h, i: (0, h))
    half = jax.ShapeDtypeStruct((m, N_MEM_HEADS * HEAD_DIM), F32)
    rshape = jax.ShapeDtypeStruct((1, HEAD_DIM), F32)
    return pl.pallas_call(
        body, name="mem_bwd", grid=(N_MEM_HEADS, t // tq),
        in_specs=[qblk, kblk, vblk, row, row, pl.BlockSpec((tq, HEAD_DIM), lambda h, i: (i, dooff + h))],
        out_specs=[oblk, kout, kout, row, row],
        out_shape=[jax.ShapeDtypeStruct((t, N_MEM_HEADS * HEAD_DIM), F32), half, half, rshape, rshape],
        compiler_params=_cparams(("arbitrary", "arbitrary")),
    )(p, mkv, mkv, gq, gk, dmix)


def _shift_down(x, s):
    if s == 0:
        return x
    r = lax.broadcasted_iota(jnp.int32, x.shape, 0)
    return jnp.where(r >= s, pltpu.roll(x, s, 0), 0.0)


def _shift_up(x, s):
    if s == 0:
        return x
    n = x.shape[0]
    r = lax.broadcasted_iota(jnp.int32, x.shape, 0)
    return jnp.where(r < n - s, pltpu.roll(x, n - s, 0), 0.0)


def _conv_fn(x0, x1, x2, x3, w0, w1, w2, w3, kind):
    y = _silu(x0 * w0 + x1 * w1 + x2 * w2 + x3 * w3)
    if kind == 2:
        return y
    y = y * lax.rsqrt(jnp.sum(y * y, axis=-1, keepdims=True) + NORM_EPS)
    return y * (HEAD_DIM ** -0.5) if kind == 0 else y


def _conv_fwd(p, off, conv_w, ng):
    t = p.shape[0]

    def body(x_ref, w_ref, o_ref):
        kind = pl.program_id(0) // ng
        x = x_ref[...]
        xs = [_shift_down(x, CONV_WIDTH - 1 - j) for j in range(CONV_WIDTH)]
        ws = [w_ref[j:j + 1, :] for j in range(CONV_WIDTH)]
        for kd in range(3):
            @pl.when(kind == kd)
            def _(kd=kd):
                o_ref[...] = _conv_fn(*xs, *ws, kd)

    return pl.pallas_call(
        body, name="gdn_conv_fwd", grid=(3 * ng,),
        in_specs=[pl.BlockSpec((t, HEAD_DIM), lambda c: (0, off + c)),
                  pl.BlockSpec((CONV_WIDTH, HEAD_DIM), lambda c: (0, c))],
        out_specs=pl.BlockSpec((t, HEAD_DIM), lambda c: (0, c)),
        out_shape=jax.ShapeDtypeStruct((t, 3 * ng * HEAD_DIM), F32),
        compiler_params=_cparams(("parallel",)),
    )(p, conv_w)


def _conv_bwd(p, off, conv_w, dys, ng):
    t = p.shape[0]

    def body(x_ref, w_ref, dq_ref, dk_ref, dv_ref, dx_ref, dw_ref):
        kind = pl.program_id(0) // ng
        dy_refs = (dq_ref, dk_ref, dv_ref)
        x = x_ref[...]
        xs = [_shift_down(x, CONV_WIDTH - 1 - j) for j in range(CONV_WIDTH)]
        ws = [w_ref[j:j + 1, :] for j in range(CONV_WIDTH)]
        for kd in range(3):
            @pl.when(kind == kd)
            def _(kd=kd):
                _, vjp = jax.vjp(functools.partial(_conv_fn, kind=kd), *xs, *ws)
                g = vjp(dy_refs[kd][...])
                dx = _shift_up(g[0], CONV_WIDTH - 1)
                for j in range(1, CONV_WIDTH):
                    dx = dx + _shift_up(g[j], CONV_WIDTH - 1 - j)
                dx_ref[...] = dx.astype(BF16)
                for j in range(CONV_WIDTH):
                    dw_ref[j:j + 1, :] = g[CONV_WIDTH + j]

    blk = pl.BlockSpec((t, HEAD_DIM), lambda c: (0, off + c))
    head = lambda k: pl.BlockSpec((t, HEAD_DIM), lambda c: (0, jnp.where(c // ng == k, c % ng, 0)))
    wblk = pl.BlockSpec((CONV_WIDTH, HEAD_DIM), lambda c: (0, c))
    return pl.pallas_call(
        body, name="gdn_conv_bwd", grid=(3 * ng,),
        in_specs=[blk, wblk] + [head(k) for k in range(3)],
        out_specs=[blk, wblk],
        out_shape=[jax.ShapeDtypeStruct(p.shape, BF16),
                   jax.ShapeDtypeStruct((CONV_WIDTH, 3 * ng * HEAD_DIM), F32)],
        compiler_params=_cparams(("parallel",)),
    )(p, conv_w, *dys)


def _lower_inverse(lower):
    c = lower.shape[-1]
    r = lax.broadcasted_iota(jnp.int32, (1, c, c), 1)
    e = lax.broadcasted_iota(jnp.int32, (1, c, c), 2)
    hi = lax.Precision.HIGH
    inv = jnp.where(r == e, 1.0, 0.0) - lower
    pw = lower
    for _ in range(int(math.log2(c)) - 1):
        pw = _dot(pw, pw, ((1,), (0,)), hi)
        inv = inv + _dot(inv, pw, ((1,), (0,)), hi)
    return inv


@jax.custom_vjp
def _solve(lower, inv, vb, kbg):
    hi = lax.Precision.HIGH
    return _dot(inv, vb, ((1,), (0,)), hi), _dot(inv, kbg, ((1,), (0,)), hi)


def _solve_fwd(lower, inv, vb, kbg):
    u, w = _solve(lower, inv, vb, kbg)
    return (u, w), (inv, u, w)


def _solve_bwd(res, cts):
    inv, u, w = res
    dvb, dkbg = _tn(inv, cts[0]), _tn(inv, cts[1])
    return -(_nt(dvb, u) + _nt(dkbg, w)), jnp.zeros_like(inv), dvb, dkbg


_solve.defvjp(_solve_fwd, _solve_bwd)


def _wy_fn(q, k, v, gcol, grow, bcol, inv=None):
    b, c, dk = q.shape
    r = lax.broadcasted_iota(jnp.int32, (1, c, c), 1)
    e = lax.broadcasted_iota(jnp.int32, (1, c, c), 2)
    tril, strict = e <= r, e < r
    gc_col = jnp.sum(jnp.where(tril, grow, 0.0), axis=2, keepdims=True)
    gc_row = jnp.sum(jnp.where(r <= e, gcol, 0.0), axis=1, keepdims=True)
    g_last = jnp.sum(gcol, axis=1, keepdims=True)
    decay = jnp.exp(jnp.where(tril, gc_col - gc_row, NEG))
    kb, vb = k * bcol, v * bcol
    lower = jnp.where(strict, _nt(kb, k) * decay, 0.0)
    if inv is None:
        inv = _lower_inverse(lower)
    u, w = _solve(lower, inv, vb, kb * jnp.exp(gc_col))
    attn = jnp.where(tril, _nt(q, k) * decay, 0.0)
    qg = q * jnp.exp(gc_col)
    kdec = k * jnp.exp(g_last - gc_col)
    egl = jnp.broadcast_to(jnp.exp(g_last), (b, 1, dk))
    return u, w, qg, kdec, attn, egl, inv


def _scan_fn(u, w, qg, kdec, attn, egl, state):
    v_new = u - _nn(w, state)
    o = _nn(qg, state) + _nn(attn, v_new)
    return o, state * egl + _tn(kdec, v_new)


GDN_CHUNKS_PER_STEP = 4
GDN_SCAN_CHUNKS = 4


def _gdn_fwd(qkv, vals, grow, nf, ng):
    t = qkv.shape[0]
    nch = t // CHUNK

    cb = GDN_CHUNKS_PER_STEP
    *wy, inv = _gdn_wy(qkv, vals, grow, nf, ng, cb)

    sc = GDN_SCAN_CHUNKS

    def body(u_ref, w_ref, qg_ref, kd_ref, at_ref, eg_ref, o_ref, st_ref, state):
        @pl.when(pl.program_id(0) == 0)
        def _():
            state[...] = jnp.zeros_like(state)

        for c in range(sc):
            rows = slice(c * CHUNK, (c + 1) * CHUNK)
            heads = lambda ref: jnp.stack([ref[rows, h * HEAD_DIM:(h + 1) * HEAD_DIM] for h in range(ng)])
            st_ref[:, c] = state[...]
            o, new = _scan_fn(heads(u_ref), heads(w_ref), heads(qg_ref), heads(kd_ref), at_ref[:, c], eg_ref[:, c],
                              state[...])
            for h in range(ng):
                o_ref[rows, h * HEAD_DIM:(h + 1) * HEAD_DIM] = o[h]
            state[...] = new

    w = ng * HEAD_DIM
    blk = pl.BlockSpec((sc * CHUNK, w), lambda i: (i, 0))
    o, states = pl.pallas_call(
        body, name="gdn_scan_fwd", grid=(nch // sc,),
        in_specs=[blk, blk, blk, blk, pl.BlockSpec((ng, sc, CHUNK, CHUNK), lambda i: (0, i, 0, 0)),
                  pl.BlockSpec((ng, sc, 1, HEAD_DIM), lambda i: (0, i, 0, 0))],
        out_specs=[blk, pl.BlockSpec((ng, sc, HEAD_DIM, HEAD_DIM), lambda i: (0, i, 0, 0))],
        out_shape=[jax.ShapeDtypeStruct((t, w), F32),
                   jax.ShapeDtypeStruct((ng, nch, HEAD_DIM, HEAD_DIM), F32)],
        scratch_shapes=[pltpu.VMEM((ng, HEAD_DIM, HEAD_DIM), F32)],
        compiler_params=_cparams(("arbitrary",)),
    )(*wy)
    return o, (wy, inv, states)


def _wy_batch(q_ref, k_ref, v_ref, vals_ref, gr_ref, nf, ng, cb):
    idx = [(c, h) for c in range(cb) for h in range(ng)]
    rows = lambda c: slice(c * CHUNK, (c + 1) * CHUNK)
    lanes = lambda h: slice(h * HEAD_DIM, (h + 1) * HEAD_DIM)
    wide = lambda ref: jnp.stack([ref[rows(c), lanes(h)] for c, h in idx])
    col = lambda lane0: jnp.stack([vals_ref[rows(c), lane0 + h:lane0 + h + 1] for c, h in idx])
    return idx, (wide(q_ref), wide(k_ref), wide(v_ref), col(nf), jnp.stack([gr_ref[h, c] for c, h in idx]),
                 col(nf + ng))


def _gdn_wy(qkv, vals, grow, nf, ng, cb):
    t = qkv.shape[0]
    nch = t // CHUNK

    def body(q_ref, k_ref, v_ref, vals_ref, gr_ref, u_ref, w_ref, qg_ref, kd_ref, at_ref, eg_ref, inv_ref):
        idx, args = _wy_batch(q_ref, k_ref, v_ref, vals_ref, gr_ref, nf, ng, cb)
        u, w, qg, kd, at, eg, inv = _wy_fn(*args)
        for b, (c, h) in enumerate(idx):
            rows, lanes = slice(c * CHUNK, (c + 1) * CHUNK), slice(h * HEAD_DIM, (h + 1) * HEAD_DIM)
            u_ref[rows, lanes] = u[b]
            w_ref[rows, lanes] = w[b]
            qg_ref[rows, lanes] = qg[b]
            kd_ref[rows, lanes] = kd[b]
            at_ref[h, c] = at[b]
            eg_ref[h, c] = eg[b]
            inv_ref[h, c] = inv[b]

    wd = ng * HEAD_DIM
    blk = lambda o: pl.BlockSpec((cb * CHUNK, wd), lambda i: (i, o))
    col = pl.BlockSpec((cb * CHUNK, HEAD_DIM), lambda i: (i, 0))
    sq = pl.BlockSpec((ng, cb, CHUNK, CHUNK), lambda i: (0, i, 0, 0))
    wide = jax.ShapeDtypeStruct((t, wd), F32)
    sq_shape = jax.ShapeDtypeStruct((ng, nch, CHUNK, CHUNK), F32)
    return pl.pallas_call(
        body, name="gdn_wy_fwd", grid=(nch // cb,),
        in_specs=[blk(0), blk(1), blk(2), col, pl.BlockSpec((ng, cb, 1, CHUNK), lambda i: (0, i, 0, 0))],
        out_specs=[blk(0), blk(0), blk(0), blk(0), sq, pl.BlockSpec((ng, cb, 1, HEAD_DIM), lambda i: (0, i, 0, 0)),
                   sq],
        out_shape=[wide, wide, wide, wide, sq_shape, jax.ShapeDtypeStruct((ng, nch, 1, HEAD_DIM), F32), sq_shape],
        compiler_params=_cparams(("parallel",)),
    )(qkv, qkv, qkv, vals, grow)


def _gdn_bwd(qkv, vals, grow, saved, do, nf, ng):
    t = qkv.shape[0]
    nch = t // CHUNK
    cb = GDN_CHUNKS_PER_STEP // 2
    wy, inv, states = saved
    wd = ng * HEAD_DIM

    def scan_body(u_ref, w_ref, qg_ref, kd_ref, at_ref, eg_ref, st_ref, do_ref,
                  du_ref, dw_ref, dqg_ref, dkd_ref, dat_ref, deg_ref, dstate):
        @pl.when(pl.program_id(0) == 0)
        def _():
            dstate[...] = jnp.zeros_like(dstate)

        for c in reversed(range(sc)):
            rows = slice(c * CHUNK, (c + 1) * CHUNK)
            heads = lambda ref: jnp.stack([ref[rows, h * HEAD_DIM:(h + 1) * HEAD_DIM] for h in range(ng)])
            _, vjp = jax.vjp(_scan_fn, heads(u_ref), heads(w_ref), heads(qg_ref), heads(kd_ref), at_ref[:, c],
                             eg_ref[:, c], st_ref[:, c])
            du, dw, dqg, dkd, dat, deg, dst = vjp((heads(do_ref), dstate[...]))
            for h in range(ng):
                lanes = slice(h * HEAD_DIM, (h + 1) * HEAD_DIM)
                du_ref[rows, lanes] = du[h]
                dw_ref[rows, lanes] = dw[h]
                dqg_ref[rows, lanes] = dqg[h]
                dkd_ref[rows, lanes] = dkd[h]
            dat_ref[:, c] = dat
            deg_ref[:, c] = deg
            dstate[...] = dst

    sc = GDN_SCAN_CHUNKS
    rev = lambda i: nch // sc - 1 - i
    blk = pl.BlockSpec((sc * CHUNK, wd), lambda i: (rev(i), 0))
    atb = pl.BlockSpec((ng, sc, CHUNK, CHUNK), lambda i: (0, rev(i), 0, 0))
    egb = pl.BlockSpec((ng, sc, 1, HEAD_DIM), lambda i: (0, rev(i), 0, 0))
    wide = jax.ShapeDtypeStruct((t, wd), F32)
    at_shape = jax.ShapeDtypeStruct((ng, nch, CHUNK, CHUNK), F32)
    eg_shape = jax.ShapeDtypeStruct((ng, nch, 1, HEAD_DIM), F32)
    dwy = pl.pallas_call(
        scan_body, name="gdn_scan_bwd", grid=(nch // sc,),
        in_specs=[blk, blk, blk, blk, atb, egb,
                  pl.BlockSpec((ng, sc, HEAD_DIM, HEAD_DIM), lambda i: (0, rev(i), 0, 0)), blk],
        out_specs=[blk, blk, blk, blk, atb, egb],
        out_shape=[wide, wide, wide, wide, at_shape, eg_shape],
        scratch_shapes=[pltpu.VMEM((ng, HEAD_DIM, HEAD_DIM), F32)],
        compiler_params=_cparams(("arbitrary",)),
    )(*wy, states, do)

    def wy_body(q_ref, k_ref, v_ref, vals_ref, gr_ref, du_ref, dw_ref, dqg_ref, dkd_ref, dat_ref, deg_ref,
                inv_ref, dq_ref, dk_ref, dv_ref, dvals_ref, dgr_ref):
        idx, args = _wy_batch(q_ref, k_ref, v_ref, vals_ref, gr_ref, nf, ng, cb)
        lane = lax.broadcasted_iota(jnp.int32, (CHUNK, HEAD_DIM), 1)
        kept = jnp.stack([inv_ref[h, c] for c, h in idx])
        rows = lambda c: slice(c * CHUNK, (c + 1) * CHUNK)
        lanes = lambda h: slice(h * HEAD_DIM, (h + 1) * HEAD_DIM)
        wide_ct = lambda ref: jnp.stack([ref[rows(c), lanes(h)] for c, h in idx])
        cts = (wide_ct(du_ref), wide_ct(dw_ref), wide_ct(dqg_ref), wide_ct(dkd_ref),
               jnp.stack([dat_ref[h, c] for c, h in idx]), jnp.stack([deg_ref[h, c] for c, h in idx]))
        _, vjp = jax.vjp(lambda *a: _wy_fn(*a, inv=kept)[:6], *args)
        dq, dk, dv, dgc, dgr, dbc = vjp(cts)
        for b, (c, h) in enumerate(idx):
            dq_ref[rows(c), lanes(h)] = dq[b]
            dk_ref[rows(c), lanes(h)] = dk[b]
            dv_ref[rows(c), lanes(h)] = dv[b]
            dgr_ref[h, c] = dgr[b]
        for c in range(cb):
            acc = jnp.zeros((CHUNK, HEAD_DIM), F32)
            for h in range(ng):
                acc = jnp.where(lane == nf + h, dgc[c * ng + h], acc)
                acc = jnp.where(lane == nf + ng + h, dbc[c * ng + h], acc)
            dvals_ref[rows(c), :] = acc

    cblk = lambda o: pl.BlockSpec((cb * CHUNK, wd), lambda i: (i, o))
    col = pl.BlockSpec((cb * CHUNK, HEAD_DIM), lambda i: (i, 0))
    rowv = pl.BlockSpec((ng, cb, 1, CHUNK), lambda i: (0, i, 0, 0))
    return pl.pallas_call(
        wy_body, name="gdn_wy_bwd", grid=(nch // cb,),
        in_specs=[cblk(0), cblk(1), cblk(2), col, rowv, cblk(0), cblk(0), cblk(0), cblk(0),
                  pl.BlockSpec((ng, cb, CHUNK, CHUNK), lambda i: (0, i, 0, 0)),
                  pl.BlockSpec((ng, cb, 1, HEAD_DIM), lambda i: (0, i, 0, 0)),
                  pl.BlockSpec((ng, cb, CHUNK, CHUNK), lambda i: (0, i, 0, 0))],
        out_specs=[cblk(0), cblk(0), cblk(0), col, rowv],
        out_shape=[wide, wide, wide, jax.ShapeDtypeStruct((t, HEAD_DIM), F32),
                   jax.ShapeDtypeStruct((ng, nch, 1, CHUNK), F32)],
        compiler_params=_cparams(("parallel",)),
    )(qkv, qkv, qkv, vals, grow, *dwy, inv)


def _swiglu_fn(gate, up):
    return _silu(gate) * up


FFN_TN = 512


def _ffn_up(n2, wgu4):
    _, d, w = wgu4.shape
    t = n2.shape[0]
    tn = _tile(w, FFN_TN)
    nb = w // tn

    def body(a_ref, b_ref, gu_ref, act_ref):
        av = a_ref[...]
        gate = jnp.dot(av, b_ref[0], preferred_element_type=F32)
        up = jnp.dot(av, b_ref[1], preferred_element_type=F32)
        gu_ref[0] = gate.astype(BF16)
        gu_ref[1] = up.astype(BF16)
        act_ref[...] = _swiglu_fn(gate, up).astype(BF16)

    return pl.pallas_call(
        body, name="ffn_up", grid=(2, nb),
        in_specs=[pl.BlockSpec((t, d), lambda j, l: (0, 0)), pl.BlockSpec((2, d, tn), lambda j, l: (j, 0, l))],
        out_specs=[pl.BlockSpec((2, t, tn), lambda j, l: (j, 0, l)),
                   pl.BlockSpec((t, tn), lambda j, l: (0, j * nb + l))],
        out_shape=[jax.ShapeDtypeStruct((4, t, w), BF16), jax.ShapeDtypeStruct((t, 2 * w), BF16)],
        compiler_params=_cparams(("parallel", "parallel")),
    )(n2, wgu4)


def _ffn_dact(dh2, wd, gu, after):
    _, t, w = gu.shape
    d = dh2.shape[1]
    tn = _tile(w, FFN_TN)
    nb = w // tn

    def body(a_ref, b_ref, gu_ref, _, o_ref):
        dact = lax.dot_general(a_ref[...], b_ref[...], (((1,), (1,)), ((), ())), preferred_element_type=F32)
        _, vjp = jax.vjp(_swiglu_fn, gu_ref[0].astype(F32), gu_ref[1].astype(F32))
        dg, du = vjp(dact)
        o_ref[0] = dg.astype(BF16)
        o_ref[1] = du.astype(BF16)

    pair = pl.BlockSpec((2, t, tn), lambda j, l: (j, 0, l))
    return pl.pallas_call(
        body, name="ffn_dact", grid=(2, nb),
        in_specs=[pl.BlockSpec((t, d), lambda j, l: (0, 0)), pl.BlockSpec((tn, d), lambda j, l: (j * nb + l, 0)),
                  pair, pl.BlockSpec(after.shape, lambda j, l: (0, 0))],
        out_specs=pair, out_shape=jax.ShapeDtypeStruct(gu.shape, BF16),
        compiler_params=_cparams(("parallel", "parallel")),
    )(dh2, wd, gu, after)


def _loss_head(h2, target):
    t, d = h2.shape
    tr = _tile(t, 256, 8)

    def body(h_ref, t_ref, l_ref, d_ref, db_ref):
        @pl.when(pl.program_id(0) == 0)
        def _():
            l_ref[...] = jnp.zeros_like(l_ref)

        err = h_ref[...] - t_ref[...]
        d_ref[...] = err * (1.0 / d)
        db_ref[...] = (err * (1.0 / d)).astype(BF16)
        part = 0.5 * jnp.sum(jnp.mean(err * err, axis=-1, keepdims=True), axis=0, keepdims=True)
        lane = lax.broadcasted_iota(jnp.int32, (8, HEAD_DIM), 1)
        row = lax.broadcasted_iota(jnp.int32, (8, HEAD_DIM), 0)
        l_ref[...] += jnp.where((lane == 0) & (row == 0), part, 0.0)

    blk = pl.BlockSpec((tr, d), lambda r: (r, 0))
    return pl.pallas_call(
        body, name="loss_head", grid=(t // tr,), in_specs=[blk, blk],
        out_specs=[pl.BlockSpec((8, HEAD_DIM), lambda r: (0, 0)), blk, blk],
        out_shape=[jax.ShapeDtypeStruct((8, HEAD_DIM), F32), jax.ShapeDtypeStruct((t, d), F32),
                   jax.ShapeDtypeStruct((t, d), BF16)],
        compiler_params=_cparams(("arbitrary",)),
    )(h2, target)


def _adamw(w, g, m, v, *, g_fn=None, name):
    r, c = w.shape
    tr = _tile(r, max(8, (1 << 19) // c // 8 * 8), 8)
    gs = g if isinstance(g, tuple) else (g,)

    def body(w_ref, *refs):
        g_refs, (m_ref, v_ref, go_ref, d_ref, mo_ref, vo_ref) = refs[:len(gs)], refs[len(gs):]
        gr = g_refs[0][...] if g_fn is None else g_fn(*[ref[...] for ref in g_refs])
        mn = ADAM_B1 * m_ref[...] + (1.0 - ADAM_B1) * gr
        vn = ADAM_B2 * v_ref[...] + (1.0 - ADAM_B2) * (gr * gr)
        m_hat = mn / (1.0 - ADAM_B1 ** ADAM_STEP)
        v_hat = vn / (1.0 - ADAM_B2 ** ADAM_STEP)
        go_ref[...] = gr
        d_ref[...] = -ADAM_LR * (m_hat / (jnp.sqrt(v_hat) + ADAM_EPS) + ADAM_WD * w_ref[...])
        mo_ref[...] = mn
        vo_ref[...] = vn

    blk = pl.BlockSpec((tr, c), lambda i: (i, 0))
    gblks = [pl.BlockSpec((tr, gi.shape[1]), lambda i: (i, 0)) for gi in gs]
    return pl.pallas_call(
        body, name=name, grid=(r // tr,), in_specs=[blk] + gblks + [blk, blk], out_specs=[blk] * 4,
        out_shape=[jax.ShapeDtypeStruct((r, c), F32)] * 4,
        compiler_params=_cparams(("parallel",)),
    )(w, *gs, m, v)


class _Layout:
    def __init__(self, d):
        nh = d // HEAD_DIM
        self.nm = N_MEM_HEADS
        self.nf = (nh - self.nm) // 2
        self.ng = nh - self.nm - self.nf
        nf, ng, nm, hd = self.nf, self.ng, self.nm, HEAD_DIM
        self.o_fq, self.o_fk, self.o_fv, self.o_sm = 0, nf, 2 * nf, 3 * nf
        self.o_gq, self.o_gz, self.o_mq = 0, 3 * ng, 4 * ng
        self.cols_a = -(-(3 * nf + 1) // 4) * 4 * hd
        self.cols_b = -(-(4 * ng + nm) // 4) * 4 * hd
        self.cols = self.cols_a + self.cols_b
        sizes = [nf * hd, nf * hd, nf * hd, nf, 3 * ng * hd, ng * hd, ng, ng, nm * hd]
        starts = [sum(sizes[:i]) for i in range(len(sizes))]
        self.ref = list(zip(starts, sizes))
        self.in_cols = sum(sizes)

    def regroup(self, w):
        part = lambda i: w[:, self.ref[i][0]:self.ref[i][0] + self.ref[i][1]]
        a = [part(0), part(1), part(2), part(3), part(6), part(7)]
        b = [part(4), part(5), part(8)]
        pads = [self.cols_a - sum(p.shape[1] for p in a), self.cols_b - sum(p.shape[1] for p in b)]
        fill = [[jnp.zeros((w.shape[0], n), w.dtype)] if n else [] for n in pads]
        return jnp.concatenate(a + fill[0] + b + fill[1], axis=1)

    def ungroup(self, g):
        hd, nf, ng, nm = HEAD_DIM, self.nf, self.ng, self.nm
        sm, b0 = self.o_sm * hd, self.cols_a
        return jnp.concatenate([
            g[:, :3 * nf * hd], g[:, sm:sm + nf], g[:, b0:b0 + 3 * ng * hd],
            g[:, b0 + self.o_gz * hd:b0 + self.o_mq * hd], g[:, sm + nf:sm + nf + ng],
            g[:, sm + nf + ng:sm + nf + 2 * ng], g[:, b0 + self.o_mq * hd:b0 + (self.o_mq + nm) * hd]], axis=1)


def _lane_row(pieces):
    row = jnp.zeros((1, HEAD_DIM), F32)
    for off, a in pieces:
        row = lax.dynamic_update_slice(row, a.astype(F32), (0, off))
    return row


def _local_step(x, mem, target, prefetch, weights, reducer, sp):
    t, d = x.shape
    lay = _Layout(d)
    nf, ng, nm, hd = lay.nf, lay.ng, lay.nm, HEAD_DIM
    nch = t // CHUNK
    tq = _tile(t, 256)
    tk = tq

    u = _norm_fwd(x, 0, sp["norm_mix"], 1, d, BF16, name="norm_mix_fwd")
    prefetch("in_a", u)
    (win_a,) = weights("in_a", u)
    p_a = _mm(u, win_a, name="mm_in_a")
    pa = _lane_row([(nf, sp["gdn_a_log"])])
    pb = _lane_row([(0, sp["fox_f_bias"]), (nf, sp["gdn_dt_bias"])])
    vals, csum = _small_fwd(p_a, lay.o_sm, pa, pb, nf, ng)

    c_t = csum[:, :nf].T
    hp = _fox_heads(nf, 3)
    cr = c_t.reshape(nf, t // tk, 1, tk)
    cc = jnp.stack([jnp.pad(csum[:, g * hp:(g + 1) * hp], ((0, 0), (0, hd - hp))) for g in range(nf // hp)])
    fq = _norm_fwd(p_a, lay.o_fq, sp["fox_q_norm"], nf, hd, BF16, name="fox_qnorm_fwd")
    fk = _norm_fwd(p_a, lay.o_fk, sp["fox_k_norm"], nf, hd, BF16, name="fox_knorm_fwd")
    fv = p_a[:, lay.o_fv * hd:(lay.o_fv + nf) * hd].astype(BF16)
    o_fox, lse, mix = _fox_fwd(fq, fk, fv, cc, cr, nf, tq, tk, d)

    prefetch("in_b", lse)
    (win_b,) = weights("in_b", lse)
    prefetch("mixer", win_b)
    p = _mm(u, win_b, name="mm_in_b")
    wmkv, conv_taps = weights("mixer", p)
    sp = dict(sp, gdn_conv=conv_taps)
    qkv = _conv_fwd(p, lay.o_gq, sp["gdn_conv"], ng)
    grow = vals[:, nf:nf + ng].T.reshape(ng, nch, 1, CHUNK)
    o_g, states = _gdn_fwd(qkv, vals, grow, nf, ng)
    mix = _norm_fwd(o_g, 0, sp["gdn_out_norm"], ng, hd, BF16, z=p, zoff=lay.o_gz, into=mix, into_off=nf,
                    name="gdn_out_fwd")
    prefetch("out", mix)

    mem_n = _norm_fwd(mem, 0, sp["mem_norm"], 1, d, BF16, name="mem_norm_fwd")
    mkv = _mm(mem_n, wmkv, name="mm_memkv")
    tq_mem = _tile(t, 1024)
    mix = _mem_fwd(p, lay.o_mq, mkv, sp["mem_q_norm"], sp["mem_k_norm"], tq_mem, mix, nf + ng)
    prefetch("gate_up", mix)
    (wout,) = weights("out", mix)
    h1 = _mm(mix, wout, res=x, name="mm_out")
    n2 = _norm_fwd(h1, 0, sp["norm_ffn"], 1, d, BF16, name="norm_ffn_fwd")
    (wgu,) = weights("gate_up", n2)
    wgu4 = wgu.reshape(4, d, -1)
    gu, act = _ffn_up(n2, wgu4)
    prefetch("down", act)
    (wd,) = weights("down", act)
    h2 = _mm(act, wd, res=h1, name="mm_down")
    loss_blk, dh2, dh2_b = _loss_head(h2, target)

    g = {}
    token = reducer.pair("w_down", _mm(act, dh2_b, ta=True, out_dtype=BF16, name="mm_dw_down"))
    dgu = _ffn_dact(dh2_b, wd, gu, token)
    dw_gate_up = _mm(n2, dgu, ta=True, stack="out", out_dtype=BF16, name="mm_dw_gate_up").reshape(wgu.shape)
    token = reducer.pair("w_gate_up", dw_gate_up)
    dn2 = _mm(dgu, wgu4, tb=True, stack="sum", after=token, name="mm_dn2")
    token = reducer.ship("ffn", ["w_down", "w_gate_up"], dn2)
    dh1, g["norm_ffn"] = _norm_bwd(h1, 0, sp["norm_ffn"] + token[0, 0], dn2, 0, 1, d, res=dh2,
                                   name="norm_ffn_bwd")
    token = reducer.pair("w_out", _mm(mix, dh1, ta=True, out_dtype=BF16, name="mm_dw_out"))
    dmix = _mm(dh1, wout, tb=True, after=token, name="mm_dmix")

    dmq, dmk, dmv, g["mem_q_norm"], g["mem_k_norm"] = _mem_bwd(
        p, lay.o_mq, mkv, sp["mem_q_norm"], sp["mem_k_norm"], dmix, nf + ng, tq_mem)
    dmkv = jnp.concatenate([dmk, dmv], axis=1)
    token = reducer.pair("w_mem_kv", _mm(mem_n, dmkv, ta=True, out_dtype=BF16, name="mm_dw_memkv"))
    dmem_n = _mm(dmkv, wmkv, tb=True, after=token, name="mm_dmem")
    token = reducer.ship("mix", ["w_out", "w_mem_kv"], dmem_n)
    _, g["mem_norm"] = _norm_bwd(mem, 0, sp["mem_norm"], dmem_n, 0, 1, d, name="mem_norm_bwd")

    do_g, dgz, g["gdn_out_norm"] = _norm_bwd(o_g, 0, sp["gdn_out_norm"] + token[0, 0], dmix, nf, ng, hd, z=p,
                                             zoff=lay.o_gz, name="gdn_out_bwd")
    dq, dk, dv, dvals, dgr = _gdn_bwd(qkv, vals, grow, states, do_g, nf, ng)
    dgqkv, g["gdn_conv"] = _conv_bwd(p, lay.o_gq, sp["gdn_conv"], (dq, dk, dv), ng)

    dfq_n, dfk_n, dfv, dcc, dcr = _fox_bwd(fq, fk, fv, cc, cr, o_fox, lse, dmix, nf, tq, tk)
    dfq, g["fox_q_norm"] = _norm_bwd(p_a, lay.o_fq, sp["fox_q_norm"], dfq_n, 0, nf, hd, out_dtype=BF16,
                                     name="fox_qnorm_bwd")
    dfk, g["fox_k_norm"] = _norm_bwd(p_a, lay.o_fk, sp["fox_k_norm"], dfk_n, 0, nf, hd, out_dtype=BF16,
                                     name="fox_knorm_bwd")
    dc = dcc.reshape(nf, t).T + jnp.concatenate([dcr[g, :, :hp] for g in range(nf // hp)], axis=1)

    dvals = dvals + jnp.pad(dgr.reshape(ng, t).T, ((0, 0), (nf, hd - nf - ng)))
    dcsum = jnp.pad(dc, ((0, 0), (0, hd - nf)))
    dsm, dpa, dpb = _small_bwd(p_a, lay.o_sm, pa, pb, dvals, dcsum, nf, ng)
    g["fox_f_bias"] = dpb[:, :nf]
    g["gdn_dt_bias"] = dpb[:, nf:nf + ng]
    g["gdn_a_log"] = dpa[:, nf:nf + ng]

    dp_a = jnp.concatenate([dfq, dfk, dfv.astype(BF16), dsm.astype(BF16),
                            jnp.zeros((t, lay.cols_a - (lay.o_sm + 1) * hd), BF16)], axis=1)
    assert lay.o_gq == 0 and lay.o_gz == 3 * ng and lay.o_mq == lay.o_gz + ng
    rest = jnp.concatenate([dgz.astype(BF16), dmq.astype(BF16),
                            jnp.zeros((t, lay.cols_b - (lay.o_mq + nm) * hd), BF16)], axis=1)
    dp_b = lax.dynamic_update_slice(dgqkv, rest, (0, lay.o_gz * hd))
    token = reducer.pair("w_in_a", _mm(u, dp_a, ta=True, out_dtype=BF16, name="mm_dw_in_a"))
    token = reducer.pair("w_in_b", _mm(u, dp_b, ta=True, out_dtype=BF16, after=token, name="mm_dw_in_b"))
    du = _mm(dp_a, win_a, tb=True, after=token, name="mm_du_a")
    token = reducer.ship("in", ["w_in_a", "w_in_b"], du)
    du = _mm(dp_b, win_b, tb=True, res=du, after=token, name="mm_du_b")
    dx, g["norm_mix"] = _norm_bwd(x, 0, sp["norm_mix"], du, 0, 1, d, res=dh1, name="norm_mix_bwd")
    return loss_blk, dx, g


ANY = pl.BlockSpec(memory_space=pl.ANY)


def _me():
    x, y, c = lax.axis_index("x"), lax.axis_index("y"), lax.axis_index("c")
    chips = [(1 - x, y), (x, 1 - y), (1 - x, 1 - y)]
    return x, y, c, chips


def _slot(axis, k):
    return k if axis == 0 else 2 * (k % 2) + k // 2


def _slab(ref, axis, rows, cols, k, h):
    half = rows // 2
    return ref.at[pl.ds(_slot(axis, k) * rows + h * half, half), :]


def _remote(src, dst, send_sem, recv_sem, dev):
    return pltpu.make_async_remote_copy(src_ref=src, dst_ref=dst, send_sem=send_sem, recv_sem=recv_sem,
                                        device_id=dev, device_id_type=MESH)


HBM = pl.BlockSpec(memory_space=pltpu.HBM)
SEM = pl.BlockSpec(memory_space=pltpu.SEMAPHORE)
SPLIT = pltpu.CompilerParams(has_side_effects=pltpu.SideEffectType.DATAFLOW_SIDE_EFFECTING)
TOKEN = jax.ShapeDtypeStruct((8, HEAD_DIM), F32)


def _in_hbm(v):
    return pltpu.with_memory_space_constraint(v, pltpu.HBM)


def _cast_place(shard, axis, name, col_fn=None, out_cols=None, after=None):
    r, c = shard.shape
    oc = out_cols or c
    tr = _tile(r, 512 if col_fn is None else 64, 16)
    tc = _tile(c, 2048) if col_fn is None else c
    otc = tc if col_fn is None else oc
    nb = r // tr
    chip = 2 * lax.axis_index("x") + lax.axis_index("y")
    slot = jnp.reshape(_slot(axis, chip), (1,)).astype(jnp.int32)

    def body(slot_ref, x_ref, *rest):
        x = x_ref[...]
        rest[-1][...] = (x if col_fn is None else col_fn(x)).astype(BF16)

    extra = [] if after is None else [after]
    return pl.pallas_call(
        body, name=name,
        grid_spec=pltpu.PrefetchScalarGridSpec(
            num_scalar_prefetch=1, grid=(nb, c // tc),
            in_specs=[pl.BlockSpec((tr, tc), lambda i, l, s: (i, l))] + [ANY] * len(extra),
            out_specs=pl.BlockSpec((tr, otc), lambda i, l, s: (s[0] * nb + i, l))),
        out_shape=jax.ShapeDtypeStruct((4 * r, oc), BF16),
        compiler_params=_cparams(("parallel", "parallel")),
    )(slot, shard, *extra)


def _gather_start(bufs, axes, shapes, groups, name):
    n = len(bufs)

    def body(*refs):
        dst = refs[n:2 * n]
        sems = refs[2 * n:2 * n + 2 * len(groups)]
        token = refs[-1]
        x, y, c, chips = _me()
        k = 2 * x + y
        for gi, ws in enumerate(groups):
            for i, w in enumerate(ws):
                r, cl = shapes[w]
                place = _slab(dst[w], axes[w], r, cl, k, c)
                for j, (px, py) in enumerate(chips):
                    _remote(place, place, sems[2 * gi].at[3 * i + j], sems[2 * gi + 1].at[3 * i + j],
                            (px, py, c)).start()
        token[...] = jnp.zeros_like(token)

    sem_shapes = [pltpu.SemaphoreType.DMA((3 * len(ws),)) for ws in groups for _ in range(2)]
    outs = pl.pallas_call(
        body, name=name, in_specs=[HBM] * n,
        out_specs=[HBM] * n + [SEM] * len(sem_shapes) + [pl.BlockSpec(memory_space=pltpu.VMEM)],
        out_shape=[pltpu.HBM(b.shape, b.dtype) for b in bufs] + sem_shapes + [TOKEN],
        input_output_aliases={w: w for w in range(n)}, compiler_params=SPLIT,
    )(*[_in_hbm(b) for b in bufs])
    sems = outs[n:-1]
    return outs[:n], [(sems[2 * g], sems[2 * g + 1]) for g in range(len(groups))], outs[-1]


def _gather_wait(bufs, axes, shapes, sems, after, name):
    n = len(bufs)

    def body(*refs):
        send_sems, recv_sems = refs[n], refs[n + 1]
        dst = refs[n + 3:]
        x, y, c, chips = _me()
        k = 2 * x + y
        for i in range(n):
            r, cl = shapes[i]
            for j, (px, py) in enumerate(chips):
                got = _slab(dst[i], axes[i], r, cl, 2 * px + py, c)
                _remote(got, got, send_sems.at[3 * i + j], recv_sems.at[3 * i + j], (px, py, c)).wait_recv()
        for i in range(n):
            r, cl = shapes[i]
            mine = _slab(dst[i], axes[i], r, cl, k, c)
            for j, (px, py) in enumerate(chips):
                _remote(mine, mine, send_sems.at[3 * i + j], recv_sems.at[3 * i + j], (px, py, c)).wait_send()

    return pl.pallas_call(
        body, name=name, in_specs=[HBM] * n + [SEM, SEM, ANY], out_specs=[HBM] * n,
        out_shape=[pltpu.HBM(b.shape, b.dtype) for b in bufs],
        input_output_aliases={i: i for i in range(n)}, compiler_params=SPLIT,
    )(*bufs, sems[0], sems[1], after)


def _split_start(name, arrays, geometry, count):
    n = len(arrays)

    def body(*refs):
        send, recv, token = refs[2 * n:]
        for i, (src, dst, _, dev) in enumerate(geometry(refs[n:2 * n])):
            _remote(src, dst, send.at[i], recv.at[i], dev).start()
        token[...] = jnp.zeros_like(token)

    sem = pltpu.SemaphoreType.DMA((count,))
    outs = pl.pallas_call(
        body, name=name, in_specs=[HBM] * n,
        out_specs=[HBM] * n + [SEM, SEM, pl.BlockSpec(memory_space=pltpu.VMEM)],
        out_shape=[pltpu.HBM(v.shape, v.dtype) for v in arrays] + [sem, sem, TOKEN],
        input_output_aliases={i: i for i in range(n)}, compiler_params=SPLIT,
    )(*[_in_hbm(v) for v in arrays])
    return list(outs[:n]), (outs[n], outs[n + 1]), outs[-1]


def _split_wait(name, arrays, sems, after, geometry):
    n = len(arrays)

    def body(*refs):
        send, recv = refs[n], refs[n + 1]
        copies = geometry(refs[n + 3:])
        for i, (_, _, land, dev) in enumerate(copies):
            _remote(land, land, send.at[i], recv.at[i], dev).wait_recv()
        for i, (src, _, _, dev) in enumerate(copies):
            _remote(src, src, send.at[i], recv.at[i], dev).wait_send()

    return list(pl.pallas_call(
        body, name=name, in_specs=[HBM] * n + [SEM, SEM, ANY], out_specs=[HBM] * n,
        out_shape=[pltpu.HBM(v.shape, v.dtype) for v in arrays],
        input_output_aliases={i: i for i in range(n)}, compiler_params=SPLIT,
    )(*arrays, sems[0], sems[1], after))


def _forward_geometry(axes, shapes):
    def geometry(bufs):
        x, y, c, chips = _me()
        out = []
        for i, buf in enumerate(bufs):
            r, cl = shapes[i]
            for px, py in chips:
                got = _slab(buf, axes[i], r, cl, 2 * px + py, c)
                out.append((got, got, _slab(buf, axes[i], r, cl, 2 * px + py, 1 - c), (x, y, 1 - c)))
        return out
    return geometry


def _pair_geometry(axes, shapes):
    def geometry(refs):
        n = len(refs) // 2
        x, y, c, _ = _me()
        out = []
        for w in range(n):
            r, cl = shapes[w]
            for j in range(4):
                land = refs[n + w].at[j]
                out.append((_slab(refs[w], axes[w], r, cl, j, 1 - c), land, land, (x, y, 1 - c)))
        return out
    return geometry


def _after_all(name, token, *arrays):
    def body(*refs):
        refs[-1][...] = jnp.zeros_like(refs[-1])

    return pl.pallas_call(
        body, name=name, in_specs=[ANY] * (1 + len(arrays)), out_specs=pl.BlockSpec(memory_space=pltpu.VMEM),
        out_shape=TOKEN,
    )(token, *arrays)


def _swap_geometry(bufs):
    x, y, c, _ = _me()
    return [(b.at[c], b.at[c], b.at[1 - c], (x, y, 1 - c)) for b in bufs]


def _chip_start(parts, tag):
    n = len(parts)

    def body(*refs):
        src, land = refs[2 * n:3 * n], refs[3 * n:4 * n]
        send_sems, recv_sems, token = refs[4 * n:]
        x, y, c, chips = _me()
        k = 2 * x + y
        for w in range(n):
            for j, (px, py) in enumerate(chips):
                _remote(src[w].at[2 * px + py], land[w].at[k], send_sems.at[3 * w + j], recv_sems.at[3 * w + j],
                        (px, py, c)).start()
        token[...] = jnp.zeros_like(token)

    lands = [lax.empty(p.shape, p.dtype) for p in parts]
    sem = pltpu.SemaphoreType.DMA((3 * n,))
    outs = pl.pallas_call(
        body, name="reduce_ici_start_" + tag, in_specs=[HBM] * (2 * n),
        out_specs=[HBM] * (2 * n) + [SEM, SEM, pl.BlockSpec(memory_space=pltpu.VMEM)],
        out_shape=[pltpu.HBM(p.shape, p.dtype) for p in parts + lands] + [sem, sem, TOKEN],
        input_output_aliases={i: i for i in range(2 * n)}, compiler_params=SPLIT,
    )(*[_in_hbm(v) for v in parts + lands])
    return outs[:n], outs[n:2 * n], outs[2 * n], outs[2 * n + 1], outs[-1]


def _chip_wait(parts, lands, send_sems, recv_sems, after, tag):
    n = len(parts)

    def body(*refs):
        send, recv = refs[2 * n], refs[2 * n + 1]
        src, land = refs[2 * n + 3:3 * n + 3], refs[3 * n + 3:]
        x, y, c, chips = _me()
        for w in range(n):
            for j, (px, py) in enumerate(chips):
                got = land[w].at[2 * px + py]
                _remote(got, got, send.at[3 * w + j], recv.at[3 * w + j], (px, py, c)).wait_recv()
        for w in range(n):
            for j, (px, py) in enumerate(chips):
                sent = src[w].at[2 * px + py]
                _remote(sent, sent, send.at[3 * w + j], recv.at[3 * w + j], (px, py, c)).wait_send()

    outs = pl.pallas_call(
        body, name="reduce_ici_wait_" + tag, in_specs=[HBM] * (2 * n) + [SEM, SEM, ANY], out_specs=[HBM] * (2 * n),
        out_shape=[pltpu.HBM(p.shape, p.dtype) for p in parts + lands],
        input_output_aliases={i: i for i in range(2 * n)}, compiler_params=SPLIT,
    )(*parts, *lands, send_sems, recv_sems, after)
    chip = 2 * lax.axis_index("x") + lax.axis_index("y")
    return [lax.dynamic_update_slice(s, lax.dynamic_index_in_dim(p, chip, 0, keepdims=True), (chip, 0, 0))
            for p, s in zip(outs[:n], outs[n:])]


def _half_swap(halves, tag):
    n = len(halves)
    core = lax.axis_index("c")
    bufs = [lax.dynamic_update_slice(lax.empty((2,) + h.shape, h.dtype), h[None], (core, 0, 0)) for h in halves]

    def body(*refs):
        dst = refs[n:2 * n]
        send_sems, recv_sems = refs[2 * n:]
        x, y, c, _ = _me()
        sibling = (x, y, 1 - c)
        cps = []
        for w in range(n):
            cp = _remote(dst[w].at[c], dst[w].at[c], send_sems.at[w], recv_sems.at[w], sibling)
            cp.start()
            cps.append(cp)
        for w in range(n):
            other = dst[w].at[1 - c]
            _remote(other, other, send_sems.at[w], recv_sems.at[w], sibling).wait_recv()
        for cp in cps:
            cp.wait_send()

    outs = pl.pallas_call(
        body, name="reduce_half_swap_" + tag, in_specs=[ANY] * n, out_specs=[ANY] * n,
        out_shape=[jax.ShapeDtypeStruct(b.shape, b.dtype) for b in bufs],
        input_output_aliases={w: w for w in range(n)},
        scratch_shapes=[pltpu.SemaphoreType.DMA((n,)), pltpu.SemaphoreType.DMA((n,))],
    )(*bufs)
    return [o.reshape(2 * o.shape[1], o.shape[2]) for o in outs]


def _add_parts(full, axis, rows, sib, name):
    _, r, c = sib.shape
    tr, tc = _tile(r, 1024, 16), _tile(c, 2048)
    nb = r // tr
    core = jnp.reshape(lax.axis_index("c"), (1,)).astype(jnp.int32)

    def body(c_ref, a_ref, b_ref, o_ref):
        o_ref[0] = (a_ref[...].astype(F32) + b_ref[0].astype(F32)).astype(BF16)

    blk = pl.BlockSpec((1, tr, tc), lambda j, i, l, cr: (j, i, l))
    return pl.pallas_call(
        body, name=name,
        grid_spec=pltpu.PrefetchScalarGridSpec(
            num_scalar_prefetch=1, grid=(4, nb, c // tc),
            in_specs=[pl.BlockSpec((tr, tc), lambda j, i, l, cr: ((_slot(axis, j) * 2 + cr[0]) * nb + i, l)), blk],
            out_specs=blk),
        out_shape=jax.ShapeDtypeStruct(sib.shape, BF16),
        compiler_params=_cparams(("parallel", "parallel", "parallel")),
    )(core, full, sib)


def _sum_slots(a, name):
    _, r, c = a.shape
    tr, tc = _tile(r, 512, 8), _tile(c, 2048)

    def body(a_ref, o_ref):
        v = a_ref[...].astype(F32)
        o_ref[...] = ((v[0] + v[1]) + v[2]) + v[3]

    return pl.pallas_call(
        body, name=name, grid=(r // tr, c // tc),
        in_specs=[pl.BlockSpec((4, tr, tc), lambda i, l: (0, i, l))],
        out_specs=pl.BlockSpec((tr, tc), lambda i, l: (i, l)),
        out_shape=jax.ShapeDtypeStruct((r, c), F32),
        compiler_params=_cparams(("parallel", "parallel")),
    )(a)


class _Reducer:
    def __init__(self, spec):
        self.spec = spec
        self.paired = {}
        self.pending = []

    def pair(self, name, full):
        ax, shp = self.spec[name]
        land = lax.empty((4, shp[0] // 2, shp[1]), full.dtype)
        arrays, sems, token = _split_start("reduce_pair_start_" + name, [full, land], _pair_geometry([ax], [shp]), 4)
        self.paired[name] = (arrays, sems)
        return token

    def ship(self, tag, names, after):
        parts = []
        for n in names:
            ax, shp = self.spec[n]
            arrays, sems = self.paired.pop(n)
            full, sib = _split_wait("reduce_pair_wait_" + n, arrays, sems, after, _pair_geometry([ax], [shp]))
            parts.append(_add_parts(full, ax, shp[0], sib, name=f"reduce_add_{n}"))
        parts, lands, send, recv, token = _chip_start(parts, tag)
        self.pending.append((tag, names, parts, lands, send, recv))
        return token

    def finish(self, after, tags):
        out = {}
        for tag, names, parts, lands, send, recv in [p for p in self.pending if p[0] in tags]:
            slots = _chip_wait(parts, lands, send, recv, after, tag)
            halves = [_sum_slots(s, name=f"reduce_sum_{n}") for n, s in zip(names, slots)]
            out.update(zip(names, _half_swap(halves, tag)))
        return out

    def finish_start(self, after, tag):
        (_, names, parts, lands, send, recv), = [p for p in self.pending if p[0] == tag]
        slots = _chip_wait(parts, lands, send, recv, after, tag)
        halves = [_sum_slots(s, name=f"reduce_sum_{n}") for n, s in zip(names, slots)]
        core = lax.axis_index("c")
        bufs = [lax.dynamic_update_slice(lax.empty((2,) + h.shape, h.dtype), h[None], (core, 0, 0)) for h in halves]
        bufs, sems, _ = _split_start("reduce_half_swap_start_" + tag, bufs, _swap_geometry, len(bufs))
        return tag, names, bufs, sems

    def swap_wait(self, started, after):
        tag, names, bufs, sems = started
        outs = _split_wait("reduce_half_swap_wait_" + tag, bufs, sems, after, _swap_geometry)
        return dict(zip(names, [o.reshape(2 * o.shape[1], o.shape[2]) for o in outs]))


def _allreduce_small(pack, after):
    rows = pack.shape[0]

    def body(p_ref, _, o_ref, slots, send_sems, recv_sems):
        x, y, c, _ = _me()
        me = 4 * x + 2 * y + c
        slots[me] = p_ref[...]
        cps = []
        for r in range(1, 8):
            peer = (x ^ (r >> 2), y ^ ((r >> 1) & 1), c ^ (r & 1))
            cp = _remote(p_ref, slots.at[me], send_sems.at[r - 1], recv_sems.at[r - 1], peer)
            cp.start()
            cps.append(cp)
        for r in range(1, 8):
            frm = me ^ r
            _remote(slots.at[frm], slots.at[frm], send_sems.at[r - 1], recv_sems.at[r - 1], (x, y, c)).wait_recv()
        for cp in cps:
            cp.wait_send()
        acc = slots[0]
        for s in range(1, 8):
            acc = acc + slots[s]
        o_ref[...] = acc

    vm = pl.BlockSpec(memory_space=pltpu.VMEM)
    return pl.pallas_call(
        body, name="allreduce_small", in_specs=[vm, ANY], out_specs=vm,
        out_shape=jax.ShapeDtypeStruct(pack.shape, F32),
        scratch_shapes=[pltpu.VMEM((8, rows, HEAD_DIM), F32), pltpu.SemaphoreType.DMA((7,)),
                        pltpu.SemaphoreType.DMA((7,))],
    )(pack, after)


_ROWS = ["norm_mix", "norm_ffn", "mem_norm", "fox_q_norm", "fox_k_norm", "gdn_out_norm", "mem_q_norm",
         "mem_k_norm", "fox_f_bias", "gdn_a_log", "gdn_dt_bias"]


def _pack_rows(vals):
    out = []
    for name in _ROWS:
        v = vals[name].reshape(-1)
        n = -(-v.shape[0] // HEAD_DIM) * HEAD_DIM
        out.append(jnp.pad(v, (0, n - v.shape[0])).reshape(-1, HEAD_DIM))
    return jnp.concatenate(out, axis=0)


def _unpack_rows(pack, like):
    out, r = {}, 0
    for name in _ROWS:
        n = like[name].shape[-1]
        nr = -(-n // HEAD_DIM)
        out[name] = pack[r:r + nr].reshape(1, -1)[:, :n]
        r += nr
    return out, r


def kernel(x, mem, norm_mix, w_in, fox_f_bias, fox_q_norm, fox_k_norm, gdn_conv, gdn_a_log, gdn_dt_bias, gdn_out_norm, mem_norm, w_mem_kv, mem_q_norm, mem_k_norm, w_out, norm_ffn, w_gate_up, w_down, loss_target, m_norm_mix, m_w_in, m_fox_f_bias, m_fox_q_norm, m_fox_k_norm, m_gdn_conv, m_gdn_a_log, m_gdn_dt_bias, m_gdn_out_norm, m_mem_norm, m_w_mem_kv, m_mem_q_norm, m_mem_k_norm, m_w_out, m_norm_ffn, m_w_gate_up, m_w_down, v_norm_mix, v_w_in, v_fox_f_bias, v_fox_q_norm, v_fox_k_norm, v_gdn_conv, v_gdn_a_log, v_gdn_dt_bias, v_gdn_out_norm, v_mem_norm, v_w_mem_kv, v_mem_q_norm, v_mem_k_norm, v_w_out, v_norm_ffn, v_w_gate_up, v_w_down):
    a = dict(locals())
    d = x.shape[-1]
    lay = _Layout(d)
    chip = 2 * lax.axis_index("x") + lax.axis_index("y")
    small = {n: a[n] for n in _ROWS}
    big = ["w_in", "w_mem_kv", "w_out", "w_gate_up", "w_down"]
    axes = [0, 0, 0, 1, 0]

    conv_cols = gdn_conv.shape[-1]
    conv_n = CONV_WIDTH * conv_cols
    conv_rows = -(-conv_n // HEAD_DIM)
    conv_blk = jnp.pad(gdn_conv.reshape(-1), (0, 32 * HEAD_DIM - conv_n)).reshape(32, HEAD_DIM)
    axis_of = dict(zip(big, axes), conv=0, w_in_a=0, w_in_b=0)
    shape_of = {n: a[n].shape[1:] for n in big[1:]}
    shape_of.update(w_in_a=(w_in.shape[1], lay.cols_a), w_in_b=(w_in.shape[1], lay.cols_b), conv=conv_blk.shape)
    placed = {"w_in_a": _cast_place(w_in[0], 0, "cast_w_in_a", lambda v: lay.regroup(v)[:, :lay.cols_a], lay.cols_a),
              "conv": lax.dynamic_update_slice(lax.empty((4 * 32, HEAD_DIM), F32), conv_blk, (chip * 32, 0))}
    grouped = {"in_a": ["w_in_a"], "in_b": ["w_in_b"], "mixer": ["w_mem_kv", "conv"], "out": ["w_out"],
               "gate_up": ["w_gate_up"], "down": ["w_down"]}
    inflight = {}

    def start(tags, name):
        names = [n for t in tags for n in grouped[t]]
        bufs, sems, token = _gather_start([placed[n] for n in names], [axis_of[n] for n in names],
                                          [shape_of[n] for n in names],
                                          [[names.index(n) for n in grouped[t]] for t in tags], name)
        for t, pair in zip(tags, sems):
            inflight[t] = ([bufs[names.index(n)] for n in grouped[t]], pair)
        return token

    first = start(["in_a"], "gather_ici_start_in")
    placed["w_in_b"] = _cast_place(w_in[0], 0, "cast_w_in_b", lambda v: lay.regroup(v)[:, lay.cols_a:], lay.cols_b,
                                   after=first)
    placed.update({n: _cast_place(a[n][0], axis_of[n], "cast_" + n, after=first) for n in big[1:]})
    all_started = start(["in_b", "mixer", "out", "gate_up", "down"], "gather_ici_start_rest")
    all_started = _after_all("moments_ready", all_started, m_w_in[0], v_w_in[0])

    forwarding = {}

    def prefetch(tag, after):
        bufs, sem_pair = inflight.pop(tag)
        ax, shp = [axis_of[n] for n in grouped[tag]], [shape_of[n] for n in grouped[tag]]
        got = _gather_wait(bufs, ax, shp, sem_pair, all_started if tag == "in_a" else after,
                           "gather_ici_wait_" + tag)
        geometry = _forward_geometry(ax, shp)
        got, sems, _ = _split_start("gather_forward_start_" + tag, got, geometry, 3 * len(got))
        forwarding[tag] = (got, sems, geometry)

    def weights(tag, after):
        got, sems, geometry = forwarding.pop(tag)
        got = _split_wait("gather_forward_wait_" + tag, got, sems, after, geometry)
        if tag != "mixer":
            return got
        taps = got[1].reshape(4, 32 * HEAD_DIM)[:, :conv_n].reshape(4, CONV_WIDTH, conv_cols)
        return got[0], jnp.transpose(taps, (1, 0, 2)).reshape(CONV_WIDTH, 4 * conv_cols)

    sp = dict(small)
    reducer = _Reducer({n: (axis_of[n], shape_of[n]) for n in big[1:] + ["w_in_a", "w_in_b"]})
    loss_blk, dx, g = _local_step(x[0], mem[0], loss_target[0], prefetch, weights, reducer, sp)

    gsmall = {n: g[n] for n in _ROWS}
    pack = jnp.concatenate([_pack_rows(gsmall), g["gdn_conv"].reshape(-1, HEAD_DIM), loss_blk], axis=0)
    pack = jnp.pad(pack, ((0, -pack.shape[0] % 8), (0, 0)))
    out = {"grad_x": dx[None]}

    def adamw_shards(reduced):
        if "w_in_a" in reduced:
            reduced = {"w_in": (reduced["w_in_a"], reduced["w_in_b"])}
        for n, gsh in reduced.items():
            join = (lambda ga, gb: lay.ungroup(jnp.concatenate([ga, gb], axis=1))) if n == "w_in" else None
            res = _adamw(a[n][0], gsh, a["m_" + n][0], a["v_" + n][0], g_fn=join, name="adamw_" + n)
            for pre, r in zip(["grad_", "delta_", "new_m_", "new_v_"], res):
                out[pre + n] = r[None]
        return res[0]

    mix_swap = reducer.finish_start(dx, "mix")
    ffn_swap = reducer.finish_start(mix_swap[2][0], "ffn")
    done = adamw_shards(reducer.swap_wait(mix_swap, ffn_swap[2][0]))
    done = adamw_shards(reducer.swap_wait(ffn_swap, done))
    tot = _allreduce_small(pack, done)
    gs, r0 = _unpack_rows(tot, small)
    conv_g = tot[r0:r0 + CONV_WIDTH * 4 * conv_cols // HEAD_DIM].reshape(CONV_WIDTH, 4 * conv_cols)
    gs_conv = lax.dynamic_slice_in_dim(conv_g, chip * conv_cols, conv_cols, axis=1)
    out["loss"] = tot[r0 + CONV_WIDTH * 4 * conv_cols // HEAD_DIM, 0]
    adamw_shards(reducer.finish(tot, ("in",)))
    conv_pad = lambda v: jnp.pad(v.reshape(-1), (0, conv_rows * HEAD_DIM - conv_n)).reshape(conv_rows, HEAD_DIM)
    packs = []
    for src, cv in [(small, gdn_conv), (gs, gs_conv), ({n: a["m_" + n] for n in _ROWS}, m_gdn_conv),
                    ({n: a["v_" + n] for n in _ROWS}, v_gdn_conv)]:
        packs.append(jnp.concatenate([_pack_rows(src), conv_pad(cv)], axis=0))
    res = _adamw(*packs, name="adamw_small")
    for pre, r in zip(["grad_", "delta_", "new_m_", "new_v_"], res):
        vals, r1 = _unpack_rows(r, small)
        for n in _ROWS:
            out[pre + n] = vals[n]
        out[pre + "gdn_conv"] = r[r1:r1 + conv_rows].reshape(-1)[:conv_n].reshape(gdn_conv.shape)
    names = ["norm_mix", "w_in", "fox_f_bias", "fox_q_norm", "fox_k_norm", "gdn_conv", "gdn_a_log", "gdn_dt_bias",
             "gdn_out_norm", "mem_norm", "w_mem_kv", "mem_q_norm", "mem_k_norm", "w_out", "norm_ffn", "w_gate_up",
             "w_down"]
    return (out["loss"], out["grad_x"], *[out[p + n] for p in ["grad_", "delta_", "new_m_", "new_v_"] for n in names])
```

```python
import functools
import math

import jax
import jax.numpy as jnp
from jax import lax
from jax.experimental import pallas as pl
from jax.experimental.pallas import tpu as pltpu

F32, BF16 = jnp.float32, jnp.bfloat16
HEAD_DIM = 128
CHUNK = 64
N_MEM_HEADS = 4
CONV_WIDTH = 4
NORM_EPS = 1e-6
ADAM_LR, ADAM_B1, ADAM_B2, ADAM_EPS, ADAM_WD, ADAM_STEP = 0.001, 0.9, 0.999, 1e-08, 0.01, 10
VMEM_LIMIT = 48 * 1024 * 1024
NEG = -1e30
MESH = pl.DeviceIdType.MESH


def _cparams(sem=None, **kw):
    if sem is not None:
        kw["dimension_semantics"] = sem
    return pltpu.CompilerParams(vmem_limit_bytes=VMEM_LIMIT, **kw)


def _tile(n, target, mult=128):
    best = None
    d = mult
    while d <= min(n, target):
        if n % d == 0:
            best = d
        d += mult
    return best if best is not None else n


def _dot(a, b, dims, hi):
    if a.ndim == 3:
        dn = (((dims[0][0] + 1,), (dims[1][0] + 1,)), ((0,), (0,)))
    else:
        dn = (dims, ((), ()))
    if hi is not None:
        return lax.dot_general(a, b, dn, precision=hi, preferred_element_type=F32)
    return lax.dot_general(a.astype(BF16), b.astype(BF16), dn, preferred_element_type=F32)


def _make_dots(hi, cotangent=None):
    @jax.custom_vjp
    def nn(a, b):
        return _dot(a, b, ((1,), (0,)), hi)

    @jax.custom_vjp
    def nt(a, b):
        return _dot(a, b, ((1,), (1,)), hi)

    @jax.custom_vjp
    def tn(a, b):
        return _dot(a, b, ((0,), (0,)), hi)

    bnn, bnt, btn = cotangent or (nn, nt, tn)
    nn.defvjp(lambda a, b: (nn(a, b), (a, b)), lambda r, g: (bnt(g, r[1]), btn(r[0], g)))
    nt.defvjp(lambda a, b: (nt(a, b), (a, b)), lambda r, g: (bnn(g, r[1]), btn(g, r[0])))
    tn.defvjp(lambda a, b: (tn(a, b), (a, b)), lambda r, g: (bnt(r[1], g), bnn(r[0], g)))
    return nn, nt, tn


_nn, _nt, _tn = _make_dots(None)
_nn_hi, _nt_hi, _tn_hi = _make_dots(lax.Precision.HIGHEST)


def _sigmoid(x):
    return jax.nn.sigmoid(x)


@jax.custom_vjp
def _softplus(x):
    return jnp.maximum(x, 0.0) + jnp.log(1.0 + jnp.exp(-jnp.abs(x)))


_softplus.defvjp(lambda x: (_softplus(x), x), lambda x, g: (g * _sigmoid(x),))


def _silu(x):
    return x * _sigmoid(x)


def _rms_fn(x, gain, z=None):
    y = x * lax.rsqrt(jnp.mean(x * x, axis=-1, keepdims=True) + NORM_EPS) * gain
    if z is not None:
        y = y * _silu(z)
    return y


def _mm(a, b, *, ta=False, tb=False, out_dtype=F32, res=None, stack=None, after=None, name):
    a2, b2 = a.shape[-2:], b.shape[-2:]
    ns = b.shape[0] if stack else 1
    m = a2[1] if ta else a2[0]
    k = a2[0] if ta else a2[1]
    n = b2[0] if tb else b2[1]
    assert k == (b2[1] if tb else b2[0])
    tm, tn, tk = _mm_tiles(m, n, k, ns if stack == "sum" else 1, a.dtype.itemsize, b.dtype.itemsize,
                           jnp.dtype(out_dtype).itemsize, res is not None)
    nk = k // tk
    single = nk == 1 and stack != "sum"
    dims = ((0 if ta else 1,), (1 if tb else 0,))
    if stack == "sum":
        order = lambda g0, g1, g2, g3: (g2, g0, g1, g3)
        grid = (m // tm, n // tn, ns, nk)
    else:
        order = lambda g0, g1, g2, g3: (g0, g1, g2, g3)
        grid = (ns, m // tm, n // tn, nk)

    def body(*refs):
        if after is not None:
            refs = refs[:2 + (res is not None)] + refs[3 + (res is not None):]
        if single:
            a_ref, b_ref = refs[:2]
            r = lax.dot_general(a_ref[...].astype(BF16), b_ref[...].astype(BF16), (dims, ((), ())),
                                preferred_element_type=F32)
            if res is not None:
                r = r + refs[2][...]
            refs[-1][...] = r.astype(out_dtype)
            return
        if res is None:
            a_ref, b_ref, o_ref, acc = refs
        else:
            a_ref, b_ref, r_ref, o_ref, acc = refs
        s, _, _, kk = order(*[pl.program_id(d) for d in range(4)])
        first = kk == 0
        last = kk == nk - 1
        if stack == "sum":
            first, last = first & (s == 0), last & (s == ns - 1)

        @pl.when(first)
        def _():
            acc[...] = jnp.zeros_like(acc)

        acc[...] += lax.dot_general(a_ref[...].astype(BF16), b_ref[...].astype(BF16), (dims, ((), ())),
                                    preferred_element_type=F32)

        @pl.when(last)
        def _():
            r = acc[...]
            if res is not None:
                r = r + r_ref[...]
            o_ref[...] = r.astype(out_dtype)

    def spec(shape, idx, stacked):
        if stacked:
            return pl.BlockSpec((None,) + shape, lambda *g: (order(*g)[0],) + idx(*order(*g)))
        return pl.BlockSpec(shape, lambda *g: idx(*order(*g)))

    a_spec = (spec((tk, tm), lambda s, i, j, kk: (kk, i), stack == "sum") if ta
              else spec((tm, tk), lambda s, i, j, kk: (i, kk), stack == "sum"))
    b_spec = (spec((tn, tk), lambda s, i, j, kk: (j, kk), bool(stack)) if tb
              else spec((tk, tn), lambda s, i, j, kk: (kk, j), bool(stack)))
    o_spec = spec((tm, tn), lambda s, i, j, kk: (i, j), stack == "out")
    ins, specs = [a, b], [a_spec, b_spec]
    if res is not None:
        ins.append(res)
        specs.append(o_spec)
    if after is not None:
        ins.append(after)
        specs.append(pl.BlockSpec(after.shape, lambda *g: (0,) * after.ndim))
    sem = (("parallel", "parallel", "arbitrary", "arbitrary") if stack == "sum"
           else ("parallel", "parallel", "parallel", "arbitrary"))
    return pl.pallas_call(
        body, name=name, grid=grid, in_specs=specs, out_specs=o_spec,
        out_shape=jax.ShapeDtypeStruct(((ns,) if stack == "out" else ()) + (m, n), out_dtype),
        scratch_shapes=[] if single else [pltpu.VMEM((tm, tn), F32)],
        compiler_params=_cparams(sem),
    )(*ins)


MM_VMEM_BUDGET = 40 * 1024 * 1024
MXU_WIDTH = 256


def _mm_tiles(m, n, k, ns, sa, sb, so, has_res):
    def divs(x, mult, cap):
        out = [d for d in range(mult, min(x, cap) + 1, mult) if x % d == 0]
        return out or [x]

    best = None
    for tk in divs(k, 128, 8192):
        nk = (k // tk) * ns
        for tm in divs(m, 8, 2048):
            for tn in divs(n, 128, 2048):
                vmem = 2 * (tm * tk * sa + tk * tn * sb + tm * tn * so) + (2 * tm * tn * 4 if has_res else 0)
                vmem += tm * tn * 4 if nk > 1 else 0
                if vmem > MM_VMEM_BUDGET:
                    continue
                steps = (m // tm) * (n // tn) * nk
                traffic = (m // tm) * k * n * sb * ns + (n // tn if nk > 1 else 1) * m * k * sa * ns
                cost = steps * 0.4e-6 + traffic / 2.5e12 + (nk * m * n * 8 / 6e12 if nk > 1 else 0)
                cost += 2.0 * m * n * k * ns / 7e14 * (-(-tn // MXU_WIDTH) * MXU_WIDTH / tn)
                if best is None or cost < best[0]:
                    best = (cost, tm, tn, tk)
    return best[1:]


def _norm_fwd(x, xoff, gain, ncol, w, out_dtype, *, z=None, zoff=0, into=None, into_off=0, name):
    t = x.shape[0]
    tr = _tile(t, max(256, (1 << 18) // w), 8)

    def body(*refs):
        x_ref, g_ref, o_ref = refs[0], refs[1], refs[-1]
        y = _rms_fn(x_ref[...], g_ref[...]) if z is None else _rms_fn(x_ref[...], g_ref[...], refs[2][...])
        o_ref[...] = y.astype(out_dtype)

    ins = [x, gain]
    specs = [pl.BlockSpec((tr, w), lambda j, r: (r, xoff + j)), pl.BlockSpec((1, w), lambda j, r: (0, 0))]
    if z is not None:
        ins.append(z)
        specs.append(pl.BlockSpec((tr, w), lambda j, r: (r, zoff + j)))
    aliases = {}
    if into is not None:
        aliases = {len(ins): 0}
        ins.append(into)
        specs.append(pl.BlockSpec(memory_space=pl.ANY))
    return pl.pallas_call(
        body, name=name, grid=(ncol, t // tr), in_specs=specs,
        out_specs=pl.BlockSpec((tr, w), lambda j, r: (r, into_off + j)),
        out_shape=jax.ShapeDtypeStruct((t, ncol * w) if into is None else into.shape, out_dtype),
        input_output_aliases=aliases, compiler_params=_cparams(("parallel", "parallel")),
    )(*ins)


def _norm_bwd(x, xoff, gain, dy, dyoff, ncol, w, *, z=None, zoff=0, res=None, out_dtype=F32, also_bf16=False,
              name):
    t = x.shape[0]
    tr = _tile(t, max(256, (1 << 18) // w), 8)

    def body(*refs):
        it = iter(refs)
        x_ref, g_ref = next(it), next(it)
        z_ref = next(it) if z is not None else None
        dy_ref = next(it)
        r_ref = next(it) if res is not None else None
        dx_ref = next(it)
        dz_ref = next(it) if z is not None else None
        lo_ref = next(it) if also_bf16 else None
        dg_ref = next(it)

        @pl.when((pl.program_id(0) == 0) & (pl.program_id(1) == 0))
        def _():
            dg_ref[...] = jnp.zeros_like(dg_ref)

        args = (x_ref[...], g_ref[...]) + ((z_ref[...],) if z is not None else ())
        _, vjp = jax.vjp(_rms_fn, *args)
        grads = vjp(dy_ref[...].astype(F32))
        dx = grads[0]
        if res is not None:
            dx = dx + r_ref[...]
        dx_ref[...] = dx.astype(out_dtype)
        if also_bf16:
            lo_ref[...] = dx.astype(BF16)
        if z is not None:
            dz_ref[...] = grads[2]
        dg_ref[...] += grads[1]

    ins = [x, gain]
    specs = [pl.BlockSpec((tr, w), lambda j, r: (r, xoff + j)), pl.BlockSpec((1, w), lambda j, r: (0, 0))]
    if z is not None:
        ins.append(z)
        specs.append(pl.BlockSpec((tr, w), lambda j, r: (r, zoff + j)))
    ins.append(dy)
    specs.append(pl.BlockSpec((tr, w), lambda j, r: (r, dyoff + j)))
    blk = pl.BlockSpec((tr, w), lambda j, r: (r, j))
    if res is not None:
        ins.append(res)
        specs.append(blk)
    full = jax.ShapeDtypeStruct((t, ncol * w), F32)
    out_shape, out_specs = [jax.ShapeDtypeStruct((t, ncol * w), out_dtype)], [blk]
    if z is not None:
        out_shape.append(full)
        out_specs.append(blk)
    if also_bf16:
        out_shape.append(jax.ShapeDtypeStruct((t, ncol * w), BF16))
        out_specs.append(blk)
    out_shape.append(jax.ShapeDtypeStruct((1, w), F32))
    out_specs.append(pl.BlockSpec((1, w), lambda j, r: (0, 0)))
    return pl.pallas_call(
        body, name=name, grid=(ncol, t // tr), in_specs=specs, out_specs=out_specs, out_shape=out_shape,
        compiler_params=_cparams(("arbitrary", "arbitrary")),
    )(*ins)


def _small_fn(x, pa, pb, nf, ng):
    lane = lax.broadcasted_iota(jnp.int32, x.shape, 1)
    zz = x + pb
    logf = -_softplus(-zz)
    g = -jnp.exp(pa) * _softplus(zz)
    beta = _sigmoid(x)
    return jnp.where(lane < nf, logf, jnp.where(lane < nf + ng, g, beta))


def _tri(n, upper):
    r = lax.broadcasted_iota(jnp.int32, (n, n), 0)
    c = lax.broadcasted_iota(jnp.int32, (n, n), 1)
    return jnp.where((c >= r) if upper else (c <= r), 1.0, 0.0).astype(F32)


def _small_fwd(p, off, pa, pb, nf, ng):
    t = p.shape[0]
    blk = HEAD_DIM
    nb = t // blk

    def body(x_ref, pa_ref, pb_ref, v_ref, c_ref):
        v_ref[...] = _small_fn(x_ref[...], pa_ref[...], pb_ref[...], nf, ng)
        tri = _tri(blk, False)

        carry = jnp.zeros((1, HEAD_DIM), F32)
        for i in range(nb):
            rows = slice(i * blk, (i + 1) * blk)
            c = _nn_hi(tri, v_ref[rows, :]) + carry
            c_ref[rows, :] = c
            carry = c[blk - 1:blk, :]

    row = pl.BlockSpec((1, HEAD_DIM), lambda i: (0, 0))
    out = pl.BlockSpec((t, HEAD_DIM), lambda i: (0, 0))
    return pl.pallas_call(
        body, name="small_fwd", grid=(1,),
        in_specs=[pl.BlockSpec((t, HEAD_DIM), lambda i: (0, off)), row, row], out_specs=[out, out],
        out_shape=[jax.ShapeDtypeStruct((t, HEAD_DIM), F32)] * 2,
        compiler_params=_cparams(("arbitrary",)),
    )(p, pa, pb)


def _small_bwd(p, off, pa, pb, dvals, dcsum, nf, ng):
    t = p.shape[0]
    blk = HEAD_DIM
    nb = t // blk

    def body(x_ref, pa_ref, pb_ref, dv_ref, dc_ref, dx_ref, dpa_ref, dpb_ref, tot_ref):
        tri = _tri(blk, True)

        carry = jnp.zeros((1, HEAD_DIM), F32)
        for i in reversed(range(nb)):
            rows = slice(i * blk, (i + 1) * blk)
            c = _nn_hi(tri, dc_ref[rows, :]) + carry
            tot_ref[rows, :] = c + dv_ref[rows, :]
            carry = c[0:1, :]
        f = functools.partial(_small_fn, nf=nf, ng=ng)
        _, vjp = jax.vjp(f, x_ref[...], pa_ref[...], pb_ref[...])
        dx, dpa, dpb = vjp(tot_ref[...])
        dx_ref[...] = dx
        dpa_ref[...] = dpa
        dpb_ref[...] = dpb

    row = pl.BlockSpec((1, HEAD_DIM), lambda i: (0, 0))
    full = pl.BlockSpec((t, HEAD_DIM), lambda i: (0, 0))
    return pl.pallas_call(
        body, name="small_bwd", grid=(1,),
        in_specs=[pl.BlockSpec((t, HEAD_DIM), lambda i: (0, off)), row, row, full, full],
        out_specs=[full, row, row],
        out_shape=[jax.ShapeDtypeStruct((t, HEAD_DIM), F32), jax.ShapeDtypeStruct((1, HEAD_DIM), F32),
                   jax.ShapeDtypeStruct((1, HEAD_DIM), F32)],
        scratch_shapes=[pltpu.VMEM((t, HEAD_DIM), F32)],
        compiler_params=_cparams(("arbitrary",)),
    )(p, pa, pb, dvals, dcsum)


def _fox_heads(nf, most):
    return next(h for h in range(most, 0, -1) if nf % h == 0)


def _fox_fwd(q, k, v, cc, cr, nf, tq, tk, d_mix):
    t = q.shape[0]
    scale = HEAD_DIM ** -0.5
    assert tq == tk

    vt = jnp.transpose(v.reshape(t // tk, tk, nf, HEAD_DIM), (2, 0, 3, 1))

    hp = _fox_heads(nf, 3)
    lanes = lambda h: slice(h * HEAD_DIM, (h + 1) * HEAD_DIM)

    def body(q_ref, k_ref, vt_ref, cc_ref, cr_ref, o_ref, lse_ref, mix_ref):
        i = pl.program_id(1)
        qs = [q_ref[:, lanes(h)] for h in range(hp)]
        cqs = [cr_ref[h, i] for h in range(hp)]
        ones = jnp.ones((8, tk), BF16)
        diff = lax.broadcasted_iota(jnp.int32, (tk, tq), 0) - lax.broadcasted_iota(jnp.int32, (tk, tq), 1)

        def scores(h, j):
            ks = pl.ds(pl.multiple_of(j * tk, tk), tk)
            return lax.dot_general(k_ref[ks, lanes(h)], qs[h], (((1,), (1,)), ((), ())),
                                   preferred_element_type=F32)

        def tile(h, j, m, l, acc, s, masked):
            ks = pl.ds(pl.multiple_of(j * tk, tk), tk)
            s = s * scale + cqs[h] - cc_ref[0, ks, h:h + 1]
            if masked:
                s = jnp.where(diff <= 0, s, NEG)
            m_new = jnp.maximum(m, jnp.max(s, axis=0, keepdims=True))
            pr = jnp.exp(s - m_new).astype(BF16)
            alpha = jnp.exp(m - m_new)
            l = alpha * l + jnp.dot(ones, pr, preferred_element_type=F32)[:1]
            acc = alpha * acc + jnp.dot(vt_ref[h, j], pr, preferred_element_type=F32)
            return m_new, l, acc

        def step(j, carry):
            nxt = [scores(h, j + 1) for h in range(hp)]
            return tuple(tile(h, j, *carry[h], False) + (nxt[h],) for h in range(hp))

        init = tuple((jnp.full((1, tq), NEG, F32), jnp.zeros((1, tq), F32), jnp.zeros((HEAD_DIM, tq), F32),
                      scores(h, 0)) for h in range(hp))
        carry = lax.fori_loop(0, i, step, init)
        for h in range(hp):
            m, l, acc = tile(h, i, *carry[h], True)
            o = jnp.transpose(acc / l)
            o_ref[:, lanes(h)] = o
            mix_ref[:, lanes(h)] = o.astype(BF16)
            lse_ref[h, 0] = m + jnp.log(l)

    w = hp * HEAD_DIM
    qblk = pl.BlockSpec((tq, w), lambda h, i: (i, h))
    return pl.pallas_call(
        body, name="fox_fwd", grid=(nf // hp, t // tq),
        in_specs=[qblk, pl.BlockSpec((t, w), lambda h, i: (0, h)),
                  pl.BlockSpec((hp, t // tk, HEAD_DIM, tk), lambda h, i: (h, 0, 0, 0)),
                  pl.BlockSpec((1, t, HEAD_DIM), lambda h, i: (h, 0, 0)),
                  pl.BlockSpec((hp, t // tk, 1, tk), lambda h, i: (h, 0, 0, 0))],
        out_specs=[qblk, pl.BlockSpec((hp, 1, 1, tq), lambda h, i: (h, i, 0, 0)), qblk],
        out_shape=[jax.ShapeDtypeStruct((t, nf * HEAD_DIM), F32), jax.ShapeDtypeStruct((nf, t // tq, 1, tq), F32),
                   jax.ShapeDtypeStruct((t, d_mix), BF16)],
        compiler_params=_cparams(("parallel", "parallel")),
    )(q, k, vt, cc, cr)


def _fox_bwd(q, k, v, cc, cr, o, lse, dmix, nf, tq, tk):
    t = q.shape[0]
    scale = HEAD_DIM ** -0.5
    assert tq == tk
    hp = _fox_heads(nf, 3)
    lanes = lambda h: slice(h * HEAD_DIM, (h + 1) * HEAD_DIM)
    kt = jnp.transpose(k.reshape(t // tk, tk, nf, HEAD_DIM), (2, 0, 3, 1))

    def body(q_ref, k_ref, kt_ref, v_ref, cc_ref, cr_ref, o_ref, lse_ref, do_ref,
             dq_ref, dk_ref, dv_ref, dcq_ref, dck_ref):
        i = pl.program_id(1)

        @pl.when(i == 0)
        def _():
            dk_ref[...] = jnp.zeros_like(dk_ref)
            dv_ref[...] = jnp.zeros_like(dv_ref)
            dck_ref[...] = jnp.zeros_like(dck_ref)

        diff = lax.broadcasted_iota(jnp.int32, (tk, tq), 0) - lax.broadcasted_iota(jnp.int32, (tk, tq), 1)
        lane = lax.broadcasted_iota(jnp.int32, (tk, HEAD_DIM), 1)
        qs = [q_ref[:, lanes(h)] for h in range(hp)]
        dos = [do_ref[:, lanes(h)] for h in range(hp)]
        do_b = [d.astype(BF16) for d in dos]
        cqs = [cr_ref[h, i] for h in range(hp)]
        lses = [lse_ref[h, 0] for h in range(hp)]
        deltas = [jnp.sum(jnp.transpose(dos[h] * o_ref[:, lanes(h)]), axis=0, keepdims=True) for h in range(hp)]

        def products(h, j):
            ks = pl.ds(pl.multiple_of(j * tk, tk), tk)
            nt = (((1,), (1,)), ((), ()))
            return (lax.dot_general(k_ref[ks, lanes(h)], qs[h], nt, preferred_element_type=F32),
                    lax.dot_general(v_ref[ks, lanes(h)], do_b[h], nt, preferred_element_type=F32))

        def tile(h, j, dqt, dcq, s, dp, masked):
            ks = pl.ds(pl.multiple_of(j * tk, tk), tk)
            pr = jnp.exp(s * scale + cqs[h] - cc_ref[0, ks, h:h + 1] - lses[h])
            if masked:
                pr = jnp.where(diff <= 0, pr, 0.0)
            ds = pr * (dp - deltas[h])
            ds_b = ds.astype(BF16)
            dqt = dqt + jnp.dot(kt_ref[h, j], ds_b, preferred_element_type=F32)
            dk_ref[ks, lanes(h)] += jnp.dot(ds_b, qs[h], preferred_element_type=F32) * scale
            dv_ref[ks, lanes(h)] += jnp.dot(pr.astype(BF16), do_b[h], preferred_element_type=F32)
            dck_ref[0, ks, :] -= jnp.where(lane == h, jnp.sum(ds, axis=1, keepdims=True), 0.0)
            return dqt, dcq + jnp.sum(ds, axis=0, keepdims=True)

        def step(j, carry):
            nxt = [products(h, j + 1) for h in range(hp)]
            return tuple(tile(h, j, *carry[h], False) + nxt[h] for h in range(hp))

        init = tuple((jnp.zeros((HEAD_DIM, tq), F32), jnp.zeros((1, tq), F32)) + products(h, 0) for h in range(hp))
        carry = lax.fori_loop(0, i, step, init)
        for h in range(hp):
            dqt, dcq = tile(h, i, *carry[h], True)
            dq_ref[:, lanes(h)] = jnp.transpose(dqt) * scale
            dcq_ref[h, 0] = dcq

    w = hp * HEAD_DIM
    head_all = pl.BlockSpec((t, w), lambda h, i: (0, h))
    qblk = pl.BlockSpec((tq, w), lambda h, i: (i, h))
    colv = pl.BlockSpec((1, t, HEAD_DIM), lambda h, i: (h, 0, 0))
    rows_all = pl.BlockSpec((hp, t // tk, 1, tk), lambda h, i: (h, 0, 0, 0))
    row_blk = pl.BlockSpec((hp, 1, 1, tq), lambda h, i: (h, i, 0, 0))
    wide = jax.ShapeDtypeStruct((t, nf * HEAD_DIM), F32)
    return pl.pallas_call(
        body, name="fox_bwd", grid=(nf // hp, t // tq),
        in_specs=[qblk, head_all, pl.BlockSpec((hp, t // tk, HEAD_DIM, tk), lambda h, i: (h, 0, 0, 0)), head_all,
                  colv, rows_all, qblk, row_blk, qblk],
        out_specs=[qblk, head_all, head_all, row_blk, colv],
        out_shape=[wide, wide, wide, jax.ShapeDtypeStruct((nf, t // tq, 1, tq), F32),
                   jax.ShapeDtypeStruct((nf // hp, t, HEAD_DIM), F32)],
        compiler_params=_cparams(("parallel", "arbitrary")),
    )(q, k, kt, v, cc, cr, o, lse, dmix)


def _mem_fn(mq, mk, mv, gq, gk):
    qn = _rms_fn(mq, gq)
    kn = _rms_fn(mk, gk)
    s = _nt(qn, kn) * (HEAD_DIM ** -0.5)
    e = jnp.exp(s - lax.stop_gradient(jnp.max(s, axis=1, keepdims=True)))
    pr = e / jnp.sum(e, axis=1, keepdims=True)
    return _nn(pr, mv)


def _mem_specs(t, m, tq, qoff):
    qblk = pl.BlockSpec((tq, HEAD_DIM), lambda h, i: (i, qoff + h))
    kblk = pl.BlockSpec((m, HEAD_DIM), lambda h, i: (0, h))
    vblk = pl.BlockSpec((m, HEAD_DIM), lambda h, i: (0, N_MEM_HEADS + h))
    row = pl.BlockSpec((1, HEAD_DIM), lambda h, i: (0, 0))
    return qblk, kblk, vblk, row


def _mem_fwd(p, qoff, mkv, gq, gk, tq, into, into_off):
    t, m = p.shape[0], mkv.shape[0]
    qblk, kblk, vblk, row = _mem_specs(t, m, tq, qoff)

    def body(q_ref, k_ref, v_ref, gq_ref, gk_ref, _, o_ref):
        o_ref[...] = _mem_fn(q_ref[...], k_ref[...], v_ref[...], gq_ref[...], gk_ref[...]).astype(BF16)

    return pl.pallas_call(
        body, name="mem_fwd", grid=(N_MEM_HEADS, t // tq),
        in_specs=[qblk, kblk, vblk, row, row, pl.BlockSpec(memory_space=pl.ANY)],
        out_specs=pl.BlockSpec((tq, HEAD_DIM), lambda h, i: (i, into_off + h)),
        out_shape=jax.ShapeDtypeStruct(into.shape, BF16), input_output_aliases={5: 0},
        compiler_params=_cparams(("parallel", "parallel")),
    )(p, mkv, mkv, gq, gk, into)


def _mem_bwd(p, qoff, mkv, gq, gk, dmix, dooff, tq):
    t, m = p.shape[0], mkv.shape[0]
    qblk, kblk, vblk, row = _mem_specs(t, m, tq, qoff)

    def body(q_ref, k_ref, v_ref, gq_ref, gk_ref, do_ref, dq_ref, dkv_k_ref, dkv_v_ref, dgq_ref, dgk_ref):
        h, i = pl.program_id(0), pl.program_id(1)

        @pl.when((h == 0) & (i == 0))
        def _():
            dgq_ref[...] = jnp.zeros_like(dgq_ref)
            dgk_ref[...] = jnp.zeros_like(dgk_ref)

        @pl.when(i == 0)
        def _():
            dkv_k_ref[...] = jnp.zeros_like(dkv_k_ref)
            dkv_v_ref[...] = jnp.zeros_like(dkv_v_ref)

        _, vjp = jax.vjp(_mem_fn, q_ref[...], k_ref[...], v_ref[...], gq_ref[...], gk_ref[...])
        dq, dk, dv, dgq, dgk = vjp(do_ref[...])
        dq_ref[...] = dq
        dkv_k_ref[...] += dk
        dkv_v_ref[...] += dv
        dgq_ref[...] += dgq
        dgk_ref[...] += dgk

    oblk = pl.BlockSpec((tq, HEAD_DIM), lambda h, i: (i, h))
    kout = pl.BlockSpec((m, HEAD_DIM), lambda h, i: (0, h))
    half = jax.ShapeDtypeStruct((m, N_MEM_HEADS * HEAD_DIM), F32)
    rshape = jax.ShapeDtypeStruct((1, HEAD_DIM), F32)
    return pl.pallas_call(
        body, name="mem_bwd", grid=(N_MEM_HEADS, t // tq),
        in_specs=[qblk, kblk, vblk, row, row, pl.BlockSpec((tq, HEAD_DIM), lambda h, i: (i, dooff + h))],
        out_specs=[oblk, kout, kout, row, row],
        out_shape=[jax.ShapeDtypeStruct((t, N_MEM_HEADS * HEAD_DIM), F32), half, half, rshape, rshape],
        compiler_params=_cparams(("arbitrary", "arbitrary")),
    )(p, mkv, mkv, gq, gk, dmix)


def _shift_down(x, s):
    if s == 0:
        return x
    r = lax.broadcasted_iota(jnp.int32, x.shape, 0)
    return jnp.where(r >= s, pltpu.roll(x, s, 0), 0.0)


def _shift_up(x, s):
    if s == 0:
        return x
    n = x.shape[0]
    r = lax.broadcasted_iota(jnp.int32, x.shape, 0)
    return jnp.where(r < n - s, pltpu.roll(x, n - s, 0), 0.0)


def _conv_fn(x0, x1, x2, x3, w0, w1, w2, w3, kind):
    y = _silu(x0 * w0 + x1 * w1 + x2 * w2 + x3 * w3)
    if kind == 2:
        return y
    y = y * lax.rsqrt(jnp.sum(y * y, axis=-1, keepdims=True) + NORM_EPS)
    return y * (HEAD_DIM ** -0.5) if kind == 0 else y


def _conv_fwd(p, off, conv_w, ng):
    t = p.shape[0]

    def body(x_ref, w_ref, o_ref):
        kind = pl.program_id(0) // ng
        x = x_ref[...]
        xs = [_shift_down(x, CONV_WIDTH - 1 - j) for j in range(CONV_WIDTH)]
        ws = [w_ref[j:j + 1, :] for j in range(CONV_WIDTH)]
        for kd in range(3):
            @pl.when(kind == kd)
            def _(kd=kd):
                o_ref[...] = _conv_fn(*xs, *ws, kd)

    return pl.pallas_call(
        body, name="gdn_conv_fwd", grid=(3 * ng,),
        in_specs=[pl.BlockSpec((t, HEAD_DIM), lambda c: (0, off + c)),
                  pl.BlockSpec((CONV_WIDTH, HEAD_DIM), lambda c: (0, c))],
        out_specs=pl.BlockSpec((t, HEAD_DIM), lambda c: (0, c)),
        out_shape=jax.ShapeDtypeStruct((t, 3 * ng * HEAD_DIM), F32),
        compiler_params=_cparams(("parallel",)),
    )(p, conv_w)


def _conv_bwd(p, off, conv_w, dys, ng):
    t = p.shape[0]

    def body(x_ref, w_ref, dq_ref, dk_ref, dv_ref, dx_ref, dw_ref):
        kind = pl.program_id(0) // ng
        dy_refs = (dq_ref, dk_ref, dv_ref)
        x = x_ref[...]
        xs = [_shift_down(x, CONV_WIDTH - 1 - j) for j in range(CONV_WIDTH)]
        ws = [w_ref[j:j + 1, :] for j in range(CONV_WIDTH)]
        for kd in range(3):
            @pl.when(kind == kd)
            def _(kd=kd):
                _, vjp = jax.vjp(functools.partial(_conv_fn, kind=kd), *xs, *ws)
                g = vjp(dy_refs[kd][...])
                dx = _shift_up(g[0], CONV_WIDTH - 1)
                for j in range(1, CONV_WIDTH):
                    dx = dx + _shift_up(g[j], CONV_WIDTH - 1 - j)
                dx_ref[...] = dx.astype(BF16)
                for j in range(CONV_WIDTH):
                    dw_ref[j:j + 1, :] = g[CONV_WIDTH + j]

    blk = pl.BlockSpec((t, HEAD_DIM), lambda c: (0, off + c))
    head = lambda k: pl.BlockSpec((t, HEAD_DIM), lambda c: (0, jnp.where(c // ng == k, c % ng, 0)))
    wblk = pl.BlockSpec((CONV_WIDTH, HEAD_DIM), lambda c: (0, c))
    return pl.pallas_call(
        body, name="gdn_conv_bwd", grid=(3 * ng,),
        in_specs=[blk, wblk] + [head(k) for k in range(3)],
        out_specs=[blk, wblk],
        out_shape=[jax.ShapeDtypeStruct(p.shape, BF16),
                   jax.ShapeDtypeStruct((CONV_WIDTH, 3 * ng * HEAD_DIM), F32)],
        compiler_params=_cparams(("parallel",)),
    )(p, conv_w, *dys)


def _lower_inverse(lower):
    c = lower.shape[-1]
    r = lax.broadcasted_iota(jnp.int32, (1, c, c), 1)
    e = lax.broadcasted_iota(jnp.int32, (1, c, c), 2)
    hi = lax.Precision.HIGH
    inv = jnp.where(r == e, 1.0, 0.0) - lower
    pw = lower
    for _ in range(int(math.log2(c)) - 1):
        pw = _dot(pw, pw, ((1,), (0,)), hi)
        inv = inv + _dot(inv, pw, ((1,), (0,)), hi)
    return inv


@jax.custom_vjp
def _solve(lower, inv, vb, kbg):
    hi = lax.Precision.HIGH
    return _dot(inv, vb, ((1,), (0,)), hi), _dot(inv, kbg, ((1,), (0,)), hi)


def _solve_fwd(lower, inv, vb, kbg):
    u, w = _solve(lower, inv, vb, kbg)
    return (u, w), (inv, u, w)


def _solve_bwd(res, cts):
    inv, u, w = res
    dvb, dkbg = _tn(inv, cts[0]), _tn(inv, cts[1])
    return -(_nt(dvb, u) + _nt(dkbg, w)), jnp.zeros_like(inv), dvb, dkbg


_solve.defvjp(_solve_fwd, _solve_bwd)


def _wy_fn(q, k, v, gcol, grow, bcol, inv=None):
    b, c, dk = q.shape
    r = lax.broadcasted_iota(jnp.int32, (1, c, c), 1)
    e = lax.broadcasted_iota(jnp.int32, (1, c, c), 2)
    tril, strict = e <= r, e < r
    gc_col = jnp.sum(jnp.where(tril, grow, 0.0), axis=2, keepdims=True)
    gc_row = jnp.sum(jnp.where(r <= e, gcol, 0.0), axis=1, keepdims=True)
    g_last = jnp.sum(gcol, axis=1, keepdims=True)
    decay = jnp.exp(jnp.where(tril, gc_col - gc_row, NEG))
    kb, vb = k * bcol, v * bcol
    lower = jnp.where(strict, _nt(kb, k) * decay, 0.0)
    if inv is None:
        inv = _lower_inverse(lower)
    u, w = _solve(lower, inv, vb, kb * jnp.exp(gc_col))
    attn = jnp.where(tril, _nt(q, k) * decay, 0.0)
    qg = q * jnp.exp(gc_col)
    kdec = k * jnp.exp(g_last - gc_col)
    egl = jnp.broadcast_to(jnp.exp(g_last), (b, 1, dk))
    return u, w, qg, kdec, attn, egl, inv


def _scan_fn(u, w, qg, kdec, attn, egl, state):
    v_new = u - _nn(w, state)
    o = _nn(qg, state) + _nn(attn, v_new)
    return o, state * egl + _tn(kdec, v_new)


GDN_CHUNKS_PER_STEP = 4
GDN_SCAN_CHUNKS = 4


def _gdn_fwd(qkv, vals, grow, nf, ng):
    t = qkv.shape[0]
    nch = t // CHUNK

    cb = GDN_CHUNKS_PER_STEP
    *wy, inv = _gdn_wy(qkv, vals, grow, nf, ng, cb)

    sc = GDN_SCAN_CHUNKS

    def body(u_ref, w_ref, qg_ref, kd_ref, at_ref, eg_ref, o_ref, st_ref, state):
        @pl.when(pl.program_id(0) == 0)
        def _():
            state[...] = jnp.zeros_like(state)

        for c in range(sc):
            rows = slice(c * CHUNK, (c + 1) * CHUNK)
            heads = lambda ref: jnp.stack([ref[rows, h * HEAD_DIM:(h + 1) * HEAD_DIM] for h in range(ng)])
            st_ref[:, c] = state[...]
            o, new = _scan_fn(heads(u_ref), heads(w_ref), heads(qg_ref), heads(kd_ref), at_ref[:, c], eg_ref[:, c],
                              state[...])
            for h in range(ng):
                o_ref[rows, h * HEAD_DIM:(h + 1) * HEAD_DIM] = o[h]
            state[...] = new

    w = ng * HEAD_DIM
    blk = pl.BlockSpec((sc * CHUNK, w), lambda i: (i, 0))
    o, states = pl.pallas_call(
        body, name="gdn_scan_fwd", grid=(nch // sc,),
        in_specs=[blk, blk, blk, blk, pl.BlockSpec((ng, sc, CHUNK, CHUNK), lambda i: (0, i, 0, 0)),
                  pl.BlockSpec((ng, sc, 1, HEAD_DIM), lambda i: (0, i, 0, 0))],
        out_specs=[blk, pl.BlockSpec((ng, sc, HEAD_DIM, HEAD_DIM), lambda i: (0, i, 0, 0))],
        out_shape=[jax.ShapeDtypeStruct((t, w), F32),
                   jax.ShapeDtypeStruct((ng, nch, HEAD_DIM, HEAD_DIM), F32)],
        scratch_shapes=[pltpu.VMEM((ng, HEAD_DIM, HEAD_DIM), F32)],
        compiler_params=_cparams(("arbitrary",)),
    )(*wy)
    return o, (wy, inv, states)


def _wy_batch(q_ref, k_ref, v_ref, vals_ref, gr_ref, nf, ng, cb):
    idx = [(c, h) for c in range(cb) for h in range(ng)]
    rows = lambda c: slice(c * CHUNK, (c + 1) * CHUNK)
    lanes = lambda h: slice(h * HEAD_DIM, (h + 1) * HEAD_DIM)
    wide = lambda ref: jnp.stack([ref[rows(c), lanes(h)] for c, h in idx])
    col = lambda lane0: jnp.stack([vals_ref[rows(c), lane0 + h:lane0 + h + 1] for c, h in idx])
    return idx, (wide(q_ref), wide(k_ref), wide(v_ref), col(nf), jnp.stack([gr_ref[h, c] for c, h in idx]),
                 col(nf + ng))


def _gdn_wy(qkv, vals, grow, nf, ng, cb):
    t = qkv.shape[0]
    nch = t // CHUNK

    def body(q_ref, k_ref, v_ref, vals_ref, gr_ref, u_ref, w_ref, qg_ref, kd_ref, at_ref, eg_ref, inv_ref):
        idx, args = _wy_batch(q_ref, k_ref, v_ref, vals_ref, gr_ref, nf, ng, cb)
        u, w, qg, kd, at, eg, inv = _wy_fn(*args)
        for b, (c, h) in enumerate(idx):
            rows, lanes = slice(c * CHUNK, (c + 1) * CHUNK), slice(h * HEAD_DIM, (h + 1) * HEAD_DIM)
            u_ref[rows, lanes] = u[b]
            w_ref[rows, lanes] = w[b]
            qg_ref[rows, lanes] = qg[b]
            kd_ref[rows, lanes] = kd[b]
            at_ref[h, c] = at[b]
            eg_ref[h, c] = eg[b]
            inv_ref[h, c] = inv[b]

    wd = ng * HEAD_DIM
    blk = lambda o: pl.BlockSpec((cb * CHUNK, wd), lambda i: (i, o))
    col = pl.BlockSpec((cb * CHUNK, HEAD_DIM), lambda i: (i, 0))
    sq = pl.BlockSpec((ng, cb, CHUNK, CHUNK), lambda i: (0, i, 0, 0))
    wide = jax.ShapeDtypeStruct((t, wd), F32)
    sq_shape = jax.ShapeDtypeStruct((ng, nch, CHUNK, CHUNK), F32)
    return pl.pallas_call(
        body, name="gdn_wy_fwd", grid=(nch // cb,),
        in_specs=[blk(0), blk(1), blk(2), col, pl.BlockSpec((ng, cb, 1, CHUNK), lambda i: (0, i, 0, 0))],
        out_specs=[blk(0), blk(0), blk(0), blk(0), sq, pl.BlockSpec((ng, cb, 1, HEAD_DIM), lambda i: (0, i, 0, 0)),
                   sq],
        out_shape=[wide, wide, wide, wide, sq_shape, jax.ShapeDtypeStruct((ng, nch, 1, HEAD_DIM), F32), sq_shape],
        compiler_params=_cparams(("parallel",)),
    )(qkv, qkv, qkv, vals, grow)


def _gdn_bwd(qkv, vals, grow, saved, do, nf, ng):
    t = qkv.shape[0]
    nch = t // CHUNK
    cb = GDN_CHUNKS_PER_STEP // 2
    wy, inv, states = saved
    wd = ng * HEAD_DIM

    def scan_body(u_ref, w_ref, qg_ref, kd_ref, at_ref, eg_ref, st_ref, do_ref,
                  du_ref, dw_ref, dqg_ref, dkd_ref, dat_ref, deg_ref, dstate):
        @pl.when(pl.program_id(0) == 0)
        def _():
            dstate[...] = jnp.zeros_like(dstate)

        for c in reversed(range(sc)):
            rows = slice(c * CHUNK, (c + 1) * CHUNK)
            heads = lambda ref: jnp.stack([ref[rows, h * HEAD_DIM:(h + 1) * HEAD_DIM] for h in range(ng)])
            _, vjp = jax.vjp(_scan_fn, heads(u_ref), heads(w_ref), heads(qg_ref), heads(kd_ref), at_ref[:, c],
                             eg_ref[:, c], st_ref[:, c])
            du, dw, dqg, dkd, dat, deg, dst = vjp((heads(do_ref), dstate[...]))
            for h in range(ng):
                lanes = slice(h * HEAD_DIM, (h + 1) * HEAD_DIM)
                du_ref[rows, lanes] = du[h]
                dw_ref[rows, lanes] = dw[h]
                dqg_ref[rows, lanes] = dqg[h]
                dkd_ref[rows, lanes] = dkd[h]
            dat_ref[:, c] = dat
            deg_ref[:, c] = deg
            dstate[...] = dst

    sc = GDN_SCAN_CHUNKS
    rev = lambda i: nch // sc - 1 - i
    blk = pl.BlockSpec((sc * CHUNK, wd), lambda i: (rev(i), 0))
    atb = pl.BlockSpec((ng, sc, CHUNK, CHUNK), lambda i: (0, rev(i), 0, 0))
    egb = pl.BlockSpec((ng, sc, 1, HEAD_DIM), lambda i: (0, rev(i), 0, 0))
    wide = jax.ShapeDtypeStruct((t, wd), F32)
    at_shape = jax.ShapeDtypeStruct((ng, nch, CHUNK, CHUNK), F32)
    eg_shape = jax.ShapeDtypeStruct((ng, nch, 1, HEAD_DIM), F32)
    dwy = pl.pallas_call(
        scan_body, name="gdn_scan_bwd", grid=(nch // sc,),
        in_specs=[blk, blk, blk, blk, atb, egb,
                  pl.BlockSpec((ng, sc, HEAD_DIM, HEAD_DIM), lambda i: (0, rev(i), 0, 0)), blk],
        out_specs=[blk, blk, blk, blk, atb, egb],
        out_shape=[wide, wide, wide, wide, at_shape, eg_shape],
        scratch_shapes=[pltpu.VMEM((ng, HEAD_DIM, HEAD_DIM), F32)],
        compiler_params=_cparams(("arbitrary",)),
    )(*wy, states, do)

    def wy_body(q_ref, k_ref, v_ref, vals_ref, gr_ref, du_ref, dw_ref, dqg_ref, dkd_ref, dat_ref, deg_ref,
                inv_ref, dq_ref, dk_ref, dv_ref, dvals_ref, dgr_ref):
        idx, args = _wy_batch(q_ref, k_ref, v_ref, vals_ref, gr_ref, nf, ng, cb)
        lane = lax.broadcasted_iota(jnp.int32, (CHUNK, HEAD_DIM), 1)
        kept = jnp.stack([inv_ref[h, c] for c, h in idx])
        rows = lambda c: slice(c * CHUNK, (c + 1) * CHUNK)
        lanes = lambda h: slice(h * HEAD_DIM, (h + 1) * HEAD_DIM)
        wide_ct = lambda ref: jnp.stack([ref[rows(c), lanes(h)] for c, h in idx])
        cts = (wide_ct(du_ref), wide_ct(dw_ref), wide_ct(dqg_ref), wide_ct(dkd_ref),
               jnp.stack([dat_ref[h, c] for c, h in idx]), jnp.stack([deg_ref[h, c] for c, h in idx]))
        _, vjp = jax.vjp(lambda *a: _wy_fn(*a, inv=kept)[:6], *args)
        dq, dk, dv, dgc, dgr, dbc = vjp(cts)
        for b, (c, h) in enumerate(idx):
            dq_ref[rows(c), lanes(h)] = dq[b]
            dk_ref[rows(c), lanes(h)] = dk[b]
            dv_ref[rows(c), lanes(h)] = dv[b]
            dgr_ref[h, c] = dgr[b]
        for c in range(cb):
            acc = jnp.zeros((CHUNK, HEAD_DIM), F32)
            for h in range(ng):
                acc = jnp.where(lane == nf + h, dgc[c * ng + h], acc)
                acc = jnp.where(lane == nf + ng + h, dbc[c * ng + h], acc)
            dvals_ref[rows(c), :] = acc

    cblk = lambda o: pl.BlockSpec((cb * CHUNK, wd), lambda i: (i, o))
    col = pl.BlockSpec((cb * CHUNK, HEAD_DIM), lambda i: (i, 0))
    rowv = pl.BlockSpec((ng, cb, 1, CHUNK), lambda i: (0, i, 0, 0))
    return pl.pallas_call(
        wy_body, name="gdn_wy_bwd", grid=(nch // cb,),
        in_specs=[cblk(0), cblk(1), cblk(2), col, rowv, cblk(0), cblk(0), cblk(0), cblk(0),
                  pl.BlockSpec((ng, cb, CHUNK, CHUNK), lambda i: (0, i, 0, 0)),
                  pl.BlockSpec((ng, cb, 1, HEAD_DIM), lambda i: (0, i, 0, 0)),
                  pl.BlockSpec((ng, cb, CHUNK, CHUNK), lambda i: (0, i, 0, 0))],
        out_specs=[cblk(0), cblk(0), cblk(0), col, rowv],
        out_shape=[wide, wide, wide, jax.ShapeDtypeStruct((t, HEAD_DIM), F32),
                   jax.ShapeDtypeStruct((ng, nch, 1, CHUNK), F32)],
        compiler_params=_cparams(("parallel",)),
    )(qkv, qkv, qkv, vals, grow, *dwy, inv)


def _swiglu_fn(gate, up):
    return _silu(gate) * up


FFN_TN = 512


def _ffn_up(n2, wgu4):
    _, d, w = wgu4.shape
    t = n2.shape[0]
    tn = _tile(w, FFN_TN)
    nb = w // tn

    def body(a_ref, b_ref, gu_ref, act_ref):
        av = a_ref[...]
        gate = jnp.dot(av, b_ref[0], preferred_element_type=F32)
        up = jnp.dot(av, b_ref[1], preferred_element_type=F32)
        gu_ref[0] = gate.astype(BF16)
        gu_ref[1] = up.astype(BF16)
        act_ref[...] = _swiglu_fn(gate, up).astype(BF16)

    return pl.pallas_call(
        body, name="ffn_up", grid=(2, nb),
        in_specs=[pl.BlockSpec((t, d), lambda j, l: (0, 0)), pl.BlockSpec((2, d, tn), lambda j, l: (j, 0, l))],
        out_specs=[pl.BlockSpec((2, t, tn), lambda j, l: (j, 0, l)),
                   pl.BlockSpec((t, tn), lambda j, l: (0, j * nb + l))],
        out_shape=[jax.ShapeDtypeStruct((4, t, w), BF16), jax.ShapeDtypeStruct((t, 2 * w), BF16)],
        compiler_params=_cparams(("parallel", "parallel")),
    )(n2, wgu4)


def _ffn_dact(dh2, wd, gu, after):
    _, t, w = gu.shape
    d = dh2.shape[1]
    tn = _tile(w, FFN_TN)
    nb = w // tn

    def body(a_ref, b_ref, gu_ref, _, o_ref):
        dact = lax.dot_general(a_ref[...], b_ref[...], (((1,), (1,)), ((), ())), preferred_element_type=F32)
        _, vjp = jax.vjp(_swiglu_fn, gu_ref[0].astype(F32), gu_ref[1].astype(F32))
        dg, du = vjp(dact)
        o_ref[0] = dg.astype(BF16)
        o_ref[1] = du.astype(BF16)

    pair = pl.BlockSpec((2, t, tn), lambda j, l: (j, 0, l))
    return pl.pallas_call(
        body, name="ffn_dact", grid=(2, nb),
        in_specs=[pl.BlockSpec((t, d), lambda j, l: (0, 0)), pl.BlockSpec((tn, d), lambda j, l: (j * nb + l, 0)),
                  pair, pl.BlockSpec(after.shape, lambda j, l: (0, 0))],
        out_specs=pair, out_shape=jax.ShapeDtypeStruct(gu.shape, BF16),
        compiler_params=_cparams(("parallel", "parallel")),
    )(dh2, wd, gu, after)


def _loss_head(h2, target):
    t, d = h2.shape
    tr = _tile(t, 256, 8)

    def body(h_ref, t_ref, l_ref, d_ref, db_ref):
        @pl.when(pl.program_id(0) == 0)
        def _():
            l_ref[...] = jnp.zeros_like(l_ref)

        err = h_ref[...] - t_ref[...]
        d_ref[...] = err * (1.0 / d)
        db_ref[...] = (err * (1.0 / d)).astype(BF16)
        part = 0.5 * jnp.sum(jnp.mean(err * err, axis=-1, keepdims=True), axis=0, keepdims=True)
        lane = lax.broadcasted_iota(jnp.int32, (8, HEAD_DIM), 1)
        row = lax.broadcasted_iota(jnp.int32, (8, HEAD_DIM), 0)
        l_ref[...] += jnp.where((lane == 0) & (row == 0), part, 0.0)

    blk = pl.BlockSpec((tr, d), lambda r: (r, 0))
    return pl.pallas_call(
        body, name="loss_head", grid=(t // tr,), in_specs=[blk, blk],
        out_specs=[pl.BlockSpec((8, HEAD_DIM), lambda r: (0, 0)), blk, blk],
        out_shape=[jax.ShapeDtypeStruct((8, HEAD_DIM), F32), jax.ShapeDtypeStruct((t, d), F32),
                   jax.ShapeDtypeStruct((t, d), BF16)],
        compiler_params=_cparams(("arbitrary",)),
    )(h2, target)


def _adamw(w, g, m, v, *, g_fn=None, name):
    r, c = w.shape
    tr = _tile(r, max(8, (1 << 19) // c // 8 * 8), 8)
    gs = g if isinstance(g, tuple) else (g,)

    def body(w_ref, *refs):
        g_refs, (m_ref, v_ref, go_ref, d_ref, mo_ref, vo_ref) = refs[:len(gs)], refs[len(gs):]
        gr = g_refs[0][...] if g_fn is None else g_fn(*[ref[...] for ref in g_refs])
        mn = ADAM_B1 * m_ref[...] + (1.0 - ADAM_B1) * gr
        vn = ADAM_B2 * v_ref[...] + (1.0 - ADAM_B2) * (gr * gr)
        m_hat = mn / (1.0 - ADAM_B1 ** ADAM_STEP)
        v_hat = vn / (1.0 - ADAM_B2 ** ADAM_STEP)
        go_ref[...] = gr
        d_ref[...] = -ADAM_LR * (m_hat / (jnp.sqrt(v_hat) + ADAM_EPS) + ADAM_WD * w_ref[...])
        mo_ref[...] = mn
        vo_ref[...] = vn

    blk = pl.BlockSpec((tr, c), lambda i: (i, 0))
    gblks = [pl.BlockSpec((tr, gi.shape[1]), lambda i: (i, 0)) for gi in gs]
    return pl.pallas_call(
        body, name=name, grid=(r // tr,), in_specs=[blk] + gblks + [blk, blk], out_specs=[blk] * 4,
        out_shape=[jax.ShapeDtypeStruct((r, c), F32)] * 4,
        compiler_params=_cparams(("parallel",)),
    )(w, *gs, m, v)


class _Layout:
    def __init__(self, d):
        nh = d // HEAD_DIM
        self.nm = N_MEM_HEADS
        self.nf = (nh - self.nm) // 2
        self.ng = nh - self.nm - self.nf
        nf, ng, nm, hd = self.nf, self.ng, self.nm, HEAD_DIM
        self.o_fq, self.o_fk, self.o_fv, self.o_sm = 0, nf, 2 * nf, 3 * nf
        self.o_gq, self.o_gz, self.o_mq = 0, 3 * ng, 4 * ng
        self.cols_a = -(-(3 * nf + 1) // 4) * 4 * hd
        self.cols_b = -(-(4 * ng + nm) // 4) * 4 * hd
        self.cols = self.cols_a + self.cols_b
        sizes = [nf * hd, nf * hd, nf * hd, nf, 3 * ng * hd, ng * hd, ng, ng, nm * hd]
        starts = [sum(sizes[:i]) for i in range(len(sizes))]
        self.ref = list(zip(starts, sizes))
        self.in_cols = sum(sizes)

    def regroup(self, w):
        part = lambda i: w[:, self.ref[i][0]:self.ref[i][0] + self.ref[i][1]]
        a = [part(0), part(1), part(2), part(3), part(6), part(7)]
        b = [part(4), part(5), part(8)]
        pads = [self.cols_a - sum(p.shape[1] for p in a), self.cols_b - sum(p.shape[1] for p in b)]
        fill = [[jnp.zeros((w.shape[0], n), w.dtype)] if n else [] for n in pads]
        return jnp.concatenate(a + fill[0] + b + fill[1], axis=1)

    def ungroup(self, g):
        hd, nf, ng, nm = HEAD_DIM, self.nf, self.ng, self.nm
        sm, b0 = self.o_sm * hd, self.cols_a
        return jnp.concatenate([
            g[:, :3 * nf * hd], g[:, sm:sm + nf], g[:, b0:b0 + 3 * ng * hd],
            g[:, b0 + self.o_gz * hd:b0 + self.o_mq * hd], g[:, sm + nf:sm + nf + ng],
            g[:, sm + nf + ng:sm + nf + 2 * ng], g[:, b0 + self.o_mq * hd:b0 + (self.o_mq + nm) * hd]], axis=1)


def _lane_row(pieces):
    row = jnp.zeros((1, HEAD_DIM), F32)
    for off, a in pieces:
        row = lax.dynamic_update_slice(row, a.astype(F32), (0, off))
    return row


def _local_step(x, mem, target, prefetch, weights, reducer, sp):
    t, d = x.shape
    lay = _Layout(d)
    nf, ng, nm, hd = lay.nf, lay.ng, lay.nm, HEAD_DIM
    nch = t // CHUNK
    tq = _tile(t, 256)
    tk = tq

    u = _norm_fwd(x, 0, sp["norm_mix"], 1, d, BF16, name="norm_mix_fwd")
    prefetch("in_a", u)
    (win_a,) = weights("in_a", u)
    p_a = _mm(u, win_a, name="mm_in_a")
    pa = _lane_row([(nf, sp["gdn_a_log"])])
    pb = _lane_row([(0, sp["fox_f_bias"]), (nf, sp["gdn_dt_bias"])])
    vals, csum = _small_fwd(p_a, lay.o_sm, pa, pb, nf, ng)

    c_t = csum[:, :nf].T
    hp = _fox_heads(nf, 3)
    cr = c_t.reshape(nf, t // tk, 1, tk)
    cc = jnp.stack([jnp.pad(csum[:, g * hp:(g + 1) * hp], ((0, 0), (0, hd - hp))) for g in range(nf // hp)])
    fq = _norm_fwd(p_a, lay.o_fq, sp["fox_q_norm"], nf, hd, BF16, name="fox_qnorm_fwd")
    fk = _norm_fwd(p_a, lay.o_fk, sp["fox_k_norm"], nf, hd, BF16, name="fox_knorm_fwd")
    fv = p_a[:, lay.o_fv * hd:(lay.o_fv + nf) * hd].astype(BF16)
    o_fox, lse, mix = _fox_fwd(fq, fk, fv, cc, cr, nf, tq, tk, d)

    prefetch("in_b", lse)
    (win_b,) = weights("in_b", lse)
    prefetch("mixer", win_b)
    p = _mm(u, win_b, name="mm_in_b")
    wmkv, conv_taps = weights("mixer", p)
    sp = dict(sp, gdn_conv=conv_taps)
    qkv = _conv_fwd(p, lay.o_gq, sp["gdn_conv"], ng)
    grow = vals[:, nf:nf + ng].T.reshape(ng, nch, 1, CHUNK)
    o_g, states = _gdn_fwd(qkv, vals, grow, nf, ng)
    mix = _norm_fwd(o_g, 0, sp["gdn_out_norm"], ng, hd, BF16, z=p, zoff=lay.o_gz, into=mix, into_off=nf,
                    name="gdn_out_fwd")
    prefetch("out", mix)

    mem_n = _norm_fwd(mem, 0, sp["mem_norm"], 1, d, BF16, name="mem_norm_fwd")
    mkv = _mm(mem_n, wmkv, name="mm_memkv")
    tq_mem = _tile(t, 1024)
    mix = _mem_fwd(p, lay.o_mq, mkv, sp["mem_q_norm"], sp["mem_k_norm"], tq_mem, mix, nf + ng)
    prefetch("gate_up", mix)
    (wout,) = weights("out", mix)
    h1 = _mm(mix, wout, res=x, name="mm_out")
    n2 = _norm_fwd(h1, 0, sp["norm_ffn"], 1, d, BF16, name="norm_ffn_fwd")
    (wgu,) = weights("gate_up", n2)
    wgu4 = wgu.reshape(4, d, -1)
    gu, act = _ffn_up(n2, wgu4)
    prefetch("down", act)
    (wd,) = weights("down", act)
    h2 = _mm(act, wd, res=h1, name="mm_down")
    loss_blk, dh2, dh2_b = _loss_head(h2, target)

    g = {}
    token = reducer.pair("w_down", _mm(act, dh2_b, ta=True, out_dtype=BF16, name="mm_dw_down"))
    dgu = _ffn_dact(dh2_b, wd, gu, token)
    dw_gate_up = _mm(n2, dgu, ta=True, stack="out", out_dtype=BF16, name="mm_dw_gate_up").reshape(wgu.shape)
    token = reducer.pair("w_gate_up", dw_gate_up)
    dn2 = _mm(dgu, wgu4, tb=True, stack="sum", after=token, name="mm_dn2")
    token = reducer.ship("ffn", ["w_down", "w_gate_up"], dn2)
    dh1, dh1_b, g["norm_ffn"] = _norm_bwd(h1, 0, sp["norm_ffn"] + token[0, 0], dn2, 0, 1, d, res=dh2,
                                          also_bf16=True, name="norm_ffn_bwd")
    token = reducer.pair("w_out", _mm(mix, dh1_b, ta=True, out_dtype=BF16, name="mm_dw_out"))
    dmix = _mm(dh1_b, wout, tb=True, after=token, name="mm_dmix")

    dmq, dmk, dmv, g["mem_q_norm"], g["mem_k_norm"] = _mem_bwd(
        p, lay.o_mq, mkv, sp["mem_q_norm"], sp["mem_k_norm"], dmix, nf + ng, tq_mem)
    dmkv = jnp.concatenate([dmk, dmv], axis=1)
    token = reducer.pair("w_mem_kv", _mm(mem_n, dmkv, ta=True, out_dtype=BF16, name="mm_dw_memkv"))
    dmem_n = _mm(dmkv, wmkv, tb=True, after=token, name="mm_dmem")
    token = reducer.ship("mix", ["w_out", "w_mem_kv"], dmem_n)
    _, g["mem_norm"] = _norm_bwd(mem, 0, sp["mem_norm"], dmem_n, 0, 1, d, name="mem_norm_bwd")

    do_g, dgz, g["gdn_out_norm"] = _norm_bwd(o_g, 0, sp["gdn_out_norm"] + token[0, 0], dmix, nf, ng, hd, z=p,
                                             zoff=lay.o_gz, name="gdn_out_bwd")
    dq, dk, dv, dvals, dgr = _gdn_bwd(qkv, vals, grow, states, do_g, nf, ng)
    dgqkv, g["gdn_conv"] = _conv_bwd(p, lay.o_gq, sp["gdn_conv"], (dq, dk, dv), ng)

    dfq_n, dfk_n, dfv, dcc, dcr = _fox_bwd(fq, fk, fv, cc, cr, o_fox, lse, dmix, nf, tq, tk)
    dfq, g["fox_q_norm"] = _norm_bwd(p_a, lay.o_fq, sp["fox_q_norm"], dfq_n, 0, nf, hd, out_dtype=BF16,
                                     name="fox_qnorm_bwd")
    dfk, g["fox_k_norm"] = _norm_bwd(p_a, lay.o_fk, sp["fox_k_norm"], dfk_n, 0, nf, hd, out_dtype=BF16,
                                     name="fox_knorm_bwd")
    dc = dcc.reshape(nf, t).T + jnp.concatenate([dcr[g, :, :hp] for g in range(nf // hp)], axis=1)

    dvals = dvals + jnp.pad(dgr.reshape(ng, t).T, ((0, 0), (nf, hd - nf - ng)))
    dcsum = jnp.pad(dc, ((0, 0), (0, hd - nf)))
    dsm, dpa, dpb = _small_bwd(p_a, lay.o_sm, pa, pb, dvals, dcsum, nf, ng)
    g["fox_f_bias"] = dpb[:, :nf]
    g["gdn_dt_bias"] = dpb[:, nf:nf + ng]
    g["gdn_a_log"] = dpa[:, nf:nf + ng]

    dp_a = jnp.concatenate([dfq, dfk, dfv.astype(BF16), dsm.astype(BF16),
                            jnp.zeros((t, lay.cols_a - (lay.o_sm + 1) * hd), BF16)], axis=1)
    assert lay.o_gq == 0 and lay.o_gz == 3 * ng and lay.o_mq == lay.o_gz + ng
    rest = jnp.concatenate([dgz.astype(BF16), dmq.astype(BF16),
                            jnp.zeros((t, lay.cols_b - (lay.o_mq + nm) * hd), BF16)], axis=1)
    dp_b = lax.dynamic_update_slice(dgqkv, rest, (0, lay.o_gz * hd))
    token = reducer.pair("w_in_a", _mm(u, dp_a, ta=True, out_dtype=BF16, name="mm_dw_in_a"))
    token = reducer.pair("w_in_b", _mm(u, dp_b, ta=True, out_dtype=BF16, after=token, name="mm_dw_in_b"))
    du = _mm(dp_a, win_a, tb=True, after=token, name="mm_du_a")
    token = reducer.ship("in", ["w_in_a", "w_in_b"], du)
    du = _mm(dp_b, win_b, tb=True, res=du, after=token, name="mm_du_b")
    dx, g["norm_mix"] = _norm_bwd(x, 0, sp["norm_mix"], du, 0, 1, d, res=dh1, name="norm_mix_bwd")
    return loss_blk, dx, g


ANY = pl.BlockSpec(memory_space=pl.ANY)


def _me():
    x, y, c = lax.axis_index("x"), lax.axis_index("y"), lax.axis_index("c")
    chips = [(1 - x, y), (x, 1 - y), (1 - x, 1 - y)]
    return x, y, c, chips


def _slot(axis, k):
    return k if axis == 0 else 2 * (k % 2) + k // 2


def _slab(ref, axis, rows, cols, k, h):
    half = rows // 2
    return ref.at[pl.ds(_slot(axis, k) * rows + h * half, half), :]


def _remote(src, dst, send_sem, recv_sem, dev):
    return pltpu.make_async_remote_copy(src_ref=src, dst_ref=dst, send_sem=send_sem, recv_sem=recv_sem,
                                        device_id=dev, device_id_type=MESH)


HBM = pl.BlockSpec(memory_space=pltpu.HBM)
SEM = pl.BlockSpec(memory_space=pltpu.SEMAPHORE)
SPLIT = pltpu.CompilerParams(has_side_effects=pltpu.SideEffectType.DATAFLOW_SIDE_EFFECTING)
TOKEN = jax.ShapeDtypeStruct((8, HEAD_DIM), F32)


def _in_hbm(v):
    return pltpu.with_memory_space_constraint(v, pltpu.HBM)


def _cast_place(shard, axis, name, col_fn=None, out_cols=None, after=None):
    r, c = shard.shape
    oc = out_cols or c
    tr = _tile(r, 512 if col_fn is None else 64, 16)
    tc = _tile(c, 2048) if col_fn is None else c
    otc = tc if col_fn is None else oc
    nb = r // tr
    chip = 2 * lax.axis_index("x") + lax.axis_index("y")
    slot = jnp.reshape(_slot(axis, chip), (1,)).astype(jnp.int32)

    def body(slot_ref, x_ref, *rest):
        x = x_ref[...]
        rest[-1][...] = (x if col_fn is None else col_fn(x)).astype(BF16)

    extra = [] if after is None else [after]
    return pl.pallas_call(
        body, name=name,
        grid_spec=pltpu.PrefetchScalarGridSpec(
            num_scalar_prefetch=1, grid=(nb, c // tc),
            in_specs=[pl.BlockSpec((tr, tc), lambda i, l, s: (i, l))] + [ANY] * len(extra),
            out_specs=pl.BlockSpec((tr, otc), lambda i, l, s: (s[0] * nb + i, l))),
        out_shape=jax.ShapeDtypeStruct((4 * r, oc), BF16),
        compiler_params=_cparams(("parallel", "parallel")),
    )(slot, shard, *extra)


def _gather_start(bufs, axes, shapes, groups, name):
    n = len(bufs)

    def body(*refs):
        dst = refs[n:2 * n]
        sems = refs[2 * n:2 * n + 2 * len(groups)]
        token = refs[-1]
        x, y, c, chips = _me()
        k = 2 * x + y
        for gi, ws in enumerate(groups):
            for i, w in enumerate(ws):
                r, cl = shapes[w]
                place = _slab(dst[w], axes[w], r, cl, k, c)
                for j, (px, py) in enumerate(chips):
                    _remote(place, place, sems[2 * gi].at[3 * i + j], sems[2 * gi + 1].at[3 * i + j],
                            (px, py, c)).start()
        token[...] = jnp.zeros_like(token)

    sem_shapes = [pltpu.SemaphoreType.DMA((3 * len(ws),)) for ws in groups for _ in range(2)]
    outs = pl.pallas_call(
        body, name=name, in_specs=[HBM] * n,
        out_specs=[HBM] * n + [SEM] * len(sem_shapes) + [pl.BlockSpec(memory_space=pltpu.VMEM)],
        out_shape=[pltpu.HBM(b.shape, b.dtype) for b in bufs] + sem_shapes + [TOKEN],
        input_output_aliases={w: w for w in range(n)}, compiler_params=SPLIT,
    )(*[_in_hbm(b) for b in bufs])
    sems = outs[n:-1]
    return outs[:n], [(sems[2 * g], sems[2 * g + 1]) for g in range(len(groups))], outs[-1]


def _gather_wait(bufs, axes, shapes, sems, after, name):
    n = len(bufs)

    def body(*refs):
        send_sems, recv_sems = refs[n], refs[n + 1]
        dst = refs[n + 3:]
        x, y, c, chips = _me()
        k = 2 * x + y
        for i in range(n):
            r, cl = shapes[i]
            for j, (px, py) in enumerate(chips):
                got = _slab(dst[i], axes[i], r, cl, 2 * px + py, c)
                _remote(got, got, send_sems.at[3 * i + j], recv_sems.at[3 * i + j], (px, py, c)).wait_recv()
        for i in range(n):
            r, cl = shapes[i]
            mine = _slab(dst[i], axes[i], r, cl, k, c)
            for j, (px, py) in enumerate(chips):
                _remote(mine, mine, send_sems.at[3 * i + j], recv_sems.at[3 * i + j], (px, py, c)).wait_send()

    return pl.pallas_call(
        body, name=name, in_specs=[HBM] * n + [SEM, SEM, ANY], out_specs=[HBM] * n,
        out_shape=[pltpu.HBM(b.shape, b.dtype) for b in bufs],
        input_output_aliases={i: i for i in range(n)}, compiler_params=SPLIT,
    )(*bufs, sems[0], sems[1], after)


def _split_start(name, arrays, geometry, count):
    n = len(arrays)

    def body(*refs):
        send, recv, token = refs[2 * n:]
        for i, (src, dst, _, dev) in enumerate(geometry(refs[n:2 * n])):
            _remote(src, dst, send.at[i], recv.at[i], dev).start()
        token[...] = jnp.zeros_like(token)

    sem = pltpu.SemaphoreType.DMA((count,))
    outs = pl.pallas_call(
        body, name=name, in_specs=[HBM] * n,
        out_specs=[HBM] * n + [SEM, SEM, pl.BlockSpec(memory_space=pltpu.VMEM)],
        out_shape=[pltpu.HBM(v.shape, v.dtype) for v in arrays] + [sem, sem, TOKEN],
        input_output_aliases={i: i for i in range(n)}, compiler_params=SPLIT,
    )(*[_in_hbm(v) for v in arrays])
    return list(outs[:n]), (outs[n], outs[n + 1]), outs[-1]


def _split_wait(name, arrays, sems, after, geometry):
    n = len(arrays)

    def body(*refs):
        send, recv = refs[n], refs[n + 1]
        copies = geometry(refs[n + 3:])
        for i, (_, _, land, dev) in enumerate(copies):
            _remote(land, land, send.at[i], recv.at[i], dev).wait_recv()
        for i, (src, _, _, dev) in enumerate(copies):
            _remote(src, src, send.at[i], recv.at[i], dev).wait_send()

    return list(pl.pallas_call(
        body, name=name, in_specs=[HBM] * n + [SEM, SEM, ANY], out_specs=[HBM] * n,
        out_shape=[pltpu.HBM(v.shape, v.dtype) for v in arrays],
        input_output_aliases={i: i for i in range(n)}, compiler_params=SPLIT,
    )(*arrays, sems[0], sems[1], after))


def _forward_geometry(axes, shapes):
    def geometry(bufs):
        x, y, c, chips = _me()
        out = []
        for i, buf in enumerate(bufs):
            r, cl = shapes[i]
            for px, py in chips:
                got = _slab(buf, axes[i], r, cl, 2 * px + py, c)
                out.append((got, got, _slab(buf, axes[i], r, cl, 2 * px + py, 1 - c), (x, y, 1 - c)))
        return out
    return geometry


def _pair_geometry(axes, shapes):
    def geometry(refs):
        n = len(refs) // 2
        x, y, c, _ = _me()
        out = []
        for w in range(n):
            r, cl = shapes[w]
            for j in range(4):
                land = refs[n + w].at[j]
                out.append((_slab(refs[w], axes[w], r, cl, j, 1 - c), land, land, (x, y, 1 - c)))
        return out
    return geometry


def _after_all(name, token, *arrays):
    def body(*refs):
        refs[-1][...] = jnp.zeros_like(refs[-1])

    return pl.pallas_call(
        body, name=name, in_specs=[ANY] * (1 + len(arrays)), out_specs=pl.BlockSpec(memory_space=pltpu.VMEM),
        out_shape=TOKEN,
    )(token, *arrays)


def _swap_geometry(bufs):
    x, y, c, _ = _me()
    return [(b.at[c], b.at[c], b.at[1 - c], (x, y, 1 - c)) for b in bufs]


def _chip_start(parts, tag):
    n = len(parts)

    def body(*refs):
        src, land = refs[2 * n:3 * n], refs[3 * n:4 * n]
        send_sems, recv_sems, token = refs[4 * n:]
        x, y, c, chips = _me()
        k = 2 * x + y
        for w in range(n):
            for j, (px, py) in enumerate(chips):
                _remote(src[w].at[2 * px + py], land[w].at[k], send_sems.at[3 * w + j], recv_sems.at[3 * w + j],
                        (px, py, c)).start()
        token[...] = jnp.zeros_like(token)

    lands = [lax.empty(p.shape, p.dtype) for p in parts]
    sem = pltpu.SemaphoreType.DMA((3 * n,))
    outs = pl.pallas_call(
        body, name="reduce_ici_start_" + tag, in_specs=[HBM] * (2 * n),
        out_specs=[HBM] * (2 * n) + [SEM, SEM, pl.BlockSpec(memory_space=pltpu.VMEM)],
        out_shape=[pltpu.HBM(p.shape, p.dtype) for p in parts + lands] + [sem, sem, TOKEN],
        input_output_aliases={i: i for i in range(2 * n)}, compiler_params=SPLIT,
    )(*[_in_hbm(v) for v in parts + lands])
    return outs[:n], outs[n:2 * n], outs[2 * n], outs[2 * n + 1], outs[-1]


def _chip_wait(parts, lands, send_sems, recv_sems, after, tag):
    n = len(parts)

    def body(*refs):
        send, recv = refs[2 * n], refs[2 * n + 1]
        src, land = refs[2 * n + 3:3 * n + 3], refs[3 * n + 3:]
        x, y, c, chips = _me()
        for w in range(n):
            for j, (px, py) in enumerate(chips):
                got = land[w].at[2 * px + py]
                _remote(got, got, send.at[3 * w + j], recv.at[3 * w + j], (px, py, c)).wait_recv()
        for w in range(n):
            for j, (px, py) in enumerate(chips):
                sent = src[w].at[2 * px + py]
                _remote(sent, sent, send.at[3 * w + j], recv.at[3 * w + j], (px, py, c)).wait_send()

    outs = pl.pallas_call(
        body, name="reduce_ici_wait_" + tag, in_specs=[HBM] * (2 * n) + [SEM, SEM, ANY], out_specs=[HBM] * (2 * n),
        out_shape=[pltpu.HBM(p.shape, p.dtype) for p in parts + lands],
        input_output_aliases={i: i for i in range(2 * n)}, compiler_params=SPLIT,
    )(*parts, *lands, send_sems, recv_sems, after)
    chip = 2 * lax.axis_index("x") + lax.axis_index("y")
    return [lax.dynamic_update_slice(s, lax.dynamic_index_in_dim(p, chip, 0, keepdims=True), (chip, 0, 0))
            for p, s in zip(outs[:n], outs[n:])]


def _half_swap(halves, tag):
    n = len(halves)
    core = lax.axis_index("c")
    bufs = [lax.dynamic_update_slice(lax.empty((2,) + h.shape, h.dtype), h[None], (core, 0, 0)) for h in halves]

    def body(*refs):
        dst = refs[n:2 * n]
        send_sems, recv_sems = refs[2 * n:]
        x, y, c, _ = _me()
        sibling = (x, y, 1 - c)
        cps = []
        for w in range(n):
            cp = _remote(dst[w].at[c], dst[w].at[c], send_sems.at[w], recv_sems.at[w], sibling)
            cp.start()
            cps.append(cp)
        for w in range(n):
            other = dst[w].at[1 - c]
            _remote(other, other, send_sems.at[w], recv_sems.at[w], sibling).wait_recv()
        for cp in cps:
            cp.wait_send()

    outs = pl.pallas_call(
        body, name="reduce_half_swap_" + tag, in_specs=[ANY] * n, out_specs=[ANY] * n,
        out_shape=[jax.ShapeDtypeStruct(b.shape, b.dtype) for b in bufs],
        input_output_aliases={w: w for w in range(n)},
        scratch_shapes=[pltpu.SemaphoreType.DMA((n,)), pltpu.SemaphoreType.DMA((n,))],
    )(*bufs)
    return [o.reshape(2 * o.shape[1], o.shape[2]) for o in outs]


def _add_parts(full, axis, rows, sib, name):
    _, r, c = sib.shape
    tr, tc = _tile(r, 1024, 16), _tile(c, 2048)
    nb = r // tr
    core = jnp.reshape(lax.axis_index("c"), (1,)).astype(jnp.int32)

    def body(c_ref, a_ref, b_ref, o_ref):
        o_ref[0] = (a_ref[...].astype(F32) + b_ref[0].astype(F32)).astype(BF16)

    blk = pl.BlockSpec((1, tr, tc), lambda j, i, l, cr: (j, i, l))
    return pl.pallas_call(
        body, name=name,
        grid_spec=pltpu.PrefetchScalarGridSpec(
            num_scalar_prefetch=1, grid=(4, nb, c // tc),
            in_specs=[pl.BlockSpec((tr, tc), lambda j, i, l, cr: ((_slot(axis, j) * 2 + cr[0]) * nb + i, l)), blk],
            out_specs=blk),
        out_shape=jax.ShapeDtypeStruct(sib.shape, BF16),
        compiler_params=_cparams(("parallel", "parallel", "parallel")),
    )(core, full, sib)


def _sum_slots(a, name):
    _, r, c = a.shape
    tr, tc = _tile(r, 512, 8), _tile(c, 2048)

    def body(a_ref, o_ref):
        v = a_ref[...].astype(F32)
        o_ref[...] = ((v[0] + v[1]) + v[2]) + v[3]

    return pl.pallas_call(
        body, name=name, grid=(r // tr, c // tc),
        in_specs=[pl.BlockSpec((4, tr, tc), lambda i, l: (0, i, l))],
        out_specs=pl.BlockSpec((tr, tc), lambda i, l: (i, l)),
        out_shape=jax.ShapeDtypeStruct((r, c), F32),
        compiler_params=_cparams(("parallel", "parallel")),
    )(a)


class _Reducer:
    def __init__(self, spec):
        self.spec = spec
        self.paired = {}
        self.pending = []

    def pair(self, name, full):
        ax, shp = self.spec[name]
        land = lax.empty((4, shp[0] // 2, shp[1]), full.dtype)
        arrays, sems, token = _split_start("reduce_pair_start_" + name, [full, land], _pair_geometry([ax], [shp]), 4)
        self.paired[name] = (arrays, sems)
        return token

    def ship(self, tag, names, after):
        parts = []
        for n in names:
            ax, shp = self.spec[n]
            arrays, sems = self.paired.pop(n)
            full, sib = _split_wait("reduce_pair_wait_" + n, arrays, sems, after, _pair_geometry([ax], [shp]))
            parts.append(_add_parts(full, ax, shp[0], sib, name=f"reduce_add_{n}"))
        parts, lands, send, recv, token = _chip_start(parts, tag)
        self.pending.append((tag, names, parts, lands, send, recv))
        return token

    def finish(self, after, tags):
        out = {}
        for tag, names, parts, lands, send, recv in [p for p in self.pending if p[0] in tags]:
            slots = _chip_wait(parts, lands, send, recv, after, tag)
            halves = [_sum_slots(s, name=f"reduce_sum_{n}") for n, s in zip(names, slots)]
            out.update(zip(names, _half_swap(halves, tag)))
        return out

    def finish_start(self, after, tag):
        (_, names, parts, lands, send, recv), = [p for p in self.pending if p[0] == tag]
        slots = _chip_wait(parts, lands, send, recv, after, tag)
        halves = [_sum_slots(s, name=f"reduce_sum_{n}") for n, s in zip(names, slots)]
        core = lax.axis_index("c")
        bufs = [lax.dynamic_update_slice(lax.empty((2,) + h.shape, h.dtype), h[None], (core, 0, 0)) for h in halves]
        bufs, sems, _ = _split_start("reduce_half_swap_start_" + tag, bufs, _swap_geometry, len(bufs))
        return tag, names, bufs, sems

    def swap_wait(self, started, after):
        tag, names, bufs, sems = started
        outs = _split_wait("reduce_half_swap_wait_" + tag, bufs, sems, after, _swap_geometry)
        return dict(zip(names, [o.reshape(2 * o.shape[1], o.shape[2]) for o in outs]))


def _allreduce_small(pack, after):
    rows = pack.shape[0]

    def body(p_ref, _, o_ref, slots, send_sems, recv_sems):
        x, y, c, _ = _me()
        me = 4 * x + 2 * y + c
        slots[me] = p_ref[...]
        cps = []
        for r in range(1, 8):
            peer = (x ^ (r >> 2), y ^ ((r >> 1) & 1), c ^ (r & 1))
            cp = _remote(p_ref, slots.at[me], send_sems.at[r - 1], recv_sems.at[r - 1], peer)
            cp.start()
            cps.append(cp)
        for r in range(1, 8):
            frm = me ^ r
            _remote(slots.at[frm], slots.at[frm], send_sems.at[r - 1], recv_sems.at[r - 1], (x, y, c)).wait_recv()
        for cp in cps:
            cp.wait_send()
        acc = slots[0]
        for s in range(1, 8):
            acc = acc + slots[s]
        o_ref[...] = acc

    vm = pl.BlockSpec(memory_space=pltpu.VMEM)
    return pl.pallas_call(
        body, name="allreduce_small", in_specs=[vm, ANY], out_specs=vm,
        out_shape=jax.ShapeDtypeStruct(pack.shape, F32),
        scratch_shapes=[pltpu.VMEM((8, rows, HEAD_DIM), F32), pltpu.SemaphoreType.DMA((7,)),
                        pltpu.SemaphoreType.DMA((7,))],
    )(pack, after)


_ROWS = ["norm_mix", "norm_ffn", "mem_norm", "fox_q_norm", "fox_k_norm", "gdn_out_norm", "mem_q_norm",
         "mem_k_norm", "fox_f_bias", "gdn_a_log", "gdn_dt_bias"]


def _pack_rows(vals):
    out = []
    for name in _ROWS:
        v = vals[name].reshape(-1)
        n = -(-v.shape[0] // HEAD_DIM) * HEAD_DIM
        out.append(jnp.pad(v, (0, n - v.shape[0])).reshape(-1, HEAD_DIM))
    return jnp.concatenate(out, axis=0)


def _unpack_rows(pack, like):
    out, r = {}, 0
    for name in _ROWS:
        n = like[name].shape[-1]
        nr = -(-n // HEAD_DIM)
        out[name] = pack[r:r + nr].reshape(1, -1)[:, :n]
        r += nr
    return out, r


def kernel(x, mem, norm_mix, w_in, fox_f_bias, fox_q_norm, fox_k_norm, gdn_conv, gdn_a_log, gdn_dt_bias, gdn_out_norm, mem_norm, w_mem_kv, mem_q_norm, mem_k_norm, w_out, norm_ffn, w_gate_up, w_down, loss_target, m_norm_mix, m_w_in, m_fox_f_bias, m_fox_q_norm, m_fox_k_norm, m_gdn_conv, m_gdn_a_log, m_gdn_dt_bias, m_gdn_out_norm, m_mem_norm, m_w_mem_kv, m_mem_q_norm, m_mem_k_norm, m_w_out, m_norm_ffn, m_w_gate_up, m_w_down, v_norm_mix, v_w_in, v_fox_f_bias, v_fox_q_norm, v_fox_k_norm, v_gdn_conv, v_gdn_a_log, v_gdn_dt_bias, v_gdn_out_norm, v_mem_norm, v_w_mem_kv, v_mem_q_norm, v_mem_k_norm, v_w_out, v_norm_ffn, v_w_gate_up, v_w_down):
    a = dict(locals())
    d = x.shape[-1]
    lay = _Layout(d)
    chip = 2 * lax.axis_index("x") + lax.axis_index("y")
    small = {n: a[n] for n in _ROWS}
    big = ["w_in", "w_mem_kv", "w_out", "w_gate_up", "w_down"]
    axes = [0, 0, 0, 1, 0]

    conv_cols = gdn_conv.shape[-1]
    conv_n = CONV_WIDTH * conv_cols
    conv_rows = -(-conv_n // HEAD_DIM)
    conv_blk = jnp.pad(gdn_conv.reshape(-1), (0, 32 * HEAD_DIM - conv_n)).reshape(32, HEAD_DIM)
    axis_of = dict(zip(big, axes), conv=0, w_in_a=0, w_in_b=0)
    shape_of = {n: a[n].shape[1:] for n in big[1:]}
    shape_of.update(w_in_a=(w_in.shape[1], lay.cols_a), w_in_b=(w_in.shape[1], lay.cols_b), conv=conv_blk.shape)
    placed = {"w_in_a": _cast_place(w_in[0], 0, "cast_w_in_a", lambda v: lay.regroup(v)[:, :lay.cols_a], lay.cols_a),
              "conv": lax.dynamic_update_slice(lax.empty((4 * 32, HEAD_DIM), F32), conv_blk, (chip * 32, 0))}
    grouped = {"in_a": ["w_in_a"], "in_b": ["w_in_b"], "mixer": ["w_mem_kv", "conv"], "out": ["w_out"],
               "gate_up": ["w_gate_up"], "down": ["w_down"]}
    inflight = {}

    def start(tags, name):
        names = [n for t in tags for n in grouped[t]]
        bufs, sems, token = _gather_start([placed[n] for n in names], [axis_of[n] for n in names],
                                          [shape_of[n] for n in names],
                                          [[names.index(n) for n in grouped[t]] for t in tags], name)
        for t, pair in zip(tags, sems):
            inflight[t] = ([bufs[names.index(n)] for n in grouped[t]], pair)
        return token

    first = start(["in_a"], "gather_ici_start_in")
    placed["w_in_b"] = _cast_place(w_in[0], 0, "cast_w_in_b", lambda v: lay.regroup(v)[:, lay.cols_a:], lay.cols_b,
                                   after=first)
    placed.update({n: _cast_place(a[n][0], axis_of[n], "cast_" + n, after=first) for n in big[1:]})
    all_started = start(["in_b", "mixer", "out", "gate_up", "down"], "gather_ici_start_rest")
    all_started = _after_all("moments_ready", all_started, m_w_in[0], v_w_in[0])

    forwarding = {}

    def prefetch(tag, after):
        bufs, sem_pair = inflight.pop(tag)
        ax, shp = [axis_of[n] for n in grouped[tag]], [shape_of[n] for n in grouped[tag]]
        got = _gather_wait(bufs, ax, shp, sem_pair, all_started if tag == "in_a" else after,
                           "gather_ici_wait_" + tag)
        geometry = _forward_geometry(ax, shp)
        got, sems, _ = _split_start("gather_forward_start_" + tag, got, geometry, 3 * len(got))
        forwarding[tag] = (got, sems, geometry)

    def weights(tag, after):
        got, sems, geometry = forwarding.pop(tag)
        got = _split_wait("gather_forward_wait_" + tag, got, sems, after, geometry)
        if tag != "mixer":
            return got
        taps = got[1].reshape(4, 32 * HEAD_DIM)[:, :conv_n].reshape(4, CONV_WIDTH, conv_cols)
        return got[0], jnp.transpose(taps, (1, 0, 2)).reshape(CONV_WIDTH, 4 * conv_cols)

    sp = dict(small)
    reducer = _Reducer({n: (axis_of[n], shape_of[n]) for n in big[1:] + ["w_in_a", "w_in_b"]})
    loss_blk, dx, g = _local_step(x[0], mem[0], loss_target[0], prefetch, weights, reducer, sp)

    gsmall = {n: g[n] for n in _ROWS}
    pack = jnp.concatenate([_pack_rows(gsmall), g["gdn_conv"].reshape(-1, HEAD_DIM), loss_blk], axis=0)
    pack = jnp.pad(pack, ((0, -pack.shape[0] % 8), (0, 0)))
    out = {"grad_x": dx[None]}

    def adamw_shards(reduced):
        if "w_in_a" in reduced:
            reduced = {"w_in": (reduced["w_in_a"], reduced["w_in_b"])}
        for n, gsh in reduced.items():
            join = (lambda ga, gb: lay.ungroup(jnp.concatenate([ga, gb], axis=1))) if n == "w_in" else None
            res = _adamw(a[n][0], gsh, a["m_" + n][0], a["v_" + n][0], g_fn=join, name="adamw_" + n)
            for pre, r in zip(["grad_", "delta_", "new_m_", "new_v_"], res):
                out[pre + n] = r[None]
        return res[0]

    mix_swap = reducer.finish_start(dx, "mix")
    ffn_swap = reducer.finish_start(mix_swap[2][0], "ffn")
    done = adamw_shards(reducer.swap_wait(mix_swap, ffn_swap[2][0]))
    done = adamw_shards(reducer.swap_wait(ffn_swap, done))
    tot = _allreduce_small(pack, done)
    gs, r0 = _unpack_rows(tot, small)
    conv_g = tot[r0:r0 + CONV_WIDTH * 4 * conv_cols // HEAD_DIM].reshape(CONV_WIDTH, 4 * conv_cols)
    gs_conv = lax.dynamic_slice_in_dim(conv_g, chip * conv_cols, conv_cols, axis=1)
    out["loss"] = tot[r0 + CONV_WIDTH * 4 * conv_cols // HEAD_DIM, 0]
    adamw_shards(reducer.finish(tot, ("in",)))
    conv_pad = lambda v: jnp.pad(v.reshape(-1), (0, conv_rows * HEAD_DIM - conv_n)).reshape(conv_rows, HEAD_DIM)
    packs = []
    for src, cv in [(small, gdn_conv), (gs, gs_conv), ({n: a["m_" + n] for n in _ROWS}, m_gdn_conv),
                    ({n: a["v_" + n] for n in _ROWS}, v_gdn_conv)]:
        packs.append(jnp.concatenate([_pack_rows(src), conv_pad(cv)], axis=0))
    res = _adamw(*packs, name="adamw_small")
    for pre, r in zip(["grad_", "delta_", "new_m_", "new_v_"], res):
        vals, r1 = _unpack_rows(r, small)
        for n in _ROWS:
            out[pre + n] = vals[n]
        out[pre + "gdn_conv"] = r[r1:r1 + conv_rows].reshape(-1)[:conv_n].reshape(gdn_conv.shape)
    names = ["norm_mix", "w_in", "fox_f_bias", "fox_q_norm", "fox_k_norm", "gdn_conv", "gdn_a_log", "gdn_dt_bias",
             "gdn_out_norm", "mem_norm", "w_mem_kv", "mem_q_norm", "mem_k_norm", "w_out", "norm_ffn", "w_gate_up",
             "w_down"]
    return (out["loss"], out["grad_x"], *[out[p + n] for p in ["grad_", "delta_", "new_m_", "new_v_"] for n in names])
```

```python
import functools
import math

import jax
import jax.numpy as jnp
from jax import lax
from jax.experimental import pallas as pl
from jax.experimental.pallas import tpu as pltpu

F32, BF16 = jnp.float32, jnp.bfloat16
HEAD_DIM = 128
CHUNK = 64
N_MEM_HEADS = 4
CONV_WIDTH = 4
NORM_EPS = 1e-6
ADAM_LR, ADAM_B1, ADAM_B2, ADAM_EPS, ADAM_WD, ADAM_STEP = 0.001, 0.9, 0.999, 1e-08, 0.01, 10
VMEM_LIMIT = 48 * 1024 * 1024
NEG = -1e30
MESH = pl.DeviceIdType.MESH


def _cparams(sem=None, **kw):
    if sem is not None:
        kw["dimension_semantics"] = sem
    return pltpu.CompilerParams(vmem_limit_bytes=VMEM_LIMIT, **kw)


def _tile(n, target, mult=128):
    best = None
    d = mult
    while d <= min(n, target):
        if n % d == 0:
            best = d
        d += mult
    return best if best is not None else n


def _dot(a, b, dims, hi):
    if a.ndim == 3:
        dn = (((dims[0][0] + 1,), (dims[1][0] + 1,)), ((0,), (0,)))
    else:
        dn = (dims, ((), ()))
    if hi is not None:
        return lax.dot_general(a, b, dn, precision=hi, preferred_element_type=F32)
    return lax.dot_general(a.astype(BF16), b.astype(BF16), dn, preferred_element_type=F32)


def _make_dots(hi, cotangent=None):
    @jax.custom_vjp
    def nn(a, b):
        return _dot(a, b, ((1,), (0,)), hi)

    @jax.custom_vjp
    def nt(a, b):
        return _dot(a, b, ((1,), (1,)), hi)

    @jax.custom_vjp
    def tn(a, b):
        return _dot(a, b, ((0,), (0,)), hi)

    bnn, bnt, btn = cotangent or (nn, nt, tn)
    nn.defvjp(lambda a, b: (nn(a, b), (a, b)), lambda r, g: (bnt(g, r[1]), btn(r[0], g)))
    nt.defvjp(lambda a, b: (nt(a, b), (a, b)), lambda r, g: (bnn(g, r[1]), btn(g, r[0])))
    tn.defvjp(lambda a, b: (tn(a, b), (a, b)), lambda r, g: (bnt(r[1], g), bnn(r[0], g)))
    return nn, nt, tn


_nn, _nt, _tn = _make_dots(None)
_nn_hi, _nt_hi, _tn_hi = _make_dots(lax.Precision.HIGHEST)


def _sigmoid(x):
    return jax.nn.sigmoid(x)


@jax.custom_vjp
def _softplus(x):
    return jnp.maximum(x, 0.0) + jnp.log(1.0 + jnp.exp(-jnp.abs(x)))


_softplus.defvjp(lambda x: (_softplus(x), x), lambda x, g: (g * _sigmoid(x),))


def _silu(x):
    return x * _sigmoid(x)


def _rms_fn(x, gain, z=None):
    y = x * lax.rsqrt(jnp.mean(x * x, axis=-1, keepdims=True) + NORM_EPS) * gain
    if z is not None:
        y = y * _silu(z)
    return y


def _mm(a, b, *, ta=False, tb=False, out_dtype=F32, res=None, stack=None, after=None, name):
    a2, b2 = a.shape[-2:], b.shape[-2:]
    ns = b.shape[0] if stack else 1
    m = a2[1] if ta else a2[0]
    k = a2[0] if ta else a2[1]
    n = b2[0] if tb else b2[1]
    assert k == (b2[1] if tb else b2[0])
    tm, tn, tk = _mm_tiles(m, n, k, ns if stack == "sum" else 1, a.dtype.itemsize, b.dtype.itemsize,
                           jnp.dtype(out_dtype).itemsize, res is not None)
    nk = k // tk
    single = nk == 1 and stack != "sum"
    dims = ((0 if ta else 1,), (1 if tb else 0,))
    if stack == "sum":
        order = lambda g0, g1, g2, g3: (g2, g0, g1, g3)
        grid = (m // tm, n // tn, ns, nk)
    else:
        order = lambda g0, g1, g2, g3: (g0, g1, g2, g3)
        grid = (ns, m // tm, n // tn, nk)

    def body(*refs):
        if after is not None:
            refs = refs[:2 + (res is not None)] + refs[3 + (res is not None):]
        if single:
            a_ref, b_ref = refs[:2]
            r = lax.dot_general(a_ref[...].astype(BF16), b_ref[...].astype(BF16), (dims, ((), ())),
                                preferred_element_type=F32)
            if res is not None:
                r = r + refs[2][...]
            refs[-1][...] = r.astype(out_dtype)
            return
        if res is None:
            a_ref, b_ref, o_ref, acc = refs
        else:
            a_ref, b_ref, r_ref, o_ref, acc = refs
        s, _, _, kk = order(*[pl.program_id(d) for d in range(4)])
        first = kk == 0
        last = kk == nk - 1
        if stack == "sum":
            first, last = first & (s == 0), last & (s == ns - 1)

        @pl.when(first)
        def _():
            acc[...] = jnp.zeros_like(acc)

        acc[...] += lax.dot_general(a_ref[...].astype(BF16), b_ref[...].astype(BF16), (dims, ((), ())),
                                    preferred_element_type=F32)

        @pl.when(last)
        def _():
            r = acc[...]
            if res is not None:
                r = r + r_ref[...]
            o_ref[...] = r.astype(out_dtype)

    def spec(shape, idx, stacked):
        if stacked:
            return pl.BlockSpec((None,) + shape, lambda *g: (order(*g)[0],) + idx(*order(*g)))
        return pl.BlockSpec(shape, lambda *g: idx(*order(*g)))

    a_spec = (spec((tk, tm), lambda s, i, j, kk: (kk, i), stack == "sum") if ta
              else spec((tm, tk), lambda s, i, j, kk: (i, kk), stack == "sum"))
    b_spec = (spec((tn, tk), lambda s, i, j, kk: (j, kk), bool(stack)) if tb
              else spec((tk, tn), lambda s, i, j, kk: (kk, j), bool(stack)))
    o_spec = spec((tm, tn), lambda s, i, j, kk: (i, j), stack == "out")
    ins, specs = [a, b], [a_spec, b_spec]
    if res is not None:
        ins.append(res)
        specs.append(o_spec)
    if after is not None:
        ins.append(after)
        specs.append(pl.BlockSpec(after.shape, lambda *g: (0,) * after.ndim))
    sem = (("parallel", "parallel", "arbitrary", "arbitrary") if stack == "sum"
           else ("parallel", "parallel", "parallel", "arbitrary"))
    return pl.pallas_call(
        body, name=name, grid=grid, in_specs=specs, out_specs=o_spec,
        out_shape=jax.ShapeDtypeStruct(((ns,) if stack == "out" else ()) + (m, n), out_dtype),
        scratch_shapes=[] if single else [pltpu.VMEM((tm, tn), F32)],
        compiler_params=_cparams(sem),
    )(*ins)


MM_VMEM_BUDGET = 40 * 1024 * 1024
MXU_WIDTH = 256


def _mm_tiles(m, n, k, ns, sa, sb, so, has_res):
    def divs(x, mult, cap):
        out = [d for d in range(mult, min(x, cap) + 1, mult) if x % d == 0]
        return out or [x]

    best = None
    for tk in divs(k, 128, 8192):
        nk = (k // tk) * ns
        for tm in divs(m, 8, 2048):
            for tn in divs(n, 128, 2048):
                vmem = 2 * (tm * tk * sa + tk * tn * sb + tm * tn * so) + (2 * tm * tn * 4 if has_res else 0)
                vmem += tm * tn * 4 if nk > 1 else 0
                if vmem > MM_VMEM_BUDGET:
                    continue
                steps = (m // tm) * (n // tn) * nk
                traffic = (m // tm) * k * n * sb * ns + (n // tn if nk > 1 else 1) * m * k * sa * ns
                cost = steps * 0.4e-6 + traffic / 2.5e12 + (nk * m * n * 8 / 6e12 if nk > 1 else 0)
                cost += 2.0 * m * n * k * ns / 7e14 * (-(-tn // MXU_WIDTH) * MXU_WIDTH / tn)
                if best is None or cost < best[0]:
                    best = (cost, tm, tn, tk)
    return best[1:]


def _norm_fwd(x, xoff, gain, ncol, w, out_dtype, *, z=None, zoff=0, into=None, into_off=0, name):
    t = x.shape[0]
    tr = _tile(t, max(256, (1 << 18) // w), 8)

    def body(*refs):
        x_ref, g_ref, o_ref = refs[0], refs[1], refs[-1]
        y = _rms_fn(x_ref[...], g_ref[...]) if z is None else _rms_fn(x_ref[...], g_ref[...], refs[2][...])
        o_ref[...] = y.astype(out_dtype)

    ins = [x, gain]
    specs = [pl.BlockSpec((tr, w), lambda j, r: (r, xoff + j)), pl.BlockSpec((1, w), lambda j, r: (0, 0))]
    if z is not None:
        ins.append(z)
        specs.append(pl.BlockSpec((tr, w), lambda j, r: (r, zoff + j)))
    aliases = {}
    if into is not None:
        aliases = {len(ins): 0}
        ins.append(into)
        specs.append(pl.BlockSpec(memory_space=pl.ANY))
    return pl.pallas_call(
        body, name=name, grid=(ncol, t // tr), in_specs=specs,
        out_specs=pl.BlockSpec((tr, w), lambda j, r: (r, into_off + j)),
        out_shape=jax.ShapeDtypeStruct((t, ncol * w) if into is None else into.shape, out_dtype),
        input_output_aliases=aliases, compiler_params=_cparams(("parallel", "parallel")),
    )(*ins)


def _norm_bwd(x, xoff, gain, dy, dyoff, ncol, w, *, z=None, zoff=0, res=None, out_dtype=F32, name):
    t = x.shape[0]
    tr = _tile(t, max(256, (1 << 18) // w), 8)

    def body(*refs):
        it = iter(refs)
        x_ref, g_ref = next(it), next(it)
        z_ref = next(it) if z is not None else None
        dy_ref = next(it)
        r_ref = next(it) if res is not None else None
        dx_ref = next(it)
        dz_ref = next(it) if z is not None else None
        dg_ref = next(it)

        @pl.when((pl.program_id(0) == 0) & (pl.program_id(1) == 0))
        def _():
            dg_ref[...] = jnp.zeros_like(dg_ref)

        args = (x_ref[...], g_ref[...]) + ((z_ref[...],) if z is not None else ())
        _, vjp = jax.vjp(_rms_fn, *args)
        grads = vjp(dy_ref[...].astype(F32))
        dx = grads[0]
        if res is not None:
            dx = dx + r_ref[...]
        dx_ref[...] = dx.astype(out_dtype)
        if z is not None:
            dz_ref[...] = grads[2]
        dg_ref[...] += grads[1]

    ins = [x, gain]
    specs = [pl.BlockSpec((tr, w), lambda j, r: (r, xoff + j)), pl.BlockSpec((1, w), lambda j, r: (0, 0))]
    if z is not None:
        ins.append(z)
        specs.append(pl.BlockSpec((tr, w), lambda j, r: (r, zoff + j)))
    ins.append(dy)
    specs.append(pl.BlockSpec((tr, w), lambda j, r: (r, dyoff + j)))
    blk = pl.BlockSpec((tr, w), lambda j, r: (r, j))
    if res is not None:
        ins.append(res)
        specs.append(blk)
    full = jax.ShapeDtypeStruct((t, ncol * w), F32)
    out_shape, out_specs = [jax.ShapeDtypeStruct((t, ncol * w), out_dtype)], [blk]
    if z is not None:
        out_shape.append(full)
        out_specs.append(blk)
    out_shape.append(jax.ShapeDtypeStruct((1, w), F32))
    out_specs.append(pl.BlockSpec((1, w), lambda j, r: (0, 0)))
    return pl.pallas_call(
        body, name=name, grid=(ncol, t // tr), in_specs=specs, out_specs=out_specs, out_shape=out_shape,
        compiler_params=_cparams(("arbitrary", "arbitrary")),
    )(*ins)


def _small_fn(x, pa, pb, nf, ng):
    lane = lax.broadcasted_iota(jnp.int32, x.shape, 1)
    zz = x + pb
    logf = -_softplus(-zz)
    g = -jnp.exp(pa) * _softplus(zz)
    beta = _sigmoid(x)
    return jnp.where(lane < nf, logf, jnp.where(lane < nf + ng, g, beta))


def _tri(n, upper):
    r = lax.broadcasted_iota(jnp.int32, (n, n), 0)
    c = lax.broadcasted_iota(jnp.int32, (n, n), 1)
    return jnp.where((c >= r) if upper else (c <= r), 1.0, 0.0).astype(F32)


def _small_fwd(p, off, pa, pb, nf, ng):
    t = p.shape[0]
    blk = HEAD_DIM
    nb = t // blk

    def body(x_ref, pa_ref, pb_ref, v_ref, c_ref):
        v_ref[...] = _small_fn(x_ref[...], pa_ref[...], pb_ref[...], nf, ng)
        tri = _tri(blk, False)

        carry = jnp.zeros((1, HEAD_DIM), F32)
        for i in range(nb):
            rows = slice(i * blk, (i + 1) * blk)
            c = _nn_hi(tri, v_ref[rows, :]) + carry
            c_ref[rows, :] = c
            carry = c[blk - 1:blk, :]

    row = pl.BlockSpec((1, HEAD_DIM), lambda i: (0, 0))
    out = pl.BlockSpec((t, HEAD_DIM), lambda i: (0, 0))
    return pl.pallas_call(
        body, name="small_fwd", grid=(1,),
        in_specs=[pl.BlockSpec((t, HEAD_DIM), lambda i: (0, off)), row, row], out_specs=[out, out],
        out_shape=[jax.ShapeDtypeStruct((t, HEAD_DIM), F32)] * 2,
        compiler_params=_cparams(("arbitrary",)),
    )(p, pa, pb)


def _small_bwd(p, off, pa, pb, dvals, dcsum, nf, ng):
    t = p.shape[0]
    blk = HEAD_DIM
    nb = t // blk

    def body(x_ref, pa_ref, pb_ref, dv_ref, dc_ref, dx_ref, dpa_ref, dpb_ref, tot_ref):
        tri = _tri(blk, True)

        carry = jnp.zeros((1, HEAD_DIM), F32)
        for i in reversed(range(nb)):
            rows = slice(i * blk, (i + 1) * blk)
            c = _nn_hi(tri, dc_ref[rows, :]) + carry
            tot_ref[rows, :] = c + dv_ref[rows, :]
            carry = c[0:1, :]
        f = functools.partial(_small_fn, nf=nf, ng=ng)
        _, vjp = jax.vjp(f, x_ref[...], pa_ref[...], pb_ref[...])
        dx, dpa, dpb = vjp(tot_ref[...])
        dx_ref[...] = dx
        dpa_ref[...] = dpa
        dpb_ref[...] = dpb

    row = pl.BlockSpec((1, HEAD_DIM), lambda i: (0, 0))
    full = pl.BlockSpec((t, HEAD_DIM), lambda i: (0, 0))
    return pl.pallas_call(
        body, name="small_bwd", grid=(1,),
        in_specs=[pl.BlockSpec((t, HEAD_DIM), lambda i: (0, off)), row, row, full, full],
        out_specs=[full, row, row],
        out_shape=[jax.ShapeDtypeStruct((t, HEAD_DIM), F32), jax.ShapeDtypeStruct((1, HEAD_DIM), F32),
                   jax.ShapeDtypeStruct((1, HEAD_DIM), F32)],
        scratch_shapes=[pltpu.VMEM((t, HEAD_DIM), F32)],
        compiler_params=_cparams(("arbitrary",)),
    )(p, pa, pb, dvals, dcsum)


def _fox_heads(nf, most):
    return next(h for h in range(most, 0, -1) if nf % h == 0)


def _fox_fwd(q, k, v, cc, cr, nf, tq, tk, d_mix):
    t = q.shape[0]
    scale = HEAD_DIM ** -0.5
    assert tq == tk

    vt = jnp.transpose(v.reshape(t // tk, tk, nf, HEAD_DIM), (2, 0, 3, 1))

    hp = _fox_heads(nf, 3)
    lanes = lambda h: slice(h * HEAD_DIM, (h + 1) * HEAD_DIM)

    def body(q_ref, k_ref, vt_ref, cc_ref, cr_ref, o_ref, lse_ref, mix_ref):
        i = pl.program_id(1)
        qs = [q_ref[:, lanes(h)] for h in range(hp)]
        cqs = [cr_ref[h, i] for h in range(hp)]
        ones = jnp.ones((8, tk), BF16)
        diff = lax.broadcasted_iota(jnp.int32, (tk, tq), 0) - lax.broadcasted_iota(jnp.int32, (tk, tq), 1)

        def scores(h, j):
            ks = pl.ds(pl.multiple_of(j * tk, tk), tk)
            return lax.dot_general(k_ref[ks, lanes(h)], qs[h], (((1,), (1,)), ((), ())),
                                   preferred_element_type=F32)

        def tile(h, j, m, l, acc, s, masked):
            ks = pl.ds(pl.multiple_of(j * tk, tk), tk)
            s = s * scale + cqs[h] - cc_ref[0, ks, h:h + 1]
            if masked:
                s = jnp.where(diff <= 0, s, NEG)
            m_new = jnp.maximum(m, jnp.max(s, axis=0, keepdims=True))
            pr = jnp.exp(s - m_new).astype(BF16)
            alpha = jnp.exp(m - m_new)
            l = alpha * l + jnp.dot(ones, pr, preferred_element_type=F32)[:1]
            acc = alpha * acc + jnp.dot(vt_ref[h, j], pr, preferred_element_type=F32)
            return m_new, l, acc

        def step(j, carry):
            nxt = [scores(h, j + 1) for h in range(hp)]
            return tuple(tile(h, j, *carry[h], False) + (nxt[h],) for h in range(hp))

        init = tuple((jnp.full((1, tq), NEG, F32), jnp.zeros((1, tq), F32), jnp.zeros((HEAD_DIM, tq), F32),
                      scores(h, 0)) for h in range(hp))
        carry = lax.fori_loop(0, i, step, init)
        for h in range(hp):
            m, l, acc = tile(h, i, *carry[h], True)
            o = jnp.transpose(acc / l)
            o_ref[:, lanes(h)] = o
            mix_ref[:, lanes(h)] = o.astype(BF16)
            lse_ref[h, 0] = m + jnp.log(l)

    w = hp * HEAD_DIM
    qblk = pl.BlockSpec((tq, w), lambda h, i: (i, h))
    return pl.pallas_call(
        body, name="fox_fwd", grid=(nf // hp, t // tq),
        in_specs=[qblk, pl.BlockSpec((t, w), lambda h, i: (0, h)),
                  pl.BlockSpec((hp, t // tk, HEAD_DIM, tk), lambda h, i: (h, 0, 0, 0)),
                  pl.BlockSpec((1, t, HEAD_DIM), lambda h, i: (h, 0, 0)),
                  pl.BlockSpec((hp, t // tk, 1, tk), lambda h, i: (h, 0, 0, 0))],
        out_specs=[qblk, pl.BlockSpec((hp, 1, 1, tq), lambda h, i: (h, i, 0, 0)), qblk],
        out_shape=[jax.ShapeDtypeStruct((t, nf * HEAD_DIM), F32), jax.ShapeDtypeStruct((nf, t // tq, 1, tq), F32),
                   jax.ShapeDtypeStruct((t, d_mix), BF16)],
        compiler_params=_cparams(("parallel", "parallel")),
    )(q, k, vt, cc, cr)


def _fox_bwd(q, k, v, cc, cr, o, lse, dmix, nf, tq, tk):
    t = q.shape[0]
    scale = HEAD_DIM ** -0.5
    assert tq == tk
    hp = _fox_heads(nf, 3)
    lanes = lambda h: slice(h * HEAD_DIM, (h + 1) * HEAD_DIM)
    kt = jnp.transpose(k.reshape(t // tk, tk, nf, HEAD_DIM), (2, 0, 3, 1))

    def body(q_ref, k_ref, kt_ref, v_ref, cc_ref, cr_ref, o_ref, lse_ref, do_ref,
             dq_ref, dk_ref, dv_ref, dcq_ref, dck_ref):
        i = pl.program_id(1)

        @pl.when(i == 0)
        def _():
            dk_ref[...] = jnp.zeros_like(dk_ref)
            dv_ref[...] = jnp.zeros_like(dv_ref)
            dck_ref[...] = jnp.zeros_like(dck_ref)

        diff = lax.broadcasted_iota(jnp.int32, (tk, tq), 0) - lax.broadcasted_iota(jnp.int32, (tk, tq), 1)
        lane = lax.broadcasted_iota(jnp.int32, (tk, HEAD_DIM), 1)
        qs = [q_ref[:, lanes(h)] for h in range(hp)]
        dos = [do_ref[:, lanes(h)] for h in range(hp)]
        do_b = [d.astype(BF16) for d in dos]
        cqs = [cr_ref[h, i] for h in range(hp)]
        lses = [lse_ref[h, 0] for h in range(hp)]
        deltas = [jnp.sum(jnp.transpose(dos[h] * o_ref[:, lanes(h)]), axis=0, keepdims=True) for h in range(hp)]

        def products(h, j):
            ks = pl.ds(pl.multiple_of(j * tk, tk), tk)
            nt = (((1,), (1,)), ((), ()))
            return (lax.dot_general(k_ref[ks, lanes(h)], qs[h], nt, preferred_element_type=F32),
                    lax.dot_general(v_ref[ks, lanes(h)], do_b[h], nt, preferred_element_type=F32))

        def tile(h, j, dqt, dcq, s, dp, masked):
            ks = pl.ds(pl.multiple_of(j * tk, tk), tk)
            pr = jnp.exp(s * scale + cqs[h] - cc_ref[0, ks, h:h + 1] - lses[h])
            if masked:
                pr = jnp.where(diff <= 0, pr, 0.0)
            ds = pr * (dp - deltas[h])
            ds_b = ds.astype(BF16)
            dqt = dqt + jnp.dot(kt_ref[h, j], ds_b, preferred_element_type=F32)
            dk_ref[ks, lanes(h)] += jnp.dot(ds_b, qs[h], preferred_element_type=F32) * scale
            dv_ref[ks, lanes(h)] += jnp.dot(pr.astype(BF16), do_b[h], preferred_element_type=F32)
            dck_ref[0, ks, :] -= jnp.where(lane == h, jnp.sum(ds, axis=1, keepdims=True), 0.0)
            return dqt, dcq + jnp.sum(ds, axis=0, keepdims=True)

        def step(j, carry):
            nxt = [products(h, j + 1) for h in range(hp)]
            return tuple(tile(h, j, *carry[h], False) + nxt[h] for h in range(hp))

        init = tuple((jnp.zeros((HEAD_DIM, tq), F32), jnp.zeros((1, tq), F32)) + products(h, 0) for h in range(hp))
        carry = lax.fori_loop(0, i, step, init)
        for h in range(hp):
            dqt, dcq = tile(h, i, *carry[h], True)
            dq_ref[:, lanes(h)] = jnp.transpose(dqt) * scale
            dcq_ref[h, 0] = dcq

    w = hp * HEAD_DIM
    head_all = pl.BlockSpec((t, w), lambda h, i: (0, h))
    qblk = pl.BlockSpec((tq, w), lambda h, i: (i, h))
    colv = pl.BlockSpec((1, t, HEAD_DIM), lambda h, i: (h, 0, 0))
    rows_all = pl.BlockSpec((hp, t // tk, 1, tk), lambda h, i: (h, 0, 0, 0))
    row_blk = pl.BlockSpec((hp, 1, 1, tq), lambda h, i: (h, i, 0, 0))
    wide = jax.ShapeDtypeStruct((t, nf * HEAD_DIM), F32)
    return pl.pallas_call(
        body, name="fox_bwd", grid=(nf // hp, t // tq),
        in_specs=[qblk, head_all, pl.BlockSpec((hp, t // tk, HEAD_DIM, tk), lambda h, i: (h, 0, 0, 0)), head_all,
                  colv, rows_all, qblk, row_blk, qblk],
        out_specs=[qblk, head_all, head_all, row_blk, colv],
        out_shape=[wide, wide, wide, jax.ShapeDtypeStruct((nf, t // tq, 1, tq), F32),
                   jax.ShapeDtypeStruct((nf // hp, t, HEAD_DIM), F32)],
        compiler_params=_cparams(("parallel", "arbitrary")),
    )(q, k, kt, v, cc, cr, o, lse, dmix)


def _mem_fn(mq, mk, mv, gq, gk):
    qn = _rms_fn(mq, gq)
    kn = _rms_fn(mk, gk)
    s = _nt(qn, kn) * (HEAD_DIM ** -0.5)
    e = jnp.exp(s - lax.stop_gradient(jnp.max(s, axis=1, keepdims=True)))
    pr = e / jnp.sum(e, axis=1, keepdims=True)
    return _nn(pr, mv)


def _mem_specs(t, m, tq, qoff):
    qblk = pl.BlockSpec((tq, HEAD_DIM), lambda h, i: (i, qoff + h))
    kblk = pl.BlockSpec((m, HEAD_DIM), lambda h, i: (0, h))
    vblk = pl.BlockSpec((m, HEAD_DIM), lambda h, i: (0, N_MEM_HEADS + h))
    row = pl.BlockSpec((1, HEAD_DIM), lambda h, i: (0, 0))
    return qblk, kblk, vblk, row


def _mem_fwd(p, qoff, mkv, gq, gk, tq, into, into_off):
    t, m = p.shape[0], mkv.shape[0]
    qblk, kblk, vblk, row = _mem_specs(t, m, tq, qoff)

    def body(q_ref, k_ref, v_ref, gq_ref, gk_ref, _, o_ref):
        o_ref[...] = _mem_fn(q_ref[...], k_ref[...], v_ref[...], gq_ref[...], gk_ref[...]).astype(BF16)

    return pl.pallas_call(
        body, name="mem_fwd", grid=(N_MEM_HEADS, t // tq),
        in_specs=[qblk, kblk, vblk, row, row, pl.BlockSpec(memory_space=pl.ANY)],
        out_specs=pl.BlockSpec((tq, HEAD_DIM), lambda h, i: (i, into_off + h)),
        out_shape=jax.ShapeDtypeStruct(into.shape, BF16), input_output_aliases={5: 0},
        compiler_params=_cparams(("parallel", "parallel")),
    )(p, mkv, mkv, gq, gk, into)


def _mem_bwd(p, qoff, mkv, gq, gk, dmix, dooff, tq):
    t, m = p.shape[0], mkv.shape[0]
    qblk, kblk, vblk, row = _mem_specs(t, m, tq, qoff)

    def body(q_ref, k_ref, v_ref, gq_ref, gk_ref, do_ref, dq_ref, dkv_k_ref, dkv_v_ref, dgq_ref, dgk_ref):
        h, i = pl.program_id(0), pl.program_id(1)

        @pl.when((h == 0) & (i == 0))
        def _():
            dgq_ref[...] = jnp.zeros_like(dgq_ref)
            dgk_ref[...] = jnp.zeros_like(dgk_ref)

        @pl.when(i == 0)
        def _():
            dkv_k_ref[...] = jnp.zeros_like(dkv_k_ref)
            dkv_v_ref[...] = jnp.zeros_like(dkv_v_ref)

        _, vjp = jax.vjp(_mem_fn, q_ref[...], k_ref[...], v_ref[...], gq_ref[...], gk_ref[...])
        dq, dk, dv, dgq, dgk = vjp(do_ref[...])
        dq_ref[...] = dq
        dkv_k_ref[...] += dk
        dkv_v_ref[...] += dv
        dgq_ref[...] += dgq
        dgk_ref[...] += dgk

    oblk = pl.BlockSpec((tq, HEAD_DIM), lambda h, i: (i, h))
    kout = pl.BlockSpec((m, HEAD_DIM), lambda h, i: (0, h))
    half = jax.ShapeDtypeStruct((m, N_MEM_HEADS * HEAD_DIM), F32)
    rshape = jax.ShapeDtypeStruct((1, HEAD_DIM), F32)
    return pl.pallas_call(
        body, name="mem_bwd", grid=(N_MEM_HEADS, t // tq),
        in_specs=[qblk, kblk, vblk, row, row, pl.BlockSpec((tq, HEAD_DIM), lambda h, i: (i, dooff + h))],
        out_specs=[oblk, kout, kout, row, row],
        out_shape=[jax.ShapeDtypeStruct((t, N_MEM_HEADS * HEAD_DIM), F32), half, half, rshape, rshape],
        compiler_params=_cparams(("arbitrary", "arbitrary")),
    )(p, mkv, mkv, gq, gk, dmix)


def _shift_down(x, s):
    if s == 0:
        return x
    r = lax.broadcasted_iota(jnp.int32, x.shape, 0)
    return jnp.where(r >= s, pltpu.roll(x, s, 0), 0.0)


def _shift_up(x, s):
    if s == 0:
        return x
    n = x.shape[0]
    r = lax.broadcasted_iota(jnp.int32, x.shape, 0)
    return jnp.where(r < n - s, pltpu.roll(x, n - s, 0), 0.0)


CONV_ROWS = 128


def _conv_fn(x0, x1, x2, x3, w0, w1, w2, w3, kind):
    y = _silu(x0 * w0 + x1 * w1 + x2 * w2 + x3 * w3)
    if kind == 2:
        return y
    y = y * lax.rsqrt(jnp.sum(y * y, axis=-1, keepdims=True) + NORM_EPS)
    return y * (HEAD_DIM ** -0.5) if kind == 0 else y


def _conv_fwd(p, off, conv_w, ng):
    t = p.shape[0]

    def body(x_ref, w_ref, o_ref):
        kind = pl.program_id(0) // ng
        x = x_ref[...]
        xs = [_shift_down(x, CONV_WIDTH - 1 - j) for j in range(CONV_WIDTH)]
        ws = [w_ref[j:j + 1, :] for j in range(CONV_WIDTH)]
        for kd in range(3):
            @pl.when(kind == kd)
            def _(kd=kd):
                for r0 in range(0, t, CONV_ROWS):
                    rows = slice(r0, r0 + CONV_ROWS)
                    o_ref[rows, :] = _conv_fn(*[x[rows] for x in xs], *ws, kd)

    return pl.pallas_call(
        body, name="gdn_conv_fwd", grid=(3 * ng,),
        in_specs=[pl.BlockSpec((t, HEAD_DIM), lambda c: (0, off + c)),
                  pl.BlockSpec((CONV_WIDTH, HEAD_DIM), lambda c: (0, c))],
        out_specs=pl.BlockSpec((t, HEAD_DIM), lambda c: (0, c)),
        out_shape=jax.ShapeDtypeStruct((t, 3 * ng * HEAD_DIM), F32),
        compiler_params=_cparams(("parallel",)),
    )(p, conv_w)


def _conv_bwd(p, off, conv_w, dys, ng):
    t = p.shape[0]

    def body(x_ref, w_ref, dq_ref, dk_ref, dv_ref, dx_ref, dw_ref):
        kind = pl.program_id(0) // ng
        dy_refs = (dq_ref, dk_ref, dv_ref)
        x = x_ref[...]
        xs = [_shift_down(x, CONV_WIDTH - 1 - j) for j in range(CONV_WIDTH)]
        ws = [w_ref[j:j + 1, :] for j in range(CONV_WIDTH)]
        for kd in range(3):
            @pl.when(kind == kd)
            def _(kd=kd):
                parts = []
                for r0 in range(0, t, CONV_ROWS):
                    rows = slice(r0, r0 + CONV_ROWS)
                    _, vjp = jax.vjp(functools.partial(_conv_fn, kind=kd), *[x[rows] for x in xs], *ws)
                    parts.append(vjp(dy_refs[kd][rows, :]))
                g = [jnp.concatenate([part[j] for part in parts], axis=0) for j in range(CONV_WIDTH)]
                g += [sum(part[CONV_WIDTH + j] for part in parts) for j in range(CONV_WIDTH)]
                dx = _shift_up(g[0], CONV_WIDTH - 1)
                for j in range(1, CONV_WIDTH):
                    dx = dx + _shift_up(g[j], CONV_WIDTH - 1 - j)
                dx_ref[...] = dx.astype(BF16)
                for j in range(CONV_WIDTH):
                    dw_ref[j:j + 1, :] = g[CONV_WIDTH + j]

    blk = pl.BlockSpec((t, HEAD_DIM), lambda c: (0, off + c))
    head = lambda k: pl.BlockSpec((t, HEAD_DIM), lambda c: (0, jnp.where(c // ng == k, c % ng, 0)))
    wblk = pl.BlockSpec((CONV_WIDTH, HEAD_DIM), lambda c: (0, c))
    return pl.pallas_call(
        body, name="gdn_conv_bwd", grid=(3 * ng,),
        in_specs=[blk, wblk] + [head(k) for k in range(3)],
        out_specs=[blk, wblk],
        out_shape=[jax.ShapeDtypeStruct(p.shape, BF16),
                   jax.ShapeDtypeStruct((CONV_WIDTH, 3 * ng * HEAD_DIM), F32)],
        compiler_params=_cparams(("parallel",)),
    )(p, conv_w, *dys)


def _lower_inverse(lower):
    c = lower.shape[-1]
    r = lax.broadcasted_iota(jnp.int32, (1, c, c), 1)
    e = lax.broadcasted_iota(jnp.int32, (1, c, c), 2)
    hi = lax.Precision.HIGH
    inv = jnp.where(r == e, 1.0, 0.0) - lower
    pw = lower
    for _ in range(int(math.log2(c)) - 1):
        pw = _dot(pw, pw, ((1,), (0,)), hi)
        inv = inv + _dot(inv, pw, ((1,), (0,)), hi)
    return inv


@jax.custom_vjp
def _solve(lower, inv, vb, kbg):
    hi = lax.Precision.HIGH
    return _dot(inv, vb, ((1,), (0,)), hi), _dot(inv, kbg, ((1,), (0,)), hi)


def _solve_fwd(lower, inv, vb, kbg):
    u, w = _solve(lower, inv, vb, kbg)
    return (u, w), (inv, u, w)


def _solve_bwd(res, cts):
    inv, u, w = res
    dvb, dkbg = _tn(inv, cts[0]), _tn(inv, cts[1])
    return -(_nt(dvb, u) + _nt(dkbg, w)), jnp.zeros_like(inv), dvb, dkbg


_solve.defvjp(_solve_fwd, _solve_bwd)


def _wy_fn(q, k, v, gcol, grow, bcol, inv=None):
    b, c, dk = q.shape
    r = lax.broadcasted_iota(jnp.int32, (1, c, c), 1)
    e = lax.broadcasted_iota(jnp.int32, (1, c, c), 2)
    tril, strict = e <= r, e < r
    gc_col = jnp.sum(jnp.where(tril, grow, 0.0), axis=2, keepdims=True)
    gc_row = jnp.sum(jnp.where(r <= e, gcol, 0.0), axis=1, keepdims=True)
    g_last = jnp.sum(gcol, axis=1, keepdims=True)
    decay = jnp.exp(jnp.where(tril, gc_col - gc_row, NEG))
    kb, vb = k * bcol, v * bcol
    lower = jnp.where(strict, _nt(kb, k) * decay, 0.0)
    if inv is None:
        inv = _lower_inverse(lower)
    u, w = _solve(lower, inv, vb, kb * jnp.exp(gc_col))
    attn = jnp.where(tril, _nt(q, k) * decay, 0.0)
    qg = q * jnp.exp(gc_col)
    kdec = k * jnp.exp(g_last - gc_col)
    egl = jnp.broadcast_to(jnp.exp(g_last), (b, 1, dk))
    return u, w, qg, kdec, attn, egl, inv


def _scan_fn(u, w, qg, kdec, attn, egl, state):
    v_new = u - _nn(w, state)
    o = _nn(qg, state) + _nn(attn, v_new)
    return o, state * egl + _tn(kdec, v_new)


GDN_CHUNKS_PER_STEP = 4
GDN_SCAN_CHUNKS = 4


def _gdn_fwd(qkv, vals, grow, nf, ng):
    t = qkv.shape[0]
    nch = t // CHUNK

    cb = GDN_CHUNKS_PER_STEP
    *wy, inv = _gdn_wy(qkv, vals, grow, nf, ng, cb)

    sc = GDN_SCAN_CHUNKS

    def body(u_ref, w_ref, qg_ref, kd_ref, at_ref, eg_ref, o_ref, st_ref, state):
        @pl.when(pl.program_id(0) == 0)
        def _():
            state[...] = jnp.zeros_like(state)

        for c in range(sc):
            rows = slice(c * CHUNK, (c + 1) * CHUNK)
            heads = lambda ref: jnp.stack([ref[rows, h * HEAD_DIM:(h + 1) * HEAD_DIM] for h in range(ng)])
            st_ref[:, c] = state[...]
            o, new = _scan_fn(heads(u_ref), heads(w_ref), heads(qg_ref), heads(kd_ref), at_ref[:, c], eg_ref[:, c],
                              state[...])
            for h in range(ng):
                o_ref[rows, h * HEAD_DIM:(h + 1) * HEAD_DIM] = o[h]
            state[...] = new

    w = ng * HEAD_DIM
    blk = pl.BlockSpec((sc * CHUNK, w), lambda i: (i, 0))
    o, states = pl.pallas_call(
        body, name="gdn_scan_fwd", grid=(nch // sc,),
        in_specs=[blk, blk, blk, blk, pl.BlockSpec((ng, sc, CHUNK, CHUNK), lambda i: (0, i, 0, 0)),
                  pl.BlockSpec((ng, sc, 1, HEAD_DIM), lambda i: (0, i, 0, 0))],
        out_specs=[blk, pl.BlockSpec((ng, sc, HEAD_DIM, HEAD_DIM), lambda i: (0, i, 0, 0))],
        out_shape=[jax.ShapeDtypeStruct((t, w), F32),
                   jax.ShapeDtypeStruct((ng, nch, HEAD_DIM, HEAD_DIM), F32)],
        scratch_shapes=[pltpu.VMEM((ng, HEAD_DIM, HEAD_DIM), F32)],
        compiler_params=_cparams(("arbitrary",)),
    )(*wy)
    return o, (wy, inv, states)


def _wy_batch(q_ref, k_ref, v_ref, vals_ref, gr_ref, nf, ng, cb):
    idx = [(c, h) for c in range(cb) for h in range(ng)]
    rows = lambda c: slice(c * CHUNK, (c + 1) * CHUNK)
    lanes = lambda h: slice(h * HEAD_DIM, (h + 1) * HEAD_DIM)
    wide = lambda ref: jnp.stack([ref[rows(c), lanes(h)] for c, h in idx])
    col = lambda lane0: jnp.stack([vals_ref[rows(c), lane0 + h:lane0 + h + 1] for c, h in idx])
    return idx, (wide(q_ref), wide(k_ref), wide(v_ref), col(nf), jnp.stack([gr_ref[h, c] for c, h in idx]),
                 col(nf + ng))


def _gdn_wy(qkv, vals, grow, nf, ng, cb):
    t = qkv.shape[0]
    nch = t // CHUNK

    def body(q_ref, k_ref, v_ref, vals_ref, gr_ref, u_ref, w_ref, qg_ref, kd_ref, at_ref, eg_ref, inv_ref):
        idx, args = _wy_batch(q_ref, k_ref, v_ref, vals_ref, gr_ref, nf, ng, cb)
        u, w, qg, kd, at, eg, inv = _wy_fn(*args)
        for b, (c, h) in enumerate(idx):
            rows, lanes = slice(c * CHUNK, (c + 1) * CHUNK), slice(h * HEAD_DIM, (h + 1) * HEAD_DIM)
            u_ref[rows, lanes] = u[b]
            w_ref[rows, lanes] = w[b]
            qg_ref[rows, lanes] = qg[b]
            kd_ref[rows, lanes] = kd[b]
            at_ref[h, c] = at[b]
            eg_ref[h, c] = eg[b]
            inv_ref[h, c] = inv[b]

    wd = ng * HEAD_DIM
    blk = lambda o: pl.BlockSpec((cb * CHUNK, wd), lambda i: (i, o))
    col = pl.BlockSpec((cb * CHUNK, HEAD_DIM), lambda i: (i, 0))
    sq = pl.BlockSpec((ng, cb, CHUNK, CHUNK), lambda i: (0, i, 0, 0))
    wide = jax.ShapeDtypeStruct((t, wd), F32)
    sq_shape = jax.ShapeDtypeStruct((ng, nch, CHUNK, CHUNK), F32)
    return pl.pallas_call(
        body, name="gdn_wy_fwd", grid=(nch // cb,),
        in_specs=[blk(0), blk(1), blk(2), col, pl.BlockSpec((ng, cb, 1, CHUNK), lambda i: (0, i, 0, 0))],
        out_specs=[blk(0), blk(0), blk(0), blk(0), sq, pl.BlockSpec((ng, cb, 1, HEAD_DIM), lambda i: (0, i, 0, 0)),
                   sq],
        out_shape=[wide, wide, wide, wide, sq_shape, jax.ShapeDtypeStruct((ng, nch, 1, HEAD_DIM), F32), sq_shape],
        compiler_params=_cparams(("parallel",)),
    )(qkv, qkv, qkv, vals, grow)


def _gdn_bwd(qkv, vals, grow, saved, do, nf, ng):
    t = qkv.shape[0]
    nch = t // CHUNK
    cb = GDN_CHUNKS_PER_STEP // 2
    wy, inv, states = saved
    wd = ng * HEAD_DIM

    def scan_body(u_ref, w_ref, qg_ref, kd_ref, at_ref, eg_ref, st_ref, do_ref,
                  du_ref, dw_ref, dqg_ref, dkd_ref, dat_ref, deg_ref, dstate):
        @pl.when(pl.program_id(0) == 0)
        def _():
            dstate[...] = jnp.zeros_like(dstate)

        for c in reversed(range(sc)):
            rows = slice(c * CHUNK, (c + 1) * CHUNK)
            heads = lambda ref: jnp.stack([ref[rows, h * HEAD_DIM:(h + 1) * HEAD_DIM] for h in range(ng)])
            _, vjp = jax.vjp(_scan_fn, heads(u_ref), heads(w_ref), heads(qg_ref), heads(kd_ref), at_ref[:, c],
                             eg_ref[:, c], st_ref[:, c])
            du, dw, dqg, dkd, dat, deg, dst = vjp((heads(do_ref), dstate[...]))
            for h in range(ng):
                lanes = slice(h * HEAD_DIM, (h + 1) * HEAD_DIM)
                du_ref[rows, lanes] = du[h]
                dw_ref[rows, lanes] = dw[h]
                dqg_ref[rows, lanes] = dqg[h]
                dkd_ref[rows, lanes] = dkd[h]
            dat_ref[:, c] = dat
            deg_ref[:, c] = deg
            dstate[...] = dst

    sc = GDN_SCAN_CHUNKS
    rev = lambda i: nch // sc - 1 - i
    blk = pl.BlockSpec((sc * CHUNK, wd), lambda i: (rev(i), 0))
    atb = pl.BlockSpec((ng, sc, CHUNK, CHUNK), lambda i: (0, rev(i), 0, 0))
    egb = pl.BlockSpec((ng, sc, 1, HEAD_DIM), lambda i: (0, rev(i), 0, 0))
    wide = jax.ShapeDtypeStruct((t, wd), F32)
    at_shape = jax.ShapeDtypeStruct((ng, nch, CHUNK, CHUNK), F32)
    eg_shape = jax.ShapeDtypeStruct((ng, nch, 1, HEAD_DIM), F32)
    dwy = pl.pallas_call(
        scan_body, name="gdn_scan_bwd", grid=(nch // sc,),
        in_specs=[blk, blk, blk, blk, atb, egb,
                  pl.BlockSpec((ng, sc, HEAD_DIM, HEAD_DIM), lambda i: (0, rev(i), 0, 0)), blk],
        out_specs=[blk, blk, blk, blk, atb, egb],
        out_shape=[wide, wide, wide, wide, at_shape, eg_shape],
        scratch_shapes=[pltpu.VMEM((ng, HEAD_DIM, HEAD_DIM), F32)],
        compiler_params=_cparams(("arbitrary",)),
    )(*wy, states, do)

    def wy_body(q_ref, k_ref, v_ref, vals_ref, gr_ref, du_ref, dw_ref, dqg_ref, dkd_ref, dat_ref, deg_ref,
                inv_ref, dq_ref, dk_ref, dv_ref, dvals_ref, dgr_ref):
        idx, args = _wy_batch(q_ref, k_ref, v_ref, vals_ref, gr_ref, nf, ng, cb)
        lane = lax.broadcasted_iota(jnp.int32, (CHUNK, HEAD_DIM), 1)
        kept = jnp.stack([inv_ref[h, c] for c, h in idx])
        rows = lambda c: slice(c * CHUNK, (c + 1) * CHUNK)
        lanes = lambda h: slice(h * HEAD_DIM, (h + 1) * HEAD_DIM)
        wide_ct = lambda ref: jnp.stack([ref[rows(c), lanes(h)] for c, h in idx])
        cts = (wide_ct(du_ref), wide_ct(dw_ref), wide_ct(dqg_ref), wide_ct(dkd_ref),
               jnp.stack([dat_ref[h, c] for c, h in idx]), jnp.stack([deg_ref[h, c] for c, h in idx]))
        _, vjp = jax.vjp(lambda *a: _wy_fn(*a, inv=kept)[:6], *args)
        dq, dk, dv, dgc, dgr, dbc = vjp(cts)
        for b, (c, h) in enumerate(idx):
            dq_ref[rows(c), lanes(h)] = dq[b]
            dk_ref[rows(c), lanes(h)] = dk[b]
            dv_ref[rows(c), lanes(h)] = dv[b]
            dgr_ref[h, c] = dgr[b]
        for c in range(cb):
            acc = jnp.zeros((CHUNK, HEAD_DIM), F32)
            for h in range(ng):
                acc = jnp.where(lane == nf + h, dgc[c * ng + h], acc)
                acc = jnp.where(lane == nf + ng + h, dbc[c * ng + h], acc)
            dvals_ref[rows(c), :] = acc

    cblk = lambda o: pl.BlockSpec((cb * CHUNK, wd), lambda i: (i, o))
    col = pl.BlockSpec((cb * CHUNK, HEAD_DIM), lambda i: (i, 0))
    rowv = pl.BlockSpec((ng, cb, 1, CHUNK), lambda i: (0, i, 0, 0))
    return pl.pallas_call(
        wy_body, name="gdn_wy_bwd", grid=(nch // cb,),
        in_specs=[cblk(0), cblk(1), cblk(2), col, rowv, cblk(0), cblk(0), cblk(0), cblk(0),
                  pl.BlockSpec((ng, cb, CHUNK, CHUNK), lambda i: (0, i, 0, 0)),
                  pl.BlockSpec((ng, cb, 1, HEAD_DIM), lambda i: (0, i, 0, 0)),
                  pl.BlockSpec((ng, cb, CHUNK, CHUNK), lambda i: (0, i, 0, 0))],
        out_specs=[cblk(0), cblk(0), cblk(0), col, rowv],
        out_shape=[wide, wide, wide, jax.ShapeDtypeStruct((t, HEAD_DIM), F32),
                   jax.ShapeDtypeStruct((ng, nch, 1, CHUNK), F32)],
        compiler_params=_cparams(("parallel",)),
    )(qkv, qkv, qkv, vals, grow, *dwy, inv)


def _swiglu_fn(gate, up):
    return _silu(gate) * up


FFN_TN = 512


def _ffn_up(n2, wgu4):
    _, d, w = wgu4.shape
    t = n2.shape[0]
    tn = _tile(w, FFN_TN)
    nb = w // tn

    def body(a_ref, b_ref, gu_ref, act_ref):
        av = a_ref[...]
        gate = jnp.dot(av, b_ref[0], preferred_element_type=F32)
        up = jnp.dot(av, b_ref[1], preferred_element_type=F32)
        gu_ref[0] = gate.astype(BF16)
        gu_ref[1] = up.astype(BF16)
        act_ref[...] = _swiglu_fn(gate, up).astype(BF16)

    return pl.pallas_call(
        body, name="ffn_up", grid=(2, nb),
        in_specs=[pl.BlockSpec((t, d), lambda j, l: (0, 0)), pl.BlockSpec((2, d, tn), lambda j, l: (j, 0, l))],
        out_specs=[pl.BlockSpec((2, t, tn), lambda j, l: (j, 0, l)),
                   pl.BlockSpec((t, tn), lambda j, l: (0, j * nb + l))],
        out_shape=[jax.ShapeDtypeStruct((4, t, w), BF16), jax.ShapeDtypeStruct((t, 2 * w), BF16)],
        compiler_params=_cparams(("parallel", "parallel")),
    )(n2, wgu4)


def _ffn_dact(dh2, wd, gu, after):
    _, t, w = gu.shape
    d = dh2.shape[1]
    tn = _tile(w, FFN_TN)
    nb = w // tn

    def body(a_ref, b_ref, gu_ref, _, o_ref):
        dact = lax.dot_general(a_ref[...], b_ref[...], (((1,), (1,)), ((), ())), preferred_element_type=F32)
        _, vjp = jax.vjp(_swiglu_fn, gu_ref[0].astype(F32), gu_ref[1].astype(F32))
        dg, du = vjp(dact)
        o_ref[0] = dg.astype(BF16)
        o_ref[1] = du.astype(BF16)

    pair = pl.BlockSpec((2, t, tn), lambda j, l: (j, 0, l))
    return pl.pallas_call(
        body, name="ffn_dact", grid=(2, nb),
        in_specs=[pl.BlockSpec((t, d), lambda j, l: (0, 0)), pl.BlockSpec((tn, d), lambda j, l: (j * nb + l, 0)),
                  pair, pl.BlockSpec(after.shape, lambda j, l: (0, 0))],
        out_specs=pair, out_shape=jax.ShapeDtypeStruct(gu.shape, BF16),
        compiler_params=_cparams(("parallel", "parallel")),
    )(dh2, wd, gu, after)


def _loss_head(h2, target):
    t, d = h2.shape
    tr = _tile(t, 256, 8)

    def body(h_ref, t_ref, l_ref, d_ref, db_ref):
        @pl.when(pl.program_id(0) == 0)
        def _():
            l_ref[...] = jnp.zeros_like(l_ref)

        err = h_ref[...] - t_ref[...]
        d_ref[...] = err * (1.0 / d)
        db_ref[...] = (err * (1.0 / d)).astype(BF16)
        part = 0.5 * jnp.sum(jnp.mean(err * err, axis=-1, keepdims=True), axis=0, keepdims=True)
        lane = lax.broadcasted_iota(jnp.int32, (8, HEAD_DIM), 1)
        row = lax.broadcasted_iota(jnp.int32, (8, HEAD_DIM), 0)
        l_ref[...] += jnp.where((lane == 0) & (row == 0), part, 0.0)

    blk = pl.BlockSpec((tr, d), lambda r: (r, 0))
    return pl.pallas_call(
        body, name="loss_head", grid=(t // tr,), in_specs=[blk, blk],
        out_specs=[pl.BlockSpec((8, HEAD_DIM), lambda r: (0, 0)), blk, blk],
        out_shape=[jax.ShapeDtypeStruct((8, HEAD_DIM), F32), jax.ShapeDtypeStruct((t, d), F32),
                   jax.ShapeDtypeStruct((t, d), BF16)],
        compiler_params=_cparams(("arbitrary",)),
    )(h2, target)


def _adamw(w, g, m, v, *, g_fn=None, name):
    r, c = w.shape
    tr = _tile(r, max(8, (1 << 19) // c // 8 * 8), 8)
    gs = g if isinstance(g, tuple) else (g,)

    def body(w_ref, *refs):
        g_refs, (m_ref, v_ref, go_ref, d_ref, mo_ref, vo_ref) = refs[:len(gs)], refs[len(gs):]
        gr = g_refs[0][...] if g_fn is None else g_fn(*[ref[...] for ref in g_refs])
        mn = ADAM_B1 * m_ref[...] + (1.0 - ADAM_B1) * gr
        vn = ADAM_B2 * v_ref[...] + (1.0 - ADAM_B2) * (gr * gr)
        m_hat = mn / (1.0 - ADAM_B1 ** ADAM_STEP)
        v_hat = vn / (1.0 - ADAM_B2 ** ADAM_STEP)
        go_ref[...] = gr
        d_ref[...] = -ADAM_LR * (m_hat / (jnp.sqrt(v_hat) + ADAM_EPS) + ADAM_WD * w_ref[...])
        mo_ref[...] = mn
        vo_ref[...] = vn

    blk = pl.BlockSpec((tr, c), lambda i: (i, 0))
    gblks = [pl.BlockSpec((tr, gi.shape[1]), lambda i: (i, 0)) for gi in gs]
    return pl.pallas_call(
        body, name=name, grid=(r // tr,), in_specs=[blk] + gblks + [blk, blk], out_specs=[blk] * 4,
        out_shape=[jax.ShapeDtypeStruct((r, c), F32)] * 4,
        compiler_params=_cparams(("parallel",)),
    )(w, *gs, m, v)


class _Layout:
    def __init__(self, d):
        nh = d // HEAD_DIM
        self.nm = N_MEM_HEADS
        self.nf = (nh - self.nm) // 2
        self.ng = nh - self.nm - self.nf
        nf, ng, nm, hd = self.nf, self.ng, self.nm, HEAD_DIM
        self.o_fq, self.o_fk, self.o_fv, self.o_sm = 0, nf, 2 * nf, 3 * nf
        self.o_gq, self.o_gz, self.o_mq = 0, 3 * ng, 4 * ng
        self.cols_a = -(-(3 * nf + 1) // 4) * 4 * hd
        self.cols_b = -(-(4 * ng + nm) // 4) * 4 * hd
        self.cols = self.cols_a + self.cols_b
        sizes = [nf * hd, nf * hd, nf * hd, nf, 3 * ng * hd, ng * hd, ng, ng, nm * hd]
        starts = [sum(sizes[:i]) for i in range(len(sizes))]
        self.ref = list(zip(starts, sizes))
        self.in_cols = sum(sizes)

    def regroup(self, w):
        part = lambda i: w[:, self.ref[i][0]:self.ref[i][0] + self.ref[i][1]]
        a = [part(0), part(1), part(2), part(3), part(6), part(7)]
        b = [part(4), part(5), part(8)]
        pads = [self.cols_a - sum(p.shape[1] for p in a), self.cols_b - sum(p.shape[1] for p in b)]
        fill = [[jnp.zeros((w.shape[0], n), w.dtype)] if n else [] for n in pads]
        return jnp.concatenate(a + fill[0] + b + fill[1], axis=1)

    def ungroup(self, g):
        hd, nf, ng, nm = HEAD_DIM, self.nf, self.ng, self.nm
        sm, b0 = self.o_sm * hd, self.cols_a
        return jnp.concatenate([
            g[:, :3 * nf * hd], g[:, sm:sm + nf], g[:, b0:b0 + 3 * ng * hd],
            g[:, b0 + self.o_gz * hd:b0 + self.o_mq * hd], g[:, sm + nf:sm + nf + ng],
            g[:, sm + nf + ng:sm + nf + 2 * ng], g[:, b0 + self.o_mq * hd:b0 + (self.o_mq + nm) * hd]], axis=1)


def _lane_row(pieces):
    row = jnp.zeros((1, HEAD_DIM), F32)
    for off, a in pieces:
        row = lax.dynamic_update_slice(row, a.astype(F32), (0, off))
    return row


def _local_step(x, mem, target, prefetch, weights, reducer, sp):
    t, d = x.shape
    lay = _Layout(d)
    nf, ng, nm, hd = lay.nf, lay.ng, lay.nm, HEAD_DIM
    nch = t // CHUNK
    tq = _tile(t, 256)
    tk = tq

    u = _norm_fwd(x, 0, sp["norm_mix"], 1, d, BF16, name="norm_mix_fwd")
    prefetch("in_a", u)
    (win_a,) = weights("in_a", u)
    p_a = _mm(u, win_a, name="mm_in_a")
    pa = _lane_row([(nf, sp["gdn_a_log"])])
    pb = _lane_row([(0, sp["fox_f_bias"]), (nf, sp["gdn_dt_bias"])])
    vals, csum = _small_fwd(p_a, lay.o_sm, pa, pb, nf, ng)

    c_t = csum[:, :nf].T
    hp = _fox_heads(nf, 3)
    cr = c_t.reshape(nf, t // tk, 1, tk)
    cc = jnp.stack([jnp.pad(csum[:, g * hp:(g + 1) * hp], ((0, 0), (0, hd - hp))) for g in range(nf // hp)])
    fq = _norm_fwd(p_a, lay.o_fq, sp["fox_q_norm"], nf, hd, BF16, name="fox_qnorm_fwd")
    fk = _norm_fwd(p_a, lay.o_fk, sp["fox_k_norm"], nf, hd, BF16, name="fox_knorm_fwd")
    fv = p_a[:, lay.o_fv * hd:(lay.o_fv + nf) * hd].astype(BF16)
    o_fox, lse, mix = _fox_fwd(fq, fk, fv, cc, cr, nf, tq, tk, d)

    prefetch("in_b", lse)
    (win_b,) = weights("in_b", lse)
    prefetch("mixer", win_b)
    p = _mm(u, win_b, name="mm_in_b")
    wmkv, conv_taps = weights("mixer", p)
    sp = dict(sp, gdn_conv=conv_taps)
    qkv = _conv_fwd(p, lay.o_gq, sp["gdn_conv"], ng)
    grow = vals[:, nf:nf + ng].T.reshape(ng, nch, 1, CHUNK)
    o_g, states = _gdn_fwd(qkv, vals, grow, nf, ng)
    mix = _norm_fwd(o_g, 0, sp["gdn_out_norm"], ng, hd, BF16, z=p, zoff=lay.o_gz, into=mix, into_off=nf,
                    name="gdn_out_fwd")
    prefetch("out", mix)

    mem_n = _norm_fwd(mem, 0, sp["mem_norm"], 1, d, BF16, name="mem_norm_fwd")
    mkv = _mm(mem_n, wmkv, name="mm_memkv")
    tq_mem = _tile(t, 1024)
    mix = _mem_fwd(p, lay.o_mq, mkv, sp["mem_q_norm"], sp["mem_k_norm"], tq_mem, mix, nf + ng)
    prefetch("gate_up", mix)
    (wout,) = weights("out", mix)
    h1 = _mm(mix, wout, res=x, name="mm_out")
    n2 = _norm_fwd(h1, 0, sp["norm_ffn"], 1, d, BF16, name="norm_ffn_fwd")
    (wgu,) = weights("gate_up", n2)
    wgu4 = wgu.reshape(4, d, -1)
    gu, act = _ffn_up(n2, wgu4)
    prefetch("down", act)
    (wd,) = weights("down", act)
    h2 = _mm(act, wd, res=h1, name="mm_down")
    loss_blk, dh2, dh2_b = _loss_head(h2, target)

    g = {}
    token = reducer.pair("w_down", _mm(act, dh2_b, ta=True, out_dtype=BF16, name="mm_dw_down"))
    dgu = _ffn_dact(dh2_b, wd, gu, token)
    dw_gate_up = _mm(n2, dgu, ta=True, stack="out", out_dtype=BF16, name="mm_dw_gate_up").reshape(wgu.shape)
    token = reducer.pair("w_gate_up", dw_gate_up)
    dn2 = _mm(dgu, wgu4, tb=True, stack="sum", after=token, name="mm_dn2")
    token = reducer.ship("ffn", ["w_down", "w_gate_up"], dn2)
    dh1, g["norm_ffn"] = _norm_bwd(h1, 0, sp["norm_ffn"] + token[0, 0], dn2, 0, 1, d, res=dh2,
                                   name="norm_ffn_bwd")
    token = reducer.pair("w_out", _mm(mix, dh1, ta=True, out_dtype=BF16, name="mm_dw_out"))
    dmix = _mm(dh1, wout, tb=True, after=token, name="mm_dmix")

    dmq, dmk, dmv, g["mem_q_norm"], g["mem_k_norm"] = _mem_bwd(
        p, lay.o_mq, mkv, sp["mem_q_norm"], sp["mem_k_norm"], dmix, nf + ng, tq_mem)
    dmkv = jnp.concatenate([dmk, dmv], axis=1)
    token = reducer.pair("w_mem_kv", _mm(mem_n, dmkv, ta=True, out_dtype=BF16, name="mm_dw_memkv"))
    dmem_n = _mm(dmkv, wmkv, tb=True, after=token, name="mm_dmem")
    token = reducer.ship("mix", ["w_out", "w_mem_kv"], dmem_n)
    _, g["mem_norm"] = _norm_bwd(mem, 0, sp["mem_norm"], dmem_n, 0, 1, d, name="mem_norm_bwd")

    do_g, dgz, g["gdn_out_norm"] = _norm_bwd(o_g, 0, sp["gdn_out_norm"] + token[0, 0], dmix, nf, ng, hd, z=p,
                                             zoff=lay.o_gz, name="gdn_out_bwd")
    dq, dk, dv, dvals, dgr = _gdn_bwd(qkv, vals, grow, states, do_g, nf, ng)
    dgqkv, g["gdn_conv"] = _conv_bwd(p, lay.o_gq, sp["gdn_conv"], (dq, dk, dv), ng)

    dfq_n, dfk_n, dfv, dcc, dcr = _fox_bwd(fq, fk, fv, cc, cr, o_fox, lse, dmix, nf, tq, tk)
    dfq, g["fox_q_norm"] = _norm_bwd(p_a, lay.o_fq, sp["fox_q_norm"], dfq_n, 0, nf, hd, out_dtype=BF16,
                                     name="fox_qnorm_bwd")
    dfk, g["fox_k_norm"] = _norm_bwd(p_a, lay.o_fk, sp["fox_k_norm"], dfk_n, 0, nf, hd, out_dtype=BF16,
                                     name="fox_knorm_bwd")
    dc = dcc.reshape(nf, t).T + jnp.concatenate([dcr[g, :, :hp] for g in range(nf // hp)], axis=1)

    dvals = dvals + jnp.pad(dgr.reshape(ng, t).T, ((0, 0), (nf, hd - nf - ng)))
    dcsum = jnp.pad(dc, ((0, 0), (0, hd - nf)))
    dsm, dpa, dpb = _small_bwd(p_a, lay.o_sm, pa, pb, dvals, dcsum, nf, ng)
    g["fox_f_bias"] = dpb[:, :nf]
    g["gdn_dt_bias"] = dpb[:, nf:nf + ng]
    g["gdn_a_log"] = dpa[:, nf:nf + ng]

    dp_a = jnp.concatenate([dfq, dfk, dfv.astype(BF16), dsm.astype(BF16),
                            jnp.zeros((t, lay.cols_a - (lay.o_sm + 1) * hd), BF16)], axis=1)
    assert lay.o_gq == 0 and lay.o_gz == 3 * ng and lay.o_mq == lay.o_gz + ng
    rest = jnp.concatenate([dgz.astype(BF16), dmq.astype(BF16),
                            jnp.zeros((t, lay.cols_b - (lay.o_mq + nm) * hd), BF16)], axis=1)
    dp_b = lax.dynamic_update_slice(dgqkv, rest, (0, lay.o_gz * hd))
    token = reducer.pair("w_in_a", _mm(u, dp_a, ta=True, out_dtype=BF16, name="mm_dw_in_a"))
    token = reducer.pair("w_in_b", _mm(u, dp_b, ta=True, out_dtype=BF16, after=token, name="mm_dw_in_b"))
    du = _mm(dp_a, win_a, tb=True, after=token, name="mm_du_a")
    token = reducer.ship("in", ["w_in_a", "w_in_b"], du)
    du = _mm(dp_b, win_b, tb=True, res=du, after=token, name="mm_du_b")
    dx, g["norm_mix"] = _norm_bwd(x, 0, sp["norm_mix"], du, 0, 1, d, res=dh1, name="norm_mix_bwd")
    return loss_blk, dx, g


ANY = pl.BlockSpec(memory_space=pl.ANY)


def _me():
    x, y, c = lax.axis_index("x"), lax.axis_index("y"), lax.axis_index("c")
    chips = [(1 - x, y), (x, 1 - y), (1 - x, 1 - y)]
    return x, y, c, chips


def _slot(axis, k):
    return k if axis == 0 else 2 * (k % 2) + k // 2


def _slab(ref, axis, rows, cols, k, h):
    half = rows // 2
    return ref.at[pl.ds(_slot(axis, k) * rows + h * half, half), :]


def _remote(src, dst, send_sem, recv_sem, dev):
    return pltpu.make_async_remote_copy(src_ref=src, dst_ref=dst, send_sem=send_sem, recv_sem=recv_sem,
                                        device_id=dev, device_id_type=MESH)


HBM = pl.BlockSpec(memory_space=pltpu.HBM)
SEM = pl.BlockSpec(memory_space=pltpu.SEMAPHORE)
SPLIT = pltpu.CompilerParams(has_side_effects=pltpu.SideEffectType.DATAFLOW_SIDE_EFFECTING)
TOKEN = jax.ShapeDtypeStruct((8, HEAD_DIM), F32)


def _in_hbm(v):
    return pltpu.with_memory_space_constraint(v, pltpu.HBM)


def _cast_place(shard, axis, name, col_fn=None, out_cols=None, after=None):
    r, c = shard.shape
    oc = out_cols or c
    tr = _tile(r, 512 if col_fn is None else 64, 16)
    tc = _tile(c, 2048) if col_fn is None else c
    otc = tc if col_fn is None else oc
    nb = r // tr
    chip = 2 * lax.axis_index("x") + lax.axis_index("y")
    slot = jnp.reshape(_slot(axis, chip), (1,)).astype(jnp.int32)

    def body(slot_ref, x_ref, *rest):
        x = x_ref[...]
        rest[-1][...] = (x if col_fn is None else col_fn(x)).astype(BF16)

    extra = [] if after is None else [after]
    return pl.pallas_call(
        body, name=name,
        grid_spec=pltpu.PrefetchScalarGridSpec(
            num_scalar_prefetch=1, grid=(nb, c // tc),
            in_specs=[pl.BlockSpec((tr, tc), lambda i, l, s: (i, l))] + [ANY] * len(extra),
            out_specs=pl.BlockSpec((tr, otc), lambda i, l, s: (s[0] * nb + i, l))),
        out_shape=jax.ShapeDtypeStruct((4 * r, oc), BF16),
        compiler_params=_cparams(("parallel", "parallel")),
    )(slot, shard, *extra)


def _gather_start(bufs, axes, shapes, groups, name):
    n = len(bufs)

    def body(*refs):
        dst = refs[n:2 * n]
        sems = refs[2 * n:2 * n + 2 * len(groups)]
        token = refs[-1]
        x, y, c, chips = _me()
        k = 2 * x + y
        for gi, ws in enumerate(groups):
            for i, w in enumerate(ws):
                r, cl = shapes[w]
                place = _slab(dst[w], axes[w], r, cl, k, c)
                for j, (px, py) in enumerate(chips):
                    _remote(place, place, sems[2 * gi].at[3 * i + j], sems[2 * gi + 1].at[3 * i + j],
                            (px, py, c)).start()
        token[...] = jnp.zeros_like(token)

    sem_shapes = [pltpu.SemaphoreType.DMA((3 * len(ws),)) for ws in groups for _ in range(2)]
    outs = pl.pallas_call(
        body, name=name, in_specs=[HBM] * n,
        out_specs=[HBM] * n + [SEM] * len(sem_shapes) + [pl.BlockSpec(memory_space=pltpu.VMEM)],
        out_shape=[pltpu.HBM(b.shape, b.dtype) for b in bufs] + sem_shapes + [TOKEN],
        input_output_aliases={w: w for w in range(n)}, compiler_params=SPLIT,
    )(*[_in_hbm(b) for b in bufs])
    sems = outs[n:-1]
    return outs[:n], [(sems[2 * g], sems[2 * g + 1]) for g in range(len(groups))], outs[-1]


def _gather_wait(bufs, axes, shapes, sems, after, name):
    n = len(bufs)

    def body(*refs):
        send_sems, recv_sems = refs[n], refs[n + 1]
        dst = refs[n + 3:]
        x, y, c, chips = _me()
        k = 2 * x + y
        for i in range(n):
            r, cl = shapes[i]
            for j, (px, py) in enumerate(chips):
                got = _slab(dst[i], axes[i], r, cl, 2 * px + py, c)
                _remote(got, got, send_sems.at[3 * i + j], recv_sems.at[3 * i + j], (px, py, c)).wait_recv()
        for i in range(n):
            r, cl = shapes[i]
            mine = _slab(dst[i], axes[i], r, cl, k, c)
            for j, (px, py) in enumerate(chips):
                _remote(mine, mine, send_sems.at[3 * i + j], recv_sems.at[3 * i + j], (px, py, c)).wait_send()

    return pl.pallas_call(
        body, name=name, in_specs=[HBM] * n + [SEM, SEM, ANY], out_specs=[HBM] * n,
        out_shape=[pltpu.HBM(b.shape, b.dtype) for b in bufs],
        input_output_aliases={i: i for i in range(n)}, compiler_params=SPLIT,
    )(*bufs, sems[0], sems[1], after)


def _split_start(name, arrays, geometry, count):
    n = len(arrays)

    def body(*refs):
        send, recv, token = refs[2 * n:]
        for i, (src, dst, _, dev) in enumerate(geometry(refs[n:2 * n])):
            _remote(src, dst, send.at[i], recv.at[i], dev).start()
        token[...] = jnp.zeros_like(token)

    sem = pltpu.SemaphoreType.DMA((count,))
    outs = pl.pallas_call(
        body, name=name, in_specs=[HBM] * n,
        out_specs=[HBM] * n + [SEM, SEM, pl.BlockSpec(memory_space=pltpu.VMEM)],
        out_shape=[pltpu.HBM(v.shape, v.dtype) for v in arrays] + [sem, sem, TOKEN],
        input_output_aliases={i: i for i in range(n)}, compiler_params=SPLIT,
    )(*[_in_hbm(v) for v in arrays])
    return list(outs[:n]), (outs[n], outs[n + 1]), outs[-1]


def _split_wait(name, arrays, sems, after, geometry):
    n = len(arrays)

    def body(*refs):
        send, recv = refs[n], refs[n + 1]
        copies = geometry(refs[n + 3:])
        for i, (_, _, land, dev) in enumerate(copies):
            _remote(land, land, send.at[i], recv.at[i], dev).wait_recv()
        for i, (src, _, _, dev) in enumerate(copies):
            _remote(src, src, send.at[i], recv.at[i], dev).wait_send()

    return list(pl.pallas_call(
        body, name=name, in_specs=[HBM] * n + [SEM, SEM, ANY], out_specs=[HBM] * n,
        out_shape=[pltpu.HBM(v.shape, v.dtype) for v in arrays],
        input_output_aliases={i: i for i in range(n)}, compiler_params=SPLIT,
    )(*arrays, sems[0], sems[1], after))


def _forward_geometry(axes, shapes):
    def geometry(bufs):
        x, y, c, chips = _me()
        out = []
        for i, buf in enumerate(bufs):
            r, cl = shapes[i]
            for px, py in chips:
                got = _slab(buf, axes[i], r, cl, 2 * px + py, c)
                out.append((got, got, _slab(buf, axes[i], r, cl, 2 * px + py, 1 - c), (x, y, 1 - c)))
        return out
    return geometry


def _pair_geometry(axes, shapes):
    def geometry(refs):
        n = len(refs) // 2
        x, y, c, _ = _me()
        out = []
        for w in range(n):
            r, cl = shapes[w]
            for j in range(4):
                land = refs[n + w].at[j]
                out.append((_slab(refs[w], axes[w], r, cl, j, 1 - c), land, land, (x, y, 1 - c)))
        return out
    return geometry


def _after_all(name, token, *arrays):
    def body(*refs):
        refs[-1][...] = jnp.zeros_like(refs[-1])

    return pl.pallas_call(
        body, name=name, in_specs=[ANY] * (1 + len(arrays)), out_specs=pl.BlockSpec(memory_space=pltpu.VMEM),
        out_shape=TOKEN,
    )(token, *arrays)


def _swap_geometry(bufs):
    x, y, c, _ = _me()
    return [(b.at[c], b.at[c], b.at[1 - c], (x, y, 1 - c)) for b in bufs]


def _chip_start(parts, tag):
    n = len(parts)

    def body(*refs):
        src, land = refs[2 * n:3 * n], refs[3 * n:4 * n]
        send_sems, recv_sems, token = refs[4 * n:]
        x, y, c, chips = _me()
        k = 2 * x + y
        for w in range(n):
            for j, (px, py) in enumerate(chips):
                _remote(src[w].at[2 * px + py], land[w].at[k], send_sems.at[3 * w + j], recv_sems.at[3 * w + j],
                        (px, py, c)).start()
        token[...] = jnp.zeros_like(token)

    lands = [lax.empty(p.shape, p.dtype) for p in parts]
    sem = pltpu.SemaphoreType.DMA((3 * n,))
    outs = pl.pallas_call(
        body, name="reduce_ici_start_" + tag, in_specs=[HBM] * (2 * n),
        out_specs=[HBM] * (2 * n) + [SEM, SEM, pl.BlockSpec(memory_space=pltpu.VMEM)],
        out_shape=[pltpu.HBM(p.shape, p.dtype) for p in parts + lands] + [sem, sem, TOKEN],
        input_output_aliases={i: i for i in range(2 * n)}, compiler_params=SPLIT,
    )(*[_in_hbm(v) for v in parts + lands])
    return outs[:n], outs[n:2 * n], outs[2 * n], outs[2 * n + 1], outs[-1]


def _chip_wait(parts, lands, send_sems, recv_sems, after, tag):
    n = len(parts)

    def body(*refs):
        send, recv = refs[2 * n], refs[2 * n + 1]
        src, land = refs[2 * n + 3:3 * n + 3], refs[3 * n + 3:]
        x, y, c, chips = _me()
        for w in range(n):
            for j, (px, py) in enumerate(chips):
                got = land[w].at[2 * px + py]
                _remote(got, got, send.at[3 * w + j], recv.at[3 * w + j], (px, py, c)).wait_recv()
        for w in range(n):
            for j, (px, py) in enumerate(chips):
                sent = src[w].at[2 * px + py]
                _remote(sent, sent, send.at[3 * w + j], recv.at[3 * w + j], (px, py, c)).wait_send()

    outs = pl.pallas_call(
        body, name="reduce_ici_wait_" + tag, in_specs=[HBM] * (2 * n) + [SEM, SEM, ANY], out_specs=[HBM] * (2 * n),
        out_shape=[pltpu.HBM(p.shape, p.dtype) for p in parts + lands],
        input_output_aliases={i: i for i in range(2 * n)}, compiler_params=SPLIT,
    )(*parts, *lands, send_sems, recv_sems, after)
    chip = 2 * lax.axis_index("x") + lax.axis_index("y")
    return [lax.dynamic_update_slice(s, lax.dynamic_index_in_dim(p, chip, 0, keepdims=True), (chip, 0, 0))
            for p, s in zip(outs[:n], outs[n:])]


def _half_swap(halves, tag):
    n = len(halves)
    core = lax.axis_index("c")
    bufs = [lax.dynamic_update_slice(lax.empty((2,) + h.shape, h.dtype), h[None], (core, 0, 0)) for h in halves]

    def body(*refs):
        dst = refs[n:2 * n]
        send_sems, recv_sems = refs[2 * n:]
        x, y, c, _ = _me()
        sibling = (x, y, 1 - c)
        cps = []
        for w in range(n):
            cp = _remote(dst[w].at[c], dst[w].at[c], send_sems.at[w], recv_sems.at[w], sibling)
            cp.start()
            cps.append(cp)
        for w in range(n):
            other = dst[w].at[1 - c]
            _remote(other, other, send_sems.at[w], recv_sems.at[w], sibling).wait_recv()
        for cp in cps:
            cp.wait_send()

    outs = pl.pallas_call(
        body, name="reduce_half_swap_" + tag, in_specs=[ANY] * n, out_specs=[ANY] * n,
        out_shape=[jax.ShapeDtypeStruct(b.shape, b.dtype) for b in bufs],
        input_output_aliases={w: w for w in range(n)},
        scratch_shapes=[pltpu.SemaphoreType.DMA((n,)), pltpu.SemaphoreType.DMA((n,))],
    )(*bufs)
    return [o.reshape(2 * o.shape[1], o.shape[2]) for o in outs]


def _add_parts(full, axis, rows, sib, name):
    _, r, c = sib.shape
    tr, tc = _tile(r, 1024, 16), _tile(c, 2048)
    nb = r // tr
    core = jnp.reshape(lax.axis_index("c"), (1,)).astype(jnp.int32)

    def body(c_ref, a_ref, b_ref, o_ref):
        o_ref[0] = (a_ref[...].astype(F32) + b_ref[0].astype(F32)).astype(BF16)

    blk = pl.BlockSpec((1, tr, tc), lambda j, i, l, cr: (j, i, l))
    return pl.pallas_call(
        body, name=name,
        grid_spec=pltpu.PrefetchScalarGridSpec(
            num_scalar_prefetch=1, grid=(4, nb, c // tc),
            in_specs=[pl.BlockSpec((tr, tc), lambda j, i, l, cr: ((_slot(axis, j) * 2 + cr[0]) * nb + i, l)), blk],
            out_specs=blk),
        out_shape=jax.ShapeDtypeStruct(sib.shape, BF16),
        compiler_params=_cparams(("parallel", "parallel", "parallel")),
    )(core, full, sib)


def _sum_slots(a, name):
    _, r, c = a.shape
    tr, tc = _tile(r, 512, 8), _tile(c, 2048)

    def body(a_ref, o_ref):
        v = a_ref[...].astype(F32)
        o_ref[...] = ((v[0] + v[1]) + v[2]) + v[3]

    return pl.pallas_call(
        body, name=name, grid=(r // tr, c // tc),
        in_specs=[pl.BlockSpec((4, tr, tc), lambda i, l: (0, i, l))],
        out_specs=pl.BlockSpec((tr, tc), lambda i, l: (i, l)),
        out_shape=jax.ShapeDtypeStruct((r, c), F32),
        compiler_params=_cparams(("parallel", "parallel")),
    )(a)


class _Reducer:
    def __init__(self, spec):
        self.spec = spec
        self.paired = {}
        self.pending = []

    def pair(self, name, full):
        ax, shp = self.spec[name]
        land = lax.empty((4, shp[0] // 2, shp[1]), full.dtype)
        arrays, sems, token = _split_start("reduce_pair_start_" + name, [full, land], _pair_geometry([ax], [shp]), 4)
        self.paired[name] = (arrays, sems)
        return token

    def ship(self, tag, names, after):
        parts = []
        for n in names:
            ax, shp = self.spec[n]
            arrays, sems = self.paired.pop(n)
            full, sib = _split_wait("reduce_pair_wait_" + n, arrays, sems, after, _pair_geometry([ax], [shp]))
            parts.append(_add_parts(full, ax, shp[0], sib, name=f"reduce_add_{n}"))
        parts, lands, send, recv, token = _chip_start(parts, tag)
        self.pending.append((tag, names, parts, lands, send, recv))
        return token

    def finish(self, after, tags):
        out = {}
        for tag, names, parts, lands, send, recv in [p for p in self.pending if p[0] in tags]:
            slots = _chip_wait(parts, lands, send, recv, after, tag)
            halves = [_sum_slots(s, name=f"reduce_sum_{n}") for n, s in zip(names, slots)]
            out.update(zip(names, _half_swap(halves, tag)))
        return out

    def finish_start(self, after, tag):
        (_, names, parts, lands, send, recv), = [p for p in self.pending if p[0] == tag]
        slots = _chip_wait(parts, lands, send, recv, after, tag)
        halves = [_sum_slots(s, name=f"reduce_sum_{n}") for n, s in zip(names, slots)]
        core = lax.axis_index("c")
        bufs = [lax.dynamic_update_slice(lax.empty((2,) + h.shape, h.dtype), h[None], (core, 0, 0)) for h in halves]
        bufs, sems, _ = _split_start("reduce_half_swap_start_" + tag, bufs, _swap_geometry, len(bufs))
        return tag, names, bufs, sems

    def swap_wait(self, started, after):
        tag, names, bufs, sems = started
        outs = _split_wait("reduce_half_swap_wait_" + tag, bufs, sems, after, _swap_geometry)
        return dict(zip(names, [o.reshape(2 * o.shape[1], o.shape[2]) for o in outs]))


def _allreduce_small(pack, after):
    rows = pack.shape[0]

    def body(p_ref, _, o_ref, slots, send_sems, recv_sems):
        x, y, c, _ = _me()
        me = 4 * x + 2 * y + c
        slots[me] = p_ref[...]
        cps = []
        for r in range(1, 8):
            peer = (x ^ (r >> 2), y ^ ((r >> 1) & 1), c ^ (r & 1))
            cp = _remote(p_ref, slots.at[me], send_sems.at[r - 1], recv_sems.at[r - 1], peer)
            cp.start()
            cps.append(cp)
        for r in range(1, 8):
            frm = me ^ r
            _remote(slots.at[frm], slots.at[frm], send_sems.at[r - 1], recv_sems.at[r - 1], (x, y, c)).wait_recv()
        for cp in cps:
            cp.wait_send()
        acc = slots[0]
        for s in range(1, 8):
            acc = acc + slots[s]
        o_ref[...] = acc

    vm = pl.BlockSpec(memory_space=pltpu.VMEM)
    return pl.pallas_call(
        body, name="allreduce_small", in_specs=[vm, ANY], out_specs=vm,
        out_shape=jax.ShapeDtypeStruct(pack.shape, F32),
        scratch_shapes=[pltpu.VMEM((8, rows, HEAD_DIM), F32), pltpu.SemaphoreType.DMA((7,)),
                        pltpu.SemaphoreType.DMA((7,))],
    )(pack, after)


_ROWS = ["norm_mix", "norm_ffn", "mem_norm", "fox_q_norm", "fox_k_norm", "gdn_out_norm", "mem_q_norm",
         "mem_k_norm", "fox_f_bias", "gdn_a_log", "gdn_dt_bias"]


def _pack_rows(vals):
    out = []
    for name in _ROWS:
        v = vals[name].reshape(-1)
        n = -(-v.shape[0] // HEAD_DIM) * HEAD_DIM
        out.append(jnp.pad(v, (0, n - v.shape[0])).reshape(-1, HEAD_DIM))
    return jnp.concatenate(out, axis=0)


def _unpack_rows(pack, like):
    out, r = {}, 0
    for name in _ROWS:
        n = like[name].shape[-1]
        nr = -(-n // HEAD_DIM)
        out[name] = pack[r:r + nr].reshape(1, -1)[:, :n]
        r += nr
    return out, r


def kernel(x, mem, norm_mix, w_in, fox_f_bias, fox_q_norm, fox_k_norm, gdn_conv, gdn_a_log, gdn_dt_bias, gdn_out_norm, mem_norm, w_mem_kv, mem_q_norm, mem_k_norm, w_out, norm_ffn, w_gate_up, w_down, loss_target, m_norm_mix, m_w_in, m_fox_f_bias, m_fox_q_norm, m_fox_k_norm, m_gdn_conv, m_gdn_a_log, m_gdn_dt_bias, m_gdn_out_norm, m_mem_norm, m_w_mem_kv, m_mem_q_norm, m_mem_k_norm, m_w_out, m_norm_ffn, m_w_gate_up, m_w_down, v_norm_mix, v_w_in, v_fox_f_bias, v_fox_q_norm, v_fox_k_norm, v_gdn_conv, v_gdn_a_log, v_gdn_dt_bias, v_gdn_out_norm, v_mem_norm, v_w_mem_kv, v_mem_q_norm, v_mem_k_norm, v_w_out, v_norm_ffn, v_w_gate_up, v_w_down):
    a = dict(locals())
    d = x.shape[-1]
    lay = _Layout(d)
    chip = 2 * lax.axis_index("x") + lax.axis_index("y")
    small = {n: a[n] for n in _ROWS}
    big = ["w_in", "w_mem_kv", "w_out", "w_gate_up", "w_down"]
    axes = [0, 0, 0, 1, 0]

    conv_cols = gdn_conv.shape[-1]
    conv_n = CONV_WIDTH * conv_cols
    conv_rows = -(-conv_n // HEAD_DIM)
    conv_blk = jnp.pad(gdn_conv.reshape(-1), (0, 32 * HEAD_DIM - conv_n)).reshape(32, HEAD_DIM)
    axis_of = dict(zip(big, axes), conv=0, w_in_a=0, w_in_b=0)
    shape_of = {n: a[n].shape[1:] for n in big[1:]}
    shape_of.update(w_in_a=(w_in.shape[1], lay.cols_a), w_in_b=(w_in.shape[1], lay.cols_b), conv=conv_blk.shape)
    placed = {"w_in_a": _cast_place(w_in[0], 0, "cast_w_in_a", lambda v: lay.regroup(v)[:, :lay.cols_a], lay.cols_a),
              "conv": lax.dynamic_update_slice(lax.empty((4 * 32, HEAD_DIM), F32), conv_blk, (chip * 32, 0))}
    grouped = {"in_a": ["w_in_a"], "in_b": ["w_in_b"], "mixer": ["w_mem_kv", "conv"], "out": ["w_out"],
               "gate_up": ["w_gate_up"], "down": ["w_down"]}
    inflight = {}

    def start(tags, name):
        names = [n for t in tags for n in grouped[t]]
        bufs, sems, token = _gather_start([placed[n] for n in names], [axis_of[n] for n in names],
                                          [shape_of[n] for n in names],
                                          [[names.index(n) for n in grouped[t]] for t in tags], name)
        for t, pair in zip(tags, sems):
            inflight[t] = ([bufs[names.index(n)] for n in grouped[t]], pair)
        return token

    first = start(["in_a"], "gather_ici_start_in")
    placed["w_in_b"] = _cast_place(w_in[0], 0, "cast_w_in_b", lambda v: lay.regroup(v)[:, lay.cols_a:], lay.cols_b,
                                   after=first)
    placed.update({n: _cast_place(a[n][0], axis_of[n], "cast_" + n, after=first) for n in big[1:]})
    all_started = start(["in_b", "mixer", "out", "gate_up", "down"], "gather_ici_start_rest")
    all_started = _after_all("moments_ready", all_started, m_w_in[0], v_w_in[0])

    forwarding = {}

    def prefetch(tag, after):
        bufs, sem_pair = inflight.pop(tag)
        ax, shp = [axis_of[n] for n in grouped[tag]], [shape_of[n] for n in grouped[tag]]
        got = _gather_wait(bufs, ax, shp, sem_pair, all_started if tag == "in_a" else after,
                           "gather_ici_wait_" + tag)
        geometry = _forward_geometry(ax, shp)
        got, sems, _ = _split_start("gather_forward_start_" + tag, got, geometry, 3 * len(got))
        forwarding[tag] = (got, sems, geometry)

    def weights(tag, after):
        got, sems, geometry = forwarding.pop(tag)
        got = _split_wait("gather_forward_wait_" + tag, got, sems, after, geometry)
        if tag != "mixer":
            return got
        taps = got[1].reshape(4, 32 * HEAD_DIM)[:, :conv_n].reshape(4, CONV_WIDTH, conv_cols)
        return got[0], jnp.transpose(taps, (1, 0, 2)).reshape(CONV_WIDTH, 4 * conv_cols)

    sp = dict(small)
    reducer = _Reducer({n: (axis_of[n], shape_of[n]) for n in big[1:] + ["w_in_a", "w_in_b"]})
    loss_blk, dx, g = _local_step(x[0], mem[0], loss_target[0], prefetch, weights, reducer, sp)

    gsmall = {n: g[n] for n in _ROWS}
    pack = jnp.concatenate([_pack_rows(gsmall), g["gdn_conv"].reshape(-1, HEAD_DIM), loss_blk], axis=0)
    pack = jnp.pad(pack, ((0, -pack.shape[0] % 8), (0, 0)))
    out = {"grad_x": dx[None]}

    def adamw_shards(reduced):
        if "w_in_a" in reduced:
            reduced = {"w_in": (reduced["w_in_a"], reduced["w_in_b"])}
        for n, gsh in reduced.items():
            join = (lambda ga, gb: lay.ungroup(jnp.concatenate([ga, gb], axis=1))) if n == "w_in" else None
            res = _adamw(a[n][0], gsh, a["m_" + n][0], a["v_" + n][0], g_fn=join, name="adamw_" + n)
            for pre, r in zip(["grad_", "delta_", "new_m_", "new_v_"], res):
                out[pre + n] = r[None]
        return res[0]

    mix_swap = reducer.finish_start(dx, "mix")
    ffn_swap = reducer.finish_start(mix_swap[2][0], "ffn")
    done = adamw_shards(reducer.swap_wait(mix_swap, ffn_swap[2][0]))
    done = adamw_shards(reducer.swap_wait(ffn_swap, done))
    tot = _allreduce_small(pack, done)
    gs, r0 = _unpack_rows(tot, small)
    conv_g = tot[r0:r0 + CONV_WIDTH * 4 * conv_cols // HEAD_DIM].reshape(CONV_WIDTH, 4 * conv_cols)
    gs_conv = lax.dynamic_slice_in_dim(conv_g, chip * conv_cols, conv_cols, axis=1)
    out["loss"] = tot[r0 + CONV_WIDTH * 4 * conv_cols // HEAD_DIM, 0]
    adamw_shards(reducer.finish(tot, ("in",)))
    conv_pad = lambda v: jnp.pad(v.reshape(-1), (0, conv_rows * HEAD_DIM - conv_n)).reshape(conv_rows, HEAD_DIM)
    packs = []
    for src, cv in [(small, gdn_conv), (gs, gs_conv), ({n: a["m_" + n] for n in _ROWS}, m_gdn_conv),
                    ({n: a["v_" + n] for n in _ROWS}, v_gdn_conv)]:
        packs.append(jnp.concatenate([_pack_rows(src), conv_pad(cv)], axis=0))
    res = _adamw(*packs, name="adamw_small")
    for pre, r in zip(["grad_", "delta_", "new_m_", "new_v_"], res):
        vals, r1 = _unpack_rows(r, small)
        for n in _ROWS:
            out[pre + n] = vals[n]
        out[pre + "gdn_conv"] = r[r1:r1 + conv_rows].reshape(-1)[:conv_n].reshape(gdn_conv.shape)
    names = ["norm_mix", "w_in", "fox_f_bias", "fox_q_norm", "fox_k_norm", "gdn_conv", "gdn_a_log", "gdn_dt_bias",
             "gdn_out_norm", "mem_norm", "w_mem_kv", "mem_q_norm", "mem_k_norm", "w_out", "norm_ffn", "w_gate_up",
             "w_down"]
    return (out["loss"], out["grad_x"], *[out[p + n] for p in ["grad_", "delta_", "new_m_", "new_v_"] for n in names])
```
